```python
import math
import jax, jax.numpy as jnp
from jax import lax
import numpy as np

D_MODEL = 1024
BATCH = 8
SEQ = 4096
DEPTH = 1

MEM_LEN = 256
D_FF = 2752
MLA_HEADS = 4
MLA_Q_RANK = 384
MLA_KV_RANK = 256
MLA_NOPE = 128
MLA_ROPE = 64
MLA_V = 128
MLA_QK = MLA_NOPE + MLA_ROPE
MLA_WIDTH = MLA_HEADS * MLA_V
SSM_WIDTH = D_MODEL - MLA_WIDTH
SSM_GROUP = 16
SSM_GROUPS = SSM_WIDTH // SSM_GROUP
SSM_STATE = 64
DT_MIN = 1e-3
DT_MAX = 1e-1
XATTN_HEADS = 4
XATTN_HEAD_DIM = 128
XATTN_WIDTH = XATTN_HEADS * XATTN_HEAD_DIM
IN_SPLITS = [MLA_Q_RANK, MLA_Q_RANK + MLA_KV_RANK, MLA_Q_RANK + MLA_KV_RANK + MLA_ROPE]
IN_WIDTH = MLA_Q_RANK + MLA_KV_RANK + MLA_ROPE + SSM_WIDTH
Q_BLOCK = 128
ROPE_THETA = 10000.0
EPS = 1e-6

kernel_name = "hymba_mla_s5_macaron_memxattn"


def rms_norm(x, g):
    xf = x.astype(jnp.float32)
    y = xf * lax.rsqrt(jnp.mean(xf * xf, axis=-1, keepdims=True) + EPS)
    return (y * g.astype(jnp.float32)).astype(x.dtype)


def swiglu(h, w_gate, w_up, w_down):
    return (jax.nn.silu(h @ w_gate) * (h @ w_up)) @ w_down


def rope(x, pos):
    half = x.shape[-1] // 2
    inv = ROPE_THETA ** (-jnp.arange(half, dtype=jnp.float32) / half)
    ang = (pos.astype(jnp.float32)[..., None] * inv)[:, :, None, :]
    cos, sin = jnp.cos(ang), jnp.sin(ang)
    x1 = x[..., :half].astype(jnp.float32)
    x2 = x[..., half:].astype(jnp.float32)
    return jnp.concatenate([x1 * cos - x2 * sin, x2 * cos + x1 * sin], axis=-1).astype(x.dtype)


def causal_block_attention(q, k, v):
    B, S, H, Dk = q.shape
    Dv = v.shape[-1]
    nb = S // Q_BLOCK
    scale = Dk ** -0.5
    qb = q.reshape(B, nb, Q_BLOCK, H, Dk).transpose(1, 0, 2, 3, 4)
    kpos = jnp.arange(S)

    def one_block(args):
        q_blk, i = args
        s = jnp.einsum('bqhd,bkhd->bhqk', q_blk, k).astype(jnp.float32) * scale
        qpos = i * Q_BLOCK + jnp.arange(Q_BLOCK)
        s = jnp.where(qpos[:, None] >= kpos[None, :], s, -jnp.inf)
        p = jax.nn.softmax(s, axis=-1).astype(v.dtype)
        return jnp.einsum('bhqk,bkhd->bqhd', p, v)

    out = lax.map(one_block, (qb, jnp.arange(nb)))
    return out.transpose(1, 0, 2, 3, 4).reshape(B, S, H, Dv)


def mla_mixer(c_q_in, c_kv_in, k_r, pos, q_norm, w_uq, kv_norm, w_ukv, qk_norm_q, qk_norm_k):
    B, S, _ = c_q_in.shape
    c_q = rms_norm(c_q_in, q_norm)
    q = (c_q @ w_uq).reshape(B, S, MLA_HEADS, MLA_QK)
    c_kv = rms_norm(c_kv_in, kv_norm)
    kv = (c_kv @ w_ukv).reshape(B, S, MLA_HEADS, MLA_NOPE + MLA_V)
    k_nope, v = kv[..., :MLA_NOPE], kv[..., MLA_NOPE:]
    k_rope = jnp.broadcast_to(k_r[:, :, None, :], (B, S, MLA_HEADS, MLA_ROPE))
    k = jnp.concatenate([k_nope, k_rope], axis=-1)
    q = rms_norm(q, qk_norm_q)
    k = rms_norm(k, qk_norm_k)
    q = jnp.concatenate([q[..., :MLA_NOPE], rope(q[..., MLA_NOPE:], pos)], axis=-1)
    k = jnp.concatenate([k[..., :MLA_NOPE], rope(k[..., MLA_NOPE:], pos)], axis=-1)
    o = causal_block_attention(q, k, v)
    return o.reshape(B, S, MLA_WIDTH)


def _complex_linear_combine(e1, e2):
    a1r, a1i, b1r, b1i = e1
    a2r, a2i, b2r, b2i = e2
    ar = a2r * a1r - a2i * a1i
    ai = a2r * a1i + a2i * a1r
    br = a2r * b1r - a2i * b1i + b2r
    bi = a2r * b1i + a2i * b1r + b2i
    return (ar, ai, br, bi)


def s5_mixer(u, a_re, a_im, log_dt, b_re, b_im, c_re, c_im, d, w_glu, b_glu):
    B, S, _ = u.shape
    f32 = jnp.float32
    uf = u.astype(f32).reshape(B, S, SSM_GROUPS, SSM_GROUP)
    lr, li = a_re.astype(f32), a_im.astype(f32)
    dt = jnp.exp(log_dt.astype(f32))[:, None]
    decay = jnp.exp(lr * dt)
    ar = decay * jnp.cos(li * dt)
    ai = decay * jnp.sin(li * dt)
    den = lr * lr + li * li
    nr = ar - 1.0
    coef_r = (nr * lr + ai * li) / den
    coef_i = (ai * lr - nr * li) / den
    br, bi = b_re.astype(f32), b_im.astype(f32)
    bbar_r = coef_r[..., None] * br - coef_i[..., None] * bi
    bbar_i = coef_r[..., None] * bi + coef_i[..., None] * br
    bu_r = jnp.einsum('bsgh,gph->bsgp', uf, bbar_r)
    bu_i = jnp.einsum('bsgh,gph->bsgp', uf, bbar_i)
    ar_t = jnp.broadcast_to(ar, bu_r.shape)
    ai_t = jnp.broadcast_to(ai, bu_r.shape)
    _, _, xr, xi = lax.associative_scan(_complex_linear_combine, (ar_t, ai_t, bu_r, bu_i), axis=1)
    y = (jnp.einsum('bsgp,ghp->bsgh', xr, c_re.astype(f32))
         - jnp.einsum('bsgp,ghp->bsgh', xi, c_im.astype(f32))
         + d.astype(f32) * uf)
    y = y.reshape(B, S, SSM_WIDTH)
    g = jax.nn.gelu(y)
    out = g * jax.nn.sigmoid(g @ w_glu.astype(f32) + b_glu.astype(f32))
    return out.astype(u.dtype)


def memory_cross_attention(h, mem, mem_norm, w_q, w_kv, qn, kn, w_o):
    B, S, _ = h.shape
    M = mem.shape[1]
    q = (h @ w_q).reshape(B, S, XATTN_HEADS, XATTN_HEAD_DIM)
    m = rms_norm(mem, mem_norm)
    kv = (m @ w_kv).reshape(B, M, 2, XATTN_HEADS, XATTN_HEAD_DIM)
    k, v = kv[:, :, 0], kv[:, :, 1]
    q = rms_norm(q, qn)
    k = rms_norm(k, kn)
    s = jnp.einsum('bshd,bmhd->bhsm', q, k).astype(jnp.float32) * (XATTN_HEAD_DIM ** -0.5)
    p = jax.nn.softmax(s, axis=-1).astype(v.dtype)
    o = jnp.einsum('bhsm,bmhd->bshd', p, v).reshape(B, S, XATTN_WIDTH)
    return o @ w_o


def _fwd_setup_inputs(seed: int = 0) -> dict:
    key = jax.random.key(seed)
    ks = iter(jax.random.split(key, 48))
    f32 = jnp.float32

    def nrm(shape):
        return jax.random.normal(next(ks), (DEPTH,) + shape, f32)

    def w(shape, fan_in):
        return nrm(shape) * (fan_in ** -0.5)

    def gain(dim):
        return 1.0 + 0.02 * nrm((dim,))

    x = jax.random.normal(next(ks), (BATCH, SEQ, D_MODEL), f32)
    mem = jax.random.normal(next(ks), (BATCH, MEM_LEN, D_MODEL), f32)
    positions = (jax.random.randint(next(ks), (BATCH, 1), 0, 1024, jnp.int32)
                 + jnp.arange(SEQ, dtype=jnp.int32)[None, :])
    ffn1_norm = gain(D_MODEL)
    ffn1_w_gate = w((D_MODEL, D_FF), D_MODEL)
    ffn1_w_up = w((D_MODEL, D_FF), D_MODEL)
    ffn1_w_down = w((D_FF, D_MODEL), D_FF)
    mix_norm = gain(D_MODEL)
    w_in = w((D_MODEL, IN_WIDTH), D_MODEL)
    mla_q_norm = gain(MLA_Q_RANK)
    mla_w_uq = w((MLA_Q_RANK, MLA_HEADS * MLA_QK), MLA_Q_RANK)
    mla_kv_norm = gain(MLA_KV_RANK)
    mla_w_ukv = w((MLA_KV_RANK, MLA_HEADS * (MLA_NOPE + MLA_V)), MLA_KV_RANK)
    mla_qk_norm_q = gain(MLA_QK)
    mla_qk_norm_k = gain(MLA_QK)
    n = jnp.arange(SSM_STATE, dtype=f32)
    ssm_a_re = -0.5 + 0.01 * nrm((SSM_GROUPS, SSM_STATE))
    ssm_a_im = math.pi * n + 0.01 * nrm((SSM_GROUPS, SSM_STATE))
    ssm_log_dt = jax.random.uniform(next(ks), (DEPTH, SSM_GROUPS), f32, math.log(DT_MIN), math.log(DT_MAX))
    ssm_b_re = nrm((SSM_GROUPS, SSM_STATE, SSM_GROUP)) * (0.5 / SSM_GROUP) ** 0.5
    ssm_b_im = nrm((SSM_GROUPS, SSM_STATE, SSM_GROUP)) * (0.5 / SSM_GROUP) ** 0.5
    ssm_c_re = nrm((SSM_GROUPS, SSM_GROUP, SSM_STATE)) * (0.5 / SSM_STATE) ** 0.5
    ssm_c_im = nrm((SSM_GROUPS, SSM_GROUP, SSM_STATE)) * (0.5 / SSM_STATE) ** 0.5
    ssm_d = nrm((SSM_GROUPS, SSM_GROUP))
    ssm_w_glu = w((SSM_WIDTH, SSM_WIDTH), SSM_WIDTH)
    ssm_b_glu = 0.01 * nrm((SSM_WIDTH,))
    out_norm_mla = gain(MLA_WIDTH)
    out_norm_ssm = gain(SSM_WIDTH)
    w_o = w((MLA_WIDTH + SSM_WIDTH, D_MODEL), MLA_WIDTH + SSM_WIDTH)
    xattn_norm = gain(D_MODEL)
    mem_norm = gain(D_MODEL)
    xattn_w_q = w((D_MODEL, XATTN_WIDTH), D_MODEL)
    xattn_w_kv = w((D_MODEL, 2 * XATTN_WIDTH), D_MODEL)
    xattn_q_norm = gain(XATTN_HEAD_DIM)
    xattn_k_norm = gain(XATTN_HEAD_DIM)
    xattn_w_o = w((XATTN_WIDTH, D_MODEL), XATTN_WIDTH)
    ffn2_norm = gain(D_MODEL)
    ffn2_w_gate = w((D_MODEL, D_FF), D_MODEL)
    ffn2_w_up = w((D_MODEL, D_FF), D_MODEL)
    ffn2_w_down = w((D_FF, D_MODEL), D_FF)
    return {
        'x': x, 'mem': mem, 'positions': positions,
        'ffn1_norm': ffn1_norm, 'ffn1_w_gate': ffn1_w_gate, 'ffn1_w_up': ffn1_w_up, 'ffn1_w_down': ffn1_w_down,
        'mix_norm': mix_norm, 'w_in': w_in,
        'mla_q_norm': mla_q_norm, 'mla_w_uq': mla_w_uq, 'mla_kv_norm': mla_kv_norm, 'mla_w_ukv': mla_w_ukv,
        'mla_qk_norm_q': mla_qk_norm_q, 'mla_qk_norm_k': mla_qk_norm_k,
        'ssm_a_re': ssm_a_re, 'ssm_a_im': ssm_a_im, 'ssm_log_dt': ssm_log_dt,
        'ssm_b_re': ssm_b_re, 'ssm_b_im': ssm_b_im, 'ssm_c_re': ssm_c_re, 'ssm_c_im': ssm_c_im,
        'ssm_d': ssm_d, 'ssm_w_glu': ssm_w_glu, 'ssm_b_glu': ssm_b_glu,
        'out_norm_mla': out_norm_mla, 'out_norm_ssm': out_norm_ssm, 'w_o': w_o,
        'xattn_norm': xattn_norm, 'mem_norm': mem_norm, 'xattn_w_q': xattn_w_q, 'xattn_w_kv': xattn_w_kv,
        'xattn_q_norm': xattn_q_norm, 'xattn_k_norm': xattn_k_norm, 'xattn_w_o': xattn_w_o,
        'ffn2_norm': ffn2_norm, 'ffn2_w_gate': ffn2_w_gate, 'ffn2_w_up': ffn2_w_up, 'ffn2_w_down': ffn2_w_down,
    }


def _fwd_reference(x, mem, positions,
              ffn1_norm, ffn1_w_gate, ffn1_w_up, ffn1_w_down,
              mix_norm, w_in,
              mla_q_norm, mla_w_uq, mla_kv_norm, mla_w_ukv, mla_qk_norm_q, mla_qk_norm_k,
              ssm_a_re, ssm_a_im, ssm_log_dt, ssm_b_re, ssm_b_im, ssm_c_re, ssm_c_im,
              ssm_d, ssm_w_glu, ssm_b_glu,
              out_norm_mla, out_norm_ssm, w_o,
              xattn_norm, mem_norm, xattn_w_q, xattn_w_kv, xattn_q_norm, xattn_k_norm, xattn_w_o,
              ffn2_norm, ffn2_w_gate, ffn2_w_up, ffn2_w_down):
    for l in range(DEPTH):
        x = x + 0.5 * swiglu(rms_norm(x, ffn1_norm[l]), ffn1_w_gate[l], ffn1_w_up[l], ffn1_w_down[l])
        h = rms_norm(x, mix_norm[l])
        proj = h @ w_in[l]
        c_q_in, c_kv_in, k_r, u = jnp.split(proj, IN_SPLITS, axis=-1)
        y_mla = mla_mixer(c_q_in, c_kv_in, k_r, positions, mla_q_norm[l], mla_w_uq[l],
                          mla_kv_norm[l], mla_w_ukv[l], mla_qk_norm_q[l], mla_qk_norm_k[l])
        y_ssm = s5_mixer(u, ssm_a_re[l], ssm_a_im[l], ssm_log_dt[l], ssm_b_re[l], ssm_b_im[l],
                         ssm_c_re[l], ssm_c_im[l], ssm_d[l], ssm_w_glu[l], ssm_b_glu[l])
        y = jnp.concatenate([rms_norm(y_mla, out_norm_mla[l]), rms_norm(y_ssm, out_norm_ssm[l])], axis=-1)
        x = x + y @ w_o[l]
        x = x + memory_cross_attention(rms_norm(x, xattn_norm[l]), mem, mem_norm[l], xattn_w_q[l],
                                       xattn_w_kv[l], xattn_q_norm[l], xattn_k_norm[l], xattn_w_o[l])
        x = x + 0.5 * swiglu(rms_norm(x, ffn2_norm[l]), ffn2_w_gate[l], ffn2_w_up[l], ffn2_w_down[l])
    return x


import jax as _jax
import jax.numpy as _jnp

TWIN_FORMAT = 'train_step'
FWD_PARAMS = ['x', 'mem', 'positions', 'ffn1_norm', 'ffn1_w_gate', 'ffn1_w_up', 'ffn1_w_down', 'mix_norm', 'w_in', 'mla_q_norm', 'mla_w_uq', 'mla_kv_norm', 'mla_w_ukv', 'mla_qk_norm_q', 'mla_qk_norm_k', 'ssm_a_re', 'ssm_a_im', 'ssm_log_dt', 'ssm_b_re', 'ssm_b_im', 'ssm_c_re', 'ssm_c_im', 'ssm_d', 'ssm_w_glu', 'ssm_b_glu', 'out_norm_mla', 'out_norm_ssm', 'w_o', 'xattn_norm', 'mem_norm', 'xattn_w_q', 'xattn_w_kv', 'xattn_q_norm', 'xattn_k_norm', 'xattn_w_o', 'ffn2_norm', 'ffn2_w_gate', 'ffn2_w_up', 'ffn2_w_down']
TWIN_WEIGHTS = ['ffn1_norm', 'ffn1_w_gate', 'ffn1_w_up', 'ffn1_w_down', 'mix_norm', 'w_in', 'mla_q_norm', 'mla_w_uq', 'mla_kv_norm', 'mla_w_ukv', 'mla_qk_norm_q', 'mla_qk_norm_k', 'ssm_a_re', 'ssm_a_im', 'ssm_log_dt', 'ssm_b_re', 'ssm_b_im', 'ssm_c_re', 'ssm_c_im', 'ssm_d', 'ssm_w_glu', 'ssm_b_glu', 'out_norm_mla', 'out_norm_ssm', 'w_o', 'xattn_norm', 'mem_norm', 'xattn_w_q', 'xattn_w_kv', 'xattn_q_norm', 'xattn_k_norm', 'xattn_w_o', 'ffn2_norm', 'ffn2_w_gate', 'ffn2_w_up', 'ffn2_w_down']
TWIN_DIFF_INPUT = 'x'
TWIN_INPUTS = ['x', 'mem', 'positions', 'ffn1_norm', 'ffn1_w_gate', 'ffn1_w_up', 'ffn1_w_down', 'mix_norm', 'w_in', 'mla_q_norm', 'mla_w_uq', 'mla_kv_norm', 'mla_w_ukv', 'mla_qk_norm_q', 'mla_qk_norm_k', 'ssm_a_re', 'ssm_a_im', 'ssm_log_dt', 'ssm_b_re', 'ssm_b_im', 'ssm_c_re', 'ssm_c_im', 'ssm_d', 'ssm_w_glu', 'ssm_b_glu', 'out_norm_mla', 'out_norm_ssm', 'w_o', 'xattn_norm', 'mem_norm', 'xattn_w_q', 'xattn_w_kv', 'xattn_q_norm', 'xattn_k_norm', 'xattn_w_o', 'ffn2_norm', 'ffn2_w_gate', 'ffn2_w_up', 'ffn2_w_down', 'loss_target', 'm_ffn1_norm', 'm_ffn1_w_gate', 'm_ffn1_w_up', 'm_ffn1_w_down', 'm_mix_norm', 'm_w_in', 'm_mla_q_norm', 'm_mla_w_uq', 'm_mla_kv_norm', 'm_mla_w_ukv', 'm_mla_qk_norm_q', 'm_mla_qk_norm_k', 'm_ssm_a_re', 'm_ssm_a_im', 'm_ssm_log_dt', 'm_ssm_b_re', 'm_ssm_b_im', 'm_ssm_c_re', 'm_ssm_c_im', 'm_ssm_d', 'm_ssm_w_glu', 'm_ssm_b_glu', 'm_out_norm_mla', 'm_out_norm_ssm', 'm_w_o', 'm_xattn_norm', 'm_mem_norm', 'm_xattn_w_q', 'm_xattn_w_kv', 'm_xattn_q_norm', 'm_xattn_k_norm', 'm_xattn_w_o', 'm_ffn2_norm', 'm_ffn2_w_gate', 'm_ffn2_w_up', 'm_ffn2_w_down', 'v_ffn1_norm', 'v_ffn1_w_gate', 'v_ffn1_w_up', 'v_ffn1_w_down', 'v_mix_norm', 'v_w_in', 'v_mla_q_norm', 'v_mla_w_uq', 'v_mla_kv_norm', 'v_mla_w_ukv', 'v_mla_qk_norm_q', 'v_mla_qk_norm_k', 'v_ssm_a_re', 'v_ssm_a_im', 'v_ssm_log_dt', 'v_ssm_b_re', 'v_ssm_b_im', 'v_ssm_c_re', 'v_ssm_c_im', 'v_ssm_d', 'v_ssm_w_glu', 'v_ssm_b_glu', 'v_out_norm_mla', 'v_out_norm_ssm', 'v_w_o', 'v_xattn_norm', 'v_mem_norm', 'v_xattn_w_q', 'v_xattn_w_kv', 'v_xattn_q_norm', 'v_xattn_k_norm', 'v_xattn_w_o', 'v_ffn2_norm', 'v_ffn2_w_gate', 'v_ffn2_w_up', 'v_ffn2_w_down']
TWIN_OUTPUTS = ['loss', 'grad_x', 'grad_ffn1_norm', 'grad_ffn1_w_gate', 'grad_ffn1_w_up', 'grad_ffn1_w_down', 'grad_mix_norm', 'grad_w_in', 'grad_mla_q_norm', 'grad_mla_w_uq', 'grad_mla_kv_norm', 'grad_mla_w_ukv', 'grad_mla_qk_norm_q', 'grad_mla_qk_norm_k', 'grad_ssm_a_re', 'grad_ssm_a_im', 'grad_ssm_log_dt', 'grad_ssm_b_re', 'grad_ssm_b_im', 'grad_ssm_c_re', 'grad_ssm_c_im', 'grad_ssm_d', 'grad_ssm_w_glu', 'grad_ssm_b_glu', 'grad_out_norm_mla', 'grad_out_norm_ssm', 'grad_w_o', 'grad_xattn_norm', 'grad_mem_norm', 'grad_xattn_w_q', 'grad_xattn_w_kv', 'grad_xattn_q_norm', 'grad_xattn_k_norm', 'grad_xattn_w_o', 'grad_ffn2_norm', 'grad_ffn2_w_gate', 'grad_ffn2_w_up', 'grad_ffn2_w_down', 'delta_ffn1_norm', 'delta_ffn1_w_gate', 'delta_ffn1_w_up', 'delta_ffn1_w_down', 'delta_mix_norm', 'delta_w_in', 'delta_mla_q_norm', 'delta_mla_w_uq', 'delta_mla_kv_norm', 'delta_mla_w_ukv', 'delta_mla_qk_norm_q', 'delta_mla_qk_norm_k', 'delta_ssm_a_re', 'delta_ssm_a_im', 'delta_ssm_log_dt', 'delta_ssm_b_re', 'delta_ssm_b_im', 'delta_ssm_c_re', 'delta_ssm_c_im', 'delta_ssm_d', 'delta_ssm_w_glu', 'delta_ssm_b_glu', 'delta_out_norm_mla', 'delta_out_norm_ssm', 'delta_w_o', 'delta_xattn_norm', 'delta_mem_norm', 'delta_xattn_w_q', 'delta_xattn_w_kv', 'delta_xattn_q_norm', 'delta_xattn_k_norm', 'delta_xattn_w_o', 'delta_ffn2_norm', 'delta_ffn2_w_gate', 'delta_ffn2_w_up', 'delta_ffn2_w_down', 'new_m_ffn1_norm', 'new_m_ffn1_w_gate', 'new_m_ffn1_w_up', 'new_m_ffn1_w_down', 'new_m_mix_norm', 'new_m_w_in', 'new_m_mla_q_norm', 'new_m_mla_w_uq', 'new_m_mla_kv_norm', 'new_m_mla_w_ukv', 'new_m_mla_qk_norm_q', 'new_m_mla_qk_norm_k', 'new_m_ssm_a_re', 'new_m_ssm_a_im', 'new_m_ssm_log_dt', 'new_m_ssm_b_re', 'new_m_ssm_b_im', 'new_m_ssm_c_re', 'new_m_ssm_c_im', 'new_m_ssm_d', 'new_m_ssm_w_glu', 'new_m_ssm_b_glu', 'new_m_out_norm_mla', 'new_m_out_norm_ssm', 'new_m_w_o', 'new_m_xattn_norm', 'new_m_mem_norm', 'new_m_xattn_w_q', 'new_m_xattn_w_kv', 'new_m_xattn_q_norm', 'new_m_xattn_k_norm', 'new_m_xattn_w_o', 'new_m_ffn2_norm', 'new_m_ffn2_w_gate', 'new_m_ffn2_w_up', 'new_m_ffn2_w_down', 'new_v_ffn1_norm', 'new_v_ffn1_w_gate', 'new_v_ffn1_w_up', 'new_v_ffn1_w_down', 'new_v_mix_norm', 'new_v_w_in', 'new_v_mla_q_norm', 'new_v_mla_w_uq', 'new_v_mla_kv_norm', 'new_v_mla_w_ukv', 'new_v_mla_qk_norm_q', 'new_v_mla_qk_norm_k', 'new_v_ssm_a_re', 'new_v_ssm_a_im', 'new_v_ssm_log_dt', 'new_v_ssm_b_re', 'new_v_ssm_b_im', 'new_v_ssm_c_re', 'new_v_ssm_c_im', 'new_v_ssm_d', 'new_v_ssm_w_glu', 'new_v_ssm_b_glu', 'new_v_out_norm_mla', 'new_v_out_norm_ssm', 'new_v_w_o', 'new_v_xattn_norm', 'new_v_mem_norm', 'new_v_xattn_w_q', 'new_v_xattn_w_kv', 'new_v_xattn_q_norm', 'new_v_xattn_k_norm', 'new_v_xattn_w_o', 'new_v_ffn2_norm', 'new_v_ffn2_w_gate', 'new_v_ffn2_w_up', 'new_v_ffn2_w_down']
TWIN_LEAF_KINDS = {'loss': 'loss', 'grad_x': 'grad_x', 'grad_ffn1_norm': 'grad_w', 'grad_ffn1_w_gate': 'grad_w', 'grad_ffn1_w_up': 'grad_w', 'grad_ffn1_w_down': 'grad_w', 'grad_mix_norm': 'grad_w', 'grad_w_in': 'grad_w', 'grad_mla_q_norm': 'grad_w', 'grad_mla_w_uq': 'grad_w', 'grad_mla_kv_norm': 'grad_w', 'grad_mla_w_ukv': 'grad_w', 'grad_mla_qk_norm_q': 'grad_w', 'grad_mla_qk_norm_k': 'grad_w', 'grad_ssm_a_re': 'grad_w', 'grad_ssm_a_im': 'grad_w', 'grad_ssm_log_dt': 'grad_w', 'grad_ssm_b_re': 'grad_w', 'grad_ssm_b_im': 'grad_w', 'grad_ssm_c_re': 'grad_w', 'grad_ssm_c_im': 'grad_w', 'grad_ssm_d': 'grad_w', 'grad_ssm_w_glu': 'grad_w', 'grad_ssm_b_glu': 'grad_w', 'grad_out_norm_mla': 'grad_w', 'grad_out_norm_ssm': 'grad_w', 'grad_w_o': 'grad_w', 'grad_xattn_norm': 'grad_w', 'grad_mem_norm': 'grad_w', 'grad_xattn_w_q': 'grad_w', 'grad_xattn_w_kv': 'grad_w', 'grad_xattn_q_norm': 'grad_w', 'grad_xattn_k_norm': 'grad_w', 'grad_xattn_w_o': 'grad_w', 'grad_ffn2_norm': 'grad_w', 'grad_ffn2_w_gate': 'grad_w', 'grad_ffn2_w_up': 'grad_w', 'grad_ffn2_w_down': 'grad_w', 'delta_ffn1_norm': 'delta_w', 'delta_ffn1_w_gate': 'delta_w', 'delta_ffn1_w_up': 'delta_w', 'delta_ffn1_w_down': 'delta_w', 'delta_mix_norm': 'delta_w', 'delta_w_in': 'delta_w', 'delta_mla_q_norm': 'delta_w', 'delta_mla_w_uq': 'delta_w', 'delta_mla_kv_norm': 'delta_w', 'delta_mla_w_ukv': 'delta_w', 'delta_mla_qk_norm_q': 'delta_w', 'delta_mla_qk_norm_k': 'delta_w', 'delta_ssm_a_re': 'delta_w', 'delta_ssm_a_im': 'delta_w', 'delta_ssm_log_dt': 'delta_w', 'delta_ssm_b_re': 'delta_w', 'delta_ssm_b_im': 'delta_w', 'delta_ssm_c_re': 'delta_w', 'delta_ssm_c_im': 'delta_w', 'delta_ssm_d': 'delta_w', 'delta_ssm_w_glu': 'delta_w', 'delta_ssm_b_glu': 'delta_w', 'delta_out_norm_mla': 'delta_w', 'delta_out_norm_ssm': 'delta_w', 'delta_w_o': 'delta_w', 'delta_xattn_norm': 'delta_w', 'delta_mem_norm': 'delta_w', 'delta_xattn_w_q': 'delta_w', 'delta_xattn_w_kv': 'delta_w', 'delta_xattn_q_norm': 'delta_w', 'delta_xattn_k_norm': 'delta_w', 'delta_xattn_w_o': 'delta_w', 'delta_ffn2_norm': 'delta_w', 'delta_ffn2_w_gate': 'delta_w', 'delta_ffn2_w_up': 'delta_w', 'delta_ffn2_w_down': 'delta_w', 'new_m_ffn1_norm': 'new_m', 'new_m_ffn1_w_gate': 'new_m', 'new_m_ffn1_w_up': 'new_m', 'new_m_ffn1_w_down': 'new_m', 'new_m_mix_norm': 'new_m', 'new_m_w_in': 'new_m', 'new_m_mla_q_norm': 'new_m', 'new_m_mla_w_uq': 'new_m', 'new_m_mla_kv_norm': 'new_m', 'new_m_mla_w_ukv': 'new_m', 'new_m_mla_qk_norm_q': 'new_m', 'new_m_mla_qk_norm_k': 'new_m', 'new_m_ssm_a_re': 'new_m', 'new_m_ssm_a_im': 'new_m', 'new_m_ssm_log_dt': 'new_m', 'new_m_ssm_b_re': 'new_m', 'new_m_ssm_b_im': 'new_m', 'new_m_ssm_c_re': 'new_m', 'new_m_ssm_c_im': 'new_m', 'new_m_ssm_d': 'new_m', 'new_m_ssm_w_glu': 'new_m', 'new_m_ssm_b_glu': 'new_m', 'new_m_out_norm_mla': 'new_m', 'new_m_out_norm_ssm': 'new_m', 'new_m_w_o': 'new_m', 'new_m_xattn_norm': 'new_m', 'new_m_mem_norm': 'new_m', 'new_m_xattn_w_q': 'new_m', 'new_m_xattn_w_kv': 'new_m', 'new_m_xattn_q_norm': 'new_m', 'new_m_xattn_k_norm': 'new_m', 'new_m_xattn_w_o': 'new_m', 'new_m_ffn2_norm': 'new_m', 'new_m_ffn2_w_gate': 'new_m', 'new_m_ffn2_w_up': 'new_m', 'new_m_ffn2_w_down': 'new_m', 'new_v_ffn1_norm': 'new_v', 'new_v_ffn1_w_gate': 'new_v', 'new_v_ffn1_w_up': 'new_v', 'new_v_ffn1_w_down': 'new_v', 'new_v_mix_norm': 'new_v', 'new_v_w_in': 'new_v', 'new_v_mla_q_norm': 'new_v', 'new_v_mla_w_uq': 'new_v', 'new_v_mla_kv_norm': 'new_v', 'new_v_mla_w_ukv': 'new_v', 'new_v_mla_qk_norm_q': 'new_v', 'new_v_mla_qk_norm_k': 'new_v', 'new_v_ssm_a_re': 'new_v', 'new_v_ssm_a_im': 'new_v', 'new_v_ssm_log_dt': 'new_v', 'new_v_ssm_b_re': 'new_v', 'new_v_ssm_b_im': 'new_v', 'new_v_ssm_c_re': 'new_v', 'new_v_ssm_c_im': 'new_v', 'new_v_ssm_d': 'new_v', 'new_v_ssm_w_glu': 'new_v', 'new_v_ssm_b_glu': 'new_v', 'new_v_out_norm_mla': 'new_v', 'new_v_out_norm_ssm': 'new_v', 'new_v_w_o': 'new_v', 'new_v_xattn_norm': 'new_v', 'new_v_mem_norm': 'new_v', 'new_v_xattn_w_q': 'new_v', 'new_v_xattn_w_kv': 'new_v', 'new_v_xattn_q_norm': 'new_v', 'new_v_xattn_k_norm': 'new_v', 'new_v_xattn_w_o': 'new_v', 'new_v_ffn2_norm': 'new_v', 'new_v_ffn2_w_gate': 'new_v', 'new_v_ffn2_w_up': 'new_v', 'new_v_ffn2_w_down': 'new_v'}


def _forward(args):
    return _fwd_reference(*[args[k] for k in FWD_PARAMS])


def _output_shape():
    def fwd():
        inp = _fwd_setup_inputs(0)
        return _fwd_reference(*[inp[k] for k in FWD_PARAMS])
    out = _jax.eval_shape(fwd)
    return out.shape, out.dtype

N_MICROBATCH = 1
ADAM_LR = 0.001
ADAM_B1 = 0.9
ADAM_B2 = 0.999
ADAM_EPS = 1e-08
ADAM_WD = 0.01
ADAM_STEP = 10
PER_EXAMPLE_BATCH_AXIS = {'x': 0, 'mem': 0, 'positions': 0, 'loss_target': 0}
SHARED_INPUTS = []
_WEIGHT_DTYPES = {'ffn1_norm': _jnp.float32, 'ffn1_w_gate': _jnp.float32, 'ffn1_w_up': _jnp.float32, 'ffn1_w_down': _jnp.float32, 'mix_norm': _jnp.float32, 'w_in': _jnp.float32, 'mla_q_norm': _jnp.float32, 'mla_w_uq': _jnp.float32, 'mla_kv_norm': _jnp.float32, 'mla_w_ukv': _jnp.float32, 'mla_qk_norm_q': _jnp.float32, 'mla_qk_norm_k': _jnp.float32, 'ssm_a_re': _jnp.float32, 'ssm_a_im': _jnp.float32, 'ssm_log_dt': _jnp.float32, 'ssm_b_re': _jnp.float32, 'ssm_b_im': _jnp.float32, 'ssm_c_re': _jnp.float32, 'ssm_c_im': _jnp.float32, 'ssm_d': _jnp.float32, 'ssm_w_glu': _jnp.float32, 'ssm_b_glu': _jnp.float32, 'out_norm_mla': _jnp.float32, 'out_norm_ssm': _jnp.float32, 'w_o': _jnp.float32, 'xattn_norm': _jnp.float32, 'mem_norm': _jnp.float32, 'xattn_w_q': _jnp.float32, 'xattn_w_kv': _jnp.float32, 'xattn_q_norm': _jnp.float32, 'xattn_k_norm': _jnp.float32, 'xattn_w_o': _jnp.float32, 'ffn2_norm': _jnp.float32, 'ffn2_w_gate': _jnp.float32, 'ffn2_w_up': _jnp.float32, 'ffn2_w_down': _jnp.float32}
MOMENT_SCALE = {'ffn1_norm': 6.246302e+00, 'ffn1_w_gate': 3.867825e-01, 'ffn1_w_up': 3.986868e-01, 'ffn1_w_down': 6.830839e-01, 'mix_norm': 1.626125e+00, 'w_in': 1.406545e+00, 'mla_q_norm': 1.105633e+00, 'mla_w_uq': 7.196378e-01, 'mla_kv_norm': 4.632531e+00, 'mla_w_ukv': 1.766514e+00, 'mla_qk_norm_q': 1.970823e+00, 'mla_qk_norm_k': 1.964046e+00, 'ssm_a_re': 2.169944e-02, 'ssm_a_im': 2.014536e-02, 'ssm_log_dt': 8.435971e+00, 'ssm_b_re': 1.554651e-02, 'ssm_b_im': 1.589438e-02, 'ssm_c_re': 3.091804e-02, 'ssm_c_im': 3.239374e-02, 'ssm_d': 6.157296e+00, 'ssm_w_glu': 9.953687e-01, 'ssm_b_glu': 2.977911e+00, 'out_norm_mla': 3.229295e+01, 'out_norm_ssm': 6.151518e+01, 'w_o': 4.214589e+00, 'xattn_norm': 1.470694e-01, 'mem_norm': 5.245382e-01, 'xattn_w_q': 1.984084e-01, 'xattn_w_kv': 3.589848e-01, 'xattn_q_norm': 2.483792e+00, 'xattn_k_norm': 2.487907e+00, 'xattn_w_o': 3.183619e-01, 'ffn2_norm': 6.197267e+00, 'ffn2_w_gate': 2.484900e-01, 'ffn2_w_up': 2.509468e-01, 'ffn2_w_down': 4.003042e-01}


def _to_microbatches(a, axis):
    t = _jnp.moveaxis(a, axis, 0)
    t = t.reshape((N_MICROBATCH, t.shape[0] // N_MICROBATCH) + t.shape[1:])
    return _jnp.moveaxis(t, 1, axis + 1)


def setup_inputs(seed: int = 0) -> dict:
    inp = _fwd_setup_inputs(seed)
    key = _jax.random.fold_in(_jax.random.key(seed), 7919)
    shape, _ = _output_shape()
    out = dict(inp)
    out["loss_target"] = _jax.random.normal(_jax.random.fold_in(key, 0), shape, _jnp.float32)
    for i, name in enumerate(TWIN_WEIGHTS):
        w = inp[name].astype(_jnp.float32)
        if MOMENT_SCALE is None:
            s = _jnp.sqrt(_jnp.mean(_jnp.square(w)) + 1e-30)
        else:
            s = MOMENT_SCALE[name]
        km, kv = _jax.random.split(_jax.random.fold_in(key, i + 1))
        out[name] = w
        out["m_" + name] = s * _jax.random.normal(km, w.shape, _jnp.float32)
        out["v_" + name] = (s * s) * _jax.random.uniform(kv, w.shape, _jnp.float32, 0.5, 1.5)
    if N_MICROBATCH > 1:
        for name, axis in PER_EXAMPLE_BATCH_AXIS.items():
            out[name] = _to_microbatches(out[name], axis)
    return {'x': out['x'], 'mem': out['mem'], 'positions': out['positions'], 'ffn1_norm': out['ffn1_norm'], 'ffn1_w_gate': out['ffn1_w_gate'], 'ffn1_w_up': out['ffn1_w_up'], 'ffn1_w_down': out['ffn1_w_down'], 'mix_norm': out['mix_norm'], 'w_in': out['w_in'], 'mla_q_norm': out['mla_q_norm'], 'mla_w_uq': out['mla_w_uq'], 'mla_kv_norm': out['mla_kv_norm'], 'mla_w_ukv': out['mla_w_ukv'], 'mla_qk_norm_q': out['mla_qk_norm_q'], 'mla_qk_norm_k': out['mla_qk_norm_k'], 'ssm_a_re': out['ssm_a_re'], 'ssm_a_im': out['ssm_a_im'], 'ssm_log_dt': out['ssm_log_dt'], 'ssm_b_re': out['ssm_b_re'], 'ssm_b_im': out['ssm_b_im'], 'ssm_c_re': out['ssm_c_re'], 'ssm_c_im': out['ssm_c_im'], 'ssm_d': out['ssm_d'], 'ssm_w_glu': out['ssm_w_glu'], 'ssm_b_glu': out['ssm_b_glu'], 'out_norm_mla': out['out_norm_mla'], 'out_norm_ssm': out['out_norm_ssm'], 'w_o': out['w_o'], 'xattn_norm': out['xattn_norm'], 'mem_norm': out['mem_norm'], 'xattn_w_q': out['xattn_w_q'], 'xattn_w_kv': out['xattn_w_kv'], 'xattn_q_norm': out['xattn_q_norm'], 'xattn_k_norm': out['xattn_k_norm'], 'xattn_w_o': out['xattn_w_o'], 'ffn2_norm': out['ffn2_norm'], 'ffn2_w_gate': out['ffn2_w_gate'], 'ffn2_w_up': out['ffn2_w_up'], 'ffn2_w_down': out['ffn2_w_down'], 'loss_target': out['loss_target'], 'm_ffn1_norm': out['m_ffn1_norm'], 'm_ffn1_w_gate': out['m_ffn1_w_gate'], 'm_ffn1_w_up': out['m_ffn1_w_up'], 'm_ffn1_w_down': out['m_ffn1_w_down'], 'm_mix_norm': out['m_mix_norm'], 'm_w_in': out['m_w_in'], 'm_mla_q_norm': out['m_mla_q_norm'], 'm_mla_w_uq': out['m_mla_w_uq'], 'm_mla_kv_norm': out['m_mla_kv_norm'], 'm_mla_w_ukv': out['m_mla_w_ukv'], 'm_mla_qk_norm_q': out['m_mla_qk_norm_q'], 'm_mla_qk_norm_k': out['m_mla_qk_norm_k'], 'm_ssm_a_re': out['m_ssm_a_re'], 'm_ssm_a_im': out['m_ssm_a_im'], 'm_ssm_log_dt': out['m_ssm_log_dt'], 'm_ssm_b_re': out['m_ssm_b_re'], 'm_ssm_b_im': out['m_ssm_b_im'], 'm_ssm_c_re': out['m_ssm_c_re'], 'm_ssm_c_im': out['m_ssm_c_im'], 'm_ssm_d': out['m_ssm_d'], 'm_ssm_w_glu': out['m_ssm_w_glu'], 'm_ssm_b_glu': out['m_ssm_b_glu'], 'm_out_norm_mla': out['m_out_norm_mla'], 'm_out_norm_ssm': out['m_out_norm_ssm'], 'm_w_o': out['m_w_o'], 'm_xattn_norm': out['m_xattn_norm'], 'm_mem_norm': out['m_mem_norm'], 'm_xattn_w_q': out['m_xattn_w_q'], 'm_xattn_w_kv': out['m_xattn_w_kv'], 'm_xattn_q_norm': out['m_xattn_q_norm'], 'm_xattn_k_norm': out['m_xattn_k_norm'], 'm_xattn_w_o': out['m_xattn_w_o'], 'm_ffn2_norm': out['m_ffn2_norm'], 'm_ffn2_w_gate': out['m_ffn2_w_gate'], 'm_ffn2_w_up': out['m_ffn2_w_up'], 'm_ffn2_w_down': out['m_ffn2_w_down'], 'v_ffn1_norm': out['v_ffn1_norm'], 'v_ffn1_w_gate': out['v_ffn1_w_gate'], 'v_ffn1_w_up': out['v_ffn1_w_up'], 'v_ffn1_w_down': out['v_ffn1_w_down'], 'v_mix_norm': out['v_mix_norm'], 'v_w_in': out['v_w_in'], 'v_mla_q_norm': out['v_mla_q_norm'], 'v_mla_w_uq': out['v_mla_w_uq'], 'v_mla_kv_norm': out['v_mla_kv_norm'], 'v_mla_w_ukv': out['v_mla_w_ukv'], 'v_mla_qk_norm_q': out['v_mla_qk_norm_q'], 'v_mla_qk_norm_k': out['v_mla_qk_norm_k'], 'v_ssm_a_re': out['v_ssm_a_re'], 'v_ssm_a_im': out['v_ssm_a_im'], 'v_ssm_log_dt': out['v_ssm_log_dt'], 'v_ssm_b_re': out['v_ssm_b_re'], 'v_ssm_b_im': out['v_ssm_b_im'], 'v_ssm_c_re': out['v_ssm_c_re'], 'v_ssm_c_im': out['v_ssm_c_im'], 'v_ssm_d': out['v_ssm_d'], 'v_ssm_w_glu': out['v_ssm_w_glu'], 'v_ssm_b_glu': out['v_ssm_b_glu'], 'v_out_norm_mla': out['v_out_norm_mla'], 'v_out_norm_ssm': out['v_out_norm_ssm'], 'v_w_o': out['v_w_o'], 'v_xattn_norm': out['v_xattn_norm'], 'v_mem_norm': out['v_mem_norm'], 'v_xattn_w_q': out['v_xattn_w_q'], 'v_xattn_w_kv': out['v_xattn_w_kv'], 'v_xattn_q_norm': out['v_xattn_q_norm'], 'v_xattn_k_norm': out['v_xattn_k_norm'], 'v_xattn_w_o': out['v_xattn_w_o'], 'v_ffn2_norm': out['v_ffn2_norm'], 'v_ffn2_w_gate': out['v_ffn2_w_gate'], 'v_ffn2_w_up': out['v_ffn2_w_up'], 'v_ffn2_w_down': out['v_ffn2_w_down']}


def _loss(weights, diff, rest, loss_target):
    with _jax.named_scope("forward"):
        args = {**rest, TWIN_DIFF_INPUT: diff, **{k: w.astype(_WEIGHT_DTYPES[k]) for k, w in weights.items()}}
        y = _forward(args)
    with _jax.named_scope("loss_head"):
        err = _jnp.square(y.astype(_jnp.float32) - loss_target)
        return 0.5 * _jnp.sum(_jnp.mean(err, axis=-1)) if err.ndim else 0.5 * err


def _adamw(w, g, m, v):
    m = ADAM_B1 * m + (1.0 - ADAM_B1) * g
    v = ADAM_B2 * v + (1.0 - ADAM_B2) * _jnp.square(g)
    m_hat = m / (1.0 - ADAM_B1 ** ADAM_STEP)
    v_hat = v / (1.0 - ADAM_B2 ** ADAM_STEP)
    delta = -ADAM_LR * (m_hat / (_jnp.sqrt(v_hat) + ADAM_EPS) + ADAM_WD * w)
    return delta, m, v


def reference(x, mem, positions, ffn1_norm, ffn1_w_gate, ffn1_w_up, ffn1_w_down, mix_norm, w_in, mla_q_norm, mla_w_uq, mla_kv_norm, mla_w_ukv, mla_qk_norm_q, mla_qk_norm_k, ssm_a_re, ssm_a_im, ssm_log_dt, ssm_b_re, ssm_b_im, ssm_c_re, ssm_c_im, ssm_d, ssm_w_glu, ssm_b_glu, out_norm_mla, out_norm_ssm, w_o, xattn_norm, mem_norm, xattn_w_q, xattn_w_kv, xattn_q_norm, xattn_k_norm, xattn_w_o, ffn2_norm, ffn2_w_gate, ffn2_w_up, ffn2_w_down, loss_target, m_ffn1_norm, m_ffn1_w_gate, m_ffn1_w_up, m_ffn1_w_down, m_mix_norm, m_w_in, m_mla_q_norm, m_mla_w_uq, m_mla_kv_norm, m_mla_w_ukv, m_mla_qk_norm_q, m_mla_qk_norm_k, m_ssm_a_re, m_ssm_a_im, m_ssm_log_dt, m_ssm_b_re, m_ssm_b_im, m_ssm_c_re, m_ssm_c_im, m_ssm_d, m_ssm_w_glu, m_ssm_b_glu, m_out_norm_mla, m_out_norm_ssm, m_w_o, m_xattn_norm, m_mem_norm, m_xattn_w_q, m_xattn_w_kv, m_xattn_q_norm, m_xattn_k_norm, m_xattn_w_o, m_ffn2_norm, m_ffn2_w_gate, m_ffn2_w_up, m_ffn2_w_down, v_ffn1_norm, v_ffn1_w_gate, v_ffn1_w_up, v_ffn1_w_down, v_mix_norm, v_w_in, v_mla_q_norm, v_mla_w_uq, v_mla_kv_norm, v_mla_w_ukv, v_mla_qk_norm_q, v_mla_qk_norm_k, v_ssm_a_re, v_ssm_a_im, v_ssm_log_dt, v_ssm_b_re, v_ssm_b_im, v_ssm_c_re, v_ssm_c_im, v_ssm_d, v_ssm_w_glu, v_ssm_b_glu, v_out_norm_mla, v_out_norm_ssm, v_w_o, v_xattn_norm, v_mem_norm, v_xattn_w_q, v_xattn_w_kv, v_xattn_q_norm, v_xattn_k_norm, v_xattn_w_o, v_ffn2_norm, v_ffn2_w_gate, v_ffn2_w_up, v_ffn2_w_down):
    given = dict(x=x, mem=mem, positions=positions, ffn1_norm=ffn1_norm, ffn1_w_gate=ffn1_w_gate, ffn1_w_up=ffn1_w_up, ffn1_w_down=ffn1_w_down, mix_norm=mix_norm, w_in=w_in, mla_q_norm=mla_q_norm, mla_w_uq=mla_w_uq, mla_kv_norm=mla_kv_norm, mla_w_ukv=mla_w_ukv, mla_qk_norm_q=mla_qk_norm_q, mla_qk_norm_k=mla_qk_norm_k, ssm_a_re=ssm_a_re, ssm_a_im=ssm_a_im, ssm_log_dt=ssm_log_dt, ssm_b_re=ssm_b_re, ssm_b_im=ssm_b_im, ssm_c_re=ssm_c_re, ssm_c_im=ssm_c_im, ssm_d=ssm_d, ssm_w_glu=ssm_w_glu, ssm_b_glu=ssm_b_glu, out_norm_mla=out_norm_mla, out_norm_ssm=out_norm_ssm, w_o=w_o, xattn_norm=xattn_norm, mem_norm=mem_norm, xattn_w_q=xattn_w_q, xattn_w_kv=xattn_w_kv, xattn_q_norm=xattn_q_norm, xattn_k_norm=xattn_k_norm, xattn_w_o=xattn_w_o, ffn2_norm=ffn2_norm, ffn2_w_gate=ffn2_w_gate, ffn2_w_up=ffn2_w_up, ffn2_w_down=ffn2_w_down, loss_target=loss_target, m_ffn1_norm=m_ffn1_norm, m_ffn1_w_gate=m_ffn1_w_gate, m_ffn1_w_up=m_ffn1_w_up, m_ffn1_w_down=m_ffn1_w_down, m_mix_norm=m_mix_norm, m_w_in=m_w_in, m_mla_q_norm=m_mla_q_norm, m_mla_w_uq=m_mla_w_uq, m_mla_kv_norm=m_mla_kv_norm, m_mla_w_ukv=m_mla_w_ukv, m_mla_qk_norm_q=m_mla_qk_norm_q, m_mla_qk_norm_k=m_mla_qk_norm_k, m_ssm_a_re=m_ssm_a_re, m_ssm_a_im=m_ssm_a_im, m_ssm_log_dt=m_ssm_log_dt, m_ssm_b_re=m_ssm_b_re, m_ssm_b_im=m_ssm_b_im, m_ssm_c_re=m_ssm_c_re, m_ssm_c_im=m_ssm_c_im, m_ssm_d=m_ssm_d, m_ssm_w_glu=m_ssm_w_glu, m_ssm_b_glu=m_ssm_b_glu, m_out_norm_mla=m_out_norm_mla, m_out_norm_ssm=m_out_norm_ssm, m_w_o=m_w_o, m_xattn_norm=m_xattn_norm, m_mem_norm=m_mem_norm, m_xattn_w_q=m_xattn_w_q, m_xattn_w_kv=m_xattn_w_kv, m_xattn_q_norm=m_xattn_q_norm, m_xattn_k_norm=m_xattn_k_norm, m_xattn_w_o=m_xattn_w_o, m_ffn2_norm=m_ffn2_norm, m_ffn2_w_gate=m_ffn2_w_gate, m_ffn2_w_up=m_ffn2_w_up, m_ffn2_w_down=m_ffn2_w_down, v_ffn1_norm=v_ffn1_norm, v_ffn1_w_gate=v_ffn1_w_gate, v_ffn1_w_up=v_ffn1_w_up, v_ffn1_w_down=v_ffn1_w_down, v_mix_norm=v_mix_norm, v_w_in=v_w_in, v_mla_q_norm=v_mla_q_norm, v_mla_w_uq=v_mla_w_uq, v_mla_kv_norm=v_mla_kv_norm, v_mla_w_ukv=v_mla_w_ukv, v_mla_qk_norm_q=v_mla_qk_norm_q, v_mla_qk_norm_k=v_mla_qk_norm_k, v_ssm_a_re=v_ssm_a_re, v_ssm_a_im=v_ssm_a_im, v_ssm_log_dt=v_ssm_log_dt, v_ssm_b_re=v_ssm_b_re, v_ssm_b_im=v_ssm_b_im, v_ssm_c_re=v_ssm_c_re, v_ssm_c_im=v_ssm_c_im, v_ssm_d=v_ssm_d, v_ssm_w_glu=v_ssm_w_glu, v_ssm_b_glu=v_ssm_b_glu, v_out_norm_mla=v_out_norm_mla, v_out_norm_ssm=v_out_norm_ssm, v_w_o=v_w_o, v_xattn_norm=v_xattn_norm, v_mem_norm=v_mem_norm, v_xattn_w_q=v_xattn_w_q, v_xattn_w_kv=v_xattn_w_kv, v_xattn_q_norm=v_xattn_q_norm, v_xattn_k_norm=v_xattn_k_norm, v_xattn_w_o=v_xattn_w_o, v_ffn2_norm=v_ffn2_norm, v_ffn2_w_gate=v_ffn2_w_gate, v_ffn2_w_up=v_ffn2_w_up, v_ffn2_w_down=v_ffn2_w_down)
    weights = {n: given[n] for n in TWIN_WEIGHTS}
    shared = {n: given[n] for n in SHARED_INPUTS}
    per_example = {n: given[n] for n in ['x', 'mem', 'positions']}
    grad_fn = _jax.value_and_grad(_loss, argnums=(0, 1))

    def one_microbatch(ex, loss_target):
        ex = dict(ex)
        diff = ex.pop(TWIN_DIFF_INPUT)
        return grad_fn(weights, diff, {**shared, **ex}, loss_target)

    if N_MICROBATCH == 1:
        loss, (grad_w, grad_x) = one_microbatch(per_example, given["loss_target"])
    else:
        def body(carry, xs):
            loss_sum, grad_sum = carry
            l_k, (gw_k, gx_k) = one_microbatch(xs[0], xs[1])
            with _jax.named_scope("update"):
                return (loss_sum + l_k, _jax.tree.map(_jnp.add, grad_sum, gw_k)), gx_k

        init = (_jnp.zeros((), _jnp.float32), _jax.tree.map(_jnp.zeros_like, weights))
        (loss, grad_w), grad_x = _jax.lax.scan(body, init, (per_example, given["loss_target"]))
    with _jax.named_scope("update"):
        delta_w, new_m, new_v = {}, {}, {}
        for n in TWIN_WEIGHTS:
            delta_w[n], new_m[n], new_v[n] = _adamw(weights[n], grad_w[n], given["m_" + n], given["v_" + n])
    return (loss, grad_x, *[grad_w[n] for n in TWIN_WEIGHTS], *[delta_w[n] for n in TWIN_WEIGHTS],
            *[new_m[n] for n in TWIN_WEIGHTS], *[new_v[n] for n in TWIN_WEIGHTS])
```

```python
import functools
import math

import jax
import jax.numpy as jnp
import numpy as np
from jax import lax
from jax.experimental import pallas as pl
from jax.experimental.pallas import tpu as pltpu

F32, BF16 = jnp.float32, jnp.bfloat16
EPS = 1e-6
MESH = pl.DeviceIdType.MESH

D_MODEL, D_FF = 1024, 2752
MLA_HEADS, MLA_Q_RANK, MLA_KV_RANK, MLA_NOPE, MLA_ROPE, MLA_V = 4, 384, 256, 128, 64, 128
MLA_QK = MLA_NOPE + MLA_ROPE
HEAD_PAD = 256
SSM_WIDTH, SSM_GROUP, SSM_GROUPS, SSM_STATE = 512, 16, 32, 64
SSM_LANES = SSM_GROUPS * SSM_STATE
XH, XHD = 4, 128
ROPE_THETA = 10000.0
ADAM_LR, ADAM_B1, ADAM_B2, ADAM_EPS, ADAM_WD, ADAM_STEP = 0.001, 0.9, 0.999, 1e-08, 0.01, 10
N_CHIPS, N_CORES, N_DEV = 4, 2, 8

VMEM_LIMIT_BYTES = 56 * 2**20
TOKEN_TILE = 512
ATTN_TILE = 512
SCAN_TIME_TILE = 1024
SCAN_LANE_TILE = 256
SUBLANES = 8


def _params(n_axes):
    return pltpu.CompilerParams(dimension_semantics=("arbitrary",) * n_axes, vmem_limit_bytes=VMEM_LIMIT_BYTES)


def _first(axes):
    cond = None
    for a in axes:
        c = pl.program_id(a) == 0
        cond = c if cond is None else jnp.logical_and(cond, c)
    return cond


def mm(name, xs, ws, out_dtypes, *, trans=False, adds=None, tm=TOKEN_TILE):
    rows = xs[0].shape[0]
    tm = min(tm, rows)
    n_in, n_out = len(xs), len(out_dtypes)
    pairs = [(i, j) for i in range(n_in) for j in range(n_out) if ws[i][j] is not None]
    w_list = [ws[i][j] for (i, j) in pairs]
    adds = list(adds) if adds is not None else [None] * n_out
    add_list = [a for a in adds if a is not None]
    out_cols = [None] * n_out
    for (i, j), w in zip(pairs, w_list):
        out_cols[j] = w.shape[0] if trans else w.shape[1]
    contract = (((1,), (1 if trans else 0,)), ((), ()))

    def body(*refs):
        x_refs = refs[:n_in]
        w_refs = refs[n_in:n_in + len(pairs)]
        a_refs = list(refs[n_in + len(pairs):n_in + len(pairs) + len(add_list)])
        o_refs = refs[n_in + len(pairs) + len(add_list):]
        xb = [None] * n_in
        for j in range(n_out):
            acc = None
            for p, (i, jj) in enumerate(pairs):
                if jj != j:
                    continue
                if xb[i] is None:
                    xb[i] = x_refs[i][...].astype(BF16)
                d = lax.dot_general(xb[i], w_refs[p][...].astype(BF16), contract, preferred_element_type=F32)
                acc = d if acc is None else acc + d
            if adds[j] is not None:
                acc = acc + a_refs.pop(0)[...].astype(F32)
            o_refs[j][...] = acc.astype(o_refs[j].dtype)

    in_specs = ([pl.BlockSpec((tm, x.shape[1]), lambda i: (i, 0)) for x in xs]
                + [pl.BlockSpec(w.shape, lambda i: (0, 0)) for w in w_list]
                + [pl.BlockSpec((tm, a.shape[1]), lambda i: (i, 0)) for a in add_list])
    outs = pl.pallas_call(
        body, name=name, grid=(rows // tm,), in_specs=in_specs,
        out_specs=[pl.BlockSpec((tm, n), lambda i: (i, 0)) for n in out_cols],
        out_shape=[jax.ShapeDtypeStruct((rows, n), dt) for n, dt in zip(out_cols, out_dtypes)],
        compiler_params=_params(1),
    )(*xs, *w_list, *add_list)
    return list(outs)


def mm_tn(name, xs, dys, pairs, *, tm=TOKEN_TILE):
    rows = xs[0].shape[0]
    tm = min(tm, rows)
    n_x, n_dy = len(xs), len(dys)
    contract = (((0,), (0,)), ((), ()))

    def body(*refs):
        x_refs, dy_refs, o_refs = refs[:n_x], refs[n_x:n_x + n_dy], refs[n_x + n_dy:]
        first = pl.program_id(0) == 0
        for k, (i, js) in enumerate(pairs):
            dy = None
            for j in js:
                t = dy_refs[j][...].astype(F32)
                dy = t if dy is None else dy + t
            d = lax.dot_general(x_refs[i][...].astype(BF16), dy.astype(BF16), contract, preferred_element_type=F32)

            @pl.when(first)
            def _(d=d, k=k):
                o_refs[k][...] = d

            @pl.when(jnp.logical_not(first))
            def _(d=d, k=k):
                o_refs[k][...] += d

    shapes = [(xs[i].shape[1], dys[js[0]].shape[1]) for (i, js) in pairs]
    outs = pl.pallas_call(
        body, name=name, grid=(rows // tm,),
        in_specs=[pl.BlockSpec((tm, a.shape[1]), lambda i: (i, 0)) for a in (*xs, *dys)],
        out_specs=[pl.BlockSpec(s, lambda i: (0, 0)) for s in shapes],
        out_shape=[jax.ShapeDtypeStruct(s, F32) for s in shapes],
        compiler_params=_params(1),
    )(*xs, *dys)
    return list(outs)


class In:
    def __init__(self, arr, block, imap, *, diff=False, acc=None, grad=None):
        self.arr, self.block, self.imap, self.diff, self.acc, self.grad = arr, block, imap, diff, acc, grad

    def spec(self):
        return pl.BlockSpec(self.block, self.imap)


class Out:
    def __init__(self, shape, dtype, block, imap):
        self.shape, self.dtype, self.block, self.imap = shape, dtype, block, imap

    def spec(self):
        return pl.BlockSpec(self.block, self.imap)


def seg_fwd(name, f, grid, ins, outs):
    n_in = len(ins)

    def body(*refs):
        res = f(*[r[...] for r in refs[:n_in]])
        for o_ref, r in zip(refs[n_in:], res):
            o_ref[...] = r.astype(o_ref.dtype)

    res = pl.pallas_call(
        body, name=name, grid=grid, in_specs=[i.spec() for i in ins], out_specs=[o.spec() for o in outs],
        out_shape=[jax.ShapeDtypeStruct(o.shape, o.dtype) for o in outs], compiler_params=_params(len(grid)),
    )(*[i.arr for i in ins])
    return list(res)


def seg_bwd(name, f, grid, ins, outs, cts, adds=None):
    n_in, n_ct = len(ins), len(cts)
    diff_idx = [k for k, i in enumerate(ins) if i.diff]
    adds = adds or {}
    add_keys = sorted(adds)
    add_list = [adds[k] for k in add_keys]

    def body(*refs):
        in_refs, ct_refs = refs[:n_in], refs[n_in:n_in + n_ct]
        add_refs = dict(zip(add_keys, refs[n_in + n_ct:n_in + n_ct + len(add_list)]))
        g_refs = refs[n_in + n_ct + len(add_list):]
        vals = [r[...] for r in in_refs]

        def g(*dv):
            full = list(vals)
            for k, v in zip(diff_idx, dv):
                full[k] = v
            return tuple(f(*full))

        _, pull = jax.vjp(g, *[vals[k].astype(F32) for k in diff_idx])
        grads = pull(tuple(c[...].astype(F32) for c in ct_refs))
        for n, (k, gr) in enumerate(zip(diff_idx, grads)):
            if k in add_refs:
                gr = gr + add_refs[k][...].astype(F32)
            if ins[k].acc is None:
                g_refs[n][...] = gr.astype(g_refs[n].dtype)
            else:
                first = _first(ins[k].acc)

                @pl.when(first)
                def _(gr=gr, n=n):
                    g_refs[n][...] = gr

                @pl.when(jnp.logical_not(first))
                def _(gr=gr, n=n):
                    g_refs[n][...] += gr

    g_specs, g_shapes = [], []
    for k in diff_idx:
        i = ins[k]
        shape, block, imap = i.grad if i.grad is not None else (i.arr.shape, i.block, i.imap)
        g_specs.append(pl.BlockSpec(block, imap))
        g_shapes.append(jax.ShapeDtypeStruct(shape, F32))
    in_specs = ([i.spec() for i in ins] + [o.spec() for o in outs]
                + [pl.BlockSpec(ins[k].block, ins[k].imap) for k in add_keys])
    res = pl.pallas_call(
        body, name=name, grid=grid, in_specs=in_specs, out_specs=g_specs, out_shape=g_shapes,
        compiler_params=_params(len(grid)),
    )(*[i.arr for i in ins], *cts, *add_list)
    return list(res)


def _rms(x, g):
    return x * lax.rsqrt(jnp.mean(x * x, axis=-1, keepdims=True) + EPS) * g


@jax.custom_vjp
def _bdot_nt(a, b):
    return lax.dot_general(a.astype(BF16), b.astype(BF16), (((1,), (1,)), ((), ())), preferred_element_type=F32)


def _bdot_nt_fwd(a, b):
    return _bdot_nt(a, b), (a, b)


def _bdot_nt_bwd(res, g):
    a, b = res
    gb = g.astype(BF16)
    da = lax.dot_general(gb, b.astype(BF16), (((1,), (0,)), ((), ())), preferred_element_type=F32)
    db = lax.dot_general(gb, a.astype(BF16), (((0,), (0,)), ((), ())), preferred_element_type=F32)
    return da, db


_bdot_nt.defvjp(_bdot_nt_fwd, _bdot_nt_bwd)


@jax.custom_vjp
def _bdot_nn(a, b):
    return lax.dot_general(a.astype(BF16), b.astype(BF16), (((1,), (0,)), ((), ())), preferred_element_type=F32)


def _bdot_nn_fwd(a, b):
    return _bdot_nn(a, b), (a, b)


def _bdot_nn_bwd(res, g):
    a, b = res
    gb = g.astype(BF16)
    da = lax.dot_general(gb, b.astype(BF16), (((1,), (1,)), ((), ())), preferred_element_type=F32)
    db = lax.dot_general(a.astype(BF16), gb, (((0,), (0,)), ((), ())), preferred_element_type=F32)
    return da, db


_bdot_nn.defvjp(_bdot_nn_fwd, _bdot_nn_bwd)


@functools.partial(jax.custom_vjp, nondiff_argnums=(1,))
def _lane_roll(x, shift):
    return pltpu.roll(x, shift, 1)


def _lane_roll_fwd(x, shift):
    return pltpu.roll(x, shift, 1), None


def _lane_roll_bwd(shift, _, g):
    return (pltpu.roll(g, (g.shape[1] - shift) % g.shape[1], 1),)


_lane_roll.defvjp(_lane_roll_fwd, _lane_roll_bwd)


def _hp_dot(a, b):
    return jnp.dot(a, b, precision=lax.Precision.HIGHEST, preferred_element_type=F32)


def ffn_fwd(name, x, gain, wg, wu, wd, tm=TOKEN_TILE):
    S, D = x.shape
    nsh, _, Fs = wg.shape

    def body(x_ref, gain_ref, wg_ref, wu_ref, wd_ref, xo_ref, h_ref, g_ref, u_ref, acc):
        j = pl.program_id(1)

        @pl.when(j == 0)
        def _():
            h_ref[...] = _rms(x_ref[...], gain_ref[...]).astype(BF16)
            acc[...] = jnp.zeros_like(acc)

        h = h_ref[...]
        g = jnp.dot(h, wg_ref[...], preferred_element_type=F32)
        u = jnp.dot(h, wu_ref[...], preferred_element_type=F32)
        g_ref[...] = g.astype(BF16)
        u_ref[...] = u.astype(BF16)
        a = g * jax.nn.sigmoid(g) * u
        acc[...] += jnp.dot(a.astype(BF16), wd_ref[...], preferred_element_type=F32)

        @pl.when(j == nsh - 1)
        def _():
            xo_ref[...] = x_ref[...] + 0.5 * acc[...]

    return pl.pallas_call(
        body, name=name, grid=(S // tm, nsh),
        in_specs=[pl.BlockSpec((tm, D), lambda i, j: (i, 0)), pl.BlockSpec((1, D), lambda i, j: (0, 0)),
                  pl.BlockSpec((None, D, Fs), lambda i, j: (j, 0, 0)), pl.BlockSpec((None, D, Fs), lambda i, j: (j, 0, 0)),
                  pl.BlockSpec((None, Fs, D), lambda i, j: (j, 0, 0))],
        out_specs=[pl.BlockSpec((tm, D), lambda i, j: (i, 0)), pl.BlockSpec((tm, D), lambda i, j: (i, 0)),
                   pl.BlockSpec((None, tm, Fs), lambda i, j: (j, i, 0)), pl.BlockSpec((None, tm, Fs), lambda i, j: (j, i, 0))],
        out_shape=[jax.ShapeDtypeStruct((S, D), F32), jax.ShapeDtypeStruct((S, D), BF16),
                   jax.ShapeDtypeStruct((nsh, S, Fs), BF16), jax.ShapeDtypeStruct((nsh, S, Fs), BF16)],
        scratch_shapes=[pltpu.VMEM((tm, D), F32)], compiler_params=_params(2),
    )(x, gain, wg, wu, wd)


def ffn_bwd_act(name, dxo, x, gain, g, u, wg, wu, wd, tm=TOKEN_TILE):
    S, D = x.shape
    nsh, _, Fs = wg.shape
    nt_dims = (((1,), (1,)), ((), ()))

    def body(dxo_ref, x_ref, gain_ref, g_ref, u_ref, wg_ref, wu_ref, wd_ref, dx_ref, dgain_ref, dg_ref, du_ref, dd, dh):
        i, j = pl.program_id(0), pl.program_id(1)

        @pl.when(j == 0)
        def _():
            dd[...] = (0.5 * dxo_ref[...]).astype(BF16)
            dh[...] = jnp.zeros_like(dh)

        da = lax.dot_general(dd[...], wd_ref[...], nt_dims, preferred_element_type=F32)
        gf, uf = g_ref[...].astype(F32), u_ref[...].astype(F32)
        sig = jax.nn.sigmoid(gf)
        dgv = (da * uf * (sig * (1.0 + gf * (1.0 - sig)))).astype(BF16)
        duv = (da * (gf * sig)).astype(BF16)
        dg_ref[...] = dgv
        du_ref[...] = duv
        dh[...] += (lax.dot_general(dgv, wg_ref[...], nt_dims, preferred_element_type=F32)
                    + lax.dot_general(duv, wu_ref[...], nt_dims, preferred_element_type=F32))

        @pl.when(j == nsh - 1)
        def _():
            xv = x_ref[...]
            r = lax.rsqrt(jnp.mean(xv * xv, axis=-1, keepdims=True) + EPS)
            xhat = xv * r
            dhv = dh[...]
            dxn = dhv * gain_ref[...]
            dx_ref[...] = dxo_ref[...] + r * (dxn - xhat * jnp.mean(dxn * xhat, axis=-1, keepdims=True))
            part = jnp.sum(dhv * xhat, axis=0, keepdims=True)

            @pl.when(i == 0)
            def _():
                dgain_ref[...] = part

            @pl.when(i != 0)
            def _():
                dgain_ref[...] += part

    return pl.pallas_call(
        body, name=name, grid=(S // tm, nsh),
        in_specs=[pl.BlockSpec((tm, D), lambda i, j: (i, 0)), pl.BlockSpec((tm, D), lambda i, j: (i, 0)),
                  pl.BlockSpec((1, D), lambda i, j: (0, 0)),
                  pl.BlockSpec((None, tm, Fs), lambda i, j: (j, i, 0)), pl.BlockSpec((None, tm, Fs), lambda i, j: (j, i, 0)),
                  pl.BlockSpec((None, D, Fs), lambda i, j: (j, 0, 0)), pl.BlockSpec((None, D, Fs), lambda i, j: (j, 0, 0)),
                  pl.BlockSpec((None, Fs, D), lambda i, j: (j, 0, 0))],
        out_specs=[pl.BlockSpec((tm, D), lambda i, j: (i, 0)), pl.BlockSpec((1, D), lambda i, j: (0, 0)),
                   pl.BlockSpec((None, tm, Fs), lambda i, j: (j, i, 0)), pl.BlockSpec((None, tm, Fs), lambda i, j: (j, i, 0))],
        out_shape=[jax.ShapeDtypeStruct((S, D), F32), jax.ShapeDtypeStruct((1, D), F32),
                   jax.ShapeDtypeStruct((nsh, S, Fs), BF16), jax.ShapeDtypeStruct((nsh, S, Fs), BF16)],
        scratch_shapes=[pltpu.VMEM((tm, D), BF16), pltpu.VMEM((tm, D), F32)], compiler_params=_params(2),
    )(dxo, x, gain, g, u, wg, wu, wd)


def ffn_bwd_w(name, h, dxo, g, u, dg, du, tm=TOKEN_TILE):
    S, D = h.shape
    nsh, _, Fs = g.shape
    tn_dims = (((0,), (0,)), ((), ()))

    def body(h_ref, dxo_ref, g_ref, u_ref, dg_ref, du_ref, dwg_ref, dwu_ref, dwd_ref):
        i = pl.program_id(1)
        gf, uf = g_ref[...].astype(F32), u_ref[...].astype(F32)
        a = (gf * jax.nn.sigmoid(gf) * uf).astype(BF16)
        dd = (0.5 * dxo_ref[...]).astype(BF16)
        hv = h_ref[...]
        pg = lax.dot_general(hv, dg_ref[...], tn_dims, preferred_element_type=F32)
        pu = lax.dot_general(hv, du_ref[...], tn_dims, preferred_element_type=F32)
        pd = lax.dot_general(a, dd, tn_dims, preferred_element_type=F32)

        @pl.when(i == 0)
        def _():
            dwg_ref[...] = pg
            dwu_ref[...] = pu
            dwd_ref[...] = pd

        @pl.when(i != 0)
        def _():
            dwg_ref[...] += pg
            dwu_ref[...] += pu
            dwd_ref[...] += pd

    act = pl.BlockSpec((None, tm, Fs), lambda j, i: (j, i, 0))
    return pl.pallas_call(
        body, name=name, grid=(nsh, S // tm),
        in_specs=[pl.BlockSpec((tm, D), lambda j, i: (i, 0)), pl.BlockSpec((tm, D), lambda j, i: (i, 0)), act, act, act, act],
        out_specs=[pl.BlockSpec((None, D, Fs), lambda j, i: (j, 0, 0)), pl.BlockSpec((None, D, Fs), lambda j, i: (j, 0, 0)),
                   pl.BlockSpec((None, Fs, D), lambda j, i: (j, 0, 0))],
        out_shape=[jax.ShapeDtypeStruct((nsh, D, Fs), F32), jax.ShapeDtypeStruct((nsh, D, Fs), F32),
                   jax.ShapeDtypeStruct((nsh, Fs, D), F32)],
        compiler_params=_params(2),
    )(h, dxo, g, u, dg, du)


NEG_BIG = -1e30


def _scores(q, k, scale, diag):
    s = lax.dot_general(q, k, (((1,), (1,)), ((), ())), preferred_element_type=F32) * scale
    row = lax.broadcasted_iota(jnp.int32, s.shape, 0)
    col = lax.broadcasted_iota(jnp.int32, s.shape, 1)
    return jnp.where(jnp.logical_or(jnp.logical_not(diag), row >= col), s, NEG_BIG)


def attn_fwd(name, q, k, v, scale, t=ATTN_TILE):
    H, S, Dk = q.shape
    Dv = v.shape[1] // H
    n = S // t

    def body(q_ref, k_ref, v_ref, o_ref, lse_ref, m_sc, l_sc, acc):
        qi, ki = pl.program_id(1), pl.program_id(2)

        @pl.when(ki == 0)
        def _():
            m_sc[...] = jnp.full_like(m_sc, NEG_BIG)
            l_sc[...] = jnp.zeros_like(l_sc)
            acc[...] = jnp.zeros_like(acc)

        @pl.when(ki <= qi)
        def _():
            s = _scores(q_ref[...], k_ref[...], scale, ki == qi)
            m_new = jnp.maximum(m_sc[...], jnp.max(s, axis=-1, keepdims=True))
            alpha = jnp.exp(m_sc[...] - m_new)
            p = jnp.exp(s - m_new)
            l_sc[...] = alpha * l_sc[...] + jnp.sum(p, axis=-1, keepdims=True)
            acc[...] = alpha * acc[...] + jnp.dot(p.astype(BF16), v_ref[...].astype(BF16), preferred_element_type=F32)
            m_sc[...] = m_new

        @pl.when(ki == n - 1)
        def _():
            o_ref[...] = acc[...] / l_sc[...]
            lse_ref[...] = m_sc[...] + jnp.log(l_sc[...])

    return pl.pallas_call(
        body, name=name, grid=(H, n, n),
        in_specs=[pl.BlockSpec((None, t, Dk), lambda h, qi, ki: (h, qi, 0)),
                  pl.BlockSpec((None, t, Dk), lambda h, qi, ki: (h, jnp.minimum(ki, qi), 0)),
                  pl.BlockSpec((t, Dv), lambda h, qi, ki: (jnp.minimum(ki, qi), h))],
        out_specs=[pl.BlockSpec((t, Dv), lambda h, qi, ki: (qi, h)), pl.BlockSpec((None, t, 1), lambda h, qi, ki: (h, qi, 0))],
        out_shape=[jax.ShapeDtypeStruct((S, H * Dv), F32), jax.ShapeDtypeStruct((H, S, 1), F32)],
        scratch_shapes=[pltpu.VMEM((t, 1), F32), pltpu.VMEM((t, 1), F32), pltpu.VMEM((t, Dv), F32)],
        compiler_params=_params(3),
    )(q, k, v)


def _attn_probs(q_ref, k_ref, v_ref, do_ref, o_ref, lse_ref, scale, diag):
    s = _scores(q_ref[...], k_ref[...], scale, diag)
    p = jnp.exp(s - lse_ref[...])
    dov = do_ref[...]
    delta = jnp.sum(dov * o_ref[...], axis=-1, keepdims=True)
    dob = dov.astype(BF16)
    dp = lax.dot_general(dob, v_ref[...].astype(BF16), (((1,), (1,)), ((), ())), preferred_element_type=F32)
    ds = p * (dp - delta) * scale
    return p, ds, dob


def attn_bwd_kv(name, q, k, v, do, o, lse, scale, t=ATTN_TILE):
    H, S, Dk = q.shape
    Dv = v.shape[1] // H
    n = S // t
    tn_dims = (((0,), (0,)), ((), ()))

    def body(q_ref, k_ref, v_ref, do_ref, o_ref, lse_ref, dk_ref, dv_ref):
        ki, qi = pl.program_id(1), pl.program_id(2)

        @pl.when(qi == 0)
        def _():
            dk_ref[...] = jnp.zeros_like(dk_ref)
            dv_ref[...] = jnp.zeros_like(dv_ref)

        @pl.when(qi >= ki)
        def _():
            p, ds, dob = _attn_probs(q_ref, k_ref, v_ref, do_ref, o_ref, lse_ref, scale, ki == qi)
            dv_ref[...] += lax.dot_general(p.astype(BF16), dob, tn_dims, preferred_element_type=F32)
            dk_ref[...] += lax.dot_general(ds.astype(BF16), q_ref[...], tn_dims, preferred_element_type=F32)

    qmap = lambda h, ki, qi: (h, jnp.maximum(qi, ki), 0)
    rowmap = lambda h, ki, qi: (jnp.maximum(qi, ki), h)
    return pl.pallas_call(
        body, name=name, grid=(H, n, n),
        in_specs=[pl.BlockSpec((None, t, Dk), qmap), pl.BlockSpec((None, t, Dk), lambda h, ki, qi: (h, ki, 0)),
                  pl.BlockSpec((t, Dv), lambda h, ki, qi: (ki, h)), pl.BlockSpec((t, Dv), rowmap), pl.BlockSpec((t, Dv), rowmap),
                  pl.BlockSpec((None, t, 1), qmap)],
        out_specs=[pl.BlockSpec((None, t, Dk), lambda h, ki, qi: (h, ki, 0)), pl.BlockSpec((t, Dv), lambda h, ki, qi: (ki, h))],
        out_shape=[jax.ShapeDtypeStruct((H, S, Dk), F32), jax.ShapeDtypeStruct((S, H * Dv), F32)],
        compiler_params=_params(3),
    )(q, k, v, do, o, lse)


def attn_bwd_q(name, q, k, v, do, o, lse, scale, t=ATTN_TILE):
    H, S, Dk = q.shape
    Dv = v.shape[1] // H
    n = S // t

    def body(q_ref, k_ref, v_ref, do_ref, o_ref, lse_ref, dq_ref):
        qi, ki = pl.program_id(1), pl.program_id(2)

        @pl.when(ki == 0)
        def _():
            dq_ref[...] = jnp.zeros_like(dq_ref)

        @pl.when(ki <= qi)
        def _():
            _, ds, _ = _attn_probs(q_ref, k_ref, v_ref, do_ref, o_ref, lse_ref, scale, ki == qi)
            dq_ref[...] += jnp.dot(ds.astype(BF16), k_ref[...], preferred_element_type=F32)

    kmap = lambda h, qi, ki: (h, jnp.minimum(ki, qi), 0)
    rowmap = lambda h, qi, ki: (qi, h)
    return pl.pallas_call(
        body, name=name, grid=(H, n, n),
        in_specs=[pl.BlockSpec((None, t, Dk), lambda h, qi, ki: (h, qi, 0)), pl.BlockSpec((None, t, Dk), kmap),
                  pl.BlockSpec((t, Dv), lambda h, qi, ki: (jnp.minimum(ki, qi), h)), pl.BlockSpec((t, Dv), rowmap),
                  pl.BlockSpec((t, Dv), rowmap), pl.BlockSpec((None, t, 1), lambda h, qi, ki: (h, qi, 0))],
        out_specs=pl.BlockSpec((None, t, Dk), lambda h, qi, ki: (h, qi, 0)),
        out_shape=jax.ShapeDtypeStruct((H, S, Dk), F32),
        compiler_params=_params(3),
    )(q, k, v, do, o, lse)


def _cmul(ar, ai, br, bi):
    return ar * br - ai * bi, ar * bi + ai * br


def ssm_scan(name, br, bi, a_r8, a_i8, *, reverse=False, states=None):
    S, L = br.shape
    tc, lb = min(SCAN_TIME_TILE, S), SCAN_LANE_TILE
    nt, nblk = S // tc, tc // SUBLANES
    with_da = states is not None

    def body(*refs):
        if with_da:
            br_ref, bi_ref, ar_ref, ai_ref, xr_ref, xi_ref, pr_ref, pi_ref, or_ref, oi_ref, dar_ref, dai_ref, cr_sc, ci_sc = refs
        else:
            br_ref, bi_ref, ar_ref, ai_ref, or_ref, oi_ref, cr_sc, ci_sc = refs
        t = pl.program_id(1)
        chunk = (nt - 1 - t) if reverse else t

        @pl.when(t == 0)
        def _():
            cr_sc[...] = jnp.zeros_like(cr_sc)
            ci_sc[...] = jnp.zeros_like(ci_sc)
            if with_da:
                dar_ref[...] = jnp.zeros_like(dar_ref)
                dai_ref[...] = jnp.zeros_like(dai_ref)

        a1r = ar_ref[0:1, :]
        a1i = -ai_ref[0:1, :] if reverse else ai_ref[0:1, :]
        a2r, a2i = _cmul(a1r, a1i, a1r, a1i)
        a4r, a4i = _cmul(a2r, a2i, a2r, a2i)
        row = lax.broadcasted_iota(jnp.int32, (SUBLANES, lb), 0)
        pw_r, pw_i = jnp.zeros((SUBLANES, lb), F32), jnp.zeros((SUBLANES, lb), F32)
        cur_r, cur_i = a1r, a1i
        for e in range(SUBLANES):
            r_at = (SUBLANES - 1 - e) if reverse else e
            pw_r = jnp.where(row == r_at, cur_r, pw_r)
            pw_i = jnp.where(row == r_at, cur_i, pw_i)
            cur_r, cur_i = _cmul(cur_r, cur_i, a1r, a1i)
        steps = ((1, a1r, a1i), (2, a2r, a2i), (4, a4r, a4i))

        def block(jb, carry):
            if with_da:
                cr, ci, acc_r, acc_i = carry
            else:
                cr, ci = carry
            idx = (nblk - 1 - jb) if reverse else jb
            r0 = pl.multiple_of(idx * SUBLANES, SUBLANES)
            xr = br_ref[pl.ds(r0, SUBLANES), :]
            xi = bi_ref[pl.ds(r0, SUBLANES), :]
            for d, pr, pi in steps:
                if reverse:
                    keep = row < SUBLANES - d
                    sr, si = pltpu.roll(xr, SUBLANES - d, 0), pltpu.roll(xi, SUBLANES - d, 0)
                else:
                    keep = row >= d
                    sr, si = pltpu.roll(xr, d, 0), pltpu.roll(xi, d, 0)
                sr, si = jnp.where(keep, sr, 0.0), jnp.where(keep, si, 0.0)
                xr, xi = xr + pr * sr - pi * si, xi + pr * si + pi * sr
            xr, xi = xr + pw_r * cr - pw_i * ci, xi + pw_r * ci + pw_i * cr
            or_ref[pl.ds(r0, SUBLANES), :] = xr
            oi_ref[pl.ds(r0, SUBLANES), :] = xi
            edge = 0 if reverse else SUBLANES - 1
            cr, ci = xr[edge:edge + 1, :], xi[edge:edge + 1, :]
            if not with_da:
                return cr, ci
            fr = xr_ref[pl.ds(r0, SUBLANES), :]
            fi = xi_ref[pl.ds(r0, SUBLANES), :]
            rp = pl.multiple_of(jnp.maximum(idx - 1, 0) * SUBLANES, SUBLANES)
            inside = idx > 0
            before_r = jnp.where(inside, xr_ref[pl.ds(rp, SUBLANES), :], pr_ref[...])
            before_i = jnp.where(inside, xi_ref[pl.ds(rp, SUBLANES), :], pi_ref[...])
            live = jnp.where(jnp.logical_or(inside, chunk > 0), 1.0, 0.0)
            last_r = before_r[SUBLANES - 1:SUBLANES, :] * live
            last_i = before_i[SUBLANES - 1:SUBLANES, :] * live
            pvr = jnp.where(row == 0, last_r, pltpu.roll(fr, 1, 0))
            pvi = jnp.where(row == 0, last_i, pltpu.roll(fi, 1, 0))
            acc_r = acc_r + xr * pvr + xi * pvi
            acc_i = acc_i + xi * pvr - xr * pvi
            return cr, ci, acc_r, acc_i

        init = (cr_sc[...], ci_sc[...])
        if with_da:
            init = init + (jnp.zeros((SUBLANES, lb), F32), jnp.zeros((SUBLANES, lb), F32))
        fin = lax.fori_loop(0, nblk, block, init)
        cr_sc[...] = fin[0]
        ci_sc[...] = fin[1]
        if with_da:
            dar_ref[...] += fin[2]
            dai_ref[...] += fin[3]

    tmap = (lambda l, t: (nt - 1 - t, l)) if reverse else (lambda l, t: (t, l))
    seq = pl.BlockSpec((tc, lb), tmap)
    par = pl.BlockSpec((SUBLANES, lb), lambda l, t: (0, l))
    in_specs, args = [seq, seq, par, par], [br, bi, a_r8, a_i8]
    out_specs = [seq, seq]
    out_shape = [jax.ShapeDtypeStruct((S, L), F32), jax.ShapeDtypeStruct((S, L), F32)]
    if with_da:
        pmap = lambda l, t: (jnp.maximum((nt - 1 - t) * (tc // SUBLANES) - 1, 0), l)
        prev = pl.BlockSpec((SUBLANES, lb), pmap)
        in_specs += [seq, seq, prev, prev]
        args += [states[0], states[1], states[0], states[1]]
        out_specs += [par, par]
        out_shape += [jax.ShapeDtypeStruct((SUBLANES, L), F32), jax.ShapeDtypeStruct((SUBLANES, L), F32)]
    return pl.pallas_call(
        body, name=name, grid=(L // lb, nt), in_specs=in_specs, out_specs=out_specs, out_shape=out_shape,
        scratch_shapes=[pltpu.VMEM((1, lb), F32), pltpu.VMEM((1, lb), F32)], compiler_params=_params(2),
    )(*args)


PREP_LANES = 512


def _ssm_prep_f(a_re, a_im, log_dt, bt_re, bt_im, c_re, c_im):
    off = pl.program_id(0) * PREP_LANES
    iota = lambda shape, d: lax.broadcasted_iota(jnp.int32, shape, d)
    grp_of_row = lambda shape: iota(shape, 0) >> int(math.log2(SSM_GROUP))
    grp_of_lane = lambda shape: (iota(shape, 1) + off) >> int(math.log2(SSM_STATE))
    rep = (grp_of_row((SSM_WIDTH, SSM_GROUPS)) == iota((SSM_WIDTH, SSM_GROUPS), 1)).astype(F32)
    til = ((iota((SSM_STATE, PREP_LANES), 1) & (SSM_STATE - 1)) == iota((SSM_STATE, PREP_LANES), 0)).astype(F32)
    m_rows = (grp_of_row((SSM_WIDTH, PREP_LANES)) == grp_of_lane((SSM_WIDTH, PREP_LANES))).astype(F32)
    m_grp = (iota((SSM_GROUPS, PREP_LANES), 0) == grp_of_lane((SSM_GROUPS, PREP_LANES))).astype(F32)
    dt = jnp.exp(log_dt)
    decay = jnp.exp(a_re * dt)
    ar = decay * jnp.cos(a_im * dt)
    ai = decay * jnp.sin(a_im * dt)
    den = a_re * a_re + a_im * a_im
    nr = ar - 1.0
    coef_r = (nr * a_re + ai * a_im) / den
    coef_i = (ai * a_re - nr * a_im) / den
    cr, ci = _hp_dot(rep, coef_r), _hp_dot(rep, coef_i)
    bb_r = cr * bt_re - ci * bt_im
    bb_i = cr * bt_im + ci * bt_re
    big = lambda m: _hp_dot(m, til) * m_rows
    lanes = lambda m: jnp.broadcast_to(jnp.sum(_hp_dot(m, til) * m_grp, axis=0, keepdims=True), (SUBLANES, PREP_LANES))
    return lanes(ar), lanes(ai), big(bb_r), big(bb_i), big(c_re), -big(c_im)


def _whole(arr, **kw):
    nd = arr.ndim
    return In(arr, arr.shape, lambda *_: (0,) * nd, **kw)


def loss_grad(name, y, target, tm=TOKEN_TILE):
    S, D = y.shape

    def body(y_ref, t_ref, dy_ref, part_ref):
        err = y_ref[...] - t_ref[...]
        dy_ref[...] = err * (1.0 / D)
        part_ref[...] = jnp.full(part_ref.shape, 0.5 * jnp.sum(jnp.mean(err * err, axis=-1)), F32)

    return pl.pallas_call(
        body, name=name, grid=(S // tm,),
        in_specs=[pl.BlockSpec((tm, D), lambda i: (i, 0)), pl.BlockSpec((tm, D), lambda i: (i, 0))],
        out_specs=[pl.BlockSpec((tm, D), lambda i: (i, 0)), pl.BlockSpec((SUBLANES, 128), lambda i: (i, 0))],
        out_shape=[jax.ShapeDtypeStruct((S, D), F32), jax.ShapeDtypeStruct((S // tm * SUBLANES, 128), F32)],
        compiler_params=_params(1),
    )(y, target)


def _adamw_math(w, g, m, v):
    m = ADAM_B1 * m + (1.0 - ADAM_B1) * g
    v = ADAM_B2 * v + (1.0 - ADAM_B2) * (g * g)
    m_hat = m / (1.0 - ADAM_B1 ** ADAM_STEP)
    v_hat = v / (1.0 - ADAM_B2 ** ADAM_STEP)
    delta = -ADAM_LR * (m_hat / (jnp.sqrt(v_hat) + ADAM_EPS) + ADAM_WD * w)
    return delta, m, v


def adamw_big(name, w, g, m, v):
    R, C = w.shape
    tr = R
    for cand in (512, 344, 256, 128):
        if R % cand == 0:
            tr = cand
            break

    def body(w_ref, g_ref, m_ref, v_ref, d_ref, nm_ref, nv_ref):
        d, nm, nv = _adamw_math(w_ref[...], g_ref[...], m_ref[...], v_ref[...])
        d_ref[...] = d
        nm_ref[...] = nm
        nv_ref[...] = nv

    spec = pl.BlockSpec((tr, C), lambda i: (i, 0))
    return pl.pallas_call(
        body, name=name, grid=(R // tr,), in_specs=[spec] * 4, out_specs=[spec] * 3,
        out_shape=[jax.ShapeDtypeStruct((R, C), F32)] * 3, compiler_params=_params(1),
    )(w, g, m, v)


def adamw_small(name, ws, gs, ms, vs):
    n = len(ws)

    def body(*refs):
        for k in range(n):
            d, nm, nv = _adamw_math(refs[k][...], refs[n + k][...], refs[2 * n + k][...], refs[3 * n + k][...])
            refs[4 * n + k][...] = d
            refs[5 * n + k][...] = nm
            refs[6 * n + k][...] = nv

    vm = pl.BlockSpec(memory_space=pltpu.VMEM)
    shapes = [jax.ShapeDtypeStruct(w.shape, F32) for w in ws]
    res = pl.pallas_call(
        body, name=name, in_specs=[vm] * (4 * n), out_specs=[vm] * (3 * n), out_shape=shapes * 3,
        compiler_params=pltpu.CompilerParams(vmem_limit_bytes=VMEM_LIMIT_BYTES),
    )(*ws, *gs, *ms, *vs)
    return res[:n], res[n:2 * n], res[2 * n:]


def _place():
    return lax.axis_index("x"), lax.axis_index("y"), lax.axis_index("c")


def _other_chips(x, y):
    return [(1 - x, y), (x, 1 - y), (1 - x, 1 - y)]


HBM = pl.BlockSpec(memory_space=pl.ANY)


def all_gather_halves(name, blocks):
    n = len(blocks)

    def body(*refs):
        in_refs, out_refs = refs[:n], refs[n:2 * n]
        send_sems, recv_sems, local_sems = refs[2 * n:]
        x, y, c = _place()
        me, sibling = (x, y, c), (x, y, 1 - c)
        chips = _other_chips(x, y)

        def slot(a, px, py, pc):
            return out_refs[a].at[4 * px + 2 * py + pc]

        def copy(a, k, block, to, src=None):
            return pltpu.make_async_remote_copy(
                src_ref=slot(a, *block) if src is None else src, dst_ref=slot(a, *block),
                send_sem=send_sems.at[a, k], recv_sem=recv_sems.at[a, k], device_id=to, device_id_type=MESH)

        mine = [pltpu.make_async_copy(in_refs[a], slot(a, *me), local_sems.at[a]) for a in range(n)]
        for cp in mine:
            cp.start()
        first = []
        for a in range(n):
            first.append(copy(a, 0, me, sibling, src=in_refs[a]))
            first += [copy(a, 1 + j, me, (*chip, c), src=in_refs[a]) for j, chip in enumerate(chips)]
        for cp in first:
            cp.start()
        passed = []
        for j, chip in enumerate(chips):
            for a in range(n):
                copy(a, 1 + j, (*chip, c), me).wait_recv()
                fw = copy(a, 4 + j, (*chip, c), sibling)
                fw.start()
                passed.append(fw)
        for a in range(n):
            copy(a, 0, sibling, me).wait_recv()
            for j, chip in enumerate(chips):
                copy(a, 4 + j, (*chip, 1 - c), me).wait_recv()
        for cp in first + passed:
            cp.wait_send()
        for cp in mine:
            cp.wait()

    res = pl.pallas_call(
        body, name=name, in_specs=[HBM] * n, out_specs=[HBM] * n,
        out_shape=[jax.ShapeDtypeStruct((N_DEV,) + b.shape, b.dtype) for b in blocks],
        scratch_shapes=[pltpu.SemaphoreType.DMA((n, 7)), pltpu.SemaphoreType.DMA((n, 7)), pltpu.SemaphoreType.DMA((n,))],
    )(*blocks)
    return list(res)


def pair_send_halves(name, grads):
    n = len(grads)

    def body(*refs):
        in_refs, out_refs = refs[:n], refs[n:2 * n]
        send_sems, recv_sems = refs[2 * n:]
        x, y, c = _place()
        cps = []
        for a in range(n):
            cp = pltpu.make_async_remote_copy(
                src_ref=in_refs[a].at[:, 1 - c], dst_ref=out_refs[a], send_sem=send_sems.at[a], recv_sem=recv_sems.at[a],
                device_id=(x, y, 1 - c), device_id_type=MESH)
            cp.start()
            cps.append(cp)
        for cp in cps:
            cp.wait()

    res = pl.pallas_call(
        body, name=name, in_specs=[HBM] * n, out_specs=[HBM] * n,
        out_shape=[jax.ShapeDtypeStruct((g.shape[0],) + g.shape[2:], g.dtype) for g in grads],
        scratch_shapes=[pltpu.SemaphoreType.DMA((n,)), pltpu.SemaphoreType.DMA((n,))],
    )(*grads)
    return list(res)


def pair_add(name, grad, got, c_arr):
    nsh, _, M, N = grad.shape
    tr = M
    for cand in (512, 256, 192, 128, 64, 16):
        if M % cand == 0:
            tr = cand
            break

    def body(c_ref, g_ref, p_ref, o_ref):
        o_ref[...] = (g_ref[...] + p_ref[...]).astype(BF16)

    return pl.pallas_call(
        body, name=name,
        grid_spec=pltpu.PrefetchScalarGridSpec(
            num_scalar_prefetch=1, grid=(nsh, M // tr),
            in_specs=[pl.BlockSpec((None, None, tr, N), lambda j, i, c_ref: (j, c_ref[0], i, 0)),
                      pl.BlockSpec((None, tr, N), lambda j, i, c_ref: (j, i, 0))],
            out_specs=pl.BlockSpec((None, tr, N), lambda j, i, c_ref: (j, i, 0))),
        out_shape=jax.ShapeDtypeStruct((nsh, M, N), BF16), compiler_params=_params(2),
    )(c_arr, grad, got)


def scatter_to_chips(name, parts):
    n = len(parts)

    def body(*refs):
        in_refs, out_refs = refs[:n], refs[n:2 * n]
        send_sems, recv_sems, local_sems = refs[2 * n:]
        x, y, c = _place()
        mine = 2 * x + y
        chips = _other_chips(x, y)
        local = [pltpu.make_async_copy(in_refs[a].at[mine], out_refs[a].at[mine], local_sems.at[a]) for a in range(n)]
        for cp in local:
            cp.start()
        cps = []
        for a in range(n):
            for k, (px, py) in enumerate(chips):
                cp = pltpu.make_async_remote_copy(
                    src_ref=in_refs[a].at[2 * px + py], dst_ref=out_refs[a].at[mine],
                    send_sem=send_sems.at[a, k], recv_sem=recv_sems.at[a, k], device_id=(px, py, c), device_id_type=MESH)
                cp.start()
                cps.append((cp, a, k, px, py))
        for cp, a, k, px, py in cps:
            pltpu.make_async_remote_copy(
                src_ref=in_refs[a].at[mine], dst_ref=out_refs[a].at[2 * px + py],
                send_sem=send_sems.at[a, k], recv_sem=recv_sems.at[a, k], device_id=(px, py, c), device_id_type=MESH).wait_recv()
        for cp, *_ in cps:
            cp.wait_send()
        for cp in local:
            cp.wait()

    res = pl.pallas_call(
        body, name=name, in_specs=[HBM] * n, out_specs=[HBM] * n,
        out_shape=[jax.ShapeDtypeStruct(p.shape, p.dtype) for p in parts],
        scratch_shapes=[pltpu.SemaphoreType.DMA((n, 3)), pltpu.SemaphoreType.DMA((n, 3)), pltpu.SemaphoreType.DMA((n,))],
    )(*parts)
    return list(res)


def sum_chips(name, q):
    nsh, M, N = q.shape
    tr = M
    for cand in (512, 256, 192, 128, 64, 16):
        if M % cand == 0:
            tr = cand
            break

    def body(q_ref, o_ref):
        acc = q_ref[0].astype(F32)
        for j in range(1, nsh):
            acc = acc + q_ref[j].astype(F32)
        o_ref[...] = acc

    return pl.pallas_call(
        body, name=name, grid=(M // tr,), in_specs=[pl.BlockSpec((nsh, tr, N), lambda i: (0, i, 0))],
        out_specs=pl.BlockSpec((tr, N), lambda i: (i, 0)), out_shape=jax.ShapeDtypeStruct((M, N), F32),
        compiler_params=_params(1),
    )(q)


def pair_exchange(name, halves):
    n = len(halves)

    def body(*refs):
        in_refs, out_refs = refs[:n], refs[n:2 * n]
        send_sems, recv_sems, local_sems = refs[2 * n:]
        x, y, c = _place()
        local = [pltpu.make_async_copy(in_refs[a], out_refs[a].at[c], local_sems.at[a]) for a in range(n)]
        for cp in local:
            cp.start()
        cps = []
        for a in range(n):
            cp = pltpu.make_async_remote_copy(
                src_ref=in_refs[a], dst_ref=out_refs[a].at[c], send_sem=send_sems.at[a], recv_sem=recv_sems.at[a],
                device_id=(x, y, 1 - c), device_id_type=MESH)
            cp.start()
            cps.append(cp)
        for a in range(n):
            pltpu.make_async_remote_copy(
                src_ref=in_refs[a], dst_ref=out_refs[a].at[1 - c], send_sem=send_sems.at[a], recv_sem=recv_sems.at[a],
                device_id=(x, y, 1 - c), device_id_type=MESH).wait_recv()
        for cp in cps:
            cp.wait_send()
        for cp in local:
            cp.wait()

    res = pl.pallas_call(
        body, name=name, in_specs=[HBM] * n, out_specs=[HBM] * n,
        out_shape=[jax.ShapeDtypeStruct((N_CORES,) + h.shape, h.dtype) for h in halves],
        scratch_shapes=[pltpu.SemaphoreType.DMA((n,)), pltpu.SemaphoreType.DMA((n,)), pltpu.SemaphoreType.DMA((n,))],
    )(*halves)
    return list(res)


def all_reduce_small(name, v):
    R, C = v.shape

    def body(v_ref, o_ref, gath, send_sems, recv_sems):
        x, y, c = _place()
        me, sibling = (x, y, c), (x, y, 1 - c)
        chips = _other_chips(x, y)

        def slot(px, py, pc):
            return gath.at[4 * px + 2 * py + pc]

        def copy(k, block, to, src=None):
            return pltpu.make_async_remote_copy(
                src_ref=slot(*block) if src is None else src, dst_ref=slot(*block),
                send_sem=send_sems.at[k], recv_sem=recv_sems.at[k], device_id=to, device_id_type=MESH)

        first = [copy(0, me, sibling, src=v_ref)]
        first += [copy(1 + j, me, (*chip, c), src=v_ref) for j, chip in enumerate(chips)]
        for cp in first:
            cp.start()
        slot(*me)[...] = v_ref[...]
        passed = [copy(4 + j, (*chip, c), sibling) for j, chip in enumerate(chips)]
        for j, chip in enumerate(chips):
            copy(1 + j, (*chip, c), me).wait_recv()
            passed[j].start()
        copy(0, sibling, me).wait_recv()
        for j, chip in enumerate(chips):
            copy(4 + j, (*chip, 1 - c), me).wait_recv()
        for cp in first + passed:
            cp.wait_send()
        acc = gath[0]
        for d in range(1, N_DEV):
            acc = acc + gath[d]
        o_ref[...] = acc

    vm = pl.BlockSpec(memory_space=pltpu.VMEM)
    return pl.pallas_call(
        body, name=name, in_specs=[vm], out_specs=vm, out_shape=jax.ShapeDtypeStruct((R, C), F32),
        scratch_shapes=[pltpu.VMEM((N_DEV, R, C), F32), pltpu.SemaphoreType.DMA((7,)), pltpu.SemaphoreType.DMA((7,))],
        compiler_params=pltpu.CompilerParams(vmem_limit_bytes=VMEM_LIMIT_BYTES),
    )(v)


LANES = 128
PACK_ROW_MULTIPLE = 1024
SMALL_SHARDED = {
    "w_in": ((D_MODEL, 1216), 1), "mla_w_uq": ((MLA_Q_RANK, 768), 1), "mla_w_ukv": ((MLA_KV_RANK, 1024), 1),
    "ssm_w_glu": ((SSM_WIDTH, SSM_WIDTH), 0), "w_o": ((D_MODEL, D_MODEL), 0), "xattn_w_q": ((D_MODEL, 512), 0),
    "xattn_w_kv": ((D_MODEL, 1024), 0), "xattn_w_o": ((512, D_MODEL), 1),
}
FFN_NAMES = ["ffn1_w_gate", "ffn1_w_up", "ffn1_w_down", "ffn2_w_gate", "ffn2_w_up", "ffn2_w_down"]


def _shard_shape(name):
    (r, cdim), ax = SMALL_SHARDED[name]
    return (r // N_CHIPS, cdim) if ax == 0 else (r, cdim // N_CHIPS)


def _pack_shards(shards):
    parts = []
    for name in SMALL_SHARDED:
        a = shards[name]
        lead = a.shape[:-2]
        parts.append(a.reshape(lead + (a.shape[-2] * a.shape[-1] // LANES, LANES)))
    rows = sum(q.shape[-2] for q in parts)
    parts.append(jnp.zeros(lead + (-rows % PACK_ROW_MULTIPLE, LANES), parts[0].dtype))
    return jnp.concatenate(parts, axis=-2)


def _unpack_shards(packed):
    out, r0 = {}, 0
    lead = packed.shape[:-2]
    for name in SMALL_SHARDED:
        r, cdim = _shard_shape(name)
        rows = r * cdim // LANES
        out[name] = packed[..., r0:r0 + rows, :].reshape(lead + (r, cdim))
        r0 += rows
    return out


def _full_from_shards(name, sh):
    (r, cdim), ax = SMALL_SHARDED[name]
    if ax == 0:
        return sh.reshape(r, cdim)
    return jnp.transpose(sh, (1, 0, 2)).reshape(r, cdim)


def _shards_from_full(name, full):
    (r, cdim), ax = SMALL_SHARDED[name]
    if ax == 0:
        return full.reshape(N_CHIPS, r // N_CHIPS, cdim)
    return jnp.transpose(full.reshape(r, N_CHIPS, cdim // N_CHIPS), (1, 0, 2))


SMALL_REPL = {
    "ffn1_norm": (1, 1024), "mix_norm": (1, 1024), "mla_q_norm": (1, 384), "mla_kv_norm": (1, 256),
    "mla_qk_norm_q": (1, 192), "mla_qk_norm_k": (1, 192), "ssm_a_re": (32, 64), "ssm_a_im": (32, 64),
    "ssm_log_dt": (32, 1), "ssm_b_re": (32, 64, 16), "ssm_b_im": (32, 64, 16), "ssm_c_re": (32, 16, 64),
    "ssm_c_im": (32, 16, 64), "ssm_d": (1, 512), "ssm_b_glu": (1, 512), "out_norm_mla": (1, 512),
    "out_norm_ssm": (1, 512), "xattn_norm": (1, 1024), "mem_norm": (1, 1024), "xattn_q_norm": (1, 128),
    "xattn_k_norm": (1, 128), "ffn2_norm": (1, 1024),
}


def _pack_repl(grads):
    flat = jnp.concatenate([grads[n].reshape(-1) for n in SMALL_REPL])
    rows = -(-flat.shape[0] // (LANES * SUBLANES)) * SUBLANES
    return jnp.pad(flat, (0, rows * LANES - flat.shape[0])).reshape(rows, LANES)


def _unpack_repl(packed):
    flat, out, o = packed.reshape(-1), {}, 0
    for n, shp in SMALL_REPL.items():
        size = int(np.prod(shp))
        out[n] = flat[o:o + size].reshape(shp)
        o += size
    return out


def _rope_tables(positions):
    half = MLA_ROPE // 2
    inv = ROPE_THETA ** (-jnp.arange(half, dtype=F32) / half)
    ang = positions.astype(F32)[:, None] * inv[None, :]
    cos, sin = jnp.cos(ang), jnp.sin(ang)
    S = positions.shape[0]
    z = lambda w: jnp.zeros((S, w), F32)
    keep = jnp.concatenate([jnp.ones((S, MLA_NOPE), F32), cos, cos, z(HEAD_PAD - MLA_QK)], axis=1)
    from_hi = jnp.concatenate([z(MLA_NOPE), -sin, z(HEAD_PAD - MLA_NOPE - half)], axis=1)
    from_lo = jnp.concatenate([z(MLA_NOPE + half), sin, z(HEAD_PAD - MLA_QK)], axis=1)
    return keep, from_hi, from_lo


def _norm_rope(x, g, keep, from_hi, from_lo):
    y = x * lax.rsqrt(jnp.sum(x * x, axis=-1, keepdims=True) * (1.0 / MLA_QK) + EPS) * g
    half = MLA_ROPE // 2
    return y * keep + _lane_roll(y, HEAD_PAD - half) * from_hi + _lane_roll(y, half) * from_lo


def local_step(x, mem, positions, target, w, wb):
    S = x.shape[0]
    tm = min(TOKEN_TILE, S)
    nt = S // tm
    g1 = (nt,)
    gh = (MLA_HEADS, nt)
    tile = lambda arr, cols=None, cb=0, **kw: In(arr, (tm, arr.shape[1] if cols is None else cols), lambda i, cb=cb: (i, cb), **kw)
    par = lambda arr, **kw: In(arr, arr.shape, lambda *_: (0, 0), diff=True, **kw)
    otile = lambda cols, dt: Out((S, cols), dt, (tm, cols), lambda i: (i, 0))
    grads = {}

    x1, h1, g_1, u_1 = ffn_fwd("ffn1_fwd", x, w["ffn1_norm"], wb["ffn1_w_gate"], wb["ffn1_w_up"], wb["ffn1_w_down"])

    f_norm = lambda xv, g: (_rms(xv, g),)
    mix_ins = [tile(x1, diff=True), par(w["mix_norm"], acc=(0,))]
    mix_outs = [otile(D_MODEL, BF16)]
    (h2,) = seg_fwd("mix_norm_fwd", f_norm, g1, mix_ins, mix_outs)

    pq, pkv, pkr, pu = mm("w_in_fwd", [h2], [[wb["w_in_q"], wb["w_in_kv"], wb["w_in_kr"], wb["w_in_u"]]], [F32] * 4)

    f_lat = lambda a, b, ga, gb: (_rms(a, ga), _rms(b, gb))
    lat_ins = [tile(pq, diff=True), tile(pkv, diff=True), par(w["mla_q_norm"], acc=(0,)), par(w["mla_kv_norm"], acc=(0,))]
    lat_outs = [otile(MLA_Q_RANK, BF16), otile(MLA_KV_RANK, BF16)]
    cq, ckv = seg_fwd("latent_norm_fwd", f_lat, g1, lat_ins, lat_outs)

    (q0,) = mm("w_uq_fwd", [cq], [[wb["w_uq"]]], [F32])
    kn0, v0 = mm("w_ukv_fwd", [ckv], [[wb["w_ukv_k"], wb["w_ukv_v"]]], [F32, F32])

    keep, from_hi, from_lo = _rope_tables(positions)

    def f_qk(qh, knh, kr, kp, fh, fl, gq, gk):
        return (_norm_rope(qh, gq, kp, fh, fl), _norm_rope(jnp.concatenate([knh, kr], axis=-1), gk, kp, fh, fl))

    hmap = lambda h, i: (i, h)
    tmap0 = lambda h, i: (i, 0)
    qk_ins = [In(q0, (tm, HEAD_PAD), hmap, diff=True), In(kn0, (tm, MLA_NOPE), hmap, diff=True),
              In(pkr, (tm, LANES), tmap0, diff=True,
                 grad=((MLA_HEADS, S, LANES), (None, tm, LANES), lambda h, i: (h, i, 0))),
              In(keep, (tm, HEAD_PAD), tmap0), In(from_hi, (tm, HEAD_PAD), tmap0), In(from_lo, (tm, HEAD_PAD), tmap0),
              In(w["qk_gain_q"], (1, HEAD_PAD), lambda h, i: (0, 0), diff=True, acc=(0, 1)),
              In(w["qk_gain_k"], (1, HEAD_PAD), lambda h, i: (0, 0), diff=True, acc=(0, 1))]
    head_out = Out((MLA_HEADS, S, HEAD_PAD), BF16, (None, tm, HEAD_PAD), lambda h, i: (h, i, 0))
    qh, kh = seg_fwd("qk_norm_rope_fwd", f_qk, gh, qk_ins, [head_out, head_out])

    scale = MLA_QK ** -0.5
    o_mla, lse = attn_fwd("mla_attn_fwd", qh, kh, v0, scale, t=min(ATTN_TILE, S))

    prep_grid = (SSM_LANES // PREP_LANES,)
    prep_ins = [_whole(w[k], diff=True, acc=(0,))
                for k in ("ssm_a_re", "ssm_a_im", "ssm_log_dt", "ssm_bt_re", "ssm_bt_im", "ssm_c2_re", "ssm_c2_im")]
    lane_out = lambda rows, dt: Out((rows, SSM_LANES), dt, (rows, PREP_LANES), lambda i: (0, i))
    prep_outs = [lane_out(SUBLANES, F32)] * 2 + [lane_out(SSM_WIDTH, BF16)] * 4
    a_r8, a_i8, bb_r, bb_i, cb_r, ncb_i = seg_fwd("ssm_prep_fwd", _ssm_prep_f, prep_grid, prep_ins, prep_outs)

    bu_r, bu_i = mm("ssm_bu_fwd", [pu], [[bb_r, bb_i]], [F32, F32], tm=256)
    xs_r, xs_i = ssm_scan("ssm_scan_fwd", bu_r, bu_i, a_r8, a_i8)
    (y_lin,) = mm("ssm_cx_fwd", [xs_r, xs_i], [[cb_r], [ncb_i]], [F32], trans=True, tm=256)

    f_gelu = lambda yl, u, d: (jax.nn.gelu(yl + d * u),)
    gelu_ins = [tile(y_lin, diff=True), tile(pu, diff=True), par(w["ssm_d"], acc=(0,))]
    (gl,) = seg_fwd("ssm_gelu_fwd", f_gelu, g1, gelu_ins, [otile(SSM_WIDTH, F32)])
    (z,) = mm("ssm_glu_fwd", [gl], [[wb["ssm_w_glu"]]], [F32])

    def f_glu(g, zz, o, b, gm, gs):
        return (_rms(o, gm), _rms(g * jax.nn.sigmoid(zz + b), gs))

    glu_ins = [tile(gl, diff=True), tile(z, diff=True), tile(o_mla, diff=True), par(w["ssm_b_glu"], acc=(0,)),
               par(w["out_norm_mla"], acc=(0,)), par(w["out_norm_ssm"], acc=(0,))]
    glu_outs = [otile(SSM_WIDTH, BF16), otile(SSM_WIDTH, BF16)]
    yn_mla, yn_ssm = seg_fwd("glu_out_norm_fwd", f_glu, g1, glu_ins, glu_outs)

    (x2,) = mm("w_o_fwd", [yn_mla, yn_ssm], [[wb["w_o_mla"]], [wb["w_o_ssm"]]], [F32], adds=[x1])

    xn_ins = [tile(x2, diff=True), par(w["xattn_norm"], acc=(0,))]
    (h3,) = seg_fwd("xattn_norm_fwd", f_norm, g1, xn_ins, [otile(D_MODEL, BF16)])
    (xq0,) = mm("xattn_q_fwd", [h3], [[wb["xattn_w_q"]]], [F32])

    M = mem.shape[0]
    mem_ins = [In(mem, (M, D_MODEL), lambda i: (0, 0)), par(w["mem_norm"], acc=(0,))]
    (mn,) = seg_fwd("mem_norm_fwd", f_norm, (1,), mem_ins, [Out((M, D_MODEL), BF16, (M, D_MODEL), lambda i: (0, 0))])
    (kvm,) = mm("xattn_kv_fwd", [mn], [[wb["xattn_w_kv"]]], [F32])

    xscale = XHD ** -0.5

    def f_xattn(qv, kv_k, kv_v, gq, gk):
        qn, kn = _rms(qv, gq), _rms(kv_k, gk)
        s = _bdot_nt(qn, kn) * xscale
        p = jax.nn.softmax(s, axis=-1)
        return (_bdot_nn(p, kv_v),)

    xa_ins = [In(xq0, (tm, XHD), hmap, diff=True),
              In(kvm, (M, XHD), lambda h, i: (0, h), diff=True, acc=(1,), grad=((M, XH * XHD), (M, XHD), lambda h, i: (0, h))),
              In(kvm, (M, XHD), lambda h, i: (0, XH + h), diff=True, acc=(1,), grad=((M, XH * XHD), (M, XHD), lambda h, i: (0, h))),
              In(w["xattn_q_norm"], (1, XHD), lambda h, i: (0, 0), diff=True, acc=(0, 1)),
              In(w["xattn_k_norm"], (1, XHD), lambda h, i: (0, 0), diff=True, acc=(0, 1))]
    xa_outs = [Out((S, XH * XHD), F32, (tm, XHD), hmap)]
    (o2,) = seg_fwd("xattn_fwd", f_xattn, gh, xa_ins, xa_outs)

    (x3,) = mm("xattn_o_fwd", [o2], [[wb["xattn_w_o"]]], [F32], adds=[x2])

    x4, h4, g_2, u_2 = ffn_fwd("ffn2_fwd", x3, w["ffn2_norm"], wb["ffn2_w_gate"], wb["ffn2_w_up"], wb["ffn2_w_down"])
    dx4, parts = loss_grad("loss", x4, target)
    loss = jnp.sum(parts[::SUBLANES, 0])

    dx3, grads["ffn2_norm"], dg_2, du_2 = ffn_bwd_act("ffn2_bwd_act", dx4, x3, w["ffn2_norm"], g_2, u_2,
                                                      wb["ffn2_w_gate"], wb["ffn2_w_up"], wb["ffn2_w_down"])
    grads["ffn2_w_gate"], grads["ffn2_w_up"], grads["ffn2_w_down"] = ffn_bwd_w("ffn2_bwd_w", h4, dx4, g_2, u_2, dg_2, du_2)

    (do2,) = mm("xattn_o_bwd", [dx3], [[wb["xattn_w_o"]]], [F32], trans=True)
    (grads["xattn_w_o"],) = mm_tn("xattn_o_bwd_w", [o2], [dx3], [(0, [0])])

    dxq0, dkm, dvm, grads["xattn_q_norm"], grads["xattn_k_norm"] = seg_bwd("xattn_bwd", f_xattn, gh, xa_ins, xa_outs, [do2])
    (dh3,) = mm("xattn_q_bwd", [dxq0], [[wb["xattn_w_q"]]], [F32], trans=True)
    (grads["xattn_w_q"],) = mm_tn("xattn_q_bwd_w", [h3], [dxq0], [(0, [0])])
    wkv_k, wkv_v = wb["xattn_w_kv"][:, :XH * XHD], wb["xattn_w_kv"][:, XH * XHD:]
    (dmn,) = mm("xattn_kv_bwd", [dkm, dvm], [[wkv_k], [wkv_v]], [F32], trans=True)
    gk_w, gv_w = mm_tn("xattn_kv_bwd_w", [mn], [dkm, dvm], [(0, [0]), (0, [1])])
    grads["xattn_w_kv"] = jnp.concatenate([gk_w, gv_w], axis=1)
    (grads["mem_norm"],) = seg_bwd("mem_norm_bwd", f_norm, (1,), mem_ins,
                                   [Out((M, D_MODEL), BF16, (M, D_MODEL), lambda i: (0, 0))], [dmn])

    dx2, grads["xattn_norm"] = seg_bwd("xattn_norm_bwd", f_norm, g1, xn_ins, [otile(D_MODEL, BF16)], [dh3], adds={0: dx3})

    dyn_mla, dyn_ssm = mm("w_o_bwd", [dx2], [[wb["w_o_mla"], wb["w_o_ssm"]]], [F32, F32], trans=True)
    go_mla, go_ssm = mm_tn("w_o_bwd_w", [yn_mla, yn_ssm], [dx2], [(0, [0]), (1, [0])])
    grads["w_o"] = jnp.concatenate([go_mla, go_ssm], axis=0)

    dgl_a, dz, do_mla, grads["ssm_b_glu"], grads["out_norm_mla"], grads["out_norm_ssm"] = seg_bwd(
        "glu_out_norm_bwd", f_glu, g1, glu_ins, glu_outs, [dyn_mla, dyn_ssm])
    (dgl,) = mm("ssm_glu_bwd", [dz], [[wb["ssm_w_glu"]]], [F32], trans=True, adds=[dgl_a])
    (grads["ssm_w_glu"],) = mm_tn("ssm_glu_bwd_w", [gl], [dz], [(0, [0])])
    dy_lin, du_a, grads["ssm_d"] = seg_bwd("ssm_gelu_bwd", f_gelu, g1, gelu_ins, [otile(SSM_WIDTH, F32)], [dgl])

    gx_r, gx_i = mm("ssm_cx_bwd", [dy_lin], [[cb_r, ncb_i]], [F32, F32], tm=256)
    dcb_r, dncb_i = mm_tn("ssm_cx_bwd_w", [dy_lin], [xs_r, xs_i], [(0, [0]), (0, [1])], tm=256)
    lam_r, lam_i, da_r8, da_i8 = ssm_scan("ssm_scan_bwd", gx_r, gx_i, a_r8, a_i8, reverse=True, states=(xs_r, xs_i))
    (du,) = mm("ssm_bu_bwd", [lam_r, lam_i], [[bb_r], [bb_i]], [F32], trans=True, adds=[du_a], tm=256)
    dbb_r, dbb_i = mm_tn("ssm_bu_bwd_w", [pu], [lam_r, lam_i], [(0, [0]), (0, [1])], tm=256)
    prep_g = seg_bwd("ssm_prep_bwd", _ssm_prep_f, prep_grid, prep_ins, prep_outs, [da_r8, da_i8, dbb_r, dbb_i, dcb_r, dncb_i])
    for k, gname in enumerate(("ssm_a_re", "ssm_a_im", "ssm_log_dt", "ssm_bt_re", "ssm_bt_im", "ssm_c2_re", "ssm_c2_im")):
        grads[gname] = prep_g[k]

    ta = min(ATTN_TILE, S)
    dkh, dv0 = attn_bwd_kv("mla_attn_bwd_kv", qh, kh, v0, do_mla, o_mla, lse, scale, t=ta)
    dqh = attn_bwd_q("mla_attn_bwd_q", qh, kh, v0, do_mla, o_mla, lse, scale, t=ta)

    dq0, dkn0, dkr4, grads["qk_gain_q"], grads["qk_gain_k"] = seg_bwd(
        "qk_norm_rope_bwd", f_qk, gh, qk_ins, [head_out, head_out], [dqh, dkh])

    (dcq,) = mm("w_uq_bwd", [dq0], [[wb["w_uq"]]], [F32], trans=True)
    (grads["w_uq"],) = mm_tn("w_uq_bwd_w", [cq], [dq0], [(0, [0])])
    (dckv,) = mm("w_ukv_bwd", [dkn0, dv0], [[wb["w_ukv_k"]], [wb["w_ukv_v"]]], [F32], trans=True)
    grads["w_ukv_k"], grads["w_ukv_v"] = mm_tn("w_ukv_bwd_w", [ckv], [dkn0, dv0], [(0, [0]), (0, [1])])

    dpq, dpkv, grads["mla_q_norm"], grads["mla_kv_norm"] = seg_bwd("latent_norm_bwd", f_lat, g1, lat_ins, lat_outs, [dcq, dckv])

    dkr_list = [dkr4[h] for h in range(MLA_HEADS)]
    (dh2,) = mm("w_in_bwd", [dpq, dpkv, du] + dkr_list,
                [[wb["w_in_q"]], [wb["w_in_kv"]], [wb["w_in_u"]]] + [[wb["w_in_kr"]]] * MLA_HEADS, [F32], trans=True)
    grads["w_in_q"], grads["w_in_kv"], grads["w_in_u"], grads["w_in_kr"] = mm_tn(
        "w_in_bwd_w", [h2], [dpq, dpkv, du] + dkr_list, [(0, [0]), (0, [1]), (0, [2]), (0, [3, 4, 5, 6])])

    dx1, grads["mix_norm"] = seg_bwd("mix_norm_bwd", f_norm, g1, mix_ins, mix_outs, [dh2], adds={0: dx2})

    dx, grads["ffn1_norm"], dg_1, du_1 = ffn_bwd_act("ffn1_bwd_act", dx1, x, w["ffn1_norm"], g_1, u_1,
                                                     wb["ffn1_w_gate"], wb["ffn1_w_up"], wb["ffn1_w_down"])
    grads["ffn1_w_gate"], grads["ffn1_w_up"], grads["ffn1_w_down"] = ffn_bwd_w("ffn1_bwd_w", h1, dx1, g_1, u_1, dg_1, du_1)
    return loss, dx, grads


def _pad_cols(a, n):
    return jnp.pad(a, ((0, 0), (0, n - a.shape[1])))


def _step_weights(full_bf16):
    wb = {}
    w_in = full_bf16["w_in"]
    wb["w_in_q"] = w_in[:, :MLA_Q_RANK]
    wb["w_in_kv"] = w_in[:, MLA_Q_RANK:MLA_Q_RANK + MLA_KV_RANK]
    wb["w_in_kr"] = _pad_cols(w_in[:, MLA_Q_RANK + MLA_KV_RANK:MLA_Q_RANK + MLA_KV_RANK + MLA_ROPE], LANES)
    wb["w_in_u"] = w_in[:, MLA_Q_RANK + MLA_KV_RANK + MLA_ROPE:]
    uq = full_bf16["mla_w_uq"].reshape(MLA_Q_RANK, MLA_HEADS, MLA_QK)
    wb["w_uq"] = jnp.pad(uq, ((0, 0), (0, 0), (0, HEAD_PAD - MLA_QK))).reshape(MLA_Q_RANK, MLA_HEADS * HEAD_PAD)
    ukv = full_bf16["mla_w_ukv"].reshape(MLA_KV_RANK, MLA_HEADS, MLA_NOPE + MLA_V)
    wb["w_ukv_k"] = ukv[:, :, :MLA_NOPE].reshape(MLA_KV_RANK, MLA_HEADS * MLA_NOPE)
    wb["w_ukv_v"] = ukv[:, :, MLA_NOPE:].reshape(MLA_KV_RANK, MLA_HEADS * MLA_V)
    wb["ssm_w_glu"] = full_bf16["ssm_w_glu"]
    wb["w_o_mla"] = full_bf16["w_o"][:SSM_WIDTH]
    wb["w_o_ssm"] = full_bf16["w_o"][SSM_WIDTH:]
    for k in ("xattn_w_q", "xattn_w_kv", "xattn_w_o"):
        wb[k] = full_bf16[k]
    return wb


def _problem_grads(g):
    out = {}
    kr = g["w_in_kr"][:, :MLA_ROPE]
    out["w_in"] = jnp.concatenate([g["w_in_q"], g["w_in_kv"], kr, g["w_in_u"]], axis=1)
    out["mla_w_uq"] = g["w_uq"].reshape(MLA_Q_RANK, MLA_HEADS, HEAD_PAD)[:, :, :MLA_QK].reshape(MLA_Q_RANK, MLA_HEADS * MLA_QK)
    k3 = g["w_ukv_k"].reshape(MLA_KV_RANK, MLA_HEADS, MLA_NOPE)
    v3 = g["w_ukv_v"].reshape(MLA_KV_RANK, MLA_HEADS, MLA_V)
    out["mla_w_ukv"] = jnp.concatenate([k3, v3], axis=2).reshape(MLA_KV_RANK, MLA_HEADS * (MLA_NOPE + MLA_V))
    for k in ("ssm_w_glu", "w_o", "xattn_w_q", "xattn_w_kv", "xattn_w_o"):
        out[k] = g[k]
    out["mla_qk_norm_q"] = g["qk_gain_q"][:, :MLA_QK]
    out["mla_qk_norm_k"] = g["qk_gain_k"][:, :MLA_QK]
    out["ssm_b_re"] = jnp.transpose(g["ssm_bt_re"].reshape(SSM_GROUPS, SSM_GROUP, SSM_STATE), (0, 2, 1))
    out["ssm_b_im"] = jnp.transpose(g["ssm_bt_im"].reshape(SSM_GROUPS, SSM_GROUP, SSM_STATE), (0, 2, 1))
    out["ssm_c_re"] = g["ssm_c2_re"].reshape(SSM_GROUPS, SSM_GROUP, SSM_STATE)
    out["ssm_c_im"] = g["ssm_c2_im"].reshape(SSM_GROUPS, SSM_GROUP, SSM_STATE)
    for k in SMALL_REPL:
        if k not in out:
            out[k] = g[k]
    for k in FFN_NAMES:
        out[k] = g[k]
    return out


def _step_params(p):
    row = lambda a: a.reshape(1, -1)
    w = {k: row(p[k]) for k in ("ffn1_norm", "mix_norm", "mla_q_norm", "mla_kv_norm", "ssm_b_glu", "out_norm_mla",
                                "out_norm_ssm", "xattn_norm", "mem_norm", "xattn_q_norm", "xattn_k_norm", "ffn2_norm")}
    w["qk_gain_q"] = _pad_cols(row(p["mla_qk_norm_q"]), HEAD_PAD)
    w["qk_gain_k"] = _pad_cols(row(p["mla_qk_norm_k"]), HEAD_PAD)
    w["ssm_a_re"], w["ssm_a_im"] = p["ssm_a_re"], p["ssm_a_im"]
    w["ssm_log_dt"] = p["ssm_log_dt"].reshape(SSM_GROUPS, 1)
    w["ssm_bt_re"] = jnp.transpose(p["ssm_b_re"], (0, 2, 1)).reshape(SSM_WIDTH, SSM_STATE)
    w["ssm_bt_im"] = jnp.transpose(p["ssm_b_im"], (0, 2, 1)).reshape(SSM_WIDTH, SSM_STATE)
    w["ssm_c2_re"] = p["ssm_c_re"].reshape(SSM_WIDTH, SSM_STATE)
    w["ssm_c2_im"] = p["ssm_c_im"].reshape(SSM_WIDTH, SSM_STATE)
    w["ssm_d"] = p["ssm_d"].reshape(1, SSM_WIDTH)
    return w


ARG_NAMES = ['x', 'mem', 'positions', 'ffn1_norm', 'ffn1_w_gate', 'ffn1_w_up', 'ffn1_w_down', 'mix_norm', 'w_in', 'mla_q_norm', 'mla_w_uq', 'mla_kv_norm', 'mla_w_ukv', 'mla_qk_norm_q', 'mla_qk_norm_k', 'ssm_a_re', 'ssm_a_im', 'ssm_log_dt', 'ssm_b_re', 'ssm_b_im', 'ssm_c_re', 'ssm_c_im', 'ssm_d', 'ssm_w_glu', 'ssm_b_glu', 'out_norm_mla', 'out_norm_ssm', 'w_o', 'xattn_norm', 'mem_norm', 'xattn_w_q', 'xattn_w_kv', 'xattn_q_norm', 'xattn_k_norm', 'xattn_w_o', 'ffn2_norm', 'ffn2_w_gate', 'ffn2_w_up', 'ffn2_w_down']
WEIGHT_NAMES = ARG_NAMES[3:]


def _gather_weights(p, c):
    half = lambda a: lax.dynamic_slice_in_dim(a, c * (a.shape[0] // 2), a.shape[0] // 2, axis=0)
    packed = _pack_shards({k: p[k].astype(BF16) for k in SMALL_SHARDED})
    ffn1 = [half(p[k].astype(BF16)) for k in FFN_NAMES[:3]]
    ffn2 = [half(p[k].astype(BF16)) for k in FFN_NAMES[3:]]
    got1 = all_gather_halves("all_gather_weights_a", ffn1 + [half(packed)])
    got2 = all_gather_halves("all_gather_weights_b", ffn2)
    wb = {}
    for k, a in zip(FFN_NAMES, got1[:3] + got2):
        wb[k] = a.reshape(N_CHIPS, 2 * a.shape[1], a.shape[2])
    small = _unpack_shards(got1[3].reshape(N_CHIPS, -1, LANES))
    wb.update(_step_weights({k: _full_from_shards(k, small[k]) for k in SMALL_SHARDED}))
    return wb


def _reduce_sharded(gfull, c):
    c_arr = jnp.reshape(c, (1,)).astype(jnp.int32)
    packed = _pack_shards({k: _shards_from_full(k, gfull[k]) for k in SMALL_SHARDED})
    arrs = [gfull[k] for k in FFN_NAMES] + [packed]
    split = [a.reshape(N_CHIPS, 2, a.shape[1] // 2, a.shape[2]) for a in arrs]
    got = pair_send_halves("grad_pair_send", split)
    parts = [pair_add(f"grad_pair_add_{k}", s, g, c_arr) for k, (s, g) in enumerate(zip(split, got))]
    landed = scatter_to_chips("grad_scatter", parts)
    halves = [sum_chips(f"grad_sum_{k}", q) for k, q in enumerate(landed)]
    both = pair_exchange("grad_pair_exchange", halves)
    shards = [b.reshape(2 * b.shape[1], b.shape[2]) for b in both]
    out = dict(zip(FFN_NAMES, shards[:6]))
    out.update(_unpack_shards(shards[6]))
    return out


def kernel(x, mem, positions, ffn1_norm, ffn1_w_gate, ffn1_w_up, ffn1_w_down, mix_norm, w_in, mla_q_norm, mla_w_uq, mla_kv_norm, mla_w_ukv, mla_qk_norm_q, mla_qk_norm_k, ssm_a_re, ssm_a_im, ssm_log_dt, ssm_b_re, ssm_b_im, ssm_c_re, ssm_c_im, ssm_d, ssm_w_glu, ssm_b_glu, out_norm_mla, out_norm_ssm, w_o, xattn_norm, mem_norm, xattn_w_q, xattn_w_kv, xattn_q_norm, xattn_k_norm, xattn_w_o, ffn2_norm, ffn2_w_gate, ffn2_w_up, ffn2_w_down, loss_target, m_ffn1_norm, m_ffn1_w_gate, m_ffn1_w_up, m_ffn1_w_down, m_mix_norm, m_w_in, m_mla_q_norm, m_mla_w_uq, m_mla_kv_norm, m_mla_w_ukv, m_mla_qk_norm_q, m_mla_qk_norm_k, m_ssm_a_re, m_ssm_a_im, m_ssm_log_dt, m_ssm_b_re, m_ssm_b_im, m_ssm_c_re, m_ssm_c_im, m_ssm_d, m_ssm_w_glu, m_ssm_b_glu, m_out_norm_mla, m_out_norm_ssm, m_w_o, m_xattn_norm, m_mem_norm, m_xattn_w_q, m_xattn_w_kv, m_xattn_q_norm, m_xattn_k_norm, m_xattn_w_o, m_ffn2_norm, m_ffn2_w_gate, m_ffn2_w_up, m_ffn2_w_down, v_ffn1_norm, v_ffn1_w_gate, v_ffn1_w_up, v_ffn1_w_down, v_mix_norm, v_w_in, v_mla_q_norm, v_mla_w_uq, v_mla_kv_norm, v_mla_w_ukv, v_mla_qk_norm_q, v_mla_qk_norm_k, v_ssm_a_re, v_ssm_a_im, v_ssm_log_dt, v_ssm_b_re, v_ssm_b_im, v_ssm_c_re, v_ssm_c_im, v_ssm_d, v_ssm_w_glu, v_ssm_b_glu, v_out_norm_mla, v_out_norm_ssm, v_w_o, v_xattn_norm, v_mem_norm, v_xattn_w_q, v_xattn_w_kv, v_xattn_q_norm, v_xattn_k_norm, v_xattn_w_o, v_ffn2_norm, v_ffn2_w_gate, v_ffn2_w_up, v_ffn2_w_down):
    args = dict(locals())
    c = lax.axis_index("c")
    p = {k: args[k][0] for k in WEIGHT_NAMES}
    mom = {k: args["m_" + k][0] for k in WEIGHT_NAMES}
    var = {k: args["v_" + k][0] for k in WEIGHT_NAMES}

    wb = _gather_weights(p, c)
    loss, dx, g = local_step(x[0], mem[0], positions[0], loss_target[0], _step_params(p), wb)
    gfull = _problem_grads(g)

    grad = _reduce_sharded(gfull, c)
    grad.update(_unpack_repl(all_reduce_small("grad_all_reduce_small", _pack_repl({k: gfull[k] for k in SMALL_REPL}))))
    loss = lax.psum(loss, ("x", "y", "c"))

    delta, new_m, new_v = {}, {}, {}
    small = [k for k in WEIGHT_NAMES if k not in FFN_NAMES and k not in SMALL_SHARDED]
    as2d = lambda a: a.reshape(-1, a.shape[-1])
    for k in WEIGHT_NAMES:
        if k in small:
            continue
        d, nm, nv = adamw_big("adamw_" + k, as2d(p[k]), as2d(grad[k]), as2d(mom[k]), as2d(var[k]))
        delta[k], new_m[k], new_v[k] = d, nm, nv
    ds, nms, nvs = adamw_small("adamw_small", [as2d(p[k]) for k in small], [as2d(grad[k].reshape(p[k].shape)) for k in small],
                               [as2d(mom[k]) for k in small], [as2d(var[k]) for k in small])
    for k, d, nm, nv in zip(small, ds, nms, nvs):
        delta[k], new_m[k], new_v[k] = d, nm, nv

    shaped = lambda d, k: d.reshape(args[k].shape)
    return (loss, dx[None], *[shaped(grad[k], k) for k in WEIGHT_NAMES], *[shaped(delta[k], k) for k in WEIGHT_NAMES],
            *[shaped(new_m[k], k) for k in WEIGHT_NAMES], *[shaped(new_v[k], k) for k in WEIGHT_NAMES])
```

```python
import functools
import math

import jax
import jax.numpy as jnp
import numpy as np
from jax import lax
from jax.experimental import pallas as pl
from jax.experimental.pallas import tpu as pltpu

F32, BF16 = jnp.float32, jnp.bfloat16
EPS = 1e-6
MESH = pl.DeviceIdType.MESH

D_MODEL, D_FF = 1024, 2752
MLA_HEADS, MLA_Q_RANK, MLA_KV_RANK, MLA_NOPE, MLA_ROPE, MLA_V = 4, 384, 256, 128, 64, 128
MLA_QK = MLA_NOPE + MLA_ROPE
HEAD_PAD = 256
SSM_WIDTH, SSM_GROUP, SSM_GROUPS, SSM_STATE = 512, 16, 32, 64
SSM_LANES = SSM_GROUPS * SSM_STATE
XH, XHD = 4, 128
ROPE_THETA = 10000.0
ADAM_LR, ADAM_B1, ADAM_B2, ADAM_EPS, ADAM_WD, ADAM_STEP = 0.001, 0.9, 0.999, 1e-08, 0.01, 10
N_CHIPS, N_CORES, N_DEV = 4, 2, 8

VMEM_LIMIT_BYTES = 56 * 2**20
TOKEN_TILE = 512
ATTN_TILE = 512
SCAN_TIME_TILE = 1024
SCAN_LANE_TILE = 256
SUBLANES = 8


def _params(n_axes):
    return pltpu.CompilerParams(dimension_semantics=("arbitrary",) * n_axes, vmem_limit_bytes=VMEM_LIMIT_BYTES)


def _first(axes):
    cond = None
    for a in axes:
        c = pl.program_id(a) == 0
        cond = c if cond is None else jnp.logical_and(cond, c)
    return cond


def mm(name, xs, ws, out_dtypes, *, trans=False, adds=None, tm=TOKEN_TILE):
    rows = xs[0].shape[0]
    tm = min(tm, rows)
    n_in, n_out = len(xs), len(out_dtypes)
    pairs = [(i, j) for i in range(n_in) for j in range(n_out) if ws[i][j] is not None]
    w_list = [ws[i][j] for (i, j) in pairs]
    adds = list(adds) if adds is not None else [None] * n_out
    add_list = [a for a in adds if a is not None]
    out_cols = [None] * n_out
    for (i, j), w in zip(pairs, w_list):
        out_cols[j] = w.shape[0] if trans else w.shape[1]
    contract = (((1,), (1 if trans else 0,)), ((), ()))

    def body(*refs):
        x_refs = refs[:n_in]
        w_refs = refs[n_in:n_in + len(pairs)]
        a_refs = list(refs[n_in + len(pairs):n_in + len(pairs) + len(add_list)])
        o_refs = refs[n_in + len(pairs) + len(add_list):]
        xb = [None] * n_in
        for j in range(n_out):
            acc = None
            for p, (i, jj) in enumerate(pairs):
                if jj != j:
                    continue
                if xb[i] is None:
                    xb[i] = x_refs[i][...].astype(BF16)
                d = lax.dot_general(xb[i], w_refs[p][...].astype(BF16), contract, preferred_element_type=F32)
                acc = d if acc is None else acc + d
            if adds[j] is not None:
                acc = acc + a_refs.pop(0)[...].astype(F32)
            o_refs[j][...] = acc.astype(o_refs[j].dtype)

    in_specs = ([pl.BlockSpec((tm, x.shape[1]), lambda i: (i, 0)) for x in xs]
                + [pl.BlockSpec(w.shape, lambda i: (0, 0)) for w in w_list]
                + [pl.BlockSpec((tm, a.shape[1]), lambda i: (i, 0)) for a in add_list])
    outs = pl.pallas_call(
        body, name=name, grid=(rows // tm,), in_specs=in_specs,
        out_specs=[pl.BlockSpec((tm, n), lambda i: (i, 0)) for n in out_cols],
        out_shape=[jax.ShapeDtypeStruct((rows, n), dt) for n, dt in zip(out_cols, out_dtypes)],
        compiler_params=_params(1),
    )(*xs, *w_list, *add_list)
    return list(outs)


def mm_tn(name, xs, dys, pairs, *, tm=TOKEN_TILE):
    rows = xs[0].shape[0]
    tm = min(tm, rows)
    n_x, n_dy = len(xs), len(dys)
    contract = (((0,), (0,)), ((), ()))

    def body(*refs):
        x_refs, dy_refs, o_refs = refs[:n_x], refs[n_x:n_x + n_dy], refs[n_x + n_dy:]
        @pl.when(pl.program_id(0) == 0)
        def _():
            for o in o_refs:
                o[...] = jnp.zeros_like(o)

        for k, (i, js) in enumerate(pairs):
            dy = None
            for j in js:
                t = dy_refs[j][...].astype(F32)
                dy = t if dy is None else dy + t
            o_refs[k][...] += lax.dot_general(x_refs[i][...].astype(BF16), dy.astype(BF16), contract,
                                              preferred_element_type=F32)

    shapes = [(xs[i].shape[1], dys[js[0]].shape[1]) for (i, js) in pairs]
    outs = pl.pallas_call(
        body, name=name, grid=(rows // tm,),
        in_specs=[pl.BlockSpec((tm, a.shape[1]), lambda i: (i, 0)) for a in (*xs, *dys)],
        out_specs=[pl.BlockSpec(s, lambda i: (0, 0)) for s in shapes],
        out_shape=[jax.ShapeDtypeStruct(s, F32) for s in shapes],
        compiler_params=_params(1),
    )(*xs, *dys)
    return list(outs)


class In:
    def __init__(self, arr, block, imap, *, diff=False, acc=None, grad=None):
        self.arr, self.block, self.imap, self.diff, self.acc, self.grad = arr, block, imap, diff, acc, grad

    def spec(self):
        return pl.BlockSpec(self.block, self.imap)


class Out:
    def __init__(self, shape, dtype, block, imap):
        self.shape, self.dtype, self.block, self.imap = shape, dtype, block, imap

    def spec(self):
        return pl.BlockSpec(self.block, self.imap)


def seg_fwd(name, f, grid, ins, outs):
    n_in = len(ins)

    def body(*refs):
        res = f(*[r[...] for r in refs[:n_in]])
        for o_ref, r in zip(refs[n_in:], res):
            o_ref[...] = r.astype(o_ref.dtype)

    res = pl.pallas_call(
        body, name=name, grid=grid, in_specs=[i.spec() for i in ins], out_specs=[o.spec() for o in outs],
        out_shape=[jax.ShapeDtypeStruct(o.shape, o.dtype) for o in outs], compiler_params=_params(len(grid)),
    )(*[i.arr for i in ins])
    return list(res)


def seg_bwd(name, f, grid, ins, outs, cts, adds=None):
    n_in, n_ct = len(ins), len(cts)
    diff_idx = [k for k, i in enumerate(ins) if i.diff]
    adds = adds or {}
    add_keys = sorted(adds)
    add_list = [adds[k] for k in add_keys]

    def body(*refs):
        in_refs, ct_refs = refs[:n_in], refs[n_in:n_in + n_ct]
        add_refs = dict(zip(add_keys, refs[n_in + n_ct:n_in + n_ct + len(add_list)]))
        g_refs = refs[n_in + n_ct + len(add_list):]
        vals = [r[...] for r in in_refs]

        def g(*dv):
            full = list(vals)
            for k, v in zip(diff_idx, dv):
                full[k] = v
            return tuple(f(*full))

        _, pull = jax.vjp(g, *[vals[k].astype(F32) for k in diff_idx])
        grads = pull(tuple(c[...].astype(F32) for c in ct_refs))
        for n, (k, gr) in enumerate(zip(diff_idx, grads)):
            if k in add_refs:
                gr = gr + add_refs[k][...].astype(F32)
            if ins[k].acc is None:
                g_refs[n][...] = gr.astype(g_refs[n].dtype)
            else:
                @pl.when(_first(ins[k].acc))
                def _(n=n):
                    g_refs[n][...] = jnp.zeros_like(g_refs[n])

                g_refs[n][...] += gr

    g_specs, g_shapes = [], []
    for k in diff_idx:
        i = ins[k]
        shape, block, imap = i.grad if i.grad is not None else (i.arr.shape, i.block, i.imap)
        g_specs.append(pl.BlockSpec(block, imap))
        g_shapes.append(jax.ShapeDtypeStruct(shape, F32))
    in_specs = ([i.spec() for i in ins] + [o.spec() for o in outs]
                + [pl.BlockSpec(ins[k].block, ins[k].imap) for k in add_keys])
    res = pl.pallas_call(
        body, name=name, grid=grid, in_specs=in_specs, out_specs=g_specs, out_shape=g_shapes,
        compiler_params=_params(len(grid)),
    )(*[i.arr for i in ins], *cts, *add_list)
    return list(res)


def _rms(x, g):
    return x * lax.rsqrt(jnp.mean(x * x, axis=-1, keepdims=True) + EPS) * g


@jax.custom_vjp
def _bdot_nt(a, b):
    return lax.dot_general(a.astype(BF16), b.astype(BF16), (((1,), (1,)), ((), ())), preferred_element_type=F32)


def _bdot_nt_fwd(a, b):
    return _bdot_nt(a, b), (a, b)


def _bdot_nt_bwd(res, g):
    a, b = res
    gb = g.astype(BF16)
    da = lax.dot_general(gb, b.astype(BF16), (((1,), (0,)), ((), ())), preferred_element_type=F32)
    db = lax.dot_general(gb, a.astype(BF16), (((0,), (0,)), ((), ())), preferred_element_type=F32)
    return da, db


_bdot_nt.defvjp(_bdot_nt_fwd, _bdot_nt_bwd)


@jax.custom_vjp
def _bdot_nn(a, b):
    return lax.dot_general(a.astype(BF16), b.astype(BF16), (((1,), (0,)), ((), ())), preferred_element_type=F32)


def _bdot_nn_fwd(a, b):
    return _bdot_nn(a, b), (a, b)


def _bdot_nn_bwd(res, g):
    a, b = res
    gb = g.astype(BF16)
    da = lax.dot_general(gb, b.astype(BF16), (((1,), (1,)), ((), ())), preferred_element_type=F32)
    db = lax.dot_general(a.astype(BF16), gb, (((0,), (0,)), ((), ())), preferred_element_type=F32)
    return da, db


_bdot_nn.defvjp(_bdot_nn_fwd, _bdot_nn_bwd)


@functools.partial(jax.custom_vjp, nondiff_argnums=(1,))
def _lane_roll(x, shift):
    return pltpu.roll(x, shift, 1)


def _lane_roll_fwd(x, shift):
    return pltpu.roll(x, shift, 1), None


def _lane_roll_bwd(shift, _, g):
    return (pltpu.roll(g, (g.shape[1] - shift) % g.shape[1], 1),)


_lane_roll.defvjp(_lane_roll_fwd, _lane_roll_bwd)


def _hp_dot(a, b):
    return jnp.dot(a, b, precision=lax.Precision.HIGHEST, preferred_element_type=F32)


def ffn_fwd(name, x, gain, wg, wu, wd, tm=TOKEN_TILE):
    S, D = x.shape
    nsh, _, Fs = wg.shape

    def body(x_ref, gain_ref, wg_ref, wu_ref, wd_ref, xo_ref, h_ref, g_ref, u_ref, acc):
        j = pl.program_id(1)

        @pl.when(j == 0)
        def _():
            h_ref[...] = _rms(x_ref[...], gain_ref[...]).astype(BF16)
            acc[...] = jnp.zeros_like(acc)

        h = h_ref[...]
        g = jnp.dot(h, wg_ref[...], preferred_element_type=F32)
        u = jnp.dot(h, wu_ref[...], preferred_element_type=F32)
        g_ref[...] = g.astype(BF16)
        u_ref[...] = u.astype(BF16)
        a = g * jax.nn.sigmoid(g) * u
        acc[...] += jnp.dot(a.astype(BF16), wd_ref[...], preferred_element_type=F32)

        @pl.when(j == nsh - 1)
        def _():
            xo_ref[...] = x_ref[...] + 0.5 * acc[...]

    return pl.pallas_call(
        body, name=name, grid=(S // tm, nsh),
        in_specs=[pl.BlockSpec((tm, D), lambda i, j: (i, 0)), pl.BlockSpec((1, D), lambda i, j: (0, 0)),
                  pl.BlockSpec((None, D, Fs), lambda i, j: (j, 0, 0)), pl.BlockSpec((None, D, Fs), lambda i, j: (j, 0, 0)),
                  pl.BlockSpec((None, Fs, D), lambda i, j: (j, 0, 0))],
        out_specs=[pl.BlockSpec((tm, D), lambda i, j: (i, 0)), pl.BlockSpec((tm, D), lambda i, j: (i, 0)),
                   pl.BlockSpec((None, tm, Fs), lambda i, j: (j, i, 0)), pl.BlockSpec((None, tm, Fs), lambda i, j: (j, i, 0))],
        out_shape=[jax.ShapeDtypeStruct((S, D), F32), jax.ShapeDtypeStruct((S, D), BF16),
                   jax.ShapeDtypeStruct((nsh, S, Fs), BF16), jax.ShapeDtypeStruct((nsh, S, Fs), BF16)],
        scratch_shapes=[pltpu.VMEM((tm, D), F32)], compiler_params=_params(2),
    )(x, gain, wg, wu, wd)


def ffn_bwd_act(name, dxo, x, gain, g, u, wg, wu, wd, tm=TOKEN_TILE):
    S, D = x.shape
    nsh, _, Fs = wg.shape
    nt_dims = (((1,), (1,)), ((), ()))

    def body(dxo_ref, x_ref, gain_ref, g_ref, u_ref, wg_ref, wu_ref, wd_ref, dx_ref, dgain_ref, dg_ref, du_ref, dd, dh):
        i, j = pl.program_id(0), pl.program_id(1)

        @pl.when(j == 0)
        def _():
            dd[...] = (0.5 * dxo_ref[...]).astype(BF16)
            dh[...] = jnp.zeros_like(dh)

        da = lax.dot_general(dd[...], wd_ref[...], nt_dims, preferred_element_type=F32)
        gf, uf = g_ref[...].astype(F32), u_ref[...].astype(F32)
        sig = jax.nn.sigmoid(gf)
        dgv = (da * uf * (sig * (1.0 + gf * (1.0 - sig)))).astype(BF16)
        duv = (da * (gf * sig)).astype(BF16)
        dg_ref[...] = dgv
        du_ref[...] = duv
        dh[...] += (lax.dot_general(dgv, wg_ref[...], nt_dims, preferred_element_type=F32)
                    + lax.dot_general(duv, wu_ref[...], nt_dims, preferred_element_type=F32))

        @pl.when(j == nsh - 1)
        def _():
            xv = x_ref[...]
            r = lax.rsqrt(jnp.mean(xv * xv, axis=-1, keepdims=True) + EPS)
            xhat = xv * r
            dhv = dh[...]
            dxn = dhv * gain_ref[...]
            dx_ref[...] = dxo_ref[...] + r * (dxn - xhat * jnp.mean(dxn * xhat, axis=-1, keepdims=True))
            part = jnp.sum(dhv * xhat, axis=0, keepdims=True)

            @pl.when(i == 0)
            def _():
                dgain_ref[...] = part

            @pl.when(i != 0)
            def _():
                dgain_ref[...] += part

    return pl.pallas_call(
        body, name=name, grid=(S // tm, nsh),
        in_specs=[pl.BlockSpec((tm, D), lambda i, j: (i, 0)), pl.BlockSpec((tm, D), lambda i, j: (i, 0)),
                  pl.BlockSpec((1, D), lambda i, j: (0, 0)),
                  pl.BlockSpec((None, tm, Fs), lambda i, j: (j, i, 0)), pl.BlockSpec((None, tm, Fs), lambda i, j: (j, i, 0)),
                  pl.BlockSpec((None, D, Fs), lambda i, j: (j, 0, 0)), pl.BlockSpec((None, D, Fs), lambda i, j: (j, 0, 0)),
                  pl.BlockSpec((None, Fs, D), lambda i, j: (j, 0, 0))],
        out_specs=[pl.BlockSpec((tm, D), lambda i, j: (i, 0)), pl.BlockSpec((1, D), lambda i, j: (0, 0)),
                   pl.BlockSpec((None, tm, Fs), lambda i, j: (j, i, 0)), pl.BlockSpec((None, tm, Fs), lambda i, j: (j, i, 0))],
        out_shape=[jax.ShapeDtypeStruct((S, D), F32), jax.ShapeDtypeStruct((1, D), F32),
                   jax.ShapeDtypeStruct((nsh, S, Fs), BF16), jax.ShapeDtypeStruct((nsh, S, Fs), BF16)],
        scratch_shapes=[pltpu.VMEM((tm, D), BF16), pltpu.VMEM((tm, D), F32)], compiler_params=_params(2),
    )(dxo, x, gain, g, u, wg, wu, wd)


def ffn_bwd_w(name, h, dxo, g, u, dg, du, tm=TOKEN_TILE):
    S, D = h.shape
    nsh, _, Fs = g.shape
    tn_dims = (((0,), (0,)), ((), ()))

    def body(h_ref, dxo_ref, g_ref, u_ref, dg_ref, du_ref, dwg_ref, dwu_ref, dwd_ref):
        i = pl.program_id(1)
        gf, uf = g_ref[...].astype(F32), u_ref[...].astype(F32)
        a = (gf * jax.nn.sigmoid(gf) * uf).astype(BF16)
        dd = (0.5 * dxo_ref[...]).astype(BF16)
        hv = h_ref[...]

        @pl.when(i == 0)
        def _():
            dwg_ref[...] = jnp.zeros_like(dwg_ref)
            dwu_ref[...] = jnp.zeros_like(dwu_ref)
            dwd_ref[...] = jnp.zeros_like(dwd_ref)

        dwg_ref[...] += lax.dot_general(hv, dg_ref[...], tn_dims, preferred_element_type=F32)
        dwu_ref[...] += lax.dot_general(hv, du_ref[...], tn_dims, preferred_element_type=F32)
        dwd_ref[...] += lax.dot_general(a, dd, tn_dims, preferred_element_type=F32)

    act = pl.BlockSpec((None, tm, Fs), lambda j, i: (j, i, 0))
    return pl.pallas_call(
        body, name=name, grid=(nsh, S // tm),
        in_specs=[pl.BlockSpec((tm, D), lambda j, i: (i, 0)), pl.BlockSpec((tm, D), lambda j, i: (i, 0)), act, act, act, act],
        out_specs=[pl.BlockSpec((None, D, Fs), lambda j, i: (j, 0, 0)), pl.BlockSpec((None, D, Fs), lambda j, i: (j, 0, 0)),
                   pl.BlockSpec((None, Fs, D), lambda j, i: (j, 0, 0))],
        out_shape=[jax.ShapeDtypeStruct((nsh, D, Fs), F32), jax.ShapeDtypeStruct((nsh, D, Fs), F32),
                   jax.ShapeDtypeStruct((nsh, Fs, D), F32)],
        compiler_params=_params(2),
    )(h, dxo, g, u, dg, du)


NEG_BIG = -1e30


def _causal_pairs(n, by_key):
    pairs = [(qi, ki) for qi in range(n) for ki in range(qi + 1)]
    if by_key:
        pairs.sort(key=lambda p: (p[1], p[0]))
    return jnp.asarray([p[0] for p in pairs], jnp.int32), jnp.asarray([p[1] for p in pairs], jnp.int32)


def _scores(q, k, masked):
    s = lax.dot_general(q, k, (((1,), (1,)), ((), ())), preferred_element_type=F32)
    if masked:
        row = lax.broadcasted_iota(jnp.int32, s.shape, 0)
        col = lax.broadcasted_iota(jnp.int32, s.shape, 1)
        s = jnp.where(row >= col, s, NEG_BIG)
    return s


def attn_fwd(name, q, k, v, t=ATTN_TILE):
    H, S, Dk = q.shape
    Dv = v.shape[1] // H
    qt, kt = _causal_pairs(S // t, by_key=False)

    def body(qt_ref, kt_ref, q_ref, k_ref, v_ref, o_ref, lse_ref, m_sc, l_sc, acc):
        qi, ki = qt_ref[pl.program_id(1)], kt_ref[pl.program_id(1)]

        @pl.when(ki == 0)
        def _():
            m_sc[...] = jnp.full_like(m_sc, NEG_BIG)
            l_sc[...] = jnp.zeros_like(l_sc)
            acc[...] = jnp.zeros_like(acc)

        def step(masked):
            s = _scores(q_ref[...], k_ref[...], masked)
            m_new = jnp.maximum(m_sc[...], jnp.max(s, axis=-1, keepdims=True))
            alpha = jnp.exp(m_sc[...] - m_new)
            p = jnp.exp(s - m_new)
            l_sc[...] = alpha * l_sc[...] + jnp.sum(p, axis=-1, keepdims=True)
            acc[...] = alpha * acc[...] + jnp.dot(p.astype(BF16), v_ref[...].astype(BF16), preferred_element_type=F32)
            m_sc[...] = m_new

        @pl.when(ki < qi)
        def _():
            step(False)

        @pl.when(ki == qi)
        def _():
            step(True)
            o_ref[...] = acc[...] / l_sc[...]
            lse_ref[...] = m_sc[...] + jnp.log(l_sc[...])

    return pl.pallas_call(
        body, name=name,
        grid_spec=pltpu.PrefetchScalarGridSpec(
            num_scalar_prefetch=2, grid=(H, qt.shape[0]),
            in_specs=[pl.BlockSpec((None, t, Dk), lambda h, s, qt, kt: (h, qt[s], 0)),
                      pl.BlockSpec((None, t, Dk), lambda h, s, qt, kt: (h, kt[s], 0)),
                      pl.BlockSpec((t, Dv), lambda h, s, qt, kt: (kt[s], h))],
            out_specs=[pl.BlockSpec((t, Dv), lambda h, s, qt, kt: (qt[s], h)),
                       pl.BlockSpec((None, t, 1), lambda h, s, qt, kt: (h, qt[s], 0))],
            scratch_shapes=[pltpu.VMEM((t, 1), F32), pltpu.VMEM((t, 1), F32), pltpu.VMEM((t, Dv), F32)]),
        out_shape=[jax.ShapeDtypeStruct((S, H * Dv), F32), jax.ShapeDtypeStruct((H, S, 1), F32)],
        compiler_params=_params(2),
    )(qt, kt, q, k, v)


def attn_bwd(name, q, k, v, do, o, lse, t=ATTN_TILE):
    H, S, Dk = q.shape
    Dv = v.shape[1] // H
    qt, kt = _causal_pairs(S // t, by_key=True)
    tn_dims = (((0,), (0,)), ((), ()))

    def body(qt_ref, kt_ref, q_ref, k_ref, v_ref, do_ref, o_ref, lse_ref, dq_ref, dk_ref, dv_ref):
        step_id = pl.program_id(1)
        qi, ki = qt_ref[step_id], kt_ref[step_id]

        @pl.when(step_id == 0)
        def _():
            dq_ref[...] = jnp.zeros_like(dq_ref)

        def step(masked):
            s = _scores(q_ref[...], k_ref[...], masked)
            p = jnp.exp(s - lse_ref[...])
            dov = do_ref[...]
            delta = jnp.sum(dov * o_ref[...], axis=-1, keepdims=True)
            dob = dov.astype(BF16)
            dp = lax.dot_general(dob, v_ref[...].astype(BF16), (((1,), (1,)), ((), ())), preferred_element_type=F32)
            ds = (p * (dp - delta)).astype(BF16)
            pdv = lax.dot_general(p.astype(BF16), dob, tn_dims, preferred_element_type=F32)
            pdk = lax.dot_general(ds, q_ref[...], tn_dims, preferred_element_type=F32)
            rows = pl.ds(pl.multiple_of(qi * t, t), t)
            dq_ref[rows, :] += jnp.dot(ds, k_ref[...], preferred_element_type=F32)
            return pdk, pdv

        @pl.when(ki == qi)
        def _():
            dk_ref[...] = jnp.zeros_like(dk_ref)
            dv_ref[...] = jnp.zeros_like(dv_ref)

        def accumulate(masked):
            pdk, pdv = step(masked)
            dk_ref[...] += pdk
            dv_ref[...] += pdv

        @pl.when(ki == qi)
        def _():
            accumulate(True)

        @pl.when(ki < qi)
        def _():
            accumulate(False)

    qmap = lambda h, s, qt, kt: (h, qt[s], 0)
    kmap = lambda h, s, qt, kt: (h, kt[s], 0)
    qrow = lambda h, s, qt, kt: (qt[s], h)
    krow = lambda h, s, qt, kt: (kt[s], h)
    return pl.pallas_call(
        body, name=name,
        grid_spec=pltpu.PrefetchScalarGridSpec(
            num_scalar_prefetch=2, grid=(H, qt.shape[0]),
            in_specs=[pl.BlockSpec((None, t, Dk), qmap), pl.BlockSpec((None, t, Dk), kmap), pl.BlockSpec((t, Dv), krow),
                      pl.BlockSpec((t, Dv), qrow), pl.BlockSpec((t, Dv), qrow), pl.BlockSpec((None, t, 1), qmap)],
            out_specs=[pl.BlockSpec((None, S, Dk), lambda h, s, qt, kt: (h, 0, 0)), pl.BlockSpec((None, t, Dk), kmap),
                       pl.BlockSpec((t, Dv), krow)]),
        out_shape=[jax.ShapeDtypeStruct((H, S, Dk), F32), jax.ShapeDtypeStruct((H, S, Dk), F32),
                   jax.ShapeDtypeStruct((S, H * Dv), F32)],
        compiler_params=_params(2),
    )(qt, kt, q, k, v, do, o, lse)


def _cmul(ar, ai, br, bi):
    return ar * br - ai * bi, ar * bi + ai * br


def ssm_scan(name, br, bi, a_r8, a_i8, *, reverse=False, states=None):
    S, L = br.shape
    tc, lb = min(SCAN_TIME_TILE, S), SCAN_LANE_TILE
    nt, nblk = S // tc, tc // SUBLANES
    with_da = states is not None

    def body(*refs):
        if with_da:
            br_ref, bi_ref, ar_ref, ai_ref, xr_ref, xi_ref, pr_ref, pi_ref, or_ref, oi_ref, dar_ref, dai_ref, cr_sc, ci_sc = refs
        else:
            br_ref, bi_ref, ar_ref, ai_ref, or_ref, oi_ref, cr_sc, ci_sc = refs
        t = pl.program_id(1)
        chunk = (nt - 1 - t) if reverse else t

        @pl.when(t == 0)
        def _():
            cr_sc[...] = jnp.zeros_like(cr_sc)
            ci_sc[...] = jnp.zeros_like(ci_sc)
            if with_da:
                dar_ref[...] = jnp.zeros_like(dar_ref)
                dai_ref[...] = jnp.zeros_like(dai_ref)

        a1r = ar_ref[0:1, :]
        a1i = -ai_ref[0:1, :] if reverse else ai_ref[0:1, :]
        a2r, a2i = _cmul(a1r, a1i, a1r, a1i)
        a4r, a4i = _cmul(a2r, a2i, a2r, a2i)
        row = lax.broadcasted_iota(jnp.int32, (SUBLANES, lb), 0)
        pw_r, pw_i = jnp.zeros((SUBLANES, lb), F32), jnp.zeros((SUBLANES, lb), F32)
        cur_r, cur_i = a1r, a1i
        for e in range(SUBLANES):
            r_at = (SUBLANES - 1 - e) if reverse else e
            pw_r = jnp.where(row == r_at, cur_r, pw_r)
            pw_i = jnp.where(row == r_at, cur_i, pw_i)
            cur_r, cur_i = _cmul(cur_r, cur_i, a1r, a1i)
        steps = ((1, a1r, a1i), (2, a2r, a2i), (4, a4r, a4i))

        def block(jb, carry):
            if with_da:
                cr, ci, acc_r, acc_i = carry
            else:
                cr, ci = carry
            idx = (nblk - 1 - jb) if reverse else jb
            r0 = pl.multiple_of(idx * SUBLANES, SUBLANES)
            xr = br_ref[pl.ds(r0, SUBLANES), :]
            xi = bi_ref[pl.ds(r0, SUBLANES), :]
            for d, pr, pi in steps:
                if reverse:
                    keep = row < SUBLANES - d
                    sr, si = pltpu.roll(xr, SUBLANES - d, 0), pltpu.roll(xi, SUBLANES - d, 0)
                else:
                    keep = row >= d
                    sr, si = pltpu.roll(xr, d, 0), pltpu.roll(xi, d, 0)
                sr, si = jnp.where(keep, sr, 0.0), jnp.where(keep, si, 0.0)
                xr, xi = xr + pr * sr - pi * si, xi + pr * si + pi * sr
            xr, xi = xr + pw_r * cr - pw_i * ci, xi + pw_r * ci + pw_i * cr
            or_ref[pl.ds(r0, SUBLANES), :] = xr
            oi_ref[pl.ds(r0, SUBLANES), :] = xi
            edge = 0 if reverse else SUBLANES - 1
            cr, ci = xr[edge:edge + 1, :], xi[edge:edge + 1, :]
            if not with_da:
                return cr, ci
            fr = xr_ref[pl.ds(r0, SUBLANES), :]
            fi = xi_ref[pl.ds(r0, SUBLANES), :]
            rp = pl.multiple_of(jnp.maximum(idx - 1, 0) * SUBLANES, SUBLANES)
            inside = idx > 0
            before_r = jnp.where(inside, xr_ref[pl.ds(rp, SUBLANES), :], pr_ref[...])
            before_i = jnp.where(inside, xi_ref[pl.ds(rp, SUBLANES), :], pi_ref[...])
            live = jnp.where(jnp.logical_or(inside, chunk > 0), 1.0, 0.0)
            last_r = before_r[SUBLANES - 1:SUBLANES, :] * live
            last_i = before_i[SUBLANES - 1:SUBLANES, :] * live
            pvr = jnp.where(row == 0, last_r, pltpu.roll(fr, 1, 0))
            pvi = jnp.where(row == 0, last_i, pltpu.roll(fi, 1, 0))
            acc_r = acc_r + xr * pvr + xi * pvi
            acc_i = acc_i + xi * pvr - xr * pvi
            return cr, ci, acc_r, acc_i

        init = (cr_sc[...], ci_sc[...])
        if with_da:
            init = init + (jnp.zeros((SUBLANES, lb), F32), jnp.zeros((SUBLANES, lb), F32))
        fin = lax.fori_loop(0, nblk, block, init)
        cr_sc[...] = fin[0]
        ci_sc[...] = fin[1]
        if with_da:
            dar_ref[...] += fin[2]
            dai_ref[...] += fin[3]

    tmap = (lambda l, t: (nt - 1 - t, l)) if reverse else (lambda l, t: (t, l))
    seq = pl.BlockSpec((tc, lb), tmap)
    par = pl.BlockSpec((SUBLANES, lb), lambda l, t: (0, l))
    in_specs, args = [seq, seq, par, par], [br, bi, a_r8, a_i8]
    out_specs = [seq, seq]
    out_shape = [jax.ShapeDtypeStruct((S, L), F32), jax.ShapeDtypeStruct((S, L), F32)]
    if with_da:
        pmap = lambda l, t: (jnp.maximum((nt - 1 - t) * (tc // SUBLANES) - 1, 0), l)
        prev = pl.BlockSpec((SUBLANES, lb), pmap)
        in_specs += [seq, seq, prev, prev]
        args += [states[0], states[1], states[0], states[1]]
        out_specs += [par, par]
        out_shape += [jax.ShapeDtypeStruct((SUBLANES, L), F32), jax.ShapeDtypeStruct((SUBLANES, L), F32)]
    return pl.pallas_call(
        body, name=name, grid=(L // lb, nt), in_specs=in_specs, out_specs=out_specs, out_shape=out_shape,
        scratch_shapes=[pltpu.VMEM((1, lb), F32), pltpu.VMEM((1, lb), F32)], compiler_params=_params(2),
    )(*args)


PREP_LANES = 512


def _ssm_prep_f(a_re, a_im, log_dt, bt_re, bt_im, c_re, c_im):
    off = pl.program_id(0) * PREP_LANES
    iota = lambda shape, d: lax.broadcasted_iota(jnp.int32, shape, d)
    grp_of_row = lambda shape: iota(shape, 0) >> int(math.log2(SSM_GROUP))
    grp_of_lane = lambda shape: (iota(shape, 1) + off) >> int(math.log2(SSM_STATE))
    rep = (grp_of_row((SSM_WIDTH, SSM_GROUPS)) == iota((SSM_WIDTH, SSM_GROUPS), 1)).astype(F32)
    til = ((iota((SSM_STATE, PREP_LANES), 1) & (SSM_STATE - 1)) == iota((SSM_STATE, PREP_LANES), 0)).astype(F32)
    m_rows = (grp_of_row((SSM_WIDTH, PREP_LANES)) == grp_of_lane((SSM_WIDTH, PREP_LANES))).astype(F32)
    m_grp = (iota((SSM_GROUPS, PREP_LANES), 0) == grp_of_lane((SSM_GROUPS, PREP_LANES))).astype(F32)
    dt = jnp.exp(log_dt)
    decay = jnp.exp(a_re * dt)
    ar = decay * jnp.cos(a_im * dt)
    ai = decay * jnp.sin(a_im * dt)
    den = a_re * a_re + a_im * a_im
    nr = ar - 1.0
    coef_r = (nr * a_re + ai * a_im) / den
    coef_i = (ai * a_re - nr * a_im) / den
    cr, ci = _hp_dot(rep, coef_r), _hp_dot(rep, coef_i)
    bb_r = cr * bt_re - ci * bt_im
    bb_i = cr * bt_im + ci * bt_re
    big = lambda m: _hp_dot(m, til) * m_rows
    lanes = lambda m: jnp.broadcast_to(jnp.sum(_hp_dot(m, til) * m_grp, axis=0, keepdims=True), (SUBLANES, PREP_LANES))
    return lanes(ar), lanes(ai), big(bb_r), big(bb_i), big(c_re), -big(c_im)


def _whole(arr, **kw):
    nd = arr.ndim
    return In(arr, arr.shape, lambda *_: (0,) * nd, **kw)


def loss_grad(name, y, target, tm=TOKEN_TILE):
    S, D = y.shape

    def body(y_ref, t_ref, dy_ref, part_ref):
        err = y_ref[...] - t_ref[...]
        dy_ref[...] = err * (1.0 / D)
        part_ref[...] = jnp.full(part_ref.shape, 0.5 * jnp.sum(jnp.mean(err * err, axis=-1)), F32)

    return pl.pallas_call(
        body, name=name, grid=(S // tm,),
        in_specs=[pl.BlockSpec((tm, D), lambda i: (i, 0)), pl.BlockSpec((tm, D), lambda i: (i, 0))],
        out_specs=[pl.BlockSpec((tm, D), lambda i: (i, 0)), pl.BlockSpec((SUBLANES, 128), lambda i: (i, 0))],
        out_shape=[jax.ShapeDtypeStruct((S, D), F32), jax.ShapeDtypeStruct((S // tm * SUBLANES, 128), F32)],
        compiler_params=_params(1),
    )(y, target)


def _adamw_math(w, g, m, v):
    m = ADAM_B1 * m + (1.0 - ADAM_B1) * g
    v = ADAM_B2 * v + (1.0 - ADAM_B2) * (g * g)
    m_hat = m / (1.0 - ADAM_B1 ** ADAM_STEP)
    v_hat = v / (1.0 - ADAM_B2 ** ADAM_STEP)
    delta = -ADAM_LR * (m_hat / (jnp.sqrt(v_hat) + ADAM_EPS) + ADAM_WD * w)
    return delta, m, v


def adamw_big(name, w, g, m, v):
    R, C = w.shape
    tr = R
    for cand in (512, 344, 256, 128):
        if R % cand == 0:
            tr = cand
            break

    def body(w_ref, g_ref, m_ref, v_ref, d_ref, nm_ref, nv_ref):
        d, nm, nv = _adamw_math(w_ref[...], g_ref[...], m_ref[...], v_ref[...])
        d_ref[...] = d
        nm_ref[...] = nm
        nv_ref[...] = nv

    spec = pl.BlockSpec((tr, C), lambda i: (i, 0))
    return pl.pallas_call(
        body, name=name, grid=(R // tr,), in_specs=[spec] * 4, out_specs=[spec] * 3,
        out_shape=[jax.ShapeDtypeStruct((R, C), F32)] * 3, compiler_params=_params(1),
    )(w, g, m, v)


def adamw_small(name, ws, gs, ms, vs):
    n = len(ws)

    def body(*refs):
        for k in range(n):
            d, nm, nv = _adamw_math(refs[k][...], refs[n + k][...], refs[2 * n + k][...], refs[3 * n + k][...])
            refs[4 * n + k][...] = d
            refs[5 * n + k][...] = nm
            refs[6 * n + k][...] = nv

    vm = pl.BlockSpec(memory_space=pltpu.VMEM)
    shapes = [jax.ShapeDtypeStruct(w.shape, F32) for w in ws]
    res = pl.pallas_call(
        body, name=name, in_specs=[vm] * (4 * n), out_specs=[vm] * (3 * n), out_shape=shapes * 3,
        compiler_params=pltpu.CompilerParams(vmem_limit_bytes=VMEM_LIMIT_BYTES),
    )(*ws, *gs, *ms, *vs)
    return res[:n], res[n:2 * n], res[2 * n:]


def _place():
    return lax.axis_index("x"), lax.axis_index("y"), lax.axis_index("c")


def _other_chips(x, y):
    return [(1 - x, y), (x, 1 - y), (1 - x, 1 - y)]


HBM = pl.BlockSpec(memory_space=pl.ANY)


def all_gather_halves(name, blocks):
    n = len(blocks)

    def body(*refs):
        in_refs, out_refs = refs[:n], refs[n:2 * n]
        send_sems, recv_sems = refs[2 * n:]
        x, y, c = _place()
        me, sibling = (x, y, c), (x, y, 1 - c)
        chips = _other_chips(x, y)

        def slot(a, px, py, pc):
            return out_refs[a].at[4 * px + 2 * py + pc]

        def copy(a, k, block, to, src=None):
            return pltpu.make_async_remote_copy(
                src_ref=slot(a, *block) if src is None else src, dst_ref=slot(a, *block),
                send_sem=send_sems.at[a, k], recv_sem=recv_sems.at[a, k], device_id=to, device_id_type=MESH)

        first = []
        for a in range(n):
            first.append(copy(a, 0, me, sibling, src=in_refs[a]))
            first += [copy(a, 1 + j, me, (*chip, c), src=in_refs[a]) for j, chip in enumerate(chips)]
        for cp in first:
            cp.start()
        passed = []
        for j, chip in enumerate(chips):
            for a in range(n):
                copy(a, 1 + j, (*chip, c), me).wait_recv()
                fw = copy(a, 4 + j, (*chip, c), sibling)
                fw.start()
                passed.append(fw)
        for a in range(n):
            copy(a, 0, sibling, me).wait_recv()
            for j, chip in enumerate(chips):
                copy(a, 4 + j, (*chip, 1 - c), me).wait_recv()
        for cp in first + passed:
            cp.wait_send()

    res = pl.pallas_call(
        body, name=name, in_specs=[HBM] * n, out_specs=[HBM] * n,
        out_shape=[jax.ShapeDtypeStruct((N_DEV,) + b.shape, b.dtype) for b in blocks],
        scratch_shapes=[pltpu.SemaphoreType.DMA((n, 7)), pltpu.SemaphoreType.DMA((n, 7))],
    )(*blocks)
    return list(res)


def pair_send_halves(name, grads):
    n = len(grads)

    def body(*refs):
        in_refs, out_refs = refs[:n], refs[n:2 * n]
        send_sems, recv_sems = refs[2 * n:]
        x, y, c = _place()
        cps = []
        for a in range(n):
            cp = pltpu.make_async_remote_copy(
                src_ref=in_refs[a].at[:, 1 - c], dst_ref=out_refs[a], send_sem=send_sems.at[a], recv_sem=recv_sems.at[a],
                device_id=(x, y, 1 - c), device_id_type=MESH)
            cp.start()
            cps.append(cp)
        for cp in cps:
            cp.wait()

    res = pl.pallas_call(
        body, name=name, in_specs=[HBM] * n, out_specs=[HBM] * n,
        out_shape=[jax.ShapeDtypeStruct((g.shape[0],) + g.shape[2:], g.dtype) for g in grads],
        scratch_shapes=[pltpu.SemaphoreType.DMA((n,)), pltpu.SemaphoreType.DMA((n,))],
    )(*grads)
    return list(res)


def pair_add(name, grad, got, c_arr):
    nsh, _, M, N = grad.shape
    tr = M
    for cand in (512, 256, 192, 128, 64, 16):
        if M % cand == 0:
            tr = cand
            break

    def body(c_ref, g_ref, p_ref, o_ref):
        o_ref[...] = (g_ref[...] + p_ref[...]).astype(BF16)

    return pl.pallas_call(
        body, name=name,
        grid_spec=pltpu.PrefetchScalarGridSpec(
            num_scalar_prefetch=1, grid=(nsh, M // tr),
            in_specs=[pl.BlockSpec((None, None, tr, N), lambda j, i, c_ref: (j, c_ref[0], i, 0)),
                      pl.BlockSpec((None, tr, N), lambda j, i, c_ref: (j, i, 0))],
            out_specs=pl.BlockSpec((None, tr, N), lambda j, i, c_ref: (j, i, 0))),
        out_shape=jax.ShapeDtypeStruct((nsh, M, N), BF16), compiler_params=_params(2),
    )(c_arr, grad, got)


def scatter_to_chips(name, parts):
    n = len(parts)

    def body(*refs):
        in_refs, out_refs = refs[:n], refs[n:2 * n]
        send_sems, recv_sems = refs[2 * n:]
        x, y, c = _place()
        mine = 2 * x + y
        chips = _other_chips(x, y)
        cps = []
        for a in range(n):
            for k, (px, py) in enumerate(chips):
                cp = pltpu.make_async_remote_copy(
                    src_ref=in_refs[a].at[2 * px + py], dst_ref=out_refs[a].at[mine],
                    send_sem=send_sems.at[a, k], recv_sem=recv_sems.at[a, k], device_id=(px, py, c), device_id_type=MESH)
                cp.start()
                cps.append((cp, a, k, px, py))
        for cp, a, k, px, py in cps:
            pltpu.make_async_remote_copy(
                src_ref=in_refs[a].at[mine], dst_ref=out_refs[a].at[2 * px + py],
                send_sem=send_sems.at[a, k], recv_sem=recv_sems.at[a, k], device_id=(px, py, c), device_id_type=MESH).wait_recv()
        for cp, *_ in cps:
            cp.wait_send()

    res = pl.pallas_call(
        body, name=name, in_specs=[HBM] * n, out_specs=[HBM] * n,
        out_shape=[jax.ShapeDtypeStruct(p.shape, p.dtype) for p in parts],
        scratch_shapes=[pltpu.SemaphoreType.DMA((n, 3)), pltpu.SemaphoreType.DMA((n, 3))],
    )(*parts)
    return list(res)


def sum_chips(name, q):
    nsh, M, N = q.shape
    tr = M
    for cand in (512, 256, 192, 128, 64, 16):
        if M % cand == 0:
            tr = cand
            break

    def body(q_ref, o_ref):
        acc = q_ref[0].astype(F32)
        for j in range(1, nsh):
            acc = acc + q_ref[j].astype(F32)
        o_ref[...] = acc

    return pl.pallas_call(
        body, name=name, grid=(M // tr,), in_specs=[pl.BlockSpec((nsh, tr, N), lambda i: (0, i, 0))],
        out_specs=pl.BlockSpec((tr, N), lambda i: (i, 0)), out_shape=jax.ShapeDtypeStruct((M, N), F32),
        compiler_params=_params(1),
    )(q)


def pair_exchange(name, halves):
    n = len(halves)

    def body(*refs):
        in_refs, out_refs = refs[:n], refs[n:2 * n]
        send_sems, recv_sems = refs[2 * n:]
        x, y, c = _place()
        cps = []
        for a in range(n):
            cp = pltpu.make_async_remote_copy(
                src_ref=in_refs[a], dst_ref=out_refs[a], send_sem=send_sems.at[a], recv_sem=recv_sems.at[a],
                device_id=(x, y, 1 - c), device_id_type=MESH)
            cp.start()
            cps.append(cp)
        for cp in cps:
            cp.wait()

    res = pl.pallas_call(
        body, name=name, in_specs=[HBM] * n, out_specs=[HBM] * n,
        out_shape=[jax.ShapeDtypeStruct(h.shape, h.dtype) for h in halves],
        scratch_shapes=[pltpu.SemaphoreType.DMA((n,)), pltpu.SemaphoreType.DMA((n,))],
    )(*halves)
    return list(res)


def all_reduce_small(name, v):
    R, C = v.shape

    def body(v_ref, o_ref, gath, send_sems, recv_sems):
        x, y, c = _place()
        me, sibling = (x, y, c), (x, y, 1 - c)
        chips = _other_chips(x, y)

        def slot(px, py, pc):
            return gath.at[4 * px + 2 * py + pc]

        def copy(k, block, to, src=None):
            return pltpu.make_async_remote_copy(
                src_ref=slot(*block) if src is None else src, dst_ref=slot(*block),
                send_sem=send_sems.at[k], recv_sem=recv_sems.at[k], device_id=to, device_id_type=MESH)

        first = [copy(0, me, sibling, src=v_ref)]
        first += [copy(1 + j, me, (*chip, c), src=v_ref) for j, chip in enumerate(chips)]
        for cp in first:
            cp.start()
        slot(*me)[...] = v_ref[...]
        passed = [copy(4 + j, (*chip, c), sibling) for j, chip in enumerate(chips)]
        for j, chip in enumerate(chips):
            copy(1 + j, (*chip, c), me).wait_recv()
            passed[j].start()
        copy(0, sibling, me).wait_recv()
        for j, chip in enumerate(chips):
            copy(4 + j, (*chip, 1 - c), me).wait_recv()
        for cp in first + passed:
            cp.wait_send()
        acc = gath[0]
        for d in range(1, N_DEV):
            acc = acc + gath[d]
        o_ref[...] = acc

    vm = pl.BlockSpec(memory_space=pltpu.VMEM)
    return pl.pallas_call(
        body, name=name, in_specs=[vm], out_specs=vm, out_shape=jax.ShapeDtypeStruct((R, C), F32),
        scratch_shapes=[pltpu.VMEM((N_DEV, R, C), F32), pltpu.SemaphoreType.DMA((7,)), pltpu.SemaphoreType.DMA((7,))],
        compiler_params=pltpu.CompilerParams(vmem_limit_bytes=VMEM_LIMIT_BYTES),
    )(v)


LANES = 128
PACK_ROW_MULTIPLE = 1024
SMALL_SHARDED = {
    "w_in": ((D_MODEL, 1216), 1), "mla_w_uq": ((MLA_Q_RANK, 768), 1), "mla_w_ukv": ((MLA_KV_RANK, 1024), 1),
    "ssm_w_glu": ((SSM_WIDTH, SSM_WIDTH), 0), "w_o": ((D_MODEL, D_MODEL), 0), "xattn_w_q": ((D_MODEL, 512), 0),
    "xattn_w_kv": ((D_MODEL, 1024), 0), "xattn_w_o": ((512, D_MODEL), 1),
}
FFN_NAMES = ["ffn1_w_gate", "ffn1_w_up", "ffn1_w_down", "ffn2_w_gate", "ffn2_w_up", "ffn2_w_down"]


def _shard_shape(name):
    (r, cdim), ax = SMALL_SHARDED[name]
    return (r // N_CHIPS, cdim) if ax == 0 else (r, cdim // N_CHIPS)


def _pack_shards(shards):
    parts = []
    for name in SMALL_SHARDED:
        a = shards[name]
        lead = a.shape[:-2]
        parts.append(a.reshape(lead + (a.shape[-2] * a.shape[-1] // LANES, LANES)))
    rows = sum(q.shape[-2] for q in parts)
    parts.append(jnp.zeros(lead + (-rows % PACK_ROW_MULTIPLE, LANES), parts[0].dtype))
    return jnp.concatenate(parts, axis=-2)


def _unpack_shards(packed):
    out, r0 = {}, 0
    lead = packed.shape[:-2]
    for name in SMALL_SHARDED:
        r, cdim = _shard_shape(name)
        rows = r * cdim // LANES
        out[name] = packed[..., r0:r0 + rows, :].reshape(lead + (r, cdim))
        r0 += rows
    return out


def _full_from_shards(name, sh):
    (r, cdim), ax = SMALL_SHARDED[name]
    if ax == 0:
        return sh.reshape(r, cdim)
    return jnp.transpose(sh, (1, 0, 2)).reshape(r, cdim)


def _shards_from_full(name, full):
    (r, cdim), ax = SMALL_SHARDED[name]
    if ax == 0:
        return full.reshape(N_CHIPS, r // N_CHIPS, cdim)
    return jnp.transpose(full.reshape(r, N_CHIPS, cdim // N_CHIPS), (1, 0, 2))


SMALL_REPL = {
    "ffn1_norm": (1, 1024), "mix_norm": (1, 1024), "mla_q_norm": (1, 384), "mla_kv_norm": (1, 256),
    "mla_qk_norm_q": (1, 192), "mla_qk_norm_k": (1, 192), "ssm_a_re": (32, 64), "ssm_a_im": (32, 64),
    "ssm_log_dt": (32, 1), "ssm_b_re": (32, 64, 16), "ssm_b_im": (32, 64, 16), "ssm_c_re": (32, 16, 64),
    "ssm_c_im": (32, 16, 64), "ssm_d": (1, 512), "ssm_b_glu": (1, 512), "out_norm_mla": (1, 512),
    "out_norm_ssm": (1, 512), "xattn_norm": (1, 1024), "mem_norm": (1, 1024), "xattn_q_norm": (1, 128),
    "xattn_k_norm": (1, 128), "ffn2_norm": (1, 1024),
}


def _pack_repl(grads):
    flat = jnp.concatenate([grads[n].reshape(-1) for n in SMALL_REPL])
    rows = -(-flat.shape[0] // (LANES * SUBLANES)) * SUBLANES
    return jnp.pad(flat, (0, rows * LANES - flat.shape[0])).reshape(rows, LANES)


def _unpack_repl(packed):
    flat, out, o = packed.reshape(-1), {}, 0
    for n, shp in SMALL_REPL.items():
        size = int(np.prod(shp))
        out[n] = flat[o:o + size].reshape(shp)
        o += size
    return out


def _rope_tables(positions):
    half = MLA_ROPE // 2
    inv = ROPE_THETA ** (-jnp.arange(half, dtype=F32) / half)
    ang = positions.astype(F32)[:, None] * inv[None, :]
    cos, sin = jnp.cos(ang), jnp.sin(ang)
    S = positions.shape[0]
    z = lambda w: jnp.zeros((S, w), F32)
    keep = jnp.concatenate([jnp.ones((S, MLA_NOPE), F32), cos, cos, z(HEAD_PAD - MLA_QK)], axis=1)
    from_hi = jnp.concatenate([z(MLA_NOPE), -sin, z(HEAD_PAD - MLA_NOPE - half)], axis=1)
    from_lo = jnp.concatenate([z(MLA_NOPE + half), sin, z(HEAD_PAD - MLA_QK)], axis=1)
    return keep, from_hi, from_lo


def _norm_rope(x, g, keep, from_hi, from_lo):
    y = x * lax.rsqrt(jnp.sum(x * x, axis=-1, keepdims=True) * (1.0 / MLA_QK) + EPS) * g
    half = MLA_ROPE // 2
    return y * keep + _lane_roll(y, HEAD_PAD - half) * from_hi + _lane_roll(y, half) * from_lo


def local_step(x, mem, positions, target, w, wb):
    S = x.shape[0]
    tm = min(TOKEN_TILE, S)
    nt = S // tm
    g1 = (nt,)
    gh = (MLA_HEADS, nt)
    tile = lambda arr, cols=None, cb=0, **kw: In(arr, (tm, arr.shape[1] if cols is None else cols), lambda i, cb=cb: (i, cb), **kw)
    par = lambda arr, **kw: In(arr, arr.shape, lambda *_: (0, 0), diff=True, **kw)
    otile = lambda cols, dt: Out((S, cols), dt, (tm, cols), lambda i: (i, 0))
    grads = {}

    x1, h1, g_1, u_1 = ffn_fwd("ffn1_fwd", x, w["ffn1_norm"], wb["ffn1_w_gate"], wb["ffn1_w_up"], wb["ffn1_w_down"])

    f_norm = lambda xv, g: (_rms(xv, g),)
    mix_ins = [tile(x1, diff=True), par(w["mix_norm"], acc=(0,))]
    mix_outs = [otile(D_MODEL, BF16)]
    (h2,) = seg_fwd("mix_norm_fwd", f_norm, g1, mix_ins, mix_outs)

    pq, pkv, pkr, pu = mm("w_in_fwd", [h2], [[wb["w_in_q"], wb["w_in_kv"], wb["w_in_kr"], wb["w_in_u"]]], [F32] * 4)

    f_lat = lambda a, b, ga, gb: (_rms(a, ga), _rms(b, gb))
    lat_ins = [tile(pq, diff=True), tile(pkv, diff=True), par(w["mla_q_norm"], acc=(0,)), par(w["mla_kv_norm"], acc=(0,))]
    lat_outs = [otile(MLA_Q_RANK, BF16), otile(MLA_KV_RANK, BF16)]
    cq, ckv = seg_fwd("latent_norm_fwd", f_lat, g1, lat_ins, lat_outs)

    (q0,) = mm("w_uq_fwd", [cq], [[wb["w_uq"]]], [F32])
    kn0, v0 = mm("w_ukv_fwd", [ckv], [[wb["w_ukv_k"], wb["w_ukv_v"]]], [F32, F32])

    keep, from_hi, from_lo = _rope_tables(positions)

    scale = MLA_QK ** -0.5

    def f_qk(qh, knh, kr, kp, fh, fl, gq, gk):
        return (_norm_rope(qh, gq, kp, fh, fl) * scale, _norm_rope(jnp.concatenate([knh, kr], axis=-1), gk, kp, fh, fl))

    hmap = lambda h, i: (i, h)
    tmap0 = lambda h, i: (i, 0)
    qk_ins = [In(q0, (tm, HEAD_PAD), hmap, diff=True), In(kn0, (tm, MLA_NOPE), hmap, diff=True),
              In(pkr, (tm, LANES), tmap0, diff=True,
                 grad=((MLA_HEADS, S, LANES), (None, tm, LANES), lambda h, i: (h, i, 0))),
              In(keep, (tm, HEAD_PAD), tmap0), In(from_hi, (tm, HEAD_PAD), tmap0), In(from_lo, (tm, HEAD_PAD), tmap0),
              In(w["qk_gain_q"], (1, HEAD_PAD), lambda h, i: (0, 0), diff=True, acc=(0, 1)),
              In(w["qk_gain_k"], (1, HEAD_PAD), lambda h, i: (0, 0), diff=True, acc=(0, 1))]
    head_out = Out((MLA_HEADS, S, HEAD_PAD), BF16, (None, tm, HEAD_PAD), lambda h, i: (h, i, 0))
    qh, kh = seg_fwd("qk_norm_rope_fwd", f_qk, gh, qk_ins, [head_out, head_out])

    o_mla, lse = attn_fwd("mla_attn_fwd", qh, kh, v0, t=min(ATTN_TILE, S))

    prep_grid = (SSM_LANES // PREP_LANES,)
    prep_ins = [_whole(w[k], diff=True, acc=(0,))
                for k in ("ssm_a_re", "ssm_a_im", "ssm_log_dt", "ssm_bt_re", "ssm_bt_im", "ssm_c2_re", "ssm_c2_im")]
    lane_out = lambda rows, dt: Out((rows, SSM_LANES), dt, (rows, PREP_LANES), lambda i: (0, i))
    prep_outs = [lane_out(SUBLANES, F32)] * 2 + [lane_out(SSM_WIDTH, BF16)] * 4
    a_r8, a_i8, bb_r, bb_i, cb_r, ncb_i = seg_fwd("ssm_prep_fwd", _ssm_prep_f, prep_grid, prep_ins, prep_outs)

    bu_r, bu_i = mm("ssm_bu_fwd", [pu], [[bb_r, bb_i]], [F32, F32], tm=256)
    xs_r, xs_i = ssm_scan("ssm_scan_fwd", bu_r, bu_i, a_r8, a_i8)
    (y_lin,) = mm("ssm_cx_fwd", [xs_r, xs_i], [[cb_r], [ncb_i]], [F32], trans=True, tm=256)

    f_gelu = lambda yl, u, d: (jax.nn.gelu(yl + d * u),)
    gelu_ins = [tile(y_lin, diff=True), tile(pu, diff=True), par(w["ssm_d"], acc=(0,))]
    (gl,) = seg_fwd("ssm_gelu_fwd", f_gelu, g1, gelu_ins, [otile(SSM_WIDTH, F32)])
    (z,) = mm("ssm_glu_fwd", [gl], [[wb["ssm_w_glu"]]], [F32])

    def f_glu(g, zz, o, b, gm, gs):
        return (_rms(o, gm), _rms(g * jax.nn.sigmoid(zz + b), gs))

    glu_ins = [tile(gl, diff=True), tile(z, diff=True), tile(o_mla, diff=True), par(w["ssm_b_glu"], acc=(0,)),
               par(w["out_norm_mla"], acc=(0,)), par(w["out_norm_ssm"], acc=(0,))]
    glu_outs = [otile(SSM_WIDTH, BF16), otile(SSM_WIDTH, BF16)]
    yn_mla, yn_ssm = seg_fwd("glu_out_norm_fwd", f_glu, g1, glu_ins, glu_outs)

    (x2,) = mm("w_o_fwd", [yn_mla, yn_ssm], [[wb["w_o_mla"]], [wb["w_o_ssm"]]], [F32], adds=[x1])

    xn_ins = [tile(x2, diff=True), par(w["xattn_norm"], acc=(0,))]
    (h3,) = seg_fwd("xattn_norm_fwd", f_norm, g1, xn_ins, [otile(D_MODEL, BF16)])
    (xq0,) = mm("xattn_q_fwd", [h3], [[wb["xattn_w_q"]]], [F32])

    M = mem.shape[0]
    mem_ins = [In(mem, (M, D_MODEL), lambda i: (0, 0)), par(w["mem_norm"], acc=(0,))]
    (mn,) = seg_fwd("mem_norm_fwd", f_norm, (1,), mem_ins, [Out((M, D_MODEL), BF16, (M, D_MODEL), lambda i: (0, 0))])
    (kvm,) = mm("xattn_kv_fwd", [mn], [[wb["xattn_w_kv"]]], [F32])

    xscale = XHD ** -0.5

    def f_xattn(qv, kv_k, kv_v, gq, gk):
        qn, kn = _rms(qv, gq), _rms(kv_k, gk)
        s = _bdot_nt(qn, kn) * xscale
        p = jax.nn.softmax(s, axis=-1)
        return (_bdot_nn(p, kv_v),)

    xa_ins = [In(xq0, (tm, XHD), hmap, diff=True),
              In(kvm, (M, XHD), lambda h, i: (0, h), diff=True, acc=(1,), grad=((M, XH * XHD), (M, XHD), lambda h, i: (0, h))),
              In(kvm, (M, XHD), lambda h, i: (0, XH + h), diff=True, acc=(1,), grad=((M, XH * XHD), (M, XHD), lambda h, i: (0, h))),
              In(w["xattn_q_norm"], (1, XHD), lambda h, i: (0, 0), diff=True, acc=(0, 1)),
              In(w["xattn_k_norm"], (1, XHD), lambda h, i: (0, 0), diff=True, acc=(0, 1))]
    xa_outs = [Out((S, XH * XHD), F32, (tm, XHD), hmap)]
    (o2,) = seg_fwd("xattn_fwd", f_xattn, gh, xa_ins, xa_outs)

    (x3,) = mm("xattn_o_fwd", [o2], [[wb["xattn_w_o"]]], [F32], adds=[x2])

    x4, h4, g_2, u_2 = ffn_fwd("ffn2_fwd", x3, w["ffn2_norm"], wb["ffn2_w_gate"], wb["ffn2_w_up"], wb["ffn2_w_down"])
    dx4, parts = loss_grad("loss", x4, target)
    loss = jnp.sum(parts[::SUBLANES, 0])

    dx3, grads["ffn2_norm"], dg_2, du_2 = ffn_bwd_act("ffn2_bwd_act", dx4, x3, w["ffn2_norm"], g_2, u_2,
                                                      wb["ffn2_w_gate"], wb["ffn2_w_up"], wb["ffn2_w_down"])
    grads["ffn2_w_gate"], grads["ffn2_w_up"], grads["ffn2_w_down"] = ffn_bwd_w("ffn2_bwd_w", h4, dx4, g_2, u_2, dg_2, du_2)

    (do2,) = mm("xattn_o_bwd", [dx3], [[wb["xattn_w_o"]]], [F32], trans=True)
    (grads["xattn_w_o"],) = mm_tn("xattn_o_bwd_w", [o2], [dx3], [(0, [0])])

    dxq0, dkm, dvm, grads["xattn_q_norm"], grads["xattn_k_norm"] = seg_bwd("xattn_bwd", f_xattn, gh, xa_ins, xa_outs, [do2])
    (dh3,) = mm("xattn_q_bwd", [dxq0], [[wb["xattn_w_q"]]], [F32], trans=True)
    (grads["xattn_w_q"],) = mm_tn("xattn_q_bwd_w", [h3], [dxq0], [(0, [0])])
    wkv_k, wkv_v = wb["xattn_w_kv"][:, :XH * XHD], wb["xattn_w_kv"][:, XH * XHD:]
    (dmn,) = mm("xattn_kv_bwd", [dkm, dvm], [[wkv_k], [wkv_v]], [F32], trans=True)
    gk_w, gv_w = mm_tn("xattn_kv_bwd_w", [mn], [dkm, dvm], [(0, [0]), (0, [1])])
    grads["xattn_w_kv"] = jnp.concatenate([gk_w, gv_w], axis=1)
    (grads["mem_norm"],) = seg_bwd("mem_norm_bwd", f_norm, (1,), mem_ins,
                                   [Out((M, D_MODEL), BF16, (M, D_MODEL), lambda i: (0, 0))], [dmn])

    dx2, grads["xattn_norm"] = seg_bwd("xattn_norm_bwd", f_norm, g1, xn_ins, [otile(D_MODEL, BF16)], [dh3], adds={0: dx3})

    dyn_mla, dyn_ssm = mm("w_o_bwd", [dx2], [[wb["w_o_mla"], wb["w_o_ssm"]]], [F32, F32], trans=True)
    go_mla, go_ssm = mm_tn("w_o_bwd_w", [yn_mla, yn_ssm], [dx2], [(0, [0]), (1, [0])])
    grads["w_o"] = jnp.concatenate([go_mla, go_ssm], axis=0)

    dgl_a, dz, do_mla, grads["ssm_b_glu"], grads["out_norm_mla"], grads["out_norm_ssm"] = seg_bwd(
        "glu_out_norm_bwd", f_glu, g1, glu_ins, glu_outs, [dyn_mla, dyn_ssm])
    (dgl,) = mm("ssm_glu_bwd", [dz], [[wb["ssm_w_glu"]]], [F32], trans=True, adds=[dgl_a])
    (grads["ssm_w_glu"],) = mm_tn("ssm_glu_bwd_w", [gl], [dz], [(0, [0])])
    dy_lin, du_a, grads["ssm_d"] = seg_bwd("ssm_gelu_bwd", f_gelu, g1, gelu_ins, [otile(SSM_WIDTH, F32)], [dgl])

    gx_r, gx_i = mm("ssm_cx_bwd", [dy_lin], [[cb_r, ncb_i]], [F32, F32], tm=256)
    dcb_r, dncb_i = mm_tn("ssm_cx_bwd_w", [dy_lin], [xs_r, xs_i], [(0, [0]), (0, [1])], tm=256)
    lam_r, lam_i, da_r8, da_i8 = ssm_scan("ssm_scan_bwd", gx_r, gx_i, a_r8, a_i8, reverse=True, states=(xs_r, xs_i))
    (du,) = mm("ssm_bu_bwd", [lam_r, lam_i], [[bb_r], [bb_i]], [F32], trans=True, adds=[du_a], tm=256)
    dbb_r, dbb_i = mm_tn("ssm_bu_bwd_w", [pu], [lam_r, lam_i], [(0, [0]), (0, [1])], tm=256)
    prep_g = seg_bwd("ssm_prep_bwd", _ssm_prep_f, prep_grid, prep_ins, prep_outs, [da_r8, da_i8, dbb_r, dbb_i, dcb_r, dncb_i])
    for k, gname in enumerate(("ssm_a_re", "ssm_a_im", "ssm_log_dt", "ssm_bt_re", "ssm_bt_im", "ssm_c2_re", "ssm_c2_im")):
        grads[gname] = prep_g[k]

    dqh, dkh, dv0 = attn_bwd("mla_attn_bwd", qh, kh, v0, do_mla, o_mla, lse, t=min(ATTN_TILE, S))

    dq0, dkn0, dkr4, grads["qk_gain_q"], grads["qk_gain_k"] = seg_bwd(
        "qk_norm_rope_bwd", f_qk, gh, qk_ins, [head_out, head_out], [dqh, dkh])

    (dcq,) = mm("w_uq_bwd", [dq0], [[wb["w_uq"]]], [F32], trans=True)
    (grads["w_uq"],) = mm_tn("w_uq_bwd_w", [cq], [dq0], [(0, [0])])
    (dckv,) = mm("w_ukv_bwd", [dkn0, dv0], [[wb["w_ukv_k"]], [wb["w_ukv_v"]]], [F32], trans=True)
    grads["w_ukv_k"], grads["w_ukv_v"] = mm_tn("w_ukv_bwd_w", [ckv], [dkn0, dv0], [(0, [0]), (0, [1])])

    dpq, dpkv, grads["mla_q_norm"], grads["mla_kv_norm"] = seg_bwd("latent_norm_bwd", f_lat, g1, lat_ins, lat_outs, [dcq, dckv])

    dkr_list = [dkr4[h] for h in range(MLA_HEADS)]
    (dh2,) = mm("w_in_bwd", [dpq, dpkv, du] + dkr_list,
                [[wb["w_in_q"]], [wb["w_in_kv"]], [wb["w_in_u"]]] + [[wb["w_in_kr"]]] * MLA_HEADS, [F32], trans=True)
    grads["w_in_q"], grads["w_in_kv"], grads["w_in_u"], grads["w_in_kr"] = mm_tn(
        "w_in_bwd_w", [h2], [dpq, dpkv, du] + dkr_list, [(0, [0]), (0, [1]), (0, [2]), (0, [3, 4, 5, 6])])

    dx1, grads["mix_norm"] = seg_bwd("mix_norm_bwd", f_norm, g1, mix_ins, mix_outs, [dh2], adds={0: dx2})

    dx, grads["ffn1_norm"], dg_1, du_1 = ffn_bwd_act("ffn1_bwd_act", dx1, x, w["ffn1_norm"], g_1, u_1,
                                                     wb["ffn1_w_gate"], wb["ffn1_w_up"], wb["ffn1_w_down"])
    grads["ffn1_w_gate"], grads["ffn1_w_up"], grads["ffn1_w_down"] = ffn_bwd_w("ffn1_bwd_w", h1, dx1, g_1, u_1, dg_1, du_1)
    return loss, dx, grads


def _pad_cols(a, n):
    return jnp.pad(a, ((0, 0), (0, n - a.shape[1])))


def _step_weights(full_bf16):
    wb = {}
    w_in = full_bf16["w_in"]
    wb["w_in_q"] = w_in[:, :MLA_Q_RANK]
    wb["w_in_kv"] = w_in[:, MLA_Q_RANK:MLA_Q_RANK + MLA_KV_RANK]
    wb["w_in_kr"] = _pad_cols(w_in[:, MLA_Q_RANK + MLA_KV_RANK:MLA_Q_RANK + MLA_KV_RANK + MLA_ROPE], LANES)
    wb["w_in_u"] = w_in[:, MLA_Q_RANK + MLA_KV_RANK + MLA_ROPE:]
    uq = full_bf16["mla_w_uq"].reshape(MLA_Q_RANK, MLA_HEADS, MLA_QK)
    wb["w_uq"] = jnp.pad(uq, ((0, 0), (0, 0), (0, HEAD_PAD - MLA_QK))).reshape(MLA_Q_RANK, MLA_HEADS * HEAD_PAD)
    ukv = full_bf16["mla_w_ukv"].reshape(MLA_KV_RANK, MLA_HEADS, MLA_NOPE + MLA_V)
    wb["w_ukv_k"] = ukv[:, :, :MLA_NOPE].reshape(MLA_KV_RANK, MLA_HEADS * MLA_NOPE)
    wb["w_ukv_v"] = ukv[:, :, MLA_NOPE:].reshape(MLA_KV_RANK, MLA_HEADS * MLA_V)
    wb["ssm_w_glu"] = full_bf16["ssm_w_glu"]
    wb["w_o_mla"] = full_bf16["w_o"][:SSM_WIDTH]
    wb["w_o_ssm"] = full_bf16["w_o"][SSM_WIDTH:]
    for k in ("xattn_w_q", "xattn_w_kv", "xattn_w_o"):
        wb[k] = full_bf16[k]
    return wb


def _problem_grads(g):
    out = {}
    kr = g["w_in_kr"][:, :MLA_ROPE]
    out["w_in"] = jnp.concatenate([g["w_in_q"], g["w_in_kv"], kr, g["w_in_u"]], axis=1)
    out["mla_w_uq"] = g["w_uq"].reshape(MLA_Q_RANK, MLA_HEADS, HEAD_PAD)[:, :, :MLA_QK].reshape(MLA_Q_RANK, MLA_HEADS * MLA_QK)
    k3 = g["w_ukv_k"].reshape(MLA_KV_RANK, MLA_HEADS, MLA_NOPE)
    v3 = g["w_ukv_v"].reshape(MLA_KV_RANK, MLA_HEADS, MLA_V)
    out["mla_w_ukv"] = jnp.concatenate([k3, v3], axis=2).reshape(MLA_KV_RANK, MLA_HEADS * (MLA_NOPE + MLA_V))
    for k in ("ssm_w_glu", "w_o", "xattn_w_q", "xattn_w_kv", "xattn_w_o"):
        out[k] = g[k]
    out["mla_qk_norm_q"] = g["qk_gain_q"][:, :MLA_QK]
    out["mla_qk_norm_k"] = g["qk_gain_k"][:, :MLA_QK]
    out["ssm_b_re"] = jnp.transpose(g["ssm_bt_re"].reshape(SSM_GROUPS, SSM_GROUP, SSM_STATE), (0, 2, 1))
    out["ssm_b_im"] = jnp.transpose(g["ssm_bt_im"].reshape(SSM_GROUPS, SSM_GROUP, SSM_STATE), (0, 2, 1))
    out["ssm_c_re"] = g["ssm_c2_re"].reshape(SSM_GROUPS, SSM_GROUP, SSM_STATE)
    out["ssm_c_im"] = g["ssm_c2_im"].reshape(SSM_GROUPS, SSM_GROUP, SSM_STATE)
    for k in SMALL_REPL:
        if k not in out:
            out[k] = g[k]
    for k in FFN_NAMES:
        out[k] = g[k]
    return out


def _step_params(p):
    row = lambda a: a.reshape(1, -1)
    w = {k: row(p[k]) for k in ("ffn1_norm", "mix_norm", "mla_q_norm", "mla_kv_norm", "ssm_b_glu", "out_norm_mla",
                                "out_norm_ssm", "xattn_norm", "mem_norm", "xattn_q_norm", "xattn_k_norm", "ffn2_norm")}
    w["qk_gain_q"] = _pad_cols(row(p["mla_qk_norm_q"]), HEAD_PAD)
    w["qk_gain_k"] = _pad_cols(row(p["mla_qk_norm_k"]), HEAD_PAD)
    w["ssm_a_re"], w["ssm_a_im"] = p["ssm_a_re"], p["ssm_a_im"]
    w["ssm_log_dt"] = p["ssm_log_dt"].reshape(SSM_GROUPS, 1)
    w["ssm_bt_re"] = jnp.transpose(p["ssm_b_re"], (0, 2, 1)).reshape(SSM_WIDTH, SSM_STATE)
    w["ssm_bt_im"] = jnp.transpose(p["ssm_b_im"], (0, 2, 1)).reshape(SSM_WIDTH, SSM_STATE)
    w["ssm_c2_re"] = p["ssm_c_re"].reshape(SSM_WIDTH, SSM_STATE)
    w["ssm_c2_im"] = p["ssm_c_im"].reshape(SSM_WIDTH, SSM_STATE)
    w["ssm_d"] = p["ssm_d"].reshape(1, SSM_WIDTH)
    return w


ARG_NAMES = ['x', 'mem', 'positions', 'ffn1_norm', 'ffn1_w_gate', 'ffn1_w_up', 'ffn1_w_down', 'mix_norm', 'w_in', 'mla_q_norm', 'mla_w_uq', 'mla_kv_norm', 'mla_w_ukv', 'mla_qk_norm_q', 'mla_qk_norm_k', 'ssm_a_re', 'ssm_a_im', 'ssm_log_dt', 'ssm_b_re', 'ssm_b_im', 'ssm_c_re', 'ssm_c_im', 'ssm_d', 'ssm_w_glu', 'ssm_b_glu', 'out_norm_mla', 'out_norm_ssm', 'w_o', 'xattn_norm', 'mem_norm', 'xattn_w_q', 'xattn_w_kv', 'xattn_q_norm', 'xattn_k_norm', 'xattn_w_o', 'ffn2_norm', 'ffn2_w_gate', 'ffn2_w_up', 'ffn2_w_down']
WEIGHT_NAMES = ARG_NAMES[3:]


def _gather_weights(p, c):
    half = lambda a: lax.dynamic_slice_in_dim(a, c * (a.shape[0] // 2), a.shape[0] // 2, axis=0)
    packed = _pack_shards({k: p[k].astype(BF16) for k in SMALL_SHARDED})
    ffn1 = [half(p[k].astype(BF16)) for k in FFN_NAMES[:3]]
    ffn2 = [half(p[k].astype(BF16)) for k in FFN_NAMES[3:]]
    me = 4 * lax.axis_index("x") + 2 * lax.axis_index("y") + c
    own = lambda got, blocks: [lax.dynamic_update_index_in_dim(g, b, me, 0) for g, b in zip(got, blocks)]
    got1 = own(all_gather_halves("all_gather_weights_a", ffn1 + [half(packed)]), ffn1 + [half(packed)])
    got2 = own(all_gather_halves("all_gather_weights_b", ffn2), ffn2)
    wb = {}
    for k, a in zip(FFN_NAMES, got1[:3] + got2):
        wb[k] = a.reshape(N_CHIPS, 2 * a.shape[1], a.shape[2])
    small = _unpack_shards(got1[3].reshape(N_CHIPS, -1, LANES))
    wb.update(_step_weights({k: _full_from_shards(k, small[k]) for k in SMALL_SHARDED}))
    return wb


def _reduce_sharded(gfull, c):
    c_arr = jnp.reshape(c, (1,)).astype(jnp.int32)
    packed = _pack_shards({k: _shards_from_full(k, gfull[k]) for k in SMALL_SHARDED})
    arrs = [gfull[k] for k in FFN_NAMES] + [packed]
    split = [a.reshape(N_CHIPS, 2, a.shape[1] // 2, a.shape[2]) for a in arrs]
    got = pair_send_halves("grad_pair_send", split)
    parts = [pair_add(f"grad_pair_add_{k}", s, g, c_arr) for k, (s, g) in enumerate(zip(split, got))]
    chip = 2 * lax.axis_index("x") + lax.axis_index("y")
    landed = [lax.dynamic_update_index_in_dim(q, lax.dynamic_index_in_dim(p, chip, 0, keepdims=False), chip, 0)
              for q, p in zip(scatter_to_chips("grad_scatter", parts), parts)]
    halves = [sum_chips(f"grad_sum_{k}", q) for k, q in enumerate(landed)]
    theirs = pair_exchange("grad_pair_exchange", halves)
    shards = [jnp.where(c == 0, jnp.concatenate([h, t], axis=0), jnp.concatenate([t, h], axis=0))
              for h, t in zip(halves, theirs)]
    out = dict(zip(FFN_NAMES, shards[:6]))
    out.update(_unpack_shards(shards[6]))
    return out


def kernel(x, mem, positions, ffn1_norm, ffn1_w_gate, ffn1_w_up, ffn1_w_down, mix_norm, w_in, mla_q_norm, mla_w_uq, mla_kv_norm, mla_w_ukv, mla_qk_norm_q, mla_qk_norm_k, ssm_a_re, ssm_a_im, ssm_log_dt, ssm_b_re, ssm_b_im, ssm_c_re, ssm_c_im, ssm_d, ssm_w_glu, ssm_b_glu, out_norm_mla, out_norm_ssm, w_o, xattn_norm, mem_norm, xattn_w_q, xattn_w_kv, xattn_q_norm, xattn_k_norm, xattn_w_o, ffn2_norm, ffn2_w_gate, ffn2_w_up, ffn2_w_down, loss_target, m_ffn1_norm, m_ffn1_w_gate, m_ffn1_w_up, m_ffn1_w_down, m_mix_norm, m_w_in, m_mla_q_norm, m_mla_w_uq, m_mla_kv_norm, m_mla_w_ukv, m_mla_qk_norm_q, m_mla_qk_norm_k, m_ssm_a_re, m_ssm_a_im, m_ssm_log_dt, m_ssm_b_re, m_ssm_b_im, m_ssm_c_re, m_ssm_c_im, m_ssm_d, m_ssm_w_glu, m_ssm_b_glu, m_out_norm_mla, m_out_norm_ssm, m_w_o, m_xattn_norm, m_mem_norm, m_xattn_w_q, m_xattn_w_kv, m_xattn_q_norm, m_xattn_k_norm, m_xattn_w_o, m_ffn2_norm, m_ffn2_w_gate, m_ffn2_w_up, m_ffn2_w_down, v_ffn1_norm, v_ffn1_w_gate, v_ffn1_w_up, v_ffn1_w_down, v_mix_norm, v_w_in, v_mla_q_norm, v_mla_w_uq, v_mla_kv_norm, v_mla_w_ukv, v_mla_qk_norm_q, v_mla_qk_norm_k, v_ssm_a_re, v_ssm_a_im, v_ssm_log_dt, v_ssm_b_re, v_ssm_b_im, v_ssm_c_re, v_ssm_c_im, v_ssm_d, v_ssm_w_glu, v_ssm_b_glu, v_out_norm_mla, v_out_norm_ssm, v_w_o, v_xattn_norm, v_mem_norm, v_xattn_w_q, v_xattn_w_kv, v_xattn_q_norm, v_xattn_k_norm, v_xattn_w_o, v_ffn2_norm, v_ffn2_w_gate, v_ffn2_w_up, v_ffn2_w_down):
    args = dict(locals())
    c = lax.axis_index("c")
    p = {k: args[k][0] for k in WEIGHT_NAMES}
    mom = {k: args["m_" + k][0] for k in WEIGHT_NAMES}
    var = {k: args["v_" + k][0] for k in WEIGHT_NAMES}

    wb = _gather_weights(p, c)
    loss, dx, g = local_step(x[0], mem[0], positions[0], loss_target[0], _step_params(p), wb)
    gfull = _problem_grads(g)

    grad = _reduce_sharded(gfull, c)
    grad.update(_unpack_repl(all_reduce_small("grad_all_reduce_small", _pack_repl({k: gfull[k] for k in SMALL_REPL}))))
    loss = lax.psum(loss, ("x", "y", "c"))

    delta, new_m, new_v = {}, {}, {}
    small = [k for k in WEIGHT_NAMES if k not in FFN_NAMES and k not in SMALL_SHARDED]
    as2d = lambda a: a.reshape(-1, a.shape[-1])
    for k in WEIGHT_NAMES:
        if k in small:
            continue
        d, nm, nv = adamw_big("adamw_" + k, as2d(p[k]), as2d(grad[k]), as2d(mom[k]), as2d(var[k]))
        delta[k], new_m[k], new_v[k] = d, nm, nv
    ds, nms, nvs = adamw_small("adamw_small", [as2d(p[k]) for k in small], [as2d(grad[k].reshape(p[k].shape)) for k in small],
                               [as2d(mom[k]) for k in small], [as2d(var[k]) for k in small])
    for k, d, nm, nv in zip(small, ds, nms, nvs):
        delta[k], new_m[k], new_v[k] = d, nm, nv

    shaped = lambda d, k: d.reshape(args[k].shape)
    return (loss, dx[None], *[shaped(grad[k], k) for k in WEIGHT_NAMES], *[shaped(delta[k], k) for k in WEIGHT_NAMES],
            *[shaped(new_m[k], k) for k in WEIGHT_NAMES], *[shaped(new_v[k], k) for k in WEIGHT_NAMES])
```

```python
import functools
import math

import jax
import jax.numpy as jnp
import numpy as np
from jax import lax
from jax.experimental import pallas as pl
from jax.experimental.pallas import tpu as pltpu

F32, BF16 = jnp.float32, jnp.bfloat16
EPS = 1e-6
MESH = pl.DeviceIdType.MESH

D_MODEL, D_FF = 1024, 2752
MLA_HEADS, MLA_Q_RANK, MLA_KV_RANK, MLA_NOPE, MLA_ROPE, MLA_V = 4, 384, 256, 128, 64, 128
MLA_QK = MLA_NOPE + MLA_ROPE
HEAD_PAD = 256
SSM_WIDTH, SSM_GROUP, SSM_GROUPS, SSM_STATE = 512, 16, 32, 64
SSM_LANES = SSM_GROUPS * SSM_STATE
XH, XHD = 4, 128
ROPE_THETA = 10000.0
ADAM_LR, ADAM_B1, ADAM_B2, ADAM_EPS, ADAM_WD, ADAM_STEP = 0.001, 0.9, 0.999, 1e-08, 0.01, 10
N_CHIPS, N_CORES, N_DEV = 4, 2, 8

VMEM_LIMIT_BYTES = 56 * 2**20
TOKEN_TILE = 512
ATTN_TILE = 512
SCAN_TIME_TILE = 1024
SCAN_LANE_TILE = 256
SUBLANES = 8


def _params(n_axes):
    return pltpu.CompilerParams(dimension_semantics=("arbitrary",) * n_axes, vmem_limit_bytes=VMEM_LIMIT_BYTES)


def _first(axes):
    cond = None
    for a in axes:
        c = pl.program_id(a) == 0
        cond = c if cond is None else jnp.logical_and(cond, c)
    return cond


def mm(name, xs, ws, out_dtypes, *, trans=False, adds=None, tm=TOKEN_TILE):
    rows = xs[0].shape[0]
    tm = min(tm, rows)
    n_in, n_out = len(xs), len(out_dtypes)
    pairs = [(i, j) for i in range(n_in) for j in range(n_out) if ws[i][j] is not None]
    w_list = [ws[i][j] for (i, j) in pairs]
    adds = list(adds) if adds is not None else [None] * n_out
    add_list = [a for a in adds if a is not None]
    out_cols = [None] * n_out
    for (i, j), w in zip(pairs, w_list):
        out_cols[j] = w.shape[0] if trans else w.shape[1]
    contract = (((1,), (1 if trans else 0,)), ((), ()))

    def body(*refs):
        x_refs = refs[:n_in]
        w_refs = refs[n_in:n_in + len(pairs)]
        a_refs = list(refs[n_in + len(pairs):n_in + len(pairs) + len(add_list)])
        o_refs = refs[n_in + len(pairs) + len(add_list):]
        xb = [None] * n_in
        for j in range(n_out):
            acc = None
            for p, (i, jj) in enumerate(pairs):
                if jj != j:
                    continue
                if xb[i] is None:
                    xb[i] = x_refs[i][...].astype(BF16)
                d = lax.dot_general(xb[i], w_refs[p][...].astype(BF16), contract, preferred_element_type=F32)
                acc = d if acc is None else acc + d
            if adds[j] is not None:
                acc = acc + a_refs.pop(0)[...].astype(F32)
            o_refs[j][...] = acc.astype(o_refs[j].dtype)

    in_specs = ([pl.BlockSpec((tm, x.shape[1]), lambda i: (i, 0)) for x in xs]
                + [pl.BlockSpec(w.shape, lambda i: (0, 0)) for w in w_list]
                + [pl.BlockSpec((tm, a.shape[1]), lambda i: (i, 0)) for a in add_list])
    outs = pl.pallas_call(
        body, name=name, grid=(rows // tm,), in_specs=in_specs,
        out_specs=[pl.BlockSpec((tm, n), lambda i: (i, 0)) for n in out_cols],
        out_shape=[jax.ShapeDtypeStruct((rows, n), dt) for n, dt in zip(out_cols, out_dtypes)],
        compiler_params=_params(1),
    )(*xs, *w_list, *add_list)
    return list(outs)


def mm_tn(name, xs, dys, pairs, *, tm=TOKEN_TILE):
    rows = xs[0].shape[0]
    tm = min(tm, rows)
    n_x, n_dy = len(xs), len(dys)
    contract = (((0,), (0,)), ((), ()))

    def body(*refs):
        x_refs, dy_refs, o_refs = refs[:n_x], refs[n_x:n_x + n_dy], refs[n_x + n_dy:]
        @pl.when(pl.program_id(0) == 0)
        def _():
            for o in o_refs:
                o[...] = jnp.zeros_like(o)

        for k, (i, js) in enumerate(pairs):
            dy = None
            for j in js:
                t = dy_refs[j][...].astype(F32)
                dy = t if dy is None else dy + t
            o_refs[k][...] += lax.dot_general(x_refs[i][...].astype(BF16), dy.astype(BF16), contract,
                                              preferred_element_type=F32)

    shapes = [(xs[i].shape[1], dys[js[0]].shape[1]) for (i, js) in pairs]
    outs = pl.pallas_call(
        body, name=name, grid=(rows // tm,),
        in_specs=[pl.BlockSpec((tm, a.shape[1]), lambda i: (i, 0)) for a in (*xs, *dys)],
        out_specs=[pl.BlockSpec(s, lambda i: (0, 0)) for s in shapes],
        out_shape=[jax.ShapeDtypeStruct(s, F32) for s in shapes],
        compiler_params=_params(1),
    )(*xs, *dys)
    return list(outs)


class In:
    def __init__(self, arr, block, imap, *, diff=False, acc=None, grad=None):
        self.arr, self.block, self.imap, self.diff, self.acc, self.grad = arr, block, imap, diff, acc, grad

    def spec(self):
        return pl.BlockSpec(self.block, self.imap)


class Out:
    def __init__(self, shape, dtype, block, imap):
        self.shape, self.dtype, self.block, self.imap = shape, dtype, block, imap

    def spec(self):
        return pl.BlockSpec(self.block, self.imap)


def seg_fwd(name, f, grid, ins, outs):
    n_in = len(ins)

    def body(*refs):
        res = f(*[r[...] for r in refs[:n_in]])
        for o_ref, r in zip(refs[n_in:], res):
            o_ref[...] = r.astype(o_ref.dtype)

    res = pl.pallas_call(
        body, name=name, grid=grid, in_specs=[i.spec() for i in ins], out_specs=[o.spec() for o in outs],
        out_shape=[jax.ShapeDtypeStruct(o.shape, o.dtype) for o in outs], compiler_params=_params(len(grid)),
    )(*[i.arr for i in ins])
    return list(res)


def seg_bwd(name, f, grid, ins, outs, cts, adds=None):
    n_in, n_ct = len(ins), len(cts)
    diff_idx = [k for k, i in enumerate(ins) if i.diff]
    adds = adds or {}
    add_keys = sorted(adds)
    add_list = [adds[k] for k in add_keys]

    def body(*refs):
        in_refs, ct_refs = refs[:n_in], refs[n_in:n_in + n_ct]
        add_refs = dict(zip(add_keys, refs[n_in + n_ct:n_in + n_ct + len(add_list)]))
        g_refs = refs[n_in + n_ct + len(add_list):]
        vals = [r[...] for r in in_refs]

        def g(*dv):
            full = list(vals)
            for k, v in zip(diff_idx, dv):
                full[k] = v
            return tuple(f(*full))

        _, pull = jax.vjp(g, *[vals[k].astype(F32) for k in diff_idx])
        grads = pull(tuple(c[...].astype(F32) for c in ct_refs))
        for n, (k, gr) in enumerate(zip(diff_idx, grads)):
            if k in add_refs:
                gr = gr + add_refs[k][...].astype(F32)
            if ins[k].acc is None:
                g_refs[n][...] = gr.astype(g_refs[n].dtype)
            else:
                @pl.when(_first(ins[k].acc))
                def _(n=n):
                    g_refs[n][...] = jnp.zeros_like(g_refs[n])

                g_refs[n][...] += gr

    g_specs, g_shapes = [], []
    for k in diff_idx:
        i = ins[k]
        shape, block, imap = i.grad if i.grad is not None else (i.arr.shape, i.block, i.imap)
        g_specs.append(pl.BlockSpec(block, imap))
        g_shapes.append(jax.ShapeDtypeStruct(shape, F32))
    in_specs = ([i.spec() for i in ins] + [o.spec() for o in outs]
                + [pl.BlockSpec(ins[k].block, ins[k].imap) for k in add_keys])
    res = pl.pallas_call(
        body, name=name, grid=grid, in_specs=in_specs, out_specs=g_specs, out_shape=g_shapes,
        compiler_params=_params(len(grid)),
    )(*[i.arr for i in ins], *cts, *add_list)
    return list(res)


def _rms(x, g):
    return x * lax.rsqrt(jnp.mean(x * x, axis=-1, keepdims=True) + EPS) * g


@jax.custom_vjp
def _bdot_nt(a, b):
    return lax.dot_general(a.astype(BF16), b.astype(BF16), (((1,), (1,)), ((), ())), preferred_element_type=F32)


def _bdot_nt_fwd(a, b):
    return _bdot_nt(a, b), (a, b)


def _bdot_nt_bwd(res, g):
    a, b = res
    gb = g.astype(BF16)
    da = lax.dot_general(gb, b.astype(BF16), (((1,), (0,)), ((), ())), preferred_element_type=F32)
    db = lax.dot_general(gb, a.astype(BF16), (((0,), (0,)), ((), ())), preferred_element_type=F32)
    return da, db


_bdot_nt.defvjp(_bdot_nt_fwd, _bdot_nt_bwd)


@jax.custom_vjp
def _bdot_nn(a, b):
    return lax.dot_general(a.astype(BF16), b.astype(BF16), (((1,), (0,)), ((), ())), preferred_element_type=F32)


def _bdot_nn_fwd(a, b):
    return _bdot_nn(a, b), (a, b)


def _bdot_nn_bwd(res, g):
    a, b = res
    gb = g.astype(BF16)
    da = lax.dot_general(gb, b.astype(BF16), (((1,), (1,)), ((), ())), preferred_element_type=F32)
    db = lax.dot_general(a.astype(BF16), gb, (((0,), (0,)), ((), ())), preferred_element_type=F32)
    return da, db


_bdot_nn.defvjp(_bdot_nn_fwd, _bdot_nn_bwd)


@functools.partial(jax.custom_vjp, nondiff_argnums=(1,))
def _lane_roll(x, shift):
    return pltpu.roll(x, shift, 1)


def _lane_roll_fwd(x, shift):
    return pltpu.roll(x, shift, 1), None


def _lane_roll_bwd(shift, _, g):
    return (pltpu.roll(g, (g.shape[1] - shift) % g.shape[1], 1),)


_lane_roll.defvjp(_lane_roll_fwd, _lane_roll_bwd)


def _hp_dot(a, b):
    return jnp.dot(a, b, precision=lax.Precision.HIGHEST, preferred_element_type=F32)


NT_DIMS = (((1,), (1,)), ((), ()))
TN_DIMS = (((0,), (0,)), ((), ()))


def ffn_fwd(name, x, gain, wg, wu, wd, tm=TOKEN_TILE):
    S, D = x.shape
    nsh, Fs, _ = wg.shape

    def body(x_ref, gain_ref, wg_ref, wu_ref, wd_ref, xo_ref, h_ref, g_ref, u_ref, acc):
        j = pl.program_id(1)

        @pl.when(j == 0)
        def _():
            h_ref[...] = _rms(x_ref[...], gain_ref[...]).astype(BF16)
            acc[...] = jnp.zeros_like(acc)

        h = h_ref[...]
        g = lax.dot_general(h, wg_ref[...], NT_DIMS, preferred_element_type=F32)
        u = lax.dot_general(h, wu_ref[...], NT_DIMS, preferred_element_type=F32)
        g_ref[...] = g.astype(BF16)
        u_ref[...] = u.astype(BF16)
        a = g * jax.nn.sigmoid(g) * u
        acc[...] += jnp.dot(a.astype(BF16), wd_ref[...], preferred_element_type=F32)

        @pl.when(j == nsh - 1)
        def _():
            xo_ref[...] = x_ref[...] + 0.5 * acc[...]

    return pl.pallas_call(
        body, name=name, grid=(S // tm, nsh),
        in_specs=[pl.BlockSpec((tm, D), lambda i, j: (i, 0)), pl.BlockSpec((1, D), lambda i, j: (0, 0)),
                  pl.BlockSpec((None, Fs, D), lambda i, j: (j, 0, 0)), pl.BlockSpec((None, Fs, D), lambda i, j: (j, 0, 0)),
                  pl.BlockSpec((None, Fs, D), lambda i, j: (j, 0, 0))],
        out_specs=[pl.BlockSpec((tm, D), lambda i, j: (i, 0)), pl.BlockSpec((tm, D), lambda i, j: (i, 0)),
                   pl.BlockSpec((None, tm, Fs), lambda i, j: (j, i, 0)), pl.BlockSpec((None, tm, Fs), lambda i, j: (j, i, 0))],
        out_shape=[jax.ShapeDtypeStruct((S, D), F32), jax.ShapeDtypeStruct((S, D), BF16),
                   jax.ShapeDtypeStruct((nsh, S, Fs), BF16), jax.ShapeDtypeStruct((nsh, S, Fs), BF16)],
        scratch_shapes=[pltpu.VMEM((tm, D), F32)], compiler_params=_params(2),
    )(x, gain, wg, wu, wd)


def ffn_bwd_act(name, dxo, x, gain, g, u, wg, wu, wd, tm=TOKEN_TILE):
    S, D = x.shape
    nsh, Fs, _ = wg.shape

    def body(dxo_ref, x_ref, gain_ref, g_ref, u_ref, wg_ref, wu_ref, wd_ref, dx_ref, dgain_ref, dg_ref, du_ref, dd, dh):
        i, j = pl.program_id(0), pl.program_id(1)

        @pl.when(j == 0)
        def _():
            dd[...] = (0.5 * dxo_ref[...]).astype(BF16)
            dh[...] = jnp.zeros_like(dh)

        da = lax.dot_general(dd[...], wd_ref[...], NT_DIMS, preferred_element_type=F32)
        gf, uf = g_ref[...].astype(F32), u_ref[...].astype(F32)
        sig = jax.nn.sigmoid(gf)
        dgv = (da * uf * (sig * (1.0 + gf * (1.0 - sig)))).astype(BF16)
        duv = (da * (gf * sig)).astype(BF16)
        dg_ref[...] = dgv
        du_ref[...] = duv
        dh[...] += (jnp.dot(dgv, wg_ref[...], preferred_element_type=F32)
                    + jnp.dot(duv, wu_ref[...], preferred_element_type=F32))

        @pl.when(j == nsh - 1)
        def _():
            xv = x_ref[...]
            r = lax.rsqrt(jnp.mean(xv * xv, axis=-1, keepdims=True) + EPS)
            xhat = xv * r
            dhv = dh[...]
            dxn = dhv * gain_ref[...]
            dx_ref[...] = dxo_ref[...] + r * (dxn - xhat * jnp.mean(dxn * xhat, axis=-1, keepdims=True))
            part = jnp.sum(dhv * xhat, axis=0, keepdims=True)

            @pl.when(i == 0)
            def _():
                dgain_ref[...] = part

            @pl.when(i != 0)
            def _():
                dgain_ref[...] += part

    return pl.pallas_call(
        body, name=name, grid=(S // tm, nsh),
        in_specs=[pl.BlockSpec((tm, D), lambda i, j: (i, 0)), pl.BlockSpec((tm, D), lambda i, j: (i, 0)),
                  pl.BlockSpec((1, D), lambda i, j: (0, 0)),
                  pl.BlockSpec((None, tm, Fs), lambda i, j: (j, i, 0)), pl.BlockSpec((None, tm, Fs), lambda i, j: (j, i, 0)),
                  pl.BlockSpec((None, Fs, D), lambda i, j: (j, 0, 0)), pl.BlockSpec((None, Fs, D), lambda i, j: (j, 0, 0)),
                  pl.BlockSpec((None, Fs, D), lambda i, j: (j, 0, 0))],
        out_specs=[pl.BlockSpec((tm, D), lambda i, j: (i, 0)), pl.BlockSpec((1, D), lambda i, j: (0, 0)),
                   pl.BlockSpec((None, tm, Fs), lambda i, j: (j, i, 0)), pl.BlockSpec((None, tm, Fs), lambda i, j: (j, i, 0))],
        out_shape=[jax.ShapeDtypeStruct((S, D), F32), jax.ShapeDtypeStruct((1, D), F32),
                   jax.ShapeDtypeStruct((nsh, S, Fs), BF16), jax.ShapeDtypeStruct((nsh, S, Fs), BF16)],
        scratch_shapes=[pltpu.VMEM((tm, D), BF16), pltpu.VMEM((tm, D), F32)], compiler_params=_params(2),
    )(dxo, x, gain, g, u, wg, wu, wd)


def ffn_bwd_w(name, h, dxo, g, u, dg, du, tm=TOKEN_TILE):
    S, D = h.shape
    nsh, _, Fs = g.shape

    def body(h_ref, dxo_ref, g_ref, u_ref, dg_ref, du_ref, dwg_ref, dwu_ref, dwd_ref):
        i = pl.program_id(1)
        gf, uf = g_ref[...].astype(F32), u_ref[...].astype(F32)
        a = (gf * jax.nn.sigmoid(gf) * uf).astype(BF16)
        dd = (0.5 * dxo_ref[...]).astype(BF16)
        hv = h_ref[...]

        @pl.when(i == 0)
        def _():
            dwg_ref[...] = jnp.zeros_like(dwg_ref)
            dwu_ref[...] = jnp.zeros_like(dwu_ref)
            dwd_ref[...] = jnp.zeros_like(dwd_ref)

        dwg_ref[...] += lax.dot_general(dg_ref[...], hv, TN_DIMS, preferred_element_type=F32)
        dwu_ref[...] += lax.dot_general(du_ref[...], hv, TN_DIMS, preferred_element_type=F32)
        dwd_ref[...] += lax.dot_general(a, dd, TN_DIMS, preferred_element_type=F32)

    act = pl.BlockSpec((None, tm, Fs), lambda j, i: (j, i, 0))
    wspec = pl.BlockSpec((None, Fs, D), lambda j, i: (j, 0, 0))
    return pl.pallas_call(
        body, name=name, grid=(nsh, S // tm),
        in_specs=[pl.BlockSpec((tm, D), lambda j, i: (i, 0)), pl.BlockSpec((tm, D), lambda j, i: (i, 0)), act, act, act, act],
        out_specs=[wspec, wspec, wspec], out_shape=[jax.ShapeDtypeStruct((nsh, Fs, D), F32)] * 3,
        compiler_params=_params(2),
    )(h, dxo, g, u, dg, du)


NEG_BIG = -1e30


def _causal_pairs(n, by_key):
    pairs = [(qi, ki) for qi in range(n) for ki in range(qi + 1)]
    if by_key:
        pairs.sort(key=lambda p: (p[1], p[0]))
    return jnp.asarray([p[0] for p in pairs], jnp.int32), jnp.asarray([p[1] for p in pairs], jnp.int32)


def _scores(q, k, masked):
    s = lax.dot_general(q, k, (((1,), (1,)), ((), ())), preferred_element_type=F32)
    if masked:
        row = lax.broadcasted_iota(jnp.int32, s.shape, 0)
        col = lax.broadcasted_iota(jnp.int32, s.shape, 1)
        s = jnp.where(row >= col, s, NEG_BIG)
    return s


def attn_fwd(name, q, k, v, t=ATTN_TILE):
    H, S, Dk = q.shape
    Dv = v.shape[1] // H
    qt, kt = _causal_pairs(S // t, by_key=False)

    def body(qt_ref, kt_ref, q_ref, k_ref, v_ref, o_ref, lse_ref, m_sc, l_sc, acc):
        qi, ki = qt_ref[pl.program_id(1)], kt_ref[pl.program_id(1)]

        @pl.when(ki == 0)
        def _():
            m_sc[...] = jnp.full_like(m_sc, NEG_BIG)
            l_sc[...] = jnp.zeros_like(l_sc)
            acc[...] = jnp.zeros_like(acc)

        def step(masked):
            s = _scores(q_ref[...], k_ref[...], masked)
            m_new = jnp.maximum(m_sc[...], jnp.max(s, axis=-1, keepdims=True))
            alpha = jnp.exp(m_sc[...] - m_new)
            p = jnp.exp(s - m_new)
            l_sc[...] = alpha * l_sc[...] + jnp.sum(p, axis=-1, keepdims=True)
            acc[...] = alpha * acc[...] + jnp.dot(p.astype(BF16), v_ref[...].astype(BF16), preferred_element_type=F32)
            m_sc[...] = m_new

        @pl.when(ki < qi)
        def _():
            step(False)

        @pl.when(ki == qi)
        def _():
            step(True)
            o_ref[...] = acc[...] / l_sc[...]
            lse_ref[...] = m_sc[...] + jnp.log(l_sc[...])

    return pl.pallas_call(
        body, name=name,
        grid_spec=pltpu.PrefetchScalarGridSpec(
            num_scalar_prefetch=2, grid=(H, qt.shape[0]),
            in_specs=[pl.BlockSpec((None, t, Dk), lambda h, s, qt, kt: (h, qt[s], 0)),
                      pl.BlockSpec((None, t, Dk), lambda h, s, qt, kt: (h, kt[s], 0)),
                      pl.BlockSpec((t, Dv), lambda h, s, qt, kt: (kt[s], h))],
            out_specs=[pl.BlockSpec((t, Dv), lambda h, s, qt, kt: (qt[s], h)),
                       pl.BlockSpec((None, t, 1), lambda h, s, qt, kt: (h, qt[s], 0))],
            scratch_shapes=[pltpu.VMEM((t, 1), F32), pltpu.VMEM((t, 1), F32), pltpu.VMEM((t, Dv), F32)]),
        out_shape=[jax.ShapeDtypeStruct((S, H * Dv), F32), jax.ShapeDtypeStruct((H, S, 1), F32)],
        compiler_params=_params(2),
    )(qt, kt, q, k, v)


def attn_bwd(name, q, k, v, do, o, lse, t=ATTN_TILE):
    H, S, Dk = q.shape
    Dv = v.shape[1] // H
    qt, kt = _causal_pairs(S // t, by_key=True)
    tn_dims = (((0,), (0,)), ((), ()))

    def body(qt_ref, kt_ref, q_ref, k_ref, v_ref, do_ref, o_ref, lse_ref, dq_ref, dk_ref, dv_ref):
        step_id = pl.program_id(1)
        qi, ki = qt_ref[step_id], kt_ref[step_id]

        @pl.when(step_id == 0)
        def _():
            dq_ref[...] = jnp.zeros_like(dq_ref)

        def step(masked):
            s = _scores(q_ref[...], k_ref[...], masked)
            p = jnp.exp(s - lse_ref[...])
            dov = do_ref[...]
            delta = jnp.sum(dov * o_ref[...], axis=-1, keepdims=True)
            dob = dov.astype(BF16)
            dp = lax.dot_general(dob, v_ref[...].astype(BF16), (((1,), (1,)), ((), ())), preferred_element_type=F32)
            ds = (p * (dp - delta)).astype(BF16)
            pdv = lax.dot_general(p.astype(BF16), dob, tn_dims, preferred_element_type=F32)
            pdk = lax.dot_general(ds, q_ref[...], tn_dims, preferred_element_type=F32)
            rows = pl.ds(pl.multiple_of(qi * t, t), t)
            dq_ref[rows, :] += jnp.dot(ds, k_ref[...], preferred_element_type=F32)
            return pdk, pdv

        @pl.when(ki == qi)
        def _():
            dk_ref[...] = jnp.zeros_like(dk_ref)
            dv_ref[...] = jnp.zeros_like(dv_ref)

        def accumulate(masked):
            pdk, pdv = step(masked)
            dk_ref[...] += pdk
            dv_ref[...] += pdv

        @pl.when(ki == qi)
        def _():
            accumulate(True)

        @pl.when(ki < qi)
        def _():
            accumulate(False)

    qmap = lambda h, s, qt, kt: (h, qt[s], 0)
    kmap = lambda h, s, qt, kt: (h, kt[s], 0)
    qrow = lambda h, s, qt, kt: (qt[s], h)
    krow = lambda h, s, qt, kt: (kt[s], h)
    return pl.pallas_call(
        body, name=name,
        grid_spec=pltpu.PrefetchScalarGridSpec(
            num_scalar_prefetch=2, grid=(H, qt.shape[0]),
            in_specs=[pl.BlockSpec((None, t, Dk), qmap), pl.BlockSpec((None, t, Dk), kmap), pl.BlockSpec((t, Dv), krow),
                      pl.BlockSpec((t, Dv), qrow), pl.BlockSpec((t, Dv), qrow), pl.BlockSpec((None, t, 1), qmap)],
            out_specs=[pl.BlockSpec((None, S, Dk), lambda h, s, qt, kt: (h, 0, 0)), pl.BlockSpec((None, t, Dk), kmap),
                       pl.BlockSpec((t, Dv), krow)]),
        out_shape=[jax.ShapeDtypeStruct((H, S, Dk), F32), jax.ShapeDtypeStruct((H, S, Dk), F32),
                   jax.ShapeDtypeStruct((S, H * Dv), F32)],
        compiler_params=_params(2),
    )(qt, kt, q, k, v, do, o, lse)


def _cmul(ar, ai, br, bi):
    return ar * br - ai * bi, ar * bi + ai * br


def ssm_scan(name, br, bi, a_r8, a_i8, *, reverse=False, states=None):
    S, L = br.shape
    tc, lb = min(SCAN_TIME_TILE, S), SCAN_LANE_TILE
    nt, nblk = S // tc, tc // SUBLANES
    with_da = states is not None

    def body(*refs):
        if with_da:
            br_ref, bi_ref, ar_ref, ai_ref, xr_ref, xi_ref, pr_ref, pi_ref, or_ref, oi_ref, dar_ref, dai_ref, cr_sc, ci_sc = refs
        else:
            br_ref, bi_ref, ar_ref, ai_ref, or_ref, oi_ref, cr_sc, ci_sc = refs
        t = pl.program_id(1)
        chunk = (nt - 1 - t) if reverse else t

        @pl.when(t == 0)
        def _():
            cr_sc[...] = jnp.zeros_like(cr_sc)
            ci_sc[...] = jnp.zeros_like(ci_sc)
            if with_da:
                dar_ref[...] = jnp.zeros_like(dar_ref)
                dai_ref[...] = jnp.zeros_like(dai_ref)

        a1r = ar_ref[0:1, :]
        a1i = -ai_ref[0:1, :] if reverse else ai_ref[0:1, :]
        a2r, a2i = _cmul(a1r, a1i, a1r, a1i)
        a4r, a4i = _cmul(a2r, a2i, a2r, a2i)
        row = lax.broadcasted_iota(jnp.int32, (SUBLANES, lb), 0)
        pw_r, pw_i = jnp.zeros((SUBLANES, lb), F32), jnp.zeros((SUBLANES, lb), F32)
        cur_r, cur_i = a1r, a1i
        for e in range(SUBLANES):
            r_at = (SUBLANES - 1 - e) if reverse else e
            pw_r = jnp.where(row == r_at, cur_r, pw_r)
            pw_i = jnp.where(row == r_at, cur_i, pw_i)
            cur_r, cur_i = _cmul(cur_r, cur_i, a1r, a1i)
        steps = ((1, a1r, a1i), (2, a2r, a2i), (4, a4r, a4i))

        def block(jb, carry):
            if with_da:
                cr, ci, acc_r, acc_i = carry
            else:
                cr, ci = carry
            idx = (nblk - 1 - jb) if reverse else jb
            r0 = pl.multiple_of(idx * SUBLANES, SUBLANES)
            xr = br_ref[pl.ds(r0, SUBLANES), :]
            xi = bi_ref[pl.ds(r0, SUBLANES), :]
            for d, pr, pi in steps:
                if reverse:
                    keep = row < SUBLANES - d
                    sr, si = pltpu.roll(xr, SUBLANES - d, 0), pltpu.roll(xi, SUBLANES - d, 0)
                else:
                    keep = row >= d
                    sr, si = pltpu.roll(xr, d, 0), pltpu.roll(xi, d, 0)
                sr, si = jnp.where(keep, sr, 0.0), jnp.where(keep, si, 0.0)
                xr, xi = xr + pr * sr - pi * si, xi + pr * si + pi * sr
            xr, xi = xr + pw_r * cr - pw_i * ci, xi + pw_r * ci + pw_i * cr
            or_ref[pl.ds(r0, SUBLANES), :] = xr
            oi_ref[pl.ds(r0, SUBLANES), :] = xi
            edge = 0 if reverse else SUBLANES - 1
            cr, ci = xr[edge:edge + 1, :], xi[edge:edge + 1, :]
            if not with_da:
                return cr, ci
            fr = xr_ref[pl.ds(r0, SUBLANES), :]
            fi = xi_ref[pl.ds(r0, SUBLANES), :]
            rp = pl.multiple_of(jnp.maximum(idx - 1, 0) * SUBLANES, SUBLANES)
            inside = idx > 0
            before_r = jnp.where(inside, xr_ref[pl.ds(rp, SUBLANES), :], pr_ref[...])
            before_i = jnp.where(inside, xi_ref[pl.ds(rp, SUBLANES), :], pi_ref[...])
            live = jnp.where(jnp.logical_or(inside, chunk > 0), 1.0, 0.0)
            last_r = before_r[SUBLANES - 1:SUBLANES, :] * live
            last_i = before_i[SUBLANES - 1:SUBLANES, :] * live
            pvr = jnp.where(row == 0, last_r, pltpu.roll(fr, 1, 0))
            pvi = jnp.where(row == 0, last_i, pltpu.roll(fi, 1, 0))
            acc_r = acc_r + xr * pvr + xi * pvi
            acc_i = acc_i + xi * pvr - xr * pvi
            return cr, ci, acc_r, acc_i

        init = (cr_sc[...], ci_sc[...])
        if with_da:
            init = init + (jnp.zeros((SUBLANES, lb), F32), jnp.zeros((SUBLANES, lb), F32))
        fin = lax.fori_loop(0, nblk, block, init)
        cr_sc[...] = fin[0]
        ci_sc[...] = fin[1]
        if with_da:
            dar_ref[...] += fin[2]
            dai_ref[...] += fin[3]

    tmap = (lambda l, t: (nt - 1 - t, l)) if reverse else (lambda l, t: (t, l))
    seq = pl.BlockSpec((tc, lb), tmap)
    par = pl.BlockSpec((SUBLANES, lb), lambda l, t: (0, l))
    in_specs, args = [seq, seq, par, par], [br, bi, a_r8, a_i8]
    out_specs = [seq, seq]
    out_shape = [jax.ShapeDtypeStruct((S, L), F32), jax.ShapeDtypeStruct((S, L), F32)]
    if with_da:
        pmap = lambda l, t: (jnp.maximum((nt - 1 - t) * (tc // SUBLANES) - 1, 0), l)
        prev = pl.BlockSpec((SUBLANES, lb), pmap)
        in_specs += [seq, seq, prev, prev]
        args += [states[0], states[1], states[0], states[1]]
        out_specs += [par, par]
        out_shape += [jax.ShapeDtypeStruct((SUBLANES, L), F32), jax.ShapeDtypeStruct((SUBLANES, L), F32)]
    return pl.pallas_call(
        body, name=name, grid=(L // lb, nt), in_specs=in_specs, out_specs=out_specs, out_shape=out_shape,
        scratch_shapes=[pltpu.VMEM((1, lb), F32), pltpu.VMEM((1, lb), F32)], compiler_params=_params(2),
    )(*args)


PREP_LANES = 512


def _ssm_prep_f(a_re, a_im, log_dt, bt_re, bt_im, c_re, c_im):
    off = pl.program_id(0) * PREP_LANES
    iota = lambda shape, d: lax.broadcasted_iota(jnp.int32, shape, d)
    grp_of_row = lambda shape: iota(shape, 0) >> int(math.log2(SSM_GROUP))
    grp_of_lane = lambda shape: (iota(shape, 1) + off) >> int(math.log2(SSM_STATE))
    rep = (grp_of_row((SSM_WIDTH, SSM_GROUPS)) == iota((SSM_WIDTH, SSM_GROUPS), 1)).astype(F32)
    til = ((iota((SSM_STATE, PREP_LANES), 1) & (SSM_STATE - 1)) == iota((SSM_STATE, PREP_LANES), 0)).astype(F32)
    m_rows = (grp_of_row((SSM_WIDTH, PREP_LANES)) == grp_of_lane((SSM_WIDTH, PREP_LANES))).astype(F32)
    m_grp = (iota((SSM_GROUPS, PREP_LANES), 0) == grp_of_lane((SSM_GROUPS, PREP_LANES))).astype(F32)
    dt = jnp.exp(log_dt)
    decay = jnp.exp(a_re * dt)
    ar = decay * jnp.cos(a_im * dt)
    ai = decay * jnp.sin(a_im * dt)
    den = a_re * a_re + a_im * a_im
    nr = ar - 1.0
    coef_r = (nr * a_re + ai * a_im) / den
    coef_i = (ai * a_re - nr * a_im) / den
    cr, ci = _hp_dot(rep, coef_r), _hp_dot(rep, coef_i)
    bb_r = cr * bt_re - ci * bt_im
    bb_i = cr * bt_im + ci * bt_re
    big = lambda m: _hp_dot(m, til) * m_rows
    lanes = lambda m: jnp.broadcast_to(jnp.sum(_hp_dot(m, til) * m_grp, axis=0, keepdims=True), (SUBLANES, PREP_LANES))
    return lanes(ar), lanes(ai), big(bb_r), big(bb_i), big(c_re), -big(c_im)


def _whole(arr, **kw):
    nd = arr.ndim
    return In(arr, arr.shape, lambda *_: (0,) * nd, **kw)


def loss_grad(name, y, target, tm=TOKEN_TILE):
    S, D = y.shape

    def body(y_ref, t_ref, dy_ref, part_ref):
        err = y_ref[...] - t_ref[...]
        dy_ref[...] = err * (1.0 / D)
        part_ref[...] = jnp.full(part_ref.shape, 0.5 * jnp.sum(jnp.mean(err * err, axis=-1)), F32)

    return pl.pallas_call(
        body, name=name, grid=(S // tm,),
        in_specs=[pl.BlockSpec((tm, D), lambda i: (i, 0)), pl.BlockSpec((tm, D), lambda i: (i, 0))],
        out_specs=[pl.BlockSpec((tm, D), lambda i: (i, 0)), pl.BlockSpec((SUBLANES, 128), lambda i: (i, 0))],
        out_shape=[jax.ShapeDtypeStruct((S, D), F32), jax.ShapeDtypeStruct((S // tm * SUBLANES, 128), F32)],
        compiler_params=_params(1),
    )(y, target)


def _adamw_math(w, g, m, v):
    m = ADAM_B1 * m + (1.0 - ADAM_B1) * g
    v = ADAM_B2 * v + (1.0 - ADAM_B2) * (g * g)
    m_hat = m / (1.0 - ADAM_B1 ** ADAM_STEP)
    v_hat = v / (1.0 - ADAM_B2 ** ADAM_STEP)
    delta = -ADAM_LR * (m_hat / (jnp.sqrt(v_hat) + ADAM_EPS) + ADAM_WD * w)
    return delta, m, v


def adamw_big(name, w, g, m, v):
    R, C = w.shape
    tr = R
    for cand in (512, 344, 256, 128):
        if R % cand == 0:
            tr = cand
            break

    def body(w_ref, g_ref, m_ref, v_ref, d_ref, nm_ref, nv_ref):
        d, nm, nv = _adamw_math(w_ref[...], g_ref[...], m_ref[...], v_ref[...])
        d_ref[...] = d
        nm_ref[...] = nm
        nv_ref[...] = nv

    spec = pl.BlockSpec((tr, C), lambda i: (i, 0))
    return pl.pallas_call(
        body, name=name, grid=(R // tr,), in_specs=[spec] * 4, out_specs=[spec] * 3,
        out_shape=[jax.ShapeDtypeStruct((R, C), F32)] * 3, compiler_params=_params(1),
    )(w, g, m, v)


def adamw_small(name, ws, gs, ms, vs):
    n = len(ws)

    def body(*refs):
        for k in range(n):
            d, nm, nv = _adamw_math(refs[k][...], refs[n + k][...], refs[2 * n + k][...], refs[3 * n + k][...])
            refs[4 * n + k][...] = d
            refs[5 * n + k][...] = nm
            refs[6 * n + k][...] = nv

    vm = pl.BlockSpec(memory_space=pltpu.VMEM)
    shapes = [jax.ShapeDtypeStruct(w.shape, F32) for w in ws]
    res = pl.pallas_call(
        body, name=name, in_specs=[vm] * (4 * n), out_specs=[vm] * (3 * n), out_shape=shapes * 3,
        compiler_params=pltpu.CompilerParams(vmem_limit_bytes=VMEM_LIMIT_BYTES),
    )(*ws, *gs, *ms, *vs)
    return res[:n], res[n:2 * n], res[2 * n:]


def _place():
    return lax.axis_index("x"), lax.axis_index("y"), lax.axis_index("c")


def _other_chips(x, y):
    return [(1 - x, y), (x, 1 - y), (1 - x, 1 - y)]


HBM = pl.BlockSpec(memory_space=pl.ANY)


def all_gather_halves(name, blocks):
    n = len(blocks)

    def body(*refs):
        in_refs, out_refs = refs[:n], refs[n:2 * n]
        send_sems, recv_sems = refs[2 * n:]
        x, y, c = _place()
        me, sibling = (x, y, c), (x, y, 1 - c)
        chips = _other_chips(x, y)

        def slot(a, px, py, pc):
            return out_refs[a].at[4 * px + 2 * py + pc]

        def copy(a, k, block, to, src=None):
            return pltpu.make_async_remote_copy(
                src_ref=slot(a, *block) if src is None else src, dst_ref=slot(a, *block),
                send_sem=send_sems.at[a, k], recv_sem=recv_sems.at[a, k], device_id=to, device_id_type=MESH)

        first = []
        for a in range(n):
            first.append(copy(a, 0, me, sibling, src=in_refs[a]))
            first += [copy(a, 1 + j, me, (*chip, c), src=in_refs[a]) for j, chip in enumerate(chips)]
        for cp in first:
            cp.start()
        passed = []
        for j, chip in enumerate(chips):
            for a in range(n):
                copy(a, 1 + j, (*chip, c), me).wait_recv()
                fw = copy(a, 4 + j, (*chip, c), sibling)
                fw.start()
                passed.append(fw)
        for a in range(n):
            copy(a, 0, sibling, me).wait_recv()
            for j, chip in enumerate(chips):
                copy(a, 4 + j, (*chip, 1 - c), me).wait_recv()
        for cp in first + passed:
            cp.wait_send()

    res = pl.pallas_call(
        body, name=name, in_specs=[HBM] * n, out_specs=[HBM] * n,
        out_shape=[jax.ShapeDtypeStruct((N_DEV,) + b.shape, b.dtype) for b in blocks],
        scratch_shapes=[pltpu.SemaphoreType.DMA((n, 7)), pltpu.SemaphoreType.DMA((n, 7))],
    )(*blocks)
    return list(res)


def pair_send_halves(name, grads):
    n = len(grads)

    def body(*refs):
        in_refs, out_refs = refs[:n], refs[n:2 * n]
        send_sems, recv_sems = refs[2 * n:]
        x, y, c = _place()
        cps = []
        for a in range(n):
            cp = pltpu.make_async_remote_copy(
                src_ref=in_refs[a].at[:, 1 - c], dst_ref=out_refs[a], send_sem=send_sems.at[a], recv_sem=recv_sems.at[a],
                device_id=(x, y, 1 - c), device_id_type=MESH)
            cp.start()
            cps.append(cp)
        for cp in cps:
            cp.wait()

    res = pl.pallas_call(
        body, name=name, in_specs=[HBM] * n, out_specs=[HBM] * n,
        out_shape=[jax.ShapeDtypeStruct((g.shape[0],) + g.shape[2:], g.dtype) for g in grads],
        scratch_shapes=[pltpu.SemaphoreType.DMA((n,)), pltpu.SemaphoreType.DMA((n,))],
    )(*grads)
    return list(res)


def pair_add(name, grad, got, c_arr):
    nsh, _, M, N = grad.shape
    tr = M
    for cand in (512, 256, 192, 128, 64, 16):
        if M % cand == 0:
            tr = cand
            break

    def body(c_ref, g_ref, p_ref, o_ref):
        o_ref[...] = (g_ref[...] + p_ref[...]).astype(BF16)

    return pl.pallas_call(
        body, name=name,
        grid_spec=pltpu.PrefetchScalarGridSpec(
            num_scalar_prefetch=1, grid=(nsh, M // tr),
            in_specs=[pl.BlockSpec((None, None, tr, N), lambda j, i, c_ref: (j, c_ref[0], i, 0)),
                      pl.BlockSpec((None, tr, N), lambda j, i, c_ref: (j, i, 0))],
            out_specs=pl.BlockSpec((None, tr, N), lambda j, i, c_ref: (j, i, 0))),
        out_shape=jax.ShapeDtypeStruct((nsh, M, N), BF16), compiler_params=_params(2),
    )(c_arr, grad, got)


def scatter_to_chips(name, parts):
    n = len(parts)

    def body(*refs):
        in_refs, out_refs = refs[:n], refs[n:2 * n]
        send_sems, recv_sems = refs[2 * n:]
        x, y, c = _place()
        mine = 2 * x + y
        chips = _other_chips(x, y)
        cps = []
        for a in range(n):
            for k, (px, py) in enumerate(chips):
                cp = pltpu.make_async_remote_copy(
                    src_ref=in_refs[a].at[2 * px + py], dst_ref=out_refs[a].at[mine],
                    send_sem=send_sems.at[a, k], recv_sem=recv_sems.at[a, k], device_id=(px, py, c), device_id_type=MESH)
                cp.start()
                cps.append((cp, a, k, px, py))
        for cp, a, k, px, py in cps:
            pltpu.make_async_remote_copy(
                src_ref=in_refs[a].at[mine], dst_ref=out_refs[a].at[2 * px + py],
                send_sem=send_sems.at[a, k], recv_sem=recv_sems.at[a, k], device_id=(px, py, c), device_id_type=MESH).wait_recv()
        for cp, *_ in cps:
            cp.wait_send()

    res = pl.pallas_call(
        body, name=name, in_specs=[HBM] * n, out_specs=[HBM] * n,
        out_shape=[jax.ShapeDtypeStruct(p.shape, p.dtype) for p in parts],
        scratch_shapes=[pltpu.SemaphoreType.DMA((n, 3)), pltpu.SemaphoreType.DMA((n, 3))],
    )(*parts)
    return list(res)


def sum_chips(name, q):
    nsh, M, N = q.shape
    tr = M
    for cand in (512, 256, 192, 128, 64, 16):
        if M % cand == 0:
            tr = cand
            break

    def body(q_ref, o_ref):
        acc = q_ref[0].astype(F32)
        for j in range(1, nsh):
            acc = acc + q_ref[j].astype(F32)
        o_ref[...] = acc

    return pl.pallas_call(
        body, name=name, grid=(M // tr,), in_specs=[pl.BlockSpec((nsh, tr, N), lambda i: (0, i, 0))],
        out_specs=pl.BlockSpec((tr, N), lambda i: (i, 0)), out_shape=jax.ShapeDtypeStruct((M, N), F32),
        compiler_params=_params(1),
    )(q)


def pair_exchange(name, halves):
    n = len(halves)

    def body(*refs):
        in_refs, out_refs = refs[:n], refs[n:2 * n]
        send_sems, recv_sems = refs[2 * n:]
        x, y, c = _place()
        cps = []
        for a in range(n):
            cp = pltpu.make_async_remote_copy(
                src_ref=in_refs[a], dst_ref=out_refs[a], send_sem=send_sems.at[a], recv_sem=recv_sems.at[a],
                device_id=(x, y, 1 - c), device_id_type=MESH)
            cp.start()
            cps.append(cp)
        for cp in cps:
            cp.wait()

    res = pl.pallas_call(
        body, name=name, in_specs=[HBM] * n, out_specs=[HBM] * n,
        out_shape=[jax.ShapeDtypeStruct(h.shape, h.dtype) for h in halves],
        scratch_shapes=[pltpu.SemaphoreType.DMA((n,)), pltpu.SemaphoreType.DMA((n,))],
    )(*halves)
    return list(res)


SEM = pl.BlockSpec(memory_space=pltpu.SEMAPHORE)
IN_HBM = pl.BlockSpec(memory_space=pltpu.HBM)
SPLIT_COPY = pltpu.CompilerParams(has_side_effects=pltpu.SideEffectType.DATAFLOW_SIDE_EFFECTING)


def _scatter_copies(src_refs, dst_refs, send_sems, recv_sems):
    x, y, c = _place()
    mine = 2 * x + y
    return [pltpu.make_async_remote_copy(
        src_ref=src_refs[a].at[2 * px + py], dst_ref=dst_refs[a].at[mine], send_sem=send_sems.at[a * (N_CHIPS - 1) + k],
        recv_sem=recv_sems.at[a * (N_CHIPS - 1) + k], device_id=(px, py, c), device_id_type=MESH)
        for a in range(len(src_refs)) for k, (px, py) in enumerate(_other_chips(x, y))]


def _gather_copies(src_refs, dst_refs, send_sems, recv_sems):
    x, y, c = _place()
    me = 4 * x + 2 * y + c
    cps = []
    for a in range(len(src_refs)):
        for k in range(1, N_DEV):
            to = (1 - x if k & 4 else x, 1 - y if k & 2 else y, 1 - c if k & 1 else c)
            s = a * (N_DEV - 1) + k - 1
            cps.append(pltpu.make_async_remote_copy(
                src_ref=src_refs[a], dst_ref=dst_refs[a].at[me], send_sem=send_sems.at[s], recv_sem=recv_sems.at[s],
                device_id=to, device_id_type=MESH))
    return cps


def split_copy_start(name, copies, n_sem, srcs, land_shapes):
    n = len(srcs)
    lands = [lax.empty(s.shape, s.dtype) for s in land_shapes]

    def body(*refs):
        for cp in copies(refs[:n], refs[n:2 * n], refs[2 * n], refs[2 * n + 1]):
            cp.start()
        refs[-1][...] = jnp.zeros_like(refs[-1])

    thru = [pltpu.HBM(a.shape, a.dtype) for a in (*srcs, *lands)]
    res = pl.pallas_call(
        body, name=name, in_specs=[IN_HBM] * (2 * n),
        out_specs=(SEM, SEM, *[IN_HBM] * (2 * n), pl.BlockSpec(memory_space=pltpu.VMEM)),
        out_shape=(pltpu.SemaphoreType.DMA((n * n_sem,)), pltpu.SemaphoreType.DMA((n * n_sem,)), *thru,
                   jax.ShapeDtypeStruct((SUBLANES, LANES), F32)),
        input_output_aliases={i: 2 + i for i in range(2 * n)}, compiler_params=SPLIT_COPY,
    )(*[pltpu.with_memory_space_constraint(a, pltpu.HBM) for a in (*srcs, *lands)])
    return (copies, n, res[0], res[1], res[2:2 + 2 * n]), res[-1][0, 0]


def split_copy_wait(name, handle, after):
    copies, n, send_sems, recv_sems, thru = handle

    def body(*refs):
        for cp in copies(refs[:n], refs[n:2 * n], refs[2 * n], refs[2 * n + 1]):
            cp.wait_send()
            cp.wait_recv()

    res = pl.pallas_call(
        body, name=name, in_specs=[IN_HBM] * (2 * n) + [SEM, SEM, pl.BlockSpec(memory_space=pl.ANY)],
        out_specs=[IN_HBM] * (2 * n), out_shape=[pltpu.HBM(a.shape, a.dtype) for a in thru],
        input_output_aliases={i: i for i in range(2 * n)}, compiler_params=SPLIT_COPY,
    )(*thru, send_sems, recv_sems, after)
    return list(res[n:])


def all_reduce_small(name, v):
    R, C = v.shape

    def body(v_ref, o_ref, gath, send_sems, recv_sems):
        x, y, c = _place()
        me, sibling = (x, y, c), (x, y, 1 - c)
        chips = _other_chips(x, y)

        def slot(px, py, pc):
            return gath.at[4 * px + 2 * py + pc]

        def copy(k, block, to, src=None):
            return pltpu.make_async_remote_copy(
                src_ref=slot(*block) if src is None else src, dst_ref=slot(*block),
                send_sem=send_sems.at[k], recv_sem=recv_sems.at[k], device_id=to, device_id_type=MESH)

        first = [copy(0, me, sibling, src=v_ref)]
        first += [copy(1 + j, me, (*chip, c), src=v_ref) for j, chip in enumerate(chips)]
        for cp in first:
            cp.start()
        slot(*me)[...] = v_ref[...]
        passed = [copy(4 + j, (*chip, c), sibling) for j, chip in enumerate(chips)]
        for j, chip in enumerate(chips):
            copy(1 + j, (*chip, c), me).wait_recv()
            passed[j].start()
        copy(0, sibling, me).wait_recv()
        for j, chip in enumerate(chips):
            copy(4 + j, (*chip, 1 - c), me).wait_recv()
        for cp in first + passed:
            cp.wait_send()
        acc = gath[0]
        for d in range(1, N_DEV):
            acc = acc + gath[d]
        o_ref[...] = acc

    vm = pl.BlockSpec(memory_space=pltpu.VMEM)
    return pl.pallas_call(
        body, name=name, in_specs=[vm], out_specs=vm, out_shape=jax.ShapeDtypeStruct((R, C), F32),
        scratch_shapes=[pltpu.VMEM((N_DEV, R, C), F32), pltpu.SemaphoreType.DMA((7,)), pltpu.SemaphoreType.DMA((7,))],
        compiler_params=pltpu.CompilerParams(vmem_limit_bytes=VMEM_LIMIT_BYTES),
    )(v)


LANES = 128
PACK_ROW_MULTIPLE = 1024
SMALL_SHARDED = {
    "w_in": ((D_MODEL, 1216), 1), "mla_w_uq": ((MLA_Q_RANK, 768), 1), "mla_w_ukv": ((MLA_KV_RANK, 1024), 1),
    "ssm_w_glu": ((SSM_WIDTH, SSM_WIDTH), 0), "w_o": ((D_MODEL, D_MODEL), 0), "xattn_w_q": ((D_MODEL, 512), 0),
    "xattn_w_kv": ((D_MODEL, 1024), 0), "xattn_w_o": ((512, D_MODEL), 1),
}
FFN_NAMES = ["ffn1_w_gate", "ffn1_w_up", "ffn1_w_down", "ffn2_w_gate", "ffn2_w_up", "ffn2_w_down"]
TRANSPOSED_VIEW = ("ffn1_w_gate", "ffn1_w_up", "ffn2_w_gate", "ffn2_w_up", "w_in", "mla_w_uq")


def _shard_shape(name):
    (r, cdim), ax = SMALL_SHARDED[name]
    return (r // N_CHIPS, cdim) if ax == 0 else (r, cdim // N_CHIPS)


def _pack_shards(shards):
    parts = []
    for name in SMALL_SHARDED:
        a = shards[name]
        lead = a.shape[:-2]
        parts.append(a.reshape(lead + (a.shape[-2] * a.shape[-1] // LANES, LANES)))
    rows = sum(q.shape[-2] for q in parts)
    parts.append(jnp.zeros(lead + (-rows % PACK_ROW_MULTIPLE, LANES), parts[0].dtype))
    return jnp.concatenate(parts, axis=-2)


def _unpack_shards(packed):
    out, r0 = {}, 0
    lead = packed.shape[:-2]
    for name in SMALL_SHARDED:
        r, cdim = _shard_shape(name)
        rows = r * cdim // LANES
        out[name] = packed[..., r0:r0 + rows, :].reshape(lead + (r, cdim))
        r0 += rows
    return out


def _full_from_shards(name, sh):
    (r, cdim), ax = SMALL_SHARDED[name]
    if ax == 0:
        return sh.reshape(r, cdim)
    return jnp.transpose(sh, (1, 0, 2)).reshape(r, cdim)


def _shards_from_full(name, full):
    (r, cdim), ax = SMALL_SHARDED[name]
    if ax == 0:
        return full.reshape(N_CHIPS, r // N_CHIPS, cdim)
    return jnp.transpose(full.reshape(r, N_CHIPS, cdim // N_CHIPS), (1, 0, 2))


SMALL_REPL = {
    "ffn1_norm": (1, 1024), "mix_norm": (1, 1024), "mla_q_norm": (1, 384), "mla_kv_norm": (1, 256),
    "mla_qk_norm_q": (1, 192), "mla_qk_norm_k": (1, 192), "ssm_a_re": (32, 64), "ssm_a_im": (32, 64),
    "ssm_log_dt": (32, 1), "ssm_b_re": (32, 64, 16), "ssm_b_im": (32, 64, 16), "ssm_c_re": (32, 16, 64),
    "ssm_c_im": (32, 16, 64), "ssm_d": (1, 512), "ssm_b_glu": (1, 512), "out_norm_mla": (1, 512),
    "out_norm_ssm": (1, 512), "xattn_norm": (1, 1024), "mem_norm": (1, 1024), "xattn_q_norm": (1, 128),
    "xattn_k_norm": (1, 128), "ffn2_norm": (1, 1024),
}


def _pack_repl(grads):
    flat = jnp.concatenate([grads[n].reshape(-1) for n in SMALL_REPL])
    rows = -(-flat.shape[0] // (LANES * SUBLANES)) * SUBLANES
    return jnp.pad(flat, (0, rows * LANES - flat.shape[0])).reshape(rows, LANES)


def _unpack_repl(packed):
    flat, out, o = packed.reshape(-1), {}, 0
    for n, shp in SMALL_REPL.items():
        size = int(np.prod(shp))
        out[n] = flat[o:o + size].reshape(shp)
        o += size
    return out


def _rope_tables(positions):
    half = MLA_ROPE // 2
    inv = ROPE_THETA ** (-jnp.arange(half, dtype=F32) / half)
    ang = positions.astype(F32)[:, None] * inv[None, :]
    cos, sin = jnp.cos(ang), jnp.sin(ang)
    S = positions.shape[0]
    z = lambda w: jnp.zeros((S, w), F32)
    keep = jnp.concatenate([jnp.ones((S, MLA_NOPE), F32), cos, cos, z(HEAD_PAD - MLA_QK)], axis=1)
    from_hi = jnp.concatenate([z(MLA_NOPE), -sin, z(HEAD_PAD - MLA_NOPE - half)], axis=1)
    from_lo = jnp.concatenate([z(MLA_NOPE + half), sin, z(HEAD_PAD - MLA_QK)], axis=1)
    return keep, from_hi, from_lo


def _norm_rope(x, g, keep, from_hi, from_lo):
    y = x * lax.rsqrt(jnp.sum(x * x, axis=-1, keepdims=True) * (1.0 / MLA_QK) + EPS) * g
    half = MLA_ROPE // 2
    return y * keep + _lane_roll(y, HEAD_PAD - half) * from_hi + _lane_roll(y, half) * from_lo


def local_step(x, mem, positions, target, w, wb, ffn2_weights=None, on_grads=None):
    if ffn2_weights is None:
        ffn2_weights = lambda after: [wb[k] for k in FFN_NAMES[3:]]
    if on_grads is None:
        on_grads = lambda tag, g: 0.0
    S = x.shape[0]
    tm = min(TOKEN_TILE, S)
    nt = S // tm
    g1 = (nt,)
    gh = (MLA_HEADS, nt)
    tile = lambda arr, cols=None, cb=0, **kw: In(arr, (tm, arr.shape[1] if cols is None else cols), lambda i, cb=cb: (i, cb), **kw)
    par = lambda arr, **kw: In(arr, arr.shape, lambda *_: (0, 0), diff=True, **kw)
    otile = lambda cols, dt: Out((S, cols), dt, (tm, cols), lambda i: (i, 0))
    grads = {}

    x1, h1, g_1, u_1 = ffn_fwd("ffn1_fwd", x, w["ffn1_norm"], wb["ffn1_w_gate"], wb["ffn1_w_up"], wb["ffn1_w_down"])

    f_norm = lambda xv, g: (_rms(xv, g),)
    mix_ins = [tile(x1, diff=True), par(w["mix_norm"], acc=(0,))]
    mix_outs = [otile(D_MODEL, BF16)]
    (h2,) = seg_fwd("mix_norm_fwd", f_norm, g1, mix_ins, mix_outs)

    pq, pkv, pkr, pu = mm("w_in_fwd", [h2], [[wb["w_in_q"], wb["w_in_kv"], wb["w_in_kr"], wb["w_in_u"]]], [F32] * 4)

    f_lat = lambda a, b, ga, gb: (_rms(a, ga), _rms(b, gb))
    lat_ins = [tile(pq, diff=True), tile(pkv, diff=True), par(w["mla_q_norm"], acc=(0,)), par(w["mla_kv_norm"], acc=(0,))]
    lat_outs = [otile(MLA_Q_RANK, BF16), otile(MLA_KV_RANK, BF16)]
    cq, ckv = seg_fwd("latent_norm_fwd", f_lat, g1, lat_ins, lat_outs)

    (q0,) = mm("w_uq_fwd", [cq], [[wb["w_uq"]]], [F32])
    kn0, v0 = mm("w_ukv_fwd", [ckv], [[wb["w_ukv_k"], wb["w_ukv_v"]]], [F32, F32])

    keep, from_hi, from_lo = _rope_tables(positions)

    scale = MLA_QK ** -0.5

    def f_qk(qh, knh, kr, kp, fh, fl, gq, gk):
        return (_norm_rope(qh, gq, kp, fh, fl) * scale, _norm_rope(jnp.concatenate([knh, kr], axis=-1), gk, kp, fh, fl))

    hmap = lambda h, i: (i, h)
    tmap0 = lambda h, i: (i, 0)
    qk_ins = [In(q0, (tm, HEAD_PAD), hmap, diff=True), In(kn0, (tm, MLA_NOPE), hmap, diff=True),
              In(pkr, (tm, LANES), tmap0, diff=True,
                 grad=((MLA_HEADS, S, LANES), (None, tm, LANES), lambda h, i: (h, i, 0))),
              In(keep, (tm, HEAD_PAD), tmap0), In(from_hi, (tm, HEAD_PAD), tmap0), In(from_lo, (tm, HEAD_PAD), tmap0),
              In(w["qk_gain_q"], (1, HEAD_PAD), lambda h, i: (0, 0), diff=True, acc=(0, 1)),
              In(w["qk_gain_k"], (1, HEAD_PAD), lambda h, i: (0, 0), diff=True, acc=(0, 1))]
    head_out = Out((MLA_HEADS, S, HEAD_PAD), BF16, (None, tm, HEAD_PAD), lambda h, i: (h, i, 0))
    qh, kh = seg_fwd("qk_norm_rope_fwd", f_qk, gh, qk_ins, [head_out, head_out])

    o_mla, lse = attn_fwd("mla_attn_fwd", qh, kh, v0, t=min(ATTN_TILE, S))

    prep_grid = (SSM_LANES // PREP_LANES,)
    prep_ins = [_whole(w[k], diff=True, acc=(0,))
                for k in ("ssm_a_re", "ssm_a_im", "ssm_log_dt", "ssm_bt_re", "ssm_bt_im", "ssm_c2_re", "ssm_c2_im")]
    lane_out = lambda rows, dt: Out((rows, SSM_LANES), dt, (rows, PREP_LANES), lambda i: (0, i))
    prep_outs = [lane_out(SUBLANES, F32)] * 2 + [lane_out(SSM_WIDTH, BF16)] * 4
    a_r8, a_i8, bb_r, bb_i, cb_r, ncb_i = seg_fwd("ssm_prep_fwd", _ssm_prep_f, prep_grid, prep_ins, prep_outs)

    bu_r, bu_i = mm("ssm_bu_fwd", [pu], [[bb_r, bb_i]], [F32, F32], tm=256)
    xs_r, xs_i = ssm_scan("ssm_scan_fwd", bu_r, bu_i, a_r8, a_i8)
    (y_lin,) = mm("ssm_cx_fwd", [xs_r, xs_i], [[cb_r], [ncb_i]], [F32], trans=True, tm=256)

    f_gelu = lambda yl, u, d: (jax.nn.gelu(yl + d * u),)
    gelu_ins = [tile(y_lin, diff=True), tile(pu, diff=True), par(w["ssm_d"], acc=(0,))]
    (gl,) = seg_fwd("ssm_gelu_fwd", f_gelu, g1, gelu_ins, [otile(SSM_WIDTH, F32)])
    (z,) = mm("ssm_glu_fwd", [gl], [[wb["ssm_w_glu"]]], [F32])

    def f_glu(g, zz, o, b, gm, gs):
        return (_rms(o, gm), _rms(g * jax.nn.sigmoid(zz + b), gs))

    glu_ins = [tile(gl, diff=True), tile(z, diff=True), tile(o_mla, diff=True), par(w["ssm_b_glu"], acc=(0,)),
               par(w["out_norm_mla"], acc=(0,)), par(w["out_norm_ssm"], acc=(0,))]
    glu_outs = [otile(SSM_WIDTH, BF16), otile(SSM_WIDTH, BF16)]
    yn_mla, yn_ssm = seg_fwd("glu_out_norm_fwd", f_glu, g1, glu_ins, glu_outs)

    (x2,) = mm("w_o_fwd", [yn_mla, yn_ssm], [[wb["w_o_mla"]], [wb["w_o_ssm"]]], [F32], adds=[x1])

    xn_ins = [tile(x2, diff=True), par(w["xattn_norm"], acc=(0,))]
    (h3,) = seg_fwd("xattn_norm_fwd", f_norm, g1, xn_ins, [otile(D_MODEL, BF16)])
    (xq0,) = mm("xattn_q_fwd", [h3], [[wb["xattn_w_q"]]], [F32])

    M = mem.shape[0]
    mem_ins = [In(mem, (M, D_MODEL), lambda i: (0, 0)), par(w["mem_norm"], acc=(0,))]
    (mn,) = seg_fwd("mem_norm_fwd", f_norm, (1,), mem_ins, [Out((M, D_MODEL), BF16, (M, D_MODEL), lambda i: (0, 0))])
    (kvm,) = mm("xattn_kv_fwd", [mn], [[wb["xattn_w_kv"]]], [F32])

    xscale = XHD ** -0.5

    def f_xattn(qv, kv_k, kv_v, gq, gk):
        qn, kn = _rms(qv, gq), _rms(kv_k, gk)
        s = _bdot_nt(qn, kn) * xscale
        p = jax.nn.softmax(s, axis=-1)
        return (_bdot_nn(p, kv_v),)

    def xa_ins(gain_q):
        return [In(xq0, (tm, XHD), hmap, diff=True),
                In(kvm, (M, XHD), lambda h, i: (0, h), diff=True, acc=(1,), grad=((M, XH * XHD), (M, XHD), lambda h, i: (0, h))),
                In(kvm, (M, XHD), lambda h, i: (0, XH + h), diff=True, acc=(1,), grad=((M, XH * XHD), (M, XHD), lambda h, i: (0, h))),
                In(gain_q, (1, XHD), lambda h, i: (0, 0), diff=True, acc=(0, 1)),
                In(w["xattn_k_norm"], (1, XHD), lambda h, i: (0, 0), diff=True, acc=(0, 1))]

    xa_outs = [Out((S, XH * XHD), F32, (tm, XHD), hmap)]
    (o2,) = seg_fwd("xattn_fwd", f_xattn, gh, xa_ins(w["xattn_q_norm"]), xa_outs)

    (x3,) = mm("xattn_o_fwd", [o2], [[wb["xattn_w_o"]]], [F32], adds=[x2])

    wg2, wu2, wd2 = ffn2_weights(x3)
    x4, h4, g_2, u_2 = ffn_fwd("ffn2_fwd", x3, w["ffn2_norm"], wg2, wu2, wd2)
    dx4, parts = loss_grad("loss", x4, target)
    loss = jnp.sum(parts[::SUBLANES, 0])

    dx3, grads["ffn2_norm"], dg_2, du_2 = ffn_bwd_act("ffn2_bwd_act", dx4, x3, w["ffn2_norm"], g_2, u_2, wg2, wu2, wd2)
    grads["ffn2_w_gate"], grads["ffn2_w_up"], grads["ffn2_w_down"] = ffn_bwd_w("ffn2_bwd_w", h4, dx4, g_2, u_2, dg_2, du_2)
    sent = on_grads("ffn2", grads)

    (do2,) = mm("xattn_o_bwd", [dx3], [[wb["xattn_w_o"]]], [F32], trans=True)
    (grads["xattn_w_o"],) = mm_tn("xattn_o_bwd_w", [o2], [dx3], [(0, [0])])

    dxq0, dkm, dvm, grads["xattn_q_norm"], grads["xattn_k_norm"] = seg_bwd(
        "xattn_bwd", f_xattn, gh, xa_ins(w["xattn_q_norm"] + sent), xa_outs, [do2])
    (dh3,) = mm("xattn_q_bwd", [dxq0], [[wb["xattn_w_q"]]], [F32], trans=True)
    (grads["xattn_w_q"],) = mm_tn("xattn_q_bwd_w", [h3], [dxq0], [(0, [0])])
    wkv_k, wkv_v = wb["xattn_w_kv"][:, :XH * XHD], wb["xattn_w_kv"][:, XH * XHD:]
    (dmn,) = mm("xattn_kv_bwd", [dkm, dvm], [[wkv_k], [wkv_v]], [F32], trans=True)
    gk_w, gv_w = mm_tn("xattn_kv_bwd_w", [mn], [dkm, dvm], [(0, [0]), (0, [1])])
    grads["xattn_w_kv"] = jnp.concatenate([gk_w, gv_w], axis=1)
    (grads["mem_norm"],) = seg_bwd("mem_norm_bwd", f_norm, (1,), mem_ins,
                                   [Out((M, D_MODEL), BF16, (M, D_MODEL), lambda i: (0, 0))], [dmn])

    dx2, grads["xattn_norm"] = seg_bwd("xattn_norm_bwd", f_norm, g1, xn_ins, [otile(D_MODEL, BF16)], [dh3], adds={0: dx3})

    dyn_mla, dyn_ssm = mm("w_o_bwd", [dx2], [[wb["w_o_mla"], wb["w_o_ssm"]]], [F32, F32], trans=True)
    go_mla, go_ssm = mm_tn("w_o_bwd_w", [yn_mla, yn_ssm], [dx2], [(0, [0]), (1, [0])])
    grads["w_o"] = jnp.concatenate([go_mla, go_ssm], axis=0)

    dgl_a, dz, do_mla, grads["ssm_b_glu"], grads["out_norm_mla"], grads["out_norm_ssm"] = seg_bwd(
        "glu_out_norm_bwd", f_glu, g1, glu_ins, glu_outs, [dyn_mla, dyn_ssm])
    (dgl,) = mm("ssm_glu_bwd", [dz], [[wb["ssm_w_glu"]]], [F32], trans=True, adds=[dgl_a])
    (grads["ssm_w_glu"],) = mm_tn("ssm_glu_bwd_w", [gl], [dz], [(0, [0])])
    dy_lin, du_a, grads["ssm_d"] = seg_bwd("ssm_gelu_bwd", f_gelu, g1, gelu_ins, [otile(SSM_WIDTH, F32)], [dgl])

    gx_r, gx_i = mm("ssm_cx_bwd", [dy_lin], [[cb_r, ncb_i]], [F32, F32], tm=256)
    dcb_r, dncb_i = mm_tn("ssm_cx_bwd_w", [dy_lin], [xs_r, xs_i], [(0, [0]), (0, [1])], tm=256)
    lam_r, lam_i, da_r8, da_i8 = ssm_scan("ssm_scan_bwd", gx_r, gx_i, a_r8, a_i8, reverse=True, states=(xs_r, xs_i))
    (du,) = mm("ssm_bu_bwd", [lam_r, lam_i], [[bb_r], [bb_i]], [F32], trans=True, adds=[du_a], tm=256)
    dbb_r, dbb_i = mm_tn("ssm_bu_bwd_w", [pu], [lam_r, lam_i], [(0, [0]), (0, [1])], tm=256)
    prep_g = seg_bwd("ssm_prep_bwd", _ssm_prep_f, prep_grid, prep_ins, prep_outs, [da_r8, da_i8, dbb_r, dbb_i, dcb_r, dncb_i])
    for k, gname in enumerate(("ssm_a_re", "ssm_a_im", "ssm_log_dt", "ssm_bt_re", "ssm_bt_im", "ssm_c2_re", "ssm_c2_im")):
        grads[gname] = prep_g[k]

    dqh, dkh, dv0 = attn_bwd("mla_attn_bwd", qh, kh, v0, do_mla, o_mla, lse, t=min(ATTN_TILE, S))

    dq0, dkn0, dkr4, grads["qk_gain_q"], grads["qk_gain_k"] = seg_bwd(
        "qk_norm_rope_bwd", f_qk, gh, qk_ins, [head_out, head_out], [dqh, dkh])

    (dcq,) = mm("w_uq_bwd", [dq0], [[wb["w_uq"]]], [F32], trans=True)
    (grads["w_uq"],) = mm_tn("w_uq_bwd_w", [cq], [dq0], [(0, [0])])
    (dckv,) = mm("w_ukv_bwd", [dkn0, dv0], [[wb["w_ukv_k"]], [wb["w_ukv_v"]]], [F32], trans=True)
    grads["w_ukv_k"], grads["w_ukv_v"] = mm_tn("w_ukv_bwd_w", [ckv], [dkn0, dv0], [(0, [0]), (0, [1])])

    dpq, dpkv, grads["mla_q_norm"], grads["mla_kv_norm"] = seg_bwd("latent_norm_bwd", f_lat, g1, lat_ins, lat_outs, [dcq, dckv])

    dkr_list = [dkr4[h] for h in range(MLA_HEADS)]
    (dh2,) = mm("w_in_bwd", [dpq, dpkv, du] + dkr_list,
                [[wb["w_in_q"]], [wb["w_in_kv"]], [wb["w_in_u"]]] + [[wb["w_in_kr"]]] * MLA_HEADS, [F32], trans=True)
    grads["w_in_q"], grads["w_in_kv"], grads["w_in_u"], grads["w_in_kr"] = mm_tn(
        "w_in_bwd_w", [h2], [dpq, dpkv, du] + dkr_list, [(0, [0]), (0, [1]), (0, [2]), (0, [3, 4, 5, 6])])

    sent = on_grads("small", grads)
    dx1, grads["mix_norm"] = seg_bwd("mix_norm_bwd", f_norm, g1, mix_ins, mix_outs, [dh2], adds={0: dx2})

    dx, grads["ffn1_norm"], dg_1, du_1 = ffn_bwd_act("ffn1_bwd_act", dx1, x, w["ffn1_norm"] + sent, g_1, u_1,
                                                     wb["ffn1_w_gate"], wb["ffn1_w_up"], wb["ffn1_w_down"])
    grads["ffn1_w_gate"], grads["ffn1_w_up"], grads["ffn1_w_down"] = ffn_bwd_w("ffn1_bwd_w", h1, dx1, g_1, u_1, dg_1, du_1)
    return loss, dx, grads


def _pad_cols(a, n):
    return jnp.pad(a, ((0, 0), (0, n - a.shape[1])))


def _step_weights(full_bf16):
    wb = {}
    w_in = full_bf16["w_in"]
    wb["w_in_q"] = w_in[:, :MLA_Q_RANK]
    wb["w_in_kv"] = w_in[:, MLA_Q_RANK:MLA_Q_RANK + MLA_KV_RANK]
    wb["w_in_kr"] = _pad_cols(w_in[:, MLA_Q_RANK + MLA_KV_RANK:MLA_Q_RANK + MLA_KV_RANK + MLA_ROPE], LANES)
    wb["w_in_u"] = w_in[:, MLA_Q_RANK + MLA_KV_RANK + MLA_ROPE:]
    uq = full_bf16["mla_w_uq"].reshape(MLA_Q_RANK, MLA_HEADS, MLA_QK)
    wb["w_uq"] = jnp.pad(uq, ((0, 0), (0, 0), (0, HEAD_PAD - MLA_QK))).reshape(MLA_Q_RANK, MLA_HEADS * HEAD_PAD)
    ukv = full_bf16["mla_w_ukv"].reshape(MLA_KV_RANK, MLA_HEADS, MLA_NOPE + MLA_V)
    wb["w_ukv_k"] = ukv[:, :, :MLA_NOPE].reshape(MLA_KV_RANK, MLA_HEADS * MLA_NOPE)
    wb["w_ukv_v"] = ukv[:, :, MLA_NOPE:].reshape(MLA_KV_RANK, MLA_HEADS * MLA_V)
    wb["ssm_w_glu"] = full_bf16["ssm_w_glu"]
    wb["w_o_mla"] = full_bf16["w_o"][:SSM_WIDTH]
    wb["w_o_ssm"] = full_bf16["w_o"][SSM_WIDTH:]
    for k in ("xattn_w_q", "xattn_w_kv", "xattn_w_o"):
        wb[k] = full_bf16[k]
    return wb


def _problem_sharded_grads(g):
    out = {}
    kr = g["w_in_kr"][:, :MLA_ROPE]
    out["w_in"] = jnp.concatenate([g["w_in_q"], g["w_in_kv"], kr, g["w_in_u"]], axis=1)
    out["mla_w_uq"] = g["w_uq"].reshape(MLA_Q_RANK, MLA_HEADS, HEAD_PAD)[:, :, :MLA_QK].reshape(MLA_Q_RANK, MLA_HEADS * MLA_QK)
    k3 = g["w_ukv_k"].reshape(MLA_KV_RANK, MLA_HEADS, MLA_NOPE)
    v3 = g["w_ukv_v"].reshape(MLA_KV_RANK, MLA_HEADS, MLA_V)
    out["mla_w_ukv"] = jnp.concatenate([k3, v3], axis=2).reshape(MLA_KV_RANK, MLA_HEADS * (MLA_NOPE + MLA_V))
    for k in ("ssm_w_glu", "w_o", "xattn_w_q", "xattn_w_kv", "xattn_w_o"):
        out[k] = g[k]
    return out


def _problem_repl_grads(g):
    out = {}
    out["mla_qk_norm_q"] = g["qk_gain_q"][:, :MLA_QK]
    out["mla_qk_norm_k"] = g["qk_gain_k"][:, :MLA_QK]
    out["ssm_b_re"] = jnp.transpose(g["ssm_bt_re"].reshape(SSM_GROUPS, SSM_GROUP, SSM_STATE), (0, 2, 1))
    out["ssm_b_im"] = jnp.transpose(g["ssm_bt_im"].reshape(SSM_GROUPS, SSM_GROUP, SSM_STATE), (0, 2, 1))
    out["ssm_c_re"] = g["ssm_c2_re"].reshape(SSM_GROUPS, SSM_GROUP, SSM_STATE)
    out["ssm_c_im"] = g["ssm_c2_im"].reshape(SSM_GROUPS, SSM_GROUP, SSM_STATE)
    for k in SMALL_REPL:
        if k not in out:
            out[k] = g[k]
    return out


def _problem_grads(g):
    out = {**_problem_sharded_grads(g), **_problem_repl_grads(g)}
    out.update({k: g[k] for k in FFN_NAMES})
    return out


def _step_params(p):
    row = lambda a: a.reshape(1, -1)
    w = {k: row(p[k]) for k in ("ffn1_norm", "mix_norm", "mla_q_norm", "mla_kv_norm", "ssm_b_glu", "out_norm_mla",
                                "out_norm_ssm", "xattn_norm", "mem_norm", "xattn_q_norm", "xattn_k_norm", "ffn2_norm")}
    w["qk_gain_q"] = _pad_cols(row(p["mla_qk_norm_q"]), HEAD_PAD)
    w["qk_gain_k"] = _pad_cols(row(p["mla_qk_norm_k"]), HEAD_PAD)
    w["ssm_a_re"], w["ssm_a_im"] = p["ssm_a_re"], p["ssm_a_im"]
    w["ssm_log_dt"] = p["ssm_log_dt"].reshape(SSM_GROUPS, 1)
    w["ssm_bt_re"] = jnp.transpose(p["ssm_b_re"], (0, 2, 1)).reshape(SSM_WIDTH, SSM_STATE)
    w["ssm_bt_im"] = jnp.transpose(p["ssm_b_im"], (0, 2, 1)).reshape(SSM_WIDTH, SSM_STATE)
    w["ssm_c2_re"] = p["ssm_c_re"].reshape(SSM_WIDTH, SSM_STATE)
    w["ssm_c2_im"] = p["ssm_c_im"].reshape(SSM_WIDTH, SSM_STATE)
    w["ssm_d"] = p["ssm_d"].reshape(1, SSM_WIDTH)
    return w


ARG_NAMES = ['x', 'mem', 'positions', 'ffn1_norm', 'ffn1_w_gate', 'ffn1_w_up', 'ffn1_w_down', 'mix_norm', 'w_in', 'mla_q_norm', 'mla_w_uq', 'mla_kv_norm', 'mla_w_ukv', 'mla_qk_norm_q', 'mla_qk_norm_k', 'ssm_a_re', 'ssm_a_im', 'ssm_log_dt', 'ssm_b_re', 'ssm_b_im', 'ssm_c_re', 'ssm_c_im', 'ssm_d', 'ssm_w_glu', 'ssm_b_glu', 'out_norm_mla', 'out_norm_ssm', 'w_o', 'xattn_norm', 'mem_norm', 'xattn_w_q', 'xattn_w_kv', 'xattn_q_norm', 'xattn_k_norm', 'xattn_w_o', 'ffn2_norm', 'ffn2_w_gate', 'ffn2_w_up', 'ffn2_w_down']
WEIGHT_NAMES = ARG_NAMES[3:]


def _gather_weights(p, c):
    half = lambda a: lax.dynamic_slice_in_dim(a, c * (a.shape[0] // 2), a.shape[0] // 2, axis=0)
    packed = _pack_shards({k: p[k].astype(BF16) for k in SMALL_SHARDED})
    ffn1 = [half(p[k].astype(BF16)) for k in FFN_NAMES[:3]] + [half(packed)]
    ffn2 = [half(p[k].astype(BF16)) for k in FFN_NAMES[3:]]
    me = 4 * lax.axis_index("x") + 2 * lax.axis_index("y") + c
    own = lambda got, blocks: [lax.dynamic_update_index_in_dim(g, b, me, 0) for g, b in zip(got, blocks)]
    as_shards = lambda a: a.reshape(N_CHIPS, 2 * a.shape[1], a.shape[2])
    got1 = own(all_gather_halves("all_gather_weights_a", ffn1), ffn1)
    got1, ffn2 = lax.optimization_barrier((got1, ffn2))
    flight, sent = split_copy_start("gather_ffn2_start", _gather_copies, N_DEV - 1, ffn2,
                                    [jax.ShapeDtypeStruct((N_DEV,) + b.shape, b.dtype) for b in ffn2])
    wb = {k: as_shards(a) for k, a in zip(FFN_NAMES[:3], got1[:3])}
    small = _unpack_shards(got1[3].reshape(N_CHIPS, -1, LANES))
    wb.update(_step_weights({k: _full_from_shards(k, small[k]) for k in SMALL_SHARDED}))

    def ffn2_weights(after):
        return [as_shards(a) for a in own(split_copy_wait("gather_ffn2_wait", flight, after), ffn2)]

    return wb, ffn2_weights, sent


class _GradReduce:
    def __init__(self, c):
        self.c, self.c_arr = c, jnp.reshape(c, (1,)).astype(jnp.int32)
        self.chip = 2 * lax.axis_index("x") + lax.axis_index("y")
        self.flights = []

    def start(self, tag, arrs):
        split = [a.reshape(N_CHIPS, 2, a.shape[1] // 2, a.shape[2]) for a in arrs]
        got = pair_send_halves(f"grad_pair_send_{tag}", split)
        parts = [pair_add(f"grad_pair_add_{tag}_{k}", s, g, self.c_arr) for k, (s, g) in enumerate(zip(split, got))]
        flight, sent = split_copy_start(f"grad_scatter_start_{tag}", _scatter_copies, N_CHIPS - 1, parts, parts)
        self.flights.append((tag, parts, flight))
        return sent

    def finish(self, after):
        halves = []
        for tag, parts, flight in self.flights:
            landed = split_copy_wait(f"grad_scatter_wait_{tag}", flight, after)
            for k, (q, p) in enumerate(zip(landed, parts)):
                mine = lax.dynamic_index_in_dim(p, self.chip, 0, keepdims=False)
                halves.append(sum_chips(f"grad_sum_{tag}_{k}", lax.dynamic_update_index_in_dim(q, mine, self.chip, 0)))
        tags = "_".join(t for t, _, _ in self.flights)
        self.flights = []
        theirs = pair_exchange(f"grad_pair_exchange_{tags}", halves)
        return [jnp.where(self.c == 0, jnp.concatenate([h, t], axis=0), jnp.concatenate([t, h], axis=0))
                for h, t in zip(halves, theirs)]


def kernel(x, mem, positions, ffn1_norm, ffn1_w_gate, ffn1_w_up, ffn1_w_down, mix_norm, w_in, mla_q_norm, mla_w_uq, mla_kv_norm, mla_w_ukv, mla_qk_norm_q, mla_qk_norm_k, ssm_a_re, ssm_a_im, ssm_log_dt, ssm_b_re, ssm_b_im, ssm_c_re, ssm_c_im, ssm_d, ssm_w_glu, ssm_b_glu, out_norm_mla, out_norm_ssm, w_o, xattn_norm, mem_norm, xattn_w_q, xattn_w_kv, xattn_q_norm, xattn_k_norm, xattn_w_o, ffn2_norm, ffn2_w_gate, ffn2_w_up, ffn2_w_down, loss_target, m_ffn1_norm, m_ffn1_w_gate, m_ffn1_w_up, m_ffn1_w_down, m_mix_norm, m_w_in, m_mla_q_norm, m_mla_w_uq, m_mla_kv_norm, m_mla_w_ukv, m_mla_qk_norm_q, m_mla_qk_norm_k, m_ssm_a_re, m_ssm_a_im, m_ssm_log_dt, m_ssm_b_re, m_ssm_b_im, m_ssm_c_re, m_ssm_c_im, m_ssm_d, m_ssm_w_glu, m_ssm_b_glu, m_out_norm_mla, m_out_norm_ssm, m_w_o, m_xattn_norm, m_mem_norm, m_xattn_w_q, m_xattn_w_kv, m_xattn_q_norm, m_xattn_k_norm, m_xattn_w_o, m_ffn2_norm, m_ffn2_w_gate, m_ffn2_w_up, m_ffn2_w_down, v_ffn1_norm, v_ffn1_w_gate, v_ffn1_w_up, v_ffn1_w_down, v_mix_norm, v_w_in, v_mla_q_norm, v_mla_w_uq, v_mla_kv_norm, v_mla_w_ukv, v_mla_qk_norm_q, v_mla_qk_norm_k, v_ssm_a_re, v_ssm_a_im, v_ssm_log_dt, v_ssm_b_re, v_ssm_b_im, v_ssm_c_re, v_ssm_c_im, v_ssm_d, v_ssm_w_glu, v_ssm_b_glu, v_out_norm_mla, v_out_norm_ssm, v_w_o, v_xattn_norm, v_mem_norm, v_xattn_w_q, v_xattn_w_kv, v_xattn_q_norm, v_xattn_k_norm, v_xattn_w_o, v_ffn2_norm, v_ffn2_w_gate, v_ffn2_w_up, v_ffn2_w_down):
    args = dict(locals())
    c = lax.axis_index("c")
    view = lambda k, a: jnp.swapaxes(a, 0, 1) if k in TRANSPOSED_VIEW else a
    p = {k: view(k, args[k][0]) for k in WEIGHT_NAMES}
    mom = {k: view(k, args["m_" + k][0]) for k in WEIGHT_NAMES}
    var = {k: view(k, args["v_" + k][0]) for k in WEIGHT_NAMES}
    natural = {k: view(k, p[k]) for k in WEIGHT_NAMES}

    wb, ffn2_weights, sent = _gather_weights({k: (p[k] if k in FFN_NAMES else natural[k]) for k in WEIGHT_NAMES}, c)
    w = _step_params(natural)
    w["ffn1_norm"] = w["ffn1_norm"] + sent
    early, late = _GradReduce(c), _GradReduce(c)

    def on_grads(tag, g):
        if tag == "ffn2":
            return early.start(tag, [g[k] for k in FFN_NAMES[3:]])
        full = _problem_sharded_grads(g)
        return early.start(tag, [_pack_shards({k: _shards_from_full(k, full[k]) for k in SMALL_SHARDED})])

    loss, dx, g = local_step(x[0], mem[0], positions[0], loss_target[0], w, wb, ffn2_weights, on_grads)
    loss = lax.psum(loss, ("x", "y", "c"))

    sent = late.start("ffn1", [g[k] for k in FFN_NAMES[:3]])
    shards = early.finish(dx[:SUBLANES, :LANES] + sent)
    grad = dict(zip(FFN_NAMES[3:], shards[:3]))
    small_sharded = _unpack_shards(shards[3])
    grad.update({k: view(k, small_sharded[k]) for k in SMALL_SHARDED})
    repl = _problem_repl_grads(g)
    grad.update(_unpack_repl(all_reduce_small("grad_all_reduce_small", _pack_repl({k: repl[k] for k in SMALL_REPL}))))

    delta, new_m, new_v = {}, {}, {}
    small = [k for k in WEIGHT_NAMES if k not in FFN_NAMES and k not in SMALL_SHARDED]
    as2d = lambda a: a.reshape(-1, a.shape[-1])

    def update(k):
        delta[k], new_m[k], new_v[k] = adamw_big("adamw_" + k, as2d(p[k]), as2d(grad[k]), as2d(mom[k]), as2d(var[k]))

    for k in WEIGHT_NAMES:
        if k not in small and k not in FFN_NAMES[:3]:
            update(k)
    ds, nms, nvs = adamw_small("adamw_small", [as2d(p[k]) for k in small], [as2d(grad[k].reshape(p[k].shape)) for k in small],
                               [as2d(mom[k]) for k in small], [as2d(var[k]) for k in small])
    for k, d, nm, nv in zip(small, ds, nms, nvs):
        delta[k], new_m[k], new_v[k] = d, nm, nv

    grad.update(zip(FFN_NAMES[:3], late.finish(delta[FFN_NAMES[-1]])))
    for k in FFN_NAMES[:3]:
        update(k)

    shaped = lambda d, k: view(k, d.reshape(p[k].shape)).reshape(args[k].shape)
    return (loss, dx[None], *[shaped(grad[k], k) for k in WEIGHT_NAMES], *[shaped(delta[k], k) for k in WEIGHT_NAMES],
            *[shaped(new_m[k], k) for k in WEIGHT_NAMES], *[shaped(new_v[k], k) for k in WEIGHT_NAMES])
```

```python
import functools
import math

import jax
import jax.numpy as jnp
import numpy as np
from jax import lax
from jax.experimental import pallas as pl
from jax.experimental.pallas import tpu as pltpu

F32, BF16 = jnp.float32, jnp.bfloat16
EPS = 1e-6
MESH = pl.DeviceIdType.MESH

D_MODEL, D_FF = 1024, 2752
MLA_HEADS, MLA_Q_RANK, MLA_KV_RANK, MLA_NOPE, MLA_ROPE, MLA_V = 4, 384, 256, 128, 64, 128
MLA_QK = MLA_NOPE + MLA_ROPE
HEAD_PAD = 256
SSM_WIDTH, SSM_GROUP, SSM_GROUPS, SSM_STATE = 512, 16, 32, 64
SSM_LANES = SSM_GROUPS * SSM_STATE
XH, XHD = 4, 128
ROPE_THETA = 10000.0
ADAM_LR, ADAM_B1, ADAM_B2, ADAM_EPS, ADAM_WD, ADAM_STEP = 0.001, 0.9, 0.999, 1e-08, 0.01, 10
N_CHIPS, N_CORES, N_DEV = 4, 2, 8

VMEM_LIMIT_BYTES = 56 * 2**20
TOKEN_TILE = 512
ATTN_TILE = 512
SCAN_TIME_TILE = 1024
SCAN_LANE_TILE = 256
SUBLANES = 8


def _params(n_axes):
    return pltpu.CompilerParams(dimension_semantics=("arbitrary",) * n_axes, vmem_limit_bytes=VMEM_LIMIT_BYTES)


def _first(axes):
    cond = None
    for a in axes:
        c = pl.program_id(a) == 0
        cond = c if cond is None else jnp.logical_and(cond, c)
    return cond


def mm(name, xs, ws, out_dtypes, *, trans=False, adds=None, tm=TOKEN_TILE):
    rows = xs[0].shape[0]
    tm = min(tm, rows)
    n_in, n_out = len(xs), len(out_dtypes)
    pairs = [(i, j) for i in range(n_in) for j in range(n_out) if ws[i][j] is not None]
    w_list = [ws[i][j] for (i, j) in pairs]
    adds = list(adds) if adds is not None else [None] * n_out
    add_list = [a for a in adds if a is not None]
    out_cols = [None] * n_out
    for (i, j), w in zip(pairs, w_list):
        out_cols[j] = w.shape[0] if trans else w.shape[1]
    contract = (((1,), (1 if trans else 0,)), ((), ()))

    def body(*refs):
        x_refs = refs[:n_in]
        w_refs = refs[n_in:n_in + len(pairs)]
        a_refs = list(refs[n_in + len(pairs):n_in + len(pairs) + len(add_list)])
        o_refs = refs[n_in + len(pairs) + len(add_list):]
        xb = [None] * n_in
        for j in range(n_out):
            acc = None
            for p, (i, jj) in enumerate(pairs):
                if jj != j:
                    continue
                if xb[i] is None:
                    xb[i] = x_refs[i][...].astype(BF16)
                d = lax.dot_general(xb[i], w_refs[p][...].astype(BF16), contract, preferred_element_type=F32)
                acc = d if acc is None else acc + d
            if adds[j] is not None:
                acc = acc + a_refs.pop(0)[...].astype(F32)
            o_refs[j][...] = acc.astype(o_refs[j].dtype)

    in_specs = ([pl.BlockSpec((tm, x.shape[1]), lambda i: (i, 0)) for x in xs]
                + [pl.BlockSpec(w.shape, lambda i: (0, 0)) for w in w_list]
                + [pl.BlockSpec((tm, a.shape[1]), lambda i: (i, 0)) for a in add_list])
    outs = pl.pallas_call(
        body, name=name, grid=(rows // tm,), in_specs=in_specs,
        out_specs=[pl.BlockSpec((tm, n), lambda i: (i, 0)) for n in out_cols],
        out_shape=[jax.ShapeDtypeStruct((rows, n), dt) for n, dt in zip(out_cols, out_dtypes)],
        compiler_params=_params(1),
    )(*xs, *w_list, *add_list)
    return list(outs)


def mm_tn(name, xs, dys, pairs, *, tm=TOKEN_TILE):
    rows = xs[0].shape[0]
    tm = min(tm, rows)
    n_x, n_dy = len(xs), len(dys)
    contract = (((0,), (0,)), ((), ()))

    def body(*refs):
        x_refs, dy_refs, o_refs = refs[:n_x], refs[n_x:n_x + n_dy], refs[n_x + n_dy:]
        @pl.when(pl.program_id(0) == 0)
        def _():
            for o in o_refs:
                o[...] = jnp.zeros_like(o)

        for k, (i, js) in enumerate(pairs):
            dy = None
            for j in js:
                t = dy_refs[j][...].astype(F32)
                dy = t if dy is None else dy + t
            o_refs[k][...] += lax.dot_general(x_refs[i][...].astype(BF16), dy.astype(BF16), contract,
                                              preferred_element_type=F32)

    shapes = [(xs[i].shape[1], dys[js[0]].shape[1]) for (i, js) in pairs]
    outs = pl.pallas_call(
        body, name=name, grid=(rows // tm,),
        in_specs=[pl.BlockSpec((tm, a.shape[1]), lambda i: (i, 0)) for a in (*xs, *dys)],
        out_specs=[pl.BlockSpec(s, lambda i: (0, 0)) for s in shapes],
        out_shape=[jax.ShapeDtypeStruct(s, F32) for s in shapes],
        compiler_params=_params(1),
    )(*xs, *dys)
    return list(outs)


def bd_mm(name, x, ws, out_dtypes, tm=1024):
    rows = x.shape[0]
    tm = min(tm, rows)
    nb, kb = ws[0].shape[0], ws[0].shape[1]
    n_w = len(ws)

    def body(*refs):
        xb = refs[0][...].astype(BF16)
        for j in range(n_w):
            refs[1 + n_w + j][...] = jnp.dot(xb, refs[1 + j][...], preferred_element_type=F32).astype(refs[1 + n_w + j].dtype)

    return list(pl.pallas_call(
        body, name=name, grid=(rows // tm, nb),
        in_specs=[pl.BlockSpec((tm, kb), lambda i, b: (i, b))] + [pl.BlockSpec((None, kb, w.shape[2]), lambda i, b: (b, 0, 0)) for w in ws],
        out_specs=[pl.BlockSpec((tm, w.shape[2]), lambda i, b: (i, b)) for w in ws],
        out_shape=[jax.ShapeDtypeStruct((rows, nb * w.shape[2]), dt) for w, dt in zip(ws, out_dtypes)],
        compiler_params=_params(2),
    )(x, *ws))


def bd_mm_t(name, xs, ws, add=None, tm=1024):
    rows = xs[0].shape[0]
    tm = min(tm, rows)
    nb, kb = ws[0].shape[0], ws[0].shape[1]
    n = len(xs)

    def body(*refs):
        acc = None
        for k in range(n):
            d = lax.dot_general(refs[k][...].astype(BF16), refs[n + k][...], NT_DIMS, preferred_element_type=F32)
            acc = d if acc is None else acc + d
        if add is not None:
            acc = acc + refs[2 * n][...]
        refs[-1][...] = acc

    ins = ([pl.BlockSpec((tm, w.shape[2]), lambda i, b: (i, b)) for w in ws]
           + [pl.BlockSpec((None, kb, w.shape[2]), lambda i, b: (b, 0, 0)) for w in ws]
           + ([pl.BlockSpec((tm, kb), lambda i, b: (i, b))] if add is not None else []))
    return pl.pallas_call(
        body, name=name, grid=(rows // tm, nb), in_specs=ins, out_specs=pl.BlockSpec((tm, kb), lambda i, b: (i, b)),
        out_shape=jax.ShapeDtypeStruct((rows, nb * kb), F32), compiler_params=_params(2),
    )(*xs, *ws, *([add] if add is not None else []))


def bd_tn(name, x, dys, nb, tm=1024):
    rows = x.shape[0]
    tm = min(tm, rows)
    kb = x.shape[1] // nb
    n = len(dys)

    def body(*refs):
        @pl.when(pl.program_id(1) == 0)
        def _():
            for o in refs[1 + n:]:
                o[...] = jnp.zeros_like(o)

        xb = refs[0][...].astype(BF16)
        for j in range(n):
            refs[1 + n + j][...] += lax.dot_general(xb, refs[1 + j][...].astype(BF16), TN_DIMS, preferred_element_type=F32)

    return list(pl.pallas_call(
        body, name=name, grid=(nb, rows // tm),
        in_specs=[pl.BlockSpec((tm, kb), lambda b, i: (i, b))] + [pl.BlockSpec((tm, d.shape[1] // nb), lambda b, i: (i, b)) for d in dys],
        out_specs=[pl.BlockSpec((None, kb, d.shape[1] // nb), lambda b, i: (b, 0, 0)) for d in dys],
        out_shape=[jax.ShapeDtypeStruct((nb, kb, d.shape[1] // nb), F32) for d in dys],
        compiler_params=_params(2),
    )(x, *dys))


class In:
    def __init__(self, arr, block, imap, *, diff=False, acc=None, grad=None):
        self.arr, self.block, self.imap, self.diff, self.acc, self.grad = arr, block, imap, diff, acc, grad

    def spec(self):
        return pl.BlockSpec(self.block, self.imap)


class Out:
    def __init__(self, shape, dtype, block, imap):
        self.shape, self.dtype, self.block, self.imap = shape, dtype, block, imap

    def spec(self):
        return pl.BlockSpec(self.block, self.imap)


def seg_fwd(name, f, grid, ins, outs):
    n_in = len(ins)

    def body(*refs):
        res = f(*[r[...] for r in refs[:n_in]])
        for o_ref, r in zip(refs[n_in:], res):
            o_ref[...] = r.astype(o_ref.dtype)

    res = pl.pallas_call(
        body, name=name, grid=grid, in_specs=[i.spec() for i in ins], out_specs=[o.spec() for o in outs],
        out_shape=[jax.ShapeDtypeStruct(o.shape, o.dtype) for o in outs], compiler_params=_params(len(grid)),
    )(*[i.arr for i in ins])
    return list(res)


def seg_bwd(name, f, grid, ins, outs, cts, adds=None):
    n_in, n_ct = len(ins), len(cts)
    diff_idx = [k for k, i in enumerate(ins) if i.diff]
    adds = adds or {}
    add_keys = sorted(adds)
    add_list = [adds[k] for k in add_keys]

    def body(*refs):
        in_refs, ct_refs = refs[:n_in], refs[n_in:n_in + n_ct]
        add_refs = dict(zip(add_keys, refs[n_in + n_ct:n_in + n_ct + len(add_list)]))
        g_refs = refs[n_in + n_ct + len(add_list):]
        vals = [r[...] for r in in_refs]

        def g(*dv):
            full = list(vals)
            for k, v in zip(diff_idx, dv):
                full[k] = v
            return tuple(f(*full))

        _, pull = jax.vjp(g, *[vals[k].astype(F32) for k in diff_idx])
        grads = pull(tuple(c[...].astype(F32) for c in ct_refs))
        for n, (k, gr) in enumerate(zip(diff_idx, grads)):
            if k in add_refs:
                gr = gr + add_refs[k][...].astype(F32)
            if ins[k].acc is None:
                g_refs[n][...] = gr.astype(g_refs[n].dtype)
            else:
                @pl.when(_first(ins[k].acc))
                def _(n=n):
                    g_refs[n][...] = jnp.zeros_like(g_refs[n])

                g_refs[n][...] += gr

    g_specs, g_shapes = [], []
    for k in diff_idx:
        i = ins[k]
        shape, block, imap = i.grad if i.grad is not None else (i.arr.shape, i.block, i.imap)
        g_specs.append(pl.BlockSpec(block, imap))
        g_shapes.append(jax.ShapeDtypeStruct(shape, F32))
    in_specs = ([i.spec() for i in ins] + [o.spec() for o in outs]
                + [pl.BlockSpec(ins[k].block, ins[k].imap) for k in add_keys])
    res = pl.pallas_call(
        body, name=name, grid=grid, in_specs=in_specs, out_specs=g_specs, out_shape=g_shapes,
        compiler_params=_params(len(grid)),
    )(*[i.arr for i in ins], *cts, *add_list)
    return list(res)


def _rms(x, g):
    return x * lax.rsqrt(jnp.mean(x * x, axis=-1, keepdims=True) + EPS) * g


@jax.custom_vjp
def _bdot_nt(a, b):
    return lax.dot_general(a.astype(BF16), b.astype(BF16), (((1,), (1,)), ((), ())), preferred_element_type=F32)


def _bdot_nt_fwd(a, b):
    return _bdot_nt(a, b), (a, b)


def _bdot_nt_bwd(res, g):
    a, b = res
    gb = g.astype(BF16)
    da = lax.dot_general(gb, b.astype(BF16), (((1,), (0,)), ((), ())), preferred_element_type=F32)
    db = lax.dot_general(gb, a.astype(BF16), (((0,), (0,)), ((), ())), preferred_element_type=F32)
    return da, db


_bdot_nt.defvjp(_bdot_nt_fwd, _bdot_nt_bwd)


@jax.custom_vjp
def _bdot_nn(a, b):
    return lax.dot_general(a.astype(BF16), b.astype(BF16), (((1,), (0,)), ((), ())), preferred_element_type=F32)


def _bdot_nn_fwd(a, b):
    return _bdot_nn(a, b), (a, b)


def _bdot_nn_bwd(res, g):
    a, b = res
    gb = g.astype(BF16)
    da = lax.dot_general(gb, b.astype(BF16), (((1,), (1,)), ((), ())), preferred_element_type=F32)
    db = lax.dot_general(a.astype(BF16), gb, (((0,), (0,)), ((), ())), preferred_element_type=F32)
    return da, db


_bdot_nn.defvjp(_bdot_nn_fwd, _bdot_nn_bwd)


@functools.partial(jax.custom_vjp, nondiff_argnums=(1,))
def _lane_roll(x, shift):
    return pltpu.roll(x, shift, 1)


def _lane_roll_fwd(x, shift):
    return pltpu.roll(x, shift, 1), None


def _lane_roll_bwd(shift, _, g):
    return (pltpu.roll(g, (g.shape[1] - shift) % g.shape[1], 1),)


_lane_roll.defvjp(_lane_roll_fwd, _lane_roll_bwd)


def _hp_dot(a, b):
    return jnp.dot(a, b, precision=lax.Precision.HIGHEST, preferred_element_type=F32)


NT_DIMS = (((1,), (1,)), ((), ()))
TN_DIMS = (((0,), (0,)), ((), ()))


def ffn_fwd(name, x, gain, wg, wu, wd, tm=TOKEN_TILE):
    S, D = x.shape
    nsh, Fs, _ = wg.shape

    def body(x_ref, gain_ref, wg_ref, wu_ref, wd_ref, xo_ref, h_ref, g_ref, u_ref, acc):
        j = pl.program_id(1)

        @pl.when(j == 0)
        def _():
            h_ref[...] = _rms(x_ref[...], gain_ref[...]).astype(BF16)
            acc[...] = jnp.zeros_like(acc)

        h = h_ref[...]
        g = lax.dot_general(h, wg_ref[...], NT_DIMS, preferred_element_type=F32)
        u = lax.dot_general(h, wu_ref[...], NT_DIMS, preferred_element_type=F32)
        g_ref[...] = g.astype(BF16)
        u_ref[...] = u.astype(BF16)
        a = g * jax.nn.sigmoid(g) * u
        acc[...] += jnp.dot(a.astype(BF16), wd_ref[...], preferred_element_type=F32)

        @pl.when(j == nsh - 1)
        def _():
            xo_ref[...] = x_ref[...] + 0.5 * acc[...]

    return pl.pallas_call(
        body, name=name, grid=(S // tm, nsh),
        in_specs=[pl.BlockSpec((tm, D), lambda i, j: (i, 0)), pl.BlockSpec((1, D), lambda i, j: (0, 0)),
                  pl.BlockSpec((None, Fs, D), lambda i, j: (j, 0, 0)), pl.BlockSpec((None, Fs, D), lambda i, j: (j, 0, 0)),
                  pl.BlockSpec((None, Fs, D), lambda i, j: (j, 0, 0))],
        out_specs=[pl.BlockSpec((tm, D), lambda i, j: (i, 0)), pl.BlockSpec((tm, D), lambda i, j: (i, 0)),
                   pl.BlockSpec((None, tm, Fs), lambda i, j: (j, i, 0)), pl.BlockSpec((None, tm, Fs), lambda i, j: (j, i, 0))],
        out_shape=[jax.ShapeDtypeStruct((S, D), F32), jax.ShapeDtypeStruct((S, D), BF16),
                   jax.ShapeDtypeStruct((nsh, S, Fs), BF16), jax.ShapeDtypeStruct((nsh, S, Fs), BF16)],
        scratch_shapes=[pltpu.VMEM((tm, D), F32)], compiler_params=_params(2),
    )(x, gain, wg, wu, wd)


def ffn_bwd_act(name, dxo, x, gain, g, u, wg, wu, wd, tm=TOKEN_TILE):
    S, D = x.shape
    nsh, Fs, _ = wg.shape

    def body(dxo_ref, x_ref, gain_ref, g_ref, u_ref, wg_ref, wu_ref, wd_ref, dx_ref, dgain_ref, dg_ref, du_ref, dd, dh):
        i, j = pl.program_id(0), pl.program_id(1)

        @pl.when(j == 0)
        def _():
            dd[...] = (0.5 * dxo_ref[...]).astype(BF16)
            dh[...] = jnp.zeros_like(dh)

        da = lax.dot_general(dd[...], wd_ref[...], NT_DIMS, preferred_element_type=F32)
        gf, uf = g_ref[...].astype(F32), u_ref[...].astype(F32)
        sig = jax.nn.sigmoid(gf)
        dgv = (da * uf * (sig * (1.0 + gf * (1.0 - sig)))).astype(BF16)
        duv = (da * (gf * sig)).astype(BF16)
        dg_ref[...] = dgv
        du_ref[...] = duv
        dh[...] += (jnp.dot(dgv, wg_ref[...], preferred_element_type=F32)
                    + jnp.dot(duv, wu_ref[...], preferred_element_type=F32))

        @pl.when(j == nsh - 1)
        def _():
            xv = x_ref[...]
            r = lax.rsqrt(jnp.mean(xv * xv, axis=-1, keepdims=True) + EPS)
            xhat = xv * r
            dhv = dh[...]
            dxn = dhv * gain_ref[...]
            dx_ref[...] = dxo_ref[...] + r * (dxn - xhat * jnp.mean(dxn * xhat, axis=-1, keepdims=True))
            part = jnp.sum(dhv * xhat, axis=0, keepdims=True)

            @pl.when(i == 0)
            def _():
                dgain_ref[...] = part

            @pl.when(i != 0)
            def _():
                dgain_ref[...] += part

    return pl.pallas_call(
        body, name=name, grid=(S // tm, nsh),
        in_specs=[pl.BlockSpec((tm, D), lambda i, j: (i, 0)), pl.BlockSpec((tm, D), lambda i, j: (i, 0)),
                  pl.BlockSpec((1, D), lambda i, j: (0, 0)),
                  pl.BlockSpec((None, tm, Fs), lambda i, j: (j, i, 0)), pl.BlockSpec((None, tm, Fs), lambda i, j: (j, i, 0)),
                  pl.BlockSpec((None, Fs, D), lambda i, j: (j, 0, 0)), pl.BlockSpec((None, Fs, D), lambda i, j: (j, 0, 0)),
                  pl.BlockSpec((None, Fs, D), lambda i, j: (j, 0, 0))],
        out_specs=[pl.BlockSpec((tm, D), lambda i, j: (i, 0)), pl.BlockSpec((1, D), lambda i, j: (0, 0)),
                   pl.BlockSpec((None, tm, Fs), lambda i, j: (j, i, 0)), pl.BlockSpec((None, tm, Fs), lambda i, j: (j, i, 0))],
        out_shape=[jax.ShapeDtypeStruct((S, D), F32), jax.ShapeDtypeStruct((1, D), F32),
                   jax.ShapeDtypeStruct((nsh, S, Fs), BF16), jax.ShapeDtypeStruct((nsh, S, Fs), BF16)],
        scratch_shapes=[pltpu.VMEM((tm, D), BF16), pltpu.VMEM((tm, D), F32)], compiler_params=_params(2),
    )(dxo, x, gain, g, u, wg, wu, wd)


def ffn_bwd_w(name, h, dxo, g, u, dg, du, tm=TOKEN_TILE):
    S, D = h.shape
    nsh, _, Fs = g.shape

    def body(h_ref, dxo_ref, g_ref, u_ref, dg_ref, du_ref, dwg_ref, dwu_ref, dwd_ref):
        i = pl.program_id(1)
        gf, uf = g_ref[...].astype(F32), u_ref[...].astype(F32)
        a = (gf * jax.nn.sigmoid(gf) * uf).astype(BF16)
        dd = (0.5 * dxo_ref[...]).astype(BF16)
        hv = h_ref[...]

        @pl.when(i == 0)
        def _():
            dwg_ref[...] = jnp.zeros_like(dwg_ref)
            dwu_ref[...] = jnp.zeros_like(dwu_ref)
            dwd_ref[...] = jnp.zeros_like(dwd_ref)

        dwg_ref[...] += lax.dot_general(dg_ref[...], hv, TN_DIMS, preferred_element_type=F32)
        dwu_ref[...] += lax.dot_general(du_ref[...], hv, TN_DIMS, preferred_element_type=F32)
        dwd_ref[...] += lax.dot_general(a, dd, TN_DIMS, preferred_element_type=F32)

    act = pl.BlockSpec((None, tm, Fs), lambda j, i: (j, i, 0))
    wspec = pl.BlockSpec((None, Fs, D), lambda j, i: (j, 0, 0))
    return pl.pallas_call(
        body, name=name, grid=(nsh, S // tm),
        in_specs=[pl.BlockSpec((tm, D), lambda j, i: (i, 0)), pl.BlockSpec((tm, D), lambda j, i: (i, 0)), act, act, act, act],
        out_specs=[wspec, wspec, wspec], out_shape=[jax.ShapeDtypeStruct((nsh, Fs, D), F32)] * 3,
        compiler_params=_params(2),
    )(h, dxo, g, u, dg, du)


NEG_BIG = -1e30


def _causal_pairs(n, by_key):
    pairs = [(qi, ki) for qi in range(n) for ki in range(qi + 1)]
    if by_key:
        pairs.sort(key=lambda p: (p[1], p[0]))
    return jnp.asarray([p[0] for p in pairs], jnp.int32), jnp.asarray([p[1] for p in pairs], jnp.int32)


def _scores(q, k, masked):
    s = lax.dot_general(q, k, (((1,), (1,)), ((), ())), preferred_element_type=F32)
    if masked:
        row = lax.broadcasted_iota(jnp.int32, s.shape, 0)
        col = lax.broadcasted_iota(jnp.int32, s.shape, 1)
        s = jnp.where(row >= col, s, NEG_BIG)
    return s


def attn_fwd(name, q, k, v, t=ATTN_TILE):
    H, S, Dk = q.shape
    Dv = v.shape[1] // H
    qt, kt = _causal_pairs(S // t, by_key=False)

    def body(qt_ref, kt_ref, q_ref, k_ref, v_ref, o_ref, lse_ref, m_sc, l_sc, acc):
        qi, ki = qt_ref[pl.program_id(1)], kt_ref[pl.program_id(1)]

        @pl.when(ki == 0)
        def _():
            m_sc[...] = jnp.full_like(m_sc, NEG_BIG)
            l_sc[...] = jnp.zeros_like(l_sc)
            acc[...] = jnp.zeros_like(acc)

        def step(masked):
            s = _scores(q_ref[...], k_ref[...], masked)
            m_new = jnp.maximum(m_sc[...], jnp.max(s, axis=-1, keepdims=True))
            alpha = jnp.exp(m_sc[...] - m_new)
            p = jnp.exp(s - m_new)
            l_sc[...] = alpha * l_sc[...] + jnp.sum(p, axis=-1, keepdims=True)
            acc[...] = alpha * acc[...] + jnp.dot(p.astype(BF16), v_ref[...].astype(BF16), preferred_element_type=F32)
            m_sc[...] = m_new

        @pl.when(ki < qi)
        def _():
            step(False)

        @pl.when(ki == qi)
        def _():
            step(True)
            o_ref[...] = acc[...] / l_sc[...]
            lse_ref[...] = m_sc[...] + jnp.log(l_sc[...])

    return pl.pallas_call(
        body, name=name,
        grid_spec=pltpu.PrefetchScalarGridSpec(
            num_scalar_prefetch=2, grid=(H, qt.shape[0]),
            in_specs=[pl.BlockSpec((None, t, Dk), lambda h, s, qt, kt: (h, qt[s], 0)),
                      pl.BlockSpec((None, t, Dk), lambda h, s, qt, kt: (h, kt[s], 0)),
                      pl.BlockSpec((t, Dv), lambda h, s, qt, kt: (kt[s], h))],
            out_specs=[pl.BlockSpec((t, Dv), lambda h, s, qt, kt: (qt[s], h)),
                       pl.BlockSpec((None, t, 1), lambda h, s, qt, kt: (h, qt[s], 0))],
            scratch_shapes=[pltpu.VMEM((t, 1), F32), pltpu.VMEM((t, 1), F32), pltpu.VMEM((t, Dv), F32)]),
        out_shape=[jax.ShapeDtypeStruct((S, H * Dv), F32), jax.ShapeDtypeStruct((H, S, 1), F32)],
        compiler_params=_params(2),
    )(qt, kt, q, k, v)


def attn_bwd(name, q, k, v, do, o, lse, t=ATTN_TILE):
    H, S, Dk = q.shape
    Dv = v.shape[1] // H
    qt, kt = _causal_pairs(S // t, by_key=True)
    tn_dims = (((0,), (0,)), ((), ()))

    def body(qt_ref, kt_ref, q_ref, k_ref, v_ref, do_ref, o_ref, lse_ref, dq_ref, dk_ref, dv_ref):
        step_id = pl.program_id(1)
        qi, ki = qt_ref[step_id], kt_ref[step_id]

        @pl.when(step_id == 0)
        def _():
            dq_ref[...] = jnp.zeros_like(dq_ref)

        def step(masked):
            s = _scores(q_ref[...], k_ref[...], masked)
            p = jnp.exp(s - lse_ref[...])
            dov = do_ref[...]
            delta = jnp.sum(dov * o_ref[...], axis=-1, keepdims=True)
            dob = dov.astype(BF16)
            dp = lax.dot_general(dob, v_ref[...].astype(BF16), (((1,), (1,)), ((), ())), preferred_element_type=F32)
            ds = (p * (dp - delta)).astype(BF16)
            pdv = lax.dot_general(p.astype(BF16), dob, tn_dims, preferred_element_type=F32)
            pdk = lax.dot_general(ds, q_ref[...], tn_dims, preferred_element_type=F32)
            rows = pl.ds(pl.multiple_of(qi * t, t), t)
            dq_ref[rows, :] += jnp.dot(ds, k_ref[...], preferred_element_type=F32)
            return pdk, pdv

        @pl.when(ki == qi)
        def _():
            dk_ref[...] = jnp.zeros_like(dk_ref)
            dv_ref[...] = jnp.zeros_like(dv_ref)

        def accumulate(masked):
            pdk, pdv = step(masked)
            dk_ref[...] += pdk
            dv_ref[...] += pdv

        @pl.when(ki == qi)
        def _():
            accumulate(True)

        @pl.when(ki < qi)
        def _():
            accumulate(False)

    qmap = lambda h, s, qt, kt: (h, qt[s], 0)
    kmap = lambda h, s, qt, kt: (h, kt[s], 0)
    qrow = lambda h, s, qt, kt: (qt[s], h)
    krow = lambda h, s, qt, kt: (kt[s], h)
    return pl.pallas_call(
        body, name=name,
        grid_spec=pltpu.PrefetchScalarGridSpec(
            num_scalar_prefetch=2, grid=(H, qt.shape[0]),
            in_specs=[pl.BlockSpec((None, t, Dk), qmap), pl.BlockSpec((None, t, Dk), kmap), pl.BlockSpec((t, Dv), krow),
                      pl.BlockSpec((t, Dv), qrow), pl.BlockSpec((t, Dv), qrow), pl.BlockSpec((None, t, 1), qmap)],
            out_specs=[pl.BlockSpec((None, S, Dk), lambda h, s, qt, kt: (h, 0, 0)), pl.BlockSpec((None, t, Dk), kmap),
                       pl.BlockSpec((t, Dv), krow)]),
        out_shape=[jax.ShapeDtypeStruct((H, S, Dk), F32), jax.ShapeDtypeStruct((H, S, Dk), F32),
                   jax.ShapeDtypeStruct((S, H * Dv), F32)],
        compiler_params=_params(2),
    )(qt, kt, q, k, v, do, o, lse)


def _cmul(ar, ai, br, bi):
    return ar * br - ai * bi, ar * bi + ai * br


def ssm_scan(name, br, bi, a_r8, a_i8, *, reverse=False, states=None):
    S, L = br.shape
    tc, lb = min(SCAN_TIME_TILE, S), SCAN_LANE_TILE
    nt, nblk = S // tc, tc // SUBLANES
    with_da = states is not None

    def body(*refs):
        if with_da:
            br_ref, bi_ref, ar_ref, ai_ref, xr_ref, xi_ref, pr_ref, pi_ref, or_ref, oi_ref, dar_ref, dai_ref, cr_sc, ci_sc = refs
        else:
            br_ref, bi_ref, ar_ref, ai_ref, or_ref, oi_ref, cr_sc, ci_sc = refs
        t = pl.program_id(1)
        chunk = (nt - 1 - t) if reverse else t

        @pl.when(t == 0)
        def _():
            cr_sc[...] = jnp.zeros_like(cr_sc)
            ci_sc[...] = jnp.zeros_like(ci_sc)
            if with_da:
                dar_ref[...] = jnp.zeros_like(dar_ref)
                dai_ref[...] = jnp.zeros_like(dai_ref)

        a1r = ar_ref[0:1, :]
        a1i = -ai_ref[0:1, :] if reverse else ai_ref[0:1, :]
        a2r, a2i = _cmul(a1r, a1i, a1r, a1i)
        a4r, a4i = _cmul(a2r, a2i, a2r, a2i)
        row = lax.broadcasted_iota(jnp.int32, (SUBLANES, lb), 0)
        pw_r, pw_i = jnp.zeros((SUBLANES, lb), F32), jnp.zeros((SUBLANES, lb), F32)
        cur_r, cur_i = a1r, a1i
        for e in range(SUBLANES):
            r_at = (SUBLANES - 1 - e) if reverse else e
            pw_r = jnp.where(row == r_at, cur_r, pw_r)
            pw_i = jnp.where(row == r_at, cur_i, pw_i)
            cur_r, cur_i = _cmul(cur_r, cur_i, a1r, a1i)
        steps = ((1, a1r, a1i), (2, a2r, a2i), (4, a4r, a4i))

        def block(jb, carry):
            if with_da:
                cr, ci, acc_r, acc_i = carry
            else:
                cr, ci = carry
            idx = (nblk - 1 - jb) if reverse else jb
            r0 = pl.multiple_of(idx * SUBLANES, SUBLANES)
            xr = br_ref[pl.ds(r0, SUBLANES), :]
            xi = bi_ref[pl.ds(r0, SUBLANES), :]
            for d, pr, pi in steps:
                if reverse:
                    keep = row < SUBLANES - d
                    sr, si = pltpu.roll(xr, SUBLANES - d, 0), pltpu.roll(xi, SUBLANES - d, 0)
                else:
                    keep = row >= d
                    sr, si = pltpu.roll(xr, d, 0), pltpu.roll(xi, d, 0)
                sr, si = jnp.where(keep, sr, 0.0), jnp.where(keep, si, 0.0)
                xr, xi = xr + pr * sr - pi * si, xi + pr * si + pi * sr
            xr, xi = xr + pw_r * cr - pw_i * ci, xi + pw_r * ci + pw_i * cr
            or_ref[pl.ds(r0, SUBLANES), :] = xr
            oi_ref[pl.ds(r0, SUBLANES), :] = xi
            edge = 0 if reverse else SUBLANES - 1
            cr, ci = xr[edge:edge + 1, :], xi[edge:edge + 1, :]
            if not with_da:
                return cr, ci
            fr = xr_ref[pl.ds(r0, SUBLANES), :]
            fi = xi_ref[pl.ds(r0, SUBLANES), :]
            rp = pl.multiple_of(jnp.maximum(idx - 1, 0) * SUBLANES, SUBLANES)
            inside = idx > 0
            before_r = jnp.where(inside, xr_ref[pl.ds(rp, SUBLANES), :], pr_ref[...])
            before_i = jnp.where(inside, xi_ref[pl.ds(rp, SUBLANES), :], pi_ref[...])
            live = jnp.where(jnp.logical_or(inside, chunk > 0), 1.0, 0.0)
            last_r = before_r[SUBLANES - 1:SUBLANES, :] * live
            last_i = before_i[SUBLANES - 1:SUBLANES, :] * live
            pvr = jnp.where(row == 0, last_r, pltpu.roll(fr, 1, 0))
            pvi = jnp.where(row == 0, last_i, pltpu.roll(fi, 1, 0))
            acc_r = acc_r + xr * pvr + xi * pvi
            acc_i = acc_i + xi * pvr - xr * pvi
            return cr, ci, acc_r, acc_i

        init = (cr_sc[...], ci_sc[...])
        if with_da:
            init = init + (jnp.zeros((SUBLANES, lb), F32), jnp.zeros((SUBLANES, lb), F32))
        fin = lax.fori_loop(0, nblk, block, init)
        cr_sc[...] = fin[0]
        ci_sc[...] = fin[1]
        if with_da:
            dar_ref[...] += fin[2]
            dai_ref[...] += fin[3]

    tmap = (lambda l, t: (nt - 1 - t, l)) if reverse else (lambda l, t: (t, l))
    seq = pl.BlockSpec((tc, lb), tmap)
    par = pl.BlockSpec((SUBLANES, lb), lambda l, t: (0, l))
    in_specs, args = [seq, seq, par, par], [br, bi, a_r8, a_i8]
    out_specs = [seq, seq]
    out_shape = [jax.ShapeDtypeStruct((S, L), F32), jax.ShapeDtypeStruct((S, L), F32)]
    if with_da:
        pmap = lambda l, t: (jnp.maximum((nt - 1 - t) * (tc // SUBLANES) - 1, 0), l)
        prev = pl.BlockSpec((SUBLANES, lb), pmap)
        in_specs += [seq, seq, prev, prev]
        args += [states[0], states[1], states[0], states[1]]
        out_specs += [par, par]
        out_shape += [jax.ShapeDtypeStruct((SUBLANES, L), F32), jax.ShapeDtypeStruct((SUBLANES, L), F32)]
    return pl.pallas_call(
        body, name=name, grid=(L // lb, nt), in_specs=in_specs, out_specs=out_specs, out_shape=out_shape,
        scratch_shapes=[pltpu.VMEM((1, lb), F32), pltpu.VMEM((1, lb), F32)], compiler_params=_params(2),
    )(*args)


SSM_BLOCKS = 4
BLOCK_CH = SSM_WIDTH // SSM_BLOCKS
PREP_LANES = SSM_LANES // SSM_BLOCKS


def _ssm_prep_f(a_re, a_im, log_dt, bt_re, bt_im, c_re, c_im):
    first_group = pl.program_id(0) * (SSM_GROUPS // SSM_BLOCKS)
    iota = lambda shape, d: lax.broadcasted_iota(jnp.int32, shape, d)
    grp_of_row = lambda shape: iota(shape, 0) >> int(math.log2(SSM_GROUP))
    grp_of_lane = lambda shape: iota(shape, 1) >> int(math.log2(SSM_STATE))
    rep = (grp_of_row((BLOCK_CH, SSM_GROUPS)) + first_group == iota((BLOCK_CH, SSM_GROUPS), 1)).astype(F32)
    til = ((iota((SSM_STATE, PREP_LANES), 1) & (SSM_STATE - 1)) == iota((SSM_STATE, PREP_LANES), 0)).astype(F32)
    m_rows = (grp_of_row((BLOCK_CH, PREP_LANES)) == grp_of_lane((BLOCK_CH, PREP_LANES))).astype(F32)
    m_grp = (iota((SSM_GROUPS, PREP_LANES), 0) == grp_of_lane((SSM_GROUPS, PREP_LANES)) + first_group).astype(F32)
    dt = jnp.exp(log_dt)
    decay = jnp.exp(a_re * dt)
    ar = decay * jnp.cos(a_im * dt)
    ai = decay * jnp.sin(a_im * dt)
    den = a_re * a_re + a_im * a_im
    nr = ar - 1.0
    coef_r = (nr * a_re + ai * a_im) / den
    coef_i = (ai * a_re - nr * a_im) / den
    cr, ci = _hp_dot(rep, coef_r), _hp_dot(rep, coef_i)
    bb_r = cr * bt_re - ci * bt_im
    bb_i = cr * bt_im + ci * bt_re
    big = lambda m: _hp_dot(m, til) * m_rows
    lanes = lambda m: jnp.broadcast_to(jnp.sum(_hp_dot(m, til) * m_grp, axis=0, keepdims=True), (SUBLANES, PREP_LANES))
    return lanes(ar), lanes(ai), big(bb_r), big(bb_i), big(c_re), -big(c_im)


def _whole(arr, **kw):
    nd = arr.ndim
    return In(arr, arr.shape, lambda *_: (0,) * nd, **kw)


def loss_grad(name, y, target, tm=TOKEN_TILE):
    S, D = y.shape

    def body(y_ref, t_ref, dy_ref, part_ref):
        err = y_ref[...] - t_ref[...]
        dy_ref[...] = err * (1.0 / D)
        part_ref[...] = jnp.full(part_ref.shape, 0.5 * jnp.sum(jnp.mean(err * err, axis=-1)), F32)

    return pl.pallas_call(
        body, name=name, grid=(S // tm,),
        in_specs=[pl.BlockSpec((tm, D), lambda i: (i, 0)), pl.BlockSpec((tm, D), lambda i: (i, 0))],
        out_specs=[pl.BlockSpec((tm, D), lambda i: (i, 0)), pl.BlockSpec((SUBLANES, 128), lambda i: (i, 0))],
        out_shape=[jax.ShapeDtypeStruct((S, D), F32), jax.ShapeDtypeStruct((S // tm * SUBLANES, 128), F32)],
        compiler_params=_params(1),
    )(y, target)


def _adamw_math(w, g, m, v):
    m = ADAM_B1 * m + (1.0 - ADAM_B1) * g
    v = ADAM_B2 * v + (1.0 - ADAM_B2) * (g * g)
    m_hat = m / (1.0 - ADAM_B1 ** ADAM_STEP)
    v_hat = v / (1.0 - ADAM_B2 ** ADAM_STEP)
    delta = -ADAM_LR * (m_hat / (jnp.sqrt(v_hat) + ADAM_EPS) + ADAM_WD * w)
    return delta, m, v


def adamw_big(name, w, g, m, v):
    R, C = w.shape
    tr = R
    for cand in (512, 344, 256, 128):
        if R % cand == 0:
            tr = cand
            break

    def body(w_ref, g_ref, m_ref, v_ref, d_ref, nm_ref, nv_ref):
        d, nm, nv = _adamw_math(w_ref[...], g_ref[...], m_ref[...], v_ref[...])
        d_ref[...] = d
        nm_ref[...] = nm
        nv_ref[...] = nv

    spec = pl.BlockSpec((tr, C), lambda i: (i, 0))
    return pl.pallas_call(
        body, name=name, grid=(R // tr,), in_specs=[spec] * 4, out_specs=[spec] * 3,
        out_shape=[jax.ShapeDtypeStruct((R, C), F32)] * 3, compiler_params=_params(1),
    )(w, g, m, v)


def adamw_small(name, ws, gs, ms, vs):
    n = len(ws)

    def body(*refs):
        for k in range(n):
            d, nm, nv = _adamw_math(refs[k][...], refs[n + k][...], refs[2 * n + k][...], refs[3 * n + k][...])
            refs[4 * n + k][...] = d
            refs[5 * n + k][...] = nm
            refs[6 * n + k][...] = nv

    vm = pl.BlockSpec(memory_space=pltpu.VMEM)
    shapes = [jax.ShapeDtypeStruct(w.shape, F32) for w in ws]
    res = pl.pallas_call(
        body, name=name, in_specs=[vm] * (4 * n), out_specs=[vm] * (3 * n), out_shape=shapes * 3,
        compiler_params=pltpu.CompilerParams(vmem_limit_bytes=VMEM_LIMIT_BYTES),
    )(*ws, *gs, *ms, *vs)
    return res[:n], res[n:2 * n], res[2 * n:]


def _place():
    return lax.axis_index("x"), lax.axis_index("y"), lax.axis_index("c")


def _other_chips(x, y):
    return [(1 - x, y), (x, 1 - y), (1 - x, 1 - y)]


HBM = pl.BlockSpec(memory_space=pl.ANY)


def all_gather_halves(name, blocks):
    n = len(blocks)

    def body(*refs):
        in_refs, out_refs = refs[:n], refs[n:2 * n]
        send_sems, recv_sems = refs[2 * n:]
        x, y, c = _place()
        me, sibling = (x, y, c), (x, y, 1 - c)
        chips = _other_chips(x, y)

        def slot(a, px, py, pc):
            return out_refs[a].at[4 * px + 2 * py + pc]

        def copy(a, k, block, to, src=None):
            return pltpu.make_async_remote_copy(
                src_ref=slot(a, *block) if src is None else src, dst_ref=slot(a, *block),
                send_sem=send_sems.at[a, k], recv_sem=recv_sems.at[a, k], device_id=to, device_id_type=MESH)

        first = []
        for a in range(n):
            first.append(copy(a, 0, me, sibling, src=in_refs[a]))
            first += [copy(a, 1 + j, me, (*chip, c), src=in_refs[a]) for j, chip in enumerate(chips)]
        for cp in first:
            cp.start()
        passed = []
        for j, chip in enumerate(chips):
            for a in range(n):
                copy(a, 1 + j, (*chip, c), me).wait_recv()
                fw = copy(a, 4 + j, (*chip, c), sibling)
                fw.start()
                passed.append(fw)
        for a in range(n):
            copy(a, 0, sibling, me).wait_recv()
            for j, chip in enumerate(chips):
                copy(a, 4 + j, (*chip, 1 - c), me).wait_recv()
        for cp in first + passed:
            cp.wait_send()

    res = pl.pallas_call(
        body, name=name, in_specs=[HBM] * n, out_specs=[HBM] * n,
        out_shape=[jax.ShapeDtypeStruct((N_DEV,) + b.shape, b.dtype) for b in blocks],
        scratch_shapes=[pltpu.SemaphoreType.DMA((n, 7)), pltpu.SemaphoreType.DMA((n, 7))],
    )(*blocks)
    return list(res)


def pair_send_halves(name, grads):
    n = len(grads)

    def body(*refs):
        in_refs, out_refs = refs[:n], refs[n:2 * n]
        send_sems, recv_sems = refs[2 * n:]
        x, y, c = _place()
        cps = []
        for a in range(n):
            cp = pltpu.make_async_remote_copy(
                src_ref=in_refs[a].at[:, 1 - c], dst_ref=out_refs[a], send_sem=send_sems.at[a], recv_sem=recv_sems.at[a],
                device_id=(x, y, 1 - c), device_id_type=MESH)
            cp.start()
            cps.append(cp)
        for cp in cps:
            cp.wait()

    res = pl.pallas_call(
        body, name=name, in_specs=[HBM] * n, out_specs=[HBM] * n,
        out_shape=[jax.ShapeDtypeStruct((g.shape[0],) + g.shape[2:], g.dtype) for g in grads],
        scratch_shapes=[pltpu.SemaphoreType.DMA((n,)), pltpu.SemaphoreType.DMA((n,))],
    )(*grads)
    return list(res)


def pair_add(name, grad, got, c_arr):
    nsh, _, M, N = grad.shape
    tr = M
    for cand in (512, 256, 192, 128, 64, 16):
        if M % cand == 0:
            tr = cand
            break

    def body(c_ref, g_ref, p_ref, o_ref):
        o_ref[...] = (g_ref[...] + p_ref[...]).astype(BF16)

    return pl.pallas_call(
        body, name=name,
        grid_spec=pltpu.PrefetchScalarGridSpec(
            num_scalar_prefetch=1, grid=(nsh, M // tr),
            in_specs=[pl.BlockSpec((None, None, tr, N), lambda j, i, c_ref: (j, c_ref[0], i, 0)),
                      pl.BlockSpec((None, tr, N), lambda j, i, c_ref: (j, i, 0))],
            out_specs=pl.BlockSpec((None, tr, N), lambda j, i, c_ref: (j, i, 0))),
        out_shape=jax.ShapeDtypeStruct((nsh, M, N), BF16), compiler_params=_params(2),
    )(c_arr, grad, got)


def scatter_to_chips(name, parts):
    n = len(parts)

    def body(*refs):
        in_refs, out_refs = refs[:n], refs[n:2 * n]
        send_sems, recv_sems = refs[2 * n:]
        x, y, c = _place()
        mine = 2 * x + y
        chips = _other_chips(x, y)
        cps = []
        for a in range(n):
            for k, (px, py) in enumerate(chips):
                cp = pltpu.make_async_remote_copy(
                    src_ref=in_refs[a].at[2 * px + py], dst_ref=out_refs[a].at[mine],
                    send_sem=send_sems.at[a, k], recv_sem=recv_sems.at[a, k], device_id=(px, py, c), device_id_type=MESH)
                cp.start()
                cps.append((cp, a, k, px, py))
        for cp, a, k, px, py in cps:
            pltpu.make_async_remote_copy(
                src_ref=in_refs[a].at[mine], dst_ref=out_refs[a].at[2 * px + py],
                send_sem=send_sems.at[a, k], recv_sem=recv_sems.at[a, k], device_id=(px, py, c), device_id_type=MESH).wait_recv()
        for cp, *_ in cps:
            cp.wait_send()

    res = pl.pallas_call(
        body, name=name, in_specs=[HBM] * n, out_specs=[HBM] * n,
        out_shape=[jax.ShapeDtypeStruct(p.shape, p.dtype) for p in parts],
        scratch_shapes=[pltpu.SemaphoreType.DMA((n, 3)), pltpu.SemaphoreType.DMA((n, 3))],
    )(*parts)
    return list(res)


def sum_chips(name, q):
    nsh, M, N = q.shape
    tr = M
    for cand in (512, 256, 192, 128, 64, 16):
        if M % cand == 0:
            tr = cand
            break

    def body(q_ref, o_ref):
        acc = q_ref[0].astype(F32)
        for j in range(1, nsh):
            acc = acc + q_ref[j].astype(F32)
        o_ref[...] = acc

    return pl.pallas_call(
        body, name=name, grid=(M // tr,), in_specs=[pl.BlockSpec((nsh, tr, N), lambda i: (0, i, 0))],
        out_specs=pl.BlockSpec((tr, N), lambda i: (i, 0)), out_shape=jax.ShapeDtypeStruct((M, N), F32),
        compiler_params=_params(1),
    )(q)


def pair_exchange(name, halves):
    n = len(halves)

    def body(*refs):
        in_refs, out_refs = refs[:n], refs[n:2 * n]
        send_sems, recv_sems = refs[2 * n:]
        x, y, c = _place()
        cps = []
        for a in range(n):
            cp = pltpu.make_async_remote_copy(
                src_ref=in_refs[a], dst_ref=out_refs[a], send_sem=send_sems.at[a], recv_sem=recv_sems.at[a],
                device_id=(x, y, 1 - c), device_id_type=MESH)
            cp.start()
            cps.append(cp)
        for cp in cps:
            cp.wait()

    res = pl.pallas_call(
        body, name=name, in_specs=[HBM] * n, out_specs=[HBM] * n,
        out_shape=[jax.ShapeDtypeStruct(h.shape, h.dtype) for h in halves],
        scratch_shapes=[pltpu.SemaphoreType.DMA((n,)), pltpu.SemaphoreType.DMA((n,))],
    )(*halves)
    return list(res)


SEM = pl.BlockSpec(memory_space=pltpu.SEMAPHORE)
IN_HBM = pl.BlockSpec(memory_space=pltpu.HBM)
SPLIT_COPY = pltpu.CompilerParams(has_side_effects=pltpu.SideEffectType.DATAFLOW_SIDE_EFFECTING)


def _scatter_copies(src_refs, dst_refs, send_sems, recv_sems):
    x, y, c = _place()
    mine = 2 * x + y
    return [pltpu.make_async_remote_copy(
        src_ref=src_refs[a].at[2 * px + py], dst_ref=dst_refs[a].at[mine], send_sem=send_sems.at[a * (N_CHIPS - 1) + k],
        recv_sem=recv_sems.at[a * (N_CHIPS - 1) + k], device_id=(px, py, c), device_id_type=MESH)
        for a in range(len(src_refs)) for k, (px, py) in enumerate(_other_chips(x, y))]


def _gather_copies(src_refs, dst_refs, send_sems, recv_sems):
    x, y, c = _place()
    me = 4 * x + 2 * y + c
    cps = []
    for a in range(len(src_refs)):
        for k in range(1, N_DEV):
            to = (1 - x if k & 4 else x, 1 - y if k & 2 else y, 1 - c if k & 1 else c)
            s = a * (N_DEV - 1) + k - 1
            cps.append(pltpu.make_async_remote_copy(
                src_ref=src_refs[a], dst_ref=dst_refs[a].at[me], send_sem=send_sems.at[s], recv_sem=recv_sems.at[s],
                device_id=to, device_id_type=MESH))
    return cps


def split_copy_start(name, copies, n_sem, srcs, land_shapes):
    n = len(srcs)
    lands = [lax.empty(s.shape, s.dtype) for s in land_shapes]

    def body(*refs):
        for cp in copies(refs[:n], refs[n:2 * n], refs[2 * n], refs[2 * n + 1]):
            cp.start()
        refs[-1][...] = jnp.zeros_like(refs[-1])

    thru = [pltpu.HBM(a.shape, a.dtype) for a in (*srcs, *lands)]
    res = pl.pallas_call(
        body, name=name, in_specs=[IN_HBM] * (2 * n),
        out_specs=(SEM, SEM, *[IN_HBM] * (2 * n), pl.BlockSpec(memory_space=pltpu.VMEM)),
        out_shape=(pltpu.SemaphoreType.DMA((n * n_sem,)), pltpu.SemaphoreType.DMA((n * n_sem,)), *thru,
                   jax.ShapeDtypeStruct((SUBLANES, LANES), F32)),
        input_output_aliases={i: 2 + i for i in range(2 * n)}, compiler_params=SPLIT_COPY,
    )(*[pltpu.with_memory_space_constraint(a, pltpu.HBM) for a in (*srcs, *lands)])
    return (copies, n, res[0], res[1], res[2:2 + 2 * n]), res[-1][0, 0]


def split_copy_wait(name, handle, after):
    copies, n, send_sems, recv_sems, thru = handle

    def body(*refs):
        for cp in copies(refs[:n], refs[n:2 * n], refs[2 * n], refs[2 * n + 1]):
            cp.wait_send()
            cp.wait_recv()

    res = pl.pallas_call(
        body, name=name, in_specs=[IN_HBM] * (2 * n) + [SEM, SEM, pl.BlockSpec(memory_space=pl.ANY)],
        out_specs=[IN_HBM] * (2 * n), out_shape=[pltpu.HBM(a.shape, a.dtype) for a in thru],
        input_output_aliases={i: i for i in range(2 * n)}, compiler_params=SPLIT_COPY,
    )(*thru, send_sems, recv_sems, after)
    return list(res[n:])


def all_reduce_small(name, v):
    R, C = v.shape

    def body(v_ref, o_ref, gath, send_sems, recv_sems):
        x, y, c = _place()
        me, sibling = (x, y, c), (x, y, 1 - c)
        chips = _other_chips(x, y)

        def slot(px, py, pc):
            return gath.at[4 * px + 2 * py + pc]

        def copy(k, block, to, src=None):
            return pltpu.make_async_remote_copy(
                src_ref=slot(*block) if src is None else src, dst_ref=slot(*block),
                send_sem=send_sems.at[k], recv_sem=recv_sems.at[k], device_id=to, device_id_type=MESH)

        first = [copy(0, me, sibling, src=v_ref)]
        first += [copy(1 + j, me, (*chip, c), src=v_ref) for j, chip in enumerate(chips)]
        for cp in first:
            cp.start()
        slot(*me)[...] = v_ref[...]
        passed = [copy(4 + j, (*chip, c), sibling) for j, chip in enumerate(chips)]
        for j, chip in enumerate(chips):
            copy(1 + j, (*chip, c), me).wait_recv()
            passed[j].start()
        copy(0, sibling, me).wait_recv()
        for j, chip in enumerate(chips):
            copy(4 + j, (*chip, 1 - c), me).wait_recv()
        for cp in first + passed:
            cp.wait_send()
        acc = gath[0]
        for d in range(1, N_DEV):
            acc = acc + gath[d]
        o_ref[...] = acc

    vm = pl.BlockSpec(memory_space=pltpu.VMEM)
    return pl.pallas_call(
        body, name=name, in_specs=[vm], out_specs=vm, out_shape=jax.ShapeDtypeStruct((R, C), F32),
        scratch_shapes=[pltpu.VMEM((N_DEV, R, C), F32), pltpu.SemaphoreType.DMA((7,)), pltpu.SemaphoreType.DMA((7,))],
        compiler_params=pltpu.CompilerParams(vmem_limit_bytes=VMEM_LIMIT_BYTES),
    )(v)


LANES = 128
PACK_ROW_MULTIPLE = 1024
SMALL_SHARDED = {
    "w_in": ((D_MODEL, 1216), 1), "mla_w_uq": ((MLA_Q_RANK, 768), 1), "mla_w_ukv": ((MLA_KV_RANK, 1024), 1),
    "ssm_w_glu": ((SSM_WIDTH, SSM_WIDTH), 0), "w_o": ((D_MODEL, D_MODEL), 0), "xattn_w_q": ((D_MODEL, 512), 0),
    "xattn_w_kv": ((D_MODEL, 1024), 0), "xattn_w_o": ((512, D_MODEL), 1),
}
FFN_NAMES = ["ffn1_w_gate", "ffn1_w_up", "ffn1_w_down", "ffn2_w_gate", "ffn2_w_up", "ffn2_w_down"]
TRANSPOSED_VIEW = ("ffn1_w_gate", "ffn1_w_up", "ffn2_w_gate", "ffn2_w_up", "w_in", "mla_w_uq")


def _shard_shape(name):
    (r, cdim), ax = SMALL_SHARDED[name]
    return (r // N_CHIPS, cdim) if ax == 0 else (r, cdim // N_CHIPS)


def _pack_shards(shards):
    parts = []
    for name in SMALL_SHARDED:
        a = shards[name]
        lead = a.shape[:-2]
        parts.append(a.reshape(lead + (a.shape[-2] * a.shape[-1] // LANES, LANES)))
    rows = sum(q.shape[-2] for q in parts)
    parts.append(jnp.zeros(lead + (-rows % PACK_ROW_MULTIPLE, LANES), parts[0].dtype))
    return jnp.concatenate(parts, axis=-2)


def _unpack_shards(packed):
    out, r0 = {}, 0
    lead = packed.shape[:-2]
    for name in SMALL_SHARDED:
        r, cdim = _shard_shape(name)
        rows = r * cdim // LANES
        out[name] = packed[..., r0:r0 + rows, :].reshape(lead + (r, cdim))
        r0 += rows
    return out


def _full_from_shards(name, sh):
    (r, cdim), ax = SMALL_SHARDED[name]
    if ax == 0:
        return sh.reshape(r, cdim)
    return jnp.transpose(sh, (1, 0, 2)).reshape(r, cdim)


def _shards_from_full(name, full):
    (r, cdim), ax = SMALL_SHARDED[name]
    if ax == 0:
        return full.reshape(N_CHIPS, r // N_CHIPS, cdim)
    return jnp.transpose(full.reshape(r, N_CHIPS, cdim // N_CHIPS), (1, 0, 2))


SMALL_REPL = {
    "ffn1_norm": (1, 1024), "mix_norm": (1, 1024), "mla_q_norm": (1, 384), "mla_kv_norm": (1, 256),
    "mla_qk_norm_q": (1, 192), "mla_qk_norm_k": (1, 192), "ssm_a_re": (32, 64), "ssm_a_im": (32, 64),
    "ssm_log_dt": (32, 1), "ssm_b_re": (32, 64, 16), "ssm_b_im": (32, 64, 16), "ssm_c_re": (32, 16, 64),
    "ssm_c_im": (32, 16, 64), "ssm_d": (1, 512), "ssm_b_glu": (1, 512), "out_norm_mla": (1, 512),
    "out_norm_ssm": (1, 512), "xattn_norm": (1, 1024), "mem_norm": (1, 1024), "xattn_q_norm": (1, 128),
    "xattn_k_norm": (1, 128), "ffn2_norm": (1, 1024),
}


def _pack_repl(grads):
    flat = jnp.concatenate([grads[n].reshape(-1) for n in SMALL_REPL])
    rows = -(-flat.shape[0] // (LANES * SUBLANES)) * SUBLANES
    return jnp.pad(flat, (0, rows * LANES - flat.shape[0])).reshape(rows, LANES)


def _unpack_repl(packed):
    flat, out, o = packed.reshape(-1), {}, 0
    for n, shp in SMALL_REPL.items():
        size = int(np.prod(shp))
        out[n] = flat[o:o + size].reshape(shp)
        o += size
    return out


def _rope_tables(positions):
    half = MLA_ROPE // 2
    inv = ROPE_THETA ** (-jnp.arange(half, dtype=F32) / half)
    ang = positions.astype(F32)[:, None] * inv[None, :]
    cos, sin = jnp.cos(ang), jnp.sin(ang)
    S = positions.shape[0]
    z = lambda w: jnp.zeros((S, w), F32)
    keep = jnp.concatenate([jnp.ones((S, MLA_NOPE), F32), cos, cos, z(HEAD_PAD - MLA_QK)], axis=1)
    from_hi = jnp.concatenate([z(MLA_NOPE), -sin, z(HEAD_PAD - MLA_NOPE - half)], axis=1)
    from_lo = jnp.concatenate([z(MLA_NOPE + half), sin, z(HEAD_PAD - MLA_QK)], axis=1)
    return keep, from_hi, from_lo


def _norm_rope(x, g, keep, from_hi, from_lo):
    y = x * lax.rsqrt(jnp.sum(x * x, axis=-1, keepdims=True) * (1.0 / MLA_QK) + EPS) * g
    half = MLA_ROPE // 2
    return y * keep + _lane_roll(y, HEAD_PAD - half) * from_hi + _lane_roll(y, half) * from_lo


def local_step(x, mem, positions, target, w, wb, small_weights=None, ffn2_weights=None, on_grads=None):
    if small_weights is None:
        small_weights = lambda after: {}
    if ffn2_weights is None:
        ffn2_weights = lambda after: [wb[k] for k in FFN_NAMES[3:]]
    if on_grads is None:
        on_grads = lambda tag, g: 0.0
    S = x.shape[0]
    tm = min(TOKEN_TILE, S)
    nt = S // tm
    g1 = (nt,)
    gh = (MLA_HEADS, nt)
    tile = lambda arr, cols=None, cb=0, **kw: In(arr, (tm, arr.shape[1] if cols is None else cols), lambda i, cb=cb: (i, cb), **kw)
    par = lambda arr, **kw: In(arr, arr.shape, lambda *_: (0, 0), diff=True, **kw)
    otile = lambda cols, dt: Out((S, cols), dt, (tm, cols), lambda i: (i, 0))
    grads = {}

    x1, h1, g_1, u_1 = ffn_fwd("ffn1_fwd", x, w["ffn1_norm"], wb["ffn1_w_gate"], wb["ffn1_w_up"], wb["ffn1_w_down"])

    f_norm = lambda xv, g: (_rms(xv, g),)
    mix_ins = [tile(x1, diff=True), par(w["mix_norm"], acc=(0,))]
    mix_outs = [otile(D_MODEL, BF16)]
    (h2,) = seg_fwd("mix_norm_fwd", f_norm, g1, mix_ins, mix_outs)
    wb = {**wb, **small_weights(h2)}

    pq, pkv, pkr, pu = mm("w_in_fwd", [h2], [[wb["w_in_q"], wb["w_in_kv"], wb["w_in_kr"], wb["w_in_u"]]], [F32] * 4)

    f_lat = lambda a, b, ga, gb: (_rms(a, ga), _rms(b, gb))
    lat_ins = [tile(pq, diff=True), tile(pkv, diff=True), par(w["mla_q_norm"], acc=(0,)), par(w["mla_kv_norm"], acc=(0,))]
    lat_outs = [otile(MLA_Q_RANK, BF16), otile(MLA_KV_RANK, BF16)]
    cq, ckv = seg_fwd("latent_norm_fwd", f_lat, g1, lat_ins, lat_outs)

    (q0,) = mm("w_uq_fwd", [cq], [[wb["w_uq"]]], [F32])
    kn0, v0 = mm("w_ukv_fwd", [ckv], [[wb["w_ukv_k"], wb["w_ukv_v"]]], [F32, F32])

    keep, from_hi, from_lo = _rope_tables(positions)

    scale = MLA_QK ** -0.5

    def f_qk(qh, knh, kr, kp, fh, fl, gq, gk):
        return (_norm_rope(qh, gq, kp, fh, fl) * scale, _norm_rope(jnp.concatenate([knh, kr], axis=-1), gk, kp, fh, fl))

    hmap = lambda h, i: (i, h)
    tmap0 = lambda h, i: (i, 0)
    qk_ins = [In(q0, (tm, HEAD_PAD), hmap, diff=True), In(kn0, (tm, MLA_NOPE), hmap, diff=True),
              In(pkr, (tm, LANES), tmap0, diff=True,
                 grad=((MLA_HEADS, S, LANES), (None, tm, LANES), lambda h, i: (h, i, 0))),
              In(keep, (tm, HEAD_PAD), tmap0), In(from_hi, (tm, HEAD_PAD), tmap0), In(from_lo, (tm, HEAD_PAD), tmap0),
              In(w["qk_gain_q"], (1, HEAD_PAD), lambda h, i: (0, 0), diff=True, acc=(0, 1)),
              In(w["qk_gain_k"], (1, HEAD_PAD), lambda h, i: (0, 0), diff=True, acc=(0, 1))]
    head_out = Out((MLA_HEADS, S, HEAD_PAD), BF16, (None, tm, HEAD_PAD), lambda h, i: (h, i, 0))
    qh, kh = seg_fwd("qk_norm_rope_fwd", f_qk, gh, qk_ins, [head_out, head_out])

    o_mla, lse = attn_fwd("mla_attn_fwd", qh, kh, v0, t=min(ATTN_TILE, S))

    prep_grid = (SSM_BLOCKS,)
    prep_ins = ([_whole(w[k], diff=True, acc=(0,)) for k in ("ssm_a_re", "ssm_a_im", "ssm_log_dt")]
                + [In(w[k], (BLOCK_CH, SSM_STATE), lambda i: (i, 0), diff=True)
                   for k in ("ssm_bt_re", "ssm_bt_im", "ssm_c2_re", "ssm_c2_im")])
    blk_out = Out((SSM_BLOCKS, BLOCK_CH, PREP_LANES), BF16, (None, BLOCK_CH, PREP_LANES), lambda i: (i, 0, 0))
    prep_outs = [Out((SUBLANES, SSM_LANES), F32, (SUBLANES, PREP_LANES), lambda i: (0, i))] * 2 + [blk_out] * 4
    a_r8, a_i8, bb_r, bb_i, cb_r, ncb_i = seg_fwd("ssm_prep_fwd", _ssm_prep_f, prep_grid, prep_ins, prep_outs)

    bu_r, bu_i = bd_mm("ssm_bu_fwd", pu, [bb_r, bb_i], [F32, F32])
    xs_r, xs_i = ssm_scan("ssm_scan_fwd", bu_r, bu_i, a_r8, a_i8)
    y_lin = bd_mm_t("ssm_cx_fwd", [xs_r, xs_i], [cb_r, ncb_i])

    f_gelu = lambda yl, u, d: (jax.nn.gelu(yl + d * u),)
    gelu_ins = [tile(y_lin, diff=True), tile(pu, diff=True), par(w["ssm_d"], acc=(0,))]
    (gl,) = seg_fwd("ssm_gelu_fwd", f_gelu, g1, gelu_ins, [otile(SSM_WIDTH, F32)])
    (z,) = mm("ssm_glu_fwd", [gl], [[wb["ssm_w_glu"]]], [F32])

    def f_glu(g, zz, o, b, gm, gs):
        return (_rms(o, gm), _rms(g * jax.nn.sigmoid(zz + b), gs))

    glu_ins = [tile(gl, diff=True), tile(z, diff=True), tile(o_mla, diff=True), par(w["ssm_b_glu"], acc=(0,)),
               par(w["out_norm_mla"], acc=(0,)), par(w["out_norm_ssm"], acc=(0,))]
    glu_outs = [otile(SSM_WIDTH, BF16), otile(SSM_WIDTH, BF16)]
    yn_mla, yn_ssm = seg_fwd("glu_out_norm_fwd", f_glu, g1, glu_ins, glu_outs)

    (x2,) = mm("w_o_fwd", [yn_mla, yn_ssm], [[wb["w_o_mla"]], [wb["w_o_ssm"]]], [F32], adds=[x1])

    xn_ins = [tile(x2, diff=True), par(w["xattn_norm"], acc=(0,))]
    (h3,) = seg_fwd("xattn_norm_fwd", f_norm, g1, xn_ins, [otile(D_MODEL, BF16)])
    (xq0,) = mm("xattn_q_fwd", [h3], [[wb["xattn_w_q"]]], [F32])

    M = mem.shape[0]
    mem_ins = [In(mem, (M, D_MODEL), lambda i: (0, 0)), par(w["mem_norm"], acc=(0,))]
    (mn,) = seg_fwd("mem_norm_fwd", f_norm, (1,), mem_ins, [Out((M, D_MODEL), BF16, (M, D_MODEL), lambda i: (0, 0))])
    (kvm,) = mm("xattn_kv_fwd", [mn], [[wb["xattn_w_kv"]]], [F32])

    xscale = XHD ** -0.5

    def f_xattn(qv, kv_k, kv_v, gq, gk):
        qn, kn = _rms(qv, gq), _rms(kv_k, gk)
        s = _bdot_nt(qn, kn) * xscale
        p = jax.nn.softmax(s, axis=-1)
        return (_bdot_nn(p, kv_v),)

    def xa_ins(gain_q):
        return [In(xq0, (tm, XHD), hmap, diff=True),
                In(kvm, (M, XHD), lambda h, i: (0, h), diff=True, acc=(1,), grad=((M, XH * XHD), (M, XHD), lambda h, i: (0, h))),
                In(kvm, (M, XHD), lambda h, i: (0, XH + h), diff=True, acc=(1,), grad=((M, XH * XHD), (M, XHD), lambda h, i: (0, h))),
                In(gain_q, (1, XHD), lambda h, i: (0, 0), diff=True, acc=(0, 1)),
                In(w["xattn_k_norm"], (1, XHD), lambda h, i: (0, 0), diff=True, acc=(0, 1))]

    xa_outs = [Out((S, XH * XHD), F32, (tm, XHD), hmap)]
    (o2,) = seg_fwd("xattn_fwd", f_xattn, gh, xa_ins(w["xattn_q_norm"]), xa_outs)

    (x3,) = mm("xattn_o_fwd", [o2], [[wb["xattn_w_o"]]], [F32], adds=[x2])

    wg2, wu2, wd2 = ffn2_weights(x3)
    x4, h4, g_2, u_2 = ffn_fwd("ffn2_fwd", x3, w["ffn2_norm"], wg2, wu2, wd2)
    dx4, parts = loss_grad("loss", x4, target)
    loss = jnp.sum(parts[::SUBLANES, 0])

    dx3, grads["ffn2_norm"], dg_2, du_2 = ffn_bwd_act("ffn2_bwd_act", dx4, x3, w["ffn2_norm"], g_2, u_2, wg2, wu2, wd2)
    grads["ffn2_w_gate"], grads["ffn2_w_up"], grads["ffn2_w_down"] = ffn_bwd_w("ffn2_bwd_w", h4, dx4, g_2, u_2, dg_2, du_2)
    sent = on_grads("ffn2", grads)

    (do2,) = mm("xattn_o_bwd", [dx3], [[wb["xattn_w_o"]]], [F32], trans=True)
    (grads["xattn_w_o"],) = mm_tn("xattn_o_bwd_w", [o2], [dx3], [(0, [0])])

    dxq0, dkm, dvm, grads["xattn_q_norm"], grads["xattn_k_norm"] = seg_bwd(
        "xattn_bwd", f_xattn, gh, xa_ins(w["xattn_q_norm"] + sent), xa_outs, [do2])
    (dh3,) = mm("xattn_q_bwd", [dxq0], [[wb["xattn_w_q"]]], [F32], trans=True)
    (grads["xattn_w_q"],) = mm_tn("xattn_q_bwd_w", [h3], [dxq0], [(0, [0])])
    wkv_k, wkv_v = wb["xattn_w_kv"][:, :XH * XHD], wb["xattn_w_kv"][:, XH * XHD:]
    (dmn,) = mm("xattn_kv_bwd", [dkm, dvm], [[wkv_k], [wkv_v]], [F32], trans=True)
    gk_w, gv_w = mm_tn("xattn_kv_bwd_w", [mn], [dkm, dvm], [(0, [0]), (0, [1])])
    grads["xattn_w_kv"] = jnp.concatenate([gk_w, gv_w], axis=1)
    (grads["mem_norm"],) = seg_bwd("mem_norm_bwd", f_norm, (1,), mem_ins,
                                   [Out((M, D_MODEL), BF16, (M, D_MODEL), lambda i: (0, 0))], [dmn])

    dx2, grads["xattn_norm"] = seg_bwd("xattn_norm_bwd", f_norm, g1, xn_ins, [otile(D_MODEL, BF16)], [dh3], adds={0: dx3})

    dyn_mla, dyn_ssm = mm("w_o_bwd", [dx2], [[wb["w_o_mla"], wb["w_o_ssm"]]], [F32, F32], trans=True)
    go_mla, go_ssm = mm_tn("w_o_bwd_w", [yn_mla, yn_ssm], [dx2], [(0, [0]), (1, [0])])
    grads["w_o"] = jnp.concatenate([go_mla, go_ssm], axis=0)

    dgl_a, dz, do_mla, grads["ssm_b_glu"], grads["out_norm_mla"], grads["out_norm_ssm"] = seg_bwd(
        "glu_out_norm_bwd", f_glu, g1, glu_ins, glu_outs, [dyn_mla, dyn_ssm])
    (dgl,) = mm("ssm_glu_bwd", [dz], [[wb["ssm_w_glu"]]], [F32], trans=True, adds=[dgl_a])
    (grads["ssm_w_glu"],) = mm_tn("ssm_glu_bwd_w", [gl], [dz], [(0, [0])])
    dy_lin, du_a, grads["ssm_d"] = seg_bwd("ssm_gelu_bwd", f_gelu, g1, gelu_ins, [otile(SSM_WIDTH, F32)], [dgl])

    gx_r, gx_i = bd_mm("ssm_cx_bwd", dy_lin, [cb_r, ncb_i], [F32, F32])
    dcb_r, dncb_i = bd_tn("ssm_cx_bwd_w", dy_lin, [xs_r, xs_i], SSM_BLOCKS)
    lam_r, lam_i, da_r8, da_i8 = ssm_scan("ssm_scan_bwd", gx_r, gx_i, a_r8, a_i8, reverse=True, states=(xs_r, xs_i))
    du = bd_mm_t("ssm_bu_bwd", [lam_r, lam_i], [bb_r, bb_i], add=du_a)
    dbb_r, dbb_i = bd_tn("ssm_bu_bwd_w", pu, [lam_r, lam_i], SSM_BLOCKS)
    prep_g = seg_bwd("ssm_prep_bwd", _ssm_prep_f, prep_grid, prep_ins, prep_outs, [da_r8, da_i8, dbb_r, dbb_i, dcb_r, dncb_i])
    for k, gname in enumerate(("ssm_a_re", "ssm_a_im", "ssm_log_dt", "ssm_bt_re", "ssm_bt_im", "ssm_c2_re", "ssm_c2_im")):
        grads[gname] = prep_g[k]

    dqh, dkh, dv0 = attn_bwd("mla_attn_bwd", qh, kh, v0, do_mla, o_mla, lse, t=min(ATTN_TILE, S))

    dq0, dkn0, dkr4, grads["qk_gain_q"], grads["qk_gain_k"] = seg_bwd(
        "qk_norm_rope_bwd", f_qk, gh, qk_ins, [head_out, head_out], [dqh, dkh])

    (dcq,) = mm("w_uq_bwd", [dq0], [[wb["w_uq"]]], [F32], trans=True)
    (grads["w_uq"],) = mm_tn("w_uq_bwd_w", [cq], [dq0], [(0, [0])])
    (dckv,) = mm("w_ukv_bwd", [dkn0, dv0], [[wb["w_ukv_k"]], [wb["w_ukv_v"]]], [F32], trans=True)
    grads["w_ukv_k"], grads["w_ukv_v"] = mm_tn("w_ukv_bwd_w", [ckv], [dkn0, dv0], [(0, [0]), (0, [1])])

    dpq, dpkv, grads["mla_q_norm"], grads["mla_kv_norm"] = seg_bwd("latent_norm_bwd", f_lat, g1, lat_ins, lat_outs, [dcq, dckv])

    dkr_list = [dkr4[h] for h in range(MLA_HEADS)]
    (dh2,) = mm("w_in_bwd", [dpq, dpkv, du] + dkr_list,
                [[wb["w_in_q"]], [wb["w_in_kv"]], [wb["w_in_u"]]] + [[wb["w_in_kr"]]] * MLA_HEADS, [F32], trans=True)
    grads["w_in_q"], grads["w_in_kv"], grads["w_in_u"], grads["w_in_kr"] = mm_tn(
        "w_in_bwd_w", [h2], [dpq, dpkv, du] + dkr_list, [(0, [0]), (0, [1]), (0, [2]), (0, [3, 4, 5, 6])])

    sent = on_grads("small", grads)
    dx1, grads["mix_norm"] = seg_bwd("mix_norm_bwd", f_norm, g1, mix_ins, mix_outs, [dh2], adds={0: dx2})

    dx, grads["ffn1_norm"], dg_1, du_1 = ffn_bwd_act("ffn1_bwd_act", dx1, x, w["ffn1_norm"] + sent, g_1, u_1,
                                                     wb["ffn1_w_gate"], wb["ffn1_w_up"], wb["ffn1_w_down"])
    grads["ffn1_w_gate"], grads["ffn1_w_up"], grads["ffn1_w_down"] = ffn_bwd_w("ffn1_bwd_w", h1, dx1, g_1, u_1, dg_1, du_1)
    return loss, dx, grads


def _pad_cols(a, n):
    return jnp.pad(a, ((0, 0), (0, n - a.shape[1])))


def _step_weights(full_bf16):
    wb = {}
    w_in = full_bf16["w_in"]
    wb["w_in_q"] = w_in[:, :MLA_Q_RANK]
    wb["w_in_kv"] = w_in[:, MLA_Q_RANK:MLA_Q_RANK + MLA_KV_RANK]
    wb["w_in_kr"] = _pad_cols(w_in[:, MLA_Q_RANK + MLA_KV_RANK:MLA_Q_RANK + MLA_KV_RANK + MLA_ROPE], LANES)
    wb["w_in_u"] = w_in[:, MLA_Q_RANK + MLA_KV_RANK + MLA_ROPE:]
    uq = full_bf16["mla_w_uq"].reshape(MLA_Q_RANK, MLA_HEADS, MLA_QK)
    wb["w_uq"] = jnp.pad(uq, ((0, 0), (0, 0), (0, HEAD_PAD - MLA_QK))).reshape(MLA_Q_RANK, MLA_HEADS * HEAD_PAD)
    ukv = full_bf16["mla_w_ukv"].reshape(MLA_KV_RANK, MLA_HEADS, MLA_NOPE + MLA_V)
    wb["w_ukv_k"] = ukv[:, :, :MLA_NOPE].reshape(MLA_KV_RANK, MLA_HEADS * MLA_NOPE)
    wb["w_ukv_v"] = ukv[:, :, MLA_NOPE:].reshape(MLA_KV_RANK, MLA_HEADS * MLA_V)
    wb["ssm_w_glu"] = full_bf16["ssm_w_glu"]
    wb["w_o_mla"] = full_bf16["w_o"][:SSM_WIDTH]
    wb["w_o_ssm"] = full_bf16["w_o"][SSM_WIDTH:]
    for k in ("xattn_w_q", "xattn_w_kv", "xattn_w_o"):
        wb[k] = full_bf16[k]
    return wb


def _problem_sharded_grads(g):
    out = {}
    kr = g["w_in_kr"][:, :MLA_ROPE]
    out["w_in"] = jnp.concatenate([g["w_in_q"], g["w_in_kv"], kr, g["w_in_u"]], axis=1)
    out["mla_w_uq"] = g["w_uq"].reshape(MLA_Q_RANK, MLA_HEADS, HEAD_PAD)[:, :, :MLA_QK].reshape(MLA_Q_RANK, MLA_HEADS * MLA_QK)
    k3 = g["w_ukv_k"].reshape(MLA_KV_RANK, MLA_HEADS, MLA_NOPE)
    v3 = g["w_ukv_v"].reshape(MLA_KV_RANK, MLA_HEADS, MLA_V)
    out["mla_w_ukv"] = jnp.concatenate([k3, v3], axis=2).reshape(MLA_KV_RANK, MLA_HEADS * (MLA_NOPE + MLA_V))
    for k in ("ssm_w_glu", "w_o", "xattn_w_q", "xattn_w_kv", "xattn_w_o"):
        out[k] = g[k]
    return out


def _problem_repl_grads(g):
    out = {}
    out["mla_qk_norm_q"] = g["qk_gain_q"][:, :MLA_QK]
    out["mla_qk_norm_k"] = g["qk_gain_k"][:, :MLA_QK]
    out["ssm_b_re"] = jnp.transpose(g["ssm_bt_re"].reshape(SSM_GROUPS, SSM_GROUP, SSM_STATE), (0, 2, 1))
    out["ssm_b_im"] = jnp.transpose(g["ssm_bt_im"].reshape(SSM_GROUPS, SSM_GROUP, SSM_STATE), (0, 2, 1))
    out["ssm_c_re"] = g["ssm_c2_re"].reshape(SSM_GROUPS, SSM_GROUP, SSM_STATE)
    out["ssm_c_im"] = g["ssm_c2_im"].reshape(SSM_GROUPS, SSM_GROUP, SSM_STATE)
    for k in SMALL_REPL:
        if k not in out:
            out[k] = g[k]
    return out


def _problem_grads(g):
    out = {**_problem_sharded_grads(g), **_problem_repl_grads(g)}
    out.update({k: g[k] for k in FFN_NAMES})
    return out


def _step_params(p):
    row = lambda a: a.reshape(1, -1)
    w = {k: row(p[k]) for k in ("ffn1_norm", "mix_norm", "mla_q_norm", "mla_kv_norm", "ssm_b_glu", "out_norm_mla",
                                "out_norm_ssm", "xattn_norm", "mem_norm", "xattn_q_norm", "xattn_k_norm", "ffn2_norm")}
    w["qk_gain_q"] = _pad_cols(row(p["mla_qk_norm_q"]), HEAD_PAD)
    w["qk_gain_k"] = _pad_cols(row(p["mla_qk_norm_k"]), HEAD_PAD)
    w["ssm_a_re"], w["ssm_a_im"] = p["ssm_a_re"], p["ssm_a_im"]
    w["ssm_log_dt"] = p["ssm_log_dt"].reshape(SSM_GROUPS, 1)
    w["ssm_bt_re"] = jnp.transpose(p["ssm_b_re"], (0, 2, 1)).reshape(SSM_WIDTH, SSM_STATE)
    w["ssm_bt_im"] = jnp.transpose(p["ssm_b_im"], (0, 2, 1)).reshape(SSM_WIDTH, SSM_STATE)
    w["ssm_c2_re"] = p["ssm_c_re"].reshape(SSM_WIDTH, SSM_STATE)
    w["ssm_c2_im"] = p["ssm_c_im"].reshape(SSM_WIDTH, SSM_STATE)
    w["ssm_d"] = p["ssm_d"].reshape(1, SSM_WIDTH)
    return w


ARG_NAMES = ['x', 'mem', 'positions', 'ffn1_norm', 'ffn1_w_gate', 'ffn1_w_up', 'ffn1_w_down', 'mix_norm', 'w_in', 'mla_q_norm', 'mla_w_uq', 'mla_kv_norm', 'mla_w_ukv', 'mla_qk_norm_q', 'mla_qk_norm_k', 'ssm_a_re', 'ssm_a_im', 'ssm_log_dt', 'ssm_b_re', 'ssm_b_im', 'ssm_c_re', 'ssm_c_im', 'ssm_d', 'ssm_w_glu', 'ssm_b_glu', 'out_norm_mla', 'out_norm_ssm', 'w_o', 'xattn_norm', 'mem_norm', 'xattn_w_q', 'xattn_w_kv', 'xattn_q_norm', 'xattn_k_norm', 'xattn_w_o', 'ffn2_norm', 'ffn2_w_gate', 'ffn2_w_up', 'ffn2_w_down']
WEIGHT_NAMES = ARG_NAMES[3:]


def _gather_weights(p, c):
    half = lambda a: lax.dynamic_slice_in_dim(a, c * (a.shape[0] // 2), a.shape[0] // 2, axis=0)
    ffn1 = [half(p[k].astype(BF16)) for k in FFN_NAMES[:3]]
    small = [half(_pack_shards({k: p[k].astype(BF16) for k in SMALL_SHARDED}))]
    ffn2 = [half(p[k].astype(BF16)) for k in FFN_NAMES[3:]]
    me = 4 * lax.axis_index("x") + 2 * lax.axis_index("y") + c
    own = lambda got, blocks: [lax.dynamic_update_index_in_dim(g, b, me, 0) for g, b in zip(got, blocks)]
    as_shards = lambda a: a.reshape(N_CHIPS, 2 * a.shape[1], a.shape[2])
    landing = lambda blocks: [jax.ShapeDtypeStruct((N_DEV,) + b.shape, b.dtype) for b in blocks]
    got1 = own(all_gather_halves("all_gather_weights_a", ffn1), ffn1)
    got1, small = lax.optimization_barrier((got1, small))
    flight_s, sent_s = split_copy_start("gather_small_start", _gather_copies, N_DEV - 1, small, landing(small))
    sent_s, ffn2 = lax.optimization_barrier((sent_s, ffn2))
    flight_2, sent_2 = split_copy_start("gather_ffn2_start", _gather_copies, N_DEV - 1, ffn2, landing(ffn2))
    wb = {k: as_shards(a) for k, a in zip(FFN_NAMES[:3], got1)}

    def small_weights(after):
        (got,) = own(split_copy_wait("gather_small_wait", flight_s, after), small)
        shards = _unpack_shards(got.reshape(N_CHIPS, -1, LANES))
        return _step_weights({k: _full_from_shards(k, shards[k]) for k in SMALL_SHARDED})

    def ffn2_weights(after):
        return [as_shards(a) for a in own(split_copy_wait("gather_ffn2_wait", flight_2, after), ffn2)]

    return wb, small_weights, ffn2_weights, sent_s + sent_2


class _GradReduce:
    def __init__(self, c):
        self.c, self.c_arr = c, jnp.reshape(c, (1,)).astype(jnp.int32)
        self.chip = 2 * lax.axis_index("x") + lax.axis_index("y")
        self.flights = []

    def start(self, tag, arrs):
        split = [a.reshape(N_CHIPS, 2, a.shape[1] // 2, a.shape[2]) for a in arrs]
        got = pair_send_halves(f"grad_pair_send_{tag}", split)
        parts = [pair_add(f"grad_pair_add_{tag}_{k}", s, g, self.c_arr) for k, (s, g) in enumerate(zip(split, got))]
        flight, sent = split_copy_start(f"grad_scatter_start_{tag}", _scatter_copies, N_CHIPS - 1, parts, parts)
        self.flights.append((tag, parts, flight))
        return sent

    def finish(self, after):
        halves = []
        for tag, parts, flight in self.flights:
            landed = split_copy_wait(f"grad_scatter_wait_{tag}", flight, after)
            for k, (q, p) in enumerate(zip(landed, parts)):
                mine = lax.dynamic_index_in_dim(p, self.chip, 0, keepdims=False)
                halves.append(sum_chips(f"grad_sum_{tag}_{k}", lax.dynamic_update_index_in_dim(q, mine, self.chip, 0)))
        tags = "_".join(t for t, _, _ in self.flights)
        self.flights = []
        theirs = pair_exchange(f"grad_pair_exchange_{tags}", halves)
        return [jnp.where(self.c == 0, jnp.concatenate([h, t], axis=0), jnp.concatenate([t, h], axis=0))
                for h, t in zip(halves, theirs)]


def kernel(x, mem, positions, ffn1_norm, ffn1_w_gate, ffn1_w_up, ffn1_w_down, mix_norm, w_in, mla_q_norm, mla_w_uq, mla_kv_norm, mla_w_ukv, mla_qk_norm_q, mla_qk_norm_k, ssm_a_re, ssm_a_im, ssm_log_dt, ssm_b_re, ssm_b_im, ssm_c_re, ssm_c_im, ssm_d, ssm_w_glu, ssm_b_glu, out_norm_mla, out_norm_ssm, w_o, xattn_norm, mem_norm, xattn_w_q, xattn_w_kv, xattn_q_norm, xattn_k_norm, xattn_w_o, ffn2_norm, ffn2_w_gate, ffn2_w_up, ffn2_w_down, loss_target, m_ffn1_norm, m_ffn1_w_gate, m_ffn1_w_up, m_ffn1_w_down, m_mix_norm, m_w_in, m_mla_q_norm, m_mla_w_uq, m_mla_kv_norm, m_mla_w_ukv, m_mla_qk_norm_q, m_mla_qk_norm_k, m_ssm_a_re, m_ssm_a_im, m_ssm_log_dt, m_ssm_b_re, m_ssm_b_im, m_ssm_c_re, m_ssm_c_im, m_ssm_d, m_ssm_w_glu, m_ssm_b_glu, m_out_norm_mla, m_out_norm_ssm, m_w_o, m_xattn_norm, m_mem_norm, m_xattn_w_q, m_xattn_w_kv, m_xattn_q_norm, m_xattn_k_norm, m_xattn_w_o, m_ffn2_norm, m_ffn2_w_gate, m_ffn2_w_up, m_ffn2_w_down, v_ffn1_norm, v_ffn1_w_gate, v_ffn1_w_up, v_ffn1_w_down, v_mix_norm, v_w_in, v_mla_q_norm, v_mla_w_uq, v_mla_kv_norm, v_mla_w_ukv, v_mla_qk_norm_q, v_mla_qk_norm_k, v_ssm_a_re, v_ssm_a_im, v_ssm_log_dt, v_ssm_b_re, v_ssm_b_im, v_ssm_c_re, v_ssm_c_im, v_ssm_d, v_ssm_w_glu, v_ssm_b_glu, v_out_norm_mla, v_out_norm_ssm, v_w_o, v_xattn_norm, v_mem_norm, v_xattn_w_q, v_xattn_w_kv, v_xattn_q_norm, v_xattn_k_norm, v_xattn_w_o, v_ffn2_norm, v_ffn2_w_gate, v_ffn2_w_up, v_ffn2_w_down):
    args = dict(locals())
    c = lax.axis_index("c")
    view = lambda k, a: jnp.swapaxes(a, 0, 1) if k in TRANSPOSED_VIEW else a
    p = {k: view(k, args[k][0]) for k in WEIGHT_NAMES}
    mom = {k: view(k, args["m_" + k][0]) for k in WEIGHT_NAMES}
    var = {k: view(k, args["v_" + k][0]) for k in WEIGHT_NAMES}
    natural = {k: view(k, p[k]) for k in WEIGHT_NAMES}

    wb, small_weights, ffn2_weights, sent = _gather_weights(
        {k: (p[k] if k in FFN_NAMES else natural[k]) for k in WEIGHT_NAMES}, c)
    w = _step_params(natural)
    w["ffn1_norm"] = w["ffn1_norm"] + sent
    early, late = _GradReduce(c), _GradReduce(c)

    def on_grads(tag, g):
        if tag == "ffn2":
            return early.start(tag, [g[k] for k in FFN_NAMES[3:]])
        full = _problem_sharded_grads(g)
        return early.start(tag, [_pack_shards({k: _shards_from_full(k, full[k]) for k in SMALL_SHARDED})])

    loss, dx, g = local_step(x[0], mem[0], positions[0], loss_target[0], w, wb, small_weights, ffn2_weights, on_grads)
    loss = lax.psum(loss, ("x", "y", "c"))

    sent = late.start("ffn1", [g[k] for k in FFN_NAMES[:3]])
    shards = early.finish(dx[:SUBLANES, :LANES] + sent)
    grad = dict(zip(FFN_NAMES[3:], shards[:3]))
    small_sharded = _unpack_shards(shards[3])
    grad.update({k: view(k, small_sharded[k]) for k in SMALL_SHARDED})
    repl = _problem_repl_grads(g)
    grad.update(_unpack_repl(all_reduce_small("grad_all_reduce_small", _pack_repl({k: repl[k] for k in SMALL_REPL}))))

    delta, new_m, new_v = {}, {}, {}
    small = [k for k in WEIGHT_NAMES if k not in FFN_NAMES and k not in SMALL_SHARDED]
    as2d = lambda a: a.reshape(-1, a.shape[-1])

    def update(k):
        delta[k], new_m[k], new_v[k] = adamw_big("adamw_" + k, as2d(p[k]), as2d(grad[k]), as2d(mom[k]), as2d(var[k]))

    for k in WEIGHT_NAMES:
        if k not in small and k not in FFN_NAMES[:3]:
            update(k)
    ds, nms, nvs = adamw_small("adamw_small", [as2d(p[k]) for k in small], [as2d(grad[k].reshape(p[k].shape)) for k in small],
                               [as2d(mom[k]) for k in small], [as2d(var[k]) for k in small])
    for k, d, nm, nv in zip(small, ds, nms, nvs):
        delta[k], new_m[k], new_v[k] = d, nm, nv

    grad.update(zip(FFN_NAMES[:3], late.finish(delta[FFN_NAMES[-1]])))
    for k in FFN_NAMES[:3]:
        update(k)

    shaped = lambda d, k: view(k, d.reshape(p[k].shape)).reshape(args[k].shape)
    return (loss, dx[None], *[shaped(grad[k], k) for k in WEIGHT_NAMES], *[shaped(delta[k], k) for k in WEIGHT_NAMES],
            *[shaped(new_m[k], k) for k in WEIGHT_NAMES], *[shaped(new_v[k], k) for k in WEIGHT_NAMES])
```

```python
import functools
import math

import jax
import jax.numpy as jnp
import numpy as np
from jax import lax
from jax.experimental import pallas as pl
from jax.experimental.pallas import tpu as pltpu

F32, BF16 = jnp.float32, jnp.bfloat16
EPS = 1e-6
MESH = pl.DeviceIdType.MESH

D_MODEL, D_FF = 1024, 2752
MLA_HEADS, MLA_Q_RANK, MLA_KV_RANK, MLA_NOPE, MLA_ROPE, MLA_V = 4, 384, 256, 128, 64, 128
MLA_QK = MLA_NOPE + MLA_ROPE
HEAD_PAD = 256
SSM_WIDTH, SSM_GROUP, SSM_GROUPS, SSM_STATE = 512, 16, 32, 64
SSM_LANES = SSM_GROUPS * SSM_STATE
XH, XHD = 4, 128
ROPE_THETA = 10000.0
ADAM_LR, ADAM_B1, ADAM_B2, ADAM_EPS, ADAM_WD, ADAM_STEP = 0.001, 0.9, 0.999, 1e-08, 0.01, 10
N_CHIPS, N_CORES, N_DEV = 4, 2, 8

VMEM_LIMIT_BYTES = 56 * 2**20
TOKEN_TILE = 512
ATTN_TILE = 512
SCAN_TIME_TILE = 1024
SCAN_LANE_TILE = 256
SUBLANES = 8


def _params(n_axes):
    return pltpu.CompilerParams(dimension_semantics=("arbitrary",) * n_axes, vmem_limit_bytes=VMEM_LIMIT_BYTES)


def _first(axes):
    cond = None
    for a in axes:
        c = pl.program_id(a) == 0
        cond = c if cond is None else jnp.logical_and(cond, c)
    return cond


def mm(name, xs, ws, out_dtypes, *, trans=False, adds=None, tm=TOKEN_TILE):
    rows = xs[0].shape[0]
    tm = min(tm, rows)
    n_in, n_out = len(xs), len(out_dtypes)
    pairs = [(i, j) for i in range(n_in) for j in range(n_out) if ws[i][j] is not None]
    w_list = [ws[i][j] for (i, j) in pairs]
    adds = list(adds) if adds is not None else [None] * n_out
    add_list = [a for a in adds if a is not None]
    out_cols = [None] * n_out
    for (i, j), w in zip(pairs, w_list):
        out_cols[j] = w.shape[0] if trans else w.shape[1]
    contract = (((1,), (1 if trans else 0,)), ((), ()))

    def body(*refs):
        x_refs = refs[:n_in]
        w_refs = refs[n_in:n_in + len(pairs)]
        a_refs = list(refs[n_in + len(pairs):n_in + len(pairs) + len(add_list)])
        o_refs = refs[n_in + len(pairs) + len(add_list):]
        xb = [None] * n_in
        for j in range(n_out):
            acc = None
            for p, (i, jj) in enumerate(pairs):
                if jj != j:
                    continue
                if xb[i] is None:
                    xb[i] = x_refs[i][...].astype(BF16)
                d = lax.dot_general(xb[i], w_refs[p][...].astype(BF16), contract, preferred_element_type=F32)
                acc = d if acc is None else acc + d
            if adds[j] is not None:
                acc = acc + a_refs.pop(0)[...].astype(F32)
            o_refs[j][...] = acc.astype(o_refs[j].dtype)

    in_specs = ([pl.BlockSpec((tm, x.shape[1]), lambda i: (i, 0)) for x in xs]
                + [pl.BlockSpec(w.shape, lambda i: (0, 0)) for w in w_list]
                + [pl.BlockSpec((tm, a.shape[1]), lambda i: (i, 0)) for a in add_list])
    outs = pl.pallas_call(
        body, name=name, grid=(rows // tm,), in_specs=in_specs,
        out_specs=[pl.BlockSpec((tm, n), lambda i: (i, 0)) for n in out_cols],
        out_shape=[jax.ShapeDtypeStruct((rows, n), dt) for n, dt in zip(out_cols, out_dtypes)],
        compiler_params=_params(1),
    )(*xs, *w_list, *add_list)
    return list(outs)


def mm_tn(name, xs, dys, pairs, *, tm=TOKEN_TILE):
    rows = xs[0].shape[0]
    tm = min(tm, rows)
    n_x, n_dy = len(xs), len(dys)
    contract = (((0,), (0,)), ((), ()))

    def body(*refs):
        x_refs, dy_refs, o_refs = refs[:n_x], refs[n_x:n_x + n_dy], refs[n_x + n_dy:]
        @pl.when(pl.program_id(0) == 0)
        def _():
            for o in o_refs:
                o[...] = jnp.zeros_like(o)

        for k, (i, js) in enumerate(pairs):
            dy = None
            for j in js:
                t = dy_refs[j][...].astype(F32)
                dy = t if dy is None else dy + t
            o_refs[k][...] += lax.dot_general(x_refs[i][...].astype(BF16), dy.astype(BF16), contract,
                                              preferred_element_type=F32)

    shapes = [(xs[i].shape[1], dys[js[0]].shape[1]) for (i, js) in pairs]
    outs = pl.pallas_call(
        body, name=name, grid=(rows // tm,),
        in_specs=[pl.BlockSpec((tm, a.shape[1]), lambda i: (i, 0)) for a in (*xs, *dys)],
        out_specs=[pl.BlockSpec(s, lambda i: (0, 0)) for s in shapes],
        out_shape=[jax.ShapeDtypeStruct(s, F32) for s in shapes],
        compiler_params=_params(1),
    )(*xs, *dys)
    return list(outs)


class In:
    def __init__(self, arr, block, imap, *, diff=False, acc=None, grad=None):
        self.arr, self.block, self.imap, self.diff, self.acc, self.grad = arr, block, imap, diff, acc, grad

    def spec(self):
        return pl.BlockSpec(self.block, self.imap)


class Out:
    def __init__(self, shape, dtype, block, imap):
        self.shape, self.dtype, self.block, self.imap = shape, dtype, block, imap

    def spec(self):
        return pl.BlockSpec(self.block, self.imap)


def seg_fwd(name, f, grid, ins, outs):
    n_in = len(ins)

    def body(*refs):
        res = f(*[r[...] for r in refs[:n_in]])
        for o_ref, r in zip(refs[n_in:], res):
            o_ref[...] = r.astype(o_ref.dtype)

    res = pl.pallas_call(
        body, name=name, grid=grid, in_specs=[i.spec() for i in ins], out_specs=[o.spec() for o in outs],
        out_shape=[jax.ShapeDtypeStruct(o.shape, o.dtype) for o in outs], compiler_params=_params(len(grid)),
    )(*[i.arr for i in ins])
    return list(res)


def seg_bwd(name, f, grid, ins, outs, cts, adds=None):
    n_in, n_ct = len(ins), len(cts)
    diff_idx = [k for k, i in enumerate(ins) if i.diff]
    adds = adds or {}
    add_keys = sorted(adds)
    add_list = [adds[k] for k in add_keys]

    def body(*refs):
        in_refs, ct_refs = refs[:n_in], refs[n_in:n_in + n_ct]
        add_refs = dict(zip(add_keys, refs[n_in + n_ct:n_in + n_ct + len(add_list)]))
        g_refs = refs[n_in + n_ct + len(add_list):]
        vals = [r[...] for r in in_refs]

        def g(*dv):
            full = list(vals)
            for k, v in zip(diff_idx, dv):
                full[k] = v
            return tuple(f(*full))

        _, pull = jax.vjp(g, *[vals[k].astype(F32) for k in diff_idx])
        grads = pull(tuple(c[...].astype(F32) for c in ct_refs))
        for n, (k, gr) in enumerate(zip(diff_idx, grads)):
            if k in add_refs:
                gr = gr + add_refs[k][...].astype(F32)
            if ins[k].acc is None:
                g_refs[n][...] = gr.astype(g_refs[n].dtype)
            else:
                @pl.when(_first(ins[k].acc))
                def _(n=n):
                    g_refs[n][...] = jnp.zeros_like(g_refs[n])

                g_refs[n][...] += gr

    g_specs, g_shapes = [], []
    for k in diff_idx:
        i = ins[k]
        shape, block, imap = i.grad if i.grad is not None else (i.arr.shape, i.block, i.imap)
        g_specs.append(pl.BlockSpec(block, imap))
        g_shapes.append(jax.ShapeDtypeStruct(shape, F32))
    in_specs = ([i.spec() for i in ins] + [o.spec() for o in outs]
                + [pl.BlockSpec(ins[k].block, ins[k].imap) for k in add_keys])
    res = pl.pallas_call(
        body, name=name, grid=grid, in_specs=in_specs, out_specs=g_specs, out_shape=g_shapes,
        compiler_params=_params(len(grid)),
    )(*[i.arr for i in ins], *cts, *add_list)
    return list(res)


def _rms(x, g):
    return x * lax.rsqrt(jnp.mean(x * x, axis=-1, keepdims=True) + EPS) * g


@jax.custom_vjp
def _bdot_nt(a, b):
    return lax.dot_general(a.astype(BF16), b.astype(BF16), (((1,), (1,)), ((), ())), preferred_element_type=F32)


def _bdot_nt_fwd(a, b):
    return _bdot_nt(a, b), (a, b)


def _bdot_nt_bwd(res, g):
    a, b = res
    gb = g.astype(BF16)
    da = lax.dot_general(gb, b.astype(BF16), (((1,), (0,)), ((), ())), preferred_element_type=F32)
    db = lax.dot_general(gb, a.astype(BF16), (((0,), (0,)), ((), ())), preferred_element_type=F32)
    return da, db


_bdot_nt.defvjp(_bdot_nt_fwd, _bdot_nt_bwd)


@jax.custom_vjp
def _bdot_nn(a, b):
    return lax.dot_general(a.astype(BF16), b.astype(BF16), (((1,), (0,)), ((), ())), preferred_element_type=F32)


def _bdot_nn_fwd(a, b):
    return _bdot_nn(a, b), (a, b)


def _bdot_nn_bwd(res, g):
    a, b = res
    gb = g.astype(BF16)
    da = lax.dot_general(gb, b.astype(BF16), (((1,), (1,)), ((), ())), preferred_element_type=F32)
    db = lax.dot_general(a.astype(BF16), gb, (((0,), (0,)), ((), ())), preferred_element_type=F32)
    return da, db


_bdot_nn.defvjp(_bdot_nn_fwd, _bdot_nn_bwd)


@functools.partial(jax.custom_vjp, nondiff_argnums=(1,))
def _lane_roll(x, shift):
    return pltpu.roll(x, shift, 1)


def _lane_roll_fwd(x, shift):
    return pltpu.roll(x, shift, 1), None


def _lane_roll_bwd(shift, _, g):
    return (pltpu.roll(g, (g.shape[1] - shift) % g.shape[1], 1),)


_lane_roll.defvjp(_lane_roll_fwd, _lane_roll_bwd)


def _hp_dot(a, b):
    return jnp.dot(a, b, precision=lax.Precision.HIGHEST, preferred_element_type=F32)


NT_DIMS = (((1,), (1,)), ((), ()))
TN_DIMS = (((0,), (0,)), ((), ()))


def ffn_fwd(name, x, gain, wg, wu, wd, tm=TOKEN_TILE):
    S, D = x.shape
    nsh, Fs, _ = wg.shape

    def body(x_ref, gain_ref, wg_ref, wu_ref, wd_ref, xo_ref, h_ref, g_ref, u_ref, acc):
        j = pl.program_id(1)

        @pl.when(j == 0)
        def _():
            h_ref[...] = _rms(x_ref[...], gain_ref[...]).astype(BF16)
            acc[...] = jnp.zeros_like(acc)

        h = h_ref[...]
        g = lax.dot_general(h, wg_ref[...], NT_DIMS, preferred_element_type=F32)
        u = lax.dot_general(h, wu_ref[...], NT_DIMS, preferred_element_type=F32)
        g_ref[...] = g.astype(BF16)
        u_ref[...] = u.astype(BF16)
        a = g * jax.nn.sigmoid(g) * u
        acc[...] += jnp.dot(a.astype(BF16), wd_ref[...], preferred_element_type=F32)

        @pl.when(j == nsh - 1)
        def _():
            xo_ref[...] = x_ref[...] + 0.5 * acc[...]

    return pl.pallas_call(
        body, name=name, grid=(S // tm, nsh),
        in_specs=[pl.BlockSpec((tm, D), lambda i, j: (i, 0)), pl.BlockSpec((1, D), lambda i, j: (0, 0)),
                  pl.BlockSpec((None, Fs, D), lambda i, j: (j, 0, 0)), pl.BlockSpec((None, Fs, D), lambda i, j: (j, 0, 0)),
                  pl.BlockSpec((None, Fs, D), lambda i, j: (j, 0, 0))],
        out_specs=[pl.BlockSpec((tm, D), lambda i, j: (i, 0)), pl.BlockSpec((tm, D), lambda i, j: (i, 0)),
                   pl.BlockSpec((None, tm, Fs), lambda i, j: (j, i, 0)), pl.BlockSpec((None, tm, Fs), lambda i, j: (j, i, 0))],
        out_shape=[jax.ShapeDtypeStruct((S, D), F32), jax.ShapeDtypeStruct((S, D), BF16),
                   jax.ShapeDtypeStruct((nsh, S, Fs), BF16), jax.ShapeDtypeStruct((nsh, S, Fs), BF16)],
        scratch_shapes=[pltpu.VMEM((tm, D), F32)], compiler_params=_params(2),
    )(x, gain, wg, wu, wd)


def ffn_bwd_act(name, dxo, x, gain, g, u, wg, wu, wd, tm=TOKEN_TILE):
    S, D = x.shape
    nsh, Fs, _ = wg.shape

    def body(dxo_ref, x_ref, gain_ref, g_ref, u_ref, wg_ref, wu_ref, wd_ref, dx_ref, dgain_ref, dg_ref, du_ref, dd, dh):
        i, j = pl.program_id(0), pl.program_id(1)

        @pl.when(j == 0)
        def _():
            dd[...] = (0.5 * dxo_ref[...]).astype(BF16)
            dh[...] = jnp.zeros_like(dh)

        da = lax.dot_general(dd[...], wd_ref[...], NT_DIMS, preferred_element_type=F32)
        gf, uf = g_ref[...].astype(F32), u_ref[...].astype(F32)
        sig = jax.nn.sigmoid(gf)
        dgv = (da * uf * (sig * (1.0 + gf * (1.0 - sig)))).astype(BF16)
        duv = (da * (gf * sig)).astype(BF16)
        dg_ref[...] = dgv
        du_ref[...] = duv
        dh[...] += (jnp.dot(dgv, wg_ref[...], preferred_element_type=F32)
                    + jnp.dot(duv, wu_ref[...], preferred_element_type=F32))

        @pl.when(j == nsh - 1)
        def _():
            xv = x_ref[...]
            r = lax.rsqrt(jnp.mean(xv * xv, axis=-1, keepdims=True) + EPS)
            xhat = xv * r
            dhv = dh[...]
            dxn = dhv * gain_ref[...]
            dx_ref[...] = dxo_ref[...] + r * (dxn - xhat * jnp.mean(dxn * xhat, axis=-1, keepdims=True))
            part = jnp.sum(dhv * xhat, axis=0, keepdims=True)

            @pl.when(i == 0)
            def _():
                dgain_ref[...] = part

            @pl.when(i != 0)
            def _():
                dgain_ref[...] += part

    return pl.pallas_call(
        body, name=name, grid=(S // tm, nsh),
        in_specs=[pl.BlockSpec((tm, D), lambda i, j: (i, 0)), pl.BlockSpec((tm, D), lambda i, j: (i, 0)),
                  pl.BlockSpec((1, D), lambda i, j: (0, 0)),
                  pl.BlockSpec((None, tm, Fs), lambda i, j: (j, i, 0)), pl.BlockSpec((None, tm, Fs), lambda i, j: (j, i, 0)),
                  pl.BlockSpec((None, Fs, D), lambda i, j: (j, 0, 0)), pl.BlockSpec((None, Fs, D), lambda i, j: (j, 0, 0)),
                  pl.BlockSpec((None, Fs, D), lambda i, j: (j, 0, 0))],
        out_specs=[pl.BlockSpec((tm, D), lambda i, j: (i, 0)), pl.BlockSpec((1, D), lambda i, j: (0, 0)),
                   pl.BlockSpec((None, tm, Fs), lambda i, j: (j, i, 0)), pl.BlockSpec((None, tm, Fs), lambda i, j: (j, i, 0))],
        out_shape=[jax.ShapeDtypeStruct((S, D), F32), jax.ShapeDtypeStruct((1, D), F32),
                   jax.ShapeDtypeStruct((nsh, S, Fs), BF16), jax.ShapeDtypeStruct((nsh, S, Fs), BF16)],
        scratch_shapes=[pltpu.VMEM((tm, D), BF16), pltpu.VMEM((tm, D), F32)], compiler_params=_params(2),
    )(dxo, x, gain, g, u, wg, wu, wd)


def ffn_bwd_w(name, h, dxo, g, u, dg, du, tm=TOKEN_TILE):
    S, D = h.shape
    nsh, _, Fs = g.shape

    def body(h_ref, dxo_ref, g_ref, u_ref, dg_ref, du_ref, dwg_ref, dwu_ref, dwd_ref):
        i = pl.program_id(1)
        gf, uf = g_ref[...].astype(F32), u_ref[...].astype(F32)
        a = (gf * jax.nn.sigmoid(gf) * uf).astype(BF16)
        dd = (0.5 * dxo_ref[...]).astype(BF16)
        hv = h_ref[...]

        @pl.when(i == 0)
        def _():
            dwg_ref[...] = jnp.zeros_like(dwg_ref)
            dwu_ref[...] = jnp.zeros_like(dwu_ref)
            dwd_ref[...] = jnp.zeros_like(dwd_ref)

        dwg_ref[...] += lax.dot_general(dg_ref[...], hv, TN_DIMS, preferred_element_type=F32)
        dwu_ref[...] += lax.dot_general(du_ref[...], hv, TN_DIMS, preferred_element_type=F32)
        dwd_ref[...] += lax.dot_general(a, dd, TN_DIMS, preferred_element_type=F32)

    act = pl.BlockSpec((None, tm, Fs), lambda j, i: (j, i, 0))
    wspec = pl.BlockSpec((None, Fs, D), lambda j, i: (j, 0, 0))
    return pl.pallas_call(
        body, name=name, grid=(nsh, S // tm),
        in_specs=[pl.BlockSpec((tm, D), lambda j, i: (i, 0)), pl.BlockSpec((tm, D), lambda j, i: (i, 0)), act, act, act, act],
        out_specs=[wspec, wspec, wspec], out_shape=[jax.ShapeDtypeStruct((nsh, Fs, D), F32)] * 3,
        compiler_params=_params(2),
    )(h, dxo, g, u, dg, du)


NEG_BIG = -1e30


def _causal_pairs(n, by_key):
    pairs = [(qi, ki) for qi in range(n) for ki in range(qi + 1)]
    if by_key:
        pairs.sort(key=lambda p: (p[1], p[0]))
    return jnp.asarray([p[0] for p in pairs], jnp.int32), jnp.asarray([p[1] for p in pairs], jnp.int32)


def _scores(q, k, masked):
    s = lax.dot_general(q, k, (((1,), (1,)), ((), ())), preferred_element_type=F32)
    if masked:
        row = lax.broadcasted_iota(jnp.int32, s.shape, 0)
        col = lax.broadcasted_iota(jnp.int32, s.shape, 1)
        s = jnp.where(row >= col, s, NEG_BIG)
    return s


def attn_fwd(name, q, k, v, t=ATTN_TILE):
    H, S, Dk = q.shape
    Dv = v.shape[1] // H
    qt, kt = _causal_pairs(S // t, by_key=False)

    def body(qt_ref, kt_ref, q_ref, k_ref, v_ref, o_ref, lse_ref, m_sc, l_sc, acc):
        qi, ki = qt_ref[pl.program_id(1)], kt_ref[pl.program_id(1)]

        @pl.when(ki == 0)
        def _():
            m_sc[...] = jnp.full_like(m_sc, NEG_BIG)
            l_sc[...] = jnp.zeros_like(l_sc)
            acc[...] = jnp.zeros_like(acc)

        def step(masked):
            s = _scores(q_ref[...], k_ref[...], masked)
            m_new = jnp.maximum(m_sc[...], jnp.max(s, axis=-1, keepdims=True))
            alpha = jnp.exp(m_sc[...] - m_new)
            p = jnp.exp(s - m_new)
            l_sc[...] = alpha * l_sc[...] + jnp.sum(p, axis=-1, keepdims=True)
            acc[...] = alpha * acc[...] + jnp.dot(p.astype(BF16), v_ref[...].astype(BF16), preferred_element_type=F32)
            m_sc[...] = m_new

        @pl.when(ki < qi)
        def _():
            step(False)

        @pl.when(ki == qi)
        def _():
            step(True)
            o_ref[...] = acc[...] / l_sc[...]
            lse_ref[...] = m_sc[...] + jnp.log(l_sc[...])

    return pl.pallas_call(
        body, name=name,
        grid_spec=pltpu.PrefetchScalarGridSpec(
            num_scalar_prefetch=2, grid=(H, qt.shape[0]),
            in_specs=[pl.BlockSpec((None, t, Dk), lambda h, s, qt, kt: (h, qt[s], 0)),
                      pl.BlockSpec((None, t, Dk), lambda h, s, qt, kt: (h, kt[s], 0)),
                      pl.BlockSpec((t, Dv), lambda h, s, qt, kt: (kt[s], h))],
            out_specs=[pl.BlockSpec((t, Dv), lambda h, s, qt, kt: (qt[s], h)),
                       pl.BlockSpec((None, t, 1), lambda h, s, qt, kt: (h, qt[s], 0))],
            scratch_shapes=[pltpu.VMEM((t, 1), F32), pltpu.VMEM((t, 1), F32), pltpu.VMEM((t, Dv), F32)]),
        out_shape=[jax.ShapeDtypeStruct((S, H * Dv), F32), jax.ShapeDtypeStruct((H, S, 1), F32)],
        compiler_params=_params(2),
    )(qt, kt, q, k, v)


def attn_bwd(name, q, k, v, do, o, lse, t=ATTN_TILE):
    H, S, Dk = q.shape
    Dv = v.shape[1] // H
    qt, kt = _causal_pairs(S // t, by_key=True)
    tn_dims = (((0,), (0,)), ((), ()))

    def body(qt_ref, kt_ref, q_ref, k_ref, v_ref, do_ref, o_ref, lse_ref, dq_ref, dk_ref, dv_ref):
        step_id = pl.program_id(1)
        qi, ki = qt_ref[step_id], kt_ref[step_id]

        @pl.when(step_id == 0)
        def _():
            dq_ref[...] = jnp.zeros_like(dq_ref)

        def step(masked):
            s = _scores(q_ref[...], k_ref[...], masked)
            p = jnp.exp(s - lse_ref[...])
            dov = do_ref[...]
            delta = jnp.sum(dov * o_ref[...], axis=-1, keepdims=True)
            dob = dov.astype(BF16)
            dp = lax.dot_general(dob, v_ref[...].astype(BF16), (((1,), (1,)), ((), ())), preferred_element_type=F32)
            ds = (p * (dp - delta)).astype(BF16)
            pdv = lax.dot_general(p.astype(BF16), dob, tn_dims, preferred_element_type=F32)
            pdk = lax.dot_general(ds, q_ref[...], tn_dims, preferred_element_type=F32)
            rows = pl.ds(pl.multiple_of(qi * t, t), t)
            dq_ref[rows, :] += jnp.dot(ds, k_ref[...], preferred_element_type=F32)
            return pdk, pdv

        @pl.when(ki == qi)
        def _():
            dk_ref[...] = jnp.zeros_like(dk_ref)
            dv_ref[...] = jnp.zeros_like(dv_ref)

        def accumulate(masked):
            pdk, pdv = step(masked)
            dk_ref[...] += pdk
            dv_ref[...] += pdv

        @pl.when(ki == qi)
        def _():
            accumulate(True)

        @pl.when(ki < qi)
        def _():
            accumulate(False)

    qmap = lambda h, s, qt, kt: (h, qt[s], 0)
    kmap = lambda h, s, qt, kt: (h, kt[s], 0)
    qrow = lambda h, s, qt, kt: (qt[s], h)
    krow = lambda h, s, qt, kt: (kt[s], h)
    return pl.pallas_call(
        body, name=name,
        grid_spec=pltpu.PrefetchScalarGridSpec(
            num_scalar_prefetch=2, grid=(H, qt.shape[0]),
            in_specs=[pl.BlockSpec((None, t, Dk), qmap), pl.BlockSpec((None, t, Dk), kmap), pl.BlockSpec((t, Dv), krow),
                      pl.BlockSpec((t, Dv), qrow), pl.BlockSpec((t, Dv), qrow), pl.BlockSpec((None, t, 1), qmap)],
            out_specs=[pl.BlockSpec((None, S, Dk), lambda h, s, qt, kt: (h, 0, 0)), pl.BlockSpec((None, t, Dk), kmap),
                       pl.BlockSpec((t, Dv), krow)]),
        out_shape=[jax.ShapeDtypeStruct((H, S, Dk), F32), jax.ShapeDtypeStruct((H, S, Dk), F32),
                   jax.ShapeDtypeStruct((S, H * Dv), F32)],
        compiler_params=_params(2),
    )(qt, kt, q, k, v, do, o, lse)


def _cmul(ar, ai, br, bi):
    return ar * br - ai * bi, ar * bi + ai * br


def _scan_tile(x_r, x_i, ar_ref, ai_ref, cr_sc, ci_sc, *, reverse, first, states=None):
    tc, lanes = x_r.shape
    nblk, lb = tc // SUBLANES, SCAN_LANE_TILE
    with_da = states is not None
    if with_da:
        xr_all, xi_all, pr_all, pi_all, dar_all, dai_all, chunk = states

    @pl.when(first)
    def _():
        cr_sc[...] = jnp.zeros_like(cr_sc)
        ci_sc[...] = jnp.zeros_like(ci_sc)
        if with_da:
            dar_all[...] = jnp.zeros_like(dar_all)
            dai_all[...] = jnp.zeros_like(dai_all)

    row = lax.broadcasted_iota(jnp.int32, (SUBLANES, lb), 0)
    for l0 in range(0, lanes, lb):
        _scan_lanes(x_r.at[:, pl.ds(l0, lb)], x_i.at[:, pl.ds(l0, lb)], ar_ref[0:1, pl.ds(l0, lb)],
                    ai_ref[0:1, pl.ds(l0, lb)], cr_sc.at[:, pl.ds(l0, lb)], ci_sc.at[:, pl.ds(l0, lb)], row, reverse,
                    nblk, None if not with_da else tuple(r.at[:, pl.ds(l0, lb)] for r in states[:6]) + (chunk,))


def _scan_lanes(x_r, x_i, a1r, a1i, cr_sc, ci_sc, row, reverse, nblk, states):
    lb = x_r.shape[1]
    with_da = states is not None
    if with_da:
        xr_ref, xi_ref, pr_ref, pi_ref, dar_ref, dai_ref, chunk = states
    if reverse:
        a1i = -a1i
    a2r, a2i = _cmul(a1r, a1i, a1r, a1i)
    a4r, a4i = _cmul(a2r, a2i, a2r, a2i)
    pw_r, pw_i = jnp.zeros((SUBLANES, lb), F32), jnp.zeros((SUBLANES, lb), F32)
    cur_r, cur_i = a1r, a1i
    for e in range(SUBLANES):
        r_at = (SUBLANES - 1 - e) if reverse else e
        pw_r = jnp.where(row == r_at, cur_r, pw_r)
        pw_i = jnp.where(row == r_at, cur_i, pw_i)
        cur_r, cur_i = _cmul(cur_r, cur_i, a1r, a1i)
    steps = ((1, a1r, a1i), (2, a2r, a2i), (4, a4r, a4i))

    def block(jb, carry):
        if with_da:
            cr, ci, acc_r, acc_i = carry
        else:
            cr, ci = carry
        idx = (nblk - 1 - jb) if reverse else jb
        r0 = pl.multiple_of(idx * SUBLANES, SUBLANES)
        xr = x_r[pl.ds(r0, SUBLANES), :]
        xi = x_i[pl.ds(r0, SUBLANES), :]
        for d, pr, pi in steps:
            if reverse:
                keep = row < SUBLANES - d
                sr, si = pltpu.roll(xr, SUBLANES - d, 0), pltpu.roll(xi, SUBLANES - d, 0)
            else:
                keep = row >= d
                sr, si = pltpu.roll(xr, d, 0), pltpu.roll(xi, d, 0)
            sr, si = jnp.where(keep, sr, 0.0), jnp.where(keep, si, 0.0)
            xr, xi = xr + pr * sr - pi * si, xi + pr * si + pi * sr
        xr, xi = xr + pw_r * cr - pw_i * ci, xi + pw_r * ci + pw_i * cr
        x_r[pl.ds(r0, SUBLANES), :] = xr
        x_i[pl.ds(r0, SUBLANES), :] = xi
        edge = 0 if reverse else SUBLANES - 1
        cr, ci = xr[edge:edge + 1, :], xi[edge:edge + 1, :]
        if not with_da:
            return cr, ci
        fr = xr_ref[pl.ds(r0, SUBLANES), :]
        fi = xi_ref[pl.ds(r0, SUBLANES), :]
        rp = pl.multiple_of(jnp.maximum(idx - 1, 0) * SUBLANES, SUBLANES)
        inside = idx > 0
        before_r = jnp.where(inside, xr_ref[pl.ds(rp, SUBLANES), :], pr_ref[...])
        before_i = jnp.where(inside, xi_ref[pl.ds(rp, SUBLANES), :], pi_ref[...])
        live = jnp.where(jnp.logical_or(inside, chunk > 0), 1.0, 0.0)
        last_r = before_r[SUBLANES - 1:SUBLANES, :] * live
        last_i = before_i[SUBLANES - 1:SUBLANES, :] * live
        pvr = jnp.where(row == 0, last_r, pltpu.roll(fr, 1, 0))
        pvi = jnp.where(row == 0, last_i, pltpu.roll(fi, 1, 0))
        acc_r = acc_r + xr * pvr + xi * pvi
        acc_i = acc_i + xi * pvr - xr * pvi
        return cr, ci, acc_r, acc_i

    init = (cr_sc[...], ci_sc[...])
    if with_da:
        init = init + (jnp.zeros((SUBLANES, lb), F32), jnp.zeros((SUBLANES, lb), F32))
    fin = lax.fori_loop(0, nblk, block, init)
    cr_sc[...] = fin[0]
    ci_sc[...] = fin[1]
    if with_da:
        dar_ref[...] += fin[2]
        dai_ref[...] += fin[3]


SSM_BLOCKS = 4
BLOCK_CH = SSM_WIDTH // SSM_BLOCKS
PREP_LANES = SSM_LANES // SSM_BLOCKS


def ssm_fwd(name, u, bb_r, bb_i, cb_r, ncb_i, a_r8, a_i8):
    S = u.shape[0]
    tc = min(SCAN_TIME_TILE, S)

    def body(u_ref, bbr_ref, bbi_ref, cbr_ref, ncbi_ref, ar_ref, ai_ref, xr_ref, xi_ref, y_ref, cr_sc, ci_sc):
        ub = u_ref[...].astype(BF16)
        xr_ref[...] = jnp.dot(ub, bbr_ref[...], preferred_element_type=F32)
        xi_ref[...] = jnp.dot(ub, bbi_ref[...], preferred_element_type=F32)
        _scan_tile(xr_ref, xi_ref, ar_ref, ai_ref, cr_sc, ci_sc, reverse=False, first=pl.program_id(1) == 0)
        y_ref[...] = (lax.dot_general(xr_ref[...].astype(BF16), cbr_ref[...], NT_DIMS, preferred_element_type=F32)
                      + lax.dot_general(xi_ref[...].astype(BF16), ncbi_ref[...], NT_DIMS, preferred_element_type=F32))

    ch = pl.BlockSpec((tc, BLOCK_CH), lambda b, t: (t, b))
    st = pl.BlockSpec((tc, PREP_LANES), lambda b, t: (t, b))
    wt = pl.BlockSpec((None, BLOCK_CH, PREP_LANES), lambda b, t: (b, 0, 0))
    par = pl.BlockSpec((SUBLANES, PREP_LANES), lambda b, t: (0, b))
    return pl.pallas_call(
        body, name=name, grid=(SSM_BLOCKS, S // tc), in_specs=[ch, wt, wt, wt, wt, par, par], out_specs=[st, st, ch],
        out_shape=[jax.ShapeDtypeStruct((S, SSM_LANES), F32), jax.ShapeDtypeStruct((S, SSM_LANES), F32),
                   jax.ShapeDtypeStruct((S, SSM_WIDTH), F32)],
        scratch_shapes=[pltpu.VMEM((1, PREP_LANES), F32), pltpu.VMEM((1, PREP_LANES), F32)], compiler_params=_params(2),
    )(u, bb_r, bb_i, cb_r, ncb_i, a_r8, a_i8)


def ssm_bwd(name, dy, du_add, u, xs_r, xs_i, bb_r, bb_i, cb_r, ncb_i, a_r8, a_i8):
    S = u.shape[0]
    tc = min(SCAN_TIME_TILE, S)
    nt = S // tc

    def body(dy_ref, dua_ref, u_ref, xr_ref, xi_ref, pr_ref, pi_ref, bbr_ref, bbi_ref, cbr_ref, ncbi_ref, ar_ref, ai_ref,
             du_ref, dbbr_ref, dbbi_ref, dcbr_ref, dncbi_ref, dar_ref, dai_ref, lr_sc, li_sc, cr_sc, ci_sc):
        t = pl.program_id(1)
        first = t == 0

        @pl.when(first)
        def _():
            for r in (dbbr_ref, dbbi_ref, dcbr_ref, dncbi_ref):
                r[...] = jnp.zeros_like(r)

        dyb = dy_ref[...].astype(BF16)
        lr_sc[...] = jnp.dot(dyb, cbr_ref[...], preferred_element_type=F32)
        li_sc[...] = jnp.dot(dyb, ncbi_ref[...], preferred_element_type=F32)
        _scan_tile(lr_sc, li_sc, ar_ref, ai_ref, cr_sc, ci_sc, reverse=True, first=first,
                   states=(xr_ref, xi_ref, pr_ref, pi_ref, dar_ref, dai_ref, nt - 1 - t))
        lrb, lib = lr_sc[...].astype(BF16), li_sc[...].astype(BF16)
        du_ref[...] = (lax.dot_general(lrb, bbr_ref[...], NT_DIMS, preferred_element_type=F32)
                       + lax.dot_general(lib, bbi_ref[...], NT_DIMS, preferred_element_type=F32) + dua_ref[...])
        ub = u_ref[...].astype(BF16)
        dbbr_ref[...] += lax.dot_general(ub, lrb, TN_DIMS, preferred_element_type=F32)
        dbbi_ref[...] += lax.dot_general(ub, lib, TN_DIMS, preferred_element_type=F32)
        dcbr_ref[...] += lax.dot_general(dyb, xr_ref[...].astype(BF16), TN_DIMS, preferred_element_type=F32)
        dncbi_ref[...] += lax.dot_general(dyb, xi_ref[...].astype(BF16), TN_DIMS, preferred_element_type=F32)

    ch = pl.BlockSpec((tc, BLOCK_CH), lambda b, t: (nt - 1 - t, b))
    st = pl.BlockSpec((tc, PREP_LANES), lambda b, t: (nt - 1 - t, b))
    prev = pl.BlockSpec((SUBLANES, PREP_LANES), lambda b, t: (jnp.maximum((nt - 1 - t) * (tc // SUBLANES) - 1, 0), b))
    wt = pl.BlockSpec((None, BLOCK_CH, PREP_LANES), lambda b, t: (b, 0, 0))
    par = pl.BlockSpec((SUBLANES, PREP_LANES), lambda b, t: (0, b))
    blk = jax.ShapeDtypeStruct((SSM_BLOCKS, BLOCK_CH, PREP_LANES), F32)
    rows8 = jax.ShapeDtypeStruct((SUBLANES, SSM_LANES), F32)
    return pl.pallas_call(
        body, name=name, grid=(SSM_BLOCKS, nt), in_specs=[ch, ch, ch, st, st, prev, prev, wt, wt, wt, wt, par, par],
        out_specs=[ch, wt, wt, wt, wt, par, par],
        out_shape=[jax.ShapeDtypeStruct((S, SSM_WIDTH), F32), blk, blk, blk, blk, rows8, rows8],
        scratch_shapes=[pltpu.VMEM((tc, PREP_LANES), F32), pltpu.VMEM((tc, PREP_LANES), F32),
                        pltpu.VMEM((1, PREP_LANES), F32), pltpu.VMEM((1, PREP_LANES), F32)],
        compiler_params=_params(2),
    )(dy, du_add, u, xs_r, xs_i, xs_r, xs_i, bb_r, bb_i, cb_r, ncb_i, a_r8, a_i8)


def _ssm_prep_f(a_re, a_im, log_dt, bt_re, bt_im, c_re, c_im):
    first_group = pl.program_id(0) * (SSM_GROUPS // SSM_BLOCKS)
    iota = lambda shape, d: lax.broadcasted_iota(jnp.int32, shape, d)
    grp_of_row = lambda shape: iota(shape, 0) >> int(math.log2(SSM_GROUP))
    grp_of_lane = lambda shape: iota(shape, 1) >> int(math.log2(SSM_STATE))
    rep = (grp_of_row((BLOCK_CH, SSM_GROUPS)) + first_group == iota((BLOCK_CH, SSM_GROUPS), 1)).astype(F32)
    til = ((iota((SSM_STATE, PREP_LANES), 1) & (SSM_STATE - 1)) == iota((SSM_STATE, PREP_LANES), 0)).astype(F32)
    m_rows = (grp_of_row((BLOCK_CH, PREP_LANES)) == grp_of_lane((BLOCK_CH, PREP_LANES))).astype(F32)
    m_grp = (iota((SSM_GROUPS, PREP_LANES), 0) == grp_of_lane((SSM_GROUPS, PREP_LANES)) + first_group).astype(F32)
    dt = jnp.exp(log_dt)
    decay = jnp.exp(a_re * dt)
    ar = decay * jnp.cos(a_im * dt)
    ai = decay * jnp.sin(a_im * dt)
    den = a_re * a_re + a_im * a_im
    nr = ar - 1.0
    coef_r = (nr * a_re + ai * a_im) / den
    coef_i = (ai * a_re - nr * a_im) / den
    cr, ci = _hp_dot(rep, coef_r), _hp_dot(rep, coef_i)
    bb_r = cr * bt_re - ci * bt_im
    bb_i = cr * bt_im + ci * bt_re
    big = lambda m: _hp_dot(m, til) * m_rows
    lanes = lambda m: jnp.broadcast_to(jnp.sum(_hp_dot(m, til) * m_grp, axis=0, keepdims=True), (SUBLANES, PREP_LANES))
    return lanes(ar), lanes(ai), big(bb_r), big(bb_i), big(c_re), -big(c_im)


def _whole(arr, **kw):
    nd = arr.ndim
    return In(arr, arr.shape, lambda *_: (0,) * nd, **kw)


def loss_grad(name, y, target, tm=TOKEN_TILE):
    S, D = y.shape

    def body(y_ref, t_ref, dy_ref, part_ref):
        err = y_ref[...] - t_ref[...]
        dy_ref[...] = err * (1.0 / D)
        part_ref[...] = jnp.full(part_ref.shape, 0.5 * jnp.sum(jnp.mean(err * err, axis=-1)), F32)

    return pl.pallas_call(
        body, name=name, grid=(S // tm,),
        in_specs=[pl.BlockSpec((tm, D), lambda i: (i, 0)), pl.BlockSpec((tm, D), lambda i: (i, 0))],
        out_specs=[pl.BlockSpec((tm, D), lambda i: (i, 0)), pl.BlockSpec((SUBLANES, 128), lambda i: (i, 0))],
        out_shape=[jax.ShapeDtypeStruct((S, D), F32), jax.ShapeDtypeStruct((S // tm * SUBLANES, 128), F32)],
        compiler_params=_params(1),
    )(y, target)


def _adamw_math(w, g, m, v):
    m = ADAM_B1 * m + (1.0 - ADAM_B1) * g
    v = ADAM_B2 * v + (1.0 - ADAM_B2) * (g * g)
    m_hat = m / (1.0 - ADAM_B1 ** ADAM_STEP)
    v_hat = v / (1.0 - ADAM_B2 ** ADAM_STEP)
    delta = -ADAM_LR * (m_hat / (jnp.sqrt(v_hat) + ADAM_EPS) + ADAM_WD * w)
    return delta, m, v


def adamw_big(name, w, g, m, v):
    R, C = w.shape
    tr = R
    for cand in (512, 344, 256, 128):
        if R % cand == 0:
            tr = cand
            break

    def body(w_ref, g_ref, m_ref, v_ref, d_ref, nm_ref, nv_ref):
        d, nm, nv = _adamw_math(w_ref[...], g_ref[...], m_ref[...], v_ref[...])
        d_ref[...] = d
        nm_ref[...] = nm
        nv_ref[...] = nv

    spec = pl.BlockSpec((tr, C), lambda i: (i, 0))
    return pl.pallas_call(
        body, name=name, grid=(R // tr,), in_specs=[spec] * 4, out_specs=[spec] * 3,
        out_shape=[jax.ShapeDtypeStruct((R, C), F32)] * 3, compiler_params=_params(1),
    )(w, g, m, v)


def adamw_small(name, ws, gs, ms, vs):
    n = len(ws)

    def body(*refs):
        for k in range(n):
            d, nm, nv = _adamw_math(refs[k][...], refs[n + k][...], refs[2 * n + k][...], refs[3 * n + k][...])
            refs[4 * n + k][...] = d
            refs[5 * n + k][...] = nm
            refs[6 * n + k][...] = nv

    vm = pl.BlockSpec(memory_space=pltpu.VMEM)
    shapes = [jax.ShapeDtypeStruct(w.shape, F32) for w in ws]
    res = pl.pallas_call(
        body, name=name, in_specs=[vm] * (4 * n), out_specs=[vm] * (3 * n), out_shape=shapes * 3,
        compiler_params=pltpu.CompilerParams(vmem_limit_bytes=VMEM_LIMIT_BYTES),
    )(*ws, *gs, *ms, *vs)
    return res[:n], res[n:2 * n], res[2 * n:]


def _place():
    return lax.axis_index("x"), lax.axis_index("y"), lax.axis_index("c")


def _other_chips(x, y):
    return [(1 - x, y), (x, 1 - y), (1 - x, 1 - y)]


HBM = pl.BlockSpec(memory_space=pl.ANY)


def all_gather_halves(name, blocks):
    n = len(blocks)

    def body(*refs):
        in_refs, out_refs = refs[:n], refs[n:2 * n]
        send_sems, recv_sems = refs[2 * n:]
        x, y, c = _place()
        me, sibling = (x, y, c), (x, y, 1 - c)
        chips = _other_chips(x, y)

        def slot(a, px, py, pc):
            return out_refs[a].at[4 * px + 2 * py + pc]

        def copy(a, k, block, to, src=None):
            return pltpu.make_async_remote_copy(
                src_ref=slot(a, *block) if src is None else src, dst_ref=slot(a, *block),
                send_sem=send_sems.at[a, k], recv_sem=recv_sems.at[a, k], device_id=to, device_id_type=MESH)

        first = []
        for a in range(n):
            first.append(copy(a, 0, me, sibling, src=in_refs[a]))
            first += [copy(a, 1 + j, me, (*chip, c), src=in_refs[a]) for j, chip in enumerate(chips)]
        for cp in first:
            cp.start()
        passed = []
        for j, chip in enumerate(chips):
            for a in range(n):
                copy(a, 1 + j, (*chip, c), me).wait_recv()
                fw = copy(a, 4 + j, (*chip, c), sibling)
                fw.start()
                passed.append(fw)
        for a in range(n):
            copy(a, 0, sibling, me).wait_recv()
            for j, chip in enumerate(chips):
                copy(a, 4 + j, (*chip, 1 - c), me).wait_recv()
        for cp in first + passed:
            cp.wait_send()

    res = pl.pallas_call(
        body, name=name, in_specs=[HBM] * n, out_specs=[HBM] * n,
        out_shape=[jax.ShapeDtypeStruct((N_DEV,) + b.shape, b.dtype) for b in blocks],
        scratch_shapes=[pltpu.SemaphoreType.DMA((n, 7)), pltpu.SemaphoreType.DMA((n, 7))],
    )(*blocks)
    return list(res)


def pair_send_halves(name, grads):
    n = len(grads)

    def body(*refs):
        in_refs, out_refs = refs[:n], refs[n:2 * n]
        send_sems, recv_sems = refs[2 * n:]
        x, y, c = _place()
        cps = []
        for a in range(n):
            cp = pltpu.make_async_remote_copy(
                src_ref=in_refs[a].at[:, 1 - c], dst_ref=out_refs[a], send_sem=send_sems.at[a], recv_sem=recv_sems.at[a],
                device_id=(x, y, 1 - c), device_id_type=MESH)
            cp.start()
            cps.append(cp)
        for cp in cps:
            cp.wait()

    res = pl.pallas_call(
        body, name=name, in_specs=[HBM] * n, out_specs=[HBM] * n,
        out_shape=[jax.ShapeDtypeStruct((g.shape[0],) + g.shape[2:], g.dtype) for g in grads],
        scratch_shapes=[pltpu.SemaphoreType.DMA((n,)), pltpu.SemaphoreType.DMA((n,))],
    )(*grads)
    return list(res)


def pair_add(name, grad, got, c_arr):
    nsh, _, M, N = grad.shape
    tr = M
    for cand in (512, 256, 192, 128, 64, 16):
        if M % cand == 0:
            tr = cand
            break

    def body(c_ref, g_ref, p_ref, o_ref):
        o_ref[...] = (g_ref[...] + p_ref[...]).astype(BF16)

    return pl.pallas_call(
        body, name=name,
        grid_spec=pltpu.PrefetchScalarGridSpec(
            num_scalar_prefetch=1, grid=(nsh, M // tr),
            in_specs=[pl.BlockSpec((None, None, tr, N), lambda j, i, c_ref: (j, c_ref[0], i, 0)),
                      pl.BlockSpec((None, tr, N), lambda j, i, c_ref: (j, i, 0))],
            out_specs=pl.BlockSpec((None, tr, N), lambda j, i, c_ref: (j, i, 0))),
        out_shape=jax.ShapeDtypeStruct((nsh, M, N), BF16), compiler_params=_params(2),
    )(c_arr, grad, got)


def scatter_to_chips(name, parts):
    n = len(parts)

    def body(*refs):
        in_refs, out_refs = refs[:n], refs[n:2 * n]
        send_sems, recv_sems = refs[2 * n:]
        x, y, c = _place()
        mine = 2 * x + y
        chips = _other_chips(x, y)
        cps = []
        for a in range(n):
            for k, (px, py) in enumerate(chips):
                cp = pltpu.make_async_remote_copy(
                    src_ref=in_refs[a].at[2 * px + py], dst_ref=out_refs[a].at[mine],
                    send_sem=send_sems.at[a, k], recv_sem=recv_sems.at[a, k], device_id=(px, py, c), device_id_type=MESH)
                cp.start()
                cps.append((cp, a, k, px, py))
        for cp, a, k, px, py in cps:
            pltpu.make_async_remote_copy(
                src_ref=in_refs[a].at[mine], dst_ref=out_refs[a].at[2 * px + py],
                send_sem=send_sems.at[a, k], recv_sem=recv_sems.at[a, k], device_id=(px, py, c), device_id_type=MESH).wait_recv()
        for cp, *_ in cps:
            cp.wait_send()

    res = pl.pallas_call(
        body, name=name, in_specs=[HBM] * n, out_specs=[HBM] * n,
        out_shape=[jax.ShapeDtypeStruct(p.shape, p.dtype) for p in parts],
        scratch_shapes=[pltpu.SemaphoreType.DMA((n, 3)), pltpu.SemaphoreType.DMA((n, 3))],
    )(*parts)
    return list(res)


def sum_chips(name, q):
    nsh, M, N = q.shape
    tr = M
    for cand in (512, 256, 192, 128, 64, 16):
        if M % cand == 0:
            tr = cand
            break

    def body(q_ref, o_ref):
        acc = q_ref[0].astype(F32)
        for j in range(1, nsh):
            acc = acc + q_ref[j].astype(F32)
        o_ref[...] = acc

    return pl.pallas_call(
        body, name=name, grid=(M // tr,), in_specs=[pl.BlockSpec((nsh, tr, N), lambda i: (0, i, 0))],
        out_specs=pl.BlockSpec((tr, N), lambda i: (i, 0)), out_shape=jax.ShapeDtypeStruct((M, N), F32),
        compiler_params=_params(1),
    )(q)


def pair_exchange(name, halves):
    n = len(halves)

    def body(*refs):
        in_refs, out_refs = refs[:n], refs[n:2 * n]
        send_sems, recv_sems = refs[2 * n:]
        x, y, c = _place()
        cps = []
        for a in range(n):
            cp = pltpu.make_async_remote_copy(
                src_ref=in_refs[a], dst_ref=out_refs[a], send_sem=send_sems.at[a], recv_sem=recv_sems.at[a],
                device_id=(x, y, 1 - c), device_id_type=MESH)
            cp.start()
            cps.append(cp)
        for cp in cps:
            cp.wait()

    res = pl.pallas_call(
        body, name=name, in_specs=[HBM] * n, out_specs=[HBM] * n,
        out_shape=[jax.ShapeDtypeStruct(h.shape, h.dtype) for h in halves],
        scratch_shapes=[pltpu.SemaphoreType.DMA((n,)), pltpu.SemaphoreType.DMA((n,))],
    )(*halves)
    return list(res)


SEM = pl.BlockSpec(memory_space=pltpu.SEMAPHORE)
IN_HBM = pl.BlockSpec(memory_space=pltpu.HBM)
SPLIT_COPY = pltpu.CompilerParams(has_side_effects=pltpu.SideEffectType.DATAFLOW_SIDE_EFFECTING)


def _scatter_copies(src_refs, dst_refs, send_sems, recv_sems):
    x, y, c = _place()
    mine = 2 * x + y
    return [pltpu.make_async_remote_copy(
        src_ref=src_refs[a].at[2 * px + py], dst_ref=dst_refs[a].at[mine], send_sem=send_sems.at[a * (N_CHIPS - 1) + k],
        recv_sem=recv_sems.at[a * (N_CHIPS - 1) + k], device_id=(px, py, c), device_id_type=MESH)
        for a in range(len(src_refs)) for k, (px, py) in enumerate(_other_chips(x, y))]


def _gather_copies(src_refs, dst_refs, send_sems, recv_sems):
    x, y, c = _place()
    me = 4 * x + 2 * y + c
    cps = []
    for a in range(len(src_refs)):
        for k in range(1, N_DEV):
            to = (1 - x if k & 4 else x, 1 - y if k & 2 else y, 1 - c if k & 1 else c)
            s = a * (N_DEV - 1) + k - 1
            cps.append(pltpu.make_async_remote_copy(
                src_ref=src_refs[a], dst_ref=dst_refs[a].at[me], send_sem=send_sems.at[s], recv_sem=recv_sems.at[s],
                device_id=to, device_id_type=MESH))
    return cps


def split_copy_start(name, copies, n_sem, srcs, land_shapes):
    n = len(srcs)
    lands = [lax.empty(s.shape, s.dtype) for s in land_shapes]

    def body(*refs):
        for cp in copies(refs[:n], refs[n:2 * n], refs[2 * n], refs[2 * n + 1]):
            cp.start()
        refs[-1][...] = jnp.zeros_like(refs[-1])

    thru = [pltpu.HBM(a.shape, a.dtype) for a in (*srcs, *lands)]
    res = pl.pallas_call(
        body, name=name, in_specs=[IN_HBM] * (2 * n),
        out_specs=(SEM, SEM, *[IN_HBM] * (2 * n), pl.BlockSpec(memory_space=pltpu.VMEM)),
        out_shape=(pltpu.SemaphoreType.DMA((n * n_sem,)), pltpu.SemaphoreType.DMA((n * n_sem,)), *thru,
                   jax.ShapeDtypeStruct((SUBLANES, LANES), F32)),
        input_output_aliases={i: 2 + i for i in range(2 * n)}, compiler_params=SPLIT_COPY,
    )(*[pltpu.with_memory_space_constraint(a, pltpu.HBM) for a in (*srcs, *lands)])
    return (copies, n, res[0], res[1], res[2:2 + 2 * n]), res[-1][0, 0]


def split_copy_wait(name, handle, after):
    copies, n, send_sems, recv_sems, thru = handle

    def body(*refs):
        for cp in copies(refs[:n], refs[n:2 * n], refs[2 * n], refs[2 * n + 1]):
            cp.wait_send()
            cp.wait_recv()

    res = pl.pallas_call(
        body, name=name, in_specs=[IN_HBM] * (2 * n) + [SEM, SEM, pl.BlockSpec(memory_space=pl.ANY)],
        out_specs=[IN_HBM] * (2 * n), out_shape=[pltpu.HBM(a.shape, a.dtype) for a in thru],
        input_output_aliases={i: i for i in range(2 * n)}, compiler_params=SPLIT_COPY,
    )(*thru, send_sems, recv_sems, after)
    return list(res[n:])


def all_reduce_small(name, v):
    R, C = v.shape

    def body(v_ref, o_ref, gath, send_sems, recv_sems):
        x, y, c = _place()
        me, sibling = (x, y, c), (x, y, 1 - c)
        chips = _other_chips(x, y)

        def slot(px, py, pc):
            return gath.at[4 * px + 2 * py + pc]

        def copy(k, block, to, src=None):
            return pltpu.make_async_remote_copy(
                src_ref=slot(*block) if src is None else src, dst_ref=slot(*block),
                send_sem=send_sems.at[k], recv_sem=recv_sems.at[k], device_id=to, device_id_type=MESH)

        first = [copy(0, me, sibling, src=v_ref)]
        first += [copy(1 + j, me, (*chip, c), src=v_ref) for j, chip in enumerate(chips)]
        for cp in first:
            cp.start()
        slot(*me)[...] = v_ref[...]
        passed = [copy(4 + j, (*chip, c), sibling) for j, chip in enumerate(chips)]
        for j, chip in enumerate(chips):
            copy(1 + j, (*chip, c), me).wait_recv()
            passed[j].start()
        copy(0, sibling, me).wait_recv()
        for j, chip in enumerate(chips):
            copy(4 + j, (*chip, 1 - c), me).wait_recv()
        for cp in first + passed:
            cp.wait_send()
        acc = gath[0]
        for d in range(1, N_DEV):
            acc = acc + gath[d]
        o_ref[...] = acc

    vm = pl.BlockSpec(memory_space=pltpu.VMEM)
    return pl.pallas_call(
        body, name=name, in_specs=[vm], out_specs=vm, out_shape=jax.ShapeDtypeStruct((R, C), F32),
        scratch_shapes=[pltpu.VMEM((N_DEV, R, C), F32), pltpu.SemaphoreType.DMA((7,)), pltpu.SemaphoreType.DMA((7,))],
        compiler_params=pltpu.CompilerParams(vmem_limit_bytes=VMEM_LIMIT_BYTES),
    )(v)


LANES = 128
PACK_ROW_MULTIPLE = 1024
SMALL_SHARDED = {
    "w_in": ((D_MODEL, 1216), 1), "mla_w_uq": ((MLA_Q_RANK, 768), 1), "mla_w_ukv": ((MLA_KV_RANK, 1024), 1),
    "ssm_w_glu": ((SSM_WIDTH, SSM_WIDTH), 0), "w_o": ((D_MODEL, D_MODEL), 0), "xattn_w_q": ((D_MODEL, 512), 0),
    "xattn_w_kv": ((D_MODEL, 1024), 0), "xattn_w_o": ((512, D_MODEL), 1),
}
FFN_NAMES = ["ffn1_w_gate", "ffn1_w_up", "ffn1_w_down", "ffn2_w_gate", "ffn2_w_up", "ffn2_w_down"]
TRANSPOSED_VIEW = ("ffn1_w_gate", "ffn1_w_up", "ffn2_w_gate", "ffn2_w_up", "w_in", "mla_w_uq")


def _shard_shape(name):
    (r, cdim), ax = SMALL_SHARDED[name]
    return (r // N_CHIPS, cdim) if ax == 0 else (r, cdim // N_CHIPS)


def _pack_shards(shards):
    parts = []
    for name in SMALL_SHARDED:
        a = shards[name]
        lead = a.shape[:-2]
        parts.append(a.reshape(lead + (a.shape[-2] * a.shape[-1] // LANES, LANES)))
    rows = sum(q.shape[-2] for q in parts)
    parts.append(jnp.zeros(lead + (-rows % PACK_ROW_MULTIPLE, LANES), parts[0].dtype))
    return jnp.concatenate(parts, axis=-2)


def _unpack_shards(packed):
    out, r0 = {}, 0
    lead = packed.shape[:-2]
    for name in SMALL_SHARDED:
        r, cdim = _shard_shape(name)
        rows = r * cdim // LANES
        out[name] = packed[..., r0:r0 + rows, :].reshape(lead + (r, cdim))
        r0 += rows
    return out


def _full_from_shards(name, sh):
    (r, cdim), ax = SMALL_SHARDED[name]
    if ax == 0:
        return sh.reshape(r, cdim)
    return jnp.transpose(sh, (1, 0, 2)).reshape(r, cdim)


def _shards_from_full(name, full):
    (r, cdim), ax = SMALL_SHARDED[name]
    if ax == 0:
        return full.reshape(N_CHIPS, r // N_CHIPS, cdim)
    return jnp.transpose(full.reshape(r, N_CHIPS, cdim // N_CHIPS), (1, 0, 2))


SMALL_REPL = {
    "ffn1_norm": (1, 1024), "mix_norm": (1, 1024), "mla_q_norm": (1, 384), "mla_kv_norm": (1, 256),
    "mla_qk_norm_q": (1, 192), "mla_qk_norm_k": (1, 192), "ssm_a_re": (32, 64), "ssm_a_im": (32, 64),
    "ssm_log_dt": (32, 1), "ssm_b_re": (32, 64, 16), "ssm_b_im": (32, 64, 16), "ssm_c_re": (32, 16, 64),
    "ssm_c_im": (32, 16, 64), "ssm_d": (1, 512), "ssm_b_glu": (1, 512), "out_norm_mla": (1, 512),
    "out_norm_ssm": (1, 512), "xattn_norm": (1, 1024), "mem_norm": (1, 1024), "xattn_q_norm": (1, 128),
    "xattn_k_norm": (1, 128), "ffn2_norm": (1, 1024),
}


def _pack_repl(grads):
    flat = jnp.concatenate([grads[n].reshape(-1) for n in SMALL_REPL])
    rows = -(-flat.shape[0] // (LANES * SUBLANES)) * SUBLANES
    return jnp.pad(flat, (0, rows * LANES - flat.shape[0])).reshape(rows, LANES)


def _unpack_repl(packed):
    flat, out, o = packed.reshape(-1), {}, 0
    for n, shp in SMALL_REPL.items():
        size = int(np.prod(shp))
        out[n] = flat[o:o + size].reshape(shp)
        o += size
    return out


def _rope_tables(positions):
    half = MLA_ROPE // 2
    inv = ROPE_THETA ** (-jnp.arange(half, dtype=F32) / half)
    ang = positions.astype(F32)[:, None] * inv[None, :]
    cos, sin = jnp.cos(ang), jnp.sin(ang)
    S = positions.shape[0]
    z = lambda w: jnp.zeros((S, w), F32)
    keep = jnp.concatenate([jnp.ones((S, MLA_NOPE), F32), cos, cos, z(HEAD_PAD - MLA_QK)], axis=1)
    from_hi = jnp.concatenate([z(MLA_NOPE), -sin, z(HEAD_PAD - MLA_NOPE - half)], axis=1)
    from_lo = jnp.concatenate([z(MLA_NOPE + half), sin, z(HEAD_PAD - MLA_QK)], axis=1)
    return keep, from_hi, from_lo


def _norm_rope(x, g, keep, from_hi, from_lo):
    y = x * lax.rsqrt(jnp.sum(x * x, axis=-1, keepdims=True) * (1.0 / MLA_QK) + EPS) * g
    half = MLA_ROPE // 2
    return y * keep + _lane_roll(y, HEAD_PAD - half) * from_hi + _lane_roll(y, half) * from_lo


def local_step(x, mem, positions, target, w, wb, small_weights=None, ffn2_weights=None, on_grads=None):
    if small_weights is None:
        small_weights = lambda after: {}
    if ffn2_weights is None:
        ffn2_weights = lambda after: [wb[k] for k in FFN_NAMES[3:]]
    if on_grads is None:
        on_grads = lambda tag, g: 0.0
    S = x.shape[0]
    tm = min(TOKEN_TILE, S)
    nt = S // tm
    g1 = (nt,)
    gh = (MLA_HEADS, nt)
    tile = lambda arr, cols=None, cb=0, **kw: In(arr, (tm, arr.shape[1] if cols is None else cols), lambda i, cb=cb: (i, cb), **kw)
    par = lambda arr, **kw: In(arr, arr.shape, lambda *_: (0, 0), diff=True, **kw)
    otile = lambda cols, dt: Out((S, cols), dt, (tm, cols), lambda i: (i, 0))
    grads = {}

    x1, h1, g_1, u_1 = ffn_fwd("ffn1_fwd", x, w["ffn1_norm"], wb["ffn1_w_gate"], wb["ffn1_w_up"], wb["ffn1_w_down"])

    f_norm = lambda xv, g: (_rms(xv, g),)
    mix_ins = [tile(x1, diff=True), par(w["mix_norm"], acc=(0,))]
    mix_outs = [otile(D_MODEL, BF16)]
    (h2,) = seg_fwd("mix_norm_fwd", f_norm, g1, mix_ins, mix_outs)
    wb = {**wb, **small_weights(h2)}

    pq, pkv, pkr, pu = mm("w_in_fwd", [h2], [[wb["w_in_q"], wb["w_in_kv"], wb["w_in_kr"], wb["w_in_u"]]], [F32] * 4)

    f_lat = lambda a, b, ga, gb: (_rms(a, ga), _rms(b, gb))
    lat_ins = [tile(pq, diff=True), tile(pkv, diff=True), par(w["mla_q_norm"], acc=(0,)), par(w["mla_kv_norm"], acc=(0,))]
    lat_outs = [otile(MLA_Q_RANK, BF16), otile(MLA_KV_RANK, BF16)]
    cq, ckv = seg_fwd("latent_norm_fwd", f_lat, g1, lat_ins, lat_outs)

    (q0,) = mm("w_uq_fwd", [cq], [[wb["w_uq"]]], [F32])
    kn0, v0 = mm("w_ukv_fwd", [ckv], [[wb["w_ukv_k"], wb["w_ukv_v"]]], [F32, F32])

    keep, from_hi, from_lo = _rope_tables(positions)

    scale = MLA_QK ** -0.5

    def f_qk(qh, knh, kr, kp, fh, fl, gq, gk):
        return (_norm_rope(qh, gq, kp, fh, fl) * scale, _norm_rope(jnp.concatenate([knh, kr], axis=-1), gk, kp, fh, fl))

    hmap = lambda h, i: (i, h)
    tmap0 = lambda h, i: (i, 0)
    qk_ins = [In(q0, (tm, HEAD_PAD), hmap, diff=True), In(kn0, (tm, MLA_NOPE), hmap, diff=True),
              In(pkr, (tm, LANES), tmap0, diff=True,
                 grad=((MLA_HEADS, S, LANES), (None, tm, LANES), lambda h, i: (h, i, 0))),
              In(keep, (tm, HEAD_PAD), tmap0), In(from_hi, (tm, HEAD_PAD), tmap0), In(from_lo, (tm, HEAD_PAD), tmap0),
              In(w["qk_gain_q"], (1, HEAD_PAD), lambda h, i: (0, 0), diff=True, acc=(0, 1)),
              In(w["qk_gain_k"], (1, HEAD_PAD), lambda h, i: (0, 0), diff=True, acc=(0, 1))]
    head_out = Out((MLA_HEADS, S, HEAD_PAD), BF16, (None, tm, HEAD_PAD), lambda h, i: (h, i, 0))
    qh, kh = seg_fwd("qk_norm_rope_fwd", f_qk, gh, qk_ins, [head_out, head_out])

    o_mla, lse = attn_fwd("mla_attn_fwd", qh, kh, v0, t=min(ATTN_TILE, S))

    prep_grid = (SSM_BLOCKS,)
    prep_ins = ([_whole(w[k], diff=True, acc=(0,)) for k in ("ssm_a_re", "ssm_a_im", "ssm_log_dt")]
                + [In(w[k], (BLOCK_CH, SSM_STATE), lambda i: (i, 0), diff=True)
                   for k in ("ssm_bt_re", "ssm_bt_im", "ssm_c2_re", "ssm_c2_im")])
    blk_out = Out((SSM_BLOCKS, BLOCK_CH, PREP_LANES), BF16, (None, BLOCK_CH, PREP_LANES), lambda i: (i, 0, 0))
    prep_outs = [Out((SUBLANES, SSM_LANES), F32, (SUBLANES, PREP_LANES), lambda i: (0, i))] * 2 + [blk_out] * 4
    a_r8, a_i8, bb_r, bb_i, cb_r, ncb_i = seg_fwd("ssm_prep_fwd", _ssm_prep_f, prep_grid, prep_ins, prep_outs)

    xs_r, xs_i, y_lin = ssm_fwd("ssm_fwd", pu, bb_r, bb_i, cb_r, ncb_i, a_r8, a_i8)

    f_gelu = lambda yl, u, d: (jax.nn.gelu(yl + d * u),)
    gelu_ins = [tile(y_lin, diff=True), tile(pu, diff=True), par(w["ssm_d"], acc=(0,))]
    (gl,) = seg_fwd("ssm_gelu_fwd", f_gelu, g1, gelu_ins, [otile(SSM_WIDTH, F32)])
    (z,) = mm("ssm_glu_fwd", [gl], [[wb["ssm_w_glu"]]], [F32])

    def f_glu(g, zz, o, b, gm, gs):
        return (_rms(o, gm), _rms(g * jax.nn.sigmoid(zz + b), gs))

    glu_ins = [tile(gl, diff=True), tile(z, diff=True), tile(o_mla, diff=True), par(w["ssm_b_glu"], acc=(0,)),
               par(w["out_norm_mla"], acc=(0,)), par(w["out_norm_ssm"], acc=(0,))]
    glu_outs = [otile(SSM_WIDTH, BF16), otile(SSM_WIDTH, BF16)]
    yn_mla, yn_ssm = seg_fwd("glu_out_norm_fwd", f_glu, g1, glu_ins, glu_outs)

    (x2,) = mm("w_o_fwd", [yn_mla, yn_ssm], [[wb["w_o_mla"]], [wb["w_o_ssm"]]], [F32], adds=[x1])

    xn_ins = [tile(x2, diff=True), par(w["xattn_norm"], acc=(0,))]
    (h3,) = seg_fwd("xattn_norm_fwd", f_norm, g1, xn_ins, [otile(D_MODEL, BF16)])
    (xq0,) = mm("xattn_q_fwd", [h3], [[wb["xattn_w_q"]]], [F32])

    M = mem.shape[0]
    mem_ins = [In(mem, (M, D_MODEL), lambda i: (0, 0)), par(w["mem_norm"], acc=(0,))]
    (mn,) = seg_fwd("mem_norm_fwd", f_norm, (1,), mem_ins, [Out((M, D_MODEL), BF16, (M, D_MODEL), lambda i: (0, 0))])
    (kvm,) = mm("xattn_kv_fwd", [mn], [[wb["xattn_w_kv"]]], [F32])

    xscale = XHD ** -0.5

    def f_xattn(qv, kv_k, kv_v, gq, gk):
        qn, kn = _rms(qv, gq), _rms(kv_k, gk)
        s = _bdot_nt(qn, kn) * xscale
        p = jax.nn.softmax(s, axis=-1)
        return (_bdot_nn(p, kv_v),)

    def xa_ins(gain_q):
        return [In(xq0, (tm, XHD), hmap, diff=True),
                In(kvm, (M, XHD), lambda h, i: (0, h), diff=True, acc=(1,), grad=((M, XH * XHD), (M, XHD), lambda h, i: (0, h))),
                In(kvm, (M, XHD), lambda h, i: (0, XH + h), diff=True, acc=(1,), grad=((M, XH * XHD), (M, XHD), lambda h, i: (0, h))),
                In(gain_q, (1, XHD), lambda h, i: (0, 0), diff=True, acc=(0, 1)),
                In(w["xattn_k_norm"], (1, XHD), lambda h, i: (0, 0), diff=True, acc=(0, 1))]

    xa_outs = [Out((S, XH * XHD), F32, (tm, XHD), hmap)]
    (o2,) = seg_fwd("xattn_fwd", f_xattn, gh, xa_ins(w["xattn_q_norm"]), xa_outs)

    (x3,) = mm("xattn_o_fwd", [o2], [[wb["xattn_w_o"]]], [F32], adds=[x2])

    wg2, wu2, wd2 = ffn2_weights(x3)
    x4, h4, g_2, u_2 = ffn_fwd("ffn2_fwd", x3, w["ffn2_norm"], wg2, wu2, wd2)
    dx4, parts = loss_grad("loss", x4, target)
    loss = jnp.sum(parts[::SUBLANES, 0])

    dx3, grads["ffn2_norm"], dg_2, du_2 = ffn_bwd_act("ffn2_bwd_act", dx4, x3, w["ffn2_norm"], g_2, u_2, wg2, wu2, wd2)
    grads["ffn2_w_gate"], grads["ffn2_w_up"], grads["ffn2_w_down"] = ffn_bwd_w("ffn2_bwd_w", h4, dx4, g_2, u_2, dg_2, du_2)
    sent = on_grads("ffn2", grads)

    (do2,) = mm("xattn_o_bwd", [dx3], [[wb["xattn_w_o"]]], [F32], trans=True)
    (grads["xattn_w_o"],) = mm_tn("xattn_o_bwd_w", [o2], [dx3], [(0, [0])])

    dxq0, dkm, dvm, grads["xattn_q_norm"], grads["xattn_k_norm"] = seg_bwd(
        "xattn_bwd", f_xattn, gh, xa_ins(w["xattn_q_norm"] + sent), xa_outs, [do2])
    (dh3,) = mm("xattn_q_bwd", [dxq0], [[wb["xattn_w_q"]]], [F32], trans=True)
    (grads["xattn_w_q"],) = mm_tn("xattn_q_bwd_w", [h3], [dxq0], [(0, [0])])
    wkv_k, wkv_v = wb["xattn_w_kv"][:, :XH * XHD], wb["xattn_w_kv"][:, XH * XHD:]
    (dmn,) = mm("xattn_kv_bwd", [dkm, dvm], [[wkv_k], [wkv_v]], [F32], trans=True)
    gk_w, gv_w = mm_tn("xattn_kv_bwd_w", [mn], [dkm, dvm], [(0, [0]), (0, [1])])
    grads["xattn_w_kv"] = jnp.concatenate([gk_w, gv_w], axis=1)
    (grads["mem_norm"],) = seg_bwd("mem_norm_bwd", f_norm, (1,), mem_ins,
                                   [Out((M, D_MODEL), BF16, (M, D_MODEL), lambda i: (0, 0))], [dmn])

    dx2, grads["xattn_norm"] = seg_bwd("xattn_norm_bwd", f_norm, g1, xn_ins, [otile(D_MODEL, BF16)], [dh3], adds={0: dx3})

    dyn_mla, dyn_ssm = mm("w_o_bwd", [dx2], [[wb["w_o_mla"], wb["w_o_ssm"]]], [F32, F32], trans=True)
    go_mla, go_ssm = mm_tn("w_o_bwd_w", [yn_mla, yn_ssm], [dx2], [(0, [0]), (1, [0])])
    grads["w_o"] = jnp.concatenate([go_mla, go_ssm], axis=0)

    dgl_a, dz, do_mla, grads["ssm_b_glu"], grads["out_norm_mla"], grads["out_norm_ssm"] = seg_bwd(
        "glu_out_norm_bwd", f_glu, g1, glu_ins, glu_outs, [dyn_mla, dyn_ssm])
    (dgl,) = mm("ssm_glu_bwd", [dz], [[wb["ssm_w_glu"]]], [F32], trans=True, adds=[dgl_a])
    (grads["ssm_w_glu"],) = mm_tn("ssm_glu_bwd_w", [gl], [dz], [(0, [0])])
    dy_lin, du_a, grads["ssm_d"] = seg_bwd("ssm_gelu_bwd", f_gelu, g1, gelu_ins, [otile(SSM_WIDTH, F32)], [dgl])

    du, dbb_r, dbb_i, dcb_r, dncb_i, da_r8, da_i8 = ssm_bwd("ssm_bwd", dy_lin, du_a, pu, xs_r, xs_i, bb_r, bb_i, cb_r, ncb_i,
                                                            a_r8, a_i8)
    prep_g = seg_bwd("ssm_prep_bwd", _ssm_prep_f, prep_grid, prep_ins, prep_outs, [da_r8, da_i8, dbb_r, dbb_i, dcb_r, dncb_i])
    for k, gname in enumerate(("ssm_a_re", "ssm_a_im", "ssm_log_dt", "ssm_bt_re", "ssm_bt_im", "ssm_c2_re", "ssm_c2_im")):
        grads[gname] = prep_g[k]

    dqh, dkh, dv0 = attn_bwd("mla_attn_bwd", qh, kh, v0, do_mla, o_mla, lse, t=min(ATTN_TILE, S))

    dq0, dkn0, dkr4, grads["qk_gain_q"], grads["qk_gain_k"] = seg_bwd(
        "qk_norm_rope_bwd", f_qk, gh, qk_ins, [head_out, head_out], [dqh, dkh])

    (dcq,) = mm("w_uq_bwd", [dq0], [[wb["w_uq"]]], [F32], trans=True)
    (grads["w_uq"],) = mm_tn("w_uq_bwd_w", [cq], [dq0], [(0, [0])])
    (dckv,) = mm("w_ukv_bwd", [dkn0, dv0], [[wb["w_ukv_k"]], [wb["w_ukv_v"]]], [F32], trans=True)
    grads["w_ukv_k"], grads["w_ukv_v"] = mm_tn("w_ukv_bwd_w", [ckv], [dkn0, dv0], [(0, [0]), (0, [1])])

    dpq, dpkv, grads["mla_q_norm"], grads["mla_kv_norm"] = seg_bwd("latent_norm_bwd", f_lat, g1, lat_ins, lat_outs, [dcq, dckv])

    dkr_list = [dkr4[h] for h in range(MLA_HEADS)]
    (dh2,) = mm("w_in_bwd", [dpq, dpkv, du] + dkr_list,
                [[wb["w_in_q"]], [wb["w_in_kv"]], [wb["w_in_u"]]] + [[wb["w_in_kr"]]] * MLA_HEADS, [F32], trans=True)
    grads["w_in_q"], grads["w_in_kv"], grads["w_in_u"], grads["w_in_kr"] = mm_tn(
        "w_in_bwd_w", [h2], [dpq, dpkv, du] + dkr_list, [(0, [0]), (0, [1]), (0, [2]), (0, [3, 4, 5, 6])])

    sent = on_grads("small", grads)
    dx1, grads["mix_norm"] = seg_bwd("mix_norm_bwd", f_norm, g1, mix_ins, mix_outs, [dh2], adds={0: dx2})

    dx, grads["ffn1_norm"], dg_1, du_1 = ffn_bwd_act("ffn1_bwd_act", dx1, x, w["ffn1_norm"] + sent, g_1, u_1,
                                                     wb["ffn1_w_gate"], wb["ffn1_w_up"], wb["ffn1_w_down"])
    grads["ffn1_w_gate"], grads["ffn1_w_up"], grads["ffn1_w_down"] = ffn_bwd_w("ffn1_bwd_w", h1, dx1, g_1, u_1, dg_1, du_1)
    return loss, dx, grads


def _pad_cols(a, n):
    return jnp.pad(a, ((0, 0), (0, n - a.shape[1])))


def _step_weights(full_bf16):
    wb = {}
    w_in = full_bf16["w_in"]
    wb["w_in_q"] = w_in[:, :MLA_Q_RANK]
    wb["w_in_kv"] = w_in[:, MLA_Q_RANK:MLA_Q_RANK + MLA_KV_RANK]
    wb["w_in_kr"] = _pad_cols(w_in[:, MLA_Q_RANK + MLA_KV_RANK:MLA_Q_RANK + MLA_KV_RANK + MLA_ROPE], LANES)
    wb["w_in_u"] = w_in[:, MLA_Q_RANK + MLA_KV_RANK + MLA_ROPE:]
    uq = full_bf16["mla_w_uq"].reshape(MLA_Q_RANK, MLA_HEADS, MLA_QK)
    wb["w_uq"] = jnp.pad(uq, ((0, 0), (0, 0), (0, HEAD_PAD - MLA_QK))).reshape(MLA_Q_RANK, MLA_HEADS * HEAD_PAD)
    ukv = full_bf16["mla_w_ukv"].reshape(MLA_KV_RANK, MLA_HEADS, MLA_NOPE + MLA_V)
    wb["w_ukv_k"] = ukv[:, :, :MLA_NOPE].reshape(MLA_KV_RANK, MLA_HEADS * MLA_NOPE)
    wb["w_ukv_v"] = ukv[:, :, MLA_NOPE:].reshape(MLA_KV_RANK, MLA_HEADS * MLA_V)
    wb["ssm_w_glu"] = full_bf16["ssm_w_glu"]
    wb["w_o_mla"] = full_bf16["w_o"][:SSM_WIDTH]
    wb["w_o_ssm"] = full_bf16["w_o"][SSM_WIDTH:]
    for k in ("xattn_w_q", "xattn_w_kv", "xattn_w_o"):
        wb[k] = full_bf16[k]
    return wb


def _problem_sharded_grads(g):
    out = {}
    kr = g["w_in_kr"][:, :MLA_ROPE]
    out["w_in"] = jnp.concatenate([g["w_in_q"], g["w_in_kv"], kr, g["w_in_u"]], axis=1)
    out["mla_w_uq"] = g["w_uq"].reshape(MLA_Q_RANK, MLA_HEADS, HEAD_PAD)[:, :, :MLA_QK].reshape(MLA_Q_RANK, MLA_HEADS * MLA_QK)
    k3 = g["w_ukv_k"].reshape(MLA_KV_RANK, MLA_HEADS, MLA_NOPE)
    v3 = g["w_ukv_v"].reshape(MLA_KV_RANK, MLA_HEADS, MLA_V)
    out["mla_w_ukv"] = jnp.concatenate([k3, v3], axis=2).reshape(MLA_KV_RANK, MLA_HEADS * (MLA_NOPE + MLA_V))
    for k in ("ssm_w_glu", "w_o", "xattn_w_q", "xattn_w_kv", "xattn_w_o"):
        out[k] = g[k]
    return out


def _problem_repl_grads(g):
    out = {}
    out["mla_qk_norm_q"] = g["qk_gain_q"][:, :MLA_QK]
    out["mla_qk_norm_k"] = g["qk_gain_k"][:, :MLA_QK]
    out["ssm_b_re"] = jnp.transpose(g["ssm_bt_re"].reshape(SSM_GROUPS, SSM_GROUP, SSM_STATE), (0, 2, 1))
    out["ssm_b_im"] = jnp.transpose(g["ssm_bt_im"].reshape(SSM_GROUPS, SSM_GROUP, SSM_STATE), (0, 2, 1))
    out["ssm_c_re"] = g["ssm_c2_re"].reshape(SSM_GROUPS, SSM_GROUP, SSM_STATE)
    out["ssm_c_im"] = g["ssm_c2_im"].reshape(SSM_GROUPS, SSM_GROUP, SSM_STATE)
    for k in SMALL_REPL:
        if k not in out:
            out[k] = g[k]
    return out


def _problem_grads(g):
    out = {**_problem_sharded_grads(g), **_problem_repl_grads(g)}
    out.update({k: g[k] for k in FFN_NAMES})
    return out


def _step_params(p):
    row = lambda a: a.reshape(1, -1)
    w = {k: row(p[k]) for k in ("ffn1_norm", "mix_norm", "mla_q_norm", "mla_kv_norm", "ssm_b_glu", "out_norm_mla",
                                "out_norm_ssm", "xattn_norm", "mem_norm", "xattn_q_norm", "xattn_k_norm", "ffn2_norm")}
    w["qk_gain_q"] = _pad_cols(row(p["mla_qk_norm_q"]), HEAD_PAD)
    w["qk_gain_k"] = _pad_cols(row(p["mla_qk_norm_k"]), HEAD_PAD)
    w["ssm_a_re"], w["ssm_a_im"] = p["ssm_a_re"], p["ssm_a_im"]
    w["ssm_log_dt"] = p["ssm_log_dt"].reshape(SSM_GROUPS, 1)
    w["ssm_bt_re"] = jnp.transpose(p["ssm_b_re"], (0, 2, 1)).reshape(SSM_WIDTH, SSM_STATE)
    w["ssm_bt_im"] = jnp.transpose(p["ssm_b_im"], (0, 2, 1)).reshape(SSM_WIDTH, SSM_STATE)
    w["ssm_c2_re"] = p["ssm_c_re"].reshape(SSM_WIDTH, SSM_STATE)
    w["ssm_c2_im"] = p["ssm_c_im"].reshape(SSM_WIDTH, SSM_STATE)
    w["ssm_d"] = p["ssm_d"].reshape(1, SSM_WIDTH)
    return w


ARG_NAMES = ['x', 'mem', 'positions', 'ffn1_norm', 'ffn1_w_gate', 'ffn1_w_up', 'ffn1_w_down', 'mix_norm', 'w_in', 'mla_q_norm', 'mla_w_uq', 'mla_kv_norm', 'mla_w_ukv', 'mla_qk_norm_q', 'mla_qk_norm_k', 'ssm_a_re', 'ssm_a_im', 'ssm_log_dt', 'ssm_b_re', 'ssm_b_im', 'ssm_c_re', 'ssm_c_im', 'ssm_d', 'ssm_w_glu', 'ssm_b_glu', 'out_norm_mla', 'out_norm_ssm', 'w_o', 'xattn_norm', 'mem_norm', 'xattn_w_q', 'xattn_w_kv', 'xattn_q_norm', 'xattn_k_norm', 'xattn_w_o', 'ffn2_norm', 'ffn2_w_gate', 'ffn2_w_up', 'ffn2_w_down']
WEIGHT_NAMES = ARG_NAMES[3:]


def _gather_weights(p, c):
    half = lambda a: lax.dynamic_slice_in_dim(a, c * (a.shape[0] // 2), a.shape[0] // 2, axis=0)
    ffn1 = [half(p[k].astype(BF16)) for k in FFN_NAMES[:3]]
    small = [half(_pack_shards({k: p[k].astype(BF16) for k in SMALL_SHARDED}))]
    ffn2 = [half(p[k].astype(BF16)) for k in FFN_NAMES[3:]]
    me = 4 * lax.axis_index("x") + 2 * lax.axis_index("y") + c
    own = lambda got, blocks: [lax.dynamic_update_index_in_dim(g, b, me, 0) for g, b in zip(got, blocks)]
    as_shards = lambda a: a.reshape(N_CHIPS, 2 * a.shape[1], a.shape[2])
    landing = lambda blocks: [jax.ShapeDtypeStruct((N_DEV,) + b.shape, b.dtype) for b in blocks]
    got1 = own(all_gather_halves("all_gather_weights_a", ffn1), ffn1)
    got1, small = lax.optimization_barrier((got1, small))
    flight_s, sent_s = split_copy_start("gather_small_start", _gather_copies, N_DEV - 1, small, landing(small))
    sent_s, ffn2 = lax.optimization_barrier((sent_s, ffn2))
    flight_2, sent_2 = split_copy_start("gather_ffn2_start", _gather_copies, N_DEV - 1, ffn2, landing(ffn2))
    wb = {k: as_shards(a) for k, a in zip(FFN_NAMES[:3], got1)}

    def small_weights(after):
        (got,) = own(split_copy_wait("gather_small_wait", flight_s, after), small)
        shards = _unpack_shards(got.reshape(N_CHIPS, -1, LANES))
        return _step_weights({k: _full_from_shards(k, shards[k]) for k in SMALL_SHARDED})

    def ffn2_weights(after):
        return [as_shards(a) for a in own(split_copy_wait("gather_ffn2_wait", flight_2, after), ffn2)]

    return wb, small_weights, ffn2_weights, sent_s + sent_2


class _GradReduce:
    def __init__(self, c):
        self.c, self.c_arr = c, jnp.reshape(c, (1,)).astype(jnp.int32)
        self.chip = 2 * lax.axis_index("x") + lax.axis_index("y")
        self.flights = []

    def start(self, tag, arrs):
        split = [a.reshape(N_CHIPS, 2, a.shape[1] // 2, a.shape[2]) for a in arrs]
        got = pair_send_halves(f"grad_pair_send_{tag}", split)
        parts = [pair_add(f"grad_pair_add_{tag}_{k}", s, g, self.c_arr) for k, (s, g) in enumerate(zip(split, got))]
        flight, sent = split_copy_start(f"grad_scatter_start_{tag}", _scatter_copies, N_CHIPS - 1, parts, parts)
        self.flights.append((tag, parts, flight))
        return sent

    def finish(self, after):
        halves = []
        for tag, parts, flight in self.flights:
            landed = split_copy_wait(f"grad_scatter_wait_{tag}", flight, after)
            for k, (q, p) in enumerate(zip(landed, parts)):
                mine = lax.dynamic_index_in_dim(p, self.chip, 0, keepdims=False)
                halves.append(sum_chips(f"grad_sum_{tag}_{k}", lax.dynamic_update_index_in_dim(q, mine, self.chip, 0)))
        tags = "_".join(t for t, _, _ in self.flights)
        self.flights = []
        theirs = pair_exchange(f"grad_pair_exchange_{tags}", halves)
        return [jnp.where(self.c == 0, jnp.concatenate([h, t], axis=0), jnp.concatenate([t, h], axis=0))
                for h, t in zip(halves, theirs)]


def kernel(x, mem, positions, ffn1_norm, ffn1_w_gate, ffn1_w_up, ffn1_w_down, mix_norm, w_in, mla_q_norm, mla_w_uq, mla_kv_norm, mla_w_ukv, mla_qk_norm_q, mla_qk_norm_k, ssm_a_re, ssm_a_im, ssm_log_dt, ssm_b_re, ssm_b_im, ssm_c_re, ssm_c_im, ssm_d, ssm_w_glu, ssm_b_glu, out_norm_mla, out_norm_ssm, w_o, xattn_norm, mem_norm, xattn_w_q, xattn_w_kv, xattn_q_norm, xattn_k_norm, xattn_w_o, ffn2_norm, ffn2_w_gate, ffn2_w_up, ffn2_w_down, loss_target, m_ffn1_norm, m_ffn1_w_gate, m_ffn1_w_up, m_ffn1_w_down, m_mix_norm, m_w_in, m_mla_q_norm, m_mla_w_uq, m_mla_kv_norm, m_mla_w_ukv, m_mla_qk_norm_q, m_mla_qk_norm_k, m_ssm_a_re, m_ssm_a_im, m_ssm_log_dt, m_ssm_b_re, m_ssm_b_im, m_ssm_c_re, m_ssm_c_im, m_ssm_d, m_ssm_w_glu, m_ssm_b_glu, m_out_norm_mla, m_out_norm_ssm, m_w_o, m_xattn_norm, m_mem_norm, m_xattn_w_q, m_xattn_w_kv, m_xattn_q_norm, m_xattn_k_norm, m_xattn_w_o, m_ffn2_norm, m_ffn2_w_gate, m_ffn2_w_up, m_ffn2_w_down, v_ffn1_norm, v_ffn1_w_gate, v_ffn1_w_up, v_ffn1_w_down, v_mix_norm, v_w_in, v_mla_q_norm, v_mla_w_uq, v_mla_kv_norm, v_mla_w_ukv, v_mla_qk_norm_q, v_mla_qk_norm_k, v_ssm_a_re, v_ssm_a_im, v_ssm_log_dt, v_ssm_b_re, v_ssm_b_im, v_ssm_c_re, v_ssm_c_im, v_ssm_d, v_ssm_w_glu, v_ssm_b_glu, v_out_norm_mla, v_out_norm_ssm, v_w_o, v_xattn_norm, v_mem_norm, v_xattn_w_q, v_xattn_w_kv, v_xattn_q_norm, v_xattn_k_norm, v_xattn_w_o, v_ffn2_norm, v_ffn2_w_gate, v_ffn2_w_up, v_ffn2_w_down):
    args = dict(locals())
    c = lax.axis_index("c")
    view = lambda k, a: jnp.swapaxes(a, 0, 1) if k in TRANSPOSED_VIEW else a
    p = {k: view(k, args[k][0]) for k in WEIGHT_NAMES}
    mom = {k: view(k, args["m_" + k][0]) for k in WEIGHT_NAMES}
    var = {k: view(k, args["v_" + k][0]) for k in WEIGHT_NAMES}
    natural = {k: view(k, p[k]) for k in WEIGHT_NAMES}

    wb, small_weights, ffn2_weights, sent = _gather_weights(
        {k: (p[k] if k in FFN_NAMES else natural[k]) for k in WEIGHT_NAMES}, c)
    w = _step_params(natural)
    w["ffn1_norm"] = w["ffn1_norm"] + sent
    early, late = _GradReduce(c), _GradReduce(c)

    def on_grads(tag, g):
        if tag == "ffn2":
            return early.start(tag, [g[k] for k in FFN_NAMES[3:]])
        full = _problem_sharded_grads(g)
        return early.start(tag, [_pack_shards({k: _shards_from_full(k, full[k]) for k in SMALL_SHARDED})])

    loss, dx, g = local_step(x[0], mem[0], positions[0], loss_target[0], w, wb, small_weights, ffn2_weights, on_grads)
    loss = lax.psum(loss, ("x", "y", "c"))

    sent = late.start("ffn1", [g[k] for k in FFN_NAMES[:3]])
    shards = early.finish(dx[:SUBLANES, :LANES] + sent)
    grad = dict(zip(FFN_NAMES[3:], shards[:3]))
    small_sharded = _unpack_shards(shards[3])
    grad.update({k: view(k, small_sharded[k]) for k in SMALL_SHARDED})
    repl = _problem_repl_grads(g)
    grad.update(_unpack_repl(all_reduce_small("grad_all_reduce_small", _pack_repl({k: repl[k] for k in SMALL_REPL}))))

    delta, new_m, new_v = {}, {}, {}
    small = [k for k in WEIGHT_NAMES if k not in FFN_NAMES and k not in SMALL_SHARDED]
    as2d = lambda a: a.reshape(-1, a.shape[-1])

    def update(k):
        delta[k], new_m[k], new_v[k] = adamw_big("adamw_" + k, as2d(p[k]), as2d(grad[k]), as2d(mom[k]), as2d(var[k]))

    for k in WEIGHT_NAMES:
        if k not in small and k not in FFN_NAMES[:3]:
            update(k)
    ds, nms, nvs = adamw_small("adamw_small", [as2d(p[k]) for k in small], [as2d(grad[k].reshape(p[k].shape)) for k in small],
                               [as2d(mom[k]) for k in small], [as2d(var[k]) for k in small])
    for k, d, nm, nv in zip(small, ds, nms, nvs):
        delta[k], new_m[k], new_v[k] = d, nm, nv

    grad.update(zip(FFN_NAMES[:3], late.finish(delta[FFN_NAMES[-1]])))
    for k in FFN_NAMES[:3]:
        update(k)

    shaped = lambda d, k: view(k, d.reshape(p[k].shape)).reshape(args[k].shape)
    return (loss, dx[None], *[shaped(grad[k], k) for k in WEIGHT_NAMES], *[shaped(delta[k], k) for k in WEIGHT_NAMES],
            *[shaped(new_m[k], k) for k in WEIGHT_NAMES], *[shaped(new_v[k], k) for k in WEIGHT_NAMES])
```

```python
import functools
import math

import jax
import jax.numpy as jnp
import numpy as np
from jax import lax
from jax.experimental import pallas as pl
from jax.experimental.pallas import tpu as pltpu

F32, BF16 = jnp.float32, jnp.bfloat16
EPS = 1e-6
MESH = pl.DeviceIdType.MESH

D_MODEL, D_FF = 1024, 2752
MLA_HEADS, MLA_Q_RANK, MLA_KV_RANK, MLA_NOPE, MLA_ROPE, MLA_V = 4, 384, 256, 128, 64, 128
MLA_QK = MLA_NOPE + MLA_ROPE
HEAD_PAD = 256
SSM_WIDTH, SSM_GROUP, SSM_GROUPS, SSM_STATE = 512, 16, 32, 64
SSM_LANES = SSM_GROUPS * SSM_STATE
XH, XHD = 4, 128
ROPE_THETA = 10000.0
ADAM_LR, ADAM_B1, ADAM_B2, ADAM_EPS, ADAM_WD, ADAM_STEP = 0.001, 0.9, 0.999, 1e-08, 0.01, 10
N_CHIPS, N_CORES, N_DEV = 4, 2, 8

VMEM_LIMIT_BYTES = 56 * 2**20
TOKEN_TILE = 512
FFN_WIDE_TILE = 1024
ATTN_TILE = 512
SCAN_TIME_TILE = 1024
SCAN_LANE_TILE = 256
SUBLANES = 8


def _params(n_axes):
    return pltpu.CompilerParams(dimension_semantics=("arbitrary",) * n_axes, vmem_limit_bytes=VMEM_LIMIT_BYTES)


def _first(axes):
    cond = None
    for a in axes:
        c = pl.program_id(a) == 0
        cond = c if cond is None else jnp.logical_and(cond, c)
    return cond


def mm(name, xs, ws, out_dtypes, *, trans=False, adds=None, tm=TOKEN_TILE):
    rows = xs[0].shape[0]
    tm = min(tm, rows)
    n_in, n_out = len(xs), len(out_dtypes)
    pairs = [(i, j) for i in range(n_in) for j in range(n_out) if ws[i][j] is not None]
    w_list = [ws[i][j] for (i, j) in pairs]
    adds = list(adds) if adds is not None else [None] * n_out
    add_list = [a for a in adds if a is not None]
    out_cols = [None] * n_out
    for (i, j), w in zip(pairs, w_list):
        out_cols[j] = w.shape[0] if trans else w.shape[1]
    contract = (((1,), (1 if trans else 0,)), ((), ()))

    def body(*refs):
        x_refs = refs[:n_in]
        w_refs = refs[n_in:n_in + len(pairs)]
        a_refs = list(refs[n_in + len(pairs):n_in + len(pairs) + len(add_list)])
        o_refs = refs[n_in + len(pairs) + len(add_list):]
        xb = [None] * n_in
        for j in range(n_out):
            acc = None
            for p, (i, jj) in enumerate(pairs):
                if jj != j:
                    continue
                if xb[i] is None:
                    xb[i] = x_refs[i][...].astype(BF16)
                d = lax.dot_general(xb[i], w_refs[p][...].astype(BF16), contract, preferred_element_type=F32)
                acc = d if acc is None else acc + d
            if adds[j] is not None:
                acc = acc + a_refs.pop(0)[...].astype(F32)
            o_refs[j][...] = acc.astype(o_refs[j].dtype)

    in_specs = ([pl.BlockSpec((tm, x.shape[1]), lambda i: (i, 0)) for x in xs]
                + [pl.BlockSpec(w.shape, lambda i: (0, 0)) for w in w_list]
                + [pl.BlockSpec((tm, a.shape[1]), lambda i: (i, 0)) for a in add_list])
    outs = pl.pallas_call(
        body, name=name, grid=(rows // tm,), in_specs=in_specs,
        out_specs=[pl.BlockSpec((tm, n), lambda i: (i, 0)) for n in out_cols],
        out_shape=[jax.ShapeDtypeStruct((rows, n), dt) for n, dt in zip(out_cols, out_dtypes)],
        compiler_params=_params(1),
    )(*xs, *w_list, *add_list)
    return list(outs)


def mm_tn(name, xs, dys, pairs, *, tm=TOKEN_TILE):
    rows = xs[0].shape[0]
    tm = min(tm, rows)
    n_x, n_dy = len(xs), len(dys)
    contract = (((0,), (0,)), ((), ()))

    def body(*refs):
        x_refs, dy_refs, o_refs = refs[:n_x], refs[n_x:n_x + n_dy], refs[n_x + n_dy:]
        @pl.when(pl.program_id(0) == 0)
        def _():
            for o in o_refs:
                o[...] = jnp.zeros_like(o)

        for k, (i, js) in enumerate(pairs):
            dy = None
            for j in js:
                t = dy_refs[j][...].astype(F32)
                dy = t if dy is None else dy + t
            o_refs[k][...] += lax.dot_general(x_refs[i][...].astype(BF16), dy.astype(BF16), contract,
                                              preferred_element_type=F32)

    shapes = [(xs[i].shape[1], dys[js[0]].shape[1]) for (i, js) in pairs]
    outs = pl.pallas_call(
        body, name=name, grid=(rows // tm,),
        in_specs=[pl.BlockSpec((tm, a.shape[1]), lambda i: (i, 0)) for a in (*xs, *dys)],
        out_specs=[pl.BlockSpec(s, lambda i: (0, 0)) for s in shapes],
        out_shape=[jax.ShapeDtypeStruct(s, F32) for s in shapes],
        compiler_params=_params(1),
    )(*xs, *dys)
    return list(outs)


class In:
    def __init__(self, arr, block, imap, *, diff=False, acc=None, grad=None):
        self.arr, self.block, self.imap, self.diff, self.acc, self.grad = arr, block, imap, diff, acc, grad

    def spec(self):
        return pl.BlockSpec(self.block, self.imap)


class Out:
    def __init__(self, shape, dtype, block, imap):
        self.shape, self.dtype, self.block, self.imap = shape, dtype, block, imap

    def spec(self):
        return pl.BlockSpec(self.block, self.imap)


def seg_fwd(name, f, grid, ins, outs):
    n_in = len(ins)

    def body(*refs):
        res = f(*[r[...] for r in refs[:n_in]])
        for o_ref, r in zip(refs[n_in:], res):
            o_ref[...] = r.astype(o_ref.dtype)

    res = pl.pallas_call(
        body, name=name, grid=grid, in_specs=[i.spec() for i in ins], out_specs=[o.spec() for o in outs],
        out_shape=[jax.ShapeDtypeStruct(o.shape, o.dtype) for o in outs], compiler_params=_params(len(grid)),
    )(*[i.arr for i in ins])
    return list(res)


def seg_bwd(name, f, grid, ins, outs, cts, adds=None):
    n_in, n_ct = len(ins), len(cts)
    diff_idx = [k for k, i in enumerate(ins) if i.diff]
    adds = adds or {}
    add_keys = sorted(adds)
    add_list = [adds[k] for k in add_keys]

    def body(*refs):
        in_refs, ct_refs = refs[:n_in], refs[n_in:n_in + n_ct]
        add_refs = dict(zip(add_keys, refs[n_in + n_ct:n_in + n_ct + len(add_list)]))
        g_refs = refs[n_in + n_ct + len(add_list):]
        vals = [r[...] for r in in_refs]

        def g(*dv):
            full = list(vals)
            for k, v in zip(diff_idx, dv):
                full[k] = v
            return tuple(f(*full))

        _, pull = jax.vjp(g, *[vals[k].astype(F32) for k in diff_idx])
        grads = pull(tuple(c[...].astype(F32) for c in ct_refs))
        for n, (k, gr) in enumerate(zip(diff_idx, grads)):
            if k in add_refs:
                gr = gr + add_refs[k][...].astype(F32)
            if ins[k].acc is None:
                g_refs[n][...] = gr.astype(g_refs[n].dtype)
            else:
                @pl.when(_first(ins[k].acc))
                def _(n=n):
                    g_refs[n][...] = jnp.zeros_like(g_refs[n])

                g_refs[n][...] += gr

    g_specs, g_shapes = [], []
    for k in diff_idx:
        i = ins[k]
        shape, block, imap = i.grad if i.grad is not None else (i.arr.shape, i.block, i.imap)
        g_specs.append(pl.BlockSpec(block, imap))
        g_shapes.append(jax.ShapeDtypeStruct(shape, F32))
    in_specs = ([i.spec() for i in ins] + [o.spec() for o in outs]
                + [pl.BlockSpec(ins[k].block, ins[k].imap) for k in add_keys])
    res = pl.pallas_call(
        body, name=name, grid=grid, in_specs=in_specs, out_specs=g_specs, out_shape=g_shapes,
        compiler_params=_params(len(grid)),
    )(*[i.arr for i in ins], *cts, *add_list)
    return list(res)


def _rms(x, g):
    return x * lax.rsqrt(jnp.mean(x * x, axis=-1, keepdims=True) + EPS) * g


@jax.custom_vjp
def _bdot_nt(a, b):
    return lax.dot_general(a.astype(BF16), b.astype(BF16), (((1,), (1,)), ((), ())), preferred_element_type=F32)


def _bdot_nt_fwd(a, b):
    return _bdot_nt(a, b), (a, b)


def _bdot_nt_bwd(res, g):
    a, b = res
    gb = g.astype(BF16)
    da = lax.dot_general(gb, b.astype(BF16), (((1,), (0,)), ((), ())), preferred_element_type=F32)
    db = lax.dot_general(gb, a.astype(BF16), (((0,), (0,)), ((), ())), preferred_element_type=F32)
    return da, db


_bdot_nt.defvjp(_bdot_nt_fwd, _bdot_nt_bwd)


@jax.custom_vjp
def _bdot_nn(a, b):
    return lax.dot_general(a.astype(BF16), b.astype(BF16), (((1,), (0,)), ((), ())), preferred_element_type=F32)


def _bdot_nn_fwd(a, b):
    return _bdot_nn(a, b), (a, b)


def _bdot_nn_bwd(res, g):
    a, b = res
    gb = g.astype(BF16)
    da = lax.dot_general(gb, b.astype(BF16), (((1,), (1,)), ((), ())), preferred_element_type=F32)
    db = lax.dot_general(a.astype(BF16), gb, (((0,), (0,)), ((), ())), preferred_element_type=F32)
    return da, db


_bdot_nn.defvjp(_bdot_nn_fwd, _bdot_nn_bwd)


@functools.partial(jax.custom_vjp, nondiff_argnums=(1,))
def _lane_roll(x, shift):
    return pltpu.roll(x, shift, 1)


def _lane_roll_fwd(x, shift):
    return pltpu.roll(x, shift, 1), None


def _lane_roll_bwd(shift, _, g):
    return (pltpu.roll(g, (g.shape[1] - shift) % g.shape[1], 1),)


_lane_roll.defvjp(_lane_roll_fwd, _lane_roll_bwd)


def _hp_dot(a, b):
    return jnp.dot(a, b, precision=lax.Precision.HIGHEST, preferred_element_type=F32)


NT_DIMS = (((1,), (1,)), ((), ()))
TN_DIMS = (((0,), (0,)), ((), ()))


def ffn_fwd(name, x, gain, wg, wu, wd, target=None, tm=FFN_WIDE_TILE):
    S, D = x.shape
    tm = min(tm, S)
    nsh, Fs, _ = wg.shape
    with_loss = target is not None

    def body(*refs):
        if with_loss:
            x_ref, gain_ref, wg_ref, wu_ref, wd_ref, t_ref, xo_ref, h_ref, g_ref, u_ref, part_ref, acc = refs
        else:
            x_ref, gain_ref, wg_ref, wu_ref, wd_ref, xo_ref, h_ref, g_ref, u_ref, acc = refs
        j = pl.program_id(1)

        @pl.when(j == 0)
        def _():
            h_ref[...] = _rms(x_ref[...], gain_ref[...]).astype(BF16)
            acc[...] = jnp.zeros_like(acc)

        h = h_ref[...]
        g = lax.dot_general(h, wg_ref[...], NT_DIMS, preferred_element_type=F32)
        u = lax.dot_general(h, wu_ref[...], NT_DIMS, preferred_element_type=F32)
        g_ref[...] = g.astype(BF16)
        u_ref[...] = u.astype(BF16)
        a = g * jax.nn.sigmoid(g) * u
        acc[...] += jnp.dot(a.astype(BF16), wd_ref[...], preferred_element_type=F32)

        @pl.when(j == nsh - 1)
        def _():
            y = x_ref[...] + 0.5 * acc[...]
            if with_loss:
                err = y - t_ref[...]
                xo_ref[...] = err * (1.0 / D)
                part_ref[...] = jnp.full(part_ref.shape, 0.5 * jnp.sum(jnp.mean(err * err, axis=-1)), F32)
            else:
                xo_ref[...] = y

    rows = pl.BlockSpec((tm, D), lambda i, j: (i, 0))
    wspec = pl.BlockSpec((None, Fs, D), lambda i, j: (j, 0, 0))
    act = pl.BlockSpec((None, tm, Fs), lambda i, j: (j, i, 0))
    in_specs, args = [rows, pl.BlockSpec((1, D), lambda i, j: (0, 0)), wspec, wspec, wspec], [x, gain, wg, wu, wd]
    out_specs = [rows, rows, act, act]
    out_shape = [jax.ShapeDtypeStruct((S, D), F32), jax.ShapeDtypeStruct((S, D), BF16),
                 jax.ShapeDtypeStruct((nsh, S, Fs), BF16), jax.ShapeDtypeStruct((nsh, S, Fs), BF16)]
    if with_loss:
        in_specs.append(rows)
        args.append(target)
        out_specs.append(pl.BlockSpec((SUBLANES, 128), lambda i, j: (i, 0)))
        out_shape.append(jax.ShapeDtypeStruct((S // tm * SUBLANES, 128), F32))
    return pl.pallas_call(
        body, name=name, grid=(S // tm, nsh), in_specs=in_specs, out_specs=out_specs, out_shape=out_shape,
        scratch_shapes=[pltpu.VMEM((tm, D), F32)], compiler_params=_params(2),
    )(*args)


def ffn_bwd_act(name, dxo, x, gain, g, u, wg, wu, wd, tm=TOKEN_TILE):
    S, D = x.shape
    nsh, Fs, _ = wg.shape

    def body(dxo_ref, x_ref, gain_ref, g_ref, u_ref, wg_ref, wu_ref, wd_ref, dx_ref, dgain_ref, dg_ref, du_ref, dd, dh):
        i, j = pl.program_id(0), pl.program_id(1)

        @pl.when(j == 0)
        def _():
            dd[...] = (0.5 * dxo_ref[...]).astype(BF16)
            dh[...] = jnp.zeros_like(dh)

        da = lax.dot_general(dd[...], wd_ref[...], NT_DIMS, preferred_element_type=F32)
        gf, uf = g_ref[...].astype(F32), u_ref[...].astype(F32)
        sig = jax.nn.sigmoid(gf)
        dgv = (da * uf * (sig * (1.0 + gf * (1.0 - sig)))).astype(BF16)
        duv = (da * (gf * sig)).astype(BF16)
        dg_ref[...] = dgv
        du_ref[...] = duv
        dh[...] += (jnp.dot(dgv, wg_ref[...], preferred_element_type=F32)
                    + jnp.dot(duv, wu_ref[...], preferred_element_type=F32))

        @pl.when(j == nsh - 1)
        def _():
            xv = x_ref[...]
            r = lax.rsqrt(jnp.mean(xv * xv, axis=-1, keepdims=True) + EPS)
            xhat = xv * r
            dhv = dh[...]
            dxn = dhv * gain_ref[...]
            dx_ref[...] = dxo_ref[...] + r * (dxn - xhat * jnp.mean(dxn * xhat, axis=-1, keepdims=True))
            part = jnp.sum(dhv * xhat, axis=0, keepdims=True)

            @pl.when(i == 0)
            def _():
                dgain_ref[...] = part

            @pl.when(i != 0)
            def _():
                dgain_ref[...] += part

    return pl.pallas_call(
        body, name=name, grid=(S // tm, nsh),
        in_specs=[pl.BlockSpec((tm, D), lambda i, j: (i, 0)), pl.BlockSpec((tm, D), lambda i, j: (i, 0)),
                  pl.BlockSpec((1, D), lambda i, j: (0, 0)),
                  pl.BlockSpec((None, tm, Fs), lambda i, j: (j, i, 0)), pl.BlockSpec((None, tm, Fs), lambda i, j: (j, i, 0)),
                  pl.BlockSpec((None, Fs, D), lambda i, j: (j, 0, 0)), pl.BlockSpec((None, Fs, D), lambda i, j: (j, 0, 0)),
                  pl.BlockSpec((None, Fs, D), lambda i, j: (j, 0, 0))],
        out_specs=[pl.BlockSpec((tm, D), lambda i, j: (i, 0)), pl.BlockSpec((1, D), lambda i, j: (0, 0)),
                   pl.BlockSpec((None, tm, Fs), lambda i, j: (j, i, 0)), pl.BlockSpec((None, tm, Fs), lambda i, j: (j, i, 0))],
        out_shape=[jax.ShapeDtypeStruct((S, D), F32), jax.ShapeDtypeStruct((1, D), F32),
                   jax.ShapeDtypeStruct((nsh, S, Fs), BF16), jax.ShapeDtypeStruct((nsh, S, Fs), BF16)],
        scratch_shapes=[pltpu.VMEM((tm, D), BF16), pltpu.VMEM((tm, D), F32)], compiler_params=_params(2),
    )(dxo, x, gain, g, u, wg, wu, wd)


def ffn_bwd_w(name, h, dxo, g, u, dg, du, tm=FFN_WIDE_TILE):
    S, D = h.shape
    tm = min(tm, S)
    nsh, _, Fs = g.shape

    def body(h_ref, dxo_ref, g_ref, u_ref, dg_ref, du_ref, dwg_ref, dwu_ref, dwd_ref):
        i = pl.program_id(1)
        gf, uf = g_ref[...].astype(F32), u_ref[...].astype(F32)
        a = (gf * jax.nn.sigmoid(gf) * uf).astype(BF16)
        dd = (0.5 * dxo_ref[...]).astype(BF16)
        hv = h_ref[...]

        @pl.when(i == 0)
        def _():
            dwg_ref[...] = jnp.zeros_like(dwg_ref)
            dwu_ref[...] = jnp.zeros_like(dwu_ref)
            dwd_ref[...] = jnp.zeros_like(dwd_ref)

        dwg_ref[...] += lax.dot_general(dg_ref[...], hv, TN_DIMS, preferred_element_type=F32)
        dwu_ref[...] += lax.dot_general(du_ref[...], hv, TN_DIMS, preferred_element_type=F32)
        dwd_ref[...] += lax.dot_general(a, dd, TN_DIMS, preferred_element_type=F32)

    act = pl.BlockSpec((None, tm, Fs), lambda j, i: (j, i, 0))
    wspec = pl.BlockSpec((None, Fs, D), lambda j, i: (j, 0, 0))
    return pl.pallas_call(
        body, name=name, grid=(nsh, S // tm),
        in_specs=[pl.BlockSpec((tm, D), lambda j, i: (i, 0)), pl.BlockSpec((tm, D), lambda j, i: (i, 0)), act, act, act, act],
        out_specs=[wspec, wspec, wspec], out_shape=[jax.ShapeDtypeStruct((nsh, Fs, D), F32)] * 3,
        compiler_params=_params(2),
    )(h, dxo, g, u, dg, du)


NEG_BIG = -1e30


def _causal_pairs(n, by_key):
    pairs = [(qi, ki) for qi in range(n) for ki in range(qi + 1)]
    if by_key:
        pairs.sort(key=lambda p: (p[1], p[0]))
    return jnp.asarray([p[0] for p in pairs], jnp.int32), jnp.asarray([p[1] for p in pairs], jnp.int32)


def _scores(q, k, masked):
    s = lax.dot_general(q, k, (((1,), (1,)), ((), ())), preferred_element_type=F32)
    if masked:
        row = lax.broadcasted_iota(jnp.int32, s.shape, 0)
        col = lax.broadcasted_iota(jnp.int32, s.shape, 1)
        s = jnp.where(row >= col, s, NEG_BIG)
    return s


def attn_fwd(name, q, k, v, t=ATTN_TILE):
    H, S, Dk = q.shape
    Dv = v.shape[1] // H
    qt, kt = _causal_pairs(S // t, by_key=False)

    def body(qt_ref, kt_ref, q_ref, k_ref, v_ref, o_ref, lse_ref, m_sc, l_sc, acc):
        qi, ki = qt_ref[pl.program_id(1)], kt_ref[pl.program_id(1)]

        @pl.when(ki == 0)
        def _():
            m_sc[...] = jnp.full_like(m_sc, NEG_BIG)
            l_sc[...] = jnp.zeros_like(l_sc)
            acc[...] = jnp.zeros_like(acc)

        def step(masked):
            s = _scores(q_ref[...], k_ref[...], masked)
            m_new = jnp.maximum(m_sc[...], jnp.max(s, axis=-1, keepdims=True))
            alpha = jnp.exp(m_sc[...] - m_new)
            p = jnp.exp(s - m_new)
            l_sc[...] = alpha * l_sc[...] + jnp.sum(p, axis=-1, keepdims=True)
            acc[...] = alpha * acc[...] + jnp.dot(p.astype(BF16), v_ref[...].astype(BF16), preferred_element_type=F32)
            m_sc[...] = m_new

        @pl.when(ki < qi)
        def _():
            step(False)

        @pl.when(ki == qi)
        def _():
            step(True)
            o_ref[...] = acc[...] / l_sc[...]
            lse_ref[...] = m_sc[...] + jnp.log(l_sc[...])

    return pl.pallas_call(
        body, name=name,
        grid_spec=pltpu.PrefetchScalarGridSpec(
            num_scalar_prefetch=2, grid=(H, qt.shape[0]),
            in_specs=[pl.BlockSpec((None, t, Dk), lambda h, s, qt, kt: (h, qt[s], 0)),
                      pl.BlockSpec((None, t, Dk), lambda h, s, qt, kt: (h, kt[s], 0)),
                      pl.BlockSpec((t, Dv), lambda h, s, qt, kt: (kt[s], h))],
            out_specs=[pl.BlockSpec((t, Dv), lambda h, s, qt, kt: (qt[s], h)),
                       pl.BlockSpec((None, t, 1), lambda h, s, qt, kt: (h, qt[s], 0))],
            scratch_shapes=[pltpu.VMEM((t, 1), F32), pltpu.VMEM((t, 1), F32), pltpu.VMEM((t, Dv), F32)]),
        out_shape=[jax.ShapeDtypeStruct((S, H * Dv), F32), jax.ShapeDtypeStruct((H, S, 1), F32)],
        compiler_params=_params(2),
    )(qt, kt, q, k, v)


def attn_bwd(name, q, k, v, do, o, lse, t=ATTN_TILE):
    H, S, Dk = q.shape
    Dv = v.shape[1] // H
    qt, kt = _causal_pairs(S // t, by_key=True)
    tn_dims = (((0,), (0,)), ((), ()))

    def body(qt_ref, kt_ref, q_ref, k_ref, v_ref, do_ref, o_ref, lse_ref, dq_ref, dk_ref, dv_ref):
        step_id = pl.program_id(1)
        qi, ki = qt_ref[step_id], kt_ref[step_id]

        @pl.when(step_id == 0)
        def _():
            dq_ref[...] = jnp.zeros_like(dq_ref)

        def step(masked):
            s = _scores(q_ref[...], k_ref[...], masked)
            p = jnp.exp(s - lse_ref[...])
            dov = do_ref[...]
            delta = jnp.sum(dov * o_ref[...], axis=-1, keepdims=True)
            dob = dov.astype(BF16)
            dp = lax.dot_general(dob, v_ref[...].astype(BF16), (((1,), (1,)), ((), ())), preferred_element_type=F32)
            ds = (p * (dp - delta)).astype(BF16)
            pdv = lax.dot_general(p.astype(BF16), dob, tn_dims, preferred_element_type=F32)
            pdk = lax.dot_general(ds, q_ref[...], tn_dims, preferred_element_type=F32)
            rows = pl.ds(pl.multiple_of(qi * t, t), t)
            dq_ref[rows, :] += jnp.dot(ds, k_ref[...], preferred_element_type=F32)
            return pdk, pdv

        @pl.when(ki == qi)
        def _():
            dk_ref[...] = jnp.zeros_like(dk_ref)
            dv_ref[...] = jnp.zeros_like(dv_ref)

        def accumulate(masked):
            pdk, pdv = step(masked)
            dk_ref[...] += pdk
            dv_ref[...] += pdv

        @pl.when(ki == qi)
        def _():
            accumulate(True)

        @pl.when(ki < qi)
        def _():
            accumulate(False)

    qmap = lambda h, s, qt, kt: (h, qt[s], 0)
    kmap = lambda h, s, qt, kt: (h, kt[s], 0)
    qrow = lambda h, s, qt, kt: (qt[s], h)
    krow = lambda h, s, qt, kt: (kt[s], h)
    return pl.pallas_call(
        body, name=name,
        grid_spec=pltpu.PrefetchScalarGridSpec(
            num_scalar_prefetch=2, grid=(H, qt.shape[0]),
            in_specs=[pl.BlockSpec((None, t, Dk), qmap), pl.BlockSpec((None, t, Dk), kmap), pl.BlockSpec((t, Dv), krow),
                      pl.BlockSpec((t, Dv), qrow), pl.BlockSpec((t, Dv), qrow), pl.BlockSpec((None, t, 1), qmap)],
            out_specs=[pl.BlockSpec((None, S, Dk), lambda h, s, qt, kt: (h, 0, 0)), pl.BlockSpec((None, t, Dk), kmap),
                       pl.BlockSpec((t, Dv), krow)]),
        out_shape=[jax.ShapeDtypeStruct((H, S, Dk), F32), jax.ShapeDtypeStruct((H, S, Dk), F32),
                   jax.ShapeDtypeStruct((S, H * Dv), F32)],
        compiler_params=_params(2),
    )(qt, kt, q, k, v, do, o, lse)


def _cmul(ar, ai, br, bi):
    return ar * br - ai * bi, ar * bi + ai * br


def _scan_tile(x_r, x_i, ar_ref, ai_ref, cr_sc, ci_sc, *, reverse, first, states=None):
    tc, lanes = x_r.shape
    nblk, lb = tc // SUBLANES, SCAN_LANE_TILE
    with_da = states is not None
    if with_da:
        xr_all, xi_all, pr_all, pi_all, dar_all, dai_all, chunk = states

    @pl.when(first)
    def _():
        cr_sc[...] = jnp.zeros_like(cr_sc)
        ci_sc[...] = jnp.zeros_like(ci_sc)
        if with_da:
            dar_all[...] = jnp.zeros_like(dar_all)
            dai_all[...] = jnp.zeros_like(dai_all)

    row = lax.broadcasted_iota(jnp.int32, (SUBLANES, lb), 0)
    for l0 in range(0, lanes, lb):
        _scan_lanes(x_r.at[:, pl.ds(l0, lb)], x_i.at[:, pl.ds(l0, lb)], ar_ref[0:1, pl.ds(l0, lb)],
                    ai_ref[0:1, pl.ds(l0, lb)], cr_sc.at[:, pl.ds(l0, lb)], ci_sc.at[:, pl.ds(l0, lb)], row, reverse,
                    nblk, None if not with_da else tuple(r.at[:, pl.ds(l0, lb)] for r in states[:6]) + (chunk,))


def _scan_lanes(x_r, x_i, a1r, a1i, cr_sc, ci_sc, row, reverse, nblk, states):
    lb = x_r.shape[1]
    with_da = states is not None
    if with_da:
        xr_ref, xi_ref, pr_ref, pi_ref, dar_ref, dai_ref, chunk = states
    if reverse:
        a1i = -a1i
    a2r, a2i = _cmul(a1r, a1i, a1r, a1i)
    a4r, a4i = _cmul(a2r, a2i, a2r, a2i)
    pw_r, pw_i = jnp.zeros((SUBLANES, lb), F32), jnp.zeros((SUBLANES, lb), F32)
    cur_r, cur_i = a1r, a1i
    for e in range(SUBLANES):
        r_at = (SUBLANES - 1 - e) if reverse else e
        pw_r = jnp.where(row == r_at, cur_r, pw_r)
        pw_i = jnp.where(row == r_at, cur_i, pw_i)
        cur_r, cur_i = _cmul(cur_r, cur_i, a1r, a1i)
    steps = []
    for d, pr, pi in ((1, a1r, a1i), (2, a2r, a2i), (4, a4r, a4i)):
        keep = (row < SUBLANES - d) if reverse else (row >= d)
        steps.append((d, jnp.where(keep, pr, 0.0), jnp.where(keep, pi, 0.0)))

    def block(jb, carry):
        if with_da:
            cr, ci, acc_r, acc_i = carry
        else:
            cr, ci = carry
        idx = (nblk - 1 - jb) if reverse else jb
        r0 = pl.multiple_of(idx * SUBLANES, SUBLANES)
        xr = x_r[pl.ds(r0, SUBLANES), :]
        xi = x_i[pl.ds(r0, SUBLANES), :]
        for d, pr, pi in steps:
            shift = SUBLANES - d if reverse else d
            sr, si = pltpu.roll(xr, shift, 0), pltpu.roll(xi, shift, 0)
            xr, xi = xr + pr * sr - pi * si, xi + pr * si + pi * sr
        xr, xi = xr + pw_r * cr - pw_i * ci, xi + pw_r * ci + pw_i * cr
        x_r[pl.ds(r0, SUBLANES), :] = xr
        x_i[pl.ds(r0, SUBLANES), :] = xi
        edge = 0 if reverse else SUBLANES - 1
        cr, ci = xr[edge:edge + 1, :], xi[edge:edge + 1, :]
        if not with_da:
            return cr, ci
        fr = xr_ref[pl.ds(r0, SUBLANES), :]
        fi = xi_ref[pl.ds(r0, SUBLANES), :]
        rp = pl.multiple_of(jnp.maximum(idx - 1, 0) * SUBLANES, SUBLANES)
        inside = idx > 0
        before_r = jnp.where(inside, xr_ref[pl.ds(rp, SUBLANES), :], pr_ref[...])
        before_i = jnp.where(inside, xi_ref[pl.ds(rp, SUBLANES), :], pi_ref[...])
        live = jnp.where(jnp.logical_or(inside, chunk > 0), 1.0, 0.0)
        last_r = before_r[SUBLANES - 1:SUBLANES, :] * live
        last_i = before_i[SUBLANES - 1:SUBLANES, :] * live
        pvr = jnp.where(row == 0, last_r, pltpu.roll(fr, 1, 0))
        pvi = jnp.where(row == 0, last_i, pltpu.roll(fi, 1, 0))
        acc_r = acc_r + xr * pvr + xi * pvi
        acc_i = acc_i + xi * pvr - xr * pvi
        return cr, ci, acc_r, acc_i

    init = (cr_sc[...], ci_sc[...])
    if with_da:
        init = init + (jnp.zeros((SUBLANES, lb), F32), jnp.zeros((SUBLANES, lb), F32))
    fin = lax.fori_loop(0, nblk, block, init)
    cr_sc[...] = fin[0]
    ci_sc[...] = fin[1]
    if with_da:
        dar_ref[...] += fin[2]
        dai_ref[...] += fin[3]


SSM_BLOCKS = 4
BLOCK_CH = SSM_WIDTH // SSM_BLOCKS
PREP_LANES = SSM_LANES // SSM_BLOCKS


def ssm_fwd(name, u, bb_r, bb_i, cb_r, ncb_i, a_r8, a_i8):
    S = u.shape[0]
    tc = min(SCAN_TIME_TILE, S)

    def body(u_ref, bbr_ref, bbi_ref, cbr_ref, ncbi_ref, ar_ref, ai_ref, xr_ref, xi_ref, y_ref, cr_sc, ci_sc):
        ub = u_ref[...].astype(BF16)
        xr_ref[...] = jnp.dot(ub, bbr_ref[...], preferred_element_type=F32)
        xi_ref[...] = jnp.dot(ub, bbi_ref[...], preferred_element_type=F32)
        _scan_tile(xr_ref, xi_ref, ar_ref, ai_ref, cr_sc, ci_sc, reverse=False, first=pl.program_id(1) == 0)
        y_ref[...] = (lax.dot_general(xr_ref[...].astype(BF16), cbr_ref[...], NT_DIMS, preferred_element_type=F32)
                      + lax.dot_general(xi_ref[...].astype(BF16), ncbi_ref[...], NT_DIMS, preferred_element_type=F32))

    ch = pl.BlockSpec((tc, BLOCK_CH), lambda b, t: (t, b))
    st = pl.BlockSpec((tc, PREP_LANES), lambda b, t: (t, b))
    wt = pl.BlockSpec((None, BLOCK_CH, PREP_LANES), lambda b, t: (b, 0, 0))
    par = pl.BlockSpec((SUBLANES, PREP_LANES), lambda b, t: (0, b))
    return pl.pallas_call(
        body, name=name, grid=(SSM_BLOCKS, S // tc), in_specs=[ch, wt, wt, wt, wt, par, par], out_specs=[st, st, ch],
        out_shape=[jax.ShapeDtypeStruct((S, SSM_LANES), F32), jax.ShapeDtypeStruct((S, SSM_LANES), F32),
                   jax.ShapeDtypeStruct((S, SSM_WIDTH), F32)],
        scratch_shapes=[pltpu.VMEM((1, PREP_LANES), F32), pltpu.VMEM((1, PREP_LANES), F32)], compiler_params=_params(2),
    )(u, bb_r, bb_i, cb_r, ncb_i, a_r8, a_i8)


def ssm_bwd(name, dy, du_add, u, xs_r, xs_i, bb_r, bb_i, cb_r, ncb_i, a_r8, a_i8):
    S = u.shape[0]
    tc = min(SCAN_TIME_TILE, S)
    nt = S // tc

    def body(dy_ref, dua_ref, u_ref, xr_ref, xi_ref, pr_ref, pi_ref, bbr_ref, bbi_ref, cbr_ref, ncbi_ref, ar_ref, ai_ref,
             du_ref, dbbr_ref, dbbi_ref, dcbr_ref, dncbi_ref, dar_ref, dai_ref, lr_sc, li_sc, cr_sc, ci_sc):
        t = pl.program_id(1)
        first = t == 0

        @pl.when(first)
        def _():
            for r in (dbbr_ref, dbbi_ref, dcbr_ref, dncbi_ref):
                r[...] = jnp.zeros_like(r)

        dyb = dy_ref[...].astype(BF16)
        lr_sc[...] = jnp.dot(dyb, cbr_ref[...], preferred_element_type=F32)
        li_sc[...] = jnp.dot(dyb, ncbi_ref[...], preferred_element_type=F32)
        _scan_tile(lr_sc, li_sc, ar_ref, ai_ref, cr_sc, ci_sc, reverse=True, first=first,
                   states=(xr_ref, xi_ref, pr_ref, pi_ref, dar_ref, dai_ref, nt - 1 - t))
        lrb, lib = lr_sc[...].astype(BF16), li_sc[...].astype(BF16)
        du_ref[...] = (lax.dot_general(lrb, bbr_ref[...], NT_DIMS, preferred_element_type=F32)
                       + lax.dot_general(lib, bbi_ref[...], NT_DIMS, preferred_element_type=F32) + dua_ref[...])
        ub = u_ref[...].astype(BF16)
        dbbr_ref[...] += lax.dot_general(ub, lrb, TN_DIMS, preferred_element_type=F32)
        dbbi_ref[...] += lax.dot_general(ub, lib, TN_DIMS, preferred_element_type=F32)
        dcbr_ref[...] += lax.dot_general(dyb, xr_ref[...].astype(BF16), TN_DIMS, preferred_element_type=F32)
        dncbi_ref[...] += lax.dot_general(dyb, xi_ref[...].astype(BF16), TN_DIMS, preferred_element_type=F32)

    ch = pl.BlockSpec((tc, BLOCK_CH), lambda b, t: (nt - 1 - t, b))
    st = pl.BlockSpec((tc, PREP_LANES), lambda b, t: (nt - 1 - t, b))
    prev = pl.BlockSpec((SUBLANES, PREP_LANES), lambda b, t: (jnp.maximum((nt - 1 - t) * (tc // SUBLANES) - 1, 0), b))
    wt = pl.BlockSpec((None, BLOCK_CH, PREP_LANES), lambda b, t: (b, 0, 0))
    par = pl.BlockSpec((SUBLANES, PREP_LANES), lambda b, t: (0, b))
    blk = jax.ShapeDtypeStruct((SSM_BLOCKS, BLOCK_CH, PREP_LANES), F32)
    rows8 = jax.ShapeDtypeStruct((SUBLANES, SSM_LANES), F32)
    return pl.pallas_call(
        body, name=name, grid=(SSM_BLOCKS, nt), in_specs=[ch, ch, ch, st, st, prev, prev, wt, wt, wt, wt, par, par],
        out_specs=[ch, wt, wt, wt, wt, par, par],
        out_shape=[jax.ShapeDtypeStruct((S, SSM_WIDTH), F32), blk, blk, blk, blk, rows8, rows8],
        scratch_shapes=[pltpu.VMEM((tc, PREP_LANES), F32), pltpu.VMEM((tc, PREP_LANES), F32),
                        pltpu.VMEM((1, PREP_LANES), F32), pltpu.VMEM((1, PREP_LANES), F32)],
        compiler_params=_params(2),
    )(dy, du_add, u, xs_r, xs_i, xs_r, xs_i, bb_r, bb_i, cb_r, ncb_i, a_r8, a_i8)


def _ssm_prep_f(a_re, a_im, log_dt, bt_re, bt_im, c_re, c_im):
    first_group = pl.program_id(0) * (SSM_GROUPS // SSM_BLOCKS)
    iota = lambda shape, d: lax.broadcasted_iota(jnp.int32, shape, d)
    grp_of_row = lambda shape: iota(shape, 0) >> int(math.log2(SSM_GROUP))
    grp_of_lane = lambda shape: iota(shape, 1) >> int(math.log2(SSM_STATE))
    rep = (grp_of_row((BLOCK_CH, SSM_GROUPS)) + first_group == iota((BLOCK_CH, SSM_GROUPS), 1)).astype(F32)
    til = ((iota((SSM_STATE, PREP_LANES), 1) & (SSM_STATE - 1)) == iota((SSM_STATE, PREP_LANES), 0)).astype(F32)
    m_rows = (grp_of_row((BLOCK_CH, PREP_LANES)) == grp_of_lane((BLOCK_CH, PREP_LANES))).astype(F32)
    m_grp = (iota((SSM_GROUPS, PREP_LANES), 0) == grp_of_lane((SSM_GROUPS, PREP_LANES)) + first_group).astype(F32)
    dt = jnp.exp(log_dt)
    decay = jnp.exp(a_re * dt)
    ar = decay * jnp.cos(a_im * dt)
    ai = decay * jnp.sin(a_im * dt)
    den = a_re * a_re + a_im * a_im
    nr = ar - 1.0
    coef_r = (nr * a_re + ai * a_im) / den
    coef_i = (ai * a_re - nr * a_im) / den
    cr, ci = _hp_dot(rep, coef_r), _hp_dot(rep, coef_i)
    bb_r = cr * bt_re - ci * bt_im
    bb_i = cr * bt_im + ci * bt_re
    big = lambda m: _hp_dot(m, til) * m_rows
    lanes = lambda m: jnp.broadcast_to(jnp.sum(_hp_dot(m, til) * m_grp, axis=0, keepdims=True), (SUBLANES, PREP_LANES))
    return lanes(ar), lanes(ai), big(bb_r), big(bb_i), big(c_re), -big(c_im)


def _whole(arr, **kw):
    nd = arr.ndim
    return In(arr, arr.shape, lambda *_: (0,) * nd, **kw)


def _adamw_math(w, g, m, v):
    m = ADAM_B1 * m + (1.0 - ADAM_B1) * g
    v = ADAM_B2 * v + (1.0 - ADAM_B2) * (g * g)
    m_hat = m / (1.0 - ADAM_B1 ** ADAM_STEP)
    v_hat = v / (1.0 - ADAM_B2 ** ADAM_STEP)
    delta = -ADAM_LR * (m_hat / (jnp.sqrt(v_hat) + ADAM_EPS) + ADAM_WD * w)
    return delta, m, v


def adamw_big(name, w, g, m, v):
    R, C = w.shape
    tr = R
    for cand in (512, 344, 256, 128):
        if R % cand == 0:
            tr = cand
            break

    def body(w_ref, g_ref, m_ref, v_ref, d_ref, nm_ref, nv_ref):
        d, nm, nv = _adamw_math(w_ref[...], g_ref[...], m_ref[...], v_ref[...])
        d_ref[...] = d
        nm_ref[...] = nm
        nv_ref[...] = nv

    spec = pl.BlockSpec((tr, C), lambda i: (i, 0))
    return pl.pallas_call(
        body, name=name, grid=(R // tr,), in_specs=[spec] * 4, out_specs=[spec] * 3,
        out_shape=[jax.ShapeDtypeStruct((R, C), F32)] * 3, compiler_params=_params(1),
    )(w, g, m, v)


def adamw_small(name, ws, gs, ms, vs):
    n = len(ws)

    def body(*refs):
        for k in range(n):
            d, nm, nv = _adamw_math(refs[k][...], refs[n + k][...], refs[2 * n + k][...], refs[3 * n + k][...])
            refs[4 * n + k][...] = d
            refs[5 * n + k][...] = nm
            refs[6 * n + k][...] = nv

    vm = pl.BlockSpec(memory_space=pltpu.VMEM)
    shapes = [jax.ShapeDtypeStruct(w.shape, F32) for w in ws]
    res = pl.pallas_call(
        body, name=name, in_specs=[vm] * (4 * n), out_specs=[vm] * (3 * n), out_shape=shapes * 3,
        compiler_params=pltpu.CompilerParams(vmem_limit_bytes=VMEM_LIMIT_BYTES),
    )(*ws, *gs, *ms, *vs)
    return res[:n], res[n:2 * n], res[2 * n:]


def _place():
    return lax.axis_index("x"), lax.axis_index("y"), lax.axis_index("c")


def _other_chips(x, y):
    return [(1 - x, y), (x, 1 - y), (1 - x, 1 - y)]


HBM = pl.BlockSpec(memory_space=pl.ANY)


def all_gather_halves(name, blocks):
    n = len(blocks)

    def body(*refs):
        in_refs, out_refs = refs[:n], refs[n:2 * n]
        send_sems, recv_sems = refs[2 * n:]
        x, y, c = _place()
        me, sibling = (x, y, c), (x, y, 1 - c)
        chips = _other_chips(x, y)

        def slot(a, px, py, pc):
            return out_refs[a].at[4 * px + 2 * py + pc]

        def copy(a, k, block, to, src=None):
            return pltpu.make_async_remote_copy(
                src_ref=slot(a, *block) if src is None else src, dst_ref=slot(a, *block),
                send_sem=send_sems.at[a, k], recv_sem=recv_sems.at[a, k], device_id=to, device_id_type=MESH)

        first = []
        for a in range(n):
            first.append(copy(a, 0, me, sibling, src=in_refs[a]))
            first += [copy(a, 1 + j, me, (*chip, c), src=in_refs[a]) for j, chip in enumerate(chips)]
        for cp in first:
            cp.start()
        passed = []
        for j, chip in enumerate(chips):
            for a in range(n):
                copy(a, 1 + j, (*chip, c), me).wait_recv()
                fw = copy(a, 4 + j, (*chip, c), sibling)
                fw.start()
                passed.append(fw)
        for a in range(n):
            copy(a, 0, sibling, me).wait_recv()
            for j, chip in enumerate(chips):
                copy(a, 4 + j, (*chip, 1 - c), me).wait_recv()
        for cp in first + passed:
            cp.wait_send()

    res = pl.pallas_call(
        body, name=name, in_specs=[HBM] * n, out_specs=[HBM] * n,
        out_shape=[jax.ShapeDtypeStruct((N_DEV,) + b.shape, b.dtype) for b in blocks],
        scratch_shapes=[pltpu.SemaphoreType.DMA((n, 7)), pltpu.SemaphoreType.DMA((n, 7))],
    )(*blocks)
    return list(res)


def pair_send_halves(name, grads):
    n = len(grads)

    def body(*refs):
        in_refs, out_refs = refs[:n], refs[n:2 * n]
        send_sems, recv_sems = refs[2 * n:]
        x, y, c = _place()
        cps = []
        for a in range(n):
            cp = pltpu.make_async_remote_copy(
                src_ref=in_refs[a].at[:, 1 - c], dst_ref=out_refs[a], send_sem=send_sems.at[a], recv_sem=recv_sems.at[a],
                device_id=(x, y, 1 - c), device_id_type=MESH)
            cp.start()
            cps.append(cp)
        for cp in cps:
            cp.wait()

    res = pl.pallas_call(
        body, name=name, in_specs=[HBM] * n, out_specs=[HBM] * n,
        out_shape=[jax.ShapeDtypeStruct((g.shape[0],) + g.shape[2:], g.dtype) for g in grads],
        scratch_shapes=[pltpu.SemaphoreType.DMA((n,)), pltpu.SemaphoreType.DMA((n,))],
    )(*grads)
    return list(res)


def pair_add(name, grad, got, c_arr):
    nsh, _, M, N = grad.shape
    tr = M
    for cand in (512, 256, 192, 128, 64, 16):
        if M % cand == 0:
            tr = cand
            break

    def body(c_ref, g_ref, p_ref, o_ref):
        o_ref[...] = (g_ref[...] + p_ref[...]).astype(BF16)

    return pl.pallas_call(
        body, name=name,
        grid_spec=pltpu.PrefetchScalarGridSpec(
            num_scalar_prefetch=1, grid=(nsh, M // tr),
            in_specs=[pl.BlockSpec((None, None, tr, N), lambda j, i, c_ref: (j, c_ref[0], i, 0)),
                      pl.BlockSpec((None, tr, N), lambda j, i, c_ref: (j, i, 0))],
            out_specs=pl.BlockSpec((None, tr, N), lambda j, i, c_ref: (j, i, 0))),
        out_shape=jax.ShapeDtypeStruct((nsh, M, N), BF16), compiler_params=_params(2),
    )(c_arr, grad, got)


def scatter_to_chips(name, parts):
    n = len(parts)

    def body(*refs):
        in_refs, out_refs = refs[:n], refs[n:2 * n]
        send_sems, recv_sems = refs[2 * n:]
        x, y, c = _place()
        mine = 2 * x + y
        chips = _other_chips(x, y)
        cps = []
        for a in range(n):
            for k, (px, py) in enumerate(chips):
                cp = pltpu.make_async_remote_copy(
                    src_ref=in_refs[a].at[2 * px + py], dst_ref=out_refs[a].at[mine],
                    send_sem=send_sems.at[a, k], recv_sem=recv_sems.at[a, k], device_id=(px, py, c), device_id_type=MESH)
                cp.start()
                cps.append((cp, a, k, px, py))
        for cp, a, k, px, py in cps:
            pltpu.make_async_remote_copy(
                src_ref=in_refs[a].at[mine], dst_ref=out_refs[a].at[2 * px + py],
                send_sem=send_sems.at[a, k], recv_sem=recv_sems.at[a, k], device_id=(px, py, c), device_id_type=MESH).wait_recv()
        for cp, *_ in cps:
            cp.wait_send()

    res = pl.pallas_call(
        body, name=name, in_specs=[HBM] * n, out_specs=[HBM] * n,
        out_shape=[jax.ShapeDtypeStruct(p.shape, p.dtype) for p in parts],
        scratch_shapes=[pltpu.SemaphoreType.DMA((n, 3)), pltpu.SemaphoreType.DMA((n, 3))],
    )(*parts)
    return list(res)


def sum_chips(name, q):
    nsh, M, N = q.shape
    tr = M
    for cand in (512, 256, 192, 128, 64, 16):
        if M % cand == 0:
            tr = cand
            break

    def body(q_ref, o_ref):
        acc = q_ref[0].astype(F32)
        for j in range(1, nsh):
            acc = acc + q_ref[j].astype(F32)
        o_ref[...] = acc

    return pl.pallas_call(
        body, name=name, grid=(M // tr,), in_specs=[pl.BlockSpec((nsh, tr, N), lambda i: (0, i, 0))],
        out_specs=pl.BlockSpec((tr, N), lambda i: (i, 0)), out_shape=jax.ShapeDtypeStruct((M, N), F32),
        compiler_params=_params(1),
    )(q)


def pair_exchange(name, halves):
    n = len(halves)

    def body(*refs):
        in_refs, out_refs = refs[:n], refs[n:2 * n]
        send_sems, recv_sems = refs[2 * n:]
        x, y, c = _place()
        cps = []
        for a in range(n):
            cp = pltpu.make_async_remote_copy(
                src_ref=in_refs[a], dst_ref=out_refs[a], send_sem=send_sems.at[a], recv_sem=recv_sems.at[a],
                device_id=(x, y, 1 - c), device_id_type=MESH)
            cp.start()
            cps.append(cp)
        for cp in cps:
            cp.wait()

    res = pl.pallas_call(
        body, name=name, in_specs=[HBM] * n, out_specs=[HBM] * n,
        out_shape=[jax.ShapeDtypeStruct(h.shape, h.dtype) for h in halves],
        scratch_shapes=[pltpu.SemaphoreType.DMA((n,)), pltpu.SemaphoreType.DMA((n,))],
    )(*halves)
    return list(res)


SEM = pl.BlockSpec(memory_space=pltpu.SEMAPHORE)
IN_HBM = pl.BlockSpec(memory_space=pltpu.HBM)
SPLIT_COPY = pltpu.CompilerParams(has_side_effects=pltpu.SideEffectType.DATAFLOW_SIDE_EFFECTING)


def _scatter_copies(src_refs, dst_refs, send_sems, recv_sems):
    x, y, c = _place()
    mine = 2 * x + y
    return [pltpu.make_async_remote_copy(
        src_ref=src_refs[a].at[2 * px + py], dst_ref=dst_refs[a].at[mine], send_sem=send_sems.at[a * (N_CHIPS - 1) + k],
        recv_sem=recv_sems.at[a * (N_CHIPS - 1) + k], device_id=(px, py, c), device_id_type=MESH)
        for a in range(len(src_refs)) for k, (px, py) in enumerate(_other_chips(x, y))]


def _gather_copies(src_refs, dst_refs, send_sems, recv_sems):
    x, y, c = _place()
    me = 4 * x + 2 * y + c
    cps = []
    for a in range(len(src_refs)):
        for k in range(1, N_DEV):
            to = (1 - x if k & 4 else x, 1 - y if k & 2 else y, 1 - c if k & 1 else c)
            s = a * (N_DEV - 1) + k - 1
            cps.append(pltpu.make_async_remote_copy(
                src_ref=src_refs[a], dst_ref=dst_refs[a].at[me], send_sem=send_sems.at[s], recv_sem=recv_sems.at[s],
                device_id=to, device_id_type=MESH))
    return cps


def split_copy_start(name, copies, n_sem, srcs, land_shapes):
    n = len(srcs)
    lands = [lax.empty(s.shape, s.dtype) for s in land_shapes]

    def body(*refs):
        for cp in copies(refs[:n], refs[n:2 * n], refs[2 * n], refs[2 * n + 1]):
            cp.start()
        refs[-1][...] = jnp.zeros_like(refs[-1])

    thru = [pltpu.HBM(a.shape, a.dtype) for a in (*srcs, *lands)]
    res = pl.pallas_call(
        body, name=name, in_specs=[IN_HBM] * (2 * n),
        out_specs=(SEM, SEM, *[IN_HBM] * (2 * n), pl.BlockSpec(memory_space=pltpu.VMEM)),
        out_shape=(pltpu.SemaphoreType.DMA((n * n_sem,)), pltpu.SemaphoreType.DMA((n * n_sem,)), *thru,
                   jax.ShapeDtypeStruct((SUBLANES, LANES), F32)),
        input_output_aliases={i: 2 + i for i in range(2 * n)}, compiler_params=SPLIT_COPY,
    )(*[pltpu.with_memory_space_constraint(a, pltpu.HBM) for a in (*srcs, *lands)])
    return (copies, n, res[0], res[1], res[2:2 + 2 * n]), res[-1][0, 0]


def split_copy_wait(name, handle, after):
    copies, n, send_sems, recv_sems, thru = handle

    def body(*refs):
        for cp in copies(refs[:n], refs[n:2 * n], refs[2 * n], refs[2 * n + 1]):
            cp.wait_send()
            cp.wait_recv()

    res = pl.pallas_call(
        body, name=name, in_specs=[IN_HBM] * (2 * n) + [SEM, SEM, pl.BlockSpec(memory_space=pl.ANY)],
        out_specs=[IN_HBM] * (2 * n), out_shape=[pltpu.HBM(a.shape, a.dtype) for a in thru],
        input_output_aliases={i: i for i in range(2 * n)}, compiler_params=SPLIT_COPY,
    )(*thru, send_sems, recv_sems, after)
    return list(res[n:])


def all_reduce_small(name, v):
    R, C = v.shape

    def body(v_ref, o_ref, gath, send_sems, recv_sems):
        x, y, c = _place()
        me, sibling = (x, y, c), (x, y, 1 - c)
        chips = _other_chips(x, y)

        def slot(px, py, pc):
            return gath.at[4 * px + 2 * py + pc]

        def copy(k, block, to, src=None):
            return pltpu.make_async_remote_copy(
                src_ref=slot(*block) if src is None else src, dst_ref=slot(*block),
                send_sem=send_sems.at[k], recv_sem=recv_sems.at[k], device_id=to, device_id_type=MESH)

        first = [copy(0, me, sibling, src=v_ref)]
        first += [copy(1 + j, me, (*chip, c), src=v_ref) for j, chip in enumerate(chips)]
        for cp in first:
            cp.start()
        slot(*me)[...] = v_ref[...]
        passed = [copy(4 + j, (*chip, c), sibling) for j, chip in enumerate(chips)]
        for j, chip in enumerate(chips):
            copy(1 + j, (*chip, c), me).wait_recv()
            passed[j].start()
        copy(0, sibling, me).wait_recv()
        for j, chip in enumerate(chips):
            copy(4 + j, (*chip, 1 - c), me).wait_recv()
        for cp in first + passed:
            cp.wait_send()
        acc = gath[0]
        for d in range(1, N_DEV):
            acc = acc + gath[d]
        o_ref[...] = acc

    vm = pl.BlockSpec(memory_space=pltpu.VMEM)
    return pl.pallas_call(
        body, name=name, in_specs=[vm], out_specs=vm, out_shape=jax.ShapeDtypeStruct((R, C), F32),
        scratch_shapes=[pltpu.VMEM((N_DEV, R, C), F32), pltpu.SemaphoreType.DMA((7,)), pltpu.SemaphoreType.DMA((7,))],
        compiler_params=pltpu.CompilerParams(vmem_limit_bytes=VMEM_LIMIT_BYTES),
    )(v)


LANES = 128
PACK_ROW_MULTIPLE = 1024
SMALL_SHARDED = {
    "w_in": ((D_MODEL, 1216), 1), "mla_w_uq": ((MLA_Q_RANK, 768), 1), "mla_w_ukv": ((MLA_KV_RANK, 1024), 1),
    "ssm_w_glu": ((SSM_WIDTH, SSM_WIDTH), 0), "w_o": ((D_MODEL, D_MODEL), 0), "xattn_w_q": ((D_MODEL, 512), 0),
    "xattn_w_kv": ((D_MODEL, 1024), 0), "xattn_w_o": ((512, D_MODEL), 1),
}
FFN_NAMES = ["ffn1_w_gate", "ffn1_w_up", "ffn1_w_down", "ffn2_w_gate", "ffn2_w_up", "ffn2_w_down"]
TRANSPOSED_VIEW = ("ffn1_w_gate", "ffn1_w_up", "ffn2_w_gate", "ffn2_w_up", "w_in", "mla_w_uq")


def _shard_shape(name):
    (r, cdim), ax = SMALL_SHARDED[name]
    return (r // N_CHIPS, cdim) if ax == 0 else (r, cdim // N_CHIPS)


def _pack_shards(shards):
    parts = []
    for name in SMALL_SHARDED:
        a = shards[name]
        lead = a.shape[:-2]
        parts.append(a.reshape(lead + (a.shape[-2] * a.shape[-1] // LANES, LANES)))
    rows = sum(q.shape[-2] for q in parts)
    parts.append(jnp.zeros(lead + (-rows % PACK_ROW_MULTIPLE, LANES), parts[0].dtype))
    return jnp.concatenate(parts, axis=-2)


def _unpack_shards(packed):
    out, r0 = {}, 0
    lead = packed.shape[:-2]
    for name in SMALL_SHARDED:
        r, cdim = _shard_shape(name)
        rows = r * cdim // LANES
        out[name] = packed[..., r0:r0 + rows, :].reshape(lead + (r, cdim))
        r0 += rows
    return out


def _full_from_shards(name, sh):
    (r, cdim), ax = SMALL_SHARDED[name]
    if ax == 0:
        return sh.reshape(r, cdim)
    return jnp.transpose(sh, (1, 0, 2)).reshape(r, cdim)


def _shards_from_full(name, full):
    (r, cdim), ax = SMALL_SHARDED[name]
    if ax == 0:
        return full.reshape(N_CHIPS, r // N_CHIPS, cdim)
    return jnp.transpose(full.reshape(r, N_CHIPS, cdim // N_CHIPS), (1, 0, 2))


SMALL_REPL = {
    "ffn1_norm": (1, 1024), "mix_norm": (1, 1024), "mla_q_norm": (1, 384), "mla_kv_norm": (1, 256),
    "mla_qk_norm_q": (1, 192), "mla_qk_norm_k": (1, 192), "ssm_a_re": (32, 64), "ssm_a_im": (32, 64),
    "ssm_log_dt": (32, 1), "ssm_b_re": (32, 64, 16), "ssm_b_im": (32, 64, 16), "ssm_c_re": (32, 16, 64),
    "ssm_c_im": (32, 16, 64), "ssm_d": (1, 512), "ssm_b_glu": (1, 512), "out_norm_mla": (1, 512),
    "out_norm_ssm": (1, 512), "xattn_norm": (1, 1024), "mem_norm": (1, 1024), "xattn_q_norm": (1, 128),
    "xattn_k_norm": (1, 128), "ffn2_norm": (1, 1024),
}


def _pack_repl(grads):
    flat = jnp.concatenate([grads[n].reshape(-1) for n in SMALL_REPL])
    rows = -(-flat.shape[0] // (LANES * SUBLANES)) * SUBLANES
    return jnp.pad(flat, (0, rows * LANES - flat.shape[0])).reshape(rows, LANES)


def _unpack_repl(packed):
    flat, out, o = packed.reshape(-1), {}, 0
    for n, shp in SMALL_REPL.items():
        size = int(np.prod(shp))
        out[n] = flat[o:o + size].reshape(shp)
        o += size
    return out


def _rope_tables(positions):
    half = MLA_ROPE // 2
    inv = ROPE_THETA ** (-jnp.arange(half, dtype=F32) / half)
    ang = positions.astype(F32)[:, None] * inv[None, :]
    cos, sin = jnp.cos(ang), jnp.sin(ang)
    S = positions.shape[0]
    z = lambda w: jnp.zeros((S, w), F32)
    keep = jnp.concatenate([jnp.ones((S, MLA_NOPE), F32), cos, cos, z(HEAD_PAD - MLA_QK)], axis=1)
    from_hi = jnp.concatenate([z(MLA_NOPE), -sin, z(HEAD_PAD - MLA_NOPE - half)], axis=1)
    from_lo = jnp.concatenate([z(MLA_NOPE + half), sin, z(HEAD_PAD - MLA_QK)], axis=1)
    return keep, from_hi, from_lo


def _norm_rope(x, g, keep, from_hi, from_lo):
    y = x * lax.rsqrt(jnp.sum(x * x, axis=-1, keepdims=True) * (1.0 / MLA_QK) + EPS) * g
    half = MLA_ROPE // 2
    return y * keep + _lane_roll(y, HEAD_PAD - half) * from_hi + _lane_roll(y, half) * from_lo


def local_step(x, mem, positions, target, w, wb, small_weights=None, ffn2_weights=None, on_grads=None):
    if small_weights is None:
        small_weights = lambda after: {}
    if ffn2_weights is None:
        ffn2_weights = lambda after: [wb[k] for k in FFN_NAMES[3:]]
    if on_grads is None:
        on_grads = lambda tag, g: 0.0
    S = x.shape[0]
    tm = min(TOKEN_TILE, S)
    nt = S // tm
    g1 = (nt,)
    gh = (MLA_HEADS, nt)
    tile = lambda arr, cols=None, cb=0, **kw: In(arr, (tm, arr.shape[1] if cols is None else cols), lambda i, cb=cb: (i, cb), **kw)
    par = lambda arr, **kw: In(arr, arr.shape, lambda *_: (0, 0), diff=True, **kw)
    otile = lambda cols, dt: Out((S, cols), dt, (tm, cols), lambda i: (i, 0))
    grads = {}

    x1, h1, g_1, u_1 = ffn_fwd("ffn1_fwd", x, w["ffn1_norm"], wb["ffn1_w_gate"], wb["ffn1_w_up"], wb["ffn1_w_down"])

    f_norm = lambda xv, g: (_rms(xv, g),)
    mix_ins = [tile(x1, diff=True), par(w["mix_norm"], acc=(0,))]
    mix_outs = [otile(D_MODEL, BF16)]
    (h2,) = seg_fwd("mix_norm_fwd", f_norm, g1, mix_ins, mix_outs)
    wb = {**wb, **small_weights(h2)}

    pq, pkv, pkr, pu = mm("w_in_fwd", [h2], [[wb["w_in_q"], wb["w_in_kv"], wb["w_in_kr"], wb["w_in_u"]]], [F32] * 4)

    f_lat = lambda a, b, ga, gb: (_rms(a, ga), _rms(b, gb))
    lat_ins = [tile(pq, diff=True), tile(pkv, diff=True), par(w["mla_q_norm"], acc=(0,)), par(w["mla_kv_norm"], acc=(0,))]
    lat_outs = [otile(MLA_Q_RANK, BF16), otile(MLA_KV_RANK, BF16)]
    cq, ckv = seg_fwd("latent_norm_fwd", f_lat, g1, lat_ins, lat_outs)

    (q0,) = mm("w_uq_fwd", [cq], [[wb["w_uq"]]], [F32])
    kn0, v0 = mm("w_ukv_fwd", [ckv], [[wb["w_ukv_k"], wb["w_ukv_v"]]], [F32, F32])

    keep, from_hi, from_lo = _rope_tables(positions)

    scale = MLA_QK ** -0.5

    def f_qk(qh, knh, kr, kp, fh, fl, gq, gk):
        return (_norm_rope(qh, gq, kp, fh, fl) * scale, _norm_rope(jnp.concatenate([knh, kr], axis=-1), gk, kp, fh, fl))

    hmap = lambda h, i: (i, h)
    tmap0 = lambda h, i: (i, 0)
    qk_ins = [In(q0, (tm, HEAD_PAD), hmap, diff=True), In(kn0, (tm, MLA_NOPE), hmap, diff=True),
              In(pkr, (tm, LANES), tmap0, diff=True,
                 grad=((MLA_HEADS, S, LANES), (None, tm, LANES), lambda h, i: (h, i, 0))),
              In(keep, (tm, HEAD_PAD), tmap0), In(from_hi, (tm, HEAD_PAD), tmap0), In(from_lo, (tm, HEAD_PAD), tmap0),
              In(w["qk_gain_q"], (1, HEAD_PAD), lambda h, i: (0, 0), diff=True, acc=(0, 1)),
              In(w["qk_gain_k"], (1, HEAD_PAD), lambda h, i: (0, 0), diff=True, acc=(0, 1))]
    head_out = Out((MLA_HEADS, S, HEAD_PAD), BF16, (None, tm, HEAD_PAD), lambda h, i: (h, i, 0))
    qh, kh = seg_fwd("qk_norm_rope_fwd", f_qk, gh, qk_ins, [head_out, head_out])

    o_mla, lse = attn_fwd("mla_attn_fwd", qh, kh, v0, t=min(ATTN_TILE, S))

    prep_grid = (SSM_BLOCKS,)
    prep_ins = ([_whole(w[k], diff=True, acc=(0,)) for k in ("ssm_a_re", "ssm_a_im", "ssm_log_dt")]
                + [In(w[k], (BLOCK_CH, SSM_STATE), lambda i: (i, 0), diff=True)
                   for k in ("ssm_bt_re", "ssm_bt_im", "ssm_c2_re", "ssm_c2_im")])
    blk_out = Out((SSM_BLOCKS, BLOCK_CH, PREP_LANES), BF16, (None, BLOCK_CH, PREP_LANES), lambda i: (i, 0, 0))
    prep_outs = [Out((SUBLANES, SSM_LANES), F32, (SUBLANES, PREP_LANES), lambda i: (0, i))] * 2 + [blk_out] * 4
    a_r8, a_i8, bb_r, bb_i, cb_r, ncb_i = seg_fwd("ssm_prep_fwd", _ssm_prep_f, prep_grid, prep_ins, prep_outs)

    xs_r, xs_i, y_lin = ssm_fwd("ssm_fwd", pu, bb_r, bb_i, cb_r, ncb_i, a_r8, a_i8)

    f_gelu = lambda yl, u, d: (jax.nn.gelu(yl + d * u),)
    gelu_ins = [tile(y_lin, diff=True), tile(pu, diff=True), par(w["ssm_d"], acc=(0,))]
    (gl,) = seg_fwd("ssm_gelu_fwd", f_gelu, g1, gelu_ins, [otile(SSM_WIDTH, F32)])
    (z,) = mm("ssm_glu_fwd", [gl], [[wb["ssm_w_glu"]]], [F32])

    def f_glu(g, zz, o, b, gm, gs):
        return (_rms(o, gm), _rms(g * jax.nn.sigmoid(zz + b), gs))

    glu_ins = [tile(gl, diff=True), tile(z, diff=True), tile(o_mla, diff=True), par(w["ssm_b_glu"], acc=(0,)),
               par(w["out_norm_mla"], acc=(0,)), par(w["out_norm_ssm"], acc=(0,))]
    glu_outs = [otile(SSM_WIDTH, BF16), otile(SSM_WIDTH, BF16)]
    yn_mla, yn_ssm = seg_fwd("glu_out_norm_fwd", f_glu, g1, glu_ins, glu_outs)

    (x2,) = mm("w_o_fwd", [yn_mla, yn_ssm], [[wb["w_o_mla"]], [wb["w_o_ssm"]]], [F32], adds=[x1])

    xn_ins = [tile(x2, diff=True), par(w["xattn_norm"], acc=(0,))]
    (h3,) = seg_fwd("xattn_norm_fwd", f_norm, g1, xn_ins, [otile(D_MODEL, BF16)])
    (xq0,) = mm("xattn_q_fwd", [h3], [[wb["xattn_w_q"]]], [F32])

    M = mem.shape[0]
    mem_ins = [In(mem, (M, D_MODEL), lambda i: (0, 0)), par(w["mem_norm"], acc=(0,))]
    (mn,) = seg_fwd("mem_norm_fwd", f_norm, (1,), mem_ins, [Out((M, D_MODEL), BF16, (M, D_MODEL), lambda i: (0, 0))])
    (kvm,) = mm("xattn_kv_fwd", [mn], [[wb["xattn_w_kv"]]], [F32])

    xscale = XHD ** -0.5

    def f_xattn(qv, kv_k, kv_v, gq, gk):
        qn, kn = _rms(qv, gq), _rms(kv_k, gk)
        s = _bdot_nt(qn, kn) * xscale
        p = jax.nn.softmax(s, axis=-1)
        return (_bdot_nn(p, kv_v),)

    def xa_ins(gain_q):
        return [In(xq0, (tm, XHD), hmap, diff=True),
                In(kvm, (M, XHD), lambda h, i: (0, h), diff=True, acc=(1,), grad=((M, XH * XHD), (M, XHD), lambda h, i: (0, h))),
                In(kvm, (M, XHD), lambda h, i: (0, XH + h), diff=True, acc=(1,), grad=((M, XH * XHD), (M, XHD), lambda h, i: (0, h))),
                In(gain_q, (1, XHD), lambda h, i: (0, 0), diff=True, acc=(0, 1)),
                In(w["xattn_k_norm"], (1, XHD), lambda h, i: (0, 0), diff=True, acc=(0, 1))]

    xa_outs = [Out((S, XH * XHD), F32, (tm, XHD), hmap)]
    (o2,) = seg_fwd("xattn_fwd", f_xattn, gh, xa_ins(w["xattn_q_norm"]), xa_outs)

    (x3,) = mm("xattn_o_fwd", [o2], [[wb["xattn_w_o"]]], [F32], adds=[x2])

    wg2, wu2, wd2 = ffn2_weights(x3)
    dx4, h4, g_2, u_2, parts = ffn_fwd("ffn2_fwd_loss", x3, w["ffn2_norm"], wg2, wu2, wd2, target=target)
    loss = jnp.sum(parts[::SUBLANES, 0])

    dx3, grads["ffn2_norm"], dg_2, du_2 = ffn_bwd_act("ffn2_bwd_act", dx4, x3, w["ffn2_norm"], g_2, u_2, wg2, wu2, wd2)
    grads["ffn2_w_gate"], grads["ffn2_w_up"], grads["ffn2_w_down"] = ffn_bwd_w("ffn2_bwd_w", h4, dx4, g_2, u_2, dg_2, du_2)
    sent = on_grads("ffn2", grads)

    (do2,) = mm("xattn_o_bwd", [dx3], [[wb["xattn_w_o"]]], [F32], trans=True)
    (grads["xattn_w_o"],) = mm_tn("xattn_o_bwd_w", [o2], [dx3], [(0, [0])])

    dxq0, dkm, dvm, grads["xattn_q_norm"], grads["xattn_k_norm"] = seg_bwd(
        "xattn_bwd", f_xattn, gh, xa_ins(w["xattn_q_norm"] + sent), xa_outs, [do2])
    (dh3,) = mm("xattn_q_bwd", [dxq0], [[wb["xattn_w_q"]]], [F32], trans=True)
    (grads["xattn_w_q"],) = mm_tn("xattn_q_bwd_w", [h3], [dxq0], [(0, [0])])
    wkv_k, wkv_v = wb["xattn_w_kv"][:, :XH * XHD], wb["xattn_w_kv"][:, XH * XHD:]
    (dmn,) = mm("xattn_kv_bwd", [dkm, dvm], [[wkv_k], [wkv_v]], [F32], trans=True)
    gk_w, gv_w = mm_tn("xattn_kv_bwd_w", [mn], [dkm, dvm], [(0, [0]), (0, [1])])
    grads["xattn_w_kv"] = jnp.concatenate([gk_w, gv_w], axis=1)
    (grads["mem_norm"],) = seg_bwd("mem_norm_bwd", f_norm, (1,), mem_ins,
                                   [Out((M, D_MODEL), BF16, (M, D_MODEL), lambda i: (0, 0))], [dmn])

    dx2, grads["xattn_norm"] = seg_bwd("xattn_norm_bwd", f_norm, g1, xn_ins, [otile(D_MODEL, BF16)], [dh3], adds={0: dx3})

    dyn_mla, dyn_ssm = mm("w_o_bwd", [dx2], [[wb["w_o_mla"], wb["w_o_ssm"]]], [F32, F32], trans=True)
    go_mla, go_ssm = mm_tn("w_o_bwd_w", [yn_mla, yn_ssm], [dx2], [(0, [0]), (1, [0])])
    grads["w_o"] = jnp.concatenate([go_mla, go_ssm], axis=0)

    dgl_a, dz, do_mla, grads["ssm_b_glu"], grads["out_norm_mla"], grads["out_norm_ssm"] = seg_bwd(
        "glu_out_norm_bwd", f_glu, g1, glu_ins, glu_outs, [dyn_mla, dyn_ssm])
    (dgl,) = mm("ssm_glu_bwd", [dz], [[wb["ssm_w_glu"]]], [F32], trans=True, adds=[dgl_a])
    (grads["ssm_w_glu"],) = mm_tn("ssm_glu_bwd_w", [gl], [dz], [(0, [0])])
    dy_lin, du_a, grads["ssm_d"] = seg_bwd("ssm_gelu_bwd", f_gelu, g1, gelu_ins, [otile(SSM_WIDTH, F32)], [dgl])

    du, dbb_r, dbb_i, dcb_r, dncb_i, da_r8, da_i8 = ssm_bwd("ssm_bwd", dy_lin, du_a, pu, xs_r, xs_i, bb_r, bb_i, cb_r, ncb_i,
                                                            a_r8, a_i8)
    prep_g = seg_bwd("ssm_prep_bwd", _ssm_prep_f, prep_grid, prep_ins, prep_outs, [da_r8, da_i8, dbb_r, dbb_i, dcb_r, dncb_i])
    for k, gname in enumerate(("ssm_a_re", "ssm_a_im", "ssm_log_dt", "ssm_bt_re", "ssm_bt_im", "ssm_c2_re", "ssm_c2_im")):
        grads[gname] = prep_g[k]

    dqh, dkh, dv0 = attn_bwd("mla_attn_bwd", qh, kh, v0, do_mla, o_mla, lse, t=min(ATTN_TILE, S))

    dq0, dkn0, dkr4, grads["qk_gain_q"], grads["qk_gain_k"] = seg_bwd(
        "qk_norm_rope_bwd", f_qk, gh, qk_ins, [head_out, head_out], [dqh, dkh])

    (dcq,) = mm("w_uq_bwd", [dq0], [[wb["w_uq"]]], [F32], trans=True)
    (grads["w_uq"],) = mm_tn("w_uq_bwd_w", [cq], [dq0], [(0, [0])])
    (dckv,) = mm("w_ukv_bwd", [dkn0, dv0], [[wb["w_ukv_k"]], [wb["w_ukv_v"]]], [F32], trans=True)
    grads["w_ukv_k"], grads["w_ukv_v"] = mm_tn("w_ukv_bwd_w", [ckv], [dkn0, dv0], [(0, [0]), (0, [1])])

    dpq, dpkv, grads["mla_q_norm"], grads["mla_kv_norm"] = seg_bwd("latent_norm_bwd", f_lat, g1, lat_ins, lat_outs, [dcq, dckv])

    dkr_list = [dkr4[h] for h in range(MLA_HEADS)]
    (dh2,) = mm("w_in_bwd", [dpq, dpkv, du] + dkr_list,
                [[wb["w_in_q"]], [wb["w_in_kv"]], [wb["w_in_u"]]] + [[wb["w_in_kr"]]] * MLA_HEADS, [F32], trans=True)
    grads["w_in_q"], grads["w_in_kv"], grads["w_in_u"], grads["w_in_kr"] = mm_tn(
        "w_in_bwd_w", [h2], [dpq, dpkv, du] + dkr_list, [(0, [0]), (0, [1]), (0, [2]), (0, [3, 4, 5, 6])])

    sent = on_grads("small", grads)
    dx1, grads["mix_norm"] = seg_bwd("mix_norm_bwd", f_norm, g1, mix_ins, mix_outs, [dh2], adds={0: dx2})

    dx, grads["ffn1_norm"], dg_1, du_1 = ffn_bwd_act("ffn1_bwd_act", dx1, x, w["ffn1_norm"] + sent, g_1, u_1,
                                                     wb["ffn1_w_gate"], wb["ffn1_w_up"], wb["ffn1_w_down"])
    grads["ffn1_w_gate"], grads["ffn1_w_up"], grads["ffn1_w_down"] = ffn_bwd_w("ffn1_bwd_w", h1, dx1, g_1, u_1, dg_1, du_1)
    return loss, dx, grads


def _pad_cols(a, n):
    return jnp.pad(a, ((0, 0), (0, n - a.shape[1])))


def _step_weights(full_bf16):
    wb = {}
    w_in = full_bf16["w_in"]
    wb["w_in_q"] = w_in[:, :MLA_Q_RANK]
    wb["w_in_kv"] = w_in[:, MLA_Q_RANK:MLA_Q_RANK + MLA_KV_RANK]
    wb["w_in_kr"] = _pad_cols(w_in[:, MLA_Q_RANK + MLA_KV_RANK:MLA_Q_RANK + MLA_KV_RANK + MLA_ROPE], LANES)
    wb["w_in_u"] = w_in[:, MLA_Q_RANK + MLA_KV_RANK + MLA_ROPE:]
    uq = full_bf16["mla_w_uq"].reshape(MLA_Q_RANK, MLA_HEADS, MLA_QK)
    wb["w_uq"] = jnp.pad(uq, ((0, 0), (0, 0), (0, HEAD_PAD - MLA_QK))).reshape(MLA_Q_RANK, MLA_HEADS * HEAD_PAD)
    ukv = full_bf16["mla_w_ukv"].reshape(MLA_KV_RANK, MLA_HEADS, MLA_NOPE + MLA_V)
    wb["w_ukv_k"] = ukv[:, :, :MLA_NOPE].reshape(MLA_KV_RANK, MLA_HEADS * MLA_NOPE)
    wb["w_ukv_v"] = ukv[:, :, MLA_NOPE:].reshape(MLA_KV_RANK, MLA_HEADS * MLA_V)
    wb["ssm_w_glu"] = full_bf16["ssm_w_glu"]
    wb["w_o_mla"] = full_bf16["w_o"][:SSM_WIDTH]
    wb["w_o_ssm"] = full_bf16["w_o"][SSM_WIDTH:]
    for k in ("xattn_w_q", "xattn_w_kv", "xattn_w_o"):
        wb[k] = full_bf16[k]
    return wb


def _problem_sharded_grads(g):
    out = {}
    kr = g["w_in_kr"][:, :MLA_ROPE]
    out["w_in"] = jnp.concatenate([g["w_in_q"], g["w_in_kv"], kr, g["w_in_u"]], axis=1)
    out["mla_w_uq"] = g["w_uq"].reshape(MLA_Q_RANK, MLA_HEADS, HEAD_PAD)[:, :, :MLA_QK].reshape(MLA_Q_RANK, MLA_HEADS * MLA_QK)
    k3 = g["w_ukv_k"].reshape(MLA_KV_RANK, MLA_HEADS, MLA_NOPE)
    v3 = g["w_ukv_v"].reshape(MLA_KV_RANK, MLA_HEADS, MLA_V)
    out["mla_w_ukv"] = jnp.concatenate([k3, v3], axis=2).reshape(MLA_KV_RANK, MLA_HEADS * (MLA_NOPE + MLA_V))
    for k in ("ssm_w_glu", "w_o", "xattn_w_q", "xattn_w_kv", "xattn_w_o"):
        out[k] = g[k]
    return out


def _problem_repl_grads(g):
    out = {}
    out["mla_qk_norm_q"] = g["qk_gain_q"][:, :MLA_QK]
    out["mla_qk_norm_k"] = g["qk_gain_k"][:, :MLA_QK]
    out["ssm_b_re"] = jnp.transpose(g["ssm_bt_re"].reshape(SSM_GROUPS, SSM_GROUP, SSM_STATE), (0, 2, 1))
    out["ssm_b_im"] = jnp.transpose(g["ssm_bt_im"].reshape(SSM_GROUPS, SSM_GROUP, SSM_STATE), (0, 2, 1))
    out["ssm_c_re"] = g["ssm_c2_re"].reshape(SSM_GROUPS, SSM_GROUP, SSM_STATE)
    out["ssm_c_im"] = g["ssm_c2_im"].reshape(SSM_GROUPS, SSM_GROUP, SSM_STATE)
    for k in SMALL_REPL:
        if k not in out:
            out[k] = g[k]
    return out


def _problem_grads(g):
    out = {**_problem_sharded_grads(g), **_problem_repl_grads(g)}
    out.update({k: g[k] for k in FFN_NAMES})
    return out


def _step_params(p):
    row = lambda a: a.reshape(1, -1)
    w = {k: row(p[k]) for k in ("ffn1_norm", "mix_norm", "mla_q_norm", "mla_kv_norm", "ssm_b_glu", "out_norm_mla",
                                "out_norm_ssm", "xattn_norm", "mem_norm", "xattn_q_norm", "xattn_k_norm", "ffn2_norm")}
    w["qk_gain_q"] = _pad_cols(row(p["mla_qk_norm_q"]), HEAD_PAD)
    w["qk_gain_k"] = _pad_cols(row(p["mla_qk_norm_k"]), HEAD_PAD)
    w["ssm_a_re"], w["ssm_a_im"] = p["ssm_a_re"], p["ssm_a_im"]
    w["ssm_log_dt"] = p["ssm_log_dt"].reshape(SSM_GROUPS, 1)
    w["ssm_bt_re"] = jnp.transpose(p["ssm_b_re"], (0, 2, 1)).reshape(SSM_WIDTH, SSM_STATE)
    w["ssm_bt_im"] = jnp.transpose(p["ssm_b_im"], (0, 2, 1)).reshape(SSM_WIDTH, SSM_STATE)
    w["ssm_c2_re"] = p["ssm_c_re"].reshape(SSM_WIDTH, SSM_STATE)
    w["ssm_c2_im"] = p["ssm_c_im"].reshape(SSM_WIDTH, SSM_STATE)
    w["ssm_d"] = p["ssm_d"].reshape(1, SSM_WIDTH)
    return w


ARG_NAMES = ['x', 'mem', 'positions', 'ffn1_norm', 'ffn1_w_gate', 'ffn1_w_up', 'ffn1_w_down', 'mix_norm', 'w_in', 'mla_q_norm', 'mla_w_uq', 'mla_kv_norm', 'mla_w_ukv', 'mla_qk_norm_q', 'mla_qk_norm_k', 'ssm_a_re', 'ssm_a_im', 'ssm_log_dt', 'ssm_b_re', 'ssm_b_im', 'ssm_c_re', 'ssm_c_im', 'ssm_d', 'ssm_w_glu', 'ssm_b_glu', 'out_norm_mla', 'out_norm_ssm', 'w_o', 'xattn_norm', 'mem_norm', 'xattn_w_q', 'xattn_w_kv', 'xattn_q_norm', 'xattn_k_norm', 'xattn_w_o', 'ffn2_norm', 'ffn2_w_gate', 'ffn2_w_up', 'ffn2_w_down']
WEIGHT_NAMES = ARG_NAMES[3:]


def _gather_weights(p, c):
    half = lambda a: lax.dynamic_slice_in_dim(a, c * (a.shape[0] // 2), a.shape[0] // 2, axis=0)
    ffn1 = [half(p[k].astype(BF16)) for k in FFN_NAMES[:3]]
    small = [half(_pack_shards({k: p[k].astype(BF16) for k in SMALL_SHARDED}))]
    ffn2 = [half(p[k].astype(BF16)) for k in FFN_NAMES[3:]]
    me = 4 * lax.axis_index("x") + 2 * lax.axis_index("y") + c
    own = lambda got, blocks: [lax.dynamic_update_index_in_dim(g, b, me, 0) for g, b in zip(got, blocks)]
    as_shards = lambda a: a.reshape(N_CHIPS, 2 * a.shape[1], a.shape[2])
    landing = lambda blocks: [jax.ShapeDtypeStruct((N_DEV,) + b.shape, b.dtype) for b in blocks]
    got1 = own(all_gather_halves("all_gather_weights_a", ffn1), ffn1)
    got1, small = lax.optimization_barrier((got1, small))
    flight_s, sent_s = split_copy_start("gather_small_start", _gather_copies, N_DEV - 1, small, landing(small))
    sent_s, ffn2 = lax.optimization_barrier((sent_s, ffn2))
    flight_2, sent_2 = split_copy_start("gather_ffn2_start", _gather_copies, N_DEV - 1, ffn2, landing(ffn2))
    wb = {k: as_shards(a) for k, a in zip(FFN_NAMES[:3], got1)}

    def small_weights(after):
        (got,) = own(split_copy_wait("gather_small_wait", flight_s, after), small)
        shards = _unpack_shards(got.reshape(N_CHIPS, -1, LANES))
        return _step_weights({k: _full_from_shards(k, shards[k]) for k in SMALL_SHARDED})

    def ffn2_weights(after):
        return [as_shards(a) for a in own(split_copy_wait("gather_ffn2_wait", flight_2, after), ffn2)]

    return wb, small_weights, ffn2_weights, sent_s + sent_2


class _GradReduce:
    def __init__(self, c):
        self.c, self.c_arr = c, jnp.reshape(c, (1,)).astype(jnp.int32)
        self.chip = 2 * lax.axis_index("x") + lax.axis_index("y")
        self.flights = []

    def start(self, tag, arrs):
        split = [a.reshape(N_CHIPS, 2, a.shape[1] // 2, a.shape[2]) for a in arrs]
        got = pair_send_halves(f"grad_pair_send_{tag}", split)
        parts = [pair_add(f"grad_pair_add_{tag}_{k}", s, g, self.c_arr) for k, (s, g) in enumerate(zip(split, got))]
        flight, sent = split_copy_start(f"grad_scatter_start_{tag}", _scatter_copies, N_CHIPS - 1, parts, parts)
        self.flights.append((tag, parts, flight))
        return sent

    def finish(self, after):
        halves = []
        for tag, parts, flight in self.flights:
            landed = split_copy_wait(f"grad_scatter_wait_{tag}", flight, after)
            for k, (q, p) in enumerate(zip(landed, parts)):
                mine = lax.dynamic_index_in_dim(p, self.chip, 0, keepdims=False)
                halves.append(sum_chips(f"grad_sum_{tag}_{k}", lax.dynamic_update_index_in_dim(q, mine, self.chip, 0)))
        tags = "_".join(t for t, _, _ in self.flights)
        self.flights = []
        theirs = pair_exchange(f"grad_pair_exchange_{tags}", halves)
        return [jnp.where(self.c == 0, jnp.concatenate([h, t], axis=0), jnp.concatenate([t, h], axis=0))
                for h, t in zip(halves, theirs)]


def kernel(x, mem, positions, ffn1_norm, ffn1_w_gate, ffn1_w_up, ffn1_w_down, mix_norm, w_in, mla_q_norm, mla_w_uq, mla_kv_norm, mla_w_ukv, mla_qk_norm_q, mla_qk_norm_k, ssm_a_re, ssm_a_im, ssm_log_dt, ssm_b_re, ssm_b_im, ssm_c_re, ssm_c_im, ssm_d, ssm_w_glu, ssm_b_glu, out_norm_mla, out_norm_ssm, w_o, xattn_norm, mem_norm, xattn_w_q, xattn_w_kv, xattn_q_norm, xattn_k_norm, xattn_w_o, ffn2_norm, ffn2_w_gate, ffn2_w_up, ffn2_w_down, loss_target, m_ffn1_norm, m_ffn1_w_gate, m_ffn1_w_up, m_ffn1_w_down, m_mix_norm, m_w_in, m_mla_q_norm, m_mla_w_uq, m_mla_kv_norm, m_mla_w_ukv, m_mla_qk_norm_q, m_mla_qk_norm_k, m_ssm_a_re, m_ssm_a_im, m_ssm_log_dt, m_ssm_b_re, m_ssm_b_im, m_ssm_c_re, m_ssm_c_im, m_ssm_d, m_ssm_w_glu, m_ssm_b_glu, m_out_norm_mla, m_out_norm_ssm, m_w_o, m_xattn_norm, m_mem_norm, m_xattn_w_q, m_xattn_w_kv, m_xattn_q_norm, m_xattn_k_norm, m_xattn_w_o, m_ffn2_norm, m_ffn2_w_gate, m_ffn2_w_up, m_ffn2_w_down, v_ffn1_norm, v_ffn1_w_gate, v_ffn1_w_up, v_ffn1_w_down, v_mix_norm, v_w_in, v_mla_q_norm, v_mla_w_uq, v_mla_kv_norm, v_mla_w_ukv, v_mla_qk_norm_q, v_mla_qk_norm_k, v_ssm_a_re, v_ssm_a_im, v_ssm_log_dt, v_ssm_b_re, v_ssm_b_im, v_ssm_c_re, v_ssm_c_im, v_ssm_d, v_ssm_w_glu, v_ssm_b_glu, v_out_norm_mla, v_out_norm_ssm, v_w_o, v_xattn_norm, v_mem_norm, v_xattn_w_q, v_xattn_w_kv, v_xattn_q_norm, v_xattn_k_norm, v_xattn_w_o, v_ffn2_norm, v_ffn2_w_gate, v_ffn2_w_up, v_ffn2_w_down):
    args = dict(locals())
    c = lax.axis_index("c")
    view = lambda k, a: jnp.swapaxes(a, 0, 1) if k in TRANSPOSED_VIEW else a
    p = {k: view(k, args[k][0]) for k in WEIGHT_NAMES}
    mom = {k: view(k, args["m_" + k][0]) for k in WEIGHT_NAMES}
    var = {k: view(k, args["v_" + k][0]) for k in WEIGHT_NAMES}
    natural = {k: view(k, p[k]) for k in WEIGHT_NAMES}

    wb, small_weights, ffn2_weights, sent = _gather_weights(
        {k: (p[k] if k in FFN_NAMES else natural[k]) for k in WEIGHT_NAMES}, c)
    w = _step_params(natural)
    w["ffn1_norm"] = w["ffn1_norm"] + sent
    early, late = _GradReduce(c), _GradReduce(c)

    def on_grads(tag, g):
        if tag == "ffn2":
            return early.start(tag, [g[k] for k in FFN_NAMES[3:]])
        full = _problem_sharded_grads(g)
        return early.start(tag, [_pack_shards({k: _shards_from_full(k, full[k]) for k in SMALL_SHARDED})])

    loss, dx, g = local_step(x[0], mem[0], positions[0], loss_target[0], w, wb, small_weights, ffn2_weights, on_grads)
    loss = lax.psum(loss, ("x", "y", "c"))

    sent = late.start("ffn1", [g[k] for k in FFN_NAMES[:3]])
    shards = early.finish(dx[:SUBLANES, :LANES] + sent)
    grad = dict(zip(FFN_NAMES[3:], shards[:3]))
    small_sharded = _unpack_shards(shards[3])
    grad.update({k: view(k, small_sharded[k]) for k in SMALL_SHARDED})
    repl = _problem_repl_grads(g)
    grad.update(_unpack_repl(all_reduce_small("grad_all_reduce_small", _pack_repl({k: repl[k] for k in SMALL_REPL}))))

    delta, new_m, new_v = {}, {}, {}
    small = [k for k in WEIGHT_NAMES if k not in FFN_NAMES and k not in SMALL_SHARDED]
    as2d = lambda a: a.reshape(-1, a.shape[-1])

    def update(k):
        delta[k], new_m[k], new_v[k] = adamw_big("adamw_" + k, as2d(p[k]), as2d(grad[k]), as2d(mom[k]), as2d(var[k]))

    for k in WEIGHT_NAMES:
        if k not in small and k not in FFN_NAMES[:3]:
            update(k)
    ds, nms, nvs = adamw_small("adamw_small", [as2d(p[k]) for k in small], [as2d(grad[k].reshape(p[k].shape)) for k in small],
                               [as2d(mom[k]) for k in small], [as2d(var[k]) for k in small])
    for k, d, nm, nv in zip(small, ds, nms, nvs):
        delta[k], new_m[k], new_v[k] = d, nm, nv

    grad.update(zip(FFN_NAMES[:3], late.finish(delta[FFN_NAMES[-1]])))
    for k in FFN_NAMES[:3]:
        update(k)

    shaped = lambda d, k: view(k, d.reshape(p[k].shape)).reshape(args[k].shape)
    return (loss, dx[None], *[shaped(grad[k], k) for k in WEIGHT_NAMES], *[shaped(delta[k], k) for k in WEIGHT_NAMES],
            *[shaped(new_m[k], k) for k in WEIGHT_NAMES], *[shaped(new_v[k], k) for k in WEIGHT_NAMES])
```

```python
import functools
import math

import jax
import jax.numpy as jnp
import numpy as np
from jax import lax
from jax.experimental import pallas as pl
from jax.experimental.pallas import tpu as pltpu

F32, BF16 = jnp.float32, jnp.bfloat16
EPS = 1e-6
MESH = pl.DeviceIdType.MESH

D_MODEL, D_FF = 1024, 2752
MLA_HEADS, MLA_Q_RANK, MLA_KV_RANK, MLA_NOPE, MLA_ROPE, MLA_V = 4, 384, 256, 128, 64, 128
MLA_QK = MLA_NOPE + MLA_ROPE
HEAD_PAD = 256
SSM_WIDTH, SSM_GROUP, SSM_GROUPS, SSM_STATE = 512, 16, 32, 64
SSM_LANES = SSM_GROUPS * SSM_STATE
XH, XHD = 4, 128
ROPE_THETA = 10000.0
ADAM_LR, ADAM_B1, ADAM_B2, ADAM_EPS, ADAM_WD, ADAM_STEP = 0.001, 0.9, 0.999, 1e-08, 0.01, 10
N_CHIPS, N_CORES, N_DEV = 4, 2, 8

VMEM_LIMIT_BYTES = 56 * 2**20
TOKEN_TILE = 512
FFN_WIDE_TILE = 1024
ATTN_TILE = 512
STAT_LANES = 128
SCAN_TIME_TILE = 1024
SCAN_LANE_TILE = 256
SUBLANES = 8


def _params(n_axes):
    return pltpu.CompilerParams(dimension_semantics=("arbitrary",) * n_axes, vmem_limit_bytes=VMEM_LIMIT_BYTES)


def _first(axes):
    cond = None
    for a in axes:
        c = pl.program_id(a) == 0
        cond = c if cond is None else jnp.logical_and(cond, c)
    return cond


def mm(name, xs, ws, out_dtypes, *, trans=False, adds=None, tm=TOKEN_TILE):
    rows = xs[0].shape[0]
    tm = min(tm, rows)
    n_in, n_out = len(xs), len(out_dtypes)
    pairs = [(i, j) for i in range(n_in) for j in range(n_out) if ws[i][j] is not None]
    w_list = [ws[i][j] for (i, j) in pairs]
    adds = list(adds) if adds is not None else [None] * n_out
    add_list = [a for a in adds if a is not None]
    out_cols = [None] * n_out
    for (i, j), w in zip(pairs, w_list):
        out_cols[j] = w.shape[0] if trans else w.shape[1]
    contract = (((1,), (1 if trans else 0,)), ((), ()))

    def body(*refs):
        x_refs = refs[:n_in]
        w_refs = refs[n_in:n_in + len(pairs)]
        a_refs = list(refs[n_in + len(pairs):n_in + len(pairs) + len(add_list)])
        o_refs = refs[n_in + len(pairs) + len(add_list):]
        xb = [None] * n_in
        for j in range(n_out):
            acc = None
            for p, (i, jj) in enumerate(pairs):
                if jj != j:
                    continue
                if xb[i] is None:
                    xb[i] = x_refs[i][...].astype(BF16)
                d = lax.dot_general(xb[i], w_refs[p][...].astype(BF16), contract, preferred_element_type=F32)
                acc = d if acc is None else acc + d
            if adds[j] is not None:
                acc = acc + a_refs.pop(0)[...].astype(F32)
            o_refs[j][...] = acc.astype(o_refs[j].dtype)

    in_specs = ([pl.BlockSpec((tm, x.shape[1]), lambda i: (i, 0)) for x in xs]
                + [pl.BlockSpec(w.shape, lambda i: (0, 0)) for w in w_list]
                + [pl.BlockSpec((tm, a.shape[1]), lambda i: (i, 0)) for a in add_list])
    outs = pl.pallas_call(
        body, name=name, grid=(rows // tm,), in_specs=in_specs,
        out_specs=[pl.BlockSpec((tm, n), lambda i: (i, 0)) for n in out_cols],
        out_shape=[jax.ShapeDtypeStruct((rows, n), dt) for n, dt in zip(out_cols, out_dtypes)],
        compiler_params=_params(1),
    )(*xs, *w_list, *add_list)
    return list(outs)


def mm_tn(name, xs, dys, pairs, *, tm=TOKEN_TILE):
    rows = xs[0].shape[0]
    tm = min(tm, rows)
    n_x, n_dy = len(xs), len(dys)
    contract = (((0,), (0,)), ((), ()))

    def body(*refs):
        x_refs, dy_refs, o_refs = refs[:n_x], refs[n_x:n_x + n_dy], refs[n_x + n_dy:]
        @pl.when(pl.program_id(0) == 0)
        def _():
            for o in o_refs:
                o[...] = jnp.zeros_like(o)

        for k, (i, js) in enumerate(pairs):
            dy = None
            for j in js:
                t = dy_refs[j][...].astype(F32)
                dy = t if dy is None else dy + t
            o_refs[k][...] += lax.dot_general(x_refs[i][...].astype(BF16), dy.astype(BF16), contract,
                                              preferred_element_type=F32)

    shapes = [(xs[i].shape[1], dys[js[0]].shape[1]) for (i, js) in pairs]
    outs = pl.pallas_call(
        body, name=name, grid=(rows // tm,),
        in_specs=[pl.BlockSpec((tm, a.shape[1]), lambda i: (i, 0)) for a in (*xs, *dys)],
        out_specs=[pl.BlockSpec(s, lambda i: (0, 0)) for s in shapes],
        out_shape=[jax.ShapeDtypeStruct(s, F32) for s in shapes],
        compiler_params=_params(1),
    )(*xs, *dys)
    return list(outs)


class In:
    def __init__(self, arr, block, imap, *, diff=False, acc=None, grad=None):
        self.arr, self.block, self.imap, self.diff, self.acc, self.grad = arr, block, imap, diff, acc, grad

    def spec(self):
        return pl.BlockSpec(self.block, self.imap)


class Out:
    def __init__(self, shape, dtype, block, imap):
        self.shape, self.dtype, self.block, self.imap = shape, dtype, block, imap

    def spec(self):
        return pl.BlockSpec(self.block, self.imap)


def seg_fwd(name, f, grid, ins, outs):
    n_in = len(ins)

    def body(*refs):
        res = f(*[r[...] for r in refs[:n_in]])
        for o_ref, r in zip(refs[n_in:], res):
            o_ref[...] = r.astype(o_ref.dtype)

    res = pl.pallas_call(
        body, name=name, grid=grid, in_specs=[i.spec() for i in ins], out_specs=[o.spec() for o in outs],
        out_shape=[jax.ShapeDtypeStruct(o.shape, o.dtype) for o in outs], compiler_params=_params(len(grid)),
    )(*[i.arr for i in ins])
    return list(res)


def seg_bwd(name, f, grid, ins, outs, cts, adds=None):
    n_in, n_ct = len(ins), len(cts)
    diff_idx = [k for k, i in enumerate(ins) if i.diff]
    adds = adds or {}
    add_keys = sorted(adds)
    add_list = [adds[k] for k in add_keys]

    def body(*refs):
        in_refs, ct_refs = refs[:n_in], refs[n_in:n_in + n_ct]
        add_refs = dict(zip(add_keys, refs[n_in + n_ct:n_in + n_ct + len(add_list)]))
        g_refs = refs[n_in + n_ct + len(add_list):]
        vals = [r[...] for r in in_refs]

        def g(*dv):
            full = list(vals)
            for k, v in zip(diff_idx, dv):
                full[k] = v
            return tuple(f(*full))

        _, pull = jax.vjp(g, *[vals[k].astype(F32) for k in diff_idx])
        grads = pull(tuple(c[...].astype(F32) for c in ct_refs))
        for n, (k, gr) in enumerate(zip(diff_idx, grads)):
            if k in add_refs:
                gr = gr + add_refs[k][...].astype(F32)
            if ins[k].acc is None:
                g_refs[n][...] = gr.astype(g_refs[n].dtype)
            else:
                @pl.when(_first(ins[k].acc))
                def _(n=n):
                    g_refs[n][...] = jnp.zeros_like(g_refs[n])

                g_refs[n][...] += gr

    g_specs, g_shapes = [], []
    for k in diff_idx:
        i = ins[k]
        shape, block, imap = i.grad if i.grad is not None else (i.arr.shape, i.block, i.imap)
        g_specs.append(pl.BlockSpec(block, imap))
        g_shapes.append(jax.ShapeDtypeStruct(shape, F32))
    in_specs = ([i.spec() for i in ins] + [o.spec() for o in outs]
                + [pl.BlockSpec(ins[k].block, ins[k].imap) for k in add_keys])
    res = pl.pallas_call(
        body, name=name, grid=grid, in_specs=in_specs, out_specs=g_specs, out_shape=g_shapes,
        compiler_params=_params(len(grid)),
    )(*[i.arr for i in ins], *cts, *add_list)
    return list(res)


def _rms(x, g):
    return x * lax.rsqrt(jnp.mean(x * x, axis=-1, keepdims=True) + EPS) * g


@jax.custom_vjp
def _bdot_nt(a, b):
    return lax.dot_general(a.astype(BF16), b.astype(BF16), (((1,), (1,)), ((), ())), preferred_element_type=F32)


def _bdot_nt_fwd(a, b):
    return _bdot_nt(a, b), (a, b)


def _bdot_nt_bwd(res, g):
    a, b = res
    gb = g.astype(BF16)
    da = lax.dot_general(gb, b.astype(BF16), (((1,), (0,)), ((), ())), preferred_element_type=F32)
    db = lax.dot_general(gb, a.astype(BF16), (((0,), (0,)), ((), ())), preferred_element_type=F32)
    return da, db


_bdot_nt.defvjp(_bdot_nt_fwd, _bdot_nt_bwd)


@jax.custom_vjp
def _bdot_nn(a, b):
    return lax.dot_general(a.astype(BF16), b.astype(BF16), (((1,), (0,)), ((), ())), preferred_element_type=F32)


def _bdot_nn_fwd(a, b):
    return _bdot_nn(a, b), (a, b)


def _bdot_nn_bwd(res, g):
    a, b = res
    gb = g.astype(BF16)
    da = lax.dot_general(gb, b.astype(BF16), (((1,), (1,)), ((), ())), preferred_element_type=F32)
    db = lax.dot_general(a.astype(BF16), gb, (((0,), (0,)), ((), ())), preferred_element_type=F32)
    return da, db


_bdot_nn.defvjp(_bdot_nn_fwd, _bdot_nn_bwd)


@functools.partial(jax.custom_vjp, nondiff_argnums=(1,))
def _lane_roll(x, shift):
    return pltpu.roll(x, shift, 1)


def _lane_roll_fwd(x, shift):
    return pltpu.roll(x, shift, 1), None


def _lane_roll_bwd(shift, _, g):
    return (pltpu.roll(g, (g.shape[1] - shift) % g.shape[1], 1),)


_lane_roll.defvjp(_lane_roll_fwd, _lane_roll_bwd)


def _hp_dot(a, b):
    return jnp.dot(a, b, precision=lax.Precision.HIGHEST, preferred_element_type=F32)


NT_DIMS = (((1,), (1,)), ((), ()))
TN_DIMS = (((0,), (0,)), ((), ()))


def ffn_fwd(name, x, gain, wg, wu, wd, target=None, tm=FFN_WIDE_TILE):
    S, D = x.shape
    tm = min(tm, S)
    nsh, Fs, _ = wg.shape
    with_loss = target is not None

    def body(*refs):
        if with_loss:
            x_ref, gain_ref, wg_ref, wu_ref, wd_ref, t_ref, xo_ref, h_ref, g_ref, u_ref, part_ref, acc = refs
        else:
            x_ref, gain_ref, wg_ref, wu_ref, wd_ref, xo_ref, h_ref, g_ref, u_ref, acc = refs
        j = pl.program_id(1)

        @pl.when(j == 0)
        def _():
            h_ref[...] = _rms(x_ref[...], gain_ref[...]).astype(BF16)
            acc[...] = jnp.zeros_like(acc)

        h = h_ref[...]
        g = lax.dot_general(h, wg_ref[...], NT_DIMS, preferred_element_type=F32)
        u = lax.dot_general(h, wu_ref[...], NT_DIMS, preferred_element_type=F32)
        g_ref[...] = g.astype(BF16)
        u_ref[...] = u.astype(BF16)
        a = g * jax.nn.sigmoid(g) * u
        acc[...] += jnp.dot(a.astype(BF16), wd_ref[...], preferred_element_type=F32)

        @pl.when(j == nsh - 1)
        def _():
            y = x_ref[...] + 0.5 * acc[...]
            if with_loss:
                err = y - t_ref[...]
                xo_ref[...] = err * (1.0 / D)
                part_ref[...] = jnp.full(part_ref.shape, 0.5 * jnp.sum(jnp.mean(err * err, axis=-1)), F32)
            else:
                xo_ref[...] = y

    rows = pl.BlockSpec((tm, D), lambda i, j: (i, 0))
    wspec = pl.BlockSpec((None, Fs, D), lambda i, j: (j, 0, 0))
    act = pl.BlockSpec((None, tm, Fs), lambda i, j: (j, i, 0))
    in_specs, args = [rows, pl.BlockSpec((1, D), lambda i, j: (0, 0)), wspec, wspec, wspec], [x, gain, wg, wu, wd]
    out_specs = [rows, rows, act, act]
    out_shape = [jax.ShapeDtypeStruct((S, D), F32), jax.ShapeDtypeStruct((S, D), BF16),
                 jax.ShapeDtypeStruct((nsh, S, Fs), BF16), jax.ShapeDtypeStruct((nsh, S, Fs), BF16)]
    if with_loss:
        in_specs.append(rows)
        args.append(target)
        out_specs.append(pl.BlockSpec((SUBLANES, 128), lambda i, j: (i, 0)))
        out_shape.append(jax.ShapeDtypeStruct((S // tm * SUBLANES, 128), F32))
    return pl.pallas_call(
        body, name=name, grid=(S // tm, nsh), in_specs=in_specs, out_specs=out_specs, out_shape=out_shape,
        scratch_shapes=[pltpu.VMEM((tm, D), F32)], compiler_params=_params(2),
    )(*args)


def ffn_bwd_act(name, dxo, x, gain, g, u, wg, wu, wd, tm=TOKEN_TILE):
    S, D = x.shape
    nsh, Fs, _ = wg.shape

    def body(dxo_ref, x_ref, gain_ref, g_ref, u_ref, wg_ref, wu_ref, wd_ref, dx_ref, dgain_ref, dg_ref, du_ref, dd, dh):
        i, j = pl.program_id(0), pl.program_id(1)

        @pl.when(j == 0)
        def _():
            dd[...] = (0.5 * dxo_ref[...]).astype(BF16)
            dh[...] = jnp.zeros_like(dh)

        da = lax.dot_general(dd[...], wd_ref[...], NT_DIMS, preferred_element_type=F32)
        gf, uf = g_ref[...].astype(F32), u_ref[...].astype(F32)
        sig = jax.nn.sigmoid(gf)
        dgv = (da * uf * (sig * (1.0 + gf * (1.0 - sig)))).astype(BF16)
        duv = (da * (gf * sig)).astype(BF16)
        dg_ref[...] = dgv
        du_ref[...] = duv
        dh[...] += (jnp.dot(dgv, wg_ref[...], preferred_element_type=F32)
                    + jnp.dot(duv, wu_ref[...], preferred_element_type=F32))

        @pl.when(j == nsh - 1)
        def _():
            xv = x_ref[...]
            r = lax.rsqrt(jnp.mean(xv * xv, axis=-1, keepdims=True) + EPS)
            xhat = xv * r
            dhv = dh[...]
            dxn = dhv * gain_ref[...]
            dx_ref[...] = dxo_ref[...] + r * (dxn - xhat * jnp.mean(dxn * xhat, axis=-1, keepdims=True))
            part = jnp.sum(dhv * xhat, axis=0, keepdims=True)

            @pl.when(i == 0)
            def _():
                dgain_ref[...] = part

            @pl.when(i != 0)
            def _():
                dgain_ref[...] += part

    return pl.pallas_call(
        body, name=name, grid=(S // tm, nsh),
        in_specs=[pl.BlockSpec((tm, D), lambda i, j: (i, 0)), pl.BlockSpec((tm, D), lambda i, j: (i, 0)),
                  pl.BlockSpec((1, D), lambda i, j: (0, 0)),
                  pl.BlockSpec((None, tm, Fs), lambda i, j: (j, i, 0)), pl.BlockSpec((None, tm, Fs), lambda i, j: (j, i, 0)),
                  pl.BlockSpec((None, Fs, D), lambda i, j: (j, 0, 0)), pl.BlockSpec((None, Fs, D), lambda i, j: (j, 0, 0)),
                  pl.BlockSpec((None, Fs, D), lambda i, j: (j, 0, 0))],
        out_specs=[pl.BlockSpec((tm, D), lambda i, j: (i, 0)), pl.BlockSpec((1, D), lambda i, j: (0, 0)),
                   pl.BlockSpec((None, tm, Fs), lambda i, j: (j, i, 0)), pl.BlockSpec((None, tm, Fs), lambda i, j: (j, i, 0))],
        out_shape=[jax.ShapeDtypeStruct((S, D), F32), jax.ShapeDtypeStruct((1, D), F32),
                   jax.ShapeDtypeStruct((nsh, S, Fs), BF16), jax.ShapeDtypeStruct((nsh, S, Fs), BF16)],
        scratch_shapes=[pltpu.VMEM((tm, D), BF16), pltpu.VMEM((tm, D), F32)], compiler_params=_params(2),
    )(dxo, x, gain, g, u, wg, wu, wd)


def ffn_bwd_w(name, h, dxo, g, u, dg, du, tm=FFN_WIDE_TILE):
    S, D = h.shape
    tm = min(tm, S)
    nsh, _, Fs = g.shape

    def body(h_ref, dxo_ref, g_ref, u_ref, dg_ref, du_ref, dwg_ref, dwu_ref, dwd_ref):
        i = pl.program_id(1)
        gf, uf = g_ref[...].astype(F32), u_ref[...].astype(F32)
        a = (gf * jax.nn.sigmoid(gf) * uf).astype(BF16)
        dd = (0.5 * dxo_ref[...]).astype(BF16)
        hv = h_ref[...]

        @pl.when(i == 0)
        def _():
            dwg_ref[...] = jnp.zeros_like(dwg_ref)
            dwu_ref[...] = jnp.zeros_like(dwu_ref)
            dwd_ref[...] = jnp.zeros_like(dwd_ref)

        dwg_ref[...] += lax.dot_general(dg_ref[...], hv, TN_DIMS, preferred_element_type=F32)
        dwu_ref[...] += lax.dot_general(du_ref[...], hv, TN_DIMS, preferred_element_type=F32)
        dwd_ref[...] += lax.dot_general(a, dd, TN_DIMS, preferred_element_type=F32)

    act = pl.BlockSpec((None, tm, Fs), lambda j, i: (j, i, 0))
    wspec = pl.BlockSpec((None, Fs, D), lambda j, i: (j, 0, 0))
    return pl.pallas_call(
        body, name=name, grid=(nsh, S // tm),
        in_specs=[pl.BlockSpec((tm, D), lambda j, i: (i, 0)), pl.BlockSpec((tm, D), lambda j, i: (i, 0)), act, act, act, act],
        out_specs=[wspec, wspec, wspec], out_shape=[jax.ShapeDtypeStruct((nsh, Fs, D), F32)] * 3,
        compiler_params=_params(2),
    )(h, dxo, g, u, dg, du)


NEG_BIG = -1e30


def _causal_pairs(n, by_key):
    pairs = [(qi, ki) for qi in range(n) for ki in range(qi + 1)]
    if by_key:
        pairs.sort(key=lambda p: (p[1], p[0]))
    return jnp.asarray([p[0] for p in pairs], jnp.int32), jnp.asarray([p[1] for p in pairs], jnp.int32)


def _scores(q, k, masked):
    s = lax.dot_general(q, k, (((1,), (1,)), ((), ())), preferred_element_type=F32)
    if masked:
        row = lax.broadcasted_iota(jnp.int32, s.shape, 0)
        col = lax.broadcasted_iota(jnp.int32, s.shape, 1)
        s = jnp.where(row >= col, s, NEG_BIG)
    return s


def attn_fwd(name, q, k, v, t=ATTN_TILE):
    H, S, Dk = q.shape
    Dv = v.shape[1] // H
    qt, kt = _causal_pairs(S // t, by_key=False)

    def body(qt_ref, kt_ref, q_ref, k_ref, v_ref, o_ref, lse_ref, m_sc, l_sc, acc):
        qi, ki = qt_ref[pl.program_id(1)], kt_ref[pl.program_id(1)]

        @pl.when(ki == 0)
        def _():
            m_sc[...] = jnp.full_like(m_sc, NEG_BIG)
            l_sc[...] = jnp.zeros_like(l_sc)
            acc[...] = jnp.zeros_like(acc)

        def step(masked):
            s = _scores(q_ref[...], k_ref[...], masked)
            m_prev = m_sc[...]
            m_next = jnp.maximum(m_prev, jnp.max(s, axis=-1, keepdims=True))
            alpha = jnp.exp(m_prev - m_next)
            p = jnp.exp(s - jnp.tile(m_next, (1, t // STAT_LANES)))
            l_sc[...] = alpha * l_sc[...] + jnp.sum(p, axis=-1, keepdims=True)
            acc[...] = alpha * acc[...] + jnp.dot(p.astype(BF16), v_ref[...].astype(BF16), preferred_element_type=F32)
            m_sc[...] = m_next

        @pl.when(ki < qi)
        def _():
            step(False)

        @pl.when(ki == qi)
        def _():
            step(True)
            o_ref[...] = acc[...] / l_sc[...]
            lse_ref[...] = m_sc[...] + jnp.log(l_sc[...])

    stat = pltpu.VMEM((t, STAT_LANES), F32)
    return pl.pallas_call(
        body, name=name,
        grid_spec=pltpu.PrefetchScalarGridSpec(
            num_scalar_prefetch=2, grid=(H, qt.shape[0]),
            in_specs=[pl.BlockSpec((None, t, Dk), lambda h, s, qt, kt: (h, qt[s], 0)),
                      pl.BlockSpec((None, t, Dk), lambda h, s, qt, kt: (h, kt[s], 0)),
                      pl.BlockSpec((t, Dv), lambda h, s, qt, kt: (kt[s], h))],
            out_specs=[pl.BlockSpec((t, Dv), lambda h, s, qt, kt: (qt[s], h)),
                       pl.BlockSpec((None, t, STAT_LANES), lambda h, s, qt, kt: (h, qt[s], 0))],
            scratch_shapes=[stat, stat, pltpu.VMEM((t, Dv), F32)]),
        out_shape=[jax.ShapeDtypeStruct((S, H * Dv), F32), jax.ShapeDtypeStruct((H, S, STAT_LANES), F32)],
        compiler_params=_params(2),
    )(qt, kt, q, k, v)


def attn_bwd(name, q, k, v, do, o, lse, t=ATTN_TILE):
    H, S, Dk = q.shape
    Dv = v.shape[1] // H
    qt, kt = _causal_pairs(S // t, by_key=True)
    tn_dims = (((0,), (0,)), ((), ()))

    def body(qt_ref, kt_ref, q_ref, k_ref, v_ref, do_ref, o_ref, lse_ref, dq_ref, dk_ref, dv_ref):
        step_id = pl.program_id(1)
        qi, ki = qt_ref[step_id], kt_ref[step_id]

        @pl.when(step_id == 0)
        def _():
            dq_ref[...] = jnp.zeros_like(dq_ref)

        def step(masked):
            s = _scores(q_ref[...], k_ref[...], masked)
            reps = (1, t // STAT_LANES)
            p = jnp.exp(s - jnp.tile(lse_ref[...], reps))
            dov = do_ref[...]
            delta = jnp.broadcast_to(jnp.sum(dov * o_ref[...], axis=-1, keepdims=True), (t, STAT_LANES))
            dob = dov.astype(BF16)
            dp = lax.dot_general(dob, v_ref[...].astype(BF16), (((1,), (1,)), ((), ())), preferred_element_type=F32)
            ds = (p * (dp - jnp.tile(delta, reps))).astype(BF16)
            pdv = lax.dot_general(p.astype(BF16), dob, tn_dims, preferred_element_type=F32)
            pdk = lax.dot_general(ds, q_ref[...], tn_dims, preferred_element_type=F32)
            rows = pl.ds(pl.multiple_of(qi * t, t), t)
            dq_ref[rows, :] += jnp.dot(ds, k_ref[...], preferred_element_type=F32)
            return pdk, pdv

        @pl.when(ki == qi)
        def _():
            dk_ref[...] = jnp.zeros_like(dk_ref)
            dv_ref[...] = jnp.zeros_like(dv_ref)

        def accumulate(masked):
            pdk, pdv = step(masked)
            dk_ref[...] += pdk
            dv_ref[...] += pdv

        @pl.when(ki == qi)
        def _():
            accumulate(True)

        @pl.when(ki < qi)
        def _():
            accumulate(False)

    qmap = lambda h, s, qt, kt: (h, qt[s], 0)
    kmap = lambda h, s, qt, kt: (h, kt[s], 0)
    qrow = lambda h, s, qt, kt: (qt[s], h)
    krow = lambda h, s, qt, kt: (kt[s], h)
    return pl.pallas_call(
        body, name=name,
        grid_spec=pltpu.PrefetchScalarGridSpec(
            num_scalar_prefetch=2, grid=(H, qt.shape[0]),
            in_specs=[pl.BlockSpec((None, t, Dk), qmap), pl.BlockSpec((None, t, Dk), kmap), pl.BlockSpec((t, Dv), krow),
                      pl.BlockSpec((t, Dv), qrow), pl.BlockSpec((t, Dv), qrow), pl.BlockSpec((None, t, STAT_LANES), qmap)],
            out_specs=[pl.BlockSpec((None, S, Dk), lambda h, s, qt, kt: (h, 0, 0)), pl.BlockSpec((None, t, Dk), kmap),
                       pl.BlockSpec((t, Dv), krow)]),
        out_shape=[jax.ShapeDtypeStruct((H, S, Dk), F32), jax.ShapeDtypeStruct((H, S, Dk), F32),
                   jax.ShapeDtypeStruct((S, H * Dv), F32)],
        compiler_params=_params(2),
    )(qt, kt, q, k, v, do, o, lse)


def _cmul(ar, ai, br, bi):
    return ar * br - ai * bi, ar * bi + ai * br


def _scan_tile(x_r, x_i, ar_ref, ai_ref, cr_sc, ci_sc, *, reverse, first, states=None):
    tc, lanes = x_r.shape
    nblk, lb = tc // SUBLANES, SCAN_LANE_TILE
    with_da = states is not None
    if with_da:
        xr_all, xi_all, pr_all, pi_all, dar_all, dai_all, chunk = states

    @pl.when(first)
    def _():
        cr_sc[...] = jnp.zeros_like(cr_sc)
        ci_sc[...] = jnp.zeros_like(ci_sc)
        if with_da:
            dar_all[...] = jnp.zeros_like(dar_all)
            dai_all[...] = jnp.zeros_like(dai_all)

    row = lax.broadcasted_iota(jnp.int32, (SUBLANES, lb), 0)
    for l0 in range(0, lanes, lb):
        _scan_lanes(x_r.at[:, pl.ds(l0, lb)], x_i.at[:, pl.ds(l0, lb)], ar_ref[0:1, pl.ds(l0, lb)],
                    ai_ref[0:1, pl.ds(l0, lb)], cr_sc.at[:, pl.ds(l0, lb)], ci_sc.at[:, pl.ds(l0, lb)], row, reverse,
                    nblk, None if not with_da else tuple(r.at[:, pl.ds(l0, lb)] for r in states[:6]) + (chunk,))


def _scan_lanes(x_r, x_i, a1r, a1i, cr_sc, ci_sc, row, reverse, nblk, states):
    lb = x_r.shape[1]
    with_da = states is not None
    if with_da:
        xr_ref, xi_ref, pr_ref, pi_ref, dar_ref, dai_ref, chunk = states
    if reverse:
        a1i = -a1i
    a2r, a2i = _cmul(a1r, a1i, a1r, a1i)
    a4r, a4i = _cmul(a2r, a2i, a2r, a2i)
    pw_r, pw_i = jnp.zeros((SUBLANES, lb), F32), jnp.zeros((SUBLANES, lb), F32)
    cur_r, cur_i = a1r, a1i
    for e in range(SUBLANES):
        r_at = (SUBLANES - 1 - e) if reverse else e
        pw_r = jnp.where(row == r_at, cur_r, pw_r)
        pw_i = jnp.where(row == r_at, cur_i, pw_i)
        cur_r, cur_i = _cmul(cur_r, cur_i, a1r, a1i)
    steps = []
    for d, pr, pi in ((1, a1r, a1i), (2, a2r, a2i), (4, a4r, a4i)):
        keep = (row < SUBLANES - d) if reverse else (row >= d)
        steps.append((d, jnp.where(keep, pr, 0.0), jnp.where(keep, pi, 0.0)))

    def block(jb, carry):
        if with_da:
            cr, ci, acc_r, acc_i = carry
        else:
            cr, ci = carry
        idx = (nblk - 1 - jb) if reverse else jb
        r0 = pl.multiple_of(idx * SUBLANES, SUBLANES)
        xr = x_r[pl.ds(r0, SUBLANES), :]
        xi = x_i[pl.ds(r0, SUBLANES), :]
        for d, pr, pi in steps:
            shift = SUBLANES - d if reverse else d
            sr, si = pltpu.roll(xr, shift, 0), pltpu.roll(xi, shift, 0)
            xr, xi = xr + pr * sr - pi * si, xi + pr * si + pi * sr
        xr, xi = xr + pw_r * cr - pw_i * ci, xi + pw_r * ci + pw_i * cr
        x_r[pl.ds(r0, SUBLANES), :] = xr
        x_i[pl.ds(r0, SUBLANES), :] = xi
        edge = 0 if reverse else SUBLANES - 1
        cr, ci = xr[edge:edge + 1, :], xi[edge:edge + 1, :]
        if not with_da:
            return cr, ci
        fr = xr_ref[pl.ds(r0, SUBLANES), :]
        fi = xi_ref[pl.ds(r0, SUBLANES), :]
        rp = pl.multiple_of(jnp.maximum(idx - 1, 0) * SUBLANES, SUBLANES)
        inside = idx > 0
        before_r = jnp.where(inside, xr_ref[pl.ds(rp, SUBLANES), :], pr_ref[...])
        before_i = jnp.where(inside, xi_ref[pl.ds(rp, SUBLANES), :], pi_ref[...])
        live = jnp.where(jnp.logical_or(inside, chunk > 0), 1.0, 0.0)
        last_r = before_r[SUBLANES - 1:SUBLANES, :] * live
        last_i = before_i[SUBLANES - 1:SUBLANES, :] * live
        pvr = jnp.where(row == 0, last_r, pltpu.roll(fr, 1, 0))
        pvi = jnp.where(row == 0, last_i, pltpu.roll(fi, 1, 0))
        acc_r = acc_r + xr * pvr + xi * pvi
        acc_i = acc_i + xi * pvr - xr * pvi
        return cr, ci, acc_r, acc_i

    init = (cr_sc[...], ci_sc[...])
    if with_da:
        init = init + (jnp.zeros((SUBLANES, lb), F32), jnp.zeros((SUBLANES, lb), F32))
    fin = lax.fori_loop(0, nblk, block, init)
    cr_sc[...] = fin[0]
    ci_sc[...] = fin[1]
    if with_da:
        dar_ref[...] += fin[2]
        dai_ref[...] += fin[3]


SSM_BLOCKS = 4
BLOCK_CH = SSM_WIDTH // SSM_BLOCKS
PREP_LANES = SSM_LANES // SSM_BLOCKS


def ssm_fwd(name, u, bb_r, bb_i, cb_r, ncb_i, a_r8, a_i8):
    S = u.shape[0]
    tc = min(SCAN_TIME_TILE, S)

    def body(u_ref, bbr_ref, bbi_ref, cbr_ref, ncbi_ref, ar_ref, ai_ref, xr_ref, xi_ref, y_ref, cr_sc, ci_sc):
        ub = u_ref[...].astype(BF16)
        xr_ref[...] = jnp.dot(ub, bbr_ref[...], preferred_element_type=F32)
        xi_ref[...] = jnp.dot(ub, bbi_ref[...], preferred_element_type=F32)
        _scan_tile(xr_ref, xi_ref, ar_ref, ai_ref, cr_sc, ci_sc, reverse=False, first=pl.program_id(1) == 0)
        y_ref[...] = (lax.dot_general(xr_ref[...].astype(BF16), cbr_ref[...], NT_DIMS, preferred_element_type=F32)
                      + lax.dot_general(xi_ref[...].astype(BF16), ncbi_ref[...], NT_DIMS, preferred_element_type=F32))

    ch = pl.BlockSpec((tc, BLOCK_CH), lambda b, t: (t, b))
    st = pl.BlockSpec((tc, PREP_LANES), lambda b, t: (t, b))
    wt = pl.BlockSpec((None, BLOCK_CH, PREP_LANES), lambda b, t: (b, 0, 0))
    par = pl.BlockSpec((SUBLANES, PREP_LANES), lambda b, t: (0, b))
    return pl.pallas_call(
        body, name=name, grid=(SSM_BLOCKS, S // tc), in_specs=[ch, wt, wt, wt, wt, par, par], out_specs=[st, st, ch],
        out_shape=[jax.ShapeDtypeStruct((S, SSM_LANES), F32), jax.ShapeDtypeStruct((S, SSM_LANES), F32),
                   jax.ShapeDtypeStruct((S, SSM_WIDTH), F32)],
        scratch_shapes=[pltpu.VMEM((1, PREP_LANES), F32), pltpu.VMEM((1, PREP_LANES), F32)], compiler_params=_params(2),
    )(u, bb_r, bb_i, cb_r, ncb_i, a_r8, a_i8)


def ssm_bwd(name, dy, du_add, u, xs_r, xs_i, bb_r, bb_i, cb_r, ncb_i, a_r8, a_i8):
    S = u.shape[0]
    tc = min(SCAN_TIME_TILE, S)
    nt = S // tc

    def body(dy_ref, dua_ref, u_ref, xr_ref, xi_ref, pr_ref, pi_ref, bbr_ref, bbi_ref, cbr_ref, ncbi_ref, ar_ref, ai_ref,
             du_ref, dbbr_ref, dbbi_ref, dcbr_ref, dncbi_ref, dar_ref, dai_ref, lr_sc, li_sc, cr_sc, ci_sc):
        t = pl.program_id(1)
        first = t == 0

        @pl.when(first)
        def _():
            for r in (dbbr_ref, dbbi_ref, dcbr_ref, dncbi_ref):
                r[...] = jnp.zeros_like(r)

        dyb = dy_ref[...].astype(BF16)
        lr_sc[...] = jnp.dot(dyb, cbr_ref[...], preferred_element_type=F32)
        li_sc[...] = jnp.dot(dyb, ncbi_ref[...], preferred_element_type=F32)
        _scan_tile(lr_sc, li_sc, ar_ref, ai_ref, cr_sc, ci_sc, reverse=True, first=first,
                   states=(xr_ref, xi_ref, pr_ref, pi_ref, dar_ref, dai_ref, nt - 1 - t))
        lrb, lib = lr_sc[...].astype(BF16), li_sc[...].astype(BF16)
        du_ref[...] = (lax.dot_general(lrb, bbr_ref[...], NT_DIMS, preferred_element_type=F32)
                       + lax.dot_general(lib, bbi_ref[...], NT_DIMS, preferred_element_type=F32) + dua_ref[...])
        ub = u_ref[...].astype(BF16)
        dbbr_ref[...] += lax.dot_general(ub, lrb, TN_DIMS, preferred_element_type=F32)
        dbbi_ref[...] += lax.dot_general(ub, lib, TN_DIMS, preferred_element_type=F32)
        dcbr_ref[...] += lax.dot_general(dyb, xr_ref[...].astype(BF16), TN_DIMS, preferred_element_type=F32)
        dncbi_ref[...] += lax.dot_general(dyb, xi_ref[...].astype(BF16), TN_DIMS, preferred_element_type=F32)

    ch = pl.BlockSpec((tc, BLOCK_CH), lambda b, t: (nt - 1 - t, b))
    st = pl.BlockSpec((tc, PREP_LANES), lambda b, t: (nt - 1 - t, b))
    prev = pl.BlockSpec((SUBLANES, PREP_LANES), lambda b, t: (jnp.maximum((nt - 1 - t) * (tc // SUBLANES) - 1, 0), b))
    wt = pl.BlockSpec((None, BLOCK_CH, PREP_LANES), lambda b, t: (b, 0, 0))
    par = pl.BlockSpec((SUBLANES, PREP_LANES), lambda b, t: (0, b))
    blk = jax.ShapeDtypeStruct((SSM_BLOCKS, BLOCK_CH, PREP_LANES), F32)
    rows8 = jax.ShapeDtypeStruct((SUBLANES, SSM_LANES), F32)
    return pl.pallas_call(
        body, name=name, grid=(SSM_BLOCKS, nt), in_specs=[ch, ch, ch, st, st, prev, prev, wt, wt, wt, wt, par, par],
        out_specs=[ch, wt, wt, wt, wt, par, par],
        out_shape=[jax.ShapeDtypeStruct((S, SSM_WIDTH), F32), blk, blk, blk, blk, rows8, rows8],
        scratch_shapes=[pltpu.VMEM((tc, PREP_LANES), F32), pltpu.VMEM((tc, PREP_LANES), F32),
                        pltpu.VMEM((1, PREP_LANES), F32), pltpu.VMEM((1, PREP_LANES), F32)],
        compiler_params=_params(2),
    )(dy, du_add, u, xs_r, xs_i, xs_r, xs_i, bb_r, bb_i, cb_r, ncb_i, a_r8, a_i8)


def _ssm_prep_f(a_re, a_im, log_dt, bt_re, bt_im, c_re, c_im):
    first_group = pl.program_id(0) * (SSM_GROUPS // SSM_BLOCKS)
    iota = lambda shape, d: lax.broadcasted_iota(jnp.int32, shape, d)
    grp_of_row = lambda shape: iota(shape, 0) >> int(math.log2(SSM_GROUP))
    grp_of_lane = lambda shape: iota(shape, 1) >> int(math.log2(SSM_STATE))
    rep = (grp_of_row((BLOCK_CH, SSM_GROUPS)) + first_group == iota((BLOCK_CH, SSM_GROUPS), 1)).astype(F32)
    til = ((iota((SSM_STATE, PREP_LANES), 1) & (SSM_STATE - 1)) == iota((SSM_STATE, PREP_LANES), 0)).astype(F32)
    m_rows = (grp_of_row((BLOCK_CH, PREP_LANES)) == grp_of_lane((BLOCK_CH, PREP_LANES))).astype(F32)
    m_grp = (iota((SSM_GROUPS, PREP_LANES), 0) == grp_of_lane((SSM_GROUPS, PREP_LANES)) + first_group).astype(F32)
    dt = jnp.exp(log_dt)
    decay = jnp.exp(a_re * dt)
    ar = decay * jnp.cos(a_im * dt)
    ai = decay * jnp.sin(a_im * dt)
    den = a_re * a_re + a_im * a_im
    nr = ar - 1.0
    coef_r = (nr * a_re + ai * a_im) / den
    coef_i = (ai * a_re - nr * a_im) / den
    cr, ci = _hp_dot(rep, coef_r), _hp_dot(rep, coef_i)
    bb_r = cr * bt_re - ci * bt_im
    bb_i = cr * bt_im + ci * bt_re
    big = lambda m: _hp_dot(m, til) * m_rows
    lanes = lambda m: jnp.broadcast_to(jnp.sum(_hp_dot(m, til) * m_grp, axis=0, keepdims=True), (SUBLANES, PREP_LANES))
    return lanes(ar), lanes(ai), big(bb_r), big(bb_i), big(c_re), -big(c_im)


def _whole(arr, **kw):
    nd = arr.ndim
    return In(arr, arr.shape, lambda *_: (0,) * nd, **kw)


def _adamw_math(w, g, m, v):
    m = ADAM_B1 * m + (1.0 - ADAM_B1) * g
    v = ADAM_B2 * v + (1.0 - ADAM_B2) * (g * g)
    m_hat = m / (1.0 - ADAM_B1 ** ADAM_STEP)
    v_hat = v / (1.0 - ADAM_B2 ** ADAM_STEP)
    delta = -ADAM_LR * (m_hat / (jnp.sqrt(v_hat) + ADAM_EPS) + ADAM_WD * w)
    return delta, m, v


def adamw_big(name, w, g, m, v):
    R, C = w.shape
    tr = R
    for cand in (512, 344, 256, 128):
        if R % cand == 0:
            tr = cand
            break

    def body(w_ref, g_ref, m_ref, v_ref, d_ref, nm_ref, nv_ref):
        d, nm, nv = _adamw_math(w_ref[...], g_ref[...], m_ref[...], v_ref[...])
        d_ref[...] = d
        nm_ref[...] = nm
        nv_ref[...] = nv

    spec = pl.BlockSpec((tr, C), lambda i: (i, 0))
    return pl.pallas_call(
        body, name=name, grid=(R // tr,), in_specs=[spec] * 4, out_specs=[spec] * 3,
        out_shape=[jax.ShapeDtypeStruct((R, C), F32)] * 3, compiler_params=_params(1),
    )(w, g, m, v)


def adamw_small(name, ws, gs, ms, vs):
    n = len(ws)

    def body(*refs):
        for k in range(n):
            d, nm, nv = _adamw_math(refs[k][...], refs[n + k][...], refs[2 * n + k][...], refs[3 * n + k][...])
            refs[4 * n + k][...] = d
            refs[5 * n + k][...] = nm
            refs[6 * n + k][...] = nv

    vm = pl.BlockSpec(memory_space=pltpu.VMEM)
    shapes = [jax.ShapeDtypeStruct(w.shape, F32) for w in ws]
    res = pl.pallas_call(
        body, name=name, in_specs=[vm] * (4 * n), out_specs=[vm] * (3 * n), out_shape=shapes * 3,
        compiler_params=pltpu.CompilerParams(vmem_limit_bytes=VMEM_LIMIT_BYTES),
    )(*ws, *gs, *ms, *vs)
    return res[:n], res[n:2 * n], res[2 * n:]


def _place():
    return lax.axis_index("x"), lax.axis_index("y"), lax.axis_index("c")


def _other_chips(x, y):
    return [(1 - x, y), (x, 1 - y), (1 - x, 1 - y)]


HBM = pl.BlockSpec(memory_space=pl.ANY)


def all_gather_halves(name, blocks):
    n = len(blocks)

    def body(*refs):
        in_refs, out_refs = refs[:n], refs[n:2 * n]
        send_sems, recv_sems = refs[2 * n:]
        x, y, c = _place()
        me, sibling = (x, y, c), (x, y, 1 - c)
        chips = _other_chips(x, y)

        def slot(a, px, py, pc):
            return out_refs[a].at[4 * px + 2 * py + pc]

        def copy(a, k, block, to, src=None):
            return pltpu.make_async_remote_copy(
                src_ref=slot(a, *block) if src is None else src, dst_ref=slot(a, *block),
                send_sem=send_sems.at[a, k], recv_sem=recv_sems.at[a, k], device_id=to, device_id_type=MESH)

        first = []
        for a in range(n):
            first.append(copy(a, 0, me, sibling, src=in_refs[a]))
            first += [copy(a, 1 + j, me, (*chip, c), src=in_refs[a]) for j, chip in enumerate(chips)]
        for cp in first:
            cp.start()
        passed = []
        for j, chip in enumerate(chips):
            for a in range(n):
                copy(a, 1 + j, (*chip, c), me).wait_recv()
                fw = copy(a, 4 + j, (*chip, c), sibling)
                fw.start()
                passed.append(fw)
        for a in range(n):
            copy(a, 0, sibling, me).wait_recv()
            for j, chip in enumerate(chips):
                copy(a, 4 + j, (*chip, 1 - c), me).wait_recv()
        for cp in first + passed:
            cp.wait_send()

    res = pl.pallas_call(
        body, name=name, in_specs=[HBM] * n, out_specs=[HBM] * n,
        out_shape=[jax.ShapeDtypeStruct((N_DEV,) + b.shape, b.dtype) for b in blocks],
        scratch_shapes=[pltpu.SemaphoreType.DMA((n, 7)), pltpu.SemaphoreType.DMA((n, 7))],
    )(*blocks)
    return list(res)


def pair_send_halves(name, grads):
    n = len(grads)

    def body(*refs):
        in_refs, out_refs = refs[:n], refs[n:2 * n]
        send_sems, recv_sems = refs[2 * n:]
        x, y, c = _place()
        cps = []
        for a in range(n):
            cp = pltpu.make_async_remote_copy(
                src_ref=in_refs[a].at[:, 1 - c], dst_ref=out_refs[a], send_sem=send_sems.at[a], recv_sem=recv_sems.at[a],
                device_id=(x, y, 1 - c), device_id_type=MESH)
            cp.start()
            cps.append(cp)
        for cp in cps:
            cp.wait()

    res = pl.pallas_call(
        body, name=name, in_specs=[HBM] * n, out_specs=[HBM] * n,
        out_shape=[jax.ShapeDtypeStruct((g.shape[0],) + g.shape[2:], g.dtype) for g in grads],
        scratch_shapes=[pltpu.SemaphoreType.DMA((n,)), pltpu.SemaphoreType.DMA((n,))],
    )(*grads)
    return list(res)


def pair_add(name, grad, got, c_arr):
    nsh, _, M, N = grad.shape
    tr = M
    for cand in (512, 256, 192, 128, 64, 16):
        if M % cand == 0:
            tr = cand
            break

    def body(c_ref, g_ref, p_ref, o_ref):
        o_ref[...] = (g_ref[...] + p_ref[...]).astype(BF16)

    return pl.pallas_call(
        body, name=name,
        grid_spec=pltpu.PrefetchScalarGridSpec(
            num_scalar_prefetch=1, grid=(nsh, M // tr),
            in_specs=[pl.BlockSpec((None, None, tr, N), lambda j, i, c_ref: (j, c_ref[0], i, 0)),
                      pl.BlockSpec((None, tr, N), lambda j, i, c_ref: (j, i, 0))],
            out_specs=pl.BlockSpec((None, tr, N), lambda j, i, c_ref: (j, i, 0))),
        out_shape=jax.ShapeDtypeStruct((nsh, M, N), BF16), compiler_params=_params(2),
    )(c_arr, grad, got)


def scatter_to_chips(name, parts):
    n = len(parts)

    def body(*refs):
        in_refs, out_refs = refs[:n], refs[n:2 * n]
        send_sems, recv_sems = refs[2 * n:]
        x, y, c = _place()
        mine = 2 * x + y
        chips = _other_chips(x, y)
        cps = []
        for a in range(n):
            for k, (px, py) in enumerate(chips):
                cp = pltpu.make_async_remote_copy(
                    src_ref=in_refs[a].at[2 * px + py], dst_ref=out_refs[a].at[mine],
                    send_sem=send_sems.at[a, k], recv_sem=recv_sems.at[a, k], device_id=(px, py, c), device_id_type=MESH)
                cp.start()
                cps.append((cp, a, k, px, py))
        for cp, a, k, px, py in cps:
            pltpu.make_async_remote_copy(
                src_ref=in_refs[a].at[mine], dst_ref=out_refs[a].at[2 * px + py],
                send_sem=send_sems.at[a, k], recv_sem=recv_sems.at[a, k], device_id=(px, py, c), device_id_type=MESH).wait_recv()
        for cp, *_ in cps:
            cp.wait_send()

    res = pl.pallas_call(
        body, name=name, in_specs=[HBM] * n, out_specs=[HBM] * n,
        out_shape=[jax.ShapeDtypeStruct(p.shape, p.dtype) for p in parts],
        scratch_shapes=[pltpu.SemaphoreType.DMA((n, 3)), pltpu.SemaphoreType.DMA((n, 3))],
    )(*parts)
    return list(res)


def sum_chips(name, q):
    nsh, M, N = q.shape
    tr = M
    for cand in (512, 256, 192, 128, 64, 16):
        if M % cand == 0:
            tr = cand
            break

    def body(q_ref, o_ref):
        acc = q_ref[0].astype(F32)
        for j in range(1, nsh):
            acc = acc + q_ref[j].astype(F32)
        o_ref[...] = acc

    return pl.pallas_call(
        body, name=name, grid=(M // tr,), in_specs=[pl.BlockSpec((nsh, tr, N), lambda i: (0, i, 0))],
        out_specs=pl.BlockSpec((tr, N), lambda i: (i, 0)), out_shape=jax.ShapeDtypeStruct((M, N), F32),
        compiler_params=_params(1),
    )(q)


def pair_exchange(name, halves):
    n = len(halves)

    def body(*refs):
        in_refs, out_refs = refs[:n], refs[n:2 * n]
        send_sems, recv_sems = refs[2 * n:]
        x, y, c = _place()
        cps = []
        for a in range(n):
            cp = pltpu.make_async_remote_copy(
                src_ref=in_refs[a], dst_ref=out_refs[a], send_sem=send_sems.at[a], recv_sem=recv_sems.at[a],
                device_id=(x, y, 1 - c), device_id_type=MESH)
            cp.start()
            cps.append(cp)
        for cp in cps:
            cp.wait()

    res = pl.pallas_call(
        body, name=name, in_specs=[HBM] * n, out_specs=[HBM] * n,
        out_shape=[jax.ShapeDtypeStruct(h.shape, h.dtype) for h in halves],
        scratch_shapes=[pltpu.SemaphoreType.DMA((n,)), pltpu.SemaphoreType.DMA((n,))],
    )(*halves)
    return list(res)


SEM = pl.BlockSpec(memory_space=pltpu.SEMAPHORE)
IN_HBM = pl.BlockSpec(memory_space=pltpu.HBM)
SPLIT_COPY = pltpu.CompilerParams(has_side_effects=pltpu.SideEffectType.DATAFLOW_SIDE_EFFECTING)


def _scatter_copies(src_refs, dst_refs, send_sems, recv_sems):
    x, y, c = _place()
    mine = 2 * x + y
    return [pltpu.make_async_remote_copy(
        src_ref=src_refs[a].at[2 * px + py], dst_ref=dst_refs[a].at[mine], send_sem=send_sems.at[a * (N_CHIPS - 1) + k],
        recv_sem=recv_sems.at[a * (N_CHIPS - 1) + k], device_id=(px, py, c), device_id_type=MESH)
        for a in range(len(src_refs)) for k, (px, py) in enumerate(_other_chips(x, y))]


def _gather_copies(src_refs, dst_refs, send_sems, recv_sems):
    x, y, c = _place()
    me = 4 * x + 2 * y + c
    cps = []
    for a in range(len(src_refs)):
        for k in range(1, N_DEV):
            to = (1 - x if k & 4 else x, 1 - y if k & 2 else y, 1 - c if k & 1 else c)
            s = a * (N_DEV - 1) + k - 1
            cps.append(pltpu.make_async_remote_copy(
                src_ref=src_refs[a], dst_ref=dst_refs[a].at[me], send_sem=send_sems.at[s], recv_sem=recv_sems.at[s],
                device_id=to, device_id_type=MESH))
    return cps


def split_copy_start(name, copies, n_sem, srcs, land_shapes):
    n = len(srcs)
    lands = [lax.empty(s.shape, s.dtype) for s in land_shapes]

    def body(*refs):
        for cp in copies(refs[:n], refs[n:2 * n], refs[2 * n], refs[2 * n + 1]):
            cp.start()
        refs[-1][...] = jnp.zeros_like(refs[-1])

    thru = [pltpu.HBM(a.shape, a.dtype) for a in (*srcs, *lands)]
    res = pl.pallas_call(
        body, name=name, in_specs=[IN_HBM] * (2 * n),
        out_specs=(SEM, SEM, *[IN_HBM] * (2 * n), pl.BlockSpec(memory_space=pltpu.VMEM)),
        out_shape=(pltpu.SemaphoreType.DMA((n * n_sem,)), pltpu.SemaphoreType.DMA((n * n_sem,)), *thru,
                   jax.ShapeDtypeStruct((SUBLANES, LANES), F32)),
        input_output_aliases={i: 2 + i for i in range(2 * n)}, compiler_params=SPLIT_COPY,
    )(*[pltpu.with_memory_space_constraint(a, pltpu.HBM) for a in (*srcs, *lands)])
    return (copies, n, res[0], res[1], res[2:2 + 2 * n]), res[-1][0, 0]


def split_copy_wait(name, handle, after):
    copies, n, send_sems, recv_sems, thru = handle

    def body(*refs):
        for cp in copies(refs[:n], refs[n:2 * n], refs[2 * n], refs[2 * n + 1]):
            cp.wait_send()
            cp.wait_recv()

    res = pl.pallas_call(
        body, name=name, in_specs=[IN_HBM] * (2 * n) + [SEM, SEM, pl.BlockSpec(memory_space=pl.ANY)],
        out_specs=[IN_HBM] * (2 * n), out_shape=[pltpu.HBM(a.shape, a.dtype) for a in thru],
        input_output_aliases={i: i for i in range(2 * n)}, compiler_params=SPLIT_COPY,
    )(*thru, send_sems, recv_sems, after)
    return list(res[n:])


def all_reduce_small(name, v):
    R, C = v.shape

    def body(v_ref, o_ref, gath, send_sems, recv_sems):
        x, y, c = _place()
        me, sibling = (x, y, c), (x, y, 1 - c)
        chips = _other_chips(x, y)

        def slot(px, py, pc):
            return gath.at[4 * px + 2 * py + pc]

        def copy(k, block, to, src=None):
            return pltpu.make_async_remote_copy(
                src_ref=slot(*block) if src is None else src, dst_ref=slot(*block),
                send_sem=send_sems.at[k], recv_sem=recv_sems.at[k], device_id=to, device_id_type=MESH)

        first = [copy(0, me, sibling, src=v_ref)]
        first += [copy(1 + j, me, (*chip, c), src=v_ref) for j, chip in enumerate(chips)]
        for cp in first:
            cp.start()
        slot(*me)[...] = v_ref[...]
        passed = [copy(4 + j, (*chip, c), sibling) for j, chip in enumerate(chips)]
        for j, chip in enumerate(chips):
            copy(1 + j, (*chip, c), me).wait_recv()
            passed[j].start()
        copy(0, sibling, me).wait_recv()
        for j, chip in enumerate(chips):
            copy(4 + j, (*chip, 1 - c), me).wait_recv()
        for cp in first + passed:
            cp.wait_send()
        acc = gath[0]
        for d in range(1, N_DEV):
            acc = acc + gath[d]
        o_ref[...] = acc

    vm = pl.BlockSpec(memory_space=pltpu.VMEM)
    return pl.pallas_call(
        body, name=name, in_specs=[vm], out_specs=vm, out_shape=jax.ShapeDtypeStruct((R, C), F32),
        scratch_shapes=[pltpu.VMEM((N_DEV, R, C), F32), pltpu.SemaphoreType.DMA((7,)), pltpu.SemaphoreType.DMA((7,))],
        compiler_params=pltpu.CompilerParams(vmem_limit_bytes=VMEM_LIMIT_BYTES),
    )(v)


LANES = 128
PACK_ROW_MULTIPLE = 1024
SMALL_SHARDED = {
    "w_in": ((D_MODEL, 1216), 1), "mla_w_uq": ((MLA_Q_RANK, 768), 1), "mla_w_ukv": ((MLA_KV_RANK, 1024), 1),
    "ssm_w_glu": ((SSM_WIDTH, SSM_WIDTH), 0), "w_o": ((D_MODEL, D_MODEL), 0), "xattn_w_q": ((D_MODEL, 512), 0),
    "xattn_w_kv": ((D_MODEL, 1024), 0), "xattn_w_o": ((512, D_MODEL), 1),
}
FFN_NAMES = ["ffn1_w_gate", "ffn1_w_up", "ffn1_w_down", "ffn2_w_gate", "ffn2_w_up", "ffn2_w_down"]
TRANSPOSED_VIEW = ("ffn1_w_gate", "ffn1_w_up", "ffn2_w_gate", "ffn2_w_up", "w_in", "mla_w_uq")


def _shard_shape(name):
    (r, cdim), ax = SMALL_SHARDED[name]
    return (r // N_CHIPS, cdim) if ax == 0 else (r, cdim // N_CHIPS)


def _pack_shards(shards):
    parts = []
    for name in SMALL_SHARDED:
        a = shards[name]
        lead = a.shape[:-2]
        parts.append(a.reshape(lead + (a.shape[-2] * a.shape[-1] // LANES, LANES)))
    rows = sum(q.shape[-2] for q in parts)
    parts.append(jnp.zeros(lead + (-rows % PACK_ROW_MULTIPLE, LANES), parts[0].dtype))
    return jnp.concatenate(parts, axis=-2)


def _unpack_shards(packed):
    out, r0 = {}, 0
    lead = packed.shape[:-2]
    for name in SMALL_SHARDED:
        r, cdim = _shard_shape(name)
        rows = r * cdim // LANES
        out[name] = packed[..., r0:r0 + rows, :].reshape(lead + (r, cdim))
        r0 += rows
    return out


def _full_from_shards(name, sh):
    (r, cdim), ax = SMALL_SHARDED[name]
    if ax == 0:
        return sh.reshape(r, cdim)
    return jnp.transpose(sh, (1, 0, 2)).reshape(r, cdim)


def _shards_from_full(name, full):
    (r, cdim), ax = SMALL_SHARDED[name]
    if ax == 0:
        return full.reshape(N_CHIPS, r // N_CHIPS, cdim)
    return jnp.transpose(full.reshape(r, N_CHIPS, cdim // N_CHIPS), (1, 0, 2))


SMALL_REPL = {
    "ffn1_norm": (1, 1024), "mix_norm": (1, 1024), "mla_q_norm": (1, 384), "mla_kv_norm": (1, 256),
    "mla_qk_norm_q": (1, 192), "mla_qk_norm_k": (1, 192), "ssm_a_re": (32, 64), "ssm_a_im": (32, 64),
    "ssm_log_dt": (32, 1), "ssm_b_re": (32, 64, 16), "ssm_b_im": (32, 64, 16), "ssm_c_re": (32, 16, 64),
    "ssm_c_im": (32, 16, 64), "ssm_d": (1, 512), "ssm_b_glu": (1, 512), "out_norm_mla": (1, 512),
    "out_norm_ssm": (1, 512), "xattn_norm": (1, 1024), "mem_norm": (1, 1024), "xattn_q_norm": (1, 128),
    "xattn_k_norm": (1, 128), "ffn2_norm": (1, 1024),
}


def _pack_repl(grads):
    flat = jnp.concatenate([grads[n].reshape(-1) for n in SMALL_REPL])
    rows = -(-flat.shape[0] // (LANES * SUBLANES)) * SUBLANES
    return jnp.pad(flat, (0, rows * LANES - flat.shape[0])).reshape(rows, LANES)


def _unpack_repl(packed):
    flat, out, o = packed.reshape(-1), {}, 0
    for n, shp in SMALL_REPL.items():
        size = int(np.prod(shp))
        out[n] = flat[o:o + size].reshape(shp)
        o += size
    return out


def _rope_tables(positions):
    half = MLA_ROPE // 2
    inv = ROPE_THETA ** (-jnp.arange(half, dtype=F32) / half)
    ang = positions.astype(F32)[:, None] * inv[None, :]
    cos, sin = jnp.cos(ang), jnp.sin(ang)
    S = positions.shape[0]
    z = lambda w: jnp.zeros((S, w), F32)
    keep = jnp.concatenate([jnp.ones((S, MLA_NOPE), F32), cos, cos, z(HEAD_PAD - MLA_QK)], axis=1)
    from_hi = jnp.concatenate([z(MLA_NOPE), -sin, z(HEAD_PAD - MLA_NOPE - half)], axis=1)
    from_lo = jnp.concatenate([z(MLA_NOPE + half), sin, z(HEAD_PAD - MLA_QK)], axis=1)
    return keep, from_hi, from_lo


def _norm_rope(x, g, keep, from_hi, from_lo):
    y = x * lax.rsqrt(jnp.sum(x * x, axis=-1, keepdims=True) * (1.0 / MLA_QK) + EPS) * g
    half = MLA_ROPE // 2
    return y * keep + _lane_roll(y, HEAD_PAD - half) * from_hi + _lane_roll(y, half) * from_lo


def local_step(x, mem, positions, target, w, wb, small_weights=None, ffn2_weights=None, on_grads=None):
    if small_weights is None:
        small_weights = lambda after: {}
    if ffn2_weights is None:
        ffn2_weights = lambda after: [wb[k] for k in FFN_NAMES[3:]]
    if on_grads is None:
        on_grads = lambda tag, g: 0.0
    S = x.shape[0]
    tm = min(TOKEN_TILE, S)
    nt = S // tm
    g1 = (nt,)
    gh = (MLA_HEADS, nt)
    tile = lambda arr, cols=None, cb=0, **kw: In(arr, (tm, arr.shape[1] if cols is None else cols), lambda i, cb=cb: (i, cb), **kw)
    par = lambda arr, **kw: In(arr, arr.shape, lambda *_: (0, 0), diff=True, **kw)
    otile = lambda cols, dt: Out((S, cols), dt, (tm, cols), lambda i: (i, 0))
    grads = {}

    x1, h1, g_1, u_1 = ffn_fwd("ffn1_fwd", x, w["ffn1_norm"], wb["ffn1_w_gate"], wb["ffn1_w_up"], wb["ffn1_w_down"])

    f_norm = lambda xv, g: (_rms(xv, g),)
    mix_ins = [tile(x1, diff=True), par(w["mix_norm"], acc=(0,))]
    mix_outs = [otile(D_MODEL, BF16)]
    (h2,) = seg_fwd("mix_norm_fwd", f_norm, g1, mix_ins, mix_outs)
    wb = {**wb, **small_weights(h2)}

    pq, pkv, pkr, pu = mm("w_in_fwd", [h2], [[wb["w_in_q"], wb["w_in_kv"], wb["w_in_kr"], wb["w_in_u"]]], [F32] * 4)

    f_lat = lambda a, b, ga, gb: (_rms(a, ga), _rms(b, gb))
    lat_ins = [tile(pq, diff=True), tile(pkv, diff=True), par(w["mla_q_norm"], acc=(0,)), par(w["mla_kv_norm"], acc=(0,))]
    lat_outs = [otile(MLA_Q_RANK, BF16), otile(MLA_KV_RANK, BF16)]
    cq, ckv = seg_fwd("latent_norm_fwd", f_lat, g1, lat_ins, lat_outs)

    (q0,) = mm("w_uq_fwd", [cq], [[wb["w_uq"]]], [F32])
    kn0, v0 = mm("w_ukv_fwd", [ckv], [[wb["w_ukv_k"], wb["w_ukv_v"]]], [F32, F32])

    keep, from_hi, from_lo = _rope_tables(positions)

    scale = MLA_QK ** -0.5

    def f_qk(qh, knh, kr, kp, fh, fl, gq, gk):
        return (_norm_rope(qh, gq, kp, fh, fl) * scale, _norm_rope(jnp.concatenate([knh, kr], axis=-1), gk, kp, fh, fl))

    hmap = lambda h, i: (i, h)
    tmap0 = lambda h, i: (i, 0)
    qk_ins = [In(q0, (tm, HEAD_PAD), hmap, diff=True), In(kn0, (tm, MLA_NOPE), hmap, diff=True),
              In(pkr, (tm, LANES), tmap0, diff=True,
                 grad=((MLA_HEADS, S, LANES), (None, tm, LANES), lambda h, i: (h, i, 0))),
              In(keep, (tm, HEAD_PAD), tmap0), In(from_hi, (tm, HEAD_PAD), tmap0), In(from_lo, (tm, HEAD_PAD), tmap0),
              In(w["qk_gain_q"], (1, HEAD_PAD), lambda h, i: (0, 0), diff=True, acc=(0, 1)),
              In(w["qk_gain_k"], (1, HEAD_PAD), lambda h, i: (0, 0), diff=True, acc=(0, 1))]
    head_out = Out((MLA_HEADS, S, HEAD_PAD), BF16, (None, tm, HEAD_PAD), lambda h, i: (h, i, 0))
    qh, kh = seg_fwd("qk_norm_rope_fwd", f_qk, gh, qk_ins, [head_out, head_out])

    o_mla, lse = attn_fwd("mla_attn_fwd", qh, kh, v0, t=min(ATTN_TILE, S))

    prep_grid = (SSM_BLOCKS,)
    prep_ins = ([_whole(w[k], diff=True, acc=(0,)) for k in ("ssm_a_re", "ssm_a_im", "ssm_log_dt")]
                + [In(w[k], (BLOCK_CH, SSM_STATE), lambda i: (i, 0), diff=True)
                   for k in ("ssm_bt_re", "ssm_bt_im", "ssm_c2_re", "ssm_c2_im")])
    blk_out = Out((SSM_BLOCKS, BLOCK_CH, PREP_LANES), BF16, (None, BLOCK_CH, PREP_LANES), lambda i: (i, 0, 0))
    prep_outs = [Out((SUBLANES, SSM_LANES), F32, (SUBLANES, PREP_LANES), lambda i: (0, i))] * 2 + [blk_out] * 4
    a_r8, a_i8, bb_r, bb_i, cb_r, ncb_i = seg_fwd("ssm_prep_fwd", _ssm_prep_f, prep_grid, prep_ins, prep_outs)

    xs_r, xs_i, y_lin = ssm_fwd("ssm_fwd", pu, bb_r, bb_i, cb_r, ncb_i, a_r8, a_i8)

    f_gelu = lambda yl, u, d: (jax.nn.gelu(yl + d * u),)
    gelu_ins = [tile(y_lin, diff=True), tile(pu, diff=True), par(w["ssm_d"], acc=(0,))]
    (gl,) = seg_fwd("ssm_gelu_fwd", f_gelu, g1, gelu_ins, [otile(SSM_WIDTH, F32)])
    (z,) = mm("ssm_glu_fwd", [gl], [[wb["ssm_w_glu"]]], [F32])

    def f_glu(g, zz, o, b, gm, gs):
        return (_rms(o, gm), _rms(g * jax.nn.sigmoid(zz + b), gs))

    glu_ins = [tile(gl, diff=True), tile(z, diff=True), tile(o_mla, diff=True), par(w["ssm_b_glu"], acc=(0,)),
               par(w["out_norm_mla"], acc=(0,)), par(w["out_norm_ssm"], acc=(0,))]
    glu_outs = [otile(SSM_WIDTH, BF16), otile(SSM_WIDTH, BF16)]
    yn_mla, yn_ssm = seg_fwd("glu_out_norm_fwd", f_glu, g1, glu_ins, glu_outs)

    (x2,) = mm("w_o_fwd", [yn_mla, yn_ssm], [[wb["w_o_mla"]], [wb["w_o_ssm"]]], [F32], adds=[x1])

    xn_ins = [tile(x2, diff=True), par(w["xattn_norm"], acc=(0,))]
    (h3,) = seg_fwd("xattn_norm_fwd", f_norm, g1, xn_ins, [otile(D_MODEL, BF16)])
    (xq0,) = mm("xattn_q_fwd", [h3], [[wb["xattn_w_q"]]], [F32])

    M = mem.shape[0]
    mem_ins = [In(mem, (M, D_MODEL), lambda i: (0, 0)), par(w["mem_norm"], acc=(0,))]
    (mn,) = seg_fwd("mem_norm_fwd", f_norm, (1,), mem_ins, [Out((M, D_MODEL), BF16, (M, D_MODEL), lambda i: (0, 0))])
    (kvm,) = mm("xattn_kv_fwd", [mn], [[wb["xattn_w_kv"]]], [F32])

    xscale = XHD ** -0.5

    def f_xattn(qv, kv_k, kv_v, gq, gk):
        qn, kn = _rms(qv, gq), _rms(kv_k, gk)
        s = _bdot_nt(qn, kn) * xscale
        p = jax.nn.softmax(s, axis=-1)
        return (_bdot_nn(p, kv_v),)

    def xa_ins(gain_q):
        return [In(xq0, (tm, XHD), hmap, diff=True),
                In(kvm, (M, XHD), lambda h, i: (0, h), diff=True, acc=(1,), grad=((M, XH * XHD), (M, XHD), lambda h, i: (0, h))),
                In(kvm, (M, XHD), lambda h, i: (0, XH + h), diff=True, acc=(1,), grad=((M, XH * XHD), (M, XHD), lambda h, i: (0, h))),
                In(gain_q, (1, XHD), lambda h, i: (0, 0), diff=True, acc=(0, 1)),
                In(w["xattn_k_norm"], (1, XHD), lambda h, i: (0, 0), diff=True, acc=(0, 1))]

    xa_outs = [Out((S, XH * XHD), F32, (tm, XHD), hmap)]
    (o2,) = seg_fwd("xattn_fwd", f_xattn, gh, xa_ins(w["xattn_q_norm"]), xa_outs)

    (x3,) = mm("xattn_o_fwd", [o2], [[wb["xattn_w_o"]]], [F32], adds=[x2])

    wg2, wu2, wd2 = ffn2_weights(x3)
    dx4, h4, g_2, u_2, parts = ffn_fwd("ffn2_fwd_loss", x3, w["ffn2_norm"], wg2, wu2, wd2, target=target)
    loss = jnp.sum(parts[::SUBLANES, 0])

    dx3, grads["ffn2_norm"], dg_2, du_2 = ffn_bwd_act("ffn2_bwd_act", dx4, x3, w["ffn2_norm"], g_2, u_2, wg2, wu2, wd2)
    grads["ffn2_w_gate"], grads["ffn2_w_up"], grads["ffn2_w_down"] = ffn_bwd_w("ffn2_bwd_w", h4, dx4, g_2, u_2, dg_2, du_2)
    sent = on_grads("ffn2", grads)

    (do2,) = mm("xattn_o_bwd", [dx3], [[wb["xattn_w_o"]]], [F32], trans=True)
    (grads["xattn_w_o"],) = mm_tn("xattn_o_bwd_w", [o2], [dx3], [(0, [0])])

    dxq0, dkm, dvm, grads["xattn_q_norm"], grads["xattn_k_norm"] = seg_bwd(
        "xattn_bwd", f_xattn, gh, xa_ins(w["xattn_q_norm"] + sent), xa_outs, [do2])
    (dh3,) = mm("xattn_q_bwd", [dxq0], [[wb["xattn_w_q"]]], [F32], trans=True)
    (grads["xattn_w_q"],) = mm_tn("xattn_q_bwd_w", [h3], [dxq0], [(0, [0])])
    wkv_k, wkv_v = wb["xattn_w_kv"][:, :XH * XHD], wb["xattn_w_kv"][:, XH * XHD:]
    (dmn,) = mm("xattn_kv_bwd", [dkm, dvm], [[wkv_k], [wkv_v]], [F32], trans=True)
    gk_w, gv_w = mm_tn("xattn_kv_bwd_w", [mn], [dkm, dvm], [(0, [0]), (0, [1])])
    grads["xattn_w_kv"] = jnp.concatenate([gk_w, gv_w], axis=1)
    (grads["mem_norm"],) = seg_bwd("mem_norm_bwd", f_norm, (1,), mem_ins,
                                   [Out((M, D_MODEL), BF16, (M, D_MODEL), lambda i: (0, 0))], [dmn])

    dx2, grads["xattn_norm"] = seg_bwd("xattn_norm_bwd", f_norm, g1, xn_ins, [otile(D_MODEL, BF16)], [dh3], adds={0: dx3})

    dyn_mla, dyn_ssm = mm("w_o_bwd", [dx2], [[wb["w_o_mla"], wb["w_o_ssm"]]], [F32, F32], trans=True)
    go_mla, go_ssm = mm_tn("w_o_bwd_w", [yn_mla, yn_ssm], [dx2], [(0, [0]), (1, [0])])
    grads["w_o"] = jnp.concatenate([go_mla, go_ssm], axis=0)

    dgl_a, dz, do_mla, grads["ssm_b_glu"], grads["out_norm_mla"], grads["out_norm_ssm"] = seg_bwd(
        "glu_out_norm_bwd", f_glu, g1, glu_ins, glu_outs, [dyn_mla, dyn_ssm])
    (dgl,) = mm("ssm_glu_bwd", [dz], [[wb["ssm_w_glu"]]], [F32], trans=True, adds=[dgl_a])
    (grads["ssm_w_glu"],) = mm_tn("ssm_glu_bwd_w", [gl], [dz], [(0, [0])])
    dy_lin, du_a, grads["ssm_d"] = seg_bwd("ssm_gelu_bwd", f_gelu, g1, gelu_ins, [otile(SSM_WIDTH, F32)], [dgl])

    du, dbb_r, dbb_i, dcb_r, dncb_i, da_r8, da_i8 = ssm_bwd("ssm_bwd", dy_lin, du_a, pu, xs_r, xs_i, bb_r, bb_i, cb_r, ncb_i,
                                                            a_r8, a_i8)
    prep_g = seg_bwd("ssm_prep_bwd", _ssm_prep_f, prep_grid, prep_ins, prep_outs, [da_r8, da_i8, dbb_r, dbb_i, dcb_r, dncb_i])
    for k, gname in enumerate(("ssm_a_re", "ssm_a_im", "ssm_log_dt", "ssm_bt_re", "ssm_bt_im", "ssm_c2_re", "ssm_c2_im")):
        grads[gname] = prep_g[k]

    dqh, dkh, dv0 = attn_bwd("mla_attn_bwd", qh, kh, v0, do_mla, o_mla, lse, t=min(ATTN_TILE, S))

    dq0, dkn0, dkr4, grads["qk_gain_q"], grads["qk_gain_k"] = seg_bwd(
        "qk_norm_rope_bwd", f_qk, gh, qk_ins, [head_out, head_out], [dqh, dkh])

    (dcq,) = mm("w_uq_bwd", [dq0], [[wb["w_uq"]]], [F32], trans=True)
    (grads["w_uq"],) = mm_tn("w_uq_bwd_w", [cq], [dq0], [(0, [0])])
    (dckv,) = mm("w_ukv_bwd", [dkn0, dv0], [[wb["w_ukv_k"]], [wb["w_ukv_v"]]], [F32], trans=True)
    grads["w_ukv_k"], grads["w_ukv_v"] = mm_tn("w_ukv_bwd_w", [ckv], [dkn0, dv0], [(0, [0]), (0, [1])])

    dpq, dpkv, grads["mla_q_norm"], grads["mla_kv_norm"] = seg_bwd("latent_norm_bwd", f_lat, g1, lat_ins, lat_outs, [dcq, dckv])

    dkr_list = [dkr4[h] for h in range(MLA_HEADS)]
    (dh2,) = mm("w_in_bwd", [dpq, dpkv, du] + dkr_list,
                [[wb["w_in_q"]], [wb["w_in_kv"]], [wb["w_in_u"]]] + [[wb["w_in_kr"]]] * MLA_HEADS, [F32], trans=True)
    grads["w_in_q"], grads["w_in_kv"], grads["w_in_u"], grads["w_in_kr"] = mm_tn(
        "w_in_bwd_w", [h2], [dpq, dpkv, du] + dkr_list, [(0, [0]), (0, [1]), (0, [2]), (0, [3, 4, 5, 6])])

    sent = on_grads("small", grads)
    dx1, grads["mix_norm"] = seg_bwd("mix_norm_bwd", f_norm, g1, mix_ins, mix_outs, [dh2], adds={0: dx2})

    dx, grads["ffn1_norm"], dg_1, du_1 = ffn_bwd_act("ffn1_bwd_act", dx1, x, w["ffn1_norm"] + sent, g_1, u_1,
                                                     wb["ffn1_w_gate"], wb["ffn1_w_up"], wb["ffn1_w_down"])
    grads["ffn1_w_gate"], grads["ffn1_w_up"], grads["ffn1_w_down"] = ffn_bwd_w("ffn1_bwd_w", h1, dx1, g_1, u_1, dg_1, du_1)
    return loss, dx, grads


def _pad_cols(a, n):
    return jnp.pad(a, ((0, 0), (0, n - a.shape[1])))


def _step_weights(full_bf16):
    wb = {}
    w_in = full_bf16["w_in"]
    wb["w_in_q"] = w_in[:, :MLA_Q_RANK]
    wb["w_in_kv"] = w_in[:, MLA_Q_RANK:MLA_Q_RANK + MLA_KV_RANK]
    wb["w_in_kr"] = _pad_cols(w_in[:, MLA_Q_RANK + MLA_KV_RANK:MLA_Q_RANK + MLA_KV_RANK + MLA_ROPE], LANES)
    wb["w_in_u"] = w_in[:, MLA_Q_RANK + MLA_KV_RANK + MLA_ROPE:]
    uq = full_bf16["mla_w_uq"].reshape(MLA_Q_RANK, MLA_HEADS, MLA_QK)
    wb["w_uq"] = jnp.pad(uq, ((0, 0), (0, 0), (0, HEAD_PAD - MLA_QK))).reshape(MLA_Q_RANK, MLA_HEADS * HEAD_PAD)
    ukv = full_bf16["mla_w_ukv"].reshape(MLA_KV_RANK, MLA_HEADS, MLA_NOPE + MLA_V)
    wb["w_ukv_k"] = ukv[:, :, :MLA_NOPE].reshape(MLA_KV_RANK, MLA_HEADS * MLA_NOPE)
    wb["w_ukv_v"] = ukv[:, :, MLA_NOPE:].reshape(MLA_KV_RANK, MLA_HEADS * MLA_V)
    wb["ssm_w_glu"] = full_bf16["ssm_w_glu"]
    wb["w_o_mla"] = full_bf16["w_o"][:SSM_WIDTH]
    wb["w_o_ssm"] = full_bf16["w_o"][SSM_WIDTH:]
    for k in ("xattn_w_q", "xattn_w_kv", "xattn_w_o"):
        wb[k] = full_bf16[k]
    return wb


def _problem_sharded_grads(g):
    out = {}
    kr = g["w_in_kr"][:, :MLA_ROPE]
    out["w_in"] = jnp.concatenate([g["w_in_q"], g["w_in_kv"], kr, g["w_in_u"]], axis=1)
    out["mla_w_uq"] = g["w_uq"].reshape(MLA_Q_RANK, MLA_HEADS, HEAD_PAD)[:, :, :MLA_QK].reshape(MLA_Q_RANK, MLA_HEADS * MLA_QK)
    k3 = g["w_ukv_k"].reshape(MLA_KV_RANK, MLA_HEADS, MLA_NOPE)
    v3 = g["w_ukv_v"].reshape(MLA_KV_RANK, MLA_HEADS, MLA_V)
    out["mla_w_ukv"] = jnp.concatenate([k3, v3], axis=2).reshape(MLA_KV_RANK, MLA_HEADS * (MLA_NOPE + MLA_V))
    for k in ("ssm_w_glu", "w_o", "xattn_w_q", "xattn_w_kv", "xattn_w_o"):
        out[k] = g[k]
    return out


def _problem_repl_grads(g):
    out = {}
    out["mla_qk_norm_q"] = g["qk_gain_q"][:, :MLA_QK]
    out["mla_qk_norm_k"] = g["qk_gain_k"][:, :MLA_QK]
    out["ssm_b_re"] = jnp.transpose(g["ssm_bt_re"].reshape(SSM_GROUPS, SSM_GROUP, SSM_STATE), (0, 2, 1))
    out["ssm_b_im"] = jnp.transpose(g["ssm_bt_im"].reshape(SSM_GROUPS, SSM_GROUP, SSM_STATE), (0, 2, 1))
    out["ssm_c_re"] = g["ssm_c2_re"].reshape(SSM_GROUPS, SSM_GROUP, SSM_STATE)
    out["ssm_c_im"] = g["ssm_c2_im"].reshape(SSM_GROUPS, SSM_GROUP, SSM_STATE)
    for k in SMALL_REPL:
        if k not in out:
            out[k] = g[k]
    return out


def _problem_grads(g):
    out = {**_problem_sharded_grads(g), **_problem_repl_grads(g)}
    out.update({k: g[k] for k in FFN_NAMES})
    return out


def _step_params(p):
    row = lambda a: a.reshape(1, -1)
    w = {k: row(p[k]) for k in ("ffn1_norm", "mix_norm", "mla_q_norm", "mla_kv_norm", "ssm_b_glu", "out_norm_mla",
                                "out_norm_ssm", "xattn_norm", "mem_norm", "xattn_q_norm", "xattn_k_norm", "ffn2_norm")}
    w["qk_gain_q"] = _pad_cols(row(p["mla_qk_norm_q"]), HEAD_PAD)
    w["qk_gain_k"] = _pad_cols(row(p["mla_qk_norm_k"]), HEAD_PAD)
    w["ssm_a_re"], w["ssm_a_im"] = p["ssm_a_re"], p["ssm_a_im"]
    w["ssm_log_dt"] = p["ssm_log_dt"].reshape(SSM_GROUPS, 1)
    w["ssm_bt_re"] = jnp.transpose(p["ssm_b_re"], (0, 2, 1)).reshape(SSM_WIDTH, SSM_STATE)
    w["ssm_bt_im"] = jnp.transpose(p["ssm_b_im"], (0, 2, 1)).reshape(SSM_WIDTH, SSM_STATE)
    w["ssm_c2_re"] = p["ssm_c_re"].reshape(SSM_WIDTH, SSM_STATE)
    w["ssm_c2_im"] = p["ssm_c_im"].reshape(SSM_WIDTH, SSM_STATE)
    w["ssm_d"] = p["ssm_d"].reshape(1, SSM_WIDTH)
    return w


ARG_NAMES = ['x', 'mem', 'positions', 'ffn1_norm', 'ffn1_w_gate', 'ffn1_w_up', 'ffn1_w_down', 'mix_norm', 'w_in', 'mla_q_norm', 'mla_w_uq', 'mla_kv_norm', 'mla_w_ukv', 'mla_qk_norm_q', 'mla_qk_norm_k', 'ssm_a_re', 'ssm_a_im', 'ssm_log_dt', 'ssm_b_re', 'ssm_b_im', 'ssm_c_re', 'ssm_c_im', 'ssm_d', 'ssm_w_glu', 'ssm_b_glu', 'out_norm_mla', 'out_norm_ssm', 'w_o', 'xattn_norm', 'mem_norm', 'xattn_w_q', 'xattn_w_kv', 'xattn_q_norm', 'xattn_k_norm', 'xattn_w_o', 'ffn2_norm', 'ffn2_w_gate', 'ffn2_w_up', 'ffn2_w_down']
WEIGHT_NAMES = ARG_NAMES[3:]


def _gather_weights(p, c):
    half = lambda a: lax.dynamic_slice_in_dim(a, c * (a.shape[0] // 2), a.shape[0] // 2, axis=0)
    ffn1 = [half(p[k].astype(BF16)) for k in FFN_NAMES[:3]]
    small = [half(_pack_shards({k: p[k].astype(BF16) for k in SMALL_SHARDED}))]
    ffn2 = [half(p[k].astype(BF16)) for k in FFN_NAMES[3:]]
    me = 4 * lax.axis_index("x") + 2 * lax.axis_index("y") + c
    own = lambda got, blocks: [lax.dynamic_update_index_in_dim(g, b, me, 0) for g, b in zip(got, blocks)]
    as_shards = lambda a: a.reshape(N_CHIPS, 2 * a.shape[1], a.shape[2])
    landing = lambda blocks: [jax.ShapeDtypeStruct((N_DEV,) + b.shape, b.dtype) for b in blocks]
    got1 = own(all_gather_halves("all_gather_weights_a", ffn1), ffn1)
    got1, small = lax.optimization_barrier((got1, small))
    flight_s, sent_s = split_copy_start("gather_small_start", _gather_copies, N_DEV - 1, small, landing(small))
    sent_s, ffn2 = lax.optimization_barrier((sent_s, ffn2))
    flight_2, sent_2 = split_copy_start("gather_ffn2_start", _gather_copies, N_DEV - 1, ffn2, landing(ffn2))
    wb = {k: as_shards(a) for k, a in zip(FFN_NAMES[:3], got1)}

    def small_weights(after):
        (got,) = own(split_copy_wait("gather_small_wait", flight_s, after), small)
        shards = _unpack_shards(got.reshape(N_CHIPS, -1, LANES))
        return _step_weights({k: _full_from_shards(k, shards[k]) for k in SMALL_SHARDED})

    def ffn2_weights(after):
        return [as_shards(a) for a in own(split_copy_wait("gather_ffn2_wait", flight_2, after), ffn2)]

    return wb, small_weights, ffn2_weights, sent_s + sent_2


class _GradReduce:
    def __init__(self, c):
        self.c, self.c_arr = c, jnp.reshape(c, (1,)).astype(jnp.int32)
        self.chip = 2 * lax.axis_index("x") + lax.axis_index("y")
        self.flights = []

    def start(self, tag, arrs):
        split = [a.reshape(N_CHIPS, 2, a.shape[1] // 2, a.shape[2]) for a in arrs]
        got = pair_send_halves(f"grad_pair_send_{tag}", split)
        parts = [pair_add(f"grad_pair_add_{tag}_{k}", s, g, self.c_arr) for k, (s, g) in enumerate(zip(split, got))]
        flight, sent = split_copy_start(f"grad_scatter_start_{tag}", _scatter_copies, N_CHIPS - 1, parts, parts)
        self.flights.append((tag, parts, flight))
        return sent

    def finish(self, after):
        halves = []
        for tag, parts, flight in self.flights:
            landed = split_copy_wait(f"grad_scatter_wait_{tag}", flight, after)
            for k, (q, p) in enumerate(zip(landed, parts)):
                mine = lax.dynamic_index_in_dim(p, self.chip, 0, keepdims=False)
                halves.append(sum_chips(f"grad_sum_{tag}_{k}", lax.dynamic_update_index_in_dim(q, mine, self.chip, 0)))
        tags = "_".join(t for t, _, _ in self.flights)
        self.flights = []
        theirs = pair_exchange(f"grad_pair_exchange_{tags}", halves)
        return [jnp.where(self.c == 0, jnp.concatenate([h, t], axis=0), jnp.concatenate([t, h], axis=0))
                for h, t in zip(halves, theirs)]


def kernel(x, mem, positions, ffn1_norm, ffn1_w_gate, ffn1_w_up, ffn1_w_down, mix_norm, w_in, mla_q_norm, mla_w_uq, mla_kv_norm, mla_w_ukv, mla_qk_norm_q, mla_qk_norm_k, ssm_a_re, ssm_a_im, ssm_log_dt, ssm_b_re, ssm_b_im, ssm_c_re, ssm_c_im, ssm_d, ssm_w_glu, ssm_b_glu, out_norm_mla, out_norm_ssm, w_o, xattn_norm, mem_norm, xattn_w_q, xattn_w_kv, xattn_q_norm, xattn_k_norm, xattn_w_o, ffn2_norm, ffn2_w_gate, ffn2_w_up, ffn2_w_down, loss_target, m_ffn1_norm, m_ffn1_w_gate, m_ffn1_w_up, m_ffn1_w_down, m_mix_norm, m_w_in, m_mla_q_norm, m_mla_w_uq, m_mla_kv_norm, m_mla_w_ukv, m_mla_qk_norm_q, m_mla_qk_norm_k, m_ssm_a_re, m_ssm_a_im, m_ssm_log_dt, m_ssm_b_re, m_ssm_b_im, m_ssm_c_re, m_ssm_c_im, m_ssm_d, m_ssm_w_glu, m_ssm_b_glu, m_out_norm_mla, m_out_norm_ssm, m_w_o, m_xattn_norm, m_mem_norm, m_xattn_w_q, m_xattn_w_kv, m_xattn_q_norm, m_xattn_k_norm, m_xattn_w_o, m_ffn2_norm, m_ffn2_w_gate, m_ffn2_w_up, m_ffn2_w_down, v_ffn1_norm, v_ffn1_w_gate, v_ffn1_w_up, v_ffn1_w_down, v_mix_norm, v_w_in, v_mla_q_norm, v_mla_w_uq, v_mla_kv_norm, v_mla_w_ukv, v_mla_qk_norm_q, v_mla_qk_norm_k, v_ssm_a_re, v_ssm_a_im, v_ssm_log_dt, v_ssm_b_re, v_ssm_b_im, v_ssm_c_re, v_ssm_c_im, v_ssm_d, v_ssm_w_glu, v_ssm_b_glu, v_out_norm_mla, v_out_norm_ssm, v_w_o, v_xattn_norm, v_mem_norm, v_xattn_w_q, v_xattn_w_kv, v_xattn_q_norm, v_xattn_k_norm, v_xattn_w_o, v_ffn2_norm, v_ffn2_w_gate, v_ffn2_w_up, v_ffn2_w_down):
    args = dict(locals())
    c = lax.axis_index("c")
    view = lambda k, a: jnp.swapaxes(a, 0, 1) if k in TRANSPOSED_VIEW else a
    p = {k: view(k, args[k][0]) for k in WEIGHT_NAMES}
    mom = {k: view(k, args["m_" + k][0]) for k in WEIGHT_NAMES}
    var = {k: view(k, args["v_" + k][0]) for k in WEIGHT_NAMES}
    natural = {k: view(k, p[k]) for k in WEIGHT_NAMES}

    wb, small_weights, ffn2_weights, sent = _gather_weights(
        {k: (p[k] if k in FFN_NAMES else natural[k]) for k in WEIGHT_NAMES}, c)
    w = _step_params(natural)
    w["ffn1_norm"] = w["ffn1_norm"] + sent
    early, late = _GradReduce(c), _GradReduce(c)

    def on_grads(tag, g):
        if tag == "ffn2":
            return early.start(tag, [g[k] for k in FFN_NAMES[3:]])
        full = _problem_sharded_grads(g)
        return early.start(tag, [_pack_shards({k: _shards_from_full(k, full[k]) for k in SMALL_SHARDED})])

    loss, dx, g = local_step(x[0], mem[0], positions[0], loss_target[0], w, wb, small_weights, ffn2_weights, on_grads)
    loss = lax.psum(loss, ("x", "y", "c"))

    sent = late.start("ffn1", [g[k] for k in FFN_NAMES[:3]])
    shards = early.finish(dx[:SUBLANES, :LANES] + sent)
    grad = dict(zip(FFN_NAMES[3:], shards[:3]))
    small_sharded = _unpack_shards(shards[3])
    grad.update({k: view(k, small_sharded[k]) for k in SMALL_SHARDED})
    repl = _problem_repl_grads(g)
    grad.update(_unpack_repl(all_reduce_small("grad_all_reduce_small", _pack_repl({k: repl[k] for k in SMALL_REPL}))))

    delta, new_m, new_v = {}, {}, {}
    small = [k for k in WEIGHT_NAMES if k not in FFN_NAMES and k not in SMALL_SHARDED]
    as2d = lambda a: a.reshape(-1, a.shape[-1])

    def update(k):
        delta[k], new_m[k], new_v[k] = adamw_big("adamw_" + k, as2d(p[k]), as2d(grad[k]), as2d(mom[k]), as2d(var[k]))

    for k in WEIGHT_NAMES:
        if k not in small and k not in FFN_NAMES[:3]:
            update(k)
    ds, nms, nvs = adamw_small("adamw_small", [as2d(p[k]) for k in small], [as2d(grad[k].reshape(p[k].shape)) for k in small],
                               [as2d(mom[k]) for k in small], [as2d(var[k]) for k in small])
    for k, d, nm, nv in zip(small, ds, nms, nvs):
        delta[k], new_m[k], new_v[k] = d, nm, nv

    grad.update(zip(FFN_NAMES[:3], late.finish(delta[FFN_NAMES[-1]])))
    for k in FFN_NAMES[:3]:
        update(k)

    shaped = lambda d, k: view(k, d.reshape(p[k].shape)).reshape(args[k].shape)
    return (loss, dx[None], *[shaped(grad[k], k) for k in WEIGHT_NAMES], *[shaped(delta[k], k) for k in WEIGHT_NAMES],
            *[shaped(new_m[k], k) for k in WEIGHT_NAMES], *[shaped(new_v[k], k) for k in WEIGHT_NAMES])
```

```python
import functools
import math

import jax
import jax.numpy as jnp
import numpy as np
from jax import lax
from jax.experimental import pallas as pl
from jax.experimental.pallas import tpu as pltpu

F32, BF16 = jnp.float32, jnp.bfloat16
EPS = 1e-6
MESH = pl.DeviceIdType.MESH

D_MODEL, D_FF = 1024, 2752
MLA_HEADS, MLA_Q_RANK, MLA_KV_RANK, MLA_NOPE, MLA_ROPE, MLA_V = 4, 384, 256, 128, 64, 128
MLA_QK = MLA_NOPE + MLA_ROPE
HEAD_PAD = 256
SSM_WIDTH, SSM_GROUP, SSM_GROUPS, SSM_STATE = 512, 16, 32, 64
SSM_LANES = SSM_GROUPS * SSM_STATE
XH, XHD = 4, 128
ROPE_THETA = 10000.0
ADAM_LR, ADAM_B1, ADAM_B2, ADAM_EPS, ADAM_WD, ADAM_STEP = 0.001, 0.9, 0.999, 1e-08, 0.01, 10
N_CHIPS, N_CORES, N_DEV = 4, 2, 8

VMEM_LIMIT_BYTES = 56 * 2**20
TOKEN_TILE = 512
FUSED_TILE = 256
FFN_WIDE_TILE = 1024
ATTN_TILE = 512
STAT_LANES = 128
SCAN_TIME_TILE = 1024
SCAN_LANE_TILE = 256
SUBLANES = 8


def _params(n_axes):
    return pltpu.CompilerParams(dimension_semantics=("arbitrary",) * n_axes, vmem_limit_bytes=VMEM_LIMIT_BYTES)


def _first(axes):
    cond = None
    for a in axes:
        c = pl.program_id(a) == 0
        cond = c if cond is None else jnp.logical_and(cond, c)
    return cond


def mm(name, xs, ws, out_dtypes, *, trans=False, adds=None, tm=TOKEN_TILE):
    rows = xs[0].shape[0]
    tm = min(tm, rows)
    n_in, n_out = len(xs), len(out_dtypes)
    pairs = [(i, j) for i in range(n_in) for j in range(n_out) if ws[i][j] is not None]
    w_list = [ws[i][j] for (i, j) in pairs]
    adds = list(adds) if adds is not None else [None] * n_out
    add_list = [a for a in adds if a is not None]
    out_cols = [None] * n_out
    for (i, j), w in zip(pairs, w_list):
        out_cols[j] = w.shape[0] if trans else w.shape[1]
    contract = (((1,), (1 if trans else 0,)), ((), ()))

    def body(*refs):
        x_refs = refs[:n_in]
        w_refs = refs[n_in:n_in + len(pairs)]
        a_refs = list(refs[n_in + len(pairs):n_in + len(pairs) + len(add_list)])
        o_refs = refs[n_in + len(pairs) + len(add_list):]
        xb = [None] * n_in
        for j in range(n_out):
            acc = None
            for p, (i, jj) in enumerate(pairs):
                if jj != j:
                    continue
                if xb[i] is None:
                    xb[i] = x_refs[i][...].astype(BF16)
                d = lax.dot_general(xb[i], w_refs[p][...].astype(BF16), contract, preferred_element_type=F32)
                acc = d if acc is None else acc + d
            if adds[j] is not None:
                acc = acc + a_refs.pop(0)[...].astype(F32)
            o_refs[j][...] = acc.astype(o_refs[j].dtype)

    in_specs = ([pl.BlockSpec((tm, x.shape[1]), lambda i: (i, 0)) for x in xs]
                + [pl.BlockSpec(w.shape, lambda i: (0, 0)) for w in w_list]
                + [pl.BlockSpec((tm, a.shape[1]), lambda i: (i, 0)) for a in add_list])
    outs = pl.pallas_call(
        body, name=name, grid=(rows // tm,), in_specs=in_specs,
        out_specs=[pl.BlockSpec((tm, n), lambda i: (i, 0)) for n in out_cols],
        out_shape=[jax.ShapeDtypeStruct((rows, n), dt) for n, dt in zip(out_cols, out_dtypes)],
        compiler_params=_params(1),
    )(*xs, *w_list, *add_list)
    return list(outs)


def mm_tn(name, xs, dys, pairs, *, tm=TOKEN_TILE):
    rows = xs[0].shape[0]
    tm = min(tm, rows)
    n_x, n_dy = len(xs), len(dys)
    contract = (((0,), (0,)), ((), ()))

    def body(*refs):
        x_refs, dy_refs, o_refs = refs[:n_x], refs[n_x:n_x + n_dy], refs[n_x + n_dy:]
        @pl.when(pl.program_id(0) == 0)
        def _():
            for o in o_refs:
                o[...] = jnp.zeros_like(o)

        for k, (i, js) in enumerate(pairs):
            dy = None
            for j in js:
                t = dy_refs[j][...].astype(F32)
                dy = t if dy is None else dy + t
            o_refs[k][...] += lax.dot_general(x_refs[i][...].astype(BF16), dy.astype(BF16), contract,
                                              preferred_element_type=F32)

    shapes = [(xs[i].shape[1], dys[js[0]].shape[1]) for (i, js) in pairs]
    outs = pl.pallas_call(
        body, name=name, grid=(rows // tm,),
        in_specs=[pl.BlockSpec((tm, a.shape[1]), lambda i: (i, 0)) for a in (*xs, *dys)],
        out_specs=[pl.BlockSpec(s, lambda i: (0, 0)) for s in shapes],
        out_shape=[jax.ShapeDtypeStruct(s, F32) for s in shapes],
        compiler_params=_params(1),
    )(*xs, *dys)
    return list(outs)


class In:
    def __init__(self, arr, block=None, imap=None, *, diff=False, acc=None, grad=None, weight=False):
        self.arr, self.block, self.imap, self.diff, self.acc, self.grad = arr, block, imap, diff, acc, grad
        self.weight = weight

    def spec(self):
        return pl.BlockSpec(memory_space=pltpu.VMEM) if self.weight else pl.BlockSpec(self.block, self.imap)


class Out:
    def __init__(self, shape, dtype, block, imap):
        self.shape, self.dtype, self.block, self.imap = shape, dtype, block, imap

    def spec(self):
        return pl.BlockSpec(self.block, self.imap)


class Wt:
    def __init__(self, ref, zeros=None):
        self.ref, self.zeros = ref, zeros


@jax.custom_vjp
def _wdot(a, w, z):
    return jnp.dot(a.astype(BF16), w, preferred_element_type=F32)


def _wdot_fwd(a, w, z):
    return _wdot(a, w, z), (a, w)


def _wdot_bwd(res, g):
    a, w = res
    gb = g.astype(BF16)
    da = lax.dot_general(gb, w, (((1,), (1,)), ((), ())), preferred_element_type=F32)
    dz = lax.dot_general(a.astype(BF16), gb, (((0,), (0,)), ((), ())), preferred_element_type=F32)
    return da, None, dz


_wdot.defvjp(_wdot_fwd, _wdot_bwd)


def wdot(a, wt, head=None):
    w = wt.ref[...] if head is None else wt.ref[head]
    if wt.zeros is None:
        return jnp.dot(a.astype(BF16), w, preferred_element_type=F32)
    return _wdot(a, w, wt.zeros[0 if head is None else head])


def seg_fwd(name, f, grid, ins, outs):
    n_in = len(ins)

    def body(*refs):
        res = f(*[Wt(r) if i.weight else r[...] for i, r in zip(ins, refs[:n_in])])
        for o_ref, r in zip(refs[n_in:], res):
            o_ref[...] = r.astype(o_ref.dtype)

    res = pl.pallas_call(
        body, name=name, grid=grid, in_specs=[i.spec() for i in ins], out_specs=[o.spec() for o in outs],
        out_shape=[jax.ShapeDtypeStruct(o.shape, o.dtype) for o in outs], compiler_params=_params(len(grid)),
    )(*[i.arr for i in ins])
    return list(res)


def seg_bwd(name, f, grid, ins, outs, cts, adds=None):
    n_in, n_ct = len(ins), len(cts)
    grad_idx = [k for k, i in enumerate(ins) if i.diff or i.weight]
    adds = adds or {}
    add_keys = sorted(adds)
    add_list = [adds[k] for k in add_keys]
    heads = {k: (ins[k].arr.shape[0] if ins[k].arr.ndim == 3 else 1) for k in grad_idx if ins[k].weight}

    def body(*refs):
        in_refs, ct_refs = refs[:n_in], refs[n_in:n_in + n_ct]
        add_refs = dict(zip(add_keys, refs[n_in + n_ct:n_in + n_ct + len(add_list)]))
        g_refs = dict(zip(grad_idx, refs[n_in + n_ct + len(add_list):]))
        vals = [None if i.weight else r[...] for i, r in zip(ins, in_refs)]
        primals, owner = [], []
        for k in grad_idx:
            if ins[k].weight:
                for h in range(heads[k]):
                    primals.append(jnp.zeros(ins[k].arr.shape[-2:], F32))
                    owner.append((k, h))
            else:
                primals.append(vals[k].astype(F32))
                owner.append((k, None))

        def g(*dv):
            full = list(vals)
            zeros = {}
            for (k, h), v in zip(owner, dv):
                if h is None:
                    full[k] = v
                else:
                    zeros.setdefault(k, []).append(v)
            for k, z in zeros.items():
                full[k] = Wt(in_refs[k], z)
            return tuple(f(*full))

        _, pull = jax.vjp(g, *primals)
        grads = pull(tuple(c[...].astype(F32) for c in ct_refs))
        for k in grad_idx:
            if ins[k].weight or ins[k].acc is not None:
                @pl.when(_first(range(len(grid)) if ins[k].weight else ins[k].acc))
                def _(k=k):
                    g_refs[k][...] = jnp.zeros_like(g_refs[k])

        for (k, h), gr in zip(owner, grads):
            if ins[k].weight:
                if ins[k].arr.ndim == 3:
                    g_refs[k][h] += gr
                else:
                    g_refs[k][...] += gr
                continue
            if k in add_refs:
                gr = gr + add_refs[k][...].astype(F32)
            if ins[k].acc is None:
                g_refs[k][...] = gr.astype(g_refs[k].dtype)
            else:
                g_refs[k][...] += gr

    g_specs, g_shapes = [], []
    for k in grad_idx:
        i = ins[k]
        if i.weight:
            g_specs.append(pl.BlockSpec(memory_space=pltpu.VMEM))
            g_shapes.append(jax.ShapeDtypeStruct(i.arr.shape, F32))
            continue
        shape, block, imap = i.grad if i.grad is not None else (i.arr.shape, i.block, i.imap)
        g_specs.append(pl.BlockSpec(block, imap))
        g_shapes.append(jax.ShapeDtypeStruct(shape, F32))
    in_specs = ([i.spec() for i in ins] + [o.spec() for o in outs]
                + [pl.BlockSpec(ins[k].block, ins[k].imap) for k in add_keys])
    res = pl.pallas_call(
        body, name=name, grid=grid, in_specs=in_specs, out_specs=g_specs, out_shape=g_shapes,
        compiler_params=_params(len(grid)),
    )(*[i.arr for i in ins], *cts, *add_list)
    return list(res)


def _rms(x, g):
    return x * lax.rsqrt(jnp.mean(x * x, axis=-1, keepdims=True) + EPS) * g


@jax.custom_vjp
def _bdot_nt(a, b):
    return lax.dot_general(a.astype(BF16), b.astype(BF16), (((1,), (1,)), ((), ())), preferred_element_type=F32)


def _bdot_nt_fwd(a, b):
    return _bdot_nt(a, b), (a, b)


def _bdot_nt_bwd(res, g):
    a, b = res
    gb = g.astype(BF16)
    da = lax.dot_general(gb, b.astype(BF16), (((1,), (0,)), ((), ())), preferred_element_type=F32)
    db = lax.dot_general(gb, a.astype(BF16), (((0,), (0,)), ((), ())), preferred_element_type=F32)
    return da, db


_bdot_nt.defvjp(_bdot_nt_fwd, _bdot_nt_bwd)


@jax.custom_vjp
def _bdot_nn(a, b):
    return lax.dot_general(a.astype(BF16), b.astype(BF16), (((1,), (0,)), ((), ())), preferred_element_type=F32)


def _bdot_nn_fwd(a, b):
    return _bdot_nn(a, b), (a, b)


def _bdot_nn_bwd(res, g):
    a, b = res
    gb = g.astype(BF16)
    da = lax.dot_general(gb, b.astype(BF16), (((1,), (1,)), ((), ())), preferred_element_type=F32)
    db = lax.dot_general(a.astype(BF16), gb, (((0,), (0,)), ((), ())), preferred_element_type=F32)
    return da, db


_bdot_nn.defvjp(_bdot_nn_fwd, _bdot_nn_bwd)


@functools.partial(jax.custom_vjp, nondiff_argnums=(1,))
def _lane_roll(x, shift):
    return pltpu.roll(x, shift, 1)


def _lane_roll_fwd(x, shift):
    return pltpu.roll(x, shift, 1), None


def _lane_roll_bwd(shift, _, g):
    return (pltpu.roll(g, (g.shape[1] - shift) % g.shape[1], 1),)


_lane_roll.defvjp(_lane_roll_fwd, _lane_roll_bwd)


def _hp_dot(a, b):
    return jnp.dot(a, b, precision=lax.Precision.HIGHEST, preferred_element_type=F32)


NT_DIMS = (((1,), (1,)), ((), ()))
TN_DIMS = (((0,), (0,)), ((), ()))


def ffn_fwd(name, x, gain, wg, wu, wd, target=None, tm=FFN_WIDE_TILE):
    S, D = x.shape
    tm = min(tm, S)
    nsh, Fs, _ = wg.shape
    with_loss = target is not None

    def body(*refs):
        if with_loss:
            x_ref, gain_ref, wg_ref, wu_ref, wd_ref, t_ref, xo_ref, h_ref, g_ref, u_ref, part_ref, acc = refs
        else:
            x_ref, gain_ref, wg_ref, wu_ref, wd_ref, xo_ref, h_ref, g_ref, u_ref, acc = refs
        j = pl.program_id(1)

        @pl.when(j == 0)
        def _():
            h_ref[...] = _rms(x_ref[...], gain_ref[...]).astype(BF16)
            acc[...] = jnp.zeros_like(acc)

        h = h_ref[...]
        g = lax.dot_general(h, wg_ref[...], NT_DIMS, preferred_element_type=F32)
        u = lax.dot_general(h, wu_ref[...], NT_DIMS, preferred_element_type=F32)
        g_ref[...] = g.astype(BF16)
        u_ref[...] = u.astype(BF16)
        a = g * jax.nn.sigmoid(g) * u
        acc[...] += jnp.dot(a.astype(BF16), wd_ref[...], preferred_element_type=F32)

        @pl.when(j == nsh - 1)
        def _():
            y = x_ref[...] + 0.5 * acc[...]
            if with_loss:
                err = y - t_ref[...]
                xo_ref[...] = err * (1.0 / D)
                part_ref[...] = jnp.full(part_ref.shape, 0.5 * jnp.sum(jnp.mean(err * err, axis=-1)), F32)
            else:
                xo_ref[...] = y

    rows = pl.BlockSpec((tm, D), lambda i, j: (i, 0))
    wspec = pl.BlockSpec((None, Fs, D), lambda i, j: (j, 0, 0))
    act = pl.BlockSpec((None, tm, Fs), lambda i, j: (j, i, 0))
    in_specs, args = [rows, pl.BlockSpec((1, D), lambda i, j: (0, 0)), wspec, wspec, wspec], [x, gain, wg, wu, wd]
    out_specs = [rows, rows, act, act]
    out_shape = [jax.ShapeDtypeStruct((S, D), F32), jax.ShapeDtypeStruct((S, D), BF16),
                 jax.ShapeDtypeStruct((nsh, S, Fs), BF16), jax.ShapeDtypeStruct((nsh, S, Fs), BF16)]
    if with_loss:
        in_specs.append(rows)
        args.append(target)
        out_specs.append(pl.BlockSpec((SUBLANES, 128), lambda i, j: (i, 0)))
        out_shape.append(jax.ShapeDtypeStruct((S // tm * SUBLANES, 128), F32))
    return pl.pallas_call(
        body, name=name, grid=(S // tm, nsh), in_specs=in_specs, out_specs=out_specs, out_shape=out_shape,
        scratch_shapes=[pltpu.VMEM((tm, D), F32)], compiler_params=_params(2),
    )(*args)


def ffn_bwd_act(name, dxo, x, gain, g, u, wg, wu, wd, tm=TOKEN_TILE):
    S, D = x.shape
    nsh, Fs, _ = wg.shape

    def body(dxo_ref, x_ref, gain_ref, g_ref, u_ref, wg_ref, wu_ref, wd_ref, dx_ref, dgain_ref, dg_ref, du_ref, dd, dh):
        i, j = pl.program_id(0), pl.program_id(1)

        @pl.when(j == 0)
        def _():
            dd[...] = (0.5 * dxo_ref[...]).astype(BF16)
            dh[...] = jnp.zeros_like(dh)

        da = lax.dot_general(dd[...], wd_ref[...], NT_DIMS, preferred_element_type=F32)
        gf, uf = g_ref[...].astype(F32), u_ref[...].astype(F32)
        sig = jax.nn.sigmoid(gf)
        dgv = (da * uf * (sig * (1.0 + gf * (1.0 - sig)))).astype(BF16)
        duv = (da * (gf * sig)).astype(BF16)
        dg_ref[...] = dgv
        du_ref[...] = duv
        dh[...] += (jnp.dot(dgv, wg_ref[...], preferred_element_type=F32)
                    + jnp.dot(duv, wu_ref[...], preferred_element_type=F32))

        @pl.when(j == nsh - 1)
        def _():
            xv = x_ref[...]
            r = lax.rsqrt(jnp.mean(xv * xv, axis=-1, keepdims=True) + EPS)
            xhat = xv * r
            dhv = dh[...]
            dxn = dhv * gain_ref[...]
            dx_ref[...] = dxo_ref[...] + r * (dxn - xhat * jnp.mean(dxn * xhat, axis=-1, keepdims=True))
            part = jnp.sum(dhv * xhat, axis=0, keepdims=True)

            @pl.when(i == 0)
            def _():
                dgain_ref[...] = part

            @pl.when(i != 0)
            def _():
                dgain_ref[...] += part

    return pl.pallas_call(
        body, name=name, grid=(S // tm, nsh),
        in_specs=[pl.BlockSpec((tm, D), lambda i, j: (i, 0)), pl.BlockSpec((tm, D), lambda i, j: (i, 0)),
                  pl.BlockSpec((1, D), lambda i, j: (0, 0)),
                  pl.BlockSpec((None, tm, Fs), lambda i, j: (j, i, 0)), pl.BlockSpec((None, tm, Fs), lambda i, j: (j, i, 0)),
                  pl.BlockSpec((None, Fs, D), lambda i, j: (j, 0, 0)), pl.BlockSpec((None, Fs, D), lambda i, j: (j, 0, 0)),
                  pl.BlockSpec((None, Fs, D), lambda i, j: (j, 0, 0))],
        out_specs=[pl.BlockSpec((tm, D), lambda i, j: (i, 0)), pl.BlockSpec((1, D), lambda i, j: (0, 0)),
                   pl.BlockSpec((None, tm, Fs), lambda i, j: (j, i, 0)), pl.BlockSpec((None, tm, Fs), lambda i, j: (j, i, 0))],
        out_shape=[jax.ShapeDtypeStruct((S, D), F32), jax.ShapeDtypeStruct((1, D), F32),
                   jax.ShapeDtypeStruct((nsh, S, Fs), BF16), jax.ShapeDtypeStruct((nsh, S, Fs), BF16)],
        scratch_shapes=[pltpu.VMEM((tm, D), BF16), pltpu.VMEM((tm, D), F32)], compiler_params=_params(2),
    )(dxo, x, gain, g, u, wg, wu, wd)


def ffn_bwd_w(name, h, dxo, g, u, dg, du, tm=FFN_WIDE_TILE):
    S, D = h.shape
    tm = min(tm, S)
    nsh, _, Fs = g.shape

    def body(h_ref, dxo_ref, g_ref, u_ref, dg_ref, du_ref, dwg_ref, dwu_ref, dwd_ref):
        i = pl.program_id(1)
        gf, uf = g_ref[...].astype(F32), u_ref[...].astype(F32)
        a = (gf * jax.nn.sigmoid(gf) * uf).astype(BF16)
        dd = (0.5 * dxo_ref[...]).astype(BF16)
        hv = h_ref[...]

        @pl.when(i == 0)
        def _():
            dwg_ref[...] = jnp.zeros_like(dwg_ref)
            dwu_ref[...] = jnp.zeros_like(dwu_ref)
            dwd_ref[...] = jnp.zeros_like(dwd_ref)

        dwg_ref[...] += lax.dot_general(dg_ref[...], hv, TN_DIMS, preferred_element_type=F32)
        dwu_ref[...] += lax.dot_general(du_ref[...], hv, TN_DIMS, preferred_element_type=F32)
        dwd_ref[...] += lax.dot_general(a, dd, TN_DIMS, preferred_element_type=F32)

    act = pl.BlockSpec((None, tm, Fs), lambda j, i: (j, i, 0))
    wspec = pl.BlockSpec((None, Fs, D), lambda j, i: (j, 0, 0))
    return pl.pallas_call(
        body, name=name, grid=(nsh, S // tm),
        in_specs=[pl.BlockSpec((tm, D), lambda j, i: (i, 0)), pl.BlockSpec((tm, D), lambda j, i: (i, 0)), act, act, act, act],
        out_specs=[wspec, wspec, wspec], out_shape=[jax.ShapeDtypeStruct((nsh, Fs, D), F32)] * 3,
        compiler_params=_params(2),
    )(h, dxo, g, u, dg, du)


NEG_BIG = -1e30


def _causal_pairs(n, by_key):
    pairs = [(qi, ki) for qi in range(n) for ki in range(qi + 1)]
    if by_key:
        pairs.sort(key=lambda p: (p[1], p[0]))
    return jnp.asarray([p[0] for p in pairs], jnp.int32), jnp.asarray([p[1] for p in pairs], jnp.int32)


def _scores(q, k, masked):
    s = lax.dot_general(q, k, (((1,), (1,)), ((), ())), preferred_element_type=F32)
    if masked:
        row = lax.broadcasted_iota(jnp.int32, s.shape, 0)
        col = lax.broadcasted_iota(jnp.int32, s.shape, 1)
        s = jnp.where(row >= col, s, NEG_BIG)
    return s


def attn_fwd(name, q, k, v, t=ATTN_TILE):
    S, Dk = q.shape[0], HEAD_PAD
    H = q.shape[1] // Dk
    Dv = v.shape[1] // H
    qt, kt = _causal_pairs(S // t, by_key=False)

    def body(qt_ref, kt_ref, q_ref, k_ref, v_ref, o_ref, lse_ref, m_sc, l_sc, acc):
        qi, ki = qt_ref[pl.program_id(1)], kt_ref[pl.program_id(1)]

        @pl.when(ki == 0)
        def _():
            m_sc[...] = jnp.full_like(m_sc, NEG_BIG)
            l_sc[...] = jnp.zeros_like(l_sc)
            acc[...] = jnp.zeros_like(acc)

        def step(masked):
            s = _scores(q_ref[...], k_ref[...], masked)
            m_prev = m_sc[...]
            m_next = jnp.maximum(m_prev, jnp.max(s, axis=-1, keepdims=True))
            alpha = jnp.exp(m_prev - m_next)
            p = jnp.exp(s - jnp.tile(m_next, (1, t // STAT_LANES)))
            l_sc[...] = alpha * l_sc[...] + jnp.sum(p, axis=-1, keepdims=True)
            acc[...] = alpha * acc[...] + jnp.dot(p.astype(BF16), v_ref[...].astype(BF16), preferred_element_type=F32)
            m_sc[...] = m_next

        @pl.when(ki < qi)
        def _():
            step(False)

        @pl.when(ki == qi)
        def _():
            step(True)
            o_ref[...] = acc[...] / l_sc[...]
            lse_ref[...] = m_sc[...] + jnp.log(l_sc[...])

    stat = pltpu.VMEM((t, STAT_LANES), F32)
    return pl.pallas_call(
        body, name=name,
        grid_spec=pltpu.PrefetchScalarGridSpec(
            num_scalar_prefetch=2, grid=(H, qt.shape[0]),
            in_specs=[pl.BlockSpec((t, Dk), lambda h, s, qt, kt: (qt[s], h)),
                      pl.BlockSpec((t, Dk), lambda h, s, qt, kt: (kt[s], h)),
                      pl.BlockSpec((t, Dv), lambda h, s, qt, kt: (kt[s], h))],
            out_specs=[pl.BlockSpec((t, Dv), lambda h, s, qt, kt: (qt[s], h)),
                       pl.BlockSpec((None, t, STAT_LANES), lambda h, s, qt, kt: (h, qt[s], 0))],
            scratch_shapes=[stat, stat, pltpu.VMEM((t, Dv), F32)]),
        out_shape=[jax.ShapeDtypeStruct((S, H * Dv), F32), jax.ShapeDtypeStruct((H, S, STAT_LANES), F32)],
        compiler_params=_params(2),
    )(qt, kt, q, k, v)


def attn_bwd(name, q, k, v, do, o, lse, t=ATTN_TILE):
    S, Dk = q.shape[0], HEAD_PAD
    H = q.shape[1] // Dk
    Dv = v.shape[1] // H
    qt, kt = _causal_pairs(S // t, by_key=True)
    tn_dims = (((0,), (0,)), ((), ()))

    def body(qt_ref, kt_ref, q_ref, k_ref, v_ref, do_ref, o_ref, lse_ref, dq_ref, dk_ref, dv_ref):
        step_id = pl.program_id(1)
        qi, ki = qt_ref[step_id], kt_ref[step_id]

        @pl.when(step_id == 0)
        def _():
            dq_ref[...] = jnp.zeros_like(dq_ref)

        def step(masked):
            s = _scores(q_ref[...], k_ref[...], masked)
            reps = (1, t // STAT_LANES)
            p = jnp.exp(s - jnp.tile(lse_ref[...], reps))
            dov = do_ref[...]
            delta = jnp.broadcast_to(jnp.sum(dov * o_ref[...], axis=-1, keepdims=True), (t, STAT_LANES))
            dob = dov.astype(BF16)
            dp = lax.dot_general(dob, v_ref[...].astype(BF16), (((1,), (1,)), ((), ())), preferred_element_type=F32)
            ds = (p * (dp - jnp.tile(delta, reps))).astype(BF16)
            pdv = lax.dot_general(p.astype(BF16), dob, tn_dims, preferred_element_type=F32)
            pdk = lax.dot_general(ds, q_ref[...], tn_dims, preferred_element_type=F32)
            rows = pl.ds(pl.multiple_of(qi * t, t), t)
            dq_ref[rows, :] += jnp.dot(ds, k_ref[...], preferred_element_type=F32)
            return pdk, pdv

        @pl.when(ki == qi)
        def _():
            dk_ref[...] = jnp.zeros_like(dk_ref)
            dv_ref[...] = jnp.zeros_like(dv_ref)

        def accumulate(masked):
            pdk, pdv = step(masked)
            dk_ref[...] += pdk
            dv_ref[...] += pdv

        @pl.when(ki == qi)
        def _():
            accumulate(True)

        @pl.when(ki < qi)
        def _():
            accumulate(False)

    qrow = lambda h, s, qt, kt: (qt[s], h)
    krow = lambda h, s, qt, kt: (kt[s], h)
    return pl.pallas_call(
        body, name=name,
        grid_spec=pltpu.PrefetchScalarGridSpec(
            num_scalar_prefetch=2, grid=(H, qt.shape[0]),
            in_specs=[pl.BlockSpec((t, Dk), qrow), pl.BlockSpec((t, Dk), krow), pl.BlockSpec((t, Dv), krow),
                      pl.BlockSpec((t, Dv), qrow), pl.BlockSpec((t, Dv), qrow),
                      pl.BlockSpec((None, t, STAT_LANES), lambda h, s, qt, kt: (h, qt[s], 0))],
            out_specs=[pl.BlockSpec((S, Dk), lambda h, s, qt, kt: (0, h)), pl.BlockSpec((t, Dk), krow),
                       pl.BlockSpec((t, Dv), krow)]),
        out_shape=[jax.ShapeDtypeStruct((S, H * Dk), F32), jax.ShapeDtypeStruct((S, H * Dk), F32),
                   jax.ShapeDtypeStruct((S, H * Dv), F32)],
        compiler_params=_params(2),
    )(qt, kt, q, k, v, do, o, lse)


def _cmul(ar, ai, br, bi):
    return ar * br - ai * bi, ar * bi + ai * br


def _scan_tile(x_r, x_i, ar_ref, ai_ref, cr_sc, ci_sc, *, reverse, first, states=None):
    tc, lanes = x_r.shape
    nblk, lb = tc // SUBLANES, SCAN_LANE_TILE
    with_da = states is not None
    if with_da:
        xr_all, xi_all, pr_all, pi_all, dar_all, dai_all, chunk = states

    @pl.when(first)
    def _():
        cr_sc[...] = jnp.zeros_like(cr_sc)
        ci_sc[...] = jnp.zeros_like(ci_sc)
        if with_da:
            dar_all[...] = jnp.zeros_like(dar_all)
            dai_all[...] = jnp.zeros_like(dai_all)

    row = lax.broadcasted_iota(jnp.int32, (SUBLANES, lb), 0)
    for l0 in range(0, lanes, lb):
        _scan_lanes(x_r.at[:, pl.ds(l0, lb)], x_i.at[:, pl.ds(l0, lb)], ar_ref[0:1, pl.ds(l0, lb)],
                    ai_ref[0:1, pl.ds(l0, lb)], cr_sc.at[:, pl.ds(l0, lb)], ci_sc.at[:, pl.ds(l0, lb)], row, reverse,
                    nblk, None if not with_da else tuple(r.at[:, pl.ds(l0, lb)] for r in states[:6]) + (chunk,))


def _scan_lanes(x_r, x_i, a1r, a1i, cr_sc, ci_sc, row, reverse, nblk, states):
    lb = x_r.shape[1]
    with_da = states is not None
    if with_da:
        xr_ref, xi_ref, pr_ref, pi_ref, dar_ref, dai_ref, chunk = states
    if reverse:
        a1i = -a1i
    a2r, a2i = _cmul(a1r, a1i, a1r, a1i)
    a4r, a4i = _cmul(a2r, a2i, a2r, a2i)
    pw_r, pw_i = jnp.zeros((SUBLANES, lb), F32), jnp.zeros((SUBLANES, lb), F32)
    cur_r, cur_i = a1r, a1i
    for e in range(SUBLANES):
        r_at = (SUBLANES - 1 - e) if reverse else e
        pw_r = jnp.where(row == r_at, cur_r, pw_r)
        pw_i = jnp.where(row == r_at, cur_i, pw_i)
        cur_r, cur_i = _cmul(cur_r, cur_i, a1r, a1i)
    steps = []
    for d, pr, pi in ((1, a1r, a1i), (2, a2r, a2i), (4, a4r, a4i)):
        keep = (row < SUBLANES - d) if reverse else (row >= d)
        steps.append((d, jnp.where(keep, pr, 0.0), jnp.where(keep, pi, 0.0)))

    def block(jb, carry):
        if with_da:
            cr, ci, acc_r, acc_i = carry
        else:
            cr, ci = carry
        idx = (nblk - 1 - jb) if reverse else jb
        r0 = pl.multiple_of(idx * SUBLANES, SUBLANES)
        xr = x_r[pl.ds(r0, SUBLANES), :]
        xi = x_i[pl.ds(r0, SUBLANES), :]
        for d, pr, pi in steps:
            shift = SUBLANES - d if reverse else d
            sr, si = pltpu.roll(xr, shift, 0), pltpu.roll(xi, shift, 0)
            xr, xi = xr + pr * sr - pi * si, xi + pr * si + pi * sr
        xr, xi = xr + pw_r * cr - pw_i * ci, xi + pw_r * ci + pw_i * cr
        x_r[pl.ds(r0, SUBLANES), :] = xr
        x_i[pl.ds(r0, SUBLANES), :] = xi
        edge = 0 if reverse else SUBLANES - 1
        cr, ci = xr[edge:edge + 1, :], xi[edge:edge + 1, :]
        if not with_da:
            return cr, ci
        fr = xr_ref[pl.ds(r0, SUBLANES), :]
        fi = xi_ref[pl.ds(r0, SUBLANES), :]
        rp = pl.multiple_of(jnp.maximum(idx - 1, 0) * SUBLANES, SUBLANES)
        inside = idx > 0
        before_r = jnp.where(inside, xr_ref[pl.ds(rp, SUBLANES), :], pr_ref[...])
        before_i = jnp.where(inside, xi_ref[pl.ds(rp, SUBLANES), :], pi_ref[...])
        live = jnp.where(jnp.logical_or(inside, chunk > 0), 1.0, 0.0)
        last_r = before_r[SUBLANES - 1:SUBLANES, :] * live
        last_i = before_i[SUBLANES - 1:SUBLANES, :] * live
        pvr = jnp.where(row == 0, last_r, pltpu.roll(fr, 1, 0))
        pvi = jnp.where(row == 0, last_i, pltpu.roll(fi, 1, 0))
        acc_r = acc_r + xr * pvr + xi * pvi
        acc_i = acc_i + xi * pvr - xr * pvi
        return cr, ci, acc_r, acc_i

    init = (cr_sc[...], ci_sc[...])
    if with_da:
        init = init + (jnp.zeros((SUBLANES, lb), F32), jnp.zeros((SUBLANES, lb), F32))
    fin = lax.fori_loop(0, nblk, block, init)
    cr_sc[...] = fin[0]
    ci_sc[...] = fin[1]
    if with_da:
        dar_ref[...] += fin[2]
        dai_ref[...] += fin[3]


SSM_BLOCKS = 4
BLOCK_CH = SSM_WIDTH // SSM_BLOCKS
PREP_LANES = SSM_LANES // SSM_BLOCKS


def ssm_fwd(name, u, bb_r, bb_i, cb_r, ncb_i, a_r8, a_i8):
    S = u.shape[0]
    tc = min(SCAN_TIME_TILE, S)

    def body(u_ref, bbr_ref, bbi_ref, cbr_ref, ncbi_ref, ar_ref, ai_ref, xr_ref, xi_ref, y_ref, cr_sc, ci_sc):
        ub = u_ref[...].astype(BF16)
        xr_ref[...] = jnp.dot(ub, bbr_ref[...], preferred_element_type=F32)
        xi_ref[...] = jnp.dot(ub, bbi_ref[...], preferred_element_type=F32)
        _scan_tile(xr_ref, xi_ref, ar_ref, ai_ref, cr_sc, ci_sc, reverse=False, first=pl.program_id(1) == 0)
        y_ref[...] = (lax.dot_general(xr_ref[...].astype(BF16), cbr_ref[...], NT_DIMS, preferred_element_type=F32)
                      + lax.dot_general(xi_ref[...].astype(BF16), ncbi_ref[...], NT_DIMS, preferred_element_type=F32))

    ch = pl.BlockSpec((tc, BLOCK_CH), lambda b, t: (t, b))
    st = pl.BlockSpec((tc, PREP_LANES), lambda b, t: (t, b))
    wt = pl.BlockSpec((None, BLOCK_CH, PREP_LANES), lambda b, t: (b, 0, 0))
    par = pl.BlockSpec((SUBLANES, PREP_LANES), lambda b, t: (0, b))
    return pl.pallas_call(
        body, name=name, grid=(SSM_BLOCKS, S // tc), in_specs=[ch, wt, wt, wt, wt, par, par], out_specs=[st, st, ch],
        out_shape=[jax.ShapeDtypeStruct((S, SSM_LANES), F32), jax.ShapeDtypeStruct((S, SSM_LANES), F32),
                   jax.ShapeDtypeStruct((S, SSM_WIDTH), F32)],
        scratch_shapes=[pltpu.VMEM((1, PREP_LANES), F32), pltpu.VMEM((1, PREP_LANES), F32)], compiler_params=_params(2),
    )(u, bb_r, bb_i, cb_r, ncb_i, a_r8, a_i8)


def ssm_bwd(name, dy, du_add, u, xs_r, xs_i, bb_r, bb_i, cb_r, ncb_i, a_r8, a_i8):
    S = u.shape[0]
    tc = min(SCAN_TIME_TILE, S)
    nt = S // tc

    def body(dy_ref, dua_ref, u_ref, xr_ref, xi_ref, pr_ref, pi_ref, bbr_ref, bbi_ref, cbr_ref, ncbi_ref, ar_ref, ai_ref,
             du_ref, dbbr_ref, dbbi_ref, dcbr_ref, dncbi_ref, dar_ref, dai_ref, lr_sc, li_sc, cr_sc, ci_sc):
        t = pl.program_id(1)
        first = t == 0

        @pl.when(first)
        def _():
            for r in (dbbr_ref, dbbi_ref, dcbr_ref, dncbi_ref):
                r[...] = jnp.zeros_like(r)

        dyb = dy_ref[...].astype(BF16)
        lr_sc[...] = jnp.dot(dyb, cbr_ref[...], preferred_element_type=F32)
        li_sc[...] = jnp.dot(dyb, ncbi_ref[...], preferred_element_type=F32)
        _scan_tile(lr_sc, li_sc, ar_ref, ai_ref, cr_sc, ci_sc, reverse=True, first=first,
                   states=(xr_ref, xi_ref, pr_ref, pi_ref, dar_ref, dai_ref, nt - 1 - t))
        lrb, lib = lr_sc[...].astype(BF16), li_sc[...].astype(BF16)
        du_ref[...] = (lax.dot_general(lrb, bbr_ref[...], NT_DIMS, preferred_element_type=F32)
                       + lax.dot_general(lib, bbi_ref[...], NT_DIMS, preferred_element_type=F32) + dua_ref[...])
        ub = u_ref[...].astype(BF16)
        dbbr_ref[...] += lax.dot_general(ub, lrb, TN_DIMS, preferred_element_type=F32)
        dbbi_ref[...] += lax.dot_general(ub, lib, TN_DIMS, preferred_element_type=F32)
        dcbr_ref[...] += lax.dot_general(dyb, xr_ref[...].astype(BF16), TN_DIMS, preferred_element_type=F32)
        dncbi_ref[...] += lax.dot_general(dyb, xi_ref[...].astype(BF16), TN_DIMS, preferred_element_type=F32)

    ch = pl.BlockSpec((tc, BLOCK_CH), lambda b, t: (nt - 1 - t, b))
    st = pl.BlockSpec((tc, PREP_LANES), lambda b, t: (nt - 1 - t, b))
    prev = pl.BlockSpec((SUBLANES, PREP_LANES), lambda b, t: (jnp.maximum((nt - 1 - t) * (tc // SUBLANES) - 1, 0), b))
    wt = pl.BlockSpec((None, BLOCK_CH, PREP_LANES), lambda b, t: (b, 0, 0))
    par = pl.BlockSpec((SUBLANES, PREP_LANES), lambda b, t: (0, b))
    blk = jax.ShapeDtypeStruct((SSM_BLOCKS, BLOCK_CH, PREP_LANES), F32)
    rows8 = jax.ShapeDtypeStruct((SUBLANES, SSM_LANES), F32)
    return pl.pallas_call(
        body, name=name, grid=(SSM_BLOCKS, nt), in_specs=[ch, ch, ch, st, st, prev, prev, wt, wt, wt, wt, par, par],
        out_specs=[ch, wt, wt, wt, wt, par, par],
        out_shape=[jax.ShapeDtypeStruct((S, SSM_WIDTH), F32), blk, blk, blk, blk, rows8, rows8],
        scratch_shapes=[pltpu.VMEM((tc, PREP_LANES), F32), pltpu.VMEM((tc, PREP_LANES), F32),
                        pltpu.VMEM((1, PREP_LANES), F32), pltpu.VMEM((1, PREP_LANES), F32)],
        compiler_params=_params(2),
    )(dy, du_add, u, xs_r, xs_i, xs_r, xs_i, bb_r, bb_i, cb_r, ncb_i, a_r8, a_i8)


def _ssm_prep_f(a_re, a_im, log_dt, bt_re, bt_im, c_re, c_im):
    first_group = pl.program_id(0) * (SSM_GROUPS // SSM_BLOCKS)
    iota = lambda shape, d: lax.broadcasted_iota(jnp.int32, shape, d)
    grp_of_row = lambda shape: iota(shape, 0) >> int(math.log2(SSM_GROUP))
    grp_of_lane = lambda shape: iota(shape, 1) >> int(math.log2(SSM_STATE))
    rep = (grp_of_row((BLOCK_CH, SSM_GROUPS)) + first_group == iota((BLOCK_CH, SSM_GROUPS), 1)).astype(F32)
    til = ((iota((SSM_STATE, PREP_LANES), 1) & (SSM_STATE - 1)) == iota((SSM_STATE, PREP_LANES), 0)).astype(F32)
    m_rows = (grp_of_row((BLOCK_CH, PREP_LANES)) == grp_of_lane((BLOCK_CH, PREP_LANES))).astype(F32)
    m_grp = (iota((SSM_GROUPS, PREP_LANES), 0) == grp_of_lane((SSM_GROUPS, PREP_LANES)) + first_group).astype(F32)
    dt = jnp.exp(log_dt)
    decay = jnp.exp(a_re * dt)
    ar = decay * jnp.cos(a_im * dt)
    ai = decay * jnp.sin(a_im * dt)
    den = a_re * a_re + a_im * a_im
    nr = ar - 1.0
    coef_r = (nr * a_re + ai * a_im) / den
    coef_i = (ai * a_re - nr * a_im) / den
    cr, ci = _hp_dot(rep, coef_r), _hp_dot(rep, coef_i)
    bb_r = cr * bt_re - ci * bt_im
    bb_i = cr * bt_im + ci * bt_re
    big = lambda m: _hp_dot(m, til) * m_rows
    lanes = lambda m: jnp.broadcast_to(jnp.sum(_hp_dot(m, til) * m_grp, axis=0, keepdims=True), (SUBLANES, PREP_LANES))
    return lanes(ar), lanes(ai), big(bb_r), big(bb_i), big(c_re), -big(c_im)


def _whole(arr, **kw):
    nd = arr.ndim
    return In(arr, arr.shape, lambda *_: (0,) * nd, **kw)


def _adamw_math(w, g, m, v):
    m = ADAM_B1 * m + (1.0 - ADAM_B1) * g
    v = ADAM_B2 * v + (1.0 - ADAM_B2) * (g * g)
    m_hat = m / (1.0 - ADAM_B1 ** ADAM_STEP)
    v_hat = v / (1.0 - ADAM_B2 ** ADAM_STEP)
    delta = -ADAM_LR * (m_hat / (jnp.sqrt(v_hat) + ADAM_EPS) + ADAM_WD * w)
    return delta, m, v


def adamw_big(name, w, g, m, v):
    R, C = w.shape
    tr = R
    for cand in (512, 344, 256, 128):
        if R % cand == 0:
            tr = cand
            break

    def body(w_ref, g_ref, m_ref, v_ref, d_ref, nm_ref, nv_ref):
        d, nm, nv = _adamw_math(w_ref[...], g_ref[...], m_ref[...], v_ref[...])
        d_ref[...] = d
        nm_ref[...] = nm
        nv_ref[...] = nv

    spec = pl.BlockSpec((tr, C), lambda i: (i, 0))
    return pl.pallas_call(
        body, name=name, grid=(R // tr,), in_specs=[spec] * 4, out_specs=[spec] * 3,
        out_shape=[jax.ShapeDtypeStruct((R, C), F32)] * 3, compiler_params=_params(1),
    )(w, g, m, v)


def adamw_small(name, ws, gs, ms, vs):
    n = len(ws)

    def body(*refs):
        for k in range(n):
            d, nm, nv = _adamw_math(refs[k][...], refs[n + k][...], refs[2 * n + k][...], refs[3 * n + k][...])
            refs[4 * n + k][...] = d
            refs[5 * n + k][...] = nm
            refs[6 * n + k][...] = nv

    vm = pl.BlockSpec(memory_space=pltpu.VMEM)
    shapes = [jax.ShapeDtypeStruct(w.shape, F32) for w in ws]
    res = pl.pallas_call(
        body, name=name, in_specs=[vm] * (4 * n), out_specs=[vm] * (3 * n), out_shape=shapes * 3,
        compiler_params=pltpu.CompilerParams(vmem_limit_bytes=VMEM_LIMIT_BYTES),
    )(*ws, *gs, *ms, *vs)
    return res[:n], res[n:2 * n], res[2 * n:]


def _place():
    return lax.axis_index("x"), lax.axis_index("y"), lax.axis_index("c")


def _other_chips(x, y):
    return [(1 - x, y), (x, 1 - y), (1 - x, 1 - y)]


HBM = pl.BlockSpec(memory_space=pl.ANY)


def all_gather_halves(name, blocks):
    n = len(blocks)

    def body(*refs):
        in_refs, out_refs = refs[:n], refs[n:2 * n]
        send_sems, recv_sems = refs[2 * n:]
        x, y, c = _place()
        me, sibling = (x, y, c), (x, y, 1 - c)
        chips = _other_chips(x, y)

        def slot(a, px, py, pc):
            return out_refs[a].at[4 * px + 2 * py + pc]

        def copy(a, k, block, to, src=None):
            return pltpu.make_async_remote_copy(
                src_ref=slot(a, *block) if src is None else src, dst_ref=slot(a, *block),
                send_sem=send_sems.at[a, k], recv_sem=recv_sems.at[a, k], device_id=to, device_id_type=MESH)

        first = []
        for a in range(n):
            first.append(copy(a, 0, me, sibling, src=in_refs[a]))
            first += [copy(a, 1 + j, me, (*chip, c), src=in_refs[a]) for j, chip in enumerate(chips)]
        for cp in first:
            cp.start()
        passed = []
        for j, chip in enumerate(chips):
            for a in range(n):
                copy(a, 1 + j, (*chip, c), me).wait_recv()
                fw = copy(a, 4 + j, (*chip, c), sibling)
                fw.start()
                passed.append(fw)
        for a in range(n):
            copy(a, 0, sibling, me).wait_recv()
            for j, chip in enumerate(chips):
                copy(a, 4 + j, (*chip, 1 - c), me).wait_recv()
        for cp in first + passed:
            cp.wait_send()

    res = pl.pallas_call(
        body, name=name, in_specs=[HBM] * n, out_specs=[HBM] * n,
        out_shape=[jax.ShapeDtypeStruct((N_DEV,) + b.shape, b.dtype) for b in blocks],
        scratch_shapes=[pltpu.SemaphoreType.DMA((n, 7)), pltpu.SemaphoreType.DMA((n, 7))],
    )(*blocks)
    return list(res)


def pair_send_halves(name, grads):
    n = len(grads)

    def body(*refs):
        in_refs, out_refs = refs[:n], refs[n:2 * n]
        send_sems, recv_sems = refs[2 * n:]
        x, y, c = _place()
        cps = []
        for a in range(n):
            cp = pltpu.make_async_remote_copy(
                src_ref=in_refs[a].at[:, 1 - c], dst_ref=out_refs[a], send_sem=send_sems.at[a], recv_sem=recv_sems.at[a],
                device_id=(x, y, 1 - c), device_id_type=MESH)
            cp.start()
            cps.append(cp)
        for cp in cps:
            cp.wait()

    res = pl.pallas_call(
        body, name=name, in_specs=[HBM] * n, out_specs=[HBM] * n,
        out_shape=[jax.ShapeDtypeStruct((g.shape[0],) + g.shape[2:], g.dtype) for g in grads],
        scratch_shapes=[pltpu.SemaphoreType.DMA((n,)), pltpu.SemaphoreType.DMA((n,))],
    )(*grads)
    return list(res)


def pair_add(name, grad, got, c_arr):
    nsh, _, M, N = grad.shape
    tr = M
    for cand in (512, 256, 192, 128, 64, 16):
        if M % cand == 0:
            tr = cand
            break

    def body(c_ref, g_ref, p_ref, o_ref):
        o_ref[...] = (g_ref[...] + p_ref[...]).astype(BF16)

    return pl.pallas_call(
        body, name=name,
        grid_spec=pltpu.PrefetchScalarGridSpec(
            num_scalar_prefetch=1, grid=(nsh, M // tr),
            in_specs=[pl.BlockSpec((None, None, tr, N), lambda j, i, c_ref: (j, c_ref[0], i, 0)),
                      pl.BlockSpec((None, tr, N), lambda j, i, c_ref: (j, i, 0))],
            out_specs=pl.BlockSpec((None, tr, N), lambda j, i, c_ref: (j, i, 0))),
        out_shape=jax.ShapeDtypeStruct((nsh, M, N), BF16), compiler_params=_params(2),
    )(c_arr, grad, got)


def scatter_to_chips(name, parts):
    n = len(parts)

    def body(*refs):
        in_refs, out_refs = refs[:n], refs[n:2 * n]
        send_sems, recv_sems = refs[2 * n:]
        x, y, c = _place()
        mine = 2 * x + y
        chips = _other_chips(x, y)
        cps = []
        for a in range(n):
            for k, (px, py) in enumerate(chips):
                cp = pltpu.make_async_remote_copy(
                    src_ref=in_refs[a].at[2 * px + py], dst_ref=out_refs[a].at[mine],
                    send_sem=send_sems.at[a, k], recv_sem=recv_sems.at[a, k], device_id=(px, py, c), device_id_type=MESH)
                cp.start()
                cps.append((cp, a, k, px, py))
        for cp, a, k, px, py in cps:
            pltpu.make_async_remote_copy(
                src_ref=in_refs[a].at[mine], dst_ref=out_refs[a].at[2 * px + py],
                send_sem=send_sems.at[a, k], recv_sem=recv_sems.at[a, k], device_id=(px, py, c), device_id_type=MESH).wait_recv()
        for cp, *_ in cps:
            cp.wait_send()

    res = pl.pallas_call(
        body, name=name, in_specs=[HBM] * n, out_specs=[HBM] * n,
        out_shape=[jax.ShapeDtypeStruct(p.shape, p.dtype) for p in parts],
        scratch_shapes=[pltpu.SemaphoreType.DMA((n, 3)), pltpu.SemaphoreType.DMA((n, 3))],
    )(*parts)
    return list(res)


def sum_chips(name, q):
    nsh, M, N = q.shape
    tr = M
    for cand in (512, 256, 192, 128, 64, 16):
        if M % cand == 0:
            tr = cand
            break

    def body(q_ref, o_ref):
        acc = q_ref[0].astype(F32)
        for j in range(1, nsh):
            acc = acc + q_ref[j].astype(F32)
        o_ref[...] = acc

    return pl.pallas_call(
        body, name=name, grid=(M // tr,), in_specs=[pl.BlockSpec((nsh, tr, N), lambda i: (0, i, 0))],
        out_specs=pl.BlockSpec((tr, N), lambda i: (i, 0)), out_shape=jax.ShapeDtypeStruct((M, N), F32),
        compiler_params=_params(1),
    )(q)


def pair_exchange(name, halves):
    n = len(halves)

    def body(*refs):
        in_refs, out_refs = refs[:n], refs[n:2 * n]
        send_sems, recv_sems = refs[2 * n:]
        x, y, c = _place()
        cps = []
        for a in range(n):
            cp = pltpu.make_async_remote_copy(
                src_ref=in_refs[a], dst_ref=out_refs[a], send_sem=send_sems.at[a], recv_sem=recv_sems.at[a],
                device_id=(x, y, 1 - c), device_id_type=MESH)
            cp.start()
            cps.append(cp)
        for cp in cps:
            cp.wait()

    res = pl.pallas_call(
        body, name=name, in_specs=[HBM] * n, out_specs=[HBM] * n,
        out_shape=[jax.ShapeDtypeStruct(h.shape, h.dtype) for h in halves],
        scratch_shapes=[pltpu.SemaphoreType.DMA((n,)), pltpu.SemaphoreType.DMA((n,))],
    )(*halves)
    return list(res)


SEM = pl.BlockSpec(memory_space=pltpu.SEMAPHORE)
IN_HBM = pl.BlockSpec(memory_space=pltpu.HBM)
SPLIT_COPY = pltpu.CompilerParams(has_side_effects=pltpu.SideEffectType.DATAFLOW_SIDE_EFFECTING)


def _scatter_copies(src_refs, dst_refs, send_sems, recv_sems):
    x, y, c = _place()
    mine = 2 * x + y
    return [pltpu.make_async_remote_copy(
        src_ref=src_refs[a].at[2 * px + py], dst_ref=dst_refs[a].at[mine], send_sem=send_sems.at[a * (N_CHIPS - 1) + k],
        recv_sem=recv_sems.at[a * (N_CHIPS - 1) + k], device_id=(px, py, c), device_id_type=MESH)
        for a in range(len(src_refs)) for k, (px, py) in enumerate(_other_chips(x, y))]


def _gather_copies(src_refs, dst_refs, send_sems, recv_sems):
    x, y, c = _place()
    me = 4 * x + 2 * y + c
    cps = []
    for a in range(len(src_refs)):
        for k in range(1, N_DEV):
            to = (1 - x if k & 4 else x, 1 - y if k & 2 else y, 1 - c if k & 1 else c)
            s = a * (N_DEV - 1) + k - 1
            cps.append(pltpu.make_async_remote_copy(
                src_ref=src_refs[a], dst_ref=dst_refs[a].at[me], send_sem=send_sems.at[s], recv_sem=recv_sems.at[s],
                device_id=to, device_id_type=MESH))
    return cps


def split_copy_start(name, copies, n_sem, srcs, land_shapes):
    n = len(srcs)
    lands = [lax.empty(s.shape, s.dtype) for s in land_shapes]

    def body(*refs):
        for cp in copies(refs[:n], refs[n:2 * n], refs[2 * n], refs[2 * n + 1]):
            cp.start()
        refs[-1][...] = jnp.zeros_like(refs[-1])

    thru = [pltpu.HBM(a.shape, a.dtype) for a in (*srcs, *lands)]
    res = pl.pallas_call(
        body, name=name, in_specs=[IN_HBM] * (2 * n),
        out_specs=(SEM, SEM, *[IN_HBM] * (2 * n), pl.BlockSpec(memory_space=pltpu.VMEM)),
        out_shape=(pltpu.SemaphoreType.DMA((n * n_sem,)), pltpu.SemaphoreType.DMA((n * n_sem,)), *thru,
                   jax.ShapeDtypeStruct((SUBLANES, LANES), F32)),
        input_output_aliases={i: 2 + i for i in range(2 * n)}, compiler_params=SPLIT_COPY,
    )(*[pltpu.with_memory_space_constraint(a, pltpu.HBM) for a in (*srcs, *lands)])
    return (copies, n, res[0], res[1], res[2:2 + 2 * n]), res[-1][0, 0]


def split_copy_wait(name, handle, after):
    copies, n, send_sems, recv_sems, thru = handle

    def body(*refs):
        for cp in copies(refs[:n], refs[n:2 * n], refs[2 * n], refs[2 * n + 1]):
            cp.wait_send()
            cp.wait_recv()

    res = pl.pallas_call(
        body, name=name, in_specs=[IN_HBM] * (2 * n) + [SEM, SEM, pl.BlockSpec(memory_space=pl.ANY)],
        out_specs=[IN_HBM] * (2 * n), out_shape=[pltpu.HBM(a.shape, a.dtype) for a in thru],
        input_output_aliases={i: i for i in range(2 * n)}, compiler_params=SPLIT_COPY,
    )(*thru, send_sems, recv_sems, after)
    return list(res[n:])


def all_reduce_small(name, v):
    R, C = v.shape

    def body(v_ref, o_ref, gath, send_sems, recv_sems):
        x, y, c = _place()
        me, sibling = (x, y, c), (x, y, 1 - c)
        chips = _other_chips(x, y)

        def slot(px, py, pc):
            return gath.at[4 * px + 2 * py + pc]

        def copy(k, block, to, src=None):
            return pltpu.make_async_remote_copy(
                src_ref=slot(*block) if src is None else src, dst_ref=slot(*block),
                send_sem=send_sems.at[k], recv_sem=recv_sems.at[k], device_id=to, device_id_type=MESH)

        first = [copy(0, me, sibling, src=v_ref)]
        first += [copy(1 + j, me, (*chip, c), src=v_ref) for j, chip in enumerate(chips)]
        for cp in first:
            cp.start()
        slot(*me)[...] = v_ref[...]
        passed = [copy(4 + j, (*chip, c), sibling) for j, chip in enumerate(chips)]
        for j, chip in enumerate(chips):
            copy(1 + j, (*chip, c), me).wait_recv()
            passed[j].start()
        copy(0, sibling, me).wait_recv()
        for j, chip in enumerate(chips):
            copy(4 + j, (*chip, 1 - c), me).wait_recv()
        for cp in first + passed:
            cp.wait_send()
        acc = gath[0]
        for d in range(1, N_DEV):
            acc = acc + gath[d]
        o_ref[...] = acc

    vm = pl.BlockSpec(memory_space=pltpu.VMEM)
    return pl.pallas_call(
        body, name=name, in_specs=[vm], out_specs=vm, out_shape=jax.ShapeDtypeStruct((R, C), F32),
        scratch_shapes=[pltpu.VMEM((N_DEV, R, C), F32), pltpu.SemaphoreType.DMA((7,)), pltpu.SemaphoreType.DMA((7,))],
        compiler_params=pltpu.CompilerParams(vmem_limit_bytes=VMEM_LIMIT_BYTES),
    )(v)


LANES = 128
PACK_ROW_MULTIPLE = 1024
SMALL_SHARDED = {
    "w_in": ((D_MODEL, 1216), 1), "mla_w_uq": ((MLA_Q_RANK, 768), 1), "mla_w_ukv": ((MLA_KV_RANK, 1024), 1),
    "ssm_w_glu": ((SSM_WIDTH, SSM_WIDTH), 0), "w_o": ((D_MODEL, D_MODEL), 0), "xattn_w_q": ((D_MODEL, 512), 0),
    "xattn_w_kv": ((D_MODEL, 1024), 0), "xattn_w_o": ((512, D_MODEL), 1),
}
FFN_NAMES = ["ffn1_w_gate", "ffn1_w_up", "ffn1_w_down", "ffn2_w_gate", "ffn2_w_up", "ffn2_w_down"]
TRANSPOSED_VIEW = ("ffn1_w_gate", "ffn1_w_up", "ffn2_w_gate", "ffn2_w_up", "w_in", "mla_w_uq")


def _shard_shape(name):
    (r, cdim), ax = SMALL_SHARDED[name]
    return (r // N_CHIPS, cdim) if ax == 0 else (r, cdim // N_CHIPS)


def _pack_shards(shards):
    parts = []
    for name in SMALL_SHARDED:
        a = shards[name]
        lead = a.shape[:-2]
        parts.append(a.reshape(lead + (a.shape[-2] * a.shape[-1] // LANES, LANES)))
    rows = sum(q.shape[-2] for q in parts)
    parts.append(jnp.zeros(lead + (-rows % PACK_ROW_MULTIPLE, LANES), parts[0].dtype))
    return jnp.concatenate(parts, axis=-2)


def _unpack_shards(packed):
    out, r0 = {}, 0
    lead = packed.shape[:-2]
    for name in SMALL_SHARDED:
        r, cdim = _shard_shape(name)
        rows = r * cdim // LANES
        out[name] = packed[..., r0:r0 + rows, :].reshape(lead + (r, cdim))
        r0 += rows
    return out


def _full_from_shards(name, sh):
    (r, cdim), ax = SMALL_SHARDED[name]
    if ax == 0:
        return sh.reshape(r, cdim)
    return jnp.transpose(sh, (1, 0, 2)).reshape(r, cdim)


def _shards_from_full(name, full):
    (r, cdim), ax = SMALL_SHARDED[name]
    if ax == 0:
        return full.reshape(N_CHIPS, r // N_CHIPS, cdim)
    return jnp.transpose(full.reshape(r, N_CHIPS, cdim // N_CHIPS), (1, 0, 2))


SMALL_REPL = {
    "ffn1_norm": (1, 1024), "mix_norm": (1, 1024), "mla_q_norm": (1, 384), "mla_kv_norm": (1, 256),
    "mla_qk_norm_q": (1, 192), "mla_qk_norm_k": (1, 192), "ssm_a_re": (32, 64), "ssm_a_im": (32, 64),
    "ssm_log_dt": (32, 1), "ssm_b_re": (32, 64, 16), "ssm_b_im": (32, 64, 16), "ssm_c_re": (32, 16, 64),
    "ssm_c_im": (32, 16, 64), "ssm_d": (1, 512), "ssm_b_glu": (1, 512), "out_norm_mla": (1, 512),
    "out_norm_ssm": (1, 512), "xattn_norm": (1, 1024), "mem_norm": (1, 1024), "xattn_q_norm": (1, 128),
    "xattn_k_norm": (1, 128), "ffn2_norm": (1, 1024),
}


def _pack_repl(grads):
    flat = jnp.concatenate([grads[n].reshape(-1) for n in SMALL_REPL])
    rows = -(-flat.shape[0] // (LANES * SUBLANES)) * SUBLANES
    return jnp.pad(flat, (0, rows * LANES - flat.shape[0])).reshape(rows, LANES)


def _unpack_repl(packed):
    flat, out, o = packed.reshape(-1), {}, 0
    for n, shp in SMALL_REPL.items():
        size = int(np.prod(shp))
        out[n] = flat[o:o + size].reshape(shp)
        o += size
    return out


def _rope_tables(positions):
    half = MLA_ROPE // 2
    inv = ROPE_THETA ** (-jnp.arange(half, dtype=F32) / half)
    ang = positions.astype(F32)[:, None] * inv[None, :]
    cos, sin = jnp.cos(ang), jnp.sin(ang)
    S = positions.shape[0]
    z = lambda w: jnp.zeros((S, w), F32)
    keep = jnp.concatenate([jnp.ones((S, MLA_NOPE), F32), cos, cos, z(HEAD_PAD - MLA_QK)], axis=1)
    from_hi = jnp.concatenate([z(MLA_NOPE), -sin, z(HEAD_PAD - MLA_NOPE - half)], axis=1)
    from_lo = jnp.concatenate([z(MLA_NOPE + half), sin, z(HEAD_PAD - MLA_QK)], axis=1)
    return keep, from_hi, from_lo


def _norm_rope(x, g, keep, from_hi, from_lo):
    y = x * lax.rsqrt(jnp.sum(x * x, axis=-1, keepdims=True) * (1.0 / MLA_QK) + EPS) * g
    half = MLA_ROPE // 2
    return y * keep + _lane_roll(y, HEAD_PAD - half) * from_hi + _lane_roll(y, half) * from_lo


def local_step(x, mem, positions, target, w, wb, small_weights=None, ffn2_weights=None, on_grads=None):
    if small_weights is None:
        small_weights = lambda after: {}
    if ffn2_weights is None:
        ffn2_weights = lambda after: [wb[k] for k in FFN_NAMES[3:]]
    if on_grads is None:
        on_grads = lambda tag, g: 0.0
    S = x.shape[0]
    tm = min(FUSED_TILE, S)
    g1 = (S // tm,)
    tile = lambda arr, **kw: In(arr, (tm, arr.shape[1]), lambda i: (i, 0), **kw)
    par = lambda arr, **kw: In(arr, arr.shape, lambda *_: (0, 0), diff=True, acc=(0,), **kw)
    wt = lambda arr: In(arr, weight=True)
    otile = lambda cols, dt: Out((S, cols), dt, (tm, cols), lambda i: (i, 0))
    grads = {}

    x1, h1, g_1, u_1 = ffn_fwd("ffn1_fwd", x, w["ffn1_norm"], wb["ffn1_w_gate"], wb["ffn1_w_up"], wb["ffn1_w_down"])
    wb = {**wb, **small_weights(x1)}

    keep, from_hi, from_lo = _rope_tables(positions)
    scale = MLA_QK ** -0.5

    def f_pre(xv, kp, fh, fl, g_mix, w_q, w_kv, w_kr, w_u, g_q, g_kv, w_uq, w_ukv, gq, gk):
        h2 = _rms(xv, g_mix)
        cq, ckv = _rms(wdot(h2, w_q), g_q), _rms(wdot(h2, w_kv), g_kv)
        kr = wdot(h2, w_kr)
        qs, ks, vs = [], [], []
        for h in range(MLA_HEADS):
            qs.append(_norm_rope(wdot(cq, w_uq, h), gq, kp, fh, fl) * scale)
            kv = wdot(ckv, w_ukv, h)
            ks.append(_norm_rope(jnp.concatenate([kv[:, :MLA_NOPE], kr], axis=-1), gk, kp, fh, fl))
            vs.append(kv[:, MLA_NOPE:])
        return jnp.concatenate(qs, axis=-1), jnp.concatenate(ks, axis=-1), jnp.concatenate(vs, axis=-1), wdot(h2, w_u)

    def pre_ins(gain_mix):
        return [tile(x1, diff=True), tile(keep), tile(from_hi), tile(from_lo), par(gain_mix),
                wt(wb["w_in_q"]), wt(wb["w_in_kv"]), wt(wb["w_in_kr"]), wt(wb["w_in_u"]),
                par(w["mla_q_norm"]), par(w["mla_kv_norm"]), wt(wb["w_uq"]), wt(wb["w_ukv"]),
                par(w["qk_gain_q"]), par(w["qk_gain_k"])]

    pre_outs = [otile(MLA_HEADS * HEAD_PAD, BF16), otile(MLA_HEADS * HEAD_PAD, BF16), otile(MLA_HEADS * MLA_V, BF16),
                otile(SSM_WIDTH, F32)]
    qh, kh, v0, pu = seg_fwd("pre_mixer_fwd", f_pre, g1, pre_ins(w["mix_norm"]), pre_outs)

    o_mla, lse = attn_fwd("mla_attn_fwd", qh, kh, v0, t=min(ATTN_TILE, S))

    prep_grid = (SSM_BLOCKS,)
    prep_ins = ([_whole(w[k], diff=True, acc=(0,)) for k in ("ssm_a_re", "ssm_a_im", "ssm_log_dt")]
                + [In(w[k], (BLOCK_CH, SSM_STATE), lambda i: (i, 0), diff=True)
                   for k in ("ssm_bt_re", "ssm_bt_im", "ssm_c2_re", "ssm_c2_im")])
    blk_out = Out((SSM_BLOCKS, BLOCK_CH, PREP_LANES), BF16, (None, BLOCK_CH, PREP_LANES), lambda i: (i, 0, 0))
    prep_outs = [Out((SUBLANES, SSM_LANES), F32, (SUBLANES, PREP_LANES), lambda i: (0, i))] * 2 + [blk_out] * 4
    a_r8, a_i8, bb_r, bb_i, cb_r, ncb_i = seg_fwd("ssm_prep_fwd", _ssm_prep_f, prep_grid, prep_ins, prep_outs)

    xs_r, xs_i, y_lin = ssm_fwd("ssm_fwd", pu, bb_r, bb_i, cb_r, ncb_i, a_r8, a_i8)

    M = mem.shape[0]
    f_norm = lambda xv, g: (_rms(xv, g),)
    mem_ins = [In(mem, (M, D_MODEL), lambda i: (0, 0)), par(w["mem_norm"])]
    mem_outs = [Out((M, D_MODEL), BF16, (M, D_MODEL), lambda i: (0, 0))]
    (mn,) = seg_fwd("mem_norm_fwd", f_norm, (1,), mem_ins, mem_outs)
    (kvm,) = mm("xattn_kv_fwd", [mn], [[wb["xattn_w_kv"]]], [F32])

    xscale = XHD ** -0.5

    def f_post(o, yl, u, xv, kv, d, w_glu, b, gm, gs, w_o1, w_o2, gx, w_xq, gq, gk, w_xo):
        gl = jax.nn.gelu(yl + d * u)
        so = gl * jax.nn.sigmoid(wdot(gl, w_glu) + b)
        x2 = xv + wdot(_rms(o, gm), w_o1) + wdot(_rms(so, gs), w_o2)
        h3 = _rms(x2, gx)
        heads = []
        for h in range(XH):
            qn = _rms(wdot(h3, w_xq, h), gq)
            kn = _rms(kv[:, h * XHD:(h + 1) * XHD], gk)
            p = jax.nn.softmax(_bdot_nt(qn, kn) * xscale, axis=-1)
            heads.append(_bdot_nn(p, kv[:, (XH + h) * XHD:(XH + h + 1) * XHD]))
        return (x2 + wdot(jnp.concatenate(heads, axis=-1), w_xo),)

    def post_ins(gain_d):
        return [tile(o_mla, diff=True), tile(y_lin, diff=True), tile(pu, diff=True), tile(x1, diff=True),
                In(kvm, kvm.shape, lambda i: (0, 0), diff=True, acc=(0,)), par(gain_d), wt(wb["ssm_w_glu"]),
                par(w["ssm_b_glu"]), par(w["out_norm_mla"]), par(w["out_norm_ssm"]), wt(wb["w_o_mla"]), wt(wb["w_o_ssm"]),
                par(w["xattn_norm"]), wt(wb["xattn_w_q"]), par(w["xattn_q_norm"]), par(w["xattn_k_norm"]),
                wt(wb["xattn_w_o"])]

    post_outs = [otile(D_MODEL, F32)]
    (x3,) = seg_fwd("post_mixer_fwd", f_post, g1, post_ins(w["ssm_d"]), post_outs)

    wg2, wu2, wd2 = ffn2_weights(x3)
    dx4, h4, g_2, u_2, parts = ffn_fwd("ffn2_fwd_loss", x3, w["ffn2_norm"], wg2, wu2, wd2, target=target)
    loss = jnp.sum(parts[::SUBLANES, 0])

    dx3, grads["ffn2_norm"], dg_2, du_2 = ffn_bwd_act("ffn2_bwd_act", dx4, x3, w["ffn2_norm"], g_2, u_2, wg2, wu2, wd2)
    grads["ffn2_w_gate"], grads["ffn2_w_up"], grads["ffn2_w_down"] = ffn_bwd_w("ffn2_bwd_w", h4, dx4, g_2, u_2, dg_2, du_2)
    sent = on_grads("ffn2", grads)

    (do_mla, dy_lin, du_a, dx1_a, dkvm, grads["ssm_d"], grads["ssm_w_glu"], grads["ssm_b_glu"], grads["out_norm_mla"],
     grads["out_norm_ssm"], grads["w_o_mla"], grads["w_o_ssm"], grads["xattn_norm"], grads["xattn_w_q"],
     grads["xattn_q_norm"], grads["xattn_k_norm"], grads["xattn_w_o"]) = seg_bwd(
        "post_mixer_bwd", f_post, g1, post_ins(w["ssm_d"] + sent), post_outs, [dx3])

    (dmn,) = mm("xattn_kv_bwd", [dkvm], [[wb["xattn_w_kv"]]], [F32], trans=True)
    (grads["xattn_w_kv"],) = mm_tn("xattn_kv_bwd_w", [mn], [dkvm], [(0, [0])])
    (grads["mem_norm"],) = seg_bwd("mem_norm_bwd", f_norm, (1,), mem_ins, mem_outs, [dmn])

    du, dbb_r, dbb_i, dcb_r, dncb_i, da_r8, da_i8 = ssm_bwd("ssm_bwd", dy_lin, du_a, pu, xs_r, xs_i, bb_r, bb_i, cb_r, ncb_i,
                                                            a_r8, a_i8)
    prep_g = seg_bwd("ssm_prep_bwd", _ssm_prep_f, prep_grid, prep_ins, prep_outs, [da_r8, da_i8, dbb_r, dbb_i, dcb_r, dncb_i])
    for k, gname in enumerate(("ssm_a_re", "ssm_a_im", "ssm_log_dt", "ssm_bt_re", "ssm_bt_im", "ssm_c2_re", "ssm_c2_im")):
        grads[gname] = prep_g[k]

    dqh, dkh, dv0 = attn_bwd("mla_attn_bwd", qh, kh, v0, do_mla, o_mla, lse, t=min(ATTN_TILE, S))

    (dx1, grads["mix_norm"], grads["w_in_q"], grads["w_in_kv"], grads["w_in_kr"], grads["w_in_u"], grads["mla_q_norm"],
     grads["mla_kv_norm"], grads["w_uq"], grads["w_ukv"], grads["qk_gain_q"], grads["qk_gain_k"]) = seg_bwd(
        "pre_mixer_bwd", f_pre, g1, pre_ins(w["mix_norm"]), pre_outs, [dqh, dkh, dv0, du], adds={0: dx1_a})

    sent = on_grads("small", grads)
    dx, grads["ffn1_norm"], dg_1, du_1 = ffn_bwd_act("ffn1_bwd_act", dx1, x, w["ffn1_norm"] + sent, g_1, u_1,
                                                     wb["ffn1_w_gate"], wb["ffn1_w_up"], wb["ffn1_w_down"])
    grads["ffn1_w_gate"], grads["ffn1_w_up"], grads["ffn1_w_down"] = ffn_bwd_w("ffn1_bwd_w", h1, dx1, g_1, u_1, dg_1, du_1)
    return loss, dx, grads


def _pad_cols(a, n):
    return jnp.pad(a, ((0, 0), (0, n - a.shape[1])))


def _step_weights(shards):
    wb = {}
    w_in = _full_from_shards("w_in", shards["w_in"])
    wb["w_in_q"] = w_in[:, :MLA_Q_RANK]
    wb["w_in_kv"] = w_in[:, MLA_Q_RANK:MLA_Q_RANK + MLA_KV_RANK]
    wb["w_in_kr"] = _pad_cols(w_in[:, MLA_Q_RANK + MLA_KV_RANK:MLA_Q_RANK + MLA_KV_RANK + MLA_ROPE], LANES)
    wb["w_in_u"] = w_in[:, MLA_Q_RANK + MLA_KV_RANK + MLA_ROPE:]
    wb["w_uq"] = jnp.pad(shards["mla_w_uq"], ((0, 0), (0, 0), (0, HEAD_PAD - MLA_QK)))
    wb["w_ukv"] = shards["mla_w_ukv"]
    wb["ssm_w_glu"] = _full_from_shards("ssm_w_glu", shards["ssm_w_glu"])
    w_o = _full_from_shards("w_o", shards["w_o"])
    wb["w_o_mla"], wb["w_o_ssm"] = w_o[:SSM_WIDTH], w_o[SSM_WIDTH:]
    w_xq = _full_from_shards("xattn_w_q", shards["xattn_w_q"])
    wb["xattn_w_q"] = jnp.transpose(w_xq.reshape(D_MODEL, XH, XHD), (1, 0, 2))
    wb["xattn_w_kv"] = _full_from_shards("xattn_w_kv", shards["xattn_w_kv"])
    wb["xattn_w_o"] = _full_from_shards("xattn_w_o", shards["xattn_w_o"])
    return wb


def _sharded_grads(g):
    out = {}
    kr = g["w_in_kr"][:, :MLA_ROPE]
    out["w_in"] = _shards_from_full("w_in", jnp.concatenate([g["w_in_q"], g["w_in_kv"], kr, g["w_in_u"]], axis=1))
    out["mla_w_uq"] = g["w_uq"][:, :, :MLA_QK]
    out["mla_w_ukv"] = g["w_ukv"]
    out["ssm_w_glu"] = _shards_from_full("ssm_w_glu", g["ssm_w_glu"])
    out["w_o"] = _shards_from_full("w_o", jnp.concatenate([g["w_o_mla"], g["w_o_ssm"]], axis=0))
    w_xq = jnp.transpose(g["xattn_w_q"], (1, 0, 2)).reshape(D_MODEL, XH * XHD)
    out["xattn_w_q"] = _shards_from_full("xattn_w_q", w_xq)
    out["xattn_w_kv"] = _shards_from_full("xattn_w_kv", g["xattn_w_kv"])
    out["xattn_w_o"] = _shards_from_full("xattn_w_o", g["xattn_w_o"])
    return out


def _problem_repl_grads(g):
    out = {}
    out["mla_qk_norm_q"] = g["qk_gain_q"][:, :MLA_QK]
    out["mla_qk_norm_k"] = g["qk_gain_k"][:, :MLA_QK]
    out["ssm_b_re"] = jnp.transpose(g["ssm_bt_re"].reshape(SSM_GROUPS, SSM_GROUP, SSM_STATE), (0, 2, 1))
    out["ssm_b_im"] = jnp.transpose(g["ssm_bt_im"].reshape(SSM_GROUPS, SSM_GROUP, SSM_STATE), (0, 2, 1))
    out["ssm_c_re"] = g["ssm_c2_re"].reshape(SSM_GROUPS, SSM_GROUP, SSM_STATE)
    out["ssm_c_im"] = g["ssm_c2_im"].reshape(SSM_GROUPS, SSM_GROUP, SSM_STATE)
    for k in SMALL_REPL:
        if k not in out:
            out[k] = g[k]
    return out


def _problem_grads(g):
    out = {k: _full_from_shards(k, v) for k, v in _sharded_grads(g).items()}
    out.update(_problem_repl_grads(g))
    out.update({k: g[k] for k in FFN_NAMES})
    return out


def _step_params(p):
    row = lambda a: a.reshape(1, -1)
    w = {k: row(p[k]) for k in ("ffn1_norm", "mix_norm", "mla_q_norm", "mla_kv_norm", "ssm_b_glu", "out_norm_mla",
                                "out_norm_ssm", "xattn_norm", "mem_norm", "xattn_q_norm", "xattn_k_norm", "ffn2_norm")}
    w["qk_gain_q"] = _pad_cols(row(p["mla_qk_norm_q"]), HEAD_PAD)
    w["qk_gain_k"] = _pad_cols(row(p["mla_qk_norm_k"]), HEAD_PAD)
    w["ssm_a_re"], w["ssm_a_im"] = p["ssm_a_re"], p["ssm_a_im"]
    w["ssm_log_dt"] = p["ssm_log_dt"].reshape(SSM_GROUPS, 1)
    w["ssm_bt_re"] = jnp.transpose(p["ssm_b_re"], (0, 2, 1)).reshape(SSM_WIDTH, SSM_STATE)
    w["ssm_bt_im"] = jnp.transpose(p["ssm_b_im"], (0, 2, 1)).reshape(SSM_WIDTH, SSM_STATE)
    w["ssm_c2_re"] = p["ssm_c_re"].reshape(SSM_WIDTH, SSM_STATE)
    w["ssm_c2_im"] = p["ssm_c_im"].reshape(SSM_WIDTH, SSM_STATE)
    w["ssm_d"] = p["ssm_d"].reshape(1, SSM_WIDTH)
    return w


ARG_NAMES = ['x', 'mem', 'positions', 'ffn1_norm', 'ffn1_w_gate', 'ffn1_w_up', 'ffn1_w_down', 'mix_norm', 'w_in', 'mla_q_norm', 'mla_w_uq', 'mla_kv_norm', 'mla_w_ukv', 'mla_qk_norm_q', 'mla_qk_norm_k', 'ssm_a_re', 'ssm_a_im', 'ssm_log_dt', 'ssm_b_re', 'ssm_b_im', 'ssm_c_re', 'ssm_c_im', 'ssm_d', 'ssm_w_glu', 'ssm_b_glu', 'out_norm_mla', 'out_norm_ssm', 'w_o', 'xattn_norm', 'mem_norm', 'xattn_w_q', 'xattn_w_kv', 'xattn_q_norm', 'xattn_k_norm', 'xattn_w_o', 'ffn2_norm', 'ffn2_w_gate', 'ffn2_w_up', 'ffn2_w_down']
WEIGHT_NAMES = ARG_NAMES[3:]


def _gather_weights(p, c):
    half = lambda a: lax.dynamic_slice_in_dim(a, c * (a.shape[0] // 2), a.shape[0] // 2, axis=0)
    ffn1 = [half(p[k].astype(BF16)) for k in FFN_NAMES[:3]]
    small = [half(_pack_shards({k: p[k].astype(BF16) for k in SMALL_SHARDED}))]
    ffn2 = [half(p[k].astype(BF16)) for k in FFN_NAMES[3:]]
    me = 4 * lax.axis_index("x") + 2 * lax.axis_index("y") + c
    own = lambda got, blocks: [lax.dynamic_update_index_in_dim(g, b, me, 0) for g, b in zip(got, blocks)]
    as_shards = lambda a: a.reshape(N_CHIPS, 2 * a.shape[1], a.shape[2])
    landing = lambda blocks: [jax.ShapeDtypeStruct((N_DEV,) + b.shape, b.dtype) for b in blocks]
    got1 = own(all_gather_halves("all_gather_weights_a", ffn1), ffn1)
    got1, small = lax.optimization_barrier((got1, small))
    flight_s, sent_s = split_copy_start("gather_small_start", _gather_copies, N_DEV - 1, small, landing(small))
    sent_s, ffn2 = lax.optimization_barrier((sent_s, ffn2))
    flight_2, sent_2 = split_copy_start("gather_ffn2_start", _gather_copies, N_DEV - 1, ffn2, landing(ffn2))
    wb = {k: as_shards(a) for k, a in zip(FFN_NAMES[:3], got1)}

    def small_weights(after):
        (got,) = own(split_copy_wait("gather_small_wait", flight_s, after), small)
        return _step_weights(_unpack_shards(got.reshape(N_CHIPS, -1, LANES)))

    def ffn2_weights(after):
        return [as_shards(a) for a in own(split_copy_wait("gather_ffn2_wait", flight_2, after), ffn2)]

    return wb, small_weights, ffn2_weights, sent_s + sent_2


class _GradReduce:
    def __init__(self, c):
        self.c, self.c_arr = c, jnp.reshape(c, (1,)).astype(jnp.int32)
        self.chip = 2 * lax.axis_index("x") + lax.axis_index("y")
        self.flights = []

    def start(self, tag, arrs):
        split = [a.reshape(N_CHIPS, 2, a.shape[1] // 2, a.shape[2]) for a in arrs]
        got = pair_send_halves(f"grad_pair_send_{tag}", split)
        parts = [pair_add(f"grad_pair_add_{tag}_{k}", s, g, self.c_arr) for k, (s, g) in enumerate(zip(split, got))]
        flight, sent = split_copy_start(f"grad_scatter_start_{tag}", _scatter_copies, N_CHIPS - 1, parts, parts)
        self.flights.append((tag, parts, flight))
        return sent

    def finish(self, after):
        halves = []
        for tag, parts, flight in self.flights:
            landed = split_copy_wait(f"grad_scatter_wait_{tag}", flight, after)
            for k, (q, p) in enumerate(zip(landed, parts)):
                mine = lax.dynamic_index_in_dim(p, self.chip, 0, keepdims=False)
                halves.append(sum_chips(f"grad_sum_{tag}_{k}", lax.dynamic_update_index_in_dim(q, mine, self.chip, 0)))
        tags = "_".join(t for t, _, _ in self.flights)
        self.flights = []
        theirs = pair_exchange(f"grad_pair_exchange_{tags}", halves)
        return [jnp.where(self.c == 0, jnp.concatenate([h, t], axis=0), jnp.concatenate([t, h], axis=0))
                for h, t in zip(halves, theirs)]


def kernel(x, mem, positions, ffn1_norm, ffn1_w_gate, ffn1_w_up, ffn1_w_down, mix_norm, w_in, mla_q_norm, mla_w_uq, mla_kv_norm, mla_w_ukv, mla_qk_norm_q, mla_qk_norm_k, ssm_a_re, ssm_a_im, ssm_log_dt, ssm_b_re, ssm_b_im, ssm_c_re, ssm_c_im, ssm_d, ssm_w_glu, ssm_b_glu, out_norm_mla, out_norm_ssm, w_o, xattn_norm, mem_norm, xattn_w_q, xattn_w_kv, xattn_q_norm, xattn_k_norm, xattn_w_o, ffn2_norm, ffn2_w_gate, ffn2_w_up, ffn2_w_down, loss_target, m_ffn1_norm, m_ffn1_w_gate, m_ffn1_w_up, m_ffn1_w_down, m_mix_norm, m_w_in, m_mla_q_norm, m_mla_w_uq, m_mla_kv_norm, m_mla_w_ukv, m_mla_qk_norm_q, m_mla_qk_norm_k, m_ssm_a_re, m_ssm_a_im, m_ssm_log_dt, m_ssm_b_re, m_ssm_b_im, m_ssm_c_re, m_ssm_c_im, m_ssm_d, m_ssm_w_glu, m_ssm_b_glu, m_out_norm_mla, m_out_norm_ssm, m_w_o, m_xattn_norm, m_mem_norm, m_xattn_w_q, m_xattn_w_kv, m_xattn_q_norm, m_xattn_k_norm, m_xattn_w_o, m_ffn2_norm, m_ffn2_w_gate, m_ffn2_w_up, m_ffn2_w_down, v_ffn1_norm, v_ffn1_w_gate, v_ffn1_w_up, v_ffn1_w_down, v_mix_norm, v_w_in, v_mla_q_norm, v_mla_w_uq, v_mla_kv_norm, v_mla_w_ukv, v_mla_qk_norm_q, v_mla_qk_norm_k, v_ssm_a_re, v_ssm_a_im, v_ssm_log_dt, v_ssm_b_re, v_ssm_b_im, v_ssm_c_re, v_ssm_c_im, v_ssm_d, v_ssm_w_glu, v_ssm_b_glu, v_out_norm_mla, v_out_norm_ssm, v_w_o, v_xattn_norm, v_mem_norm, v_xattn_w_q, v_xattn_w_kv, v_xattn_q_norm, v_xattn_k_norm, v_xattn_w_o, v_ffn2_norm, v_ffn2_w_gate, v_ffn2_w_up, v_ffn2_w_down):
    args = dict(locals())
    c = lax.axis_index("c")
    view = lambda k, a: jnp.swapaxes(a, 0, 1) if k in TRANSPOSED_VIEW else a
    p = {k: view(k, args[k][0]) for k in WEIGHT_NAMES}
    mom = {k: view(k, args["m_" + k][0]) for k in WEIGHT_NAMES}
    var = {k: view(k, args["v_" + k][0]) for k in WEIGHT_NAMES}
    natural = {k: view(k, p[k]) for k in WEIGHT_NAMES}

    wb, small_weights, ffn2_weights, sent = _gather_weights(
        {k: (p[k] if k in FFN_NAMES else natural[k]) for k in WEIGHT_NAMES}, c)
    w = _step_params(natural)
    w["ffn1_norm"] = w["ffn1_norm"] + sent
    early, late = _GradReduce(c), _GradReduce(c)

    def on_grads(tag, g):
        if tag == "ffn2":
            return early.start(tag, [g[k] for k in FFN_NAMES[3:]])
        return early.start(tag, [_pack_shards(_sharded_grads(g))])

    loss, dx, g = local_step(x[0], mem[0], positions[0], loss_target[0], w, wb, small_weights, ffn2_weights, on_grads)
    loss = lax.psum(loss, ("x", "y", "c"))

    sent = late.start("ffn1", [g[k] for k in FFN_NAMES[:3]])
    shards = early.finish(dx[:SUBLANES, :LANES] + sent)
    grad = dict(zip(FFN_NAMES[3:], shards[:3]))
    small_sharded = _unpack_shards(shards[3])
    grad.update({k: view(k, small_sharded[k]) for k in SMALL_SHARDED})
    repl = _problem_repl_grads(g)
    grad.update(_unpack_repl(all_reduce_small("grad_all_reduce_small", _pack_repl({k: repl[k] for k in SMALL_REPL}))))

    delta, new_m, new_v = {}, {}, {}
    small = [k for k in WEIGHT_NAMES if k not in FFN_NAMES and k not in SMALL_SHARDED]
    as2d = lambda a: a.reshape(-1, a.shape[-1])

    def update(k):
        delta[k], new_m[k], new_v[k] = adamw_big("adamw_" + k, as2d(p[k]), as2d(grad[k]), as2d(mom[k]), as2d(var[k]))

    for k in WEIGHT_NAMES:
        if k not in small and k not in FFN_NAMES[:3]:
            update(k)
    ds, nms, nvs = adamw_small("adamw_small", [as2d(p[k]) for k in small], [as2d(grad[k].reshape(p[k].shape)) for k in small],
                               [as2d(mom[k]) for k in small], [as2d(var[k]) for k in small])
    for k, d, nm, nv in zip(small, ds, nms, nvs):
        delta[k], new_m[k], new_v[k] = d, nm, nv

    grad.update(zip(FFN_NAMES[:3], late.finish(delta[FFN_NAMES[-1]])))
    for k in FFN_NAMES[:3]:
        update(k)

    shaped = lambda d, k: view(k, d.reshape(p[k].shape)).reshape(args[k].shape)
    return (loss, dx[None], *[shaped(grad[k], k) for k in WEIGHT_NAMES], *[shaped(delta[k], k) for k in WEIGHT_NAMES],
            *[shaped(new_m[k], k) for k in WEIGHT_NAMES], *[shaped(new_v[k], k) for k in WEIGHT_NAMES])
```

```python
import functools
import math

import jax
import jax.numpy as jnp
import numpy as np
from jax import lax
from jax.experimental import pallas as pl
from jax.experimental.pallas import tpu as pltpu

F32, BF16 = jnp.float32, jnp.bfloat16
EPS = 1e-6
MESH = pl.DeviceIdType.MESH

D_MODEL, D_FF = 1024, 2752
MLA_HEADS, MLA_Q_RANK, MLA_KV_RANK, MLA_NOPE, MLA_ROPE, MLA_V = 4, 384, 256, 128, 64, 128
MLA_QK = MLA_NOPE + MLA_ROPE
HEAD_PAD = 256
SSM_WIDTH, SSM_GROUP, SSM_GROUPS, SSM_STATE = 512, 16, 32, 64
SSM_LANES = SSM_GROUPS * SSM_STATE
XH, XHD = 4, 128
ROPE_THETA = 10000.0
ADAM_LR, ADAM_B1, ADAM_B2, ADAM_EPS, ADAM_WD, ADAM_STEP = 0.001, 0.9, 0.999, 1e-08, 0.01, 10
N_CHIPS, N_CORES, N_DEV = 4, 2, 8

VMEM_LIMIT_BYTES = 56 * 2**20
TOKEN_TILE = 512
FUSED_TILE = 256
FFN_WIDE_TILE = 1024
ATTN_TILE = 512
STAT_LANES = 128
SCAN_TIME_TILE = 1024
SCAN_LANE_TILE = 256
SUBLANES = 8


def _params(n_axes):
    return pltpu.CompilerParams(dimension_semantics=("arbitrary",) * n_axes, vmem_limit_bytes=VMEM_LIMIT_BYTES)


def _first(axes):
    cond = None
    for a in axes:
        c = pl.program_id(a) == 0
        cond = c if cond is None else jnp.logical_and(cond, c)
    return cond


def mm(name, xs, ws, out_dtypes, *, trans=False, adds=None, tm=TOKEN_TILE):
    rows = xs[0].shape[0]
    tm = min(tm, rows)
    n_in, n_out = len(xs), len(out_dtypes)
    pairs = [(i, j) for i in range(n_in) for j in range(n_out) if ws[i][j] is not None]
    w_list = [ws[i][j] for (i, j) in pairs]
    adds = list(adds) if adds is not None else [None] * n_out
    add_list = [a for a in adds if a is not None]
    out_cols = [None] * n_out
    for (i, j), w in zip(pairs, w_list):
        out_cols[j] = w.shape[0] if trans else w.shape[1]
    contract = (((1,), (1 if trans else 0,)), ((), ()))

    def body(*refs):
        x_refs = refs[:n_in]
        w_refs = refs[n_in:n_in + len(pairs)]
        a_refs = list(refs[n_in + len(pairs):n_in + len(pairs) + len(add_list)])
        o_refs = refs[n_in + len(pairs) + len(add_list):]
        xb = [None] * n_in
        for j in range(n_out):
            acc = None
            for p, (i, jj) in enumerate(pairs):
                if jj != j:
                    continue
                if xb[i] is None:
                    xb[i] = x_refs[i][...].astype(BF16)
                d = lax.dot_general(xb[i], w_refs[p][...].astype(BF16), contract, preferred_element_type=F32)
                acc = d if acc is None else acc + d
            if adds[j] is not None:
                acc = acc + a_refs.pop(0)[...].astype(F32)
            o_refs[j][...] = acc.astype(o_refs[j].dtype)

    in_specs = ([pl.BlockSpec((tm, x.shape[1]), lambda i: (i, 0)) for x in xs]
                + [pl.BlockSpec(w.shape, lambda i: (0, 0)) for w in w_list]
                + [pl.BlockSpec((tm, a.shape[1]), lambda i: (i, 0)) for a in add_list])
    outs = pl.pallas_call(
        body, name=name, grid=(rows // tm,), in_specs=in_specs,
        out_specs=[pl.BlockSpec((tm, n), lambda i: (i, 0)) for n in out_cols],
        out_shape=[jax.ShapeDtypeStruct((rows, n), dt) for n, dt in zip(out_cols, out_dtypes)],
        compiler_params=_params(1),
    )(*xs, *w_list, *add_list)
    return list(outs)


def mm_tn(name, xs, dys, pairs, *, tm=TOKEN_TILE):
    rows = xs[0].shape[0]
    tm = min(tm, rows)
    n_x, n_dy = len(xs), len(dys)
    contract = (((0,), (0,)), ((), ()))

    def body(*refs):
        x_refs, dy_refs, o_refs = refs[:n_x], refs[n_x:n_x + n_dy], refs[n_x + n_dy:]
        @pl.when(pl.program_id(0) == 0)
        def _():
            for o in o_refs:
                o[...] = jnp.zeros_like(o)

        for k, (i, js) in enumerate(pairs):
            dy = None
            for j in js:
                t = dy_refs[j][...].astype(F32)
                dy = t if dy is None else dy + t
            o_refs[k][...] += lax.dot_general(x_refs[i][...].astype(BF16), dy.astype(BF16), contract,
                                              preferred_element_type=F32)

    shapes = [(xs[i].shape[1], dys[js[0]].shape[1]) for (i, js) in pairs]
    outs = pl.pallas_call(
        body, name=name, grid=(rows // tm,),
        in_specs=[pl.BlockSpec((tm, a.shape[1]), lambda i: (i, 0)) for a in (*xs, *dys)],
        out_specs=[pl.BlockSpec(s, lambda i: (0, 0)) for s in shapes],
        out_shape=[jax.ShapeDtypeStruct(s, F32) for s in shapes],
        compiler_params=_params(1),
    )(*xs, *dys)
    return list(outs)


class In:
    def __init__(self, arr, block=None, imap=None, *, diff=False, acc=None, grad=None, weight=False):
        self.arr, self.block, self.imap, self.diff, self.acc, self.grad = arr, block, imap, diff, acc, grad
        self.weight = weight

    def spec(self):
        return pl.BlockSpec(memory_space=pltpu.VMEM) if self.weight else pl.BlockSpec(self.block, self.imap)


class Out:
    def __init__(self, shape, dtype, block, imap):
        self.shape, self.dtype, self.block, self.imap = shape, dtype, block, imap

    def spec(self):
        return pl.BlockSpec(self.block, self.imap)


class Wt:
    def __init__(self, ref, zeros=None):
        self.ref, self.zeros = ref, zeros


@jax.custom_vjp
def _wdot(a, w, z):
    return jnp.dot(a.astype(BF16), w, preferred_element_type=F32)


def _wdot_fwd(a, w, z):
    return _wdot(a, w, z), (a, w)


def _wdot_bwd(res, g):
    a, w = res
    gb = g.astype(BF16)
    da = lax.dot_general(gb, w, (((1,), (1,)), ((), ())), preferred_element_type=F32)
    dz = lax.dot_general(a.astype(BF16), gb, (((0,), (0,)), ((), ())), preferred_element_type=F32)
    return da, None, dz


_wdot.defvjp(_wdot_fwd, _wdot_bwd)


def wdot(a, wt, head=None):
    w = wt.ref[...] if head is None else wt.ref[head]
    if wt.zeros is None:
        return jnp.dot(a.astype(BF16), w, preferred_element_type=F32)
    return _wdot(a, w, wt.zeros[0 if head is None else head])


def seg_fwd(name, f, grid, ins, outs):
    n_in = len(ins)

    def body(*refs):
        res = f(*[Wt(r) if i.weight else r[...] for i, r in zip(ins, refs[:n_in])])
        for o_ref, r in zip(refs[n_in:], res):
            o_ref[...] = r.astype(o_ref.dtype)

    res = pl.pallas_call(
        body, name=name, grid=grid, in_specs=[i.spec() for i in ins], out_specs=[o.spec() for o in outs],
        out_shape=[jax.ShapeDtypeStruct(o.shape, o.dtype) for o in outs], compiler_params=_params(len(grid)),
    )(*[i.arr for i in ins])
    return list(res)


def seg_bwd(name, f, grid, ins, outs, cts, adds=None):
    n_in, n_ct = len(ins), len(cts)
    grad_idx = [k for k, i in enumerate(ins) if i.diff or i.weight]
    adds = adds or {}
    add_keys = sorted(adds)
    add_list = [adds[k] for k in add_keys]
    heads = {k: (ins[k].arr.shape[0] if ins[k].arr.ndim == 3 else 1) for k in grad_idx if ins[k].weight}

    def body(*refs):
        in_refs, ct_refs = refs[:n_in], refs[n_in:n_in + n_ct]
        add_refs = dict(zip(add_keys, refs[n_in + n_ct:n_in + n_ct + len(add_list)]))
        g_refs = dict(zip(grad_idx, refs[n_in + n_ct + len(add_list):]))
        vals = [None if i.weight else r[...] for i, r in zip(ins, in_refs)]
        primals, owner = [], []
        for k in grad_idx:
            if ins[k].weight:
                for h in range(heads[k]):
                    primals.append(jnp.zeros(ins[k].arr.shape[-2:], F32))
                    owner.append((k, h))
            else:
                primals.append(vals[k].astype(F32))
                owner.append((k, None))

        def g(*dv):
            full = list(vals)
            zeros = {}
            for (k, h), v in zip(owner, dv):
                if h is None:
                    full[k] = v
                else:
                    zeros.setdefault(k, []).append(v)
            for k, z in zeros.items():
                full[k] = Wt(in_refs[k], z)
            return tuple(f(*full))

        _, pull = jax.vjp(g, *primals)
        grads = pull(tuple(c[...].astype(F32) for c in ct_refs))
        for k in grad_idx:
            if ins[k].weight or ins[k].acc is not None:
                @pl.when(_first(range(len(grid)) if ins[k].weight else ins[k].acc))
                def _(k=k):
                    g_refs[k][...] = jnp.zeros_like(g_refs[k])

        for (k, h), gr in zip(owner, grads):
            if ins[k].weight:
                if ins[k].arr.ndim == 3:
                    g_refs[k][h] += gr
                else:
                    g_refs[k][...] += gr
                continue
            if k in add_refs:
                gr = gr + add_refs[k][...].astype(F32)
            if ins[k].acc is None:
                g_refs[k][...] = gr.astype(g_refs[k].dtype)
            else:
                g_refs[k][...] += gr

    g_specs, g_shapes = [], []
    for k in grad_idx:
        i = ins[k]
        if i.weight:
            g_specs.append(pl.BlockSpec(memory_space=pltpu.VMEM))
            g_shapes.append(jax.ShapeDtypeStruct(i.arr.shape, F32))
            continue
        shape, block, imap = i.grad if i.grad is not None else (i.arr.shape, i.block, i.imap)
        g_specs.append(pl.BlockSpec(block, imap))
        g_shapes.append(jax.ShapeDtypeStruct(shape, F32))
    in_specs = ([i.spec() for i in ins] + [o.spec() for o in outs]
                + [pl.BlockSpec(ins[k].block, ins[k].imap) for k in add_keys])
    res = pl.pallas_call(
        body, name=name, grid=grid, in_specs=in_specs, out_specs=g_specs, out_shape=g_shapes,
        compiler_params=_params(len(grid)),
    )(*[i.arr for i in ins], *cts, *add_list)
    return list(res)


def _rms(x, g):
    return x * lax.rsqrt(jnp.mean(x * x, axis=-1, keepdims=True) + EPS) * g


@jax.custom_vjp
def _bdot_nt(a, b):
    return lax.dot_general(a.astype(BF16), b.astype(BF16), (((1,), (1,)), ((), ())), preferred_element_type=F32)


def _bdot_nt_fwd(a, b):
    return _bdot_nt(a, b), (a, b)


def _bdot_nt_bwd(res, g):
    a, b = res
    gb = g.astype(BF16)
    da = lax.dot_general(gb, b.astype(BF16), (((1,), (0,)), ((), ())), preferred_element_type=F32)
    db = lax.dot_general(gb, a.astype(BF16), (((0,), (0,)), ((), ())), preferred_element_type=F32)
    return da, db


_bdot_nt.defvjp(_bdot_nt_fwd, _bdot_nt_bwd)


@jax.custom_vjp
def _bdot_nn(a, b):
    return lax.dot_general(a.astype(BF16), b.astype(BF16), (((1,), (0,)), ((), ())), preferred_element_type=F32)


def _bdot_nn_fwd(a, b):
    return _bdot_nn(a, b), (a, b)


def _bdot_nn_bwd(res, g):
    a, b = res
    gb = g.astype(BF16)
    da = lax.dot_general(gb, b.astype(BF16), (((1,), (1,)), ((), ())), preferred_element_type=F32)
    db = lax.dot_general(a.astype(BF16), gb, (((0,), (0,)), ((), ())), preferred_element_type=F32)
    return da, db


_bdot_nn.defvjp(_bdot_nn_fwd, _bdot_nn_bwd)


@functools.partial(jax.custom_vjp, nondiff_argnums=(1,))
def _lane_roll(x, shift):
    return pltpu.roll(x, shift, 1)


def _lane_roll_fwd(x, shift):
    return pltpu.roll(x, shift, 1), None


def _lane_roll_bwd(shift, _, g):
    return (pltpu.roll(g, (g.shape[1] - shift) % g.shape[1], 1),)


_lane_roll.defvjp(_lane_roll_fwd, _lane_roll_bwd)


def _hp_dot(a, b):
    return jnp.dot(a, b, precision=lax.Precision.HIGHEST, preferred_element_type=F32)


NT_DIMS = (((1,), (1,)), ((), ()))
TN_DIMS = (((0,), (0,)), ((), ()))


def ffn_fwd(name, x, gain, wg, wu, wd, target=None, tm=FFN_WIDE_TILE):
    S, D = x.shape
    tm = min(tm, S)
    nsh, Fs, _ = wg.shape
    with_loss = target is not None

    def body(*refs):
        if with_loss:
            x_ref, gain_ref, wg_ref, wu_ref, wd_ref, t_ref, xo_ref, h_ref, g_ref, u_ref, part_ref, acc = refs
        else:
            x_ref, gain_ref, wg_ref, wu_ref, wd_ref, xo_ref, h_ref, g_ref, u_ref, acc = refs
        j = pl.program_id(1)

        @pl.when(j == 0)
        def _():
            h_ref[...] = _rms(x_ref[...], gain_ref[...]).astype(BF16)
            acc[...] = jnp.zeros_like(acc)

        h = h_ref[...]
        g = lax.dot_general(h, wg_ref[...], NT_DIMS, preferred_element_type=F32)
        u = lax.dot_general(h, wu_ref[...], NT_DIMS, preferred_element_type=F32)
        g_ref[...] = g.astype(BF16)
        u_ref[...] = u.astype(BF16)
        a = g * jax.nn.sigmoid(g) * u
        acc[...] += jnp.dot(a.astype(BF16), wd_ref[...], preferred_element_type=F32)

        @pl.when(j == nsh - 1)
        def _():
            y = x_ref[...] + 0.5 * acc[...]
            if with_loss:
                err = y - t_ref[...]
                xo_ref[...] = err * (1.0 / D)
                part_ref[...] = jnp.full(part_ref.shape, 0.5 * jnp.sum(jnp.mean(err * err, axis=-1)), F32)
            else:
                xo_ref[...] = y

    rows = pl.BlockSpec((tm, D), lambda i, j: (i, 0))
    wspec = pl.BlockSpec((None, Fs, D), lambda i, j: (j, 0, 0))
    act = pl.BlockSpec((None, tm, Fs), lambda i, j: (j, i, 0))
    in_specs, args = [rows, pl.BlockSpec((1, D), lambda i, j: (0, 0)), wspec, wspec, wspec], [x, gain, wg, wu, wd]
    out_specs = [rows, rows, act, act]
    out_shape = [jax.ShapeDtypeStruct((S, D), F32), jax.ShapeDtypeStruct((S, D), BF16),
                 jax.ShapeDtypeStruct((nsh, S, Fs), BF16), jax.ShapeDtypeStruct((nsh, S, Fs), BF16)]
    if with_loss:
        in_specs.append(rows)
        args.append(target)
        out_specs.append(pl.BlockSpec((SUBLANES, 128), lambda i, j: (i, 0)))
        out_shape.append(jax.ShapeDtypeStruct((S // tm * SUBLANES, 128), F32))
    return pl.pallas_call(
        body, name=name, grid=(S // tm, nsh), in_specs=in_specs, out_specs=out_specs, out_shape=out_shape,
        scratch_shapes=[pltpu.VMEM((tm, D), F32)], compiler_params=_params(2),
    )(*args)


def ffn_bwd_act(name, dxo, x, gain, g, u, wg, wu, wd, tm=FFN_WIDE_TILE):
    S, D = x.shape
    tm = min(tm, S)
    nsh, Fs, _ = wg.shape
    once = pl.Buffered(1)

    def body(dxo_ref, x_ref, gain_ref, g_ref, u_ref, wg_ref, wu_ref, wd_ref, dx_ref, dgain_ref, dg_ref, du_ref, dd, dh):
        i, j = pl.program_id(0), pl.program_id(1)

        @pl.when(j == 0)
        def _():
            dd[...] = (0.5 * dxo_ref[...]).astype(BF16)
            dh[...] = jnp.zeros_like(dh)

        da = lax.dot_general(dd[...], wd_ref[...], NT_DIMS, preferred_element_type=F32)
        gf, uf = g_ref[...].astype(F32), u_ref[...].astype(F32)
        sig = jax.nn.sigmoid(gf)
        dgv = (da * uf * (sig * (1.0 + gf * (1.0 - sig)))).astype(BF16)
        duv = (da * (gf * sig)).astype(BF16)
        dg_ref[...] = dgv
        du_ref[...] = duv
        dh[...] += (jnp.dot(dgv, wg_ref[...], preferred_element_type=F32)
                    + jnp.dot(duv, wu_ref[...], preferred_element_type=F32))

        @pl.when(j == nsh - 1)
        def _():
            xv = x_ref[...]
            r = lax.rsqrt(jnp.mean(xv * xv, axis=-1, keepdims=True) + EPS)
            xhat = xv * r
            dhv = dh[...]
            dxn = dhv * gain_ref[...]
            dx_ref[...] = dxo_ref[...] + r * (dxn - xhat * jnp.mean(dxn * xhat, axis=-1, keepdims=True))
            part = jnp.sum(dhv * xhat, axis=0, keepdims=True)

            @pl.when(i == 0)
            def _():
                dgain_ref[...] = part

            @pl.when(i != 0)
            def _():
                dgain_ref[...] += part

    return pl.pallas_call(
        body, name=name, grid=(S // tm, nsh),
        in_specs=[pl.BlockSpec((tm, D), lambda i, j: (i, 0), pipeline_mode=once),
                  pl.BlockSpec((tm, D), lambda i, j: (i, 0), pipeline_mode=once),
                  pl.BlockSpec((1, D), lambda i, j: (0, 0)),
                  pl.BlockSpec((None, tm, Fs), lambda i, j: (j, i, 0)), pl.BlockSpec((None, tm, Fs), lambda i, j: (j, i, 0)),
                  pl.BlockSpec((None, Fs, D), lambda i, j: (j, 0, 0)), pl.BlockSpec((None, Fs, D), lambda i, j: (j, 0, 0)),
                  pl.BlockSpec((None, Fs, D), lambda i, j: (j, 0, 0))],
        out_specs=[pl.BlockSpec((tm, D), lambda i, j: (i, 0)), pl.BlockSpec((1, D), lambda i, j: (0, 0)),
                   pl.BlockSpec((None, tm, Fs), lambda i, j: (j, i, 0)), pl.BlockSpec((None, tm, Fs), lambda i, j: (j, i, 0))],
        out_shape=[jax.ShapeDtypeStruct((S, D), F32), jax.ShapeDtypeStruct((1, D), F32),
                   jax.ShapeDtypeStruct((nsh, S, Fs), BF16), jax.ShapeDtypeStruct((nsh, S, Fs), BF16)],
        scratch_shapes=[pltpu.VMEM((tm, D), BF16), pltpu.VMEM((tm, D), F32)], compiler_params=_params(2),
    )(dxo, x, gain, g, u, wg, wu, wd)


def ffn_bwd_w(name, h, dxo, g, u, dg, du, tm=FFN_WIDE_TILE):
    S, D = h.shape
    tm = min(tm, S)
    nsh, _, Fs = g.shape

    def body(h_ref, dxo_ref, g_ref, u_ref, dg_ref, du_ref, dwg_ref, dwu_ref, dwd_ref):
        i = pl.program_id(1)
        gf, uf = g_ref[...].astype(F32), u_ref[...].astype(F32)
        a = (gf * jax.nn.sigmoid(gf) * uf).astype(BF16)
        dd = (0.5 * dxo_ref[...]).astype(BF16)
        hv = h_ref[...]

        @pl.when(i == 0)
        def _():
            dwg_ref[...] = jnp.zeros_like(dwg_ref)
            dwu_ref[...] = jnp.zeros_like(dwu_ref)
            dwd_ref[...] = jnp.zeros_like(dwd_ref)

        dwg_ref[...] += lax.dot_general(dg_ref[...], hv, TN_DIMS, preferred_element_type=F32)
        dwu_ref[...] += lax.dot_general(du_ref[...], hv, TN_DIMS, preferred_element_type=F32)
        dwd_ref[...] += lax.dot_general(a, dd, TN_DIMS, preferred_element_type=F32)

    act = pl.BlockSpec((None, tm, Fs), lambda j, i: (j, i, 0))
    wspec = pl.BlockSpec((None, Fs, D), lambda j, i: (j, 0, 0))
    return pl.pallas_call(
        body, name=name, grid=(nsh, S // tm),
        in_specs=[pl.BlockSpec((tm, D), lambda j, i: (i, 0)), pl.BlockSpec((tm, D), lambda j, i: (i, 0)), act, act, act, act],
        out_specs=[wspec, wspec, wspec], out_shape=[jax.ShapeDtypeStruct((nsh, Fs, D), F32)] * 3,
        compiler_params=_params(2),
    )(h, dxo, g, u, dg, du)


NEG_BIG = -1e30


def _causal_pairs(n, by_key):
    pairs = [(qi, ki) for qi in range(n) for ki in range(qi + 1)]
    if by_key:
        pairs.sort(key=lambda p: (p[1], p[0]))
    return jnp.asarray([p[0] for p in pairs], jnp.int32), jnp.asarray([p[1] for p in pairs], jnp.int32)


def _scores(q, k, masked):
    s = lax.dot_general(q, k, (((1,), (1,)), ((), ())), preferred_element_type=F32)
    if masked:
        row = lax.broadcasted_iota(jnp.int32, s.shape, 0)
        col = lax.broadcasted_iota(jnp.int32, s.shape, 1)
        s = jnp.where(row >= col, s, NEG_BIG)
    return s


def attn_fwd(name, q, k, v, t=ATTN_TILE):
    S, Dk = q.shape[0], HEAD_PAD
    H = q.shape[1] // Dk
    Dv = v.shape[1] // H
    qt, kt = _causal_pairs(S // t, by_key=False)

    def body(qt_ref, kt_ref, q_ref, k_ref, v_ref, o_ref, lse_ref, m_sc, l_sc, acc):
        qi, ki = qt_ref[pl.program_id(1)], kt_ref[pl.program_id(1)]

        @pl.when(ki == 0)
        def _():
            m_sc[...] = jnp.full_like(m_sc, NEG_BIG)
            l_sc[...] = jnp.zeros_like(l_sc)
            acc[...] = jnp.zeros_like(acc)

        def step(masked):
            s = _scores(q_ref[...], k_ref[...], masked)
            m_prev = m_sc[...]
            m_next = jnp.maximum(m_prev, jnp.max(s, axis=-1, keepdims=True))
            alpha = jnp.exp(m_prev - m_next)
            p = jnp.exp(s - jnp.tile(m_next, (1, t // STAT_LANES)))
            l_sc[...] = alpha * l_sc[...] + jnp.sum(p, axis=-1, keepdims=True)
            acc[...] = alpha * acc[...] + jnp.dot(p.astype(BF16), v_ref[...].astype(BF16), preferred_element_type=F32)
            m_sc[...] = m_next

        @pl.when(ki < qi)
        def _():
            step(False)

        @pl.when(ki == qi)
        def _():
            step(True)
            o_ref[...] = acc[...] / l_sc[...]
            lse_ref[...] = m_sc[...] + jnp.log(l_sc[...])

    stat = pltpu.VMEM((t, STAT_LANES), F32)
    return pl.pallas_call(
        body, name=name,
        grid_spec=pltpu.PrefetchScalarGridSpec(
            num_scalar_prefetch=2, grid=(H, qt.shape[0]),
            in_specs=[pl.BlockSpec((t, Dk), lambda h, s, qt, kt: (qt[s], h)),
                      pl.BlockSpec((t, Dk), lambda h, s, qt, kt: (kt[s], h)),
                      pl.BlockSpec((t, Dv), lambda h, s, qt, kt: (kt[s], h))],
            out_specs=[pl.BlockSpec((t, Dv), lambda h, s, qt, kt: (qt[s], h)),
                       pl.BlockSpec((None, t, STAT_LANES), lambda h, s, qt, kt: (h, qt[s], 0))],
            scratch_shapes=[stat, stat, pltpu.VMEM((t, Dv), F32)]),
        out_shape=[jax.ShapeDtypeStruct((S, H * Dv), F32), jax.ShapeDtypeStruct((H, S, STAT_LANES), F32)],
        compiler_params=_params(2),
    )(qt, kt, q, k, v)


def attn_bwd(name, q, k, v, do, o, lse, t=ATTN_TILE):
    S, Dk = q.shape[0], HEAD_PAD
    H = q.shape[1] // Dk
    Dv = v.shape[1] // H
    qt, kt = _causal_pairs(S // t, by_key=True)
    tn_dims = (((0,), (0,)), ((), ()))

    def body(qt_ref, kt_ref, q_ref, k_ref, v_ref, do_ref, o_ref, lse_ref, dq_ref, dk_ref, dv_ref):
        step_id = pl.program_id(1)
        qi, ki = qt_ref[step_id], kt_ref[step_id]

        @pl.when(step_id == 0)
        def _():
            dq_ref[...] = jnp.zeros_like(dq_ref)

        def step(masked):
            s = _scores(q_ref[...], k_ref[...], masked)
            reps = (1, t // STAT_LANES)
            p = jnp.exp(s - jnp.tile(lse_ref[...], reps))
            dov = do_ref[...]
            delta = jnp.broadcast_to(jnp.sum(dov * o_ref[...], axis=-1, keepdims=True), (t, STAT_LANES))
            dob = dov.astype(BF16)
            dp = lax.dot_general(dob, v_ref[...].astype(BF16), (((1,), (1,)), ((), ())), preferred_element_type=F32)
            ds = (p * (dp - jnp.tile(delta, reps))).astype(BF16)
            pdv = lax.dot_general(p.astype(BF16), dob, tn_dims, preferred_element_type=F32)
            pdk = lax.dot_general(ds, q_ref[...], tn_dims, preferred_element_type=F32)
            rows = pl.ds(pl.multiple_of(qi * t, t), t)
            dq_ref[rows, :] += jnp.dot(ds, k_ref[...], preferred_element_type=F32)
            return pdk, pdv

        @pl.when(ki == qi)
        def _():
            dk_ref[...] = jnp.zeros_like(dk_ref)
            dv_ref[...] = jnp.zeros_like(dv_ref)

        def accumulate(masked):
            pdk, pdv = step(masked)
            dk_ref[...] += pdk
            dv_ref[...] += pdv

        @pl.when(ki == qi)
        def _():
            accumulate(True)

        @pl.when(ki < qi)
        def _():
            accumulate(False)

    qrow = lambda h, s, qt, kt: (qt[s], h)
    krow = lambda h, s, qt, kt: (kt[s], h)
    return pl.pallas_call(
        body, name=name,
        grid_spec=pltpu.PrefetchScalarGridSpec(
            num_scalar_prefetch=2, grid=(H, qt.shape[0]),
            in_specs=[pl.BlockSpec((t, Dk), qrow), pl.BlockSpec((t, Dk), krow), pl.BlockSpec((t, Dv), krow),
                      pl.BlockSpec((t, Dv), qrow), pl.BlockSpec((t, Dv), qrow),
                      pl.BlockSpec((None, t, STAT_LANES), lambda h, s, qt, kt: (h, qt[s], 0))],
            out_specs=[pl.BlockSpec((S, Dk), lambda h, s, qt, kt: (0, h)), pl.BlockSpec((t, Dk), krow),
                       pl.BlockSpec((t, Dv), krow)]),
        out_shape=[jax.ShapeDtypeStruct((S, H * Dk), F32), jax.ShapeDtypeStruct((S, H * Dk), F32),
                   jax.ShapeDtypeStruct((S, H * Dv), F32)],
        compiler_params=_params(2),
    )(qt, kt, q, k, v, do, o, lse)


def _cmul(ar, ai, br, bi):
    return ar * br - ai * bi, ar * bi + ai * br


def _scan_tile(x_r, x_i, ar_ref, ai_ref, cr_sc, ci_sc, *, reverse, first, states=None):
    tc, lanes = x_r.shape
    nblk, lb = tc // SUBLANES, SCAN_LANE_TILE
    with_da = states is not None
    if with_da:
        xr_all, xi_all, pr_all, pi_all, dar_all, dai_all, chunk = states

    @pl.when(first)
    def _():
        cr_sc[...] = jnp.zeros_like(cr_sc)
        ci_sc[...] = jnp.zeros_like(ci_sc)
        if with_da:
            dar_all[...] = jnp.zeros_like(dar_all)
            dai_all[...] = jnp.zeros_like(dai_all)

    row = lax.broadcasted_iota(jnp.int32, (SUBLANES, lb), 0)
    for l0 in range(0, lanes, lb):
        _scan_lanes(x_r.at[:, pl.ds(l0, lb)], x_i.at[:, pl.ds(l0, lb)], ar_ref[0:1, pl.ds(l0, lb)],
                    ai_ref[0:1, pl.ds(l0, lb)], cr_sc.at[:, pl.ds(l0, lb)], ci_sc.at[:, pl.ds(l0, lb)], row, reverse,
                    nblk, None if not with_da else tuple(r.at[:, pl.ds(l0, lb)] for r in states[:6]) + (chunk,))


def _scan_lanes(x_r, x_i, a1r, a1i, cr_sc, ci_sc, row, reverse, nblk, states):
    lb = x_r.shape[1]
    with_da = states is not None
    if with_da:
        xr_ref, xi_ref, pr_ref, pi_ref, dar_ref, dai_ref, chunk = states
    if reverse:
        a1i = -a1i
    a2r, a2i = _cmul(a1r, a1i, a1r, a1i)
    a4r, a4i = _cmul(a2r, a2i, a2r, a2i)
    pw_r, pw_i = jnp.zeros((SUBLANES, lb), F32), jnp.zeros((SUBLANES, lb), F32)
    cur_r, cur_i = a1r, a1i
    for e in range(SUBLANES):
        r_at = (SUBLANES - 1 - e) if reverse else e
        pw_r = jnp.where(row == r_at, cur_r, pw_r)
        pw_i = jnp.where(row == r_at, cur_i, pw_i)
        cur_r, cur_i = _cmul(cur_r, cur_i, a1r, a1i)
    steps = []
    for d, pr, pi in ((1, a1r, a1i), (2, a2r, a2i), (4, a4r, a4i)):
        keep = (row < SUBLANES - d) if reverse else (row >= d)
        steps.append((d, jnp.where(keep, pr, 0.0), jnp.where(keep, pi, 0.0)))

    def block(jb, carry):
        if with_da:
            cr, ci, acc_r, acc_i = carry
        else:
            cr, ci = carry
        idx = (nblk - 1 - jb) if reverse else jb
        r0 = pl.multiple_of(idx * SUBLANES, SUBLANES)
        xr = x_r[pl.ds(r0, SUBLANES), :]
        xi = x_i[pl.ds(r0, SUBLANES), :]
        for d, pr, pi in steps:
            shift = SUBLANES - d if reverse else d
            sr, si = pltpu.roll(xr, shift, 0), pltpu.roll(xi, shift, 0)
            xr, xi = xr + pr * sr - pi * si, xi + pr * si + pi * sr
        xr, xi = xr + pw_r * cr - pw_i * ci, xi + pw_r * ci + pw_i * cr
        x_r[pl.ds(r0, SUBLANES), :] = xr
        x_i[pl.ds(r0, SUBLANES), :] = xi
        edge = 0 if reverse else SUBLANES - 1
        cr, ci = xr[edge:edge + 1, :], xi[edge:edge + 1, :]
        if not with_da:
            return cr, ci
        fr = xr_ref[pl.ds(r0, SUBLANES), :]
        fi = xi_ref[pl.ds(r0, SUBLANES), :]
        rp = pl.multiple_of(jnp.maximum(idx - 1, 0) * SUBLANES, SUBLANES)
        inside = idx > 0
        before_r = jnp.where(inside, xr_ref[pl.ds(rp, SUBLANES), :], pr_ref[...])
        before_i = jnp.where(inside, xi_ref[pl.ds(rp, SUBLANES), :], pi_ref[...])
        live = jnp.where(jnp.logical_or(inside, chunk > 0), 1.0, 0.0)
        last_r = before_r[SUBLANES - 1:SUBLANES, :] * live
        last_i = before_i[SUBLANES - 1:SUBLANES, :] * live
        pvr = jnp.where(row == 0, last_r, pltpu.roll(fr, 1, 0))
        pvi = jnp.where(row == 0, last_i, pltpu.roll(fi, 1, 0))
        acc_r = acc_r + xr * pvr + xi * pvi
        acc_i = acc_i + xi * pvr - xr * pvi
        return cr, ci, acc_r, acc_i

    init = (cr_sc[...], ci_sc[...])
    if with_da:
        init = init + (jnp.zeros((SUBLANES, lb), F32), jnp.zeros((SUBLANES, lb), F32))
    fin = lax.fori_loop(0, nblk, block, init)
    cr_sc[...] = fin[0]
    ci_sc[...] = fin[1]
    if with_da:
        dar_ref[...] += fin[2]
        dai_ref[...] += fin[3]


SSM_BLOCKS = 4
BLOCK_CH = SSM_WIDTH // SSM_BLOCKS
PREP_LANES = SSM_LANES // SSM_BLOCKS


def ssm_fwd(name, u, bb_r, bb_i, cb_r, ncb_i, a_r8, a_i8):
    S = u.shape[0]
    tc = min(SCAN_TIME_TILE, S)

    def body(u_ref, bbr_ref, bbi_ref, cbr_ref, ncbi_ref, ar_ref, ai_ref, xr_ref, xi_ref, y_ref, cr_sc, ci_sc):
        ub = u_ref[...].astype(BF16)
        xr_ref[...] = jnp.dot(ub, bbr_ref[...], preferred_element_type=F32)
        xi_ref[...] = jnp.dot(ub, bbi_ref[...], preferred_element_type=F32)
        _scan_tile(xr_ref, xi_ref, ar_ref, ai_ref, cr_sc, ci_sc, reverse=False, first=pl.program_id(1) == 0)
        y_ref[...] = (lax.dot_general(xr_ref[...].astype(BF16), cbr_ref[...], NT_DIMS, preferred_element_type=F32)
                      + lax.dot_general(xi_ref[...].astype(BF16), ncbi_ref[...], NT_DIMS, preferred_element_type=F32))

    ch = pl.BlockSpec((tc, BLOCK_CH), lambda b, t: (t, b))
    st = pl.BlockSpec((tc, PREP_LANES), lambda b, t: (t, b))
    wt = pl.BlockSpec((None, BLOCK_CH, PREP_LANES), lambda b, t: (b, 0, 0))
    par = pl.BlockSpec((SUBLANES, PREP_LANES), lambda b, t: (0, b))
    return pl.pallas_call(
        body, name=name, grid=(SSM_BLOCKS, S // tc), in_specs=[ch, wt, wt, wt, wt, par, par], out_specs=[st, st, ch],
        out_shape=[jax.ShapeDtypeStruct((S, SSM_LANES), F32), jax.ShapeDtypeStruct((S, SSM_LANES), F32),
                   jax.ShapeDtypeStruct((S, SSM_WIDTH), F32)],
        scratch_shapes=[pltpu.VMEM((1, PREP_LANES), F32), pltpu.VMEM((1, PREP_LANES), F32)], compiler_params=_params(2),
    )(u, bb_r, bb_i, cb_r, ncb_i, a_r8, a_i8)


def ssm_bwd(name, dy, du_add, u, xs_r, xs_i, bb_r, bb_i, cb_r, ncb_i, a_r8, a_i8):
    S = u.shape[0]
    tc = min(SCAN_TIME_TILE, S)
    nt = S // tc

    def body(dy_ref, dua_ref, u_ref, xr_ref, xi_ref, pr_ref, pi_ref, bbr_ref, bbi_ref, cbr_ref, ncbi_ref, ar_ref, ai_ref,
             du_ref, dbbr_ref, dbbi_ref, dcbr_ref, dncbi_ref, dar_ref, dai_ref, lr_sc, li_sc, cr_sc, ci_sc):
        t = pl.program_id(1)
        first = t == 0

        @pl.when(first)
        def _():
            for r in (dbbr_ref, dbbi_ref, dcbr_ref, dncbi_ref):
                r[...] = jnp.zeros_like(r)

        dyb = dy_ref[...].astype(BF16)
        lr_sc[...] = jnp.dot(dyb, cbr_ref[...], preferred_element_type=F32)
        li_sc[...] = jnp.dot(dyb, ncbi_ref[...], preferred_element_type=F32)
        _scan_tile(lr_sc, li_sc, ar_ref, ai_ref, cr_sc, ci_sc, reverse=True, first=first,
                   states=(xr_ref, xi_ref, pr_ref, pi_ref, dar_ref, dai_ref, nt - 1 - t))
        lrb, lib = lr_sc[...].astype(BF16), li_sc[...].astype(BF16)
        du_ref[...] = (lax.dot_general(lrb, bbr_ref[...], NT_DIMS, preferred_element_type=F32)
                       + lax.dot_general(lib, bbi_ref[...], NT_DIMS, preferred_element_type=F32) + dua_ref[...])
        ub = u_ref[...].astype(BF16)
        dbbr_ref[...] += lax.dot_general(ub, lrb, TN_DIMS, preferred_element_type=F32)
        dbbi_ref[...] += lax.dot_general(ub, lib, TN_DIMS, preferred_element_type=F32)
        dcbr_ref[...] += lax.dot_general(dyb, xr_ref[...].astype(BF16), TN_DIMS, preferred_element_type=F32)
        dncbi_ref[...] += lax.dot_general(dyb, xi_ref[...].astype(BF16), TN_DIMS, preferred_element_type=F32)

    ch = pl.BlockSpec((tc, BLOCK_CH), lambda b, t: (nt - 1 - t, b))
    st = pl.BlockSpec((tc, PREP_LANES), lambda b, t: (nt - 1 - t, b))
    prev = pl.BlockSpec((SUBLANES, PREP_LANES), lambda b, t: (jnp.maximum((nt - 1 - t) * (tc // SUBLANES) - 1, 0), b))
    wt = pl.BlockSpec((None, BLOCK_CH, PREP_LANES), lambda b, t: (b, 0, 0))
    par = pl.BlockSpec((SUBLANES, PREP_LANES), lambda b, t: (0, b))
    blk = jax.ShapeDtypeStruct((SSM_BLOCKS, BLOCK_CH, PREP_LANES), F32)
    rows8 = jax.ShapeDtypeStruct((SUBLANES, SSM_LANES), F32)
    return pl.pallas_call(
        body, name=name, grid=(SSM_BLOCKS, nt), in_specs=[ch, ch, ch, st, st, prev, prev, wt, wt, wt, wt, par, par],
        out_specs=[ch, wt, wt, wt, wt, par, par],
        out_shape=[jax.ShapeDtypeStruct((S, SSM_WIDTH), F32), blk, blk, blk, blk, rows8, rows8],
        scratch_shapes=[pltpu.VMEM((tc, PREP_LANES), F32), pltpu.VMEM((tc, PREP_LANES), F32),
                        pltpu.VMEM((1, PREP_LANES), F32), pltpu.VMEM((1, PREP_LANES), F32)],
        compiler_params=_params(2),
    )(dy, du_add, u, xs_r, xs_i, xs_r, xs_i, bb_r, bb_i, cb_r, ncb_i, a_r8, a_i8)


def _ssm_prep_f(a_re, a_im, log_dt, bt_re, bt_im, c_re, c_im):
    first_group = pl.program_id(0) * (SSM_GROUPS // SSM_BLOCKS)
    iota = lambda shape, d: lax.broadcasted_iota(jnp.int32, shape, d)
    grp_of_row = lambda shape: iota(shape, 0) >> int(math.log2(SSM_GROUP))
    grp_of_lane = lambda shape: iota(shape, 1) >> int(math.log2(SSM_STATE))
    rep = (grp_of_row((BLOCK_CH, SSM_GROUPS)) + first_group == iota((BLOCK_CH, SSM_GROUPS), 1)).astype(F32)
    til = ((iota((SSM_STATE, PREP_LANES), 1) & (SSM_STATE - 1)) == iota((SSM_STATE, PREP_LANES), 0)).astype(F32)
    m_rows = (grp_of_row((BLOCK_CH, PREP_LANES)) == grp_of_lane((BLOCK_CH, PREP_LANES))).astype(F32)
    m_grp = (iota((SSM_GROUPS, PREP_LANES), 0) == grp_of_lane((SSM_GROUPS, PREP_LANES)) + first_group).astype(F32)
    dt = jnp.exp(log_dt)
    decay = jnp.exp(a_re * dt)
    ar = decay * jnp.cos(a_im * dt)
    ai = decay * jnp.sin(a_im * dt)
    den = a_re * a_re + a_im * a_im
    nr = ar - 1.0
    coef_r = (nr * a_re + ai * a_im) / den
    coef_i = (ai * a_re - nr * a_im) / den
    cr, ci = _hp_dot(rep, coef_r), _hp_dot(rep, coef_i)
    bb_r = cr * bt_re - ci * bt_im
    bb_i = cr * bt_im + ci * bt_re
    big = lambda m: _hp_dot(m, til) * m_rows
    lanes = lambda m: jnp.broadcast_to(jnp.sum(_hp_dot(m, til) * m_grp, axis=0, keepdims=True), (SUBLANES, PREP_LANES))
    return lanes(ar), lanes(ai), big(bb_r), big(bb_i), big(c_re), -big(c_im)


def _whole(arr, **kw):
    nd = arr.ndim
    return In(arr, arr.shape, lambda *_: (0,) * nd, **kw)


def _adamw_math(w, g, m, v):
    m = ADAM_B1 * m + (1.0 - ADAM_B1) * g
    v = ADAM_B2 * v + (1.0 - ADAM_B2) * (g * g)
    m_hat = m / (1.0 - ADAM_B1 ** ADAM_STEP)
    v_hat = v / (1.0 - ADAM_B2 ** ADAM_STEP)
    delta = -ADAM_LR * (m_hat / (jnp.sqrt(v_hat) + ADAM_EPS) + ADAM_WD * w)
    return delta, m, v


def adamw_big(name, w, g, m, v):
    R, C = w.shape
    tr = R
    for cand in (512, 344, 256, 128):
        if R % cand == 0:
            tr = cand
            break

    def body(w_ref, g_ref, m_ref, v_ref, d_ref, nm_ref, nv_ref):
        d, nm, nv = _adamw_math(w_ref[...], g_ref[...], m_ref[...], v_ref[...])
        d_ref[...] = d
        nm_ref[...] = nm
        nv_ref[...] = nv

    spec = pl.BlockSpec((tr, C), lambda i: (i, 0))
    return pl.pallas_call(
        body, name=name, grid=(R // tr,), in_specs=[spec] * 4, out_specs=[spec] * 3,
        out_shape=[jax.ShapeDtypeStruct((R, C), F32)] * 3, compiler_params=_params(1),
    )(w, g, m, v)


def adamw_small(name, ws, gs, ms, vs):
    n = len(ws)

    def body(*refs):
        for k in range(n):
            d, nm, nv = _adamw_math(refs[k][...], refs[n + k][...], refs[2 * n + k][...], refs[3 * n + k][...])
            refs[4 * n + k][...] = d
            refs[5 * n + k][...] = nm
            refs[6 * n + k][...] = nv

    vm = pl.BlockSpec(memory_space=pltpu.VMEM)
    shapes = [jax.ShapeDtypeStruct(w.shape, F32) for w in ws]
    res = pl.pallas_call(
        body, name=name, in_specs=[vm] * (4 * n), out_specs=[vm] * (3 * n), out_shape=shapes * 3,
        compiler_params=pltpu.CompilerParams(vmem_limit_bytes=VMEM_LIMIT_BYTES),
    )(*ws, *gs, *ms, *vs)
    return res[:n], res[n:2 * n], res[2 * n:]


def _place():
    return lax.axis_index("x"), lax.axis_index("y"), lax.axis_index("c")


def _other_chips(x, y):
    return [(1 - x, y), (x, 1 - y), (1 - x, 1 - y)]


HBM = pl.BlockSpec(memory_space=pl.ANY)


def all_gather_halves(name, blocks):
    n = len(blocks)

    def body(*refs):
        in_refs, out_refs = refs[:n], refs[n:2 * n]
        send_sems, recv_sems = refs[2 * n:]
        x, y, c = _place()
        me, sibling = (x, y, c), (x, y, 1 - c)
        chips = _other_chips(x, y)

        def slot(a, px, py, pc):
            return out_refs[a].at[4 * px + 2 * py + pc]

        def copy(a, k, block, to, src=None):
            return pltpu.make_async_remote_copy(
                src_ref=slot(a, *block) if src is None else src, dst_ref=slot(a, *block),
                send_sem=send_sems.at[a, k], recv_sem=recv_sems.at[a, k], device_id=to, device_id_type=MESH)

        first = []
        for a in range(n):
            first.append(copy(a, 0, me, sibling, src=in_refs[a]))
            first += [copy(a, 1 + j, me, (*chip, c), src=in_refs[a]) for j, chip in enumerate(chips)]
        for cp in first:
            cp.start()
        passed = []
        for j, chip in enumerate(chips):
            for a in range(n):
                copy(a, 1 + j, (*chip, c), me).wait_recv()
                fw = copy(a, 4 + j, (*chip, c), sibling)
                fw.start()
                passed.append(fw)
        for a in range(n):
            copy(a, 0, sibling, me).wait_recv()
            for j, chip in enumerate(chips):
                copy(a, 4 + j, (*chip, 1 - c), me).wait_recv()
        for cp in first + passed:
            cp.wait_send()

    res = pl.pallas_call(
        body, name=name, in_specs=[HBM] * n, out_specs=[HBM] * n,
        out_shape=[jax.ShapeDtypeStruct((N_DEV,) + b.shape, b.dtype) for b in blocks],
        scratch_shapes=[pltpu.SemaphoreType.DMA((n, 7)), pltpu.SemaphoreType.DMA((n, 7))],
    )(*blocks)
    return list(res)


def pair_send_halves(name, grads):
    n = len(grads)

    def body(*refs):
        in_refs, out_refs = refs[:n], refs[n:2 * n]
        send_sems, recv_sems = refs[2 * n:]
        x, y, c = _place()
        cps = []
        for a in range(n):
            cp = pltpu.make_async_remote_copy(
                src_ref=in_refs[a].at[:, 1 - c], dst_ref=out_refs[a], send_sem=send_sems.at[a], recv_sem=recv_sems.at[a],
                device_id=(x, y, 1 - c), device_id_type=MESH)
            cp.start()
            cps.append(cp)
        for cp in cps:
            cp.wait()

    res = pl.pallas_call(
        body, name=name, in_specs=[HBM] * n, out_specs=[HBM] * n,
        out_shape=[jax.ShapeDtypeStruct((g.shape[0],) + g.shape[2:], g.dtype) for g in grads],
        scratch_shapes=[pltpu.SemaphoreType.DMA((n,)), pltpu.SemaphoreType.DMA((n,))],
    )(*grads)
    return list(res)


def pair_add(name, grad, got, c_arr):
    nsh, _, M, N = grad.shape
    tr = M
    for cand in (512, 256, 192, 128, 64, 16):
        if M % cand == 0:
            tr = cand
            break

    def body(c_ref, g_ref, p_ref, o_ref):
        o_ref[...] = (g_ref[...] + p_ref[...]).astype(BF16)

    return pl.pallas_call(
        body, name=name,
        grid_spec=pltpu.PrefetchScalarGridSpec(
            num_scalar_prefetch=1, grid=(nsh, M // tr),
            in_specs=[pl.BlockSpec((None, None, tr, N), lambda j, i, c_ref: (j, c_ref[0], i, 0)),
                      pl.BlockSpec((None, tr, N), lambda j, i, c_ref: (j, i, 0))],
            out_specs=pl.BlockSpec((None, tr, N), lambda j, i, c_ref: (j, i, 0))),
        out_shape=jax.ShapeDtypeStruct((nsh, M, N), BF16), compiler_params=_params(2),
    )(c_arr, grad, got)


def scatter_to_chips(name, parts):
    n = len(parts)

    def body(*refs):
        in_refs, out_refs = refs[:n], refs[n:2 * n]
        send_sems, recv_sems = refs[2 * n:]
        x, y, c = _place()
        mine = 2 * x + y
        chips = _other_chips(x, y)
        cps = []
        for a in range(n):
            for k, (px, py) in enumerate(chips):
                cp = pltpu.make_async_remote_copy(
                    src_ref=in_refs[a].at[2 * px + py], dst_ref=out_refs[a].at[mine],
                    send_sem=send_sems.at[a, k], recv_sem=recv_sems.at[a, k], device_id=(px, py, c), device_id_type=MESH)
                cp.start()
                cps.append((cp, a, k, px, py))
        for cp, a, k, px, py in cps:
            pltpu.make_async_remote_copy(
                src_ref=in_refs[a].at[mine], dst_ref=out_refs[a].at[2 * px + py],
                send_sem=send_sems.at[a, k], recv_sem=recv_sems.at[a, k], device_id=(px, py, c), device_id_type=MESH).wait_recv()
        for cp, *_ in cps:
            cp.wait_send()

    res = pl.pallas_call(
        body, name=name, in_specs=[HBM] * n, out_specs=[HBM] * n,
        out_shape=[jax.ShapeDtypeStruct(p.shape, p.dtype) for p in parts],
        scratch_shapes=[pltpu.SemaphoreType.DMA((n, 3)), pltpu.SemaphoreType.DMA((n, 3))],
    )(*parts)
    return list(res)


def sum_chips(name, q):
    nsh, M, N = q.shape
    tr = M
    for cand in (512, 256, 192, 128, 64, 16):
        if M % cand == 0:
            tr = cand
            break

    def body(q_ref, o_ref):
        acc = q_ref[0].astype(F32)
        for j in range(1, nsh):
            acc = acc + q_ref[j].astype(F32)
        o_ref[...] = acc

    return pl.pallas_call(
        body, name=name, grid=(M // tr,), in_specs=[pl.BlockSpec((nsh, tr, N), lambda i: (0, i, 0))],
        out_specs=pl.BlockSpec((tr, N), lambda i: (i, 0)), out_shape=jax.ShapeDtypeStruct((M, N), F32),
        compiler_params=_params(1),
    )(q)


def pair_exchange(name, halves):
    n = len(halves)

    def body(*refs):
        in_refs, out_refs = refs[:n], refs[n:2 * n]
        send_sems, recv_sems = refs[2 * n:]
        x, y, c = _place()
        cps = []
        for a in range(n):
            cp = pltpu.make_async_remote_copy(
                src_ref=in_refs[a], dst_ref=out_refs[a], send_sem=send_sems.at[a], recv_sem=recv_sems.at[a],
                device_id=(x, y, 1 - c), device_id_type=MESH)
            cp.start()
            cps.append(cp)
        for cp in cps:
            cp.wait()

    res = pl.pallas_call(
        body, name=name, in_specs=[HBM] * n, out_specs=[HBM] * n,
        out_shape=[jax.ShapeDtypeStruct(h.shape, h.dtype) for h in halves],
        scratch_shapes=[pltpu.SemaphoreType.DMA((n,)), pltpu.SemaphoreType.DMA((n,))],
    )(*halves)
    return list(res)


SEM = pl.BlockSpec(memory_space=pltpu.SEMAPHORE)
IN_HBM = pl.BlockSpec(memory_space=pltpu.HBM)
SPLIT_COPY = pltpu.CompilerParams(has_side_effects=pltpu.SideEffectType.DATAFLOW_SIDE_EFFECTING)


def _scatter_copies(src_refs, dst_refs, send_sems, recv_sems):
    x, y, c = _place()
    mine = 2 * x + y
    return [pltpu.make_async_remote_copy(
        src_ref=src_refs[a].at[2 * px + py], dst_ref=dst_refs[a].at[mine], send_sem=send_sems.at[a * (N_CHIPS - 1) + k],
        recv_sem=recv_sems.at[a * (N_CHIPS - 1) + k], device_id=(px, py, c), device_id_type=MESH)
        for a in range(len(src_refs)) for k, (px, py) in enumerate(_other_chips(x, y))]


def _pair_copies(src_refs, dst_refs, send_sems, recv_sems):
    x, y, c = _place()
    return [pltpu.make_async_remote_copy(
        src_ref=src_refs[a].at[:, 1 - c], dst_ref=dst_refs[a], send_sem=send_sems.at[a], recv_sem=recv_sems.at[a],
        device_id=(x, y, 1 - c), device_id_type=MESH) for a in range(len(src_refs))]


def _gather_copies(src_refs, dst_refs, send_sems, recv_sems):
    x, y, c = _place()
    me = 4 * x + 2 * y + c
    cps = []
    for a in range(len(src_refs)):
        for k in range(1, N_DEV):
            to = (1 - x if k & 4 else x, 1 - y if k & 2 else y, 1 - c if k & 1 else c)
            s = a * (N_DEV - 1) + k - 1
            cps.append(pltpu.make_async_remote_copy(
                src_ref=src_refs[a], dst_ref=dst_refs[a].at[me], send_sem=send_sems.at[s], recv_sem=recv_sems.at[s],
                device_id=to, device_id_type=MESH))
    return cps


def split_copy_start(name, copies, n_sem, srcs, land_shapes):
    n = len(srcs)
    lands = [lax.empty(s.shape, s.dtype) for s in land_shapes]

    def body(*refs):
        for cp in copies(refs[:n], refs[n:2 * n], refs[2 * n], refs[2 * n + 1]):
            cp.start()
        refs[-1][...] = jnp.zeros_like(refs[-1])

    thru = [pltpu.HBM(a.shape, a.dtype) for a in (*srcs, *lands)]
    res = pl.pallas_call(
        body, name=name, in_specs=[IN_HBM] * (2 * n),
        out_specs=(SEM, SEM, *[IN_HBM] * (2 * n), pl.BlockSpec(memory_space=pltpu.VMEM)),
        out_shape=(pltpu.SemaphoreType.DMA((n * n_sem,)), pltpu.SemaphoreType.DMA((n * n_sem,)), *thru,
                   jax.ShapeDtypeStruct((SUBLANES, LANES), F32)),
        input_output_aliases={i: 2 + i for i in range(2 * n)}, compiler_params=SPLIT_COPY,
    )(*[pltpu.with_memory_space_constraint(a, pltpu.HBM) for a in (*srcs, *lands)])
    return (copies, n, res[0], res[1], res[2:2 + 2 * n]), res[-1][0, 0]


def split_copy_wait(name, handle, after):
    copies, n, send_sems, recv_sems, thru = handle

    def body(*refs):
        for cp in copies(refs[:n], refs[n:2 * n], refs[2 * n], refs[2 * n + 1]):
            cp.wait_send()
            cp.wait_recv()

    res = pl.pallas_call(
        body, name=name, in_specs=[IN_HBM] * (2 * n) + [SEM, SEM, pl.BlockSpec(memory_space=pl.ANY)],
        out_specs=[IN_HBM] * (2 * n), out_shape=[pltpu.HBM(a.shape, a.dtype) for a in thru],
        input_output_aliases={i: i for i in range(2 * n)}, compiler_params=SPLIT_COPY,
    )(*thru, send_sems, recv_sems, after)
    return list(res[:n]), list(res[n:])


def all_reduce_small(name, v):
    R, C = v.shape

    def body(v_ref, o_ref, gath, send_sems, recv_sems):
        x, y, c = _place()
        me, sibling = (x, y, c), (x, y, 1 - c)
        chips = _other_chips(x, y)

        def slot(px, py, pc):
            return gath.at[4 * px + 2 * py + pc]

        def copy(k, block, to, src=None):
            return pltpu.make_async_remote_copy(
                src_ref=slot(*block) if src is None else src, dst_ref=slot(*block),
                send_sem=send_sems.at[k], recv_sem=recv_sems.at[k], device_id=to, device_id_type=MESH)

        first = [copy(0, me, sibling, src=v_ref)]
        first += [copy(1 + j, me, (*chip, c), src=v_ref) for j, chip in enumerate(chips)]
        for cp in first:
            cp.start()
        slot(*me)[...] = v_ref[...]
        passed = [copy(4 + j, (*chip, c), sibling) for j, chip in enumerate(chips)]
        for j, chip in enumerate(chips):
            copy(1 + j, (*chip, c), me).wait_recv()
            passed[j].start()
        copy(0, sibling, me).wait_recv()
        for j, chip in enumerate(chips):
            copy(4 + j, (*chip, 1 - c), me).wait_recv()
        for cp in first + passed:
            cp.wait_send()
        acc = gath[0]
        for d in range(1, N_DEV):
            acc = acc + gath[d]
        o_ref[...] = acc

    vm = pl.BlockSpec(memory_space=pltpu.VMEM)
    return pl.pallas_call(
        body, name=name, in_specs=[vm], out_specs=vm, out_shape=jax.ShapeDtypeStruct((R, C), F32),
        scratch_shapes=[pltpu.VMEM((N_DEV, R, C), F32), pltpu.SemaphoreType.DMA((7,)), pltpu.SemaphoreType.DMA((7,))],
        compiler_params=pltpu.CompilerParams(vmem_limit_bytes=VMEM_LIMIT_BYTES),
    )(v)


LANES = 128
PACK_ROW_MULTIPLE = 1024
SMALL_SHARDED = {
    "w_in": ((D_MODEL, 1216), 1), "mla_w_uq": ((MLA_Q_RANK, 768), 1), "mla_w_ukv": ((MLA_KV_RANK, 1024), 1),
    "ssm_w_glu": ((SSM_WIDTH, SSM_WIDTH), 0), "w_o": ((D_MODEL, D_MODEL), 0), "xattn_w_q": ((D_MODEL, 512), 0),
    "xattn_w_kv": ((D_MODEL, 1024), 0), "xattn_w_o": ((512, D_MODEL), 1),
}
FFN_NAMES = ["ffn1_w_gate", "ffn1_w_up", "ffn1_w_down", "ffn2_w_gate", "ffn2_w_up", "ffn2_w_down"]
TRANSPOSED_VIEW = ("ffn1_w_gate", "ffn1_w_up", "ffn2_w_gate", "ffn2_w_up", "w_in", "mla_w_uq")


def _shard_shape(name):
    (r, cdim), ax = SMALL_SHARDED[name]
    return (r // N_CHIPS, cdim) if ax == 0 else (r, cdim // N_CHIPS)


def _pack_shards(shards):
    parts = []
    for name in SMALL_SHARDED:
        a = shards[name]
        lead = a.shape[:-2]
        parts.append(a.reshape(lead + (a.shape[-2] * a.shape[-1] // LANES, LANES)))
    rows = sum(q.shape[-2] for q in parts)
    parts.append(jnp.zeros(lead + (-rows % PACK_ROW_MULTIPLE, LANES), parts[0].dtype))
    return jnp.concatenate(parts, axis=-2)


def _unpack_shards(packed):
    out, r0 = {}, 0
    lead = packed.shape[:-2]
    for name in SMALL_SHARDED:
        r, cdim = _shard_shape(name)
        rows = r * cdim // LANES
        out[name] = packed[..., r0:r0 + rows, :].reshape(lead + (r, cdim))
        r0 += rows
    return out


def _full_from_shards(name, sh):
    (r, cdim), ax = SMALL_SHARDED[name]
    if ax == 0:
        return sh.reshape(r, cdim)
    return jnp.transpose(sh, (1, 0, 2)).reshape(r, cdim)


def _shards_from_full(name, full):
    (r, cdim), ax = SMALL_SHARDED[name]
    if ax == 0:
        return full.reshape(N_CHIPS, r // N_CHIPS, cdim)
    return jnp.transpose(full.reshape(r, N_CHIPS, cdim // N_CHIPS), (1, 0, 2))


SMALL_REPL = {
    "ffn1_norm": (1, 1024), "mix_norm": (1, 1024), "mla_q_norm": (1, 384), "mla_kv_norm": (1, 256),
    "mla_qk_norm_q": (1, 192), "mla_qk_norm_k": (1, 192), "ssm_a_re": (32, 64), "ssm_a_im": (32, 64),
    "ssm_log_dt": (32, 1), "ssm_b_re": (32, 64, 16), "ssm_b_im": (32, 64, 16), "ssm_c_re": (32, 16, 64),
    "ssm_c_im": (32, 16, 64), "ssm_d": (1, 512), "ssm_b_glu": (1, 512), "out_norm_mla": (1, 512),
    "out_norm_ssm": (1, 512), "xattn_norm": (1, 1024), "mem_norm": (1, 1024), "xattn_q_norm": (1, 128),
    "xattn_k_norm": (1, 128), "ffn2_norm": (1, 1024),
}


def _pack_repl(grads):
    flat = jnp.concatenate([grads[n].reshape(-1) for n in SMALL_REPL])
    rows = -(-flat.shape[0] // (LANES * SUBLANES)) * SUBLANES
    return jnp.pad(flat, (0, rows * LANES - flat.shape[0])).reshape(rows, LANES)


def _unpack_repl(packed):
    flat, out, o = packed.reshape(-1), {}, 0
    for n, shp in SMALL_REPL.items():
        size = int(np.prod(shp))
        out[n] = flat[o:o + size].reshape(shp)
        o += size
    return out


def _rope_tables(positions):
    half = MLA_ROPE // 2
    inv = ROPE_THETA ** (-jnp.arange(half, dtype=F32) / half)
    ang = positions.astype(F32)[:, None] * inv[None, :]
    cos, sin = jnp.cos(ang), jnp.sin(ang)
    S = positions.shape[0]
    z = lambda w: jnp.zeros((S, w), F32)
    keep = jnp.concatenate([jnp.ones((S, MLA_NOPE), F32), cos, cos, z(HEAD_PAD - MLA_QK)], axis=1)
    from_hi = jnp.concatenate([z(MLA_NOPE), -sin, z(HEAD_PAD - MLA_NOPE - half)], axis=1)
    from_lo = jnp.concatenate([z(MLA_NOPE + half), sin, z(HEAD_PAD - MLA_QK)], axis=1)
    return keep, from_hi, from_lo


def _norm_rope(x, g, keep, from_hi, from_lo):
    y = x * lax.rsqrt(jnp.sum(x * x, axis=-1, keepdims=True) * (1.0 / MLA_QK) + EPS) * g
    half = MLA_ROPE // 2
    return y * keep + _lane_roll(y, HEAD_PAD - half) * from_hi + _lane_roll(y, half) * from_lo


def local_step(x, mem, positions, target, w, wb, small_weights=None, ffn2_weights=None, on_grads=None):
    if small_weights is None:
        small_weights = lambda after: {}
    if ffn2_weights is None:
        ffn2_weights = lambda after: [wb[k] for k in FFN_NAMES[3:]]
    if on_grads is None:
        on_grads = lambda tag, g: 0.0
    S = x.shape[0]
    tm = min(FUSED_TILE, S)
    g1 = (S // tm,)
    tile = lambda arr, **kw: In(arr, (tm, arr.shape[1]), lambda i: (i, 0), **kw)
    par = lambda arr, **kw: In(arr, arr.shape, lambda *_: (0, 0), diff=True, acc=(0,), **kw)
    wt = lambda arr: In(arr, weight=True)
    otile = lambda cols, dt: Out((S, cols), dt, (tm, cols), lambda i: (i, 0))
    grads = {}

    x1, h1, g_1, u_1 = ffn_fwd("ffn1_fwd", x, w["ffn1_norm"], wb["ffn1_w_gate"], wb["ffn1_w_up"], wb["ffn1_w_down"])
    wb = {**wb, **small_weights(x1)}

    keep, from_hi, from_lo = _rope_tables(positions)
    scale = MLA_QK ** -0.5

    def f_pre(xv, kp, fh, fl, g_mix, w_q, w_kv, w_kr, w_u, g_q, g_kv, w_uq, w_ukv, gq, gk):
        h2 = _rms(xv, g_mix)
        cq, ckv = _rms(wdot(h2, w_q), g_q), _rms(wdot(h2, w_kv), g_kv)
        kr = wdot(h2, w_kr)
        qs, ks, vs = [], [], []
        for h in range(MLA_HEADS):
            qs.append(_norm_rope(wdot(cq, w_uq, h), gq, kp, fh, fl) * scale)
            kv = wdot(ckv, w_ukv, h)
            ks.append(_norm_rope(jnp.concatenate([kv[:, :MLA_NOPE], kr], axis=-1), gk, kp, fh, fl))
            vs.append(kv[:, MLA_NOPE:])
        return jnp.concatenate(qs, axis=-1), jnp.concatenate(ks, axis=-1), jnp.concatenate(vs, axis=-1), wdot(h2, w_u)

    def pre_ins(gain_mix):
        return [tile(x1, diff=True), tile(keep), tile(from_hi), tile(from_lo), par(gain_mix),
                wt(wb["w_in_q"]), wt(wb["w_in_kv"]), wt(wb["w_in_kr"]), wt(wb["w_in_u"]),
                par(w["mla_q_norm"]), par(w["mla_kv_norm"]), wt(wb["w_uq"]), wt(wb["w_ukv"]),
                par(w["qk_gain_q"]), par(w["qk_gain_k"])]

    pre_outs = [otile(MLA_HEADS * HEAD_PAD, BF16), otile(MLA_HEADS * HEAD_PAD, BF16), otile(MLA_HEADS * MLA_V, BF16),
                otile(SSM_WIDTH, F32)]
    qh, kh, v0, pu = seg_fwd("pre_mixer_fwd", f_pre, g1, pre_ins(w["mix_norm"]), pre_outs)

    o_mla, lse = attn_fwd("mla_attn_fwd", qh, kh, v0, t=min(ATTN_TILE, S))

    prep_grid = (SSM_BLOCKS,)
    prep_ins = ([_whole(w[k], diff=True, acc=(0,)) for k in ("ssm_a_re", "ssm_a_im", "ssm_log_dt")]
                + [In(w[k], (BLOCK_CH, SSM_STATE), lambda i: (i, 0), diff=True)
                   for k in ("ssm_bt_re", "ssm_bt_im", "ssm_c2_re", "ssm_c2_im")])
    blk_out = Out((SSM_BLOCKS, BLOCK_CH, PREP_LANES), BF16, (None, BLOCK_CH, PREP_LANES), lambda i: (i, 0, 0))
    prep_outs = [Out((SUBLANES, SSM_LANES), F32, (SUBLANES, PREP_LANES), lambda i: (0, i))] * 2 + [blk_out] * 4
    a_r8, a_i8, bb_r, bb_i, cb_r, ncb_i = seg_fwd("ssm_prep_fwd", _ssm_prep_f, prep_grid, prep_ins, prep_outs)

    xs_r, xs_i, y_lin = ssm_fwd("ssm_fwd", pu, bb_r, bb_i, cb_r, ncb_i, a_r8, a_i8)

    M = mem.shape[0]
    f_norm = lambda xv, g: (_rms(xv, g),)
    mem_ins = [In(mem, (M, D_MODEL), lambda i: (0, 0)), par(w["mem_norm"])]
    mem_outs = [Out((M, D_MODEL), BF16, (M, D_MODEL), lambda i: (0, 0))]
    (mn,) = seg_fwd("mem_norm_fwd", f_norm, (1,), mem_ins, mem_outs)
    (kvm,) = mm("xattn_kv_fwd", [mn], [[wb["xattn_w_kv"]]], [F32])

    xscale = XHD ** -0.5

    def f_post(o, yl, u, xv, kv, d, w_glu, b, gm, gs, w_o1, w_o2, gx, w_xq, gq, gk, w_xo):
        gl = jax.nn.gelu(yl + d * u)
        so = gl * jax.nn.sigmoid(wdot(gl, w_glu) + b)
        x2 = xv + wdot(_rms(o, gm), w_o1) + wdot(_rms(so, gs), w_o2)
        h3 = _rms(x2, gx)
        heads = []
        for h in range(XH):
            qn = _rms(wdot(h3, w_xq, h), gq)
            kn = _rms(kv[:, h * XHD:(h + 1) * XHD], gk)
            p = jax.nn.softmax(_bdot_nt(qn, kn) * xscale, axis=-1)
            heads.append(_bdot_nn(p, kv[:, (XH + h) * XHD:(XH + h + 1) * XHD]))
        return (x2 + wdot(jnp.concatenate(heads, axis=-1), w_xo),)

    def post_ins(gain_d):
        return [tile(o_mla, diff=True), tile(y_lin, diff=True), tile(pu, diff=True), tile(x1, diff=True),
                In(kvm, kvm.shape, lambda i: (0, 0), diff=True, acc=(0,)), par(gain_d), wt(wb["ssm_w_glu"]),
                par(w["ssm_b_glu"]), par(w["out_norm_mla"]), par(w["out_norm_ssm"]), wt(wb["w_o_mla"]), wt(wb["w_o_ssm"]),
                par(w["xattn_norm"]), wt(wb["xattn_w_q"]), par(w["xattn_q_norm"]), par(w["xattn_k_norm"]),
                wt(wb["xattn_w_o"])]

    post_outs = [otile(D_MODEL, F32)]
    (x3,) = seg_fwd("post_mixer_fwd", f_post, g1, post_ins(w["ssm_d"]), post_outs)

    wg2, wu2, wd2 = ffn2_weights(x3)
    dx4, h4, g_2, u_2, parts = ffn_fwd("ffn2_fwd_loss", x3, w["ffn2_norm"], wg2, wu2, wd2, target=target)
    loss = jnp.sum(parts[::SUBLANES, 0])

    dx3, grads["ffn2_norm"], dg_2, du_2 = ffn_bwd_act("ffn2_bwd_act", dx4, x3, w["ffn2_norm"], g_2, u_2, wg2, wu2, wd2)
    grads["ffn2_w_gate"], grads["ffn2_w_up"], grads["ffn2_w_down"] = ffn_bwd_w("ffn2_bwd_w", h4, dx4, g_2, u_2, dg_2, du_2)
    sent = on_grads("ffn2", grads)

    (do_mla, dy_lin, du_a, dx1_a, dkvm, grads["ssm_d"], grads["ssm_w_glu"], grads["ssm_b_glu"], grads["out_norm_mla"],
     grads["out_norm_ssm"], grads["w_o_mla"], grads["w_o_ssm"], grads["xattn_norm"], grads["xattn_w_q"],
     grads["xattn_q_norm"], grads["xattn_k_norm"], grads["xattn_w_o"]) = seg_bwd(
        "post_mixer_bwd", f_post, g1, post_ins(w["ssm_d"] + sent), post_outs, [dx3])

    (dmn,) = mm("xattn_kv_bwd", [dkvm], [[wb["xattn_w_kv"]]], [F32], trans=True)
    (grads["xattn_w_kv"],) = mm_tn("xattn_kv_bwd_w", [mn], [dkvm], [(0, [0])])
    (grads["mem_norm"],) = seg_bwd("mem_norm_bwd", f_norm, (1,), mem_ins, mem_outs, [dmn])

    du, dbb_r, dbb_i, dcb_r, dncb_i, da_r8, da_i8 = ssm_bwd("ssm_bwd", dy_lin, du_a, pu, xs_r, xs_i, bb_r, bb_i, cb_r, ncb_i,
                                                            a_r8, a_i8)
    prep_g = seg_bwd("ssm_prep_bwd", _ssm_prep_f, prep_grid, prep_ins, prep_outs, [da_r8, da_i8, dbb_r, dbb_i, dcb_r, dncb_i])
    for k, gname in enumerate(("ssm_a_re", "ssm_a_im", "ssm_log_dt", "ssm_bt_re", "ssm_bt_im", "ssm_c2_re", "ssm_c2_im")):
        grads[gname] = prep_g[k]

    dqh, dkh, dv0 = attn_bwd("mla_attn_bwd", qh, kh, v0, do_mla, o_mla, lse, t=min(ATTN_TILE, S))

    (dx1, grads["mix_norm"], grads["w_in_q"], grads["w_in_kv"], grads["w_in_kr"], grads["w_in_u"], grads["mla_q_norm"],
     grads["mla_kv_norm"], grads["w_uq"], grads["w_ukv"], grads["qk_gain_q"], grads["qk_gain_k"]) = seg_bwd(
        "pre_mixer_bwd", f_pre, g1, pre_ins(w["mix_norm"]), pre_outs, [dqh, dkh, dv0, du], adds={0: dx1_a})

    sent = on_grads("small", grads)
    dx, grads["ffn1_norm"], dg_1, du_1 = ffn_bwd_act("ffn1_bwd_act", dx1, x, w["ffn1_norm"] + sent, g_1, u_1,
                                                     wb["ffn1_w_gate"], wb["ffn1_w_up"], wb["ffn1_w_down"])
    grads["ffn1_w_gate"], grads["ffn1_w_up"], grads["ffn1_w_down"] = ffn_bwd_w("ffn1_bwd_w", h1, dx1, g_1, u_1, dg_1, du_1)
    return loss, dx, grads


def _pad_cols(a, n):
    return jnp.pad(a, ((0, 0), (0, n - a.shape[1])))


def _step_weights(shards):
    wb = {}
    w_in = _full_from_shards("w_in", shards["w_in"])
    wb["w_in_q"] = w_in[:, :MLA_Q_RANK]
    wb["w_in_kv"] = w_in[:, MLA_Q_RANK:MLA_Q_RANK + MLA_KV_RANK]
    wb["w_in_kr"] = _pad_cols(w_in[:, MLA_Q_RANK + MLA_KV_RANK:MLA_Q_RANK + MLA_KV_RANK + MLA_ROPE], LANES)
    wb["w_in_u"] = w_in[:, MLA_Q_RANK + MLA_KV_RANK + MLA_ROPE:]
    wb["w_uq"] = jnp.pad(shards["mla_w_uq"], ((0, 0), (0, 0), (0, HEAD_PAD - MLA_QK)))
    wb["w_ukv"] = shards["mla_w_ukv"]
    wb["ssm_w_glu"] = _full_from_shards("ssm_w_glu", shards["ssm_w_glu"])
    w_o = _full_from_shards("w_o", shards["w_o"])
    wb["w_o_mla"], wb["w_o_ssm"] = w_o[:SSM_WIDTH], w_o[SSM_WIDTH:]
    w_xq = _full_from_shards("xattn_w_q", shards["xattn_w_q"])
    wb["xattn_w_q"] = jnp.transpose(w_xq.reshape(D_MODEL, XH, XHD), (1, 0, 2))
    wb["xattn_w_kv"] = _full_from_shards("xattn_w_kv", shards["xattn_w_kv"])
    wb["xattn_w_o"] = _full_from_shards("xattn_w_o", shards["xattn_w_o"])
    return wb


def _sharded_grads(g):
    out = {}
    kr = g["w_in_kr"][:, :MLA_ROPE]
    out["w_in"] = _shards_from_full("w_in", jnp.concatenate([g["w_in_q"], g["w_in_kv"], kr, g["w_in_u"]], axis=1))
    out["mla_w_uq"] = g["w_uq"][:, :, :MLA_QK]
    out["mla_w_ukv"] = g["w_ukv"]
    out["ssm_w_glu"] = _shards_from_full("ssm_w_glu", g["ssm_w_glu"])
    out["w_o"] = _shards_from_full("w_o", jnp.concatenate([g["w_o_mla"], g["w_o_ssm"]], axis=0))
    w_xq = jnp.transpose(g["xattn_w_q"], (1, 0, 2)).reshape(D_MODEL, XH * XHD)
    out["xattn_w_q"] = _shards_from_full("xattn_w_q", w_xq)
    out["xattn_w_kv"] = _shards_from_full("xattn_w_kv", g["xattn_w_kv"])
    out["xattn_w_o"] = _shards_from_full("xattn_w_o", g["xattn_w_o"])
    return out


def _problem_repl_grads(g):
    out = {}
    out["mla_qk_norm_q"] = g["qk_gain_q"][:, :MLA_QK]
    out["mla_qk_norm_k"] = g["qk_gain_k"][:, :MLA_QK]
    out["ssm_b_re"] = jnp.transpose(g["ssm_bt_re"].reshape(SSM_GROUPS, SSM_GROUP, SSM_STATE), (0, 2, 1))
    out["ssm_b_im"] = jnp.transpose(g["ssm_bt_im"].reshape(SSM_GROUPS, SSM_GROUP, SSM_STATE), (0, 2, 1))
    out["ssm_c_re"] = g["ssm_c2_re"].reshape(SSM_GROUPS, SSM_GROUP, SSM_STATE)
    out["ssm_c_im"] = g["ssm_c2_im"].reshape(SSM_GROUPS, SSM_GROUP, SSM_STATE)
    for k in SMALL_REPL:
        if k not in out:
            out[k] = g[k]
    return out


def _problem_grads(g):
    out = {k: _full_from_shards(k, v) for k, v in _sharded_grads(g).items()}
    out.update(_problem_repl_grads(g))
    out.update({k: g[k] for k in FFN_NAMES})
    return out


def _step_params(p):
    row = lambda a: a.reshape(1, -1)
    w = {k: row(p[k]) for k in ("ffn1_norm", "mix_norm", "mla_q_norm", "mla_kv_norm", "ssm_b_glu", "out_norm_mla",
                                "out_norm_ssm", "xattn_norm", "mem_norm", "xattn_q_norm", "xattn_k_norm", "ffn2_norm")}
    w["qk_gain_q"] = _pad_cols(row(p["mla_qk_norm_q"]), HEAD_PAD)
    w["qk_gain_k"] = _pad_cols(row(p["mla_qk_norm_k"]), HEAD_PAD)
    w["ssm_a_re"], w["ssm_a_im"] = p["ssm_a_re"], p["ssm_a_im"]
    w["ssm_log_dt"] = p["ssm_log_dt"].reshape(SSM_GROUPS, 1)
    w["ssm_bt_re"] = jnp.transpose(p["ssm_b_re"], (0, 2, 1)).reshape(SSM_WIDTH, SSM_STATE)
    w["ssm_bt_im"] = jnp.transpose(p["ssm_b_im"], (0, 2, 1)).reshape(SSM_WIDTH, SSM_STATE)
    w["ssm_c2_re"] = p["ssm_c_re"].reshape(SSM_WIDTH, SSM_STATE)
    w["ssm_c2_im"] = p["ssm_c_im"].reshape(SSM_WIDTH, SSM_STATE)
    w["ssm_d"] = p["ssm_d"].reshape(1, SSM_WIDTH)
    return w


ARG_NAMES = ['x', 'mem', 'positions', 'ffn1_norm', 'ffn1_w_gate', 'ffn1_w_up', 'ffn1_w_down', 'mix_norm', 'w_in', 'mla_q_norm', 'mla_w_uq', 'mla_kv_norm', 'mla_w_ukv', 'mla_qk_norm_q', 'mla_qk_norm_k', 'ssm_a_re', 'ssm_a_im', 'ssm_log_dt', 'ssm_b_re', 'ssm_b_im', 'ssm_c_re', 'ssm_c_im', 'ssm_d', 'ssm_w_glu', 'ssm_b_glu', 'out_norm_mla', 'out_norm_ssm', 'w_o', 'xattn_norm', 'mem_norm', 'xattn_w_q', 'xattn_w_kv', 'xattn_q_norm', 'xattn_k_norm', 'xattn_w_o', 'ffn2_norm', 'ffn2_w_gate', 'ffn2_w_up', 'ffn2_w_down']
WEIGHT_NAMES = ARG_NAMES[3:]


def _gather_weights(p, c):
    half = lambda a: lax.dynamic_slice_in_dim(a, c * (a.shape[0] // 2), a.shape[0] // 2, axis=0)
    ffn1 = [half(p[k].astype(BF16)) for k in FFN_NAMES[:3]]
    small = [half(_pack_shards({k: p[k].astype(BF16) for k in SMALL_SHARDED}))]
    ffn2 = [half(p[k].astype(BF16)) for k in FFN_NAMES[3:]]
    me = 4 * lax.axis_index("x") + 2 * lax.axis_index("y") + c
    own = lambda got, blocks: [lax.dynamic_update_index_in_dim(g, b, me, 0) for g, b in zip(got, blocks)]
    as_shards = lambda a: a.reshape(N_CHIPS, 2 * a.shape[1], a.shape[2])
    landing = lambda blocks: [jax.ShapeDtypeStruct((N_DEV,) + b.shape, b.dtype) for b in blocks]
    got1 = own(all_gather_halves("all_gather_weights_a", ffn1), ffn1)
    got1, small = lax.optimization_barrier((got1, small))
    flight_s, sent_s = split_copy_start("gather_small_start", _gather_copies, N_DEV - 1, small, landing(small))
    sent_s, ffn2 = lax.optimization_barrier((sent_s, ffn2))
    flight_2, sent_2 = split_copy_start("gather_ffn2_start", _gather_copies, N_DEV - 1, ffn2, landing(ffn2))
    wb = {k: as_shards(a) for k, a in zip(FFN_NAMES[:3], got1)}

    def small_weights(after):
        mine, got = split_copy_wait("gather_small_wait", flight_s, after)
        return _step_weights(_unpack_shards(own(got, mine)[0].reshape(N_CHIPS, -1, LANES)))

    def ffn2_weights(after):
        mine, got = split_copy_wait("gather_ffn2_wait", flight_2, after)
        return [as_shards(a) for a in own(got, mine)]

    return wb, small_weights, ffn2_weights, sent_s + sent_2


class _GradReduce:
    def __init__(self, c):
        self.c, self.c_arr = c, jnp.reshape(c, (1,)).astype(jnp.int32)
        self.chip = 2 * lax.axis_index("x") + lax.axis_index("y")
        self.flights = []

    def start(self, tag, arrs):
        split = [a.reshape(N_CHIPS, 2, a.shape[1] // 2, a.shape[2]) for a in arrs]
        return self._scatter(tag, split, pair_send_halves(f"grad_pair_send_{tag}", split))

    def send(self, tag, arrs):
        split = [a.reshape(N_CHIPS, 2, a.shape[1] // 2, a.shape[2]) for a in arrs]
        lands = [jax.ShapeDtypeStruct((s.shape[0],) + s.shape[2:], s.dtype) for s in split]
        flight, sent = split_copy_start(f"grad_pair_send_start_{tag}", _pair_copies, 1, split, lands)
        self.sending = (tag, flight)
        return sent

    def scatter(self, tag, after):
        sent_tag, flight = self.sending
        assert sent_tag == tag
        return self._scatter(tag, *split_copy_wait(f"grad_pair_send_wait_{tag}", flight, after))

    def _scatter(self, tag, split, got):
        parts = [pair_add(f"grad_pair_add_{tag}_{k}", s, g, self.c_arr) for k, (s, g) in enumerate(zip(split, got))]
        flight, sent = split_copy_start(f"grad_scatter_start_{tag}", _scatter_copies, N_CHIPS - 1, parts, parts)
        self.flights.append((tag, flight))
        return sent

    def finish(self, after):
        halves = []
        for tag, flight in self.flights:
            parts, landed = split_copy_wait(f"grad_scatter_wait_{tag}", flight, after)
            for k, (q, p) in enumerate(zip(landed, parts)):
                mine = lax.dynamic_index_in_dim(p, self.chip, 0, keepdims=False)
                halves.append(sum_chips(f"grad_sum_{tag}_{k}", lax.dynamic_update_index_in_dim(q, mine, self.chip, 0)))
        tags = "_".join(t for t, _ in self.flights)
        self.flights = []
        theirs = pair_exchange(f"grad_pair_exchange_{tags}", halves)
        return [jnp.where(self.c == 0, jnp.concatenate([h, t], axis=0), jnp.concatenate([t, h], axis=0))
                for h, t in zip(halves, theirs)]


def kernel(x, mem, positions, ffn1_norm, ffn1_w_gate, ffn1_w_up, ffn1_w_down, mix_norm, w_in, mla_q_norm, mla_w_uq, mla_kv_norm, mla_w_ukv, mla_qk_norm_q, mla_qk_norm_k, ssm_a_re, ssm_a_im, ssm_log_dt, ssm_b_re, ssm_b_im, ssm_c_re, ssm_c_im, ssm_d, ssm_w_glu, ssm_b_glu, out_norm_mla, out_norm_ssm, w_o, xattn_norm, mem_norm, xattn_w_q, xattn_w_kv, xattn_q_norm, xattn_k_norm, xattn_w_o, ffn2_norm, ffn2_w_gate, ffn2_w_up, ffn2_w_down, loss_target, m_ffn1_norm, m_ffn1_w_gate, m_ffn1_w_up, m_ffn1_w_down, m_mix_norm, m_w_in, m_mla_q_norm, m_mla_w_uq, m_mla_kv_norm, m_mla_w_ukv, m_mla_qk_norm_q, m_mla_qk_norm_k, m_ssm_a_re, m_ssm_a_im, m_ssm_log_dt, m_ssm_b_re, m_ssm_b_im, m_ssm_c_re, m_ssm_c_im, m_ssm_d, m_ssm_w_glu, m_ssm_b_glu, m_out_norm_mla, m_out_norm_ssm, m_w_o, m_xattn_norm, m_mem_norm, m_xattn_w_q, m_xattn_w_kv, m_xattn_q_norm, m_xattn_k_norm, m_xattn_w_o, m_ffn2_norm, m_ffn2_w_gate, m_ffn2_w_up, m_ffn2_w_down, v_ffn1_norm, v_ffn1_w_gate, v_ffn1_w_up, v_ffn1_w_down, v_mix_norm, v_w_in, v_mla_q_norm, v_mla_w_uq, v_mla_kv_norm, v_mla_w_ukv, v_mla_qk_norm_q, v_mla_qk_norm_k, v_ssm_a_re, v_ssm_a_im, v_ssm_log_dt, v_ssm_b_re, v_ssm_b_im, v_ssm_c_re, v_ssm_c_im, v_ssm_d, v_ssm_w_glu, v_ssm_b_glu, v_out_norm_mla, v_out_norm_ssm, v_w_o, v_xattn_norm, v_mem_norm, v_xattn_w_q, v_xattn_w_kv, v_xattn_q_norm, v_xattn_k_norm, v_xattn_w_o, v_ffn2_norm, v_ffn2_w_gate, v_ffn2_w_up, v_ffn2_w_down):
    args = dict(locals())
    c = lax.axis_index("c")
    view = lambda k, a: jnp.swapaxes(a, 0, 1) if k in TRANSPOSED_VIEW else a
    p = {k: view(k, args[k][0]) for k in WEIGHT_NAMES}
    mom = {k: view(k, args["m_" + k][0]) for k in WEIGHT_NAMES}
    var = {k: view(k, args["v_" + k][0]) for k in WEIGHT_NAMES}
    natural = {k: view(k, p[k]) for k in WEIGHT_NAMES}

    wb, small_weights, ffn2_weights, sent = _gather_weights(
        {k: (p[k] if k in FFN_NAMES else natural[k]) for k in WEIGHT_NAMES}, c)
    w = _step_params(natural)
    w["ffn1_norm"] = w["ffn1_norm"] + sent
    early, late = _GradReduce(c), _GradReduce(c)

    def on_grads(tag, g):
        if tag == "ffn2":
            return early.send(tag, [g[k] for k in FFN_NAMES[3:]])
        packed = _pack_shards(_sharded_grads(g))
        return early.scatter("ffn2", packed[0, :SUBLANES]) + early.start(tag, [packed])

    loss, dx, g = local_step(x[0], mem[0], positions[0], loss_target[0], w, wb, small_weights, ffn2_weights, on_grads)
    loss = lax.psum(loss, ("x", "y", "c"))

    sent = late.start("ffn1", [g[k] for k in FFN_NAMES[:3]])
    shards = early.finish(dx[:SUBLANES, :LANES] + sent)
    grad = dict(zip(FFN_NAMES[3:], shards[:3]))
    small_sharded = _unpack_shards(shards[3])
    grad.update({k: view(k, small_sharded[k]) for k in SMALL_SHARDED})
    repl = _problem_repl_grads(g)
    grad.update(_unpack_repl(all_reduce_small("grad_all_reduce_small", _pack_repl({k: repl[k] for k in SMALL_REPL}))))

    delta, new_m, new_v = {}, {}, {}
    small = [k for k in WEIGHT_NAMES if k not in FFN_NAMES and k not in SMALL_SHARDED]
    as2d = lambda a: a.reshape(-1, a.shape[-1])

    def update(k):
        delta[k], new_m[k], new_v[k] = adamw_big("adamw_" + k, as2d(p[k]), as2d(grad[k]), as2d(mom[k]), as2d(var[k]))

    for k in WEIGHT_NAMES:
        if k not in small and k not in FFN_NAMES[:3]:
            update(k)
    ds, nms, nvs = adamw_small("adamw_small", [as2d(p[k]) for k in small], [as2d(grad[k].reshape(p[k].shape)) for k in small],
                               [as2d(mom[k]) for k in small], [as2d(var[k]) for k in small])
    for k, d, nm, nv in zip(small, ds, nms, nvs):
        delta[k], new_m[k], new_v[k] = d, nm, nv

    grad.update(zip(FFN_NAMES[:3], late.finish(delta[FFN_NAMES[-1]])))
    for k in FFN_NAMES[:3]:
        update(k)

    shaped = lambda d, k: view(k, d.reshape(p[k].shape)).reshape(args[k].shape)
    return (loss, dx[None], *[shaped(grad[k], k) for k in WEIGHT_NAMES], *[shaped(delta[k], k) for k in WEIGHT_NAMES],
            *[shaped(new_m[k], k) for k in WEIGHT_NAMES], *[shaped(new_v[k], k) for k in WEIGHT_NAMES])
```

```python
import functools
import math

import jax
import jax.numpy as jnp
import numpy as np
from jax import lax
from jax.experimental import pallas as pl
from jax.experimental.pallas import tpu as pltpu

F32, BF16 = jnp.float32, jnp.bfloat16
EPS = 1e-6
MESH = pl.DeviceIdType.MESH

D_MODEL, D_FF = 1024, 2752
MLA_HEADS, MLA_Q_RANK, MLA_KV_RANK, MLA_NOPE, MLA_ROPE, MLA_V = 4, 384, 256, 128, 64, 128
MLA_QK = MLA_NOPE + MLA_ROPE
HEAD_PAD = 256
SSM_WIDTH, SSM_GROUP, SSM_GROUPS, SSM_STATE = 512, 16, 32, 64
SSM_LANES = SSM_GROUPS * SSM_STATE
XH, XHD = 4, 128
ROPE_THETA = 10000.0
ADAM_LR, ADAM_B1, ADAM_B2, ADAM_EPS, ADAM_WD, ADAM_STEP = 0.001, 0.9, 0.999, 1e-08, 0.01, 10
N_CHIPS, N_CORES, N_DEV = 4, 2, 8

VMEM_LIMIT_BYTES = 56 * 2**20
TOKEN_TILE = 512
FUSED_TILE = 256
FFN_WIDE_TILE = 1024
ATTN_TILE = 512
STAT_LANES = 128
SCAN_TIME_TILE = 1024
SCAN_LANE_TILE = 256
SUBLANES = 8


def _params(n_axes):
    return pltpu.CompilerParams(dimension_semantics=("arbitrary",) * n_axes, vmem_limit_bytes=VMEM_LIMIT_BYTES)


def _first(axes):
    cond = None
    for a in axes:
        c = pl.program_id(a) == 0
        cond = c if cond is None else jnp.logical_and(cond, c)
    return cond


def mm(name, xs, ws, out_dtypes, *, trans=False, adds=None, tm=TOKEN_TILE):
    rows = xs[0].shape[0]
    tm = min(tm, rows)
    n_in, n_out = len(xs), len(out_dtypes)
    pairs = [(i, j) for i in range(n_in) for j in range(n_out) if ws[i][j] is not None]
    w_list = [ws[i][j] for (i, j) in pairs]
    adds = list(adds) if adds is not None else [None] * n_out
    add_list = [a for a in adds if a is not None]
    out_cols = [None] * n_out
    for (i, j), w in zip(pairs, w_list):
        out_cols[j] = w.shape[0] if trans else w.shape[1]
    contract = (((1,), (1 if trans else 0,)), ((), ()))

    def body(*refs):
        x_refs = refs[:n_in]
        w_refs = refs[n_in:n_in + len(pairs)]
        a_refs = list(refs[n_in + len(pairs):n_in + len(pairs) + len(add_list)])
        o_refs = refs[n_in + len(pairs) + len(add_list):]
        xb = [None] * n_in
        for j in range(n_out):
            acc = None
            for p, (i, jj) in enumerate(pairs):
                if jj != j:
                    continue
                if xb[i] is None:
                    xb[i] = x_refs[i][...].astype(BF16)
                d = lax.dot_general(xb[i], w_refs[p][...].astype(BF16), contract, preferred_element_type=F32)
                acc = d if acc is None else acc + d
            if adds[j] is not None:
                acc = acc + a_refs.pop(0)[...].astype(F32)
            o_refs[j][...] = acc.astype(o_refs[j].dtype)

    in_specs = ([pl.BlockSpec((tm, x.shape[1]), lambda i: (i, 0)) for x in xs]
                + [pl.BlockSpec(w.shape, lambda i: (0, 0)) for w in w_list]
                + [pl.BlockSpec((tm, a.shape[1]), lambda i: (i, 0)) for a in add_list])
    outs = pl.pallas_call(
        body, name=name, grid=(rows // tm,), in_specs=in_specs,
        out_specs=[pl.BlockSpec((tm, n), lambda i: (i, 0)) for n in out_cols],
        out_shape=[jax.ShapeDtypeStruct((rows, n), dt) for n, dt in zip(out_cols, out_dtypes)],
        compiler_params=_params(1),
    )(*xs, *w_list, *add_list)
    return list(outs)


def mm_tn(name, xs, dys, pairs, *, tm=TOKEN_TILE):
    rows = xs[0].shape[0]
    tm = min(tm, rows)
    n_x, n_dy = len(xs), len(dys)
    contract = (((0,), (0,)), ((), ()))

    def body(*refs):
        x_refs, dy_refs, o_refs = refs[:n_x], refs[n_x:n_x + n_dy], refs[n_x + n_dy:]
        @pl.when(pl.program_id(0) == 0)
        def _():
            for o in o_refs:
                o[...] = jnp.zeros_like(o)

        for k, (i, js) in enumerate(pairs):
            dy = None
            for j in js:
                t = dy_refs[j][...].astype(F32)
                dy = t if dy is None else dy + t
            o_refs[k][...] += lax.dot_general(x_refs[i][...].astype(BF16), dy.astype(BF16), contract,
                                              preferred_element_type=F32)

    shapes = [(xs[i].shape[1], dys[js[0]].shape[1]) for (i, js) in pairs]
    outs = pl.pallas_call(
        body, name=name, grid=(rows // tm,),
        in_specs=[pl.BlockSpec((tm, a.shape[1]), lambda i: (i, 0)) for a in (*xs, *dys)],
        out_specs=[pl.BlockSpec(s, lambda i: (0, 0)) for s in shapes],
        out_shape=[jax.ShapeDtypeStruct(s, F32) for s in shapes],
        compiler_params=_params(1),
    )(*xs, *dys)
    return list(outs)


class In:
    def __init__(self, arr, block=None, imap=None, *, diff=False, acc=None, grad=None, weight=False):
        self.arr, self.block, self.imap, self.diff, self.acc, self.grad = arr, block, imap, diff, acc, grad
        self.weight = weight

    def spec(self):
        return pl.BlockSpec(memory_space=pltpu.VMEM) if self.weight else pl.BlockSpec(self.block, self.imap)


class Out:
    def __init__(self, shape, dtype, block, imap):
        self.shape, self.dtype, self.block, self.imap = shape, dtype, block, imap

    def spec(self):
        return pl.BlockSpec(self.block, self.imap)


class Wt:
    def __init__(self, ref, zeros=None):
        self.ref, self.zeros = ref, zeros


@jax.custom_vjp
def _wdot(a, w, z):
    return jnp.dot(a.astype(BF16), w, preferred_element_type=F32)


def _wdot_fwd(a, w, z):
    return _wdot(a, w, z), (a, w)


def _wdot_bwd(res, g):
    a, w = res
    gb = g.astype(BF16)
    da = lax.dot_general(gb, w, (((1,), (1,)), ((), ())), preferred_element_type=F32)
    dz = lax.dot_general(a.astype(BF16), gb, (((0,), (0,)), ((), ())), preferred_element_type=F32)
    return da, None, dz


_wdot.defvjp(_wdot_fwd, _wdot_bwd)


def wdot(a, wt, head=None):
    w = wt.ref[...] if head is None else wt.ref[head]
    if wt.zeros is None:
        return jnp.dot(a.astype(BF16), w, preferred_element_type=F32)
    return _wdot(a, w, wt.zeros[0 if head is None else head])


def seg_fwd(name, f, grid, ins, outs):
    n_in = len(ins)

    def body(*refs):
        res = f(*[Wt(r) if i.weight else r[...] for i, r in zip(ins, refs[:n_in])])
        for o_ref, r in zip(refs[n_in:], res):
            o_ref[...] = r.astype(o_ref.dtype)

    res = pl.pallas_call(
        body, name=name, grid=grid, in_specs=[i.spec() for i in ins], out_specs=[o.spec() for o in outs],
        out_shape=[jax.ShapeDtypeStruct(o.shape, o.dtype) for o in outs], compiler_params=_params(len(grid)),
    )(*[i.arr for i in ins])
    return list(res)


def seg_bwd(name, f, grid, ins, outs, cts, adds=None):
    n_in, n_ct = len(ins), len(cts)
    grad_idx = [k for k, i in enumerate(ins) if i.diff or i.weight]
    adds = adds or {}
    add_keys = sorted(adds)
    add_list = [adds[k] for k in add_keys]
    heads = {k: (ins[k].arr.shape[0] if ins[k].arr.ndim == 3 else 1) for k in grad_idx if ins[k].weight}

    def body(*refs):
        in_refs, ct_refs = refs[:n_in], refs[n_in:n_in + n_ct]
        add_refs = dict(zip(add_keys, refs[n_in + n_ct:n_in + n_ct + len(add_list)]))
        g_refs = dict(zip(grad_idx, refs[n_in + n_ct + len(add_list):]))
        vals = [None if i.weight else r[...] for i, r in zip(ins, in_refs)]
        primals, owner = [], []
        for k in grad_idx:
            if ins[k].weight:
                for h in range(heads[k]):
                    primals.append(jnp.zeros(ins[k].arr.shape[-2:], F32))
                    owner.append((k, h))
            else:
                primals.append(vals[k].astype(F32))
                owner.append((k, None))

        def g(*dv):
            full = list(vals)
            zeros = {}
            for (k, h), v in zip(owner, dv):
                if h is None:
                    full[k] = v
                else:
                    zeros.setdefault(k, []).append(v)
            for k, z in zeros.items():
                full[k] = Wt(in_refs[k], z)
            return tuple(f(*full))

        _, pull = jax.vjp(g, *primals)
        grads = pull(tuple(c[...].astype(F32) for c in ct_refs))
        for k in grad_idx:
            if ins[k].weight or ins[k].acc is not None:
                @pl.when(_first(range(len(grid)) if ins[k].weight else ins[k].acc))
                def _(k=k):
                    g_refs[k][...] = jnp.zeros_like(g_refs[k])

        for (k, h), gr in zip(owner, grads):
            if ins[k].weight:
                if ins[k].arr.ndim == 3:
                    g_refs[k][h] += gr
                else:
                    g_refs[k][...] += gr
                continue
            if k in add_refs:
                gr = gr + add_refs[k][...].astype(F32)
            if ins[k].acc is None:
                g_refs[k][...] = gr.astype(g_refs[k].dtype)
            else:
                g_refs[k][...] += gr

    g_specs, g_shapes = [], []
    for k in grad_idx:
        i = ins[k]
        if i.weight:
            g_specs.append(pl.BlockSpec(memory_space=pltpu.VMEM))
            g_shapes.append(jax.ShapeDtypeStruct(i.arr.shape, F32))
            continue
        shape, block, imap = i.grad if i.grad is not None else (i.arr.shape, i.block, i.imap)
        g_specs.append(pl.BlockSpec(block, imap))
        g_shapes.append(jax.ShapeDtypeStruct(shape, F32))
    in_specs = ([i.spec() for i in ins] + [o.spec() for o in outs]
                + [pl.BlockSpec(ins[k].block, ins[k].imap) for k in add_keys])
    res = pl.pallas_call(
        body, name=name, grid=grid, in_specs=in_specs, out_specs=g_specs, out_shape=g_shapes,
        compiler_params=_params(len(grid)),
    )(*[i.arr for i in ins], *cts, *add_list)
    return list(res)


def _rms(x, g):
    return x * lax.rsqrt(jnp.mean(x * x, axis=-1, keepdims=True) + EPS) * g


@jax.custom_vjp
def _bdot_nt(a, b):
    return lax.dot_general(a.astype(BF16), b.astype(BF16), (((1,), (1,)), ((), ())), preferred_element_type=F32)


def _bdot_nt_fwd(a, b):
    return _bdot_nt(a, b), (a, b)


def _bdot_nt_bwd(res, g):
    a, b = res
    gb = g.astype(BF16)
    da = lax.dot_general(gb, b.astype(BF16), (((1,), (0,)), ((), ())), preferred_element_type=F32)
    db = lax.dot_general(gb, a.astype(BF16), (((0,), (0,)), ((), ())), preferred_element_type=F32)
    return da, db


_bdot_nt.defvjp(_bdot_nt_fwd, _bdot_nt_bwd)


@jax.custom_vjp
def _bdot_nn(a, b):
    return lax.dot_general(a.astype(BF16), b.astype(BF16), (((1,), (0,)), ((), ())), preferred_element_type=F32)


def _bdot_nn_fwd(a, b):
    return _bdot_nn(a, b), (a, b)


def _bdot_nn_bwd(res, g):
    a, b = res
    gb = g.astype(BF16)
    da = lax.dot_general(gb, b.astype(BF16), (((1,), (1,)), ((), ())), preferred_element_type=F32)
    db = lax.dot_general(a.astype(BF16), gb, (((0,), (0,)), ((), ())), preferred_element_type=F32)
    return da, db


_bdot_nn.defvjp(_bdot_nn_fwd, _bdot_nn_bwd)


@functools.partial(jax.custom_vjp, nondiff_argnums=(1,))
def _lane_roll(x, shift):
    return pltpu.roll(x, shift, 1)


def _lane_roll_fwd(x, shift):
    return pltpu.roll(x, shift, 1), None


def _lane_roll_bwd(shift, _, g):
    return (pltpu.roll(g, (g.shape[1] - shift) % g.shape[1], 1),)


_lane_roll.defvjp(_lane_roll_fwd, _lane_roll_bwd)


def _hp_dot(a, b):
    return jnp.dot(a, b, precision=lax.Precision.HIGHEST, preferred_element_type=F32)


NT_DIMS = (((1,), (1,)), ((), ()))
TN_DIMS = (((0,), (0,)), ((), ()))


def ffn_fwd(name, x, gain, wg, wu, wd, target=None, tm=FFN_WIDE_TILE):
    S, D = x.shape
    tm = min(tm, S)
    nsh, Fs, _ = wg.shape
    with_loss = target is not None

    def body(*refs):
        if with_loss:
            x_ref, gain_ref, wg_ref, wu_ref, wd_ref, t_ref, xo_ref, h_ref, g_ref, u_ref, part_ref, acc = refs
        else:
            x_ref, gain_ref, wg_ref, wu_ref, wd_ref, xo_ref, h_ref, g_ref, u_ref, acc = refs
        j = pl.program_id(1)

        @pl.when(j == 0)
        def _():
            h_ref[...] = _rms(x_ref[...], gain_ref[...]).astype(BF16)
            acc[...] = jnp.zeros_like(acc)

        h = h_ref[...]
        g = lax.dot_general(h, wg_ref[...], NT_DIMS, preferred_element_type=F32)
        u = lax.dot_general(h, wu_ref[...], NT_DIMS, preferred_element_type=F32)
        g_ref[...] = g.astype(BF16)
        u_ref[...] = u.astype(BF16)
        a = g * jax.nn.sigmoid(g) * u
        acc[...] += jnp.dot(a.astype(BF16), wd_ref[...], preferred_element_type=F32)

        @pl.when(j == nsh - 1)
        def _():
            y = x_ref[...] + 0.5 * acc[...]
            if with_loss:
                err = y - t_ref[...]
                xo_ref[...] = err * (1.0 / D)
                part_ref[...] = jnp.full(part_ref.shape, 0.5 * jnp.sum(jnp.mean(err * err, axis=-1)), F32)
            else:
                xo_ref[...] = y

    rows = pl.BlockSpec((tm, D), lambda i, j: (i, 0))
    wspec = pl.BlockSpec((None, Fs, D), lambda i, j: (j, 0, 0))
    act = pl.BlockSpec((None, tm, Fs), lambda i, j: (j, i, 0))
    in_specs, args = [rows, pl.BlockSpec((1, D), lambda i, j: (0, 0)), wspec, wspec, wspec], [x, gain, wg, wu, wd]
    out_specs = [rows, rows, act, act]
    out_shape = [jax.ShapeDtypeStruct((S, D), F32), jax.ShapeDtypeStruct((S, D), BF16),
                 jax.ShapeDtypeStruct((nsh, S, Fs), BF16), jax.ShapeDtypeStruct((nsh, S, Fs), BF16)]
    if with_loss:
        in_specs.append(rows)
        args.append(target)
        out_specs.append(pl.BlockSpec((SUBLANES, 128), lambda i, j: (i, 0)))
        out_shape.append(jax.ShapeDtypeStruct((S // tm * SUBLANES, 128), F32))
    return pl.pallas_call(
        body, name=name, grid=(S // tm, nsh), in_specs=in_specs, out_specs=out_specs, out_shape=out_shape,
        scratch_shapes=[pltpu.VMEM((tm, D), F32)], compiler_params=_params(2),
    )(*args)


def ffn_bwd_act(name, dxo, x, gain, g, u, wg, wu, wd, tm=TOKEN_TILE):
    S, D = x.shape
    tm = min(tm, S)
    nsh, Fs, _ = wg.shape

    def body(dxo_ref, x_ref, gain_ref, g_ref, u_ref, wg_ref, wu_ref, wd_ref, dx_ref, dgain_ref, dg_ref, du_ref, dd, dh):
        i, j = pl.program_id(0), pl.program_id(1)

        @pl.when(j == 0)
        def _():
            dd[...] = (0.5 * dxo_ref[...]).astype(BF16)
            dh[...] = jnp.zeros_like(dh)

        da = lax.dot_general(dd[...], wd_ref[...], NT_DIMS, preferred_element_type=F32)
        gf, uf = g_ref[...].astype(F32), u_ref[...].astype(F32)
        sig = jax.nn.sigmoid(gf)
        dgv = (da * uf * (sig * (1.0 + gf * (1.0 - sig)))).astype(BF16)
        duv = (da * (gf * sig)).astype(BF16)
        dg_ref[...] = dgv
        du_ref[...] = duv
        dh[...] += (jnp.dot(dgv, wg_ref[...], preferred_element_type=F32)
                    + jnp.dot(duv, wu_ref[...], preferred_element_type=F32))

        @pl.when(j == nsh - 1)
        def _():
            xv = x_ref[...]
            r = lax.rsqrt(jnp.mean(xv * xv, axis=-1, keepdims=True) + EPS)
            xhat = xv * r
            dhv = dh[...]
            dxn = dhv * gain_ref[...]
            dx_ref[...] = dxo_ref[...] + r * (dxn - xhat * jnp.mean(dxn * xhat, axis=-1, keepdims=True))
            part = jnp.sum(dhv * xhat, axis=0, keepdims=True)

            @pl.when(i == 0)
            def _():
                dgain_ref[...] = part

            @pl.when(i != 0)
            def _():
                dgain_ref[...] += part

    return pl.pallas_call(
        body, name=name, grid=(S // tm, nsh),
        in_specs=[pl.BlockSpec((tm, D), lambda i, j: (i, 0)), pl.BlockSpec((tm, D), lambda i, j: (i, 0)),
                  pl.BlockSpec((1, D), lambda i, j: (0, 0)),
                  pl.BlockSpec((None, tm, Fs), lambda i, j: (j, i, 0)), pl.BlockSpec((None, tm, Fs), lambda i, j: (j, i, 0)),
                  pl.BlockSpec((None, Fs, D), lambda i, j: (j, 0, 0)), pl.BlockSpec((None, Fs, D), lambda i, j: (j, 0, 0)),
                  pl.BlockSpec((None, Fs, D), lambda i, j: (j, 0, 0))],
        out_specs=[pl.BlockSpec((tm, D), lambda i, j: (i, 0)), pl.BlockSpec((1, D), lambda i, j: (0, 0)),
                   pl.BlockSpec((None, tm, Fs), lambda i, j: (j, i, 0)), pl.BlockSpec((None, tm, Fs), lambda i, j: (j, i, 0))],
        out_shape=[jax.ShapeDtypeStruct((S, D), F32), jax.ShapeDtypeStruct((1, D), F32),
                   jax.ShapeDtypeStruct((nsh, S, Fs), BF16), jax.ShapeDtypeStruct((nsh, S, Fs), BF16)],
        scratch_shapes=[pltpu.VMEM((tm, D), BF16), pltpu.VMEM((tm, D), F32)], compiler_params=_params(2),
    )(dxo, x, gain, g, u, wg, wu, wd)


def ffn_bwd_w(name, h, dxo, g, u, dg, du, tm=FFN_WIDE_TILE):
    S, D = h.shape
    tm = min(tm, S)
    nsh, _, Fs = g.shape

    def body(h_ref, dxo_ref, g_ref, u_ref, dg_ref, du_ref, dwg_ref, dwu_ref, dwd_ref):
        i = pl.program_id(1)
        gf, uf = g_ref[...].astype(F32), u_ref[...].astype(F32)
        a = (gf * jax.nn.sigmoid(gf) * uf).astype(BF16)
        dd = (0.5 * dxo_ref[...]).astype(BF16)
        hv = h_ref[...]

        @pl.when(i == 0)
        def _():
            dwg_ref[...] = jnp.zeros_like(dwg_ref)
            dwu_ref[...] = jnp.zeros_like(dwu_ref)
            dwd_ref[...] = jnp.zeros_like(dwd_ref)

        dwg_ref[...] += lax.dot_general(dg_ref[...], hv, TN_DIMS, preferred_element_type=F32)
        dwu_ref[...] += lax.dot_general(du_ref[...], hv, TN_DIMS, preferred_element_type=F32)
        dwd_ref[...] += lax.dot_general(a, dd, TN_DIMS, preferred_element_type=F32)

    act = pl.BlockSpec((None, tm, Fs), lambda j, i: (j, i, 0))
    wspec = pl.BlockSpec((None, Fs, D), lambda j, i: (j, 0, 0))
    return pl.pallas_call(
        body, name=name, grid=(nsh, S // tm),
        in_specs=[pl.BlockSpec((tm, D), lambda j, i: (i, 0)), pl.BlockSpec((tm, D), lambda j, i: (i, 0)), act, act, act, act],
        out_specs=[wspec, wspec, wspec], out_shape=[jax.ShapeDtypeStruct((nsh, Fs, D), F32)] * 3,
        compiler_params=_params(2),
    )(h, dxo, g, u, dg, du)


NEG_BIG = -1e30


def _causal_pairs(n, by_key):
    pairs = [(qi, ki) for qi in range(n) for ki in range(qi + 1)]
    if by_key:
        pairs.sort(key=lambda p: (p[1], p[0]))
    return jnp.asarray([p[0] for p in pairs], jnp.int32), jnp.asarray([p[1] for p in pairs], jnp.int32)


def _scores(q, k, masked):
    s = lax.dot_general(q, k, (((1,), (1,)), ((), ())), preferred_element_type=F32)
    if masked:
        row = lax.broadcasted_iota(jnp.int32, s.shape, 0)
        col = lax.broadcasted_iota(jnp.int32, s.shape, 1)
        s = jnp.where(row >= col, s, NEG_BIG)
    return s


def attn_fwd(name, q, k, v, t=ATTN_TILE):
    S, Dk = q.shape[0], HEAD_PAD
    H = q.shape[1] // Dk
    Dv = v.shape[1] // H
    qt, kt = _causal_pairs(S // t, by_key=False)

    def body(qt_ref, kt_ref, q_ref, k_ref, v_ref, o_ref, lse_ref, m_sc, l_sc, acc):
        qi, ki = qt_ref[pl.program_id(1)], kt_ref[pl.program_id(1)]

        @pl.when(ki == 0)
        def _():
            m_sc[...] = jnp.full_like(m_sc, NEG_BIG)
            l_sc[...] = jnp.zeros_like(l_sc)
            acc[...] = jnp.zeros_like(acc)

        def step(masked):
            s = _scores(q_ref[...], k_ref[...], masked)
            m_prev = m_sc[...]
            m_next = jnp.maximum(m_prev, jnp.max(s, axis=-1, keepdims=True))
            alpha = jnp.exp(m_prev - m_next)
            p = jnp.exp(s - jnp.tile(m_next, (1, t // STAT_LANES)))
            l_sc[...] = alpha * l_sc[...] + jnp.sum(p, axis=-1, keepdims=True)
            acc[...] = alpha * acc[...] + jnp.dot(p.astype(BF16), v_ref[...].astype(BF16), preferred_element_type=F32)
            m_sc[...] = m_next

        @pl.when(ki < qi)
        def _():
            step(False)

        @pl.when(ki == qi)
        def _():
            step(True)
            o_ref[...] = acc[...] / l_sc[...]
            lse_ref[...] = m_sc[...] + jnp.log(l_sc[...])

    stat = pltpu.VMEM((t, STAT_LANES), F32)
    return pl.pallas_call(
        body, name=name,
        grid_spec=pltpu.PrefetchScalarGridSpec(
            num_scalar_prefetch=2, grid=(H, qt.shape[0]),
            in_specs=[pl.BlockSpec((t, Dk), lambda h, s, qt, kt: (qt[s], h)),
                      pl.BlockSpec((t, Dk), lambda h, s, qt, kt: (kt[s], h)),
                      pl.BlockSpec((t, Dv), lambda h, s, qt, kt: (kt[s], h))],
            out_specs=[pl.BlockSpec((t, Dv), lambda h, s, qt, kt: (qt[s], h)),
                       pl.BlockSpec((None, t, STAT_LANES), lambda h, s, qt, kt: (h, qt[s], 0))],
            scratch_shapes=[stat, stat, pltpu.VMEM((t, Dv), F32)]),
        out_shape=[jax.ShapeDtypeStruct((S, H * Dv), F32), jax.ShapeDtypeStruct((H, S, STAT_LANES), F32)],
        compiler_params=_params(2),
    )(qt, kt, q, k, v)


def attn_bwd(name, q, k, v, do, o, lse, t=ATTN_TILE):
    S, Dk = q.shape[0], HEAD_PAD
    H = q.shape[1] // Dk
    Dv = v.shape[1] // H
    qt, kt = _causal_pairs(S // t, by_key=True)
    tn_dims = (((0,), (0,)), ((), ()))

    def body(qt_ref, kt_ref, q_ref, k_ref, v_ref, do_ref, o_ref, lse_ref, dq_ref, dk_ref, dv_ref):
        step_id = pl.program_id(1)
        qi, ki = qt_ref[step_id], kt_ref[step_id]

        @pl.when(step_id == 0)
        def _():
            dq_ref[...] = jnp.zeros_like(dq_ref)

        def step(masked):
            s = _scores(q_ref[...], k_ref[...], masked)
            reps = (1, t // STAT_LANES)
            p = jnp.exp(s - jnp.tile(lse_ref[...], reps))
            dov = do_ref[...]
            delta = jnp.broadcast_to(jnp.sum(dov * o_ref[...], axis=-1, keepdims=True), (t, STAT_LANES))
            dob = dov.astype(BF16)
            dp = lax.dot_general(dob, v_ref[...].astype(BF16), (((1,), (1,)), ((), ())), preferred_element_type=F32)
            ds = (p * (dp - jnp.tile(delta, reps))).astype(BF16)
            pdv = lax.dot_general(p.astype(BF16), dob, tn_dims, preferred_element_type=F32)
            pdk = lax.dot_general(ds, q_ref[...], tn_dims, preferred_element_type=F32)
            rows = pl.ds(pl.multiple_of(qi * t, t), t)
            dq_ref[rows, :] += jnp.dot(ds, k_ref[...], preferred_element_type=F32)
            return pdk, pdv

        @pl.when(ki == qi)
        def _():
            dk_ref[...] = jnp.zeros_like(dk_ref)
            dv_ref[...] = jnp.zeros_like(dv_ref)

        def accumulate(masked):
            pdk, pdv = step(masked)
            dk_ref[...] += pdk
            dv_ref[...] += pdv

        @pl.when(ki == qi)
        def _():
            accumulate(True)

        @pl.when(ki < qi)
        def _():
            accumulate(False)

    qrow = lambda h, s, qt, kt: (qt[s], h)
    krow = lambda h, s, qt, kt: (kt[s], h)
    return pl.pallas_call(
        body, name=name,
        grid_spec=pltpu.PrefetchScalarGridSpec(
            num_scalar_prefetch=2, grid=(H, qt.shape[0]),
            in_specs=[pl.BlockSpec((t, Dk), qrow), pl.BlockSpec((t, Dk), krow), pl.BlockSpec((t, Dv), krow),
                      pl.BlockSpec((t, Dv), qrow), pl.BlockSpec((t, Dv), qrow),
                      pl.BlockSpec((None, t, STAT_LANES), lambda h, s, qt, kt: (h, qt[s], 0))],
            out_specs=[pl.BlockSpec((S, Dk), lambda h, s, qt, kt: (0, h)), pl.BlockSpec((t, Dk), krow),
                       pl.BlockSpec((t, Dv), krow)]),
        out_shape=[jax.ShapeDtypeStruct((S, H * Dk), F32), jax.ShapeDtypeStruct((S, H * Dk), F32),
                   jax.ShapeDtypeStruct((S, H * Dv), F32)],
        compiler_params=_params(2),
    )(qt, kt, q, k, v, do, o, lse)


def _cmul(ar, ai, br, bi):
    return ar * br - ai * bi, ar * bi + ai * br


def _scan_tile(x_r, x_i, ar_ref, ai_ref, cr_sc, ci_sc, *, reverse, first, states=None):
    tc, lanes = x_r.shape
    nblk, lb = tc // SUBLANES, SCAN_LANE_TILE
    with_da = states is not None
    if with_da:
        xr_all, xi_all, pr_all, pi_all, dar_all, dai_all, chunk = states

    @pl.when(first)
    def _():
        cr_sc[...] = jnp.zeros_like(cr_sc)
        ci_sc[...] = jnp.zeros_like(ci_sc)
        if with_da:
            dar_all[...] = jnp.zeros_like(dar_all)
            dai_all[...] = jnp.zeros_like(dai_all)

    row = lax.broadcasted_iota(jnp.int32, (SUBLANES, lb), 0)
    for l0 in range(0, lanes, lb):
        _scan_lanes(x_r.at[:, pl.ds(l0, lb)], x_i.at[:, pl.ds(l0, lb)], ar_ref[0:1, pl.ds(l0, lb)],
                    ai_ref[0:1, pl.ds(l0, lb)], cr_sc.at[:, pl.ds(l0, lb)], ci_sc.at[:, pl.ds(l0, lb)], row, reverse,
                    nblk, None if not with_da else tuple(r.at[:, pl.ds(l0, lb)] for r in states[:6]) + (chunk,))


def _scan_lanes(x_r, x_i, a1r, a1i, cr_sc, ci_sc, row, reverse, nblk, states):
    lb = x_r.shape[1]
    with_da = states is not None
    if with_da:
        xr_ref, xi_ref, pr_ref, pi_ref, dar_ref, dai_ref, chunk = states
    if reverse:
        a1i = -a1i
    a2r, a2i = _cmul(a1r, a1i, a1r, a1i)
    a4r, a4i = _cmul(a2r, a2i, a2r, a2i)
    pw_r, pw_i = jnp.zeros((SUBLANES, lb), F32), jnp.zeros((SUBLANES, lb), F32)
    cur_r, cur_i = a1r, a1i
    for e in range(SUBLANES):
        r_at = (SUBLANES - 1 - e) if reverse else e
        pw_r = jnp.where(row == r_at, cur_r, pw_r)
        pw_i = jnp.where(row == r_at, cur_i, pw_i)
        cur_r, cur_i = _cmul(cur_r, cur_i, a1r, a1i)
    steps = []
    for d, pr, pi in ((1, a1r, a1i), (2, a2r, a2i), (4, a4r, a4i)):
        keep = (row < SUBLANES - d) if reverse else (row >= d)
        steps.append((d, jnp.where(keep, pr, 0.0), jnp.where(keep, pi, 0.0)))

    def block(jb, carry):
        if with_da:
            cr, ci, acc_r, acc_i = carry
        else:
            cr, ci = carry
        idx = (nblk - 1 - jb) if reverse else jb
        r0 = pl.multiple_of(idx * SUBLANES, SUBLANES)
        xr = x_r[pl.ds(r0, SUBLANES), :]
        xi = x_i[pl.ds(r0, SUBLANES), :]
        for d, pr, pi in steps:
            shift = SUBLANES - d if reverse else d
            sr, si = pltpu.roll(xr, shift, 0), pltpu.roll(xi, shift, 0)
            xr, xi = xr + pr * sr - pi * si, xi + pr * si + pi * sr
        xr, xi = xr + pw_r * cr - pw_i * ci, xi + pw_r * ci + pw_i * cr
        x_r[pl.ds(r0, SUBLANES), :] = xr
        x_i[pl.ds(r0, SUBLANES), :] = xi
        edge = 0 if reverse else SUBLANES - 1
        cr, ci = xr[edge:edge + 1, :], xi[edge:edge + 1, :]
        if not with_da:
            return cr, ci
        fr = xr_ref[pl.ds(r0, SUBLANES), :]
        fi = xi_ref[pl.ds(r0, SUBLANES), :]
        rp = pl.multiple_of(jnp.maximum(idx - 1, 0) * SUBLANES, SUBLANES)
        inside = idx > 0
        before_r = jnp.where(inside, xr_ref[pl.ds(rp, SUBLANES), :], pr_ref[...])
        before_i = jnp.where(inside, xi_ref[pl.ds(rp, SUBLANES), :], pi_ref[...])
        live = jnp.where(jnp.logical_or(inside, chunk > 0), 1.0, 0.0)
        last_r = before_r[SUBLANES - 1:SUBLANES, :] * live
        last_i = before_i[SUBLANES - 1:SUBLANES, :] * live
        pvr = jnp.where(row == 0, last_r, pltpu.roll(fr, 1, 0))
        pvi = jnp.where(row == 0, last_i, pltpu.roll(fi, 1, 0))
        acc_r = acc_r + xr * pvr + xi * pvi
        acc_i = acc_i + xi * pvr - xr * pvi
        return cr, ci, acc_r, acc_i

    init = (cr_sc[...], ci_sc[...])
    if with_da:
        init = init + (jnp.zeros((SUBLANES, lb), F32), jnp.zeros((SUBLANES, lb), F32))
    fin = lax.fori_loop(0, nblk, block, init)
    cr_sc[...] = fin[0]
    ci_sc[...] = fin[1]
    if with_da:
        dar_ref[...] += fin[2]
        dai_ref[...] += fin[3]


SSM_BLOCKS = 4
BLOCK_CH = SSM_WIDTH // SSM_BLOCKS
PREP_LANES = SSM_LANES // SSM_BLOCKS


def ssm_fwd(name, u, bb_r, bb_i, cb_r, ncb_i, a_r8, a_i8):
    S = u.shape[0]
    tc = min(SCAN_TIME_TILE, S)

    def body(u_ref, bbr_ref, bbi_ref, cbr_ref, ncbi_ref, ar_ref, ai_ref, xr_ref, xi_ref, y_ref, cr_sc, ci_sc):
        ub = u_ref[...].astype(BF16)
        xr_ref[...] = jnp.dot(ub, bbr_ref[...], preferred_element_type=F32)
        xi_ref[...] = jnp.dot(ub, bbi_ref[...], preferred_element_type=F32)
        _scan_tile(xr_ref, xi_ref, ar_ref, ai_ref, cr_sc, ci_sc, reverse=False, first=pl.program_id(1) == 0)
        y_ref[...] = (lax.dot_general(xr_ref[...].astype(BF16), cbr_ref[...], NT_DIMS, preferred_element_type=F32)
                      + lax.dot_general(xi_ref[...].astype(BF16), ncbi_ref[...], NT_DIMS, preferred_element_type=F32))

    ch = pl.BlockSpec((tc, BLOCK_CH), lambda b, t: (t, b))
    st = pl.BlockSpec((tc, PREP_LANES), lambda b, t: (t, b))
    wt = pl.BlockSpec((None, BLOCK_CH, PREP_LANES), lambda b, t: (b, 0, 0))
    par = pl.BlockSpec((SUBLANES, PREP_LANES), lambda b, t: (0, b))
    return pl.pallas_call(
        body, name=name, grid=(SSM_BLOCKS, S // tc), in_specs=[ch, wt, wt, wt, wt, par, par], out_specs=[st, st, ch],
        out_shape=[jax.ShapeDtypeStruct((S, SSM_LANES), F32), jax.ShapeDtypeStruct((S, SSM_LANES), F32),
                   jax.ShapeDtypeStruct((S, SSM_WIDTH), F32)],
        scratch_shapes=[pltpu.VMEM((1, PREP_LANES), F32), pltpu.VMEM((1, PREP_LANES), F32)], compiler_params=_params(2),
    )(u, bb_r, bb_i, cb_r, ncb_i, a_r8, a_i8)


def ssm_bwd(name, dy, du_add, u, xs_r, xs_i, bb_r, bb_i, cb_r, ncb_i, a_r8, a_i8):
    S = u.shape[0]
    tc = min(SCAN_TIME_TILE, S)
    nt = S // tc

    def body(dy_ref, dua_ref, u_ref, xr_ref, xi_ref, pr_ref, pi_ref, bbr_ref, bbi_ref, cbr_ref, ncbi_ref, ar_ref, ai_ref,
             du_ref, dbbr_ref, dbbi_ref, dcbr_ref, dncbi_ref, dar_ref, dai_ref, lr_sc, li_sc, cr_sc, ci_sc):
        t = pl.program_id(1)
        first = t == 0

        @pl.when(first)
        def _():
            for r in (dbbr_ref, dbbi_ref, dcbr_ref, dncbi_ref):
                r[...] = jnp.zeros_like(r)

        dyb = dy_ref[...].astype(BF16)
        lr_sc[...] = jnp.dot(dyb, cbr_ref[...], preferred_element_type=F32)
        li_sc[...] = jnp.dot(dyb, ncbi_ref[...], preferred_element_type=F32)
        _scan_tile(lr_sc, li_sc, ar_ref, ai_ref, cr_sc, ci_sc, reverse=True, first=first,
                   states=(xr_ref, xi_ref, pr_ref, pi_ref, dar_ref, dai_ref, nt - 1 - t))
        lrb, lib = lr_sc[...].astype(BF16), li_sc[...].astype(BF16)
        du_ref[...] = (lax.dot_general(lrb, bbr_ref[...], NT_DIMS, preferred_element_type=F32)
                       + lax.dot_general(lib, bbi_ref[...], NT_DIMS, preferred_element_type=F32) + dua_ref[...])
        ub = u_ref[...].astype(BF16)
        dbbr_ref[...] += lax.dot_general(ub, lrb, TN_DIMS, preferred_element_type=F32)
        dbbi_ref[...] += lax.dot_general(ub, lib, TN_DIMS, preferred_element_type=F32)
        dcbr_ref[...] += lax.dot_general(dyb, xr_ref[...].astype(BF16), TN_DIMS, preferred_element_type=F32)
        dncbi_ref[...] += lax.dot_general(dyb, xi_ref[...].astype(BF16), TN_DIMS, preferred_element_type=F32)

    ch = pl.BlockSpec((tc, BLOCK_CH), lambda b, t: (nt - 1 - t, b))
    st = pl.BlockSpec((tc, PREP_LANES), lambda b, t: (nt - 1 - t, b))
    prev = pl.BlockSpec((SUBLANES, PREP_LANES), lambda b, t: (jnp.maximum((nt - 1 - t) * (tc // SUBLANES) - 1, 0), b))
    wt = pl.BlockSpec((None, BLOCK_CH, PREP_LANES), lambda b, t: (b, 0, 0))
    par = pl.BlockSpec((SUBLANES, PREP_LANES), lambda b, t: (0, b))
    blk = jax.ShapeDtypeStruct((SSM_BLOCKS, BLOCK_CH, PREP_LANES), F32)
    rows8 = jax.ShapeDtypeStruct((SUBLANES, SSM_LANES), F32)
    return pl.pallas_call(
        body, name=name, grid=(SSM_BLOCKS, nt), in_specs=[ch, ch, ch, st, st, prev, prev, wt, wt, wt, wt, par, par],
        out_specs=[ch, wt, wt, wt, wt, par, par],
        out_shape=[jax.ShapeDtypeStruct((S, SSM_WIDTH), F32), blk, blk, blk, blk, rows8, rows8],
        scratch_shapes=[pltpu.VMEM((tc, PREP_LANES), F32), pltpu.VMEM((tc, PREP_LANES), F32),
                        pltpu.VMEM((1, PREP_LANES), F32), pltpu.VMEM((1, PREP_LANES), F32)],
        compiler_params=_params(2),
    )(dy, du_add, u, xs_r, xs_i, xs_r, xs_i, bb_r, bb_i, cb_r, ncb_i, a_r8, a_i8)


def _ssm_prep_f(a_re, a_im, log_dt, bt_re, bt_im, c_re, c_im):
    first_group = pl.program_id(0) * (SSM_GROUPS // SSM_BLOCKS)
    iota = lambda shape, d: lax.broadcasted_iota(jnp.int32, shape, d)
    grp_of_row = lambda shape: iota(shape, 0) >> int(math.log2(SSM_GROUP))
    grp_of_lane = lambda shape: iota(shape, 1) >> int(math.log2(SSM_STATE))
    rep = (grp_of_row((BLOCK_CH, SSM_GROUPS)) + first_group == iota((BLOCK_CH, SSM_GROUPS), 1)).astype(F32)
    til = ((iota((SSM_STATE, PREP_LANES), 1) & (SSM_STATE - 1)) == iota((SSM_STATE, PREP_LANES), 0)).astype(F32)
    m_rows = (grp_of_row((BLOCK_CH, PREP_LANES)) == grp_of_lane((BLOCK_CH, PREP_LANES))).astype(F32)
    m_grp = (iota((SSM_GROUPS, PREP_LANES), 0) == grp_of_lane((SSM_GROUPS, PREP_LANES)) + first_group).astype(F32)
    dt = jnp.exp(log_dt)
    decay = jnp.exp(a_re * dt)
    ar = decay * jnp.cos(a_im * dt)
    ai = decay * jnp.sin(a_im * dt)
    den = a_re * a_re + a_im * a_im
    nr = ar - 1.0
    coef_r = (nr * a_re + ai * a_im) / den
    coef_i = (ai * a_re - nr * a_im) / den
    cr, ci = _hp_dot(rep, coef_r), _hp_dot(rep, coef_i)
    bb_r = cr * bt_re - ci * bt_im
    bb_i = cr * bt_im + ci * bt_re
    big = lambda m: _hp_dot(m, til) * m_rows
    lanes = lambda m: jnp.broadcast_to(jnp.sum(_hp_dot(m, til) * m_grp, axis=0, keepdims=True), (SUBLANES, PREP_LANES))
    return lanes(ar), lanes(ai), big(bb_r), big(bb_i), big(c_re), -big(c_im)


def _whole(arr, **kw):
    nd = arr.ndim
    return In(arr, arr.shape, lambda *_: (0,) * nd, **kw)


def _adamw_math(w, g, m, v):
    m = ADAM_B1 * m + (1.0 - ADAM_B1) * g
    v = ADAM_B2 * v + (1.0 - ADAM_B2) * (g * g)
    m_hat = m / (1.0 - ADAM_B1 ** ADAM_STEP)
    v_hat = v / (1.0 - ADAM_B2 ** ADAM_STEP)
    delta = -ADAM_LR * (m_hat / (jnp.sqrt(v_hat) + ADAM_EPS) + ADAM_WD * w)
    return delta, m, v


def adamw_big(name, w, g, m, v):
    R, C = w.shape
    tr = R
    for cand in (512, 344, 256, 128):
        if R % cand == 0:
            tr = cand
            break

    def body(w_ref, g_ref, m_ref, v_ref, d_ref, nm_ref, nv_ref):
        d, nm, nv = _adamw_math(w_ref[...], g_ref[...], m_ref[...], v_ref[...])
        d_ref[...] = d
        nm_ref[...] = nm
        nv_ref[...] = nv

    spec = pl.BlockSpec((tr, C), lambda i: (i, 0))
    return pl.pallas_call(
        body, name=name, grid=(R // tr,), in_specs=[spec] * 4, out_specs=[spec] * 3,
        out_shape=[jax.ShapeDtypeStruct((R, C), F32)] * 3, compiler_params=_params(1),
    )(w, g, m, v)


def adamw_small(name, ws, gs, ms, vs):
    n = len(ws)

    def body(*refs):
        for k in range(n):
            d, nm, nv = _adamw_math(refs[k][...], refs[n + k][...], refs[2 * n + k][...], refs[3 * n + k][...])
            refs[4 * n + k][...] = d
            refs[5 * n + k][...] = nm
            refs[6 * n + k][...] = nv

    vm = pl.BlockSpec(memory_space=pltpu.VMEM)
    shapes = [jax.ShapeDtypeStruct(w.shape, F32) for w in ws]
    res = pl.pallas_call(
        body, name=name, in_specs=[vm] * (4 * n), out_specs=[vm] * (3 * n), out_shape=shapes * 3,
        compiler_params=pltpu.CompilerParams(vmem_limit_bytes=VMEM_LIMIT_BYTES),
    )(*ws, *gs, *ms, *vs)
    return res[:n], res[n:2 * n], res[2 * n:]


def _place():
    return lax.axis_index("x"), lax.axis_index("y"), lax.axis_index("c")


def _other_chips(x, y):
    return [(1 - x, y), (x, 1 - y), (1 - x, 1 - y)]


HBM = pl.BlockSpec(memory_space=pl.ANY)


def all_gather_halves(name, blocks):
    n = len(blocks)

    def body(*refs):
        in_refs, out_refs = refs[:n], refs[n:2 * n]
        send_sems, recv_sems = refs[2 * n:]
        x, y, c = _place()
        me, sibling = (x, y, c), (x, y, 1 - c)
        chips = _other_chips(x, y)

        def slot(a, px, py, pc):
            return out_refs[a].at[4 * px + 2 * py + pc]

        def copy(a, k, block, to, src=None):
            return pltpu.make_async_remote_copy(
                src_ref=slot(a, *block) if src is None else src, dst_ref=slot(a, *block),
                send_sem=send_sems.at[a, k], recv_sem=recv_sems.at[a, k], device_id=to, device_id_type=MESH)

        first = []
        for a in range(n):
            first.append(copy(a, 0, me, sibling, src=in_refs[a]))
            first += [copy(a, 1 + j, me, (*chip, c), src=in_refs[a]) for j, chip in enumerate(chips)]
        for cp in first:
            cp.start()
        passed = []
        for j, chip in enumerate(chips):
            for a in range(n):
                copy(a, 1 + j, (*chip, c), me).wait_recv()
                fw = copy(a, 4 + j, (*chip, c), sibling)
                fw.start()
                passed.append(fw)
        for a in range(n):
            copy(a, 0, sibling, me).wait_recv()
            for j, chip in enumerate(chips):
                copy(a, 4 + j, (*chip, 1 - c), me).wait_recv()
        for cp in first + passed:
            cp.wait_send()

    res = pl.pallas_call(
        body, name=name, in_specs=[HBM] * n, out_specs=[HBM] * n,
        out_shape=[jax.ShapeDtypeStruct((N_DEV,) + b.shape, b.dtype) for b in blocks],
        scratch_shapes=[pltpu.SemaphoreType.DMA((n, 7)), pltpu.SemaphoreType.DMA((n, 7))],
    )(*blocks)
    return list(res)


def pair_send_halves(name, grads):
    n = len(grads)

    def body(*refs):
        in_refs, out_refs = refs[:n], refs[n:2 * n]
        send_sems, recv_sems = refs[2 * n:]
        x, y, c = _place()
        cps = []
        for a in range(n):
            cp = pltpu.make_async_remote_copy(
                src_ref=in_refs[a].at[:, 1 - c], dst_ref=out_refs[a], send_sem=send_sems.at[a], recv_sem=recv_sems.at[a],
                device_id=(x, y, 1 - c), device_id_type=MESH)
            cp.start()
            cps.append(cp)
        for cp in cps:
            cp.wait()

    res = pl.pallas_call(
        body, name=name, in_specs=[HBM] * n, out_specs=[HBM] * n,
        out_shape=[jax.ShapeDtypeStruct((g.shape[0],) + g.shape[2:], g.dtype) for g in grads],
        scratch_shapes=[pltpu.SemaphoreType.DMA((n,)), pltpu.SemaphoreType.DMA((n,))],
    )(*grads)
    return list(res)


def pair_add(name, grad, got, c_arr):
    nsh, _, M, N = grad.shape
    tr = M
    for cand in (512, 256, 192, 128, 64, 16):
        if M % cand == 0:
            tr = cand
            break

    def body(c_ref, g_ref, p_ref, o_ref):
        o_ref[...] = (g_ref[...] + p_ref[...]).astype(BF16)

    return pl.pallas_call(
        body, name=name,
        grid_spec=pltpu.PrefetchScalarGridSpec(
            num_scalar_prefetch=1, grid=(nsh, M // tr),
            in_specs=[pl.BlockSpec((None, None, tr, N), lambda j, i, c_ref: (j, c_ref[0], i, 0)),
                      pl.BlockSpec((None, tr, N), lambda j, i, c_ref: (j, i, 0))],
            out_specs=pl.BlockSpec((None, tr, N), lambda j, i, c_ref: (j, i, 0))),
        out_shape=jax.ShapeDtypeStruct((nsh, M, N), BF16), compiler_params=_params(2),
    )(c_arr, grad, got)


def scatter_to_chips(name, parts):
    n = len(parts)

    def body(*refs):
        in_refs, out_refs = refs[:n], refs[n:2 * n]
        send_sems, recv_sems = refs[2 * n:]
        x, y, c = _place()
        mine = 2 * x + y
        chips = _other_chips(x, y)
        cps = []
        for a in range(n):
            for k, (px, py) in enumerate(chips):
                cp = pltpu.make_async_remote_copy(
                    src_ref=in_refs[a].at[2 * px + py], dst_ref=out_refs[a].at[mine],
                    send_sem=send_sems.at[a, k], recv_sem=recv_sems.at[a, k], device_id=(px, py, c), device_id_type=MESH)
                cp.start()
                cps.append((cp, a, k, px, py))
        for cp, a, k, px, py in cps:
            pltpu.make_async_remote_copy(
                src_ref=in_refs[a].at[mine], dst_ref=out_refs[a].at[2 * px + py],
                send_sem=send_sems.at[a, k], recv_sem=recv_sems.at[a, k], device_id=(px, py, c), device_id_type=MESH).wait_recv()
        for cp, *_ in cps:
            cp.wait_send()

    res = pl.pallas_call(
        body, name=name, in_specs=[HBM] * n, out_specs=[HBM] * n,
        out_shape=[jax.ShapeDtypeStruct(p.shape, p.dtype) for p in parts],
        scratch_shapes=[pltpu.SemaphoreType.DMA((n, 3)), pltpu.SemaphoreType.DMA((n, 3))],
    )(*parts)
    return list(res)


def sum_chips(name, q):
    nsh, M, N = q.shape
    tr = M
    for cand in (512, 256, 192, 128, 64, 16):
        if M % cand == 0:
            tr = cand
            break

    def body(q_ref, o_ref):
        acc = q_ref[0].astype(F32)
        for j in range(1, nsh):
            acc = acc + q_ref[j].astype(F32)
        o_ref[...] = acc

    return pl.pallas_call(
        body, name=name, grid=(M // tr,), in_specs=[pl.BlockSpec((nsh, tr, N), lambda i: (0, i, 0))],
        out_specs=pl.BlockSpec((tr, N), lambda i: (i, 0)), out_shape=jax.ShapeDtypeStruct((M, N), F32),
        compiler_params=_params(1),
    )(q)


def pair_exchange(name, halves):
    n = len(halves)

    def body(*refs):
        in_refs, out_refs = refs[:n], refs[n:2 * n]
        send_sems, recv_sems = refs[2 * n:]
        x, y, c = _place()
        cps = []
        for a in range(n):
            cp = pltpu.make_async_remote_copy(
                src_ref=in_refs[a], dst_ref=out_refs[a], send_sem=send_sems.at[a], recv_sem=recv_sems.at[a],
                device_id=(x, y, 1 - c), device_id_type=MESH)
            cp.start()
            cps.append(cp)
        for cp in cps:
            cp.wait()

    res = pl.pallas_call(
        body, name=name, in_specs=[HBM] * n, out_specs=[HBM] * n,
        out_shape=[jax.ShapeDtypeStruct(h.shape, h.dtype) for h in halves],
        scratch_shapes=[pltpu.SemaphoreType.DMA((n,)), pltpu.SemaphoreType.DMA((n,))],
    )(*halves)
    return list(res)


SEM = pl.BlockSpec(memory_space=pltpu.SEMAPHORE)
IN_HBM = pl.BlockSpec(memory_space=pltpu.HBM)
SPLIT_COPY = pltpu.CompilerParams(has_side_effects=pltpu.SideEffectType.DATAFLOW_SIDE_EFFECTING)


def _scatter_copies(src_refs, dst_refs, send_sems, recv_sems):
    x, y, c = _place()
    mine = 2 * x + y
    return [pltpu.make_async_remote_copy(
        src_ref=src_refs[a].at[2 * px + py], dst_ref=dst_refs[a].at[mine], send_sem=send_sems.at[a * (N_CHIPS - 1) + k],
        recv_sem=recv_sems.at[a * (N_CHIPS - 1) + k], device_id=(px, py, c), device_id_type=MESH)
        for a in range(len(src_refs)) for k, (px, py) in enumerate(_other_chips(x, y))]


def _pair_copies(src_refs, dst_refs, send_sems, recv_sems):
    x, y, c = _place()
    return [pltpu.make_async_remote_copy(
        src_ref=src_refs[a].at[:, 1 - c], dst_ref=dst_refs[a], send_sem=send_sems.at[a], recv_sem=recv_sems.at[a],
        device_id=(x, y, 1 - c), device_id_type=MESH) for a in range(len(src_refs))]


def _gather_copies(src_refs, dst_refs, send_sems, recv_sems):
    x, y, c = _place()
    me = 4 * x + 2 * y + c
    cps = []
    for a in range(len(src_refs)):
        for k in range(1, N_DEV):
            to = (1 - x if k & 4 else x, 1 - y if k & 2 else y, 1 - c if k & 1 else c)
            s = a * (N_DEV - 1) + k - 1
            cps.append(pltpu.make_async_remote_copy(
                src_ref=src_refs[a], dst_ref=dst_refs[a].at[me], send_sem=send_sems.at[s], recv_sem=recv_sems.at[s],
                device_id=to, device_id_type=MESH))
    return cps


def split_copy_start(name, copies, n_sem, srcs, land_shapes):
    n = len(srcs)
    lands = [lax.empty(s.shape, s.dtype) for s in land_shapes]

    def body(*refs):
        for cp in copies(refs[:n], refs[n:2 * n], refs[2 * n], refs[2 * n + 1]):
            cp.start()
        refs[-1][...] = jnp.zeros_like(refs[-1])

    thru = [pltpu.HBM(a.shape, a.dtype) for a in (*srcs, *lands)]
    res = pl.pallas_call(
        body, name=name, in_specs=[IN_HBM] * (2 * n),
        out_specs=(SEM, SEM, *[IN_HBM] * (2 * n), pl.BlockSpec(memory_space=pltpu.VMEM)),
        out_shape=(pltpu.SemaphoreType.DMA((n * n_sem,)), pltpu.SemaphoreType.DMA((n * n_sem,)), *thru,
                   jax.ShapeDtypeStruct((SUBLANES, LANES), F32)),
        input_output_aliases={i: 2 + i for i in range(2 * n)}, compiler_params=SPLIT_COPY,
    )(*[pltpu.with_memory_space_constraint(a, pltpu.HBM) for a in (*srcs, *lands)])
    return (copies, n, res[0], res[1], res[2:2 + 2 * n]), res[-1][0, 0]


def split_copy_wait(name, handle, after):
    copies, n, send_sems, recv_sems, thru = handle

    def body(*refs):
        for cp in copies(refs[:n], refs[n:2 * n], refs[2 * n], refs[2 * n + 1]):
            cp.wait_send()
            cp.wait_recv()

    res = pl.pallas_call(
        body, name=name, in_specs=[IN_HBM] * (2 * n) + [SEM, SEM, pl.BlockSpec(memory_space=pl.ANY)],
        out_specs=[IN_HBM] * (2 * n), out_shape=[pltpu.HBM(a.shape, a.dtype) for a in thru],
        input_output_aliases={i: i for i in range(2 * n)}, compiler_params=SPLIT_COPY,
    )(*thru, send_sems, recv_sems, after)
    return list(res[:n]), list(res[n:])


def all_reduce_small(name, v):
    R, C = v.shape

    def body(v_ref, o_ref, gath, send_sems, recv_sems):
        x, y, c = _place()
        me, sibling = (x, y, c), (x, y, 1 - c)
        chips = _other_chips(x, y)

        def slot(px, py, pc):
            return gath.at[4 * px + 2 * py + pc]

        def copy(k, block, to, src=None):
            return pltpu.make_async_remote_copy(
                src_ref=slot(*block) if src is None else src, dst_ref=slot(*block),
                send_sem=send_sems.at[k], recv_sem=recv_sems.at[k], device_id=to, device_id_type=MESH)

        first = [copy(0, me, sibling, src=v_ref)]
        first += [copy(1 + j, me, (*chip, c), src=v_ref) for j, chip in enumerate(chips)]
        for cp in first:
            cp.start()
        slot(*me)[...] = v_ref[...]
        passed = [copy(4 + j, (*chip, c), sibling) for j, chip in enumerate(chips)]
        for j, chip in enumerate(chips):
            copy(1 + j, (*chip, c), me).wait_recv()
            passed[j].start()
        copy(0, sibling, me).wait_recv()
        for j, chip in enumerate(chips):
            copy(4 + j, (*chip, 1 - c), me).wait_recv()
        for cp in first + passed:
            cp.wait_send()
        acc = gath[0]
        for d in range(1, N_DEV):
            acc = acc + gath[d]
        o_ref[...] = acc

    vm = pl.BlockSpec(memory_space=pltpu.VMEM)
    return pl.pallas_call(
        body, name=name, in_specs=[vm], out_specs=vm, out_shape=jax.ShapeDtypeStruct((R, C), F32),
        scratch_shapes=[pltpu.VMEM((N_DEV, R, C), F32), pltpu.SemaphoreType.DMA((7,)), pltpu.SemaphoreType.DMA((7,))],
        compiler_params=pltpu.CompilerParams(vmem_limit_bytes=VMEM_LIMIT_BYTES),
    )(v)


LANES = 128
PACK_ROW_MULTIPLE = 1024
SMALL_SHARDED = {
    "w_in": ((D_MODEL, 1216), 1), "mla_w_uq": ((MLA_Q_RANK, 768), 1), "mla_w_ukv": ((MLA_KV_RANK, 1024), 1),
    "ssm_w_glu": ((SSM_WIDTH, SSM_WIDTH), 0), "w_o": ((D_MODEL, D_MODEL), 0), "xattn_w_q": ((D_MODEL, 512), 0),
    "xattn_w_kv": ((D_MODEL, 1024), 0), "xattn_w_o": ((512, D_MODEL), 1),
}
FFN_NAMES = ["ffn1_w_gate", "ffn1_w_up", "ffn1_w_down", "ffn2_w_gate", "ffn2_w_up", "ffn2_w_down"]
TRANSPOSED_VIEW = ("ffn1_w_gate", "ffn1_w_up", "ffn2_w_gate", "ffn2_w_up", "w_in", "mla_w_uq")


def _shard_shape(name):
    (r, cdim), ax = SMALL_SHARDED[name]
    return (r // N_CHIPS, cdim) if ax == 0 else (r, cdim // N_CHIPS)


def _pack_shards(shards):
    parts = []
    for name in SMALL_SHARDED:
        a = shards[name]
        lead = a.shape[:-2]
        parts.append(a.reshape(lead + (a.shape[-2] * a.shape[-1] // LANES, LANES)))
    rows = sum(q.shape[-2] for q in parts)
    parts.append(jnp.zeros(lead + (-rows % PACK_ROW_MULTIPLE, LANES), parts[0].dtype))
    return jnp.concatenate(parts, axis=-2)


def _unpack_shards(packed):
    out, r0 = {}, 0
    lead = packed.shape[:-2]
    for name in SMALL_SHARDED:
        r, cdim = _shard_shape(name)
        rows = r * cdim // LANES
        out[name] = packed[..., r0:r0 + rows, :].reshape(lead + (r, cdim))
        r0 += rows
    return out


def _full_from_shards(name, sh):
    (r, cdim), ax = SMALL_SHARDED[name]
    if ax == 0:
        return sh.reshape(r, cdim)
    return jnp.transpose(sh, (1, 0, 2)).reshape(r, cdim)


def _shards_from_full(name, full):
    (r, cdim), ax = SMALL_SHARDED[name]
    if ax == 0:
        return full.reshape(N_CHIPS, r // N_CHIPS, cdim)
    return jnp.transpose(full.reshape(r, N_CHIPS, cdim // N_CHIPS), (1, 0, 2))


SMALL_REPL = {
    "ffn1_norm": (1, 1024), "mix_norm": (1, 1024), "mla_q_norm": (1, 384), "mla_kv_norm": (1, 256),
    "mla_qk_norm_q": (1, 192), "mla_qk_norm_k": (1, 192), "ssm_a_re": (32, 64), "ssm_a_im": (32, 64),
    "ssm_log_dt": (32, 1), "ssm_b_re": (32, 64, 16), "ssm_b_im": (32, 64, 16), "ssm_c_re": (32, 16, 64),
    "ssm_c_im": (32, 16, 64), "ssm_d": (1, 512), "ssm_b_glu": (1, 512), "out_norm_mla": (1, 512),
    "out_norm_ssm": (1, 512), "xattn_norm": (1, 1024), "mem_norm": (1, 1024), "xattn_q_norm": (1, 128),
    "xattn_k_norm": (1, 128), "ffn2_norm": (1, 1024),
}


def _pack_repl(grads, loss):
    flat = jnp.concatenate([grads[n].reshape(-1) for n in SMALL_REPL] + [loss.reshape(1)])
    rows = -(-flat.shape[0] // (LANES * SUBLANES)) * SUBLANES
    return jnp.pad(flat, (0, rows * LANES - flat.shape[0])).reshape(rows, LANES)


def _unpack_repl(packed):
    flat, out, o = packed.reshape(-1), {}, 0
    for n, shp in SMALL_REPL.items():
        size = int(np.prod(shp))
        out[n] = flat[o:o + size].reshape(shp)
        o += size
    out["loss"] = flat[o]
    return out


def _rope_tables(positions):
    half = MLA_ROPE // 2
    inv = ROPE_THETA ** (-jnp.arange(half, dtype=F32) / half)
    ang = positions.astype(F32)[:, None] * inv[None, :]
    cos, sin = jnp.cos(ang), jnp.sin(ang)
    S = positions.shape[0]
    z = lambda w: jnp.zeros((S, w), F32)
    keep = jnp.concatenate([jnp.ones((S, MLA_NOPE), F32), cos, cos, z(HEAD_PAD - MLA_QK)], axis=1)
    from_hi = jnp.concatenate([z(MLA_NOPE), -sin, z(HEAD_PAD - MLA_NOPE - half)], axis=1)
    from_lo = jnp.concatenate([z(MLA_NOPE + half), sin, z(HEAD_PAD - MLA_QK)], axis=1)
    return keep, from_hi, from_lo


def _norm_rope(x, g, keep, from_hi, from_lo):
    y = x * lax.rsqrt(jnp.sum(x * x, axis=-1, keepdims=True) * (1.0 / MLA_QK) + EPS) * g
    half = MLA_ROPE // 2
    return y * keep + _lane_roll(y, HEAD_PAD - half) * from_hi + _lane_roll(y, half) * from_lo


def local_step(x, mem, positions, target, w, wb, small_weights=None, ffn2_weights=None, on_grads=None):
    if small_weights is None:
        small_weights = lambda after: {}
    if ffn2_weights is None:
        ffn2_weights = lambda after: [wb[k] for k in FFN_NAMES[3:]]
    if on_grads is None:
        on_grads = lambda tag, g: 0.0
    S = x.shape[0]
    tm = min(FUSED_TILE, S)
    g1 = (S // tm,)
    tile = lambda arr, **kw: In(arr, (tm, arr.shape[1]), lambda i: (i, 0), **kw)
    par = lambda arr, **kw: In(arr, arr.shape, lambda *_: (0, 0), diff=True, acc=(0,), **kw)
    wt = lambda arr: In(arr, weight=True)
    otile = lambda cols, dt: Out((S, cols), dt, (tm, cols), lambda i: (i, 0))
    grads = {}

    x1, h1, g_1, u_1 = ffn_fwd("ffn1_fwd", x, w["ffn1_norm"], wb["ffn1_w_gate"], wb["ffn1_w_up"], wb["ffn1_w_down"])
    wb = {**wb, **small_weights(x1)}

    keep, from_hi, from_lo = _rope_tables(positions)
    scale = MLA_QK ** -0.5

    def f_pre(xv, kp, fh, fl, g_mix, w_q, w_kv, w_kr, w_u, g_q, g_kv, w_uq, w_ukv, gq, gk):
        h2 = _rms(xv, g_mix)
        cq, ckv = _rms(wdot(h2, w_q), g_q), _rms(wdot(h2, w_kv), g_kv)
        kr = wdot(h2, w_kr)
        qs, ks, vs = [], [], []
        for h in range(MLA_HEADS):
            qs.append(_norm_rope(wdot(cq, w_uq, h), gq, kp, fh, fl) * scale)
            kv = wdot(ckv, w_ukv, h)
            ks.append(_norm_rope(jnp.concatenate([kv[:, :MLA_NOPE], kr], axis=-1), gk, kp, fh, fl))
            vs.append(kv[:, MLA_NOPE:])
        return jnp.concatenate(qs, axis=-1), jnp.concatenate(ks, axis=-1), jnp.concatenate(vs, axis=-1), wdot(h2, w_u)

    def pre_ins(gain_mix):
        return [tile(x1, diff=True), tile(keep), tile(from_hi), tile(from_lo), par(gain_mix),
                wt(wb["w_in_q"]), wt(wb["w_in_kv"]), wt(wb["w_in_kr"]), wt(wb["w_in_u"]),
                par(w["mla_q_norm"]), par(w["mla_kv_norm"]), wt(wb["w_uq"]), wt(wb["w_ukv"]),
                par(w["qk_gain_q"]), par(w["qk_gain_k"])]

    pre_outs = [otile(MLA_HEADS * HEAD_PAD, BF16), otile(MLA_HEADS * HEAD_PAD, BF16), otile(MLA_HEADS * MLA_V, BF16),
                otile(SSM_WIDTH, F32)]
    qh, kh, v0, pu = seg_fwd("pre_mixer_fwd", f_pre, g1, pre_ins(w["mix_norm"]), pre_outs)

    o_mla, lse = attn_fwd("mla_attn_fwd", qh, kh, v0, t=min(ATTN_TILE, S))

    prep_grid = (SSM_BLOCKS,)
    prep_ins = ([_whole(w[k], diff=True, acc=(0,)) for k in ("ssm_a_re", "ssm_a_im", "ssm_log_dt")]
                + [In(w[k], (BLOCK_CH, SSM_STATE), lambda i: (i, 0), diff=True)
                   for k in ("ssm_bt_re", "ssm_bt_im", "ssm_c2_re", "ssm_c2_im")])
    blk_out = Out((SSM_BLOCKS, BLOCK_CH, PREP_LANES), BF16, (None, BLOCK_CH, PREP_LANES), lambda i: (i, 0, 0))
    prep_outs = [Out((SUBLANES, SSM_LANES), F32, (SUBLANES, PREP_LANES), lambda i: (0, i))] * 2 + [blk_out] * 4
    a_r8, a_i8, bb_r, bb_i, cb_r, ncb_i = seg_fwd("ssm_prep_fwd", _ssm_prep_f, prep_grid, prep_ins, prep_outs)

    xs_r, xs_i, y_lin = ssm_fwd("ssm_fwd", pu, bb_r, bb_i, cb_r, ncb_i, a_r8, a_i8)

    M = mem.shape[0]
    f_norm = lambda xv, g: (_rms(xv, g),)
    mem_ins = [In(mem, (M, D_MODEL), lambda i: (0, 0)), par(w["mem_norm"])]
    mem_outs = [Out((M, D_MODEL), BF16, (M, D_MODEL), lambda i: (0, 0))]
    (mn,) = seg_fwd("mem_norm_fwd", f_norm, (1,), mem_ins, mem_outs)
    (kvm,) = mm("xattn_kv_fwd", [mn], [[wb["xattn_w_kv"]]], [F32])

    xscale = XHD ** -0.5

    def f_post(o, yl, u, xv, kv, d, w_glu, b, gm, gs, w_o1, w_o2, gx, w_xq, gq, gk, w_xo):
        gl = jax.nn.gelu(yl + d * u)
        so = gl * jax.nn.sigmoid(wdot(gl, w_glu) + b)
        x2 = xv + wdot(_rms(o, gm), w_o1) + wdot(_rms(so, gs), w_o2)
        h3 = _rms(x2, gx)
        heads = []
        for h in range(XH):
            qn = _rms(wdot(h3, w_xq, h), gq)
            kn = _rms(kv[:, h * XHD:(h + 1) * XHD], gk)
            p = jax.nn.softmax(_bdot_nt(qn, kn) * xscale, axis=-1)
            heads.append(_bdot_nn(p, kv[:, (XH + h) * XHD:(XH + h + 1) * XHD]))
        return (x2 + wdot(jnp.concatenate(heads, axis=-1), w_xo),)

    def post_ins(gain_d):
        return [tile(o_mla, diff=True), tile(y_lin, diff=True), tile(pu, diff=True), tile(x1, diff=True),
                In(kvm, kvm.shape, lambda i: (0, 0), diff=True, acc=(0,)), par(gain_d), wt(wb["ssm_w_glu"]),
                par(w["ssm_b_glu"]), par(w["out_norm_mla"]), par(w["out_norm_ssm"]), wt(wb["w_o_mla"]), wt(wb["w_o_ssm"]),
                par(w["xattn_norm"]), wt(wb["xattn_w_q"]), par(w["xattn_q_norm"]), par(w["xattn_k_norm"]),
                wt(wb["xattn_w_o"])]

    post_outs = [otile(D_MODEL, F32)]
    (x3,) = seg_fwd("post_mixer_fwd", f_post, g1, post_ins(w["ssm_d"]), post_outs)

    wg2, wu2, wd2 = ffn2_weights(x3)
    dx4, h4, g_2, u_2, parts = ffn_fwd("ffn2_fwd_loss", x3, w["ffn2_norm"], wg2, wu2, wd2, target=target)
    loss = jnp.sum(parts[::SUBLANES, 0])

    dx3, grads["ffn2_norm"], dg_2, du_2 = ffn_bwd_act("ffn2_bwd_act", dx4, x3, w["ffn2_norm"], g_2, u_2, wg2, wu2, wd2)
    grads["ffn2_w_gate"], grads["ffn2_w_up"], grads["ffn2_w_down"] = ffn_bwd_w("ffn2_bwd_w", h4, dx4, g_2, u_2, dg_2, du_2)
    sent = on_grads("ffn2", grads)

    (do_mla, dy_lin, du_a, dx1_a, dkvm, grads["ssm_d"], grads["ssm_w_glu"], grads["ssm_b_glu"], grads["out_norm_mla"],
     grads["out_norm_ssm"], grads["w_o_mla"], grads["w_o_ssm"], grads["xattn_norm"], grads["xattn_w_q"],
     grads["xattn_q_norm"], grads["xattn_k_norm"], grads["xattn_w_o"]) = seg_bwd(
        "post_mixer_bwd", f_post, g1, post_ins(w["ssm_d"] + sent), post_outs, [dx3])

    (dmn,) = mm("xattn_kv_bwd", [dkvm], [[wb["xattn_w_kv"]]], [F32], trans=True)
    (grads["xattn_w_kv"],) = mm_tn("xattn_kv_bwd_w", [mn], [dkvm], [(0, [0])])
    (grads["mem_norm"],) = seg_bwd("mem_norm_bwd", f_norm, (1,), mem_ins, mem_outs, [dmn])

    du, dbb_r, dbb_i, dcb_r, dncb_i, da_r8, da_i8 = ssm_bwd("ssm_bwd", dy_lin, du_a, pu, xs_r, xs_i, bb_r, bb_i, cb_r, ncb_i,
                                                            a_r8, a_i8)
    prep_g = seg_bwd("ssm_prep_bwd", _ssm_prep_f, prep_grid, prep_ins, prep_outs, [da_r8, da_i8, dbb_r, dbb_i, dcb_r, dncb_i])
    for k, gname in enumerate(("ssm_a_re", "ssm_a_im", "ssm_log_dt", "ssm_bt_re", "ssm_bt_im", "ssm_c2_re", "ssm_c2_im")):
        grads[gname] = prep_g[k]

    dqh, dkh, dv0 = attn_bwd("mla_attn_bwd", qh, kh, v0, do_mla, o_mla, lse, t=min(ATTN_TILE, S))

    (dx1, grads["mix_norm"], grads["w_in_q"], grads["w_in_kv"], grads["w_in_kr"], grads["w_in_u"], grads["mla_q_norm"],
     grads["mla_kv_norm"], grads["w_uq"], grads["w_ukv"], grads["qk_gain_q"], grads["qk_gain_k"]) = seg_bwd(
        "pre_mixer_bwd", f_pre, g1, pre_ins(w["mix_norm"]), pre_outs, [dqh, dkh, dv0, du], adds={0: dx1_a})

    sent = on_grads("small", grads)
    dx, grads["ffn1_norm"], dg_1, du_1 = ffn_bwd_act("ffn1_bwd_act", dx1, x, w["ffn1_norm"] + sent, g_1, u_1,
                                                     wb["ffn1_w_gate"], wb["ffn1_w_up"], wb["ffn1_w_down"])
    grads["ffn1_w_gate"], grads["ffn1_w_up"], grads["ffn1_w_down"] = ffn_bwd_w("ffn1_bwd_w", h1, dx1, g_1, u_1, dg_1, du_1)
    return loss, dx, grads


def _pad_cols(a, n):
    return jnp.pad(a, ((0, 0), (0, n - a.shape[1])))


def _step_weights(shards):
    wb = {}
    w_in = _full_from_shards("w_in", shards["w_in"])
    wb["w_in_q"] = w_in[:, :MLA_Q_RANK]
    wb["w_in_kv"] = w_in[:, MLA_Q_RANK:MLA_Q_RANK + MLA_KV_RANK]
    wb["w_in_kr"] = _pad_cols(w_in[:, MLA_Q_RANK + MLA_KV_RANK:MLA_Q_RANK + MLA_KV_RANK + MLA_ROPE], LANES)
    wb["w_in_u"] = w_in[:, MLA_Q_RANK + MLA_KV_RANK + MLA_ROPE:]
    wb["w_uq"] = jnp.pad(shards["mla_w_uq"], ((0, 0), (0, 0), (0, HEAD_PAD - MLA_QK)))
    wb["w_ukv"] = shards["mla_w_ukv"]
    wb["ssm_w_glu"] = _full_from_shards("ssm_w_glu", shards["ssm_w_glu"])
    w_o = _full_from_shards("w_o", shards["w_o"])
    wb["w_o_mla"], wb["w_o_ssm"] = w_o[:SSM_WIDTH], w_o[SSM_WIDTH:]
    w_xq = _full_from_shards("xattn_w_q", shards["xattn_w_q"])
    wb["xattn_w_q"] = jnp.transpose(w_xq.reshape(D_MODEL, XH, XHD), (1, 0, 2))
    wb["xattn_w_kv"] = _full_from_shards("xattn_w_kv", shards["xattn_w_kv"])
    wb["xattn_w_o"] = _full_from_shards("xattn_w_o", shards["xattn_w_o"])
    return wb


def _sharded_grads(g):
    out = {}
    kr = g["w_in_kr"][:, :MLA_ROPE]
    out["w_in"] = _shards_from_full("w_in", jnp.concatenate([g["w_in_q"], g["w_in_kv"], kr, g["w_in_u"]], axis=1))
    out["mla_w_uq"] = g["w_uq"][:, :, :MLA_QK]
    out["mla_w_ukv"] = g["w_ukv"]
    out["ssm_w_glu"] = _shards_from_full("ssm_w_glu", g["ssm_w_glu"])
    out["w_o"] = _shards_from_full("w_o", jnp.concatenate([g["w_o_mla"], g["w_o_ssm"]], axis=0))
    w_xq = jnp.transpose(g["xattn_w_q"], (1, 0, 2)).reshape(D_MODEL, XH * XHD)
    out["xattn_w_q"] = _shards_from_full("xattn_w_q", w_xq)
    out["xattn_w_kv"] = _shards_from_full("xattn_w_kv", g["xattn_w_kv"])
    out["xattn_w_o"] = _shards_from_full("xattn_w_o", g["xattn_w_o"])
    return out


def _problem_repl_grads(g):
    out = {}
    out["mla_qk_norm_q"] = g["qk_gain_q"][:, :MLA_QK]
    out["mla_qk_norm_k"] = g["qk_gain_k"][:, :MLA_QK]
    out["ssm_b_re"] = jnp.transpose(g["ssm_bt_re"].reshape(SSM_GROUPS, SSM_GROUP, SSM_STATE), (0, 2, 1))
    out["ssm_b_im"] = jnp.transpose(g["ssm_bt_im"].reshape(SSM_GROUPS, SSM_GROUP, SSM_STATE), (0, 2, 1))
    out["ssm_c_re"] = g["ssm_c2_re"].reshape(SSM_GROUPS, SSM_GROUP, SSM_STATE)
    out["ssm_c_im"] = g["ssm_c2_im"].reshape(SSM_GROUPS, SSM_GROUP, SSM_STATE)
    for k in SMALL_REPL:
        if k not in out:
            out[k] = g[k]
    return out


def _problem_grads(g):
    out = {k: _full_from_shards(k, v) for k, v in _sharded_grads(g).items()}
    out.update(_problem_repl_grads(g))
    out.update({k: g[k] for k in FFN_NAMES})
    return out


def _step_params(p):
    row = lambda a: a.reshape(1, -1)
    w = {k: row(p[k]) for k in ("ffn1_norm", "mix_norm", "mla_q_norm", "mla_kv_norm", "ssm_b_glu", "out_norm_mla",
                                "out_norm_ssm", "xattn_norm", "mem_norm", "xattn_q_norm", "xattn_k_norm", "ffn2_norm")}
    w["qk_gain_q"] = _pad_cols(row(p["mla_qk_norm_q"]), HEAD_PAD)
    w["qk_gain_k"] = _pad_cols(row(p["mla_qk_norm_k"]), HEAD_PAD)
    w["ssm_a_re"], w["ssm_a_im"] = p["ssm_a_re"], p["ssm_a_im"]
    w["ssm_log_dt"] = p["ssm_log_dt"].reshape(SSM_GROUPS, 1)
    w["ssm_bt_re"] = jnp.transpose(p["ssm_b_re"], (0, 2, 1)).reshape(SSM_WIDTH, SSM_STATE)
    w["ssm_bt_im"] = jnp.transpose(p["ssm_b_im"], (0, 2, 1)).reshape(SSM_WIDTH, SSM_STATE)
    w["ssm_c2_re"] = p["ssm_c_re"].reshape(SSM_WIDTH, SSM_STATE)
    w["ssm_c2_im"] = p["ssm_c_im"].reshape(SSM_WIDTH, SSM_STATE)
    w["ssm_d"] = p["ssm_d"].reshape(1, SSM_WIDTH)
    return w


ARG_NAMES = ['x', 'mem', 'positions', 'ffn1_norm', 'ffn1_w_gate', 'ffn1_w_up', 'ffn1_w_down', 'mix_norm', 'w_in', 'mla_q_norm', 'mla_w_uq', 'mla_kv_norm', 'mla_w_ukv', 'mla_qk_norm_q', 'mla_qk_norm_k', 'ssm_a_re', 'ssm_a_im', 'ssm_log_dt', 'ssm_b_re', 'ssm_b_im', 'ssm_c_re', 'ssm_c_im', 'ssm_d', 'ssm_w_glu', 'ssm_b_glu', 'out_norm_mla', 'out_norm_ssm', 'w_o', 'xattn_norm', 'mem_norm', 'xattn_w_q', 'xattn_w_kv', 'xattn_q_norm', 'xattn_k_norm', 'xattn_w_o', 'ffn2_norm', 'ffn2_w_gate', 'ffn2_w_up', 'ffn2_w_down']
WEIGHT_NAMES = ARG_NAMES[3:]


def _gather_weights(p, c):
    half = lambda a: lax.dynamic_slice_in_dim(a, c * (a.shape[0] // 2), a.shape[0] // 2, axis=0)
    ffn1 = [half(p[k].astype(BF16)) for k in FFN_NAMES[:3]]
    small = [half(_pack_shards({k: p[k].astype(BF16) for k in SMALL_SHARDED}))]
    ffn2 = [half(p[k].astype(BF16)) for k in FFN_NAMES[3:]]
    me = 4 * lax.axis_index("x") + 2 * lax.axis_index("y") + c
    own = lambda got, blocks: [lax.dynamic_update_index_in_dim(g, b, me, 0) for g, b in zip(got, blocks)]
    as_shards = lambda a: a.reshape(N_CHIPS, 2 * a.shape[1], a.shape[2])
    landing = lambda blocks: [jax.ShapeDtypeStruct((N_DEV,) + b.shape, b.dtype) for b in blocks]
    got1 = own(all_gather_halves("all_gather_weights_a", ffn1), ffn1)
    got1, small = lax.optimization_barrier((got1, small))
    flight_s, sent_s = split_copy_start("gather_small_start", _gather_copies, N_DEV - 1, small, landing(small))
    sent_s, ffn2 = lax.optimization_barrier((sent_s, ffn2))
    flight_2, sent_2 = split_copy_start("gather_ffn2_start", _gather_copies, N_DEV - 1, ffn2, landing(ffn2))
    wb = {k: as_shards(a) for k, a in zip(FFN_NAMES[:3], got1)}

    def small_weights(after):
        mine, got = split_copy_wait("gather_small_wait", flight_s, after)
        return _step_weights(_unpack_shards(own(got, mine)[0].reshape(N_CHIPS, -1, LANES)))

    def ffn2_weights(after):
        mine, got = split_copy_wait("gather_ffn2_wait", flight_2, after)
        return [as_shards(a) for a in own(got, mine)]

    return wb, small_weights, ffn2_weights, sent_s + sent_2


class _GradReduce:
    def __init__(self, c):
        self.c, self.c_arr = c, jnp.reshape(c, (1,)).astype(jnp.int32)
        self.chip = 2 * lax.axis_index("x") + lax.axis_index("y")
        self.flights = []

    def start(self, tag, arrs):
        split = [a.reshape(N_CHIPS, 2, a.shape[1] // 2, a.shape[2]) for a in arrs]
        return self._scatter(tag, split, pair_send_halves(f"grad_pair_send_{tag}", split))

    def send(self, tag, arrs):
        split = [a.reshape(N_CHIPS, 2, a.shape[1] // 2, a.shape[2]) for a in arrs]
        lands = [jax.ShapeDtypeStruct((s.shape[0],) + s.shape[2:], s.dtype) for s in split]
        flight, sent = split_copy_start(f"grad_pair_send_start_{tag}", _pair_copies, 1, split, lands)
        self.sending = (tag, flight)
        return sent

    def scatter(self, tag, after):
        sent_tag, flight = self.sending
        assert sent_tag == tag
        return self._scatter(tag, *split_copy_wait(f"grad_pair_send_wait_{tag}", flight, after))

    def _scatter(self, tag, split, got):
        parts = [pair_add(f"grad_pair_add_{tag}_{k}", s, g, self.c_arr) for k, (s, g) in enumerate(zip(split, got))]
        flight, sent = split_copy_start(f"grad_scatter_start_{tag}", _scatter_copies, N_CHIPS - 1, parts, parts)
        self.flights.append((tag, flight))
        return sent

    def finish(self, after):
        halves = []
        for tag, flight in self.flights:
            parts, landed = split_copy_wait(f"grad_scatter_wait_{tag}", flight, after)
            for k, (q, p) in enumerate(zip(landed, parts)):
                mine = lax.dynamic_index_in_dim(p, self.chip, 0, keepdims=False)
                halves.append(sum_chips(f"grad_sum_{tag}_{k}", lax.dynamic_update_index_in_dim(q, mine, self.chip, 0)))
        tags = "_".join(t for t, _ in self.flights)
        self.flights = []
        theirs = pair_exchange(f"grad_pair_exchange_{tags}", halves)
        return [jnp.where(self.c == 0, jnp.concatenate([h, t], axis=0), jnp.concatenate([t, h], axis=0))
                for h, t in zip(halves, theirs)]


def kernel(x, mem, positions, ffn1_norm, ffn1_w_gate, ffn1_w_up, ffn1_w_down, mix_norm, w_in, mla_q_norm, mla_w_uq, mla_kv_norm, mla_w_ukv, mla_qk_norm_q, mla_qk_norm_k, ssm_a_re, ssm_a_im, ssm_log_dt, ssm_b_re, ssm_b_im, ssm_c_re, ssm_c_im, ssm_d, ssm_w_glu, ssm_b_glu, out_norm_mla, out_norm_ssm, w_o, xattn_norm, mem_norm, xattn_w_q, xattn_w_kv, xattn_q_norm, xattn_k_norm, xattn_w_o, ffn2_norm, ffn2_w_gate, ffn2_w_up, ffn2_w_down, loss_target, m_ffn1_norm, m_ffn1_w_gate, m_ffn1_w_up, m_ffn1_w_down, m_mix_norm, m_w_in, m_mla_q_norm, m_mla_w_uq, m_mla_kv_norm, m_mla_w_ukv, m_mla_qk_norm_q, m_mla_qk_norm_k, m_ssm_a_re, m_ssm_a_im, m_ssm_log_dt, m_ssm_b_re, m_ssm_b_im, m_ssm_c_re, m_ssm_c_im, m_ssm_d, m_ssm_w_glu, m_ssm_b_glu, m_out_norm_mla, m_out_norm_ssm, m_w_o, m_xattn_norm, m_mem_norm, m_xattn_w_q, m_xattn_w_kv, m_xattn_q_norm, m_xattn_k_norm, m_xattn_w_o, m_ffn2_norm, m_ffn2_w_gate, m_ffn2_w_up, m_ffn2_w_down, v_ffn1_norm, v_ffn1_w_gate, v_ffn1_w_up, v_ffn1_w_down, v_mix_norm, v_w_in, v_mla_q_norm, v_mla_w_uq, v_mla_kv_norm, v_mla_w_ukv, v_mla_qk_norm_q, v_mla_qk_norm_k, v_ssm_a_re, v_ssm_a_im, v_ssm_log_dt, v_ssm_b_re, v_ssm_b_im, v_ssm_c_re, v_ssm_c_im, v_ssm_d, v_ssm_w_glu, v_ssm_b_glu, v_out_norm_mla, v_out_norm_ssm, v_w_o, v_xattn_norm, v_mem_norm, v_xattn_w_q, v_xattn_w_kv, v_xattn_q_norm, v_xattn_k_norm, v_xattn_w_o, v_ffn2_norm, v_ffn2_w_gate, v_ffn2_w_up, v_ffn2_w_down):
    args = dict(locals())
    c = lax.axis_index("c")
    view = lambda k, a: jnp.swapaxes(a, 0, 1) if k in TRANSPOSED_VIEW else a
    p = {k: view(k, args[k][0]) for k in WEIGHT_NAMES}
    mom = {k: view(k, args["m_" + k][0]) for k in WEIGHT_NAMES}
    var = {k: view(k, args["v_" + k][0]) for k in WEIGHT_NAMES}
    natural = {k: view(k, p[k]) for k in WEIGHT_NAMES}

    wb, small_weights, ffn2_weights, sent = _gather_weights(
        {k: (p[k] if k in FFN_NAMES else natural[k]) for k in WEIGHT_NAMES}, c)
    w = _step_params(natural)
    w["ffn1_norm"] = w["ffn1_norm"] + sent
    early, late = _GradReduce(c), _GradReduce(c)

    def on_grads(tag, g):
        if tag == "ffn2":
            return early.send(tag, [g[k] for k in FFN_NAMES[3:]])
        packed = _pack_shards(_sharded_grads(g))
        return early.scatter("ffn2", packed[0, :SUBLANES]) + early.start(tag, [packed])

    loss, dx, g = local_step(x[0], mem[0], positions[0], loss_target[0], w, wb, small_weights, ffn2_weights, on_grads)

    sent = late.start("ffn1", [g[k] for k in FFN_NAMES[:3]])
    shards = early.finish(dx[:SUBLANES, :LANES] + sent)
    grad = dict(zip(FFN_NAMES[3:], shards[:3]))
    small_sharded = _unpack_shards(shards[3])
    grad.update({k: view(k, small_sharded[k]) for k in SMALL_SHARDED})
    repl = _problem_repl_grads(g)
    grad.update(_unpack_repl(all_reduce_small("grad_all_reduce_small", _pack_repl(repl, loss))))
    loss = grad.pop("loss")

    delta, new_m, new_v = {}, {}, {}
    small = [k for k in WEIGHT_NAMES if k not in FFN_NAMES and k not in SMALL_SHARDED]
    as2d = lambda a: a.reshape(-1, a.shape[-1])

    def update(k):
        delta[k], new_m[k], new_v[k] = adamw_big("adamw_" + k, as2d(p[k]), as2d(grad[k]), as2d(mom[k]), as2d(var[k]))

    for k in WEIGHT_NAMES:
        if k not in small and k not in FFN_NAMES[:3]:
            update(k)
    ds, nms, nvs = adamw_small("adamw_small", [as2d(p[k]) for k in small], [as2d(grad[k].reshape(p[k].shape)) for k in small],
                               [as2d(mom[k]) for k in small], [as2d(var[k]) for k in small])
    for k, d, nm, nv in zip(small, ds, nms, nvs):
        delta[k], new_m[k], new_v[k] = d, nm, nv

    grad.update(zip(FFN_NAMES[:3], late.finish(delta[FFN_NAMES[-1]])))
    for k in FFN_NAMES[:3]:
        update(k)

    shaped = lambda d, k: view(k, d.reshape(p[k].shape)).reshape(args[k].shape)
    return (loss, dx[None], *[shaped(grad[k], k) for k in WEIGHT_NAMES], *[shaped(delta[k], k) for k in WEIGHT_NAMES],
            *[shaped(new_m[k], k) for k in WEIGHT_NAMES], *[shaped(new_v[k], k) for k in WEIGHT_NAMES])
```

```python
import functools
import math

import jax
import jax.numpy as jnp
import numpy as np
from jax import lax
from jax.experimental import pallas as pl
from jax.experimental.pallas import tpu as pltpu

F32, BF16 = jnp.float32, jnp.bfloat16
EPS = 1e-6
MESH = pl.DeviceIdType.MESH

D_MODEL, D_FF = 1024, 2752
MLA_HEADS, MLA_Q_RANK, MLA_KV_RANK, MLA_NOPE, MLA_ROPE, MLA_V = 4, 384, 256, 128, 64, 128
MLA_QK = MLA_NOPE + MLA_ROPE
HEAD_PAD = 256
SSM_WIDTH, SSM_GROUP, SSM_GROUPS, SSM_STATE = 512, 16, 32, 64
SSM_LANES = SSM_GROUPS * SSM_STATE
XH, XHD = 4, 128
ROPE_THETA = 10000.0
ADAM_LR, ADAM_B1, ADAM_B2, ADAM_EPS, ADAM_WD, ADAM_STEP = 0.001, 0.9, 0.999, 1e-08, 0.01, 10
N_CHIPS, N_CORES, N_DEV = 4, 2, 8

VMEM_LIMIT_BYTES = 56 * 2**20
TOKEN_TILE = 512
FUSED_TILE = 256
FUSED_ROW_BLOCK = 128
FFN_ROW_BLOCK = 256
FFN_WIDE_TILE = 1024
ATTN_TILE = 512
STAT_LANES = 128
SCAN_TIME_TILE = 1024
SCAN_LANE_TILE = 256
SUBLANES = 8


def _params(n_axes):
    return pltpu.CompilerParams(dimension_semantics=("arbitrary",) * n_axes, vmem_limit_bytes=VMEM_LIMIT_BYTES)


def _first(axes):
    cond = None
    for a in axes:
        c = pl.program_id(a) == 0
        cond = c if cond is None else jnp.logical_and(cond, c)
    return cond


def mm(name, xs, ws, out_dtypes, *, trans=False, adds=None, tm=TOKEN_TILE):
    rows = xs[0].shape[0]
    tm = min(tm, rows)
    n_in, n_out = len(xs), len(out_dtypes)
    pairs = [(i, j) for i in range(n_in) for j in range(n_out) if ws[i][j] is not None]
    w_list = [ws[i][j] for (i, j) in pairs]
    adds = list(adds) if adds is not None else [None] * n_out
    add_list = [a for a in adds if a is not None]
    out_cols = [None] * n_out
    for (i, j), w in zip(pairs, w_list):
        out_cols[j] = w.shape[0] if trans else w.shape[1]
    contract = (((1,), (1 if trans else 0,)), ((), ()))

    def body(*refs):
        x_refs = refs[:n_in]
        w_refs = refs[n_in:n_in + len(pairs)]
        a_refs = list(refs[n_in + len(pairs):n_in + len(pairs) + len(add_list)])
        o_refs = refs[n_in + len(pairs) + len(add_list):]
        xb = [None] * n_in
        for j in range(n_out):
            acc = None
            for p, (i, jj) in enumerate(pairs):
                if jj != j:
                    continue
                if xb[i] is None:
                    xb[i] = x_refs[i][...].astype(BF16)
                d = lax.dot_general(xb[i], w_refs[p][...].astype(BF16), contract, preferred_element_type=F32)
                acc = d if acc is None else acc + d
            if adds[j] is not None:
                acc = acc + a_refs.pop(0)[...].astype(F32)
            o_refs[j][...] = acc.astype(o_refs[j].dtype)

    in_specs = ([pl.BlockSpec((tm, x.shape[1]), lambda i: (i, 0)) for x in xs]
                + [pl.BlockSpec(w.shape, lambda i: (0, 0)) for w in w_list]
                + [pl.BlockSpec((tm, a.shape[1]), lambda i: (i, 0)) for a in add_list])
    outs = pl.pallas_call(
        body, name=name, grid=(rows // tm,), in_specs=in_specs,
        out_specs=[pl.BlockSpec((tm, n), lambda i: (i, 0)) for n in out_cols],
        out_shape=[jax.ShapeDtypeStruct((rows, n), dt) for n, dt in zip(out_cols, out_dtypes)],
        compiler_params=_params(1),
    )(*xs, *w_list, *add_list)
    return list(outs)


def mm_tn(name, xs, dys, pairs, *, tm=TOKEN_TILE):
    rows = xs[0].shape[0]
    tm = min(tm, rows)
    n_x, n_dy = len(xs), len(dys)
    contract = (((0,), (0,)), ((), ()))

    def body(*refs):
        x_refs, dy_refs, o_refs = refs[:n_x], refs[n_x:n_x + n_dy], refs[n_x + n_dy:]
        @pl.when(pl.program_id(0) == 0)
        def _():
            for o in o_refs:
                o[...] = jnp.zeros_like(o)

        for k, (i, js) in enumerate(pairs):
            dy = None
            for j in js:
                t = dy_refs[j][...].astype(F32)
                dy = t if dy is None else dy + t
            o_refs[k][...] += lax.dot_general(x_refs[i][...].astype(BF16), dy.astype(BF16), contract,
                                              preferred_element_type=F32)

    shapes = [(xs[i].shape[1], dys[js[0]].shape[1]) for (i, js) in pairs]
    outs = pl.pallas_call(
        body, name=name, grid=(rows // tm,),
        in_specs=[pl.BlockSpec((tm, a.shape[1]), lambda i: (i, 0)) for a in (*xs, *dys)],
        out_specs=[pl.BlockSpec(s, lambda i: (0, 0)) for s in shapes],
        out_shape=[jax.ShapeDtypeStruct(s, F32) for s in shapes],
        compiler_params=_params(1),
    )(*xs, *dys)
    return list(outs)


class In:
    def __init__(self, arr, block=None, imap=None, *, diff=False, acc=None, grad=None, weight=False, rows=False):
        self.arr, self.block, self.imap, self.diff, self.acc, self.grad = arr, block, imap, diff, acc, grad
        self.weight, self.rows = weight, rows

    def spec(self):
        return pl.BlockSpec(memory_space=pltpu.VMEM) if self.weight else pl.BlockSpec(self.block, self.imap)


class Out:
    def __init__(self, shape, dtype, block, imap):
        self.shape, self.dtype, self.block, self.imap = shape, dtype, block, imap

    def spec(self):
        return pl.BlockSpec(self.block, self.imap)


class Wt:
    def __init__(self, ref, zeros=None):
        self.ref, self.zeros = ref, zeros


@jax.custom_vjp
def _wdot(a, w, z):
    return jnp.dot(a.astype(BF16), w, preferred_element_type=F32)


def _wdot_fwd(a, w, z):
    return _wdot(a, w, z), (a, w)


def _wdot_bwd(res, g):
    a, w = res
    gb = g.astype(BF16)
    da = lax.dot_general(gb, w, (((1,), (1,)), ((), ())), preferred_element_type=F32)
    dz = lax.dot_general(a.astype(BF16), gb, (((0,), (0,)), ((), ())), preferred_element_type=F32)
    return da, None, dz


_wdot.defvjp(_wdot_fwd, _wdot_bwd)


def wdot(a, wt, head=None):
    w = wt.ref[...] if head is None else wt.ref[head]
    if wt.zeros is None:
        return jnp.dot(a.astype(BF16), w, preferred_element_type=F32)
    return _wdot(a, w, wt.zeros[0 if head is None else head])


def seg_fwd(name, f, grid, ins, outs):
    n_in = len(ins)

    def body(*refs):
        res = f(*[Wt(r) if i.weight else r[...] for i, r in zip(ins, refs[:n_in])])
        for o_ref, r in zip(refs[n_in:], res):
            o_ref[...] = r.astype(o_ref.dtype)

    res = pl.pallas_call(
        body, name=name, grid=grid, in_specs=[i.spec() for i in ins], out_specs=[o.spec() for o in outs],
        out_shape=[jax.ShapeDtypeStruct(o.shape, o.dtype) for o in outs], compiler_params=_params(len(grid)),
    )(*[i.arr for i in ins])
    return list(res)


def seg_bwd(name, f, grid, ins, outs, cts, adds=None, row_block=None):
    n_in, n_ct = len(ins), len(cts)
    grad_idx = [k for k, i in enumerate(ins) if i.diff or i.weight]
    adds = adds or {}
    add_keys = sorted(adds)
    add_list = [adds[k] for k in add_keys]
    heads = {k: (ins[k].arr.shape[0] if ins[k].arr.ndim == 3 else 1) for k in grad_idx if ins[k].weight}
    tile_rows = outs[0].block[0]
    blocks = [None] if row_block is None else [pl.ds(r0, row_block) for r0 in range(0, tile_rows, row_block)]

    def body(*refs):
        in_refs, ct_refs = refs[:n_in], refs[n_in:n_in + n_ct]
        add_refs = dict(zip(add_keys, refs[n_in + n_ct:n_in + n_ct + len(add_list)]))
        g_refs = dict(zip(grad_idx, refs[n_in + n_ct + len(add_list):]))
        for k in grad_idx:
            if ins[k].weight or ins[k].acc is not None:
                @pl.when(_first(range(len(grid)) if ins[k].weight else ins[k].acc))
                def _(k=k):
                    g_refs[k][...] = jnp.zeros_like(g_refs[k])

        for rows in blocks:
            at = lambda ref, sliced: ref[...] if rows is None or not sliced else ref[rows, :]
            vals = [None if i.weight else at(r, i.rows) for i, r in zip(ins, in_refs)]
            primals, owner = [], []
            for k in grad_idx:
                if ins[k].weight:
                    for h in range(heads[k]):
                        primals.append(jnp.zeros(ins[k].arr.shape[-2:], F32))
                        owner.append((k, h))
                else:
                    primals.append(vals[k].astype(F32))
                    owner.append((k, None))

            def g(*dv, vals=vals, owner=owner):
                full = list(vals)
                zeros = {}
                for (k, h), v in zip(owner, dv):
                    if h is None:
                        full[k] = v
                    else:
                        zeros.setdefault(k, []).append(v)
                for k, z in zeros.items():
                    full[k] = Wt(in_refs[k], z)
                return tuple(f(*full))

            _, pull = jax.vjp(g, *primals)
            grads = pull(tuple(at(c, True).astype(F32) for c in ct_refs))
            for (k, h), gr in zip(owner, grads):
                if ins[k].weight:
                    if ins[k].arr.ndim == 3:
                        g_refs[k][h] += gr
                    else:
                        g_refs[k][...] += gr
                    continue
                if k in add_refs:
                    gr = gr + at(add_refs[k], True).astype(F32)
                if ins[k].acc is not None:
                    g_refs[k][...] += gr
                elif rows is None or not ins[k].rows:
                    g_refs[k][...] = gr.astype(g_refs[k].dtype)
                else:
                    g_refs[k][rows, :] = gr.astype(g_refs[k].dtype)

    g_specs, g_shapes = [], []
    for k in grad_idx:
        i = ins[k]
        if i.weight:
            g_specs.append(pl.BlockSpec(memory_space=pltpu.VMEM))
            g_shapes.append(jax.ShapeDtypeStruct(i.arr.shape, F32))
            continue
        shape, block, imap = i.grad if i.grad is not None else (i.arr.shape, i.block, i.imap)
        g_specs.append(pl.BlockSpec(block, imap))
        g_shapes.append(jax.ShapeDtypeStruct(shape, F32))
    in_specs = ([i.spec() for i in ins] + [o.spec() for o in outs]
                + [pl.BlockSpec(ins[k].block, ins[k].imap) for k in add_keys])
    res = pl.pallas_call(
        body, name=name, grid=grid, in_specs=in_specs, out_specs=g_specs, out_shape=g_shapes,
        compiler_params=_params(len(grid)),
    )(*[i.arr for i in ins], *cts, *add_list)
    return list(res)


def _rms(x, g):
    return x * lax.rsqrt(jnp.mean(x * x, axis=-1, keepdims=True) + EPS) * g


@jax.custom_vjp
def _bdot_nt(a, b):
    return lax.dot_general(a.astype(BF16), b.astype(BF16), (((1,), (1,)), ((), ())), preferred_element_type=F32)


def _bdot_nt_fwd(a, b):
    return _bdot_nt(a, b), (a, b)


def _bdot_nt_bwd(res, g):
    a, b = res
    gb = g.astype(BF16)
    da = lax.dot_general(gb, b.astype(BF16), (((1,), (0,)), ((), ())), preferred_element_type=F32)
    db = lax.dot_general(gb, a.astype(BF16), (((0,), (0,)), ((), ())), preferred_element_type=F32)
    return da, db


_bdot_nt.defvjp(_bdot_nt_fwd, _bdot_nt_bwd)


@jax.custom_vjp
def _bdot_nn(a, b):
    return lax.dot_general(a.astype(BF16), b.astype(BF16), (((1,), (0,)), ((), ())), preferred_element_type=F32)


def _bdot_nn_fwd(a, b):
    return _bdot_nn(a, b), (a, b)


def _bdot_nn_bwd(res, g):
    a, b = res
    gb = g.astype(BF16)
    da = lax.dot_general(gb, b.astype(BF16), (((1,), (1,)), ((), ())), preferred_element_type=F32)
    db = lax.dot_general(a.astype(BF16), gb, (((0,), (0,)), ((), ())), preferred_element_type=F32)
    return da, db


_bdot_nn.defvjp(_bdot_nn_fwd, _bdot_nn_bwd)


@functools.partial(jax.custom_vjp, nondiff_argnums=(1,))
def _lane_roll(x, shift):
    return pltpu.roll(x, shift, 1)


def _lane_roll_fwd(x, shift):
    return pltpu.roll(x, shift, 1), None


def _lane_roll_bwd(shift, _, g):
    return (pltpu.roll(g, (g.shape[1] - shift) % g.shape[1], 1),)


_lane_roll.defvjp(_lane_roll_fwd, _lane_roll_bwd)


def _hp_dot(a, b):
    return jnp.dot(a, b, precision=lax.Precision.HIGHEST, preferred_element_type=F32)


NT_DIMS = (((1,), (1,)), ((), ()))
TN_DIMS = (((0,), (0,)), ((), ()))


def ffn_fwd(name, x, gain, wg, wu, wd, target=None, tm=FFN_WIDE_TILE):
    S, D = x.shape
    tm = min(tm, S)
    nsh, Fs, _ = wg.shape
    with_loss = target is not None

    def body(*refs):
        if with_loss:
            x_ref, gain_ref, wg_ref, wu_ref, wd_ref, t_ref, xo_ref, h_ref, g_ref, u_ref, part_ref, acc = refs
        else:
            x_ref, gain_ref, wg_ref, wu_ref, wd_ref, xo_ref, h_ref, g_ref, u_ref, acc = refs
        j = pl.program_id(1)

        @pl.when(j == 0)
        def _():
            h_ref[...] = _rms(x_ref[...], gain_ref[...]).astype(BF16)
            acc[...] = jnp.zeros_like(acc)

        h = h_ref[...]
        g = lax.dot_general(h, wg_ref[...], NT_DIMS, preferred_element_type=F32)
        u = lax.dot_general(h, wu_ref[...], NT_DIMS, preferred_element_type=F32)
        g_ref[...] = g.astype(BF16)
        u_ref[...] = u.astype(BF16)
        a = g * jax.nn.sigmoid(g) * u
        acc[...] += jnp.dot(a.astype(BF16), wd_ref[...], preferred_element_type=F32)

        @pl.when(j == nsh - 1)
        def _():
            y = x_ref[...] + 0.5 * acc[...]
            if with_loss:
                err = y - t_ref[...]
                xo_ref[...] = err * (1.0 / D)
                part_ref[...] = jnp.full(part_ref.shape, 0.5 * jnp.sum(jnp.mean(err * err, axis=-1)), F32)
            else:
                xo_ref[...] = y

    rows = pl.BlockSpec((tm, D), lambda i, j: (i, 0))
    wspec = pl.BlockSpec((None, Fs, D), lambda i, j: (j, 0, 0))
    act = pl.BlockSpec((None, tm, Fs), lambda i, j: (j, i, 0))
    in_specs, args = [rows, pl.BlockSpec((1, D), lambda i, j: (0, 0)), wspec, wspec, wspec], [x, gain, wg, wu, wd]
    out_specs = [rows, rows, act, act]
    out_shape = [jax.ShapeDtypeStruct((S, D), F32), jax.ShapeDtypeStruct((S, D), BF16),
                 jax.ShapeDtypeStruct((nsh, S, Fs), BF16), jax.ShapeDtypeStruct((nsh, S, Fs), BF16)]
    if with_loss:
        in_specs.append(rows)
        args.append(target)
        out_specs.append(pl.BlockSpec((SUBLANES, 128), lambda i, j: (i, 0)))
        out_shape.append(jax.ShapeDtypeStruct((S // tm * SUBLANES, 128), F32))
    return pl.pallas_call(
        body, name=name, grid=(S // tm, nsh), in_specs=in_specs, out_specs=out_specs, out_shape=out_shape,
        scratch_shapes=[pltpu.VMEM((tm, D), F32)], compiler_params=_params(2),
    )(*args)


def ffn_bwd_act(name, dxo, x, gain, g, u, wg, wu, wd, tm=TOKEN_TILE):
    S, D = x.shape
    tm = min(tm, S)
    nsh, Fs, _ = wg.shape

    def body(dxo_ref, x_ref, gain_ref, g_ref, u_ref, wg_ref, wu_ref, wd_ref, dx_ref, dgain_ref, dg_ref, du_ref, dd, dh):
        i, j = pl.program_id(0), pl.program_id(1)

        @pl.when(j == 0)
        def _():
            dd[...] = (0.5 * dxo_ref[...]).astype(BF16)
            dh[...] = jnp.zeros_like(dh)

        for r0 in range(0, tm, FFN_ROW_BLOCK):
            rows = pl.ds(r0, FFN_ROW_BLOCK)
            da = lax.dot_general(dd[rows, :], wd_ref[...], NT_DIMS, preferred_element_type=F32)
            gf, uf = g_ref[rows, :].astype(F32), u_ref[rows, :].astype(F32)
            sig = jax.nn.sigmoid(gf)
            dgv = (da * uf * (sig * (1.0 + gf * (1.0 - sig)))).astype(BF16)
            duv = (da * (gf * sig)).astype(BF16)
            dg_ref[rows, :] = dgv
            du_ref[rows, :] = duv
            dh[rows, :] += (jnp.dot(dgv, wg_ref[...], preferred_element_type=F32)
                            + jnp.dot(duv, wu_ref[...], preferred_element_type=F32))

        @pl.when(j == nsh - 1)
        def _():
            xv = x_ref[...]
            r = lax.rsqrt(jnp.mean(xv * xv, axis=-1, keepdims=True) + EPS)
            xhat = xv * r
            dhv = dh[...]
            dxn = dhv * gain_ref[...]
            dx_ref[...] = dxo_ref[...] + r * (dxn - xhat * jnp.mean(dxn * xhat, axis=-1, keepdims=True))
            part = jnp.sum(dhv * xhat, axis=0, keepdims=True)

            @pl.when(i == 0)
            def _():
                dgain_ref[...] = part

            @pl.when(i != 0)
            def _():
                dgain_ref[...] += part

    return pl.pallas_call(
        body, name=name, grid=(S // tm, nsh),
        in_specs=[pl.BlockSpec((tm, D), lambda i, j: (i, 0)), pl.BlockSpec((tm, D), lambda i, j: (i, 0)),
                  pl.BlockSpec((1, D), lambda i, j: (0, 0)),
                  pl.BlockSpec((None, tm, Fs), lambda i, j: (j, i, 0)), pl.BlockSpec((None, tm, Fs), lambda i, j: (j, i, 0)),
                  pl.BlockSpec((None, Fs, D), lambda i, j: (j, 0, 0)), pl.BlockSpec((None, Fs, D), lambda i, j: (j, 0, 0)),
                  pl.BlockSpec((None, Fs, D), lambda i, j: (j, 0, 0))],
        out_specs=[pl.BlockSpec((tm, D), lambda i, j: (i, 0)), pl.BlockSpec((1, D), lambda i, j: (0, 0)),
                   pl.BlockSpec((None, tm, Fs), lambda i, j: (j, i, 0)), pl.BlockSpec((None, tm, Fs), lambda i, j: (j, i, 0))],
        out_shape=[jax.ShapeDtypeStruct((S, D), F32), jax.ShapeDtypeStruct((1, D), F32),
                   jax.ShapeDtypeStruct((nsh, S, Fs), BF16), jax.ShapeDtypeStruct((nsh, S, Fs), BF16)],
        scratch_shapes=[pltpu.VMEM((tm, D), BF16), pltpu.VMEM((tm, D), F32)], compiler_params=_params(2),
    )(dxo, x, gain, g, u, wg, wu, wd)


def ffn_bwd_w(name, h, dxo, g, u, dg, du, tm=FFN_WIDE_TILE):
    S, D = h.shape
    tm = min(tm, S)
    nsh, _, Fs = g.shape

    def body(h_ref, dxo_ref, g_ref, u_ref, dg_ref, du_ref, dwg_ref, dwu_ref, dwd_ref):
        i = pl.program_id(1)
        gf, uf = g_ref[...].astype(F32), u_ref[...].astype(F32)
        a = (gf * jax.nn.sigmoid(gf) * uf).astype(BF16)
        dd = (0.5 * dxo_ref[...]).astype(BF16)
        hv = h_ref[...]

        @pl.when(i == 0)
        def _():
            dwg_ref[...] = jnp.zeros_like(dwg_ref)
            dwu_ref[...] = jnp.zeros_like(dwu_ref)
            dwd_ref[...] = jnp.zeros_like(dwd_ref)

        dwg_ref[...] += lax.dot_general(dg_ref[...], hv, TN_DIMS, preferred_element_type=F32)
        dwu_ref[...] += lax.dot_general(du_ref[...], hv, TN_DIMS, preferred_element_type=F32)
        dwd_ref[...] += lax.dot_general(a, dd, TN_DIMS, preferred_element_type=F32)

    act = pl.BlockSpec((None, tm, Fs), lambda j, i: (j, i, 0))
    wspec = pl.BlockSpec((None, Fs, D), lambda j, i: (j, 0, 0))
    return pl.pallas_call(
        body, name=name, grid=(nsh, S // tm),
        in_specs=[pl.BlockSpec((tm, D), lambda j, i: (i, 0)), pl.BlockSpec((tm, D), lambda j, i: (i, 0)), act, act, act, act],
        out_specs=[wspec, wspec, wspec], out_shape=[jax.ShapeDtypeStruct((nsh, Fs, D), F32)] * 3,
        compiler_params=_params(2),
    )(h, dxo, g, u, dg, du)


NEG_BIG = -1e30


def _causal_pairs(n, by_key):
    pairs = [(qi, ki) for qi in range(n) for ki in range(qi + 1)]
    if by_key:
        pairs.sort(key=lambda p: (p[1], p[0]))
    return jnp.asarray([p[0] for p in pairs], jnp.int32), jnp.asarray([p[1] for p in pairs], jnp.int32)


def _scores(q, k, masked):
    s = lax.dot_general(q, k, (((1,), (1,)), ((), ())), preferred_element_type=F32)
    if masked:
        row = lax.broadcasted_iota(jnp.int32, s.shape, 0)
        col = lax.broadcasted_iota(jnp.int32, s.shape, 1)
        s = jnp.where(row >= col, s, NEG_BIG)
    return s


def attn_fwd(name, q, k, v, t=ATTN_TILE):
    S, Dk = q.shape[0], HEAD_PAD
    H = q.shape[1] // Dk
    Dv = v.shape[1] // H
    qt, kt = _causal_pairs(S // t, by_key=False)

    def body(qt_ref, kt_ref, q_ref, k_ref, v_ref, o_ref, lse_ref, m_sc, l_sc, acc):
        qi, ki = qt_ref[pl.program_id(1)], kt_ref[pl.program_id(1)]

        @pl.when(ki == 0)
        def _():
            m_sc[...] = jnp.full_like(m_sc, NEG_BIG)
            l_sc[...] = jnp.zeros_like(l_sc)
            acc[...] = jnp.zeros_like(acc)

        def step(masked):
            s = _scores(q_ref[...], k_ref[...], masked)
            m_prev = m_sc[...]
            m_next = jnp.maximum(m_prev, jnp.max(s, axis=-1, keepdims=True))
            alpha = jnp.exp(m_prev - m_next)
            p = jnp.exp(s - jnp.tile(m_next, (1, t // STAT_LANES)))
            l_sc[...] = alpha * l_sc[...] + jnp.sum(p, axis=-1, keepdims=True)
            acc[...] = alpha * acc[...] + jnp.dot(p.astype(BF16), v_ref[...].astype(BF16), preferred_element_type=F32)
            m_sc[...] = m_next

        @pl.when(ki < qi)
        def _():
            step(False)

        @pl.when(ki == qi)
        def _():
            step(True)
            o_ref[...] = acc[...] / l_sc[...]
            lse_ref[...] = m_sc[...] + jnp.log(l_sc[...])

    stat = pltpu.VMEM((t, STAT_LANES), F32)
    return pl.pallas_call(
        body, name=name,
        grid_spec=pltpu.PrefetchScalarGridSpec(
            num_scalar_prefetch=2, grid=(H, qt.shape[0]),
            in_specs=[pl.BlockSpec((t, Dk), lambda h, s, qt, kt: (qt[s], h)),
                      pl.BlockSpec((t, Dk), lambda h, s, qt, kt: (kt[s], h)),
                      pl.BlockSpec((t, Dv), lambda h, s, qt, kt: (kt[s], h))],
            out_specs=[pl.BlockSpec((t, Dv), lambda h, s, qt, kt: (qt[s], h)),
                       pl.BlockSpec((None, t, STAT_LANES), lambda h, s, qt, kt: (h, qt[s], 0))],
            scratch_shapes=[stat, stat, pltpu.VMEM((t, Dv), F32)]),
        out_shape=[jax.ShapeDtypeStruct((S, H * Dv), F32), jax.ShapeDtypeStruct((H, S, STAT_LANES), F32)],
        compiler_params=_params(2),
    )(qt, kt, q, k, v)


def attn_bwd(name, q, k, v, do, o, lse, t=ATTN_TILE):
    S, Dk = q.shape[0], HEAD_PAD
    H = q.shape[1] // Dk
    Dv = v.shape[1] // H
    qt, kt = _causal_pairs(S // t, by_key=True)
    tn_dims = (((0,), (0,)), ((), ()))

    def body(qt_ref, kt_ref, q_ref, k_ref, v_ref, do_ref, o_ref, lse_ref, dq_ref, dk_ref, dv_ref):
        step_id = pl.program_id(1)
        qi, ki = qt_ref[step_id], kt_ref[step_id]

        @pl.when(step_id == 0)
        def _():
            dq_ref[...] = jnp.zeros_like(dq_ref)

        def step(masked):
            s = _scores(q_ref[...], k_ref[...], masked)
            reps = (1, t // STAT_LANES)
            p = jnp.exp(s - jnp.tile(lse_ref[...], reps))
            dov = do_ref[...]
            delta = jnp.broadcast_to(jnp.sum(dov * o_ref[...], axis=-1, keepdims=True), (t, STAT_LANES))
            dob = dov.astype(BF16)
            dp = lax.dot_general(dob, v_ref[...].astype(BF16), (((1,), (1,)), ((), ())), preferred_element_type=F32)
            ds = (p * (dp - jnp.tile(delta, reps))).astype(BF16)
            pdv = lax.dot_general(p.astype(BF16), dob, tn_dims, preferred_element_type=F32)
            pdk = lax.dot_general(ds, q_ref[...], tn_dims, preferred_element_type=F32)
            rows = pl.ds(pl.multiple_of(qi * t, t), t)
            dq_ref[rows, :] += jnp.dot(ds, k_ref[...], preferred_element_type=F32)
            return pdk, pdv

        @pl.when(ki == qi)
        def _():
            dk_ref[...] = jnp.zeros_like(dk_ref)
            dv_ref[...] = jnp.zeros_like(dv_ref)

        def accumulate(masked):
            pdk, pdv = step(masked)
            dk_ref[...] += pdk
            dv_ref[...] += pdv

        @pl.when(ki == qi)
        def _():
            accumulate(True)

        @pl.when(ki < qi)
        def _():
            accumulate(False)

    qrow = lambda h, s, qt, kt: (qt[s], h)
    krow = lambda h, s, qt, kt: (kt[s], h)
    return pl.pallas_call(
        body, name=name,
        grid_spec=pltpu.PrefetchScalarGridSpec(
            num_scalar_prefetch=2, grid=(H, qt.shape[0]),
            in_specs=[pl.BlockSpec((t, Dk), qrow), pl.BlockSpec((t, Dk), krow), pl.BlockSpec((t, Dv), krow),
                      pl.BlockSpec((t, Dv), qrow), pl.BlockSpec((t, Dv), qrow),
                      pl.BlockSpec((None, t, STAT_LANES), lambda h, s, qt, kt: (h, qt[s], 0))],
            out_specs=[pl.BlockSpec((S, Dk), lambda h, s, qt, kt: (0, h)), pl.BlockSpec((t, Dk), krow),
                       pl.BlockSpec((t, Dv), krow)]),
        out_shape=[jax.ShapeDtypeStruct((S, H * Dk), F32), jax.ShapeDtypeStruct((S, H * Dk), F32),
                   jax.ShapeDtypeStruct((S, H * Dv), F32)],
        compiler_params=_params(2),
    )(qt, kt, q, k, v, do, o, lse)


def _cmul(ar, ai, br, bi):
    return ar * br - ai * bi, ar * bi + ai * br


def _scan_tile(x_r, x_i, ar_ref, ai_ref, cr_sc, ci_sc, *, reverse, first, states=None):
    tc, lanes = x_r.shape
    nblk, lb = tc // SUBLANES, SCAN_LANE_TILE
    with_da = states is not None
    if with_da:
        xr_all, xi_all, pr_all, pi_all, dar_all, dai_all, chunk = states

    @pl.when(first)
    def _():
        cr_sc[...] = jnp.zeros_like(cr_sc)
        ci_sc[...] = jnp.zeros_like(ci_sc)
        if with_da:
            dar_all[...] = jnp.zeros_like(dar_all)
            dai_all[...] = jnp.zeros_like(dai_all)

    row = lax.broadcasted_iota(jnp.int32, (SUBLANES, lb), 0)
    for l0 in range(0, lanes, lb):
        _scan_lanes(x_r.at[:, pl.ds(l0, lb)], x_i.at[:, pl.ds(l0, lb)], ar_ref[0:1, pl.ds(l0, lb)],
                    ai_ref[0:1, pl.ds(l0, lb)], cr_sc.at[:, pl.ds(l0, lb)], ci_sc.at[:, pl.ds(l0, lb)], row, reverse,
                    nblk, None if not with_da else tuple(r.at[:, pl.ds(l0, lb)] for r in states[:6]) + (chunk,))


def _scan_lanes(x_r, x_i, a1r, a1i, cr_sc, ci_sc, row, reverse, nblk, states):
    lb = x_r.shape[1]
    with_da = states is not None
    if with_da:
        xr_ref, xi_ref, pr_ref, pi_ref, dar_ref, dai_ref, chunk = states
    if reverse:
        a1i = -a1i
    a2r, a2i = _cmul(a1r, a1i, a1r, a1i)
    a4r, a4i = _cmul(a2r, a2i, a2r, a2i)
    pw_r, pw_i = jnp.zeros((SUBLANES, lb), F32), jnp.zeros((SUBLANES, lb), F32)
    cur_r, cur_i = a1r, a1i
    for e in range(SUBLANES):
        r_at = (SUBLANES - 1 - e) if reverse else e
        pw_r = jnp.where(row == r_at, cur_r, pw_r)
        pw_i = jnp.where(row == r_at, cur_i, pw_i)
        cur_r, cur_i = _cmul(cur_r, cur_i, a1r, a1i)
    steps = []
    for d, pr, pi in ((1, a1r, a1i), (2, a2r, a2i), (4, a4r, a4i)):
        keep = (row < SUBLANES - d) if reverse else (row >= d)
        steps.append((d, jnp.where(keep, pr, 0.0), jnp.where(keep, pi, 0.0)))

    def block(jb, carry):
        if with_da:
            cr, ci, acc_r, acc_i = carry
        else:
            cr, ci = carry
        idx = (nblk - 1 - jb) if reverse else jb
        r0 = pl.multiple_of(idx * SUBLANES, SUBLANES)
        xr = x_r[pl.ds(r0, SUBLANES), :]
        xi = x_i[pl.ds(r0, SUBLANES), :]
        for d, pr, pi in steps:
            shift = SUBLANES - d if reverse else d
            sr, si = pltpu.roll(xr, shift, 0), pltpu.roll(xi, shift, 0)
            xr, xi = xr + pr * sr - pi * si, xi + pr * si + pi * sr
        xr, xi = xr + pw_r * cr - pw_i * ci, xi + pw_r * ci + pw_i * cr
        x_r[pl.ds(r0, SUBLANES), :] = xr
        x_i[pl.ds(r0, SUBLANES), :] = xi
        edge = 0 if reverse else SUBLANES - 1
        cr, ci = xr[edge:edge + 1, :], xi[edge:edge + 1, :]
        if not with_da:
            return cr, ci
        fr = xr_ref[pl.ds(r0, SUBLANES), :]
        fi = xi_ref[pl.ds(r0, SUBLANES), :]
        rp = pl.multiple_of(jnp.maximum(idx - 1, 0) * SUBLANES, SUBLANES)
        inside = idx > 0
        before_r = jnp.where(inside, xr_ref[pl.ds(rp, SUBLANES), :], pr_ref[...])
        before_i = jnp.where(inside, xi_ref[pl.ds(rp, SUBLANES), :], pi_ref[...])
        live = jnp.where(jnp.logical_or(inside, chunk > 0), 1.0, 0.0)
        last_r = before_r[SUBLANES - 1:SUBLANES, :] * live
        last_i = before_i[SUBLANES - 1:SUBLANES, :] * live
        pvr = jnp.where(row == 0, last_r, pltpu.roll(fr, 1, 0))
        pvi = jnp.where(row == 0, last_i, pltpu.roll(fi, 1, 0))
        acc_r = acc_r + xr * pvr + xi * pvi
        acc_i = acc_i + xi * pvr - xr * pvi
        return cr, ci, acc_r, acc_i

    init = (cr_sc[...], ci_sc[...])
    if with_da:
        init = init + (jnp.zeros((SUBLANES, lb), F32), jnp.zeros((SUBLANES, lb), F32))
    fin = lax.fori_loop(0, nblk, block, init)
    cr_sc[...] = fin[0]
    ci_sc[...] = fin[1]
    if with_da:
        dar_ref[...] += fin[2]
        dai_ref[...] += fin[3]


SSM_BLOCKS = 4
BLOCK_CH = SSM_WIDTH // SSM_BLOCKS
PREP_LANES = SSM_LANES // SSM_BLOCKS


def ssm_fwd(name, u, bb_r, bb_i, cb_r, ncb_i, a_r8, a_i8):
    S = u.shape[0]
    tc = min(SCAN_TIME_TILE, S)

    def body(u_ref, bbr_ref, bbi_ref, cbr_ref, ncbi_ref, ar_ref, ai_ref, xr_ref, xi_ref, y_ref, cr_sc, ci_sc):
        ub = u_ref[...].astype(BF16)
        xr_ref[...] = jnp.dot(ub, bbr_ref[...], preferred_element_type=F32)
        xi_ref[...] = jnp.dot(ub, bbi_ref[...], preferred_element_type=F32)
        _scan_tile(xr_ref, xi_ref, ar_ref, ai_ref, cr_sc, ci_sc, reverse=False, first=pl.program_id(1) == 0)
        y_ref[...] = (lax.dot_general(xr_ref[...].astype(BF16), cbr_ref[...], NT_DIMS, preferred_element_type=F32)
                      + lax.dot_general(xi_ref[...].astype(BF16), ncbi_ref[...], NT_DIMS, preferred_element_type=F32))

    ch = pl.BlockSpec((tc, BLOCK_CH), lambda b, t: (t, b))
    st = pl.BlockSpec((tc, PREP_LANES), lambda b, t: (t, b))
    wt = pl.BlockSpec((None, BLOCK_CH, PREP_LANES), lambda b, t: (b, 0, 0))
    par = pl.BlockSpec((SUBLANES, PREP_LANES), lambda b, t: (0, b))
    return pl.pallas_call(
        body, name=name, grid=(SSM_BLOCKS, S // tc), in_specs=[ch, wt, wt, wt, wt, par, par], out_specs=[st, st, ch],
        out_shape=[jax.ShapeDtypeStruct((S, SSM_LANES), F32), jax.ShapeDtypeStruct((S, SSM_LANES), F32),
                   jax.ShapeDtypeStruct((S, SSM_WIDTH), F32)],
        scratch_shapes=[pltpu.VMEM((1, PREP_LANES), F32), pltpu.VMEM((1, PREP_LANES), F32)], compiler_params=_params(2),
    )(u, bb_r, bb_i, cb_r, ncb_i, a_r8, a_i8)


def ssm_bwd(name, dy, du_add, u, xs_r, xs_i, bb_r, bb_i, cb_r, ncb_i, a_r8, a_i8):
    S = u.shape[0]
    tc = min(SCAN_TIME_TILE, S)
    nt = S // tc

    def body(dy_ref, dua_ref, u_ref, xr_ref, xi_ref, pr_ref, pi_ref, bbr_ref, bbi_ref, cbr_ref, ncbi_ref, ar_ref, ai_ref,
             du_ref, dbbr_ref, dbbi_ref, dcbr_ref, dncbi_ref, dar_ref, dai_ref, lr_sc, li_sc, cr_sc, ci_sc):
        t = pl.program_id(1)
        first = t == 0

        @pl.when(first)
        def _():
            for r in (dbbr_ref, dbbi_ref, dcbr_ref, dncbi_ref):
                r[...] = jnp.zeros_like(r)

        dyb = dy_ref[...].astype(BF16)
        lr_sc[...] = jnp.dot(dyb, cbr_ref[...], preferred_element_type=F32)
        li_sc[...] = jnp.dot(dyb, ncbi_ref[...], preferred_element_type=F32)
        _scan_tile(lr_sc, li_sc, ar_ref, ai_ref, cr_sc, ci_sc, reverse=True, first=first,
                   states=(xr_ref, xi_ref, pr_ref, pi_ref, dar_ref, dai_ref, nt - 1 - t))
        lrb, lib = lr_sc[...].astype(BF16), li_sc[...].astype(BF16)
        du_ref[...] = (lax.dot_general(lrb, bbr_ref[...], NT_DIMS, preferred_element_type=F32)
                       + lax.dot_general(lib, bbi_ref[...], NT_DIMS, preferred_element_type=F32) + dua_ref[...])
        ub = u_ref[...].astype(BF16)
        dbbr_ref[...] += lax.dot_general(ub, lrb, TN_DIMS, preferred_element_type=F32)
        dbbi_ref[...] += lax.dot_general(ub, lib, TN_DIMS, preferred_element_type=F32)
        dcbr_ref[...] += lax.dot_general(dyb, xr_ref[...].astype(BF16), TN_DIMS, preferred_element_type=F32)
        dncbi_ref[...] += lax.dot_general(dyb, xi_ref[...].astype(BF16), TN_DIMS, preferred_element_type=F32)

    ch = pl.BlockSpec((tc, BLOCK_CH), lambda b, t: (nt - 1 - t, b))
    st = pl.BlockSpec((tc, PREP_LANES), lambda b, t: (nt - 1 - t, b))
    prev = pl.BlockSpec((SUBLANES, PREP_LANES), lambda b, t: (jnp.maximum((nt - 1 - t) * (tc // SUBLANES) - 1, 0), b))
    wt = pl.BlockSpec((None, BLOCK_CH, PREP_LANES), lambda b, t: (b, 0, 0))
    par = pl.BlockSpec((SUBLANES, PREP_LANES), lambda b, t: (0, b))
    blk = jax.ShapeDtypeStruct((SSM_BLOCKS, BLOCK_CH, PREP_LANES), F32)
    rows8 = jax.ShapeDtypeStruct((SUBLANES, SSM_LANES), F32)
    return pl.pallas_call(
        body, name=name, grid=(SSM_BLOCKS, nt), in_specs=[ch, ch, ch, st, st, prev, prev, wt, wt, wt, wt, par, par],
        out_specs=[ch, wt, wt, wt, wt, par, par],
        out_shape=[jax.ShapeDtypeStruct((S, SSM_WIDTH), F32), blk, blk, blk, blk, rows8, rows8],
        scratch_shapes=[pltpu.VMEM((tc, PREP_LANES), F32), pltpu.VMEM((tc, PREP_LANES), F32),
                        pltpu.VMEM((1, PREP_LANES), F32), pltpu.VMEM((1, PREP_LANES), F32)],
        compiler_params=_params(2),
    )(dy, du_add, u, xs_r, xs_i, xs_r, xs_i, bb_r, bb_i, cb_r, ncb_i, a_r8, a_i8)


def _ssm_prep_f(a_re, a_im, log_dt, bt_re, bt_im, c_re, c_im):
    first_group = pl.program_id(0) * (SSM_GROUPS // SSM_BLOCKS)
    iota = lambda shape, d: lax.broadcasted_iota(jnp.int32, shape, d)
    grp_of_row = lambda shape: iota(shape, 0) >> int(math.log2(SSM_GROUP))
    grp_of_lane = lambda shape: iota(shape, 1) >> int(math.log2(SSM_STATE))
    rep = (grp_of_row((BLOCK_CH, SSM_GROUPS)) + first_group == iota((BLOCK_CH, SSM_GROUPS), 1)).astype(F32)
    til = ((iota((SSM_STATE, PREP_LANES), 1) & (SSM_STATE - 1)) == iota((SSM_STATE, PREP_LANES), 0)).astype(F32)
    m_rows = (grp_of_row((BLOCK_CH, PREP_LANES)) == grp_of_lane((BLOCK_CH, PREP_LANES))).astype(F32)
    m_grp = (iota((SSM_GROUPS, PREP_LANES), 0) == grp_of_lane((SSM_GROUPS, PREP_LANES)) + first_group).astype(F32)
    dt = jnp.exp(log_dt)
    decay = jnp.exp(a_re * dt)
    ar = decay * jnp.cos(a_im * dt)
    ai = decay * jnp.sin(a_im * dt)
    den = a_re * a_re + a_im * a_im
    nr = ar - 1.0
    coef_r = (nr * a_re + ai * a_im) / den
    coef_i = (ai * a_re - nr * a_im) / den
    cr, ci = _hp_dot(rep, coef_r), _hp_dot(rep, coef_i)
    bb_r = cr * bt_re - ci * bt_im
    bb_i = cr * bt_im + ci * bt_re
    big = lambda m: _hp_dot(m, til) * m_rows
    lanes = lambda m: jnp.broadcast_to(jnp.sum(_hp_dot(m, til) * m_grp, axis=0, keepdims=True), (SUBLANES, PREP_LANES))
    return lanes(ar), lanes(ai), big(bb_r), big(bb_i), big(c_re), -big(c_im)


def _whole(arr, **kw):
    nd = arr.ndim
    return In(arr, arr.shape, lambda *_: (0,) * nd, **kw)


def _adamw_math(w, g, m, v):
    m = ADAM_B1 * m + (1.0 - ADAM_B1) * g
    v = ADAM_B2 * v + (1.0 - ADAM_B2) * (g * g)
    m_hat = m / (1.0 - ADAM_B1 ** ADAM_STEP)
    v_hat = v / (1.0 - ADAM_B2 ** ADAM_STEP)
    delta = -ADAM_LR * (m_hat / (jnp.sqrt(v_hat) + ADAM_EPS) + ADAM_WD * w)
    return delta, m, v


def adamw_big(name, w, g, m, v):
    R, C = w.shape
    tr = R
    for cand in (512, 344, 256, 128):
        if R % cand == 0:
            tr = cand
            break

    def body(w_ref, g_ref, m_ref, v_ref, d_ref, nm_ref, nv_ref):
        d, nm, nv = _adamw_math(w_ref[...], g_ref[...], m_ref[...], v_ref[...])
        d_ref[...] = d
        nm_ref[...] = nm
        nv_ref[...] = nv

    spec = pl.BlockSpec((tr, C), lambda i: (i, 0))
    return pl.pallas_call(
        body, name=name, grid=(R // tr,), in_specs=[spec] * 4, out_specs=[spec] * 3,
        out_shape=[jax.ShapeDtypeStruct((R, C), F32)] * 3, compiler_params=_params(1),
    )(w, g, m, v)


def adamw_small(name, ws, gs, ms, vs):
    n = len(ws)

    def body(*refs):
        for k in range(n):
            d, nm, nv = _adamw_math(refs[k][...], refs[n + k][...], refs[2 * n + k][...], refs[3 * n + k][...])
            refs[4 * n + k][...] = d
            refs[5 * n + k][...] = nm
            refs[6 * n + k][...] = nv

    vm = pl.BlockSpec(memory_space=pltpu.VMEM)
    shapes = [jax.ShapeDtypeStruct(w.shape, F32) for w in ws]
    res = pl.pallas_call(
        body, name=name, in_specs=[vm] * (4 * n), out_specs=[vm] * (3 * n), out_shape=shapes * 3,
        compiler_params=pltpu.CompilerParams(vmem_limit_bytes=VMEM_LIMIT_BYTES),
    )(*ws, *gs, *ms, *vs)
    return res[:n], res[n:2 * n], res[2 * n:]


def _place():
    return lax.axis_index("x"), lax.axis_index("y"), lax.axis_index("c")


def _other_chips(x, y):
    return [(1 - x, y), (x, 1 - y), (1 - x, 1 - y)]


HBM = pl.BlockSpec(memory_space=pl.ANY)


def all_gather_halves(name, blocks):
    n = len(blocks)

    def body(*refs):
        in_refs, out_refs = refs[:n], refs[n:2 * n]
        send_sems, recv_sems = refs[2 * n:]
        x, y, c = _place()
        me, sibling = (x, y, c), (x, y, 1 - c)
        chips = _other_chips(x, y)

        def slot(a, px, py, pc):
            return out_refs[a].at[4 * px + 2 * py + pc]

        def copy(a, k, block, to, src=None):
            return pltpu.make_async_remote_copy(
                src_ref=slot(a, *block) if src is None else src, dst_ref=slot(a, *block),
                send_sem=send_sems.at[a, k], recv_sem=recv_sems.at[a, k], device_id=to, device_id_type=MESH)

        first = []
        for a in range(n):
            first.append(copy(a, 0, me, sibling, src=in_refs[a]))
            first += [copy(a, 1 + j, me, (*chip, c), src=in_refs[a]) for j, chip in enumerate(chips)]
        for cp in first:
            cp.start()
        passed = []
        for j, chip in enumerate(chips):
            for a in range(n):
                copy(a, 1 + j, (*chip, c), me).wait_recv()
                fw = copy(a, 4 + j, (*chip, c), sibling)
                fw.start()
                passed.append(fw)
        for a in range(n):
            copy(a, 0, sibling, me).wait_recv()
            for j, chip in enumerate(chips):
                copy(a, 4 + j, (*chip, 1 - c), me).wait_recv()
        for cp in first + passed:
            cp.wait_send()

    res = pl.pallas_call(
        body, name=name, in_specs=[HBM] * n, out_specs=[HBM] * n,
        out_shape=[jax.ShapeDtypeStruct((N_DEV,) + b.shape, b.dtype) for b in blocks],
        scratch_shapes=[pltpu.SemaphoreType.DMA((n, 7)), pltpu.SemaphoreType.DMA((n, 7))],
    )(*blocks)
    return list(res)


def pair_send_halves(name, grads):
    n = len(grads)

    def body(*refs):
        in_refs, out_refs = refs[:n], refs[n:2 * n]
        send_sems, recv_sems = refs[2 * n:]
        x, y, c = _place()
        cps = []
        for a in range(n):
            cp = pltpu.make_async_remote_copy(
                src_ref=in_refs[a].at[:, 1 - c], dst_ref=out_refs[a], send_sem=send_sems.at[a], recv_sem=recv_sems.at[a],
                device_id=(x, y, 1 - c), device_id_type=MESH)
            cp.start()
            cps.append(cp)
        for cp in cps:
            cp.wait()

    res = pl.pallas_call(
        body, name=name, in_specs=[HBM] * n, out_specs=[HBM] * n,
        out_shape=[jax.ShapeDtypeStruct((g.shape[0],) + g.shape[2:], g.dtype) for g in grads],
        scratch_shapes=[pltpu.SemaphoreType.DMA((n,)), pltpu.SemaphoreType.DMA((n,))],
    )(*grads)
    return list(res)


def pair_add(name, grad, got, c_arr):
    nsh, _, M, N = grad.shape
    tr = M
    for cand in (512, 256, 192, 128, 64, 16):
        if M % cand == 0:
            tr = cand
            break

    def body(c_ref, g_ref, p_ref, o_ref):
        o_ref[...] = (g_ref[...] + p_ref[...]).astype(BF16)

    return pl.pallas_call(
        body, name=name,
        grid_spec=pltpu.PrefetchScalarGridSpec(
            num_scalar_prefetch=1, grid=(nsh, M // tr),
            in_specs=[pl.BlockSpec((None, None, tr, N), lambda j, i, c_ref: (j, c_ref[0], i, 0)),
                      pl.BlockSpec((None, tr, N), lambda j, i, c_ref: (j, i, 0))],
            out_specs=pl.BlockSpec((None, tr, N), lambda j, i, c_ref: (j, i, 0))),
        out_shape=jax.ShapeDtypeStruct((nsh, M, N), BF16), compiler_params=_params(2),
    )(c_arr, grad, got)


def scatter_to_chips(name, parts):
    n = len(parts)

    def body(*refs):
        in_refs, out_refs = refs[:n], refs[n:2 * n]
        send_sems, recv_sems = refs[2 * n:]
        x, y, c = _place()
        mine = 2 * x + y
        chips = _other_chips(x, y)
        cps = []
        for a in range(n):
            for k, (px, py) in enumerate(chips):
                cp = pltpu.make_async_remote_copy(
                    src_ref=in_refs[a].at[2 * px + py], dst_ref=out_refs[a].at[mine],
                    send_sem=send_sems.at[a, k], recv_sem=recv_sems.at[a, k], device_id=(px, py, c), device_id_type=MESH)
                cp.start()
                cps.append((cp, a, k, px, py))
        for cp, a, k, px, py in cps:
            pltpu.make_async_remote_copy(
                src_ref=in_refs[a].at[mine], dst_ref=out_refs[a].at[2 * px + py],
                send_sem=send_sems.at[a, k], recv_sem=recv_sems.at[a, k], device_id=(px, py, c), device_id_type=MESH).wait_recv()
        for cp, *_ in cps:
            cp.wait_send()

    res = pl.pallas_call(
        body, name=name, in_specs=[HBM] * n, out_specs=[HBM] * n,
        out_shape=[jax.ShapeDtypeStruct(p.shape, p.dtype) for p in parts],
        scratch_shapes=[pltpu.SemaphoreType.DMA((n, 3)), pltpu.SemaphoreType.DMA((n, 3))],
    )(*parts)
    return list(res)


def sum_chips(name, q):
    nsh, M, N = q.shape
    tr = M
    for cand in (512, 256, 192, 128, 64, 16):
        if M % cand == 0:
            tr = cand
            break

    def body(q_ref, o_ref):
        acc = q_ref[0].astype(F32)
        for j in range(1, nsh):
            acc = acc + q_ref[j].astype(F32)
        o_ref[...] = acc

    return pl.pallas_call(
        body, name=name, grid=(M // tr,), in_specs=[pl.BlockSpec((nsh, tr, N), lambda i: (0, i, 0))],
        out_specs=pl.BlockSpec((tr, N), lambda i: (i, 0)), out_shape=jax.ShapeDtypeStruct((M, N), F32),
        compiler_params=_params(1),
    )(q)


def pair_exchange(name, halves):
    n = len(halves)

    def body(*refs):
        in_refs, out_refs = refs[:n], refs[n:2 * n]
        send_sems, recv_sems = refs[2 * n:]
        x, y, c = _place()
        cps = []
        for a in range(n):
            cp = pltpu.make_async_remote_copy(
                src_ref=in_refs[a], dst_ref=out_refs[a], send_sem=send_sems.at[a], recv_sem=recv_sems.at[a],
                device_id=(x, y, 1 - c), device_id_type=MESH)
            cp.start()
            cps.append(cp)
        for cp in cps:
            cp.wait()

    res = pl.pallas_call(
        body, name=name, in_specs=[HBM] * n, out_specs=[HBM] * n,
        out_shape=[jax.ShapeDtypeStruct(h.shape, h.dtype) for h in halves],
        scratch_shapes=[pltpu.SemaphoreType.DMA((n,)), pltpu.SemaphoreType.DMA((n,))],
    )(*halves)
    return list(res)


SEM = pl.BlockSpec(memory_space=pltpu.SEMAPHORE)
IN_HBM = pl.BlockSpec(memory_space=pltpu.HBM)
SPLIT_COPY = pltpu.CompilerParams(has_side_effects=pltpu.SideEffectType.DATAFLOW_SIDE_EFFECTING)


def _scatter_copies(src_refs, dst_refs, send_sems, recv_sems):
    x, y, c = _place()
    mine = 2 * x + y
    return [pltpu.make_async_remote_copy(
        src_ref=src_refs[a].at[2 * px + py], dst_ref=dst_refs[a].at[mine], send_sem=send_sems.at[a * (N_CHIPS - 1) + k],
        recv_sem=recv_sems.at[a * (N_CHIPS - 1) + k], device_id=(px, py, c), device_id_type=MESH)
        for a in range(len(src_refs)) for k, (px, py) in enumerate(_other_chips(x, y))]


def _pair_copies(src_refs, dst_refs, send_sems, recv_sems):
    x, y, c = _place()
    return [pltpu.make_async_remote_copy(
        src_ref=src_refs[a].at[:, 1 - c], dst_ref=dst_refs[a], send_sem=send_sems.at[a], recv_sem=recv_sems.at[a],
        device_id=(x, y, 1 - c), device_id_type=MESH) for a in range(len(src_refs))]


def _gather_copies(src_refs, dst_refs, send_sems, recv_sems):
    x, y, c = _place()
    me = 4 * x + 2 * y + c
    cps = []
    for a in range(len(src_refs)):
        for k in range(1, N_DEV):
            to = (1 - x if k & 4 else x, 1 - y if k & 2 else y, 1 - c if k & 1 else c)
            s = a * (N_DEV - 1) + k - 1
            cps.append(pltpu.make_async_remote_copy(
                src_ref=src_refs[a], dst_ref=dst_refs[a].at[me], send_sem=send_sems.at[s], recv_sem=recv_sems.at[s],
                device_id=to, device_id_type=MESH))
    return cps


def split_copy_start(name, copies, n_sem, srcs, land_shapes):
    n = len(srcs)
    lands = [lax.empty(s.shape, s.dtype) for s in land_shapes]

    def body(*refs):
        for cp in copies(refs[:n], refs[n:2 * n], refs[2 * n], refs[2 * n + 1]):
            cp.start()
        refs[-1][...] = jnp.zeros_like(refs[-1])

    thru = [pltpu.HBM(a.shape, a.dtype) for a in (*srcs, *lands)]
    res = pl.pallas_call(
        body, name=name, in_specs=[IN_HBM] * (2 * n),
        out_specs=(SEM, SEM, *[IN_HBM] * (2 * n), pl.BlockSpec(memory_space=pltpu.VMEM)),
        out_shape=(pltpu.SemaphoreType.DMA((n * n_sem,)), pltpu.SemaphoreType.DMA((n * n_sem,)), *thru,
                   jax.ShapeDtypeStruct((SUBLANES, LANES), F32)),
        input_output_aliases={i: 2 + i for i in range(2 * n)}, compiler_params=SPLIT_COPY,
    )(*[pltpu.with_memory_space_constraint(a, pltpu.HBM) for a in (*srcs, *lands)])
    return (copies, n, res[0], res[1], res[2:2 + 2 * n]), res[-1][0, 0]


def split_copy_wait(name, handle, after):
    copies, n, send_sems, recv_sems, thru = handle

    def body(*refs):
        for cp in copies(refs[:n], refs[n:2 * n], refs[2 * n], refs[2 * n + 1]):
            cp.wait_send()
            cp.wait_recv()

    res = pl.pallas_call(
        body, name=name, in_specs=[IN_HBM] * (2 * n) + [SEM, SEM, pl.BlockSpec(memory_space=pl.ANY)],
        out_specs=[IN_HBM] * (2 * n), out_shape=[pltpu.HBM(a.shape, a.dtype) for a in thru],
        input_output_aliases={i: i for i in range(2 * n)}, compiler_params=SPLIT_COPY,
    )(*thru, send_sems, recv_sems, after)
    return list(res[:n]), list(res[n:])


def all_reduce_small(name, v):
    R, C = v.shape

    def body(v_ref, o_ref, gath, send_sems, recv_sems):
        x, y, c = _place()
        me, sibling = (x, y, c), (x, y, 1 - c)
        chips = _other_chips(x, y)

        def slot(px, py, pc):
            return gath.at[4 * px + 2 * py + pc]

        def copy(k, block, to, src=None):
            return pltpu.make_async_remote_copy(
                src_ref=slot(*block) if src is None else src, dst_ref=slot(*block),
                send_sem=send_sems.at[k], recv_sem=recv_sems.at[k], device_id=to, device_id_type=MESH)

        first = [copy(0, me, sibling, src=v_ref)]
        first += [copy(1 + j, me, (*chip, c), src=v_ref) for j, chip in enumerate(chips)]
        for cp in first:
            cp.start()
        slot(*me)[...] = v_ref[...]
        passed = [copy(4 + j, (*chip, c), sibling) for j, chip in enumerate(chips)]
        for j, chip in enumerate(chips):
            copy(1 + j, (*chip, c), me).wait_recv()
            passed[j].start()
        copy(0, sibling, me).wait_recv()
        for j, chip in enumerate(chips):
            copy(4 + j, (*chip, 1 - c), me).wait_recv()
        for cp in first + passed:
            cp.wait_send()
        acc = gath[0]
        for d in range(1, N_DEV):
            acc = acc + gath[d]
        o_ref[...] = acc

    vm = pl.BlockSpec(memory_space=pltpu.VMEM)
    return pl.pallas_call(
        body, name=name, in_specs=[vm], out_specs=vm, out_shape=jax.ShapeDtypeStruct((R, C), F32),
        scratch_shapes=[pltpu.VMEM((N_DEV, R, C), F32), pltpu.SemaphoreType.DMA((7,)), pltpu.SemaphoreType.DMA((7,))],
        compiler_params=pltpu.CompilerParams(vmem_limit_bytes=VMEM_LIMIT_BYTES),
    )(v)


LANES = 128
PACK_ROW_MULTIPLE = 1024
SMALL_SHARDED = {
    "w_in": ((D_MODEL, 1216), 1), "mla_w_uq": ((MLA_Q_RANK, 768), 1), "mla_w_ukv": ((MLA_KV_RANK, 1024), 1),
    "ssm_w_glu": ((SSM_WIDTH, SSM_WIDTH), 0), "w_o": ((D_MODEL, D_MODEL), 0), "xattn_w_q": ((D_MODEL, 512), 0),
    "xattn_w_kv": ((D_MODEL, 1024), 0), "xattn_w_o": ((512, D_MODEL), 1),
}
FFN_NAMES = ["ffn1_w_gate", "ffn1_w_up", "ffn1_w_down", "ffn2_w_gate", "ffn2_w_up", "ffn2_w_down"]
TRANSPOSED_VIEW = ("ffn1_w_gate", "ffn1_w_up", "ffn2_w_gate", "ffn2_w_up", "w_in", "mla_w_uq")


def _shard_shape(name):
    (r, cdim), ax = SMALL_SHARDED[name]
    return (r // N_CHIPS, cdim) if ax == 0 else (r, cdim // N_CHIPS)


def _pack_shards(shards):
    parts = []
    for name in SMALL_SHARDED:
        a = shards[name]
        lead = a.shape[:-2]
        parts.append(a.reshape(lead + (a.shape[-2] * a.shape[-1] // LANES, LANES)))
    rows = sum(q.shape[-2] for q in parts)
    parts.append(jnp.zeros(lead + (-rows % PACK_ROW_MULTIPLE, LANES), parts[0].dtype))
    return jnp.concatenate(parts, axis=-2)


def _unpack_shards(packed):
    out, r0 = {}, 0
    lead = packed.shape[:-2]
    for name in SMALL_SHARDED:
        r, cdim = _shard_shape(name)
        rows = r * cdim // LANES
        out[name] = packed[..., r0:r0 + rows, :].reshape(lead + (r, cdim))
        r0 += rows
    return out


def _full_from_shards(name, sh):
    (r, cdim), ax = SMALL_SHARDED[name]
    if ax == 0:
        return sh.reshape(r, cdim)
    return jnp.transpose(sh, (1, 0, 2)).reshape(r, cdim)


def _shards_from_full(name, full):
    (r, cdim), ax = SMALL_SHARDED[name]
    if ax == 0:
        return full.reshape(N_CHIPS, r // N_CHIPS, cdim)
    return jnp.transpose(full.reshape(r, N_CHIPS, cdim // N_CHIPS), (1, 0, 2))


SMALL_REPL = {
    "ffn1_norm": (1, 1024), "mix_norm": (1, 1024), "mla_q_norm": (1, 384), "mla_kv_norm": (1, 256),
    "mla_qk_norm_q": (1, 192), "mla_qk_norm_k": (1, 192), "ssm_a_re": (32, 64), "ssm_a_im": (32, 64),
    "ssm_log_dt": (32, 1), "ssm_b_re": (32, 64, 16), "ssm_b_im": (32, 64, 16), "ssm_c_re": (32, 16, 64),
    "ssm_c_im": (32, 16, 64), "ssm_d": (1, 512), "ssm_b_glu": (1, 512), "out_norm_mla": (1, 512),
    "out_norm_ssm": (1, 512), "xattn_norm": (1, 1024), "mem_norm": (1, 1024), "xattn_q_norm": (1, 128),
    "xattn_k_norm": (1, 128), "ffn2_norm": (1, 1024),
}


def _pack_repl(grads, loss):
    flat = jnp.concatenate([grads[n].reshape(-1) for n in SMALL_REPL] + [loss.reshape(1)])
    rows = -(-flat.shape[0] // (LANES * SUBLANES)) * SUBLANES
    return jnp.pad(flat, (0, rows * LANES - flat.shape[0])).reshape(rows, LANES)


def _unpack_repl(packed):
    flat, out, o = packed.reshape(-1), {}, 0
    for n, shp in SMALL_REPL.items():
        size = int(np.prod(shp))
        out[n] = flat[o:o + size].reshape(shp)
        o += size
    out["loss"] = flat[o]
    return out


def _rope_tables(positions):
    half = MLA_ROPE // 2
    inv = ROPE_THETA ** (-jnp.arange(half, dtype=F32) / half)
    ang = positions.astype(F32)[:, None] * inv[None, :]
    cos, sin = jnp.cos(ang), jnp.sin(ang)
    S = positions.shape[0]
    z = lambda w: jnp.zeros((S, w), F32)
    keep = jnp.concatenate([jnp.ones((S, MLA_NOPE), F32), cos, cos, z(HEAD_PAD - MLA_QK)], axis=1)
    from_hi = jnp.concatenate([z(MLA_NOPE), -sin, z(HEAD_PAD - MLA_NOPE - half)], axis=1)
    from_lo = jnp.concatenate([z(MLA_NOPE + half), sin, z(HEAD_PAD - MLA_QK)], axis=1)
    return keep, from_hi, from_lo


def _norm_rope(x, g, keep, from_hi, from_lo):
    y = x * lax.rsqrt(jnp.sum(x * x, axis=-1, keepdims=True) * (1.0 / MLA_QK) + EPS) * g
    half = MLA_ROPE // 2
    return y * keep + _lane_roll(y, HEAD_PAD - half) * from_hi + _lane_roll(y, half) * from_lo


def local_step(x, mem, positions, target, w, wb, small_weights=None, ffn2_weights=None, on_grads=None):
    if small_weights is None:
        small_weights = lambda after: {}
    if ffn2_weights is None:
        ffn2_weights = lambda after: [wb[k] for k in FFN_NAMES[3:]]
    if on_grads is None:
        on_grads = lambda tag, g: 0.0
    S = x.shape[0]
    tm = min(FUSED_TILE, S)
    g1 = (S // tm,)
    tile = lambda arr, **kw: In(arr, (tm, arr.shape[1]), lambda i: (i, 0), rows=True, **kw)
    par = lambda arr, **kw: In(arr, arr.shape, lambda *_: (0, 0), diff=True, acc=(0,), **kw)
    wt = lambda arr: In(arr, weight=True)
    otile = lambda cols, dt: Out((S, cols), dt, (tm, cols), lambda i: (i, 0))
    grads = {}

    x1, h1, g_1, u_1 = ffn_fwd("ffn1_fwd", x, w["ffn1_norm"], wb["ffn1_w_gate"], wb["ffn1_w_up"], wb["ffn1_w_down"])
    wb = {**wb, **small_weights(x1)}

    keep, from_hi, from_lo = _rope_tables(positions)
    scale = MLA_QK ** -0.5

    def f_pre(xv, kp, fh, fl, g_mix, w_q, w_kv, w_kr, w_u, g_q, g_kv, w_uq, w_ukv, gq, gk):
        h2 = _rms(xv, g_mix)
        cq, ckv = _rms(wdot(h2, w_q), g_q), _rms(wdot(h2, w_kv), g_kv)
        kr = wdot(h2, w_kr)
        qs, ks, vs = [], [], []
        for h in range(MLA_HEADS):
            qs.append(_norm_rope(wdot(cq, w_uq, h), gq, kp, fh, fl) * scale)
            kv = wdot(ckv, w_ukv, h)
            ks.append(_norm_rope(jnp.concatenate([kv[:, :MLA_NOPE], kr], axis=-1), gk, kp, fh, fl))
            vs.append(kv[:, MLA_NOPE:])
        return jnp.concatenate(qs, axis=-1), jnp.concatenate(ks, axis=-1), jnp.concatenate(vs, axis=-1), wdot(h2, w_u)

    def pre_ins(gain_mix):
        return [tile(x1, diff=True), tile(keep), tile(from_hi), tile(from_lo), par(gain_mix),
                wt(wb["w_in_q"]), wt(wb["w_in_kv"]), wt(wb["w_in_kr"]), wt(wb["w_in_u"]),
                par(w["mla_q_norm"]), par(w["mla_kv_norm"]), wt(wb["w_uq"]), wt(wb["w_ukv"]),
                par(w["qk_gain_q"]), par(w["qk_gain_k"])]

    pre_outs = [otile(MLA_HEADS * HEAD_PAD, BF16), otile(MLA_HEADS * HEAD_PAD, BF16), otile(MLA_HEADS * MLA_V, BF16),
                otile(SSM_WIDTH, F32)]
    qh, kh, v0, pu = seg_fwd("pre_mixer_fwd", f_pre, g1, pre_ins(w["mix_norm"]), pre_outs)

    o_mla, lse = attn_fwd("mla_attn_fwd", qh, kh, v0, t=min(ATTN_TILE, S))

    prep_grid = (SSM_BLOCKS,)
    prep_ins = ([_whole(w[k], diff=True, acc=(0,)) for k in ("ssm_a_re", "ssm_a_im", "ssm_log_dt")]
                + [In(w[k], (BLOCK_CH, SSM_STATE), lambda i: (i, 0), diff=True)
                   for k in ("ssm_bt_re", "ssm_bt_im", "ssm_c2_re", "ssm_c2_im")])
    blk_out = Out((SSM_BLOCKS, BLOCK_CH, PREP_LANES), BF16, (None, BLOCK_CH, PREP_LANES), lambda i: (i, 0, 0))
    prep_outs = [Out((SUBLANES, SSM_LANES), F32, (SUBLANES, PREP_LANES), lambda i: (0, i))] * 2 + [blk_out] * 4
    a_r8, a_i8, bb_r, bb_i, cb_r, ncb_i = seg_fwd("ssm_prep_fwd", _ssm_prep_f, prep_grid, prep_ins, prep_outs)

    xs_r, xs_i, y_lin = ssm_fwd("ssm_fwd", pu, bb_r, bb_i, cb_r, ncb_i, a_r8, a_i8)

    M = mem.shape[0]
    f_norm = lambda xv, g: (_rms(xv, g),)
    mem_ins = [In(mem, (M, D_MODEL), lambda i: (0, 0)), par(w["mem_norm"])]
    mem_outs = [Out((M, D_MODEL), BF16, (M, D_MODEL), lambda i: (0, 0))]
    (mn,) = seg_fwd("mem_norm_fwd", f_norm, (1,), mem_ins, mem_outs)
    (kvm,) = mm("xattn_kv_fwd", [mn], [[wb["xattn_w_kv"]]], [F32])

    xscale = XHD ** -0.5

    def f_post(o, yl, u, xv, kv, d, w_glu, b, gm, gs, w_o1, w_o2, gx, w_xq, gq, gk, w_xo):
        gl = jax.nn.gelu(yl + d * u)
        so = gl * jax.nn.sigmoid(wdot(gl, w_glu) + b)
        x2 = xv + wdot(_rms(o, gm), w_o1) + wdot(_rms(so, gs), w_o2)
        h3 = _rms(x2, gx)
        heads = []
        for h in range(XH):
            qn = _rms(wdot(h3, w_xq, h), gq)
            kn = _rms(kv[:, h * XHD:(h + 1) * XHD], gk)
            p = jax.nn.softmax(_bdot_nt(qn, kn) * xscale, axis=-1)
            heads.append(_bdot_nn(p, kv[:, (XH + h) * XHD:(XH + h + 1) * XHD]))
        return (x2 + wdot(jnp.concatenate(heads, axis=-1), w_xo),)

    def post_ins(gain_d):
        return [tile(o_mla, diff=True), tile(y_lin, diff=True), tile(pu, diff=True), tile(x1, diff=True),
                In(kvm, kvm.shape, lambda i: (0, 0), diff=True, acc=(0,)), par(gain_d), wt(wb["ssm_w_glu"]),
                par(w["ssm_b_glu"]), par(w["out_norm_mla"]), par(w["out_norm_ssm"]), wt(wb["w_o_mla"]), wt(wb["w_o_ssm"]),
                par(w["xattn_norm"]), wt(wb["xattn_w_q"]), par(w["xattn_q_norm"]), par(w["xattn_k_norm"]),
                wt(wb["xattn_w_o"])]

    post_outs = [otile(D_MODEL, F32)]
    (x3,) = seg_fwd("post_mixer_fwd", f_post, g1, post_ins(w["ssm_d"]), post_outs)

    wg2, wu2, wd2 = ffn2_weights(x3)
    dx4, h4, g_2, u_2, parts = ffn_fwd("ffn2_fwd_loss", x3, w["ffn2_norm"], wg2, wu2, wd2, target=target)
    loss = jnp.sum(parts[::SUBLANES, 0])

    dx3, grads["ffn2_norm"], dg_2, du_2 = ffn_bwd_act("ffn2_bwd_act", dx4, x3, w["ffn2_norm"], g_2, u_2, wg2, wu2, wd2)
    grads["ffn2_w_gate"], grads["ffn2_w_up"], grads["ffn2_w_down"] = ffn_bwd_w("ffn2_bwd_w", h4, dx4, g_2, u_2, dg_2, du_2)
    sent = on_grads("ffn2", grads)

    (do_mla, dy_lin, du_a, dx1_a, dkvm, grads["ssm_d"], grads["ssm_w_glu"], grads["ssm_b_glu"], grads["out_norm_mla"],
     grads["out_norm_ssm"], grads["w_o_mla"], grads["w_o_ssm"], grads["xattn_norm"], grads["xattn_w_q"],
     grads["xattn_q_norm"], grads["xattn_k_norm"], grads["xattn_w_o"]) = seg_bwd(
        "post_mixer_bwd", f_post, g1, post_ins(w["ssm_d"] + sent), post_outs, [dx3])

    (dmn,) = mm("xattn_kv_bwd", [dkvm], [[wb["xattn_w_kv"]]], [F32], trans=True)
    (grads["xattn_w_kv"],) = mm_tn("xattn_kv_bwd_w", [mn], [dkvm], [(0, [0])])
    (grads["mem_norm"],) = seg_bwd("mem_norm_bwd", f_norm, (1,), mem_ins, mem_outs, [dmn])

    du, dbb_r, dbb_i, dcb_r, dncb_i, da_r8, da_i8 = ssm_bwd("ssm_bwd", dy_lin, du_a, pu, xs_r, xs_i, bb_r, bb_i, cb_r, ncb_i,
                                                            a_r8, a_i8)
    prep_g = seg_bwd("ssm_prep_bwd", _ssm_prep_f, prep_grid, prep_ins, prep_outs, [da_r8, da_i8, dbb_r, dbb_i, dcb_r, dncb_i])
    for k, gname in enumerate(("ssm_a_re", "ssm_a_im", "ssm_log_dt", "ssm_bt_re", "ssm_bt_im", "ssm_c2_re", "ssm_c2_im")):
        grads[gname] = prep_g[k]

    dqh, dkh, dv0 = attn_bwd("mla_attn_bwd", qh, kh, v0, do_mla, o_mla, lse, t=min(ATTN_TILE, S))

    (dx1, grads["mix_norm"], grads["w_in_q"], grads["w_in_kv"], grads["w_in_kr"], grads["w_in_u"], grads["mla_q_norm"],
     grads["mla_kv_norm"], grads["w_uq"], grads["w_ukv"], grads["qk_gain_q"], grads["qk_gain_k"]) = seg_bwd(
        "pre_mixer_bwd", f_pre, g1, pre_ins(w["mix_norm"]), pre_outs, [dqh, dkh, dv0, du], adds={0: dx1_a},
        row_block=FUSED_ROW_BLOCK)

    sent = on_grads("small", grads)
    dx, grads["ffn1_norm"], dg_1, du_1 = ffn_bwd_act("ffn1_bwd_act", dx1, x, w["ffn1_norm"] + sent, g_1, u_1,
                                                     wb["ffn1_w_gate"], wb["ffn1_w_up"], wb["ffn1_w_down"])
    grads["ffn1_w_gate"], grads["ffn1_w_up"], grads["ffn1_w_down"] = ffn_bwd_w("ffn1_bwd_w", h1, dx1, g_1, u_1, dg_1, du_1)
    return loss, dx, grads


def _pad_cols(a, n):
    return jnp.pad(a, ((0, 0), (0, n - a.shape[1])))


def _step_weights(shards):
    wb = {}
    w_in = _full_from_shards("w_in", shards["w_in"])
    wb["w_in_q"] = w_in[:, :MLA_Q_RANK]
    wb["w_in_kv"] = w_in[:, MLA_Q_RANK:MLA_Q_RANK + MLA_KV_RANK]
    wb["w_in_kr"] = _pad_cols(w_in[:, MLA_Q_RANK + MLA_KV_RANK:MLA_Q_RANK + MLA_KV_RANK + MLA_ROPE], LANES)
    wb["w_in_u"] = w_in[:, MLA_Q_RANK + MLA_KV_RANK + MLA_ROPE:]
    wb["w_uq"] = jnp.pad(shards["mla_w_uq"], ((0, 0), (0, 0), (0, HEAD_PAD - MLA_QK)))
    wb["w_ukv"] = shards["mla_w_ukv"]
    wb["ssm_w_glu"] = _full_from_shards("ssm_w_glu", shards["ssm_w_glu"])
    w_o = _full_from_shards("w_o", shards["w_o"])
    wb["w_o_mla"], wb["w_o_ssm"] = w_o[:SSM_WIDTH], w_o[SSM_WIDTH:]
    w_xq = _full_from_shards("xattn_w_q", shards["xattn_w_q"])
    wb["xattn_w_q"] = jnp.transpose(w_xq.reshape(D_MODEL, XH, XHD), (1, 0, 2))
    wb["xattn_w_kv"] = _full_from_shards("xattn_w_kv", shards["xattn_w_kv"])
    wb["xattn_w_o"] = _full_from_shards("xattn_w_o", shards["xattn_w_o"])
    return wb


def _sharded_grads(g):
    out = {}
    kr = g["w_in_kr"][:, :MLA_ROPE]
    out["w_in"] = _shards_from_full("w_in", jnp.concatenate([g["w_in_q"], g["w_in_kv"], kr, g["w_in_u"]], axis=1))
    out["mla_w_uq"] = g["w_uq"][:, :, :MLA_QK]
    out["mla_w_ukv"] = g["w_ukv"]
    out["ssm_w_glu"] = _shards_from_full("ssm_w_glu", g["ssm_w_glu"])
    out["w_o"] = _shards_from_full("w_o", jnp.concatenate([g["w_o_mla"], g["w_o_ssm"]], axis=0))
    w_xq = jnp.transpose(g["xattn_w_q"], (1, 0, 2)).reshape(D_MODEL, XH * XHD)
    out["xattn_w_q"] = _shards_from_full("xattn_w_q", w_xq)
    out["xattn_w_kv"] = _shards_from_full("xattn_w_kv", g["xattn_w_kv"])
    out["xattn_w_o"] = _shards_from_full("xattn_w_o", g["xattn_w_o"])
    return out


def _problem_repl_grads(g):
    out = {}
    out["mla_qk_norm_q"] = g["qk_gain_q"][:, :MLA_QK]
    out["mla_qk_norm_k"] = g["qk_gain_k"][:, :MLA_QK]
    out["ssm_b_re"] = jnp.transpose(g["ssm_bt_re"].reshape(SSM_GROUPS, SSM_GROUP, SSM_STATE), (0, 2, 1))
    out["ssm_b_im"] = jnp.transpose(g["ssm_bt_im"].reshape(SSM_GROUPS, SSM_GROUP, SSM_STATE), (0, 2, 1))
    out["ssm_c_re"] = g["ssm_c2_re"].reshape(SSM_GROUPS, SSM_GROUP, SSM_STATE)
    out["ssm_c_im"] = g["ssm_c2_im"].reshape(SSM_GROUPS, SSM_GROUP, SSM_STATE)
    for k in SMALL_REPL:
        if k not in out:
            out[k] = g[k]
    return out


def _problem_grads(g):
    out = {k: _full_from_shards(k, v) for k, v in _sharded_grads(g).items()}
    out.update(_problem_repl_grads(g))
    out.update({k: g[k] for k in FFN_NAMES})
    return out


def _step_params(p):
    row = lambda a: a.reshape(1, -1)
    w = {k: row(p[k]) for k in ("ffn1_norm", "mix_norm", "mla_q_norm", "mla_kv_norm", "ssm_b_glu", "out_norm_mla",
                                "out_norm_ssm", "xattn_norm", "mem_norm", "xattn_q_norm", "xattn_k_norm", "ffn2_norm")}
    w["qk_gain_q"] = _pad_cols(row(p["mla_qk_norm_q"]), HEAD_PAD)
    w["qk_gain_k"] = _pad_cols(row(p["mla_qk_norm_k"]), HEAD_PAD)
    w["ssm_a_re"], w["ssm_a_im"] = p["ssm_a_re"], p["ssm_a_im"]
    w["ssm_log_dt"] = p["ssm_log_dt"].reshape(SSM_GROUPS, 1)
    w["ssm_bt_re"] = jnp.transpose(p["ssm_b_re"], (0, 2, 1)).reshape(SSM_WIDTH, SSM_STATE)
    w["ssm_bt_im"] = jnp.transpose(p["ssm_b_im"], (0, 2, 1)).reshape(SSM_WIDTH, SSM_STATE)
    w["ssm_c2_re"] = p["ssm_c_re"].reshape(SSM_WIDTH, SSM_STATE)
    w["ssm_c2_im"] = p["ssm_c_im"].reshape(SSM_WIDTH, SSM_STATE)
    w["ssm_d"] = p["ssm_d"].reshape(1, SSM_WIDTH)
    return w


ARG_NAMES = ['x', 'mem', 'positions', 'ffn1_norm', 'ffn1_w_gate', 'ffn1_w_up', 'ffn1_w_down', 'mix_norm', 'w_in', 'mla_q_norm', 'mla_w_uq', 'mla_kv_norm', 'mla_w_ukv', 'mla_qk_norm_q', 'mla_qk_norm_k', 'ssm_a_re', 'ssm_a_im', 'ssm_log_dt', 'ssm_b_re', 'ssm_b_im', 'ssm_c_re', 'ssm_c_im', 'ssm_d', 'ssm_w_glu', 'ssm_b_glu', 'out_norm_mla', 'out_norm_ssm', 'w_o', 'xattn_norm', 'mem_norm', 'xattn_w_q', 'xattn_w_kv', 'xattn_q_norm', 'xattn_k_norm', 'xattn_w_o', 'ffn2_norm', 'ffn2_w_gate', 'ffn2_w_up', 'ffn2_w_down']
WEIGHT_NAMES = ARG_NAMES[3:]


def _gather_weights(p, c):
    half = lambda a: lax.dynamic_slice_in_dim(a, c * (a.shape[0] // 2), a.shape[0] // 2, axis=0)
    ffn1 = [half(p[k].astype(BF16)) for k in FFN_NAMES[:3]]
    small = [half(_pack_shards({k: p[k].astype(BF16) for k in SMALL_SHARDED}))]
    ffn2 = [half(p[k].astype(BF16)) for k in FFN_NAMES[3:]]
    me = 4 * lax.axis_index("x") + 2 * lax.axis_index("y") + c
    own = lambda got, blocks: [lax.dynamic_update_index_in_dim(g, b, me, 0) for g, b in zip(got, blocks)]
    as_shards = lambda a: a.reshape(N_CHIPS, 2 * a.shape[1], a.shape[2])
    landing = lambda blocks: [jax.ShapeDtypeStruct((N_DEV,) + b.shape, b.dtype) for b in blocks]
    got1 = own(all_gather_halves("all_gather_weights_a", ffn1), ffn1)
    got1, small = lax.optimization_barrier((got1, small))
    flight_s, sent_s = split_copy_start("gather_small_start", _gather_copies, N_DEV - 1, small, landing(small))
    sent_s, ffn2 = lax.optimization_barrier((sent_s, ffn2))
    flight_2, sent_2 = split_copy_start("gather_ffn2_start", _gather_copies, N_DEV - 1, ffn2, landing(ffn2))
    wb = {k: as_shards(a) for k, a in zip(FFN_NAMES[:3], got1)}

    def small_weights(after):
        mine, got = split_copy_wait("gather_small_wait", flight_s, after)
        return _step_weights(_unpack_shards(own(got, mine)[0].reshape(N_CHIPS, -1, LANES)))

    def ffn2_weights(after):
        mine, got = split_copy_wait("gather_ffn2_wait", flight_2, after)
        return [as_shards(a) for a in own(got, mine)]

    return wb, small_weights, ffn2_weights, sent_s + sent_2


class _GradReduce:
    def __init__(self, c):
        self.c, self.c_arr = c, jnp.reshape(c, (1,)).astype(jnp.int32)
        self.chip = 2 * lax.axis_index("x") + lax.axis_index("y")
        self.flights = []

    def start(self, tag, arrs):
        split = [a.reshape(N_CHIPS, 2, a.shape[1] // 2, a.shape[2]) for a in arrs]
        return self._scatter(tag, split, pair_send_halves(f"grad_pair_send_{tag}", split))

    def send(self, tag, arrs):
        split = [a.reshape(N_CHIPS, 2, a.shape[1] // 2, a.shape[2]) for a in arrs]
        lands = [jax.ShapeDtypeStruct((s.shape[0],) + s.shape[2:], s.dtype) for s in split]
        flight, sent = split_copy_start(f"grad_pair_send_start_{tag}", _pair_copies, 1, split, lands)
        self.sending = (tag, flight)
        return sent

    def scatter(self, tag, after):
        sent_tag, flight = self.sending
        assert sent_tag == tag
        return self._scatter(tag, *split_copy_wait(f"grad_pair_send_wait_{tag}", flight, after))

    def _scatter(self, tag, split, got):
        parts = [pair_add(f"grad_pair_add_{tag}_{k}", s, g, self.c_arr) for k, (s, g) in enumerate(zip(split, got))]
        flight, sent = split_copy_start(f"grad_scatter_start_{tag}", _scatter_copies, N_CHIPS - 1, parts, parts)
        self.flights.append((tag, flight))
        return sent

    def finish(self, after):
        halves = []
        for tag, flight in self.flights:
            parts, landed = split_copy_wait(f"grad_scatter_wait_{tag}", flight, after)
            for k, (q, p) in enumerate(zip(landed, parts)):
                mine = lax.dynamic_index_in_dim(p, self.chip, 0, keepdims=False)
                halves.append(sum_chips(f"grad_sum_{tag}_{k}", lax.dynamic_update_index_in_dim(q, mine, self.chip, 0)))
        tags = "_".join(t for t, _ in self.flights)
        self.flights = []
        theirs = pair_exchange(f"grad_pair_exchange_{tags}", halves)
        return [jnp.where(self.c == 0, jnp.concatenate([h, t], axis=0), jnp.concatenate([t, h], axis=0))
                for h, t in zip(halves, theirs)]


def kernel(x, mem, positions, ffn1_norm, ffn1_w_gate, ffn1_w_up, ffn1_w_down, mix_norm, w_in, mla_q_norm, mla_w_uq, mla_kv_norm, mla_w_ukv, mla_qk_norm_q, mla_qk_norm_k, ssm_a_re, ssm_a_im, ssm_log_dt, ssm_b_re, ssm_b_im, ssm_c_re, ssm_c_im, ssm_d, ssm_w_glu, ssm_b_glu, out_norm_mla, out_norm_ssm, w_o, xattn_norm, mem_norm, xattn_w_q, xattn_w_kv, xattn_q_norm, xattn_k_norm, xattn_w_o, ffn2_norm, ffn2_w_gate, ffn2_w_up, ffn2_w_down, loss_target, m_ffn1_norm, m_ffn1_w_gate, m_ffn1_w_up, m_ffn1_w_down, m_mix_norm, m_w_in, m_mla_q_norm, m_mla_w_uq, m_mla_kv_norm, m_mla_w_ukv, m_mla_qk_norm_q, m_mla_qk_norm_k, m_ssm_a_re, m_ssm_a_im, m_ssm_log_dt, m_ssm_b_re, m_ssm_b_im, m_ssm_c_re, m_ssm_c_im, m_ssm_d, m_ssm_w_glu, m_ssm_b_glu, m_out_norm_mla, m_out_norm_ssm, m_w_o, m_xattn_norm, m_mem_norm, m_xattn_w_q, m_xattn_w_kv, m_xattn_q_norm, m_xattn_k_norm, m_xattn_w_o, m_ffn2_norm, m_ffn2_w_gate, m_ffn2_w_up, m_ffn2_w_down, v_ffn1_norm, v_ffn1_w_gate, v_ffn1_w_up, v_ffn1_w_down, v_mix_norm, v_w_in, v_mla_q_norm, v_mla_w_uq, v_mla_kv_norm, v_mla_w_ukv, v_mla_qk_norm_q, v_mla_qk_norm_k, v_ssm_a_re, v_ssm_a_im, v_ssm_log_dt, v_ssm_b_re, v_ssm_b_im, v_ssm_c_re, v_ssm_c_im, v_ssm_d, v_ssm_w_glu, v_ssm_b_glu, v_out_norm_mla, v_out_norm_ssm, v_w_o, v_xattn_norm, v_mem_norm, v_xattn_w_q, v_xattn_w_kv, v_xattn_q_norm, v_xattn_k_norm, v_xattn_w_o, v_ffn2_norm, v_ffn2_w_gate, v_ffn2_w_up, v_ffn2_w_down):
    args = dict(locals())
    c = lax.axis_index("c")
    view = lambda k, a: jnp.swapaxes(a, 0, 1) if k in TRANSPOSED_VIEW else a
    p = {k: view(k, args[k][0]) for k in WEIGHT_NAMES}
    mom = {k: view(k, args["m_" + k][0]) for k in WEIGHT_NAMES}
    var = {k: view(k, args["v_" + k][0]) for k in WEIGHT_NAMES}
    natural = {k: view(k, p[k]) for k in WEIGHT_NAMES}

    wb, small_weights, ffn2_weights, sent = _gather_weights(
        {k: (p[k] if k in FFN_NAMES else natural[k]) for k in WEIGHT_NAMES}, c)
    w = _step_params(natural)
    w["ffn1_norm"] = w["ffn1_norm"] + sent
    early, late = _GradReduce(c), _GradReduce(c)

    def on_grads(tag, g):
        if tag == "ffn2":
            return early.send(tag, [g[k] for k in FFN_NAMES[3:]])
        packed = _pack_shards(_sharded_grads(g))
        return early.scatter("ffn2", packed[0, :SUBLANES]) + early.start(tag, [packed])

    loss, dx, g = local_step(x[0], mem[0], positions[0], loss_target[0], w, wb, small_weights, ffn2_weights, on_grads)

    sent = late.start("ffn1", [g[k] for k in FFN_NAMES[:3]])
    shards = early.finish(dx[:SUBLANES, :LANES] + sent)
    grad = dict(zip(FFN_NAMES[3:], shards[:3]))
    small_sharded = _unpack_shards(shards[3])
    grad.update({k: view(k, small_sharded[k]) for k in SMALL_SHARDED})
    repl = _problem_repl_grads(g)
    grad.update(_unpack_repl(all_reduce_small("grad_all_reduce_small", _pack_repl(repl, loss))))
    loss = grad.pop("loss")

    delta, new_m, new_v = {}, {}, {}
    small = [k for k in WEIGHT_NAMES if k not in FFN_NAMES and k not in SMALL_SHARDED]
    as2d = lambda a: a.reshape(-1, a.shape[-1])

    def update(k):
        delta[k], new_m[k], new_v[k] = adamw_big("adamw_" + k, as2d(p[k]), as2d(grad[k]), as2d(mom[k]), as2d(var[k]))

    for k in WEIGHT_NAMES:
        if k not in small and k not in FFN_NAMES[:3]:
            update(k)
    ds, nms, nvs = adamw_small("adamw_small", [as2d(p[k]) for k in small], [as2d(grad[k].reshape(p[k].shape)) for k in small],
                               [as2d(mom[k]) for k in small], [as2d(var[k]) for k in small])
    for k, d, nm, nv in zip(small, ds, nms, nvs):
        delta[k], new_m[k], new_v[k] = d, nm, nv

    grad.update(zip(FFN_NAMES[:3], late.finish(delta[FFN_NAMES[-1]])))
    for k in FFN_NAMES[:3]:
        update(k)

    shaped = lambda d, k: view(k, d.reshape(p[k].shape)).reshape(args[k].shape)
    return (loss, dx[None], *[shaped(grad[k], k) for k in WEIGHT_NAMES], *[shaped(delta[k], k) for k in WEIGHT_NAMES],
            *[shaped(new_m[k], k) for k in WEIGHT_NAMES], *[shaped(new_v[k], k) for k in WEIGHT_NAMES])
```

```python
import functools
import math

import jax
import jax.numpy as jnp
import numpy as np
from jax import lax
from jax.experimental import pallas as pl
from jax.experimental.pallas import tpu as pltpu

F32, BF16 = jnp.float32, jnp.bfloat16
EPS = 1e-6
MESH = pl.DeviceIdType.MESH

D_MODEL, D_FF = 1024, 2752
MLA_HEADS, MLA_Q_RANK, MLA_KV_RANK, MLA_NOPE, MLA_ROPE, MLA_V = 4, 384, 256, 128, 64, 128
MLA_QK = MLA_NOPE + MLA_ROPE
HEAD_PAD = 256
SSM_WIDTH, SSM_GROUP, SSM_GROUPS, SSM_STATE = 512, 16, 32, 64
SSM_LANES = SSM_GROUPS * SSM_STATE
XH, XHD = 4, 128
ROPE_THETA = 10000.0
ADAM_LR, ADAM_B1, ADAM_B2, ADAM_EPS, ADAM_WD, ADAM_STEP = 0.001, 0.9, 0.999, 1e-08, 0.01, 10
N_CHIPS, N_CORES, N_DEV = 4, 2, 8

VMEM_LIMIT_BYTES = 56 * 2**20
TOKEN_TILE = 512
FUSED_TILE = 256
FUSED_ROW_BLOCK = 128
FFN_ROW_BLOCK = 256
FFN_WIDE_TILE = 1024
ATTN_TILE = 512
STAT_LANES = 128
SCAN_TIME_TILE = 1024
SCAN_LANE_TILE = 512
SUBLANES = 8


def _params(n_axes):
    return pltpu.CompilerParams(dimension_semantics=("arbitrary",) * n_axes, vmem_limit_bytes=VMEM_LIMIT_BYTES)


def _first(axes):
    cond = None
    for a in axes:
        c = pl.program_id(a) == 0
        cond = c if cond is None else jnp.logical_and(cond, c)
    return cond


def mm(name, xs, ws, out_dtypes, *, trans=False, adds=None, tm=TOKEN_TILE):
    rows = xs[0].shape[0]
    tm = min(tm, rows)
    n_in, n_out = len(xs), len(out_dtypes)
    pairs = [(i, j) for i in range(n_in) for j in range(n_out) if ws[i][j] is not None]
    w_list = [ws[i][j] for (i, j) in pairs]
    adds = list(adds) if adds is not None else [None] * n_out
    add_list = [a for a in adds if a is not None]
    out_cols = [None] * n_out
    for (i, j), w in zip(pairs, w_list):
        out_cols[j] = w.shape[0] if trans else w.shape[1]
    contract = (((1,), (1 if trans else 0,)), ((), ()))

    def body(*refs):
        x_refs = refs[:n_in]
        w_refs = refs[n_in:n_in + len(pairs)]
        a_refs = list(refs[n_in + len(pairs):n_in + len(pairs) + len(add_list)])
        o_refs = refs[n_in + len(pairs) + len(add_list):]
        xb = [None] * n_in
        for j in range(n_out):
            acc = None
            for p, (i, jj) in enumerate(pairs):
                if jj != j:
                    continue
                if xb[i] is None:
                    xb[i] = x_refs[i][...].astype(BF16)
                d = lax.dot_general(xb[i], w_refs[p][...].astype(BF16), contract, preferred_element_type=F32)
                acc = d if acc is None else acc + d
            if adds[j] is not None:
                acc = acc + a_refs.pop(0)[...].astype(F32)
            o_refs[j][...] = acc.astype(o_refs[j].dtype)

    in_specs = ([pl.BlockSpec((tm, x.shape[1]), lambda i: (i, 0)) for x in xs]
                + [pl.BlockSpec(w.shape, lambda i: (0, 0)) for w in w_list]
                + [pl.BlockSpec((tm, a.shape[1]), lambda i: (i, 0)) for a in add_list])
    outs = pl.pallas_call(
        body, name=name, grid=(rows // tm,), in_specs=in_specs,
        out_specs=[pl.BlockSpec((tm, n), lambda i: (i, 0)) for n in out_cols],
        out_shape=[jax.ShapeDtypeStruct((rows, n), dt) for n, dt in zip(out_cols, out_dtypes)],
        compiler_params=_params(1),
    )(*xs, *w_list, *add_list)
    return list(outs)


def mm_tn(name, xs, dys, pairs, *, tm=TOKEN_TILE):
    rows = xs[0].shape[0]
    tm = min(tm, rows)
    n_x, n_dy = len(xs), len(dys)
    contract = (((0,), (0,)), ((), ()))

    def body(*refs):
        x_refs, dy_refs, o_refs = refs[:n_x], refs[n_x:n_x + n_dy], refs[n_x + n_dy:]
        @pl.when(pl.program_id(0) == 0)
        def _():
            for o in o_refs:
                o[...] = jnp.zeros_like(o)

        for k, (i, js) in enumerate(pairs):
            dy = None
            for j in js:
                t = dy_refs[j][...].astype(F32)
                dy = t if dy is None else dy + t
            o_refs[k][...] += lax.dot_general(x_refs[i][...].astype(BF16), dy.astype(BF16), contract,
                                              preferred_element_type=F32)

    shapes = [(xs[i].shape[1], dys[js[0]].shape[1]) for (i, js) in pairs]
    outs = pl.pallas_call(
        body, name=name, grid=(rows // tm,),
        in_specs=[pl.BlockSpec((tm, a.shape[1]), lambda i: (i, 0)) for a in (*xs, *dys)],
        out_specs=[pl.BlockSpec(s, lambda i: (0, 0)) for s in shapes],
        out_shape=[jax.ShapeDtypeStruct(s, F32) for s in shapes],
        compiler_params=_params(1),
    )(*xs, *dys)
    return list(outs)


class In:
    def __init__(self, arr, block=None, imap=None, *, diff=False, acc=None, grad=None, weight=False, rows=False):
        self.arr, self.block, self.imap, self.diff, self.acc, self.grad = arr, block, imap, diff, acc, grad
        self.weight, self.rows = weight, rows

    def spec(self):
        return pl.BlockSpec(memory_space=pltpu.VMEM) if self.weight else pl.BlockSpec(self.block, self.imap)


class Out:
    def __init__(self, shape, dtype, block, imap):
        self.shape, self.dtype, self.block, self.imap = shape, dtype, block, imap

    def spec(self):
        return pl.BlockSpec(self.block, self.imap)


class Wt:
    def __init__(self, ref, zeros=None):
        self.ref, self.zeros = ref, zeros


@jax.custom_vjp
def _wdot(a, w, z):
    return jnp.dot(a.astype(BF16), w, preferred_element_type=F32)


def _wdot_fwd(a, w, z):
    return _wdot(a, w, z), (a, w)


def _wdot_bwd(res, g):
    a, w = res
    gb = g.astype(BF16)
    da = lax.dot_general(gb, w, (((1,), (1,)), ((), ())), preferred_element_type=F32)
    dz = lax.dot_general(a.astype(BF16), gb, (((0,), (0,)), ((), ())), preferred_element_type=F32)
    return da, None, dz


_wdot.defvjp(_wdot_fwd, _wdot_bwd)


def wdot(a, wt, head=None):
    w = wt.ref[...] if head is None else wt.ref[head]
    if wt.zeros is None:
        return jnp.dot(a.astype(BF16), w, preferred_element_type=F32)
    return _wdot(a, w, wt.zeros[0 if head is None else head])


def seg_fwd(name, f, grid, ins, outs):
    n_in = len(ins)

    def body(*refs):
        res = f(*[Wt(r) if i.weight else r[...] for i, r in zip(ins, refs[:n_in])])
        for o_ref, r in zip(refs[n_in:], res):
            o_ref[...] = r.astype(o_ref.dtype)

    res = pl.pallas_call(
        body, name=name, grid=grid, in_specs=[i.spec() for i in ins], out_specs=[o.spec() for o in outs],
        out_shape=[jax.ShapeDtypeStruct(o.shape, o.dtype) for o in outs], compiler_params=_params(len(grid)),
    )(*[i.arr for i in ins])
    return list(res)


def seg_bwd(name, f, grid, ins, outs, cts, adds=None, row_block=None):
    n_in, n_ct = len(ins), len(cts)
    grad_idx = [k for k, i in enumerate(ins) if i.diff or i.weight]
    adds = adds or {}
    add_keys = sorted(adds)
    add_list = [adds[k] for k in add_keys]
    heads = {k: (ins[k].arr.shape[0] if ins[k].arr.ndim == 3 else 1) for k in grad_idx if ins[k].weight}
    tile_rows = outs[0].block[0]
    blocks = [None] if row_block is None else [pl.ds(r0, row_block) for r0 in range(0, tile_rows, row_block)]

    def body(*refs):
        in_refs, ct_refs = refs[:n_in], refs[n_in:n_in + n_ct]
        add_refs = dict(zip(add_keys, refs[n_in + n_ct:n_in + n_ct + len(add_list)]))
        g_refs = dict(zip(grad_idx, refs[n_in + n_ct + len(add_list):]))
        for k in grad_idx:
            if ins[k].weight or ins[k].acc is not None:
                @pl.when(_first(range(len(grid)) if ins[k].weight else ins[k].acc))
                def _(k=k):
                    g_refs[k][...] = jnp.zeros_like(g_refs[k])

        for rows in blocks:
            at = lambda ref, sliced: ref[...] if rows is None or not sliced else ref[rows, :]
            vals = [None if i.weight else at(r, i.rows) for i, r in zip(ins, in_refs)]
            primals, owner = [], []
            for k in grad_idx:
                if ins[k].weight:
                    for h in range(heads[k]):
                        primals.append(jnp.zeros(ins[k].arr.shape[-2:], F32))
                        owner.append((k, h))
                else:
                    primals.append(vals[k].astype(F32))
                    owner.append((k, None))

            def g(*dv, vals=vals, owner=owner):
                full = list(vals)
                zeros = {}
                for (k, h), v in zip(owner, dv):
                    if h is None:
                        full[k] = v
                    else:
                        zeros.setdefault(k, []).append(v)
                for k, z in zeros.items():
                    full[k] = Wt(in_refs[k], z)
                return tuple(f(*full))

            _, pull = jax.vjp(g, *primals)
            grads = pull(tuple(at(c, True).astype(F32) for c in ct_refs))
            for (k, h), gr in zip(owner, grads):
                if ins[k].weight:
                    if ins[k].arr.ndim == 3:
                        g_refs[k][h] += gr
                    else:
                        g_refs[k][...] += gr
                    continue
                if k in add_refs:
                    gr = gr + at(add_refs[k], True).astype(F32)
                if ins[k].acc is not None:
                    g_refs[k][...] += gr
                elif rows is None or not ins[k].rows:
                    g_refs[k][...] = gr.astype(g_refs[k].dtype)
                else:
                    g_refs[k][rows, :] = gr.astype(g_refs[k].dtype)

    g_specs, g_shapes = [], []
    for k in grad_idx:
        i = ins[k]
        if i.weight:
            g_specs.append(pl.BlockSpec(memory_space=pltpu.VMEM))
            g_shapes.append(jax.ShapeDtypeStruct(i.arr.shape, F32))
            continue
        shape, block, imap = i.grad if i.grad is not None else (i.arr.shape, i.block, i.imap)
        g_specs.append(pl.BlockSpec(block, imap))
        g_shapes.append(jax.ShapeDtypeStruct(shape, F32))
    in_specs = ([i.spec() for i in ins] + [o.spec() for o in outs]
                + [pl.BlockSpec(ins[k].block, ins[k].imap) for k in add_keys])
    res = pl.pallas_call(
        body, name=name, grid=grid, in_specs=in_specs, out_specs=g_specs, out_shape=g_shapes,
        compiler_params=_params(len(grid)),
    )(*[i.arr for i in ins], *cts, *add_list)
    return list(res)


def _rms(x, g):
    return x * lax.rsqrt(jnp.mean(x * x, axis=-1, keepdims=True) + EPS) * g


@jax.custom_vjp
def _bdot_nt(a, b):
    return lax.dot_general(a.astype(BF16), b.astype(BF16), (((1,), (1,)), ((), ())), preferred_element_type=F32)


def _bdot_nt_fwd(a, b):
    return _bdot_nt(a, b), (a, b)


def _bdot_nt_bwd(res, g):
    a, b = res
    gb = g.astype(BF16)
    da = lax.dot_general(gb, b.astype(BF16), (((1,), (0,)), ((), ())), preferred_element_type=F32)
    db = lax.dot_general(gb, a.astype(BF16), (((0,), (0,)), ((), ())), preferred_element_type=F32)
    return da, db


_bdot_nt.defvjp(_bdot_nt_fwd, _bdot_nt_bwd)


@jax.custom_vjp
def _bdot_nn(a, b):
    return lax.dot_general(a.astype(BF16), b.astype(BF16), (((1,), (0,)), ((), ())), preferred_element_type=F32)


def _bdot_nn_fwd(a, b):
    return _bdot_nn(a, b), (a, b)


def _bdot_nn_bwd(res, g):
    a, b = res
    gb = g.astype(BF16)
    da = lax.dot_general(gb, b.astype(BF16), (((1,), (1,)), ((), ())), preferred_element_type=F32)
    db = lax.dot_general(a.astype(BF16), gb, (((0,), (0,)), ((), ())), preferred_element_type=F32)
    return da, db


_bdot_nn.defvjp(_bdot_nn_fwd, _bdot_nn_bwd)


@functools.partial(jax.custom_vjp, nondiff_argnums=(1,))
def _lane_roll(x, shift):
    return pltpu.roll(x, shift, 1)


def _lane_roll_fwd(x, shift):
    return pltpu.roll(x, shift, 1), None


def _lane_roll_bwd(shift, _, g):
    return (pltpu.roll(g, (g.shape[1] - shift) % g.shape[1], 1),)


_lane_roll.defvjp(_lane_roll_fwd, _lane_roll_bwd)


def _hp_dot(a, b):
    return jnp.dot(a, b, precision=lax.Precision.HIGHEST, preferred_element_type=F32)


NT_DIMS = (((1,), (1,)), ((), ()))
TN_DIMS = (((0,), (0,)), ((), ()))


def ffn_fwd(name, x, gain, wg, wu, wd, target=None, tm=FFN_WIDE_TILE):
    S, D = x.shape
    tm = min(tm, S)
    nsh, Fs, _ = wg.shape
    with_loss = target is not None

    def body(*refs):
        if with_loss:
            x_ref, gain_ref, wg_ref, wu_ref, wd_ref, t_ref, xo_ref, h_ref, g_ref, u_ref, part_ref, acc = refs
        else:
            x_ref, gain_ref, wg_ref, wu_ref, wd_ref, xo_ref, h_ref, g_ref, u_ref, acc = refs
        j = pl.program_id(1)

        @pl.when(j == 0)
        def _():
            h_ref[...] = _rms(x_ref[...], gain_ref[...]).astype(BF16)
            acc[...] = jnp.zeros_like(acc)

        h = h_ref[...]
        g = lax.dot_general(h, wg_ref[...], NT_DIMS, preferred_element_type=F32)
        u = lax.dot_general(h, wu_ref[...], NT_DIMS, preferred_element_type=F32)
        g_ref[...] = g.astype(BF16)
        u_ref[...] = u.astype(BF16)
        a = g * jax.nn.sigmoid(g) * u
        acc[...] += jnp.dot(a.astype(BF16), wd_ref[...], preferred_element_type=F32)

        @pl.when(j == nsh - 1)
        def _():
            y = x_ref[...] + 0.5 * acc[...]
            if with_loss:
                err = y - t_ref[...]
                xo_ref[...] = err * (1.0 / D)
                part_ref[...] = jnp.full(part_ref.shape, 0.5 * jnp.sum(jnp.mean(err * err, axis=-1)), F32)
            else:
                xo_ref[...] = y

    rows = pl.BlockSpec((tm, D), lambda i, j: (i, 0))
    wspec = pl.BlockSpec((None, Fs, D), lambda i, j: (j, 0, 0))
    act = pl.BlockSpec((None, tm, Fs), lambda i, j: (j, i, 0))
    in_specs, args = [rows, pl.BlockSpec((1, D), lambda i, j: (0, 0)), wspec, wspec, wspec], [x, gain, wg, wu, wd]
    out_specs = [rows, rows, act, act]
    out_shape = [jax.ShapeDtypeStruct((S, D), F32), jax.ShapeDtypeStruct((S, D), BF16),
                 jax.ShapeDtypeStruct((nsh, S, Fs), BF16), jax.ShapeDtypeStruct((nsh, S, Fs), BF16)]
    if with_loss:
        in_specs.append(rows)
        args.append(target)
        out_specs.append(pl.BlockSpec((SUBLANES, 128), lambda i, j: (i, 0)))
        out_shape.append(jax.ShapeDtypeStruct((S // tm * SUBLANES, 128), F32))
    return pl.pallas_call(
        body, name=name, grid=(S // tm, nsh), in_specs=in_specs, out_specs=out_specs, out_shape=out_shape,
        scratch_shapes=[pltpu.VMEM((tm, D), F32)], compiler_params=_params(2),
    )(*args)


def ffn_bwd_act(name, dxo, x, gain, g, u, wg, wu, wd, tm=TOKEN_TILE):
    S, D = x.shape
    tm = min(tm, S)
    nsh, Fs, _ = wg.shape

    def body(dxo_ref, x_ref, gain_ref, g_ref, u_ref, wg_ref, wu_ref, wd_ref, dx_ref, dgain_ref, dg_ref, du_ref, dd, dh):
        i, j = pl.program_id(0), pl.program_id(1)

        @pl.when(j == 0)
        def _():
            dd[...] = (0.5 * dxo_ref[...]).astype(BF16)
            dh[...] = jnp.zeros_like(dh)

        for r0 in range(0, tm, FFN_ROW_BLOCK):
            rows = pl.ds(r0, FFN_ROW_BLOCK)
            da = lax.dot_general(dd[rows, :], wd_ref[...], NT_DIMS, preferred_element_type=F32)
            gf, uf = g_ref[rows, :].astype(F32), u_ref[rows, :].astype(F32)
            sig = jax.nn.sigmoid(gf)
            dgv = (da * uf * (sig * (1.0 + gf * (1.0 - sig)))).astype(BF16)
            duv = (da * (gf * sig)).astype(BF16)
            dg_ref[rows, :] = dgv
            du_ref[rows, :] = duv
            dh[rows, :] += (jnp.dot(dgv, wg_ref[...], preferred_element_type=F32)
                            + jnp.dot(duv, wu_ref[...], preferred_element_type=F32))

        @pl.when(j == nsh - 1)
        def _():
            xv = x_ref[...]
            r = lax.rsqrt(jnp.mean(xv * xv, axis=-1, keepdims=True) + EPS)
            xhat = xv * r
            dhv = dh[...]
            dxn = dhv * gain_ref[...]
            dx_ref[...] = dxo_ref[...] + r * (dxn - xhat * jnp.mean(dxn * xhat, axis=-1, keepdims=True))
            part = jnp.sum(dhv * xhat, axis=0, keepdims=True)

            @pl.when(i == 0)
            def _():
                dgain_ref[...] = part

            @pl.when(i != 0)
            def _():
                dgain_ref[...] += part

    return pl.pallas_call(
        body, name=name, grid=(S // tm, nsh),
        in_specs=[pl.BlockSpec((tm, D), lambda i, j: (i, 0)), pl.BlockSpec((tm, D), lambda i, j: (i, 0)),
                  pl.BlockSpec((1, D), lambda i, j: (0, 0)),
                  pl.BlockSpec((None, tm, Fs), lambda i, j: (j, i, 0)), pl.BlockSpec((None, tm, Fs), lambda i, j: (j, i, 0)),
                  pl.BlockSpec((None, Fs, D), lambda i, j: (j, 0, 0)), pl.BlockSpec((None, Fs, D), lambda i, j: (j, 0, 0)),
                  pl.BlockSpec((None, Fs, D), lambda i, j: (j, 0, 0))],
        out_specs=[pl.BlockSpec((tm, D), lambda i, j: (i, 0)), pl.BlockSpec((1, D), lambda i, j: (0, 0)),
                   pl.BlockSpec((None, tm, Fs), lambda i, j: (j, i, 0)), pl.BlockSpec((None, tm, Fs), lambda i, j: (j, i, 0))],
        out_shape=[jax.ShapeDtypeStruct((S, D), F32), jax.ShapeDtypeStruct((1, D), F32),
                   jax.ShapeDtypeStruct((nsh, S, Fs), BF16), jax.ShapeDtypeStruct((nsh, S, Fs), BF16)],
        scratch_shapes=[pltpu.VMEM((tm, D), BF16), pltpu.VMEM((tm, D), F32)], compiler_params=_params(2),
    )(dxo, x, gain, g, u, wg, wu, wd)


def ffn_bwd_w(name, h, dxo, g, u, dg, du, tm=FFN_WIDE_TILE):
    S, D = h.shape
    tm = min(tm, S)
    nsh, _, Fs = g.shape

    def body(h_ref, dxo_ref, g_ref, u_ref, dg_ref, du_ref, dwg_ref, dwu_ref, dwd_ref):
        i = pl.program_id(1)
        gf, uf = g_ref[...].astype(F32), u_ref[...].astype(F32)
        a = (gf * jax.nn.sigmoid(gf) * uf).astype(BF16)
        dd = (0.5 * dxo_ref[...]).astype(BF16)
        hv = h_ref[...]

        @pl.when(i == 0)
        def _():
            dwg_ref[...] = jnp.zeros_like(dwg_ref)
            dwu_ref[...] = jnp.zeros_like(dwu_ref)
            dwd_ref[...] = jnp.zeros_like(dwd_ref)

        dwg_ref[...] += lax.dot_general(dg_ref[...], hv, TN_DIMS, preferred_element_type=F32)
        dwu_ref[...] += lax.dot_general(du_ref[...], hv, TN_DIMS, preferred_element_type=F32)
        dwd_ref[...] += lax.dot_general(a, dd, TN_DIMS, preferred_element_type=F32)

    act = pl.BlockSpec((None, tm, Fs), lambda j, i: (j, i, 0))
    wspec = pl.BlockSpec((None, Fs, D), lambda j, i: (j, 0, 0))
    return pl.pallas_call(
        body, name=name, grid=(nsh, S // tm),
        in_specs=[pl.BlockSpec((tm, D), lambda j, i: (i, 0)), pl.BlockSpec((tm, D), lambda j, i: (i, 0)), act, act, act, act],
        out_specs=[wspec, wspec, wspec], out_shape=[jax.ShapeDtypeStruct((nsh, Fs, D), F32)] * 3,
        compiler_params=_params(2),
    )(h, dxo, g, u, dg, du)


NEG_BIG = -1e30


def _causal_pairs(n, by_key):
    pairs = [(qi, ki) for qi in range(n) for ki in range(qi + 1)]
    if by_key:
        pairs.sort(key=lambda p: (p[1], p[0]))
    return jnp.asarray([p[0] for p in pairs], jnp.int32), jnp.asarray([p[1] for p in pairs], jnp.int32)


def _scores(q, k, masked):
    s = lax.dot_general(q, k, (((1,), (1,)), ((), ())), preferred_element_type=F32)
    if masked:
        row = lax.broadcasted_iota(jnp.int32, s.shape, 0)
        col = lax.broadcasted_iota(jnp.int32, s.shape, 1)
        s = jnp.where(row >= col, s, NEG_BIG)
    return s


def attn_fwd(name, q, k, v, t=ATTN_TILE):
    S, Dk = q.shape[0], HEAD_PAD
    H = q.shape[1] // Dk
    Dv = v.shape[1] // H
    qt, kt = _causal_pairs(S // t, by_key=False)

    def body(qt_ref, kt_ref, q_ref, k_ref, v_ref, o_ref, lse_ref, m_sc, l_sc, acc):
        qi, ki = qt_ref[pl.program_id(1)], kt_ref[pl.program_id(1)]

        @pl.when(ki == 0)
        def _():
            m_sc[...] = jnp.full_like(m_sc, NEG_BIG)
            l_sc[...] = jnp.zeros_like(l_sc)
            acc[...] = jnp.zeros_like(acc)

        def step(masked):
            s = _scores(q_ref[...], k_ref[...], masked)
            m_prev = m_sc[...]
            m_next = jnp.maximum(m_prev, jnp.max(s, axis=-1, keepdims=True))
            alpha = jnp.exp(m_prev - m_next)
            p = jnp.exp(s - jnp.tile(m_next, (1, t // STAT_LANES)))
            l_sc[...] = alpha * l_sc[...] + jnp.sum(p, axis=-1, keepdims=True)
            acc[...] = alpha * acc[...] + jnp.dot(p.astype(BF16), v_ref[...].astype(BF16), preferred_element_type=F32)
            m_sc[...] = m_next

        @pl.when(ki < qi)
        def _():
            step(False)

        @pl.when(ki == qi)
        def _():
            step(True)
            o_ref[...] = acc[...] / l_sc[...]
            lse_ref[...] = m_sc[...] + jnp.log(l_sc[...])

    stat = pltpu.VMEM((t, STAT_LANES), F32)
    return pl.pallas_call(
        body, name=name,
        grid_spec=pltpu.PrefetchScalarGridSpec(
            num_scalar_prefetch=2, grid=(H, qt.shape[0]),
            in_specs=[pl.BlockSpec((t, Dk), lambda h, s, qt, kt: (qt[s], h)),
                      pl.BlockSpec((t, Dk), lambda h, s, qt, kt: (kt[s], h)),
                      pl.BlockSpec((t, Dv), lambda h, s, qt, kt: (kt[s], h))],
            out_specs=[pl.BlockSpec((t, Dv), lambda h, s, qt, kt: (qt[s], h)),
                       pl.BlockSpec((None, t, STAT_LANES), lambda h, s, qt, kt: (h, qt[s], 0))],
            scratch_shapes=[stat, stat, pltpu.VMEM((t, Dv), F32)]),
        out_shape=[jax.ShapeDtypeStruct((S, H * Dv), F32), jax.ShapeDtypeStruct((H, S, STAT_LANES), F32)],
        compiler_params=_params(2),
    )(qt, kt, q, k, v)


def attn_bwd(name, q, k, v, do, o, lse, t=ATTN_TILE):
    S, Dk = q.shape[0], HEAD_PAD
    H = q.shape[1] // Dk
    Dv = v.shape[1] // H
    qt, kt = _causal_pairs(S // t, by_key=True)
    tn_dims = (((0,), (0,)), ((), ()))

    def body(qt_ref, kt_ref, q_ref, k_ref, v_ref, do_ref, o_ref, lse_ref, dq_ref, dk_ref, dv_ref):
        step_id = pl.program_id(1)
        qi, ki = qt_ref[step_id], kt_ref[step_id]

        @pl.when(step_id == 0)
        def _():
            dq_ref[...] = jnp.zeros_like(dq_ref)

        def step(masked):
            s = _scores(q_ref[...], k_ref[...], masked)
            reps = (1, t // STAT_LANES)
            p = jnp.exp(s - jnp.tile(lse_ref[...], reps))
            dov = do_ref[...]
            delta = jnp.broadcast_to(jnp.sum(dov * o_ref[...], axis=-1, keepdims=True), (t, STAT_LANES))
            dob = dov.astype(BF16)
            dp = lax.dot_general(dob, v_ref[...].astype(BF16), (((1,), (1,)), ((), ())), preferred_element_type=F32)
            ds = (p * (dp - jnp.tile(delta, reps))).astype(BF16)
            pdv = lax.dot_general(p.astype(BF16), dob, tn_dims, preferred_element_type=F32)
            pdk = lax.dot_general(ds, q_ref[...], tn_dims, preferred_element_type=F32)
            rows = pl.ds(pl.multiple_of(qi * t, t), t)
            dq_ref[rows, :] += jnp.dot(ds, k_ref[...], preferred_element_type=F32)
            return pdk, pdv

        @pl.when(ki == qi)
        def _():
            dk_ref[...] = jnp.zeros_like(dk_ref)
            dv_ref[...] = jnp.zeros_like(dv_ref)

        def accumulate(masked):
            pdk, pdv = step(masked)
            dk_ref[...] += pdk
            dv_ref[...] += pdv

        @pl.when(ki == qi)
        def _():
            accumulate(True)

        @pl.when(ki < qi)
        def _():
            accumulate(False)

    qrow = lambda h, s, qt, kt: (qt[s], h)
    krow = lambda h, s, qt, kt: (kt[s], h)
    return pl.pallas_call(
        body, name=name,
        grid_spec=pltpu.PrefetchScalarGridSpec(
            num_scalar_prefetch=2, grid=(H, qt.shape[0]),
            in_specs=[pl.BlockSpec((t, Dk), qrow), pl.BlockSpec((t, Dk), krow), pl.BlockSpec((t, Dv), krow),
                      pl.BlockSpec((t, Dv), qrow), pl.BlockSpec((t, Dv), qrow),
                      pl.BlockSpec((None, t, STAT_LANES), lambda h, s, qt, kt: (h, qt[s], 0))],
            out_specs=[pl.BlockSpec((S, Dk), lambda h, s, qt, kt: (0, h)), pl.BlockSpec((t, Dk), krow),
                       pl.BlockSpec((t, Dv), krow)]),
        out_shape=[jax.ShapeDtypeStruct((S, H * Dk), F32), jax.ShapeDtypeStruct((S, H * Dk), F32),
                   jax.ShapeDtypeStruct((S, H * Dv), F32)],
        compiler_params=_params(2),
    )(qt, kt, q, k, v, do, o, lse)


def _cmul(ar, ai, br, bi):
    return ar * br - ai * bi, ar * bi + ai * br


def _scan_tile(x_r, x_i, ar_ref, ai_ref, cr_sc, ci_sc, *, reverse, first, states=None):
    tc, lanes = x_r.shape
    nblk, lb = tc // SUBLANES, SCAN_LANE_TILE
    with_da = states is not None
    if with_da:
        xr_all, xi_all, pr_all, pi_all, dar_all, dai_all, chunk = states

    @pl.when(first)
    def _():
        cr_sc[...] = jnp.zeros_like(cr_sc)
        ci_sc[...] = jnp.zeros_like(ci_sc)
        if with_da:
            dar_all[...] = jnp.zeros_like(dar_all)
            dai_all[...] = jnp.zeros_like(dai_all)

    row = lax.broadcasted_iota(jnp.int32, (SUBLANES, lb), 0)
    for l0 in range(0, lanes, lb):
        _scan_lanes(x_r.at[:, pl.ds(l0, lb)], x_i.at[:, pl.ds(l0, lb)], ar_ref[0:1, pl.ds(l0, lb)],
                    ai_ref[0:1, pl.ds(l0, lb)], cr_sc.at[:, pl.ds(l0, lb)], ci_sc.at[:, pl.ds(l0, lb)], row, reverse,
                    nblk, None if not with_da else tuple(r.at[:, pl.ds(l0, lb)] for r in states[:6]) + (chunk,))


def _scan_lanes(x_r, x_i, a1r, a1i, cr_sc, ci_sc, row, reverse, nblk, states):
    lb = x_r.shape[1]
    with_da = states is not None
    if with_da:
        xr_ref, xi_ref, pr_ref, pi_ref, dar_ref, dai_ref, chunk = states
    if reverse:
        a1i = -a1i
    a2r, a2i = _cmul(a1r, a1i, a1r, a1i)
    a4r, a4i = _cmul(a2r, a2i, a2r, a2i)
    pw_r, pw_i = jnp.zeros((SUBLANES, lb), F32), jnp.zeros((SUBLANES, lb), F32)
    cur_r, cur_i = a1r, a1i
    for e in range(SUBLANES):
        r_at = (SUBLANES - 1 - e) if reverse else e
        pw_r = jnp.where(row == r_at, cur_r, pw_r)
        pw_i = jnp.where(row == r_at, cur_i, pw_i)
        cur_r, cur_i = _cmul(cur_r, cur_i, a1r, a1i)
    steps = []
    for d, pr, pi in ((1, a1r, a1i), (2, a2r, a2i), (4, a4r, a4i)):
        keep = (row < SUBLANES - d) if reverse else (row >= d)
        steps.append((d, jnp.where(keep, pr, 0.0), jnp.where(keep, pi, 0.0)))

    def block(jb, carry):
        if with_da:
            cr, ci, acc_r, acc_i = carry
        else:
            cr, ci = carry
        idx = (nblk - 1 - jb) if reverse else jb
        r0 = pl.multiple_of(idx * SUBLANES, SUBLANES)
        xr = x_r[pl.ds(r0, SUBLANES), :]
        xi = x_i[pl.ds(r0, SUBLANES), :]
        for d, pr, pi in steps:
            shift = SUBLANES - d if reverse else d
            sr, si = pltpu.roll(xr, shift, 0), pltpu.roll(xi, shift, 0)
            xr, xi = xr + pr * sr - pi * si, xi + pr * si + pi * sr
        xr, xi = xr + pw_r * cr - pw_i * ci, xi + pw_r * ci + pw_i * cr
        x_r[pl.ds(r0, SUBLANES), :] = xr
        x_i[pl.ds(r0, SUBLANES), :] = xi
        edge = 0 if reverse else SUBLANES - 1
        cr, ci = xr[edge:edge + 1, :], xi[edge:edge + 1, :]
        if not with_da:
            return cr, ci
        fr = xr_ref[pl.ds(r0, SUBLANES), :]
        fi = xi_ref[pl.ds(r0, SUBLANES), :]
        rp = pl.multiple_of(jnp.maximum(idx - 1, 0) * SUBLANES, SUBLANES)
        inside = idx > 0
        before_r = jnp.where(inside, xr_ref[pl.ds(rp, SUBLANES), :], pr_ref[...])
        before_i = jnp.where(inside, xi_ref[pl.ds(rp, SUBLANES), :], pi_ref[...])
        live = jnp.where(jnp.logical_or(inside, chunk > 0), 1.0, 0.0)
        last_r = before_r[SUBLANES - 1:SUBLANES, :] * live
        last_i = before_i[SUBLANES - 1:SUBLANES, :] * live
        pvr = jnp.where(row == 0, last_r, pltpu.roll(fr, 1, 0))
        pvi = jnp.where(row == 0, last_i, pltpu.roll(fi, 1, 0))
        acc_r = acc_r + xr * pvr + xi * pvi
        acc_i = acc_i + xi * pvr - xr * pvi
        return cr, ci, acc_r, acc_i

    init = (cr_sc[...], ci_sc[...])
    if with_da:
        init = init + (jnp.zeros((SUBLANES, lb), F32), jnp.zeros((SUBLANES, lb), F32))
    fin = lax.fori_loop(0, nblk, block, init)
    cr_sc[...] = fin[0]
    ci_sc[...] = fin[1]
    if with_da:
        dar_ref[...] += fin[2]
        dai_ref[...] += fin[3]


SSM_BLOCKS = 4
BLOCK_CH = SSM_WIDTH // SSM_BLOCKS
PREP_LANES = SSM_LANES // SSM_BLOCKS


def ssm_fwd(name, u, bb_r, bb_i, cb_r, ncb_i, a_r8, a_i8):
    S = u.shape[0]
    tc = min(SCAN_TIME_TILE, S)

    def body(u_ref, bbr_ref, bbi_ref, cbr_ref, ncbi_ref, ar_ref, ai_ref, xr_ref, xi_ref, y_ref, cr_sc, ci_sc):
        ub = u_ref[...].astype(BF16)
        xr_ref[...] = jnp.dot(ub, bbr_ref[...], preferred_element_type=F32)
        xi_ref[...] = jnp.dot(ub, bbi_ref[...], preferred_element_type=F32)
        _scan_tile(xr_ref, xi_ref, ar_ref, ai_ref, cr_sc, ci_sc, reverse=False, first=pl.program_id(1) == 0)
        y_ref[...] = (lax.dot_general(xr_ref[...].astype(BF16), cbr_ref[...], NT_DIMS, preferred_element_type=F32)
                      + lax.dot_general(xi_ref[...].astype(BF16), ncbi_ref[...], NT_DIMS, preferred_element_type=F32))

    ch = pl.BlockSpec((tc, BLOCK_CH), lambda b, t: (t, b))
    st = pl.BlockSpec((tc, PREP_LANES), lambda b, t: (t, b))
    wt = pl.BlockSpec((None, BLOCK_CH, PREP_LANES), lambda b, t: (b, 0, 0))
    par = pl.BlockSpec((SUBLANES, PREP_LANES), lambda b, t: (0, b))
    return pl.pallas_call(
        body, name=name, grid=(SSM_BLOCKS, S // tc), in_specs=[ch, wt, wt, wt, wt, par, par], out_specs=[st, st, ch],
        out_shape=[jax.ShapeDtypeStruct((S, SSM_LANES), F32), jax.ShapeDtypeStruct((S, SSM_LANES), F32),
                   jax.ShapeDtypeStruct((S, SSM_WIDTH), F32)],
        scratch_shapes=[pltpu.VMEM((1, PREP_LANES), F32), pltpu.VMEM((1, PREP_LANES), F32)], compiler_params=_params(2),
    )(u, bb_r, bb_i, cb_r, ncb_i, a_r8, a_i8)


def ssm_bwd(name, dy, du_add, u, xs_r, xs_i, bb_r, bb_i, cb_r, ncb_i, a_r8, a_i8):
    S = u.shape[0]
    tc = min(SCAN_TIME_TILE, S)
    nt = S // tc

    def body(dy_ref, dua_ref, u_ref, xr_ref, xi_ref, pr_ref, pi_ref, bbr_ref, bbi_ref, cbr_ref, ncbi_ref, ar_ref, ai_ref,
             du_ref, dbbr_ref, dbbi_ref, dcbr_ref, dncbi_ref, dar_ref, dai_ref, lr_sc, li_sc, cr_sc, ci_sc):
        t = pl.program_id(1)
        first = t == 0

        @pl.when(first)
        def _():
            for r in (dbbr_ref, dbbi_ref, dcbr_ref, dncbi_ref):
                r[...] = jnp.zeros_like(r)

        dyb = dy_ref[...].astype(BF16)
        lr_sc[...] = jnp.dot(dyb, cbr_ref[...], preferred_element_type=F32)
        li_sc[...] = jnp.dot(dyb, ncbi_ref[...], preferred_element_type=F32)
        _scan_tile(lr_sc, li_sc, ar_ref, ai_ref, cr_sc, ci_sc, reverse=True, first=first,
                   states=(xr_ref, xi_ref, pr_ref, pi_ref, dar_ref, dai_ref, nt - 1 - t))
        lrb, lib = lr_sc[...].astype(BF16), li_sc[...].astype(BF16)
        du_ref[...] = (lax.dot_general(lrb, bbr_ref[...], NT_DIMS, preferred_element_type=F32)
                       + lax.dot_general(lib, bbi_ref[...], NT_DIMS, preferred_element_type=F32) + dua_ref[...])
        ub = u_ref[...].astype(BF16)
        dbbr_ref[...] += lax.dot_general(ub, lrb, TN_DIMS, preferred_element_type=F32)
        dbbi_ref[...] += lax.dot_general(ub, lib, TN_DIMS, preferred_element_type=F32)
        dcbr_ref[...] += lax.dot_general(dyb, xr_ref[...].astype(BF16), TN_DIMS, preferred_element_type=F32)
        dncbi_ref[...] += lax.dot_general(dyb, xi_ref[...].astype(BF16), TN_DIMS, preferred_element_type=F32)

    ch = pl.BlockSpec((tc, BLOCK_CH), lambda b, t: (nt - 1 - t, b))
    st = pl.BlockSpec((tc, PREP_LANES), lambda b, t: (nt - 1 - t, b))
    prev = pl.BlockSpec((SUBLANES, PREP_LANES), lambda b, t: (jnp.maximum((nt - 1 - t) * (tc // SUBLANES) - 1, 0), b))
    wt = pl.BlockSpec((None, BLOCK_CH, PREP_LANES), lambda b, t: (b, 0, 0))
    par = pl.BlockSpec((SUBLANES, PREP_LANES), lambda b, t: (0, b))
    blk = jax.ShapeDtypeStruct((SSM_BLOCKS, BLOCK_CH, PREP_LANES), F32)
    rows8 = jax.ShapeDtypeStruct((SUBLANES, SSM_LANES), F32)
    return pl.pallas_call(
        body, name=name, grid=(SSM_BLOCKS, nt), in_specs=[ch, ch, ch, st, st, prev, prev, wt, wt, wt, wt, par, par],
        out_specs=[ch, wt, wt, wt, wt, par, par],
        out_shape=[jax.ShapeDtypeStruct((S, SSM_WIDTH), F32), blk, blk, blk, blk, rows8, rows8],
        scratch_shapes=[pltpu.VMEM((tc, PREP_LANES), F32), pltpu.VMEM((tc, PREP_LANES), F32),
                        pltpu.VMEM((1, PREP_LANES), F32), pltpu.VMEM((1, PREP_LANES), F32)],
        compiler_params=_params(2),
    )(dy, du_add, u, xs_r, xs_i, xs_r, xs_i, bb_r, bb_i, cb_r, ncb_i, a_r8, a_i8)


def _ssm_prep_f(a_re, a_im, log_dt, bt_re, bt_im, c_re, c_im):
    first_group = pl.program_id(0) * (SSM_GROUPS // SSM_BLOCKS)
    iota = lambda shape, d: lax.broadcasted_iota(jnp.int32, shape, d)
    grp_of_row = lambda shape: iota(shape, 0) >> int(math.log2(SSM_GROUP))
    grp_of_lane = lambda shape: iota(shape, 1) >> int(math.log2(SSM_STATE))
    rep = (grp_of_row((BLOCK_CH, SSM_GROUPS)) + first_group == iota((BLOCK_CH, SSM_GROUPS), 1)).astype(F32)
    til = ((iota((SSM_STATE, PREP_LANES), 1) & (SSM_STATE - 1)) == iota((SSM_STATE, PREP_LANES), 0)).astype(F32)
    m_rows = (grp_of_row((BLOCK_CH, PREP_LANES)) == grp_of_lane((BLOCK_CH, PREP_LANES))).astype(F32)
    m_grp = (iota((SSM_GROUPS, PREP_LANES), 0) == grp_of_lane((SSM_GROUPS, PREP_LANES)) + first_group).astype(F32)
    dt = jnp.exp(log_dt)
    decay = jnp.exp(a_re * dt)
    ar = decay * jnp.cos(a_im * dt)
    ai = decay * jnp.sin(a_im * dt)
    den = a_re * a_re + a_im * a_im
    nr = ar - 1.0
    coef_r = (nr * a_re + ai * a_im) / den
    coef_i = (ai * a_re - nr * a_im) / den
    cr, ci = _hp_dot(rep, coef_r), _hp_dot(rep, coef_i)
    bb_r = cr * bt_re - ci * bt_im
    bb_i = cr * bt_im + ci * bt_re
    big = lambda m: _hp_dot(m, til) * m_rows
    lanes = lambda m: jnp.broadcast_to(jnp.sum(_hp_dot(m, til) * m_grp, axis=0, keepdims=True), (SUBLANES, PREP_LANES))
    return lanes(ar), lanes(ai), big(bb_r), big(bb_i), big(c_re), -big(c_im)


def _whole(arr, **kw):
    nd = arr.ndim
    return In(arr, arr.shape, lambda *_: (0,) * nd, **kw)


def _adamw_math(w, g, m, v):
    m = ADAM_B1 * m + (1.0 - ADAM_B1) * g
    v = ADAM_B2 * v + (1.0 - ADAM_B2) * (g * g)
    m_hat = m / (1.0 - ADAM_B1 ** ADAM_STEP)
    v_hat = v / (1.0 - ADAM_B2 ** ADAM_STEP)
    delta = -ADAM_LR * (m_hat / (jnp.sqrt(v_hat) + ADAM_EPS) + ADAM_WD * w)
    return delta, m, v


def adamw_big(name, w, g, m, v):
    R, C = w.shape
    tr = R
    for cand in (512, 344, 256, 128):
        if R % cand == 0:
            tr = cand
            break

    def body(w_ref, g_ref, m_ref, v_ref, d_ref, nm_ref, nv_ref):
        d, nm, nv = _adamw_math(w_ref[...], g_ref[...], m_ref[...], v_ref[...])
        d_ref[...] = d
        nm_ref[...] = nm
        nv_ref[...] = nv

    spec = pl.BlockSpec((tr, C), lambda i: (i, 0))
    return pl.pallas_call(
        body, name=name, grid=(R // tr,), in_specs=[spec] * 4, out_specs=[spec] * 3,
        out_shape=[jax.ShapeDtypeStruct((R, C), F32)] * 3, compiler_params=_params(1),
    )(w, g, m, v)


def adamw_small(name, ws, gs, ms, vs):
    n = len(ws)

    def body(*refs):
        for k in range(n):
            d, nm, nv = _adamw_math(refs[k][...], refs[n + k][...], refs[2 * n + k][...], refs[3 * n + k][...])
            refs[4 * n + k][...] = d
            refs[5 * n + k][...] = nm
            refs[6 * n + k][...] = nv

    vm = pl.BlockSpec(memory_space=pltpu.VMEM)
    shapes = [jax.ShapeDtypeStruct(w.shape, F32) for w in ws]
    res = pl.pallas_call(
        body, name=name, in_specs=[vm] * (4 * n), out_specs=[vm] * (3 * n), out_shape=shapes * 3,
        compiler_params=pltpu.CompilerParams(vmem_limit_bytes=VMEM_LIMIT_BYTES),
    )(*ws, *gs, *ms, *vs)
    return res[:n], res[n:2 * n], res[2 * n:]


def _place():
    return lax.axis_index("x"), lax.axis_index("y"), lax.axis_index("c")


def _other_chips(x, y):
    return [(1 - x, y), (x, 1 - y), (1 - x, 1 - y)]


HBM = pl.BlockSpec(memory_space=pl.ANY)


def all_gather_halves(name, blocks):
    n = len(blocks)

    def body(*refs):
        in_refs, out_refs = refs[:n], refs[n:2 * n]
        send_sems, recv_sems = refs[2 * n:]
        x, y, c = _place()
        me, sibling = (x, y, c), (x, y, 1 - c)
        chips = _other_chips(x, y)

        def slot(a, px, py, pc):
            return out_refs[a].at[4 * px + 2 * py + pc]

        def copy(a, k, block, to, src=None):
            return pltpu.make_async_remote_copy(
                src_ref=slot(a, *block) if src is None else src, dst_ref=slot(a, *block),
                send_sem=send_sems.at[a, k], recv_sem=recv_sems.at[a, k], device_id=to, device_id_type=MESH)

        first = []
        for a in range(n):
            first.append(copy(a, 0, me, sibling, src=in_refs[a]))
            first += [copy(a, 1 + j, me, (*chip, c), src=in_refs[a]) for j, chip in enumerate(chips)]
        for cp in first:
            cp.start()
        passed = []
        for j, chip in enumerate(chips):
            for a in range(n):
                copy(a, 1 + j, (*chip, c), me).wait_recv()
                fw = copy(a, 4 + j, (*chip, c), sibling)
                fw.start()
                passed.append(fw)
        for a in range(n):
            copy(a, 0, sibling, me).wait_recv()
            for j, chip in enumerate(chips):
                copy(a, 4 + j, (*chip, 1 - c), me).wait_recv()
        for cp in first + passed:
            cp.wait_send()

    res = pl.pallas_call(
        body, name=name, in_specs=[HBM] * n, out_specs=[HBM] * n,
        out_shape=[jax.ShapeDtypeStruct((N_DEV,) + b.shape, b.dtype) for b in blocks],
        scratch_shapes=[pltpu.SemaphoreType.DMA((n, 7)), pltpu.SemaphoreType.DMA((n, 7))],
    )(*blocks)
    return list(res)


def pair_send_halves(name, grads):
    n = len(grads)

    def body(*refs):
        in_refs, out_refs = refs[:n], refs[n:2 * n]
        send_sems, recv_sems = refs[2 * n:]
        x, y, c = _place()
        cps = []
        for a in range(n):
            cp = pltpu.make_async_remote_copy(
                src_ref=in_refs[a].at[:, 1 - c], dst_ref=out_refs[a], send_sem=send_sems.at[a], recv_sem=recv_sems.at[a],
                device_id=(x, y, 1 - c), device_id_type=MESH)
            cp.start()
            cps.append(cp)
        for cp in cps:
            cp.wait()

    res = pl.pallas_call(
        body, name=name, in_specs=[HBM] * n, out_specs=[HBM] * n,
        out_shape=[jax.ShapeDtypeStruct((g.shape[0],) + g.shape[2:], g.dtype) for g in grads],
        scratch_shapes=[pltpu.SemaphoreType.DMA((n,)), pltpu.SemaphoreType.DMA((n,))],
    )(*grads)
    return list(res)


def pair_add(name, grad, got, c_arr):
    nsh, _, M, N = grad.shape
    tr = M
    for cand in (512, 256, 192, 128, 64, 16):
        if M % cand == 0:
            tr = cand
            break

    def body(c_ref, g_ref, p_ref, o_ref):
        o_ref[...] = (g_ref[...] + p_ref[...]).astype(BF16)

    return pl.pallas_call(
        body, name=name,
        grid_spec=pltpu.PrefetchScalarGridSpec(
            num_scalar_prefetch=1, grid=(nsh, M // tr),
            in_specs=[pl.BlockSpec((None, None, tr, N), lambda j, i, c_ref: (j, c_ref[0], i, 0)),
                      pl.BlockSpec((None, tr, N), lambda j, i, c_ref: (j, i, 0))],
            out_specs=pl.BlockSpec((None, tr, N), lambda j, i, c_ref: (j, i, 0))),
        out_shape=jax.ShapeDtypeStruct((nsh, M, N), BF16), compiler_params=_params(2),
    )(c_arr, grad, got)


def scatter_to_chips(name, parts):
    n = len(parts)

    def body(*refs):
        in_refs, out_refs = refs[:n], refs[n:2 * n]
        send_sems, recv_sems = refs[2 * n:]
        x, y, c = _place()
        mine = 2 * x + y
        chips = _other_chips(x, y)
        cps = []
        for a in range(n):
            for k, (px, py) in enumerate(chips):
                cp = pltpu.make_async_remote_copy(
                    src_ref=in_refs[a].at[2 * px + py], dst_ref=out_refs[a].at[mine],
                    send_sem=send_sems.at[a, k], recv_sem=recv_sems.at[a, k], device_id=(px, py, c), device_id_type=MESH)
                cp.start()
                cps.append((cp, a, k, px, py))
        for cp, a, k, px, py in cps:
            pltpu.make_async_remote_copy(
                src_ref=in_refs[a].at[mine], dst_ref=out_refs[a].at[2 * px + py],
                send_sem=send_sems.at[a, k], recv_sem=recv_sems.at[a, k], device_id=(px, py, c), device_id_type=MESH).wait_recv()
        for cp, *_ in cps:
            cp.wait_send()

    res = pl.pallas_call(
        body, name=name, in_specs=[HBM] * n, out_specs=[HBM] * n,
        out_shape=[jax.ShapeDtypeStruct(p.shape, p.dtype) for p in parts],
        scratch_shapes=[pltpu.SemaphoreType.DMA((n, 3)), pltpu.SemaphoreType.DMA((n, 3))],
    )(*parts)
    return list(res)


def sum_chips(name, q):
    nsh, M, N = q.shape
    tr = M
    for cand in (512, 256, 192, 128, 64, 16):
        if M % cand == 0:
            tr = cand
            break

    def body(q_ref, o_ref):
        acc = q_ref[0].astype(F32)
        for j in range(1, nsh):
            acc = acc + q_ref[j].astype(F32)
        o_ref[...] = acc

    return pl.pallas_call(
        body, name=name, grid=(M // tr,), in_specs=[pl.BlockSpec((nsh, tr, N), lambda i: (0, i, 0))],
        out_specs=pl.BlockSpec((tr, N), lambda i: (i, 0)), out_shape=jax.ShapeDtypeStruct((M, N), F32),
        compiler_params=_params(1),
    )(q)


def pair_exchange(name, halves):
    n = len(halves)

    def body(*refs):
        in_refs, out_refs = refs[:n], refs[n:2 * n]
        send_sems, recv_sems = refs[2 * n:]
        x, y, c = _place()
        cps = []
        for a in range(n):
            cp = pltpu.make_async_remote_copy(
                src_ref=in_refs[a], dst_ref=out_refs[a], send_sem=send_sems.at[a], recv_sem=recv_sems.at[a],
                device_id=(x, y, 1 - c), device_id_type=MESH)
            cp.start()
            cps.append(cp)
        for cp in cps:
            cp.wait()

    res = pl.pallas_call(
        body, name=name, in_specs=[HBM] * n, out_specs=[HBM] * n,
        out_shape=[jax.ShapeDtypeStruct(h.shape, h.dtype) for h in halves],
        scratch_shapes=[pltpu.SemaphoreType.DMA((n,)), pltpu.SemaphoreType.DMA((n,))],
    )(*halves)
    return list(res)


SEM = pl.BlockSpec(memory_space=pltpu.SEMAPHORE)
IN_HBM = pl.BlockSpec(memory_space=pltpu.HBM)
SPLIT_COPY = pltpu.CompilerParams(has_side_effects=pltpu.SideEffectType.DATAFLOW_SIDE_EFFECTING)


def _scatter_copies(src_refs, dst_refs, send_sems, recv_sems):
    x, y, c = _place()
    mine = 2 * x + y
    return [pltpu.make_async_remote_copy(
        src_ref=src_refs[a].at[2 * px + py], dst_ref=dst_refs[a].at[mine], send_sem=send_sems.at[a * (N_CHIPS - 1) + k],
        recv_sem=recv_sems.at[a * (N_CHIPS - 1) + k], device_id=(px, py, c), device_id_type=MESH)
        for a in range(len(src_refs)) for k, (px, py) in enumerate(_other_chips(x, y))]


def _pair_copies(src_refs, dst_refs, send_sems, recv_sems):
    x, y, c = _place()
    return [pltpu.make_async_remote_copy(
        src_ref=src_refs[a].at[:, 1 - c], dst_ref=dst_refs[a], send_sem=send_sems.at[a], recv_sem=recv_sems.at[a],
        device_id=(x, y, 1 - c), device_id_type=MESH) for a in range(len(src_refs))]


def _gather_copies(src_refs, dst_refs, send_sems, recv_sems):
    x, y, c = _place()
    me = 4 * x + 2 * y + c
    cps = []
    for a in range(len(src_refs)):
        for k in range(1, N_DEV):
            to = (1 - x if k & 4 else x, 1 - y if k & 2 else y, 1 - c if k & 1 else c)
            s = a * (N_DEV - 1) + k - 1
            cps.append(pltpu.make_async_remote_copy(
                src_ref=src_refs[a], dst_ref=dst_refs[a].at[me], send_sem=send_sems.at[s], recv_sem=recv_sems.at[s],
                device_id=to, device_id_type=MESH))
    return cps


def split_copy_start(name, copies, n_sem, srcs, land_shapes):
    n = len(srcs)
    lands = [lax.empty(s.shape, s.dtype) for s in land_shapes]

    def body(*refs):
        for cp in copies(refs[:n], refs[n:2 * n], refs[2 * n], refs[2 * n + 1]):
            cp.start()
        refs[-1][...] = jnp.zeros_like(refs[-1])

    thru = [pltpu.HBM(a.shape, a.dtype) for a in (*srcs, *lands)]
    res = pl.pallas_call(
        body, name=name, in_specs=[IN_HBM] * (2 * n),
        out_specs=(SEM, SEM, *[IN_HBM] * (2 * n), pl.BlockSpec(memory_space=pltpu.VMEM)),
        out_shape=(pltpu.SemaphoreType.DMA((n * n_sem,)), pltpu.SemaphoreType.DMA((n * n_sem,)), *thru,
                   jax.ShapeDtypeStruct((SUBLANES, LANES), F32)),
        input_output_aliases={i: 2 + i for i in range(2 * n)}, compiler_params=SPLIT_COPY,
    )(*[pltpu.with_memory_space_constraint(a, pltpu.HBM) for a in (*srcs, *lands)])
    return (copies, n, res[0], res[1], res[2:2 + 2 * n]), res[-1][0, 0]


def split_copy_wait(name, handle, after):
    copies, n, send_sems, recv_sems, thru = handle

    def body(*refs):
        for cp in copies(refs[:n], refs[n:2 * n], refs[2 * n], refs[2 * n + 1]):
            cp.wait_send()
            cp.wait_recv()

    res = pl.pallas_call(
        body, name=name, in_specs=[IN_HBM] * (2 * n) + [SEM, SEM, pl.BlockSpec(memory_space=pl.ANY)],
        out_specs=[IN_HBM] * (2 * n), out_shape=[pltpu.HBM(a.shape, a.dtype) for a in thru],
        input_output_aliases={i: i for i in range(2 * n)}, compiler_params=SPLIT_COPY,
    )(*thru, send_sems, recv_sems, after)
    return list(res[:n]), list(res[n:])


def all_reduce_small(name, v):
    R, C = v.shape

    def body(v_ref, o_ref, gath, send_sems, recv_sems):
        x, y, c = _place()
        me, sibling = (x, y, c), (x, y, 1 - c)
        chips = _other_chips(x, y)

        def slot(px, py, pc):
            return gath.at[4 * px + 2 * py + pc]

        def copy(k, block, to, src=None):
            return pltpu.make_async_remote_copy(
                src_ref=slot(*block) if src is None else src, dst_ref=slot(*block),
                send_sem=send_sems.at[k], recv_sem=recv_sems.at[k], device_id=to, device_id_type=MESH)

        first = [copy(0, me, sibling, src=v_ref)]
        first += [copy(1 + j, me, (*chip, c), src=v_ref) for j, chip in enumerate(chips)]
        for cp in first:
            cp.start()
        slot(*me)[...] = v_ref[...]
        passed = [copy(4 + j, (*chip, c), sibling) for j, chip in enumerate(chips)]
        for j, chip in enumerate(chips):
            copy(1 + j, (*chip, c), me).wait_recv()
            passed[j].start()
        copy(0, sibling, me).wait_recv()
        for j, chip in enumerate(chips):
            copy(4 + j, (*chip, 1 - c), me).wait_recv()
        for cp in first + passed:
            cp.wait_send()
        acc = gath[0]
        for d in range(1, N_DEV):
            acc = acc + gath[d]
        o_ref[...] = acc

    vm = pl.BlockSpec(memory_space=pltpu.VMEM)
    return pl.pallas_call(
        body, name=name, in_specs=[vm], out_specs=vm, out_shape=jax.ShapeDtypeStruct((R, C), F32),
        scratch_shapes=[pltpu.VMEM((N_DEV, R, C), F32), pltpu.SemaphoreType.DMA((7,)), pltpu.SemaphoreType.DMA((7,))],
        compiler_params=pltpu.CompilerParams(vmem_limit_bytes=VMEM_LIMIT_BYTES),
    )(v)


LANES = 128
PACK_ROW_MULTIPLE = 1024
SMALL_SHARDED = {
    "w_in": ((D_MODEL, 1216), 1), "mla_w_uq": ((MLA_Q_RANK, 768), 1), "mla_w_ukv": ((MLA_KV_RANK, 1024), 1),
    "ssm_w_glu": ((SSM_WIDTH, SSM_WIDTH), 0), "w_o": ((D_MODEL, D_MODEL), 0), "xattn_w_q": ((D_MODEL, 512), 0),
    "xattn_w_kv": ((D_MODEL, 1024), 0), "xattn_w_o": ((512, D_MODEL), 1),
}
FFN_NAMES = ["ffn1_w_gate", "ffn1_w_up", "ffn1_w_down", "ffn2_w_gate", "ffn2_w_up", "ffn2_w_down"]
TRANSPOSED_VIEW = ("ffn1_w_gate", "ffn1_w_up", "ffn2_w_gate", "ffn2_w_up", "w_in", "mla_w_uq")


def _shard_shape(name):
    (r, cdim), ax = SMALL_SHARDED[name]
    return (r // N_CHIPS, cdim) if ax == 0 else (r, cdim // N_CHIPS)


def _pack_shards(shards):
    parts = []
    for name in SMALL_SHARDED:
        a = shards[name]
        lead = a.shape[:-2]
        parts.append(a.reshape(lead + (a.shape[-2] * a.shape[-1] // LANES, LANES)))
    rows = sum(q.shape[-2] for q in parts)
    parts.append(jnp.zeros(lead + (-rows % PACK_ROW_MULTIPLE, LANES), parts[0].dtype))
    return jnp.concatenate(parts, axis=-2)


def _unpack_shards(packed):
    out, r0 = {}, 0
    lead = packed.shape[:-2]
    for name in SMALL_SHARDED:
        r, cdim = _shard_shape(name)
        rows = r * cdim // LANES
        out[name] = packed[..., r0:r0 + rows, :].reshape(lead + (r, cdim))
        r0 += rows
    return out


def _full_from_shards(name, sh):
    (r, cdim), ax = SMALL_SHARDED[name]
    if ax == 0:
        return sh.reshape(r, cdim)
    return jnp.transpose(sh, (1, 0, 2)).reshape(r, cdim)


def _shards_from_full(name, full):
    (r, cdim), ax = SMALL_SHARDED[name]
    if ax == 0:
        return full.reshape(N_CHIPS, r // N_CHIPS, cdim)
    return jnp.transpose(full.reshape(r, N_CHIPS, cdim // N_CHIPS), (1, 0, 2))


SMALL_REPL = {
    "ffn1_norm": (1, 1024), "mix_norm": (1, 1024), "mla_q_norm": (1, 384), "mla_kv_norm": (1, 256),
    "mla_qk_norm_q": (1, 192), "mla_qk_norm_k": (1, 192), "ssm_a_re": (32, 64), "ssm_a_im": (32, 64),
    "ssm_log_dt": (32, 1), "ssm_b_re": (32, 64, 16), "ssm_b_im": (32, 64, 16), "ssm_c_re": (32, 16, 64),
    "ssm_c_im": (32, 16, 64), "ssm_d": (1, 512), "ssm_b_glu": (1, 512), "out_norm_mla": (1, 512),
    "out_norm_ssm": (1, 512), "xattn_norm": (1, 1024), "mem_norm": (1, 1024), "xattn_q_norm": (1, 128),
    "xattn_k_norm": (1, 128), "ffn2_norm": (1, 1024),
}


def _pack_repl(grads, loss):
    flat = jnp.concatenate([grads[n].reshape(-1) for n in SMALL_REPL] + [loss.reshape(1)])
    rows = -(-flat.shape[0] // (LANES * SUBLANES)) * SUBLANES
    return jnp.pad(flat, (0, rows * LANES - flat.shape[0])).reshape(rows, LANES)


def _unpack_repl(packed):
    flat, out, o = packed.reshape(-1), {}, 0
    for n, shp in SMALL_REPL.items():
        size = int(np.prod(shp))
        out[n] = flat[o:o + size].reshape(shp)
        o += size
    out["loss"] = flat[o]
    return out


def _rope_tables(positions):
    half = MLA_ROPE // 2
    inv = ROPE_THETA ** (-jnp.arange(half, dtype=F32) / half)
    ang = positions.astype(F32)[:, None] * inv[None, :]
    cos, sin = jnp.cos(ang), jnp.sin(ang)
    S = positions.shape[0]
    z = lambda w: jnp.zeros((S, w), F32)
    keep = jnp.concatenate([jnp.ones((S, MLA_NOPE), F32), cos, cos, z(HEAD_PAD - MLA_QK)], axis=1)
    from_hi = jnp.concatenate([z(MLA_NOPE), -sin, z(HEAD_PAD - MLA_NOPE - half)], axis=1)
    from_lo = jnp.concatenate([z(MLA_NOPE + half), sin, z(HEAD_PAD - MLA_QK)], axis=1)
    return keep, from_hi, from_lo


def _norm_rope(x, g, keep, from_hi, from_lo):
    y = x * lax.rsqrt(jnp.sum(x * x, axis=-1, keepdims=True) * (1.0 / MLA_QK) + EPS) * g
    half = MLA_ROPE // 2
    return y * keep + _lane_roll(y, HEAD_PAD - half) * from_hi + _lane_roll(y, half) * from_lo


def local_step(x, mem, positions, target, w, wb, small_weights=None, ffn2_weights=None, on_grads=None):
    if small_weights is None:
        small_weights = lambda after: {}
    if ffn2_weights is None:
        ffn2_weights = lambda after: [wb[k] for k in FFN_NAMES[3:]]
    if on_grads is None:
        on_grads = lambda tag, g: 0.0
    S = x.shape[0]
    tm = min(FUSED_TILE, S)
    g1 = (S // tm,)
    tile = lambda arr, **kw: In(arr, (tm, arr.shape[1]), lambda i: (i, 0), rows=True, **kw)
    par = lambda arr, **kw: In(arr, arr.shape, lambda *_: (0, 0), diff=True, acc=(0,), **kw)
    wt = lambda arr: In(arr, weight=True)
    otile = lambda cols, dt: Out((S, cols), dt, (tm, cols), lambda i: (i, 0))
    grads = {}

    x1, h1, g_1, u_1 = ffn_fwd("ffn1_fwd", x, w["ffn1_norm"], wb["ffn1_w_gate"], wb["ffn1_w_up"], wb["ffn1_w_down"])
    wb = {**wb, **small_weights(x1)}

    keep, from_hi, from_lo = _rope_tables(positions)
    scale = MLA_QK ** -0.5

    def f_pre(xv, kp, fh, fl, g_mix, w_q, w_kv, w_kr, w_u, g_q, g_kv, w_uq, w_ukv, gq, gk):
        h2 = _rms(xv, g_mix)
        cq, ckv = _rms(wdot(h2, w_q), g_q), _rms(wdot(h2, w_kv), g_kv)
        kr = wdot(h2, w_kr)
        qs, ks, vs = [], [], []
        for h in range(MLA_HEADS):
            qs.append(_norm_rope(wdot(cq, w_uq, h), gq, kp, fh, fl) * scale)
            kv = wdot(ckv, w_ukv, h)
            ks.append(_norm_rope(jnp.concatenate([kv[:, :MLA_NOPE], kr], axis=-1), gk, kp, fh, fl))
            vs.append(kv[:, MLA_NOPE:])
        return jnp.concatenate(qs, axis=-1), jnp.concatenate(ks, axis=-1), jnp.concatenate(vs, axis=-1), wdot(h2, w_u)

    def pre_ins(gain_mix):
        return [tile(x1, diff=True), tile(keep), tile(from_hi), tile(from_lo), par(gain_mix),
                wt(wb["w_in_q"]), wt(wb["w_in_kv"]), wt(wb["w_in_kr"]), wt(wb["w_in_u"]),
                par(w["mla_q_norm"]), par(w["mla_kv_norm"]), wt(wb["w_uq"]), wt(wb["w_ukv"]),
                par(w["qk_gain_q"]), par(w["qk_gain_k"])]

    pre_outs = [otile(MLA_HEADS * HEAD_PAD, BF16), otile(MLA_HEADS * HEAD_PAD, BF16), otile(MLA_HEADS * MLA_V, BF16),
                otile(SSM_WIDTH, F32)]
    qh, kh, v0, pu = seg_fwd("pre_mixer_fwd", f_pre, g1, pre_ins(w["mix_norm"]), pre_outs)

    o_mla, lse = attn_fwd("mla_attn_fwd", qh, kh, v0, t=min(ATTN_TILE, S))

    prep_grid = (SSM_BLOCKS,)
    prep_ins = ([_whole(w[k], diff=True, acc=(0,)) for k in ("ssm_a_re", "ssm_a_im", "ssm_log_dt")]
                + [In(w[k], (BLOCK_CH, SSM_STATE), lambda i: (i, 0), diff=True)
                   for k in ("ssm_bt_re", "ssm_bt_im", "ssm_c2_re", "ssm_c2_im")])
    blk_out = Out((SSM_BLOCKS, BLOCK_CH, PREP_LANES), BF16, (None, BLOCK_CH, PREP_LANES), lambda i: (i, 0, 0))
    prep_outs = [Out((SUBLANES, SSM_LANES), F32, (SUBLANES, PREP_LANES), lambda i: (0, i))] * 2 + [blk_out] * 4
    a_r8, a_i8, bb_r, bb_i, cb_r, ncb_i = seg_fwd("ssm_prep_fwd", _ssm_prep_f, prep_grid, prep_ins, prep_outs)

    xs_r, xs_i, y_lin = ssm_fwd("ssm_fwd", pu, bb_r, bb_i, cb_r, ncb_i, a_r8, a_i8)

    M = mem.shape[0]
    f_norm = lambda xv, g: (_rms(xv, g),)
    mem_ins = [In(mem, (M, D_MODEL), lambda i: (0, 0)), par(w["mem_norm"])]
    mem_outs = [Out((M, D_MODEL), BF16, (M, D_MODEL), lambda i: (0, 0))]
    (mn,) = seg_fwd("mem_norm_fwd", f_norm, (1,), mem_ins, mem_outs)
    (kvm,) = mm("xattn_kv_fwd", [mn], [[wb["xattn_w_kv"]]], [F32])

    def f_knorm(kk, gk):
        return (jnp.concatenate([_rms(kk[:, h * XHD:(h + 1) * XHD], gk) for h in range(XH)], axis=-1),)

    keys = ((M, XH * XHD), (M, XH * XHD), lambda i: (0, 0))
    knorm_ins = [In(kvm, keys[1], keys[2], diff=True, grad=keys), par(w["xattn_k_norm"])]
    knorm_outs = [Out(keys[0], F32, keys[1], keys[2])]
    (kn_mem,) = seg_fwd("mem_key_norm_fwd", f_knorm, (1,), knorm_ins, knorm_outs)

    xscale = XHD ** -0.5

    def f_post(o, yl, u, xv, kn, vm, d, w_glu, b, gm, gs, w_o1, w_o2, gx, w_xq, gq, w_xo):
        gl = jax.nn.gelu(yl + d * u)
        so = gl * jax.nn.sigmoid(wdot(gl, w_glu) + b)
        x2 = xv + wdot(_rms(o, gm), w_o1) + wdot(_rms(so, gs), w_o2)
        h3 = _rms(x2, gx)
        heads = []
        for h in range(XH):
            qn = _rms(wdot(h3, w_xq, h), gq)
            p = jax.nn.softmax(_bdot_nt(qn, kn[:, h * XHD:(h + 1) * XHD]) * xscale, axis=-1)
            heads.append(_bdot_nn(p, vm[:, h * XHD:(h + 1) * XHD]))
        return (x2 + wdot(jnp.concatenate(heads, axis=-1), w_xo),)

    def post_ins(gain_d):
        half = (M, XH * XHD)
        return [tile(o_mla, diff=True), tile(y_lin, diff=True), tile(pu, diff=True), tile(x1, diff=True),
                In(kn_mem, half, lambda i: (0, 0), diff=True, acc=(0,)),
                In(kvm, half, lambda i: (0, 1), diff=True, acc=(0,), grad=(half, half, lambda i: (0, 0))),
                par(gain_d), wt(wb["ssm_w_glu"]), par(w["ssm_b_glu"]), par(w["out_norm_mla"]), par(w["out_norm_ssm"]),
                wt(wb["w_o_mla"]), wt(wb["w_o_ssm"]), par(w["xattn_norm"]), wt(wb["xattn_w_q"]), par(w["xattn_q_norm"]),
                wt(wb["xattn_w_o"])]

    post_outs = [otile(D_MODEL, F32)]
    (x3,) = seg_fwd("post_mixer_fwd", f_post, g1, post_ins(w["ssm_d"]), post_outs)

    wg2, wu2, wd2 = ffn2_weights(x3)
    dx4, h4, g_2, u_2, parts = ffn_fwd("ffn2_fwd_loss", x3, w["ffn2_norm"], wg2, wu2, wd2, target=target)
    loss = jnp.sum(parts[::SUBLANES, 0])

    dx3, grads["ffn2_norm"], dg_2, du_2 = ffn_bwd_act("ffn2_bwd_act", dx4, x3, w["ffn2_norm"], g_2, u_2, wg2, wu2, wd2)
    grads["ffn2_w_gate"], grads["ffn2_w_up"], grads["ffn2_w_down"] = ffn_bwd_w("ffn2_bwd_w", h4, dx4, g_2, u_2, dg_2, du_2)
    sent = on_grads("ffn2", grads)

    (do_mla, dy_lin, du_a, dx1_a, dkn, dvm, grads["ssm_d"], grads["ssm_w_glu"], grads["ssm_b_glu"], grads["out_norm_mla"],
     grads["out_norm_ssm"], grads["w_o_mla"], grads["w_o_ssm"], grads["xattn_norm"], grads["xattn_w_q"],
     grads["xattn_q_norm"], grads["xattn_w_o"]) = seg_bwd(
        "post_mixer_bwd", f_post, g1, post_ins(w["ssm_d"] + sent), post_outs, [dx3])

    dkk, grads["xattn_k_norm"] = seg_bwd("mem_key_norm_bwd", f_knorm, (1,), knorm_ins, knorm_outs, [dkn])
    w_kv = wb["xattn_w_kv"]
    (dmn,) = mm("xattn_kv_bwd", [dkk, dvm], [[w_kv[:, :XH * XHD]], [w_kv[:, XH * XHD:]]], [F32], trans=True)
    gk_w, gv_w = mm_tn("xattn_kv_bwd_w", [mn], [dkk, dvm], [(0, [0]), (0, [1])])
    grads["xattn_w_kv"] = jnp.concatenate([gk_w, gv_w], axis=1)
    (grads["mem_norm"],) = seg_bwd("mem_norm_bwd", f_norm, (1,), mem_ins, mem_outs, [dmn])

    du, dbb_r, dbb_i, dcb_r, dncb_i, da_r8, da_i8 = ssm_bwd("ssm_bwd", dy_lin, du_a, pu, xs_r, xs_i, bb_r, bb_i, cb_r, ncb_i,
                                                            a_r8, a_i8)
    prep_g = seg_bwd("ssm_prep_bwd", _ssm_prep_f, prep_grid, prep_ins, prep_outs, [da_r8, da_i8, dbb_r, dbb_i, dcb_r, dncb_i])
    for k, gname in enumerate(("ssm_a_re", "ssm_a_im", "ssm_log_dt", "ssm_bt_re", "ssm_bt_im", "ssm_c2_re", "ssm_c2_im")):
        grads[gname] = prep_g[k]

    dqh, dkh, dv0 = attn_bwd("mla_attn_bwd", qh, kh, v0, do_mla, o_mla, lse, t=min(ATTN_TILE, S))

    (dx1, grads["mix_norm"], grads["w_in_q"], grads["w_in_kv"], grads["w_in_kr"], grads["w_in_u"], grads["mla_q_norm"],
     grads["mla_kv_norm"], grads["w_uq"], grads["w_ukv"], grads["qk_gain_q"], grads["qk_gain_k"]) = seg_bwd(
        "pre_mixer_bwd", f_pre, g1, pre_ins(w["mix_norm"]), pre_outs, [dqh, dkh, dv0, du], adds={0: dx1_a},
        row_block=FUSED_ROW_BLOCK)

    sent = on_grads("small", grads)
    dx, grads["ffn1_norm"], dg_1, du_1 = ffn_bwd_act("ffn1_bwd_act", dx1, x, w["ffn1_norm"] + sent, g_1, u_1,
                                                     wb["ffn1_w_gate"], wb["ffn1_w_up"], wb["ffn1_w_down"])
    grads["ffn1_w_gate"], grads["ffn1_w_up"], grads["ffn1_w_down"] = ffn_bwd_w("ffn1_bwd_w", h1, dx1, g_1, u_1, dg_1, du_1)
    return loss, dx, grads


def _pad_cols(a, n):
    return jnp.pad(a, ((0, 0), (0, n - a.shape[1])))


def _step_weights(shards):
    wb = {}
    w_in = _full_from_shards("w_in", shards["w_in"])
    wb["w_in_q"] = w_in[:, :MLA_Q_RANK]
    wb["w_in_kv"] = w_in[:, MLA_Q_RANK:MLA_Q_RANK + MLA_KV_RANK]
    wb["w_in_kr"] = _pad_cols(w_in[:, MLA_Q_RANK + MLA_KV_RANK:MLA_Q_RANK + MLA_KV_RANK + MLA_ROPE], LANES)
    wb["w_in_u"] = w_in[:, MLA_Q_RANK + MLA_KV_RANK + MLA_ROPE:]
    wb["w_uq"] = jnp.pad(shards["mla_w_uq"], ((0, 0), (0, 0), (0, HEAD_PAD - MLA_QK)))
    wb["w_ukv"] = shards["mla_w_ukv"]
    wb["ssm_w_glu"] = _full_from_shards("ssm_w_glu", shards["ssm_w_glu"])
    w_o = _full_from_shards("w_o", shards["w_o"])
    wb["w_o_mla"], wb["w_o_ssm"] = w_o[:SSM_WIDTH], w_o[SSM_WIDTH:]
    w_xq = _full_from_shards("xattn_w_q", shards["xattn_w_q"])
    wb["xattn_w_q"] = jnp.transpose(w_xq.reshape(D_MODEL, XH, XHD), (1, 0, 2))
    wb["xattn_w_kv"] = _full_from_shards("xattn_w_kv", shards["xattn_w_kv"])
    wb["xattn_w_o"] = _full_from_shards("xattn_w_o", shards["xattn_w_o"])
    return wb


def _sharded_grads(g):
    out = {}
    kr = g["w_in_kr"][:, :MLA_ROPE]
    out["w_in"] = _shards_from_full("w_in", jnp.concatenate([g["w_in_q"], g["w_in_kv"], kr, g["w_in_u"]], axis=1))
    out["mla_w_uq"] = g["w_uq"][:, :, :MLA_QK]
    out["mla_w_ukv"] = g["w_ukv"]
    out["ssm_w_glu"] = _shards_from_full("ssm_w_glu", g["ssm_w_glu"])
    out["w_o"] = _shards_from_full("w_o", jnp.concatenate([g["w_o_mla"], g["w_o_ssm"]], axis=0))
    w_xq = jnp.transpose(g["xattn_w_q"], (1, 0, 2)).reshape(D_MODEL, XH * XHD)
    out["xattn_w_q"] = _shards_from_full("xattn_w_q", w_xq)
    out["xattn_w_kv"] = _shards_from_full("xattn_w_kv", g["xattn_w_kv"])
    out["xattn_w_o"] = _shards_from_full("xattn_w_o", g["xattn_w_o"])
    return out


def _problem_repl_grads(g):
    out = {}
    out["mla_qk_norm_q"] = g["qk_gain_q"][:, :MLA_QK]
    out["mla_qk_norm_k"] = g["qk_gain_k"][:, :MLA_QK]
    out["ssm_b_re"] = jnp.transpose(g["ssm_bt_re"].reshape(SSM_GROUPS, SSM_GROUP, SSM_STATE), (0, 2, 1))
    out["ssm_b_im"] = jnp.transpose(g["ssm_bt_im"].reshape(SSM_GROUPS, SSM_GROUP, SSM_STATE), (0, 2, 1))
    out["ssm_c_re"] = g["ssm_c2_re"].reshape(SSM_GROUPS, SSM_GROUP, SSM_STATE)
    out["ssm_c_im"] = g["ssm_c2_im"].reshape(SSM_GROUPS, SSM_GROUP, SSM_STATE)
    for k in SMALL_REPL:
        if k not in out:
            out[k] = g[k]
    return out


def _problem_grads(g):
    out = {k: _full_from_shards(k, v) for k, v in _sharded_grads(g).items()}
    out.update(_problem_repl_grads(g))
    out.update({k: g[k] for k in FFN_NAMES})
    return out


def _step_params(p):
    row = lambda a: a.reshape(1, -1)
    w = {k: row(p[k]) for k in ("ffn1_norm", "mix_norm", "mla_q_norm", "mla_kv_norm", "ssm_b_glu", "out_norm_mla",
                                "out_norm_ssm", "xattn_norm", "mem_norm", "xattn_q_norm", "xattn_k_norm", "ffn2_norm")}
    w["qk_gain_q"] = _pad_cols(row(p["mla_qk_norm_q"]), HEAD_PAD)
    w["qk_gain_k"] = _pad_cols(row(p["mla_qk_norm_k"]), HEAD_PAD)
    w["ssm_a_re"], w["ssm_a_im"] = p["ssm_a_re"], p["ssm_a_im"]
    w["ssm_log_dt"] = p["ssm_log_dt"].reshape(SSM_GROUPS, 1)
    w["ssm_bt_re"] = jnp.transpose(p["ssm_b_re"], (0, 2, 1)).reshape(SSM_WIDTH, SSM_STATE)
    w["ssm_bt_im"] = jnp.transpose(p["ssm_b_im"], (0, 2, 1)).reshape(SSM_WIDTH, SSM_STATE)
    w["ssm_c2_re"] = p["ssm_c_re"].reshape(SSM_WIDTH, SSM_STATE)
    w["ssm_c2_im"] = p["ssm_c_im"].reshape(SSM_WIDTH, SSM_STATE)
    w["ssm_d"] = p["ssm_d"].reshape(1, SSM_WIDTH)
    return w


ARG_NAMES = ['x', 'mem', 'positions', 'ffn1_norm', 'ffn1_w_gate', 'ffn1_w_up', 'ffn1_w_down', 'mix_norm', 'w_in', 'mla_q_norm', 'mla_w_uq', 'mla_kv_norm', 'mla_w_ukv', 'mla_qk_norm_q', 'mla_qk_norm_k', 'ssm_a_re', 'ssm_a_im', 'ssm_log_dt', 'ssm_b_re', 'ssm_b_im', 'ssm_c_re', 'ssm_c_im', 'ssm_d', 'ssm_w_glu', 'ssm_b_glu', 'out_norm_mla', 'out_norm_ssm', 'w_o', 'xattn_norm', 'mem_norm', 'xattn_w_q', 'xattn_w_kv', 'xattn_q_norm', 'xattn_k_norm', 'xattn_w_o', 'ffn2_norm', 'ffn2_w_gate', 'ffn2_w_up', 'ffn2_w_down']
WEIGHT_NAMES = ARG_NAMES[3:]


def _gather_weights(p, c):
    half = lambda a: lax.dynamic_slice_in_dim(a, c * (a.shape[0] // 2), a.shape[0] // 2, axis=0)
    ffn1 = [half(p[k].astype(BF16)) for k in FFN_NAMES[:3]]
    small = [half(_pack_shards({k: p[k].astype(BF16) for k in SMALL_SHARDED}))]
    ffn2 = [half(p[k].astype(BF16)) for k in FFN_NAMES[3:]]
    me = 4 * lax.axis_index("x") + 2 * lax.axis_index("y") + c
    own = lambda got, blocks: [lax.dynamic_update_index_in_dim(g, b, me, 0) for g, b in zip(got, blocks)]
    as_shards = lambda a: a.reshape(N_CHIPS, 2 * a.shape[1], a.shape[2])
    landing = lambda blocks: [jax.ShapeDtypeStruct((N_DEV,) + b.shape, b.dtype) for b in blocks]
    got1 = own(all_gather_halves("all_gather_weights_a", ffn1), ffn1)
    got1, small = lax.optimization_barrier((got1, small))
    flight_s, sent_s = split_copy_start("gather_small_start", _gather_copies, N_DEV - 1, small, landing(small))
    sent_s, ffn2 = lax.optimization_barrier((sent_s, ffn2))
    flight_2, sent_2 = split_copy_start("gather_ffn2_start", _gather_copies, N_DEV - 1, ffn2, landing(ffn2))
    wb = {k: as_shards(a) for k, a in zip(FFN_NAMES[:3], got1)}

    def small_weights(after):
        mine, got = split_copy_wait("gather_small_wait", flight_s, after)
        return _step_weights(_unpack_shards(own(got, mine)[0].reshape(N_CHIPS, -1, LANES)))

    def ffn2_weights(after):
        mine, got = split_copy_wait("gather_ffn2_wait", flight_2, after)
        return [as_shards(a) for a in own(got, mine)]

    return wb, small_weights, ffn2_weights, sent_s + sent_2


class _GradReduce:
    def __init__(self, c):
        self.c, self.c_arr = c, jnp.reshape(c, (1,)).astype(jnp.int32)
        self.chip = 2 * lax.axis_index("x") + lax.axis_index("y")
        self.flights = []

    def start(self, tag, arrs):
        split = [a.reshape(N_CHIPS, 2, a.shape[1] // 2, a.shape[2]) for a in arrs]
        return self._scatter(tag, split, pair_send_halves(f"grad_pair_send_{tag}", split))

    def send(self, tag, arrs):
        split = [a.reshape(N_CHIPS, 2, a.shape[1] // 2, a.shape[2]) for a in arrs]
        lands = [jax.ShapeDtypeStruct((s.shape[0],) + s.shape[2:], s.dtype) for s in split]
        flight, sent = split_copy_start(f"grad_pair_send_start_{tag}", _pair_copies, 1, split, lands)
        self.sending = (tag, flight)
        return sent

    def scatter(self, tag, after):
        sent_tag, flight = self.sending
        assert sent_tag == tag
        return self._scatter(tag, *split_copy_wait(f"grad_pair_send_wait_{tag}", flight, after))

    def _scatter(self, tag, split, got):
        parts = [pair_add(f"grad_pair_add_{tag}_{k}", s, g, self.c_arr) for k, (s, g) in enumerate(zip(split, got))]
        flight, sent = split_copy_start(f"grad_scatter_start_{tag}", _scatter_copies, N_CHIPS - 1, parts, parts)
        self.flights.append((tag, flight))
        return sent

    def finish(self, after):
        halves = []
        for tag, flight in self.flights:
            parts, landed = split_copy_wait(f"grad_scatter_wait_{tag}", flight, after)
            for k, (q, p) in enumerate(zip(landed, parts)):
                mine = lax.dynamic_index_in_dim(p, self.chip, 0, keepdims=False)
                halves.append(sum_chips(f"grad_sum_{tag}_{k}", lax.dynamic_update_index_in_dim(q, mine, self.chip, 0)))
        tags = "_".join(t for t, _ in self.flights)
        self.flights = []
        theirs = pair_exchange(f"grad_pair_exchange_{tags}", halves)
        return [jnp.where(self.c == 0, jnp.concatenate([h, t], axis=0), jnp.concatenate([t, h], axis=0))
                for h, t in zip(halves, theirs)]


def kernel(x, mem, positions, ffn1_norm, ffn1_w_gate, ffn1_w_up, ffn1_w_down, mix_norm, w_in, mla_q_norm, mla_w_uq, mla_kv_norm, mla_w_ukv, mla_qk_norm_q, mla_qk_norm_k, ssm_a_re, ssm_a_im, ssm_log_dt, ssm_b_re, ssm_b_im, ssm_c_re, ssm_c_im, ssm_d, ssm_w_glu, ssm_b_glu, out_norm_mla, out_norm_ssm, w_o, xattn_norm, mem_norm, xattn_w_q, xattn_w_kv, xattn_q_norm, xattn_k_norm, xattn_w_o, ffn2_norm, ffn2_w_gate, ffn2_w_up, ffn2_w_down, loss_target, m_ffn1_norm, m_ffn1_w_gate, m_ffn1_w_up, m_ffn1_w_down, m_mix_norm, m_w_in, m_mla_q_norm, m_mla_w_uq, m_mla_kv_norm, m_mla_w_ukv, m_mla_qk_norm_q, m_mla_qk_norm_k, m_ssm_a_re, m_ssm_a_im, m_ssm_log_dt, m_ssm_b_re, m_ssm_b_im, m_ssm_c_re, m_ssm_c_im, m_ssm_d, m_ssm_w_glu, m_ssm_b_glu, m_out_norm_mla, m_out_norm_ssm, m_w_o, m_xattn_norm, m_mem_norm, m_xattn_w_q, m_xattn_w_kv, m_xattn_q_norm, m_xattn_k_norm, m_xattn_w_o, m_ffn2_norm, m_ffn2_w_gate, m_ffn2_w_up, m_ffn2_w_down, v_ffn1_norm, v_ffn1_w_gate, v_ffn1_w_up, v_ffn1_w_down, v_mix_norm, v_w_in, v_mla_q_norm, v_mla_w_uq, v_mla_kv_norm, v_mla_w_ukv, v_mla_qk_norm_q, v_mla_qk_norm_k, v_ssm_a_re, v_ssm_a_im, v_ssm_log_dt, v_ssm_b_re, v_ssm_b_im, v_ssm_c_re, v_ssm_c_im, v_ssm_d, v_ssm_w_glu, v_ssm_b_glu, v_out_norm_mla, v_out_norm_ssm, v_w_o, v_xattn_norm, v_mem_norm, v_xattn_w_q, v_xattn_w_kv, v_xattn_q_norm, v_xattn_k_norm, v_xattn_w_o, v_ffn2_norm, v_ffn2_w_gate, v_ffn2_w_up, v_ffn2_w_down):
    args = dict(locals())
    c = lax.axis_index("c")
    view = lambda k, a: jnp.swapaxes(a, 0, 1) if k in TRANSPOSED_VIEW else a
    p = {k: view(k, args[k][0]) for k in WEIGHT_NAMES}
    mom = {k: view(k, args["m_" + k][0]) for k in WEIGHT_NAMES}
    var = {k: view(k, args["v_" + k][0]) for k in WEIGHT_NAMES}
    natural = {k: view(k, p[k]) for k in WEIGHT_NAMES}

    wb, small_weights, ffn2_weights, sent = _gather_weights(
        {k: (p[k] if k in FFN_NAMES else natural[k]) for k in WEIGHT_NAMES}, c)
    w = _step_params(natural)
    w["ffn1_norm"] = w["ffn1_norm"] + sent
    early, late = _GradReduce(c), _GradReduce(c)

    def on_grads(tag, g):
        if tag == "ffn2":
            return early.send(tag, [g[k] for k in FFN_NAMES[3:]])
        packed = _pack_shards(_sharded_grads(g))
        return early.scatter("ffn2", packed[0, :SUBLANES]) + early.start(tag, [packed])

    loss, dx, g = local_step(x[0], mem[0], positions[0], loss_target[0], w, wb, small_weights, ffn2_weights, on_grads)

    sent = late.start("ffn1", [g[k] for k in FFN_NAMES[:3]])
    shards = early.finish(dx[:SUBLANES, :LANES] + sent)
    grad = dict(zip(FFN_NAMES[3:], shards[:3]))
    small_sharded = _unpack_shards(shards[3])
    grad.update({k: view(k, small_sharded[k]) for k in SMALL_SHARDED})
    repl = _problem_repl_grads(g)
    grad.update(_unpack_repl(all_reduce_small("grad_all_reduce_small", _pack_repl(repl, loss))))
    loss = grad.pop("loss")

    delta, new_m, new_v = {}, {}, {}
    small = [k for k in WEIGHT_NAMES if k not in FFN_NAMES and k not in SMALL_SHARDED]
    as2d = lambda a: a.reshape(-1, a.shape[-1])

    def update(k):
        delta[k], new_m[k], new_v[k] = adamw_big("adamw_" + k, as2d(p[k]), as2d(grad[k]), as2d(mom[k]), as2d(var[k]))

    for k in WEIGHT_NAMES:
        if k not in small and k not in FFN_NAMES[:3]:
            update(k)
    ds, nms, nvs = adamw_small("adamw_small", [as2d(p[k]) for k in small], [as2d(grad[k].reshape(p[k].shape)) for k in small],
                               [as2d(mom[k]) for k in small], [as2d(var[k]) for k in small])
    for k, d, nm, nv in zip(small, ds, nms, nvs):
        delta[k], new_m[k], new_v[k] = d, nm, nv

    grad.update(zip(FFN_NAMES[:3], late.finish(delta[FFN_NAMES[-1]])))
    for k in FFN_NAMES[:3]:
        update(k)

    shaped = lambda d, k: view(k, d.reshape(p[k].shape)).reshape(args[k].shape)
    return (loss, dx[None], *[shaped(grad[k], k) for k in WEIGHT_NAMES], *[shaped(delta[k], k) for k in WEIGHT_NAMES],
            *[shaped(new_m[k], k) for k in WEIGHT_NAMES], *[shaped(new_v[k], k) for k in WEIGHT_NAMES])
```

```python
import functools
import math

import jax
import jax.numpy as jnp
import numpy as np
from jax import lax
from jax.experimental import pallas as pl
from jax.experimental.pallas import tpu as pltpu

F32, BF16 = jnp.float32, jnp.bfloat16
EPS = 1e-6
MESH = pl.DeviceIdType.MESH

D_MODEL, D_FF = 1024, 2752
MLA_HEADS, MLA_Q_RANK, MLA_KV_RANK, MLA_NOPE, MLA_ROPE, MLA_V = 4, 384, 256, 128, 64, 128
MLA_QK = MLA_NOPE + MLA_ROPE
HEAD_PAD = 256
SSM_WIDTH, SSM_GROUP, SSM_GROUPS, SSM_STATE = 512, 16, 32, 64
SSM_LANES = SSM_GROUPS * SSM_STATE
XH, XHD = 4, 128
ROPE_THETA = 10000.0
ADAM_LR, ADAM_B1, ADAM_B2, ADAM_EPS, ADAM_WD, ADAM_STEP = 0.001, 0.9, 0.999, 1e-08, 0.01, 10
N_CHIPS, N_CORES, N_DEV = 4, 2, 8

VMEM_LIMIT_BYTES = 56 * 2**20
TOKEN_TILE = 512
FUSED_TILE = 256
FUSED_ROW_BLOCK = 128
FFN_ROW_BLOCK = 256
FFN_WIDE_TILE = 1024
ATTN_TILE = 512
STAT_LANES = 128
SCAN_TIME_TILE = 1024
SCAN_LANE_TILE = 512
SUBLANES = 8


def _params(n_axes):
    return pltpu.CompilerParams(dimension_semantics=("arbitrary",) * n_axes, vmem_limit_bytes=VMEM_LIMIT_BYTES)


def _first(axes):
    cond = None
    for a in axes:
        c = pl.program_id(a) == 0
        cond = c if cond is None else jnp.logical_and(cond, c)
    return cond


def mm(name, xs, ws, out_dtypes, *, trans=False, adds=None, tm=TOKEN_TILE):
    rows = xs[0].shape[0]
    tm = min(tm, rows)
    n_in, n_out = len(xs), len(out_dtypes)
    pairs = [(i, j) for i in range(n_in) for j in range(n_out) if ws[i][j] is not None]
    w_list = [ws[i][j] for (i, j) in pairs]
    adds = list(adds) if adds is not None else [None] * n_out
    add_list = [a for a in adds if a is not None]
    out_cols = [None] * n_out
    for (i, j), w in zip(pairs, w_list):
        out_cols[j] = w.shape[0] if trans else w.shape[1]
    contract = (((1,), (1 if trans else 0,)), ((), ()))

    def body(*refs):
        x_refs = refs[:n_in]
        w_refs = refs[n_in:n_in + len(pairs)]
        a_refs = list(refs[n_in + len(pairs):n_in + len(pairs) + len(add_list)])
        o_refs = refs[n_in + len(pairs) + len(add_list):]
        xb = [None] * n_in
        for j in range(n_out):
            acc = None
            for p, (i, jj) in enumerate(pairs):
                if jj != j:
                    continue
                if xb[i] is None:
                    xb[i] = x_refs[i][...].astype(BF16)
                d = lax.dot_general(xb[i], w_refs[p][...].astype(BF16), contract, preferred_element_type=F32)
                acc = d if acc is None else acc + d
            if adds[j] is not None:
                acc = acc + a_refs.pop(0)[...].astype(F32)
            o_refs[j][...] = acc.astype(o_refs[j].dtype)

    in_specs = ([pl.BlockSpec((tm, x.shape[1]), lambda i: (i, 0)) for x in xs]
                + [pl.BlockSpec(w.shape, lambda i: (0, 0)) for w in w_list]
                + [pl.BlockSpec((tm, a.shape[1]), lambda i: (i, 0)) for a in add_list])
    outs = pl.pallas_call(
        body, name=name, grid=(rows // tm,), in_specs=in_specs,
        out_specs=[pl.BlockSpec((tm, n), lambda i: (i, 0)) for n in out_cols],
        out_shape=[jax.ShapeDtypeStruct((rows, n), dt) for n, dt in zip(out_cols, out_dtypes)],
        compiler_params=_params(1),
    )(*xs, *w_list, *add_list)
    return list(outs)


def mm_tn(name, xs, dys, pairs, *, tm=TOKEN_TILE):
    rows = xs[0].shape[0]
    tm = min(tm, rows)
    n_x, n_dy = len(xs), len(dys)
    contract = (((0,), (0,)), ((), ()))

    def body(*refs):
        x_refs, dy_refs, o_refs = refs[:n_x], refs[n_x:n_x + n_dy], refs[n_x + n_dy:]
        @pl.when(pl.program_id(0) == 0)
        def _():
            for o in o_refs:
                o[...] = jnp.zeros_like(o)

        for k, (i, js) in enumerate(pairs):
            dy = None
            for j in js:
                t = dy_refs[j][...].astype(F32)
                dy = t if dy is None else dy + t
            o_refs[k][...] += lax.dot_general(x_refs[i][...].astype(BF16), dy.astype(BF16), contract,
                                              preferred_element_type=F32)

    shapes = [(xs[i].shape[1], dys[js[0]].shape[1]) for (i, js) in pairs]
    outs = pl.pallas_call(
        body, name=name, grid=(rows // tm,),
        in_specs=[pl.BlockSpec((tm, a.shape[1]), lambda i: (i, 0)) for a in (*xs, *dys)],
        out_specs=[pl.BlockSpec(s, lambda i: (0, 0)) for s in shapes],
        out_shape=[jax.ShapeDtypeStruct(s, F32) for s in shapes],
        compiler_params=_params(1),
    )(*xs, *dys)
    return list(outs)


class In:
    def __init__(self, arr, block=None, imap=None, *, diff=False, acc=None, grad=None, weight=False, rows=False):
        self.arr, self.block, self.imap, self.diff, self.acc, self.grad = arr, block, imap, diff, acc, grad
        self.weight, self.rows = weight, rows

    def spec(self):
        return pl.BlockSpec(memory_space=pltpu.VMEM) if self.weight else pl.BlockSpec(self.block, self.imap)


class Out:
    def __init__(self, shape, dtype, block, imap):
        self.shape, self.dtype, self.block, self.imap = shape, dtype, block, imap

    def spec(self):
        return pl.BlockSpec(self.block, self.imap)


class Wt:
    def __init__(self, ref, zeros=None):
        self.ref, self.zeros = ref, zeros


@jax.custom_vjp
def _wdot(a, w, z):
    return jnp.dot(a.astype(BF16), w, preferred_element_type=F32)


def _wdot_fwd(a, w, z):
    return _wdot(a, w, z), (a, w)


def _wdot_bwd(res, g):
    a, w = res
    gb = g.astype(BF16)
    da = lax.dot_general(gb, w, (((1,), (1,)), ((), ())), preferred_element_type=F32)
    dz = lax.dot_general(a.astype(BF16), gb, (((0,), (0,)), ((), ())), preferred_element_type=F32)
    return da, None, dz


_wdot.defvjp(_wdot_fwd, _wdot_bwd)


def wdot(a, wt, head=None):
    w = wt.ref[...] if head is None else wt.ref[head]
    if wt.zeros is None:
        return jnp.dot(a.astype(BF16), w, preferred_element_type=F32)
    return _wdot(a, w, wt.zeros[0 if head is None else head])


def seg_fwd(name, f, grid, ins, outs):
    n_in = len(ins)

    def body(*refs):
        res = f(*[Wt(r) if i.weight else r[...] for i, r in zip(ins, refs[:n_in])])
        for o_ref, r in zip(refs[n_in:], res):
            o_ref[...] = r.astype(o_ref.dtype)

    res = pl.pallas_call(
        body, name=name, grid=grid, in_specs=[i.spec() for i in ins], out_specs=[o.spec() for o in outs],
        out_shape=[jax.ShapeDtypeStruct(o.shape, o.dtype) for o in outs], compiler_params=_params(len(grid)),
    )(*[i.arr for i in ins])
    return list(res)


def seg_bwd(name, f, grid, ins, outs, cts, adds=None, row_block=None):
    n_in, n_ct = len(ins), len(cts)
    grad_idx = [k for k, i in enumerate(ins) if i.diff or i.weight]
    adds = adds or {}
    add_keys = sorted(adds)
    add_list = [adds[k] for k in add_keys]
    heads = {k: (ins[k].arr.shape[0] if ins[k].arr.ndim == 3 else 1) for k in grad_idx if ins[k].weight}
    tile_rows = outs[0].block[0]
    blocks = [None] if row_block is None else [pl.ds(r0, row_block) for r0 in range(0, tile_rows, row_block)]

    def body(*refs):
        in_refs, ct_refs = refs[:n_in], refs[n_in:n_in + n_ct]
        add_refs = dict(zip(add_keys, refs[n_in + n_ct:n_in + n_ct + len(add_list)]))
        g_refs = dict(zip(grad_idx, refs[n_in + n_ct + len(add_list):]))
        for k in grad_idx:
            if ins[k].weight or ins[k].acc is not None:
                @pl.when(_first(range(len(grid)) if ins[k].weight else ins[k].acc))
                def _(k=k):
                    g_refs[k][...] = jnp.zeros_like(g_refs[k])

        for rows in blocks:
            at = lambda ref, sliced: ref[...] if rows is None or not sliced else ref[rows, :]
            vals = [None if i.weight else at(r, i.rows) for i, r in zip(ins, in_refs)]
            primals, owner = [], []
            for k in grad_idx:
                if ins[k].weight:
                    for h in range(heads[k]):
                        primals.append(jnp.zeros(ins[k].arr.shape[-2:], F32))
                        owner.append((k, h))
                else:
                    primals.append(vals[k].astype(F32))
                    owner.append((k, None))

            def g(*dv, vals=vals, owner=owner):
                full = list(vals)
                zeros = {}
                for (k, h), v in zip(owner, dv):
                    if h is None:
                        full[k] = v
                    else:
                        zeros.setdefault(k, []).append(v)
                for k, z in zeros.items():
                    full[k] = Wt(in_refs[k], z)
                return tuple(f(*full))

            _, pull = jax.vjp(g, *primals)
            grads = pull(tuple(at(c, True).astype(F32) for c in ct_refs))
            for (k, h), gr in zip(owner, grads):
                if ins[k].weight:
                    if ins[k].arr.ndim == 3:
                        g_refs[k][h] += gr
                    else:
                        g_refs[k][...] += gr
                    continue
                if k in add_refs:
                    gr = gr + at(add_refs[k], True).astype(F32)
                if ins[k].acc is not None:
                    g_refs[k][...] += gr
                elif rows is None or not ins[k].rows:
                    g_refs[k][...] = gr.astype(g_refs[k].dtype)
                else:
                    g_refs[k][rows, :] = gr.astype(g_refs[k].dtype)

    g_specs, g_shapes = [], []
    for k in grad_idx:
        i = ins[k]
        if i.weight:
            g_specs.append(pl.BlockSpec(memory_space=pltpu.VMEM))
            g_shapes.append(jax.ShapeDtypeStruct(i.arr.shape, F32))
            continue
        shape, block, imap = i.grad if i.grad is not None else (i.arr.shape, i.block, i.imap)
        g_specs.append(pl.BlockSpec(block, imap))
        g_shapes.append(jax.ShapeDtypeStruct(shape, F32))
    in_specs = ([i.spec() for i in ins] + [o.spec() for o in outs]
                + [pl.BlockSpec(ins[k].block, ins[k].imap) for k in add_keys])
    res = pl.pallas_call(
        body, name=name, grid=grid, in_specs=in_specs, out_specs=g_specs, out_shape=g_shapes,
        compiler_params=_params(len(grid)),
    )(*[i.arr for i in ins], *cts, *add_list)
    return list(res)


def _rms(x, g):
    return x * lax.rsqrt(jnp.mean(x * x, axis=-1, keepdims=True) + EPS) * g


@jax.custom_vjp
def _bdot_nt(a, b):
    return lax.dot_general(a.astype(BF16), b.astype(BF16), (((1,), (1,)), ((), ())), preferred_element_type=F32)


def _bdot_nt_fwd(a, b):
    return _bdot_nt(a, b), (a, b)


def _bdot_nt_bwd(res, g):
    a, b = res
    gb = g.astype(BF16)
    da = lax.dot_general(gb, b.astype(BF16), (((1,), (0,)), ((), ())), preferred_element_type=F32)
    db = lax.dot_general(gb, a.astype(BF16), (((0,), (0,)), ((), ())), preferred_element_type=F32)
    return da, db


_bdot_nt.defvjp(_bdot_nt_fwd, _bdot_nt_bwd)


@jax.custom_vjp
def _bdot_nn(a, b):
    return lax.dot_general(a.astype(BF16), b.astype(BF16), (((1,), (0,)), ((), ())), preferred_element_type=F32)


def _bdot_nn_fwd(a, b):
    return _bdot_nn(a, b), (a, b)


def _bdot_nn_bwd(res, g):
    a, b = res
    gb = g.astype(BF16)
    da = lax.dot_general(gb, b.astype(BF16), (((1,), (1,)), ((), ())), preferred_element_type=F32)
    db = lax.dot_general(a.astype(BF16), gb, (((0,), (0,)), ((), ())), preferred_element_type=F32)
    return da, db


_bdot_nn.defvjp(_bdot_nn_fwd, _bdot_nn_bwd)


@functools.partial(jax.custom_vjp, nondiff_argnums=(1,))
def _lane_roll(x, shift):
    return pltpu.roll(x, shift, 1)


def _lane_roll_fwd(x, shift):
    return pltpu.roll(x, shift, 1), None


def _lane_roll_bwd(shift, _, g):
    return (pltpu.roll(g, (g.shape[1] - shift) % g.shape[1], 1),)


_lane_roll.defvjp(_lane_roll_fwd, _lane_roll_bwd)


def _hp_dot(a, b):
    return jnp.dot(a, b, precision=lax.Precision.HIGHEST, preferred_element_type=F32)


NT_DIMS = (((1,), (1,)), ((), ()))
TN_DIMS = (((0,), (0,)), ((), ()))


def ffn_fwd(name, x, gain, wg, wu, wd, target=None, tm=FFN_WIDE_TILE):
    S, D = x.shape
    tm = min(tm, S)
    nsh, Fs, _ = wg.shape
    with_loss = target is not None

    def body(*refs):
        if with_loss:
            x_ref, gain_ref, wg_ref, wu_ref, wd_ref, t_ref, xo_ref, h_ref, g_ref, u_ref, part_ref, acc = refs
        else:
            x_ref, gain_ref, wg_ref, wu_ref, wd_ref, xo_ref, h_ref, g_ref, u_ref, acc = refs
        j = pl.program_id(1)

        @pl.when(j == 0)
        def _():
            h_ref[...] = _rms(x_ref[...], gain_ref[...]).astype(BF16)
            acc[...] = jnp.zeros_like(acc)

        h = h_ref[...]
        g = lax.dot_general(h, wg_ref[...], NT_DIMS, preferred_element_type=F32)
        u = lax.dot_general(h, wu_ref[...], NT_DIMS, preferred_element_type=F32)
        g_ref[...] = g.astype(BF16)
        u_ref[...] = u.astype(BF16)
        a = g * jax.nn.sigmoid(g) * u
        acc[...] += jnp.dot(a.astype(BF16), wd_ref[...], preferred_element_type=F32)

        @pl.when(j == nsh - 1)
        def _():
            y = x_ref[...] + 0.5 * acc[...]
            if with_loss:
                err = y - t_ref[...]
                xo_ref[...] = err * (1.0 / D)
                part_ref[...] = jnp.full(part_ref.shape, 0.5 * jnp.sum(jnp.mean(err * err, axis=-1)), F32)
            else:
                xo_ref[...] = y

    rows = pl.BlockSpec((tm, D), lambda i, j: (i, 0))
    wspec = pl.BlockSpec((None, Fs, D), lambda i, j: (j, 0, 0))
    act = pl.BlockSpec((None, tm, Fs), lambda i, j: (j, i, 0))
    in_specs, args = [rows, pl.BlockSpec((1, D), lambda i, j: (0, 0)), wspec, wspec, wspec], [x, gain, wg, wu, wd]
    out_specs = [rows, rows, act, act]
    out_shape = [jax.ShapeDtypeStruct((S, D), F32), jax.ShapeDtypeStruct((S, D), BF16),
                 jax.ShapeDtypeStruct((nsh, S, Fs), BF16), jax.ShapeDtypeStruct((nsh, S, Fs), BF16)]
    if with_loss:
        in_specs.append(rows)
        args.append(target)
        out_specs.append(pl.BlockSpec((SUBLANES, 128), lambda i, j: (i, 0)))
        out_shape.append(jax.ShapeDtypeStruct((S // tm * SUBLANES, 128), F32))
    return pl.pallas_call(
        body, name=name, grid=(S // tm, nsh), in_specs=in_specs, out_specs=out_specs, out_shape=out_shape,
        scratch_shapes=[pltpu.VMEM((tm, D), F32)], compiler_params=_params(2),
    )(*args)


def ffn_bwd_act(name, dxo, x, gain, g, u, wg, wu, wd, tm=TOKEN_TILE):
    S, D = x.shape
    tm = min(tm, S)
    nsh, Fs, _ = wg.shape

    def body(dxo_ref, x_ref, gain_ref, g_ref, u_ref, wg_ref, wu_ref, wd_ref, dx_ref, dgain_ref, dg_ref, du_ref, dd, dh):
        i, j = pl.program_id(0), pl.program_id(1)

        @pl.when(j == 0)
        def _():
            dd[...] = (0.5 * dxo_ref[...]).astype(BF16)
            dh[...] = jnp.zeros_like(dh)

        for r0 in range(0, tm, FFN_ROW_BLOCK):
            rows = pl.ds(r0, FFN_ROW_BLOCK)
            da = lax.dot_general(dd[rows, :], wd_ref[...], NT_DIMS, preferred_element_type=F32)
            gf, uf = g_ref[rows, :].astype(F32), u_ref[rows, :].astype(F32)
            sig = jax.nn.sigmoid(gf)
            dgv = (da * uf * (sig * (1.0 + gf * (1.0 - sig)))).astype(BF16)
            duv = (da * (gf * sig)).astype(BF16)
            dg_ref[rows, :] = dgv
            du_ref[rows, :] = duv
            dh[rows, :] += (jnp.dot(dgv, wg_ref[...], preferred_element_type=F32)
                            + jnp.dot(duv, wu_ref[...], preferred_element_type=F32))

        @pl.when(j == nsh - 1)
        def _():
            xv = x_ref[...]
            r = lax.rsqrt(jnp.mean(xv * xv, axis=-1, keepdims=True) + EPS)
            xhat = xv * r
            dhv = dh[...]
            dxn = dhv * gain_ref[...]
            dx_ref[...] = dxo_ref[...] + r * (dxn - xhat * jnp.mean(dxn * xhat, axis=-1, keepdims=True))
            part = jnp.sum(dhv * xhat, axis=0, keepdims=True)

            @pl.when(i == 0)
            def _():
                dgain_ref[...] = part

            @pl.when(i != 0)
            def _():
                dgain_ref[...] += part

    return pl.pallas_call(
        body, name=name, grid=(S // tm, nsh),
        in_specs=[pl.BlockSpec((tm, D), lambda i, j: (i, 0)), pl.BlockSpec((tm, D), lambda i, j: (i, 0)),
                  pl.BlockSpec((1, D), lambda i, j: (0, 0)),
                  pl.BlockSpec((None, tm, Fs), lambda i, j: (j, i, 0)), pl.BlockSpec((None, tm, Fs), lambda i, j: (j, i, 0)),
                  pl.BlockSpec((None, Fs, D), lambda i, j: (j, 0, 0)), pl.BlockSpec((None, Fs, D), lambda i, j: (j, 0, 0)),
                  pl.BlockSpec((None, Fs, D), lambda i, j: (j, 0, 0))],
        out_specs=[pl.BlockSpec((tm, D), lambda i, j: (i, 0)), pl.BlockSpec((1, D), lambda i, j: (0, 0)),
                   pl.BlockSpec((None, tm, Fs), lambda i, j: (j, i, 0)), pl.BlockSpec((None, tm, Fs), lambda i, j: (j, i, 0))],
        out_shape=[jax.ShapeDtypeStruct((S, D), F32), jax.ShapeDtypeStruct((1, D), F32),
                   jax.ShapeDtypeStruct((nsh, S, Fs), BF16), jax.ShapeDtypeStruct((nsh, S, Fs), BF16)],
        scratch_shapes=[pltpu.VMEM((tm, D), BF16), pltpu.VMEM((tm, D), F32)], compiler_params=_params(2),
    )(dxo, x, gain, g, u, wg, wu, wd)


def ffn_bwd_w(name, h, dxo, g, u, dg, du, tm=FFN_WIDE_TILE):
    S, D = h.shape
    tm = min(tm, S)
    nsh, _, Fs = g.shape

    def body(h_ref, dxo_ref, g_ref, u_ref, dg_ref, du_ref, dwg_ref, dwu_ref, dwd_ref):
        i = pl.program_id(1)
        gf, uf = g_ref[...].astype(F32), u_ref[...].astype(F32)
        a = (gf * jax.nn.sigmoid(gf) * uf).astype(BF16)
        dd = (0.5 * dxo_ref[...]).astype(BF16)
        hv = h_ref[...]

        @pl.when(i == 0)
        def _():
            dwg_ref[...] = jnp.zeros_like(dwg_ref)
            dwu_ref[...] = jnp.zeros_like(dwu_ref)
            dwd_ref[...] = jnp.zeros_like(dwd_ref)

        dwg_ref[...] += lax.dot_general(dg_ref[...], hv, TN_DIMS, preferred_element_type=F32)
        dwu_ref[...] += lax.dot_general(du_ref[...], hv, TN_DIMS, preferred_element_type=F32)
        dwd_ref[...] += lax.dot_general(a, dd, TN_DIMS, preferred_element_type=F32)

    act = pl.BlockSpec((None, tm, Fs), lambda j, i: (j, i, 0))
    wspec = pl.BlockSpec((None, Fs, D), lambda j, i: (j, 0, 0))
    return pl.pallas_call(
        body, name=name, grid=(nsh, S // tm),
        in_specs=[pl.BlockSpec((tm, D), lambda j, i: (i, 0)), pl.BlockSpec((tm, D), lambda j, i: (i, 0)), act, act, act, act],
        out_specs=[wspec, wspec, wspec], out_shape=[jax.ShapeDtypeStruct((nsh, Fs, D), F32)] * 3,
        compiler_params=_params(2),
    )(h, dxo, g, u, dg, du)


NEG_BIG = -1e30


def _causal_pairs(n, by_key):
    pairs = [(qi, ki) for qi in range(n) for ki in range(qi + 1)]
    if by_key:
        pairs.sort(key=lambda p: (p[1], p[0]))
    return jnp.asarray([p[0] for p in pairs], jnp.int32), jnp.asarray([p[1] for p in pairs], jnp.int32)


def _scores(q, k, masked):
    s = lax.dot_general(q, k, (((1,), (1,)), ((), ())), preferred_element_type=F32)
    if masked:
        row = lax.broadcasted_iota(jnp.int32, s.shape, 0)
        col = lax.broadcasted_iota(jnp.int32, s.shape, 1)
        s = jnp.where(row >= col, s, NEG_BIG)
    return s


def attn_fwd(name, q, k, v, t=ATTN_TILE):
    S, Dk = q.shape[0], HEAD_PAD
    H = q.shape[1] // Dk
    Dv = v.shape[1] // H
    qt, kt = _causal_pairs(S // t, by_key=False)

    def body(qt_ref, kt_ref, q_ref, k_ref, v_ref, o_ref, lse_ref, m_sc, l_sc, acc):
        qi, ki = qt_ref[pl.program_id(1)], kt_ref[pl.program_id(1)]

        @pl.when(ki == 0)
        def _():
            m_sc[...] = jnp.full_like(m_sc, NEG_BIG)
            l_sc[...] = jnp.zeros_like(l_sc)
            acc[...] = jnp.zeros_like(acc)

        def step(masked):
            s = _scores(q_ref[...], k_ref[...], masked)
            m_prev = m_sc[...]
            m_next = jnp.maximum(m_prev, jnp.max(s, axis=-1, keepdims=True))
            alpha = jnp.exp(m_prev - m_next)
            p = jnp.exp(s - jnp.tile(m_next, (1, t // STAT_LANES)))
            l_sc[...] = alpha * l_sc[...] + jnp.sum(p, axis=-1, keepdims=True)
            acc[...] = alpha * acc[...] + jnp.dot(p.astype(BF16), v_ref[...].astype(BF16), preferred_element_type=F32)
            m_sc[...] = m_next

        @pl.when(ki < qi)
        def _():
            step(False)

        @pl.when(ki == qi)
        def _():
            step(True)
            o_ref[...] = acc[...] / l_sc[...]
            lse_ref[...] = m_sc[...] + jnp.log(l_sc[...])

    stat = pltpu.VMEM((t, STAT_LANES), F32)
    return pl.pallas_call(
        body, name=name,
        grid_spec=pltpu.PrefetchScalarGridSpec(
            num_scalar_prefetch=2, grid=(H, qt.shape[0]),
            in_specs=[pl.BlockSpec((t, Dk), lambda h, s, qt, kt: (qt[s], h)),
                      pl.BlockSpec((t, Dk), lambda h, s, qt, kt: (kt[s], h)),
                      pl.BlockSpec((t, Dv), lambda h, s, qt, kt: (kt[s], h))],
            out_specs=[pl.BlockSpec((t, Dv), lambda h, s, qt, kt: (qt[s], h)),
                       pl.BlockSpec((None, t, STAT_LANES), lambda h, s, qt, kt: (h, qt[s], 0))],
            scratch_shapes=[stat, stat, pltpu.VMEM((t, Dv), F32)]),
        out_shape=[jax.ShapeDtypeStruct((S, H * Dv), F32), jax.ShapeDtypeStruct((H, S, STAT_LANES), F32)],
        compiler_params=_params(2),
    )(qt, kt, q, k, v)


def attn_bwd(name, q, k, v, do, o, lse, t=ATTN_TILE):
    S, Dk = q.shape[0], HEAD_PAD
    H = q.shape[1] // Dk
    Dv = v.shape[1] // H
    qt, kt = _causal_pairs(S // t, by_key=True)
    tn_dims = (((0,), (0,)), ((), ()))

    def body(qt_ref, kt_ref, q_ref, k_ref, v_ref, do_ref, o_ref, lse_ref, dq_ref, dk_ref, dv_ref):
        step_id = pl.program_id(1)
        qi, ki = qt_ref[step_id], kt_ref[step_id]

        @pl.when(step_id == 0)
        def _():
            dq_ref[...] = jnp.zeros_like(dq_ref)

        def step(masked):
            s = _scores(q_ref[...], k_ref[...], masked)
            reps = (1, t // STAT_LANES)
            p = jnp.exp(s - jnp.tile(lse_ref[...], reps))
            dov = do_ref[...]
            delta = jnp.broadcast_to(jnp.sum(dov * o_ref[...], axis=-1, keepdims=True), (t, STAT_LANES))
            dob = dov.astype(BF16)
            dp = lax.dot_general(dob, v_ref[...].astype(BF16), (((1,), (1,)), ((), ())), preferred_element_type=F32)
            ds = (p * (dp - jnp.tile(delta, reps))).astype(BF16)
            pdv = lax.dot_general(p.astype(BF16), dob, tn_dims, preferred_element_type=F32)
            pdk = lax.dot_general(ds, q_ref[...], tn_dims, preferred_element_type=F32)
            rows = pl.ds(pl.multiple_of(qi * t, t), t)
            dq_ref[rows, :] += jnp.dot(ds, k_ref[...], preferred_element_type=F32)
            return pdk, pdv

        @pl.when(ki == qi)
        def _():
            dk_ref[...] = jnp.zeros_like(dk_ref)
            dv_ref[...] = jnp.zeros_like(dv_ref)

        def accumulate(masked):
            pdk, pdv = step(masked)
            dk_ref[...] += pdk
            dv_ref[...] += pdv

        @pl.when(ki == qi)
        def _():
            accumulate(True)

        @pl.when(ki < qi)
        def _():
            accumulate(False)

    qrow = lambda h, s, qt, kt: (qt[s], h)
    krow = lambda h, s, qt, kt: (kt[s], h)
    return pl.pallas_call(
        body, name=name,
        grid_spec=pltpu.PrefetchScalarGridSpec(
            num_scalar_prefetch=2, grid=(H, qt.shape[0]),
            in_specs=[pl.BlockSpec((t, Dk), qrow), pl.BlockSpec((t, Dk), krow), pl.BlockSpec((t, Dv), krow),
                      pl.BlockSpec((t, Dv), qrow), pl.BlockSpec((t, Dv), qrow),
                      pl.BlockSpec((None, t, STAT_LANES), lambda h, s, qt, kt: (h, qt[s], 0))],
            out_specs=[pl.BlockSpec((S, Dk), lambda h, s, qt, kt: (0, h)), pl.BlockSpec((t, Dk), krow),
                       pl.BlockSpec((t, Dv), krow)]),
        out_shape=[jax.ShapeDtypeStruct((S, H * Dk), F32), jax.ShapeDtypeStruct((S, H * Dk), F32),
                   jax.ShapeDtypeStruct((S, H * Dv), F32)],
        compiler_params=_params(2),
    )(qt, kt, q, k, v, do, o, lse)


def _cmul(ar, ai, br, bi):
    return ar * br - ai * bi, ar * bi + ai * br


def _scan_tile(x_r, x_i, ar_ref, ai_ref, cr_sc, ci_sc, *, reverse, first, states=None):
    tc, lanes = x_r.shape
    nblk, lb = tc // SUBLANES, SCAN_LANE_TILE
    with_da = states is not None
    if with_da:
        xr_all, xi_all, pr_all, pi_all, dar_all, dai_all, chunk = states

    @pl.when(first)
    def _():
        cr_sc[...] = jnp.zeros_like(cr_sc)
        ci_sc[...] = jnp.zeros_like(ci_sc)
        if with_da:
            dar_all[...] = jnp.zeros_like(dar_all)
            dai_all[...] = jnp.zeros_like(dai_all)

    row = lax.broadcasted_iota(jnp.int32, (SUBLANES, lb), 0)
    for l0 in range(0, lanes, lb):
        _scan_lanes(x_r.at[:, pl.ds(l0, lb)], x_i.at[:, pl.ds(l0, lb)], ar_ref[0:1, pl.ds(l0, lb)],
                    ai_ref[0:1, pl.ds(l0, lb)], cr_sc.at[:, pl.ds(l0, lb)], ci_sc.at[:, pl.ds(l0, lb)], row, reverse,
                    nblk, None if not with_da else tuple(r.at[:, pl.ds(l0, lb)] for r in states[:6]) + (chunk,))


def _scan_lanes(x_r, x_i, a1r, a1i, cr_sc, ci_sc, row, reverse, nblk, states):
    lb = x_r.shape[1]
    with_da = states is not None
    if with_da:
        xr_ref, xi_ref, pr_ref, pi_ref, dar_ref, dai_ref, chunk = states
    if reverse:
        a1i = -a1i
    a2r, a2i = _cmul(a1r, a1i, a1r, a1i)
    a4r, a4i = _cmul(a2r, a2i, a2r, a2i)
    pw_r, pw_i = jnp.zeros((SUBLANES, lb), F32), jnp.zeros((SUBLANES, lb), F32)
    cur_r, cur_i = a1r, a1i
    for e in range(SUBLANES):
        r_at = (SUBLANES - 1 - e) if reverse else e
        pw_r = jnp.where(row == r_at, cur_r, pw_r)
        pw_i = jnp.where(row == r_at, cur_i, pw_i)
        cur_r, cur_i = _cmul(cur_r, cur_i, a1r, a1i)
    steps = []
    for d, pr, pi in ((1, a1r, a1i), (2, a2r, a2i), (4, a4r, a4i)):
        keep = (row < SUBLANES - d) if reverse else (row >= d)
        steps.append((d, jnp.where(keep, pr, 0.0), jnp.where(keep, pi, 0.0)))

    def block(jb, carry):
        if with_da:
            cr, ci, acc_r, acc_i = carry
        else:
            cr, ci = carry
        idx = (nblk - 1 - jb) if reverse else jb
        r0 = pl.multiple_of(idx * SUBLANES, SUBLANES)
        xr = x_r[pl.ds(r0, SUBLANES), :]
        xi = x_i[pl.ds(r0, SUBLANES), :]
        for d, pr, pi in steps:
            shift = SUBLANES - d if reverse else d
            sr, si = pltpu.roll(xr, shift, 0), pltpu.roll(xi, shift, 0)
            xr, xi = xr + pr * sr - pi * si, xi + pr * si + pi * sr
        xr, xi = xr + pw_r * cr - pw_i * ci, xi + pw_r * ci + pw_i * cr
        x_r[pl.ds(r0, SUBLANES), :] = xr
        x_i[pl.ds(r0, SUBLANES), :] = xi
        edge = 0 if reverse else SUBLANES - 1
        cr, ci = xr[edge:edge + 1, :], xi[edge:edge + 1, :]
        if not with_da:
            return cr, ci
        fr = xr_ref[pl.ds(r0, SUBLANES), :]
        fi = xi_ref[pl.ds(r0, SUBLANES), :]
        rp = pl.multiple_of(jnp.maximum(idx - 1, 0) * SUBLANES, SUBLANES)
        inside = idx > 0
        before_r = jnp.where(inside, xr_ref[pl.ds(rp, SUBLANES), :], pr_ref[...])
        before_i = jnp.where(inside, xi_ref[pl.ds(rp, SUBLANES), :], pi_ref[...])
        live = jnp.where(jnp.logical_or(inside, chunk > 0), 1.0, 0.0)
        last_r = before_r[SUBLANES - 1:SUBLANES, :] * live
        last_i = before_i[SUBLANES - 1:SUBLANES, :] * live
        pvr = jnp.where(row == 0, last_r, pltpu.roll(fr, 1, 0))
        pvi = jnp.where(row == 0, last_i, pltpu.roll(fi, 1, 0))
        acc_r = acc_r + xr * pvr + xi * pvi
        acc_i = acc_i + xi * pvr - xr * pvi
        return cr, ci, acc_r, acc_i

    init = (cr_sc[...], ci_sc[...])
    if with_da:
        init = init + (jnp.zeros((SUBLANES, lb), F32), jnp.zeros((SUBLANES, lb), F32))
    fin = lax.fori_loop(0, nblk, block, init)
    cr_sc[...] = fin[0]
    ci_sc[...] = fin[1]
    if with_da:
        dar_ref[...] += fin[2]
        dai_ref[...] += fin[3]


SSM_BLOCKS = 4
BLOCK_CH = SSM_WIDTH // SSM_BLOCKS
PREP_LANES = SSM_LANES // SSM_BLOCKS


def ssm_fwd(name, u, bb_r, bb_i, cb_r, ncb_i, a_r8, a_i8):
    S = u.shape[0]
    tc = min(SCAN_TIME_TILE, S)

    def body(u_ref, bbr_ref, bbi_ref, cbr_ref, ncbi_ref, ar_ref, ai_ref, xr_ref, xi_ref, y_ref, cr_sc, ci_sc):
        ub = u_ref[...].astype(BF16)
        xr_ref[...] = jnp.dot(ub, bbr_ref[...], preferred_element_type=F32)
        xi_ref[...] = jnp.dot(ub, bbi_ref[...], preferred_element_type=F32)
        _scan_tile(xr_ref, xi_ref, ar_ref, ai_ref, cr_sc, ci_sc, reverse=False, first=pl.program_id(1) == 0)
        y_ref[...] = (lax.dot_general(xr_ref[...].astype(BF16), cbr_ref[...], NT_DIMS, preferred_element_type=F32)
                      + lax.dot_general(xi_ref[...].astype(BF16), ncbi_ref[...], NT_DIMS, preferred_element_type=F32))

    ch = pl.BlockSpec((tc, BLOCK_CH), lambda b, t: (t, b))
    st = pl.BlockSpec((tc, PREP_LANES), lambda b, t: (t, b))
    wt = pl.BlockSpec((None, BLOCK_CH, PREP_LANES), lambda b, t: (b, 0, 0))
    par = pl.BlockSpec((SUBLANES, PREP_LANES), lambda b, t: (0, b))
    return pl.pallas_call(
        body, name=name, grid=(SSM_BLOCKS, S // tc), in_specs=[ch, wt, wt, wt, wt, par, par], out_specs=[st, st, ch],
        out_shape=[jax.ShapeDtypeStruct((S, SSM_LANES), F32), jax.ShapeDtypeStruct((S, SSM_LANES), F32),
                   jax.ShapeDtypeStruct((S, SSM_WIDTH), F32)],
        scratch_shapes=[pltpu.VMEM((1, PREP_LANES), F32), pltpu.VMEM((1, PREP_LANES), F32)], compiler_params=_params(2),
    )(u, bb_r, bb_i, cb_r, ncb_i, a_r8, a_i8)


def ssm_bwd(name, dy, du_add, u, xs_r, xs_i, bb_r, bb_i, cb_r, ncb_i, a_r8, a_i8):
    S = u.shape[0]
    tc = min(SCAN_TIME_TILE, S)
    nt = S // tc

    def body(dy_ref, dua_ref, u_ref, xr_ref, xi_ref, pr_ref, pi_ref, bbr_ref, bbi_ref, cbr_ref, ncbi_ref, ar_ref, ai_ref,
             du_ref, dbbr_ref, dbbi_ref, dcbr_ref, dncbi_ref, dar_ref, dai_ref, lr_sc, li_sc, cr_sc, ci_sc):
        t = pl.program_id(1)
        first = t == 0

        @pl.when(first)
        def _():
            for r in (dbbr_ref, dbbi_ref, dcbr_ref, dncbi_ref):
                r[...] = jnp.zeros_like(r)

        dyb = dy_ref[...].astype(BF16)
        lr_sc[...] = jnp.dot(dyb, cbr_ref[...], preferred_element_type=F32)
        li_sc[...] = jnp.dot(dyb, ncbi_ref[...], preferred_element_type=F32)
        _scan_tile(lr_sc, li_sc, ar_ref, ai_ref, cr_sc, ci_sc, reverse=True, first=first,
                   states=(xr_ref, xi_ref, pr_ref, pi_ref, dar_ref, dai_ref, nt - 1 - t))
        lrb, lib = lr_sc[...].astype(BF16), li_sc[...].astype(BF16)
        du_ref[...] = (lax.dot_general(lrb, bbr_ref[...], NT_DIMS, preferred_element_type=F32)
                       + lax.dot_general(lib, bbi_ref[...], NT_DIMS, preferred_element_type=F32) + dua_ref[...])
        ub = u_ref[...].astype(BF16)
        dbbr_ref[...] += lax.dot_general(ub, lrb, TN_DIMS, preferred_element_type=F32)
        dbbi_ref[...] += lax.dot_general(ub, lib, TN_DIMS, preferred_element_type=F32)
        dcbr_ref[...] += lax.dot_general(dyb, xr_ref[...].astype(BF16), TN_DIMS, preferred_element_type=F32)
        dncbi_ref[...] += lax.dot_general(dyb, xi_ref[...].astype(BF16), TN_DIMS, preferred_element_type=F32)

    ch = pl.BlockSpec((tc, BLOCK_CH), lambda b, t: (nt - 1 - t, b))
    st = pl.BlockSpec((tc, PREP_LANES), lambda b, t: (nt - 1 - t, b))
    prev = pl.BlockSpec((SUBLANES, PREP_LANES), lambda b, t: (jnp.maximum((nt - 1 - t) * (tc // SUBLANES) - 1, 0), b))
    wt = pl.BlockSpec((None, BLOCK_CH, PREP_LANES), lambda b, t: (b, 0, 0))
    par = pl.BlockSpec((SUBLANES, PREP_LANES), lambda b, t: (0, b))
    blk = jax.ShapeDtypeStruct((SSM_BLOCKS, BLOCK_CH, PREP_LANES), F32)
    rows8 = jax.ShapeDtypeStruct((SUBLANES, SSM_LANES), F32)
    return pl.pallas_call(
        body, name=name, grid=(SSM_BLOCKS, nt), in_specs=[ch, ch, ch, st, st, prev, prev, wt, wt, wt, wt, par, par],
        out_specs=[ch, wt, wt, wt, wt, par, par],
        out_shape=[jax.ShapeDtypeStruct((S, SSM_WIDTH), F32), blk, blk, blk, blk, rows8, rows8],
        scratch_shapes=[pltpu.VMEM((tc, PREP_LANES), F32), pltpu.VMEM((tc, PREP_LANES), F32),
                        pltpu.VMEM((1, PREP_LANES), F32), pltpu.VMEM((1, PREP_LANES), F32)],
        compiler_params=_params(2),
    )(dy, du_add, u, xs_r, xs_i, xs_r, xs_i, bb_r, bb_i, cb_r, ncb_i, a_r8, a_i8)


def _ssm_prep_f(a_re, a_im, log_dt, bt_re, bt_im, c_re, c_im):
    first_group = pl.program_id(0) * (SSM_GROUPS // SSM_BLOCKS)
    iota = lambda shape, d: lax.broadcasted_iota(jnp.int32, shape, d)
    grp_of_row = lambda shape: iota(shape, 0) >> int(math.log2(SSM_GROUP))
    grp_of_lane = lambda shape: iota(shape, 1) >> int(math.log2(SSM_STATE))
    rep = (grp_of_row((BLOCK_CH, SSM_GROUPS)) + first_group == iota((BLOCK_CH, SSM_GROUPS), 1)).astype(F32)
    til = ((iota((SSM_STATE, PREP_LANES), 1) & (SSM_STATE - 1)) == iota((SSM_STATE, PREP_LANES), 0)).astype(F32)
    m_rows = (grp_of_row((BLOCK_CH, PREP_LANES)) == grp_of_lane((BLOCK_CH, PREP_LANES))).astype(F32)
    m_grp = (iota((SSM_GROUPS, PREP_LANES), 0) == grp_of_lane((SSM_GROUPS, PREP_LANES)) + first_group).astype(F32)
    dt = jnp.exp(log_dt)
    decay = jnp.exp(a_re * dt)
    ar = decay * jnp.cos(a_im * dt)
    ai = decay * jnp.sin(a_im * dt)
    den = a_re * a_re + a_im * a_im
    nr = ar - 1.0
    coef_r = (nr * a_re + ai * a_im) / den
    coef_i = (ai * a_re - nr * a_im) / den
    cr, ci = _hp_dot(rep, coef_r), _hp_dot(rep, coef_i)
    bb_r = cr * bt_re - ci * bt_im
    bb_i = cr * bt_im + ci * bt_re
    big = lambda m: _hp_dot(m, til) * m_rows
    lanes = lambda m: jnp.broadcast_to(jnp.sum(_hp_dot(m, til) * m_grp, axis=0, keepdims=True), (SUBLANES, PREP_LANES))
    return lanes(ar), lanes(ai), big(bb_r), big(bb_i), big(c_re), -big(c_im)


def _whole(arr, **kw):
    nd = arr.ndim
    return In(arr, arr.shape, lambda *_: (0,) * nd, **kw)


def _adamw_math(w, g, m, v):
    m = ADAM_B1 * m + (1.0 - ADAM_B1) * g
    v = ADAM_B2 * v + (1.0 - ADAM_B2) * (g * g)
    m_hat = m / (1.0 - ADAM_B1 ** ADAM_STEP)
    v_hat = v / (1.0 - ADAM_B2 ** ADAM_STEP)
    delta = -ADAM_LR * (m_hat / (jnp.sqrt(v_hat) + ADAM_EPS) + ADAM_WD * w)
    return delta, m, v


def adamw_big(name, w, g, m, v):
    R, C = w.shape
    tr = R
    for cand in (512, 344, 256, 128):
        if R % cand == 0:
            tr = cand
            break

    def body(w_ref, g_ref, m_ref, v_ref, d_ref, nm_ref, nv_ref):
        d, nm, nv = _adamw_math(w_ref[...], g_ref[...], m_ref[...], v_ref[...])
        d_ref[...] = d
        nm_ref[...] = nm
        nv_ref[...] = nv

    spec = pl.BlockSpec((tr, C), lambda i: (i, 0))
    return pl.pallas_call(
        body, name=name, grid=(R // tr,), in_specs=[spec] * 4, out_specs=[spec] * 3,
        out_shape=[jax.ShapeDtypeStruct((R, C), F32)] * 3, compiler_params=_params(1),
    )(w, g, m, v)


def adamw_small(name, ws, gs, ms, vs):
    n = len(ws)

    def body(*refs):
        for k in range(n):
            d, nm, nv = _adamw_math(refs[k][...], refs[n + k][...], refs[2 * n + k][...], refs[3 * n + k][...])
            refs[4 * n + k][...] = d
            refs[5 * n + k][...] = nm
            refs[6 * n + k][...] = nv

    vm = pl.BlockSpec(memory_space=pltpu.VMEM)
    shapes = [jax.ShapeDtypeStruct(w.shape, F32) for w in ws]
    res = pl.pallas_call(
        body, name=name, in_specs=[vm] * (4 * n), out_specs=[vm] * (3 * n), out_shape=shapes * 3,
        compiler_params=pltpu.CompilerParams(vmem_limit_bytes=VMEM_LIMIT_BYTES),
    )(*ws, *gs, *ms, *vs)
    return res[:n], res[n:2 * n], res[2 * n:]


def _place():
    return lax.axis_index("x"), lax.axis_index("y"), lax.axis_index("c")


def _other_chips(x, y):
    return [(1 - x, y), (x, 1 - y), (1 - x, 1 - y)]


HBM = pl.BlockSpec(memory_space=pl.ANY)


def all_gather_halves(name, blocks):
    n = len(blocks)

    def body(*refs):
        in_refs, out_refs = refs[:n], refs[n:2 * n]
        send_sems, recv_sems = refs[2 * n:]
        x, y, c = _place()
        me, sibling = (x, y, c), (x, y, 1 - c)
        chips = _other_chips(x, y)

        def slot(a, px, py, pc):
            return out_refs[a].at[4 * px + 2 * py + pc]

        def copy(a, k, block, to, src=None):
            return pltpu.make_async_remote_copy(
                src_ref=slot(a, *block) if src is None else src, dst_ref=slot(a, *block),
                send_sem=send_sems.at[a, k], recv_sem=recv_sems.at[a, k], device_id=to, device_id_type=MESH)

        first = []
        for a in range(n):
            first.append(copy(a, 0, me, sibling, src=in_refs[a]))
            first += [copy(a, 1 + j, me, (*chip, c), src=in_refs[a]) for j, chip in enumerate(chips)]
        for cp in first:
            cp.start()
        passed = []
        for j, chip in enumerate(chips):
            for a in range(n):
                copy(a, 1 + j, (*chip, c), me).wait_recv()
                fw = copy(a, 4 + j, (*chip, c), sibling)
                fw.start()
                passed.append(fw)
        for a in range(n):
            copy(a, 0, sibling, me).wait_recv()
            for j, chip in enumerate(chips):
                copy(a, 4 + j, (*chip, 1 - c), me).wait_recv()
        for cp in first + passed:
            cp.wait_send()

    res = pl.pallas_call(
        body, name=name, in_specs=[HBM] * n, out_specs=[HBM] * n,
        out_shape=[jax.ShapeDtypeStruct((N_DEV,) + b.shape, b.dtype) for b in blocks],
        scratch_shapes=[pltpu.SemaphoreType.DMA((n, 7)), pltpu.SemaphoreType.DMA((n, 7))],
    )(*blocks)
    return list(res)


def pair_send_halves(name, grads):
    n = len(grads)

    def body(*refs):
        in_refs, out_refs = refs[:n], refs[n:2 * n]
        send_sems, recv_sems = refs[2 * n:]
        x, y, c = _place()
        cps = []
        for a in range(n):
            cp = pltpu.make_async_remote_copy(
                src_ref=in_refs[a].at[:, 1 - c], dst_ref=out_refs[a], send_sem=send_sems.at[a], recv_sem=recv_sems.at[a],
                device_id=(x, y, 1 - c), device_id_type=MESH)
            cp.start()
            cps.append(cp)
        for cp in cps:
            cp.wait()

    res = pl.pallas_call(
        body, name=name, in_specs=[HBM] * n, out_specs=[HBM] * n,
        out_shape=[jax.ShapeDtypeStruct((g.shape[0],) + g.shape[2:], g.dtype) for g in grads],
        scratch_shapes=[pltpu.SemaphoreType.DMA((n,)), pltpu.SemaphoreType.DMA((n,))],
    )(*grads)
    return list(res)


def pair_add(name, grad, got, c_arr):
    nsh, _, M, N = grad.shape
    tr = M
    for cand in (512, 256, 192, 128, 64, 16):
        if M % cand == 0:
            tr = cand
            break

    def body(c_ref, g_ref, p_ref, o_ref):
        o_ref[...] = (g_ref[...] + p_ref[...]).astype(BF16)

    return pl.pallas_call(
        body, name=name,
        grid_spec=pltpu.PrefetchScalarGridSpec(
            num_scalar_prefetch=1, grid=(nsh, M // tr),
            in_specs=[pl.BlockSpec((None, None, tr, N), lambda j, i, c_ref: (j, c_ref[0], i, 0)),
                      pl.BlockSpec((None, tr, N), lambda j, i, c_ref: (j, i, 0))],
            out_specs=pl.BlockSpec((None, tr, N), lambda j, i, c_ref: (j, i, 0))),
        out_shape=jax.ShapeDtypeStruct((nsh, M, N), BF16), compiler_params=_params(2),
    )(c_arr, grad, got)


def scatter_to_chips(name, parts):
    n = len(parts)

    def body(*refs):
        in_refs, out_refs = refs[:n], refs[n:2 * n]
        send_sems, recv_sems = refs[2 * n:]
        x, y, c = _place()
        mine = 2 * x + y
        chips = _other_chips(x, y)
        cps = []
        for a in range(n):
            for k, (px, py) in enumerate(chips):
                cp = pltpu.make_async_remote_copy(
                    src_ref=in_refs[a].at[2 * px + py], dst_ref=out_refs[a].at[mine],
                    send_sem=send_sems.at[a, k], recv_sem=recv_sems.at[a, k], device_id=(px, py, c), device_id_type=MESH)
                cp.start()
                cps.append((cp, a, k, px, py))
        for cp, a, k, px, py in cps:
            pltpu.make_async_remote_copy(
                src_ref=in_refs[a].at[mine], dst_ref=out_refs[a].at[2 * px + py],
                send_sem=send_sems.at[a, k], recv_sem=recv_sems.at[a, k], device_id=(px, py, c), device_id_type=MESH).wait_recv()
        for cp, *_ in cps:
            cp.wait_send()

    res = pl.pallas_call(
        body, name=name, in_specs=[HBM] * n, out_specs=[HBM] * n,
        out_shape=[jax.ShapeDtypeStruct(p.shape, p.dtype) for p in parts],
        scratch_shapes=[pltpu.SemaphoreType.DMA((n, 3)), pltpu.SemaphoreType.DMA((n, 3))],
    )(*parts)
    return list(res)


def sum_chips(name, q):
    nsh, M, N = q.shape
    tr = M
    for cand in (512, 256, 192, 128, 64, 16):
        if M % cand == 0:
            tr = cand
            break

    def body(q_ref, o_ref):
        acc = q_ref[0].astype(F32)
        for j in range(1, nsh):
            acc = acc + q_ref[j].astype(F32)
        o_ref[...] = acc

    return pl.pallas_call(
        body, name=name, grid=(M // tr,), in_specs=[pl.BlockSpec((nsh, tr, N), lambda i: (0, i, 0))],
        out_specs=pl.BlockSpec((tr, N), lambda i: (i, 0)), out_shape=jax.ShapeDtypeStruct((M, N), F32),
        compiler_params=_params(1),
    )(q)


def pair_exchange(name, halves):
    n = len(halves)

    def body(*refs):
        in_refs, out_refs = refs[:n], refs[n:2 * n]
        send_sems, recv_sems = refs[2 * n:]
        x, y, c = _place()
        cps = []
        for a in range(n):
            cp = pltpu.make_async_remote_copy(
                src_ref=in_refs[a], dst_ref=out_refs[a], send_sem=send_sems.at[a], recv_sem=recv_sems.at[a],
                device_id=(x, y, 1 - c), device_id_type=MESH)
            cp.start()
            cps.append(cp)
        for cp in cps:
            cp.wait()

    res = pl.pallas_call(
        body, name=name, in_specs=[HBM] * n, out_specs=[HBM] * n,
        out_shape=[jax.ShapeDtypeStruct(h.shape, h.dtype) for h in halves],
        scratch_shapes=[pltpu.SemaphoreType.DMA((n,)), pltpu.SemaphoreType.DMA((n,))],
    )(*halves)
    return list(res)


SEM = pl.BlockSpec(memory_space=pltpu.SEMAPHORE)
IN_HBM = pl.BlockSpec(memory_space=pltpu.HBM)
SPLIT_COPY = pltpu.CompilerParams(has_side_effects=pltpu.SideEffectType.DATAFLOW_SIDE_EFFECTING)


def _scatter_copies(src_refs, dst_refs, send_sems, recv_sems):
    x, y, c = _place()
    mine = 2 * x + y
    return [pltpu.make_async_remote_copy(
        src_ref=src_refs[a].at[2 * px + py], dst_ref=dst_refs[a].at[mine], send_sem=send_sems.at[a * (N_CHIPS - 1) + k],
        recv_sem=recv_sems.at[a * (N_CHIPS - 1) + k], device_id=(px, py, c), device_id_type=MESH)
        for a in range(len(src_refs)) for k, (px, py) in enumerate(_other_chips(x, y))]


def _pair_copies(src_refs, dst_refs, send_sems, recv_sems):
    x, y, c = _place()
    return [pltpu.make_async_remote_copy(
        src_ref=src_refs[a].at[:, 1 - c], dst_ref=dst_refs[a], send_sem=send_sems.at[a], recv_sem=recv_sems.at[a],
        device_id=(x, y, 1 - c), device_id_type=MESH) for a in range(len(src_refs))]


def _gather_copies(src_refs, dst_refs, send_sems, recv_sems):
    x, y, c = _place()
    me = 4 * x + 2 * y + c
    cps = []
    for a in range(len(src_refs)):
        for k in range(1, N_DEV):
            to = (1 - x if k & 4 else x, 1 - y if k & 2 else y, 1 - c if k & 1 else c)
            s = a * (N_DEV - 1) + k - 1
            cps.append(pltpu.make_async_remote_copy(
                src_ref=src_refs[a], dst_ref=dst_refs[a].at[me], send_sem=send_sems.at[s], recv_sem=recv_sems.at[s],
                device_id=to, device_id_type=MESH))
    return cps


def split_copy_start(name, copies, n_sem, srcs, land_shapes):
    n = len(srcs)
    lands = [lax.empty(s.shape, s.dtype) for s in land_shapes]

    def body(*refs):
        for cp in copies(refs[:n], refs[n:2 * n], refs[2 * n], refs[2 * n + 1]):
            cp.start()
        refs[-1][...] = jnp.zeros_like(refs[-1])

    thru = [pltpu.HBM(a.shape, a.dtype) for a in (*srcs, *lands)]
    res = pl.pallas_call(
        body, name=name, in_specs=[IN_HBM] * (2 * n),
        out_specs=(SEM, SEM, *[IN_HBM] * (2 * n), pl.BlockSpec(memory_space=pltpu.VMEM)),
        out_shape=(pltpu.SemaphoreType.DMA((n * n_sem,)), pltpu.SemaphoreType.DMA((n * n_sem,)), *thru,
                   jax.ShapeDtypeStruct((SUBLANES, LANES), F32)),
        input_output_aliases={i: 2 + i for i in range(2 * n)}, compiler_params=SPLIT_COPY,
    )(*[pltpu.with_memory_space_constraint(a, pltpu.HBM) for a in (*srcs, *lands)])
    return (copies, n, res[0], res[1], res[2:2 + 2 * n]), res[-1][0, 0]


def split_copy_wait(name, handle, after):
    copies, n, send_sems, recv_sems, thru = handle

    def body(*refs):
        for cp in copies(refs[:n], refs[n:2 * n], refs[2 * n], refs[2 * n + 1]):
            cp.wait_send()
            cp.wait_recv()

    res = pl.pallas_call(
        body, name=name, in_specs=[IN_HBM] * (2 * n) + [SEM, SEM, pl.BlockSpec(memory_space=pl.ANY)],
        out_specs=[IN_HBM] * (2 * n), out_shape=[pltpu.HBM(a.shape, a.dtype) for a in thru],
        input_output_aliases={i: i for i in range(2 * n)}, compiler_params=SPLIT_COPY,
    )(*thru, send_sems, recv_sems, after)
    return list(res[:n]), list(res[n:])


def all_reduce_small(name, v):
    R, C = v.shape

    def body(v_ref, o_ref, gath, send_sems, recv_sems):
        x, y, c = _place()
        me, sibling = (x, y, c), (x, y, 1 - c)
        chips = _other_chips(x, y)

        def slot(px, py, pc):
            return gath.at[4 * px + 2 * py + pc]

        def copy(k, block, to, src=None):
            return pltpu.make_async_remote_copy(
                src_ref=slot(*block) if src is None else src, dst_ref=slot(*block),
                send_sem=send_sems.at[k], recv_sem=recv_sems.at[k], device_id=to, device_id_type=MESH)

        first = [copy(0, me, sibling, src=v_ref)]
        first += [copy(1 + j, me, (*chip, c), src=v_ref) for j, chip in enumerate(chips)]
        for cp in first:
            cp.start()
        slot(*me)[...] = v_ref[...]
        passed = [copy(4 + j, (*chip, c), sibling) for j, chip in enumerate(chips)]
        for j, chip in enumerate(chips):
            copy(1 + j, (*chip, c), me).wait_recv()
            passed[j].start()
        copy(0, sibling, me).wait_recv()
        for j, chip in enumerate(chips):
            copy(4 + j, (*chip, 1 - c), me).wait_recv()
        for cp in first + passed:
            cp.wait_send()
        acc = gath[0]
        for d in range(1, N_DEV):
            acc = acc + gath[d]
        o_ref[...] = acc

    vm = pl.BlockSpec(memory_space=pltpu.VMEM)
    return pl.pallas_call(
        body, name=name, in_specs=[vm], out_specs=vm, out_shape=jax.ShapeDtypeStruct((R, C), F32),
        scratch_shapes=[pltpu.VMEM((N_DEV, R, C), F32), pltpu.SemaphoreType.DMA((7,)), pltpu.SemaphoreType.DMA((7,))],
        compiler_params=pltpu.CompilerParams(vmem_limit_bytes=VMEM_LIMIT_BYTES),
    )(v)


LANES = 128
PACK_ROW_MULTIPLE = 1024
SMALL_SHARDED = {
    "w_in": ((D_MODEL, 1216), 1), "mla_w_uq": ((MLA_Q_RANK, 768), 1), "mla_w_ukv": ((MLA_KV_RANK, 1024), 1),
    "ssm_w_glu": ((SSM_WIDTH, SSM_WIDTH), 0), "w_o": ((D_MODEL, D_MODEL), 0), "xattn_w_q": ((D_MODEL, 512), 0),
    "xattn_w_kv": ((D_MODEL, 1024), 0), "xattn_w_o": ((512, D_MODEL), 1),
}
FFN_NAMES = ["ffn1_w_gate", "ffn1_w_up", "ffn1_w_down", "ffn2_w_gate", "ffn2_w_up", "ffn2_w_down"]
TRANSPOSED_VIEW = ("ffn1_w_gate", "ffn1_w_up", "ffn2_w_gate", "ffn2_w_up", "w_in", "mla_w_uq")


def _shard_shape(name):
    (r, cdim), ax = SMALL_SHARDED[name]
    return (r // N_CHIPS, cdim) if ax == 0 else (r, cdim // N_CHIPS)


def _pack_shards(shards):
    parts = []
    for name in SMALL_SHARDED:
        a = shards[name]
        lead = a.shape[:-2]
        parts.append(a.reshape(lead + (a.shape[-2] * a.shape[-1] // LANES, LANES)))
    rows = sum(q.shape[-2] for q in parts)
    parts.append(jnp.zeros(lead + (-rows % PACK_ROW_MULTIPLE, LANES), parts[0].dtype))
    return jnp.concatenate(parts, axis=-2)


def _unpack_shards(packed):
    out, r0 = {}, 0
    lead = packed.shape[:-2]
    for name in SMALL_SHARDED:
        r, cdim = _shard_shape(name)
        rows = r * cdim // LANES
        out[name] = packed[..., r0:r0 + rows, :].reshape(lead + (r, cdim))
        r0 += rows
    return out


def _full_from_shards(name, sh):
    (r, cdim), ax = SMALL_SHARDED[name]
    if ax == 0:
        return sh.reshape(r, cdim)
    return jnp.transpose(sh, (1, 0, 2)).reshape(r, cdim)


def _shards_from_full(name, full):
    (r, cdim), ax = SMALL_SHARDED[name]
    if ax == 0:
        return full.reshape(N_CHIPS, r // N_CHIPS, cdim)
    return jnp.transpose(full.reshape(r, N_CHIPS, cdim // N_CHIPS), (1, 0, 2))


SMALL_REPL = {
    "ffn1_norm": (1, 1024), "mix_norm": (1, 1024), "mla_q_norm": (1, 384), "mla_kv_norm": (1, 256),
    "mla_qk_norm_q": (1, 192), "mla_qk_norm_k": (1, 192), "ssm_a_re": (32, 64), "ssm_a_im": (32, 64),
    "ssm_log_dt": (32, 1), "ssm_b_re": (32, 64, 16), "ssm_b_im": (32, 64, 16), "ssm_c_re": (32, 16, 64),
    "ssm_c_im": (32, 16, 64), "ssm_d": (1, 512), "ssm_b_glu": (1, 512), "out_norm_mla": (1, 512),
    "out_norm_ssm": (1, 512), "xattn_norm": (1, 1024), "mem_norm": (1, 1024), "xattn_q_norm": (1, 128),
    "xattn_k_norm": (1, 128), "ffn2_norm": (1, 1024),
}


def _pack_repl(grads, loss):
    flat = jnp.concatenate([grads[n].reshape(-1) for n in SMALL_REPL] + [loss.reshape(1)])
    rows = -(-flat.shape[0] // (LANES * SUBLANES)) * SUBLANES
    return jnp.pad(flat, (0, rows * LANES - flat.shape[0])).reshape(rows, LANES)


def _unpack_repl(packed):
    flat, out, o = packed.reshape(-1), {}, 0
    for n, shp in SMALL_REPL.items():
        size = int(np.prod(shp))
        out[n] = flat[o:o + size].reshape(shp)
        o += size
    out["loss"] = flat[o]
    return out


def _rope_tables(positions):
    half = MLA_ROPE // 2
    inv = ROPE_THETA ** (-jnp.arange(half, dtype=F32) / half)
    ang = positions.astype(F32)[:, None] * inv[None, :]
    cos, sin = jnp.cos(ang), jnp.sin(ang)
    S = positions.shape[0]
    z = lambda w: jnp.zeros((S, w), F32)
    keep = jnp.concatenate([jnp.ones((S, MLA_NOPE), F32), cos, cos, z(HEAD_PAD - MLA_QK)], axis=1)
    from_hi = jnp.concatenate([z(MLA_NOPE), -sin, z(HEAD_PAD - MLA_NOPE - half)], axis=1)
    from_lo = jnp.concatenate([z(MLA_NOPE + half), sin, z(HEAD_PAD - MLA_QK)], axis=1)
    return keep, from_hi, from_lo


def _norm_rope(x, g, keep, from_hi, from_lo):
    y = x * lax.rsqrt(jnp.sum(x * x, axis=-1, keepdims=True) * (1.0 / MLA_QK) + EPS) * g
    half = MLA_ROPE // 2
    return y * keep + _lane_roll(y, HEAD_PAD - half) * from_hi + _lane_roll(y, half) * from_lo


def local_step(x, mem, positions, target, w, wb, small_weights=None, ffn2_weights=None, on_grads=None):
    if small_weights is None:
        small_weights = lambda after: {}
    if ffn2_weights is None:
        ffn2_weights = lambda after: [wb[k] for k in FFN_NAMES[3:]]
    if on_grads is None:
        on_grads = lambda tag, g: 0.0
    S = x.shape[0]
    tm = min(FUSED_TILE, S)
    g1 = (S // tm,)
    tile = lambda arr, **kw: In(arr, (tm, arr.shape[1]), lambda i: (i, 0), rows=True, **kw)
    par = lambda arr, **kw: In(arr, arr.shape, lambda *_: (0, 0), diff=True, acc=(0,), **kw)
    wt = lambda arr: In(arr, weight=True)
    otile = lambda cols, dt: Out((S, cols), dt, (tm, cols), lambda i: (i, 0))
    grads = {}

    x1, h1, g_1, u_1 = ffn_fwd("ffn1_fwd", x, w["ffn1_norm"], wb["ffn1_w_gate"], wb["ffn1_w_up"], wb["ffn1_w_down"])
    wb = {**wb, **small_weights(x1)}

    keep, from_hi, from_lo = _rope_tables(positions)
    scale = MLA_QK ** -0.5

    def f_pre(xv, kp, fh, fl, g_mix, w_q, w_kv, w_kr, w_u, g_q, g_kv, w_uq, w_ukv, gq, gk):
        h2 = _rms(xv, g_mix)
        cq, ckv = _rms(wdot(h2, w_q), g_q), _rms(wdot(h2, w_kv), g_kv)
        kr = wdot(h2, w_kr)
        qs, ks, vs = [], [], []
        for h in range(MLA_HEADS):
            qs.append(_norm_rope(wdot(cq, w_uq, h), gq, kp, fh, fl) * scale)
            kv = wdot(ckv, w_ukv, h)
            ks.append(_norm_rope(jnp.concatenate([kv[:, :MLA_NOPE], kr], axis=-1), gk, kp, fh, fl))
            vs.append(kv[:, MLA_NOPE:])
        return jnp.concatenate(qs, axis=-1), jnp.concatenate(ks, axis=-1), jnp.concatenate(vs, axis=-1), wdot(h2, w_u)

    def pre_ins(gain_mix):
        return [tile(x1, diff=True), tile(keep), tile(from_hi), tile(from_lo), par(gain_mix),
                wt(wb["w_in_q"]), wt(wb["w_in_kv"]), wt(wb["w_in_kr"]), wt(wb["w_in_u"]),
                par(w["mla_q_norm"]), par(w["mla_kv_norm"]), wt(wb["w_uq"]), wt(wb["w_ukv"]),
                par(w["qk_gain_q"]), par(w["qk_gain_k"])]

    pre_outs = [otile(MLA_HEADS * HEAD_PAD, BF16), otile(MLA_HEADS * HEAD_PAD, BF16), otile(MLA_HEADS * MLA_V, BF16),
                otile(SSM_WIDTH, F32)]
    qh, kh, v0, pu = seg_fwd("pre_mixer_fwd", f_pre, g1, pre_ins(w["mix_norm"]), pre_outs)

    o_mla, lse = attn_fwd("mla_attn_fwd", qh, kh, v0, t=min(ATTN_TILE, S))

    prep_grid = (SSM_BLOCKS,)
    prep_ins = ([_whole(w[k], diff=True, acc=(0,)) for k in ("ssm_a_re", "ssm_a_im", "ssm_log_dt")]
                + [In(w[k], (BLOCK_CH, SSM_STATE), lambda i: (i, 0), diff=True)
                   for k in ("ssm_bt_re", "ssm_bt_im", "ssm_c2_re", "ssm_c2_im")])
    blk_out = Out((SSM_BLOCKS, BLOCK_CH, PREP_LANES), BF16, (None, BLOCK_CH, PREP_LANES), lambda i: (i, 0, 0))
    prep_outs = [Out((SUBLANES, SSM_LANES), F32, (SUBLANES, PREP_LANES), lambda i: (0, i))] * 2 + [blk_out] * 4
    a_r8, a_i8, bb_r, bb_i, cb_r, ncb_i = seg_fwd("ssm_prep_fwd", _ssm_prep_f, prep_grid, prep_ins, prep_outs)

    xs_r, xs_i, y_lin = ssm_fwd("ssm_fwd", pu, bb_r, bb_i, cb_r, ncb_i, a_r8, a_i8)

    M = mem.shape[0]
    f_norm = lambda xv, g: (_rms(xv, g),)
    mem_ins = [In(mem, (M, D_MODEL), lambda i: (0, 0)), par(w["mem_norm"])]
    mem_outs = [Out((M, D_MODEL), BF16, (M, D_MODEL), lambda i: (0, 0))]
    (mn,) = seg_fwd("mem_norm_fwd", f_norm, (1,), mem_ins, mem_outs)
    (kvm,) = mm("xattn_kv_fwd", [mn], [[wb["xattn_w_kv"]]], [F32])

    def f_knorm(kk, gk):
        return (jnp.concatenate([_rms(kk[:, h * XHD:(h + 1) * XHD], gk) for h in range(XH)], axis=-1),)

    keys = ((M, XH * XHD), (M, XH * XHD), lambda i: (0, 0))
    knorm_ins = [In(kvm, keys[1], keys[2], diff=True, grad=keys), par(w["xattn_k_norm"])]
    knorm_outs = [Out(keys[0], F32, keys[1], keys[2])]
    (kn_mem,) = seg_fwd("mem_key_norm_fwd", f_knorm, (1,), knorm_ins, knorm_outs)

    xscale = XHD ** -0.5

    def f_post(o, yl, u, xv, kn, vm, d, w_glu, b, gm, gs, w_o1, w_o2, gx, w_xq, gq, w_xo):
        gl = jax.nn.gelu(yl + d * u)
        so = gl * jax.nn.sigmoid(wdot(gl, w_glu) + b)
        x2 = xv + wdot(_rms(o, gm), w_o1) + wdot(_rms(so, gs), w_o2)
        h3 = _rms(x2, gx)
        heads = []
        for h in range(XH):
            qn = _rms(wdot(h3, w_xq, h), gq)
            p = jax.nn.softmax(_bdot_nt(qn, kn[:, h * XHD:(h + 1) * XHD]) * xscale, axis=-1)
            heads.append(_bdot_nn(p, vm[:, h * XHD:(h + 1) * XHD]))
        return (x2 + wdot(jnp.concatenate(heads, axis=-1), w_xo),)

    def post_ins(gain_d):
        half = (M, XH * XHD)
        return [tile(o_mla, diff=True), tile(y_lin, diff=True), tile(pu, diff=True), tile(x1, diff=True),
                In(kn_mem, half, lambda i: (0, 0), diff=True, acc=(0,)),
                In(kvm, half, lambda i: (0, 1), diff=True, acc=(0,), grad=(half, half, lambda i: (0, 0))),
                par(gain_d), wt(wb["ssm_w_glu"]), par(w["ssm_b_glu"]), par(w["out_norm_mla"]), par(w["out_norm_ssm"]),
                wt(wb["w_o_mla"]), wt(wb["w_o_ssm"]), par(w["xattn_norm"]), wt(wb["xattn_w_q"]), par(w["xattn_q_norm"]),
                wt(wb["xattn_w_o"])]

    post_outs = [otile(D_MODEL, F32)]
    (x3,) = seg_fwd("post_mixer_fwd", f_post, g1, post_ins(w["ssm_d"]), post_outs)

    wg2, wu2, wd2 = ffn2_weights(x3)
    dx4, h4, g_2, u_2, parts = ffn_fwd("ffn2_fwd_loss", x3, w["ffn2_norm"], wg2, wu2, wd2, target=target)
    loss = jnp.sum(parts[::SUBLANES, 0])

    dx3, grads["ffn2_norm"], dg_2, du_2 = ffn_bwd_act("ffn2_bwd_act", dx4, x3, w["ffn2_norm"], g_2, u_2, wg2, wu2, wd2)
    grads["ffn2_w_gate"], grads["ffn2_w_up"], grads["ffn2_w_down"] = ffn_bwd_w("ffn2_bwd_w", h4, dx4, g_2, u_2, dg_2, du_2)
    sent = on_grads("ffn2", grads)

    (do_mla, dy_lin, du_a, dx1_a, dkn, dvm, grads["ssm_d"], grads["ssm_w_glu"], grads["ssm_b_glu"], grads["out_norm_mla"],
     grads["out_norm_ssm"], grads["w_o_mla"], grads["w_o_ssm"], grads["xattn_norm"], grads["xattn_w_q"],
     grads["xattn_q_norm"], grads["xattn_w_o"]) = seg_bwd(
        "post_mixer_bwd", f_post, g1, post_ins(w["ssm_d"] + sent), post_outs, [dx3])

    dkk, grads["xattn_k_norm"] = seg_bwd("mem_key_norm_bwd", f_knorm, (1,), knorm_ins, knorm_outs, [dkn])
    w_kv = wb["xattn_w_kv"]
    (dmn,) = mm("xattn_kv_bwd", [dkk, dvm], [[w_kv[:, :XH * XHD]], [w_kv[:, XH * XHD:]]], [F32], trans=True)
    gk_w, gv_w = mm_tn("xattn_kv_bwd_w", [mn], [dkk, dvm], [(0, [0]), (0, [1])])
    grads["xattn_w_kv"] = jnp.concatenate([gk_w, gv_w], axis=1)
    (grads["mem_norm"],) = seg_bwd("mem_norm_bwd", f_norm, (1,), mem_ins, mem_outs, [dmn])

    du, dbb_r, dbb_i, dcb_r, dncb_i, da_r8, da_i8 = ssm_bwd("ssm_bwd", dy_lin, du_a, pu, xs_r, xs_i, bb_r, bb_i, cb_r, ncb_i,
                                                            a_r8, a_i8)
    prep_g = seg_bwd("ssm_prep_bwd", _ssm_prep_f, prep_grid, prep_ins, prep_outs, [da_r8, da_i8, dbb_r, dbb_i, dcb_r, dncb_i])
    for k, gname in enumerate(("ssm_a_re", "ssm_a_im", "ssm_log_dt", "ssm_bt_re", "ssm_bt_im", "ssm_c2_re", "ssm_c2_im")):
        grads[gname] = prep_g[k]

    dqh, dkh, dv0 = attn_bwd("mla_attn_bwd", qh, kh, v0, do_mla, o_mla, lse, t=min(ATTN_TILE, S))

    (dx1, grads["mix_norm"], grads["w_in_q"], grads["w_in_kv"], grads["w_in_kr"], grads["w_in_u"], grads["mla_q_norm"],
     grads["mla_kv_norm"], grads["w_uq"], grads["w_ukv"], grads["qk_gain_q"], grads["qk_gain_k"]) = seg_bwd(
        "pre_mixer_bwd", f_pre, g1, pre_ins(w["mix_norm"]), pre_outs, [dqh, dkh, dv0, du], adds={0: dx1_a},
        row_block=FUSED_ROW_BLOCK)

    sent = on_grads("small", grads)
    dx, grads["ffn1_norm"], dg_1, du_1 = ffn_bwd_act("ffn1_bwd_act", dx1, x, w["ffn1_norm"] + sent, g_1, u_1,
                                                     wb["ffn1_w_gate"], wb["ffn1_w_up"], wb["ffn1_w_down"])
    grads["ffn1_w_gate"], grads["ffn1_w_up"], grads["ffn1_w_down"] = ffn_bwd_w("ffn1_bwd_w", h1, dx1, g_1, u_1, dg_1, du_1)
    return loss, dx, grads


def _pad_cols(a, n):
    return jnp.pad(a, ((0, 0), (0, n - a.shape[1])))


def _step_weights(shards):
    wb = {}
    w_in = _full_from_shards("w_in", shards["w_in"])
    wb["w_in_q"] = w_in[:, :MLA_Q_RANK]
    wb["w_in_kv"] = w_in[:, MLA_Q_RANK:MLA_Q_RANK + MLA_KV_RANK]
    wb["w_in_kr"] = _pad_cols(w_in[:, MLA_Q_RANK + MLA_KV_RANK:MLA_Q_RANK + MLA_KV_RANK + MLA_ROPE], LANES)
    wb["w_in_u"] = w_in[:, MLA_Q_RANK + MLA_KV_RANK + MLA_ROPE:]
    wb["w_uq"] = jnp.pad(shards["mla_w_uq"], ((0, 0), (0, 0), (0, HEAD_PAD - MLA_QK)))
    wb["w_ukv"] = shards["mla_w_ukv"]
    wb["ssm_w_glu"] = _full_from_shards("ssm_w_glu", shards["ssm_w_glu"])
    w_o = _full_from_shards("w_o", shards["w_o"])
    wb["w_o_mla"], wb["w_o_ssm"] = w_o[:SSM_WIDTH], w_o[SSM_WIDTH:]
    w_xq = _full_from_shards("xattn_w_q", shards["xattn_w_q"])
    wb["xattn_w_q"] = jnp.transpose(w_xq.reshape(D_MODEL, XH, XHD), (1, 0, 2))
    wb["xattn_w_kv"] = _full_from_shards("xattn_w_kv", shards["xattn_w_kv"])
    wb["xattn_w_o"] = _full_from_shards("xattn_w_o", shards["xattn_w_o"])
    return wb


def _sharded_grads(g):
    out = {}
    kr = g["w_in_kr"][:, :MLA_ROPE]
    out["w_in"] = _shards_from_full("w_in", jnp.concatenate([g["w_in_q"], g["w_in_kv"], kr, g["w_in_u"]], axis=1))
    out["mla_w_uq"] = g["w_uq"][:, :, :MLA_QK]
    out["mla_w_ukv"] = g["w_ukv"]
    out["ssm_w_glu"] = _shards_from_full("ssm_w_glu", g["ssm_w_glu"])
    out["w_o"] = _shards_from_full("w_o", jnp.concatenate([g["w_o_mla"], g["w_o_ssm"]], axis=0))
    w_xq = jnp.transpose(g["xattn_w_q"], (1, 0, 2)).reshape(D_MODEL, XH * XHD)
    out["xattn_w_q"] = _shards_from_full("xattn_w_q", w_xq)
    out["xattn_w_kv"] = _shards_from_full("xattn_w_kv", g["xattn_w_kv"])
    out["xattn_w_o"] = _shards_from_full("xattn_w_o", g["xattn_w_o"])
    return out


def _problem_repl_grads(g):
    out = {}
    out["mla_qk_norm_q"] = g["qk_gain_q"][:, :MLA_QK]
    out["mla_qk_norm_k"] = g["qk_gain_k"][:, :MLA_QK]
    out["ssm_b_re"] = jnp.transpose(g["ssm_bt_re"].reshape(SSM_GROUPS, SSM_GROUP, SSM_STATE), (0, 2, 1))
    out["ssm_b_im"] = jnp.transpose(g["ssm_bt_im"].reshape(SSM_GROUPS, SSM_GROUP, SSM_STATE), (0, 2, 1))
    out["ssm_c_re"] = g["ssm_c2_re"].reshape(SSM_GROUPS, SSM_GROUP, SSM_STATE)
    out["ssm_c_im"] = g["ssm_c2_im"].reshape(SSM_GROUPS, SSM_GROUP, SSM_STATE)
    for k in SMALL_REPL:
        if k not in out:
            out[k] = g[k]
    return out


def _problem_grads(g):
    out = {k: _full_from_shards(k, v) for k, v in _sharded_grads(g).items()}
    out.update(_problem_repl_grads(g))
    out.update({k: g[k] for k in FFN_NAMES})
    return out


def _step_params(p):
    row = lambda a: a.reshape(1, -1)
    w = {k: row(p[k]) for k in ("ffn1_norm", "mix_norm", "mla_q_norm", "mla_kv_norm", "ssm_b_glu", "out_norm_mla",
                                "out_norm_ssm", "xattn_norm", "mem_norm", "xattn_q_norm", "xattn_k_norm", "ffn2_norm")}
    w["qk_gain_q"] = _pad_cols(row(p["mla_qk_norm_q"]), HEAD_PAD)
    w["qk_gain_k"] = _pad_cols(row(p["mla_qk_norm_k"]), HEAD_PAD)
    w["ssm_a_re"], w["ssm_a_im"] = p["ssm_a_re"], p["ssm_a_im"]
    w["ssm_log_dt"] = p["ssm_log_dt"].reshape(SSM_GROUPS, 1)
    w["ssm_bt_re"] = jnp.transpose(p["ssm_b_re"], (0, 2, 1)).reshape(SSM_WIDTH, SSM_STATE)
    w["ssm_bt_im"] = jnp.transpose(p["ssm_b_im"], (0, 2, 1)).reshape(SSM_WIDTH, SSM_STATE)
    w["ssm_c2_re"] = p["ssm_c_re"].reshape(SSM_WIDTH, SSM_STATE)
    w["ssm_c2_im"] = p["ssm_c_im"].reshape(SSM_WIDTH, SSM_STATE)
    w["ssm_d"] = p["ssm_d"].reshape(1, SSM_WIDTH)
    return w


ARG_NAMES = ['x', 'mem', 'positions', 'ffn1_norm', 'ffn1_w_gate', 'ffn1_w_up', 'ffn1_w_down', 'mix_norm', 'w_in', 'mla_q_norm', 'mla_w_uq', 'mla_kv_norm', 'mla_w_ukv', 'mla_qk_norm_q', 'mla_qk_norm_k', 'ssm_a_re', 'ssm_a_im', 'ssm_log_dt', 'ssm_b_re', 'ssm_b_im', 'ssm_c_re', 'ssm_c_im', 'ssm_d', 'ssm_w_glu', 'ssm_b_glu', 'out_norm_mla', 'out_norm_ssm', 'w_o', 'xattn_norm', 'mem_norm', 'xattn_w_q', 'xattn_w_kv', 'xattn_q_norm', 'xattn_k_norm', 'xattn_w_o', 'ffn2_norm', 'ffn2_w_gate', 'ffn2_w_up', 'ffn2_w_down']
WEIGHT_NAMES = ARG_NAMES[3:]


def _gather_weights(p, c):
    half = lambda a: lax.dynamic_slice_in_dim(a, c * (a.shape[0] // 2), a.shape[0] // 2, axis=0)
    ffn1 = [half(p[k].astype(BF16)) for k in FFN_NAMES[:3]]
    small = [half(_pack_shards({k: p[k].astype(BF16) for k in SMALL_SHARDED}))]
    ffn2 = [half(p[k].astype(BF16)) for k in FFN_NAMES[3:]]
    me = 4 * lax.axis_index("x") + 2 * lax.axis_index("y") + c
    own = lambda got, blocks: [lax.dynamic_update_index_in_dim(g, b, me, 0) for g, b in zip(got, blocks)]
    as_shards = lambda a: a.reshape(N_CHIPS, 2 * a.shape[1], a.shape[2])
    landing = lambda blocks: [jax.ShapeDtypeStruct((N_DEV,) + b.shape, b.dtype) for b in blocks]
    got1 = own(all_gather_halves("all_gather_weights_a", ffn1), ffn1)
    got1, small = lax.optimization_barrier((got1, small))
    flight_s, sent_s = split_copy_start("gather_small_start", _gather_copies, N_DEV - 1, small, landing(small))
    sent_s, ffn2 = lax.optimization_barrier((sent_s, ffn2))
    flight_2, sent_2 = split_copy_start("gather_ffn2_start", _gather_copies, N_DEV - 1, ffn2, landing(ffn2))
    wb = {k: as_shards(a) for k, a in zip(FFN_NAMES[:3], got1)}

    def small_weights(after):
        mine, got = split_copy_wait("gather_small_wait", flight_s, after)
        return _step_weights(_unpack_shards(own(got, mine)[0].reshape(N_CHIPS, -1, LANES)))

    def ffn2_weights(after):
        mine, got = split_copy_wait("gather_ffn2_wait", flight_2, after)
        return [as_shards(a) for a in own(got, mine)]

    return wb, small_weights, ffn2_weights, sent_s + sent_2


class _GradReduce:
    def __init__(self, c):
        self.c, self.c_arr = c, jnp.reshape(c, (1,)).astype(jnp.int32)
        self.chip = 2 * lax.axis_index("x") + lax.axis_index("y")
        self.flights = []

    def start(self, tag, arrs):
        split = [a.reshape(N_CHIPS, 2, a.shape[1] // 2, a.shape[2]) for a in arrs]
        return self._scatter(tag, split, pair_send_halves(f"grad_pair_send_{tag}", split))

    def send(self, tag, arrs):
        split = [a.reshape(N_CHIPS, 2, a.shape[1] // 2, a.shape[2]) for a in arrs]
        lands = [jax.ShapeDtypeStruct((s.shape[0],) + s.shape[2:], s.dtype) for s in split]
        flight, sent = split_copy_start(f"grad_pair_send_start_{tag}", _pair_copies, 1, split, lands)
        self.sending = (tag, flight)
        return sent

    def scatter(self, tag, after):
        sent_tag, flight = self.sending
        assert sent_tag == tag
        return self._scatter(tag, *split_copy_wait(f"grad_pair_send_wait_{tag}", flight, after))

    def _scatter(self, tag, split, got):
        parts = [pair_add(f"grad_pair_add_{tag}_{k}", s, g, self.c_arr) for k, (s, g) in enumerate(zip(split, got))]
        flight, sent = split_copy_start(f"grad_scatter_start_{tag}", _scatter_copies, N_CHIPS - 1, parts, parts)
        self.flights.append((tag, flight))
        return sent

    def finish(self, after):
        halves = []
        for tag, flight in self.flights:
            parts, landed = split_copy_wait(f"grad_scatter_wait_{tag}", flight, after)
            for k, (q, p) in enumerate(zip(landed, parts)):
                mine = lax.dynamic_index_in_dim(p, self.chip, 0, keepdims=False)
                halves.append(sum_chips(f"grad_sum_{tag}_{k}", lax.dynamic_update_index_in_dim(q, mine, self.chip, 0)))
        tags = "_".join(t for t, _ in self.flights)
        self.flights = []
        theirs = pair_exchange(f"grad_pair_exchange_{tags}", halves)
        return [jnp.where(self.c == 0, jnp.concatenate([h, t], axis=0), jnp.concatenate([t, h], axis=0))
                for h, t in zip(halves, theirs)]


def kernel(x, mem, positions, ffn1_norm, ffn1_w_gate, ffn1_w_up, ffn1_w_down, mix_norm, w_in, mla_q_norm, mla_w_uq, mla_kv_norm, mla_w_ukv, mla_qk_norm_q, mla_qk_norm_k, ssm_a_re, ssm_a_im, ssm_log_dt, ssm_b_re, ssm_b_im, ssm_c_re, ssm_c_im, ssm_d, ssm_w_glu, ssm_b_glu, out_norm_mla, out_norm_ssm, w_o, xattn_norm, mem_norm, xattn_w_q, xattn_w_kv, xattn_q_norm, xattn_k_norm, xattn_w_o, ffn2_norm, ffn2_w_gate, ffn2_w_up, ffn2_w_down, loss_target, m_ffn1_norm, m_ffn1_w_gate, m_ffn1_w_up, m_ffn1_w_down, m_mix_norm, m_w_in, m_mla_q_norm, m_mla_w_uq, m_mla_kv_norm, m_mla_w_ukv, m_mla_qk_norm_q, m_mla_qk_norm_k, m_ssm_a_re, m_ssm_a_im, m_ssm_log_dt, m_ssm_b_re, m_ssm_b_im, m_ssm_c_re, m_ssm_c_im, m_ssm_d, m_ssm_w_glu, m_ssm_b_glu, m_out_norm_mla, m_out_norm_ssm, m_w_o, m_xattn_norm, m_mem_norm, m_xattn_w_q, m_xattn_w_kv, m_xattn_q_norm, m_xattn_k_norm, m_xattn_w_o, m_ffn2_norm, m_ffn2_w_gate, m_ffn2_w_up, m_ffn2_w_down, v_ffn1_norm, v_ffn1_w_gate, v_ffn1_w_up, v_ffn1_w_down, v_mix_norm, v_w_in, v_mla_q_norm, v_mla_w_uq, v_mla_kv_norm, v_mla_w_ukv, v_mla_qk_norm_q, v_mla_qk_norm_k, v_ssm_a_re, v_ssm_a_im, v_ssm_log_dt, v_ssm_b_re, v_ssm_b_im, v_ssm_c_re, v_ssm_c_im, v_ssm_d, v_ssm_w_glu, v_ssm_b_glu, v_out_norm_mla, v_out_norm_ssm, v_w_o, v_xattn_norm, v_mem_norm, v_xattn_w_q, v_xattn_w_kv, v_xattn_q_norm, v_xattn_k_norm, v_xattn_w_o, v_ffn2_norm, v_ffn2_w_gate, v_ffn2_w_up, v_ffn2_w_down):
    args = dict(locals())
    c = lax.axis_index("c")
    view = lambda k, a: jnp.swapaxes(a, 0, 1) if k in TRANSPOSED_VIEW else a
    p = {k: view(k, args[k][0]) for k in WEIGHT_NAMES}
    mom = {k: view(k, args["m_" + k][0]) for k in WEIGHT_NAMES}
    var = {k: view(k, args["v_" + k][0]) for k in WEIGHT_NAMES}
    natural = {k: view(k, p[k]) for k in WEIGHT_NAMES}

    wb, small_weights, ffn2_weights, sent = _gather_weights(
        {k: (p[k] if k in FFN_NAMES else natural[k]) for k in WEIGHT_NAMES}, c)
    w = _step_params(natural)
    w["ffn1_norm"] = w["ffn1_norm"] + sent
    early, late = _GradReduce(c), _GradReduce(c)

    def on_grads(tag, g):
        if tag == "ffn2":
            return early.send(tag, [g[k] for k in FFN_NAMES[3:]])
        packed = _pack_shards(_sharded_grads(g))
        return early.scatter("ffn2", packed[0, :SUBLANES]) + early.start(tag, [packed])

    loss, dx, g = local_step(x[0], mem[0], positions[0], loss_target[0], w, wb, small_weights, ffn2_weights, on_grads)

    sent = late.send("ffn1", [g[k] for k in FFN_NAMES[:3]])
    shards = early.finish(dx[:SUBLANES, :LANES] + sent)
    sent = late.scatter("ffn1", shards[0][:SUBLANES, :LANES])
    grad = dict(zip(FFN_NAMES[3:], shards[:3]))
    small_sharded = _unpack_shards(shards[3])
    grad.update({k: view(k, small_sharded[k]) for k in SMALL_SHARDED})
    repl = _problem_repl_grads(g)
    grad.update(_unpack_repl(all_reduce_small("grad_all_reduce_small", _pack_repl(repl, loss + sent))))
    loss = grad.pop("loss")

    delta, new_m, new_v = {}, {}, {}
    small = [k for k in WEIGHT_NAMES if k not in FFN_NAMES and k not in SMALL_SHARDED]
    as2d = lambda a: a.reshape(-1, a.shape[-1])

    def update(k):
        delta[k], new_m[k], new_v[k] = adamw_big("adamw_" + k, as2d(p[k]), as2d(grad[k]), as2d(mom[k]), as2d(var[k]))

    for k in WEIGHT_NAMES:
        if k not in small and k not in FFN_NAMES[:3]:
            update(k)
    ds, nms, nvs = adamw_small("adamw_small", [as2d(p[k]) for k in small], [as2d(grad[k].reshape(p[k].shape)) for k in small],
                               [as2d(mom[k]) for k in small], [as2d(var[k]) for k in small])
    for k, d, nm, nv in zip(small, ds, nms, nvs):
        delta[k], new_m[k], new_v[k] = d, nm, nv

    updated = functools.reduce(jnp.add, [d[:1, :1] for d in delta.values()])
    grad.update(zip(FFN_NAMES[:3], late.finish(updated)))
    for k in FFN_NAMES[:3]:
        update(k)

    shaped = lambda d, k: view(k, d.reshape(p[k].shape)).reshape(args[k].shape)
    return (loss, dx[None], *[shaped(grad[k], k) for k in WEIGHT_NAMES], *[shaped(delta[k], k) for k in WEIGHT_NAMES],
            *[shaped(new_m[k], k) for k in WEIGHT_NAMES], *[shaped(new_v[k], k) for k in WEIGHT_NAMES])
```

```python
import functools
import math

import jax
import jax.numpy as jnp
import numpy as np
from jax import lax
from jax.experimental import pallas as pl
from jax.experimental.pallas import tpu as pltpu

F32, BF16 = jnp.float32, jnp.bfloat16
EPS = 1e-6
MESH = pl.DeviceIdType.MESH

D_MODEL, D_FF = 1024, 2752
MLA_HEADS, MLA_Q_RANK, MLA_KV_RANK, MLA_NOPE, MLA_ROPE, MLA_V = 4, 384, 256, 128, 64, 128
MLA_QK = MLA_NOPE + MLA_ROPE
HEAD_PAD = 256
SSM_WIDTH, SSM_GROUP, SSM_GROUPS, SSM_STATE = 512, 16, 32, 64
SSM_LANES = SSM_GROUPS * SSM_STATE
XH, XHD = 4, 128
ROPE_THETA = 10000.0
ADAM_LR, ADAM_B1, ADAM_B2, ADAM_EPS, ADAM_WD, ADAM_STEP = 0.001, 0.9, 0.999, 1e-08, 0.01, 10
N_CHIPS, N_CORES, N_DEV = 4, 2, 8

VMEM_LIMIT_BYTES = 56 * 2**20
TOKEN_TILE = 512
FUSED_TILE = 256
FUSED_ROW_BLOCK = 128
FFN_ROW_BLOCK = 256
FFN_WIDE_TILE = 1024
ATTN_TILE = 512
STAT_LANES = 128
SCAN_TIME_TILE = 1024
SCAN_LANE_TILE = 512
SUBLANES = 8


def _params(n_axes):
    return pltpu.CompilerParams(dimension_semantics=("arbitrary",) * n_axes, vmem_limit_bytes=VMEM_LIMIT_BYTES)


def _first(axes):
    cond = None
    for a in axes:
        c = pl.program_id(a) == 0
        cond = c if cond is None else jnp.logical_and(cond, c)
    return cond


def mm(name, xs, ws, out_dtypes, *, trans=False, adds=None, tm=TOKEN_TILE):
    rows = xs[0].shape[0]
    tm = min(tm, rows)
    n_in, n_out = len(xs), len(out_dtypes)
    pairs = [(i, j) for i in range(n_in) for j in range(n_out) if ws[i][j] is not None]
    w_list = [ws[i][j] for (i, j) in pairs]
    adds = list(adds) if adds is not None else [None] * n_out
    add_list = [a for a in adds if a is not None]
    out_cols = [None] * n_out
    for (i, j), w in zip(pairs, w_list):
        out_cols[j] = w.shape[0] if trans else w.shape[1]
    contract = (((1,), (1 if trans else 0,)), ((), ()))

    def body(*refs):
        x_refs = refs[:n_in]
        w_refs = refs[n_in:n_in + len(pairs)]
        a_refs = list(refs[n_in + len(pairs):n_in + len(pairs) + len(add_list)])
        o_refs = refs[n_in + len(pairs) + len(add_list):]
        xb = [None] * n_in
        for j in range(n_out):
            acc = None
            for p, (i, jj) in enumerate(pairs):
                if jj != j:
                    continue
                if xb[i] is None:
                    xb[i] = x_refs[i][...].astype(BF16)
                d = lax.dot_general(xb[i], w_refs[p][...].astype(BF16), contract, preferred_element_type=F32)
                acc = d if acc is None else acc + d
            if adds[j] is not None:
                acc = acc + a_refs.pop(0)[...].astype(F32)
            o_refs[j][...] = acc.astype(o_refs[j].dtype)

    in_specs = ([pl.BlockSpec((tm, x.shape[1]), lambda i: (i, 0)) for x in xs]
                + [pl.BlockSpec(w.shape, lambda i: (0, 0)) for w in w_list]
                + [pl.BlockSpec((tm, a.shape[1]), lambda i: (i, 0)) for a in add_list])
    outs = pl.pallas_call(
        body, name=name, grid=(rows // tm,), in_specs=in_specs,
        out_specs=[pl.BlockSpec((tm, n), lambda i: (i, 0)) for n in out_cols],
        out_shape=[jax.ShapeDtypeStruct((rows, n), dt) for n, dt in zip(out_cols, out_dtypes)],
        compiler_params=_params(1),
    )(*xs, *w_list, *add_list)
    return list(outs)


def mm_tn(name, xs, dys, pairs, *, tm=TOKEN_TILE):
    rows = xs[0].shape[0]
    tm = min(tm, rows)
    n_x, n_dy = len(xs), len(dys)
    contract = (((0,), (0,)), ((), ()))

    def body(*refs):
        x_refs, dy_refs, o_refs = refs[:n_x], refs[n_x:n_x + n_dy], refs[n_x + n_dy:]
        @pl.when(pl.program_id(0) == 0)
        def _():
            for o in o_refs:
                o[...] = jnp.zeros_like(o)

        for k, (i, js) in enumerate(pairs):
            dy = None
            for j in js:
                t = dy_refs[j][...].astype(F32)
                dy = t if dy is None else dy + t
            o_refs[k][...] += lax.dot_general(x_refs[i][...].astype(BF16), dy.astype(BF16), contract,
                                              preferred_element_type=F32)

    shapes = [(xs[i].shape[1], dys[js[0]].shape[1]) for (i, js) in pairs]
    outs = pl.pallas_call(
        body, name=name, grid=(rows // tm,),
        in_specs=[pl.BlockSpec((tm, a.shape[1]), lambda i: (i, 0)) for a in (*xs, *dys)],
        out_specs=[pl.BlockSpec(s, lambda i: (0, 0)) for s in shapes],
        out_shape=[jax.ShapeDtypeStruct(s, F32) for s in shapes],
        compiler_params=_params(1),
    )(*xs, *dys)
    return list(outs)


class In:
    def __init__(self, arr, block=None, imap=None, *, diff=False, acc=None, grad=None, weight=False, rows=False):
        self.arr, self.block, self.imap, self.diff, self.acc, self.grad = arr, block, imap, diff, acc, grad
        self.weight, self.rows = weight, rows

    def spec(self):
        return pl.BlockSpec(memory_space=pltpu.VMEM) if self.weight else pl.BlockSpec(self.block, self.imap)


class Out:
    def __init__(self, shape, dtype, block, imap):
        self.shape, self.dtype, self.block, self.imap = shape, dtype, block, imap

    def spec(self):
        return pl.BlockSpec(self.block, self.imap)


class Wt:
    def __init__(self, ref, zeros=None):
        self.ref, self.zeros = ref, zeros


@jax.custom_vjp
def _wdot(a, w, z):
    return jnp.dot(a.astype(BF16), w, preferred_element_type=F32)


def _wdot_fwd(a, w, z):
    return _wdot(a, w, z), (a, w)


def _wdot_bwd(res, g):
    a, w = res
    gb = g.astype(BF16)
    da = lax.dot_general(gb, w, (((1,), (1,)), ((), ())), preferred_element_type=F32)
    dz = lax.dot_general(a.astype(BF16), gb, (((0,), (0,)), ((), ())), preferred_element_type=F32)
    return da, None, dz


_wdot.defvjp(_wdot_fwd, _wdot_bwd)


def wdot(a, wt, head=None):
    w = wt.ref[...] if head is None else wt.ref[head]
    if wt.zeros is None:
        return jnp.dot(a.astype(BF16), w, preferred_element_type=F32)
    return _wdot(a, w, wt.zeros[0 if head is None else head])


def seg_fwd(name, f, grid, ins, outs):
    n_in = len(ins)

    def body(*refs):
        res = f(*[Wt(r) if i.weight else r[...] for i, r in zip(ins, refs[:n_in])])
        for o_ref, r in zip(refs[n_in:], res):
            o_ref[...] = r.astype(o_ref.dtype)

    res = pl.pallas_call(
        body, name=name, grid=grid, in_specs=[i.spec() for i in ins], out_specs=[o.spec() for o in outs],
        out_shape=[jax.ShapeDtypeStruct(o.shape, o.dtype) for o in outs], compiler_params=_params(len(grid)),
    )(*[i.arr for i in ins])
    return list(res)


def seg_bwd(name, f, grid, ins, outs, cts, adds=None, row_block=None):
    n_in, n_ct = len(ins), len(cts)
    grad_idx = [k for k, i in enumerate(ins) if i.diff or i.weight]
    adds = adds or {}
    add_keys = sorted(adds)
    add_list = [adds[k] for k in add_keys]
    heads = {k: (ins[k].arr.shape[0] if ins[k].arr.ndim == 3 else 1) for k in grad_idx if ins[k].weight}
    tile_rows = outs[0].block[0]
    blocks = [None] if row_block is None else [pl.ds(r0, row_block) for r0 in range(0, tile_rows, row_block)]

    def body(*refs):
        in_refs, ct_refs = refs[:n_in], refs[n_in:n_in + n_ct]
        add_refs = dict(zip(add_keys, refs[n_in + n_ct:n_in + n_ct + len(add_list)]))
        g_refs = dict(zip(grad_idx, refs[n_in + n_ct + len(add_list):]))
        for k in grad_idx:
            if ins[k].weight or ins[k].acc is not None:
                @pl.when(_first(range(len(grid)) if ins[k].weight else ins[k].acc))
                def _(k=k):
                    g_refs[k][...] = jnp.zeros_like(g_refs[k])

        for rows in blocks:
            at = lambda ref, sliced: ref[...] if rows is None or not sliced else ref[rows, :]
            vals = [None if i.weight else at(r, i.rows) for i, r in zip(ins, in_refs)]
            primals, owner = [], []
            for k in grad_idx:
                if ins[k].weight:
                    for h in range(heads[k]):
                        primals.append(jnp.zeros(ins[k].arr.shape[-2:], F32))
                        owner.append((k, h))
                else:
                    primals.append(vals[k].astype(F32))
                    owner.append((k, None))

            def g(*dv, vals=vals, owner=owner):
                full = list(vals)
                zeros = {}
                for (k, h), v in zip(owner, dv):
                    if h is None:
                        full[k] = v
                    else:
                        zeros.setdefault(k, []).append(v)
                for k, z in zeros.items():
                    full[k] = Wt(in_refs[k], z)
                return tuple(f(*full))

            _, pull = jax.vjp(g, *primals)
            grads = pull(tuple(at(c, True).astype(F32) for c in ct_refs))
            for (k, h), gr in zip(owner, grads):
                if ins[k].weight:
                    if ins[k].arr.ndim == 3:
                        g_refs[k][h] += gr
                    else:
                        g_refs[k][...] += gr
                    continue
                if k in add_refs:
                    gr = gr + at(add_refs[k], True).astype(F32)
                if ins[k].acc is not None:
                    g_refs[k][...] += gr
                elif rows is None or not ins[k].rows:
                    g_refs[k][...] = gr.astype(g_refs[k].dtype)
                else:
                    g_refs[k][rows, :] = gr.astype(g_refs[k].dtype)

    g_specs, g_shapes = [], []
    for k in grad_idx:
        i = ins[k]
        if i.weight:
            g_specs.append(pl.BlockSpec(memory_space=pltpu.VMEM))
            g_shapes.append(jax.ShapeDtypeStruct(i.arr.shape, F32))
            continue
        shape, block, imap = i.grad if i.grad is not None else (i.arr.shape, i.block, i.imap)
        g_specs.append(pl.BlockSpec(block, imap))
        g_shapes.append(jax.ShapeDtypeStruct(shape, F32))
    in_specs = ([i.spec() for i in ins] + [o.spec() for o in outs]
                + [pl.BlockSpec(ins[k].block, ins[k].imap) for k in add_keys])
    res = pl.pallas_call(
        body, name=name, grid=grid, in_specs=in_specs, out_specs=g_specs, out_shape=g_shapes,
        compiler_params=_params(len(grid)),
    )(*[i.arr for i in ins], *cts, *add_list)
    return list(res)


def _rms(x, g):
    return x * lax.rsqrt(jnp.mean(x * x, axis=-1, keepdims=True) + EPS) * g


@jax.custom_vjp
def _bdot_nt(a, b):
    return lax.dot_general(a.astype(BF16), b.astype(BF16), (((1,), (1,)), ((), ())), preferred_element_type=F32)


def _bdot_nt_fwd(a, b):
    return _bdot_nt(a, b), (a, b)


def _bdot_nt_bwd(res, g):
    a, b = res
    gb = g.astype(BF16)
    da = lax.dot_general(gb, b.astype(BF16), (((1,), (0,)), ((), ())), preferred_element_type=F32)
    db = lax.dot_general(gb, a.astype(BF16), (((0,), (0,)), ((), ())), preferred_element_type=F32)
    return da, db


_bdot_nt.defvjp(_bdot_nt_fwd, _bdot_nt_bwd)


@jax.custom_vjp
def _bdot_nn(a, b):
    return lax.dot_general(a.astype(BF16), b.astype(BF16), (((1,), (0,)), ((), ())), preferred_element_type=F32)


def _bdot_nn_fwd(a, b):
    return _bdot_nn(a, b), (a, b)


def _bdot_nn_bwd(res, g):
    a, b = res
    gb = g.astype(BF16)
    da = lax.dot_general(gb, b.astype(BF16), (((1,), (1,)), ((), ())), preferred_element_type=F32)
    db = lax.dot_general(a.astype(BF16), gb, (((0,), (0,)), ((), ())), preferred_element_type=F32)
    return da, db


_bdot_nn.defvjp(_bdot_nn_fwd, _bdot_nn_bwd)


@functools.partial(jax.custom_vjp, nondiff_argnums=(1,))
def _lane_roll(x, shift):
    return pltpu.roll(x, shift, 1)


def _lane_roll_fwd(x, shift):
    return pltpu.roll(x, shift, 1), None


def _lane_roll_bwd(shift, _, g):
    return (pltpu.roll(g, (g.shape[1] - shift) % g.shape[1], 1),)


_lane_roll.defvjp(_lane_roll_fwd, _lane_roll_bwd)


def _hp_dot(a, b):
    return jnp.dot(a, b, precision=lax.Precision.HIGHEST, preferred_element_type=F32)


NT_DIMS = (((1,), (1,)), ((), ()))
TN_DIMS = (((0,), (0,)), ((), ()))


def ffn_fwd(name, x, gain, wg, wu, wd, target=None, tm=FFN_WIDE_TILE):
    S, D = x.shape
    tm = min(tm, S)
    nsh, Fs, _ = wg.shape
    with_loss = target is not None

    def body(*refs):
        if with_loss:
            x_ref, gain_ref, wg_ref, wu_ref, wd_ref, t_ref, xo_ref, h_ref, g_ref, u_ref, part_ref, acc = refs
        else:
            x_ref, gain_ref, wg_ref, wu_ref, wd_ref, xo_ref, h_ref, g_ref, u_ref, acc = refs
        j = pl.program_id(1)

        @pl.when(j == 0)
        def _():
            h_ref[...] = _rms(x_ref[...], gain_ref[...]).astype(BF16)
            acc[...] = jnp.zeros_like(acc)

        h = h_ref[...]
        g = lax.dot_general(h, wg_ref[...], NT_DIMS, preferred_element_type=F32)
        u = lax.dot_general(h, wu_ref[...], NT_DIMS, preferred_element_type=F32)
        g_ref[...] = g.astype(BF16)
        u_ref[...] = u.astype(BF16)
        a = g * jax.nn.sigmoid(g) * u
        acc[...] += jnp.dot(a.astype(BF16), wd_ref[...], preferred_element_type=F32)

        @pl.when(j == nsh - 1)
        def _():
            y = x_ref[...] + 0.5 * acc[...]
            if with_loss:
                err = y - t_ref[...]
                xo_ref[...] = err * (1.0 / D)
                part_ref[...] = jnp.full(part_ref.shape, 0.5 * jnp.sum(jnp.mean(err * err, axis=-1)), F32)
            else:
                xo_ref[...] = y

    rows = pl.BlockSpec((tm, D), lambda i, j: (i, 0))
    wspec = pl.BlockSpec((None, Fs, D), lambda i, j: (j, 0, 0))
    act = pl.BlockSpec((None, tm, Fs), lambda i, j: (j, i, 0))
    in_specs, args = [rows, pl.BlockSpec((1, D), lambda i, j: (0, 0)), wspec, wspec, wspec], [x, gain, wg, wu, wd]
    out_specs = [rows, rows, act, act]
    out_shape = [jax.ShapeDtypeStruct((S, D), F32), jax.ShapeDtypeStruct((S, D), BF16),
                 jax.ShapeDtypeStruct((nsh, S, Fs), BF16), jax.ShapeDtypeStruct((nsh, S, Fs), BF16)]
    if with_loss:
        in_specs.append(rows)
        args.append(target)
        out_specs.append(pl.BlockSpec((SUBLANES, 128), lambda i, j: (i, 0)))
        out_shape.append(jax.ShapeDtypeStruct((S // tm * SUBLANES, 128), F32))
    return pl.pallas_call(
        body, name=name, grid=(S // tm, nsh), in_specs=in_specs, out_specs=out_specs, out_shape=out_shape,
        scratch_shapes=[pltpu.VMEM((tm, D), F32)], compiler_params=_params(2),
    )(*args)


def ffn_bwd_act(name, dxo, x, gain, g, u, wg, wu, wd, tm=TOKEN_TILE):
    S, D = x.shape
    tm = min(tm, S)
    nsh, Fs, _ = wg.shape

    def body(dxo_ref, x_ref, gain_ref, g_ref, u_ref, wg_ref, wu_ref, wd_ref, dx_ref, dgain_ref, dg_ref, du_ref, dd, dh):
        i, j = pl.program_id(0), pl.program_id(1)

        @pl.when(j == 0)
        def _():
            dd[...] = (0.5 * dxo_ref[...]).astype(BF16)
            dh[...] = jnp.zeros_like(dh)

        for r0 in range(0, tm, FFN_ROW_BLOCK):
            rows = pl.ds(r0, FFN_ROW_BLOCK)
            da = lax.dot_general(dd[rows, :], wd_ref[...], NT_DIMS, preferred_element_type=F32)
            gf, uf = g_ref[rows, :].astype(F32), u_ref[rows, :].astype(F32)
            sig = jax.nn.sigmoid(gf)
            dgv = (da * uf * (sig * (1.0 + gf * (1.0 - sig)))).astype(BF16)
            duv = (da * (gf * sig)).astype(BF16)
            dg_ref[rows, :] = dgv
            du_ref[rows, :] = duv
            dh[rows, :] += (jnp.dot(dgv, wg_ref[...], preferred_element_type=F32)
                            + jnp.dot(duv, wu_ref[...], preferred_element_type=F32))

        @pl.when(j == nsh - 1)
        def _():
            xv = x_ref[...]
            r = lax.rsqrt(jnp.mean(xv * xv, axis=-1, keepdims=True) + EPS)
            xhat = xv * r
            dhv = dh[...]
            dxn = dhv * gain_ref[...]
            dx_ref[...] = dxo_ref[...] + r * (dxn - xhat * jnp.mean(dxn * xhat, axis=-1, keepdims=True))
            part = jnp.sum(dhv * xhat, axis=0, keepdims=True)

            @pl.when(i == 0)
            def _():
                dgain_ref[...] = part

            @pl.when(i != 0)
            def _():
                dgain_ref[...] += part

    return pl.pallas_call(
        body, name=name, grid=(S // tm, nsh),
        in_specs=[pl.BlockSpec((tm, D), lambda i, j: (i, 0)), pl.BlockSpec((tm, D), lambda i, j: (i, 0)),
                  pl.BlockSpec((1, D), lambda i, j: (0, 0)),
                  pl.BlockSpec((None, tm, Fs), lambda i, j: (j, i, 0)), pl.BlockSpec((None, tm, Fs), lambda i, j: (j, i, 0)),
                  pl.BlockSpec((None, Fs, D), lambda i, j: (j, 0, 0)), pl.BlockSpec((None, Fs, D), lambda i, j: (j, 0, 0)),
                  pl.BlockSpec((None, Fs, D), lambda i, j: (j, 0, 0))],
        out_specs=[pl.BlockSpec((tm, D), lambda i, j: (i, 0)), pl.BlockSpec((1, D), lambda i, j: (0, 0)),
                   pl.BlockSpec((None, tm, Fs), lambda i, j: (j, i, 0)), pl.BlockSpec((None, tm, Fs), lambda i, j: (j, i, 0))],
        out_shape=[jax.ShapeDtypeStruct((S, D), F32), jax.ShapeDtypeStruct((1, D), F32),
                   jax.ShapeDtypeStruct((nsh, S, Fs), BF16), jax.ShapeDtypeStruct((nsh, S, Fs), BF16)],
        scratch_shapes=[pltpu.VMEM((tm, D), BF16), pltpu.VMEM((tm, D), F32)], compiler_params=_params(2),
    )(dxo, x, gain, g, u, wg, wu, wd)


def ffn_bwd_w(name, h, dxo, g, u, dg, du, tm=FFN_WIDE_TILE):
    S, D = h.shape
    tm = min(tm, S)
    nsh, _, Fs = g.shape

    def body(h_ref, dxo_ref, g_ref, u_ref, dg_ref, du_ref, dwg_ref, dwu_ref, dwd_ref):
        i = pl.program_id(1)
        gf, uf = g_ref[...].astype(F32), u_ref[...].astype(F32)
        a = (gf * jax.nn.sigmoid(gf) * uf).astype(BF16)
        dd = (0.5 * dxo_ref[...]).astype(BF16)
        hv = h_ref[...]

        @pl.when(i == 0)
        def _():
            dwg_ref[...] = jnp.zeros_like(dwg_ref)
            dwu_ref[...] = jnp.zeros_like(dwu_ref)
            dwd_ref[...] = jnp.zeros_like(dwd_ref)

        dwg_ref[...] += lax.dot_general(dg_ref[...], hv, TN_DIMS, preferred_element_type=F32)
        dwu_ref[...] += lax.dot_general(du_ref[...], hv, TN_DIMS, preferred_element_type=F32)
        dwd_ref[...] += lax.dot_general(a, dd, TN_DIMS, preferred_element_type=F32)

    act = pl.BlockSpec((None, tm, Fs), lambda j, i: (j, i, 0))
    wspec = pl.BlockSpec((None, Fs, D), lambda j, i: (j, 0, 0))
    return pl.pallas_call(
        body, name=name, grid=(nsh, S // tm),
        in_specs=[pl.BlockSpec((tm, D), lambda j, i: (i, 0)), pl.BlockSpec((tm, D), lambda j, i: (i, 0)), act, act, act, act],
        out_specs=[wspec, wspec, wspec], out_shape=[jax.ShapeDtypeStruct((nsh, Fs, D), F32)] * 3,
        compiler_params=_params(2),
    )(h, dxo, g, u, dg, du)


NEG_BIG = -1e30


def _causal_pairs(n, by_key):
    pairs = [(qi, ki) for qi in range(n) for ki in range(qi + 1)]
    if by_key:
        pairs.sort(key=lambda p: (p[1], p[0]))
    return jnp.asarray([p[0] for p in pairs], jnp.int32), jnp.asarray([p[1] for p in pairs], jnp.int32)


def _scores(q, k, masked):
    s = lax.dot_general(q, k, (((1,), (1,)), ((), ())), preferred_element_type=F32)
    if masked:
        row = lax.broadcasted_iota(jnp.int32, s.shape, 0)
        col = lax.broadcasted_iota(jnp.int32, s.shape, 1)
        s = jnp.where(row >= col, s, NEG_BIG)
    return s


def attn_fwd(name, q, k, v, t=ATTN_TILE):
    S, Dk = q.shape[0], HEAD_PAD
    H = q.shape[1] // Dk
    Dv = v.shape[1] // H
    qt, kt = _causal_pairs(S // t, by_key=False)

    def body(qt_ref, kt_ref, q_ref, k_ref, v_ref, o_ref, lse_ref, m_sc, l_sc, acc):
        qi, ki = qt_ref[pl.program_id(1)], kt_ref[pl.program_id(1)]

        @pl.when(ki == 0)
        def _():
            m_sc[...] = jnp.full_like(m_sc, NEG_BIG)
            l_sc[...] = jnp.zeros_like(l_sc)
            acc[...] = jnp.zeros_like(acc)

        def step(masked):
            s = _scores(q_ref[...], k_ref[...], masked)
            m_prev = m_sc[...]
            m_next = jnp.maximum(m_prev, jnp.max(s, axis=-1, keepdims=True))
            alpha = jnp.exp(m_prev - m_next)
            p = jnp.exp(s - jnp.tile(m_next, (1, t // STAT_LANES)))
            l_sc[...] = alpha * l_sc[...] + jnp.sum(p, axis=-1, keepdims=True)
            acc[...] = alpha * acc[...] + jnp.dot(p.astype(BF16), v_ref[...].astype(BF16), preferred_element_type=F32)
            m_sc[...] = m_next

        @pl.when(ki < qi)
        def _():
            step(False)

        @pl.when(ki == qi)
        def _():
            step(True)
            o_ref[...] = acc[...] / l_sc[...]
            lse_ref[...] = m_sc[...] + jnp.log(l_sc[...])

    stat = pltpu.VMEM((t, STAT_LANES), F32)
    return pl.pallas_call(
        body, name=name,
        grid_spec=pltpu.PrefetchScalarGridSpec(
            num_scalar_prefetch=2, grid=(H, qt.shape[0]),
            in_specs=[pl.BlockSpec((t, Dk), lambda h, s, qt, kt: (qt[s], h)),
                      pl.BlockSpec((t, Dk), lambda h, s, qt, kt: (kt[s], h)),
                      pl.BlockSpec((t, Dv), lambda h, s, qt, kt: (kt[s], h))],
            out_specs=[pl.BlockSpec((t, Dv), lambda h, s, qt, kt: (qt[s], h)),
                       pl.BlockSpec((None, t, STAT_LANES), lambda h, s, qt, kt: (h, qt[s], 0))],
            scratch_shapes=[stat, stat, pltpu.VMEM((t, Dv), F32)]),
        out_shape=[jax.ShapeDtypeStruct((S, H * Dv), F32), jax.ShapeDtypeStruct((H, S, STAT_LANES), F32)],
        compiler_params=_params(2),
    )(qt, kt, q, k, v)


def attn_bwd(name, q, k, v, do, o, lse, t=ATTN_TILE):
    S, Dk = q.shape[0], HEAD_PAD
    H = q.shape[1] // Dk
    Dv = v.shape[1] // H
    qt, kt = _causal_pairs(S // t, by_key=True)
    tn_dims = (((0,), (0,)), ((), ()))

    def body(qt_ref, kt_ref, q_ref, k_ref, v_ref, do_ref, o_ref, lse_ref, dq_ref, dk_ref, dv_ref):
        step_id = pl.program_id(1)
        qi, ki = qt_ref[step_id], kt_ref[step_id]

        @pl.when(step_id == 0)
        def _():
            dq_ref[...] = jnp.zeros_like(dq_ref)

        def step(masked):
            s = _scores(q_ref[...], k_ref[...], masked)
            reps = (1, t // STAT_LANES)
            p = jnp.exp(s - jnp.tile(lse_ref[...], reps))
            dov = do_ref[...]
            delta = jnp.broadcast_to(jnp.sum(dov * o_ref[...], axis=-1, keepdims=True), (t, STAT_LANES))
            dob = dov.astype(BF16)
            dp = lax.dot_general(dob, v_ref[...].astype(BF16), (((1,), (1,)), ((), ())), preferred_element_type=F32)
            ds = (p * (dp - jnp.tile(delta, reps))).astype(BF16)
            pdv = lax.dot_general(p.astype(BF16), dob, tn_dims, preferred_element_type=F32)
            pdk = lax.dot_general(ds, q_ref[...], tn_dims, preferred_element_type=F32)
            rows = pl.ds(pl.multiple_of(qi * t, t), t)
            dq_ref[rows, :] += jnp.dot(ds, k_ref[...], preferred_element_type=F32)
            return pdk, pdv

        @pl.when(ki == qi)
        def _():
            dk_ref[...] = jnp.zeros_like(dk_ref)
            dv_ref[...] = jnp.zeros_like(dv_ref)

        def accumulate(masked):
            pdk, pdv = step(masked)
            dk_ref[...] += pdk
            dv_ref[...] += pdv

        @pl.when(ki == qi)
        def _():
            accumulate(True)

        @pl.when(ki < qi)
        def _():
            accumulate(False)

    qrow = lambda h, s, qt, kt: (qt[s], h)
    krow = lambda h, s, qt, kt: (kt[s], h)
    return pl.pallas_call(
        body, name=name,
        grid_spec=pltpu.PrefetchScalarGridSpec(
            num_scalar_prefetch=2, grid=(H, qt.shape[0]),
            in_specs=[pl.BlockSpec((t, Dk), qrow), pl.BlockSpec((t, Dk), krow), pl.BlockSpec((t, Dv), krow),
                      pl.BlockSpec((t, Dv), qrow), pl.BlockSpec((t, Dv), qrow),
                      pl.BlockSpec((None, t, STAT_LANES), lambda h, s, qt, kt: (h, qt[s], 0))],
            out_specs=[pl.BlockSpec((S, Dk), lambda h, s, qt, kt: (0, h)), pl.BlockSpec((t, Dk), krow),
                       pl.BlockSpec((t, Dv), krow)]),
        out_shape=[jax.ShapeDtypeStruct((S, H * Dk), F32), jax.ShapeDtypeStruct((S, H * Dk), F32),
                   jax.ShapeDtypeStruct((S, H * Dv), F32)],
        compiler_params=_params(2),
    )(qt, kt, q, k, v, do, o, lse)


def _cmul(ar, ai, br, bi):
    return ar * br - ai * bi, ar * bi + ai * br


def _scan_tile(x_r, x_i, ar_ref, ai_ref, cr_sc, ci_sc, *, reverse, first, states=None):
    tc, lanes = x_r.shape
    nblk, lb = tc // SUBLANES, SCAN_LANE_TILE
    with_da = states is not None
    if with_da:
        xr_all, xi_all, pr_all, pi_all, dar_all, dai_all, chunk = states

    @pl.when(first)
    def _():
        cr_sc[...] = jnp.zeros_like(cr_sc)
        ci_sc[...] = jnp.zeros_like(ci_sc)
        if with_da:
            dar_all[...] = jnp.zeros_like(dar_all)
            dai_all[...] = jnp.zeros_like(dai_all)

    row = lax.broadcasted_iota(jnp.int32, (SUBLANES, lb), 0)
    for l0 in range(0, lanes, lb):
        _scan_lanes(x_r.at[:, pl.ds(l0, lb)], x_i.at[:, pl.ds(l0, lb)], ar_ref[0:1, pl.ds(l0, lb)],
                    ai_ref[0:1, pl.ds(l0, lb)], cr_sc.at[:, pl.ds(l0, lb)], ci_sc.at[:, pl.ds(l0, lb)], row, reverse,
                    nblk, None if not with_da else tuple(r.at[:, pl.ds(l0, lb)] for r in states[:6]) + (chunk,))


def _scan_lanes(x_r, x_i, a1r, a1i, cr_sc, ci_sc, row, reverse, nblk, states):
    lb = x_r.shape[1]
    with_da = states is not None
    if with_da:
        xr_ref, xi_ref, pr_ref, pi_ref, dar_ref, dai_ref, chunk = states
    if reverse:
        a1i = -a1i
    a2r, a2i = _cmul(a1r, a1i, a1r, a1i)
    a4r, a4i = _cmul(a2r, a2i, a2r, a2i)
    pw_r, pw_i = jnp.zeros((SUBLANES, lb), F32), jnp.zeros((SUBLANES, lb), F32)
    cur_r, cur_i = a1r, a1i
    for e in range(SUBLANES):
        r_at = (SUBLANES - 1 - e) if reverse else e
        pw_r = jnp.where(row == r_at, cur_r, pw_r)
        pw_i = jnp.where(row == r_at, cur_i, pw_i)
        cur_r, cur_i = _cmul(cur_r, cur_i, a1r, a1i)
    steps = []
    for d, pr, pi in ((1, a1r, a1i), (2, a2r, a2i), (4, a4r, a4i)):
        keep = (row < SUBLANES - d) if reverse else (row >= d)
        steps.append((d, jnp.where(keep, pr, 0.0), jnp.where(keep, pi, 0.0)))

    def block(jb, carry):
        if with_da:
            cr, ci, acc_r, acc_i = carry
        else:
            cr, ci = carry
        idx = (nblk - 1 - jb) if reverse else jb
        r0 = pl.multiple_of(idx * SUBLANES, SUBLANES)
        xr = x_r[pl.ds(r0, SUBLANES), :]
        xi = x_i[pl.ds(r0, SUBLANES), :]
        for d, pr, pi in steps:
            shift = SUBLANES - d if reverse else d
            sr, si = pltpu.roll(xr, shift, 0), pltpu.roll(xi, shift, 0)
            xr, xi = xr + pr * sr - pi * si, xi + pr * si + pi * sr
        xr, xi = xr + pw_r * cr - pw_i * ci, xi + pw_r * ci + pw_i * cr
        x_r[pl.ds(r0, SUBLANES), :] = xr
        x_i[pl.ds(r0, SUBLANES), :] = xi
        edge = 0 if reverse else SUBLANES - 1
        cr, ci = xr[edge:edge + 1, :], xi[edge:edge + 1, :]
        if not with_da:
            return cr, ci
        fr = xr_ref[pl.ds(r0, SUBLANES), :]
        fi = xi_ref[pl.ds(r0, SUBLANES), :]
        rp = pl.multiple_of(jnp.maximum(idx - 1, 0) * SUBLANES, SUBLANES)
        inside = idx > 0
        before_r = jnp.where(inside, xr_ref[pl.ds(rp, SUBLANES), :], pr_ref[...])
        before_i = jnp.where(inside, xi_ref[pl.ds(rp, SUBLANES), :], pi_ref[...])
        live = jnp.where(jnp.logical_or(inside, chunk > 0), 1.0, 0.0)
        last_r = before_r[SUBLANES - 1:SUBLANES, :] * live
        last_i = before_i[SUBLANES - 1:SUBLANES, :] * live
        pvr = jnp.where(row == 0, last_r, pltpu.roll(fr, 1, 0))
        pvi = jnp.where(row == 0, last_i, pltpu.roll(fi, 1, 0))
        acc_r = acc_r + xr * pvr + xi * pvi
        acc_i = acc_i + xi * pvr - xr * pvi
        return cr, ci, acc_r, acc_i

    init = (cr_sc[...], ci_sc[...])
    if with_da:
        init = init + (jnp.zeros((SUBLANES, lb), F32), jnp.zeros((SUBLANES, lb), F32))
    fin = lax.fori_loop(0, nblk, block, init)
    cr_sc[...] = fin[0]
    ci_sc[...] = fin[1]
    if with_da:
        dar_ref[...] += fin[2]
        dai_ref[...] += fin[3]


SSM_BLOCKS = 4
BLOCK_CH = SSM_WIDTH // SSM_BLOCKS
PREP_LANES = SSM_LANES // SSM_BLOCKS


def ssm_fwd(name, u, bb_r, bb_i, cb_r, ncb_i, a_r8, a_i8):
    S = u.shape[0]
    tc = min(SCAN_TIME_TILE, S)

    def body(u_ref, bbr_ref, bbi_ref, cbr_ref, ncbi_ref, ar_ref, ai_ref, xr_ref, xi_ref, y_ref, cr_sc, ci_sc):
        ub = u_ref[...].astype(BF16)
        xr_ref[...] = jnp.dot(ub, bbr_ref[...], preferred_element_type=F32)
        xi_ref[...] = jnp.dot(ub, bbi_ref[...], preferred_element_type=F32)
        _scan_tile(xr_ref, xi_ref, ar_ref, ai_ref, cr_sc, ci_sc, reverse=False, first=pl.program_id(1) == 0)
        y_ref[...] = (lax.dot_general(xr_ref[...].astype(BF16), cbr_ref[...], NT_DIMS, preferred_element_type=F32)
                      + lax.dot_general(xi_ref[...].astype(BF16), ncbi_ref[...], NT_DIMS, preferred_element_type=F32))

    ch = pl.BlockSpec((tc, BLOCK_CH), lambda b, t: (t, b))
    st = pl.BlockSpec((tc, PREP_LANES), lambda b, t: (t, b))
    wt = pl.BlockSpec((None, BLOCK_CH, PREP_LANES), lambda b, t: (b, 0, 0))
    par = pl.BlockSpec((SUBLANES, PREP_LANES), lambda b, t: (0, b))
    return pl.pallas_call(
        body, name=name, grid=(SSM_BLOCKS, S // tc), in_specs=[ch, wt, wt, wt, wt, par, par], out_specs=[st, st, ch],
        out_shape=[jax.ShapeDtypeStruct((S, SSM_LANES), F32), jax.ShapeDtypeStruct((S, SSM_LANES), F32),
                   jax.ShapeDtypeStruct((S, SSM_WIDTH), F32)],
        scratch_shapes=[pltpu.VMEM((1, PREP_LANES), F32), pltpu.VMEM((1, PREP_LANES), F32)], compiler_params=_params(2),
    )(u, bb_r, bb_i, cb_r, ncb_i, a_r8, a_i8)


def ssm_bwd(name, dy, du_add, u, xs_r, xs_i, bb_r, bb_i, cb_r, ncb_i, a_r8, a_i8):
    S = u.shape[0]
    tc = min(SCAN_TIME_TILE, S)
    nt = S // tc

    def body(dy_ref, dua_ref, u_ref, xr_ref, xi_ref, pr_ref, pi_ref, bbr_ref, bbi_ref, cbr_ref, ncbi_ref, ar_ref, ai_ref,
             du_ref, dbbr_ref, dbbi_ref, dcbr_ref, dncbi_ref, dar_ref, dai_ref, lr_sc, li_sc, cr_sc, ci_sc):
        t = pl.program_id(1)
        first = t == 0

        @pl.when(first)
        def _():
            for r in (dbbr_ref, dbbi_ref, dcbr_ref, dncbi_ref):
                r[...] = jnp.zeros_like(r)

        dyb = dy_ref[...].astype(BF16)
        lr_sc[...] = jnp.dot(dyb, cbr_ref[...], preferred_element_type=F32)
        li_sc[...] = jnp.dot(dyb, ncbi_ref[...], preferred_element_type=F32)
        _scan_tile(lr_sc, li_sc, ar_ref, ai_ref, cr_sc, ci_sc, reverse=True, first=first,
                   states=(xr_ref, xi_ref, pr_ref, pi_ref, dar_ref, dai_ref, nt - 1 - t))
        lrb, lib = lr_sc[...].astype(BF16), li_sc[...].astype(BF16)
        du_ref[...] = (lax.dot_general(lrb, bbr_ref[...], NT_DIMS, preferred_element_type=F32)
                       + lax.dot_general(lib, bbi_ref[...], NT_DIMS, preferred_element_type=F32) + dua_ref[...])
        ub = u_ref[...].astype(BF16)
        dbbr_ref[...] += lax.dot_general(ub, lrb, TN_DIMS, preferred_element_type=F32)
        dbbi_ref[...] += lax.dot_general(ub, lib, TN_DIMS, preferred_element_type=F32)
        dcbr_ref[...] += lax.dot_general(dyb, xr_ref[...].astype(BF16), TN_DIMS, preferred_element_type=F32)
        dncbi_ref[...] += lax.dot_general(dyb, xi_ref[...].astype(BF16), TN_DIMS, preferred_element_type=F32)

    ch = pl.BlockSpec((tc, BLOCK_CH), lambda b, t: (nt - 1 - t, b))
    st = pl.BlockSpec((tc, PREP_LANES), lambda b, t: (nt - 1 - t, b))
    prev = pl.BlockSpec((SUBLANES, PREP_LANES), lambda b, t: (jnp.maximum((nt - 1 - t) * (tc // SUBLANES) - 1, 0), b))
    wt = pl.BlockSpec((None, BLOCK_CH, PREP_LANES), lambda b, t: (b, 0, 0))
    par = pl.BlockSpec((SUBLANES, PREP_LANES), lambda b, t: (0, b))
    blk = jax.ShapeDtypeStruct((SSM_BLOCKS, BLOCK_CH, PREP_LANES), F32)
    rows8 = jax.ShapeDtypeStruct((SUBLANES, SSM_LANES), F32)
    return pl.pallas_call(
        body, name=name, grid=(SSM_BLOCKS, nt), in_specs=[ch, ch, ch, st, st, prev, prev, wt, wt, wt, wt, par, par],
        out_specs=[ch, wt, wt, wt, wt, par, par],
        out_shape=[jax.ShapeDtypeStruct((S, SSM_WIDTH), F32), blk, blk, blk, blk, rows8, rows8],
        scratch_shapes=[pltpu.VMEM((tc, PREP_LANES), F32), pltpu.VMEM((tc, PREP_LANES), F32),
                        pltpu.VMEM((1, PREP_LANES), F32), pltpu.VMEM((1, PREP_LANES), F32)],
        compiler_params=_params(2),
    )(dy, du_add, u, xs_r, xs_i, xs_r, xs_i, bb_r, bb_i, cb_r, ncb_i, a_r8, a_i8)


def _ssm_prep_f(a_re, a_im, log_dt, bt_re, bt_im, c_re, c_im):
    first_group = pl.program_id(0) * (SSM_GROUPS // SSM_BLOCKS)
    iota = lambda shape, d: lax.broadcasted_iota(jnp.int32, shape, d)
    grp_of_row = lambda shape: iota(shape, 0) >> int(math.log2(SSM_GROUP))
    grp_of_lane = lambda shape: iota(shape, 1) >> int(math.log2(SSM_STATE))
    rep = (grp_of_row((BLOCK_CH, SSM_GROUPS)) + first_group == iota((BLOCK_CH, SSM_GROUPS), 1)).astype(F32)
    til = ((iota((SSM_STATE, PREP_LANES), 1) & (SSM_STATE - 1)) == iota((SSM_STATE, PREP_LANES), 0)).astype(F32)
    m_rows = (grp_of_row((BLOCK_CH, PREP_LANES)) == grp_of_lane((BLOCK_CH, PREP_LANES))).astype(F32)
    m_grp = (iota((SSM_GROUPS, PREP_LANES), 0) == grp_of_lane((SSM_GROUPS, PREP_LANES)) + first_group).astype(F32)
    dt = jnp.exp(log_dt)
    decay = jnp.exp(a_re * dt)
    ar = decay * jnp.cos(a_im * dt)
    ai = decay * jnp.sin(a_im * dt)
    den = a_re * a_re + a_im * a_im
    nr = ar - 1.0
    coef_r = (nr * a_re + ai * a_im) / den
    coef_i = (ai * a_re - nr * a_im) / den
    cr, ci = _hp_dot(rep, coef_r), _hp_dot(rep, coef_i)
    bb_r = cr * bt_re - ci * bt_im
    bb_i = cr * bt_im + ci * bt_re
    big = lambda m: _hp_dot(m, til) * m_rows
    lanes = lambda m: jnp.broadcast_to(jnp.sum(_hp_dot(m, til) * m_grp, axis=0, keepdims=True), (SUBLANES, PREP_LANES))
    return lanes(ar), lanes(ai), big(bb_r), big(bb_i), big(c_re), -big(c_im)


def _whole(arr, **kw):
    nd = arr.ndim
    return In(arr, arr.shape, lambda *_: (0,) * nd, **kw)


def _adamw_math(w, g, m, v):
    m = ADAM_B1 * m + (1.0 - ADAM_B1) * g
    v = ADAM_B2 * v + (1.0 - ADAM_B2) * (g * g)
    m_hat = m / (1.0 - ADAM_B1 ** ADAM_STEP)
    v_hat = v / (1.0 - ADAM_B2 ** ADAM_STEP)
    delta = -ADAM_LR * (m_hat / (jnp.sqrt(v_hat) + ADAM_EPS) + ADAM_WD * w)
    return delta, m, v


def adamw_big(name, w, g, m, v, after):
    R, C = w.shape
    tr = R
    for cand in (512, 344, 256, 128):
        if R % cand == 0:
            tr = cand
            break

    def body(w_ref, g_ref, m_ref, v_ref, after_ref, d_ref, nm_ref, nv_ref):
        d, nm, nv = _adamw_math(w_ref[...], g_ref[...], m_ref[...], v_ref[...])
        d_ref[...] = d
        nm_ref[...] = nm
        nv_ref[...] = nv

    spec = pl.BlockSpec((tr, C), lambda i: (i, 0))
    return pl.pallas_call(
        body, name=name, grid=(R // tr,), in_specs=[spec] * 4 + [pl.BlockSpec(memory_space=pl.ANY)], out_specs=[spec] * 3,
        out_shape=[jax.ShapeDtypeStruct((R, C), F32)] * 3, compiler_params=_params(1),
    )(w, g, m, v, after)


def adamw_small(name, ws, gs, ms, vs):
    n = len(ws)

    def body(*refs):
        for k in range(n):
            d, nm, nv = _adamw_math(refs[k][...], refs[n + k][...], refs[2 * n + k][...], refs[3 * n + k][...])
            refs[4 * n + k][...] = d
            refs[5 * n + k][...] = nm
            refs[6 * n + k][...] = nv

    vm = pl.BlockSpec(memory_space=pltpu.VMEM)
    shapes = [jax.ShapeDtypeStruct(w.shape, F32) for w in ws]
    res = pl.pallas_call(
        body, name=name, in_specs=[vm] * (4 * n), out_specs=[vm] * (3 * n), out_shape=shapes * 3,
        compiler_params=pltpu.CompilerParams(vmem_limit_bytes=VMEM_LIMIT_BYTES),
    )(*ws, *gs, *ms, *vs)
    return res[:n], res[n:2 * n], res[2 * n:]


def _place():
    return lax.axis_index("x"), lax.axis_index("y"), lax.axis_index("c")


def _other_chips(x, y):
    return [(1 - x, y), (x, 1 - y), (1 - x, 1 - y)]


HBM = pl.BlockSpec(memory_space=pl.ANY)


def all_gather_halves(name, blocks):
    n = len(blocks)

    def body(*refs):
        in_refs, out_refs = refs[:n], refs[n:2 * n]
        send_sems, recv_sems = refs[2 * n:]
        x, y, c = _place()
        me, sibling = (x, y, c), (x, y, 1 - c)
        chips = _other_chips(x, y)

        def slot(a, px, py, pc):
            return out_refs[a].at[4 * px + 2 * py + pc]

        def copy(a, k, block, to, src=None):
            return pltpu.make_async_remote_copy(
                src_ref=slot(a, *block) if src is None else src, dst_ref=slot(a, *block),
                send_sem=send_sems.at[a, k], recv_sem=recv_sems.at[a, k], device_id=to, device_id_type=MESH)

        first = []
        for a in range(n):
            first.append(copy(a, 0, me, sibling, src=in_refs[a]))
            first += [copy(a, 1 + j, me, (*chip, c), src=in_refs[a]) for j, chip in enumerate(chips)]
        for cp in first:
            cp.start()
        passed = []
        for j, chip in enumerate(chips):
            for a in range(n):
                copy(a, 1 + j, (*chip, c), me).wait_recv()
                fw = copy(a, 4 + j, (*chip, c), sibling)
                fw.start()
                passed.append(fw)
        for a in range(n):
            copy(a, 0, sibling, me).wait_recv()
            for j, chip in enumerate(chips):
                copy(a, 4 + j, (*chip, 1 - c), me).wait_recv()
        for cp in first + passed:
            cp.wait_send()

    res = pl.pallas_call(
        body, name=name, in_specs=[HBM] * n, out_specs=[HBM] * n,
        out_shape=[jax.ShapeDtypeStruct((N_DEV,) + b.shape, b.dtype) for b in blocks],
        scratch_shapes=[pltpu.SemaphoreType.DMA((n, 7)), pltpu.SemaphoreType.DMA((n, 7))],
    )(*blocks)
    return list(res)


def pair_send_halves(name, grads):
    n = len(grads)

    def body(*refs):
        in_refs, out_refs = refs[:n], refs[n:2 * n]
        send_sems, recv_sems = refs[2 * n:]
        x, y, c = _place()
        cps = []
        for a in range(n):
            cp = pltpu.make_async_remote_copy(
                src_ref=in_refs[a].at[:, 1 - c], dst_ref=out_refs[a], send_sem=send_sems.at[a], recv_sem=recv_sems.at[a],
                device_id=(x, y, 1 - c), device_id_type=MESH)
            cp.start()
            cps.append(cp)
        for cp in cps:
            cp.wait()

    res = pl.pallas_call(
        body, name=name, in_specs=[HBM] * n, out_specs=[HBM] * n,
        out_shape=[jax.ShapeDtypeStruct((g.shape[0],) + g.shape[2:], g.dtype) for g in grads],
        scratch_shapes=[pltpu.SemaphoreType.DMA((n,)), pltpu.SemaphoreType.DMA((n,))],
    )(*grads)
    return list(res)


def pair_add(name, grad, got, c_arr):
    nsh, _, M, N = grad.shape
    tr = M
    for cand in (512, 256, 192, 128, 64, 16):
        if M % cand == 0:
            tr = cand
            break

    def body(c_ref, g_ref, p_ref, o_ref):
        o_ref[...] = (g_ref[...] + p_ref[...]).astype(BF16)

    return pl.pallas_call(
        body, name=name,
        grid_spec=pltpu.PrefetchScalarGridSpec(
            num_scalar_prefetch=1, grid=(nsh, M // tr),
            in_specs=[pl.BlockSpec((None, None, tr, N), lambda j, i, c_ref: (j, c_ref[0], i, 0)),
                      pl.BlockSpec((None, tr, N), lambda j, i, c_ref: (j, i, 0))],
            out_specs=pl.BlockSpec((None, tr, N), lambda j, i, c_ref: (j, i, 0))),
        out_shape=jax.ShapeDtypeStruct((nsh, M, N), BF16), compiler_params=_params(2),
    )(c_arr, grad, got)


def scatter_to_chips(name, parts):
    n = len(parts)

    def body(*refs):
        in_refs, out_refs = refs[:n], refs[n:2 * n]
        send_sems, recv_sems = refs[2 * n:]
        x, y, c = _place()
        mine = 2 * x + y
        chips = _other_chips(x, y)
        cps = []
        for a in range(n):
            for k, (px, py) in enumerate(chips):
                cp = pltpu.make_async_remote_copy(
                    src_ref=in_refs[a].at[2 * px + py], dst_ref=out_refs[a].at[mine],
                    send_sem=send_sems.at[a, k], recv_sem=recv_sems.at[a, k], device_id=(px, py, c), device_id_type=MESH)
                cp.start()
                cps.append((cp, a, k, px, py))
        for cp, a, k, px, py in cps:
            pltpu.make_async_remote_copy(
                src_ref=in_refs[a].at[mine], dst_ref=out_refs[a].at[2 * px + py],
                send_sem=send_sems.at[a, k], recv_sem=recv_sems.at[a, k], device_id=(px, py, c), device_id_type=MESH).wait_recv()
        for cp, *_ in cps:
            cp.wait_send()

    res = pl.pallas_call(
        body, name=name, in_specs=[HBM] * n, out_specs=[HBM] * n,
        out_shape=[jax.ShapeDtypeStruct(p.shape, p.dtype) for p in parts],
        scratch_shapes=[pltpu.SemaphoreType.DMA((n, 3)), pltpu.SemaphoreType.DMA((n, 3))],
    )(*parts)
    return list(res)


def sum_chips(name, q):
    nsh, M, N = q.shape
    tr = M
    for cand in (512, 256, 192, 128, 64, 16):
        if M % cand == 0:
            tr = cand
            break

    def body(q_ref, o_ref):
        acc = q_ref[0].astype(F32)
        for j in range(1, nsh):
            acc = acc + q_ref[j].astype(F32)
        o_ref[...] = acc

    return pl.pallas_call(
        body, name=name, grid=(M // tr,), in_specs=[pl.BlockSpec((nsh, tr, N), lambda i: (0, i, 0))],
        out_specs=pl.BlockSpec((tr, N), lambda i: (i, 0)), out_shape=jax.ShapeDtypeStruct((M, N), F32),
        compiler_params=_params(1),
    )(q)


def pair_exchange(name, halves):
    n = len(halves)

    def body(*refs):
        in_refs, out_refs = refs[:n], refs[n:2 * n]
        send_sems, recv_sems = refs[2 * n:]
        x, y, c = _place()
        cps = []
        for a in range(n):
            cp = pltpu.make_async_remote_copy(
                src_ref=in_refs[a], dst_ref=out_refs[a], send_sem=send_sems.at[a], recv_sem=recv_sems.at[a],
                device_id=(x, y, 1 - c), device_id_type=MESH)
            cp.start()
            cps.append(cp)
        for cp in cps:
            cp.wait()

    res = pl.pallas_call(
        body, name=name, in_specs=[HBM] * n, out_specs=[HBM] * n,
        out_shape=[jax.ShapeDtypeStruct(h.shape, h.dtype) for h in halves],
        scratch_shapes=[pltpu.SemaphoreType.DMA((n,)), pltpu.SemaphoreType.DMA((n,))],
    )(*halves)
    return list(res)


SEM = pl.BlockSpec(memory_space=pltpu.SEMAPHORE)
IN_HBM = pl.BlockSpec(memory_space=pltpu.HBM)
SPLIT_COPY = pltpu.CompilerParams(has_side_effects=pltpu.SideEffectType.DATAFLOW_SIDE_EFFECTING)


def _scatter_copies(src_refs, dst_refs, send_sems, recv_sems):
    x, y, c = _place()
    mine = 2 * x + y
    return [pltpu.make_async_remote_copy(
        src_ref=src_refs[a].at[2 * px + py], dst_ref=dst_refs[a].at[mine], send_sem=send_sems.at[a * (N_CHIPS - 1) + k],
        recv_sem=recv_sems.at[a * (N_CHIPS - 1) + k], device_id=(px, py, c), device_id_type=MESH)
        for a in range(len(src_refs)) for k, (px, py) in enumerate(_other_chips(x, y))]


def _pair_copies(src_refs, dst_refs, send_sems, recv_sems):
    x, y, c = _place()
    return [pltpu.make_async_remote_copy(
        src_ref=src_refs[a].at[:, 1 - c], dst_ref=dst_refs[a], send_sem=send_sems.at[a], recv_sem=recv_sems.at[a],
        device_id=(x, y, 1 - c), device_id_type=MESH) for a in range(len(src_refs))]


def _gather_copies(src_refs, dst_refs, send_sems, recv_sems):
    x, y, c = _place()
    me = 4 * x + 2 * y + c
    cps = []
    for a in range(len(src_refs)):
        for k in range(1, N_DEV):
            to = (1 - x if k & 4 else x, 1 - y if k & 2 else y, 1 - c if k & 1 else c)
            s = a * (N_DEV - 1) + k - 1
            cps.append(pltpu.make_async_remote_copy(
                src_ref=src_refs[a], dst_ref=dst_refs[a].at[me], send_sem=send_sems.at[s], recv_sem=recv_sems.at[s],
                device_id=to, device_id_type=MESH))
    return cps


def split_copy_start(name, copies, n_sem, srcs, land_shapes):
    n = len(srcs)
    lands = [lax.empty(s.shape, s.dtype) for s in land_shapes]

    def body(*refs):
        for cp in copies(refs[:n], refs[n:2 * n], refs[2 * n], refs[2 * n + 1]):
            cp.start()
        refs[-1][...] = jnp.zeros_like(refs[-1])

    thru = [pltpu.HBM(a.shape, a.dtype) for a in (*srcs, *lands)]
    res = pl.pallas_call(
        body, name=name, in_specs=[IN_HBM] * (2 * n),
        out_specs=(SEM, SEM, *[IN_HBM] * (2 * n), pl.BlockSpec(memory_space=pltpu.VMEM)),
        out_shape=(pltpu.SemaphoreType.DMA((n * n_sem,)), pltpu.SemaphoreType.DMA((n * n_sem,)), *thru,
                   jax.ShapeDtypeStruct((SUBLANES, LANES), F32)),
        input_output_aliases={i: 2 + i for i in range(2 * n)}, compiler_params=SPLIT_COPY,
    )(*[pltpu.with_memory_space_constraint(a, pltpu.HBM) for a in (*srcs, *lands)])
    return (copies, n, res[0], res[1], res[2:2 + 2 * n]), res[-1][0, 0]


def split_copy_wait(name, handle, after):
    copies, n, send_sems, recv_sems, thru = handle

    def body(*refs):
        for cp in copies(refs[:n], refs[n:2 * n], refs[2 * n], refs[2 * n + 1]):
            cp.wait_send()
            cp.wait_recv()

    res = pl.pallas_call(
        body, name=name, in_specs=[IN_HBM] * (2 * n) + [SEM, SEM, pl.BlockSpec(memory_space=pl.ANY)],
        out_specs=[IN_HBM] * (2 * n), out_shape=[pltpu.HBM(a.shape, a.dtype) for a in thru],
        input_output_aliases={i: i for i in range(2 * n)}, compiler_params=SPLIT_COPY,
    )(*thru, send_sems, recv_sems, after)
    return list(res[:n]), list(res[n:])


def all_reduce_small(name, v):
    R, C = v.shape

    def body(v_ref, o_ref, gath, send_sems, recv_sems):
        x, y, c = _place()
        me, sibling = (x, y, c), (x, y, 1 - c)
        chips = _other_chips(x, y)

        def slot(px, py, pc):
            return gath.at[4 * px + 2 * py + pc]

        def copy(k, block, to, src=None):
            return pltpu.make_async_remote_copy(
                src_ref=slot(*block) if src is None else src, dst_ref=slot(*block),
                send_sem=send_sems.at[k], recv_sem=recv_sems.at[k], device_id=to, device_id_type=MESH)

        first = [copy(0, me, sibling, src=v_ref)]
        first += [copy(1 + j, me, (*chip, c), src=v_ref) for j, chip in enumerate(chips)]
        for cp in first:
            cp.start()
        slot(*me)[...] = v_ref[...]
        passed = [copy(4 + j, (*chip, c), sibling) for j, chip in enumerate(chips)]
        for j, chip in enumerate(chips):
            copy(1 + j, (*chip, c), me).wait_recv()
            passed[j].start()
        copy(0, sibling, me).wait_recv()
        for j, chip in enumerate(chips):
            copy(4 + j, (*chip, 1 - c), me).wait_recv()
        for cp in first + passed:
            cp.wait_send()
        acc = gath[0]
        for d in range(1, N_DEV):
            acc = acc + gath[d]
        o_ref[...] = acc

    vm = pl.BlockSpec(memory_space=pltpu.VMEM)
    return pl.pallas_call(
        body, name=name, in_specs=[vm], out_specs=vm, out_shape=jax.ShapeDtypeStruct((R, C), F32),
        scratch_shapes=[pltpu.VMEM((N_DEV, R, C), F32), pltpu.SemaphoreType.DMA((7,)), pltpu.SemaphoreType.DMA((7,))],
        compiler_params=pltpu.CompilerParams(vmem_limit_bytes=VMEM_LIMIT_BYTES),
    )(v)


LANES = 128
PACK_ROW_MULTIPLE = 1024
SMALL_SHARDED = {
    "w_in": ((D_MODEL, 1216), 1), "mla_w_uq": ((MLA_Q_RANK, 768), 1), "mla_w_ukv": ((MLA_KV_RANK, 1024), 1),
    "ssm_w_glu": ((SSM_WIDTH, SSM_WIDTH), 0), "w_o": ((D_MODEL, D_MODEL), 0), "xattn_w_q": ((D_MODEL, 512), 0),
    "xattn_w_kv": ((D_MODEL, 1024), 0), "xattn_w_o": ((512, D_MODEL), 1),
}
FFN_NAMES = ["ffn1_w_gate", "ffn1_w_up", "ffn1_w_down", "ffn2_w_gate", "ffn2_w_up", "ffn2_w_down"]
TRANSPOSED_VIEW = ("ffn1_w_gate", "ffn1_w_up", "ffn2_w_gate", "ffn2_w_up", "w_in", "mla_w_uq")


def _shard_shape(name):
    (r, cdim), ax = SMALL_SHARDED[name]
    return (r // N_CHIPS, cdim) if ax == 0 else (r, cdim // N_CHIPS)


def _pack_shards(shards):
    parts = []
    for name in SMALL_SHARDED:
        a = shards[name]
        lead = a.shape[:-2]
        parts.append(a.reshape(lead + (a.shape[-2] * a.shape[-1] // LANES, LANES)))
    rows = sum(q.shape[-2] for q in parts)
    parts.append(jnp.zeros(lead + (-rows % PACK_ROW_MULTIPLE, LANES), parts[0].dtype))
    return jnp.concatenate(parts, axis=-2)


def _unpack_shards(packed):
    out, r0 = {}, 0
    lead = packed.shape[:-2]
    for name in SMALL_SHARDED:
        r, cdim = _shard_shape(name)
        rows = r * cdim // LANES
        out[name] = packed[..., r0:r0 + rows, :].reshape(lead + (r, cdim))
        r0 += rows
    return out


def _full_from_shards(name, sh):
    (r, cdim), ax = SMALL_SHARDED[name]
    if ax == 0:
        return sh.reshape(r, cdim)
    return jnp.transpose(sh, (1, 0, 2)).reshape(r, cdim)


def _shards_from_full(name, full):
    (r, cdim), ax = SMALL_SHARDED[name]
    if ax == 0:
        return full.reshape(N_CHIPS, r // N_CHIPS, cdim)
    return jnp.transpose(full.reshape(r, N_CHIPS, cdim // N_CHIPS), (1, 0, 2))


SMALL_REPL = {
    "ffn1_norm": (1, 1024), "mix_norm": (1, 1024), "mla_q_norm": (1, 384), "mla_kv_norm": (1, 256),
    "mla_qk_norm_q": (1, 192), "mla_qk_norm_k": (1, 192), "ssm_a_re": (32, 64), "ssm_a_im": (32, 64),
    "ssm_log_dt": (32, 1), "ssm_b_re": (32, 64, 16), "ssm_b_im": (32, 64, 16), "ssm_c_re": (32, 16, 64),
    "ssm_c_im": (32, 16, 64), "ssm_d": (1, 512), "ssm_b_glu": (1, 512), "out_norm_mla": (1, 512),
    "out_norm_ssm": (1, 512), "xattn_norm": (1, 1024), "mem_norm": (1, 1024), "xattn_q_norm": (1, 128),
    "xattn_k_norm": (1, 128), "ffn2_norm": (1, 1024),
}


def _pack_repl(grads, loss):
    flat = jnp.concatenate([grads[n].reshape(-1) for n in SMALL_REPL] + [loss.reshape(1)])
    rows = -(-flat.shape[0] // (LANES * SUBLANES)) * SUBLANES
    return jnp.pad(flat, (0, rows * LANES - flat.shape[0])).reshape(rows, LANES)


def _unpack_repl(packed):
    flat, out, o = packed.reshape(-1), {}, 0
    for n, shp in SMALL_REPL.items():
        size = int(np.prod(shp))
        out[n] = flat[o:o + size].reshape(shp)
        o += size
    out["loss"] = flat[o]
    return out


def _rope_tables(positions):
    half = MLA_ROPE // 2
    inv = ROPE_THETA ** (-jnp.arange(half, dtype=F32) / half)
    ang = positions.astype(F32)[:, None] * inv[None, :]
    cos, sin = jnp.cos(ang), jnp.sin(ang)
    S = positions.shape[0]
    z = lambda w: jnp.zeros((S, w), F32)
    keep = jnp.concatenate([jnp.ones((S, MLA_NOPE), F32), cos, cos, z(HEAD_PAD - MLA_QK)], axis=1)
    from_hi = jnp.concatenate([z(MLA_NOPE), -sin, z(HEAD_PAD - MLA_NOPE - half)], axis=1)
    from_lo = jnp.concatenate([z(MLA_NOPE + half), sin, z(HEAD_PAD - MLA_QK)], axis=1)
    return keep, from_hi, from_lo


def _norm_rope(x, g, keep, from_hi, from_lo):
    y = x * lax.rsqrt(jnp.sum(x * x, axis=-1, keepdims=True) * (1.0 / MLA_QK) + EPS) * g
    half = MLA_ROPE // 2
    return y * keep + _lane_roll(y, HEAD_PAD - half) * from_hi + _lane_roll(y, half) * from_lo


def local_step(x, mem, positions, target, w, wb, small_weights=None, ffn2_weights=None, on_grads=None):
    if small_weights is None:
        small_weights = lambda after: {}
    if ffn2_weights is None:
        ffn2_weights = lambda after: [wb[k] for k in FFN_NAMES[3:]]
    if on_grads is None:
        on_grads = lambda tag, g: 0.0
    S = x.shape[0]
    tm = min(FUSED_TILE, S)
    g1 = (S // tm,)
    tile = lambda arr, **kw: In(arr, (tm, arr.shape[1]), lambda i: (i, 0), rows=True, **kw)
    par = lambda arr, **kw: In(arr, arr.shape, lambda *_: (0, 0), diff=True, acc=(0,), **kw)
    wt = lambda arr: In(arr, weight=True)
    otile = lambda cols, dt: Out((S, cols), dt, (tm, cols), lambda i: (i, 0))
    grads = {}

    x1, h1, g_1, u_1 = ffn_fwd("ffn1_fwd", x, w["ffn1_norm"], wb["ffn1_w_gate"], wb["ffn1_w_up"], wb["ffn1_w_down"])
    wb = {**wb, **small_weights(x1)}

    keep, from_hi, from_lo = _rope_tables(positions)
    scale = MLA_QK ** -0.5

    def f_pre(xv, kp, fh, fl, g_mix, w_q, w_kv, w_kr, w_u, g_q, g_kv, w_uq, w_ukv, gq, gk):
        h2 = _rms(xv, g_mix)
        cq, ckv = _rms(wdot(h2, w_q), g_q), _rms(wdot(h2, w_kv), g_kv)
        kr = wdot(h2, w_kr)
        qs, ks, vs = [], [], []
        for h in range(MLA_HEADS):
            qs.append(_norm_rope(wdot(cq, w_uq, h), gq, kp, fh, fl) * scale)
            kv = wdot(ckv, w_ukv, h)
            ks.append(_norm_rope(jnp.concatenate([kv[:, :MLA_NOPE], kr], axis=-1), gk, kp, fh, fl))
            vs.append(kv[:, MLA_NOPE:])
        return jnp.concatenate(qs, axis=-1), jnp.concatenate(ks, axis=-1), jnp.concatenate(vs, axis=-1), wdot(h2, w_u)

    def pre_ins(gain_mix):
        return [tile(x1, diff=True), tile(keep), tile(from_hi), tile(from_lo), par(gain_mix),
                wt(wb["w_in_q"]), wt(wb["w_in_kv"]), wt(wb["w_in_kr"]), wt(wb["w_in_u"]),
                par(w["mla_q_norm"]), par(w["mla_kv_norm"]), wt(wb["w_uq"]), wt(wb["w_ukv"]),
                par(w["qk_gain_q"]), par(w["qk_gain_k"])]

    pre_outs = [otile(MLA_HEADS * HEAD_PAD, BF16), otile(MLA_HEADS * HEAD_PAD, BF16), otile(MLA_HEADS * MLA_V, BF16),
                otile(SSM_WIDTH, F32)]
    qh, kh, v0, pu = seg_fwd("pre_mixer_fwd", f_pre, g1, pre_ins(w["mix_norm"]), pre_outs)

    o_mla, lse = attn_fwd("mla_attn_fwd", qh, kh, v0, t=min(ATTN_TILE, S))

    prep_grid = (SSM_BLOCKS,)
    prep_ins = ([_whole(w[k], diff=True, acc=(0,)) for k in ("ssm_a_re", "ssm_a_im", "ssm_log_dt")]
                + [In(w[k], (BLOCK_CH, SSM_STATE), lambda i: (i, 0), diff=True)
                   for k in ("ssm_bt_re", "ssm_bt_im", "ssm_c2_re", "ssm_c2_im")])
    blk_out = Out((SSM_BLOCKS, BLOCK_CH, PREP_LANES), BF16, (None, BLOCK_CH, PREP_LANES), lambda i: (i, 0, 0))
    prep_outs = [Out((SUBLANES, SSM_LANES), F32, (SUBLANES, PREP_LANES), lambda i: (0, i))] * 2 + [blk_out] * 4
    a_r8, a_i8, bb_r, bb_i, cb_r, ncb_i = seg_fwd("ssm_prep_fwd", _ssm_prep_f, prep_grid, prep_ins, prep_outs)

    xs_r, xs_i, y_lin = ssm_fwd("ssm_fwd", pu, bb_r, bb_i, cb_r, ncb_i, a_r8, a_i8)

    M = mem.shape[0]
    f_norm = lambda xv, g: (_rms(xv, g),)
    mem_ins = [In(mem, (M, D_MODEL), lambda i: (0, 0)), par(w["mem_norm"])]
    mem_outs = [Out((M, D_MODEL), BF16, (M, D_MODEL), lambda i: (0, 0))]
    (mn,) = seg_fwd("mem_norm_fwd", f_norm, (1,), mem_ins, mem_outs)
    (kvm,) = mm("xattn_kv_fwd", [mn], [[wb["xattn_w_kv"]]], [F32])

    def f_knorm(kk, gk):
        return (jnp.concatenate([_rms(kk[:, h * XHD:(h + 1) * XHD], gk) for h in range(XH)], axis=-1),)

    keys = ((M, XH * XHD), (M, XH * XHD), lambda i: (0, 0))
    knorm_ins = [In(kvm, keys[1], keys[2], diff=True, grad=keys), par(w["xattn_k_norm"])]
    knorm_outs = [Out(keys[0], F32, keys[1], keys[2])]
    (kn_mem,) = seg_fwd("mem_key_norm_fwd", f_knorm, (1,), knorm_ins, knorm_outs)

    xscale = XHD ** -0.5

    def f_post(o, yl, u, xv, kn, vm, d, w_glu, b, gm, gs, w_o1, w_o2, gx, w_xq, gq, w_xo):
        gl = jax.nn.gelu(yl + d * u)
        so = gl * jax.nn.sigmoid(wdot(gl, w_glu) + b)
        x2 = xv + wdot(_rms(o, gm), w_o1) + wdot(_rms(so, gs), w_o2)
        h3 = _rms(x2, gx)
        heads = []
        for h in range(XH):
            qn = _rms(wdot(h3, w_xq, h), gq)
            p = jax.nn.softmax(_bdot_nt(qn, kn[:, h * XHD:(h + 1) * XHD]) * xscale, axis=-1)
            heads.append(_bdot_nn(p, vm[:, h * XHD:(h + 1) * XHD]))
        return (x2 + wdot(jnp.concatenate(heads, axis=-1), w_xo),)

    def post_ins(gain_d):
        half = (M, XH * XHD)
        return [tile(o_mla, diff=True), tile(y_lin, diff=True), tile(pu, diff=True), tile(x1, diff=True),
                In(kn_mem, half, lambda i: (0, 0), diff=True, acc=(0,)),
                In(kvm, half, lambda i: (0, 1), diff=True, acc=(0,), grad=(half, half, lambda i: (0, 0))),
                par(gain_d), wt(wb["ssm_w_glu"]), par(w["ssm_b_glu"]), par(w["out_norm_mla"]), par(w["out_norm_ssm"]),
                wt(wb["w_o_mla"]), wt(wb["w_o_ssm"]), par(w["xattn_norm"]), wt(wb["xattn_w_q"]), par(w["xattn_q_norm"]),
                wt(wb["xattn_w_o"])]

    post_outs = [otile(D_MODEL, F32)]
    (x3,) = seg_fwd("post_mixer_fwd", f_post, g1, post_ins(w["ssm_d"]), post_outs)

    wg2, wu2, wd2 = ffn2_weights(x3)
    dx4, h4, g_2, u_2, parts = ffn_fwd("ffn2_fwd_loss", x3, w["ffn2_norm"], wg2, wu2, wd2, target=target)
    loss = jnp.sum(parts[::SUBLANES, 0])

    dx3, grads["ffn2_norm"], dg_2, du_2 = ffn_bwd_act("ffn2_bwd_act", dx4, x3, w["ffn2_norm"], g_2, u_2, wg2, wu2, wd2)
    grads["ffn2_w_gate"], grads["ffn2_w_up"], grads["ffn2_w_down"] = ffn_bwd_w("ffn2_bwd_w", h4, dx4, g_2, u_2, dg_2, du_2)
    sent = on_grads("ffn2", grads)

    (do_mla, dy_lin, du_a, dx1_a, dkn, dvm, grads["ssm_d"], grads["ssm_w_glu"], grads["ssm_b_glu"], grads["out_norm_mla"],
     grads["out_norm_ssm"], grads["w_o_mla"], grads["w_o_ssm"], grads["xattn_norm"], grads["xattn_w_q"],
     grads["xattn_q_norm"], grads["xattn_w_o"]) = seg_bwd(
        "post_mixer_bwd", f_post, g1, post_ins(w["ssm_d"] + sent), post_outs, [dx3])

    dkk, grads["xattn_k_norm"] = seg_bwd("mem_key_norm_bwd", f_knorm, (1,), knorm_ins, knorm_outs, [dkn])
    w_kv = wb["xattn_w_kv"]
    (dmn,) = mm("xattn_kv_bwd", [dkk, dvm], [[w_kv[:, :XH * XHD]], [w_kv[:, XH * XHD:]]], [F32], trans=True)
    gk_w, gv_w = mm_tn("xattn_kv_bwd_w", [mn], [dkk, dvm], [(0, [0]), (0, [1])])
    grads["xattn_w_kv"] = jnp.concatenate([gk_w, gv_w], axis=1)
    (grads["mem_norm"],) = seg_bwd("mem_norm_bwd", f_norm, (1,), mem_ins, mem_outs, [dmn])

    du, dbb_r, dbb_i, dcb_r, dncb_i, da_r8, da_i8 = ssm_bwd("ssm_bwd", dy_lin, du_a, pu, xs_r, xs_i, bb_r, bb_i, cb_r, ncb_i,
                                                            a_r8, a_i8)
    prep_g = seg_bwd("ssm_prep_bwd", _ssm_prep_f, prep_grid, prep_ins, prep_outs, [da_r8, da_i8, dbb_r, dbb_i, dcb_r, dncb_i])
    for k, gname in enumerate(("ssm_a_re", "ssm_a_im", "ssm_log_dt", "ssm_bt_re", "ssm_bt_im", "ssm_c2_re", "ssm_c2_im")):
        grads[gname] = prep_g[k]

    dqh, dkh, dv0 = attn_bwd("mla_attn_bwd", qh, kh, v0, do_mla, o_mla, lse, t=min(ATTN_TILE, S))

    (dx1, grads["mix_norm"], grads["w_in_q"], grads["w_in_kv"], grads["w_in_kr"], grads["w_in_u"], grads["mla_q_norm"],
     grads["mla_kv_norm"], grads["w_uq"], grads["w_ukv"], grads["qk_gain_q"], grads["qk_gain_k"]) = seg_bwd(
        "pre_mixer_bwd", f_pre, g1, pre_ins(w["mix_norm"]), pre_outs, [dqh, dkh, dv0, du], adds={0: dx1_a},
        row_block=FUSED_ROW_BLOCK)

    sent = on_grads("small", grads)
    dx, grads["ffn1_norm"], dg_1, du_1 = ffn_bwd_act("ffn1_bwd_act", dx1, x, w["ffn1_norm"] + sent, g_1, u_1,
                                                     wb["ffn1_w_gate"], wb["ffn1_w_up"], wb["ffn1_w_down"])
    grads["ffn1_w_gate"], grads["ffn1_w_up"], grads["ffn1_w_down"] = ffn_bwd_w("ffn1_bwd_w", h1, dx1, g_1, u_1, dg_1, du_1)
    return loss, dx, grads


def _pad_cols(a, n):
    return jnp.pad(a, ((0, 0), (0, n - a.shape[1])))


def _step_weights(shards):
    wb = {}
    w_in = _full_from_shards("w_in", shards["w_in"])
    wb["w_in_q"] = w_in[:, :MLA_Q_RANK]
    wb["w_in_kv"] = w_in[:, MLA_Q_RANK:MLA_Q_RANK + MLA_KV_RANK]
    wb["w_in_kr"] = _pad_cols(w_in[:, MLA_Q_RANK + MLA_KV_RANK:MLA_Q_RANK + MLA_KV_RANK + MLA_ROPE], LANES)
    wb["w_in_u"] = w_in[:, MLA_Q_RANK + MLA_KV_RANK + MLA_ROPE:]
    wb["w_uq"] = jnp.pad(shards["mla_w_uq"], ((0, 0), (0, 0), (0, HEAD_PAD - MLA_QK)))
    wb["w_ukv"] = shards["mla_w_ukv"]
    wb["ssm_w_glu"] = _full_from_shards("ssm_w_glu", shards["ssm_w_glu"])
    w_o = _full_from_shards("w_o", shards["w_o"])
    wb["w_o_mla"], wb["w_o_ssm"] = w_o[:SSM_WIDTH], w_o[SSM_WIDTH:]
    w_xq = _full_from_shards("xattn_w_q", shards["xattn_w_q"])
    wb["xattn_w_q"] = jnp.transpose(w_xq.reshape(D_MODEL, XH, XHD), (1, 0, 2))
    wb["xattn_w_kv"] = _full_from_shards("xattn_w_kv", shards["xattn_w_kv"])
    wb["xattn_w_o"] = _full_from_shards("xattn_w_o", shards["xattn_w_o"])
    return wb


def _sharded_grads(g):
    out = {}
    kr = g["w_in_kr"][:, :MLA_ROPE]
    out["w_in"] = _shards_from_full("w_in", jnp.concatenate([g["w_in_q"], g["w_in_kv"], kr, g["w_in_u"]], axis=1))
    out["mla_w_uq"] = g["w_uq"][:, :, :MLA_QK]
    out["mla_w_ukv"] = g["w_ukv"]
    out["ssm_w_glu"] = _shards_from_full("ssm_w_glu", g["ssm_w_glu"])
    out["w_o"] = _shards_from_full("w_o", jnp.concatenate([g["w_o_mla"], g["w_o_ssm"]], axis=0))
    w_xq = jnp.transpose(g["xattn_w_q"], (1, 0, 2)).reshape(D_MODEL, XH * XHD)
    out["xattn_w_q"] = _shards_from_full("xattn_w_q", w_xq)
    out["xattn_w_kv"] = _shards_from_full("xattn_w_kv", g["xattn_w_kv"])
    out["xattn_w_o"] = _shards_from_full("xattn_w_o", g["xattn_w_o"])
    return out


def _problem_repl_grads(g):
    out = {}
    out["mla_qk_norm_q"] = g["qk_gain_q"][:, :MLA_QK]
    out["mla_qk_norm_k"] = g["qk_gain_k"][:, :MLA_QK]
    out["ssm_b_re"] = jnp.transpose(g["ssm_bt_re"].reshape(SSM_GROUPS, SSM_GROUP, SSM_STATE), (0, 2, 1))
    out["ssm_b_im"] = jnp.transpose(g["ssm_bt_im"].reshape(SSM_GROUPS, SSM_GROUP, SSM_STATE), (0, 2, 1))
    out["ssm_c_re"] = g["ssm_c2_re"].reshape(SSM_GROUPS, SSM_GROUP, SSM_STATE)
    out["ssm_c_im"] = g["ssm_c2_im"].reshape(SSM_GROUPS, SSM_GROUP, SSM_STATE)
    for k in SMALL_REPL:
        if k not in out:
            out[k] = g[k]
    return out


def _problem_grads(g):
    out = {k: _full_from_shards(k, v) for k, v in _sharded_grads(g).items()}
    out.update(_problem_repl_grads(g))
    out.update({k: g[k] for k in FFN_NAMES})
    return out


def _step_params(p):
    row = lambda a: a.reshape(1, -1)
    w = {k: row(p[k]) for k in ("ffn1_norm", "mix_norm", "mla_q_norm", "mla_kv_norm", "ssm_b_glu", "out_norm_mla",
                                "out_norm_ssm", "xattn_norm", "mem_norm", "xattn_q_norm", "xattn_k_norm", "ffn2_norm")}
    w["qk_gain_q"] = _pad_cols(row(p["mla_qk_norm_q"]), HEAD_PAD)
    w["qk_gain_k"] = _pad_cols(row(p["mla_qk_norm_k"]), HEAD_PAD)
    w["ssm_a_re"], w["ssm_a_im"] = p["ssm_a_re"], p["ssm_a_im"]
    w["ssm_log_dt"] = p["ssm_log_dt"].reshape(SSM_GROUPS, 1)
    w["ssm_bt_re"] = jnp.transpose(p["ssm_b_re"], (0, 2, 1)).reshape(SSM_WIDTH, SSM_STATE)
    w["ssm_bt_im"] = jnp.transpose(p["ssm_b_im"], (0, 2, 1)).reshape(SSM_WIDTH, SSM_STATE)
    w["ssm_c2_re"] = p["ssm_c_re"].reshape(SSM_WIDTH, SSM_STATE)
    w["ssm_c2_im"] = p["ssm_c_im"].reshape(SSM_WIDTH, SSM_STATE)
    w["ssm_d"] = p["ssm_d"].reshape(1, SSM_WIDTH)
    return w


ARG_NAMES = ['x', 'mem', 'positions', 'ffn1_norm', 'ffn1_w_gate', 'ffn1_w_up', 'ffn1_w_down', 'mix_norm', 'w_in', 'mla_q_norm', 'mla_w_uq', 'mla_kv_norm', 'mla_w_ukv', 'mla_qk_norm_q', 'mla_qk_norm_k', 'ssm_a_re', 'ssm_a_im', 'ssm_log_dt', 'ssm_b_re', 'ssm_b_im', 'ssm_c_re', 'ssm_c_im', 'ssm_d', 'ssm_w_glu', 'ssm_b_glu', 'out_norm_mla', 'out_norm_ssm', 'w_o', 'xattn_norm', 'mem_norm', 'xattn_w_q', 'xattn_w_kv', 'xattn_q_norm', 'xattn_k_norm', 'xattn_w_o', 'ffn2_norm', 'ffn2_w_gate', 'ffn2_w_up', 'ffn2_w_down']
WEIGHT_NAMES = ARG_NAMES[3:]


def _gather_weights(p, c):
    half = lambda a: lax.dynamic_slice_in_dim(a, c * (a.shape[0] // 2), a.shape[0] // 2, axis=0)
    ffn1 = [half(p[k].astype(BF16)) for k in FFN_NAMES[:3]]
    small = [half(_pack_shards({k: p[k].astype(BF16) for k in SMALL_SHARDED}))]
    ffn2 = [half(p[k].astype(BF16)) for k in FFN_NAMES[3:]]
    me = 4 * lax.axis_index("x") + 2 * lax.axis_index("y") + c
    own = lambda got, blocks: [lax.dynamic_update_index_in_dim(g, b, me, 0) for g, b in zip(got, blocks)]
    as_shards = lambda a: a.reshape(N_CHIPS, 2 * a.shape[1], a.shape[2])
    landing = lambda blocks: [jax.ShapeDtypeStruct((N_DEV,) + b.shape, b.dtype) for b in blocks]
    got1 = own(all_gather_halves("all_gather_weights_a", ffn1), ffn1)
    got1, small = lax.optimization_barrier((got1, small))
    flight_s, sent_s = split_copy_start("gather_small_start", _gather_copies, N_DEV - 1, small, landing(small))
    sent_s, ffn2 = lax.optimization_barrier((sent_s, ffn2))
    flight_2, sent_2 = split_copy_start("gather_ffn2_start", _gather_copies, N_DEV - 1, ffn2, landing(ffn2))
    wb = {k: as_shards(a) for k, a in zip(FFN_NAMES[:3], got1)}

    def small_weights(after):
        mine, got = split_copy_wait("gather_small_wait", flight_s, after)
        return _step_weights(_unpack_shards(own(got, mine)[0].reshape(N_CHIPS, -1, LANES)))

    def ffn2_weights(after):
        mine, got = split_copy_wait("gather_ffn2_wait", flight_2, after)
        return [as_shards(a) for a in own(got, mine)]

    return wb, small_weights, ffn2_weights, sent_s + sent_2


class _GradReduce:
    def __init__(self, c):
        self.c, self.c_arr = c, jnp.reshape(c, (1,)).astype(jnp.int32)
        self.chip = 2 * lax.axis_index("x") + lax.axis_index("y")
        self.flights = []

    def start(self, tag, arrs):
        split = [a.reshape(N_CHIPS, 2, a.shape[1] // 2, a.shape[2]) for a in arrs]
        return self._scatter(tag, split, pair_send_halves(f"grad_pair_send_{tag}", split))

    def send(self, tag, arrs):
        split = [a.reshape(N_CHIPS, 2, a.shape[1] // 2, a.shape[2]) for a in arrs]
        lands = [jax.ShapeDtypeStruct((s.shape[0],) + s.shape[2:], s.dtype) for s in split]
        flight, sent = split_copy_start(f"grad_pair_send_start_{tag}", _pair_copies, 1, split, lands)
        self.sending = (tag, flight)
        return sent

    def scatter(self, tag, after):
        sent_tag, flight = self.sending
        assert sent_tag == tag
        return self._scatter(tag, *split_copy_wait(f"grad_pair_send_wait_{tag}", flight, after))

    def _scatter(self, tag, split, got):
        parts = [pair_add(f"grad_pair_add_{tag}_{k}", s, g, self.c_arr) for k, (s, g) in enumerate(zip(split, got))]
        flight, sent = split_copy_start(f"grad_scatter_start_{tag}", _scatter_copies, N_CHIPS - 1, parts, parts)
        self.flights.append((tag, flight))
        return sent

    def finish(self, after):
        halves = []
        for tag, flight in self.flights:
            parts, landed = split_copy_wait(f"grad_scatter_wait_{tag}", flight, after)
            for k, (q, p) in enumerate(zip(landed, parts)):
                mine = lax.dynamic_index_in_dim(p, self.chip, 0, keepdims=False)
                halves.append(sum_chips(f"grad_sum_{tag}_{k}", lax.dynamic_update_index_in_dim(q, mine, self.chip, 0)))
        tags = "_".join(t for t, _ in self.flights)
        self.flights = []
        theirs = pair_exchange(f"grad_pair_exchange_{tags}", halves)
        return [jnp.where(self.c == 0, jnp.concatenate([h, t], axis=0), jnp.concatenate([t, h], axis=0))
                for h, t in zip(halves, theirs)]


def kernel(x, mem, positions, ffn1_norm, ffn1_w_gate, ffn1_w_up, ffn1_w_down, mix_norm, w_in, mla_q_norm, mla_w_uq, mla_kv_norm, mla_w_ukv, mla_qk_norm_q, mla_qk_norm_k, ssm_a_re, ssm_a_im, ssm_log_dt, ssm_b_re, ssm_b_im, ssm_c_re, ssm_c_im, ssm_d, ssm_w_glu, ssm_b_glu, out_norm_mla, out_norm_ssm, w_o, xattn_norm, mem_norm, xattn_w_q, xattn_w_kv, xattn_q_norm, xattn_k_norm, xattn_w_o, ffn2_norm, ffn2_w_gate, ffn2_w_up, ffn2_w_down, loss_target, m_ffn1_norm, m_ffn1_w_gate, m_ffn1_w_up, m_ffn1_w_down, m_mix_norm, m_w_in, m_mla_q_norm, m_mla_w_uq, m_mla_kv_norm, m_mla_w_ukv, m_mla_qk_norm_q, m_mla_qk_norm_k, m_ssm_a_re, m_ssm_a_im, m_ssm_log_dt, m_ssm_b_re, m_ssm_b_im, m_ssm_c_re, m_ssm_c_im, m_ssm_d, m_ssm_w_glu, m_ssm_b_glu, m_out_norm_mla, m_out_norm_ssm, m_w_o, m_xattn_norm, m_mem_norm, m_xattn_w_q, m_xattn_w_kv, m_xattn_q_norm, m_xattn_k_norm, m_xattn_w_o, m_ffn2_norm, m_ffn2_w_gate, m_ffn2_w_up, m_ffn2_w_down, v_ffn1_norm, v_ffn1_w_gate, v_ffn1_w_up, v_ffn1_w_down, v_mix_norm, v_w_in, v_mla_q_norm, v_mla_w_uq, v_mla_kv_norm, v_mla_w_ukv, v_mla_qk_norm_q, v_mla_qk_norm_k, v_ssm_a_re, v_ssm_a_im, v_ssm_log_dt, v_ssm_b_re, v_ssm_b_im, v_ssm_c_re, v_ssm_c_im, v_ssm_d, v_ssm_w_glu, v_ssm_b_glu, v_out_norm_mla, v_out_norm_ssm, v_w_o, v_xattn_norm, v_mem_norm, v_xattn_w_q, v_xattn_w_kv, v_xattn_q_norm, v_xattn_k_norm, v_xattn_w_o, v_ffn2_norm, v_ffn2_w_gate, v_ffn2_w_up, v_ffn2_w_down):
    args = dict(locals())
    c = lax.axis_index("c")
    view = lambda k, a: jnp.swapaxes(a, 0, 1) if k in TRANSPOSED_VIEW else a
    p = {k: view(k, args[k][0]) for k in WEIGHT_NAMES}
    mom = {k: view(k, args["m_" + k][0]) for k in WEIGHT_NAMES}
    var = {k: view(k, args["v_" + k][0]) for k in WEIGHT_NAMES}
    natural = {k: view(k, p[k]) for k in WEIGHT_NAMES}

    wb, small_weights, ffn2_weights, sent = _gather_weights(
        {k: (p[k] if k in FFN_NAMES else natural[k]) for k in WEIGHT_NAMES}, c)
    w = _step_params(natural)
    w["ffn1_norm"] = w["ffn1_norm"] + sent
    early, late = _GradReduce(c), _GradReduce(c)

    def on_grads(tag, g):
        if tag == "ffn2":
            return early.send(tag, [g[k] for k in FFN_NAMES[3:]])
        packed = _pack_shards(_sharded_grads(g))
        return early.scatter("ffn2", packed[0, :SUBLANES]) + early.start(tag, [packed])

    loss, dx, g = local_step(x[0], mem[0], positions[0], loss_target[0], w, wb, small_weights, ffn2_weights, on_grads)

    sent = late.send("ffn1", [g[k] for k in FFN_NAMES[:3]])
    shards = early.finish(dx[:SUBLANES, :LANES] + sent)
    grad = dict(zip(FFN_NAMES[3:], shards[:3]))
    small_sharded = _unpack_shards(shards[3])
    grad.update({k: view(k, small_sharded[k]) for k in SMALL_SHARDED})
    reduced = all_reduce_small("grad_all_reduce_small", _pack_repl(_problem_repl_grads(g), loss + sent))
    grad.update(_unpack_repl(reduced))
    loss = grad.pop("loss")
    started = jnp.reshape(late.scatter("ffn1", reduced[:SUBLANES]), (1, 1))

    delta, new_m, new_v = {}, {}, {}
    small = [k for k in WEIGHT_NAMES if k not in FFN_NAMES and k not in SMALL_SHARDED]
    as2d = lambda a: a.reshape(-1, a.shape[-1])

    def update(k, after):
        delta[k], new_m[k], new_v[k] = adamw_big("adamw_" + k, as2d(p[k]), as2d(grad[k]), as2d(mom[k]), as2d(var[k]), after)

    for k in WEIGHT_NAMES:
        if k not in small and k not in FFN_NAMES[:3]:
            update(k, started)
    ds, nms, nvs = adamw_small("adamw_small", [as2d(p[k]) for k in small], [as2d(grad[k].reshape(p[k].shape)) for k in small],
                               [as2d(mom[k]) for k in small], [as2d(var[k]) for k in small])
    for k, d, nm, nv in zip(small, ds, nms, nvs):
        delta[k], new_m[k], new_v[k] = d, nm, nv

    updated = functools.reduce(jnp.add, [d[:1, :1] for d in delta.values()])
    grad.update(zip(FFN_NAMES[:3], late.finish(updated)))
    for k in FFN_NAMES[:3]:
        update(k, updated)

    shaped = lambda d, k: view(k, d.reshape(p[k].shape)).reshape(args[k].shape)
    return (loss, dx[None], *[shaped(grad[k], k) for k in WEIGHT_NAMES], *[shaped(delta[k], k) for k in WEIGHT_NAMES],
            *[shaped(new_m[k], k) for k in WEIGHT_NAMES], *[shaped(new_v[k], k) for k in WEIGHT_NAMES])
```

```python
import functools
import math

import jax
import jax.numpy as jnp
import numpy as np
from jax import lax
from jax.experimental import pallas as pl
from jax.experimental.pallas import tpu as pltpu

F32, BF16 = jnp.float32, jnp.bfloat16
EPS = 1e-6
MESH = pl.DeviceIdType.MESH

D_MODEL, D_FF = 1024, 2752
MLA_HEADS, MLA_Q_RANK, MLA_KV_RANK, MLA_NOPE, MLA_ROPE, MLA_V = 4, 384, 256, 128, 64, 128
MLA_QK = MLA_NOPE + MLA_ROPE
HEAD_PAD = 256
SSM_WIDTH, SSM_GROUP, SSM_GROUPS, SSM_STATE = 512, 16, 32, 64
SSM_LANES = SSM_GROUPS * SSM_STATE
XH, XHD = 4, 128
ROPE_THETA = 10000.0
ADAM_LR, ADAM_B1, ADAM_B2, ADAM_EPS, ADAM_WD, ADAM_STEP = 0.001, 0.9, 0.999, 1e-08, 0.01, 10
N_CHIPS, N_CORES, N_DEV = 4, 2, 8

VMEM_LIMIT_BYTES = 56 * 2**20
TOKEN_TILE = 512
FUSED_TILE = 256
FUSED_ROW_BLOCK = 128
FFN_ROW_BLOCK = 256
FFN_WIDE_TILE = 1024
ATTN_TILE = 512
STAT_LANES = 128
SCAN_TIME_TILE = 1024
SCAN_LANE_TILE = 512
SUBLANES = 8


def _params(n_axes):
    return pltpu.CompilerParams(dimension_semantics=("arbitrary",) * n_axes, vmem_limit_bytes=VMEM_LIMIT_BYTES)


def _first(axes):
    cond = None
    for a in axes:
        c = pl.program_id(a) == 0
        cond = c if cond is None else jnp.logical_and(cond, c)
    return cond


def mm(name, xs, ws, out_dtypes, *, trans=False, adds=None, tm=TOKEN_TILE):
    rows = xs[0].shape[0]
    tm = min(tm, rows)
    n_in, n_out = len(xs), len(out_dtypes)
    pairs = [(i, j) for i in range(n_in) for j in range(n_out) if ws[i][j] is not None]
    w_list = [ws[i][j] for (i, j) in pairs]
    adds = list(adds) if adds is not None else [None] * n_out
    add_list = [a for a in adds if a is not None]
    out_cols = [None] * n_out
    for (i, j), w in zip(pairs, w_list):
        out_cols[j] = w.shape[0] if trans else w.shape[1]
    contract = (((1,), (1 if trans else 0,)), ((), ()))

    def body(*refs):
        x_refs = refs[:n_in]
        w_refs = refs[n_in:n_in + len(pairs)]
        a_refs = list(refs[n_in + len(pairs):n_in + len(pairs) + len(add_list)])
        o_refs = refs[n_in + len(pairs) + len(add_list):]
        xb = [None] * n_in
        for j in range(n_out):
            acc = None
            for p, (i, jj) in enumerate(pairs):
                if jj != j:
                    continue
                if xb[i] is None:
                    xb[i] = x_refs[i][...].astype(BF16)
                d = lax.dot_general(xb[i], w_refs[p][...].astype(BF16), contract, preferred_element_type=F32)
                acc = d if acc is None else acc + d
            if adds[j] is not None:
                acc = acc + a_refs.pop(0)[...].astype(F32)
            o_refs[j][...] = acc.astype(o_refs[j].dtype)

    in_specs = ([pl.BlockSpec((tm, x.shape[1]), lambda i: (i, 0)) for x in xs]
                + [pl.BlockSpec(w.shape, lambda i: (0, 0)) for w in w_list]
                + [pl.BlockSpec((tm, a.shape[1]), lambda i: (i, 0)) for a in add_list])
    outs = pl.pallas_call(
        body, name=name, grid=(rows // tm,), in_specs=in_specs,
        out_specs=[pl.BlockSpec((tm, n), lambda i: (i, 0)) for n in out_cols],
        out_shape=[jax.ShapeDtypeStruct((rows, n), dt) for n, dt in zip(out_cols, out_dtypes)],
        compiler_params=_params(1),
    )(*xs, *w_list, *add_list)
    return list(outs)


def mm_tn(name, xs, dys, pairs, *, tm=TOKEN_TILE):
    rows = xs[0].shape[0]
    tm = min(tm, rows)
    n_x, n_dy = len(xs), len(dys)
    contract = (((0,), (0,)), ((), ()))

    def body(*refs):
        x_refs, dy_refs, o_refs = refs[:n_x], refs[n_x:n_x + n_dy], refs[n_x + n_dy:]
        @pl.when(pl.program_id(0) == 0)
        def _():
            for o in o_refs:
                o[...] = jnp.zeros_like(o)

        for k, (i, js) in enumerate(pairs):
            dy = None
            for j in js:
                t = dy_refs[j][...].astype(F32)
                dy = t if dy is None else dy + t
            o_refs[k][...] += lax.dot_general(x_refs[i][...].astype(BF16), dy.astype(BF16), contract,
                                              preferred_element_type=F32)

    shapes = [(xs[i].shape[1], dys[js[0]].shape[1]) for (i, js) in pairs]
    outs = pl.pallas_call(
        body, name=name, grid=(rows // tm,),
        in_specs=[pl.BlockSpec((tm, a.shape[1]), lambda i: (i, 0)) for a in (*xs, *dys)],
        out_specs=[pl.BlockSpec(s, lambda i: (0, 0)) for s in shapes],
        out_shape=[jax.ShapeDtypeStruct(s, F32) for s in shapes],
        compiler_params=_params(1),
    )(*xs, *dys)
    return list(outs)


class In:
    def __init__(self, arr, block=None, imap=None, *, diff=False, acc=None, grad=None, weight=False, rows=False):
        self.arr, self.block, self.imap, self.diff, self.acc, self.grad = arr, block, imap, diff, acc, grad
        self.weight, self.rows = weight, rows

    def spec(self):
        return pl.BlockSpec(memory_space=pltpu.VMEM) if self.weight else pl.BlockSpec(self.block, self.imap)


class Out:
    def __init__(self, shape, dtype, block, imap):
        self.shape, self.dtype, self.block, self.imap = shape, dtype, block, imap

    def spec(self):
        return pl.BlockSpec(self.block, self.imap)


class Wt:
    def __init__(self, ref, zeros=None):
        self.ref, self.zeros = ref, zeros


@jax.custom_vjp
def _wdot(a, w, z):
    return jnp.dot(a.astype(BF16), w, preferred_element_type=F32)


def _wdot_fwd(a, w, z):
    return _wdot(a, w, z), (a, w)


def _wdot_bwd(res, g):
    a, w = res
    gb = g.astype(BF16)
    da = lax.dot_general(gb, w, (((1,), (1,)), ((), ())), preferred_element_type=F32)
    dz = lax.dot_general(a.astype(BF16), gb, (((0,), (0,)), ((), ())), preferred_element_type=F32)
    return da, None, dz


_wdot.defvjp(_wdot_fwd, _wdot_bwd)


def wdot(a, wt, head=None):
    w = wt.ref[...] if head is None else wt.ref[head]
    if wt.zeros is None:
        return jnp.dot(a.astype(BF16), w, preferred_element_type=F32)
    return _wdot(a, w, wt.zeros[0 if head is None else head])


def seg_fwd(name, f, grid, ins, outs):
    n_in = len(ins)

    def body(*refs):
        res = f(*[Wt(r) if i.weight else r[...] for i, r in zip(ins, refs[:n_in])])
        for o_ref, r in zip(refs[n_in:], res):
            o_ref[...] = r.astype(o_ref.dtype)

    res = pl.pallas_call(
        body, name=name, grid=grid, in_specs=[i.spec() for i in ins], out_specs=[o.spec() for o in outs],
        out_shape=[jax.ShapeDtypeStruct(o.shape, o.dtype) for o in outs], compiler_params=_params(len(grid)),
    )(*[i.arr for i in ins])
    return list(res)


def seg_bwd(name, f, grid, ins, outs, cts, adds=None, row_block=None):
    n_in, n_ct = len(ins), len(cts)
    grad_idx = [k for k, i in enumerate(ins) if i.diff or i.weight]
    adds = adds or {}
    add_keys = sorted(adds)
    add_list = [adds[k] for k in add_keys]
    heads = {k: (ins[k].arr.shape[0] if ins[k].arr.ndim == 3 else 1) for k in grad_idx if ins[k].weight}
    tile_rows = outs[0].block[0]
    blocks = [None] if row_block is None else [pl.ds(r0, row_block) for r0 in range(0, tile_rows, row_block)]

    def body(*refs):
        in_refs, ct_refs = refs[:n_in], refs[n_in:n_in + n_ct]
        add_refs = dict(zip(add_keys, refs[n_in + n_ct:n_in + n_ct + len(add_list)]))
        g_refs = dict(zip(grad_idx, refs[n_in + n_ct + len(add_list):]))
        for k in grad_idx:
            if ins[k].weight or ins[k].acc is not None:
                @pl.when(_first(range(len(grid)) if ins[k].weight else ins[k].acc))
                def _(k=k):
                    g_refs[k][...] = jnp.zeros_like(g_refs[k])

        for rows in blocks:
            at = lambda ref, sliced: ref[...] if rows is None or not sliced else ref[rows, :]
            vals = [None if i.weight else at(r, i.rows) for i, r in zip(ins, in_refs)]
            primals, owner = [], []
            for k in grad_idx:
                if ins[k].weight:
                    for h in range(heads[k]):
                        primals.append(jnp.zeros(ins[k].arr.shape[-2:], F32))
                        owner.append((k, h))
                else:
                    primals.append(vals[k].astype(F32))
                    owner.append((k, None))

            def g(*dv, vals=vals, owner=owner):
                full = list(vals)
                zeros = {}
                for (k, h), v in zip(owner, dv):
                    if h is None:
                        full[k] = v
                    else:
                        zeros.setdefault(k, []).append(v)
                for k, z in zeros.items():
                    full[k] = Wt(in_refs[k], z)
                return tuple(f(*full))

            _, pull = jax.vjp(g, *primals)
            grads = pull(tuple(at(c, True).astype(F32) for c in ct_refs))
            for (k, h), gr in zip(owner, grads):
                if ins[k].weight:
                    if ins[k].arr.ndim == 3:
                        g_refs[k][h] += gr
                    else:
                        g_refs[k][...] += gr
                    continue
                if k in add_refs:
                    gr = gr + at(add_refs[k], True).astype(F32)
                if ins[k].acc is not None:
                    g_refs[k][...] += gr
                elif rows is None or not ins[k].rows:
                    g_refs[k][...] = gr.astype(g_refs[k].dtype)
                else:
                    g_refs[k][rows, :] = gr.astype(g_refs[k].dtype)

    g_specs, g_shapes = [], []
    for k in grad_idx:
        i = ins[k]
        if i.weight:
            g_specs.append(pl.BlockSpec(memory_space=pltpu.VMEM))
            g_shapes.append(jax.ShapeDtypeStruct(i.arr.shape, F32))
            continue
        shape, block, imap = i.grad if i.grad is not None else (i.arr.shape, i.block, i.imap)
        g_specs.append(pl.BlockSpec(block, imap))
        g_shapes.append(jax.ShapeDtypeStruct(shape, F32))
    in_specs = ([i.spec() for i in ins] + [o.spec() for o in outs]
                + [pl.BlockSpec(ins[k].block, ins[k].imap) for k in add_keys])
    res = pl.pallas_call(
        body, name=name, grid=grid, in_specs=in_specs, out_specs=g_specs, out_shape=g_shapes,
        compiler_params=_params(len(grid)),
    )(*[i.arr for i in ins], *cts, *add_list)
    return list(res)


def _rms(x, g):
    return x * lax.rsqrt(jnp.mean(x * x, axis=-1, keepdims=True) + EPS) * g


@jax.custom_vjp
def _bdot_nt(a, b):
    return lax.dot_general(a.astype(BF16), b.astype(BF16), (((1,), (1,)), ((), ())), preferred_element_type=F32)


def _bdot_nt_fwd(a, b):
    return _bdot_nt(a, b), (a, b)


def _bdot_nt_bwd(res, g):
    a, b = res
    gb = g.astype(BF16)
    da = lax.dot_general(gb, b.astype(BF16), (((1,), (0,)), ((), ())), preferred_element_type=F32)
    db = lax.dot_general(gb, a.astype(BF16), (((0,), (0,)), ((), ())), preferred_element_type=F32)
    return da, db


_bdot_nt.defvjp(_bdot_nt_fwd, _bdot_nt_bwd)


@jax.custom_vjp
def _bdot_nn(a, b):
    return lax.dot_general(a.astype(BF16), b.astype(BF16), (((1,), (0,)), ((), ())), preferred_element_type=F32)


def _bdot_nn_fwd(a, b):
    return _bdot_nn(a, b), (a, b)


def _bdot_nn_bwd(res, g):
    a, b = res
    gb = g.astype(BF16)
    da = lax.dot_general(gb, b.astype(BF16), (((1,), (1,)), ((), ())), preferred_element_type=F32)
    db = lax.dot_general(a.astype(BF16), gb, (((0,), (0,)), ((), ())), preferred_element_type=F32)
    return da, db


_bdot_nn.defvjp(_bdot_nn_fwd, _bdot_nn_bwd)


@functools.partial(jax.custom_vjp, nondiff_argnums=(1,))
def _lane_roll(x, shift):
    return pltpu.roll(x, shift, 1)


def _lane_roll_fwd(x, shift):
    return pltpu.roll(x, shift, 1), None


def _lane_roll_bwd(shift, _, g):
    return (pltpu.roll(g, (g.shape[1] - shift) % g.shape[1], 1),)


_lane_roll.defvjp(_lane_roll_fwd, _lane_roll_bwd)


def _hp_dot(a, b):
    return jnp.dot(a, b, precision=lax.Precision.HIGHEST, preferred_element_type=F32)


NT_DIMS = (((1,), (1,)), ((), ()))
TN_DIMS = (((0,), (0,)), ((), ()))


def ffn_fwd(name, x, gain, wg, wu, wd, target=None, tm=FFN_WIDE_TILE):
    S, D = x.shape
    tm = min(tm, S)
    nsh, Fs, _ = wg.shape
    with_loss = target is not None

    def body(*refs):
        if with_loss:
            x_ref, gain_ref, wg_ref, wu_ref, wd_ref, t_ref, xo_ref, h_ref, g_ref, u_ref, part_ref, acc = refs
        else:
            x_ref, gain_ref, wg_ref, wu_ref, wd_ref, xo_ref, h_ref, g_ref, u_ref, acc = refs
        j = pl.program_id(1)

        @pl.when(j == 0)
        def _():
            h_ref[...] = _rms(x_ref[...], gain_ref[...]).astype(BF16)
            acc[...] = jnp.zeros_like(acc)

        h = h_ref[...]
        g = lax.dot_general(h, wg_ref[...], NT_DIMS, preferred_element_type=F32)
        u = lax.dot_general(h, wu_ref[...], NT_DIMS, preferred_element_type=F32)
        g_ref[...] = g.astype(BF16)
        u_ref[...] = u.astype(BF16)
        a = g * jax.nn.sigmoid(g) * u
        acc[...] += jnp.dot(a.astype(BF16), wd_ref[...], preferred_element_type=F32)

        @pl.when(j == nsh - 1)
        def _():
            y = x_ref[...] + 0.5 * acc[...]
            if with_loss:
                err = y - t_ref[...]
                xo_ref[...] = err * (1.0 / D)
                part_ref[...] = jnp.full(part_ref.shape, 0.5 * jnp.sum(jnp.mean(err * err, axis=-1)), F32)
            else:
                xo_ref[...] = y

    rows = pl.BlockSpec((tm, D), lambda i, j: (i, 0))
    wspec = pl.BlockSpec((None, Fs, D), lambda i, j: (j, 0, 0))
    act = pl.BlockSpec((None, tm, Fs), lambda i, j: (j, i, 0))
    in_specs, args = [rows, pl.BlockSpec((1, D), lambda i, j: (0, 0)), wspec, wspec, wspec], [x, gain, wg, wu, wd]
    out_specs = [rows, rows, act, act]
    out_shape = [jax.ShapeDtypeStruct((S, D), F32), jax.ShapeDtypeStruct((S, D), BF16),
                 jax.ShapeDtypeStruct((nsh, S, Fs), BF16), jax.ShapeDtypeStruct((nsh, S, Fs), BF16)]
    if with_loss:
        in_specs.append(rows)
        args.append(target)
        out_specs.append(pl.BlockSpec((SUBLANES, 128), lambda i, j: (i, 0)))
        out_shape.append(jax.ShapeDtypeStruct((S // tm * SUBLANES, 128), F32))
    return pl.pallas_call(
        body, name=name, grid=(S // tm, nsh), in_specs=in_specs, out_specs=out_specs, out_shape=out_shape,
        scratch_shapes=[pltpu.VMEM((tm, D), F32)], compiler_params=_params(2),
    )(*args)


def ffn_bwd_act(name, dxo, x, gain, g, u, wg, wu, wd, tm=TOKEN_TILE):
    S, D = x.shape
    tm = min(tm, S)
    nsh, Fs, _ = wg.shape

    def body(dxo_ref, x_ref, gain_ref, g_ref, u_ref, wg_ref, wu_ref, wd_ref, dx_ref, dgain_ref, dg_ref, du_ref, dd, dh):
        i, j = pl.program_id(0), pl.program_id(1)

        @pl.when(j == 0)
        def _():
            dd[...] = (0.5 * dxo_ref[...]).astype(BF16)
            dh[...] = jnp.zeros_like(dh)

        for r0 in range(0, tm, FFN_ROW_BLOCK):
            rows = pl.ds(r0, FFN_ROW_BLOCK)
            da = lax.dot_general(dd[rows, :], wd_ref[...], NT_DIMS, preferred_element_type=F32)
            gf, uf = g_ref[rows, :].astype(F32), u_ref[rows, :].astype(F32)
            sig = jax.nn.sigmoid(gf)
            dgv = (da * uf * (sig * (1.0 + gf * (1.0 - sig)))).astype(BF16)
            duv = (da * (gf * sig)).astype(BF16)
            dg_ref[rows, :] = dgv
            du_ref[rows, :] = duv
            dh[rows, :] += (jnp.dot(dgv, wg_ref[...], preferred_element_type=F32)
                            + jnp.dot(duv, wu_ref[...], preferred_element_type=F32))

        @pl.when(j == nsh - 1)
        def _():
            xv = x_ref[...]
            r = lax.rsqrt(jnp.mean(xv * xv, axis=-1, keepdims=True) + EPS)
            xhat = xv * r
            dhv = dh[...]
            dxn = dhv * gain_ref[...]
            dx_ref[...] = dxo_ref[...] + r * (dxn - xhat * jnp.mean(dxn * xhat, axis=-1, keepdims=True))
            part = jnp.sum(dhv * xhat, axis=0, keepdims=True)

            @pl.when(i == 0)
            def _():
                dgain_ref[...] = part

            @pl.when(i != 0)
            def _():
                dgain_ref[...] += part

    return pl.pallas_call(
        body, name=name, grid=(S // tm, nsh),
        in_specs=[pl.BlockSpec((tm, D), lambda i, j: (i, 0)), pl.BlockSpec((tm, D), lambda i, j: (i, 0)),
                  pl.BlockSpec((1, D), lambda i, j: (0, 0)),
                  pl.BlockSpec((None, tm, Fs), lambda i, j: (j, i, 0)), pl.BlockSpec((None, tm, Fs), lambda i, j: (j, i, 0)),
                  pl.BlockSpec((None, Fs, D), lambda i, j: (j, 0, 0)), pl.BlockSpec((None, Fs, D), lambda i, j: (j, 0, 0)),
                  pl.BlockSpec((None, Fs, D), lambda i, j: (j, 0, 0))],
        out_specs=[pl.BlockSpec((tm, D), lambda i, j: (i, 0)), pl.BlockSpec((1, D), lambda i, j: (0, 0)),
                   pl.BlockSpec((None, tm, Fs), lambda i, j: (j, i, 0)), pl.BlockSpec((None, tm, Fs), lambda i, j: (j, i, 0))],
        out_shape=[jax.ShapeDtypeStruct((S, D), F32), jax.ShapeDtypeStruct((1, D), F32),
                   jax.ShapeDtypeStruct((nsh, S, Fs), BF16), jax.ShapeDtypeStruct((nsh, S, Fs), BF16)],
        scratch_shapes=[pltpu.VMEM((tm, D), BF16), pltpu.VMEM((tm, D), F32)], compiler_params=_params(2),
    )(dxo, x, gain, g, u, wg, wu, wd)


def ffn_bwd_w(name, h, dxo, g, u, dg, du, tm=FFN_WIDE_TILE):
    S, D = h.shape
    tm = min(tm, S)
    nsh, _, Fs = g.shape

    def body(h_ref, dxo_ref, g_ref, u_ref, dg_ref, du_ref, dwg_ref, dwu_ref, dwd_ref):
        i = pl.program_id(1)
        gf, uf = g_ref[...].astype(F32), u_ref[...].astype(F32)
        a = (gf * jax.nn.sigmoid(gf) * uf).astype(BF16)
        dd = (0.5 * dxo_ref[...]).astype(BF16)
        hv = h_ref[...]

        @pl.when(i == 0)
        def _():
            dwg_ref[...] = jnp.zeros_like(dwg_ref)
            dwu_ref[...] = jnp.zeros_like(dwu_ref)
            dwd_ref[...] = jnp.zeros_like(dwd_ref)

        dwg_ref[...] += lax.dot_general(dg_ref[...], hv, TN_DIMS, preferred_element_type=F32)
        dwu_ref[...] += lax.dot_general(du_ref[...], hv, TN_DIMS, preferred_element_type=F32)
        dwd_ref[...] += lax.dot_general(a, dd, TN_DIMS, preferred_element_type=F32)

    act = pl.BlockSpec((None, tm, Fs), lambda j, i: (j, i, 0))
    wspec = pl.BlockSpec((None, Fs, D), lambda j, i: (j, 0, 0))
    return pl.pallas_call(
        body, name=name, grid=(nsh, S // tm),
        in_specs=[pl.BlockSpec((tm, D), lambda j, i: (i, 0)), pl.BlockSpec((tm, D), lambda j, i: (i, 0)), act, act, act, act],
        out_specs=[wspec, wspec, wspec], out_shape=[jax.ShapeDtypeStruct((nsh, Fs, D), F32)] * 3,
        compiler_params=_params(2),
    )(h, dxo, g, u, dg, du)


NEG_BIG = -1e30


def _causal_pairs(n, by_key):
    pairs = [(qi, ki) for qi in range(n) for ki in range(qi + 1)]
    if by_key:
        pairs.sort(key=lambda p: (p[1], p[0]))
    return jnp.asarray([p[0] for p in pairs], jnp.int32), jnp.asarray([p[1] for p in pairs], jnp.int32)


def _scores(q, k, masked):
    s = lax.dot_general(q, k, (((1,), (1,)), ((), ())), preferred_element_type=F32)
    if masked:
        row = lax.broadcasted_iota(jnp.int32, s.shape, 0)
        col = lax.broadcasted_iota(jnp.int32, s.shape, 1)
        s = jnp.where(row >= col, s, NEG_BIG)
    return s


def attn_fwd(name, q, k, v, t=ATTN_TILE):
    S, Dk = q.shape[0], HEAD_PAD
    H = q.shape[1] // Dk
    Dv = v.shape[1] // H
    qt, kt = _causal_pairs(S // t, by_key=False)

    def body(qt_ref, kt_ref, q_ref, k_ref, v_ref, o_ref, lse_ref, m_sc, l_sc, acc):
        qi, ki = qt_ref[pl.program_id(1)], kt_ref[pl.program_id(1)]

        @pl.when(ki == 0)
        def _():
            m_sc[...] = jnp.full_like(m_sc, NEG_BIG)
            l_sc[...] = jnp.zeros_like(l_sc)
            acc[...] = jnp.zeros_like(acc)

        def step(masked):
            s = _scores(q_ref[...], k_ref[...], masked)
            m_prev = m_sc[...]
            m_next = jnp.maximum(m_prev, jnp.max(s, axis=-1, keepdims=True))
            alpha = jnp.exp(m_prev - m_next)
            p = jnp.exp(s - jnp.tile(m_next, (1, t // STAT_LANES)))
            l_sc[...] = alpha * l_sc[...] + jnp.sum(p, axis=-1, keepdims=True)
            acc[...] = alpha * acc[...] + jnp.dot(p.astype(BF16), v_ref[...].astype(BF16), preferred_element_type=F32)
            m_sc[...] = m_next

        @pl.when(ki < qi)
        def _():
            step(False)

        @pl.when(ki == qi)
        def _():
            step(True)
            o_ref[...] = acc[...] / l_sc[...]
            lse_ref[...] = m_sc[...] + jnp.log(l_sc[...])

    stat = pltpu.VMEM((t, STAT_LANES), F32)
    return pl.pallas_call(
        body, name=name,
        grid_spec=pltpu.PrefetchScalarGridSpec(
            num_scalar_prefetch=2, grid=(H, qt.shape[0]),
            in_specs=[pl.BlockSpec((t, Dk), lambda h, s, qt, kt: (qt[s], h)),
                      pl.BlockSpec((t, Dk), lambda h, s, qt, kt: (kt[s], h)),
                      pl.BlockSpec((t, Dv), lambda h, s, qt, kt: (kt[s], h))],
            out_specs=[pl.BlockSpec((t, Dv), lambda h, s, qt, kt: (qt[s], h)),
                       pl.BlockSpec((None, t, STAT_LANES), lambda h, s, qt, kt: (h, qt[s], 0))],
            scratch_shapes=[stat, stat, pltpu.VMEM((t, Dv), F32)]),
        out_shape=[jax.ShapeDtypeStruct((S, H * Dv), F32), jax.ShapeDtypeStruct((H, S, STAT_LANES), F32)],
        compiler_params=_params(2),
    )(qt, kt, q, k, v)


def attn_bwd(name, q, k, v, do, o, lse, t=ATTN_TILE):
    S, Dk = q.shape[0], HEAD_PAD
    H = q.shape[1] // Dk
    Dv = v.shape[1] // H
    qt, kt = _causal_pairs(S // t, by_key=True)
    tn_dims = (((0,), (0,)), ((), ()))

    def body(qt_ref, kt_ref, q_ref, k_ref, v_ref, do_ref, o_ref, lse_ref, dq_ref, dk_ref, dv_ref):
        step_id = pl.program_id(1)
        qi, ki = qt_ref[step_id], kt_ref[step_id]

        @pl.when(step_id == 0)
        def _():
            dq_ref[...] = jnp.zeros_like(dq_ref)

        def step(masked):
            s = _scores(q_ref[...], k_ref[...], masked)
            reps = (1, t // STAT_LANES)
            p = jnp.exp(s - jnp.tile(lse_ref[...], reps))
            dov = do_ref[...]
            delta = jnp.broadcast_to(jnp.sum(dov * o_ref[...], axis=-1, keepdims=True), (t, STAT_LANES))
            dob = dov.astype(BF16)
            dp = lax.dot_general(dob, v_ref[...].astype(BF16), (((1,), (1,)), ((), ())), preferred_element_type=F32)
            ds = (p * (dp - jnp.tile(delta, reps))).astype(BF16)
            pdv = lax.dot_general(p.astype(BF16), dob, tn_dims, preferred_element_type=F32)
            pdk = lax.dot_general(ds, q_ref[...], tn_dims, preferred_element_type=F32)
            rows = pl.ds(pl.multiple_of(qi * t, t), t)
            dq_ref[rows, :] += jnp.dot(ds, k_ref[...], preferred_element_type=F32)
            return pdk, pdv

        @pl.when(ki == qi)
        def _():
            dk_ref[...] = jnp.zeros_like(dk_ref)
            dv_ref[...] = jnp.zeros_like(dv_ref)

        def accumulate(masked):
            pdk, pdv = step(masked)
            dk_ref[...] += pdk
            dv_ref[...] += pdv

        @pl.when(ki == qi)
        def _():
            accumulate(True)

        @pl.when(ki < qi)
        def _():
            accumulate(False)

    qrow = lambda h, s, qt, kt: (qt[s], h)
    krow = lambda h, s, qt, kt: (kt[s], h)
    return pl.pallas_call(
        body, name=name,
        grid_spec=pltpu.PrefetchScalarGridSpec(
            num_scalar_prefetch=2, grid=(H, qt.shape[0]),
            in_specs=[pl.BlockSpec((t, Dk), qrow), pl.BlockSpec((t, Dk), krow), pl.BlockSpec((t, Dv), krow),
                      pl.BlockSpec((t, Dv), qrow), pl.BlockSpec((t, Dv), qrow),
                      pl.BlockSpec((None, t, STAT_LANES), lambda h, s, qt, kt: (h, qt[s], 0))],
            out_specs=[pl.BlockSpec((S, Dk), lambda h, s, qt, kt: (0, h)), pl.BlockSpec((t, Dk), krow),
                       pl.BlockSpec((t, Dv), krow)]),
        out_shape=[jax.ShapeDtypeStruct((S, H * Dk), F32), jax.ShapeDtypeStruct((S, H * Dk), F32),
                   jax.ShapeDtypeStruct((S, H * Dv), F32)],
        compiler_params=_params(2),
    )(qt, kt, q, k, v, do, o, lse)


def _cmul(ar, ai, br, bi):
    return ar * br - ai * bi, ar * bi + ai * br


def _scan_tile(x_r, x_i, ar_ref, ai_ref, cr_sc, ci_sc, *, reverse, first, states=None):
    tc, lanes = x_r.shape
    nblk, lb = tc // SUBLANES, SCAN_LANE_TILE
    with_da = states is not None
    if with_da:
        xr_all, xi_all, pr_all, pi_all, dar_all, dai_all, chunk = states

    @pl.when(first)
    def _():
        cr_sc[...] = jnp.zeros_like(cr_sc)
        ci_sc[...] = jnp.zeros_like(ci_sc)
        if with_da:
            dar_all[...] = jnp.zeros_like(dar_all)
            dai_all[...] = jnp.zeros_like(dai_all)

    row = lax.broadcasted_iota(jnp.int32, (SUBLANES, lb), 0)
    for l0 in range(0, lanes, lb):
        _scan_lanes(x_r.at[:, pl.ds(l0, lb)], x_i.at[:, pl.ds(l0, lb)], ar_ref[0:1, pl.ds(l0, lb)],
                    ai_ref[0:1, pl.ds(l0, lb)], cr_sc.at[:, pl.ds(l0, lb)], ci_sc.at[:, pl.ds(l0, lb)], row, reverse,
                    nblk, None if not with_da else tuple(r.at[:, pl.ds(l0, lb)] for r in states[:6]) + (chunk,))


def _scan_lanes(x_r, x_i, a1r, a1i, cr_sc, ci_sc, row, reverse, nblk, states):
    lb = x_r.shape[1]
    with_da = states is not None
    if with_da:
        xr_ref, xi_ref, pr_ref, pi_ref, dar_ref, dai_ref, chunk = states
    if reverse:
        a1i = -a1i
    a2r, a2i = _cmul(a1r, a1i, a1r, a1i)
    a4r, a4i = _cmul(a2r, a2i, a2r, a2i)
    pw_r, pw_i = jnp.zeros((SUBLANES, lb), F32), jnp.zeros((SUBLANES, lb), F32)
    cur_r, cur_i = a1r, a1i
    for e in range(SUBLANES):
        r_at = (SUBLANES - 1 - e) if reverse else e
        pw_r = jnp.where(row == r_at, cur_r, pw_r)
        pw_i = jnp.where(row == r_at, cur_i, pw_i)
        cur_r, cur_i = _cmul(cur_r, cur_i, a1r, a1i)
    steps = []
    for d, pr, pi in ((1, a1r, a1i), (2, a2r, a2i), (4, a4r, a4i)):
        keep = (row < SUBLANES - d) if reverse else (row >= d)
        steps.append((d, jnp.where(keep, pr, 0.0), jnp.where(keep, pi, 0.0)))

    def block(jb, carry):
        if with_da:
            cr, ci, acc_r, acc_i = carry
        else:
            cr, ci = carry
        idx = (nblk - 1 - jb) if reverse else jb
        r0 = pl.multiple_of(idx * SUBLANES, SUBLANES)
        xr = x_r[pl.ds(r0, SUBLANES), :]
        xi = x_i[pl.ds(r0, SUBLANES), :]
        for d, pr, pi in steps:
            shift = SUBLANES - d if reverse else d
            sr, si = pltpu.roll(xr, shift, 0), pltpu.roll(xi, shift, 0)
            xr, xi = xr + pr * sr - pi * si, xi + pr * si + pi * sr
        xr, xi = xr + pw_r * cr - pw_i * ci, xi + pw_r * ci + pw_i * cr
        x_r[pl.ds(r0, SUBLANES), :] = xr
        x_i[pl.ds(r0, SUBLANES), :] = xi
        edge = 0 if reverse else SUBLANES - 1
        cr, ci = xr[edge:edge + 1, :], xi[edge:edge + 1, :]
        if not with_da:
            return cr, ci
        fr = xr_ref[pl.ds(r0, SUBLANES), :]
        fi = xi_ref[pl.ds(r0, SUBLANES), :]
        rp = pl.multiple_of(jnp.maximum(idx - 1, 0) * SUBLANES, SUBLANES)
        inside = idx > 0
        before_r = jnp.where(inside, xr_ref[pl.ds(rp, SUBLANES), :], pr_ref[...])
        before_i = jnp.where(inside, xi_ref[pl.ds(rp, SUBLANES), :], pi_ref[...])
        live = jnp.where(jnp.logical_or(inside, chunk > 0), 1.0, 0.0)
        last_r = before_r[SUBLANES - 1:SUBLANES, :] * live
        last_i = before_i[SUBLANES - 1:SUBLANES, :] * live
        pvr = jnp.where(row == 0, last_r, pltpu.roll(fr, 1, 0))
        pvi = jnp.where(row == 0, last_i, pltpu.roll(fi, 1, 0))
        acc_r = acc_r + xr * pvr + xi * pvi
        acc_i = acc_i + xi * pvr - xr * pvi
        return cr, ci, acc_r, acc_i

    init = (cr_sc[...], ci_sc[...])
    if with_da:
        init = init + (jnp.zeros((SUBLANES, lb), F32), jnp.zeros((SUBLANES, lb), F32))
    fin = lax.fori_loop(0, nblk, block, init)
    cr_sc[...] = fin[0]
    ci_sc[...] = fin[1]
    if with_da:
        dar_ref[...] += fin[2]
        dai_ref[...] += fin[3]


SSM_BLOCKS = 4
BLOCK_CH = SSM_WIDTH // SSM_BLOCKS
PREP_LANES = SSM_LANES // SSM_BLOCKS


def ssm_fwd(name, u, bb_r, bb_i, cb_r, ncb_i, a_r8, a_i8):
    S = u.shape[0]
    tc = min(SCAN_TIME_TILE, S)

    def body(u_ref, bbr_ref, bbi_ref, cbr_ref, ncbi_ref, ar_ref, ai_ref, xr_ref, xi_ref, y_ref, cr_sc, ci_sc):
        ub = u_ref[...].astype(BF16)
        xr_ref[...] = jnp.dot(ub, bbr_ref[...], preferred_element_type=F32)
        xi_ref[...] = jnp.dot(ub, bbi_ref[...], preferred_element_type=F32)
        _scan_tile(xr_ref, xi_ref, ar_ref, ai_ref, cr_sc, ci_sc, reverse=False, first=pl.program_id(1) == 0)
        y_ref[...] = (lax.dot_general(xr_ref[...].astype(BF16), cbr_ref[...], NT_DIMS, preferred_element_type=F32)
                      + lax.dot_general(xi_ref[...].astype(BF16), ncbi_ref[...], NT_DIMS, preferred_element_type=F32))

    ch = pl.BlockSpec((tc, BLOCK_CH), lambda b, t: (t, b))
    st = pl.BlockSpec((tc, PREP_LANES), lambda b, t: (t, b))
    wt = pl.BlockSpec((None, BLOCK_CH, PREP_LANES), lambda b, t: (b, 0, 0))
    par = pl.BlockSpec((SUBLANES, PREP_LANES), lambda b, t: (0, b))
    return pl.pallas_call(
        body, name=name, grid=(SSM_BLOCKS, S // tc), in_specs=[ch, wt, wt, wt, wt, par, par], out_specs=[st, st, ch],
        out_shape=[jax.ShapeDtypeStruct((S, SSM_LANES), F32), jax.ShapeDtypeStruct((S, SSM_LANES), F32),
                   jax.ShapeDtypeStruct((S, SSM_WIDTH), F32)],
        scratch_shapes=[pltpu.VMEM((1, PREP_LANES), F32), pltpu.VMEM((1, PREP_LANES), F32)], compiler_params=_params(2),
    )(u, bb_r, bb_i, cb_r, ncb_i, a_r8, a_i8)


def ssm_bwd(name, dy, du_add, u, xs_r, xs_i, bb_r, bb_i, cb_r, ncb_i, a_r8, a_i8):
    S = u.shape[0]
    tc = min(SCAN_TIME_TILE, S)
    nt = S // tc

    def body(dy_ref, dua_ref, u_ref, xr_ref, xi_ref, pr_ref, pi_ref, bbr_ref, bbi_ref, cbr_ref, ncbi_ref, ar_ref, ai_ref,
             du_ref, dbbr_ref, dbbi_ref, dcbr_ref, dncbi_ref, dar_ref, dai_ref, lr_sc, li_sc, cr_sc, ci_sc):
        t = pl.program_id(1)
        first = t == 0

        @pl.when(first)
        def _():
            for r in (dbbr_ref, dbbi_ref, dcbr_ref, dncbi_ref):
                r[...] = jnp.zeros_like(r)

        dyb = dy_ref[...].astype(BF16)
        lr_sc[...] = jnp.dot(dyb, cbr_ref[...], preferred_element_type=F32)
        li_sc[...] = jnp.dot(dyb, ncbi_ref[...], preferred_element_type=F32)
        _scan_tile(lr_sc, li_sc, ar_ref, ai_ref, cr_sc, ci_sc, reverse=True, first=first,
                   states=(xr_ref, xi_ref, pr_ref, pi_ref, dar_ref, dai_ref, nt - 1 - t))
        lrb, lib = lr_sc[...].astype(BF16), li_sc[...].astype(BF16)
        du_ref[...] = (lax.dot_general(lrb, bbr_ref[...], NT_DIMS, preferred_element_type=F32)
                       + lax.dot_general(lib, bbi_ref[...], NT_DIMS, preferred_element_type=F32) + dua_ref[...])
        ub = u_ref[...].astype(BF16)
        dbbr_ref[...] += lax.dot_general(ub, lrb, TN_DIMS, preferred_element_type=F32)
        dbbi_ref[...] += lax.dot_general(ub, lib, TN_DIMS, preferred_element_type=F32)
        dcbr_ref[...] += lax.dot_general(dyb, xr_ref[...].astype(BF16), TN_DIMS, preferred_element_type=F32)
        dncbi_ref[...] += lax.dot_general(dyb, xi_ref[...].astype(BF16), TN_DIMS, preferred_element_type=F32)

    ch = pl.BlockSpec((tc, BLOCK_CH), lambda b, t: (nt - 1 - t, b))
    st = pl.BlockSpec((tc, PREP_LANES), lambda b, t: (nt - 1 - t, b))
    prev = pl.BlockSpec((SUBLANES, PREP_LANES), lambda b, t: (jnp.maximum((nt - 1 - t) * (tc // SUBLANES) - 1, 0), b))
    wt = pl.BlockSpec((None, BLOCK_CH, PREP_LANES), lambda b, t: (b, 0, 0))
    par = pl.BlockSpec((SUBLANES, PREP_LANES), lambda b, t: (0, b))
    blk = jax.ShapeDtypeStruct((SSM_BLOCKS, BLOCK_CH, PREP_LANES), F32)
    rows8 = jax.ShapeDtypeStruct((SUBLANES, SSM_LANES), F32)
    return pl.pallas_call(
        body, name=name, grid=(SSM_BLOCKS, nt), in_specs=[ch, ch, ch, st, st, prev, prev, wt, wt, wt, wt, par, par],
        out_specs=[ch, wt, wt, wt, wt, par, par],
        out_shape=[jax.ShapeDtypeStruct((S, SSM_WIDTH), F32), blk, blk, blk, blk, rows8, rows8],
        scratch_shapes=[pltpu.VMEM((tc, PREP_LANES), F32), pltpu.VMEM((tc, PREP_LANES), F32),
                        pltpu.VMEM((1, PREP_LANES), F32), pltpu.VMEM((1, PREP_LANES), F32)],
        compiler_params=_params(2),
    )(dy, du_add, u, xs_r, xs_i, xs_r, xs_i, bb_r, bb_i, cb_r, ncb_i, a_r8, a_i8)


def _ssm_prep_f(a_re, a_im, log_dt, bt_re, bt_im, c_re, c_im):
    first_group = pl.program_id(0) * (SSM_GROUPS // SSM_BLOCKS)
    iota = lambda shape, d: lax.broadcasted_iota(jnp.int32, shape, d)
    grp_of_row = lambda shape: iota(shape, 0) >> int(math.log2(SSM_GROUP))
    grp_of_lane = lambda shape: iota(shape, 1) >> int(math.log2(SSM_STATE))
    rep = (grp_of_row((BLOCK_CH, SSM_GROUPS)) + first_group == iota((BLOCK_CH, SSM_GROUPS), 1)).astype(F32)
    til = ((iota((SSM_STATE, PREP_LANES), 1) & (SSM_STATE - 1)) == iota((SSM_STATE, PREP_LANES), 0)).astype(F32)
    m_rows = (grp_of_row((BLOCK_CH, PREP_LANES)) == grp_of_lane((BLOCK_CH, PREP_LANES))).astype(F32)
    m_grp = (iota((SSM_GROUPS, PREP_LANES), 0) == grp_of_lane((SSM_GROUPS, PREP_LANES)) + first_group).astype(F32)
    dt = jnp.exp(log_dt)
    decay = jnp.exp(a_re * dt)
    ar = decay * jnp.cos(a_im * dt)
    ai = decay * jnp.sin(a_im * dt)
    den = a_re * a_re + a_im * a_im
    nr = ar - 1.0
    coef_r = (nr * a_re + ai * a_im) / den
    coef_i = (ai * a_re - nr * a_im) / den
    cr, ci = _hp_dot(rep, coef_r), _hp_dot(rep, coef_i)
    bb_r = cr * bt_re - ci * bt_im
    bb_i = cr * bt_im + ci * bt_re
    big = lambda m: _hp_dot(m, til) * m_rows
    lanes = lambda m: jnp.broadcast_to(jnp.sum(_hp_dot(m, til) * m_grp, axis=0, keepdims=True), (SUBLANES, PREP_LANES))
    return lanes(ar), lanes(ai), big(bb_r), big(bb_i), big(c_re), -big(c_im)


def _whole(arr, **kw):
    nd = arr.ndim
    return In(arr, arr.shape, lambda *_: (0,) * nd, **kw)


def _adamw_math(w, g, m, v):
    m = ADAM_B1 * m + (1.0 - ADAM_B1) * g
    v = ADAM_B2 * v + (1.0 - ADAM_B2) * (g * g)
    m_hat = m / (1.0 - ADAM_B1 ** ADAM_STEP)
    v_hat = v / (1.0 - ADAM_B2 ** ADAM_STEP)
    delta = -ADAM_LR * (m_hat / (jnp.sqrt(v_hat) + ADAM_EPS) + ADAM_WD * w)
    return delta, m, v


def adamw_big(name, w, g, m, v, after):
    R, C = w.shape
    tr = R
    for cand in (512, 344, 256, 128):
        if R % cand == 0:
            tr = cand
            break

    def body(w_ref, g_ref, m_ref, v_ref, after_ref, d_ref, nm_ref, nv_ref):
        d, nm, nv = _adamw_math(w_ref[...], g_ref[...], m_ref[...], v_ref[...])
        d_ref[...] = d
        nm_ref[...] = nm
        nv_ref[...] = nv

    spec = pl.BlockSpec((tr, C), lambda i: (i, 0))
    return pl.pallas_call(
        body, name=name, grid=(R // tr,), in_specs=[spec] * 4 + [pl.BlockSpec(memory_space=pl.ANY)], out_specs=[spec] * 3,
        out_shape=[jax.ShapeDtypeStruct((R, C), F32)] * 3, compiler_params=_params(1),
    )(w, g, m, v, after)


def adamw_small(name, ws, gs, ms, vs):
    n = len(ws)

    def body(*refs):
        for k in range(n):
            d, nm, nv = _adamw_math(refs[k][...], refs[n + k][...], refs[2 * n + k][...], refs[3 * n + k][...])
            refs[4 * n + k][...] = d
            refs[5 * n + k][...] = nm
            refs[6 * n + k][...] = nv

    vm = pl.BlockSpec(memory_space=pltpu.VMEM)
    shapes = [jax.ShapeDtypeStruct(w.shape, F32) for w in ws]
    res = pl.pallas_call(
        body, name=name, in_specs=[vm] * (4 * n), out_specs=[vm] * (3 * n), out_shape=shapes * 3,
        compiler_params=pltpu.CompilerParams(vmem_limit_bytes=VMEM_LIMIT_BYTES),
    )(*ws, *gs, *ms, *vs)
    return res[:n], res[n:2 * n], res[2 * n:]


def _place():
    return lax.axis_index("x"), lax.axis_index("y"), lax.axis_index("c")


def _other_chips(x, y):
    return [(1 - x, y), (x, 1 - y), (1 - x, 1 - y)]


HBM = pl.BlockSpec(memory_space=pl.ANY)


def all_gather_halves(name, blocks):
    n = len(blocks)

    def body(*refs):
        in_refs, out_refs = refs[:n], refs[n:2 * n]
        send_sems, recv_sems = refs[2 * n:]
        x, y, c = _place()
        me, sibling = (x, y, c), (x, y, 1 - c)
        chips = _other_chips(x, y)

        def slot(a, px, py, pc):
            return out_refs[a].at[4 * px + 2 * py + pc]

        def copy(a, k, block, to, src=None):
            return pltpu.make_async_remote_copy(
                src_ref=slot(a, *block) if src is None else src, dst_ref=slot(a, *block),
                send_sem=send_sems.at[a, k], recv_sem=recv_sems.at[a, k], device_id=to, device_id_type=MESH)

        first = []
        for a in range(n):
            first.append(copy(a, 0, me, sibling, src=in_refs[a]))
            first += [copy(a, 1 + j, me, (*chip, c), src=in_refs[a]) for j, chip in enumerate(chips)]
        for cp in first:
            cp.start()
        passed = []
        for j, chip in enumerate(chips):
            for a in range(n):
                copy(a, 1 + j, (*chip, c), me).wait_recv()
                fw = copy(a, 4 + j, (*chip, c), sibling)
                fw.start()
                passed.append(fw)
        for a in range(n):
            copy(a, 0, sibling, me).wait_recv()
            for j, chip in enumerate(chips):
                copy(a, 4 + j, (*chip, 1 - c), me).wait_recv()
        for cp in first + passed:
            cp.wait_send()

    res = pl.pallas_call(
        body, name=name, in_specs=[HBM] * n, out_specs=[HBM] * n,
        out_shape=[jax.ShapeDtypeStruct((N_DEV,) + b.shape, b.dtype) for b in blocks],
        scratch_shapes=[pltpu.SemaphoreType.DMA((n, 7)), pltpu.SemaphoreType.DMA((n, 7))],
    )(*blocks)
    return list(res)


def pair_send_halves(name, grads):
    n = len(grads)

    def body(*refs):
        in_refs, out_refs = refs[:n], refs[n:2 * n]
        send_sems, recv_sems = refs[2 * n:]
        x, y, c = _place()
        cps = []
        for a in range(n):
            cp = pltpu.make_async_remote_copy(
                src_ref=in_refs[a].at[:, 1 - c], dst_ref=out_refs[a], send_sem=send_sems.at[a], recv_sem=recv_sems.at[a],
                device_id=(x, y, 1 - c), device_id_type=MESH)
            cp.start()
            cps.append(cp)
        for cp in cps:
            cp.wait()

    res = pl.pallas_call(
        body, name=name, in_specs=[HBM] * n, out_specs=[HBM] * n,
        out_shape=[jax.ShapeDtypeStruct((g.shape[0],) + g.shape[2:], g.dtype) for g in grads],
        scratch_shapes=[pltpu.SemaphoreType.DMA((n,)), pltpu.SemaphoreType.DMA((n,))],
    )(*grads)
    return list(res)


def pair_add(name, grad, got, c_arr):
    nsh, _, M, N = grad.shape
    tr = M
    for cand in (512, 256, 192, 128, 64, 16):
        if M % cand == 0:
            tr = cand
            break

    def body(c_ref, g_ref, p_ref, o_ref):
        o_ref[...] = (g_ref[...] + p_ref[...]).astype(BF16)

    return pl.pallas_call(
        body, name=name,
        grid_spec=pltpu.PrefetchScalarGridSpec(
            num_scalar_prefetch=1, grid=(nsh, M // tr),
            in_specs=[pl.BlockSpec((None, None, tr, N), lambda j, i, c_ref: (j, c_ref[0], i, 0)),
                      pl.BlockSpec((None, tr, N), lambda j, i, c_ref: (j, i, 0))],
            out_specs=pl.BlockSpec((None, tr, N), lambda j, i, c_ref: (j, i, 0))),
        out_shape=jax.ShapeDtypeStruct((nsh, M, N), BF16), compiler_params=_params(2),
    )(c_arr, grad, got)


def scatter_to_chips(name, parts):
    n = len(parts)

    def body(*refs):
        in_refs, out_refs = refs[:n], refs[n:2 * n]
        send_sems, recv_sems = refs[2 * n:]
        x, y, c = _place()
        mine = 2 * x + y
        chips = _other_chips(x, y)
        cps = []
        for a in range(n):
            for k, (px, py) in enumerate(chips):
                cp = pltpu.make_async_remote_copy(
                    src_ref=in_refs[a].at[2 * px + py], dst_ref=out_refs[a].at[mine],
                    send_sem=send_sems.at[a, k], recv_sem=recv_sems.at[a, k], device_id=(px, py, c), device_id_type=MESH)
                cp.start()
                cps.append((cp, a, k, px, py))
        for cp, a, k, px, py in cps:
            pltpu.make_async_remote_copy(
                src_ref=in_refs[a].at[mine], dst_ref=out_refs[a].at[2 * px + py],
                send_sem=send_sems.at[a, k], recv_sem=recv_sems.at[a, k], device_id=(px, py, c), device_id_type=MESH).wait_recv()
        for cp, *_ in cps:
            cp.wait_send()

    res = pl.pallas_call(
        body, name=name, in_specs=[HBM] * n, out_specs=[HBM] * n,
        out_shape=[jax.ShapeDtypeStruct(p.shape, p.dtype) for p in parts],
        scratch_shapes=[pltpu.SemaphoreType.DMA((n, 3)), pltpu.SemaphoreType.DMA((n, 3))],
    )(*parts)
    return list(res)


def sum_chips(name, q):
    nsh, M, N = q.shape
    tr = M
    for cand in (512, 256, 192, 128, 64, 16):
        if M % cand == 0:
            tr = cand
            break

    def body(q_ref, o_ref):
        acc = q_ref[0].astype(F32)
        for j in range(1, nsh):
            acc = acc + q_ref[j].astype(F32)
        o_ref[...] = acc

    return pl.pallas_call(
        body, name=name, grid=(M // tr,), in_specs=[pl.BlockSpec((nsh, tr, N), lambda i: (0, i, 0))],
        out_specs=pl.BlockSpec((tr, N), lambda i: (i, 0)), out_shape=jax.ShapeDtypeStruct((M, N), F32),
        compiler_params=_params(1),
    )(q)


def pair_exchange(name, halves):
    n = len(halves)

    def body(*refs):
        in_refs, out_refs = refs[:n], refs[n:2 * n]
        send_sems, recv_sems = refs[2 * n:]
        x, y, c = _place()
        cps = []
        for a in range(n):
            cp = pltpu.make_async_remote_copy(
                src_ref=in_refs[a], dst_ref=out_refs[a], send_sem=send_sems.at[a], recv_sem=recv_sems.at[a],
                device_id=(x, y, 1 - c), device_id_type=MESH)
            cp.start()
            cps.append(cp)
        for cp in cps:
            cp.wait()

    res = pl.pallas_call(
        body, name=name, in_specs=[HBM] * n, out_specs=[HBM] * n,
        out_shape=[jax.ShapeDtypeStruct(h.shape, h.dtype) for h in halves],
        scratch_shapes=[pltpu.SemaphoreType.DMA((n,)), pltpu.SemaphoreType.DMA((n,))],
    )(*halves)
    return list(res)


SEM = pl.BlockSpec(memory_space=pltpu.SEMAPHORE)
IN_HBM = pl.BlockSpec(memory_space=pltpu.HBM)
SPLIT_COPY = pltpu.CompilerParams(has_side_effects=pltpu.SideEffectType.DATAFLOW_SIDE_EFFECTING)


def _scatter_copies(src_refs, dst_refs, send_sems, recv_sems):
    x, y, c = _place()
    mine = 2 * x + y
    return [pltpu.make_async_remote_copy(
        src_ref=src_refs[a].at[2 * px + py], dst_ref=dst_refs[a].at[mine], send_sem=send_sems.at[a * (N_CHIPS - 1) + k],
        recv_sem=recv_sems.at[a * (N_CHIPS - 1) + k], device_id=(px, py, c), device_id_type=MESH)
        for a in range(len(src_refs)) for k, (px, py) in enumerate(_other_chips(x, y))]


def _pair_copies(src_refs, dst_refs, send_sems, recv_sems):
    x, y, c = _place()
    return [pltpu.make_async_remote_copy(
        src_ref=src_refs[a].at[:, 1 - c], dst_ref=dst_refs[a], send_sem=send_sems.at[a], recv_sem=recv_sems.at[a],
        device_id=(x, y, 1 - c), device_id_type=MESH) for a in range(len(src_refs))]


def _gather_copies(src_refs, dst_refs, send_sems, recv_sems):
    x, y, c = _place()
    me = 4 * x + 2 * y + c
    cps = []
    for a in range(len(src_refs)):
        for k in range(1, N_DEV):
            to = (1 - x if k & 4 else x, 1 - y if k & 2 else y, 1 - c if k & 1 else c)
            s = a * (N_DEV - 1) + k - 1
            cps.append(pltpu.make_async_remote_copy(
                src_ref=src_refs[a], dst_ref=dst_refs[a].at[me], send_sem=send_sems.at[s], recv_sem=recv_sems.at[s],
                device_id=to, device_id_type=MESH))
    return cps


def split_copy_start(name, copies, n_sem, srcs, land_shapes):
    n = len(srcs)
    lands = [lax.empty(s.shape, s.dtype) for s in land_shapes]

    def body(*refs):
        for cp in copies(refs[:n], refs[n:2 * n], refs[2 * n], refs[2 * n + 1]):
            cp.start()
        refs[-1][...] = jnp.zeros_like(refs[-1])

    thru = [pltpu.HBM(a.shape, a.dtype) for a in (*srcs, *lands)]
    res = pl.pallas_call(
        body, name=name, in_specs=[IN_HBM] * (2 * n),
        out_specs=(SEM, SEM, *[IN_HBM] * (2 * n), pl.BlockSpec(memory_space=pltpu.VMEM)),
        out_shape=(pltpu.SemaphoreType.DMA((n * n_sem,)), pltpu.SemaphoreType.DMA((n * n_sem,)), *thru,
                   jax.ShapeDtypeStruct((SUBLANES, LANES), F32)),
        input_output_aliases={i: 2 + i for i in range(2 * n)}, compiler_params=SPLIT_COPY,
    )(*[pltpu.with_memory_space_constraint(a, pltpu.HBM) for a in (*srcs, *lands)])
    return (copies, n, res[0], res[1], res[2:2 + 2 * n]), res[-1][0, 0]


def split_copy_wait(name, handle, after):
    copies, n, send_sems, recv_sems, thru = handle

    def body(*refs):
        for cp in copies(refs[:n], refs[n:2 * n], refs[2 * n], refs[2 * n + 1]):
            cp.wait_send()
            cp.wait_recv()

    res = pl.pallas_call(
        body, name=name, in_specs=[IN_HBM] * (2 * n) + [SEM, SEM, pl.BlockSpec(memory_space=pl.ANY)],
        out_specs=[IN_HBM] * (2 * n), out_shape=[pltpu.HBM(a.shape, a.dtype) for a in thru],
        input_output_aliases={i: i for i in range(2 * n)}, compiler_params=SPLIT_COPY,
    )(*thru, send_sems, recv_sems, after)
    return list(res[:n]), list(res[n:])


def all_reduce_small(name, v):
    R, C = v.shape

    def body(v_ref, o_ref, gath, send_sems, recv_sems):
        x, y, c = _place()
        me, sibling = (x, y, c), (x, y, 1 - c)
        chips = _other_chips(x, y)

        def slot(px, py, pc):
            return gath.at[4 * px + 2 * py + pc]

        def copy(k, block, to, src=None):
            return pltpu.make_async_remote_copy(
                src_ref=slot(*block) if src is None else src, dst_ref=slot(*block),
                send_sem=send_sems.at[k], recv_sem=recv_sems.at[k], device_id=to, device_id_type=MESH)

        first = [copy(0, me, sibling, src=v_ref)]
        first += [copy(1 + j, me, (*chip, c), src=v_ref) for j, chip in enumerate(chips)]
        for cp in first:
            cp.start()
        slot(*me)[...] = v_ref[...]
        passed = [copy(4 + j, (*chip, c), sibling) for j, chip in enumerate(chips)]
        for j, chip in enumerate(chips):
            copy(1 + j, (*chip, c), me).wait_recv()
            passed[j].start()
        copy(0, sibling, me).wait_recv()
        for j, chip in enumerate(chips):
            copy(4 + j, (*chip, 1 - c), me).wait_recv()
        for cp in first + passed:
            cp.wait_send()
        acc = gath[0]
        for d in range(1, N_DEV):
            acc = acc + gath[d]
        o_ref[...] = acc

    vm = pl.BlockSpec(memory_space=pltpu.VMEM)
    return pl.pallas_call(
        body, name=name, in_specs=[vm], out_specs=vm, out_shape=jax.ShapeDtypeStruct((R, C), F32),
        scratch_shapes=[pltpu.VMEM((N_DEV, R, C), F32), pltpu.SemaphoreType.DMA((7,)), pltpu.SemaphoreType.DMA((7,))],
        compiler_params=pltpu.CompilerParams(vmem_limit_bytes=VMEM_LIMIT_BYTES),
    )(v)


LANES = 128
PACK_ROW_MULTIPLE = 1024
SMALL_SHARDED = {
    "w_in": ((D_MODEL, 1216), 1), "mla_w_uq": ((MLA_Q_RANK, 768), 1), "mla_w_ukv": ((MLA_KV_RANK, 1024), 1),
    "ssm_w_glu": ((SSM_WIDTH, SSM_WIDTH), 0), "w_o": ((D_MODEL, D_MODEL), 0), "xattn_w_q": ((D_MODEL, 512), 0),
    "xattn_w_kv": ((D_MODEL, 1024), 0), "xattn_w_o": ((512, D_MODEL), 1),
}
FFN_NAMES = ["ffn1_w_gate", "ffn1_w_up", "ffn1_w_down", "ffn2_w_gate", "ffn2_w_up", "ffn2_w_down"]
TRANSPOSED_VIEW = ("ffn1_w_gate", "ffn1_w_up", "ffn2_w_gate", "ffn2_w_up", "w_in", "mla_w_uq")


def _shard_shape(name):
    (r, cdim), ax = SMALL_SHARDED[name]
    return (r // N_CHIPS, cdim) if ax == 0 else (r, cdim // N_CHIPS)


def _pack_shards(shards):
    parts = []
    for name in SMALL_SHARDED:
        a = shards[name]
        lead = a.shape[:-2]
        parts.append(a.reshape(lead + (a.shape[-2] * a.shape[-1] // LANES, LANES)))
    rows = sum(q.shape[-2] for q in parts)
    parts.append(jnp.zeros(lead + (-rows % PACK_ROW_MULTIPLE, LANES), parts[0].dtype))
    return jnp.concatenate(parts, axis=-2)


def _unpack_shards(packed):
    out, r0 = {}, 0
    lead = packed.shape[:-2]
    for name in SMALL_SHARDED:
        r, cdim = _shard_shape(name)
        rows = r * cdim // LANES
        out[name] = packed[..., r0:r0 + rows, :].reshape(lead + (r, cdim))
        r0 += rows
    return out


def _full_from_shards(name, sh):
    (r, cdim), ax = SMALL_SHARDED[name]
    if ax == 0:
        return sh.reshape(r, cdim)
    return jnp.transpose(sh, (1, 0, 2)).reshape(r, cdim)


def _shards_from_full(name, full):
    (r, cdim), ax = SMALL_SHARDED[name]
    if ax == 0:
        return full.reshape(N_CHIPS, r // N_CHIPS, cdim)
    return jnp.transpose(full.reshape(r, N_CHIPS, cdim // N_CHIPS), (1, 0, 2))


SMALL_REPL = {
    "ffn1_norm": (1, 1024), "mix_norm": (1, 1024), "mla_q_norm": (1, 384), "mla_kv_norm": (1, 256),
    "mla_qk_norm_q": (1, 192), "mla_qk_norm_k": (1, 192), "ssm_a_re": (32, 64), "ssm_a_im": (32, 64),
    "ssm_log_dt": (32, 1), "ssm_b_re": (32, 64, 16), "ssm_b_im": (32, 64, 16), "ssm_c_re": (32, 16, 64),
    "ssm_c_im": (32, 16, 64), "ssm_d": (1, 512), "ssm_b_glu": (1, 512), "out_norm_mla": (1, 512),
    "out_norm_ssm": (1, 512), "xattn_norm": (1, 1024), "mem_norm": (1, 1024), "xattn_q_norm": (1, 128),
    "xattn_k_norm": (1, 128), "ffn2_norm": (1, 1024),
}


def _pack_repl(grads, loss):
    flat = jnp.concatenate([grads[n].reshape(-1) for n in SMALL_REPL] + [loss.reshape(1)])
    rows = -(-flat.shape[0] // (LANES * SUBLANES)) * SUBLANES
    return jnp.pad(flat, (0, rows * LANES - flat.shape[0])).reshape(rows, LANES)


def _unpack_repl(packed):
    flat, out, o = packed.reshape(-1), {}, 0
    for n, shp in SMALL_REPL.items():
        size = int(np.prod(shp))
        out[n] = flat[o:o + size].reshape(shp)
        o += size
    out["loss"] = flat[o]
    return out


def _rope_tables(positions):
    half = MLA_ROPE // 2
    inv = ROPE_THETA ** (-jnp.arange(half, dtype=F32) / half)
    ang = positions.astype(F32)[:, None] * inv[None, :]
    cos, sin = jnp.cos(ang), jnp.sin(ang)
    S = positions.shape[0]
    z = lambda w: jnp.zeros((S, w), F32)
    keep = jnp.concatenate([jnp.ones((S, MLA_NOPE), F32), cos, cos, z(HEAD_PAD - MLA_QK)], axis=1)
    from_hi = jnp.concatenate([z(MLA_NOPE), -sin, z(HEAD_PAD - MLA_NOPE - half)], axis=1)
    from_lo = jnp.concatenate([z(MLA_NOPE + half), sin, z(HEAD_PAD - MLA_QK)], axis=1)
    return keep, from_hi, from_lo


def _norm_rope(x, g, keep, from_hi, from_lo):
    y = x * lax.rsqrt(jnp.sum(x * x, axis=-1, keepdims=True) * (1.0 / MLA_QK) + EPS) * g
    half = MLA_ROPE // 2
    return y * keep + _lane_roll(y, HEAD_PAD - half) * from_hi + _lane_roll(y, half) * from_lo


def local_step(x, mem, positions, target, w, wb, small_weights=None, ffn2_weights=None, on_grads=None):
    if small_weights is None:
        small_weights = lambda after: {}
    if ffn2_weights is None:
        ffn2_weights = lambda after: [wb[k] for k in FFN_NAMES[3:]]
    if on_grads is None:
        on_grads = lambda tag, g: 0.0
    S = x.shape[0]
    tm = min(FUSED_TILE, S)
    g1 = (S // tm,)
    tile = lambda arr, **kw: In(arr, (tm, arr.shape[1]), lambda i: (i, 0), rows=True, **kw)
    par = lambda arr, **kw: In(arr, arr.shape, lambda *_: (0, 0), diff=True, acc=(0,), **kw)
    wt = lambda arr: In(arr, weight=True)
    otile = lambda cols, dt: Out((S, cols), dt, (tm, cols), lambda i: (i, 0))
    grads = {}

    x1, h1, g_1, u_1 = ffn_fwd("ffn1_fwd", x, w["ffn1_norm"], wb["ffn1_w_gate"], wb["ffn1_w_up"], wb["ffn1_w_down"])
    wb = {**wb, **small_weights(x1)}

    keep, from_hi, from_lo = _rope_tables(positions)
    scale = MLA_QK ** -0.5

    def f_pre(xv, kp, fh, fl, g_mix, w_q, w_kv, w_kr, w_u, g_q, g_kv, w_uq, w_ukv, gq, gk):
        h2 = _rms(xv, g_mix)
        cq, ckv = _rms(wdot(h2, w_q), g_q), _rms(wdot(h2, w_kv), g_kv)
        kr = wdot(h2, w_kr)
        qs, ks, vs = [], [], []
        for h in range(MLA_HEADS):
            qs.append(_norm_rope(wdot(cq, w_uq, h), gq, kp, fh, fl) * scale)
            kv = wdot(ckv, w_ukv, h)
            ks.append(_norm_rope(jnp.concatenate([kv[:, :MLA_NOPE], kr], axis=-1), gk, kp, fh, fl))
            vs.append(kv[:, MLA_NOPE:])
        return jnp.concatenate(qs, axis=-1), jnp.concatenate(ks, axis=-1), jnp.concatenate(vs, axis=-1), wdot(h2, w_u)

    def pre_ins(gain_mix):
        return [tile(x1, diff=True), tile(keep), tile(from_hi), tile(from_lo), par(gain_mix),
                wt(wb["w_in_q"]), wt(wb["w_in_kv"]), wt(wb["w_in_kr"]), wt(wb["w_in_u"]),
                par(w["mla_q_norm"]), par(w["mla_kv_norm"]), wt(wb["w_uq"]), wt(wb["w_ukv"]),
                par(w["qk_gain_q"]), par(w["qk_gain_k"])]

    pre_outs = [otile(MLA_HEADS * HEAD_PAD, BF16), otile(MLA_HEADS * HEAD_PAD, BF16), otile(MLA_HEADS * MLA_V, BF16),
                otile(SSM_WIDTH, F32)]
    qh, kh, v0, pu = seg_fwd("pre_mixer_fwd", f_pre, g1, pre_ins(w["mix_norm"]), pre_outs)

    o_mla, lse = attn_fwd("mla_attn_fwd", qh, kh, v0, t=min(ATTN_TILE, S))

    prep_grid = (SSM_BLOCKS,)
    prep_ins = ([_whole(w[k], diff=True, acc=(0,)) for k in ("ssm_a_re", "ssm_a_im", "ssm_log_dt")]
                + [In(w[k], (BLOCK_CH, SSM_STATE), lambda i: (i, 0), diff=True)
                   for k in ("ssm_bt_re", "ssm_bt_im", "ssm_c2_re", "ssm_c2_im")])
    blk_out = Out((SSM_BLOCKS, BLOCK_CH, PREP_LANES), BF16, (None, BLOCK_CH, PREP_LANES), lambda i: (i, 0, 0))
    prep_outs = [Out((SUBLANES, SSM_LANES), F32, (SUBLANES, PREP_LANES), lambda i: (0, i))] * 2 + [blk_out] * 4
    a_r8, a_i8, bb_r, bb_i, cb_r, ncb_i = seg_fwd("ssm_prep_fwd", _ssm_prep_f, prep_grid, prep_ins, prep_outs)

    xs_r, xs_i, y_lin = ssm_fwd("ssm_fwd", pu, bb_r, bb_i, cb_r, ncb_i, a_r8, a_i8)

    M = mem.shape[0]
    f_norm = lambda xv, g: (_rms(xv, g),)
    mem_ins = [In(mem, (M, D_MODEL), lambda i: (0, 0)), par(w["mem_norm"])]
    mem_outs = [Out((M, D_MODEL), BF16, (M, D_MODEL), lambda i: (0, 0))]
    (mn,) = seg_fwd("mem_norm_fwd", f_norm, (1,), mem_ins, mem_outs)
    (kvm,) = mm("xattn_kv_fwd", [mn], [[wb["xattn_w_kv"]]], [F32])

    def f_knorm(kk, gk):
        return (jnp.concatenate([_rms(kk[:, h * XHD:(h + 1) * XHD], gk) for h in range(XH)], axis=-1),)

    keys = ((M, XH * XHD), (M, XH * XHD), lambda i: (0, 0))
    knorm_ins = [In(kvm, keys[1], keys[2], diff=True, grad=keys), par(w["xattn_k_norm"])]
    knorm_outs = [Out(keys[0], F32, keys[1], keys[2])]
    (kn_mem,) = seg_fwd("mem_key_norm_fwd", f_knorm, (1,), knorm_ins, knorm_outs)

    xscale = XHD ** -0.5

    def f_post(o, yl, u, xv, kn, vm, d, w_glu, b, gm, gs, w_o1, w_o2, gx, w_xq, gq, w_xo):
        gl = jax.nn.gelu(yl + d * u)
        so = gl * jax.nn.sigmoid(wdot(gl, w_glu) + b)
        x2 = xv + wdot(_rms(o, gm), w_o1) + wdot(_rms(so, gs), w_o2)
        h3 = _rms(x2, gx)
        heads = []
        for h in range(XH):
            qn = _rms(wdot(h3, w_xq, h), gq)
            p = jax.nn.softmax(_bdot_nt(qn, kn[:, h * XHD:(h + 1) * XHD]) * xscale, axis=-1)
            heads.append(_bdot_nn(p, vm[:, h * XHD:(h + 1) * XHD]))
        return (x2 + wdot(jnp.concatenate(heads, axis=-1), w_xo),)

    def post_ins(gain_d):
        half = (M, XH * XHD)
        return [tile(o_mla, diff=True), tile(y_lin, diff=True), tile(pu, diff=True), tile(x1, diff=True),
                In(kn_mem, half, lambda i: (0, 0), diff=True, acc=(0,)),
                In(kvm, half, lambda i: (0, 1), diff=True, acc=(0,), grad=(half, half, lambda i: (0, 0))),
                par(gain_d), wt(wb["ssm_w_glu"]), par(w["ssm_b_glu"]), par(w["out_norm_mla"]), par(w["out_norm_ssm"]),
                wt(wb["w_o_mla"]), wt(wb["w_o_ssm"]), par(w["xattn_norm"]), wt(wb["xattn_w_q"]), par(w["xattn_q_norm"]),
                wt(wb["xattn_w_o"])]

    post_outs = [otile(D_MODEL, F32)]
    (x3,) = seg_fwd("post_mixer_fwd", f_post, g1, post_ins(w["ssm_d"]), post_outs)

    wg2, wu2, wd2 = ffn2_weights(x3)
    dx4, h4, g_2, u_2, parts = ffn_fwd("ffn2_fwd_loss", x3, w["ffn2_norm"], wg2, wu2, wd2, target=target)
    loss = jnp.sum(parts[::SUBLANES, 0])

    dx3, grads["ffn2_norm"], dg_2, du_2 = ffn_bwd_act("ffn2_bwd_act", dx4, x3, w["ffn2_norm"], g_2, u_2, wg2, wu2, wd2)
    grads["ffn2_w_gate"], grads["ffn2_w_up"], grads["ffn2_w_down"] = ffn_bwd_w("ffn2_bwd_w", h4, dx4, g_2, u_2, dg_2, du_2)
    sent = on_grads("ffn2", grads)

    (do_mla, dy_lin, du_a, dx1_a, dkn, dvm, grads["ssm_d"], grads["ssm_w_glu"], grads["ssm_b_glu"], grads["out_norm_mla"],
     grads["out_norm_ssm"], grads["w_o_mla"], grads["w_o_ssm"], grads["xattn_norm"], grads["xattn_w_q"],
     grads["xattn_q_norm"], grads["xattn_w_o"]) = seg_bwd(
        "post_mixer_bwd", f_post, g1, post_ins(w["ssm_d"] + sent), post_outs, [dx3])

    dkk, grads["xattn_k_norm"] = seg_bwd("mem_key_norm_bwd", f_knorm, (1,), knorm_ins, knorm_outs, [dkn])
    w_kv = wb["xattn_w_kv"]
    (dmn,) = mm("xattn_kv_bwd", [dkk, dvm], [[w_kv[:, :XH * XHD]], [w_kv[:, XH * XHD:]]], [F32], trans=True)
    gk_w, gv_w = mm_tn("xattn_kv_bwd_w", [mn], [dkk, dvm], [(0, [0]), (0, [1])])
    grads["xattn_w_kv"] = jnp.concatenate([gk_w, gv_w], axis=1)
    (grads["mem_norm"],) = seg_bwd("mem_norm_bwd", f_norm, (1,), mem_ins, mem_outs, [dmn])

    du, dbb_r, dbb_i, dcb_r, dncb_i, da_r8, da_i8 = ssm_bwd("ssm_bwd", dy_lin, du_a, pu, xs_r, xs_i, bb_r, bb_i, cb_r, ncb_i,
                                                            a_r8, a_i8)
    prep_g = seg_bwd("ssm_prep_bwd", _ssm_prep_f, prep_grid, prep_ins, prep_outs, [da_r8, da_i8, dbb_r, dbb_i, dcb_r, dncb_i])
    for k, gname in enumerate(("ssm_a_re", "ssm_a_im", "ssm_log_dt", "ssm_bt_re", "ssm_bt_im", "ssm_c2_re", "ssm_c2_im")):
        grads[gname] = prep_g[k]

    dqh, dkh, dv0 = attn_bwd("mla_attn_bwd", qh, kh, v0, do_mla, o_mla, lse, t=min(ATTN_TILE, S))

    (dx1, grads["mix_norm"], grads["w_in_q"], grads["w_in_kv"], grads["w_in_kr"], grads["w_in_u"], grads["mla_q_norm"],
     grads["mla_kv_norm"], grads["w_uq"], grads["w_ukv"], grads["qk_gain_q"], grads["qk_gain_k"]) = seg_bwd(
        "pre_mixer_bwd", f_pre, g1, pre_ins(w["mix_norm"]), pre_outs, [dqh, dkh, dv0, du], adds={0: dx1_a},
        row_block=FUSED_ROW_BLOCK)

    sent = on_grads("small", grads)
    dx, grads["ffn1_norm"], dg_1, du_1 = ffn_bwd_act("ffn1_bwd_act", dx1, x, w["ffn1_norm"] + sent, g_1, u_1,
                                                     wb["ffn1_w_gate"], wb["ffn1_w_up"], wb["ffn1_w_down"])
    grads["ffn1_w_gate"], grads["ffn1_w_up"], grads["ffn1_w_down"] = ffn_bwd_w("ffn1_bwd_w", h1, dx1, g_1, u_1, dg_1, du_1)
    return loss, dx, grads


def _pad_cols(a, n):
    return jnp.pad(a, ((0, 0), (0, n - a.shape[1])))


def _step_weights(shards):
    wb = {}
    w_in = _full_from_shards("w_in", shards["w_in"])
    wb["w_in_q"] = w_in[:, :MLA_Q_RANK]
    wb["w_in_kv"] = w_in[:, MLA_Q_RANK:MLA_Q_RANK + MLA_KV_RANK]
    wb["w_in_kr"] = _pad_cols(w_in[:, MLA_Q_RANK + MLA_KV_RANK:MLA_Q_RANK + MLA_KV_RANK + MLA_ROPE], LANES)
    wb["w_in_u"] = w_in[:, MLA_Q_RANK + MLA_KV_RANK + MLA_ROPE:]
    wb["w_uq"] = jnp.pad(shards["mla_w_uq"], ((0, 0), (0, 0), (0, HEAD_PAD - MLA_QK)))
    wb["w_ukv"] = shards["mla_w_ukv"]
    wb["ssm_w_glu"] = _full_from_shards("ssm_w_glu", shards["ssm_w_glu"])
    w_o = _full_from_shards("w_o", shards["w_o"])
    wb["w_o_mla"], wb["w_o_ssm"] = w_o[:SSM_WIDTH], w_o[SSM_WIDTH:]
    w_xq = _full_from_shards("xattn_w_q", shards["xattn_w_q"])
    wb["xattn_w_q"] = jnp.transpose(w_xq.reshape(D_MODEL, XH, XHD), (1, 0, 2))
    wb["xattn_w_kv"] = _full_from_shards("xattn_w_kv", shards["xattn_w_kv"])
    wb["xattn_w_o"] = _full_from_shards("xattn_w_o", shards["xattn_w_o"])
    return wb


def _sharded_grads(g):
    out = {}
    kr = g["w_in_kr"][:, :MLA_ROPE]
    out["w_in"] = _shards_from_full("w_in", jnp.concatenate([g["w_in_q"], g["w_in_kv"], kr, g["w_in_u"]], axis=1))
    out["mla_w_uq"] = g["w_uq"][:, :, :MLA_QK]
    out["mla_w_ukv"] = g["w_ukv"]
    out["ssm_w_glu"] = _shards_from_full("ssm_w_glu", g["ssm_w_glu"])
    out["w_o"] = _shards_from_full("w_o", jnp.concatenate([g["w_o_mla"], g["w_o_ssm"]], axis=0))
    w_xq = jnp.transpose(g["xattn_w_q"], (1, 0, 2)).reshape(D_MODEL, XH * XHD)
    out["xattn_w_q"] = _shards_from_full("xattn_w_q", w_xq)
    out["xattn_w_kv"] = _shards_from_full("xattn_w_kv", g["xattn_w_kv"])
    out["xattn_w_o"] = _shards_from_full("xattn_w_o", g["xattn_w_o"])
    return out


def _problem_repl_grads(g):
    out = {}
    out["mla_qk_norm_q"] = g["qk_gain_q"][:, :MLA_QK]
    out["mla_qk_norm_k"] = g["qk_gain_k"][:, :MLA_QK]
    out["ssm_b_re"] = jnp.transpose(g["ssm_bt_re"].reshape(SSM_GROUPS, SSM_GROUP, SSM_STATE), (0, 2, 1))
    out["ssm_b_im"] = jnp.transpose(g["ssm_bt_im"].reshape(SSM_GROUPS, SSM_GROUP, SSM_STATE), (0, 2, 1))
    out["ssm_c_re"] = g["ssm_c2_re"].reshape(SSM_GROUPS, SSM_GROUP, SSM_STATE)
    out["ssm_c_im"] = g["ssm_c2_im"].reshape(SSM_GROUPS, SSM_GROUP, SSM_STATE)
    for k in SMALL_REPL:
        if k not in out:
            out[k] = g[k]
    return out


def _problem_grads(g):
    out = {k: _full_from_shards(k, v) for k, v in _sharded_grads(g).items()}
    out.update(_problem_repl_grads(g))
    out.update({k: g[k] for k in FFN_NAMES})
    return out


def _step_params(p):
    row = lambda a: a.reshape(1, -1)
    w = {k: row(p[k]) for k in ("ffn1_norm", "mix_norm", "mla_q_norm", "mla_kv_norm", "ssm_b_glu", "out_norm_mla",
                                "out_norm_ssm", "xattn_norm", "mem_norm", "xattn_q_norm", "xattn_k_norm", "ffn2_norm")}
    w["qk_gain_q"] = _pad_cols(row(p["mla_qk_norm_q"]), HEAD_PAD)
    w["qk_gain_k"] = _pad_cols(row(p["mla_qk_norm_k"]), HEAD_PAD)
    w["ssm_a_re"], w["ssm_a_im"] = p["ssm_a_re"], p["ssm_a_im"]
    w["ssm_log_dt"] = p["ssm_log_dt"].reshape(SSM_GROUPS, 1)
    w["ssm_bt_re"] = jnp.transpose(p["ssm_b_re"], (0, 2, 1)).reshape(SSM_WIDTH, SSM_STATE)
    w["ssm_bt_im"] = jnp.transpose(p["ssm_b_im"], (0, 2, 1)).reshape(SSM_WIDTH, SSM_STATE)
    w["ssm_c2_re"] = p["ssm_c_re"].reshape(SSM_WIDTH, SSM_STATE)
    w["ssm_c2_im"] = p["ssm_c_im"].reshape(SSM_WIDTH, SSM_STATE)
    w["ssm_d"] = p["ssm_d"].reshape(1, SSM_WIDTH)
    return w


ARG_NAMES = ['x', 'mem', 'positions', 'ffn1_norm', 'ffn1_w_gate', 'ffn1_w_up', 'ffn1_w_down', 'mix_norm', 'w_in', 'mla_q_norm', 'mla_w_uq', 'mla_kv_norm', 'mla_w_ukv', 'mla_qk_norm_q', 'mla_qk_norm_k', 'ssm_a_re', 'ssm_a_im', 'ssm_log_dt', 'ssm_b_re', 'ssm_b_im', 'ssm_c_re', 'ssm_c_im', 'ssm_d', 'ssm_w_glu', 'ssm_b_glu', 'out_norm_mla', 'out_norm_ssm', 'w_o', 'xattn_norm', 'mem_norm', 'xattn_w_q', 'xattn_w_kv', 'xattn_q_norm', 'xattn_k_norm', 'xattn_w_o', 'ffn2_norm', 'ffn2_w_gate', 'ffn2_w_up', 'ffn2_w_down']
WEIGHT_NAMES = ARG_NAMES[3:]


def _gather_weights(p, c):
    half = lambda a: lax.dynamic_slice_in_dim(a, c * (a.shape[0] // 2), a.shape[0] // 2, axis=0)
    ffn1 = [half(p[k].astype(BF16)) for k in FFN_NAMES[:3]]
    small = [half(_pack_shards({k: p[k].astype(BF16) for k in SMALL_SHARDED}))]
    ffn2 = [half(p[k].astype(BF16)) for k in FFN_NAMES[3:]]
    me = 4 * lax.axis_index("x") + 2 * lax.axis_index("y") + c
    own = lambda got, blocks: [lax.dynamic_update_index_in_dim(g, b, me, 0) for g, b in zip(got, blocks)]
    as_shards = lambda a: a.reshape(N_CHIPS, 2 * a.shape[1], a.shape[2])
    landing = lambda blocks: [jax.ShapeDtypeStruct((N_DEV,) + b.shape, b.dtype) for b in blocks]
    got1 = own(all_gather_halves("all_gather_weights_a", ffn1), ffn1)
    got1, small = lax.optimization_barrier((got1, small))
    flight_s, sent_s = split_copy_start("gather_small_start", _gather_copies, N_DEV - 1, small, landing(small))
    sent_s, ffn2 = lax.optimization_barrier((sent_s, ffn2))
    flight_2, sent_2 = split_copy_start("gather_ffn2_start", _gather_copies, N_DEV - 1, ffn2, landing(ffn2))
    wb = {k: as_shards(a) for k, a in zip(FFN_NAMES[:3], got1)}

    def small_weights(after):
        mine, got = split_copy_wait("gather_small_wait", flight_s, after)
        return _step_weights(_unpack_shards(own(got, mine)[0].reshape(N_CHIPS, -1, LANES)))

    def ffn2_weights(after):
        mine, got = split_copy_wait("gather_ffn2_wait", flight_2, after)
        return [as_shards(a) for a in own(got, mine)]

    return wb, small_weights, ffn2_weights, sent_s + sent_2


class _GradReduce:
    def __init__(self, c):
        self.c, self.c_arr = c, jnp.reshape(c, (1,)).astype(jnp.int32)
        self.chip = 2 * lax.axis_index("x") + lax.axis_index("y")
        self.flights = []

    def start(self, tag, arrs):
        split = [a.reshape(N_CHIPS, 2, a.shape[1] // 2, a.shape[2]) for a in arrs]
        return self._scatter(tag, split, pair_send_halves(f"grad_pair_send_{tag}", split))

    def send(self, tag, arrs):
        split = [a.reshape(N_CHIPS, 2, a.shape[1] // 2, a.shape[2]) for a in arrs]
        lands = [jax.ShapeDtypeStruct((s.shape[0],) + s.shape[2:], s.dtype) for s in split]
        flight, sent = split_copy_start(f"grad_pair_send_start_{tag}", _pair_copies, 1, split, lands)
        self.sending = (tag, flight)
        return sent

    def scatter(self, tag, after):
        sent_tag, flight = self.sending
        assert sent_tag == tag
        return self._scatter(tag, *split_copy_wait(f"grad_pair_send_wait_{tag}", flight, after))

    def _scatter(self, tag, split, got):
        parts = [pair_add(f"grad_pair_add_{tag}_{k}", s, g, self.c_arr) for k, (s, g) in enumerate(zip(split, got))]
        flight, sent = split_copy_start(f"grad_scatter_start_{tag}", _scatter_copies, N_CHIPS - 1, parts, parts)
        self.flights.append((tag, flight))
        return sent

    def finish(self, after):
        halves = []
        for tag, flight in self.flights:
            parts, landed = split_copy_wait(f"grad_scatter_wait_{tag}", flight, after)
            for k, (q, p) in enumerate(zip(landed, parts)):
                mine = lax.dynamic_index_in_dim(p, self.chip, 0, keepdims=False)
                halves.append(sum_chips(f"grad_sum_{tag}_{k}", lax.dynamic_update_index_in_dim(q, mine, self.chip, 0)))
        tags = "_".join(t for t, _ in self.flights)
        self.flights = []
        theirs = pair_exchange(f"grad_pair_exchange_{tags}", halves)
        return [jnp.where(self.c == 0, jnp.concatenate([h, t], axis=0), jnp.concatenate([t, h], axis=0))
                for h, t in zip(halves, theirs)]


def kernel(x, mem, positions, ffn1_norm, ffn1_w_gate, ffn1_w_up, ffn1_w_down, mix_norm, w_in, mla_q_norm, mla_w_uq, mla_kv_norm, mla_w_ukv, mla_qk_norm_q, mla_qk_norm_k, ssm_a_re, ssm_a_im, ssm_log_dt, ssm_b_re, ssm_b_im, ssm_c_re, ssm_c_im, ssm_d, ssm_w_glu, ssm_b_glu, out_norm_mla, out_norm_ssm, w_o, xattn_norm, mem_norm, xattn_w_q, xattn_w_kv, xattn_q_norm, xattn_k_norm, xattn_w_o, ffn2_norm, ffn2_w_gate, ffn2_w_up, ffn2_w_down, loss_target, m_ffn1_norm, m_ffn1_w_gate, m_ffn1_w_up, m_ffn1_w_down, m_mix_norm, m_w_in, m_mla_q_norm, m_mla_w_uq, m_mla_kv_norm, m_mla_w_ukv, m_mla_qk_norm_q, m_mla_qk_norm_k, m_ssm_a_re, m_ssm_a_im, m_ssm_log_dt, m_ssm_b_re, m_ssm_b_im, m_ssm_c_re, m_ssm_c_im, m_ssm_d, m_ssm_w_glu, m_ssm_b_glu, m_out_norm_mla, m_out_norm_ssm, m_w_o, m_xattn_norm, m_mem_norm, m_xattn_w_q, m_xattn_w_kv, m_xattn_q_norm, m_xattn_k_norm, m_xattn_w_o, m_ffn2_norm, m_ffn2_w_gate, m_ffn2_w_up, m_ffn2_w_down, v_ffn1_norm, v_ffn1_w_gate, v_ffn1_w_up, v_ffn1_w_down, v_mix_norm, v_w_in, v_mla_q_norm, v_mla_w_uq, v_mla_kv_norm, v_mla_w_ukv, v_mla_qk_norm_q, v_mla_qk_norm_k, v_ssm_a_re, v_ssm_a_im, v_ssm_log_dt, v_ssm_b_re, v_ssm_b_im, v_ssm_c_re, v_ssm_c_im, v_ssm_d, v_ssm_w_glu, v_ssm_b_glu, v_out_norm_mla, v_out_norm_ssm, v_w_o, v_xattn_norm, v_mem_norm, v_xattn_w_q, v_xattn_w_kv, v_xattn_q_norm, v_xattn_k_norm, v_xattn_w_o, v_ffn2_norm, v_ffn2_w_gate, v_ffn2_w_up, v_ffn2_w_down):
    args = dict(locals())
    c = lax.axis_index("c")
    view = lambda k, a: jnp.swapaxes(a, 0, 1) if k in TRANSPOSED_VIEW else a
    p = {k: view(k, args[k][0]) for k in WEIGHT_NAMES}
    mom = {k: view(k, args["m_" + k][0]) for k in WEIGHT_NAMES}
    var = {k: view(k, args["v_" + k][0]) for k in WEIGHT_NAMES}
    natural = {k: view(k, p[k]) for k in WEIGHT_NAMES}

    wb, small_weights, ffn2_weights, sent = _gather_weights(
        {k: (p[k] if k in FFN_NAMES else natural[k]) for k in WEIGHT_NAMES}, c)
    w = _step_params(natural)
    w["ffn1_norm"] = w["ffn1_norm"] + sent
    early, late = _GradReduce(c), _GradReduce(c)

    def on_grads(tag, g):
        if tag == "ffn2":
            return early.send(tag, [g[k] for k in FFN_NAMES[3:]])
        packed = _pack_shards(_sharded_grads(g))
        return early.scatter("ffn2", packed[0, :SUBLANES]) + early.start(tag, [packed])

    loss, dx, g = local_step(x[0], mem[0], positions[0], loss_target[0], w, wb, small_weights, ffn2_weights, on_grads)

    sent = late.send("ffn1", [g[k] for k in FFN_NAMES[:3]])
    shards = early.finish(dx[:SUBLANES, :LANES] + sent)
    grad = dict(zip(FFN_NAMES[3:], shards[:3]))
    small_sharded = _unpack_shards(shards[3])
    grad.update({k: view(k, small_sharded[k]) for k in SMALL_SHARDED})
    reduced = all_reduce_small("grad_all_reduce_small", _pack_repl(_problem_repl_grads(g), loss + sent))
    grad.update(_unpack_repl(reduced))
    loss = grad.pop("loss")
    started = jnp.reshape(late.scatter("ffn1", reduced[:SUBLANES]), (1, 1))

    delta, new_m, new_v = {}, {}, {}
    small = [k for k in WEIGHT_NAMES if k not in FFN_NAMES and k not in SMALL_SHARDED]
    as2d = lambda a: a.reshape(-1, a.shape[-1])

    def update(k, after):
        delta[k], new_m[k], new_v[k] = adamw_big("adamw_" + k, as2d(p[k]), as2d(grad[k]), as2d(mom[k]), as2d(var[k]), after)

    last = started
    for k in WEIGHT_NAMES:
        if k not in small and k not in FFN_NAMES[:3]:
            update(k, last)
            last = delta[k]
    ds, nms, nvs = adamw_small("adamw_small", [as2d(p[k]) for k in small], [as2d(grad[k].reshape(p[k].shape)) for k in small],
                               [as2d(mom[k]) for k in small], [as2d(var[k]) for k in small])
    for k, d, nm, nv in zip(small, ds, nms, nvs):
        delta[k], new_m[k], new_v[k] = d, nm, nv

    grad.update(zip(FFN_NAMES[:3], late.finish(last)))
    for k in FFN_NAMES[:3]:
        update(k, last)

    shaped = lambda d, k: view(k, d.reshape(p[k].shape)).reshape(args[k].shape)
    return (loss, dx[None], *[shaped(grad[k], k) for k in WEIGHT_NAMES], *[shaped(delta[k], k) for k in WEIGHT_NAMES],
            *[shaped(new_m[k], k) for k in WEIGHT_NAMES], *[shaped(new_v[k], k) for k in WEIGHT_NAMES])
```

```python
import functools
import math

import jax
import jax.numpy as jnp
import numpy as np
from jax import lax
from jax.experimental import pallas as pl
from jax.experimental.pallas import tpu as pltpu

F32, BF16 = jnp.float32, jnp.bfloat16
EPS = 1e-6
MESH = pl.DeviceIdType.MESH

D_MODEL, D_FF = 1024, 2752
MLA_HEADS, MLA_Q_RANK, MLA_KV_RANK, MLA_NOPE, MLA_ROPE, MLA_V = 4, 384, 256, 128, 64, 128
MLA_QK = MLA_NOPE + MLA_ROPE
HEAD_PAD = 256
SSM_WIDTH, SSM_GROUP, SSM_GROUPS, SSM_STATE = 512, 16, 32, 64
SSM_LANES = SSM_GROUPS * SSM_STATE
XH, XHD = 4, 128
ROPE_THETA = 10000.0
ADAM_LR, ADAM_B1, ADAM_B2, ADAM_EPS, ADAM_WD, ADAM_STEP = 0.001, 0.9, 0.999, 1e-08, 0.01, 10
N_CHIPS, N_CORES, N_DEV = 4, 2, 8

VMEM_LIMIT_BYTES = 56 * 2**20
TOKEN_TILE = 512
FUSED_TILE = 256
FUSED_ROW_BLOCK = 128
FFN_ROW_BLOCK = 256
FFN_WIDE_TILE = 1024
ATTN_TILE = 512
STAT_LANES = 128
SCAN_TIME_TILE = 1024
SCAN_LANE_TILE = 512
SUBLANES = 8


def _params(n_axes):
    return pltpu.CompilerParams(dimension_semantics=("arbitrary",) * n_axes, vmem_limit_bytes=VMEM_LIMIT_BYTES)


def _first(axes):
    cond = None
    for a in axes:
        c = pl.program_id(a) == 0
        cond = c if cond is None else jnp.logical_and(cond, c)
    return cond


def mm(name, xs, ws, out_dtypes, *, trans=False, adds=None, tm=TOKEN_TILE):
    rows = xs[0].shape[0]
    tm = min(tm, rows)
    n_in, n_out = len(xs), len(out_dtypes)
    pairs = [(i, j) for i in range(n_in) for j in range(n_out) if ws[i][j] is not None]
    w_list = [ws[i][j] for (i, j) in pairs]
    adds = list(adds) if adds is not None else [None] * n_out
    add_list = [a for a in adds if a is not None]
    out_cols = [None] * n_out
    for (i, j), w in zip(pairs, w_list):
        out_cols[j] = w.shape[0] if trans else w.shape[1]
    contract = (((1,), (1 if trans else 0,)), ((), ()))

    def body(*refs):
        x_refs = refs[:n_in]
        w_refs = refs[n_in:n_in + len(pairs)]
        a_refs = list(refs[n_in + len(pairs):n_in + len(pairs) + len(add_list)])
        o_refs = refs[n_in + len(pairs) + len(add_list):]
        xb = [None] * n_in
        for j in range(n_out):
            acc = None
            for p, (i, jj) in enumerate(pairs):
                if jj != j:
                    continue
                if xb[i] is None:
                    xb[i] = x_refs[i][...].astype(BF16)
                d = lax.dot_general(xb[i], w_refs[p][...].astype(BF16), contract, preferred_element_type=F32)
                acc = d if acc is None else acc + d
            if adds[j] is not None:
                acc = acc + a_refs.pop(0)[...].astype(F32)
            o_refs[j][...] = acc.astype(o_refs[j].dtype)

    in_specs = ([pl.BlockSpec((tm, x.shape[1]), lambda i: (i, 0)) for x in xs]
                + [pl.BlockSpec(w.shape, lambda i: (0, 0)) for w in w_list]
                + [pl.BlockSpec((tm, a.shape[1]), lambda i: (i, 0)) for a in add_list])
    outs = pl.pallas_call(
        body, name=name, grid=(rows // tm,), in_specs=in_specs,
        out_specs=[pl.BlockSpec((tm, n), lambda i: (i, 0)) for n in out_cols],
        out_shape=[jax.ShapeDtypeStruct((rows, n), dt) for n, dt in zip(out_cols, out_dtypes)],
        compiler_params=_params(1),
    )(*xs, *w_list, *add_list)
    return list(outs)


def mm_tn(name, xs, dys, pairs, *, tm=TOKEN_TILE):
    rows = xs[0].shape[0]
    tm = min(tm, rows)
    n_x, n_dy = len(xs), len(dys)
    contract = (((0,), (0,)), ((), ()))

    def body(*refs):
        x_refs, dy_refs, o_refs = refs[:n_x], refs[n_x:n_x + n_dy], refs[n_x + n_dy:]
        @pl.when(pl.program_id(0) == 0)
        def _():
            for o in o_refs:
                o[...] = jnp.zeros_like(o)

        for k, (i, js) in enumerate(pairs):
            dy = None
            for j in js:
                t = dy_refs[j][...].astype(F32)
                dy = t if dy is None else dy + t
            o_refs[k][...] += lax.dot_general(x_refs[i][...].astype(BF16), dy.astype(BF16), contract,
                                              preferred_element_type=F32)

    shapes = [(xs[i].shape[1], dys[js[0]].shape[1]) for (i, js) in pairs]
    outs = pl.pallas_call(
        body, name=name, grid=(rows // tm,),
        in_specs=[pl.BlockSpec((tm, a.shape[1]), lambda i: (i, 0)) for a in (*xs, *dys)],
        out_specs=[pl.BlockSpec(s, lambda i: (0, 0)) for s in shapes],
        out_shape=[jax.ShapeDtypeStruct(s, F32) for s in shapes],
        compiler_params=_params(1),
    )(*xs, *dys)
    return list(outs)


class In:
    def __init__(self, arr, block=None, imap=None, *, diff=False, acc=None, grad=None, weight=False, rows=False):
        self.arr, self.block, self.imap, self.diff, self.acc, self.grad = arr, block, imap, diff, acc, grad
        self.weight, self.rows = weight, rows

    def spec(self):
        return pl.BlockSpec(memory_space=pltpu.VMEM) if self.weight else pl.BlockSpec(self.block, self.imap)


class Out:
    def __init__(self, shape, dtype, block, imap):
        self.shape, self.dtype, self.block, self.imap = shape, dtype, block, imap

    def spec(self):
        return pl.BlockSpec(self.block, self.imap)


class Wt:
    def __init__(self, ref, zeros=None):
        self.ref, self.zeros = ref, zeros


@jax.custom_vjp
def _wdot(a, w, z):
    return jnp.dot(a.astype(BF16), w, preferred_element_type=F32)


def _wdot_fwd(a, w, z):
    return _wdot(a, w, z), (a, w)


def _wdot_bwd(res, g):
    a, w = res
    gb = g.astype(BF16)
    da = lax.dot_general(gb, w, (((1,), (1,)), ((), ())), preferred_element_type=F32)
    dz = lax.dot_general(a.astype(BF16), gb, (((0,), (0,)), ((), ())), preferred_element_type=F32)
    return da, None, dz


_wdot.defvjp(_wdot_fwd, _wdot_bwd)


def wdot(a, wt, head=None):
    w = wt.ref[...] if head is None else wt.ref[head]
    if wt.zeros is None:
        return jnp.dot(a.astype(BF16), w, preferred_element_type=F32)
    return _wdot(a, w, wt.zeros[0 if head is None else head])


def seg_fwd(name, f, grid, ins, outs):
    n_in = len(ins)

    def body(*refs):
        res = f(*[Wt(r) if i.weight else r[...] for i, r in zip(ins, refs[:n_in])])
        for o_ref, r in zip(refs[n_in:], res):
            o_ref[...] = r.astype(o_ref.dtype)

    res = pl.pallas_call(
        body, name=name, grid=grid, in_specs=[i.spec() for i in ins], out_specs=[o.spec() for o in outs],
        out_shape=[jax.ShapeDtypeStruct(o.shape, o.dtype) for o in outs], compiler_params=_params(len(grid)),
    )(*[i.arr for i in ins])
    return list(res)


def seg_bwd(name, f, grid, ins, outs, cts, adds=None, row_block=None):
    n_in, n_ct = len(ins), len(cts)
    grad_idx = [k for k, i in enumerate(ins) if i.diff or i.weight]
    adds = adds or {}
    add_keys = sorted(adds)
    add_list = [adds[k] for k in add_keys]
    heads = {k: (ins[k].arr.shape[0] if ins[k].arr.ndim == 3 else 1) for k in grad_idx if ins[k].weight}
    tile_rows = outs[0].block[0]
    blocks = [None] if row_block is None else [pl.ds(r0, row_block) for r0 in range(0, tile_rows, row_block)]

    def body(*refs):
        in_refs, ct_refs = refs[:n_in], refs[n_in:n_in + n_ct]
        add_refs = dict(zip(add_keys, refs[n_in + n_ct:n_in + n_ct + len(add_list)]))
        g_refs = dict(zip(grad_idx, refs[n_in + n_ct + len(add_list):]))
        for k in grad_idx:
            if ins[k].weight or ins[k].acc is not None:
                @pl.when(_first(range(len(grid)) if ins[k].weight else ins[k].acc))
                def _(k=k):
                    g_refs[k][...] = jnp.zeros_like(g_refs[k])

        for rows in blocks:
            at = lambda ref, sliced: ref[...] if rows is None or not sliced else ref[rows, :]
            vals = [None if i.weight else at(r, i.rows) for i, r in zip(ins, in_refs)]
            primals, owner = [], []
            for k in grad_idx:
                if ins[k].weight:
                    for h in range(heads[k]):
                        primals.append(jnp.zeros(ins[k].arr.shape[-2:], F32))
                        owner.append((k, h))
                else:
                    primals.append(vals[k].astype(F32))
                    owner.append((k, None))

            def g(*dv, vals=vals, owner=owner):
                full = list(vals)
                zeros = {}
                for (k, h), v in zip(owner, dv):
                    if h is None:
                        full[k] = v
                    else:
                        zeros.setdefault(k, []).append(v)
                for k, z in zeros.items():
                    full[k] = Wt(in_refs[k], z)
                return tuple(f(*full))

            _, pull = jax.vjp(g, *primals)
            grads = pull(tuple(at(c, True).astype(F32) for c in ct_refs))
            for (k, h), gr in zip(owner, grads):
                if ins[k].weight:
                    if ins[k].arr.ndim == 3:
                        g_refs[k][h] += gr
                    else:
                        g_refs[k][...] += gr
                    continue
                if k in add_refs:
                    gr = gr + at(add_refs[k], True).astype(F32)
                if ins[k].acc is not None:
                    g_refs[k][...] += gr
                elif rows is None or not ins[k].rows:
                    g_refs[k][...] = gr.astype(g_refs[k].dtype)
                else:
                    g_refs[k][rows, :] = gr.astype(g_refs[k].dtype)

    g_specs, g_shapes = [], []
    for k in grad_idx:
        i = ins[k]
        if i.weight:
            g_specs.append(pl.BlockSpec(memory_space=pltpu.VMEM))
            g_shapes.append(jax.ShapeDtypeStruct(i.arr.shape, F32))
            continue
        shape, block, imap = i.grad if i.grad is not None else (i.arr.shape, i.block, i.imap)
        g_specs.append(pl.BlockSpec(block, imap))
        g_shapes.append(jax.ShapeDtypeStruct(shape, F32))
    in_specs = ([i.spec() for i in ins] + [o.spec() for o in outs]
                + [pl.BlockSpec(ins[k].block, ins[k].imap) for k in add_keys])
    res = pl.pallas_call(
        body, name=name, grid=grid, in_specs=in_specs, out_specs=g_specs, out_shape=g_shapes,
        compiler_params=_params(len(grid)),
    )(*[i.arr for i in ins], *cts, *add_list)
    return list(res)


def _rms(x, g):
    return x * lax.rsqrt(jnp.mean(x * x, axis=-1, keepdims=True) + EPS) * g


@jax.custom_vjp
def _bdot_nt(a, b):
    return lax.dot_general(a.astype(BF16), b.astype(BF16), (((1,), (1,)), ((), ())), preferred_element_type=F32)


def _bdot_nt_fwd(a, b):
    return _bdot_nt(a, b), (a, b)


def _bdot_nt_bwd(res, g):
    a, b = res
    gb = g.astype(BF16)
    da = lax.dot_general(gb, b.astype(BF16), (((1,), (0,)), ((), ())), preferred_element_type=F32)
    db = lax.dot_general(gb, a.astype(BF16), (((0,), (0,)), ((), ())), preferred_element_type=F32)
    return da, db


_bdot_nt.defvjp(_bdot_nt_fwd, _bdot_nt_bwd)


@jax.custom_vjp
def _bdot_nn(a, b):
    return lax.dot_general(a.astype(BF16), b.astype(BF16), (((1,), (0,)), ((), ())), preferred_element_type=F32)


def _bdot_nn_fwd(a, b):
    return _bdot_nn(a, b), (a, b)


def _bdot_nn_bwd(res, g):
    a, b = res
    gb = g.astype(BF16)
    da = lax.dot_general(gb, b.astype(BF16), (((1,), (1,)), ((), ())), preferred_element_type=F32)
    db = lax.dot_general(a.astype(BF16), gb, (((0,), (0,)), ((), ())), preferred_element_type=F32)
    return da, db


_bdot_nn.defvjp(_bdot_nn_fwd, _bdot_nn_bwd)


@functools.partial(jax.custom_vjp, nondiff_argnums=(1,))
def _lane_roll(x, shift):
    return pltpu.roll(x, shift, 1)


def _lane_roll_fwd(x, shift):
    return pltpu.roll(x, shift, 1), None


def _lane_roll_bwd(shift, _, g):
    return (pltpu.roll(g, (g.shape[1] - shift) % g.shape[1], 1),)


_lane_roll.defvjp(_lane_roll_fwd, _lane_roll_bwd)


def _hp_dot(a, b):
    return jnp.dot(a, b, precision=lax.Precision.HIGHEST, preferred_element_type=F32)


NT_DIMS = (((1,), (1,)), ((), ()))
TN_DIMS = (((0,), (0,)), ((), ()))


def ffn_fwd(name, x, gain, wg, wu, wd, target=None, tm=FFN_WIDE_TILE):
    S, D = x.shape
    tm = min(tm, S)
    nsh, Fs, _ = wg.shape
    with_loss = target is not None

    def body(*refs):
        if with_loss:
            x_ref, gain_ref, wg_ref, wu_ref, wd_ref, t_ref, xo_ref, h_ref, g_ref, u_ref, part_ref, acc = refs
        else:
            x_ref, gain_ref, wg_ref, wu_ref, wd_ref, xo_ref, h_ref, g_ref, u_ref, acc = refs
        j = pl.program_id(1)

        @pl.when(j == 0)
        def _():
            h_ref[...] = _rms(x_ref[...], gain_ref[...]).astype(BF16)
            acc[...] = jnp.zeros_like(acc)

        h = h_ref[...]
        g = lax.dot_general(h, wg_ref[...], NT_DIMS, preferred_element_type=F32)
        u = lax.dot_general(h, wu_ref[...], NT_DIMS, preferred_element_type=F32)
        g_ref[...] = g.astype(BF16)
        u_ref[...] = u.astype(BF16)
        a = g * jax.nn.sigmoid(g) * u
        acc[...] += jnp.dot(a.astype(BF16), wd_ref[...], preferred_element_type=F32)

        @pl.when(j == nsh - 1)
        def _():
            y = x_ref[...] + 0.5 * acc[...]
            if with_loss:
                err = y - t_ref[...]
                xo_ref[...] = err * (1.0 / D)
                part_ref[...] = jnp.full(part_ref.shape, 0.5 * jnp.sum(jnp.mean(err * err, axis=-1)), F32)
            else:
                xo_ref[...] = y

    rows = pl.BlockSpec((tm, D), lambda i, j: (i, 0))
    wspec = pl.BlockSpec((None, Fs, D), lambda i, j: (j, 0, 0))
    act = pl.BlockSpec((None, tm, Fs), lambda i, j: (j, i, 0))
    in_specs, args = [rows, pl.BlockSpec((1, D), lambda i, j: (0, 0)), wspec, wspec, wspec], [x, gain, wg, wu, wd]
    out_specs = [rows, rows, act, act]
    out_shape = [jax.ShapeDtypeStruct((S, D), F32), jax.ShapeDtypeStruct((S, D), BF16),
                 jax.ShapeDtypeStruct((nsh, S, Fs), BF16), jax.ShapeDtypeStruct((nsh, S, Fs), BF16)]
    if with_loss:
        in_specs.append(rows)
        args.append(target)
        out_specs.append(pl.BlockSpec((SUBLANES, 128), lambda i, j: (i, 0)))
        out_shape.append(jax.ShapeDtypeStruct((S // tm * SUBLANES, 128), F32))
    return pl.pallas_call(
        body, name=name, grid=(S // tm, nsh), in_specs=in_specs, out_specs=out_specs, out_shape=out_shape,
        scratch_shapes=[pltpu.VMEM((tm, D), F32)], compiler_params=_params(2),
    )(*args)


def ffn_bwd_act(name, dxo, x, gain, g, u, wg, wu, wd, tm=TOKEN_TILE):
    S, D = x.shape
    tm = min(tm, S)
    nsh, Fs, _ = wg.shape

    def body(dxo_ref, x_ref, gain_ref, g_ref, u_ref, wg_ref, wu_ref, wd_ref, dx_ref, dgain_ref, dg_ref, du_ref, dd, dh):
        i, j = pl.program_id(0), pl.program_id(1)

        @pl.when(j == 0)
        def _():
            dd[...] = (0.5 * dxo_ref[...]).astype(BF16)
            dh[...] = jnp.zeros_like(dh)

        for r0 in range(0, tm, FFN_ROW_BLOCK):
            rows = pl.ds(r0, FFN_ROW_BLOCK)
            da = lax.dot_general(dd[rows, :], wd_ref[...], NT_DIMS, preferred_element_type=F32)
            gf, uf = g_ref[rows, :].astype(F32), u_ref[rows, :].astype(F32)
            sig = jax.nn.sigmoid(gf)
            dgv = (da * uf * (sig * (1.0 + gf * (1.0 - sig)))).astype(BF16)
            duv = (da * (gf * sig)).astype(BF16)
            dg_ref[rows, :] = dgv
            du_ref[rows, :] = duv
            dh[rows, :] += (jnp.dot(dgv, wg_ref[...], preferred_element_type=F32)
                            + jnp.dot(duv, wu_ref[...], preferred_element_type=F32))

        @pl.when(j == nsh - 1)
        def _():
            xv = x_ref[...]
            r = lax.rsqrt(jnp.mean(xv * xv, axis=-1, keepdims=True) + EPS)
            xhat = xv * r
            dhv = dh[...]
            dxn = dhv * gain_ref[...]
            dx_ref[...] = dxo_ref[...] + r * (dxn - xhat * jnp.mean(dxn * xhat, axis=-1, keepdims=True))
            part = jnp.sum(dhv * xhat, axis=0, keepdims=True)

            @pl.when(i == 0)
            def _():
                dgain_ref[...] = part

            @pl.when(i != 0)
            def _():
                dgain_ref[...] += part

    return pl.pallas_call(
        body, name=name, grid=(S // tm, nsh),
        in_specs=[pl.BlockSpec((tm, D), lambda i, j: (i, 0)), pl.BlockSpec((tm, D), lambda i, j: (i, 0)),
                  pl.BlockSpec((1, D), lambda i, j: (0, 0)),
                  pl.BlockSpec((None, tm, Fs), lambda i, j: (j, i, 0)), pl.BlockSpec((None, tm, Fs), lambda i, j: (j, i, 0)),
                  pl.BlockSpec((None, Fs, D), lambda i, j: (j, 0, 0)), pl.BlockSpec((None, Fs, D), lambda i, j: (j, 0, 0)),
                  pl.BlockSpec((None, Fs, D), lambda i, j: (j, 0, 0))],
        out_specs=[pl.BlockSpec((tm, D), lambda i, j: (i, 0)), pl.BlockSpec((1, D), lambda i, j: (0, 0)),
                   pl.BlockSpec((None, tm, Fs), lambda i, j: (j, i, 0)), pl.BlockSpec((None, tm, Fs), lambda i, j: (j, i, 0))],
        out_shape=[jax.ShapeDtypeStruct((S, D), F32), jax.ShapeDtypeStruct((1, D), F32),
                   jax.ShapeDtypeStruct((nsh, S, Fs), BF16), jax.ShapeDtypeStruct((nsh, S, Fs), BF16)],
        scratch_shapes=[pltpu.VMEM((tm, D), BF16), pltpu.VMEM((tm, D), F32)], compiler_params=_params(2),
    )(dxo, x, gain, g, u, wg, wu, wd)


def ffn_bwd_w(name, h, dxo, g, u, dg, du, tm=FFN_WIDE_TILE):
    S, D = h.shape
    tm = min(tm, S)
    nsh, _, Fs = g.shape

    def body(h_ref, dxo_ref, g_ref, u_ref, dg_ref, du_ref, dwg_ref, dwu_ref, dwd_ref):
        i = pl.program_id(1)
        gf, uf = g_ref[...].astype(F32), u_ref[...].astype(F32)
        a = (gf * jax.nn.sigmoid(gf) * uf).astype(BF16)
        dd = (0.5 * dxo_ref[...]).astype(BF16)
        hv = h_ref[...]

        @pl.when(i == 0)
        def _():
            dwg_ref[...] = jnp.zeros_like(dwg_ref)
            dwu_ref[...] = jnp.zeros_like(dwu_ref)
            dwd_ref[...] = jnp.zeros_like(dwd_ref)

        dwg_ref[...] += lax.dot_general(dg_ref[...], hv, TN_DIMS, preferred_element_type=F32)
        dwu_ref[...] += lax.dot_general(du_ref[...], hv, TN_DIMS, preferred_element_type=F32)
        dwd_ref[...] += lax.dot_general(a, dd, TN_DIMS, preferred_element_type=F32)

    act = pl.BlockSpec((None, tm, Fs), lambda j, i: (j, i, 0))
    wspec = pl.BlockSpec((None, Fs, D), lambda j, i: (j, 0, 0))
    return pl.pallas_call(
        body, name=name, grid=(nsh, S // tm),
        in_specs=[pl.BlockSpec((tm, D), lambda j, i: (i, 0)), pl.BlockSpec((tm, D), lambda j, i: (i, 0)), act, act, act, act],
        out_specs=[wspec, wspec, wspec], out_shape=[jax.ShapeDtypeStruct((nsh, Fs, D), F32)] * 3,
        compiler_params=_params(2),
    )(h, dxo, g, u, dg, du)


NEG_BIG = -1e30


def _causal_pairs(n, by_key):
    pairs = [(qi, ki) for qi in range(n) for ki in range(qi + 1)]
    if by_key:
        pairs.sort(key=lambda p: (p[1], p[0]))
    return jnp.asarray([p[0] for p in pairs], jnp.int32), jnp.asarray([p[1] for p in pairs], jnp.int32)


def _scores(q, k, masked):
    s = lax.dot_general(q, k, (((1,), (1,)), ((), ())), preferred_element_type=F32)
    if masked:
        row = lax.broadcasted_iota(jnp.int32, s.shape, 0)
        col = lax.broadcasted_iota(jnp.int32, s.shape, 1)
        s = jnp.where(row >= col, s, NEG_BIG)
    return s


def attn_fwd(name, q, k, v, t=ATTN_TILE):
    S, Dk = q.shape[0], HEAD_PAD
    H = q.shape[1] // Dk
    Dv = v.shape[1] // H
    qt, kt = _causal_pairs(S // t, by_key=False)

    def body(qt_ref, kt_ref, q_ref, k_ref, v_ref, o_ref, lse_ref, m_sc, l_sc, acc):
        qi, ki = qt_ref[pl.program_id(1)], kt_ref[pl.program_id(1)]

        @pl.when(ki == 0)
        def _():
            m_sc[...] = jnp.full_like(m_sc, NEG_BIG)
            l_sc[...] = jnp.zeros_like(l_sc)
            acc[...] = jnp.zeros_like(acc)

        def step(masked):
            s = _scores(q_ref[...], k_ref[...], masked)
            m_prev = m_sc[...]
            m_next = jnp.maximum(m_prev, jnp.max(s, axis=-1, keepdims=True))
            alpha = jnp.exp(m_prev - m_next)
            p = jnp.exp(s - jnp.tile(m_next, (1, t // STAT_LANES)))
            l_sc[...] = alpha * l_sc[...] + jnp.sum(p, axis=-1, keepdims=True)
            acc[...] = alpha * acc[...] + jnp.dot(p.astype(BF16), v_ref[...].astype(BF16), preferred_element_type=F32)
            m_sc[...] = m_next

        @pl.when(ki < qi)
        def _():
            step(False)

        @pl.when(ki == qi)
        def _():
            step(True)
            o_ref[...] = acc[...] / l_sc[...]
            lse_ref[...] = m_sc[...] + jnp.log(l_sc[...])

    stat = pltpu.VMEM((t, STAT_LANES), F32)
    return pl.pallas_call(
        body, name=name,
        grid_spec=pltpu.PrefetchScalarGridSpec(
            num_scalar_prefetch=2, grid=(H, qt.shape[0]),
            in_specs=[pl.BlockSpec((t, Dk), lambda h, s, qt, kt: (qt[s], h)),
                      pl.BlockSpec((t, Dk), lambda h, s, qt, kt: (kt[s], h)),
                      pl.BlockSpec((t, Dv), lambda h, s, qt, kt: (kt[s], h))],
            out_specs=[pl.BlockSpec((t, Dv), lambda h, s, qt, kt: (qt[s], h)),
                       pl.BlockSpec((None, t, STAT_LANES), lambda h, s, qt, kt: (h, qt[s], 0))],
            scratch_shapes=[stat, stat, pltpu.VMEM((t, Dv), F32)]),
        out_shape=[jax.ShapeDtypeStruct((S, H * Dv), F32), jax.ShapeDtypeStruct((H, S, STAT_LANES), F32)],
        compiler_params=_params(2),
    )(qt, kt, q, k, v)


def attn_bwd(name, q, k, v, do, o, lse, t=ATTN_TILE):
    S, Dk = q.shape[0], HEAD_PAD
    H = q.shape[1] // Dk
    Dv = v.shape[1] // H
    qt, kt = _causal_pairs(S // t, by_key=True)
    tn_dims = (((0,), (0,)), ((), ()))

    def body(qt_ref, kt_ref, q_ref, k_ref, v_ref, do_ref, o_ref, lse_ref, dq_ref, dk_ref, dv_ref):
        step_id = pl.program_id(1)
        qi, ki = qt_ref[step_id], kt_ref[step_id]

        @pl.when(step_id == 0)
        def _():
            dq_ref[...] = jnp.zeros_like(dq_ref)

        def step(masked):
            s = _scores(q_ref[...], k_ref[...], masked)
            reps = (1, t // STAT_LANES)
            p = jnp.exp(s - jnp.tile(lse_ref[...], reps))
            dov = do_ref[...]
            delta = jnp.broadcast_to(jnp.sum(dov * o_ref[...], axis=-1, keepdims=True), (t, STAT_LANES))
            dob = dov.astype(BF16)
            dp = lax.dot_general(dob, v_ref[...].astype(BF16), (((1,), (1,)), ((), ())), preferred_element_type=F32)
            ds = (p * (dp - jnp.tile(delta, reps))).astype(BF16)
            pdv = lax.dot_general(p.astype(BF16), dob, tn_dims, preferred_element_type=F32)
            pdk = lax.dot_general(ds, q_ref[...], tn_dims, preferred_element_type=F32)
            rows = pl.ds(pl.multiple_of(qi * t, t), t)
            dq_ref[rows, :] += jnp.dot(ds, k_ref[...], preferred_element_type=F32)
            return pdk, pdv

        @pl.when(ki == qi)
        def _():
            dk_ref[...] = jnp.zeros_like(dk_ref)
            dv_ref[...] = jnp.zeros_like(dv_ref)

        def accumulate(masked):
            pdk, pdv = step(masked)
            dk_ref[...] += pdk
            dv_ref[...] += pdv

        @pl.when(ki == qi)
        def _():
            accumulate(True)

        @pl.when(ki < qi)
        def _():
            accumulate(False)

    qrow = lambda h, s, qt, kt: (qt[s], h)
    krow = lambda h, s, qt, kt: (kt[s], h)
    return pl.pallas_call(
        body, name=name,
        grid_spec=pltpu.PrefetchScalarGridSpec(
            num_scalar_prefetch=2, grid=(H, qt.shape[0]),
            in_specs=[pl.BlockSpec((t, Dk), qrow), pl.BlockSpec((t, Dk), krow), pl.BlockSpec((t, Dv), krow),
                      pl.BlockSpec((t, Dv), qrow), pl.BlockSpec((t, Dv), qrow),
                      pl.BlockSpec((None, t, STAT_LANES), lambda h, s, qt, kt: (h, qt[s], 0))],
            out_specs=[pl.BlockSpec((S, Dk), lambda h, s, qt, kt: (0, h)), pl.BlockSpec((t, Dk), krow),
                       pl.BlockSpec((t, Dv), krow)]),
        out_shape=[jax.ShapeDtypeStruct((S, H * Dk), F32), jax.ShapeDtypeStruct((S, H * Dk), F32),
                   jax.ShapeDtypeStruct((S, H * Dv), F32)],
        compiler_params=_params(2),
    )(qt, kt, q, k, v, do, o, lse)


def _cmul(ar, ai, br, bi):
    return ar * br - ai * bi, ar * bi + ai * br


def _scan_tile(x_r, x_i, ar_ref, ai_ref, cr_sc, ci_sc, *, reverse, first, states=None):
    tc, lanes = x_r.shape
    nblk, lb = tc // SUBLANES, SCAN_LANE_TILE
    with_da = states is not None
    if with_da:
        xr_all, xi_all, pr_all, pi_all, dar_all, dai_all, chunk = states

    @pl.when(first)
    def _():
        cr_sc[...] = jnp.zeros_like(cr_sc)
        ci_sc[...] = jnp.zeros_like(ci_sc)
        if with_da:
            dar_all[...] = jnp.zeros_like(dar_all)
            dai_all[...] = jnp.zeros_like(dai_all)

    row = lax.broadcasted_iota(jnp.int32, (SUBLANES, lb), 0)
    for l0 in range(0, lanes, lb):
        _scan_lanes(x_r.at[:, pl.ds(l0, lb)], x_i.at[:, pl.ds(l0, lb)], ar_ref[0:1, pl.ds(l0, lb)],
                    ai_ref[0:1, pl.ds(l0, lb)], cr_sc.at[:, pl.ds(l0, lb)], ci_sc.at[:, pl.ds(l0, lb)], row, reverse,
                    nblk, None if not with_da else tuple(r.at[:, pl.ds(l0, lb)] for r in states[:6]) + (chunk,))


def _scan_lanes(x_r, x_i, a1r, a1i, cr_sc, ci_sc, row, reverse, nblk, states):
    lb = x_r.shape[1]
    with_da = states is not None
    if with_da:
        xr_ref, xi_ref, pr_ref, pi_ref, dar_ref, dai_ref, chunk = states
    if reverse:
        a1i = -a1i
    a2r, a2i = _cmul(a1r, a1i, a1r, a1i)
    a4r, a4i = _cmul(a2r, a2i, a2r, a2i)
    pw_r, pw_i = jnp.zeros((SUBLANES, lb), F32), jnp.zeros((SUBLANES, lb), F32)
    cur_r, cur_i = a1r, a1i
    for e in range(SUBLANES):
        r_at = (SUBLANES - 1 - e) if reverse else e
        pw_r = jnp.where(row == r_at, cur_r, pw_r)
        pw_i = jnp.where(row == r_at, cur_i, pw_i)
        cur_r, cur_i = _cmul(cur_r, cur_i, a1r, a1i)
    steps = []
    for d, pr, pi in ((1, a1r, a1i), (2, a2r, a2i), (4, a4r, a4i)):
        keep = (row < SUBLANES - d) if reverse else (row >= d)
        steps.append((d, jnp.where(keep, pr, 0.0), jnp.where(keep, pi, 0.0)))

    def block(jb, carry):
        if with_da:
            cr, ci, acc_r, acc_i = carry
        else:
            cr, ci = carry
        idx = (nblk - 1 - jb) if reverse else jb
        r0 = pl.multiple_of(idx * SUBLANES, SUBLANES)
        xr = x_r[pl.ds(r0, SUBLANES), :]
        xi = x_i[pl.ds(r0, SUBLANES), :]
        for d, pr, pi in steps:
            shift = SUBLANES - d if reverse else d
            sr, si = pltpu.roll(xr, shift, 0), pltpu.roll(xi, shift, 0)
            xr, xi = xr + pr * sr - pi * si, xi + pr * si + pi * sr
        xr, xi = xr + pw_r * cr - pw_i * ci, xi + pw_r * ci + pw_i * cr
        x_r[pl.ds(r0, SUBLANES), :] = xr
        x_i[pl.ds(r0, SUBLANES), :] = xi
        edge = 0 if reverse else SUBLANES - 1
        cr, ci = xr[edge:edge + 1, :], xi[edge:edge + 1, :]
        if not with_da:
            return cr, ci
        fr = xr_ref[pl.ds(r0, SUBLANES), :]
        fi = xi_ref[pl.ds(r0, SUBLANES), :]
        rp = pl.multiple_of(jnp.maximum(idx - 1, 0) * SUBLANES, SUBLANES)
        inside = idx > 0
        before_r = jnp.where(inside, xr_ref[pl.ds(rp, SUBLANES), :], pr_ref[...])
        before_i = jnp.where(inside, xi_ref[pl.ds(rp, SUBLANES), :], pi_ref[...])
        live = jnp.where(jnp.logical_or(inside, chunk > 0), 1.0, 0.0)
        last_r = before_r[SUBLANES - 1:SUBLANES, :] * live
        last_i = before_i[SUBLANES - 1:SUBLANES, :] * live
        pvr = jnp.where(row == 0, last_r, pltpu.roll(fr, 1, 0))
        pvi = jnp.where(row == 0, last_i, pltpu.roll(fi, 1, 0))
        acc_r = acc_r + xr * pvr + xi * pvi
        acc_i = acc_i + xi * pvr - xr * pvi
        return cr, ci, acc_r, acc_i

    init = (cr_sc[...], ci_sc[...])
    if with_da:
        init = init + (jnp.zeros((SUBLANES, lb), F32), jnp.zeros((SUBLANES, lb), F32))
    fin = lax.fori_loop(0, nblk, block, init)
    cr_sc[...] = fin[0]
    ci_sc[...] = fin[1]
    if with_da:
        dar_ref[...] += fin[2]
        dai_ref[...] += fin[3]


SSM_BLOCKS = 4
BLOCK_CH = SSM_WIDTH // SSM_BLOCKS
PREP_LANES = SSM_LANES // SSM_BLOCKS


def ssm_fwd(name, u, bb_r, bb_i, cb_r, ncb_i, a_r8, a_i8):
    S = u.shape[0]
    tc = min(SCAN_TIME_TILE, S)

    def body(u_ref, bbr_ref, bbi_ref, cbr_ref, ncbi_ref, ar_ref, ai_ref, xr_ref, xi_ref, y_ref, cr_sc, ci_sc):
        ub = u_ref[...].astype(BF16)
        xr_ref[...] = jnp.dot(ub, bbr_ref[...], preferred_element_type=F32)
        xi_ref[...] = jnp.dot(ub, bbi_ref[...], preferred_element_type=F32)
        _scan_tile(xr_ref, xi_ref, ar_ref, ai_ref, cr_sc, ci_sc, reverse=False, first=pl.program_id(1) == 0)
        y_ref[...] = (lax.dot_general(xr_ref[...].astype(BF16), cbr_ref[...], NT_DIMS, preferred_element_type=F32)
                      + lax.dot_general(xi_ref[...].astype(BF16), ncbi_ref[...], NT_DIMS, preferred_element_type=F32))

    ch = pl.BlockSpec((tc, BLOCK_CH), lambda b, t: (t, b))
    st = pl.BlockSpec((tc, PREP_LANES), lambda b, t: (t, b))
    wt = pl.BlockSpec((None, BLOCK_CH, PREP_LANES), lambda b, t: (b, 0, 0))
    par = pl.BlockSpec((SUBLANES, PREP_LANES), lambda b, t: (0, b))
    return pl.pallas_call(
        body, name=name, grid=(SSM_BLOCKS, S // tc), in_specs=[ch, wt, wt, wt, wt, par, par], out_specs=[st, st, ch],
        out_shape=[jax.ShapeDtypeStruct((S, SSM_LANES), F32), jax.ShapeDtypeStruct((S, SSM_LANES), F32),
                   jax.ShapeDtypeStruct((S, SSM_WIDTH), F32)],
        scratch_shapes=[pltpu.VMEM((1, PREP_LANES), F32), pltpu.VMEM((1, PREP_LANES), F32)], compiler_params=_params(2),
    )(u, bb_r, bb_i, cb_r, ncb_i, a_r8, a_i8)


def ssm_bwd(name, dy, du_add, u, xs_r, xs_i, bb_r, bb_i, cb_r, ncb_i, a_r8, a_i8):
    S = u.shape[0]
    tc = min(SCAN_TIME_TILE, S)
    nt = S // tc

    def body(dy_ref, dua_ref, u_ref, xr_ref, xi_ref, pr_ref, pi_ref, bbr_ref, bbi_ref, cbr_ref, ncbi_ref, ar_ref, ai_ref,
             du_ref, dbbr_ref, dbbi_ref, dcbr_ref, dncbi_ref, dar_ref, dai_ref, lr_sc, li_sc, cr_sc, ci_sc):
        t = pl.program_id(1)
        first = t == 0

        @pl.when(first)
        def _():
            for r in (dbbr_ref, dbbi_ref, dcbr_ref, dncbi_ref):
                r[...] = jnp.zeros_like(r)

        dyb = dy_ref[...].astype(BF16)
        lr_sc[...] = jnp.dot(dyb, cbr_ref[...], preferred_element_type=F32)
        li_sc[...] = jnp.dot(dyb, ncbi_ref[...], preferred_element_type=F32)
        _scan_tile(lr_sc, li_sc, ar_ref, ai_ref, cr_sc, ci_sc, reverse=True, first=first,
                   states=(xr_ref, xi_ref, pr_ref, pi_ref, dar_ref, dai_ref, nt - 1 - t))
        lrb, lib = lr_sc[...].astype(BF16), li_sc[...].astype(BF16)
        du_ref[...] = (lax.dot_general(lrb, bbr_ref[...], NT_DIMS, preferred_element_type=F32)
                       + lax.dot_general(lib, bbi_ref[...], NT_DIMS, preferred_element_type=F32) + dua_ref[...])
        ub = u_ref[...].astype(BF16)
        dbbr_ref[...] += lax.dot_general(ub, lrb, TN_DIMS, preferred_element_type=F32)
        dbbi_ref[...] += lax.dot_general(ub, lib, TN_DIMS, preferred_element_type=F32)
        dcbr_ref[...] += lax.dot_general(dyb, xr_ref[...].astype(BF16), TN_DIMS, preferred_element_type=F32)
        dncbi_ref[...] += lax.dot_general(dyb, xi_ref[...].astype(BF16), TN_DIMS, preferred_element_type=F32)

    ch = pl.BlockSpec((tc, BLOCK_CH), lambda b, t: (nt - 1 - t, b))
    st = pl.BlockSpec((tc, PREP_LANES), lambda b, t: (nt - 1 - t, b))
    prev = pl.BlockSpec((SUBLANES, PREP_LANES), lambda b, t: (jnp.maximum((nt - 1 - t) * (tc // SUBLANES) - 1, 0), b))
    wt = pl.BlockSpec((None, BLOCK_CH, PREP_LANES), lambda b, t: (b, 0, 0))
    par = pl.BlockSpec((SUBLANES, PREP_LANES), lambda b, t: (0, b))
    blk = jax.ShapeDtypeStruct((SSM_BLOCKS, BLOCK_CH, PREP_LANES), F32)
    rows8 = jax.ShapeDtypeStruct((SUBLANES, SSM_LANES), F32)
    return pl.pallas_call(
        body, name=name, grid=(SSM_BLOCKS, nt), in_specs=[ch, ch, ch, st, st, prev, prev, wt, wt, wt, wt, par, par],
        out_specs=[ch, wt, wt, wt, wt, par, par],
        out_shape=[jax.ShapeDtypeStruct((S, SSM_WIDTH), F32), blk, blk, blk, blk, rows8, rows8],
        scratch_shapes=[pltpu.VMEM((tc, PREP_LANES), F32), pltpu.VMEM((tc, PREP_LANES), F32),
                        pltpu.VMEM((1, PREP_LANES), F32), pltpu.VMEM((1, PREP_LANES), F32)],
        compiler_params=_params(2),
    )(dy, du_add, u, xs_r, xs_i, xs_r, xs_i, bb_r, bb_i, cb_r, ncb_i, a_r8, a_i8)


def _ssm_prep_f(a_re, a_im, log_dt, bt_re, bt_im, c_re, c_im):
    first_group = pl.program_id(0) * (SSM_GROUPS // SSM_BLOCKS)
    iota = lambda shape, d: lax.broadcasted_iota(jnp.int32, shape, d)
    grp_of_row = lambda shape: iota(shape, 0) >> int(math.log2(SSM_GROUP))
    grp_of_lane = lambda shape: iota(shape, 1) >> int(math.log2(SSM_STATE))
    rep = (grp_of_row((BLOCK_CH, SSM_GROUPS)) + first_group == iota((BLOCK_CH, SSM_GROUPS), 1)).astype(F32)
    til = ((iota((SSM_STATE, PREP_LANES), 1) & (SSM_STATE - 1)) == iota((SSM_STATE, PREP_LANES), 0)).astype(F32)
    m_rows = (grp_of_row((BLOCK_CH, PREP_LANES)) == grp_of_lane((BLOCK_CH, PREP_LANES))).astype(F32)
    m_grp = (iota((SSM_GROUPS, PREP_LANES), 0) == grp_of_lane((SSM_GROUPS, PREP_LANES)) + first_group).astype(F32)
    dt = jnp.exp(log_dt)
    decay = jnp.exp(a_re * dt)
    ar = decay * jnp.cos(a_im * dt)
    ai = decay * jnp.sin(a_im * dt)
    den = a_re * a_re + a_im * a_im
    nr = ar - 1.0
    coef_r = (nr * a_re + ai * a_im) / den
    coef_i = (ai * a_re - nr * a_im) / den
    cr, ci = _hp_dot(rep, coef_r), _hp_dot(rep, coef_i)
    bb_r = cr * bt_re - ci * bt_im
    bb_i = cr * bt_im + ci * bt_re
    big = lambda m: _hp_dot(m, til) * m_rows
    lanes = lambda m: jnp.broadcast_to(jnp.sum(_hp_dot(m, til) * m_grp, axis=0, keepdims=True), (SUBLANES, PREP_LANES))
    return lanes(ar), lanes(ai), big(bb_r), big(bb_i), big(c_re), -big(c_im)


def _whole(arr, **kw):
    nd = arr.ndim
    return In(arr, arr.shape, lambda *_: (0,) * nd, **kw)


def _adamw_math(w, g, m, v):
    m = ADAM_B1 * m + (1.0 - ADAM_B1) * g
    v = ADAM_B2 * v + (1.0 - ADAM_B2) * (g * g)
    m_hat = m / (1.0 - ADAM_B1 ** ADAM_STEP)
    v_hat = v / (1.0 - ADAM_B2 ** ADAM_STEP)
    delta = -ADAM_LR * (m_hat / (jnp.sqrt(v_hat) + ADAM_EPS) + ADAM_WD * w)
    return delta, m, v


def adamw_big(name, w, g, m, v, after):
    R, C = w.shape
    tr = R
    for cand in (512, 344, 256, 128):
        if R % cand == 0:
            tr = cand
            break

    def body(w_ref, g_ref, m_ref, v_ref, after_ref, d_ref, nm_ref, nv_ref):
        d, nm, nv = _adamw_math(w_ref[...], g_ref[...], m_ref[...], v_ref[...])
        d_ref[...] = d
        nm_ref[...] = nm
        nv_ref[...] = nv

    spec = pl.BlockSpec((tr, C), lambda i: (i, 0))
    return pl.pallas_call(
        body, name=name, grid=(R // tr,), in_specs=[spec] * 4 + [pl.BlockSpec(memory_space=pl.ANY)], out_specs=[spec] * 3,
        out_shape=[jax.ShapeDtypeStruct((R, C), F32)] * 3, compiler_params=_params(1),
    )(w, g, m, v, after)


def adamw_small(name, ws, gs, ms, vs, after):
    n = len(ws)

    def body(*refs):
        outs = refs[4 * n + 1:]
        for k in range(n):
            d, nm, nv = _adamw_math(refs[k][...], refs[n + k][...], refs[2 * n + k][...], refs[3 * n + k][...])
            outs[k][...] = d
            outs[n + k][...] = nm
            outs[2 * n + k][...] = nv

    vm = pl.BlockSpec(memory_space=pltpu.VMEM)
    shapes = [jax.ShapeDtypeStruct(w.shape, F32) for w in ws]
    res = pl.pallas_call(
        body, name=name, in_specs=[vm] * (4 * n) + [pl.BlockSpec(memory_space=pl.ANY)], out_specs=[vm] * (3 * n),
        out_shape=shapes * 3, compiler_params=pltpu.CompilerParams(vmem_limit_bytes=VMEM_LIMIT_BYTES),
    )(*ws, *gs, *ms, *vs, after)
    return res[:n], res[n:2 * n], res[2 * n:]


def _place():
    return lax.axis_index("x"), lax.axis_index("y"), lax.axis_index("c")


def _other_chips(x, y):
    return [(1 - x, y), (x, 1 - y), (1 - x, 1 - y)]


HBM = pl.BlockSpec(memory_space=pl.ANY)


def all_gather_halves(name, blocks):
    n = len(blocks)

    def body(*refs):
        in_refs, out_refs = refs[:n], refs[n:2 * n]
        send_sems, recv_sems = refs[2 * n:]
        x, y, c = _place()
        me, sibling = (x, y, c), (x, y, 1 - c)
        chips = _other_chips(x, y)

        def slot(a, px, py, pc):
            return out_refs[a].at[4 * px + 2 * py + pc]

        def copy(a, k, block, to, src=None):
            return pltpu.make_async_remote_copy(
                src_ref=slot(a, *block) if src is None else src, dst_ref=slot(a, *block),
                send_sem=send_sems.at[a, k], recv_sem=recv_sems.at[a, k], device_id=to, device_id_type=MESH)

        first = []
        for a in range(n):
            first.append(copy(a, 0, me, sibling, src=in_refs[a]))
            first += [copy(a, 1 + j, me, (*chip, c), src=in_refs[a]) for j, chip in enumerate(chips)]
        for cp in first:
            cp.start()
        passed = []
        for j, chip in enumerate(chips):
            for a in range(n):
                copy(a, 1 + j, (*chip, c), me).wait_recv()
                fw = copy(a, 4 + j, (*chip, c), sibling)
                fw.start()
                passed.append(fw)
        for a in range(n):
            copy(a, 0, sibling, me).wait_recv()
            for j, chip in enumerate(chips):
                copy(a, 4 + j, (*chip, 1 - c), me).wait_recv()
        for cp in first + passed:
            cp.wait_send()

    res = pl.pallas_call(
        body, name=name, in_specs=[HBM] * n, out_specs=[HBM] * n,
        out_shape=[jax.ShapeDtypeStruct((N_DEV,) + b.shape, b.dtype) for b in blocks],
        scratch_shapes=[pltpu.SemaphoreType.DMA((n, 7)), pltpu.SemaphoreType.DMA((n, 7))],
    )(*blocks)
    return list(res)


def pair_send_halves(name, grads):
    n = len(grads)

    def body(*refs):
        in_refs, out_refs = refs[:n], refs[n:2 * n]
        send_sems, recv_sems = refs[2 * n:]
        x, y, c = _place()
        cps = []
        for a in range(n):
            cp = pltpu.make_async_remote_copy(
                src_ref=in_refs[a].at[:, 1 - c], dst_ref=out_refs[a], send_sem=send_sems.at[a], recv_sem=recv_sems.at[a],
                device_id=(x, y, 1 - c), device_id_type=MESH)
            cp.start()
            cps.append(cp)
        for cp in cps:
            cp.wait()

    res = pl.pallas_call(
        body, name=name, in_specs=[HBM] * n, out_specs=[HBM] * n,
        out_shape=[jax.ShapeDtypeStruct((g.shape[0],) + g.shape[2:], g.dtype) for g in grads],
        scratch_shapes=[pltpu.SemaphoreType.DMA((n,)), pltpu.SemaphoreType.DMA((n,))],
    )(*grads)
    return list(res)


def pair_add(name, grad, got, c_arr):
    nsh, _, M, N = grad.shape
    tr = M
    for cand in (512, 256, 192, 128, 64, 16):
        if M % cand == 0:
            tr = cand
            break

    def body(c_ref, g_ref, p_ref, o_ref):
        o_ref[...] = (g_ref[...] + p_ref[...]).astype(BF16)

    return pl.pallas_call(
        body, name=name,
        grid_spec=pltpu.PrefetchScalarGridSpec(
            num_scalar_prefetch=1, grid=(nsh, M // tr),
            in_specs=[pl.BlockSpec((None, None, tr, N), lambda j, i, c_ref: (j, c_ref[0], i, 0)),
                      pl.BlockSpec((None, tr, N), lambda j, i, c_ref: (j, i, 0))],
            out_specs=pl.BlockSpec((None, tr, N), lambda j, i, c_ref: (j, i, 0))),
        out_shape=jax.ShapeDtypeStruct((nsh, M, N), BF16), compiler_params=_params(2),
    )(c_arr, grad, got)


def scatter_to_chips(name, parts):
    n = len(parts)

    def body(*refs):
        in_refs, out_refs = refs[:n], refs[n:2 * n]
        send_sems, recv_sems = refs[2 * n:]
        x, y, c = _place()
        mine = 2 * x + y
        chips = _other_chips(x, y)
        cps = []
        for a in range(n):
            for k, (px, py) in enumerate(chips):
                cp = pltpu.make_async_remote_copy(
                    src_ref=in_refs[a].at[2 * px + py], dst_ref=out_refs[a].at[mine],
                    send_sem=send_sems.at[a, k], recv_sem=recv_sems.at[a, k], device_id=(px, py, c), device_id_type=MESH)
                cp.start()
                cps.append((cp, a, k, px, py))
        for cp, a, k, px, py in cps:
            pltpu.make_async_remote_copy(
                src_ref=in_refs[a].at[mine], dst_ref=out_refs[a].at[2 * px + py],
                send_sem=send_sems.at[a, k], recv_sem=recv_sems.at[a, k], device_id=(px, py, c), device_id_type=MESH).wait_recv()
        for cp, *_ in cps:
            cp.wait_send()

    res = pl.pallas_call(
        body, name=name, in_specs=[HBM] * n, out_specs=[HBM] * n,
        out_shape=[jax.ShapeDtypeStruct(p.shape, p.dtype) for p in parts],
        scratch_shapes=[pltpu.SemaphoreType.DMA((n, 3)), pltpu.SemaphoreType.DMA((n, 3))],
    )(*parts)
    return list(res)


def sum_chips(name, q):
    nsh, M, N = q.shape
    tr = M
    for cand in (512, 256, 192, 128, 64, 16):
        if M % cand == 0:
            tr = cand
            break

    def body(q_ref, o_ref):
        acc = q_ref[0].astype(F32)
        for j in range(1, nsh):
            acc = acc + q_ref[j].astype(F32)
        o_ref[...] = acc

    return pl.pallas_call(
        body, name=name, grid=(M // tr,), in_specs=[pl.BlockSpec((nsh, tr, N), lambda i: (0, i, 0))],
        out_specs=pl.BlockSpec((tr, N), lambda i: (i, 0)), out_shape=jax.ShapeDtypeStruct((M, N), F32),
        compiler_params=_params(1),
    )(q)


def pair_exchange(name, halves):
    n = len(halves)

    def body(*refs):
        in_refs, out_refs = refs[:n], refs[n:2 * n]
        send_sems, recv_sems = refs[2 * n:]
        x, y, c = _place()
        cps = []
        for a in range(n):
            cp = pltpu.make_async_remote_copy(
                src_ref=in_refs[a], dst_ref=out_refs[a], send_sem=send_sems.at[a], recv_sem=recv_sems.at[a],
                device_id=(x, y, 1 - c), device_id_type=MESH)
            cp.start()
            cps.append(cp)
        for cp in cps:
            cp.wait()

    res = pl.pallas_call(
        body, name=name, in_specs=[HBM] * n, out_specs=[HBM] * n,
        out_shape=[jax.ShapeDtypeStruct(h.shape, h.dtype) for h in halves],
        scratch_shapes=[pltpu.SemaphoreType.DMA((n,)), pltpu.SemaphoreType.DMA((n,))],
    )(*halves)
    return list(res)


SEM = pl.BlockSpec(memory_space=pltpu.SEMAPHORE)
IN_HBM = pl.BlockSpec(memory_space=pltpu.HBM)
SPLIT_COPY = pltpu.CompilerParams(has_side_effects=pltpu.SideEffectType.DATAFLOW_SIDE_EFFECTING)


def _scatter_copies(src_refs, dst_refs, send_sems, recv_sems):
    x, y, c = _place()
    mine = 2 * x + y
    return [pltpu.make_async_remote_copy(
        src_ref=src_refs[a].at[2 * px + py], dst_ref=dst_refs[a].at[mine], send_sem=send_sems.at[a * (N_CHIPS - 1) + k],
        recv_sem=recv_sems.at[a * (N_CHIPS - 1) + k], device_id=(px, py, c), device_id_type=MESH)
        for a in range(len(src_refs)) for k, (px, py) in enumerate(_other_chips(x, y))]


def _pair_copies(src_refs, dst_refs, send_sems, recv_sems):
    x, y, c = _place()
    return [pltpu.make_async_remote_copy(
        src_ref=src_refs[a].at[:, 1 - c], dst_ref=dst_refs[a], send_sem=send_sems.at[a], recv_sem=recv_sems.at[a],
        device_id=(x, y, 1 - c), device_id_type=MESH) for a in range(len(src_refs))]


def _gather_copies(src_refs, dst_refs, send_sems, recv_sems):
    x, y, c = _place()
    me = 4 * x + 2 * y + c
    cps = []
    for a in range(len(src_refs)):
        for k in range(1, N_DEV):
            to = (1 - x if k & 4 else x, 1 - y if k & 2 else y, 1 - c if k & 1 else c)
            s = a * (N_DEV - 1) + k - 1
            cps.append(pltpu.make_async_remote_copy(
                src_ref=src_refs[a], dst_ref=dst_refs[a].at[me], send_sem=send_sems.at[s], recv_sem=recv_sems.at[s],
                device_id=to, device_id_type=MESH))
    return cps


def split_copy_start(name, copies, n_sem, srcs, land_shapes):
    n = len(srcs)
    lands = [lax.empty(s.shape, s.dtype) for s in land_shapes]

    def body(*refs):
        for cp in copies(refs[:n], refs[n:2 * n], refs[2 * n], refs[2 * n + 1]):
            cp.start()
        refs[-1][...] = jnp.zeros_like(refs[-1])

    thru = [pltpu.HBM(a.shape, a.dtype) for a in (*srcs, *lands)]
    res = pl.pallas_call(
        body, name=name, in_specs=[IN_HBM] * (2 * n),
        out_specs=(SEM, SEM, *[IN_HBM] * (2 * n), pl.BlockSpec(memory_space=pltpu.VMEM)),
        out_shape=(pltpu.SemaphoreType.DMA((n * n_sem,)), pltpu.SemaphoreType.DMA((n * n_sem,)), *thru,
                   jax.ShapeDtypeStruct((SUBLANES, LANES), F32)),
        input_output_aliases={i: 2 + i for i in range(2 * n)}, compiler_params=SPLIT_COPY,
    )(*[pltpu.with_memory_space_constraint(a, pltpu.HBM) for a in (*srcs, *lands)])
    return (copies, n, res[0], res[1], res[2:2 + 2 * n]), res[-1][0, 0]


def split_copy_wait(name, handle, after):
    copies, n, send_sems, recv_sems, thru = handle

    def body(*refs):
        for cp in copies(refs[:n], refs[n:2 * n], refs[2 * n], refs[2 * n + 1]):
            cp.wait_send()
            cp.wait_recv()

    res = pl.pallas_call(
        body, name=name, in_specs=[IN_HBM] * (2 * n) + [SEM, SEM, pl.BlockSpec(memory_space=pl.ANY)],
        out_specs=[IN_HBM] * (2 * n), out_shape=[pltpu.HBM(a.shape, a.dtype) for a in thru],
        input_output_aliases={i: i for i in range(2 * n)}, compiler_params=SPLIT_COPY,
    )(*thru, send_sems, recv_sems, after)
    return list(res[:n]), list(res[n:])


def all_reduce_small(name, v):
    R, C = v.shape

    def body(v_ref, o_ref, gath, send_sems, recv_sems):
        x, y, c = _place()
        me, sibling = (x, y, c), (x, y, 1 - c)
        chips = _other_chips(x, y)

        def slot(px, py, pc):
            return gath.at[4 * px + 2 * py + pc]

        def copy(k, block, to, src=None):
            return pltpu.make_async_remote_copy(
                src_ref=slot(*block) if src is None else src, dst_ref=slot(*block),
                send_sem=send_sems.at[k], recv_sem=recv_sems.at[k], device_id=to, device_id_type=MESH)

        first = [copy(0, me, sibling, src=v_ref)]
        first += [copy(1 + j, me, (*chip, c), src=v_ref) for j, chip in enumerate(chips)]
        for cp in first:
            cp.start()
        slot(*me)[...] = v_ref[...]
        passed = [copy(4 + j, (*chip, c), sibling) for j, chip in enumerate(chips)]
        for j, chip in enumerate(chips):
            copy(1 + j, (*chip, c), me).wait_recv()
            passed[j].start()
        copy(0, sibling, me).wait_recv()
        for j, chip in enumerate(chips):
            copy(4 + j, (*chip, 1 - c), me).wait_recv()
        for cp in first + passed:
            cp.wait_send()
        acc = gath[0]
        for d in range(1, N_DEV):
            acc = acc + gath[d]
        o_ref[...] = acc

    vm = pl.BlockSpec(memory_space=pltpu.VMEM)
    return pl.pallas_call(
        body, name=name, in_specs=[vm], out_specs=vm, out_shape=jax.ShapeDtypeStruct((R, C), F32),
        scratch_shapes=[pltpu.VMEM((N_DEV, R, C), F32), pltpu.SemaphoreType.DMA((7,)), pltpu.SemaphoreType.DMA((7,))],
        compiler_params=pltpu.CompilerParams(vmem_limit_bytes=VMEM_LIMIT_BYTES),
    )(v)


LANES = 128
PACK_ROW_MULTIPLE = 1024
SMALL_SHARDED = {
    "w_in": ((D_MODEL, 1216), 1), "mla_w_uq": ((MLA_Q_RANK, 768), 1), "mla_w_ukv": ((MLA_KV_RANK, 1024), 1),
    "ssm_w_glu": ((SSM_WIDTH, SSM_WIDTH), 0), "w_o": ((D_MODEL, D_MODEL), 0), "xattn_w_q": ((D_MODEL, 512), 0),
    "xattn_w_kv": ((D_MODEL, 1024), 0), "xattn_w_o": ((512, D_MODEL), 1),
}
FFN_NAMES = ["ffn1_w_gate", "ffn1_w_up", "ffn1_w_down", "ffn2_w_gate", "ffn2_w_up", "ffn2_w_down"]
TRANSPOSED_VIEW = ("ffn1_w_gate", "ffn1_w_up", "ffn2_w_gate", "ffn2_w_up", "w_in", "mla_w_uq")


def _shard_shape(name):
    (r, cdim), ax = SMALL_SHARDED[name]
    return (r // N_CHIPS, cdim) if ax == 0 else (r, cdim // N_CHIPS)


def _pack_shards(shards):
    parts = []
    for name in SMALL_SHARDED:
        a = shards[name]
        lead = a.shape[:-2]
        parts.append(a.reshape(lead + (a.shape[-2] * a.shape[-1] // LANES, LANES)))
    rows = sum(q.shape[-2] for q in parts)
    parts.append(jnp.zeros(lead + (-rows % PACK_ROW_MULTIPLE, LANES), parts[0].dtype))
    return jnp.concatenate(parts, axis=-2)


def _unpack_shards(packed):
    out, r0 = {}, 0
    lead = packed.shape[:-2]
    for name in SMALL_SHARDED:
        r, cdim = _shard_shape(name)
        rows = r * cdim // LANES
        out[name] = packed[..., r0:r0 + rows, :].reshape(lead + (r, cdim))
        r0 += rows
    return out


def _full_from_shards(name, sh):
    (r, cdim), ax = SMALL_SHARDED[name]
    if ax == 0:
        return sh.reshape(r, cdim)
    return jnp.transpose(sh, (1, 0, 2)).reshape(r, cdim)


def _shards_from_full(name, full):
    (r, cdim), ax = SMALL_SHARDED[name]
    if ax == 0:
        return full.reshape(N_CHIPS, r // N_CHIPS, cdim)
    return jnp.transpose(full.reshape(r, N_CHIPS, cdim // N_CHIPS), (1, 0, 2))


SMALL_REPL = {
    "ffn1_norm": (1, 1024), "mix_norm": (1, 1024), "mla_q_norm": (1, 384), "mla_kv_norm": (1, 256),
    "mla_qk_norm_q": (1, 192), "mla_qk_norm_k": (1, 192), "ssm_a_re": (32, 64), "ssm_a_im": (32, 64),
    "ssm_log_dt": (32, 1), "ssm_b_re": (32, 64, 16), "ssm_b_im": (32, 64, 16), "ssm_c_re": (32, 16, 64),
    "ssm_c_im": (32, 16, 64), "ssm_d": (1, 512), "ssm_b_glu": (1, 512), "out_norm_mla": (1, 512),
    "out_norm_ssm": (1, 512), "xattn_norm": (1, 1024), "mem_norm": (1, 1024), "xattn_q_norm": (1, 128),
    "xattn_k_norm": (1, 128), "ffn2_norm": (1, 1024),
}


def _pack_repl(grads, loss):
    flat = jnp.concatenate([grads[n].reshape(-1) for n in SMALL_REPL] + [loss.reshape(1)])
    rows = -(-flat.shape[0] // (LANES * SUBLANES)) * SUBLANES
    return jnp.pad(flat, (0, rows * LANES - flat.shape[0])).reshape(rows, LANES)


def _unpack_repl(packed):
    flat, out, o = packed.reshape(-1), {}, 0
    for n, shp in SMALL_REPL.items():
        size = int(np.prod(shp))
        out[n] = flat[o:o + size].reshape(shp)
        o += size
    out["loss"] = flat[o]
    return out


def _rope_tables(positions):
    half = MLA_ROPE // 2
    inv = ROPE_THETA ** (-jnp.arange(half, dtype=F32) / half)
    ang = positions.astype(F32)[:, None] * inv[None, :]
    cos, sin = jnp.cos(ang), jnp.sin(ang)
    S = positions.shape[0]
    z = lambda w: jnp.zeros((S, w), F32)
    keep = jnp.concatenate([jnp.ones((S, MLA_NOPE), F32), cos, cos, z(HEAD_PAD - MLA_QK)], axis=1)
    from_hi = jnp.concatenate([z(MLA_NOPE), -sin, z(HEAD_PAD - MLA_NOPE - half)], axis=1)
    from_lo = jnp.concatenate([z(MLA_NOPE + half), sin, z(HEAD_PAD - MLA_QK)], axis=1)
    return keep, from_hi, from_lo


def _norm_rope(x, g, keep, from_hi, from_lo):
    y = x * lax.rsqrt(jnp.sum(x * x, axis=-1, keepdims=True) * (1.0 / MLA_QK) + EPS) * g
    half = MLA_ROPE // 2
    return y * keep + _lane_roll(y, HEAD_PAD - half) * from_hi + _lane_roll(y, half) * from_lo


def local_step(x, mem, positions, target, w, wb, small_weights=None, ffn2_weights=None, on_grads=None):
    if small_weights is None:
        small_weights = lambda after: {}
    if ffn2_weights is None:
        ffn2_weights = lambda after: [wb[k] for k in FFN_NAMES[3:]]
    if on_grads is None:
        on_grads = lambda tag, g: 0.0
    S = x.shape[0]
    tm = min(FUSED_TILE, S)
    g1 = (S // tm,)
    tile = lambda arr, **kw: In(arr, (tm, arr.shape[1]), lambda i: (i, 0), rows=True, **kw)
    par = lambda arr, **kw: In(arr, arr.shape, lambda *_: (0, 0), diff=True, acc=(0,), **kw)
    wt = lambda arr: In(arr, weight=True)
    otile = lambda cols, dt: Out((S, cols), dt, (tm, cols), lambda i: (i, 0))
    grads = {}

    x1, h1, g_1, u_1 = ffn_fwd("ffn1_fwd", x, w["ffn1_norm"], wb["ffn1_w_gate"], wb["ffn1_w_up"], wb["ffn1_w_down"])
    wb = {**wb, **small_weights(x1)}

    keep, from_hi, from_lo = _rope_tables(positions)
    scale = MLA_QK ** -0.5

    def f_pre(xv, kp, fh, fl, g_mix, w_q, w_kv, w_kr, w_u, g_q, g_kv, w_uq, w_ukv, gq, gk):
        h2 = _rms(xv, g_mix)
        cq, ckv = _rms(wdot(h2, w_q), g_q), _rms(wdot(h2, w_kv), g_kv)
        kr = wdot(h2, w_kr)
        qs, ks, vs = [], [], []
        for h in range(MLA_HEADS):
            qs.append(_norm_rope(wdot(cq, w_uq, h), gq, kp, fh, fl) * scale)
            kv = wdot(ckv, w_ukv, h)
            ks.append(_norm_rope(jnp.concatenate([kv[:, :MLA_NOPE], kr], axis=-1), gk, kp, fh, fl))
            vs.append(kv[:, MLA_NOPE:])
        return jnp.concatenate(qs, axis=-1), jnp.concatenate(ks, axis=-1), jnp.concatenate(vs, axis=-1), wdot(h2, w_u)

    def pre_ins(gain_mix):
        return [tile(x1, diff=True), tile(keep), tile(from_hi), tile(from_lo), par(gain_mix),
                wt(wb["w_in_q"]), wt(wb["w_in_kv"]), wt(wb["w_in_kr"]), wt(wb["w_in_u"]),
                par(w["mla_q_norm"]), par(w["mla_kv_norm"]), wt(wb["w_uq"]), wt(wb["w_ukv"]),
                par(w["qk_gain_q"]), par(w["qk_gain_k"])]

    pre_outs = [otile(MLA_HEADS * HEAD_PAD, BF16), otile(MLA_HEADS * HEAD_PAD, BF16), otile(MLA_HEADS * MLA_V, BF16),
                otile(SSM_WIDTH, F32)]
    qh, kh, v0, pu = seg_fwd("pre_mixer_fwd", f_pre, g1, pre_ins(w["mix_norm"]), pre_outs)

    o_mla, lse = attn_fwd("mla_attn_fwd", qh, kh, v0, t=min(ATTN_TILE, S))

    prep_grid = (SSM_BLOCKS,)
    prep_ins = ([_whole(w[k], diff=True, acc=(0,)) for k in ("ssm_a_re", "ssm_a_im", "ssm_log_dt")]
                + [In(w[k], (BLOCK_CH, SSM_STATE), lambda i: (i, 0), diff=True)
                   for k in ("ssm_bt_re", "ssm_bt_im", "ssm_c2_re", "ssm_c2_im")])
    blk_out = Out((SSM_BLOCKS, BLOCK_CH, PREP_LANES), BF16, (None, BLOCK_CH, PREP_LANES), lambda i: (i, 0, 0))
    prep_outs = [Out((SUBLANES, SSM_LANES), F32, (SUBLANES, PREP_LANES), lambda i: (0, i))] * 2 + [blk_out] * 4
    a_r8, a_i8, bb_r, bb_i, cb_r, ncb_i = seg_fwd("ssm_prep_fwd", _ssm_prep_f, prep_grid, prep_ins, prep_outs)

    xs_r, xs_i, y_lin = ssm_fwd("ssm_fwd", pu, bb_r, bb_i, cb_r, ncb_i, a_r8, a_i8)

    M = mem.shape[0]
    f_norm = lambda xv, g: (_rms(xv, g),)
    mem_ins = [In(mem, (M, D_MODEL), lambda i: (0, 0)), par(w["mem_norm"])]
    mem_outs = [Out((M, D_MODEL), BF16, (M, D_MODEL), lambda i: (0, 0))]
    (mn,) = seg_fwd("mem_norm_fwd", f_norm, (1,), mem_ins, mem_outs)
    (kvm,) = mm("xattn_kv_fwd", [mn], [[wb["xattn_w_kv"]]], [F32])

    def f_knorm(kk, gk):
        return (jnp.concatenate([_rms(kk[:, h * XHD:(h + 1) * XHD], gk) for h in range(XH)], axis=-1),)

    keys = ((M, XH * XHD), (M, XH * XHD), lambda i: (0, 0))
    knorm_ins = [In(kvm, keys[1], keys[2], diff=True, grad=keys), par(w["xattn_k_norm"])]
    knorm_outs = [Out(keys[0], F32, keys[1], keys[2])]
    (kn_mem,) = seg_fwd("mem_key_norm_fwd", f_knorm, (1,), knorm_ins, knorm_outs)

    xscale = XHD ** -0.5

    def f_post(o, yl, u, xv, kn, vm, d, w_glu, b, gm, gs, w_o1, w_o2, gx, w_xq, gq, w_xo):
        gl = jax.nn.gelu(yl + d * u)
        so = gl * jax.nn.sigmoid(wdot(gl, w_glu) + b)
        x2 = xv + wdot(_rms(o, gm), w_o1) + wdot(_rms(so, gs), w_o2)
        h3 = _rms(x2, gx)
        heads = []
        for h in range(XH):
            qn = _rms(wdot(h3, w_xq, h), gq)
            p = jax.nn.softmax(_bdot_nt(qn, kn[:, h * XHD:(h + 1) * XHD]) * xscale, axis=-1)
            heads.append(_bdot_nn(p, vm[:, h * XHD:(h + 1) * XHD]))
        return (x2 + wdot(jnp.concatenate(heads, axis=-1), w_xo),)

    def post_ins(gain_d):
        half = (M, XH * XHD)
        return [tile(o_mla, diff=True), tile(y_lin, diff=True), tile(pu, diff=True), tile(x1, diff=True),
                In(kn_mem, half, lambda i: (0, 0), diff=True, acc=(0,)),
                In(kvm, half, lambda i: (0, 1), diff=True, acc=(0,), grad=(half, half, lambda i: (0, 0))),
                par(gain_d), wt(wb["ssm_w_glu"]), par(w["ssm_b_glu"]), par(w["out_norm_mla"]), par(w["out_norm_ssm"]),
                wt(wb["w_o_mla"]), wt(wb["w_o_ssm"]), par(w["xattn_norm"]), wt(wb["xattn_w_q"]), par(w["xattn_q_norm"]),
                wt(wb["xattn_w_o"])]

    post_outs = [otile(D_MODEL, F32)]
    (x3,) = seg_fwd("post_mixer_fwd", f_post, g1, post_ins(w["ssm_d"]), post_outs)

    wg2, wu2, wd2 = ffn2_weights(x3)
    dx4, h4, g_2, u_2, parts = ffn_fwd("ffn2_fwd_loss", x3, w["ffn2_norm"], wg2, wu2, wd2, target=target)
    loss = jnp.sum(parts[::SUBLANES, 0])

    dx3, grads["ffn2_norm"], dg_2, du_2 = ffn_bwd_act("ffn2_bwd_act", dx4, x3, w["ffn2_norm"], g_2, u_2, wg2, wu2, wd2)
    grads["ffn2_w_gate"], grads["ffn2_w_up"], grads["ffn2_w_down"] = ffn_bwd_w("ffn2_bwd_w", h4, dx4, g_2, u_2, dg_2, du_2)
    sent = on_grads("ffn2", grads)

    (do_mla, dy_lin, du_a, dx1_a, dkn, dvm, grads["ssm_d"], grads["ssm_w_glu"], grads["ssm_b_glu"], grads["out_norm_mla"],
     grads["out_norm_ssm"], grads["w_o_mla"], grads["w_o_ssm"], grads["xattn_norm"], grads["xattn_w_q"],
     grads["xattn_q_norm"], grads["xattn_w_o"]) = seg_bwd(
        "post_mixer_bwd", f_post, g1, post_ins(w["ssm_d"] + sent), post_outs, [dx3])

    dkk, grads["xattn_k_norm"] = seg_bwd("mem_key_norm_bwd", f_knorm, (1,), knorm_ins, knorm_outs, [dkn])
    w_kv = wb["xattn_w_kv"]
    (dmn,) = mm("xattn_kv_bwd", [dkk, dvm], [[w_kv[:, :XH * XHD]], [w_kv[:, XH * XHD:]]], [F32], trans=True)
    gk_w, gv_w = mm_tn("xattn_kv_bwd_w", [mn], [dkk, dvm], [(0, [0]), (0, [1])])
    grads["xattn_w_kv"] = jnp.concatenate([gk_w, gv_w], axis=1)
    (grads["mem_norm"],) = seg_bwd("mem_norm_bwd", f_norm, (1,), mem_ins, mem_outs, [dmn])

    du, dbb_r, dbb_i, dcb_r, dncb_i, da_r8, da_i8 = ssm_bwd("ssm_bwd", dy_lin, du_a, pu, xs_r, xs_i, bb_r, bb_i, cb_r, ncb_i,
                                                            a_r8, a_i8)
    prep_g = seg_bwd("ssm_prep_bwd", _ssm_prep_f, prep_grid, prep_ins, prep_outs, [da_r8, da_i8, dbb_r, dbb_i, dcb_r, dncb_i])
    for k, gname in enumerate(("ssm_a_re", "ssm_a_im", "ssm_log_dt", "ssm_bt_re", "ssm_bt_im", "ssm_c2_re", "ssm_c2_im")):
        grads[gname] = prep_g[k]

    dqh, dkh, dv0 = attn_bwd("mla_attn_bwd", qh, kh, v0, do_mla, o_mla, lse, t=min(ATTN_TILE, S))

    (dx1, grads["mix_norm"], grads["w_in_q"], grads["w_in_kv"], grads["w_in_kr"], grads["w_in_u"], grads["mla_q_norm"],
     grads["mla_kv_norm"], grads["w_uq"], grads["w_ukv"], grads["qk_gain_q"], grads["qk_gain_k"]) = seg_bwd(
        "pre_mixer_bwd", f_pre, g1, pre_ins(w["mix_norm"]), pre_outs, [dqh, dkh, dv0, du], adds={0: dx1_a},
        row_block=FUSED_ROW_BLOCK)

    sent = on_grads("small", grads)
    dx, grads["ffn1_norm"], dg_1, du_1 = ffn_bwd_act("ffn1_bwd_act", dx1, x, w["ffn1_norm"] + sent, g_1, u_1,
                                                     wb["ffn1_w_gate"], wb["ffn1_w_up"], wb["ffn1_w_down"])
    grads["ffn1_w_gate"], grads["ffn1_w_up"], grads["ffn1_w_down"] = ffn_bwd_w("ffn1_bwd_w", h1, dx1, g_1, u_1, dg_1, du_1)
    return loss, dx, grads


def _pad_cols(a, n):
    return jnp.pad(a, ((0, 0), (0, n - a.shape[1])))


def _step_weights(shards):
    wb = {}
    w_in = _full_from_shards("w_in", shards["w_in"])
    wb["w_in_q"] = w_in[:, :MLA_Q_RANK]
    wb["w_in_kv"] = w_in[:, MLA_Q_RANK:MLA_Q_RANK + MLA_KV_RANK]
    wb["w_in_kr"] = _pad_cols(w_in[:, MLA_Q_RANK + MLA_KV_RANK:MLA_Q_RANK + MLA_KV_RANK + MLA_ROPE], LANES)
    wb["w_in_u"] = w_in[:, MLA_Q_RANK + MLA_KV_RANK + MLA_ROPE:]
    wb["w_uq"] = jnp.pad(shards["mla_w_uq"], ((0, 0), (0, 0), (0, HEAD_PAD - MLA_QK)))
    wb["w_ukv"] = shards["mla_w_ukv"]
    wb["ssm_w_glu"] = _full_from_shards("ssm_w_glu", shards["ssm_w_glu"])
    w_o = _full_from_shards("w_o", shards["w_o"])
    wb["w_o_mla"], wb["w_o_ssm"] = w_o[:SSM_WIDTH], w_o[SSM_WIDTH:]
    w_xq = _full_from_shards("xattn_w_q", shards["xattn_w_q"])
    wb["xattn_w_q"] = jnp.transpose(w_xq.reshape(D_MODEL, XH, XHD), (1, 0, 2))
    wb["xattn_w_kv"] = _full_from_shards("xattn_w_kv", shards["xattn_w_kv"])
    wb["xattn_w_o"] = _full_from_shards("xattn_w_o", shards["xattn_w_o"])
    return wb


def _sharded_grads(g):
    out = {}
    kr = g["w_in_kr"][:, :MLA_ROPE]
    out["w_in"] = _shards_from_full("w_in", jnp.concatenate([g["w_in_q"], g["w_in_kv"], kr, g["w_in_u"]], axis=1))
    out["mla_w_uq"] = g["w_uq"][:, :, :MLA_QK]
    out["mla_w_ukv"] = g["w_ukv"]
    out["ssm_w_glu"] = _shards_from_full("ssm_w_glu", g["ssm_w_glu"])
    out["w_o"] = _shards_from_full("w_o", jnp.concatenate([g["w_o_mla"], g["w_o_ssm"]], axis=0))
    w_xq = jnp.transpose(g["xattn_w_q"], (1, 0, 2)).reshape(D_MODEL, XH * XHD)
    out["xattn_w_q"] = _shards_from_full("xattn_w_q", w_xq)
    out["xattn_w_kv"] = _shards_from_full("xattn_w_kv", g["xattn_w_kv"])
    out["xattn_w_o"] = _shards_from_full("xattn_w_o", g["xattn_w_o"])
    return out


def _problem_repl_grads(g):
    out = {}
    out["mla_qk_norm_q"] = g["qk_gain_q"][:, :MLA_QK]
    out["mla_qk_norm_k"] = g["qk_gain_k"][:, :MLA_QK]
    out["ssm_b_re"] = jnp.transpose(g["ssm_bt_re"].reshape(SSM_GROUPS, SSM_GROUP, SSM_STATE), (0, 2, 1))
    out["ssm_b_im"] = jnp.transpose(g["ssm_bt_im"].reshape(SSM_GROUPS, SSM_GROUP, SSM_STATE), (0, 2, 1))
    out["ssm_c_re"] = g["ssm_c2_re"].reshape(SSM_GROUPS, SSM_GROUP, SSM_STATE)
    out["ssm_c_im"] = g["ssm_c2_im"].reshape(SSM_GROUPS, SSM_GROUP, SSM_STATE)
    for k in SMALL_REPL:
        if k not in out:
            out[k] = g[k]
    return out


def _problem_grads(g):
    out = {k: _full_from_shards(k, v) for k, v in _sharded_grads(g).items()}
    out.update(_problem_repl_grads(g))
    out.update({k: g[k] for k in FFN_NAMES})
    return out


def _step_params(p):
    row = lambda a: a.reshape(1, -1)
    w = {k: row(p[k]) for k in ("ffn1_norm", "mix_norm", "mla_q_norm", "mla_kv_norm", "ssm_b_glu", "out_norm_mla",
                                "out_norm_ssm", "xattn_norm", "mem_norm", "xattn_q_norm", "xattn_k_norm", "ffn2_norm")}
    w["qk_gain_q"] = _pad_cols(row(p["mla_qk_norm_q"]), HEAD_PAD)
    w["qk_gain_k"] = _pad_cols(row(p["mla_qk_norm_k"]), HEAD_PAD)
    w["ssm_a_re"], w["ssm_a_im"] = p["ssm_a_re"], p["ssm_a_im"]
    w["ssm_log_dt"] = p["ssm_log_dt"].reshape(SSM_GROUPS, 1)
    w["ssm_bt_re"] = jnp.transpose(p["ssm_b_re"], (0, 2, 1)).reshape(SSM_WIDTH, SSM_STATE)
    w["ssm_bt_im"] = jnp.transpose(p["ssm_b_im"], (0, 2, 1)).reshape(SSM_WIDTH, SSM_STATE)
    w["ssm_c2_re"] = p["ssm_c_re"].reshape(SSM_WIDTH, SSM_STATE)
    w["ssm_c2_im"] = p["ssm_c_im"].reshape(SSM_WIDTH, SSM_STATE)
    w["ssm_d"] = p["ssm_d"].reshape(1, SSM_WIDTH)
    return w


ARG_NAMES = ['x', 'mem', 'positions', 'ffn1_norm', 'ffn1_w_gate', 'ffn1_w_up', 'ffn1_w_down', 'mix_norm', 'w_in', 'mla_q_norm', 'mla_w_uq', 'mla_kv_norm', 'mla_w_ukv', 'mla_qk_norm_q', 'mla_qk_norm_k', 'ssm_a_re', 'ssm_a_im', 'ssm_log_dt', 'ssm_b_re', 'ssm_b_im', 'ssm_c_re', 'ssm_c_im', 'ssm_d', 'ssm_w_glu', 'ssm_b_glu', 'out_norm_mla', 'out_norm_ssm', 'w_o', 'xattn_norm', 'mem_norm', 'xattn_w_q', 'xattn_w_kv', 'xattn_q_norm', 'xattn_k_norm', 'xattn_w_o', 'ffn2_norm', 'ffn2_w_gate', 'ffn2_w_up', 'ffn2_w_down']
WEIGHT_NAMES = ARG_NAMES[3:]


def _gather_weights(p, c):
    half = lambda a: lax.dynamic_slice_in_dim(a, c * (a.shape[0] // 2), a.shape[0] // 2, axis=0)
    ffn1 = [half(p[k].astype(BF16)) for k in FFN_NAMES[:3]]
    small = [half(_pack_shards({k: p[k].astype(BF16) for k in SMALL_SHARDED}))]
    ffn2 = [half(p[k].astype(BF16)) for k in FFN_NAMES[3:]]
    me = 4 * lax.axis_index("x") + 2 * lax.axis_index("y") + c
    own = lambda got, blocks: [lax.dynamic_update_index_in_dim(g, b, me, 0) for g, b in zip(got, blocks)]
    as_shards = lambda a: a.reshape(N_CHIPS, 2 * a.shape[1], a.shape[2])
    landing = lambda blocks: [jax.ShapeDtypeStruct((N_DEV,) + b.shape, b.dtype) for b in blocks]
    got1 = own(all_gather_halves("all_gather_weights_a", ffn1), ffn1)
    got1, small = lax.optimization_barrier((got1, small))
    flight_s, sent_s = split_copy_start("gather_small_start", _gather_copies, N_DEV - 1, small, landing(small))
    sent_s, ffn2 = lax.optimization_barrier((sent_s, ffn2))
    flight_2, sent_2 = split_copy_start("gather_ffn2_start", _gather_copies, N_DEV - 1, ffn2, landing(ffn2))
    wb = {k: as_shards(a) for k, a in zip(FFN_NAMES[:3], got1)}

    def small_weights(after):
        mine, got = split_copy_wait("gather_small_wait", flight_s, after)
        return _step_weights(_unpack_shards(own(got, mine)[0].reshape(N_CHIPS, -1, LANES)))

    def ffn2_weights(after):
        mine, got = split_copy_wait("gather_ffn2_wait", flight_2, after)
        return [as_shards(a) for a in own(got, mine)]

    return wb, small_weights, ffn2_weights, sent_s + sent_2


class _GradReduce:
    def __init__(self, c):
        self.c, self.c_arr = c, jnp.reshape(c, (1,)).astype(jnp.int32)
        self.chip = 2 * lax.axis_index("x") + lax.axis_index("y")
        self.flights = []

    def start(self, tag, arrs):
        split = [a.reshape(N_CHIPS, 2, a.shape[1] // 2, a.shape[2]) for a in arrs]
        return self._scatter(tag, split, pair_send_halves(f"grad_pair_send_{tag}", split))

    def send(self, tag, arrs):
        split = [a.reshape(N_CHIPS, 2, a.shape[1] // 2, a.shape[2]) for a in arrs]
        lands = [jax.ShapeDtypeStruct((s.shape[0],) + s.shape[2:], s.dtype) for s in split]
        flight, sent = split_copy_start(f"grad_pair_send_start_{tag}", _pair_copies, 1, split, lands)
        self.sending = (tag, flight)
        return sent

    def scatter(self, tag, after):
        sent_tag, flight = self.sending
        assert sent_tag == tag
        return self._scatter(tag, *split_copy_wait(f"grad_pair_send_wait_{tag}", flight, after))

    def _scatter(self, tag, split, got):
        parts = [pair_add(f"grad_pair_add_{tag}_{k}", s, g, self.c_arr) for k, (s, g) in enumerate(zip(split, got))]
        flight, sent = split_copy_start(f"grad_scatter_start_{tag}", _scatter_copies, N_CHIPS - 1, parts, parts)
        self.flights.append((tag, flight))
        return sent

    def finish(self, after):
        halves = []
        for tag, flight in self.flights:
            parts, landed = split_copy_wait(f"grad_scatter_wait_{tag}", flight, after)
            for k, (q, p) in enumerate(zip(landed, parts)):
                mine = lax.dynamic_index_in_dim(p, self.chip, 0, keepdims=False)
                halves.append(sum_chips(f"grad_sum_{tag}_{k}", lax.dynamic_update_index_in_dim(q, mine, self.chip, 0)))
        tags = "_".join(t for t, _ in self.flights)
        self.flights = []
        theirs = pair_exchange(f"grad_pair_exchange_{tags}", halves)
        return [jnp.where(self.c == 0, jnp.concatenate([h, t], axis=0), jnp.concatenate([t, h], axis=0))
                for h, t in zip(halves, theirs)]


def kernel(x, mem, positions, ffn1_norm, ffn1_w_gate, ffn1_w_up, ffn1_w_down, mix_norm, w_in, mla_q_norm, mla_w_uq, mla_kv_norm, mla_w_ukv, mla_qk_norm_q, mla_qk_norm_k, ssm_a_re, ssm_a_im, ssm_log_dt, ssm_b_re, ssm_b_im, ssm_c_re, ssm_c_im, ssm_d, ssm_w_glu, ssm_b_glu, out_norm_mla, out_norm_ssm, w_o, xattn_norm, mem_norm, xattn_w_q, xattn_w_kv, xattn_q_norm, xattn_k_norm, xattn_w_o, ffn2_norm, ffn2_w_gate, ffn2_w_up, ffn2_w_down, loss_target, m_ffn1_norm, m_ffn1_w_gate, m_ffn1_w_up, m_ffn1_w_down, m_mix_norm, m_w_in, m_mla_q_norm, m_mla_w_uq, m_mla_kv_norm, m_mla_w_ukv, m_mla_qk_norm_q, m_mla_qk_norm_k, m_ssm_a_re, m_ssm_a_im, m_ssm_log_dt, m_ssm_b_re, m_ssm_b_im, m_ssm_c_re, m_ssm_c_im, m_ssm_d, m_ssm_w_glu, m_ssm_b_glu, m_out_norm_mla, m_out_norm_ssm, m_w_o, m_xattn_norm, m_mem_norm, m_xattn_w_q, m_xattn_w_kv, m_xattn_q_norm, m_xattn_k_norm, m_xattn_w_o, m_ffn2_norm, m_ffn2_w_gate, m_ffn2_w_up, m_ffn2_w_down, v_ffn1_norm, v_ffn1_w_gate, v_ffn1_w_up, v_ffn1_w_down, v_mix_norm, v_w_in, v_mla_q_norm, v_mla_w_uq, v_mla_kv_norm, v_mla_w_ukv, v_mla_qk_norm_q, v_mla_qk_norm_k, v_ssm_a_re, v_ssm_a_im, v_ssm_log_dt, v_ssm_b_re, v_ssm_b_im, v_ssm_c_re, v_ssm_c_im, v_ssm_d, v_ssm_w_glu, v_ssm_b_glu, v_out_norm_mla, v_out_norm_ssm, v_w_o, v_xattn_norm, v_mem_norm, v_xattn_w_q, v_xattn_w_kv, v_xattn_q_norm, v_xattn_k_norm, v_xattn_w_o, v_ffn2_norm, v_ffn2_w_gate, v_ffn2_w_up, v_ffn2_w_down):
    args = dict(locals())
    c = lax.axis_index("c")
    view = lambda k, a: jnp.swapaxes(a, 0, 1) if k in TRANSPOSED_VIEW else a
    p = {k: view(k, args[k][0]) for k in WEIGHT_NAMES}
    mom = {k: view(k, args["m_" + k][0]) for k in WEIGHT_NAMES}
    var = {k: view(k, args["v_" + k][0]) for k in WEIGHT_NAMES}
    natural = {k: view(k, p[k]) for k in WEIGHT_NAMES}

    wb, small_weights, ffn2_weights, sent = _gather_weights(
        {k: (p[k] if k in FFN_NAMES else natural[k]) for k in WEIGHT_NAMES}, c)
    w = _step_params(natural)
    w["ffn1_norm"] = w["ffn1_norm"] + sent
    early, late = _GradReduce(c), _GradReduce(c)

    def on_grads(tag, g):
        if tag == "ffn2":
            return early.send(tag, [g[k] for k in FFN_NAMES[3:]])
        packed = _pack_shards(_sharded_grads(g))
        return early.scatter("ffn2", packed[0, :SUBLANES]) + early.start(tag, [packed])

    loss, dx, g = local_step(x[0], mem[0], positions[0], loss_target[0], w, wb, small_weights, ffn2_weights, on_grads)

    sent = late.send("ffn1", [g[k] for k in FFN_NAMES[:3]])
    shards = early.finish(dx[:SUBLANES, :LANES] + sent)
    grad = dict(zip(FFN_NAMES[3:], shards[:3]))
    small_sharded = _unpack_shards(shards[3])
    grad.update({k: view(k, small_sharded[k]) for k in SMALL_SHARDED})
    reduced = all_reduce_small("grad_all_reduce_small", _pack_repl(_problem_repl_grads(g), loss + sent))
    grad.update(_unpack_repl(reduced))
    loss = grad.pop("loss")
    started = jnp.reshape(late.scatter("ffn1", reduced[:SUBLANES]), (1, 1))

    delta, new_m, new_v = {}, {}, {}
    small = [k for k in WEIGHT_NAMES if k not in FFN_NAMES and k not in SMALL_SHARDED]
    as2d = lambda a: a.reshape(-1, a.shape[-1])

    def update(k, after):
        delta[k], new_m[k], new_v[k] = adamw_big("adamw_" + k, as2d(p[k]), as2d(grad[k]), as2d(mom[k]), as2d(var[k]), after)

    last = started
    for k in WEIGHT_NAMES:
        if k not in small and k not in FFN_NAMES[:3]:
            update(k, last)
            last = delta[k]
    ds, nms, nvs = adamw_small("adamw_small", [as2d(p[k]) for k in small], [as2d(grad[k].reshape(p[k].shape)) for k in small],
                               [as2d(mom[k]) for k in small], [as2d(var[k]) for k in small], last)
    for k, d, nm, nv in zip(small, ds, nms, nvs):
        delta[k], new_m[k], new_v[k] = d, nm, nv
    last = ds[0]

    grad.update(zip(FFN_NAMES[:3], late.finish(last)))
    for k in FFN_NAMES[:3]:
        update(k, last)

    shaped = lambda d, k: view(k, d.reshape(p[k].shape)).reshape(args[k].shape)
    return (loss, dx[None], *[shaped(grad[k], k) for k in WEIGHT_NAMES], *[shaped(delta[k], k) for k in WEIGHT_NAMES],
            *[shaped(new_m[k], k) for k in WEIGHT_NAMES], *[shaped(new_v[k], k) for k in WEIGHT_NAMES])
```

```python
import functools
import math

import jax
import jax.numpy as jnp
import numpy as np
from jax import lax
from jax.experimental import pallas as pl
from jax.experimental.pallas import tpu as pltpu

F32, BF16 = jnp.float32, jnp.bfloat16
EPS = 1e-6
MESH = pl.DeviceIdType.MESH

D_MODEL, D_FF = 1024, 2752
MLA_HEADS, MLA_Q_RANK, MLA_KV_RANK, MLA_NOPE, MLA_ROPE, MLA_V = 4, 384, 256, 128, 64, 128
MLA_QK = MLA_NOPE + MLA_ROPE
HEAD_PAD = 256
SSM_WIDTH, SSM_GROUP, SSM_GROUPS, SSM_STATE = 512, 16, 32, 64
SSM_LANES = SSM_GROUPS * SSM_STATE
XH, XHD = 4, 128
ROPE_THETA = 10000.0
ADAM_LR, ADAM_B1, ADAM_B2, ADAM_EPS, ADAM_WD, ADAM_STEP = 0.001, 0.9, 0.999, 1e-08, 0.01, 10
N_CHIPS, N_CORES, N_DEV = 4, 2, 8

VMEM_LIMIT_BYTES = 56 * 2**20
TOKEN_TILE = 512
FUSED_TILE = 256
FUSED_ROW_BLOCK = 128
FFN_ROW_BLOCK = 256
FFN_WIDE_TILE = 1024
ATTN_TILE = 1024
STAT_LANES = 128
SCAN_TIME_TILE = 1024
SCAN_LANE_TILE = 512
SUBLANES = 8


def _params(n_axes):
    return pltpu.CompilerParams(dimension_semantics=("arbitrary",) * n_axes, vmem_limit_bytes=VMEM_LIMIT_BYTES)


def _first(axes):
    cond = None
    for a in axes:
        c = pl.program_id(a) == 0
        cond = c if cond is None else jnp.logical_and(cond, c)
    return cond


def mm(name, xs, ws, out_dtypes, *, trans=False, adds=None, tm=TOKEN_TILE):
    rows = xs[0].shape[0]
    tm = min(tm, rows)
    n_in, n_out = len(xs), len(out_dtypes)
    pairs = [(i, j) for i in range(n_in) for j in range(n_out) if ws[i][j] is not None]
    w_list = [ws[i][j] for (i, j) in pairs]
    adds = list(adds) if adds is not None else [None] * n_out
    add_list = [a for a in adds if a is not None]
    out_cols = [None] * n_out
    for (i, j), w in zip(pairs, w_list):
        out_cols[j] = w.shape[0] if trans else w.shape[1]
    contract = (((1,), (1 if trans else 0,)), ((), ()))

    def body(*refs):
        x_refs = refs[:n_in]
        w_refs = refs[n_in:n_in + len(pairs)]
        a_refs = list(refs[n_in + len(pairs):n_in + len(pairs) + len(add_list)])
        o_refs = refs[n_in + len(pairs) + len(add_list):]
        xb = [None] * n_in
        for j in range(n_out):
            acc = None
            for p, (i, jj) in enumerate(pairs):
                if jj != j:
                    continue
                if xb[i] is None:
                    xb[i] = x_refs[i][...].astype(BF16)
                d = lax.dot_general(xb[i], w_refs[p][...].astype(BF16), contract, preferred_element_type=F32)
                acc = d if acc is None else acc + d
            if adds[j] is not None:
                acc = acc + a_refs.pop(0)[...].astype(F32)
            o_refs[j][...] = acc.astype(o_refs[j].dtype)

    in_specs = ([pl.BlockSpec((tm, x.shape[1]), lambda i: (i, 0)) for x in xs]
                + [pl.BlockSpec(w.shape, lambda i: (0, 0)) for w in w_list]
                + [pl.BlockSpec((tm, a.shape[1]), lambda i: (i, 0)) for a in add_list])
    outs = pl.pallas_call(
        body, name=name, grid=(rows // tm,), in_specs=in_specs,
        out_specs=[pl.BlockSpec((tm, n), lambda i: (i, 0)) for n in out_cols],
        out_shape=[jax.ShapeDtypeStruct((rows, n), dt) for n, dt in zip(out_cols, out_dtypes)],
        compiler_params=_params(1),
    )(*xs, *w_list, *add_list)
    return list(outs)


def mm_tn(name, xs, dys, pairs, *, tm=TOKEN_TILE):
    rows = xs[0].shape[0]
    tm = min(tm, rows)
    n_x, n_dy = len(xs), len(dys)
    contract = (((0,), (0,)), ((), ()))

    def body(*refs):
        x_refs, dy_refs, o_refs = refs[:n_x], refs[n_x:n_x + n_dy], refs[n_x + n_dy:]
        @pl.when(pl.program_id(0) == 0)
        def _():
            for o in o_refs:
                o[...] = jnp.zeros_like(o)

        for k, (i, js) in enumerate(pairs):
            dy = None
            for j in js:
                t = dy_refs[j][...].astype(F32)
                dy = t if dy is None else dy + t
            o_refs[k][...] += lax.dot_general(x_refs[i][...].astype(BF16), dy.astype(BF16), contract,
                                              preferred_element_type=F32)

    shapes = [(xs[i].shape[1], dys[js[0]].shape[1]) for (i, js) in pairs]
    outs = pl.pallas_call(
        body, name=name, grid=(rows // tm,),
        in_specs=[pl.BlockSpec((tm, a.shape[1]), lambda i: (i, 0)) for a in (*xs, *dys)],
        out_specs=[pl.BlockSpec(s, lambda i: (0, 0)) for s in shapes],
        out_shape=[jax.ShapeDtypeStruct(s, F32) for s in shapes],
        compiler_params=_params(1),
    )(*xs, *dys)
    return list(outs)


class In:
    def __init__(self, arr, block=None, imap=None, *, diff=False, acc=None, grad=None, weight=False, rows=False):
        self.arr, self.block, self.imap, self.diff, self.acc, self.grad = arr, block, imap, diff, acc, grad
        self.weight, self.rows = weight, rows

    def spec(self):
        return pl.BlockSpec(memory_space=pltpu.VMEM) if self.weight else pl.BlockSpec(self.block, self.imap)


class Out:
    def __init__(self, shape, dtype, block, imap):
        self.shape, self.dtype, self.block, self.imap = shape, dtype, block, imap

    def spec(self):
        return pl.BlockSpec(self.block, self.imap)


class Wt:
    def __init__(self, ref, zeros=None):
        self.ref, self.zeros = ref, zeros


@jax.custom_vjp
def _wdot(a, w, z):
    return jnp.dot(a.astype(BF16), w, preferred_element_type=F32)


def _wdot_fwd(a, w, z):
    return _wdot(a, w, z), (a, w)


def _wdot_bwd(res, g):
    a, w = res
    gb = g.astype(BF16)
    da = lax.dot_general(gb, w, (((1,), (1,)), ((), ())), preferred_element_type=F32)
    dz = lax.dot_general(a.astype(BF16), gb, (((0,), (0,)), ((), ())), preferred_element_type=F32)
    return da, None, dz


_wdot.defvjp(_wdot_fwd, _wdot_bwd)


def wdot(a, wt, head=None):
    w = wt.ref[...] if head is None else wt.ref[head]
    if wt.zeros is None:
        return jnp.dot(a.astype(BF16), w, preferred_element_type=F32)
    return _wdot(a, w, wt.zeros[0 if head is None else head])


def seg_fwd(name, f, grid, ins, outs):
    n_in = len(ins)

    def body(*refs):
        res = f(*[Wt(r) if i.weight else r[...] for i, r in zip(ins, refs[:n_in])])
        for o_ref, r in zip(refs[n_in:], res):
            o_ref[...] = r.astype(o_ref.dtype)

    res = pl.pallas_call(
        body, name=name, grid=grid, in_specs=[i.spec() for i in ins], out_specs=[o.spec() for o in outs],
        out_shape=[jax.ShapeDtypeStruct(o.shape, o.dtype) for o in outs], compiler_params=_params(len(grid)),
    )(*[i.arr for i in ins])
    return list(res)


def seg_bwd(name, f, grid, ins, outs, cts, adds=None, row_block=None):
    n_in, n_ct = len(ins), len(cts)
    grad_idx = [k for k, i in enumerate(ins) if i.diff or i.weight]
    adds = adds or {}
    add_keys = sorted(adds)
    add_list = [adds[k] for k in add_keys]
    heads = {k: (ins[k].arr.shape[0] if ins[k].arr.ndim == 3 else 1) for k in grad_idx if ins[k].weight}
    tile_rows = outs[0].block[0]
    blocks = [None] if row_block is None else [pl.ds(r0, row_block) for r0 in range(0, tile_rows, row_block)]

    def body(*refs):
        in_refs, ct_refs = refs[:n_in], refs[n_in:n_in + n_ct]
        add_refs = dict(zip(add_keys, refs[n_in + n_ct:n_in + n_ct + len(add_list)]))
        g_refs = dict(zip(grad_idx, refs[n_in + n_ct + len(add_list):]))
        for k in grad_idx:
            if ins[k].weight or ins[k].acc is not None:
                @pl.when(_first(range(len(grid)) if ins[k].weight else ins[k].acc))
                def _(k=k):
                    g_refs[k][...] = jnp.zeros_like(g_refs[k])

        for rows in blocks:
            at = lambda ref, sliced: ref[...] if rows is None or not sliced else ref[rows, :]
            vals = [None if i.weight else at(r, i.rows) for i, r in zip(ins, in_refs)]
            primals, owner = [], []
            for k in grad_idx:
                if ins[k].weight:
                    for h in range(heads[k]):
                        primals.append(jnp.zeros(ins[k].arr.shape[-2:], F32))
                        owner.append((k, h))
                else:
                    primals.append(vals[k].astype(F32))
                    owner.append((k, None))

            def g(*dv, vals=vals, owner=owner):
                full = list(vals)
                zeros = {}
                for (k, h), v in zip(owner, dv):
                    if h is None:
                        full[k] = v
                    else:
                        zeros.setdefault(k, []).append(v)
                for k, z in zeros.items():
                    full[k] = Wt(in_refs[k], z)
                return tuple(f(*full))

            _, pull = jax.vjp(g, *primals)
            grads = pull(tuple(at(c, True).astype(F32) for c in ct_refs))
            for (k, h), gr in zip(owner, grads):
                if ins[k].weight:
                    if ins[k].arr.ndim == 3:
                        g_refs[k][h] += gr
                    else:
                        g_refs[k][...] += gr
                    continue
                if k in add_refs:
                    gr = gr + at(add_refs[k], True).astype(F32)
                if ins[k].acc is not None:
                    g_refs[k][...] += gr
                elif rows is None or not ins[k].rows:
                    g_refs[k][...] = gr.astype(g_refs[k].dtype)
                else:
                    g_refs[k][rows, :] = gr.astype(g_refs[k].dtype)

    g_specs, g_shapes = [], []
    for k in grad_idx:
        i = ins[k]
        if i.weight:
            g_specs.append(pl.BlockSpec(memory_space=pltpu.VMEM))
            g_shapes.append(jax.ShapeDtypeStruct(i.arr.shape, F32))
            continue
        shape, block, imap = i.grad if i.grad is not None else (i.arr.shape, i.block, i.imap)
        g_specs.append(pl.BlockSpec(block, imap))
        g_shapes.append(jax.ShapeDtypeStruct(shape, F32))
    in_specs = ([i.spec() for i in ins] + [o.spec() for o in outs]
                + [pl.BlockSpec(ins[k].block, ins[k].imap) for k in add_keys])
    res = pl.pallas_call(
        body, name=name, grid=grid, in_specs=in_specs, out_specs=g_specs, out_shape=g_shapes,
        compiler_params=_params(len(grid)),
    )(*[i.arr for i in ins], *cts, *add_list)
    return list(res)


def _rms(x, g):
    return x * lax.rsqrt(jnp.mean(x * x, axis=-1, keepdims=True) + EPS) * g


@jax.custom_vjp
def _bdot_nt(a, b):
    return lax.dot_general(a.astype(BF16), b.astype(BF16), (((1,), (1,)), ((), ())), preferred_element_type=F32)


def _bdot_nt_fwd(a, b):
    return _bdot_nt(a, b), (a, b)


def _bdot_nt_bwd(res, g):
    a, b = res
    gb = g.astype(BF16)
    da = lax.dot_general(gb, b.astype(BF16), (((1,), (0,)), ((), ())), preferred_element_type=F32)
    db = lax.dot_general(gb, a.astype(BF16), (((0,), (0,)), ((), ())), preferred_element_type=F32)
    return da, db


_bdot_nt.defvjp(_bdot_nt_fwd, _bdot_nt_bwd)


@jax.custom_vjp
def _bdot_nn(a, b):
    return lax.dot_general(a.astype(BF16), b.astype(BF16), (((1,), (0,)), ((), ())), preferred_element_type=F32)


def _bdot_nn_fwd(a, b):
    return _bdot_nn(a, b), (a, b)


def _bdot_nn_bwd(res, g):
    a, b = res
    gb = g.astype(BF16)
    da = lax.dot_general(gb, b.astype(BF16), (((1,), (1,)), ((), ())), preferred_element_type=F32)
    db = lax.dot_general(a.astype(BF16), gb, (((0,), (0,)), ((), ())), preferred_element_type=F32)
    return da, db


_bdot_nn.defvjp(_bdot_nn_fwd, _bdot_nn_bwd)


@functools.partial(jax.custom_vjp, nondiff_argnums=(1,))
def _lane_roll(x, shift):
    return pltpu.roll(x, shift, 1)


def _lane_roll_fwd(x, shift):
    return pltpu.roll(x, shift, 1), None


def _lane_roll_bwd(shift, _, g):
    return (pltpu.roll(g, (g.shape[1] - shift) % g.shape[1], 1),)


_lane_roll.defvjp(_lane_roll_fwd, _lane_roll_bwd)


def _hp_dot(a, b):
    return jnp.dot(a, b, precision=lax.Precision.HIGHEST, preferred_element_type=F32)


NT_DIMS = (((1,), (1,)), ((), ()))
TN_DIMS = (((0,), (0,)), ((), ()))


def ffn_fwd(name, x, gain, wg, wu, wd, target=None, tm=FFN_WIDE_TILE):
    S, D = x.shape
    tm = min(tm, S)
    nsh, Fs, _ = wg.shape
    with_loss = target is not None

    def body(*refs):
        if with_loss:
            x_ref, gain_ref, wg_ref, wu_ref, wd_ref, t_ref, xo_ref, h_ref, g_ref, u_ref, part_ref, acc = refs
        else:
            x_ref, gain_ref, wg_ref, wu_ref, wd_ref, xo_ref, h_ref, g_ref, u_ref, acc = refs
        j = pl.program_id(1)

        @pl.when(j == 0)
        def _():
            h_ref[...] = _rms(x_ref[...], gain_ref[...]).astype(BF16)
            acc[...] = jnp.zeros_like(acc)

        h = h_ref[...]
        g = lax.dot_general(h, wg_ref[...], NT_DIMS, preferred_element_type=F32)
        u = lax.dot_general(h, wu_ref[...], NT_DIMS, preferred_element_type=F32)
        g_ref[...] = g.astype(BF16)
        u_ref[...] = u.astype(BF16)
        a = g * jax.nn.sigmoid(g) * u
        acc[...] += jnp.dot(a.astype(BF16), wd_ref[...], preferred_element_type=F32)

        @pl.when(j == nsh - 1)
        def _():
            y = x_ref[...] + 0.5 * acc[...]
            if with_loss:
                err = y - t_ref[...]
                xo_ref[...] = err * (1.0 / D)
                part_ref[...] = jnp.full(part_ref.shape, 0.5 * jnp.sum(jnp.mean(err * err, axis=-1)), F32)
            else:
                xo_ref[...] = y

    rows = pl.BlockSpec((tm, D), lambda i, j: (i, 0))
    wspec = pl.BlockSpec((None, Fs, D), lambda i, j: (j, 0, 0))
    act = pl.BlockSpec((None, tm, Fs), lambda i, j: (j, i, 0))
    in_specs, args = [rows, pl.BlockSpec((1, D), lambda i, j: (0, 0)), wspec, wspec, wspec], [x, gain, wg, wu, wd]
    out_specs = [rows, rows, act, act]
    out_shape = [jax.ShapeDtypeStruct((S, D), F32), jax.ShapeDtypeStruct((S, D), BF16),
                 jax.ShapeDtypeStruct((nsh, S, Fs), BF16), jax.ShapeDtypeStruct((nsh, S, Fs), BF16)]
    if with_loss:
        in_specs.append(rows)
        args.append(target)
        out_specs.append(pl.BlockSpec((SUBLANES, 128), lambda i, j: (i, 0)))
        out_shape.append(jax.ShapeDtypeStruct((S // tm * SUBLANES, 128), F32))
    return pl.pallas_call(
        body, name=name, grid=(S // tm, nsh), in_specs=in_specs, out_specs=out_specs, out_shape=out_shape,
        scratch_shapes=[pltpu.VMEM((tm, D), F32)], compiler_params=_params(2),
    )(*args)


def ffn_bwd_act(name, dxo, x, gain, g, u, wg, wu, wd, tm=TOKEN_TILE):
    S, D = x.shape
    tm = min(tm, S)
    nsh, Fs, _ = wg.shape

    def body(dxo_ref, x_ref, gain_ref, g_ref, u_ref, wg_ref, wu_ref, wd_ref, dx_ref, dgain_ref, dg_ref, du_ref, dd, dh):
        i, j = pl.program_id(0), pl.program_id(1)

        @pl.when(j == 0)
        def _():
            dd[...] = (0.5 * dxo_ref[...]).astype(BF16)
            dh[...] = jnp.zeros_like(dh)

        for r0 in range(0, tm, FFN_ROW_BLOCK):
            rows = pl.ds(r0, FFN_ROW_BLOCK)
            da = lax.dot_general(dd[rows, :], wd_ref[...], NT_DIMS, preferred_element_type=F32)
            gf, uf = g_ref[rows, :].astype(F32), u_ref[rows, :].astype(F32)
            sig = jax.nn.sigmoid(gf)
            dgv = (da * uf * (sig * (1.0 + gf * (1.0 - sig)))).astype(BF16)
            duv = (da * (gf * sig)).astype(BF16)
            dg_ref[rows, :] = dgv
            du_ref[rows, :] = duv
            dh[rows, :] += (jnp.dot(dgv, wg_ref[...], preferred_element_type=F32)
                            + jnp.dot(duv, wu_ref[...], preferred_element_type=F32))

        @pl.when(j == nsh - 1)
        def _():
            xv = x_ref[...]
            r = lax.rsqrt(jnp.mean(xv * xv, axis=-1, keepdims=True) + EPS)
            xhat = xv * r
            dhv = dh[...]
            dxn = dhv * gain_ref[...]
            dx_ref[...] = dxo_ref[...] + r * (dxn - xhat * jnp.mean(dxn * xhat, axis=-1, keepdims=True))
            part = jnp.sum(dhv * xhat, axis=0, keepdims=True)

            @pl.when(i == 0)
            def _():
                dgain_ref[...] = part

            @pl.when(i != 0)
            def _():
                dgain_ref[...] += part

    return pl.pallas_call(
        body, name=name, grid=(S // tm, nsh),
        in_specs=[pl.BlockSpec((tm, D), lambda i, j: (i, 0)), pl.BlockSpec((tm, D), lambda i, j: (i, 0)),
                  pl.BlockSpec((1, D), lambda i, j: (0, 0)),
                  pl.BlockSpec((None, tm, Fs), lambda i, j: (j, i, 0)), pl.BlockSpec((None, tm, Fs), lambda i, j: (j, i, 0)),
                  pl.BlockSpec((None, Fs, D), lambda i, j: (j, 0, 0)), pl.BlockSpec((None, Fs, D), lambda i, j: (j, 0, 0)),
                  pl.BlockSpec((None, Fs, D), lambda i, j: (j, 0, 0))],
        out_specs=[pl.BlockSpec((tm, D), lambda i, j: (i, 0)), pl.BlockSpec((1, D), lambda i, j: (0, 0)),
                   pl.BlockSpec((None, tm, Fs), lambda i, j: (j, i, 0)), pl.BlockSpec((None, tm, Fs), lambda i, j: (j, i, 0))],
        out_shape=[jax.ShapeDtypeStruct((S, D), F32), jax.ShapeDtypeStruct((1, D), F32),
                   jax.ShapeDtypeStruct((nsh, S, Fs), BF16), jax.ShapeDtypeStruct((nsh, S, Fs), BF16)],
        scratch_shapes=[pltpu.VMEM((tm, D), BF16), pltpu.VMEM((tm, D), F32)], compiler_params=_params(2),
    )(dxo, x, gain, g, u, wg, wu, wd)


def ffn_bwd_w(name, h, dxo, g, u, dg, du, tm=FFN_WIDE_TILE):
    S, D = h.shape
    tm = min(tm, S)
    nsh, _, Fs = g.shape

    def body(h_ref, dxo_ref, g_ref, u_ref, dg_ref, du_ref, dwg_ref, dwu_ref, dwd_ref):
        i = pl.program_id(1)
        gf, uf = g_ref[...].astype(F32), u_ref[...].astype(F32)
        a = (gf * jax.nn.sigmoid(gf) * uf).astype(BF16)
        dd = (0.5 * dxo_ref[...]).astype(BF16)
        hv = h_ref[...]

        @pl.when(i == 0)
        def _():
            dwg_ref[...] = jnp.zeros_like(dwg_ref)
            dwu_ref[...] = jnp.zeros_like(dwu_ref)
            dwd_ref[...] = jnp.zeros_like(dwd_ref)

        dwg_ref[...] += lax.dot_general(dg_ref[...], hv, TN_DIMS, preferred_element_type=F32)
        dwu_ref[...] += lax.dot_general(du_ref[...], hv, TN_DIMS, preferred_element_type=F32)
        dwd_ref[...] += lax.dot_general(a, dd, TN_DIMS, preferred_element_type=F32)

    act = pl.BlockSpec((None, tm, Fs), lambda j, i: (j, i, 0))
    wspec = pl.BlockSpec((None, Fs, D), lambda j, i: (j, 0, 0))
    return pl.pallas_call(
        body, name=name, grid=(nsh, S // tm),
        in_specs=[pl.BlockSpec((tm, D), lambda j, i: (i, 0)), pl.BlockSpec((tm, D), lambda j, i: (i, 0)), act, act, act, act],
        out_specs=[wspec, wspec, wspec], out_shape=[jax.ShapeDtypeStruct((nsh, Fs, D), F32)] * 3,
        compiler_params=_params(2),
    )(h, dxo, g, u, dg, du)


NEG_BIG = -1e30


def _causal_pairs(n, by_key):
    pairs = [(qi, ki) for qi in range(n) for ki in range(qi + 1)]
    if by_key:
        pairs.sort(key=lambda p: (p[1], p[0]))
    return jnp.asarray([p[0] for p in pairs], jnp.int32), jnp.asarray([p[1] for p in pairs], jnp.int32)


def _scores(q, k, masked):
    s = lax.dot_general(q, k, (((1,), (1,)), ((), ())), preferred_element_type=F32)
    if masked:
        row = lax.broadcasted_iota(jnp.int32, s.shape, 0)
        col = lax.broadcasted_iota(jnp.int32, s.shape, 1)
        s = jnp.where(row >= col, s, NEG_BIG)
    return s


def attn_fwd(name, q, k, v, t=ATTN_TILE):
    S, Dk = q.shape[0], HEAD_PAD
    H = q.shape[1] // Dk
    Dv = v.shape[1] // H
    qt, kt = _causal_pairs(S // t, by_key=False)

    def body(qt_ref, kt_ref, q_ref, k_ref, v_ref, o_ref, lse_ref, m_sc, l_sc, acc):
        qi, ki = qt_ref[pl.program_id(1)], kt_ref[pl.program_id(1)]

        @pl.when(ki == 0)
        def _():
            m_sc[...] = jnp.full_like(m_sc, NEG_BIG)
            l_sc[...] = jnp.zeros_like(l_sc)
            acc[...] = jnp.zeros_like(acc)

        def step(masked):
            s = _scores(q_ref[...], k_ref[...], masked)
            m_prev = m_sc[...]
            m_next = jnp.maximum(m_prev, jnp.max(s, axis=-1, keepdims=True))
            alpha = jnp.exp(m_prev - m_next)
            p = jnp.exp(s - jnp.tile(m_next, (1, t // STAT_LANES)))
            l_sc[...] = alpha * l_sc[...] + jnp.sum(p, axis=-1, keepdims=True)
            acc[...] = alpha * acc[...] + jnp.dot(p.astype(BF16), v_ref[...].astype(BF16), preferred_element_type=F32)
            m_sc[...] = m_next

        @pl.when(ki < qi)
        def _():
            step(False)

        @pl.when(ki == qi)
        def _():
            step(True)
            o_ref[...] = acc[...] / l_sc[...]
            lse_ref[...] = m_sc[...] + jnp.log(l_sc[...])

    stat = pltpu.VMEM((t, STAT_LANES), F32)
    return pl.pallas_call(
        body, name=name,
        grid_spec=pltpu.PrefetchScalarGridSpec(
            num_scalar_prefetch=2, grid=(H, qt.shape[0]),
            in_specs=[pl.BlockSpec((t, Dk), lambda h, s, qt, kt: (qt[s], h)),
                      pl.BlockSpec((t, Dk), lambda h, s, qt, kt: (kt[s], h)),
                      pl.BlockSpec((t, Dv), lambda h, s, qt, kt: (kt[s], h))],
            out_specs=[pl.BlockSpec((t, Dv), lambda h, s, qt, kt: (qt[s], h)),
                       pl.BlockSpec((None, t, STAT_LANES), lambda h, s, qt, kt: (h, qt[s], 0))],
            scratch_shapes=[stat, stat, pltpu.VMEM((t, Dv), F32)]),
        out_shape=[jax.ShapeDtypeStruct((S, H * Dv), F32), jax.ShapeDtypeStruct((H, S, STAT_LANES), F32)],
        compiler_params=_params(2),
    )(qt, kt, q, k, v)


def attn_bwd(name, q, k, v, do, o, lse, t=ATTN_TILE):
    S, Dk = q.shape[0], HEAD_PAD
    H = q.shape[1] // Dk
    Dv = v.shape[1] // H
    qt, kt = _causal_pairs(S // t, by_key=True)
    tn_dims = (((0,), (0,)), ((), ()))

    def body(qt_ref, kt_ref, q_ref, k_ref, v_ref, do_ref, o_ref, lse_ref, dq_ref, dk_ref, dv_ref):
        step_id = pl.program_id(1)
        qi, ki = qt_ref[step_id], kt_ref[step_id]

        @pl.when(step_id == 0)
        def _():
            dq_ref[...] = jnp.zeros_like(dq_ref)

        def step(masked):
            s = _scores(q_ref[...], k_ref[...], masked)
            reps = (1, t // STAT_LANES)
            p = jnp.exp(s - jnp.tile(lse_ref[...], reps))
            dov = do_ref[...]
            delta = jnp.broadcast_to(jnp.sum(dov * o_ref[...], axis=-1, keepdims=True), (t, STAT_LANES))
            dob = dov.astype(BF16)
            dp = lax.dot_general(dob, v_ref[...].astype(BF16), (((1,), (1,)), ((), ())), preferred_element_type=F32)
            ds = (p * (dp - jnp.tile(delta, reps))).astype(BF16)
            pdv = lax.dot_general(p.astype(BF16), dob, tn_dims, preferred_element_type=F32)
            pdk = lax.dot_general(ds, q_ref[...], tn_dims, preferred_element_type=F32)
            rows = pl.ds(pl.multiple_of(qi * t, t), t)
            dq_ref[rows, :] += jnp.dot(ds, k_ref[...], preferred_element_type=F32)
            return pdk, pdv

        @pl.when(ki == qi)
        def _():
            dk_ref[...] = jnp.zeros_like(dk_ref)
            dv_ref[...] = jnp.zeros_like(dv_ref)

        def accumulate(masked):
            pdk, pdv = step(masked)
            dk_ref[...] += pdk
            dv_ref[...] += pdv

        @pl.when(ki == qi)
        def _():
            accumulate(True)

        @pl.when(ki < qi)
        def _():
            accumulate(False)

    qrow = lambda h, s, qt, kt: (qt[s], h)
    krow = lambda h, s, qt, kt: (kt[s], h)
    return pl.pallas_call(
        body, name=name,
        grid_spec=pltpu.PrefetchScalarGridSpec(
            num_scalar_prefetch=2, grid=(H, qt.shape[0]),
            in_specs=[pl.BlockSpec((t, Dk), qrow), pl.BlockSpec((t, Dk), krow), pl.BlockSpec((t, Dv), krow),
                      pl.BlockSpec((t, Dv), qrow), pl.BlockSpec((t, Dv), qrow),
                      pl.BlockSpec((None, t, STAT_LANES), lambda h, s, qt, kt: (h, qt[s], 0))],
            out_specs=[pl.BlockSpec((S, Dk), lambda h, s, qt, kt: (0, h)), pl.BlockSpec((t, Dk), krow),
                       pl.BlockSpec((t, Dv), krow)]),
        out_shape=[jax.ShapeDtypeStruct((S, H * Dk), F32), jax.ShapeDtypeStruct((S, H * Dk), F32),
                   jax.ShapeDtypeStruct((S, H * Dv), F32)],
        compiler_params=_params(2),
    )(qt, kt, q, k, v, do, o, lse)


def _cmul(ar, ai, br, bi):
    return ar * br - ai * bi, ar * bi + ai * br


def _scan_tile(x_r, x_i, ar_ref, ai_ref, cr_sc, ci_sc, *, reverse, first, states=None):
    tc, lanes = x_r.shape
    nblk, lb = tc // SUBLANES, SCAN_LANE_TILE
    with_da = states is not None
    if with_da:
        xr_all, xi_all, pr_all, pi_all, dar_all, dai_all, chunk = states

    @pl.when(first)
    def _():
        cr_sc[...] = jnp.zeros_like(cr_sc)
        ci_sc[...] = jnp.zeros_like(ci_sc)
        if with_da:
            dar_all[...] = jnp.zeros_like(dar_all)
            dai_all[...] = jnp.zeros_like(dai_all)

    row = lax.broadcasted_iota(jnp.int32, (SUBLANES, lb), 0)
    for l0 in range(0, lanes, lb):
        _scan_lanes(x_r.at[:, pl.ds(l0, lb)], x_i.at[:, pl.ds(l0, lb)], ar_ref[0:1, pl.ds(l0, lb)],
                    ai_ref[0:1, pl.ds(l0, lb)], cr_sc.at[:, pl.ds(l0, lb)], ci_sc.at[:, pl.ds(l0, lb)], row, reverse,
                    nblk, None if not with_da else tuple(r.at[:, pl.ds(l0, lb)] for r in states[:6]) + (chunk,))


def _scan_lanes(x_r, x_i, a1r, a1i, cr_sc, ci_sc, row, reverse, nblk, states):
    lb = x_r.shape[1]
    with_da = states is not None
    if with_da:
        xr_ref, xi_ref, pr_ref, pi_ref, dar_ref, dai_ref, chunk = states
    if reverse:
        a1i = -a1i
    a2r, a2i = _cmul(a1r, a1i, a1r, a1i)
    a4r, a4i = _cmul(a2r, a2i, a2r, a2i)
    pw_r, pw_i = jnp.zeros((SUBLANES, lb), F32), jnp.zeros((SUBLANES, lb), F32)
    cur_r, cur_i = a1r, a1i
    for e in range(SUBLANES):
        r_at = (SUBLANES - 1 - e) if reverse else e
        pw_r = jnp.where(row == r_at, cur_r, pw_r)
        pw_i = jnp.where(row == r_at, cur_i, pw_i)
        cur_r, cur_i = _cmul(cur_r, cur_i, a1r, a1i)
    steps = []
    for d, pr, pi in ((1, a1r, a1i), (2, a2r, a2i), (4, a4r, a4i)):
        keep = (row < SUBLANES - d) if reverse else (row >= d)
        steps.append((d, jnp.where(keep, pr, 0.0), jnp.where(keep, pi, 0.0)))

    def block(jb, carry):
        if with_da:
            cr, ci, acc_r, acc_i = carry
        else:
            cr, ci = carry
        idx = (nblk - 1 - jb) if reverse else jb
        r0 = pl.multiple_of(idx * SUBLANES, SUBLANES)
        xr = x_r[pl.ds(r0, SUBLANES), :]
        xi = x_i[pl.ds(r0, SUBLANES), :]
        for d, pr, pi in steps:
            shift = SUBLANES - d if reverse else d
            sr, si = pltpu.roll(xr, shift, 0), pltpu.roll(xi, shift, 0)
            xr, xi = xr + pr * sr - pi * si, xi + pr * si + pi * sr
        xr, xi = xr + pw_r * cr - pw_i * ci, xi + pw_r * ci + pw_i * cr
        x_r[pl.ds(r0, SUBLANES), :] = xr
        x_i[pl.ds(r0, SUBLANES), :] = xi
        edge = 0 if reverse else SUBLANES - 1
        cr, ci = xr[edge:edge + 1, :], xi[edge:edge + 1, :]
        if not with_da:
            return cr, ci
        fr = xr_ref[pl.ds(r0, SUBLANES), :]
        fi = xi_ref[pl.ds(r0, SUBLANES), :]
        rp = pl.multiple_of(jnp.maximum(idx - 1, 0) * SUBLANES, SUBLANES)
        inside = idx > 0
        before_r = jnp.where(inside, xr_ref[pl.ds(rp, SUBLANES), :], pr_ref[...])
        before_i = jnp.where(inside, xi_ref[pl.ds(rp, SUBLANES), :], pi_ref[...])
        live = jnp.where(jnp.logical_or(inside, chunk > 0), 1.0, 0.0)
        last_r = before_r[SUBLANES - 1:SUBLANES, :] * live
        last_i = before_i[SUBLANES - 1:SUBLANES, :] * live
        pvr = jnp.where(row == 0, last_r, pltpu.roll(fr, 1, 0))
        pvi = jnp.where(row == 0, last_i, pltpu.roll(fi, 1, 0))
        acc_r = acc_r + xr * pvr + xi * pvi
        acc_i = acc_i + xi * pvr - xr * pvi
        return cr, ci, acc_r, acc_i

    init = (cr_sc[...], ci_sc[...])
    if with_da:
        init = init + (jnp.zeros((SUBLANES, lb), F32), jnp.zeros((SUBLANES, lb), F32))
    fin = lax.fori_loop(0, nblk, block, init)
    cr_sc[...] = fin[0]
    ci_sc[...] = fin[1]
    if with_da:
        dar_ref[...] += fin[2]
        dai_ref[...] += fin[3]


SSM_BLOCKS = 4
BLOCK_CH = SSM_WIDTH // SSM_BLOCKS
PREP_LANES = SSM_LANES // SSM_BLOCKS


def ssm_fwd(name, u, bb_r, bb_i, cb_r, ncb_i, a_r8, a_i8):
    S = u.shape[0]
    tc = min(SCAN_TIME_TILE, S)

    def body(u_ref, bbr_ref, bbi_ref, cbr_ref, ncbi_ref, ar_ref, ai_ref, xr_ref, xi_ref, y_ref, cr_sc, ci_sc):
        ub = u_ref[...].astype(BF16)
        xr_ref[...] = jnp.dot(ub, bbr_ref[...], preferred_element_type=F32)
        xi_ref[...] = jnp.dot(ub, bbi_ref[...], preferred_element_type=F32)
        _scan_tile(xr_ref, xi_ref, ar_ref, ai_ref, cr_sc, ci_sc, reverse=False, first=pl.program_id(1) == 0)
        y_ref[...] = (lax.dot_general(xr_ref[...].astype(BF16), cbr_ref[...], NT_DIMS, preferred_element_type=F32)
                      + lax.dot_general(xi_ref[...].astype(BF16), ncbi_ref[...], NT_DIMS, preferred_element_type=F32))

    ch = pl.BlockSpec((tc, BLOCK_CH), lambda b, t: (t, b))
    st = pl.BlockSpec((tc, PREP_LANES), lambda b, t: (t, b))
    wt = pl.BlockSpec((None, BLOCK_CH, PREP_LANES), lambda b, t: (b, 0, 0))
    par = pl.BlockSpec((SUBLANES, PREP_LANES), lambda b, t: (0, b))
    return pl.pallas_call(
        body, name=name, grid=(SSM_BLOCKS, S // tc), in_specs=[ch, wt, wt, wt, wt, par, par], out_specs=[st, st, ch],
        out_shape=[jax.ShapeDtypeStruct((S, SSM_LANES), F32), jax.ShapeDtypeStruct((S, SSM_LANES), F32),
                   jax.ShapeDtypeStruct((S, SSM_WIDTH), F32)],
        scratch_shapes=[pltpu.VMEM((1, PREP_LANES), F32), pltpu.VMEM((1, PREP_LANES), F32)], compiler_params=_params(2),
    )(u, bb_r, bb_i, cb_r, ncb_i, a_r8, a_i8)


def ssm_bwd(name, dy, du_add, u, xs_r, xs_i, bb_r, bb_i, cb_r, ncb_i, a_r8, a_i8):
    S = u.shape[0]
    tc = min(SCAN_TIME_TILE, S)
    nt = S // tc

    def body(dy_ref, dua_ref, u_ref, xr_ref, xi_ref, pr_ref, pi_ref, bbr_ref, bbi_ref, cbr_ref, ncbi_ref, ar_ref, ai_ref,
             du_ref, dbbr_ref, dbbi_ref, dcbr_ref, dncbi_ref, dar_ref, dai_ref, lr_sc, li_sc, cr_sc, ci_sc):
        t = pl.program_id(1)
        first = t == 0

        @pl.when(first)
        def _():
            for r in (dbbr_ref, dbbi_ref, dcbr_ref, dncbi_ref):
                r[...] = jnp.zeros_like(r)

        dyb = dy_ref[...].astype(BF16)
        lr_sc[...] = jnp.dot(dyb, cbr_ref[...], preferred_element_type=F32)
        li_sc[...] = jnp.dot(dyb, ncbi_ref[...], preferred_element_type=F32)
        _scan_tile(lr_sc, li_sc, ar_ref, ai_ref, cr_sc, ci_sc, reverse=True, first=first,
                   states=(xr_ref, xi_ref, pr_ref, pi_ref, dar_ref, dai_ref, nt - 1 - t))
        lrb, lib = lr_sc[...].astype(BF16), li_sc[...].astype(BF16)
        du_ref[...] = (lax.dot_general(lrb, bbr_ref[...], NT_DIMS, preferred_element_type=F32)
                       + lax.dot_general(lib, bbi_ref[...], NT_DIMS, preferred_element_type=F32) + dua_ref[...])
        ub = u_ref[...].astype(BF16)
        dbbr_ref[...] += lax.dot_general(ub, lrb, TN_DIMS, preferred_element_type=F32)
        dbbi_ref[...] += lax.dot_general(ub, lib, TN_DIMS, preferred_element_type=F32)
        dcbr_ref[...] += lax.dot_general(dyb, xr_ref[...].astype(BF16), TN_DIMS, preferred_element_type=F32)
        dncbi_ref[...] += lax.dot_general(dyb, xi_ref[...].astype(BF16), TN_DIMS, preferred_element_type=F32)

    ch = pl.BlockSpec((tc, BLOCK_CH), lambda b, t: (nt - 1 - t, b))
    st = pl.BlockSpec((tc, PREP_LANES), lambda b, t: (nt - 1 - t, b))
    prev = pl.BlockSpec((SUBLANES, PREP_LANES), lambda b, t: (jnp.maximum((nt - 1 - t) * (tc // SUBLANES) - 1, 0), b))
    wt = pl.BlockSpec((None, BLOCK_CH, PREP_LANES), lambda b, t: (b, 0, 0))
    par = pl.BlockSpec((SUBLANES, PREP_LANES), lambda b, t: (0, b))
    blk = jax.ShapeDtypeStruct((SSM_BLOCKS, BLOCK_CH, PREP_LANES), F32)
    rows8 = jax.ShapeDtypeStruct((SUBLANES, SSM_LANES), F32)
    return pl.pallas_call(
        body, name=name, grid=(SSM_BLOCKS, nt), in_specs=[ch, ch, ch, st, st, prev, prev, wt, wt, wt, wt, par, par],
        out_specs=[ch, wt, wt, wt, wt, par, par],
        out_shape=[jax.ShapeDtypeStruct((S, SSM_WIDTH), F32), blk, blk, blk, blk, rows8, rows8],
        scratch_shapes=[pltpu.VMEM((tc, PREP_LANES), F32), pltpu.VMEM((tc, PREP_LANES), F32),
                        pltpu.VMEM((1, PREP_LANES), F32), pltpu.VMEM((1, PREP_LANES), F32)],
        compiler_params=_params(2),
    )(dy, du_add, u, xs_r, xs_i, xs_r, xs_i, bb_r, bb_i, cb_r, ncb_i, a_r8, a_i8)


def _ssm_prep_f(a_re, a_im, log_dt, bt_re, bt_im, c_re, c_im):
    first_group = pl.program_id(0) * (SSM_GROUPS // SSM_BLOCKS)
    iota = lambda shape, d: lax.broadcasted_iota(jnp.int32, shape, d)
    grp_of_row = lambda shape: iota(shape, 0) >> int(math.log2(SSM_GROUP))
    grp_of_lane = lambda shape: iota(shape, 1) >> int(math.log2(SSM_STATE))
    rep = (grp_of_row((BLOCK_CH, SSM_GROUPS)) + first_group == iota((BLOCK_CH, SSM_GROUPS), 1)).astype(F32)
    til = ((iota((SSM_STATE, PREP_LANES), 1) & (SSM_STATE - 1)) == iota((SSM_STATE, PREP_LANES), 0)).astype(F32)
    m_rows = (grp_of_row((BLOCK_CH, PREP_LANES)) == grp_of_lane((BLOCK_CH, PREP_LANES))).astype(F32)
    m_grp = (iota((SSM_GROUPS, PREP_LANES), 0) == grp_of_lane((SSM_GROUPS, PREP_LANES)) + first_group).astype(F32)
    dt = jnp.exp(log_dt)
    decay = jnp.exp(a_re * dt)
    ar = decay * jnp.cos(a_im * dt)
    ai = decay * jnp.sin(a_im * dt)
    den = a_re * a_re + a_im * a_im
    nr = ar - 1.0
    coef_r = (nr * a_re + ai * a_im) / den
    coef_i = (ai * a_re - nr * a_im) / den
    cr, ci = _hp_dot(rep, coef_r), _hp_dot(rep, coef_i)
    bb_r = cr * bt_re - ci * bt_im
    bb_i = cr * bt_im + ci * bt_re
    big = lambda m: _hp_dot(m, til) * m_rows
    lanes = lambda m: jnp.broadcast_to(jnp.sum(_hp_dot(m, til) * m_grp, axis=0, keepdims=True), (SUBLANES, PREP_LANES))
    return lanes(ar), lanes(ai), big(bb_r), big(bb_i), big(c_re), -big(c_im)


def _whole(arr, **kw):
    nd = arr.ndim
    return In(arr, arr.shape, lambda *_: (0,) * nd, **kw)


def _adamw_math(w, g, m, v):
    m = ADAM_B1 * m + (1.0 - ADAM_B1) * g
    v = ADAM_B2 * v + (1.0 - ADAM_B2) * (g * g)
    m_hat = m / (1.0 - ADAM_B1 ** ADAM_STEP)
    v_hat = v / (1.0 - ADAM_B2 ** ADAM_STEP)
    delta = -ADAM_LR * (m_hat / (jnp.sqrt(v_hat) + ADAM_EPS) + ADAM_WD * w)
    return delta, m, v


def adamw_big(name, w, g, m, v, after):
    R, C = w.shape
    tr = R
    for cand in (512, 344, 256, 128):
        if R % cand == 0:
            tr = cand
            break

    def body(w_ref, g_ref, m_ref, v_ref, after_ref, d_ref, nm_ref, nv_ref):
        d, nm, nv = _adamw_math(w_ref[...], g_ref[...], m_ref[...], v_ref[...])
        d_ref[...] = d
        nm_ref[...] = nm
        nv_ref[...] = nv

    spec = pl.BlockSpec((tr, C), lambda i: (i, 0))
    return pl.pallas_call(
        body, name=name, grid=(R // tr,), in_specs=[spec] * 4 + [pl.BlockSpec(memory_space=pl.ANY)], out_specs=[spec] * 3,
        out_shape=[jax.ShapeDtypeStruct((R, C), F32)] * 3, compiler_params=_params(1),
    )(w, g, m, v, after)


def adamw_small(name, ws, gs, ms, vs, after):
    n = len(ws)

    def body(*refs):
        outs = refs[4 * n + 1:]
        for k in range(n):
            d, nm, nv = _adamw_math(refs[k][...], refs[n + k][...], refs[2 * n + k][...], refs[3 * n + k][...])
            outs[k][...] = d
            outs[n + k][...] = nm
            outs[2 * n + k][...] = nv

    vm = pl.BlockSpec(memory_space=pltpu.VMEM)
    shapes = [jax.ShapeDtypeStruct(w.shape, F32) for w in ws]
    res = pl.pallas_call(
        body, name=name, in_specs=[vm] * (4 * n) + [pl.BlockSpec(memory_space=pl.ANY)], out_specs=[vm] * (3 * n),
        out_shape=shapes * 3, compiler_params=pltpu.CompilerParams(vmem_limit_bytes=VMEM_LIMIT_BYTES),
    )(*ws, *gs, *ms, *vs, after)
    return res[:n], res[n:2 * n], res[2 * n:]


def _place():
    return lax.axis_index("x"), lax.axis_index("y"), lax.axis_index("c")


def _other_chips(x, y):
    return [(1 - x, y), (x, 1 - y), (1 - x, 1 - y)]


HBM = pl.BlockSpec(memory_space=pl.ANY)


def all_gather_halves(name, blocks):
    n = len(blocks)

    def body(*refs):
        in_refs, out_refs = refs[:n], refs[n:2 * n]
        send_sems, recv_sems = refs[2 * n:]
        x, y, c = _place()
        me, sibling = (x, y, c), (x, y, 1 - c)
        chips = _other_chips(x, y)

        def slot(a, px, py, pc):
            return out_refs[a].at[4 * px + 2 * py + pc]

        def copy(a, k, block, to, src=None):
            return pltpu.make_async_remote_copy(
                src_ref=slot(a, *block) if src is None else src, dst_ref=slot(a, *block),
                send_sem=send_sems.at[a, k], recv_sem=recv_sems.at[a, k], device_id=to, device_id_type=MESH)

        first = []
        for a in range(n):
            first.append(copy(a, 0, me, sibling, src=in_refs[a]))
            first += [copy(a, 1 + j, me, (*chip, c), src=in_refs[a]) for j, chip in enumerate(chips)]
        for cp in first:
            cp.start()
        passed = []
        for j, chip in enumerate(chips):
            for a in range(n):
                copy(a, 1 + j, (*chip, c), me).wait_recv()
                fw = copy(a, 4 + j, (*chip, c), sibling)
                fw.start()
                passed.append(fw)
        for a in range(n):
            copy(a, 0, sibling, me).wait_recv()
            for j, chip in enumerate(chips):
                copy(a, 4 + j, (*chip, 1 - c), me).wait_recv()
        for cp in first + passed:
            cp.wait_send()

    res = pl.pallas_call(
        body, name=name, in_specs=[HBM] * n, out_specs=[HBM] * n,
        out_shape=[jax.ShapeDtypeStruct((N_DEV,) + b.shape, b.dtype) for b in blocks],
        scratch_shapes=[pltpu.SemaphoreType.DMA((n, 7)), pltpu.SemaphoreType.DMA((n, 7))],
    )(*blocks)
    return list(res)


def pair_send_halves(name, grads):
    n = len(grads)

    def body(*refs):
        in_refs, out_refs = refs[:n], refs[n:2 * n]
        send_sems, recv_sems = refs[2 * n:]
        x, y, c = _place()
        cps = []
        for a in range(n):
            cp = pltpu.make_async_remote_copy(
                src_ref=in_refs[a].at[:, 1 - c], dst_ref=out_refs[a], send_sem=send_sems.at[a], recv_sem=recv_sems.at[a],
                device_id=(x, y, 1 - c), device_id_type=MESH)
            cp.start()
            cps.append(cp)
        for cp in cps:
            cp.wait()

    res = pl.pallas_call(
        body, name=name, in_specs=[HBM] * n, out_specs=[HBM] * n,
        out_shape=[jax.ShapeDtypeStruct((g.shape[0],) + g.shape[2:], g.dtype) for g in grads],
        scratch_shapes=[pltpu.SemaphoreType.DMA((n,)), pltpu.SemaphoreType.DMA((n,))],
    )(*grads)
    return list(res)


def pair_add(name, grad, got, c_arr):
    nsh, _, M, N = grad.shape
    tr = M
    for cand in (512, 256, 192, 128, 64, 16):
        if M % cand == 0:
            tr = cand
            break

    def body(c_ref, g_ref, p_ref, o_ref):
        o_ref[...] = (g_ref[...] + p_ref[...]).astype(BF16)

    return pl.pallas_call(
        body, name=name,
        grid_spec=pltpu.PrefetchScalarGridSpec(
            num_scalar_prefetch=1, grid=(nsh, M // tr),
            in_specs=[pl.BlockSpec((None, None, tr, N), lambda j, i, c_ref: (j, c_ref[0], i, 0)),
                      pl.BlockSpec((None, tr, N), lambda j, i, c_ref: (j, i, 0))],
            out_specs=pl.BlockSpec((None, tr, N), lambda j, i, c_ref: (j, i, 0))),
        out_shape=jax.ShapeDtypeStruct((nsh, M, N), BF16), compiler_params=_params(2),
    )(c_arr, grad, got)


def scatter_to_chips(name, parts):
    n = len(parts)

    def body(*refs):
        in_refs, out_refs = refs[:n], refs[n:2 * n]
        send_sems, recv_sems = refs[2 * n:]
        x, y, c = _place()
        mine = 2 * x + y
        chips = _other_chips(x, y)
        cps = []
        for a in range(n):
            for k, (px, py) in enumerate(chips):
                cp = pltpu.make_async_remote_copy(
                    src_ref=in_refs[a].at[2 * px + py], dst_ref=out_refs[a].at[mine],
                    send_sem=send_sems.at[a, k], recv_sem=recv_sems.at[a, k], device_id=(px, py, c), device_id_type=MESH)
                cp.start()
                cps.append((cp, a, k, px, py))
        for cp, a, k, px, py in cps:
            pltpu.make_async_remote_copy(
                src_ref=in_refs[a].at[mine], dst_ref=out_refs[a].at[2 * px + py],
                send_sem=send_sems.at[a, k], recv_sem=recv_sems.at[a, k], device_id=(px, py, c), device_id_type=MESH).wait_recv()
        for cp, *_ in cps:
            cp.wait_send()

    res = pl.pallas_call(
        body, name=name, in_specs=[HBM] * n, out_specs=[HBM] * n,
        out_shape=[jax.ShapeDtypeStruct(p.shape, p.dtype) for p in parts],
        scratch_shapes=[pltpu.SemaphoreType.DMA((n, 3)), pltpu.SemaphoreType.DMA((n, 3))],
    )(*parts)
    return list(res)


def sum_chips(name, q):
    nsh, M, N = q.shape
    tr = M
    for cand in (512, 256, 192, 128, 64, 16):
        if M % cand == 0:
            tr = cand
            break

    def body(q_ref, o_ref):
        acc = q_ref[0].astype(F32)
        for j in range(1, nsh):
            acc = acc + q_ref[j].astype(F32)
        o_ref[...] = acc

    return pl.pallas_call(
        body, name=name, grid=(M // tr,), in_specs=[pl.BlockSpec((nsh, tr, N), lambda i: (0, i, 0))],
        out_specs=pl.BlockSpec((tr, N), lambda i: (i, 0)), out_shape=jax.ShapeDtypeStruct((M, N), F32),
        compiler_params=_params(1),
    )(q)


def pair_exchange(name, halves):
    n = len(halves)

    def body(*refs):
        in_refs, out_refs = refs[:n], refs[n:2 * n]
        send_sems, recv_sems = refs[2 * n:]
        x, y, c = _place()
        cps = []
        for a in range(n):
            cp = pltpu.make_async_remote_copy(
                src_ref=in_refs[a], dst_ref=out_refs[a], send_sem=send_sems.at[a], recv_sem=recv_sems.at[a],
                device_id=(x, y, 1 - c), device_id_type=MESH)
            cp.start()
            cps.append(cp)
        for cp in cps:
            cp.wait()

    res = pl.pallas_call(
        body, name=name, in_specs=[HBM] * n, out_specs=[HBM] * n,
        out_shape=[jax.ShapeDtypeStruct(h.shape, h.dtype) for h in halves],
        scratch_shapes=[pltpu.SemaphoreType.DMA((n,)), pltpu.SemaphoreType.DMA((n,))],
    )(*halves)
    return list(res)


SEM = pl.BlockSpec(memory_space=pltpu.SEMAPHORE)
IN_HBM = pl.BlockSpec(memory_space=pltpu.HBM)
SPLIT_COPY = pltpu.CompilerParams(has_side_effects=pltpu.SideEffectType.DATAFLOW_SIDE_EFFECTING)


def _scatter_copies(src_refs, dst_refs, send_sems, recv_sems):
    x, y, c = _place()
    mine = 2 * x + y
    return [pltpu.make_async_remote_copy(
        src_ref=src_refs[a].at[2 * px + py], dst_ref=dst_refs[a].at[mine], send_sem=send_sems.at[a * (N_CHIPS - 1) + k],
        recv_sem=recv_sems.at[a * (N_CHIPS - 1) + k], device_id=(px, py, c), device_id_type=MESH)
        for a in range(len(src_refs)) for k, (px, py) in enumerate(_other_chips(x, y))]


def _pair_copies(src_refs, dst_refs, send_sems, recv_sems):
    x, y, c = _place()
    return [pltpu.make_async_remote_copy(
        src_ref=src_refs[a].at[:, 1 - c], dst_ref=dst_refs[a], send_sem=send_sems.at[a], recv_sem=recv_sems.at[a],
        device_id=(x, y, 1 - c), device_id_type=MESH) for a in range(len(src_refs))]


def _gather_copies(src_refs, dst_refs, send_sems, recv_sems):
    x, y, c = _place()
    me = 4 * x + 2 * y + c
    cps = []
    for a in range(len(src_refs)):
        for k in range(1, N_DEV):
            to = (1 - x if k & 4 else x, 1 - y if k & 2 else y, 1 - c if k & 1 else c)
            s = a * (N_DEV - 1) + k - 1
            cps.append(pltpu.make_async_remote_copy(
                src_ref=src_refs[a], dst_ref=dst_refs[a].at[me], send_sem=send_sems.at[s], recv_sem=recv_sems.at[s],
                device_id=to, device_id_type=MESH))
    return cps


def split_copy_start(name, copies, n_sem, srcs, land_shapes):
    n = len(srcs)
    lands = [lax.empty(s.shape, s.dtype) for s in land_shapes]

    def body(*refs):
        for cp in copies(refs[:n], refs[n:2 * n], refs[2 * n], refs[2 * n + 1]):
            cp.start()
        refs[-1][...] = jnp.zeros_like(refs[-1])

    thru = [pltpu.HBM(a.shape, a.dtype) for a in (*srcs, *lands)]
    res = pl.pallas_call(
        body, name=name, in_specs=[IN_HBM] * (2 * n),
        out_specs=(SEM, SEM, *[IN_HBM] * (2 * n), pl.BlockSpec(memory_space=pltpu.VMEM)),
        out_shape=(pltpu.SemaphoreType.DMA((n * n_sem,)), pltpu.SemaphoreType.DMA((n * n_sem,)), *thru,
                   jax.ShapeDtypeStruct((SUBLANES, LANES), F32)),
        input_output_aliases={i: 2 + i for i in range(2 * n)}, compiler_params=SPLIT_COPY,
    )(*[pltpu.with_memory_space_constraint(a, pltpu.HBM) for a in (*srcs, *lands)])
    return (copies, n, res[0], res[1], res[2:2 + 2 * n]), res[-1][0, 0]


def split_copy_wait(name, handle, after):
    copies, n, send_sems, recv_sems, thru = handle

    def body(*refs):
        for cp in copies(refs[:n], refs[n:2 * n], refs[2 * n], refs[2 * n + 1]):
            cp.wait_send()
            cp.wait_recv()

    res = pl.pallas_call(
        body, name=name, in_specs=[IN_HBM] * (2 * n) + [SEM, SEM, pl.BlockSpec(memory_space=pl.ANY)],
        out_specs=[IN_HBM] * (2 * n), out_shape=[pltpu.HBM(a.shape, a.dtype) for a in thru],
        input_output_aliases={i: i for i in range(2 * n)}, compiler_params=SPLIT_COPY,
    )(*thru, send_sems, recv_sems, after)
    return list(res[:n]), list(res[n:])


def all_reduce_small(name, v):
    R, C = v.shape

    def body(v_ref, o_ref, gath, send_sems, recv_sems):
        x, y, c = _place()
        me, sibling = (x, y, c), (x, y, 1 - c)
        chips = _other_chips(x, y)

        def slot(px, py, pc):
            return gath.at[4 * px + 2 * py + pc]

        def copy(k, block, to, src=None):
            return pltpu.make_async_remote_copy(
                src_ref=slot(*block) if src is None else src, dst_ref=slot(*block),
                send_sem=send_sems.at[k], recv_sem=recv_sems.at[k], device_id=to, device_id_type=MESH)

        first = [copy(0, me, sibling, src=v_ref)]
        first += [copy(1 + j, me, (*chip, c), src=v_ref) for j, chip in enumerate(chips)]
        for cp in first:
            cp.start()
        slot(*me)[...] = v_ref[...]
        passed = [copy(4 + j, (*chip, c), sibling) for j, chip in enumerate(chips)]
        for j, chip in enumerate(chips):
            copy(1 + j, (*chip, c), me).wait_recv()
            passed[j].start()
        copy(0, sibling, me).wait_recv()
        for j, chip in enumerate(chips):
            copy(4 + j, (*chip, 1 - c), me).wait_recv()
        for cp in first + passed:
            cp.wait_send()
        acc = gath[0]
        for d in range(1, N_DEV):
            acc = acc + gath[d]
        o_ref[...] = acc

    vm = pl.BlockSpec(memory_space=pltpu.VMEM)
    return pl.pallas_call(
        body, name=name, in_specs=[vm], out_specs=vm, out_shape=jax.ShapeDtypeStruct((R, C), F32),
        scratch_shapes=[pltpu.VMEM((N_DEV, R, C), F32), pltpu.SemaphoreType.DMA((7,)), pltpu.SemaphoreType.DMA((7,))],
        compiler_params=pltpu.CompilerParams(vmem_limit_bytes=VMEM_LIMIT_BYTES),
    )(v)


LANES = 128
PACK_ROW_MULTIPLE = 1024
SMALL_SHARDED = {
    "w_in": ((D_MODEL, 1216), 1), "mla_w_uq": ((MLA_Q_RANK, 768), 1), "mla_w_ukv": ((MLA_KV_RANK, 1024), 1),
    "ssm_w_glu": ((SSM_WIDTH, SSM_WIDTH), 0), "w_o": ((D_MODEL, D_MODEL), 0), "xattn_w_q": ((D_MODEL, 512), 0),
    "xattn_w_kv": ((D_MODEL, 1024), 0), "xattn_w_o": ((512, D_MODEL), 1),
}
FFN_NAMES = ["ffn1_w_gate", "ffn1_w_up", "ffn1_w_down", "ffn2_w_gate", "ffn2_w_up", "ffn2_w_down"]
TRANSPOSED_VIEW = ("ffn1_w_gate", "ffn1_w_up", "ffn2_w_gate", "ffn2_w_up", "w_in", "mla_w_uq")


def _shard_shape(name):
    (r, cdim), ax = SMALL_SHARDED[name]
    return (r // N_CHIPS, cdim) if ax == 0 else (r, cdim // N_CHIPS)


def _pack_shards(shards):
    parts = []
    for name in SMALL_SHARDED:
        a = shards[name]
        lead = a.shape[:-2]
        parts.append(a.reshape(lead + (a.shape[-2] * a.shape[-1] // LANES, LANES)))
    rows = sum(q.shape[-2] for q in parts)
    parts.append(jnp.zeros(lead + (-rows % PACK_ROW_MULTIPLE, LANES), parts[0].dtype))
    return jnp.concatenate(parts, axis=-2)


def _unpack_shards(packed):
    out, r0 = {}, 0
    lead = packed.shape[:-2]
    for name in SMALL_SHARDED:
        r, cdim = _shard_shape(name)
        rows = r * cdim // LANES
        out[name] = packed[..., r0:r0 + rows, :].reshape(lead + (r, cdim))
        r0 += rows
    return out


def _full_from_shards(name, sh):
    (r, cdim), ax = SMALL_SHARDED[name]
    if ax == 0:
        return sh.reshape(r, cdim)
    return jnp.transpose(sh, (1, 0, 2)).reshape(r, cdim)


def _shards_from_full(name, full):
    (r, cdim), ax = SMALL_SHARDED[name]
    if ax == 0:
        return full.reshape(N_CHIPS, r // N_CHIPS, cdim)
    return jnp.transpose(full.reshape(r, N_CHIPS, cdim // N_CHIPS), (1, 0, 2))


SMALL_REPL = {
    "ffn1_norm": (1, 1024), "mix_norm": (1, 1024), "mla_q_norm": (1, 384), "mla_kv_norm": (1, 256),
    "mla_qk_norm_q": (1, 192), "mla_qk_norm_k": (1, 192), "ssm_a_re": (32, 64), "ssm_a_im": (32, 64),
    "ssm_log_dt": (32, 1), "ssm_b_re": (32, 64, 16), "ssm_b_im": (32, 64, 16), "ssm_c_re": (32, 16, 64),
    "ssm_c_im": (32, 16, 64), "ssm_d": (1, 512), "ssm_b_glu": (1, 512), "out_norm_mla": (1, 512),
    "out_norm_ssm": (1, 512), "xattn_norm": (1, 1024), "mem_norm": (1, 1024), "xattn_q_norm": (1, 128),
    "xattn_k_norm": (1, 128), "ffn2_norm": (1, 1024),
}


def _pack_repl(grads, loss):
    flat = jnp.concatenate([grads[n].reshape(-1) for n in SMALL_REPL] + [loss.reshape(1)])
    rows = -(-flat.shape[0] // (LANES * SUBLANES)) * SUBLANES
    return jnp.pad(flat, (0, rows * LANES - flat.shape[0])).reshape(rows, LANES)


def _unpack_repl(packed):
    flat, out, o = packed.reshape(-1), {}, 0
    for n, shp in SMALL_REPL.items():
        size = int(np.prod(shp))
        out[n] = flat[o:o + size].reshape(shp)
        o += size
    out["loss"] = flat[o]
    return out


def _rope_tables(positions):
    half = MLA_ROPE // 2
    inv = ROPE_THETA ** (-jnp.arange(half, dtype=F32) / half)
    ang = positions.astype(F32)[:, None] * inv[None, :]
    cos, sin = jnp.cos(ang), jnp.sin(ang)
    S = positions.shape[0]
    z = lambda w: jnp.zeros((S, w), F32)
    keep = jnp.concatenate([jnp.ones((S, MLA_NOPE), F32), cos, cos, z(HEAD_PAD - MLA_QK)], axis=1)
    from_hi = jnp.concatenate([z(MLA_NOPE), -sin, z(HEAD_PAD - MLA_NOPE - half)], axis=1)
    from_lo = jnp.concatenate([z(MLA_NOPE + half), sin, z(HEAD_PAD - MLA_QK)], axis=1)
    return keep, from_hi, from_lo


def _norm_rope(x, g, keep, from_hi, from_lo):
    y = x * lax.rsqrt(jnp.sum(x * x, axis=-1, keepdims=True) * (1.0 / MLA_QK) + EPS) * g
    half = MLA_ROPE // 2
    return y * keep + _lane_roll(y, HEAD_PAD - half) * from_hi + _lane_roll(y, half) * from_lo


def local_step(x, mem, positions, target, w, wb, small_weights=None, ffn2_weights=None, on_grads=None):
    if small_weights is None:
        small_weights = lambda after: {}
    if ffn2_weights is None:
        ffn2_weights = lambda after: [wb[k] for k in FFN_NAMES[3:]]
    if on_grads is None:
        on_grads = lambda tag, g: 0.0
    S = x.shape[0]
    tm = min(FUSED_TILE, S)
    g1 = (S // tm,)
    tile = lambda arr, **kw: In(arr, (tm, arr.shape[1]), lambda i: (i, 0), rows=True, **kw)
    par = lambda arr, **kw: In(arr, arr.shape, lambda *_: (0, 0), diff=True, acc=(0,), **kw)
    wt = lambda arr: In(arr, weight=True)
    otile = lambda cols, dt: Out((S, cols), dt, (tm, cols), lambda i: (i, 0))
    grads = {}

    x1, h1, g_1, u_1 = ffn_fwd("ffn1_fwd", x, w["ffn1_norm"], wb["ffn1_w_gate"], wb["ffn1_w_up"], wb["ffn1_w_down"])
    wb = {**wb, **small_weights(x1)}

    keep, from_hi, from_lo = _rope_tables(positions)
    scale = MLA_QK ** -0.5

    def f_pre(xv, kp, fh, fl, g_mix, w_q, w_kv, w_kr, w_u, g_q, g_kv, w_uq, w_ukv, gq, gk):
        h2 = _rms(xv, g_mix)
        cq, ckv = _rms(wdot(h2, w_q), g_q), _rms(wdot(h2, w_kv), g_kv)
        kr = wdot(h2, w_kr)
        qs, ks, vs = [], [], []
        for h in range(MLA_HEADS):
            qs.append(_norm_rope(wdot(cq, w_uq, h), gq, kp, fh, fl) * scale)
            kv = wdot(ckv, w_ukv, h)
            ks.append(_norm_rope(jnp.concatenate([kv[:, :MLA_NOPE], kr], axis=-1), gk, kp, fh, fl))
            vs.append(kv[:, MLA_NOPE:])
        return jnp.concatenate(qs, axis=-1), jnp.concatenate(ks, axis=-1), jnp.concatenate(vs, axis=-1), wdot(h2, w_u)

    def pre_ins(gain_mix):
        return [tile(x1, diff=True), tile(keep), tile(from_hi), tile(from_lo), par(gain_mix),
                wt(wb["w_in_q"]), wt(wb["w_in_kv"]), wt(wb["w_in_kr"]), wt(wb["w_in_u"]),
                par(w["mla_q_norm"]), par(w["mla_kv_norm"]), wt(wb["w_uq"]), wt(wb["w_ukv"]),
                par(w["qk_gain_q"]), par(w["qk_gain_k"])]

    pre_outs = [otile(MLA_HEADS * HEAD_PAD, BF16), otile(MLA_HEADS * HEAD_PAD, BF16), otile(MLA_HEADS * MLA_V, BF16),
                otile(SSM_WIDTH, F32)]
    qh, kh, v0, pu = seg_fwd("pre_mixer_fwd", f_pre, g1, pre_ins(w["mix_norm"]), pre_outs)

    o_mla, lse = attn_fwd("mla_attn_fwd", qh, kh, v0, t=min(ATTN_TILE, S))

    prep_grid = (SSM_BLOCKS,)
    prep_ins = ([_whole(w[k], diff=True, acc=(0,)) for k in ("ssm_a_re", "ssm_a_im", "ssm_log_dt")]
                + [In(w[k], (BLOCK_CH, SSM_STATE), lambda i: (i, 0), diff=True)
                   for k in ("ssm_bt_re", "ssm_bt_im", "ssm_c2_re", "ssm_c2_im")])
    blk_out = Out((SSM_BLOCKS, BLOCK_CH, PREP_LANES), BF16, (None, BLOCK_CH, PREP_LANES), lambda i: (i, 0, 0))
    prep_outs = [Out((SUBLANES, SSM_LANES), F32, (SUBLANES, PREP_LANES), lambda i: (0, i))] * 2 + [blk_out] * 4
    a_r8, a_i8, bb_r, bb_i, cb_r, ncb_i = seg_fwd("ssm_prep_fwd", _ssm_prep_f, prep_grid, prep_ins, prep_outs)

    xs_r, xs_i, y_lin = ssm_fwd("ssm_fwd", pu, bb_r, bb_i, cb_r, ncb_i, a_r8, a_i8)

    M = mem.shape[0]
    f_norm = lambda xv, g: (_rms(xv, g),)
    mem_ins = [In(mem, (M, D_MODEL), lambda i: (0, 0)), par(w["mem_norm"])]
    mem_outs = [Out((M, D_MODEL), BF16, (M, D_MODEL), lambda i: (0, 0))]
    (mn,) = seg_fwd("mem_norm_fwd", f_norm, (1,), mem_ins, mem_outs)
    (kvm,) = mm("xattn_kv_fwd", [mn], [[wb["xattn_w_kv"]]], [F32])

    def f_knorm(kk, gk):
        return (jnp.concatenate([_rms(kk[:, h * XHD:(h + 1) * XHD], gk) for h in range(XH)], axis=-1),)

    keys = ((M, XH * XHD), (M, XH * XHD), lambda i: (0, 0))
    knorm_ins = [In(kvm, keys[1], keys[2], diff=True, grad=keys), par(w["xattn_k_norm"])]
    knorm_outs = [Out(keys[0], F32, keys[1], keys[2])]
    (kn_mem,) = seg_fwd("mem_key_norm_fwd", f_knorm, (1,), knorm_ins, knorm_outs)

    xscale = XHD ** -0.5

    def f_post(o, yl, u, xv, kn, vm, d, w_glu, b, gm, gs, w_o1, w_o2, gx, w_xq, gq, w_xo):
        gl = jax.nn.gelu(yl + d * u)
        so = gl * jax.nn.sigmoid(wdot(gl, w_glu) + b)
        x2 = xv + wdot(_rms(o, gm), w_o1) + wdot(_rms(so, gs), w_o2)
        h3 = _rms(x2, gx)
        heads = []
        for h in range(XH):
            qn = _rms(wdot(h3, w_xq, h), gq)
            p = jax.nn.softmax(_bdot_nt(qn, kn[:, h * XHD:(h + 1) * XHD]) * xscale, axis=-1)
            heads.append(_bdot_nn(p, vm[:, h * XHD:(h + 1) * XHD]))
        return (x2 + wdot(jnp.concatenate(heads, axis=-1), w_xo),)

    def post_ins(gain_d):
        half = (M, XH * XHD)
        return [tile(o_mla, diff=True), tile(y_lin, diff=True), tile(pu, diff=True), tile(x1, diff=True),
                In(kn_mem, half, lambda i: (0, 0), diff=True, acc=(0,)),
                In(kvm, half, lambda i: (0, 1), diff=True, acc=(0,), grad=(half, half, lambda i: (0, 0))),
                par(gain_d), wt(wb["ssm_w_glu"]), par(w["ssm_b_glu"]), par(w["out_norm_mla"]), par(w["out_norm_ssm"]),
                wt(wb["w_o_mla"]), wt(wb["w_o_ssm"]), par(w["xattn_norm"]), wt(wb["xattn_w_q"]), par(w["xattn_q_norm"]),
                wt(wb["xattn_w_o"])]

    post_outs = [otile(D_MODEL, F32)]
    (x3,) = seg_fwd("post_mixer_fwd", f_post, g1, post_ins(w["ssm_d"]), post_outs)

    wg2, wu2, wd2 = ffn2_weights(x3)
    dx4, h4, g_2, u_2, parts = ffn_fwd("ffn2_fwd_loss", x3, w["ffn2_norm"], wg2, wu2, wd2, target=target)
    loss = jnp.sum(parts[::SUBLANES, 0])

    dx3, grads["ffn2_norm"], dg_2, du_2 = ffn_bwd_act("ffn2_bwd_act", dx4, x3, w["ffn2_norm"], g_2, u_2, wg2, wu2, wd2)
    grads["ffn2_w_gate"], grads["ffn2_w_up"], grads["ffn2_w_down"] = ffn_bwd_w("ffn2_bwd_w", h4, dx4, g_2, u_2, dg_2, du_2)
    sent = on_grads("ffn2", grads)

    (do_mla, dy_lin, du_a, dx1_a, dkn, dvm, grads["ssm_d"], grads["ssm_w_glu"], grads["ssm_b_glu"], grads["out_norm_mla"],
     grads["out_norm_ssm"], grads["w_o_mla"], grads["w_o_ssm"], grads["xattn_norm"], grads["xattn_w_q"],
     grads["xattn_q_norm"], grads["xattn_w_o"]) = seg_bwd(
        "post_mixer_bwd", f_post, g1, post_ins(w["ssm_d"] + sent), post_outs, [dx3])

    dkk, grads["xattn_k_norm"] = seg_bwd("mem_key_norm_bwd", f_knorm, (1,), knorm_ins, knorm_outs, [dkn])
    w_kv = wb["xattn_w_kv"]
    (dmn,) = mm("xattn_kv_bwd", [dkk, dvm], [[w_kv[:, :XH * XHD]], [w_kv[:, XH * XHD:]]], [F32], trans=True)
    gk_w, gv_w = mm_tn("xattn_kv_bwd_w", [mn], [dkk, dvm], [(0, [0]), (0, [1])])
    grads["xattn_w_kv"] = jnp.concatenate([gk_w, gv_w], axis=1)
    (grads["mem_norm"],) = seg_bwd("mem_norm_bwd", f_norm, (1,), mem_ins, mem_outs, [dmn])

    du, dbb_r, dbb_i, dcb_r, dncb_i, da_r8, da_i8 = ssm_bwd("ssm_bwd", dy_lin, du_a, pu, xs_r, xs_i, bb_r, bb_i, cb_r, ncb_i,
                                                            a_r8, a_i8)
    prep_g = seg_bwd("ssm_prep_bwd", _ssm_prep_f, prep_grid, prep_ins, prep_outs, [da_r8, da_i8, dbb_r, dbb_i, dcb_r, dncb_i])
    for k, gname in enumerate(("ssm_a_re", "ssm_a_im", "ssm_log_dt", "ssm_bt_re", "ssm_bt_im", "ssm_c2_re", "ssm_c2_im")):
        grads[gname] = prep_g[k]

    dqh, dkh, dv0 = attn_bwd("mla_attn_bwd", qh, kh, v0, do_mla, o_mla, lse, t=min(ATTN_TILE, S))

    (dx1, grads["mix_norm"], grads["w_in_q"], grads["w_in_kv"], grads["w_in_kr"], grads["w_in_u"], grads["mla_q_norm"],
     grads["mla_kv_norm"], grads["w_uq"], grads["w_ukv"], grads["qk_gain_q"], grads["qk_gain_k"]) = seg_bwd(
        "pre_mixer_bwd", f_pre, g1, pre_ins(w["mix_norm"]), pre_outs, [dqh, dkh, dv0, du], adds={0: dx1_a},
        row_block=FUSED_ROW_BLOCK)

    sent = on_grads("small", grads)
    dx, grads["ffn1_norm"], dg_1, du_1 = ffn_bwd_act("ffn1_bwd_act", dx1, x, w["ffn1_norm"] + sent, g_1, u_1,
                                                     wb["ffn1_w_gate"], wb["ffn1_w_up"], wb["ffn1_w_down"])
    grads["ffn1_w_gate"], grads["ffn1_w_up"], grads["ffn1_w_down"] = ffn_bwd_w("ffn1_bwd_w", h1, dx1, g_1, u_1, dg_1, du_1)
    return loss, dx, grads


def _pad_cols(a, n):
    return jnp.pad(a, ((0, 0), (0, n - a.shape[1])))


def _step_weights(shards):
    wb = {}
    w_in = _full_from_shards("w_in", shards["w_in"])
    wb["w_in_q"] = w_in[:, :MLA_Q_RANK]
    wb["w_in_kv"] = w_in[:, MLA_Q_RANK:MLA_Q_RANK + MLA_KV_RANK]
    wb["w_in_kr"] = _pad_cols(w_in[:, MLA_Q_RANK + MLA_KV_RANK:MLA_Q_RANK + MLA_KV_RANK + MLA_ROPE], LANES)
    wb["w_in_u"] = w_in[:, MLA_Q_RANK + MLA_KV_RANK + MLA_ROPE:]
    wb["w_uq"] = jnp.pad(shards["mla_w_uq"], ((0, 0), (0, 0), (0, HEAD_PAD - MLA_QK)))
    wb["w_ukv"] = shards["mla_w_ukv"]
    wb["ssm_w_glu"] = _full_from_shards("ssm_w_glu", shards["ssm_w_glu"])
    w_o = _full_from_shards("w_o", shards["w_o"])
    wb["w_o_mla"], wb["w_o_ssm"] = w_o[:SSM_WIDTH], w_o[SSM_WIDTH:]
    w_xq = _full_from_shards("xattn_w_q", shards["xattn_w_q"])
    wb["xattn_w_q"] = jnp.transpose(w_xq.reshape(D_MODEL, XH, XHD), (1, 0, 2))
    wb["xattn_w_kv"] = _full_from_shards("xattn_w_kv", shards["xattn_w_kv"])
    wb["xattn_w_o"] = _full_from_shards("xattn_w_o", shards["xattn_w_o"])
    return wb


def _sharded_grads(g):
    out = {}
    kr = g["w_in_kr"][:, :MLA_ROPE]
    out["w_in"] = _shards_from_full("w_in", jnp.concatenate([g["w_in_q"], g["w_in_kv"], kr, g["w_in_u"]], axis=1))
    out["mla_w_uq"] = g["w_uq"][:, :, :MLA_QK]
    out["mla_w_ukv"] = g["w_ukv"]
    out["ssm_w_glu"] = _shards_from_full("ssm_w_glu", g["ssm_w_glu"])
    out["w_o"] = _shards_from_full("w_o", jnp.concatenate([g["w_o_mla"], g["w_o_ssm"]], axis=0))
    w_xq = jnp.transpose(g["xattn_w_q"], (1, 0, 2)).reshape(D_MODEL, XH * XHD)
    out["xattn_w_q"] = _shards_from_full("xattn_w_q", w_xq)
    out["xattn_w_kv"] = _shards_from_full("xattn_w_kv", g["xattn_w_kv"])
    out["xattn_w_o"] = _shards_from_full("xattn_w_o", g["xattn_w_o"])
    return out


def _problem_repl_grads(g):
    out = {}
    out["mla_qk_norm_q"] = g["qk_gain_q"][:, :MLA_QK]
    out["mla_qk_norm_k"] = g["qk_gain_k"][:, :MLA_QK]
    out["ssm_b_re"] = jnp.transpose(g["ssm_bt_re"].reshape(SSM_GROUPS, SSM_GROUP, SSM_STATE), (0, 2, 1))
    out["ssm_b_im"] = jnp.transpose(g["ssm_bt_im"].reshape(SSM_GROUPS, SSM_GROUP, SSM_STATE), (0, 2, 1))
    out["ssm_c_re"] = g["ssm_c2_re"].reshape(SSM_GROUPS, SSM_GROUP, SSM_STATE)
    out["ssm_c_im"] = g["ssm_c2_im"].reshape(SSM_GROUPS, SSM_GROUP, SSM_STATE)
    for k in SMALL_REPL:
        if k not in out:
            out[k] = g[k]
    return out


def _problem_grads(g):
    out = {k: _full_from_shards(k, v) for k, v in _sharded_grads(g).items()}
    out.update(_problem_repl_grads(g))
    out.update({k: g[k] for k in FFN_NAMES})
    return out


def _step_params(p):
    row = lambda a: a.reshape(1, -1)
    w = {k: row(p[k]) for k in ("ffn1_norm", "mix_norm", "mla_q_norm", "mla_kv_norm", "ssm_b_glu", "out_norm_mla",
                                "out_norm_ssm", "xattn_norm", "mem_norm", "xattn_q_norm", "xattn_k_norm", "ffn2_norm")}
    w["qk_gain_q"] = _pad_cols(row(p["mla_qk_norm_q"]), HEAD_PAD)
    w["qk_gain_k"] = _pad_cols(row(p["mla_qk_norm_k"]), HEAD_PAD)
    w["ssm_a_re"], w["ssm_a_im"] = p["ssm_a_re"], p["ssm_a_im"]
    w["ssm_log_dt"] = p["ssm_log_dt"].reshape(SSM_GROUPS, 1)
    w["ssm_bt_re"] = jnp.transpose(p["ssm_b_re"], (0, 2, 1)).reshape(SSM_WIDTH, SSM_STATE)
    w["ssm_bt_im"] = jnp.transpose(p["ssm_b_im"], (0, 2, 1)).reshape(SSM_WIDTH, SSM_STATE)
    w["ssm_c2_re"] = p["ssm_c_re"].reshape(SSM_WIDTH, SSM_STATE)
    w["ssm_c2_im"] = p["ssm_c_im"].reshape(SSM_WIDTH, SSM_STATE)
    w["ssm_d"] = p["ssm_d"].reshape(1, SSM_WIDTH)
    return w


ARG_NAMES = ['x', 'mem', 'positions', 'ffn1_norm', 'ffn1_w_gate', 'ffn1_w_up', 'ffn1_w_down', 'mix_norm', 'w_in', 'mla_q_norm', 'mla_w_uq', 'mla_kv_norm', 'mla_w_ukv', 'mla_qk_norm_q', 'mla_qk_norm_k', 'ssm_a_re', 'ssm_a_im', 'ssm_log_dt', 'ssm_b_re', 'ssm_b_im', 'ssm_c_re', 'ssm_c_im', 'ssm_d', 'ssm_w_glu', 'ssm_b_glu', 'out_norm_mla', 'out_norm_ssm', 'w_o', 'xattn_norm', 'mem_norm', 'xattn_w_q', 'xattn_w_kv', 'xattn_q_norm', 'xattn_k_norm', 'xattn_w_o', 'ffn2_norm', 'ffn2_w_gate', 'ffn2_w_up', 'ffn2_w_down']
WEIGHT_NAMES = ARG_NAMES[3:]


def _gather_weights(p, c):
    half = lambda a: lax.dynamic_slice_in_dim(a, c * (a.shape[0] // 2), a.shape[0] // 2, axis=0)
    ffn1 = [half(p[k].astype(BF16)) for k in FFN_NAMES[:3]]
    small = [half(_pack_shards({k: p[k].astype(BF16) for k in SMALL_SHARDED}))]
    ffn2 = [half(p[k].astype(BF16)) for k in FFN_NAMES[3:]]
    me = 4 * lax.axis_index("x") + 2 * lax.axis_index("y") + c
    own = lambda got, blocks: [lax.dynamic_update_index_in_dim(g, b, me, 0) for g, b in zip(got, blocks)]
    as_shards = lambda a: a.reshape(N_CHIPS, 2 * a.shape[1], a.shape[2])
    landing = lambda blocks: [jax.ShapeDtypeStruct((N_DEV,) + b.shape, b.dtype) for b in blocks]
    got1 = own(all_gather_halves("all_gather_weights_a", ffn1), ffn1)
    got1, small = lax.optimization_barrier((got1, small))
    flight_s, sent_s = split_copy_start("gather_small_start", _gather_copies, N_DEV - 1, small, landing(small))
    sent_s, ffn2 = lax.optimization_barrier((sent_s, ffn2))
    flight_2, sent_2 = split_copy_start("gather_ffn2_start", _gather_copies, N_DEV - 1, ffn2, landing(ffn2))
    wb = {k: as_shards(a) for k, a in zip(FFN_NAMES[:3], got1)}

    def small_weights(after):
        mine, got = split_copy_wait("gather_small_wait", flight_s, after)
        return _step_weights(_unpack_shards(own(got, mine)[0].reshape(N_CHIPS, -1, LANES)))

    def ffn2_weights(after):
        mine, got = split_copy_wait("gather_ffn2_wait", flight_2, after)
        return [as_shards(a) for a in own(got, mine)]

    return wb, small_weights, ffn2_weights, sent_s + sent_2


class _GradReduce:
    def __init__(self, c):
        self.c, self.c_arr = c, jnp.reshape(c, (1,)).astype(jnp.int32)
        self.chip = 2 * lax.axis_index("x") + lax.axis_index("y")
        self.flights = []

    def start(self, tag, arrs):
        split = [a.reshape(N_CHIPS, 2, a.shape[1] // 2, a.shape[2]) for a in arrs]
        return self._scatter(tag, split, pair_send_halves(f"grad_pair_send_{tag}", split))

    def send(self, tag, arrs):
        split = [a.reshape(N_CHIPS, 2, a.shape[1] // 2, a.shape[2]) for a in arrs]
        lands = [jax.ShapeDtypeStruct((s.shape[0],) + s.shape[2:], s.dtype) for s in split]
        flight, sent = split_copy_start(f"grad_pair_send_start_{tag}", _pair_copies, 1, split, lands)
        self.sending = (tag, flight)
        return sent

    def scatter(self, tag, after):
        sent_tag, flight = self.sending
        assert sent_tag == tag
        return self._scatter(tag, *split_copy_wait(f"grad_pair_send_wait_{tag}", flight, after))

    def _scatter(self, tag, split, got):
        parts = [pair_add(f"grad_pair_add_{tag}_{k}", s, g, self.c_arr) for k, (s, g) in enumerate(zip(split, got))]
        flight, sent = split_copy_start(f"grad_scatter_start_{tag}", _scatter_copies, N_CHIPS - 1, parts, parts)
        self.flights.append((tag, flight))
        return sent

    def finish(self, after):
        halves = []
        for tag, flight in self.flights:
            parts, landed = split_copy_wait(f"grad_scatter_wait_{tag}", flight, after)
            for k, (q, p) in enumerate(zip(landed, parts)):
                mine = lax.dynamic_index_in_dim(p, self.chip, 0, keepdims=False)
                halves.append(sum_chips(f"grad_sum_{tag}_{k}", lax.dynamic_update_index_in_dim(q, mine, self.chip, 0)))
        tags = "_".join(t for t, _ in self.flights)
        self.flights = []
        theirs = pair_exchange(f"grad_pair_exchange_{tags}", halves)
        return [jnp.where(self.c == 0, jnp.concatenate([h, t], axis=0), jnp.concatenate([t, h], axis=0))
                for h, t in zip(halves, theirs)]


def kernel(x, mem, positions, ffn1_norm, ffn1_w_gate, ffn1_w_up, ffn1_w_down, mix_norm, w_in, mla_q_norm, mla_w_uq, mla_kv_norm, mla_w_ukv, mla_qk_norm_q, mla_qk_norm_k, ssm_a_re, ssm_a_im, ssm_log_dt, ssm_b_re, ssm_b_im, ssm_c_re, ssm_c_im, ssm_d, ssm_w_glu, ssm_b_glu, out_norm_mla, out_norm_ssm, w_o, xattn_norm, mem_norm, xattn_w_q, xattn_w_kv, xattn_q_norm, xattn_k_norm, xattn_w_o, ffn2_norm, ffn2_w_gate, ffn2_w_up, ffn2_w_down, loss_target, m_ffn1_norm, m_ffn1_w_gate, m_ffn1_w_up, m_ffn1_w_down, m_mix_norm, m_w_in, m_mla_q_norm, m_mla_w_uq, m_mla_kv_norm, m_mla_w_ukv, m_mla_qk_norm_q, m_mla_qk_norm_k, m_ssm_a_re, m_ssm_a_im, m_ssm_log_dt, m_ssm_b_re, m_ssm_b_im, m_ssm_c_re, m_ssm_c_im, m_ssm_d, m_ssm_w_glu, m_ssm_b_glu, m_out_norm_mla, m_out_norm_ssm, m_w_o, m_xattn_norm, m_mem_norm, m_xattn_w_q, m_xattn_w_kv, m_xattn_q_norm, m_xattn_k_norm, m_xattn_w_o, m_ffn2_norm, m_ffn2_w_gate, m_ffn2_w_up, m_ffn2_w_down, v_ffn1_norm, v_ffn1_w_gate, v_ffn1_w_up, v_ffn1_w_down, v_mix_norm, v_w_in, v_mla_q_norm, v_mla_w_uq, v_mla_kv_norm, v_mla_w_ukv, v_mla_qk_norm_q, v_mla_qk_norm_k, v_ssm_a_re, v_ssm_a_im, v_ssm_log_dt, v_ssm_b_re, v_ssm_b_im, v_ssm_c_re, v_ssm_c_im, v_ssm_d, v_ssm_w_glu, v_ssm_b_glu, v_out_norm_mla, v_out_norm_ssm, v_w_o, v_xattn_norm, v_mem_norm, v_xattn_w_q, v_xattn_w_kv, v_xattn_q_norm, v_xattn_k_norm, v_xattn_w_o, v_ffn2_norm, v_ffn2_w_gate, v_ffn2_w_up, v_ffn2_w_down):
    args = dict(locals())
    c = lax.axis_index("c")
    view = lambda k, a: jnp.swapaxes(a, 0, 1) if k in TRANSPOSED_VIEW else a
    p = {k: view(k, args[k][0]) for k in WEIGHT_NAMES}
    mom = {k: view(k, args["m_" + k][0]) for k in WEIGHT_NAMES}
    var = {k: view(k, args["v_" + k][0]) for k in WEIGHT_NAMES}
    natural = {k: view(k, p[k]) for k in WEIGHT_NAMES}

    wb, small_weights, ffn2_weights, sent = _gather_weights(
        {k: (p[k] if k in FFN_NAMES else natural[k]) for k in WEIGHT_NAMES}, c)
    w = _step_params(natural)
    w["ffn1_norm"] = w["ffn1_norm"] + sent
    early, late = _GradReduce(c), _GradReduce(c)

    def on_grads(tag, g):
        if tag == "ffn2":
            return early.send(tag, [g[k] for k in FFN_NAMES[3:]])
        packed = _pack_shards(_sharded_grads(g))
        return early.scatter("ffn2", packed[0, :SUBLANES]) + early.start(tag, [packed])

    loss, dx, g = local_step(x[0], mem[0], positions[0], loss_target[0], w, wb, small_weights, ffn2_weights, on_grads)

    sent = late.send("ffn1", [g[k] for k in FFN_NAMES[:3]])
    shards = early.finish(dx[:SUBLANES, :LANES] + sent)
    grad = dict(zip(FFN_NAMES[3:], shards[:3]))
    small_sharded = _unpack_shards(shards[3])
    grad.update({k: view(k, small_sharded[k]) for k in SMALL_SHARDED})
    reduced = all_reduce_small("grad_all_reduce_small", _pack_repl(_problem_repl_grads(g), loss + sent))
    grad.update(_unpack_repl(reduced))
    loss = grad.pop("loss")
    started = jnp.reshape(late.scatter("ffn1", reduced[:SUBLANES]), (1, 1))

    delta, new_m, new_v = {}, {}, {}
    small = [k for k in WEIGHT_NAMES if k not in FFN_NAMES and k not in SMALL_SHARDED]
    as2d = lambda a: a.reshape(-1, a.shape[-1])

    def update(k, after):
        delta[k], new_m[k], new_v[k] = adamw_big("adamw_" + k, as2d(p[k]), as2d(grad[k]), as2d(mom[k]), as2d(var[k]), after)

    last = started
    for k in WEIGHT_NAMES:
        if k not in small and k not in FFN_NAMES[:3]:
            update(k, last)
            last = delta[k]
    ds, nms, nvs = adamw_small("adamw_small", [as2d(p[k]) for k in small], [as2d(grad[k].reshape(p[k].shape)) for k in small],
                               [as2d(mom[k]) for k in small], [as2d(var[k]) for k in small], last)
    for k, d, nm, nv in zip(small, ds, nms, nvs):
        delta[k], new_m[k], new_v[k] = d, nm, nv
    last = ds[0]

    grad.update(zip(FFN_NAMES[:3], late.finish(last)))
    for k in FFN_NAMES[:3]:
        update(k, last)

    shaped = lambda d, k: view(k, d.reshape(p[k].shape)).reshape(args[k].shape)
    return (loss, dx[None], *[shaped(grad[k], k) for k in WEIGHT_NAMES], *[shaped(delta[k], k) for k in WEIGHT_NAMES],
            *[shaped(new_m[k], k) for k in WEIGHT_NAMES], *[shaped(new_v[k], k) for k in WEIGHT_NAMES])
```

```python
import functools
import math

import jax
import jax.numpy as jnp
import numpy as np
from jax import lax
from jax.experimental import pallas as pl
from jax.experimental.pallas import tpu as pltpu

F32, BF16 = jnp.float32, jnp.bfloat16
EPS = 1e-6
MESH = pl.DeviceIdType.MESH

D_MODEL, D_FF = 1024, 2752
MLA_HEADS, MLA_Q_RANK, MLA_KV_RANK, MLA_NOPE, MLA_ROPE, MLA_V = 4, 384, 256, 128, 64, 128
MLA_QK = MLA_NOPE + MLA_ROPE
HEAD_PAD = 256
SSM_WIDTH, SSM_GROUP, SSM_GROUPS, SSM_STATE = 512, 16, 32, 64
SSM_LANES = SSM_GROUPS * SSM_STATE
XH, XHD = 4, 128
ROPE_THETA = 10000.0
ADAM_LR, ADAM_B1, ADAM_B2, ADAM_EPS, ADAM_WD, ADAM_STEP = 0.001, 0.9, 0.999, 1e-08, 0.01, 10
N_CHIPS, N_DEV = 4, 8

VMEM_LIMIT_BYTES = 56 * 2**20
TOKEN_TILE = 512
FUSED_TILE = 256
FUSED_ROW_BLOCK = 128
FFN_SHARDS_PER_STEP = 2
FFN_ROW_BLOCK = 256
FFN_WIDE_TILE = 1024
ATTN_TILE = 1024
STAT_LANES = 128
SCAN_TIME_TILE = 1024
SCAN_LANE_TILE = 512
SUBLANES = 8


def _params(n_axes):
    return pltpu.CompilerParams(dimension_semantics=("arbitrary",) * n_axes, vmem_limit_bytes=VMEM_LIMIT_BYTES)


def _first(axes):
    cond = None
    for a in axes:
        c = pl.program_id(a) == 0
        cond = c if cond is None else jnp.logical_and(cond, c)
    return cond


def mm(name, xs, ws, out_dtypes, *, trans=False, adds=None, tm=TOKEN_TILE):
    rows = xs[0].shape[0]
    tm = min(tm, rows)
    n_in, n_out = len(xs), len(out_dtypes)
    pairs = [(i, j) for i in range(n_in) for j in range(n_out) if ws[i][j] is not None]
    w_list = [ws[i][j] for (i, j) in pairs]
    adds = list(adds) if adds is not None else [None] * n_out
    add_list = [a for a in adds if a is not None]
    out_cols = [None] * n_out
    for (i, j), w in zip(pairs, w_list):
        out_cols[j] = w.shape[0] if trans else w.shape[1]
    contract = (((1,), (1 if trans else 0,)), ((), ()))

    def body(*refs):
        x_refs = refs[:n_in]
        w_refs = refs[n_in:n_in + len(pairs)]
        a_refs = list(refs[n_in + len(pairs):n_in + len(pairs) + len(add_list)])
        o_refs = refs[n_in + len(pairs) + len(add_list):]
        xb = [None] * n_in
        for j in range(n_out):
            acc = None
            for p, (i, jj) in enumerate(pairs):
                if jj != j:
                    continue
                if xb[i] is None:
                    xb[i] = x_refs[i][...].astype(BF16)
                d = lax.dot_general(xb[i], w_refs[p][...].astype(BF16), contract, preferred_element_type=F32)
                acc = d if acc is None else acc + d
            if adds[j] is not None:
                acc = acc + a_refs.pop(0)[...].astype(F32)
            o_refs[j][...] = acc.astype(o_refs[j].dtype)

    in_specs = ([pl.BlockSpec((tm, x.shape[1]), lambda i: (i, 0)) for x in xs]
                + [pl.BlockSpec(w.shape, lambda i: (0, 0)) for w in w_list]
                + [pl.BlockSpec((tm, a.shape[1]), lambda i: (i, 0)) for a in add_list])
    outs = pl.pallas_call(
        body, name=name, grid=(rows // tm,), in_specs=in_specs,
        out_specs=[pl.BlockSpec((tm, n), lambda i: (i, 0)) for n in out_cols],
        out_shape=[jax.ShapeDtypeStruct((rows, n), dt) for n, dt in zip(out_cols, out_dtypes)],
        compiler_params=_params(1),
    )(*xs, *w_list, *add_list)
    return list(outs)


def mm_tn(name, xs, dys, pairs, *, tm=TOKEN_TILE):
    rows = xs[0].shape[0]
    tm = min(tm, rows)
    n_x, n_dy = len(xs), len(dys)
    contract = (((0,), (0,)), ((), ()))

    def body(*refs):
        x_refs, dy_refs, o_refs = refs[:n_x], refs[n_x:n_x + n_dy], refs[n_x + n_dy:]
        @pl.when(pl.program_id(0) == 0)
        def _():
            for o in o_refs:
                o[...] = jnp.zeros_like(o)

        for k, (i, js) in enumerate(pairs):
            dy = None
            for j in js:
                t = dy_refs[j][...].astype(F32)
                dy = t if dy is None else dy + t
            o_refs[k][...] += lax.dot_general(x_refs[i][...].astype(BF16), dy.astype(BF16), contract,
                                              preferred_element_type=F32)

    shapes = [(xs[i].shape[1], dys[js[0]].shape[1]) for (i, js) in pairs]
    outs = pl.pallas_call(
        body, name=name, grid=(rows // tm,),
        in_specs=[pl.BlockSpec((tm, a.shape[1]), lambda i: (i, 0)) for a in (*xs, *dys)],
        out_specs=[pl.BlockSpec(s, lambda i: (0, 0)) for s in shapes],
        out_shape=[jax.ShapeDtypeStruct(s, F32) for s in shapes],
        compiler_params=_params(1),
    )(*xs, *dys)
    return list(outs)


class In:
    def __init__(self, arr, block=None, imap=None, *, diff=False, acc=None, grad=None, weight=False, rows=False):
        self.arr, self.block, self.imap, self.diff, self.acc, self.grad = arr, block, imap, diff, acc, grad
        self.weight, self.rows = weight, rows

    def spec(self):
        return pl.BlockSpec(memory_space=pltpu.VMEM) if self.weight else pl.BlockSpec(self.block, self.imap)


class Out:
    def __init__(self, shape, dtype, block, imap):
        self.shape, self.dtype, self.block, self.imap = shape, dtype, block, imap

    def spec(self):
        return pl.BlockSpec(self.block, self.imap)


class Wt:
    def __init__(self, ref, zeros=None):
        self.ref, self.zeros = ref, zeros


@jax.custom_vjp
def _wdot(a, w, z):
    return jnp.dot(a.astype(BF16), w, preferred_element_type=F32)


def _wdot_fwd(a, w, z):
    return _wdot(a, w, z), (a, w)


def _wdot_bwd(res, g):
    a, w = res
    gb = g.astype(BF16)
    da = lax.dot_general(gb, w, (((1,), (1,)), ((), ())), preferred_element_type=F32)
    dz = lax.dot_general(a.astype(BF16), gb, (((0,), (0,)), ((), ())), preferred_element_type=F32)
    return da, None, dz


_wdot.defvjp(_wdot_fwd, _wdot_bwd)


def wdot(a, wt, head=None):
    w = wt.ref[...] if head is None else wt.ref[head]
    if wt.zeros is None:
        return jnp.dot(a.astype(BF16), w, preferred_element_type=F32)
    return _wdot(a, w, wt.zeros[0 if head is None else head])


def seg_fwd(name, f, grid, ins, outs):
    n_in = len(ins)

    def body(*refs):
        res = f(*[Wt(r) if i.weight else r[...] for i, r in zip(ins, refs[:n_in])])
        for o_ref, r in zip(refs[n_in:], res):
            o_ref[...] = r.astype(o_ref.dtype)

    res = pl.pallas_call(
        body, name=name, grid=grid, in_specs=[i.spec() for i in ins], out_specs=[o.spec() for o in outs],
        out_shape=[jax.ShapeDtypeStruct(o.shape, o.dtype) for o in outs], compiler_params=_params(len(grid)),
    )(*[i.arr for i in ins])
    return list(res)


def seg_bwd(name, f, grid, ins, outs, cts, adds=None, row_block=None):
    n_in, n_ct = len(ins), len(cts)
    grad_idx = [k for k, i in enumerate(ins) if i.diff or i.weight]
    adds = adds or {}
    add_keys = sorted(adds)
    add_list = [adds[k] for k in add_keys]
    heads = {k: (ins[k].arr.shape[0] if ins[k].arr.ndim == 3 else 1) for k in grad_idx if ins[k].weight}
    tile_rows = outs[0].block[0]
    blocks = [None] if row_block is None else [pl.ds(r0, row_block) for r0 in range(0, tile_rows, row_block)]

    def body(*refs):
        in_refs, ct_refs = refs[:n_in], refs[n_in:n_in + n_ct]
        add_refs = dict(zip(add_keys, refs[n_in + n_ct:n_in + n_ct + len(add_list)]))
        g_refs = dict(zip(grad_idx, refs[n_in + n_ct + len(add_list):]))
        for k in grad_idx:
            if ins[k].weight or ins[k].acc is not None:
                @pl.when(_first(range(len(grid)) if ins[k].weight else ins[k].acc))
                def _(k=k):
                    g_refs[k][...] = jnp.zeros_like(g_refs[k])

        for rows in blocks:
            at = lambda ref, sliced: ref[...] if rows is None or not sliced else ref[rows, :]
            vals = [None if i.weight else at(r, i.rows) for i, r in zip(ins, in_refs)]
            primals, owner = [], []
            for k in grad_idx:
                if ins[k].weight:
                    for h in range(heads[k]):
                        primals.append(jnp.zeros(ins[k].arr.shape[-2:], F32))
                        owner.append((k, h))
                else:
                    primals.append(vals[k].astype(F32))
                    owner.append((k, None))

            def g(*dv, vals=vals, owner=owner):
                full = list(vals)
                zeros = {}
                for (k, h), v in zip(owner, dv):
                    if h is None:
                        full[k] = v
                    else:
                        zeros.setdefault(k, []).append(v)
                for k, z in zeros.items():
                    full[k] = Wt(in_refs[k], z)
                return tuple(f(*full))

            _, pull = jax.vjp(g, *primals)
            grads = pull(tuple(at(c, True).astype(F32) for c in ct_refs))
            for (k, h), gr in zip(owner, grads):
                if ins[k].weight:
                    if ins[k].arr.ndim == 3:
                        g_refs[k][h] += gr
                    else:
                        g_refs[k][...] += gr
                    continue
                if k in add_refs:
                    gr = gr + at(add_refs[k], True).astype(F32)
                if ins[k].acc is not None:
                    g_refs[k][...] += gr
                elif rows is None or not ins[k].rows:
                    g_refs[k][...] = gr.astype(g_refs[k].dtype)
                else:
                    g_refs[k][rows, :] = gr.astype(g_refs[k].dtype)

    g_specs, g_shapes = [], []
    for k in grad_idx:
        i = ins[k]
        if i.weight:
            g_specs.append(pl.BlockSpec(memory_space=pltpu.VMEM))
            g_shapes.append(jax.ShapeDtypeStruct(i.arr.shape, F32))
            continue
        shape, block, imap = i.grad if i.grad is not None else (i.arr.shape, i.block, i.imap)
        g_specs.append(pl.BlockSpec(block, imap))
        g_shapes.append(jax.ShapeDtypeStruct(shape, F32))
    in_specs = ([i.spec() for i in ins] + [o.spec() for o in outs]
                + [pl.BlockSpec(ins[k].block, ins[k].imap) for k in add_keys])
    res = pl.pallas_call(
        body, name=name, grid=grid, in_specs=in_specs, out_specs=g_specs, out_shape=g_shapes,
        compiler_params=_params(len(grid)),
    )(*[i.arr for i in ins], *cts, *add_list)
    return list(res)


def _rms(x, g):
    return x * lax.rsqrt(jnp.mean(x * x, axis=-1, keepdims=True) + EPS) * g


@jax.custom_vjp
def _bdot_nt(a, b):
    return lax.dot_general(a.astype(BF16), b.astype(BF16), (((1,), (1,)), ((), ())), preferred_element_type=F32)


def _bdot_nt_fwd(a, b):
    return _bdot_nt(a, b), (a, b)


def _bdot_nt_bwd(res, g):
    a, b = res
    gb = g.astype(BF16)
    da = lax.dot_general(gb, b.astype(BF16), (((1,), (0,)), ((), ())), preferred_element_type=F32)
    db = lax.dot_general(gb, a.astype(BF16), (((0,), (0,)), ((), ())), preferred_element_type=F32)
    return da, db


_bdot_nt.defvjp(_bdot_nt_fwd, _bdot_nt_bwd)


@jax.custom_vjp
def _bdot_nn(a, b):
    return lax.dot_general(a.astype(BF16), b.astype(BF16), (((1,), (0,)), ((), ())), preferred_element_type=F32)


def _bdot_nn_fwd(a, b):
    return _bdot_nn(a, b), (a, b)


def _bdot_nn_bwd(res, g):
    a, b = res
    gb = g.astype(BF16)
    da = lax.dot_general(gb, b.astype(BF16), (((1,), (1,)), ((), ())), preferred_element_type=F32)
    db = lax.dot_general(a.astype(BF16), gb, (((0,), (0,)), ((), ())), preferred_element_type=F32)
    return da, db


_bdot_nn.defvjp(_bdot_nn_fwd, _bdot_nn_bwd)


@functools.partial(jax.custom_vjp, nondiff_argnums=(1,))
def _lane_roll(x, shift):
    return pltpu.roll(x, shift, 1)


def _lane_roll_fwd(x, shift):
    return pltpu.roll(x, shift, 1), None


def _lane_roll_bwd(shift, _, g):
    return (pltpu.roll(g, (g.shape[1] - shift) % g.shape[1], 1),)


_lane_roll.defvjp(_lane_roll_fwd, _lane_roll_bwd)


def _hp_dot(a, b):
    return jnp.dot(a, b, precision=lax.Precision.HIGHEST, preferred_element_type=F32)


NT_DIMS = (((1,), (1,)), ((), ()))
TN_DIMS = (((0,), (0,)), ((), ()))


def ffn_fwd(name, x, gain, wg, wu, wd, target=None, tm=TOKEN_TILE):
    S, D = x.shape
    tm = min(tm, S)
    nsh, Fs, _ = wg.shape
    per = FFN_SHARDS_PER_STEP
    with_loss = target is not None

    def body(*refs):
        if with_loss:
            x_ref, gain_ref, wg_ref, wu_ref, wd_ref, t_ref, xo_ref, h_ref, g_ref, u_ref, part_ref, acc = refs
        else:
            x_ref, gain_ref, wg_ref, wu_ref, wd_ref, xo_ref, h_ref, g_ref, u_ref, acc = refs
        j = pl.program_id(1)

        @pl.when(j == 0)
        def _():
            h_ref[...] = _rms(x_ref[...], gain_ref[...]).astype(BF16)
            acc[...] = jnp.zeros_like(acc)

        h = h_ref[...]
        for s in range(per):
            g = lax.dot_general(h, wg_ref[s], NT_DIMS, preferred_element_type=F32)
            u = lax.dot_general(h, wu_ref[s], NT_DIMS, preferred_element_type=F32)
            g_ref[s] = g.astype(BF16)
            u_ref[s] = u.astype(BF16)
            a = g * jax.nn.sigmoid(g) * u
            acc[...] += jnp.dot(a.astype(BF16), wd_ref[s], preferred_element_type=F32)

        @pl.when(j == nsh // per - 1)
        def _():
            y = x_ref[...] + 0.5 * acc[...]
            if with_loss:
                err = y - t_ref[...]
                xo_ref[...] = err * (1.0 / D)
                part_ref[...] = jnp.full(part_ref.shape, 0.5 * jnp.sum(jnp.mean(err * err, axis=-1)), F32)
            else:
                xo_ref[...] = y

    rows = pl.BlockSpec((tm, D), lambda i, j: (i, 0))
    wspec = pl.BlockSpec((per, Fs, D), lambda i, j: (j, 0, 0))
    act = pl.BlockSpec((per, tm, Fs), lambda i, j: (j, i, 0))
    in_specs, args = [rows, pl.BlockSpec((1, D), lambda i, j: (0, 0)), wspec, wspec, wspec], [x, gain, wg, wu, wd]
    out_specs = [rows, rows, act, act]
    out_shape = [jax.ShapeDtypeStruct((S, D), F32), jax.ShapeDtypeStruct((S, D), BF16),
                 jax.ShapeDtypeStruct((nsh, S, Fs), BF16), jax.ShapeDtypeStruct((nsh, S, Fs), BF16)]
    if with_loss:
        in_specs.append(rows)
        args.append(target)
        out_specs.append(pl.BlockSpec((SUBLANES, 128), lambda i, j: (i, 0)))
        out_shape.append(jax.ShapeDtypeStruct((S // tm * SUBLANES, 128), F32))
    return pl.pallas_call(
        body, name=name, grid=(S // tm, nsh // per), in_specs=in_specs, out_specs=out_specs, out_shape=out_shape,
        scratch_shapes=[pltpu.VMEM((tm, D), F32)], compiler_params=_params(2),
    )(*args)


def ffn_bwd_act(name, dxo, x, gain, g, u, wg, wu, wd, tm=TOKEN_TILE):
    S, D = x.shape
    tm = min(tm, S)
    nsh, Fs, _ = wg.shape

    def body(dxo_ref, x_ref, gain_ref, g_ref, u_ref, wg_ref, wu_ref, wd_ref, dx_ref, dgain_ref, dg_ref, du_ref, dd, dh):
        i, j = pl.program_id(0), pl.program_id(1)

        @pl.when(j == 0)
        def _():
            dd[...] = (0.5 * dxo_ref[...]).astype(BF16)
            dh[...] = jnp.zeros_like(dh)

        for s in range(FFN_SHARDS_PER_STEP):
            for r0 in range(0, tm, FFN_ROW_BLOCK):
                rows = pl.ds(r0, FFN_ROW_BLOCK)
                da = lax.dot_general(dd[rows, :], wd_ref[s], NT_DIMS, preferred_element_type=F32)
                gf, uf = g_ref[s, rows, :].astype(F32), u_ref[s, rows, :].astype(F32)
                sig = jax.nn.sigmoid(gf)
                dgv = (da * uf * (sig * (1.0 + gf * (1.0 - sig)))).astype(BF16)
                duv = (da * (gf * sig)).astype(BF16)
                dg_ref[s, rows, :] = dgv
                du_ref[s, rows, :] = duv
                dh[rows, :] += (jnp.dot(dgv, wg_ref[s], preferred_element_type=F32)
                                + jnp.dot(duv, wu_ref[s], preferred_element_type=F32))

        @pl.when(j == nsh // FFN_SHARDS_PER_STEP - 1)
        def _():
            xv = x_ref[...]
            r = lax.rsqrt(jnp.mean(xv * xv, axis=-1, keepdims=True) + EPS)
            xhat = xv * r
            dhv = dh[...]
            dxn = dhv * gain_ref[...]
            dx_ref[...] = dxo_ref[...] + r * (dxn - xhat * jnp.mean(dxn * xhat, axis=-1, keepdims=True))
            part = jnp.sum(dhv * xhat, axis=0, keepdims=True)

            @pl.when(i == 0)
            def _():
                dgain_ref[...] = part

            @pl.when(i != 0)
            def _():
                dgain_ref[...] += part

    per = FFN_SHARDS_PER_STEP
    act = pl.BlockSpec((per, tm, Fs), lambda i, j: (j, i, 0))
    wspec = pl.BlockSpec((per, Fs, D), lambda i, j: (j, 0, 0))
    return pl.pallas_call(
        body, name=name, grid=(S // tm, nsh // per),
        in_specs=[pl.BlockSpec((tm, D), lambda i, j: (i, 0)), pl.BlockSpec((tm, D), lambda i, j: (i, 0)),
                  pl.BlockSpec((1, D), lambda i, j: (0, 0)), act, act, wspec, wspec, wspec],
        out_specs=[pl.BlockSpec((tm, D), lambda i, j: (i, 0)), pl.BlockSpec((1, D), lambda i, j: (0, 0)), act, act],
        out_shape=[jax.ShapeDtypeStruct((S, D), F32), jax.ShapeDtypeStruct((1, D), F32),
                   jax.ShapeDtypeStruct((nsh, S, Fs), BF16), jax.ShapeDtypeStruct((nsh, S, Fs), BF16)],
        scratch_shapes=[pltpu.VMEM((tm, D), BF16), pltpu.VMEM((tm, D), F32)], compiler_params=_params(2),
    )(dxo, x, gain, g, u, wg, wu, wd)


def ffn_bwd_w(name, h, dxo, g, u, dg, du, tm=FFN_WIDE_TILE):
    S, D = h.shape
    tm = min(tm, S)
    nsh, _, Fs = g.shape

    def body(h_ref, dxo_ref, g_ref, u_ref, dg_ref, du_ref, dwg_ref, dwu_ref, dwd_ref):
        i = pl.program_id(1)
        gf, uf = g_ref[...].astype(F32), u_ref[...].astype(F32)
        a = (gf * jax.nn.sigmoid(gf) * uf).astype(BF16)
        dd = (0.5 * dxo_ref[...]).astype(BF16)
        hv = h_ref[...]

        @pl.when(i == 0)
        def _():
            dwg_ref[...] = jnp.zeros_like(dwg_ref)
            dwu_ref[...] = jnp.zeros_like(dwu_ref)
            dwd_ref[...] = jnp.zeros_like(dwd_ref)

        dwg_ref[...] += lax.dot_general(dg_ref[...], hv, TN_DIMS, preferred_element_type=F32)
        dwu_ref[...] += lax.dot_general(du_ref[...], hv, TN_DIMS, preferred_element_type=F32)
        dwd_ref[...] += lax.dot_general(a, dd, TN_DIMS, preferred_element_type=F32)

    act = pl.BlockSpec((None, tm, Fs), lambda j, i: (j, i, 0))
    wspec = pl.BlockSpec((None, Fs, D), lambda j, i: (j, 0, 0))
    return pl.pallas_call(
        body, name=name, grid=(nsh, S // tm),
        in_specs=[pl.BlockSpec((tm, D), lambda j, i: (i, 0)), pl.BlockSpec((tm, D), lambda j, i: (i, 0)), act, act, act, act],
        out_specs=[wspec, wspec, wspec], out_shape=[jax.ShapeDtypeStruct((nsh, Fs, D), F32)] * 3,
        compiler_params=_params(2),
    )(h, dxo, g, u, dg, du)


NEG_BIG = -1e30


def _causal_pairs(n, by_key):
    pairs = [(qi, ki) for qi in range(n) for ki in range(qi + 1)]
    if by_key:
        pairs.sort(key=lambda p: (p[1], p[0]))
    return jnp.asarray([p[0] for p in pairs], jnp.int32), jnp.asarray([p[1] for p in pairs], jnp.int32)


def _scores(q, k, masked):
    s = lax.dot_general(q, k, (((1,), (1,)), ((), ())), preferred_element_type=F32)
    if masked:
        row = lax.broadcasted_iota(jnp.int32, s.shape, 0)
        col = lax.broadcasted_iota(jnp.int32, s.shape, 1)
        s = jnp.where(row >= col, s, NEG_BIG)
    return s


def attn_fwd(name, q, k, v, t=ATTN_TILE):
    S, Dk = q.shape[0], HEAD_PAD
    H = q.shape[1] // Dk
    Dv = v.shape[1] // H
    qt, kt = _causal_pairs(S // t, by_key=False)

    def body(qt_ref, kt_ref, q_ref, k_ref, v_ref, o_ref, lse_ref, m_sc, l_sc, acc):
        qi, ki = qt_ref[pl.program_id(1)], kt_ref[pl.program_id(1)]

        @pl.when(ki == 0)
        def _():
            m_sc[...] = jnp.full_like(m_sc, NEG_BIG)
            l_sc[...] = jnp.zeros_like(l_sc)
            acc[...] = jnp.zeros_like(acc)

        def step(masked):
            s = _scores(q_ref[...], k_ref[...], masked)
            m_prev = m_sc[...]
            m_next = jnp.maximum(m_prev, jnp.max(s, axis=-1, keepdims=True))
            alpha = jnp.exp(m_prev - m_next)
            p = jnp.exp(s - jnp.tile(m_next, (1, t // STAT_LANES)))
            l_sc[...] = alpha * l_sc[...] + jnp.sum(p, axis=-1, keepdims=True)
            acc[...] = alpha * acc[...] + jnp.dot(p.astype(BF16), v_ref[...].astype(BF16), preferred_element_type=F32)
            m_sc[...] = m_next

        @pl.when(ki < qi)
        def _():
            step(False)

        @pl.when(ki == qi)
        def _():
            step(True)
            o_ref[...] = acc[...] / l_sc[...]
            lse_ref[...] = m_sc[...] + jnp.log(l_sc[...])

    stat = pltpu.VMEM((t, STAT_LANES), F32)
    return pl.pallas_call(
        body, name=name,
        grid_spec=pltpu.PrefetchScalarGridSpec(
            num_scalar_prefetch=2, grid=(H, qt.shape[0]),
            in_specs=[pl.BlockSpec((t, Dk), lambda h, s, qt, kt: (qt[s], h)),
                      pl.BlockSpec((t, Dk), lambda h, s, qt, kt: (kt[s], h)),
                      pl.BlockSpec((t, Dv), lambda h, s, qt, kt: (kt[s], h))],
            out_specs=[pl.BlockSpec((t, Dv), lambda h, s, qt, kt: (qt[s], h)),
                       pl.BlockSpec((None, t, STAT_LANES), lambda h, s, qt, kt: (h, qt[s], 0))],
            scratch_shapes=[stat, stat, pltpu.VMEM((t, Dv), F32)]),
        out_shape=[jax.ShapeDtypeStruct((S, H * Dv), F32), jax.ShapeDtypeStruct((H, S, STAT_LANES), F32)],
        compiler_params=_params(2),
    )(qt, kt, q, k, v)


def attn_bwd(name, q, k, v, do, o, lse, t=ATTN_TILE):
    S, Dk = q.shape[0], HEAD_PAD
    H = q.shape[1] // Dk
    Dv = v.shape[1] // H
    qt, kt = _causal_pairs(S // t, by_key=True)
    tn_dims = (((0,), (0,)), ((), ()))

    def body(qt_ref, kt_ref, q_ref, k_ref, v_ref, do_ref, o_ref, lse_ref, dq_ref, dk_ref, dv_ref):
        step_id = pl.program_id(1)
        qi, ki = qt_ref[step_id], kt_ref[step_id]

        @pl.when(step_id == 0)
        def _():
            dq_ref[...] = jnp.zeros_like(dq_ref)

        def step(masked):
            s = _scores(q_ref[...], k_ref[...], masked)
            reps = (1, t // STAT_LANES)
            p = jnp.exp(s - jnp.tile(lse_ref[...], reps))
            dov = do_ref[...]
            delta = jnp.broadcast_to(jnp.sum(dov * o_ref[...], axis=-1, keepdims=True), (t, STAT_LANES))
            dob = dov.astype(BF16)
            dp = lax.dot_general(dob, v_ref[...].astype(BF16), (((1,), (1,)), ((), ())), preferred_element_type=F32)
            ds = (p * (dp - jnp.tile(delta, reps))).astype(BF16)
            pdv = lax.dot_general(p.astype(BF16), dob, tn_dims, preferred_element_type=F32)
            pdk = lax.dot_general(ds, q_ref[...], tn_dims, preferred_element_type=F32)
            rows = pl.ds(pl.multiple_of(qi * t, t), t)
            dq_ref[rows, :] += jnp.dot(ds, k_ref[...], preferred_element_type=F32)
            return pdk, pdv

        @pl.when(ki == qi)
        def _():
            dk_ref[...] = jnp.zeros_like(dk_ref)
            dv_ref[...] = jnp.zeros_like(dv_ref)

        def accumulate(masked):
            pdk, pdv = step(masked)
            dk_ref[...] += pdk
            dv_ref[...] += pdv

        @pl.when(ki == qi)
        def _():
            accumulate(True)

        @pl.when(ki < qi)
        def _():
            accumulate(False)

    qrow = lambda h, s, qt, kt: (qt[s], h)
    krow = lambda h, s, qt, kt: (kt[s], h)
    return pl.pallas_call(
        body, name=name,
        grid_spec=pltpu.PrefetchScalarGridSpec(
            num_scalar_prefetch=2, grid=(H, qt.shape[0]),
            in_specs=[pl.BlockSpec((t, Dk), qrow), pl.BlockSpec((t, Dk), krow), pl.BlockSpec((t, Dv), krow),
                      pl.BlockSpec((t, Dv), qrow), pl.BlockSpec((t, Dv), qrow),
                      pl.BlockSpec((None, t, STAT_LANES), lambda h, s, qt, kt: (h, qt[s], 0))],
            out_specs=[pl.BlockSpec((S, Dk), lambda h, s, qt, kt: (0, h)), pl.BlockSpec((t, Dk), krow),
                       pl.BlockSpec((t, Dv), krow)]),
        out_shape=[jax.ShapeDtypeStruct((S, H * Dk), F32), jax.ShapeDtypeStruct((S, H * Dk), F32),
                   jax.ShapeDtypeStruct((S, H * Dv), F32)],
        compiler_params=_params(2),
    )(qt, kt, q, k, v, do, o, lse)


def _cmul(ar, ai, br, bi):
    return ar * br - ai * bi, ar * bi + ai * br


def _scan_tile(x_r, x_i, ar_ref, ai_ref, cr_sc, ci_sc, *, reverse, first, states=None):
    tc, lanes = x_r.shape
    nblk, lb = tc // SUBLANES, SCAN_LANE_TILE
    with_da = states is not None
    if with_da:
        xr_all, xi_all, pr_all, pi_all, dar_all, dai_all, chunk = states

    @pl.when(first)
    def _():
        cr_sc[...] = jnp.zeros_like(cr_sc)
        ci_sc[...] = jnp.zeros_like(ci_sc)
        if with_da:
            dar_all[...] = jnp.zeros_like(dar_all)
            dai_all[...] = jnp.zeros_like(dai_all)

    row = lax.broadcasted_iota(jnp.int32, (SUBLANES, lb), 0)
    for l0 in range(0, lanes, lb):
        _scan_lanes(x_r.at[:, pl.ds(l0, lb)], x_i.at[:, pl.ds(l0, lb)], ar_ref[0:1, pl.ds(l0, lb)],
                    ai_ref[0:1, pl.ds(l0, lb)], cr_sc.at[:, pl.ds(l0, lb)], ci_sc.at[:, pl.ds(l0, lb)], row, reverse,
                    nblk, None if not with_da else tuple(r.at[:, pl.ds(l0, lb)] for r in states[:6]) + (chunk,))


def _scan_lanes(x_r, x_i, a1r, a1i, cr_sc, ci_sc, row, reverse, nblk, states):
    lb = x_r.shape[1]
    with_da = states is not None
    if with_da:
        xr_ref, xi_ref, pr_ref, pi_ref, dar_ref, dai_ref, chunk = states
    if reverse:
        a1i = -a1i
    a2r, a2i = _cmul(a1r, a1i, a1r, a1i)
    a4r, a4i = _cmul(a2r, a2i, a2r, a2i)
    pw_r, pw_i = jnp.zeros((SUBLANES, lb), F32), jnp.zeros((SUBLANES, lb), F32)
    cur_r, cur_i = a1r, a1i
    for e in range(SUBLANES):
        r_at = (SUBLANES - 1 - e) if reverse else e
        pw_r = jnp.where(row == r_at, cur_r, pw_r)
        pw_i = jnp.where(row == r_at, cur_i, pw_i)
        cur_r, cur_i = _cmul(cur_r, cur_i, a1r, a1i)
    steps = []
    for d, pr, pi in ((1, a1r, a1i), (2, a2r, a2i), (4, a4r, a4i)):
        keep = (row < SUBLANES - d) if reverse else (row >= d)
        steps.append((d, jnp.where(keep, pr, 0.0), jnp.where(keep, pi, 0.0)))

    def block(jb, carry):
        if with_da:
            cr, ci, acc_r, acc_i = carry
        else:
            cr, ci = carry
        idx = (nblk - 1 - jb) if reverse else jb
        r0 = pl.multiple_of(idx * SUBLANES, SUBLANES)
        xr = x_r[pl.ds(r0, SUBLANES), :]
        xi = x_i[pl.ds(r0, SUBLANES), :]
        for d, pr, pi in steps:
            shift = SUBLANES - d if reverse else d
            sr, si = pltpu.roll(xr, shift, 0), pltpu.roll(xi, shift, 0)
            xr, xi = xr + pr * sr - pi * si, xi + pr * si + pi * sr
        xr, xi = xr + pw_r * cr - pw_i * ci, xi + pw_r * ci + pw_i * cr
        x_r[pl.ds(r0, SUBLANES), :] = xr
        x_i[pl.ds(r0, SUBLANES), :] = xi
        edge = 0 if reverse else SUBLANES - 1
        cr, ci = xr[edge:edge + 1, :], xi[edge:edge + 1, :]
        if not with_da:
            return cr, ci
        fr = xr_ref[pl.ds(r0, SUBLANES), :]
        fi = xi_ref[pl.ds(r0, SUBLANES), :]
        rp = pl.multiple_of(jnp.maximum(idx - 1, 0) * SUBLANES, SUBLANES)
        inside = idx > 0
        before_r = jnp.where(inside, xr_ref[pl.ds(rp, SUBLANES), :], pr_ref[...])
        before_i = jnp.where(inside, xi_ref[pl.ds(rp, SUBLANES), :], pi_ref[...])
        live = jnp.where(jnp.logical_or(inside, chunk > 0), 1.0, 0.0)
        last_r = before_r[SUBLANES - 1:SUBLANES, :] * live
        last_i = before_i[SUBLANES - 1:SUBLANES, :] * live
        pvr = jnp.where(row == 0, last_r, pltpu.roll(fr, 1, 0))
        pvi = jnp.where(row == 0, last_i, pltpu.roll(fi, 1, 0))
        acc_r = acc_r + xr * pvr + xi * pvi
        acc_i = acc_i + xi * pvr - xr * pvi
        return cr, ci, acc_r, acc_i

    init = (cr_sc[...], ci_sc[...])
    if with_da:
        init = init + (jnp.zeros((SUBLANES, lb), F32), jnp.zeros((SUBLANES, lb), F32))
    fin = lax.fori_loop(0, nblk, block, init)
    cr_sc[...] = fin[0]
    ci_sc[...] = fin[1]
    if with_da:
        dar_ref[...] += fin[2]
        dai_ref[...] += fin[3]


SSM_BLOCKS = 4
BLOCK_CH = SSM_WIDTH // SSM_BLOCKS
PREP_LANES = SSM_LANES // SSM_BLOCKS


def ssm_fwd(name, u, bb_r, bb_i, cb_r, ncb_i, a_r8, a_i8):
    S = u.shape[0]
    tc = min(SCAN_TIME_TILE, S)

    def body(u_ref, bbr_ref, bbi_ref, cbr_ref, ncbi_ref, ar_ref, ai_ref, xr_ref, xi_ref, y_ref, cr_sc, ci_sc):
        ub = u_ref[...].astype(BF16)
        xr_ref[...] = jnp.dot(ub, bbr_ref[...], preferred_element_type=F32)
        xi_ref[...] = jnp.dot(ub, bbi_ref[...], preferred_element_type=F32)
        _scan_tile(xr_ref, xi_ref, ar_ref, ai_ref, cr_sc, ci_sc, reverse=False, first=pl.program_id(1) == 0)
        y_ref[...] = (lax.dot_general(xr_ref[...].astype(BF16), cbr_ref[...], NT_DIMS, preferred_element_type=F32)
                      + lax.dot_general(xi_ref[...].astype(BF16), ncbi_ref[...], NT_DIMS, preferred_element_type=F32))

    ch = pl.BlockSpec((tc, BLOCK_CH), lambda b, t: (t, b))
    st = pl.BlockSpec((tc, PREP_LANES), lambda b, t: (t, b))
    wt = pl.BlockSpec((None, BLOCK_CH, PREP_LANES), lambda b, t: (b, 0, 0))
    par = pl.BlockSpec((SUBLANES, PREP_LANES), lambda b, t: (0, b))
    return pl.pallas_call(
        body, name=name, grid=(SSM_BLOCKS, S // tc), in_specs=[ch, wt, wt, wt, wt, par, par], out_specs=[st, st, ch],
        out_shape=[jax.ShapeDtypeStruct((S, SSM_LANES), F32), jax.ShapeDtypeStruct((S, SSM_LANES), F32),
                   jax.ShapeDtypeStruct((S, SSM_WIDTH), F32)],
        scratch_shapes=[pltpu.VMEM((1, PREP_LANES), F32), pltpu.VMEM((1, PREP_LANES), F32)], compiler_params=_params(2),
    )(u, bb_r, bb_i, cb_r, ncb_i, a_r8, a_i8)


def ssm_bwd(name, dy, du_add, u, xs_r, xs_i, bb_r, bb_i, cb_r, ncb_i, a_r8, a_i8):
    S = u.shape[0]
    tc = min(SCAN_TIME_TILE, S)
    nt = S // tc

    def body(dy_ref, dua_ref, u_ref, xr_ref, xi_ref, pr_ref, pi_ref, bbr_ref, bbi_ref, cbr_ref, ncbi_ref, ar_ref, ai_ref,
             du_ref, dbbr_ref, dbbi_ref, dcbr_ref, dncbi_ref, dar_ref, dai_ref, lr_sc, li_sc, cr_sc, ci_sc):
        t = pl.program_id(1)
        first = t == 0

        @pl.when(first)
        def _():
            for r in (dbbr_ref, dbbi_ref, dcbr_ref, dncbi_ref):
                r[...] = jnp.zeros_like(r)

        dyb = dy_ref[...].astype(BF16)
        lr_sc[...] = jnp.dot(dyb, cbr_ref[...], preferred_element_type=F32)
        li_sc[...] = jnp.dot(dyb, ncbi_ref[...], preferred_element_type=F32)
        _scan_tile(lr_sc, li_sc, ar_ref, ai_ref, cr_sc, ci_sc, reverse=True, first=first,
                   states=(xr_ref, xi_ref, pr_ref, pi_ref, dar_ref, dai_ref, nt - 1 - t))
        lrb, lib = lr_sc[...].astype(BF16), li_sc[...].astype(BF16)
        du_ref[...] = (lax.dot_general(lrb, bbr_ref[...], NT_DIMS, preferred_element_type=F32)
                       + lax.dot_general(lib, bbi_ref[...], NT_DIMS, preferred_element_type=F32) + dua_ref[...])
        ub = u_ref[...].astype(BF16)
        dbbr_ref[...] += lax.dot_general(ub, lrb, TN_DIMS, preferred_element_type=F32)
        dbbi_ref[...] += lax.dot_general(ub, lib, TN_DIMS, preferred_element_type=F32)
        dcbr_ref[...] += lax.dot_general(dyb, xr_ref[...].astype(BF16), TN_DIMS, preferred_element_type=F32)
        dncbi_ref[...] += lax.dot_general(dyb, xi_ref[...].astype(BF16), TN_DIMS, preferred_element_type=F32)

    ch = pl.BlockSpec((tc, BLOCK_CH), lambda b, t: (nt - 1 - t, b))
    st = pl.BlockSpec((tc, PREP_LANES), lambda b, t: (nt - 1 - t, b))
    prev = pl.BlockSpec((SUBLANES, PREP_LANES), lambda b, t: (jnp.maximum((nt - 1 - t) * (tc // SUBLANES) - 1, 0), b))
    wt = pl.BlockSpec((None, BLOCK_CH, PREP_LANES), lambda b, t: (b, 0, 0))
    par = pl.BlockSpec((SUBLANES, PREP_LANES), lambda b, t: (0, b))
    blk = jax.ShapeDtypeStruct((SSM_BLOCKS, BLOCK_CH, PREP_LANES), F32)
    rows8 = jax.ShapeDtypeStruct((SUBLANES, SSM_LANES), F32)
    return pl.pallas_call(
        body, name=name, grid=(SSM_BLOCKS, nt), in_specs=[ch, ch, ch, st, st, prev, prev, wt, wt, wt, wt, par, par],
        out_specs=[ch, wt, wt, wt, wt, par, par],
        out_shape=[jax.ShapeDtypeStruct((S, SSM_WIDTH), F32), blk, blk, blk, blk, rows8, rows8],
        scratch_shapes=[pltpu.VMEM((tc, PREP_LANES), F32), pltpu.VMEM((tc, PREP_LANES), F32),
                        pltpu.VMEM((1, PREP_LANES), F32), pltpu.VMEM((1, PREP_LANES), F32)],
        compiler_params=_params(2),
    )(dy, du_add, u, xs_r, xs_i, xs_r, xs_i, bb_r, bb_i, cb_r, ncb_i, a_r8, a_i8)


def _ssm_prep_f(a_re, a_im, log_dt, bt_re, bt_im, c_re, c_im):
    first_group = pl.program_id(0) * (SSM_GROUPS // SSM_BLOCKS)
    iota = lambda shape, d: lax.broadcasted_iota(jnp.int32, shape, d)
    grp_of_row = lambda shape: iota(shape, 0) >> int(math.log2(SSM_GROUP))
    grp_of_lane = lambda shape: iota(shape, 1) >> int(math.log2(SSM_STATE))
    rep = (grp_of_row((BLOCK_CH, SSM_GROUPS)) + first_group == iota((BLOCK_CH, SSM_GROUPS), 1)).astype(F32)
    til = ((iota((SSM_STATE, PREP_LANES), 1) & (SSM_STATE - 1)) == iota((SSM_STATE, PREP_LANES), 0)).astype(F32)
    m_rows = (grp_of_row((BLOCK_CH, PREP_LANES)) == grp_of_lane((BLOCK_CH, PREP_LANES))).astype(F32)
    m_grp = (iota((SSM_GROUPS, PREP_LANES), 0) == grp_of_lane((SSM_GROUPS, PREP_LANES)) + first_group).astype(F32)
    dt = jnp.exp(log_dt)
    decay = jnp.exp(a_re * dt)
    ar = decay * jnp.cos(a_im * dt)
    ai = decay * jnp.sin(a_im * dt)
    den = a_re * a_re + a_im * a_im
    nr = ar - 1.0
    coef_r = (nr * a_re + ai * a_im) / den
    coef_i = (ai * a_re - nr * a_im) / den
    cr, ci = _hp_dot(rep, coef_r), _hp_dot(rep, coef_i)
    bb_r = cr * bt_re - ci * bt_im
    bb_i = cr * bt_im + ci * bt_re
    big = lambda m: _hp_dot(m, til) * m_rows
    lanes = lambda m: jnp.broadcast_to(jnp.sum(_hp_dot(m, til) * m_grp, axis=0, keepdims=True), (SUBLANES, PREP_LANES))
    return lanes(ar), lanes(ai), big(bb_r), big(bb_i), big(c_re), -big(c_im)


def _whole(arr, **kw):
    nd = arr.ndim
    return In(arr, arr.shape, lambda *_: (0,) * nd, **kw)


def _adamw_math(w, g, m, v):
    m = ADAM_B1 * m + (1.0 - ADAM_B1) * g
    v = ADAM_B2 * v + (1.0 - ADAM_B2) * (g * g)
    m_hat = m / (1.0 - ADAM_B1 ** ADAM_STEP)
    v_hat = v / (1.0 - ADAM_B2 ** ADAM_STEP)
    delta = -ADAM_LR * (m_hat / (jnp.sqrt(v_hat) + ADAM_EPS) + ADAM_WD * w)
    return delta, m, v


def adamw_big(name, w, g, m, v, after):
    R, C = w.shape
    tr = R
    for cand in (512, 344, 256, 128):
        if R % cand == 0:
            tr = cand
            break

    def body(w_ref, g_ref, m_ref, v_ref, after_ref, d_ref, nm_ref, nv_ref):
        d, nm, nv = _adamw_math(w_ref[...], g_ref[...], m_ref[...], v_ref[...])
        d_ref[...] = d
        nm_ref[...] = nm
        nv_ref[...] = nv

    spec = pl.BlockSpec((tr, C), lambda i: (i, 0))
    return pl.pallas_call(
        body, name=name, grid=(R // tr,), in_specs=[spec] * 4 + [pl.BlockSpec(memory_space=pl.ANY)], out_specs=[spec] * 3,
        out_shape=[jax.ShapeDtypeStruct((R, C), F32)] * 3, compiler_params=_params(1),
    )(w, g, m, v, after)


def adamw_small(name, ws, gs, ms, vs, after):
    n = len(ws)

    def body(*refs):
        outs = refs[4 * n + 1:]
        for k in range(n):
            d, nm, nv = _adamw_math(refs[k][...], refs[n + k][...], refs[2 * n + k][...], refs[3 * n + k][...])
            outs[k][...] = d
            outs[n + k][...] = nm
            outs[2 * n + k][...] = nv

    vm = pl.BlockSpec(memory_space=pltpu.VMEM)
    shapes = [jax.ShapeDtypeStruct(w.shape, F32) for w in ws]
    res = pl.pallas_call(
        body, name=name, in_specs=[vm] * (4 * n) + [pl.BlockSpec(memory_space=pl.ANY)], out_specs=[vm] * (3 * n),
        out_shape=shapes * 3, compiler_params=pltpu.CompilerParams(vmem_limit_bytes=VMEM_LIMIT_BYTES),
    )(*ws, *gs, *ms, *vs, after)
    return res[:n], res[n:2 * n], res[2 * n:]


def _place():
    return lax.axis_index("x"), lax.axis_index("y"), lax.axis_index("c")


def _other_chips(x, y):
    return [(1 - x, y), (x, 1 - y), (1 - x, 1 - y)]


HBM = pl.BlockSpec(memory_space=pl.ANY)


def all_gather_halves(name, blocks):
    n = len(blocks)

    def body(*refs):
        in_refs, out_refs = refs[:n], refs[n:2 * n]
        send_sems, recv_sems = refs[2 * n:]
        x, y, c = _place()
        me, sibling = (x, y, c), (x, y, 1 - c)
        chips = _other_chips(x, y)

        def slot(a, px, py, pc):
            return out_refs[a].at[4 * px + 2 * py + pc]

        def copy(a, k, block, to, src=None):
            return pltpu.make_async_remote_copy(
                src_ref=slot(a, *block) if src is None else src, dst_ref=slot(a, *block),
                send_sem=send_sems.at[a, k], recv_sem=recv_sems.at[a, k], device_id=to, device_id_type=MESH)

        first = []
        for a in range(n):
            first.append(copy(a, 0, me, sibling, src=in_refs[a]))
            first += [copy(a, 1 + j, me, (*chip, c), src=in_refs[a]) for j, chip in enumerate(chips)]
        for cp in first:
            cp.start()
        passed = []
        for j, chip in enumerate(chips):
            for a in range(n):
                copy(a, 1 + j, (*chip, c), me).wait_recv()
                fw = copy(a, 4 + j, (*chip, c), sibling)
                fw.start()
                passed.append(fw)
        for a in range(n):
            copy(a, 0, sibling, me).wait_recv()
            for j, chip in enumerate(chips):
                copy(a, 4 + j, (*chip, 1 - c), me).wait_recv()
        for cp in first + passed:
            cp.wait_send()

    res = pl.pallas_call(
        body, name=name, in_specs=[HBM] * n, out_specs=[HBM] * n,
        out_shape=[jax.ShapeDtypeStruct((N_DEV,) + b.shape, b.dtype) for b in blocks],
        scratch_shapes=[pltpu.SemaphoreType.DMA((n, 7)), pltpu.SemaphoreType.DMA((n, 7))],
    )(*blocks)
    return list(res)


def pair_send_halves(name, grads):
    n = len(grads)

    def body(*refs):
        in_refs, out_refs = refs[:n], refs[n:2 * n]
        send_sems, recv_sems = refs[2 * n:]
        x, y, c = _place()
        cps = []
        for a in range(n):
            cp = pltpu.make_async_remote_copy(
                src_ref=in_refs[a].at[:, 1 - c], dst_ref=out_refs[a], send_sem=send_sems.at[a], recv_sem=recv_sems.at[a],
                device_id=(x, y, 1 - c), device_id_type=MESH)
            cp.start()
            cps.append(cp)
        for cp in cps:
            cp.wait()

    res = pl.pallas_call(
        body, name=name, in_specs=[HBM] * n, out_specs=[HBM] * n,
        out_shape=[jax.ShapeDtypeStruct((g.shape[0],) + g.shape[2:], g.dtype) for g in grads],
        scratch_shapes=[pltpu.SemaphoreType.DMA((n,)), pltpu.SemaphoreType.DMA((n,))],
    )(*grads)
    return list(res)


def pair_add(name, grad, got, c_arr):
    nsh, _, M, N = grad.shape
    tr = M
    for cand in (512, 256, 192, 128, 64, 16):
        if M % cand == 0:
            tr = cand
            break

    def body(c_ref, g_ref, p_ref, o_ref):
        o_ref[...] = (g_ref[...] + p_ref[...]).astype(BF16)

    return pl.pallas_call(
        body, name=name,
        grid_spec=pltpu.PrefetchScalarGridSpec(
            num_scalar_prefetch=1, grid=(nsh, M // tr),
            in_specs=[pl.BlockSpec((None, None, tr, N), lambda j, i, c_ref: (j, c_ref[0], i, 0)),
                      pl.BlockSpec((None, tr, N), lambda j, i, c_ref: (j, i, 0))],
            out_specs=pl.BlockSpec((None, tr, N), lambda j, i, c_ref: (j, i, 0))),
        out_shape=jax.ShapeDtypeStruct((nsh, M, N), BF16), compiler_params=_params(2),
    )(c_arr, grad, got)


def sum_chips(name, q):
    nsh, M, N = q.shape
    tr = M
    for cand in (512, 256, 192, 128, 64, 16):
        if M % cand == 0:
            tr = cand
            break

    def body(q_ref, o_ref):
        acc = q_ref[0].astype(F32)
        for j in range(1, nsh):
            acc = acc + q_ref[j].astype(F32)
        o_ref[...] = acc

    return pl.pallas_call(
        body, name=name, grid=(M // tr,), in_specs=[pl.BlockSpec((nsh, tr, N), lambda i: (0, i, 0))],
        out_specs=pl.BlockSpec((tr, N), lambda i: (i, 0)), out_shape=jax.ShapeDtypeStruct((M, N), F32),
        compiler_params=_params(1),
    )(q)


def pair_exchange(name, halves):
    n = len(halves)

    def body(*refs):
        in_refs, out_refs = refs[:n], refs[n:2 * n]
        send_sems, recv_sems = refs[2 * n:]
        x, y, c = _place()
        cps = []
        for a in range(n):
            cp = pltpu.make_async_remote_copy(
                src_ref=in_refs[a], dst_ref=out_refs[a], send_sem=send_sems.at[a], recv_sem=recv_sems.at[a],
                device_id=(x, y, 1 - c), device_id_type=MESH)
            cp.start()
            cps.append(cp)
        for cp in cps:
            cp.wait()

    res = pl.pallas_call(
        body, name=name, in_specs=[HBM] * n, out_specs=[HBM] * n,
        out_shape=[jax.ShapeDtypeStruct(h.shape, h.dtype) for h in halves],
        scratch_shapes=[pltpu.SemaphoreType.DMA((n,)), pltpu.SemaphoreType.DMA((n,))],
    )(*halves)
    return list(res)


SEM = pl.BlockSpec(memory_space=pltpu.SEMAPHORE)
IN_HBM = pl.BlockSpec(memory_space=pltpu.HBM)
SPLIT_COPY = pltpu.CompilerParams(has_side_effects=pltpu.SideEffectType.DATAFLOW_SIDE_EFFECTING)


def _scatter_copies(src_refs, dst_refs, send_sems, recv_sems):
    x, y, c = _place()
    mine = 2 * x + y
    return [pltpu.make_async_remote_copy(
        src_ref=src_refs[a].at[2 * px + py], dst_ref=dst_refs[a].at[mine], send_sem=send_sems.at[a * (N_CHIPS - 1) + k],
        recv_sem=recv_sems.at[a * (N_CHIPS - 1) + k], device_id=(px, py, c), device_id_type=MESH)
        for a in range(len(src_refs)) for k, (px, py) in enumerate(_other_chips(x, y))]


def _pair_copies(src_refs, dst_refs, send_sems, recv_sems):
    x, y, c = _place()
    return [pltpu.make_async_remote_copy(
        src_ref=src_refs[a].at[:, 1 - c], dst_ref=dst_refs[a], send_sem=send_sems.at[a], recv_sem=recv_sems.at[a],
        device_id=(x, y, 1 - c), device_id_type=MESH) for a in range(len(src_refs))]


def _gather_copies(src_refs, dst_refs, send_sems, recv_sems):
    x, y, c = _place()
    me = 4 * x + 2 * y + c
    cps = []
    for a in range(len(src_refs)):
        for k in range(1, N_DEV):
            to = (1 - x if k & 4 else x, 1 - y if k & 2 else y, 1 - c if k & 1 else c)
            s = a * (N_DEV - 1) + k - 1
            cps.append(pltpu.make_async_remote_copy(
                src_ref=src_refs[a], dst_ref=dst_refs[a].at[me], send_sem=send_sems.at[s], recv_sem=recv_sems.at[s],
                device_id=to, device_id_type=MESH))
    return cps


def split_copy_start(name, copies, n_sem, srcs, land_shapes):
    n = len(srcs)
    lands = [lax.empty(s.shape, s.dtype) for s in land_shapes]

    def body(*refs):
        for cp in copies(refs[:n], refs[n:2 * n], refs[2 * n], refs[2 * n + 1]):
            cp.start()
        refs[-1][...] = jnp.zeros_like(refs[-1])

    thru = [pltpu.HBM(a.shape, a.dtype) for a in (*srcs, *lands)]
    res = pl.pallas_call(
        body, name=name, in_specs=[IN_HBM] * (2 * n),
        out_specs=(SEM, SEM, *[IN_HBM] * (2 * n), pl.BlockSpec(memory_space=pltpu.VMEM)),
        out_shape=(pltpu.SemaphoreType.DMA((n * n_sem,)), pltpu.SemaphoreType.DMA((n * n_sem,)), *thru,
                   jax.ShapeDtypeStruct((SUBLANES, LANES), F32)),
        input_output_aliases={i: 2 + i for i in range(2 * n)}, compiler_params=SPLIT_COPY,
    )(*[pltpu.with_memory_space_constraint(a, pltpu.HBM) for a in (*srcs, *lands)])
    return (copies, n, res[0], res[1], res[2:2 + 2 * n]), res[-1][0, 0]


def split_copy_wait(name, handle, after):
    copies, n, send_sems, recv_sems, thru = handle

    def body(*refs):
        for cp in copies(refs[:n], refs[n:2 * n], refs[2 * n], refs[2 * n + 1]):
            cp.wait_send()
            cp.wait_recv()

    res = pl.pallas_call(
        body, name=name, in_specs=[IN_HBM] * (2 * n) + [SEM, SEM, pl.BlockSpec(memory_space=pl.ANY)],
        out_specs=[IN_HBM] * (2 * n), out_shape=[pltpu.HBM(a.shape, a.dtype) for a in thru],
        input_output_aliases={i: i for i in range(2 * n)}, compiler_params=SPLIT_COPY,
    )(*thru, send_sems, recv_sems, after)
    return list(res[:n]), list(res[n:])


def all_reduce_small(name, v):
    R, C = v.shape

    def body(v_ref, o_ref, gath, send_sems, recv_sems):
        x, y, c = _place()
        me, sibling = (x, y, c), (x, y, 1 - c)
        chips = _other_chips(x, y)

        def slot(px, py, pc):
            return gath.at[4 * px + 2 * py + pc]

        def copy(k, block, to, src=None):
            return pltpu.make_async_remote_copy(
                src_ref=slot(*block) if src is None else src, dst_ref=slot(*block),
                send_sem=send_sems.at[k], recv_sem=recv_sems.at[k], device_id=to, device_id_type=MESH)

        first = [copy(0, me, sibling, src=v_ref)]
        first += [copy(1 + j, me, (*chip, c), src=v_ref) for j, chip in enumerate(chips)]
        for cp in first:
            cp.start()
        slot(*me)[...] = v_ref[...]
        passed = [copy(4 + j, (*chip, c), sibling) for j, chip in enumerate(chips)]
        for j, chip in enumerate(chips):
            copy(1 + j, (*chip, c), me).wait_recv()
            passed[j].start()
        copy(0, sibling, me).wait_recv()
        for j, chip in enumerate(chips):
            copy(4 + j, (*chip, 1 - c), me).wait_recv()
        for cp in first + passed:
            cp.wait_send()
        acc = gath[0]
        for d in range(1, N_DEV):
            acc = acc + gath[d]
        o_ref[...] = acc

    vm = pl.BlockSpec(memory_space=pltpu.VMEM)
    return pl.pallas_call(
        body, name=name, in_specs=[vm], out_specs=vm, out_shape=jax.ShapeDtypeStruct((R, C), F32),
        scratch_shapes=[pltpu.VMEM((N_DEV, R, C), F32), pltpu.SemaphoreType.DMA((7,)), pltpu.SemaphoreType.DMA((7,))],
        compiler_params=pltpu.CompilerParams(vmem_limit_bytes=VMEM_LIMIT_BYTES),
    )(v)


LANES = 128
PACK_ROW_MULTIPLE = 1024
SMALL_SHARDED = {
    "w_in": ((D_MODEL, 1216), 1), "mla_w_uq": ((MLA_Q_RANK, 768), 1), "mla_w_ukv": ((MLA_KV_RANK, 1024), 1),
    "ssm_w_glu": ((SSM_WIDTH, SSM_WIDTH), 0), "w_o": ((D_MODEL, D_MODEL), 0), "xattn_w_q": ((D_MODEL, 512), 0),
    "xattn_w_kv": ((D_MODEL, 1024), 0), "xattn_w_o": ((512, D_MODEL), 1),
}
FFN_NAMES = ["ffn1_w_gate", "ffn1_w_up", "ffn1_w_down", "ffn2_w_gate", "ffn2_w_up", "ffn2_w_down"]
TRANSPOSED_VIEW = ("ffn1_w_gate", "ffn1_w_up", "ffn2_w_gate", "ffn2_w_up", "w_in", "mla_w_uq")


def _shard_shape(name):
    (r, cdim), ax = SMALL_SHARDED[name]
    return (r // N_CHIPS, cdim) if ax == 0 else (r, cdim // N_CHIPS)


def _pack_shards(shards):
    parts = []
    for name in SMALL_SHARDED:
        a = shards[name]
        lead = a.shape[:-2]
        parts.append(a.reshape(lead + (a.shape[-2] * a.shape[-1] // LANES, LANES)))
    rows = sum(q.shape[-2] for q in parts)
    parts.append(jnp.zeros(lead + (-rows % PACK_ROW_MULTIPLE, LANES), parts[0].dtype))
    return jnp.concatenate(parts, axis=-2)


def _unpack_shards(packed):
    out, r0 = {}, 0
    lead = packed.shape[:-2]
    for name in SMALL_SHARDED:
        r, cdim = _shard_shape(name)
        rows = r * cdim // LANES
        out[name] = packed[..., r0:r0 + rows, :].reshape(lead + (r, cdim))
        r0 += rows
    return out


def _full_from_shards(name, sh):
    (r, cdim), ax = SMALL_SHARDED[name]
    if ax == 0:
        return sh.reshape(r, cdim)
    return jnp.transpose(sh, (1, 0, 2)).reshape(r, cdim)


def _shards_from_full(name, full):
    (r, cdim), ax = SMALL_SHARDED[name]
    if ax == 0:
        return full.reshape(N_CHIPS, r // N_CHIPS, cdim)
    return jnp.transpose(full.reshape(r, N_CHIPS, cdim // N_CHIPS), (1, 0, 2))


SMALL_REPL = {
    "ffn1_norm": (1, 1024), "mix_norm": (1, 1024), "mla_q_norm": (1, 384), "mla_kv_norm": (1, 256),
    "mla_qk_norm_q": (1, 192), "mla_qk_norm_k": (1, 192), "ssm_a_re": (32, 64), "ssm_a_im": (32, 64),
    "ssm_log_dt": (32, 1), "ssm_b_re": (32, 64, 16), "ssm_b_im": (32, 64, 16), "ssm_c_re": (32, 16, 64),
    "ssm_c_im": (32, 16, 64), "ssm_d": (1, 512), "ssm_b_glu": (1, 512), "out_norm_mla": (1, 512),
    "out_norm_ssm": (1, 512), "xattn_norm": (1, 1024), "mem_norm": (1, 1024), "xattn_q_norm": (1, 128),
    "xattn_k_norm": (1, 128), "ffn2_norm": (1, 1024),
}


def _pack_repl(grads, loss):
    flat = jnp.concatenate([grads[n].reshape(-1) for n in SMALL_REPL] + [loss.reshape(1)])
    rows = -(-flat.shape[0] // (LANES * SUBLANES)) * SUBLANES
    return jnp.pad(flat, (0, rows * LANES - flat.shape[0])).reshape(rows, LANES)


def _unpack_repl(packed):
    flat, out, o = packed.reshape(-1), {}, 0
    for n, shp in SMALL_REPL.items():
        size = int(np.prod(shp))
        out[n] = flat[o:o + size].reshape(shp)
        o += size
    out["loss"] = flat[o]
    return out


def _rope_tables(positions):
    half = MLA_ROPE // 2
    inv = ROPE_THETA ** (-jnp.arange(half, dtype=F32) / half)
    ang = positions.astype(F32)[:, None] * inv[None, :]
    cos, sin = jnp.cos(ang), jnp.sin(ang)
    S = positions.shape[0]
    z = lambda w: jnp.zeros((S, w), F32)
    keep = jnp.concatenate([jnp.ones((S, MLA_NOPE), F32), cos, cos, z(HEAD_PAD - MLA_QK)], axis=1)
    from_hi = jnp.concatenate([z(MLA_NOPE), -sin, z(HEAD_PAD - MLA_NOPE - half)], axis=1)
    from_lo = jnp.concatenate([z(MLA_NOPE + half), sin, z(HEAD_PAD - MLA_QK)], axis=1)
    return keep, from_hi, from_lo


def _norm_rope(x, g, keep, from_hi, from_lo):
    y = x * lax.rsqrt(jnp.sum(x * x, axis=-1, keepdims=True) * (1.0 / MLA_QK) + EPS) * g
    half = MLA_ROPE // 2
    return y * keep + _lane_roll(y, HEAD_PAD - half) * from_hi + _lane_roll(y, half) * from_lo


def local_step(x, mem, positions, target, w, wb, small_weights=None, ffn2_weights=None, on_grads=None):
    if small_weights is None:
        small_weights = lambda after: {}
    if ffn2_weights is None:
        ffn2_weights = lambda after: [wb[k] for k in FFN_NAMES[3:]]
    if on_grads is None:
        on_grads = lambda tag, g: 0.0
    S = x.shape[0]
    tm = min(FUSED_TILE, S)
    g1 = (S // tm,)
    tile = lambda arr, **kw: In(arr, (tm, arr.shape[1]), lambda i: (i, 0), rows=True, **kw)
    par = lambda arr, **kw: In(arr, arr.shape, lambda *_: (0, 0), diff=True, acc=(0,), **kw)
    wt = lambda arr: In(arr, weight=True)
    otile = lambda cols, dt: Out((S, cols), dt, (tm, cols), lambda i: (i, 0))
    grads = {}

    x1, h1, g_1, u_1 = ffn_fwd("ffn1_fwd", x, w["ffn1_norm"], wb["ffn1_w_gate"], wb["ffn1_w_up"], wb["ffn1_w_down"])
    wb = {**wb, **small_weights(x1)}

    keep, from_hi, from_lo = _rope_tables(positions)
    scale = MLA_QK ** -0.5

    def f_pre(xv, kp, fh, fl, g_mix, w_q, w_kv, w_kr, w_u, g_q, g_kv, w_uq, w_ukv, gq, gk):
        h2 = _rms(xv, g_mix)
        cq, ckv = _rms(wdot(h2, w_q), g_q), _rms(wdot(h2, w_kv), g_kv)
        kr = wdot(h2, w_kr)
        qs, ks, vs = [], [], []
        for h in range(MLA_HEADS):
            qs.append(_norm_rope(wdot(cq, w_uq, h), gq, kp, fh, fl) * scale)
            kv = wdot(ckv, w_ukv, h)
            ks.append(_norm_rope(jnp.concatenate([kv[:, :MLA_NOPE], kr], axis=-1), gk, kp, fh, fl))
            vs.append(kv[:, MLA_NOPE:])
        return jnp.concatenate(qs, axis=-1), jnp.concatenate(ks, axis=-1), jnp.concatenate(vs, axis=-1), wdot(h2, w_u)

    def pre_ins(gain_mix):
        return [tile(x1, diff=True), tile(keep), tile(from_hi), tile(from_lo), par(gain_mix),
                wt(wb["w_in_q"]), wt(wb["w_in_kv"]), wt(wb["w_in_kr"]), wt(wb["w_in_u"]),
                par(w["mla_q_norm"]), par(w["mla_kv_norm"]), wt(wb["w_uq"]), wt(wb["w_ukv"]),
                par(w["qk_gain_q"]), par(w["qk_gain_k"])]

    pre_outs = [otile(MLA_HEADS * HEAD_PAD, BF16), otile(MLA_HEADS * HEAD_PAD, BF16), otile(MLA_HEADS * MLA_V, BF16),
                otile(SSM_WIDTH, F32)]
    qh, kh, v0, pu = seg_fwd("pre_mixer_fwd", f_pre, g1, pre_ins(w["mix_norm"]), pre_outs)

    o_mla, lse = attn_fwd("mla_attn_fwd", qh, kh, v0, t=min(ATTN_TILE, S))

    prep_grid = (SSM_BLOCKS,)
    prep_ins = ([_whole(w[k], diff=True, acc=(0,)) for k in ("ssm_a_re", "ssm_a_im", "ssm_log_dt")]
                + [In(w[k], (BLOCK_CH, SSM_STATE), lambda i: (i, 0), diff=True)
                   for k in ("ssm_bt_re", "ssm_bt_im", "ssm_c2_re", "ssm_c2_im")])
    blk_out = Out((SSM_BLOCKS, BLOCK_CH, PREP_LANES), BF16, (None, BLOCK_CH, PREP_LANES), lambda i: (i, 0, 0))
    prep_outs = [Out((SUBLANES, SSM_LANES), F32, (SUBLANES, PREP_LANES), lambda i: (0, i))] * 2 + [blk_out] * 4
    a_r8, a_i8, bb_r, bb_i, cb_r, ncb_i = seg_fwd("ssm_prep_fwd", _ssm_prep_f, prep_grid, prep_ins, prep_outs)

    xs_r, xs_i, y_lin = ssm_fwd("ssm_fwd", pu, bb_r, bb_i, cb_r, ncb_i, a_r8, a_i8)

    M = mem.shape[0]
    f_norm = lambda xv, g: (_rms(xv, g),)
    mem_ins = [In(mem, (M, D_MODEL), lambda i: (0, 0)), par(w["mem_norm"])]
    mem_outs = [Out((M, D_MODEL), BF16, (M, D_MODEL), lambda i: (0, 0))]
    (mn,) = seg_fwd("mem_norm_fwd", f_norm, (1,), mem_ins, mem_outs)
    (kvm,) = mm("xattn_kv_fwd", [mn], [[wb["xattn_w_kv"]]], [F32])

    def f_knorm(kk, gk):
        return (jnp.concatenate([_rms(kk[:, h * XHD:(h + 1) * XHD], gk) for h in range(XH)], axis=-1),)

    keys = ((M, XH * XHD), (M, XH * XHD), lambda i: (0, 0))
    knorm_ins = [In(kvm, keys[1], keys[2], diff=True, grad=keys), par(w["xattn_k_norm"])]
    knorm_outs = [Out(keys[0], F32, keys[1], keys[2])]
    (kn_mem,) = seg_fwd("mem_key_norm_fwd", f_knorm, (1,), knorm_ins, knorm_outs)

    xscale = XHD ** -0.5

    def f_post(o, yl, u, xv, kn, vm, d, w_glu, b, gm, gs, w_o1, w_o2, gx, w_xq, gq, w_xo):
        gl = jax.nn.gelu(yl + d * u)
        so = gl * jax.nn.sigmoid(wdot(gl, w_glu) + b)
        x2 = xv + wdot(_rms(o, gm), w_o1) + wdot(_rms(so, gs), w_o2)
        h3 = _rms(x2, gx)
        heads = []
        for h in range(XH):
            qn = _rms(wdot(h3, w_xq, h), gq)
            p = jax.nn.softmax(_bdot_nt(qn, kn[:, h * XHD:(h + 1) * XHD]) * xscale, axis=-1)
            heads.append(_bdot_nn(p, vm[:, h * XHD:(h + 1) * XHD]))
        return (x2 + wdot(jnp.concatenate(heads, axis=-1), w_xo),)

    def post_ins(gain_d):
        half = (M, XH * XHD)
        return [tile(o_mla, diff=True), tile(y_lin, diff=True), tile(pu, diff=True), tile(x1, diff=True),
                In(kn_mem, half, lambda i: (0, 0), diff=True, acc=(0,)),
                In(kvm, half, lambda i: (0, 1), diff=True, acc=(0,), grad=(half, half, lambda i: (0, 0))),
                par(gain_d), wt(wb["ssm_w_glu"]), par(w["ssm_b_glu"]), par(w["out_norm_mla"]), par(w["out_norm_ssm"]),
                wt(wb["w_o_mla"]), wt(wb["w_o_ssm"]), par(w["xattn_norm"]), wt(wb["xattn_w_q"]), par(w["xattn_q_norm"]),
                wt(wb["xattn_w_o"])]

    post_outs = [otile(D_MODEL, F32)]
    (x3,) = seg_fwd("post_mixer_fwd", f_post, g1, post_ins(w["ssm_d"]), post_outs)

    wg2, wu2, wd2 = ffn2_weights(x3)
    dx4, h4, g_2, u_2, parts = ffn_fwd("ffn2_fwd_loss", x3, w["ffn2_norm"], wg2, wu2, wd2, target=target)
    loss = jnp.sum(parts[::SUBLANES, 0])

    dx3, grads["ffn2_norm"], dg_2, du_2 = ffn_bwd_act("ffn2_bwd_act", dx4, x3, w["ffn2_norm"], g_2, u_2, wg2, wu2, wd2)
    grads["ffn2_w_gate"], grads["ffn2_w_up"], grads["ffn2_w_down"] = ffn_bwd_w("ffn2_bwd_w", h4, dx4, g_2, u_2, dg_2, du_2)
    sent = on_grads("ffn2", grads)

    (do_mla, dy_lin, du_a, dx1_a, dkn, dvm, grads["ssm_d"], grads["ssm_w_glu"], grads["ssm_b_glu"], grads["out_norm_mla"],
     grads["out_norm_ssm"], grads["w_o_mla"], grads["w_o_ssm"], grads["xattn_norm"], grads["xattn_w_q"],
     grads["xattn_q_norm"], grads["xattn_w_o"]) = seg_bwd(
        "post_mixer_bwd", f_post, g1, post_ins(w["ssm_d"] + sent), post_outs, [dx3])

    dkk, grads["xattn_k_norm"] = seg_bwd("mem_key_norm_bwd", f_knorm, (1,), knorm_ins, knorm_outs, [dkn])
    w_kv = wb["xattn_w_kv"]
    (dmn,) = mm("xattn_kv_bwd", [dkk, dvm], [[w_kv[:, :XH * XHD]], [w_kv[:, XH * XHD:]]], [F32], trans=True)
    gk_w, gv_w = mm_tn("xattn_kv_bwd_w", [mn], [dkk, dvm], [(0, [0]), (0, [1])])
    grads["xattn_w_kv"] = jnp.concatenate([gk_w, gv_w], axis=1)
    (grads["mem_norm"],) = seg_bwd("mem_norm_bwd", f_norm, (1,), mem_ins, mem_outs, [dmn])

    du, dbb_r, dbb_i, dcb_r, dncb_i, da_r8, da_i8 = ssm_bwd("ssm_bwd", dy_lin, du_a, pu, xs_r, xs_i, bb_r, bb_i, cb_r, ncb_i,
                                                            a_r8, a_i8)
    prep_g = seg_bwd("ssm_prep_bwd", _ssm_prep_f, prep_grid, prep_ins, prep_outs, [da_r8, da_i8, dbb_r, dbb_i, dcb_r, dncb_i])
    for k, gname in enumerate(("ssm_a_re", "ssm_a_im", "ssm_log_dt", "ssm_bt_re", "ssm_bt_im", "ssm_c2_re", "ssm_c2_im")):
        grads[gname] = prep_g[k]

    dqh, dkh, dv0 = attn_bwd("mla_attn_bwd", qh, kh, v0, do_mla, o_mla, lse, t=min(ATTN_TILE, S))

    (dx1, grads["mix_norm"], grads["w_in_q"], grads["w_in_kv"], grads["w_in_kr"], grads["w_in_u"], grads["mla_q_norm"],
     grads["mla_kv_norm"], grads["w_uq"], grads["w_ukv"], grads["qk_gain_q"], grads["qk_gain_k"]) = seg_bwd(
        "pre_mixer_bwd", f_pre, g1, pre_ins(w["mix_norm"]), pre_outs, [dqh, dkh, dv0, du], adds={0: dx1_a},
        row_block=FUSED_ROW_BLOCK)

    sent = on_grads("small", grads)
    dx, grads["ffn1_norm"], dg_1, du_1 = ffn_bwd_act("ffn1_bwd_act", dx1, x, w["ffn1_norm"] + sent, g_1, u_1,
                                                     wb["ffn1_w_gate"], wb["ffn1_w_up"], wb["ffn1_w_down"])
    grads["ffn1_w_gate"], grads["ffn1_w_up"], grads["ffn1_w_down"] = ffn_bwd_w("ffn1_bwd_w", h1, dx1, g_1, u_1, dg_1, du_1)
    return loss, dx, grads


def _pad_cols(a, n):
    return jnp.pad(a, ((0, 0), (0, n - a.shape[1])))


def _step_weights(shards):
    wb = {}
    w_in = _full_from_shards("w_in", shards["w_in"])
    wb["w_in_q"] = w_in[:, :MLA_Q_RANK]
    wb["w_in_kv"] = w_in[:, MLA_Q_RANK:MLA_Q_RANK + MLA_KV_RANK]
    wb["w_in_kr"] = _pad_cols(w_in[:, MLA_Q_RANK + MLA_KV_RANK:MLA_Q_RANK + MLA_KV_RANK + MLA_ROPE], LANES)
    wb["w_in_u"] = w_in[:, MLA_Q_RANK + MLA_KV_RANK + MLA_ROPE:]
    wb["w_uq"] = jnp.pad(shards["mla_w_uq"], ((0, 0), (0, 0), (0, HEAD_PAD - MLA_QK)))
    wb["w_ukv"] = shards["mla_w_ukv"]
    wb["ssm_w_glu"] = _full_from_shards("ssm_w_glu", shards["ssm_w_glu"])
    w_o = _full_from_shards("w_o", shards["w_o"])
    wb["w_o_mla"], wb["w_o_ssm"] = w_o[:SSM_WIDTH], w_o[SSM_WIDTH:]
    w_xq = _full_from_shards("xattn_w_q", shards["xattn_w_q"])
    wb["xattn_w_q"] = jnp.transpose(w_xq.reshape(D_MODEL, XH, XHD), (1, 0, 2))
    wb["xattn_w_kv"] = _full_from_shards("xattn_w_kv", shards["xattn_w_kv"])
    wb["xattn_w_o"] = _full_from_shards("xattn_w_o", shards["xattn_w_o"])
    return wb


def _sharded_grads(g):
    out = {}
    kr = g["w_in_kr"][:, :MLA_ROPE]
    out["w_in"] = _shards_from_full("w_in", jnp.concatenate([g["w_in_q"], g["w_in_kv"], kr, g["w_in_u"]], axis=1))
    out["mla_w_uq"] = g["w_uq"][:, :, :MLA_QK]
    out["mla_w_ukv"] = g["w_ukv"]
    out["ssm_w_glu"] = _shards_from_full("ssm_w_glu", g["ssm_w_glu"])
    out["w_o"] = _shards_from_full("w_o", jnp.concatenate([g["w_o_mla"], g["w_o_ssm"]], axis=0))
    w_xq = jnp.transpose(g["xattn_w_q"], (1, 0, 2)).reshape(D_MODEL, XH * XHD)
    out["xattn_w_q"] = _shards_from_full("xattn_w_q", w_xq)
    out["xattn_w_kv"] = _shards_from_full("xattn_w_kv", g["xattn_w_kv"])
    out["xattn_w_o"] = _shards_from_full("xattn_w_o", g["xattn_w_o"])
    return out


def _problem_repl_grads(g):
    out = {}
    out["mla_qk_norm_q"] = g["qk_gain_q"][:, :MLA_QK]
    out["mla_qk_norm_k"] = g["qk_gain_k"][:, :MLA_QK]
    out["ssm_b_re"] = jnp.transpose(g["ssm_bt_re"].reshape(SSM_GROUPS, SSM_GROUP, SSM_STATE), (0, 2, 1))
    out["ssm_b_im"] = jnp.transpose(g["ssm_bt_im"].reshape(SSM_GROUPS, SSM_GROUP, SSM_STATE), (0, 2, 1))
    out["ssm_c_re"] = g["ssm_c2_re"].reshape(SSM_GROUPS, SSM_GROUP, SSM_STATE)
    out["ssm_c_im"] = g["ssm_c2_im"].reshape(SSM_GROUPS, SSM_GROUP, SSM_STATE)
    for k in SMALL_REPL:
        if k not in out:
            out[k] = g[k]
    return out


def _step_params(p):
    row = lambda a: a.reshape(1, -1)
    w = {k: row(p[k]) for k in ("ffn1_norm", "mix_norm", "mla_q_norm", "mla_kv_norm", "ssm_b_glu", "out_norm_mla",
                                "out_norm_ssm", "xattn_norm", "mem_norm", "xattn_q_norm", "xattn_k_norm", "ffn2_norm")}
    w["qk_gain_q"] = _pad_cols(row(p["mla_qk_norm_q"]), HEAD_PAD)
    w["qk_gain_k"] = _pad_cols(row(p["mla_qk_norm_k"]), HEAD_PAD)
    w["ssm_a_re"], w["ssm_a_im"] = p["ssm_a_re"], p["ssm_a_im"]
    w["ssm_log_dt"] = p["ssm_log_dt"].reshape(SSM_GROUPS, 1)
    w["ssm_bt_re"] = jnp.transpose(p["ssm_b_re"], (0, 2, 1)).reshape(SSM_WIDTH, SSM_STATE)
    w["ssm_bt_im"] = jnp.transpose(p["ssm_b_im"], (0, 2, 1)).reshape(SSM_WIDTH, SSM_STATE)
    w["ssm_c2_re"] = p["ssm_c_re"].reshape(SSM_WIDTH, SSM_STATE)
    w["ssm_c2_im"] = p["ssm_c_im"].reshape(SSM_WIDTH, SSM_STATE)
    w["ssm_d"] = p["ssm_d"].reshape(1, SSM_WIDTH)
    return w


ARG_NAMES = ['x', 'mem', 'positions', 'ffn1_norm', 'ffn1_w_gate', 'ffn1_w_up', 'ffn1_w_down', 'mix_norm', 'w_in', 'mla_q_norm', 'mla_w_uq', 'mla_kv_norm', 'mla_w_ukv', 'mla_qk_norm_q', 'mla_qk_norm_k', 'ssm_a_re', 'ssm_a_im', 'ssm_log_dt', 'ssm_b_re', 'ssm_b_im', 'ssm_c_re', 'ssm_c_im', 'ssm_d', 'ssm_w_glu', 'ssm_b_glu', 'out_norm_mla', 'out_norm_ssm', 'w_o', 'xattn_norm', 'mem_norm', 'xattn_w_q', 'xattn_w_kv', 'xattn_q_norm', 'xattn_k_norm', 'xattn_w_o', 'ffn2_norm', 'ffn2_w_gate', 'ffn2_w_up', 'ffn2_w_down']
WEIGHT_NAMES = ARG_NAMES[3:]


def _gather_weights(p, c):
    half = lambda a: lax.dynamic_slice_in_dim(a, c * (a.shape[0] // 2), a.shape[0] // 2, axis=0)
    ffn1 = [half(p[k].astype(BF16)) for k in FFN_NAMES[:3]]
    small = [half(_pack_shards({k: p[k].astype(BF16) for k in SMALL_SHARDED}))]
    ffn2 = [half(p[k].astype(BF16)) for k in FFN_NAMES[3:]]
    me = 4 * lax.axis_index("x") + 2 * lax.axis_index("y") + c
    own = lambda got, blocks: [lax.dynamic_update_index_in_dim(g, b, me, 0) for g, b in zip(got, blocks)]
    as_shards = lambda a: a.reshape(N_CHIPS, 2 * a.shape[1], a.shape[2])
    landing = lambda blocks: [jax.ShapeDtypeStruct((N_DEV,) + b.shape, b.dtype) for b in blocks]
    got1 = own(all_gather_halves("all_gather_weights_a", ffn1), ffn1)
    got1, small = lax.optimization_barrier((got1, small))
    flight_s, sent_s = split_copy_start("gather_small_start", _gather_copies, N_DEV - 1, small, landing(small))
    sent_s, ffn2 = lax.optimization_barrier((sent_s, ffn2))
    flight_2, sent_2 = split_copy_start("gather_ffn2_start", _gather_copies, N_DEV - 1, ffn2, landing(ffn2))
    wb = {k: as_shards(a) for k, a in zip(FFN_NAMES[:3], got1)}

    def small_weights(after):
        mine, got = split_copy_wait("gather_small_wait", flight_s, after)
        return _step_weights(_unpack_shards(own(got, mine)[0].reshape(N_CHIPS, -1, LANES)))

    def ffn2_weights(after):
        mine, got = split_copy_wait("gather_ffn2_wait", flight_2, after)
        return [as_shards(a) for a in own(got, mine)]

    return wb, small_weights, ffn2_weights, sent_s + sent_2


class _GradReduce:
    def __init__(self, c):
        self.c, self.c_arr = c, jnp.reshape(c, (1,)).astype(jnp.int32)
        self.chip = 2 * lax.axis_index("x") + lax.axis_index("y")
        self.flights = []

    def start(self, tag, arrs):
        split = [a.reshape(N_CHIPS, 2, a.shape[1] // 2, a.shape[2]) for a in arrs]
        return self._scatter(tag, split, pair_send_halves(f"grad_pair_send_{tag}", split))

    def send(self, tag, arrs):
        split = [a.reshape(N_CHIPS, 2, a.shape[1] // 2, a.shape[2]) for a in arrs]
        lands = [jax.ShapeDtypeStruct((s.shape[0],) + s.shape[2:], s.dtype) for s in split]
        flight, sent = split_copy_start(f"grad_pair_send_start_{tag}", _pair_copies, 1, split, lands)
        self.sending = (tag, flight)
        return sent

    def scatter(self, tag, after):
        sent_tag, flight = self.sending
        assert sent_tag == tag
        return self._scatter(tag, *split_copy_wait(f"grad_pair_send_wait_{tag}", flight, after))

    def _scatter(self, tag, split, got):
        parts = [pair_add(f"grad_pair_add_{tag}_{k}", s, g, self.c_arr) for k, (s, g) in enumerate(zip(split, got))]
        flight, sent = split_copy_start(f"grad_scatter_start_{tag}", _scatter_copies, N_CHIPS - 1, parts, parts)
        self.flights.append((tag, flight))
        return sent

    def finish(self, after):
        halves = []
        for tag, flight in self.flights:
            parts, landed = split_copy_wait(f"grad_scatter_wait_{tag}", flight, after)
            for k, (q, p) in enumerate(zip(landed, parts)):
                mine = lax.dynamic_index_in_dim(p, self.chip, 0, keepdims=False)
                halves.append(sum_chips(f"grad_sum_{tag}_{k}", lax.dynamic_update_index_in_dim(q, mine, self.chip, 0)))
        tags = "_".join(t for t, _ in self.flights)
        self.flights = []
        theirs = pair_exchange(f"grad_pair_exchange_{tags}", halves)
        return [jnp.where(self.c == 0, jnp.concatenate([h, t], axis=0), jnp.concatenate([t, h], axis=0))
                for h, t in zip(halves, theirs)]


def kernel(x, mem, positions, ffn1_norm, ffn1_w_gate, ffn1_w_up, ffn1_w_down, mix_norm, w_in, mla_q_norm, mla_w_uq, mla_kv_norm, mla_w_ukv, mla_qk_norm_q, mla_qk_norm_k, ssm_a_re, ssm_a_im, ssm_log_dt, ssm_b_re, ssm_b_im, ssm_c_re, ssm_c_im, ssm_d, ssm_w_glu, ssm_b_glu, out_norm_mla, out_norm_ssm, w_o, xattn_norm, mem_norm, xattn_w_q, xattn_w_kv, xattn_q_norm, xattn_k_norm, xattn_w_o, ffn2_norm, ffn2_w_gate, ffn2_w_up, ffn2_w_down, loss_target, m_ffn1_norm, m_ffn1_w_gate, m_ffn1_w_up, m_ffn1_w_down, m_mix_norm, m_w_in, m_mla_q_norm, m_mla_w_uq, m_mla_kv_norm, m_mla_w_ukv, m_mla_qk_norm_q, m_mla_qk_norm_k, m_ssm_a_re, m_ssm_a_im, m_ssm_log_dt, m_ssm_b_re, m_ssm_b_im, m_ssm_c_re, m_ssm_c_im, m_ssm_d, m_ssm_w_glu, m_ssm_b_glu, m_out_norm_mla, m_out_norm_ssm, m_w_o, m_xattn_norm, m_mem_norm, m_xattn_w_q, m_xattn_w_kv, m_xattn_q_norm, m_xattn_k_norm, m_xattn_w_o, m_ffn2_norm, m_ffn2_w_gate, m_ffn2_w_up, m_ffn2_w_down, v_ffn1_norm, v_ffn1_w_gate, v_ffn1_w_up, v_ffn1_w_down, v_mix_norm, v_w_in, v_mla_q_norm, v_mla_w_uq, v_mla_kv_norm, v_mla_w_ukv, v_mla_qk_norm_q, v_mla_qk_norm_k, v_ssm_a_re, v_ssm_a_im, v_ssm_log_dt, v_ssm_b_re, v_ssm_b_im, v_ssm_c_re, v_ssm_c_im, v_ssm_d, v_ssm_w_glu, v_ssm_b_glu, v_out_norm_mla, v_out_norm_ssm, v_w_o, v_xattn_norm, v_mem_norm, v_xattn_w_q, v_xattn_w_kv, v_xattn_q_norm, v_xattn_k_norm, v_xattn_w_o, v_ffn2_norm, v_ffn2_w_gate, v_ffn2_w_up, v_ffn2_w_down):
    args = dict(locals())
    c = lax.axis_index("c")
    view = lambda k, a: jnp.swapaxes(a, 0, 1) if k in TRANSPOSED_VIEW else a
    p = {k: view(k, args[k][0]) for k in WEIGHT_NAMES}
    mom = {k: view(k, args["m_" + k][0]) for k in WEIGHT_NAMES}
    var = {k: view(k, args["v_" + k][0]) for k in WEIGHT_NAMES}
    natural = {k: view(k, p[k]) for k in WEIGHT_NAMES}

    wb, small_weights, ffn2_weights, sent = _gather_weights(
        {k: (p[k] if k in FFN_NAMES else natural[k]) for k in WEIGHT_NAMES}, c)
    w = _step_params(natural)
    w["ffn1_norm"] = w["ffn1_norm"] + sent
    early, late = _GradReduce(c), _GradReduce(c)

    def on_grads(tag, g):
        if tag == "ffn2":
            return early.send(tag, [g[k] for k in FFN_NAMES[3:]])
        packed = _pack_shards(_sharded_grads(g))
        return early.scatter("ffn2", packed[0, :SUBLANES]) + early.start(tag, [packed])

    loss, dx, g = local_step(x[0], mem[0], positions[0], loss_target[0], w, wb, small_weights, ffn2_weights, on_grads)

    sent = late.send("ffn1", [g[k] for k in FFN_NAMES[:3]])
    shards = early.finish(dx[:SUBLANES, :LANES] + sent)
    grad = dict(zip(FFN_NAMES[3:], shards[:3]))
    small_sharded = _unpack_shards(shards[3])
    grad.update({k: view(k, small_sharded[k]) for k in SMALL_SHARDED})
    reduced = all_reduce_small("grad_all_reduce_small", _pack_repl(_problem_repl_grads(g), loss + sent))
    grad.update(_unpack_repl(reduced))
    loss = grad.pop("loss")
    started = jnp.reshape(late.scatter("ffn1", reduced[:SUBLANES]), (1, 1))

    delta, new_m, new_v = {}, {}, {}
    small = [k for k in WEIGHT_NAMES if k not in FFN_NAMES and k not in SMALL_SHARDED]
    as2d = lambda a: a.reshape(-1, a.shape[-1])

    def update(k, after):
        delta[k], new_m[k], new_v[k] = adamw_big("adamw_" + k, as2d(p[k]), as2d(grad[k]), as2d(mom[k]), as2d(var[k]), after)

    last = started
    for k in WEIGHT_NAMES:
        if k not in small and k not in FFN_NAMES[:3]:
            update(k, last)
            last = delta[k]
    ds, nms, nvs = adamw_small("adamw_small", [as2d(p[k]) for k in small], [as2d(grad[k].reshape(p[k].shape)) for k in small],
                               [as2d(mom[k]) for k in small], [as2d(var[k]) for k in small], last)
    for k, d, nm, nv in zip(small, ds, nms, nvs):
        delta[k], new_m[k], new_v[k] = d, nm, nv
    last = ds[0]

    grad.update(zip(FFN_NAMES[:3], late.finish(last)))
    for k in FFN_NAMES[:3]:
        update(k, last)

    shaped = lambda d, k: view(k, d.reshape(p[k].shape)).reshape(args[k].shape)
    return (loss, dx[None], *[shaped(grad[k], k) for k in WEIGHT_NAMES], *[shaped(delta[k], k) for k in WEIGHT_NAMES],
            *[shaped(new_m[k], k) for k in WEIGHT_NAMES], *[shaped(new_v[k], k) for k in WEIGHT_NAMES])
```

```python
import functools
import math

import jax
import jax.numpy as jnp
import numpy as np
from jax import lax
from jax.experimental import pallas as pl
from jax.experimental.pallas import tpu as pltpu

F32, BF16 = jnp.float32, jnp.bfloat16
EPS = 1e-6
MESH = pl.DeviceIdType.MESH

D_MODEL, D_FF = 1024, 2752
MLA_HEADS, MLA_Q_RANK, MLA_KV_RANK, MLA_NOPE, MLA_ROPE, MLA_V = 4, 384, 256, 128, 64, 128
MLA_QK = MLA_NOPE + MLA_ROPE
HEAD_PAD = 256
SSM_WIDTH, SSM_GROUP, SSM_GROUPS, SSM_STATE = 512, 16, 32, 64
SSM_LANES = SSM_GROUPS * SSM_STATE
XH, XHD = 4, 128
ROPE_THETA = 10000.0
ADAM_LR, ADAM_B1, ADAM_B2, ADAM_EPS, ADAM_WD, ADAM_STEP = 0.001, 0.9, 0.999, 1e-08, 0.01, 10
N_CHIPS, N_DEV = 4, 8

VMEM_LIMIT_BYTES = 56 * 2**20
TOKEN_TILE = 512
FUSED_TILE = 256
FUSED_ROW_BLOCK = 128
FFN_SHARDS_PER_STEP = 2
FFN_ROW_BLOCK = 256
FFN_WIDE_TILE = 1024
ATTN_TILE = 1024
STAT_LANES = 128
SCAN_TIME_TILE = 1024
SCAN_LANE_TILE = 512
SUBLANES = 8


def _params(n_axes):
    return pltpu.CompilerParams(dimension_semantics=("arbitrary",) * n_axes, vmem_limit_bytes=VMEM_LIMIT_BYTES)


def _first(axes):
    cond = None
    for a in axes:
        c = pl.program_id(a) == 0
        cond = c if cond is None else jnp.logical_and(cond, c)
    return cond


def mm(name, xs, ws, out_dtypes, *, trans=False, adds=None, tm=TOKEN_TILE):
    rows = xs[0].shape[0]
    tm = min(tm, rows)
    n_in, n_out = len(xs), len(out_dtypes)
    pairs = [(i, j) for i in range(n_in) for j in range(n_out) if ws[i][j] is not None]
    w_list = [ws[i][j] for (i, j) in pairs]
    adds = list(adds) if adds is not None else [None] * n_out
    add_list = [a for a in adds if a is not None]
    out_cols = [None] * n_out
    for (i, j), w in zip(pairs, w_list):
        out_cols[j] = w.shape[0] if trans else w.shape[1]
    contract = (((1,), (1 if trans else 0,)), ((), ()))

    def body(*refs):
        x_refs = refs[:n_in]
        w_refs = refs[n_in:n_in + len(pairs)]
        a_refs = list(refs[n_in + len(pairs):n_in + len(pairs) + len(add_list)])
        o_refs = refs[n_in + len(pairs) + len(add_list):]
        xb = [None] * n_in
        for j in range(n_out):
            acc = None
            for p, (i, jj) in enumerate(pairs):
                if jj != j:
                    continue
                if xb[i] is None:
                    xb[i] = x_refs[i][...].astype(BF16)
                d = lax.dot_general(xb[i], w_refs[p][...].astype(BF16), contract, preferred_element_type=F32)
                acc = d if acc is None else acc + d
            if adds[j] is not None:
                acc = acc + a_refs.pop(0)[...].astype(F32)
            o_refs[j][...] = acc.astype(o_refs[j].dtype)

    in_specs = ([pl.BlockSpec((tm, x.shape[1]), lambda i: (i, 0)) for x in xs]
                + [pl.BlockSpec(w.shape, lambda i: (0, 0)) for w in w_list]
                + [pl.BlockSpec((tm, a.shape[1]), lambda i: (i, 0)) for a in add_list])
    outs = pl.pallas_call(
        body, name=name, grid=(rows // tm,), in_specs=in_specs,
        out_specs=[pl.BlockSpec((tm, n), lambda i: (i, 0)) for n in out_cols],
        out_shape=[jax.ShapeDtypeStruct((rows, n), dt) for n, dt in zip(out_cols, out_dtypes)],
        compiler_params=_params(1),
    )(*xs, *w_list, *add_list)
    return list(outs)


def mm_tn(name, xs, dys, pairs, *, tm=TOKEN_TILE):
    rows = xs[0].shape[0]
    tm = min(tm, rows)
    n_x, n_dy = len(xs), len(dys)
    contract = (((0,), (0,)), ((), ()))

    def body(*refs):
        x_refs, dy_refs, o_refs = refs[:n_x], refs[n_x:n_x + n_dy], refs[n_x + n_dy:]
        @pl.when(pl.program_id(0) == 0)
        def _():
            for o in o_refs:
                o[...] = jnp.zeros_like(o)

        for k, (i, js) in enumerate(pairs):
            dy = None
            for j in js:
                t = dy_refs[j][...].astype(F32)
                dy = t if dy is None else dy + t
            o_refs[k][...] += lax.dot_general(x_refs[i][...].astype(BF16), dy.astype(BF16), contract,
                                              preferred_element_type=F32)

    shapes = [(xs[i].shape[1], dys[js[0]].shape[1]) for (i, js) in pairs]
    outs = pl.pallas_call(
        body, name=name, grid=(rows // tm,),
        in_specs=[pl.BlockSpec((tm, a.shape[1]), lambda i: (i, 0)) for a in (*xs, *dys)],
        out_specs=[pl.BlockSpec(s, lambda i: (0, 0)) for s in shapes],
        out_shape=[jax.ShapeDtypeStruct(s, F32) for s in shapes],
        compiler_params=_params(1),
    )(*xs, *dys)
    return list(outs)


class In:
    def __init__(self, arr, block=None, imap=None, *, diff=False, acc=None, grad=None, weight=False, rows=False):
        self.arr, self.block, self.imap, self.diff, self.acc, self.grad = arr, block, imap, diff, acc, grad
        self.weight, self.rows = weight, rows

    def spec(self):
        return pl.BlockSpec(memory_space=pltpu.VMEM) if self.weight else pl.BlockSpec(self.block, self.imap)


class Out:
    def __init__(self, shape, dtype, block, imap):
        self.shape, self.dtype, self.block, self.imap = shape, dtype, block, imap

    def spec(self):
        return pl.BlockSpec(self.block, self.imap)


class Wt:
    def __init__(self, ref, zeros=None):
        self.ref, self.zeros = ref, zeros


@jax.custom_vjp
def _wdot(a, w, z):
    return jnp.dot(a.astype(BF16), w, preferred_element_type=F32)


def _wdot_fwd(a, w, z):
    return _wdot(a, w, z), (a, w)


def _wdot_bwd(res, g):
    a, w = res
    gb = g.astype(BF16)
    da = lax.dot_general(gb, w, (((1,), (1,)), ((), ())), preferred_element_type=F32)
    dz = lax.dot_general(a.astype(BF16), gb, (((0,), (0,)), ((), ())), preferred_element_type=F32)
    return da, None, dz


_wdot.defvjp(_wdot_fwd, _wdot_bwd)


def wdot(a, wt, head=None):
    w = wt.ref[...] if head is None else wt.ref[head]
    if wt.zeros is None:
        return jnp.dot(a.astype(BF16), w, preferred_element_type=F32)
    return _wdot(a, w, wt.zeros[0 if head is None else head])


def seg_fwd(name, f, grid, ins, outs):
    n_in = len(ins)

    def body(*refs):
        res = f(*[Wt(r) if i.weight else r[...] for i, r in zip(ins, refs[:n_in])])
        for o_ref, r in zip(refs[n_in:], res):
            o_ref[...] = r.astype(o_ref.dtype)

    res = pl.pallas_call(
        body, name=name, grid=grid, in_specs=[i.spec() for i in ins], out_specs=[o.spec() for o in outs],
        out_shape=[jax.ShapeDtypeStruct(o.shape, o.dtype) for o in outs], compiler_params=_params(len(grid)),
    )(*[i.arr for i in ins])
    return list(res)


def seg_bwd(name, f, grid, ins, outs, cts, adds=None, row_block=None):
    n_in, n_ct = len(ins), len(cts)
    grad_idx = [k for k, i in enumerate(ins) if i.diff or i.weight]
    adds = adds or {}
    add_keys = sorted(adds)
    add_list = [adds[k] for k in add_keys]
    heads = {k: (ins[k].arr.shape[0] if ins[k].arr.ndim == 3 else 1) for k in grad_idx if ins[k].weight}
    tile_rows = outs[0].block[0]
    blocks = [None] if row_block is None else [pl.ds(r0, row_block) for r0 in range(0, tile_rows, row_block)]

    def body(*refs):
        in_refs, ct_refs = refs[:n_in], refs[n_in:n_in + n_ct]
        add_refs = dict(zip(add_keys, refs[n_in + n_ct:n_in + n_ct + len(add_list)]))
        g_refs = dict(zip(grad_idx, refs[n_in + n_ct + len(add_list):]))
        for k in grad_idx:
            if ins[k].weight or ins[k].acc is not None:
                @pl.when(_first(range(len(grid)) if ins[k].weight else ins[k].acc))
                def _(k=k):
                    g_refs[k][...] = jnp.zeros_like(g_refs[k])

        for rows in blocks:
            at = lambda ref, sliced: ref[...] if rows is None or not sliced else ref[rows, :]
            vals = [None if i.weight else at(r, i.rows) for i, r in zip(ins, in_refs)]
            primals, owner = [], []
            for k in grad_idx:
                if ins[k].weight:
                    for h in range(heads[k]):
                        primals.append(jnp.zeros(ins[k].arr.shape[-2:], F32))
                        owner.append((k, h))
                else:
                    primals.append(vals[k].astype(F32))
                    owner.append((k, None))

            def g(*dv, vals=vals, owner=owner):
                full = list(vals)
                zeros = {}
                for (k, h), v in zip(owner, dv):
                    if h is None:
                        full[k] = v
                    else:
                        zeros.setdefault(k, []).append(v)
                for k, z in zeros.items():
                    full[k] = Wt(in_refs[k], z)
                return tuple(f(*full))

            _, pull = jax.vjp(g, *primals)
            grads = pull(tuple(at(c, True).astype(F32) for c in ct_refs))
            for (k, h), gr in zip(owner, grads):
                if ins[k].weight:
                    if ins[k].arr.ndim == 3:
                        g_refs[k][h] += gr
                    else:
                        g_refs[k][...] += gr
                    continue
                if k in add_refs:
                    gr = gr + at(add_refs[k], True).astype(F32)
                if ins[k].acc is not None:
                    g_refs[k][...] += gr
                elif rows is None or not ins[k].rows:
                    g_refs[k][...] = gr.astype(g_refs[k].dtype)
                else:
                    g_refs[k][rows, :] = gr.astype(g_refs[k].dtype)

    g_specs, g_shapes = [], []
    for k in grad_idx:
        i = ins[k]
        if i.weight:
            g_specs.append(pl.BlockSpec(memory_space=pltpu.VMEM))
            g_shapes.append(jax.ShapeDtypeStruct(i.arr.shape, F32))
            continue
        shape, block, imap = i.grad if i.grad is not None else (i.arr.shape, i.block, i.imap)
        g_specs.append(pl.BlockSpec(block, imap))
        g_shapes.append(jax.ShapeDtypeStruct(shape, F32))
    in_specs = ([i.spec() for i in ins] + [o.spec() for o in outs]
                + [pl.BlockSpec(ins[k].block, ins[k].imap) for k in add_keys])
    res = pl.pallas_call(
        body, name=name, grid=grid, in_specs=in_specs, out_specs=g_specs, out_shape=g_shapes,
        compiler_params=_params(len(grid)),
    )(*[i.arr for i in ins], *cts, *add_list)
    return list(res)


def _rms(x, g):
    return x * lax.rsqrt(jnp.mean(x * x, axis=-1, keepdims=True) + EPS) * g


@jax.custom_vjp
def _bdot_nt(a, b):
    return lax.dot_general(a.astype(BF16), b.astype(BF16), (((1,), (1,)), ((), ())), preferred_element_type=F32)


def _bdot_nt_fwd(a, b):
    return _bdot_nt(a, b), (a, b)


def _bdot_nt_bwd(res, g):
    a, b = res
    gb = g.astype(BF16)
    da = lax.dot_general(gb, b.astype(BF16), (((1,), (0,)), ((), ())), preferred_element_type=F32)
    db = lax.dot_general(gb, a.astype(BF16), (((0,), (0,)), ((), ())), preferred_element_type=F32)
    return da, db


_bdot_nt.defvjp(_bdot_nt_fwd, _bdot_nt_bwd)


@jax.custom_vjp
def _bdot_nn(a, b):
    return lax.dot_general(a.astype(BF16), b.astype(BF16), (((1,), (0,)), ((), ())), preferred_element_type=F32)


def _bdot_nn_fwd(a, b):
    return _bdot_nn(a, b), (a, b)


def _bdot_nn_bwd(res, g):
    a, b = res
    gb = g.astype(BF16)
    da = lax.dot_general(gb, b.astype(BF16), (((1,), (1,)), ((), ())), preferred_element_type=F32)
    db = lax.dot_general(a.astype(BF16), gb, (((0,), (0,)), ((), ())), preferred_element_type=F32)
    return da, db


_bdot_nn.defvjp(_bdot_nn_fwd, _bdot_nn_bwd)


@functools.partial(jax.custom_vjp, nondiff_argnums=(1,))
def _lane_roll(x, shift):
    return pltpu.roll(x, shift, 1)


def _lane_roll_fwd(x, shift):
    return pltpu.roll(x, shift, 1), None


def _lane_roll_bwd(shift, _, g):
    return (pltpu.roll(g, (g.shape[1] - shift) % g.shape[1], 1),)


_lane_roll.defvjp(_lane_roll_fwd, _lane_roll_bwd)


def _hp_dot(a, b):
    return jnp.dot(a, b, precision=lax.Precision.HIGHEST, preferred_element_type=F32)


NT_DIMS = (((1,), (1,)), ((), ()))
TN_DIMS = (((0,), (0,)), ((), ()))


def ffn_fwd(name, x, gain, wg, wu, wd, target=None, tm=FFN_WIDE_TILE):
    S, D = x.shape
    tm = min(tm, S)
    nsh, Fs, _ = wg.shape
    with_loss = target is not None

    def body(*refs):
        if with_loss:
            x_ref, gain_ref, wg_ref, wu_ref, wd_ref, t_ref, xo_ref, h_ref, g_ref, u_ref, part_ref, acc = refs
        else:
            x_ref, gain_ref, wg_ref, wu_ref, wd_ref, xo_ref, h_ref, g_ref, u_ref, acc = refs
        j = pl.program_id(1)

        @pl.when(j == 0)
        def _():
            h_ref[...] = _rms(x_ref[...], gain_ref[...]).astype(BF16)
            acc[...] = jnp.zeros_like(acc)

        h = h_ref[...]
        g = lax.dot_general(h, wg_ref[...], NT_DIMS, preferred_element_type=F32)
        u = lax.dot_general(h, wu_ref[...], NT_DIMS, preferred_element_type=F32)
        g_ref[...] = g.astype(BF16)
        u_ref[...] = u.astype(BF16)
        a = g * jax.nn.sigmoid(g) * u
        acc[...] += jnp.dot(a.astype(BF16), wd_ref[...], preferred_element_type=F32)

        @pl.when(j == nsh - 1)
        def _():
            y = x_ref[...] + 0.5 * acc[...]
            if with_loss:
                err = y - t_ref[...]
                xo_ref[...] = err * (1.0 / D)
                part_ref[...] = jnp.full(part_ref.shape, 0.5 * jnp.sum(jnp.mean(err * err, axis=-1)), F32)
            else:
                xo_ref[...] = y

    rows = pl.BlockSpec((tm, D), lambda i, j: (i, 0))
    wspec = pl.BlockSpec((None, Fs, D), lambda i, j: (j, 0, 0))
    act = pl.BlockSpec((None, tm, Fs), lambda i, j: (j, i, 0))
    in_specs, args = [rows, pl.BlockSpec((1, D), lambda i, j: (0, 0)), wspec, wspec, wspec], [x, gain, wg, wu, wd]
    out_specs = [rows, rows, act, act]
    out_shape = [jax.ShapeDtypeStruct((S, D), F32), jax.ShapeDtypeStruct((S, D), BF16),
                 jax.ShapeDtypeStruct((nsh, S, Fs), BF16), jax.ShapeDtypeStruct((nsh, S, Fs), BF16)]
    if with_loss:
        in_specs.append(rows)
        args.append(target)
        out_specs.append(pl.BlockSpec((SUBLANES, 128), lambda i, j: (i, 0)))
        out_shape.append(jax.ShapeDtypeStruct((S // tm * SUBLANES, 128), F32))
    return pl.pallas_call(
        body, name=name, grid=(S // tm, nsh), in_specs=in_specs, out_specs=out_specs, out_shape=out_shape,
        scratch_shapes=[pltpu.VMEM((tm, D), F32)], compiler_params=_params(2),
    )(*args)


def ffn_bwd_act(name, dxo, x, gain, g, u, wg, wu, wd, tm=TOKEN_TILE):
    S, D = x.shape
    tm = min(tm, S)
    nsh, Fs, _ = wg.shape

    def body(dxo_ref, x_ref, gain_ref, g_ref, u_ref, wg_ref, wu_ref, wd_ref, dx_ref, dgain_ref, dg_ref, du_ref, dd, dh):
        i, j = pl.program_id(0), pl.program_id(1)

        @pl.when(j == 0)
        def _():
            dd[...] = (0.5 * dxo_ref[...]).astype(BF16)
            dh[...] = jnp.zeros_like(dh)

        for s in range(FFN_SHARDS_PER_STEP):
            for r0 in range(0, tm, FFN_ROW_BLOCK):
                rows = pl.ds(r0, FFN_ROW_BLOCK)
                da = lax.dot_general(dd[rows, :], wd_ref[s], NT_DIMS, preferred_element_type=F32)
                gf, uf = g_ref[s, rows, :].astype(F32), u_ref[s, rows, :].astype(F32)
                sig = jax.nn.sigmoid(gf)
                dgv = (da * uf * (sig * (1.0 + gf * (1.0 - sig)))).astype(BF16)
                duv = (da * (gf * sig)).astype(BF16)
                dg_ref[s, rows, :] = dgv
                du_ref[s, rows, :] = duv
                dh[rows, :] += (jnp.dot(dgv, wg_ref[s], preferred_element_type=F32)
                                + jnp.dot(duv, wu_ref[s], preferred_element_type=F32))

        @pl.when(j == nsh // FFN_SHARDS_PER_STEP - 1)
        def _():
            xv = x_ref[...]
            r = lax.rsqrt(jnp.mean(xv * xv, axis=-1, keepdims=True) + EPS)
            xhat = xv * r
            dhv = dh[...]
            dxn = dhv * gain_ref[...]
            dx_ref[...] = dxo_ref[...] + r * (dxn - xhat * jnp.mean(dxn * xhat, axis=-1, keepdims=True))
            part = jnp.sum(dhv * xhat, axis=0, keepdims=True)

            @pl.when(i == 0)
            def _():
                dgain_ref[...] = part

            @pl.when(i != 0)
            def _():
                dgain_ref[...] += part

    per = FFN_SHARDS_PER_STEP
    act = pl.BlockSpec((per, tm, Fs), lambda i, j: (j, i, 0))
    wspec = pl.BlockSpec((per, Fs, D), lambda i, j: (j, 0, 0))
    return pl.pallas_call(
        body, name=name, grid=(S // tm, nsh // per),
        in_specs=[pl.BlockSpec((tm, D), lambda i, j: (i, 0)), pl.BlockSpec((tm, D), lambda i, j: (i, 0)),
                  pl.BlockSpec((1, D), lambda i, j: (0, 0)), act, act, wspec, wspec, wspec],
        out_specs=[pl.BlockSpec((tm, D), lambda i, j: (i, 0)), pl.BlockSpec((1, D), lambda i, j: (0, 0)), act, act],
        out_shape=[jax.ShapeDtypeStruct((S, D), F32), jax.ShapeDtypeStruct((1, D), F32),
                   jax.ShapeDtypeStruct((nsh, S, Fs), BF16), jax.ShapeDtypeStruct((nsh, S, Fs), BF16)],
        scratch_shapes=[pltpu.VMEM((tm, D), BF16), pltpu.VMEM((tm, D), F32)], compiler_params=_params(2),
    )(dxo, x, gain, g, u, wg, wu, wd)


def ffn_bwd_w(name, h, dxo, g, u, dg, du, tm=FFN_WIDE_TILE):
    S, D = h.shape
    tm = min(tm, S)
    nsh, _, Fs = g.shape

    def body(h_ref, dxo_ref, g_ref, u_ref, dg_ref, du_ref, dwg_ref, dwu_ref, dwd_ref):
        i = pl.program_id(1)
        gf, uf = g_ref[...].astype(F32), u_ref[...].astype(F32)
        a = (gf * jax.nn.sigmoid(gf) * uf).astype(BF16)
        dd = (0.5 * dxo_ref[...]).astype(BF16)
        hv = h_ref[...]

        @pl.when(i == 0)
        def _():
            dwg_ref[...] = jnp.zeros_like(dwg_ref)
            dwu_ref[...] = jnp.zeros_like(dwu_ref)
            dwd_ref[...] = jnp.zeros_like(dwd_ref)

        dwg_ref[...] += lax.dot_general(dg_ref[...], hv, TN_DIMS, preferred_element_type=F32)
        dwu_ref[...] += lax.dot_general(du_ref[...], hv, TN_DIMS, preferred_element_type=F32)
        dwd_ref[...] += lax.dot_general(a, dd, TN_DIMS, preferred_element_type=F32)

    act = pl.BlockSpec((None, tm, Fs), lambda j, i: (j, i, 0))
    wspec = pl.BlockSpec((None, Fs, D), lambda j, i: (j, 0, 0))
    return pl.pallas_call(
        body, name=name, grid=(nsh, S // tm),
        in_specs=[pl.BlockSpec((tm, D), lambda j, i: (i, 0)), pl.BlockSpec((tm, D), lambda j, i: (i, 0)), act, act, act, act],
        out_specs=[wspec, wspec, wspec], out_shape=[jax.ShapeDtypeStruct((nsh, Fs, D), F32)] * 3,
        compiler_params=_params(2),
    )(h, dxo, g, u, dg, du)


NEG_BIG = -1e30


def _causal_pairs(n, by_key):
    pairs = [(qi, ki) for qi in range(n) for ki in range(qi + 1)]
    if by_key:
        pairs.sort(key=lambda p: (p[1], p[0]))
    return jnp.asarray([p[0] for p in pairs], jnp.int32), jnp.asarray([p[1] for p in pairs], jnp.int32)


def _scores(q, k, masked):
    s = lax.dot_general(q, k, (((1,), (1,)), ((), ())), preferred_element_type=F32)
    if masked:
        row = lax.broadcasted_iota(jnp.int32, s.shape, 0)
        col = lax.broadcasted_iota(jnp.int32, s.shape, 1)
        s = jnp.where(row >= col, s, NEG_BIG)
    return s


def attn_fwd(name, q, k, v, t=ATTN_TILE):
    S, Dk = q.shape[0], HEAD_PAD
    H = q.shape[1] // Dk
    Dv = v.shape[1] // H
    qt, kt = _causal_pairs(S // t, by_key=False)

    def body(qt_ref, kt_ref, q_ref, k_ref, v_ref, o_ref, lse_ref, m_sc, l_sc, acc):
        qi, ki = qt_ref[pl.program_id(1)], kt_ref[pl.program_id(1)]

        @pl.when(ki == 0)
        def _():
            m_sc[...] = jnp.full_like(m_sc, NEG_BIG)
            l_sc[...] = jnp.zeros_like(l_sc)
            acc[...] = jnp.zeros_like(acc)

        def step(masked):
            s = _scores(q_ref[...], k_ref[...], masked)
            m_prev = m_sc[...]
            m_next = jnp.maximum(m_prev, jnp.max(s, axis=-1, keepdims=True))
            alpha = jnp.exp(m_prev - m_next)
            p = jnp.exp(s - jnp.tile(m_next, (1, t // STAT_LANES)))
            l_sc[...] = alpha * l_sc[...] + jnp.sum(p, axis=-1, keepdims=True)
            acc[...] = alpha * acc[...] + jnp.dot(p.astype(BF16), v_ref[...].astype(BF16), preferred_element_type=F32)
            m_sc[...] = m_next

        @pl.when(ki < qi)
        def _():
            step(False)

        @pl.when(ki == qi)
        def _():
            step(True)
            o_ref[...] = acc[...] / l_sc[...]
            lse_ref[...] = m_sc[...] + jnp.log(l_sc[...])

    stat = pltpu.VMEM((t, STAT_LANES), F32)
    return pl.pallas_call(
        body, name=name,
        grid_spec=pltpu.PrefetchScalarGridSpec(
            num_scalar_prefetch=2, grid=(H, qt.shape[0]),
            in_specs=[pl.BlockSpec((t, Dk), lambda h, s, qt, kt: (qt[s], h)),
                      pl.BlockSpec((t, Dk), lambda h, s, qt, kt: (kt[s], h)),
                      pl.BlockSpec((t, Dv), lambda h, s, qt, kt: (kt[s], h))],
            out_specs=[pl.BlockSpec((t, Dv), lambda h, s, qt, kt: (qt[s], h)),
                       pl.BlockSpec((None, t, STAT_LANES), lambda h, s, qt, kt: (h, qt[s], 0))],
            scratch_shapes=[stat, stat, pltpu.VMEM((t, Dv), F32)]),
        out_shape=[jax.ShapeDtypeStruct((S, H * Dv), F32), jax.ShapeDtypeStruct((H, S, STAT_LANES), F32)],
        compiler_params=_params(2),
    )(qt, kt, q, k, v)


def attn_bwd(name, q, k, v, do, o, lse, t=ATTN_TILE):
    S, Dk = q.shape[0], HEAD_PAD
    H = q.shape[1] // Dk
    Dv = v.shape[1] // H
    qt, kt = _causal_pairs(S // t, by_key=True)
    tn_dims = (((0,), (0,)), ((), ()))

    def body(qt_ref, kt_ref, q_ref, k_ref, v_ref, do_ref, o_ref, lse_ref, dq_ref, dk_ref, dv_ref):
        step_id = pl.program_id(1)
        qi, ki = qt_ref[step_id], kt_ref[step_id]

        @pl.when(step_id == 0)
        def _():
            dq_ref[...] = jnp.zeros_like(dq_ref)

        def step(masked):
            s = _scores(q_ref[...], k_ref[...], masked)
            reps = (1, t // STAT_LANES)
            p = jnp.exp(s - jnp.tile(lse_ref[...], reps))
            dov = do_ref[...]
            delta = jnp.broadcast_to(jnp.sum(dov * o_ref[...], axis=-1, keepdims=True), (t, STAT_LANES))
            dob = dov.astype(BF16)
            dp = lax.dot_general(dob, v_ref[...].astype(BF16), (((1,), (1,)), ((), ())), preferred_element_type=F32)
            ds = (p * (dp - jnp.tile(delta, reps))).astype(BF16)
            pdv = lax.dot_general(p.astype(BF16), dob, tn_dims, preferred_element_type=F32)
            pdk = lax.dot_general(ds, q_ref[...], tn_dims, preferred_element_type=F32)
            rows = pl.ds(pl.multiple_of(qi * t, t), t)
            dq_ref[rows, :] += jnp.dot(ds, k_ref[...], preferred_element_type=F32)
            return pdk, pdv

        @pl.when(ki == qi)
        def _():
            dk_ref[...] = jnp.zeros_like(dk_ref)
            dv_ref[...] = jnp.zeros_like(dv_ref)

        def accumulate(masked):
            pdk, pdv = step(masked)
            dk_ref[...] += pdk
            dv_ref[...] += pdv

        @pl.when(ki == qi)
        def _():
            accumulate(True)

        @pl.when(ki < qi)
        def _():
            accumulate(False)

    qrow = lambda h, s, qt, kt: (qt[s], h)
    krow = lambda h, s, qt, kt: (kt[s], h)
    return pl.pallas_call(
        body, name=name,
        grid_spec=pltpu.PrefetchScalarGridSpec(
            num_scalar_prefetch=2, grid=(H, qt.shape[0]),
            in_specs=[pl.BlockSpec((t, Dk), qrow), pl.BlockSpec((t, Dk), krow), pl.BlockSpec((t, Dv), krow),
                      pl.BlockSpec((t, Dv), qrow), pl.BlockSpec((t, Dv), qrow),
                      pl.BlockSpec((None, t, STAT_LANES), lambda h, s, qt, kt: (h, qt[s], 0))],
            out_specs=[pl.BlockSpec((S, Dk), lambda h, s, qt, kt: (0, h)), pl.BlockSpec((t, Dk), krow),
                       pl.BlockSpec((t, Dv), krow)]),
        out_shape=[jax.ShapeDtypeStruct((S, H * Dk), F32), jax.ShapeDtypeStruct((S, H * Dk), F32),
                   jax.ShapeDtypeStruct((S, H * Dv), F32)],
        compiler_params=_params(2),
    )(qt, kt, q, k, v, do, o, lse)


def _cmul(ar, ai, br, bi):
    return ar * br - ai * bi, ar * bi + ai * br


def _scan_tile(x_r, x_i, ar_ref, ai_ref, cr_sc, ci_sc, *, reverse, first, states=None):
    tc, lanes = x_r.shape
    nblk, lb = tc // SUBLANES, SCAN_LANE_TILE
    with_da = states is not None
    if with_da:
        xr_all, xi_all, pr_all, pi_all, dar_all, dai_all, chunk = states

    @pl.when(first)
    def _():
        cr_sc[...] = jnp.zeros_like(cr_sc)
        ci_sc[...] = jnp.zeros_like(ci_sc)
        if with_da:
            dar_all[...] = jnp.zeros_like(dar_all)
            dai_all[...] = jnp.zeros_like(dai_all)

    row = lax.broadcasted_iota(jnp.int32, (SUBLANES, lb), 0)
    for l0 in range(0, lanes, lb):
        _scan_lanes(x_r.at[:, pl.ds(l0, lb)], x_i.at[:, pl.ds(l0, lb)], ar_ref[0:1, pl.ds(l0, lb)],
                    ai_ref[0:1, pl.ds(l0, lb)], cr_sc.at[:, pl.ds(l0, lb)], ci_sc.at[:, pl.ds(l0, lb)], row, reverse,
                    nblk, None if not with_da else tuple(r.at[:, pl.ds(l0, lb)] for r in states[:6]) + (chunk,))


def _scan_lanes(x_r, x_i, a1r, a1i, cr_sc, ci_sc, row, reverse, nblk, states):
    lb = x_r.shape[1]
    with_da = states is not None
    if with_da:
        xr_ref, xi_ref, pr_ref, pi_ref, dar_ref, dai_ref, chunk = states
    if reverse:
        a1i = -a1i
    a2r, a2i = _cmul(a1r, a1i, a1r, a1i)
    a4r, a4i = _cmul(a2r, a2i, a2r, a2i)
    pw_r, pw_i = jnp.zeros((SUBLANES, lb), F32), jnp.zeros((SUBLANES, lb), F32)
    cur_r, cur_i = a1r, a1i
    for e in range(SUBLANES):
        r_at = (SUBLANES - 1 - e) if reverse else e
        pw_r = jnp.where(row == r_at, cur_r, pw_r)
        pw_i = jnp.where(row == r_at, cur_i, pw_i)
        cur_r, cur_i = _cmul(cur_r, cur_i, a1r, a1i)
    steps = []
    for d, pr, pi in ((1, a1r, a1i), (2, a2r, a2i), (4, a4r, a4i)):
        keep = (row < SUBLANES - d) if reverse else (row >= d)
        steps.append((d, jnp.where(keep, pr, 0.0), jnp.where(keep, pi, 0.0)))

    def block(jb, carry):
        if with_da:
            cr, ci, acc_r, acc_i = carry
        else:
            cr, ci = carry
        idx = (nblk - 1 - jb) if reverse else jb
        r0 = pl.multiple_of(idx * SUBLANES, SUBLANES)
        xr = x_r[pl.ds(r0, SUBLANES), :]
        xi = x_i[pl.ds(r0, SUBLANES), :]
        for d, pr, pi in steps:
            shift = SUBLANES - d if reverse else d
            sr, si = pltpu.roll(xr, shift, 0), pltpu.roll(xi, shift, 0)
            xr, xi = xr + pr * sr - pi * si, xi + pr * si + pi * sr
        xr, xi = xr + pw_r * cr - pw_i * ci, xi + pw_r * ci + pw_i * cr
        x_r[pl.ds(r0, SUBLANES), :] = xr
        x_i[pl.ds(r0, SUBLANES), :] = xi
        edge = 0 if reverse else SUBLANES - 1
        cr, ci = xr[edge:edge + 1, :], xi[edge:edge + 1, :]
        if not with_da:
            return cr, ci
        fr = xr_ref[pl.ds(r0, SUBLANES), :]
        fi = xi_ref[pl.ds(r0, SUBLANES), :]
        rp = pl.multiple_of(jnp.maximum(idx - 1, 0) * SUBLANES, SUBLANES)
        inside = idx > 0
        before_r = jnp.where(inside, xr_ref[pl.ds(rp, SUBLANES), :], pr_ref[...])
        before_i = jnp.where(inside, xi_ref[pl.ds(rp, SUBLANES), :], pi_ref[...])
        live = jnp.where(jnp.logical_or(inside, chunk > 0), 1.0, 0.0)
        last_r = before_r[SUBLANES - 1:SUBLANES, :] * live
        last_i = before_i[SUBLANES - 1:SUBLANES, :] * live
        pvr = jnp.where(row == 0, last_r, pltpu.roll(fr, 1, 0))
        pvi = jnp.where(row == 0, last_i, pltpu.roll(fi, 1, 0))
        acc_r = acc_r + xr * pvr + xi * pvi
        acc_i = acc_i + xi * pvr - xr * pvi
        return cr, ci, acc_r, acc_i

    init = (cr_sc[...], ci_sc[...])
    if with_da:
        init = init + (jnp.zeros((SUBLANES, lb), F32), jnp.zeros((SUBLANES, lb), F32))
    fin = lax.fori_loop(0, nblk, block, init)
    cr_sc[...] = fin[0]
    ci_sc[...] = fin[1]
    if with_da:
        dar_ref[...] += fin[2]
        dai_ref[...] += fin[3]


SSM_BLOCKS = 4
BLOCK_CH = SSM_WIDTH // SSM_BLOCKS
PREP_LANES = SSM_LANES // SSM_BLOCKS


def ssm_fwd(name, u, bb_r, bb_i, cb_r, ncb_i, a_r8, a_i8):
    S = u.shape[0]
    tc = min(SCAN_TIME_TILE, S)

    def body(u_ref, bbr_ref, bbi_ref, cbr_ref, ncbi_ref, ar_ref, ai_ref, xr_ref, xi_ref, y_ref, cr_sc, ci_sc):
        ub = u_ref[...].astype(BF16)
        xr_ref[...] = jnp.dot(ub, bbr_ref[...], preferred_element_type=F32)
        xi_ref[...] = jnp.dot(ub, bbi_ref[...], preferred_element_type=F32)
        _scan_tile(xr_ref, xi_ref, ar_ref, ai_ref, cr_sc, ci_sc, reverse=False, first=pl.program_id(1) == 0)
        y_ref[...] = (lax.dot_general(xr_ref[...].astype(BF16), cbr_ref[...], NT_DIMS, preferred_element_type=F32)
                      + lax.dot_general(xi_ref[...].astype(BF16), ncbi_ref[...], NT_DIMS, preferred_element_type=F32))

    ch = pl.BlockSpec((tc, BLOCK_CH), lambda b, t: (t, b))
    st = pl.BlockSpec((tc, PREP_LANES), lambda b, t: (t, b))
    wt = pl.BlockSpec((None, BLOCK_CH, PREP_LANES), lambda b, t: (b, 0, 0))
    par = pl.BlockSpec((SUBLANES, PREP_LANES), lambda b, t: (0, b))
    return pl.pallas_call(
        body, name=name, grid=(SSM_BLOCKS, S // tc), in_specs=[ch, wt, wt, wt, wt, par, par], out_specs=[st, st, ch],
        out_shape=[jax.ShapeDtypeStruct((S, SSM_LANES), F32), jax.ShapeDtypeStruct((S, SSM_LANES), F32),
                   jax.ShapeDtypeStruct((S, SSM_WIDTH), F32)],
        scratch_shapes=[pltpu.VMEM((1, PREP_LANES), F32), pltpu.VMEM((1, PREP_LANES), F32)], compiler_params=_params(2),
    )(u, bb_r, bb_i, cb_r, ncb_i, a_r8, a_i8)


def ssm_bwd(name, dy, du_add, u, xs_r, xs_i, bb_r, bb_i, cb_r, ncb_i, a_r8, a_i8):
    S = u.shape[0]
    tc = min(SCAN_TIME_TILE, S)
    nt = S // tc

    def body(dy_ref, dua_ref, u_ref, xr_ref, xi_ref, pr_ref, pi_ref, bbr_ref, bbi_ref, cbr_ref, ncbi_ref, ar_ref, ai_ref,
             du_ref, dbbr_ref, dbbi_ref, dcbr_ref, dncbi_ref, dar_ref, dai_ref, lr_sc, li_sc, cr_sc, ci_sc):
        t = pl.program_id(1)
        first = t == 0

        @pl.when(first)
        def _():
            for r in (dbbr_ref, dbbi_ref, dcbr_ref, dncbi_ref):
                r[...] = jnp.zeros_like(r)

        dyb = dy_ref[...].astype(BF16)
        lr_sc[...] = jnp.dot(dyb, cbr_ref[...], preferred_element_type=F32)
        li_sc[...] = jnp.dot(dyb, ncbi_ref[...], preferred_element_type=F32)
        _scan_tile(lr_sc, li_sc, ar_ref, ai_ref, cr_sc, ci_sc, reverse=True, first=first,
                   states=(xr_ref, xi_ref, pr_ref, pi_ref, dar_ref, dai_ref, nt - 1 - t))
        lrb, lib = lr_sc[...].astype(BF16), li_sc[...].astype(BF16)
        du_ref[...] = (lax.dot_general(lrb, bbr_ref[...], NT_DIMS, preferred_element_type=F32)
                       + lax.dot_general(lib, bbi_ref[...], NT_DIMS, preferred_element_type=F32) + dua_ref[...])
        ub = u_ref[...].astype(BF16)
        dbbr_ref[...] += lax.dot_general(ub, lrb, TN_DIMS, preferred_element_type=F32)
        dbbi_ref[...] += lax.dot_general(ub, lib, TN_DIMS, preferred_element_type=F32)
        dcbr_ref[...] += lax.dot_general(dyb, xr_ref[...].astype(BF16), TN_DIMS, preferred_element_type=F32)
        dncbi_ref[...] += lax.dot_general(dyb, xi_ref[...].astype(BF16), TN_DIMS, preferred_element_type=F32)

    ch = pl.BlockSpec((tc, BLOCK_CH), lambda b, t: (nt - 1 - t, b))
    st = pl.BlockSpec((tc, PREP_LANES), lambda b, t: (nt - 1 - t, b))
    prev = pl.BlockSpec((SUBLANES, PREP_LANES), lambda b, t: (jnp.maximum((nt - 1 - t) * (tc // SUBLANES) - 1, 0), b))
    wt = pl.BlockSpec((None, BLOCK_CH, PREP_LANES), lambda b, t: (b, 0, 0))
    par = pl.BlockSpec((SUBLANES, PREP_LANES), lambda b, t: (0, b))
    blk = jax.ShapeDtypeStruct((SSM_BLOCKS, BLOCK_CH, PREP_LANES), F32)
    rows8 = jax.ShapeDtypeStruct((SUBLANES, SSM_LANES), F32)
    return pl.pallas_call(
        body, name=name, grid=(SSM_BLOCKS, nt), in_specs=[ch, ch, ch, st, st, prev, prev, wt, wt, wt, wt, par, par],
        out_specs=[ch, wt, wt, wt, wt, par, par],
        out_shape=[jax.ShapeDtypeStruct((S, SSM_WIDTH), F32), blk, blk, blk, blk, rows8, rows8],
        scratch_shapes=[pltpu.VMEM((tc, PREP_LANES), F32), pltpu.VMEM((tc, PREP_LANES), F32),
                        pltpu.VMEM((1, PREP_LANES), F32), pltpu.VMEM((1, PREP_LANES), F32)],
        compiler_params=_params(2),
    )(dy, du_add, u, xs_r, xs_i, xs_r, xs_i, bb_r, bb_i, cb_r, ncb_i, a_r8, a_i8)


def _ssm_prep_f(a_re, a_im, log_dt, bt_re, bt_im, c_re, c_im):
    first_group = pl.program_id(0) * (SSM_GROUPS // SSM_BLOCKS)
    iota = lambda shape, d: lax.broadcasted_iota(jnp.int32, shape, d)
    grp_of_row = lambda shape: iota(shape, 0) >> int(math.log2(SSM_GROUP))
    grp_of_lane = lambda shape: iota(shape, 1) >> int(math.log2(SSM_STATE))
    rep = (grp_of_row((BLOCK_CH, SSM_GROUPS)) + first_group == iota((BLOCK_CH, SSM_GROUPS), 1)).astype(F32)
    til = ((iota((SSM_STATE, PREP_LANES), 1) & (SSM_STATE - 1)) == iota((SSM_STATE, PREP_LANES), 0)).astype(F32)
    m_rows = (grp_of_row((BLOCK_CH, PREP_LANES)) == grp_of_lane((BLOCK_CH, PREP_LANES))).astype(F32)
    m_grp = (iota((SSM_GROUPS, PREP_LANES), 0) == grp_of_lane((SSM_GROUPS, PREP_LANES)) + first_group).astype(F32)
    dt = jnp.exp(log_dt)
    decay = jnp.exp(a_re * dt)
    ar = decay * jnp.cos(a_im * dt)
    ai = decay * jnp.sin(a_im * dt)
    den = a_re * a_re + a_im * a_im
    nr = ar - 1.0
    coef_r = (nr * a_re + ai * a_im) / den
    coef_i = (ai * a_re - nr * a_im) / den
    cr, ci = _hp_dot(rep, coef_r), _hp_dot(rep, coef_i)
    bb_r = cr * bt_re - ci * bt_im
    bb_i = cr * bt_im + ci * bt_re
    big = lambda m: _hp_dot(m, til) * m_rows
    lanes = lambda m: jnp.broadcast_to(jnp.sum(_hp_dot(m, til) * m_grp, axis=0, keepdims=True), (SUBLANES, PREP_LANES))
    return lanes(ar), lanes(ai), big(bb_r), big(bb_i), big(c_re), -big(c_im)


def _whole(arr, **kw):
    nd = arr.ndim
    return In(arr, arr.shape, lambda *_: (0,) * nd, **kw)


def _adamw_math(w, g, m, v):
    m = ADAM_B1 * m + (1.0 - ADAM_B1) * g
    v = ADAM_B2 * v + (1.0 - ADAM_B2) * (g * g)
    m_hat = m / (1.0 - ADAM_B1 ** ADAM_STEP)
    v_hat = v / (1.0 - ADAM_B2 ** ADAM_STEP)
    delta = -ADAM_LR * (m_hat / (jnp.sqrt(v_hat) + ADAM_EPS) + ADAM_WD * w)
    return delta, m, v


def adamw_big(name, w, g, m, v, after):
    R, C = w.shape
    tr = R
    for cand in (512, 344, 256, 128):
        if R % cand == 0:
            tr = cand
            break

    def body(w_ref, g_ref, m_ref, v_ref, after_ref, d_ref, nm_ref, nv_ref):
        d, nm, nv = _adamw_math(w_ref[...], g_ref[...], m_ref[...], v_ref[...])
        d_ref[...] = d
        nm_ref[...] = nm
        nv_ref[...] = nv

    spec = pl.BlockSpec((tr, C), lambda i: (i, 0))
    return pl.pallas_call(
        body, name=name, grid=(R // tr,), in_specs=[spec] * 4 + [pl.BlockSpec(memory_space=pl.ANY)], out_specs=[spec] * 3,
        out_shape=[jax.ShapeDtypeStruct((R, C), F32)] * 3, compiler_params=_params(1),
    )(w, g, m, v, after)


def adamw_small(name, ws, gs, ms, vs, after):
    n = len(ws)

    def body(*refs):
        outs = refs[4 * n + 1:]
        for k in range(n):
            d, nm, nv = _adamw_math(refs[k][...], refs[n + k][...], refs[2 * n + k][...], refs[3 * n + k][...])
            outs[k][...] = d
            outs[n + k][...] = nm
            outs[2 * n + k][...] = nv

    vm = pl.BlockSpec(memory_space=pltpu.VMEM)
    shapes = [jax.ShapeDtypeStruct(w.shape, F32) for w in ws]
    res = pl.pallas_call(
        body, name=name, in_specs=[vm] * (4 * n) + [pl.BlockSpec(memory_space=pl.ANY)], out_specs=[vm] * (3 * n),
        out_shape=shapes * 3, compiler_params=pltpu.CompilerParams(vmem_limit_bytes=VMEM_LIMIT_BYTES),
    )(*ws, *gs, *ms, *vs, after)
    return res[:n], res[n:2 * n], res[2 * n:]


def _place():
    return lax.axis_index("x"), lax.axis_index("y"), lax.axis_index("c")


def _other_chips(x, y):
    return [(1 - x, y), (x, 1 - y), (1 - x, 1 - y)]


HBM = pl.BlockSpec(memory_space=pl.ANY)


def all_gather_halves(name, blocks):
    n = len(blocks)

    def body(*refs):
        in_refs, out_refs = refs[:n], refs[n:2 * n]
        send_sems, recv_sems = refs[2 * n:]
        x, y, c = _place()
        me, sibling = (x, y, c), (x, y, 1 - c)
        chips = _other_chips(x, y)

        def slot(a, px, py, pc):
            return out_refs[a].at[4 * px + 2 * py + pc]

        def copy(a, k, block, to, src=None):
            return pltpu.make_async_remote_copy(
                src_ref=slot(a, *block) if src is None else src, dst_ref=slot(a, *block),
                send_sem=send_sems.at[a, k], recv_sem=recv_sems.at[a, k], device_id=to, device_id_type=MESH)

        first = []
        for a in range(n):
            first.append(copy(a, 0, me, sibling, src=in_refs[a]))
            first += [copy(a, 1 + j, me, (*chip, c), src=in_refs[a]) for j, chip in enumerate(chips)]
        for cp in first:
            cp.start()
        passed = []
        for j, chip in enumerate(chips):
            for a in range(n):
                copy(a, 1 + j, (*chip, c), me).wait_recv()
                fw = copy(a, 4 + j, (*chip, c), sibling)
                fw.start()
                passed.append(fw)
        for a in range(n):
            copy(a, 0, sibling, me).wait_recv()
            for j, chip in enumerate(chips):
                copy(a, 4 + j, (*chip, 1 - c), me).wait_recv()
        for cp in first + passed:
            cp.wait_send()

    res = pl.pallas_call(
        body, name=name, in_specs=[HBM] * n, out_specs=[HBM] * n,
        out_shape=[jax.ShapeDtypeStruct((N_DEV,) + b.shape, b.dtype) for b in blocks],
        scratch_shapes=[pltpu.SemaphoreType.DMA((n, 7)), pltpu.SemaphoreType.DMA((n, 7))],
    )(*blocks)
    return list(res)


def pair_send_halves(name, grads):
    n = len(grads)

    def body(*refs):
        in_refs, out_refs = refs[:n], refs[n:2 * n]
        send_sems, recv_sems = refs[2 * n:]
        x, y, c = _place()
        cps = []
        for a in range(n):
            cp = pltpu.make_async_remote_copy(
                src_ref=in_refs[a].at[:, 1 - c], dst_ref=out_refs[a], send_sem=send_sems.at[a], recv_sem=recv_sems.at[a],
                device_id=(x, y, 1 - c), device_id_type=MESH)
            cp.start()
            cps.append(cp)
        for cp in cps:
            cp.wait()

    res = pl.pallas_call(
        body, name=name, in_specs=[HBM] * n, out_specs=[HBM] * n,
        out_shape=[jax.ShapeDtypeStruct((g.shape[0],) + g.shape[2:], g.dtype) for g in grads],
        scratch_shapes=[pltpu.SemaphoreType.DMA((n,)), pltpu.SemaphoreType.DMA((n,))],
    )(*grads)
    return list(res)


def pair_add(name, grad, got, c_arr):
    nsh, _, M, N = grad.shape
    tr = M
    for cand in (512, 256, 192, 128, 64, 16):
        if M % cand == 0:
            tr = cand
            break

    def body(c_ref, g_ref, p_ref, o_ref):
        o_ref[...] = (g_ref[...] + p_ref[...]).astype(BF16)

    return pl.pallas_call(
        body, name=name,
        grid_spec=pltpu.PrefetchScalarGridSpec(
            num_scalar_prefetch=1, grid=(nsh, M // tr),
            in_specs=[pl.BlockSpec((None, None, tr, N), lambda j, i, c_ref: (j, c_ref[0], i, 0)),
                      pl.BlockSpec((None, tr, N), lambda j, i, c_ref: (j, i, 0))],
            out_specs=pl.BlockSpec((None, tr, N), lambda j, i, c_ref: (j, i, 0))),
        out_shape=jax.ShapeDtypeStruct((nsh, M, N), BF16), compiler_params=_params(2),
    )(c_arr, grad, got)


def sum_chips(name, q):
    nsh, M, N = q.shape
    tr = M
    for cand in (512, 256, 192, 128, 64, 16):
        if M % cand == 0:
            tr = cand
            break

    def body(q_ref, o_ref):
        acc = q_ref[0].astype(F32)
        for j in range(1, nsh):
            acc = acc + q_ref[j].astype(F32)
        o_ref[...] = acc

    return pl.pallas_call(
        body, name=name, grid=(M // tr,), in_specs=[pl.BlockSpec((nsh, tr, N), lambda i: (0, i, 0))],
        out_specs=pl.BlockSpec((tr, N), lambda i: (i, 0)), out_shape=jax.ShapeDtypeStruct((M, N), F32),
        compiler_params=_params(1),
    )(q)


def pair_exchange(name, halves):
    n = len(halves)

    def body(*refs):
        in_refs, out_refs = refs[:n], refs[n:2 * n]
        send_sems, recv_sems = refs[2 * n:]
        x, y, c = _place()
        cps = []
        for a in range(n):
            cp = pltpu.make_async_remote_copy(
                src_ref=in_refs[a], dst_ref=out_refs[a], send_sem=send_sems.at[a], recv_sem=recv_sems.at[a],
                device_id=(x, y, 1 - c), device_id_type=MESH)
            cp.start()
            cps.append(cp)
        for cp in cps:
            cp.wait()

    res = pl.pallas_call(
        body, name=name, in_specs=[HBM] * n, out_specs=[HBM] * n,
        out_shape=[jax.ShapeDtypeStruct(h.shape, h.dtype) for h in halves],
        scratch_shapes=[pltpu.SemaphoreType.DMA((n,)), pltpu.SemaphoreType.DMA((n,))],
    )(*halves)
    return list(res)


SEM = pl.BlockSpec(memory_space=pltpu.SEMAPHORE)
IN_HBM = pl.BlockSpec(memory_space=pltpu.HBM)
SPLIT_COPY = pltpu.CompilerParams(has_side_effects=pltpu.SideEffectType.DATAFLOW_SIDE_EFFECTING)


def _scatter_copies(src_refs, dst_refs, send_sems, recv_sems):
    x, y, c = _place()
    mine = 2 * x + y
    return [pltpu.make_async_remote_copy(
        src_ref=src_refs[a].at[2 * px + py], dst_ref=dst_refs[a].at[mine], send_sem=send_sems.at[a * (N_CHIPS - 1) + k],
        recv_sem=recv_sems.at[a * (N_CHIPS - 1) + k], device_id=(px, py, c), device_id_type=MESH)
        for a in range(len(src_refs)) for k, (px, py) in enumerate(_other_chips(x, y))]


def _pair_copies(src_refs, dst_refs, send_sems, recv_sems):
    x, y, c = _place()
    return [pltpu.make_async_remote_copy(
        src_ref=src_refs[a].at[:, 1 - c], dst_ref=dst_refs[a], send_sem=send_sems.at[a], recv_sem=recv_sems.at[a],
        device_id=(x, y, 1 - c), device_id_type=MESH) for a in range(len(src_refs))]


def _gather_copies(src_refs, dst_refs, send_sems, recv_sems):
    x, y, c = _place()
    me = 4 * x + 2 * y + c
    cps = []
    for a in range(len(src_refs)):
        for k in range(1, N_DEV):
            to = (1 - x if k & 4 else x, 1 - y if k & 2 else y, 1 - c if k & 1 else c)
            s = a * (N_DEV - 1) + k - 1
            cps.append(pltpu.make_async_remote_copy(
                src_ref=src_refs[a], dst_ref=dst_refs[a].at[me], send_sem=send_sems.at[s], recv_sem=recv_sems.at[s],
                device_id=to, device_id_type=MESH))
    return cps


def split_copy_start(name, copies, n_sem, srcs, land_shapes):
    n = len(srcs)
    lands = [lax.empty(s.shape, s.dtype) for s in land_shapes]

    def body(*refs):
        for cp in copies(refs[:n], refs[n:2 * n], refs[2 * n], refs[2 * n + 1]):
            cp.start()
        refs[-1][...] = jnp.zeros_like(refs[-1])

    thru = [pltpu.HBM(a.shape, a.dtype) for a in (*srcs, *lands)]
    res = pl.pallas_call(
        body, name=name, in_specs=[IN_HBM] * (2 * n),
        out_specs=(SEM, SEM, *[IN_HBM] * (2 * n), pl.BlockSpec(memory_space=pltpu.VMEM)),
        out_shape=(pltpu.SemaphoreType.DMA((n * n_sem,)), pltpu.SemaphoreType.DMA((n * n_sem,)), *thru,
                   jax.ShapeDtypeStruct((SUBLANES, LANES), F32)),
        input_output_aliases={i: 2 + i for i in range(2 * n)}, compiler_params=SPLIT_COPY,
    )(*[pltpu.with_memory_space_constraint(a, pltpu.HBM) for a in (*srcs, *lands)])
    return (copies, n, res[0], res[1], res[2:2 + 2 * n]), res[-1][0, 0]


def split_copy_wait(name, handle, after):
    copies, n, send_sems, recv_sems, thru = handle

    def body(*refs):
        for cp in copies(refs[:n], refs[n:2 * n], refs[2 * n], refs[2 * n + 1]):
            cp.wait_send()
            cp.wait_recv()

    res = pl.pallas_call(
        body, name=name, in_specs=[IN_HBM] * (2 * n) + [SEM, SEM, pl.BlockSpec(memory_space=pl.ANY)],
        out_specs=[IN_HBM] * (2 * n), out_shape=[pltpu.HBM(a.shape, a.dtype) for a in thru],
        input_output_aliases={i: i for i in range(2 * n)}, compiler_params=SPLIT_COPY,
    )(*thru, send_sems, recv_sems, after)
    return list(res[:n]), list(res[n:])


def all_reduce_small(name, v):
    R, C = v.shape

    def body(v_ref, o_ref, gath, send_sems, recv_sems):
        x, y, c = _place()
        me, sibling = (x, y, c), (x, y, 1 - c)
        chips = _other_chips(x, y)

        def slot(px, py, pc):
            return gath.at[4 * px + 2 * py + pc]

        def copy(k, block, to, src=None):
            return pltpu.make_async_remote_copy(
                src_ref=slot(*block) if src is None else src, dst_ref=slot(*block),
                send_sem=send_sems.at[k], recv_sem=recv_sems.at[k], device_id=to, device_id_type=MESH)

        first = [copy(0, me, sibling, src=v_ref)]
        first += [copy(1 + j, me, (*chip, c), src=v_ref) for j, chip in enumerate(chips)]
        for cp in first:
            cp.start()
        slot(*me)[...] = v_ref[...]
        passed = [copy(4 + j, (*chip, c), sibling) for j, chip in enumerate(chips)]
        for j, chip in enumerate(chips):
            copy(1 + j, (*chip, c), me).wait_recv()
            passed[j].start()
        copy(0, sibling, me).wait_recv()
        for j, chip in enumerate(chips):
            copy(4 + j, (*chip, 1 - c), me).wait_recv()
        for cp in first + passed:
            cp.wait_send()
        acc = gath[0]
        for d in range(1, N_DEV):
            acc = acc + gath[d]
        o_ref[...] = acc

    vm = pl.BlockSpec(memory_space=pltpu.VMEM)
    return pl.pallas_call(
        body, name=name, in_specs=[vm], out_specs=vm, out_shape=jax.ShapeDtypeStruct((R, C), F32),
        scratch_shapes=[pltpu.VMEM((N_DEV, R, C), F32), pltpu.SemaphoreType.DMA((7,)), pltpu.SemaphoreType.DMA((7,))],
        compiler_params=pltpu.CompilerParams(vmem_limit_bytes=VMEM_LIMIT_BYTES),
    )(v)


LANES = 128
PACK_ROW_MULTIPLE = 1024
SMALL_SHARDED = {
    "w_in": ((D_MODEL, 1216), 1), "mla_w_uq": ((MLA_Q_RANK, 768), 1), "mla_w_ukv": ((MLA_KV_RANK, 1024), 1),
    "ssm_w_glu": ((SSM_WIDTH, SSM_WIDTH), 0), "w_o": ((D_MODEL, D_MODEL), 0), "xattn_w_q": ((D_MODEL, 512), 0),
    "xattn_w_kv": ((D_MODEL, 1024), 0), "xattn_w_o": ((512, D_MODEL), 1),
}
FFN_NAMES = ["ffn1_w_gate", "ffn1_w_up", "ffn1_w_down", "ffn2_w_gate", "ffn2_w_up", "ffn2_w_down"]
TRANSPOSED_VIEW = ("ffn1_w_gate", "ffn1_w_up", "ffn2_w_gate", "ffn2_w_up", "w_in", "mla_w_uq")


def _shard_shape(name):
    (r, cdim), ax = SMALL_SHARDED[name]
    return (r // N_CHIPS, cdim) if ax == 0 else (r, cdim // N_CHIPS)


def _pack_shards(shards):
    parts = []
    for name in SMALL_SHARDED:
        a = shards[name]
        lead = a.shape[:-2]
        parts.append(a.reshape(lead + (a.shape[-2] * a.shape[-1] // LANES, LANES)))
    rows = sum(q.shape[-2] for q in parts)
    parts.append(jnp.zeros(lead + (-rows % PACK_ROW_MULTIPLE, LANES), parts[0].dtype))
    return jnp.concatenate(parts, axis=-2)


def _unpack_shards(packed):
    out, r0 = {}, 0
    lead = packed.shape[:-2]
    for name in SMALL_SHARDED:
        r, cdim = _shard_shape(name)
        rows = r * cdim // LANES
        out[name] = packed[..., r0:r0 + rows, :].reshape(lead + (r, cdim))
        r0 += rows
    return out


def _full_from_shards(name, sh):
    (r, cdim), ax = SMALL_SHARDED[name]
    if ax == 0:
        return sh.reshape(r, cdim)
    return jnp.transpose(sh, (1, 0, 2)).reshape(r, cdim)


def _shards_from_full(name, full):
    (r, cdim), ax = SMALL_SHARDED[name]
    if ax == 0:
        return full.reshape(N_CHIPS, r // N_CHIPS, cdim)
    return jnp.transpose(full.reshape(r, N_CHIPS, cdim // N_CHIPS), (1, 0, 2))


SMALL_REPL = {
    "ffn1_norm": (1, 1024), "mix_norm": (1, 1024), "mla_q_norm": (1, 384), "mla_kv_norm": (1, 256),
    "mla_qk_norm_q": (1, 192), "mla_qk_norm_k": (1, 192), "ssm_a_re": (32, 64), "ssm_a_im": (32, 64),
    "ssm_log_dt": (32, 1), "ssm_b_re": (32, 64, 16), "ssm_b_im": (32, 64, 16), "ssm_c_re": (32, 16, 64),
    "ssm_c_im": (32, 16, 64), "ssm_d": (1, 512), "ssm_b_glu": (1, 512), "out_norm_mla": (1, 512),
    "out_norm_ssm": (1, 512), "xattn_norm": (1, 1024), "mem_norm": (1, 1024), "xattn_q_norm": (1, 128),
    "xattn_k_norm": (1, 128), "ffn2_norm": (1, 1024),
}


def _pack_repl(grads, loss):
    flat = jnp.concatenate([grads[n].reshape(-1) for n in SMALL_REPL] + [loss.reshape(1)])
    rows = -(-flat.shape[0] // (LANES * SUBLANES)) * SUBLANES
    return jnp.pad(flat, (0, rows * LANES - flat.shape[0])).reshape(rows, LANES)


def _unpack_repl(packed):
    flat, out, o = packed.reshape(-1), {}, 0
    for n, shp in SMALL_REPL.items():
        size = int(np.prod(shp))
        out[n] = flat[o:o + size].reshape(shp)
        o += size
    out["loss"] = flat[o]
    return out


def _rope_tables(positions):
    half = MLA_ROPE // 2
    inv = ROPE_THETA ** (-jnp.arange(half, dtype=F32) / half)
    ang = positions.astype(F32)[:, None] * inv[None, :]
    cos, sin = jnp.cos(ang), jnp.sin(ang)
    S = positions.shape[0]
    z = lambda w: jnp.zeros((S, w), F32)
    keep = jnp.concatenate([jnp.ones((S, MLA_NOPE), F32), cos, cos, z(HEAD_PAD - MLA_QK)], axis=1)
    from_hi = jnp.concatenate([z(MLA_NOPE), -sin, z(HEAD_PAD - MLA_NOPE - half)], axis=1)
    from_lo = jnp.concatenate([z(MLA_NOPE + half), sin, z(HEAD_PAD - MLA_QK)], axis=1)
    return keep, from_hi, from_lo


def _norm_rope(x, g, keep, from_hi, from_lo):
    y = x * lax.rsqrt(jnp.sum(x * x, axis=-1, keepdims=True) * (1.0 / MLA_QK) + EPS) * g
    half = MLA_ROPE // 2
    return y * keep + _lane_roll(y, HEAD_PAD - half) * from_hi + _lane_roll(y, half) * from_lo


def local_step(x, mem, positions, target, w, wb, small_weights=None, ffn2_weights=None, on_grads=None):
    if small_weights is None:
        small_weights = lambda after: {}
    if ffn2_weights is None:
        ffn2_weights = lambda after: [wb[k] for k in FFN_NAMES[3:]]
    if on_grads is None:
        on_grads = lambda tag, g: 0.0
    S = x.shape[0]
    tm = min(FUSED_TILE, S)
    g1 = (S // tm,)
    tile = lambda arr, **kw: In(arr, (tm, arr.shape[1]), lambda i: (i, 0), rows=True, **kw)
    par = lambda arr, **kw: In(arr, arr.shape, lambda *_: (0, 0), diff=True, acc=(0,), **kw)
    wt = lambda arr: In(arr, weight=True)
    otile = lambda cols, dt: Out((S, cols), dt, (tm, cols), lambda i: (i, 0))
    grads = {}

    x1, h1, g_1, u_1 = ffn_fwd("ffn1_fwd", x, w["ffn1_norm"], wb["ffn1_w_gate"], wb["ffn1_w_up"], wb["ffn1_w_down"])
    wb = {**wb, **small_weights(x1)}

    keep, from_hi, from_lo = _rope_tables(positions)
    scale = MLA_QK ** -0.5

    def f_pre(xv, kp, fh, fl, g_mix, w_q, w_kv, w_kr, w_u, g_q, g_kv, w_uq, w_ukv, gq, gk):
        h2 = _rms(xv, g_mix)
        cq, ckv = _rms(wdot(h2, w_q), g_q), _rms(wdot(h2, w_kv), g_kv)
        kr = wdot(h2, w_kr)
        qs, ks, vs = [], [], []
        for h in range(MLA_HEADS):
            qs.append(_norm_rope(wdot(cq, w_uq, h), gq, kp, fh, fl) * scale)
            kv = wdot(ckv, w_ukv, h)
            ks.append(_norm_rope(jnp.concatenate([kv[:, :MLA_NOPE], kr], axis=-1), gk, kp, fh, fl))
            vs.append(kv[:, MLA_NOPE:])
        return jnp.concatenate(qs, axis=-1), jnp.concatenate(ks, axis=-1), jnp.concatenate(vs, axis=-1), wdot(h2, w_u)

    def pre_ins(gain_mix):
        return [tile(x1, diff=True), tile(keep), tile(from_hi), tile(from_lo), par(gain_mix),
                wt(wb["w_in_q"]), wt(wb["w_in_kv"]), wt(wb["w_in_kr"]), wt(wb["w_in_u"]),
                par(w["mla_q_norm"]), par(w["mla_kv_norm"]), wt(wb["w_uq"]), wt(wb["w_ukv"]),
                par(w["qk_gain_q"]), par(w["qk_gain_k"])]

    pre_outs = [otile(MLA_HEADS * HEAD_PAD, BF16), otile(MLA_HEADS * HEAD_PAD, BF16), otile(MLA_HEADS * MLA_V, BF16),
                otile(SSM_WIDTH, F32)]
    qh, kh, v0, pu = seg_fwd("pre_mixer_fwd", f_pre, g1, pre_ins(w["mix_norm"]), pre_outs)

    o_mla, lse = attn_fwd("mla_attn_fwd", qh, kh, v0, t=min(ATTN_TILE, S))

    prep_grid = (SSM_BLOCKS,)
    prep_ins = ([_whole(w[k], diff=True, acc=(0,)) for k in ("ssm_a_re", "ssm_a_im", "ssm_log_dt")]
                + [In(w[k], (BLOCK_CH, SSM_STATE), lambda i: (i, 0), diff=True)
                   for k in ("ssm_bt_re", "ssm_bt_im", "ssm_c2_re", "ssm_c2_im")])
    blk_out = Out((SSM_BLOCKS, BLOCK_CH, PREP_LANES), BF16, (None, BLOCK_CH, PREP_LANES), lambda i: (i, 0, 0))
    prep_outs = [Out((SUBLANES, SSM_LANES), F32, (SUBLANES, PREP_LANES), lambda i: (0, i))] * 2 + [blk_out] * 4
    a_r8, a_i8, bb_r, bb_i, cb_r, ncb_i = seg_fwd("ssm_prep_fwd", _ssm_prep_f, prep_grid, prep_ins, prep_outs)

    xs_r, xs_i, y_lin = ssm_fwd("ssm_fwd", pu, bb_r, bb_i, cb_r, ncb_i, a_r8, a_i8)

    M = mem.shape[0]
    f_norm = lambda xv, g: (_rms(xv, g),)
    mem_ins = [In(mem, (M, D_MODEL), lambda i: (0, 0)), par(w["mem_norm"])]
    mem_outs = [Out((M, D_MODEL), BF16, (M, D_MODEL), lambda i: (0, 0))]
    (mn,) = seg_fwd("mem_norm_fwd", f_norm, (1,), mem_ins, mem_outs)
    (kvm,) = mm("xattn_kv_fwd", [mn], [[wb["xattn_w_kv"]]], [F32])

    def f_knorm(kk, gk):
        return (jnp.concatenate([_rms(kk[:, h * XHD:(h + 1) * XHD], gk) for h in range(XH)], axis=-1),)

    keys = ((M, XH * XHD), (M, XH * XHD), lambda i: (0, 0))
    knorm_ins = [In(kvm, keys[1], keys[2], diff=True, grad=keys), par(w["xattn_k_norm"])]
    knorm_outs = [Out(keys[0], F32, keys[1], keys[2])]
    (kn_mem,) = seg_fwd("mem_key_norm_fwd", f_knorm, (1,), knorm_ins, knorm_outs)

    xscale = XHD ** -0.5

    def f_post(o, yl, u, xv, kn, vm, d, w_glu, b, gm, gs, w_o1, w_o2, gx, w_xq, gq, w_xo):
        gl = jax.nn.gelu(yl + d * u)
        so = gl * jax.nn.sigmoid(wdot(gl, w_glu) + b)
        x2 = xv + wdot(_rms(o, gm), w_o1) + wdot(_rms(so, gs), w_o2)
        h3 = _rms(x2, gx)
        heads = []
        for h in range(XH):
            qn = _rms(wdot(h3, w_xq, h), gq)
            p = jax.nn.softmax(_bdot_nt(qn, kn[:, h * XHD:(h + 1) * XHD]) * xscale, axis=-1)
            heads.append(_bdot_nn(p, vm[:, h * XHD:(h + 1) * XHD]))
        return (x2 + wdot(jnp.concatenate(heads, axis=-1), w_xo),)

    def post_ins(gain_d):
        half = (M, XH * XHD)
        return [tile(o_mla, diff=True), tile(y_lin, diff=True), tile(pu, diff=True), tile(x1, diff=True),
                In(kn_mem, half, lambda i: (0, 0), diff=True, acc=(0,)),
                In(kvm, half, lambda i: (0, 1), diff=True, acc=(0,), grad=(half, half, lambda i: (0, 0))),
                par(gain_d), wt(wb["ssm_w_glu"]), par(w["ssm_b_glu"]), par(w["out_norm_mla"]), par(w["out_norm_ssm"]),
                wt(wb["w_o_mla"]), wt(wb["w_o_ssm"]), par(w["xattn_norm"]), wt(wb["xattn_w_q"]), par(w["xattn_q_norm"]),
                wt(wb["xattn_w_o"])]

    post_outs = [otile(D_MODEL, F32)]
    (x3,) = seg_fwd("post_mixer_fwd", f_post, g1, post_ins(w["ssm_d"]), post_outs)

    wg2, wu2, wd2 = ffn2_weights(x3)
    dx4, h4, g_2, u_2, parts = ffn_fwd("ffn2_fwd_loss", x3, w["ffn2_norm"], wg2, wu2, wd2, target=target)
    loss = jnp.sum(parts[::SUBLANES, 0])

    dx3, grads["ffn2_norm"], dg_2, du_2 = ffn_bwd_act("ffn2_bwd_act", dx4, x3, w["ffn2_norm"], g_2, u_2, wg2, wu2, wd2)
    grads["ffn2_w_gate"], grads["ffn2_w_up"], grads["ffn2_w_down"] = ffn_bwd_w("ffn2_bwd_w", h4, dx4, g_2, u_2, dg_2, du_2)
    sent = on_grads("ffn2", grads)

    (do_mla, dy_lin, du_a, dx1_a, dkn, dvm, grads["ssm_d"], grads["ssm_w_glu"], grads["ssm_b_glu"], grads["out_norm_mla"],
     grads["out_norm_ssm"], grads["w_o_mla"], grads["w_o_ssm"], grads["xattn_norm"], grads["xattn_w_q"],
     grads["xattn_q_norm"], grads["xattn_w_o"]) = seg_bwd(
        "post_mixer_bwd", f_post, g1, post_ins(w["ssm_d"] + sent), post_outs, [dx3])

    dkk, grads["xattn_k_norm"] = seg_bwd("mem_key_norm_bwd", f_knorm, (1,), knorm_ins, knorm_outs, [dkn])
    w_kv = wb["xattn_w_kv"]
    (dmn,) = mm("xattn_kv_bwd", [dkk, dvm], [[w_kv[:, :XH * XHD]], [w_kv[:, XH * XHD:]]], [F32], trans=True)
    gk_w, gv_w = mm_tn("xattn_kv_bwd_w", [mn], [dkk, dvm], [(0, [0]), (0, [1])])
    grads["xattn_w_kv"] = jnp.concatenate([gk_w, gv_w], axis=1)
    (grads["mem_norm"],) = seg_bwd("mem_norm_bwd", f_norm, (1,), mem_ins, mem_outs, [dmn])

    du, dbb_r, dbb_i, dcb_r, dncb_i, da_r8, da_i8 = ssm_bwd("ssm_bwd", dy_lin, du_a, pu, xs_r, xs_i, bb_r, bb_i, cb_r, ncb_i,
                                                            a_r8, a_i8)
    prep_g = seg_bwd("ssm_prep_bwd", _ssm_prep_f, prep_grid, prep_ins, prep_outs, [da_r8, da_i8, dbb_r, dbb_i, dcb_r, dncb_i])
    for k, gname in enumerate(("ssm_a_re", "ssm_a_im", "ssm_log_dt", "ssm_bt_re", "ssm_bt_im", "ssm_c2_re", "ssm_c2_im")):
        grads[gname] = prep_g[k]

    dqh, dkh, dv0 = attn_bwd("mla_attn_bwd", qh, kh, v0, do_mla, o_mla, lse, t=min(ATTN_TILE, S))

    (dx1, grads["mix_norm"], grads["w_in_q"], grads["w_in_kv"], grads["w_in_kr"], grads["w_in_u"], grads["mla_q_norm"],
     grads["mla_kv_norm"], grads["w_uq"], grads["w_ukv"], grads["qk_gain_q"], grads["qk_gain_k"]) = seg_bwd(
        "pre_mixer_bwd", f_pre, g1, pre_ins(w["mix_norm"]), pre_outs, [dqh, dkh, dv0, du], adds={0: dx1_a},
        row_block=FUSED_ROW_BLOCK)

    sent = on_grads("small", grads)
    dx, grads["ffn1_norm"], dg_1, du_1 = ffn_bwd_act("ffn1_bwd_act", dx1, x, w["ffn1_norm"] + sent, g_1, u_1,
                                                     wb["ffn1_w_gate"], wb["ffn1_w_up"], wb["ffn1_w_down"])
    grads["ffn1_w_gate"], grads["ffn1_w_up"], grads["ffn1_w_down"] = ffn_bwd_w("ffn1_bwd_w", h1, dx1, g_1, u_1, dg_1, du_1)
    return loss, dx, grads


def _pad_cols(a, n):
    return jnp.pad(a, ((0, 0), (0, n - a.shape[1])))


def _step_weights(shards):
    wb = {}
    w_in = _full_from_shards("w_in", shards["w_in"])
    wb["w_in_q"] = w_in[:, :MLA_Q_RANK]
    wb["w_in_kv"] = w_in[:, MLA_Q_RANK:MLA_Q_RANK + MLA_KV_RANK]
    wb["w_in_kr"] = _pad_cols(w_in[:, MLA_Q_RANK + MLA_KV_RANK:MLA_Q_RANK + MLA_KV_RANK + MLA_ROPE], LANES)
    wb["w_in_u"] = w_in[:, MLA_Q_RANK + MLA_KV_RANK + MLA_ROPE:]
    wb["w_uq"] = jnp.pad(shards["mla_w_uq"], ((0, 0), (0, 0), (0, HEAD_PAD - MLA_QK)))
    wb["w_ukv"] = shards["mla_w_ukv"]
    wb["ssm_w_glu"] = _full_from_shards("ssm_w_glu", shards["ssm_w_glu"])
    w_o = _full_from_shards("w_o", shards["w_o"])
    wb["w_o_mla"], wb["w_o_ssm"] = w_o[:SSM_WIDTH], w_o[SSM_WIDTH:]
    w_xq = _full_from_shards("xattn_w_q", shards["xattn_w_q"])
    wb["xattn_w_q"] = jnp.transpose(w_xq.reshape(D_MODEL, XH, XHD), (1, 0, 2))
    wb["xattn_w_kv"] = _full_from_shards("xattn_w_kv", shards["xattn_w_kv"])
    wb["xattn_w_o"] = _full_from_shards("xattn_w_o", shards["xattn_w_o"])
    return wb


def _sharded_grads(g):
    out = {}
    kr = g["w_in_kr"][:, :MLA_ROPE]
    out["w_in"] = _shards_from_full("w_in", jnp.concatenate([g["w_in_q"], g["w_in_kv"], kr, g["w_in_u"]], axis=1))
    out["mla_w_uq"] = g["w_uq"][:, :, :MLA_QK]
    out["mla_w_ukv"] = g["w_ukv"]
    out["ssm_w_glu"] = _shards_from_full("ssm_w_glu", g["ssm_w_glu"])
    out["w_o"] = _shards_from_full("w_o", jnp.concatenate([g["w_o_mla"], g["w_o_ssm"]], axis=0))
    w_xq = jnp.transpose(g["xattn_w_q"], (1, 0, 2)).reshape(D_MODEL, XH * XHD)
    out["xattn_w_q"] = _shards_from_full("xattn_w_q", w_xq)
    out["xattn_w_kv"] = _shards_from_full("xattn_w_kv", g["xattn_w_kv"])
    out["xattn_w_o"] = _shards_from_full("xattn_w_o", g["xattn_w_o"])
    return out


def _problem_repl_grads(g):
    out = {}
    out["mla_qk_norm_q"] = g["qk_gain_q"][:, :MLA_QK]
    out["mla_qk_norm_k"] = g["qk_gain_k"][:, :MLA_QK]
    out["ssm_b_re"] = jnp.transpose(g["ssm_bt_re"].reshape(SSM_GROUPS, SSM_GROUP, SSM_STATE), (0, 2, 1))
    out["ssm_b_im"] = jnp.transpose(g["ssm_bt_im"].reshape(SSM_GROUPS, SSM_GROUP, SSM_STATE), (0, 2, 1))
    out["ssm_c_re"] = g["ssm_c2_re"].reshape(SSM_GROUPS, SSM_GROUP, SSM_STATE)
    out["ssm_c_im"] = g["ssm_c2_im"].reshape(SSM_GROUPS, SSM_GROUP, SSM_STATE)
    for k in SMALL_REPL:
        if k not in out:
            out[k] = g[k]
    return out


def _step_params(p):
    row = lambda a: a.reshape(1, -1)
    w = {k: row(p[k]) for k in ("ffn1_norm", "mix_norm", "mla_q_norm", "mla_kv_norm", "ssm_b_glu", "out_norm_mla",
                                "out_norm_ssm", "xattn_norm", "mem_norm", "xattn_q_norm", "xattn_k_norm", "ffn2_norm")}
    w["qk_gain_q"] = _pad_cols(row(p["mla_qk_norm_q"]), HEAD_PAD)
    w["qk_gain_k"] = _pad_cols(row(p["mla_qk_norm_k"]), HEAD_PAD)
    w["ssm_a_re"], w["ssm_a_im"] = p["ssm_a_re"], p["ssm_a_im"]
    w["ssm_log_dt"] = p["ssm_log_dt"].reshape(SSM_GROUPS, 1)
    w["ssm_bt_re"] = jnp.transpose(p["ssm_b_re"], (0, 2, 1)).reshape(SSM_WIDTH, SSM_STATE)
    w["ssm_bt_im"] = jnp.transpose(p["ssm_b_im"], (0, 2, 1)).reshape(SSM_WIDTH, SSM_STATE)
    w["ssm_c2_re"] = p["ssm_c_re"].reshape(SSM_WIDTH, SSM_STATE)
    w["ssm_c2_im"] = p["ssm_c_im"].reshape(SSM_WIDTH, SSM_STATE)
    w["ssm_d"] = p["ssm_d"].reshape(1, SSM_WIDTH)
    return w


ARG_NAMES = ['x', 'mem', 'positions', 'ffn1_norm', 'ffn1_w_gate', 'ffn1_w_up', 'ffn1_w_down', 'mix_norm', 'w_in', 'mla_q_norm', 'mla_w_uq', 'mla_kv_norm', 'mla_w_ukv', 'mla_qk_norm_q', 'mla_qk_norm_k', 'ssm_a_re', 'ssm_a_im', 'ssm_log_dt', 'ssm_b_re', 'ssm_b_im', 'ssm_c_re', 'ssm_c_im', 'ssm_d', 'ssm_w_glu', 'ssm_b_glu', 'out_norm_mla', 'out_norm_ssm', 'w_o', 'xattn_norm', 'mem_norm', 'xattn_w_q', 'xattn_w_kv', 'xattn_q_norm', 'xattn_k_norm', 'xattn_w_o', 'ffn2_norm', 'ffn2_w_gate', 'ffn2_w_up', 'ffn2_w_down']
WEIGHT_NAMES = ARG_NAMES[3:]


def _gather_weights(p, c):
    half = lambda a: lax.dynamic_slice_in_dim(a, c * (a.shape[0] // 2), a.shape[0] // 2, axis=0)
    ffn1 = [half(p[k].astype(BF16)) for k in FFN_NAMES[:3]]
    small = [half(_pack_shards({k: p[k].astype(BF16) for k in SMALL_SHARDED}))]
    ffn2 = [half(p[k].astype(BF16)) for k in FFN_NAMES[3:]]
    me = 4 * lax.axis_index("x") + 2 * lax.axis_index("y") + c
    own = lambda got, blocks: [lax.dynamic_update_index_in_dim(g, b, me, 0) for g, b in zip(got, blocks)]
    as_shards = lambda a: a.reshape(N_CHIPS, 2 * a.shape[1], a.shape[2])
    landing = lambda blocks: [jax.ShapeDtypeStruct((N_DEV,) + b.shape, b.dtype) for b in blocks]
    got1 = own(all_gather_halves("all_gather_weights_a", ffn1), ffn1)
    got1, small = lax.optimization_barrier((got1, small))
    flight_s, sent_s = split_copy_start("gather_small_start", _gather_copies, N_DEV - 1, small, landing(small))
    sent_s, ffn2 = lax.optimization_barrier((sent_s, ffn2))
    flight_2, sent_2 = split_copy_start("gather_ffn2_start", _gather_copies, N_DEV - 1, ffn2, landing(ffn2))
    wb = {k: as_shards(a) for k, a in zip(FFN_NAMES[:3], got1)}

    def small_weights(after):
        mine, got = split_copy_wait("gather_small_wait", flight_s, after)
        return _step_weights(_unpack_shards(own(got, mine)[0].reshape(N_CHIPS, -1, LANES)))

    def ffn2_weights(after):
        mine, got = split_copy_wait("gather_ffn2_wait", flight_2, after)
        return [as_shards(a) for a in own(got, mine)]

    return wb, small_weights, ffn2_weights, sent_s + sent_2


class _GradReduce:
    def __init__(self, c):
        self.c, self.c_arr = c, jnp.reshape(c, (1,)).astype(jnp.int32)
        self.chip = 2 * lax.axis_index("x") + lax.axis_index("y")
        self.flights = []

    def start(self, tag, arrs):
        split = [a.reshape(N_CHIPS, 2, a.shape[1] // 2, a.shape[2]) for a in arrs]
        return self._scatter(tag, split, pair_send_halves(f"grad_pair_send_{tag}", split))

    def send(self, tag, arrs):
        split = [a.reshape(N_CHIPS, 2, a.shape[1] // 2, a.shape[2]) for a in arrs]
        lands = [jax.ShapeDtypeStruct((s.shape[0],) + s.shape[2:], s.dtype) for s in split]
        flight, sent = split_copy_start(f"grad_pair_send_start_{tag}", _pair_copies, 1, split, lands)
        self.sending = (tag, flight)
        return sent

    def scatter(self, tag, after):
        sent_tag, flight = self.sending
        assert sent_tag == tag
        return self._scatter(tag, *split_copy_wait(f"grad_pair_send_wait_{tag}", flight, after))

    def _scatter(self, tag, split, got):
        parts = [pair_add(f"grad_pair_add_{tag}_{k}", s, g, self.c_arr) for k, (s, g) in enumerate(zip(split, got))]
        flight, sent = split_copy_start(f"grad_scatter_start_{tag}", _scatter_copies, N_CHIPS - 1, parts, parts)
        self.flights.append((tag, flight))
        return sent

    def finish(self, after):
        halves = []
        for tag, flight in self.flights:
            parts, landed = split_copy_wait(f"grad_scatter_wait_{tag}", flight, after)
            for k, (q, p) in enumerate(zip(landed, parts)):
                mine = lax.dynamic_index_in_dim(p, self.chip, 0, keepdims=False)
                halves.append(sum_chips(f"grad_sum_{tag}_{k}", lax.dynamic_update_index_in_dim(q, mine, self.chip, 0)))
        tags = "_".join(t for t, _ in self.flights)
        self.flights = []
        theirs = pair_exchange(f"grad_pair_exchange_{tags}", halves)
        return [jnp.where(self.c == 0, jnp.concatenate([h, t], axis=0), jnp.concatenate([t, h], axis=0))
                for h, t in zip(halves, theirs)]


def kernel(x, mem, positions, ffn1_norm, ffn1_w_gate, ffn1_w_up, ffn1_w_down, mix_norm, w_in, mla_q_norm, mla_w_uq, mla_kv_norm, mla_w_ukv, mla_qk_norm_q, mla_qk_norm_k, ssm_a_re, ssm_a_im, ssm_log_dt, ssm_b_re, ssm_b_im, ssm_c_re, ssm_c_im, ssm_d, ssm_w_glu, ssm_b_glu, out_norm_mla, out_norm_ssm, w_o, xattn_norm, mem_norm, xattn_w_q, xattn_w_kv, xattn_q_norm, xattn_k_norm, xattn_w_o, ffn2_norm, ffn2_w_gate, ffn2_w_up, ffn2_w_down, loss_target, m_ffn1_norm, m_ffn1_w_gate, m_ffn1_w_up, m_ffn1_w_down, m_mix_norm, m_w_in, m_mla_q_norm, m_mla_w_uq, m_mla_kv_norm, m_mla_w_ukv, m_mla_qk_norm_q, m_mla_qk_norm_k, m_ssm_a_re, m_ssm_a_im, m_ssm_log_dt, m_ssm_b_re, m_ssm_b_im, m_ssm_c_re, m_ssm_c_im, m_ssm_d, m_ssm_w_glu, m_ssm_b_glu, m_out_norm_mla, m_out_norm_ssm, m_w_o, m_xattn_norm, m_mem_norm, m_xattn_w_q, m_xattn_w_kv, m_xattn_q_norm, m_xattn_k_norm, m_xattn_w_o, m_ffn2_norm, m_ffn2_w_gate, m_ffn2_w_up, m_ffn2_w_down, v_ffn1_norm, v_ffn1_w_gate, v_ffn1_w_up, v_ffn1_w_down, v_mix_norm, v_w_in, v_mla_q_norm, v_mla_w_uq, v_mla_kv_norm, v_mla_w_ukv, v_mla_qk_norm_q, v_mla_qk_norm_k, v_ssm_a_re, v_ssm_a_im, v_ssm_log_dt, v_ssm_b_re, v_ssm_b_im, v_ssm_c_re, v_ssm_c_im, v_ssm_d, v_ssm_w_glu, v_ssm_b_glu, v_out_norm_mla, v_out_norm_ssm, v_w_o, v_xattn_norm, v_mem_norm, v_xattn_w_q, v_xattn_w_kv, v_xattn_q_norm, v_xattn_k_norm, v_xattn_w_o, v_ffn2_norm, v_ffn2_w_gate, v_ffn2_w_up, v_ffn2_w_down):
    args = dict(locals())
    c = lax.axis_index("c")
    view = lambda k, a: jnp.swapaxes(a, 0, 1) if k in TRANSPOSED_VIEW else a
    p = {k: view(k, args[k][0]) for k in WEIGHT_NAMES}
    mom = {k: view(k, args["m_" + k][0]) for k in WEIGHT_NAMES}
    var = {k: view(k, args["v_" + k][0]) for k in WEIGHT_NAMES}
    natural = {k: view(k, p[k]) for k in WEIGHT_NAMES}

    wb, small_weights, ffn2_weights, sent = _gather_weights(
        {k: (p[k] if k in FFN_NAMES else natural[k]) for k in WEIGHT_NAMES}, c)
    w = _step_params(natural)
    w["ffn1_norm"] = w["ffn1_norm"] + sent
    early, late = _GradReduce(c), _GradReduce(c)

    def on_grads(tag, g):
        if tag == "ffn2":
            return early.send(tag, [g[k] for k in FFN_NAMES[3:]])
        packed = _pack_shards(_sharded_grads(g))
        return early.scatter("ffn2", packed[0, :SUBLANES]) + early.start(tag, [packed])

    loss, dx, g = local_step(x[0], mem[0], positions[0], loss_target[0], w, wb, small_weights, ffn2_weights, on_grads)

    sent = late.send("ffn1", [g[k] for k in FFN_NAMES[:3]])
    shards = early.finish(dx[:SUBLANES, :LANES] + sent)
    grad = dict(zip(FFN_NAMES[3:], shards[:3]))
    small_sharded = _unpack_shards(shards[3])
    grad.update({k: view(k, small_sharded[k]) for k in SMALL_SHARDED})
    reduced = all_reduce_small("grad_all_reduce_small", _pack_repl(_problem_repl_grads(g), loss + sent))
    grad.update(_unpack_repl(reduced))
    loss = grad.pop("loss")
    started = jnp.reshape(late.scatter("ffn1", reduced[:SUBLANES]), (1, 1))

    delta, new_m, new_v = {}, {}, {}
    small = [k for k in WEIGHT_NAMES if k not in FFN_NAMES and k not in SMALL_SHARDED]
    as2d = lambda a: a.reshape(-1, a.shape[-1])

    def update(k, after):
        delta[k], new_m[k], new_v[k] = adamw_big("adamw_" + k, as2d(p[k]), as2d(grad[k]), as2d(mom[k]), as2d(var[k]), after)

    last = started
    for k in WEIGHT_NAMES:
        if k not in small and k not in FFN_NAMES[:3]:
            update(k, last)
            last = delta[k]
    ds, nms, nvs = adamw_small("adamw_small", [as2d(p[k]) for k in small], [as2d(grad[k].reshape(p[k].shape)) for k in small],
                               [as2d(mom[k]) for k in small], [as2d(var[k]) for k in small], last)
    for k, d, nm, nv in zip(small, ds, nms, nvs):
        delta[k], new_m[k], new_v[k] = d, nm, nv
    last = ds[0]

    grad.update(zip(FFN_NAMES[:3], late.finish(last)))
    for k in FFN_NAMES[:3]:
        update(k, last)

    shaped = lambda d, k: view(k, d.reshape(p[k].shape)).reshape(args[k].shape)
    return (loss, dx[None], *[shaped(grad[k], k) for k in WEIGHT_NAMES], *[shaped(delta[k], k) for k in WEIGHT_NAMES],
            *[shaped(new_m[k], k) for k in WEIGHT_NAMES], *[shaped(new_v[k], k) for k in WEIGHT_NAMES])
```

```python
import functools
import math

import jax
import jax.numpy as jnp
import numpy as np
from jax import lax
from jax.experimental import pallas as pl
from jax.experimental.pallas import tpu as pltpu

F32, BF16 = jnp.float32, jnp.bfloat16
EPS = 1e-6
MESH = pl.DeviceIdType.MESH

D_MODEL, D_FF = 1024, 2752
MLA_HEADS, MLA_Q_RANK, MLA_KV_RANK, MLA_NOPE, MLA_ROPE, MLA_V = 4, 384, 256, 128, 64, 128
MLA_QK = MLA_NOPE + MLA_ROPE
HEAD_PAD = 256
SSM_WIDTH, SSM_GROUP, SSM_GROUPS, SSM_STATE = 512, 16, 32, 64
SSM_LANES = SSM_GROUPS * SSM_STATE
XH, XHD = 4, 128
ROPE_THETA = 10000.0
ADAM_LR, ADAM_B1, ADAM_B2, ADAM_EPS, ADAM_WD, ADAM_STEP = 0.001, 0.9, 0.999, 1e-08, 0.01, 10
N_CHIPS, N_DEV = 4, 8

VMEM_LIMIT_BYTES = 56 * 2**20
TOKEN_TILE = 512
FUSED_TILE = 256
FUSED_ROW_BLOCK = 128
FFN_SHARDS_PER_STEP = 2
FFN_ROW_BLOCK = 256
FFN_WIDE_TILE = 1024
ATTN_TILE = 1024
STAT_LANES = 128
SCAN_TIME_TILE = 2048
SCAN_LANE_TILE = 512
SUBLANES = 8


def _params(n_axes):
    return pltpu.CompilerParams(dimension_semantics=("arbitrary",) * n_axes, vmem_limit_bytes=VMEM_LIMIT_BYTES)


def _first(axes):
    cond = None
    for a in axes:
        c = pl.program_id(a) == 0
        cond = c if cond is None else jnp.logical_and(cond, c)
    return cond


def mm(name, xs, ws, out_dtypes, *, trans=False, adds=None, tm=TOKEN_TILE):
    rows = xs[0].shape[0]
    tm = min(tm, rows)
    n_in, n_out = len(xs), len(out_dtypes)
    pairs = [(i, j) for i in range(n_in) for j in range(n_out) if ws[i][j] is not None]
    w_list = [ws[i][j] for (i, j) in pairs]
    adds = list(adds) if adds is not None else [None] * n_out
    add_list = [a for a in adds if a is not None]
    out_cols = [None] * n_out
    for (i, j), w in zip(pairs, w_list):
        out_cols[j] = w.shape[0] if trans else w.shape[1]
    contract = (((1,), (1 if trans else 0,)), ((), ()))

    def body(*refs):
        x_refs = refs[:n_in]
        w_refs = refs[n_in:n_in + len(pairs)]
        a_refs = list(refs[n_in + len(pairs):n_in + len(pairs) + len(add_list)])
        o_refs = refs[n_in + len(pairs) + len(add_list):]
        xb = [None] * n_in
        for j in range(n_out):
            acc = None
            for p, (i, jj) in enumerate(pairs):
                if jj != j:
                    continue
                if xb[i] is None:
                    xb[i] = x_refs[i][...].astype(BF16)
                d = lax.dot_general(xb[i], w_refs[p][...].astype(BF16), contract, preferred_element_type=F32)
                acc = d if acc is None else acc + d
            if adds[j] is not None:
                acc = acc + a_refs.pop(0)[...].astype(F32)
            o_refs[j][...] = acc.astype(o_refs[j].dtype)

    in_specs = ([pl.BlockSpec((tm, x.shape[1]), lambda i: (i, 0)) for x in xs]
                + [pl.BlockSpec(w.shape, lambda i: (0, 0)) for w in w_list]
                + [pl.BlockSpec((tm, a.shape[1]), lambda i: (i, 0)) for a in add_list])
    outs = pl.pallas_call(
        body, name=name, grid=(rows // tm,), in_specs=in_specs,
        out_specs=[pl.BlockSpec((tm, n), lambda i: (i, 0)) for n in out_cols],
        out_shape=[jax.ShapeDtypeStruct((rows, n), dt) for n, dt in zip(out_cols, out_dtypes)],
        compiler_params=_params(1),
    )(*xs, *w_list, *add_list)
    return list(outs)


def mm_tn(name, xs, dys, pairs, *, tm=TOKEN_TILE):
    rows = xs[0].shape[0]
    tm = min(tm, rows)
    n_x, n_dy = len(xs), len(dys)
    contract = (((0,), (0,)), ((), ()))

    def body(*refs):
        x_refs, dy_refs, o_refs = refs[:n_x], refs[n_x:n_x + n_dy], refs[n_x + n_dy:]
        @pl.when(pl.program_id(0) == 0)
        def _():
            for o in o_refs:
                o[...] = jnp.zeros_like(o)

        for k, (i, js) in enumerate(pairs):
            dy = None
            for j in js:
                t = dy_refs[j][...].astype(F32)
                dy = t if dy is None else dy + t
            o_refs[k][...] += lax.dot_general(x_refs[i][...].astype(BF16), dy.astype(BF16), contract,
                                              preferred_element_type=F32)

    shapes = [(xs[i].shape[1], dys[js[0]].shape[1]) for (i, js) in pairs]
    outs = pl.pallas_call(
        body, name=name, grid=(rows // tm,),
        in_specs=[pl.BlockSpec((tm, a.shape[1]), lambda i: (i, 0)) for a in (*xs, *dys)],
        out_specs=[pl.BlockSpec(s, lambda i: (0, 0)) for s in shapes],
        out_shape=[jax.ShapeDtypeStruct(s, F32) for s in shapes],
        compiler_params=_params(1),
    )(*xs, *dys)
    return list(outs)


class In:
    def __init__(self, arr, block=None, imap=None, *, diff=False, acc=None, grad=None, weight=False, rows=False):
        self.arr, self.block, self.imap, self.diff, self.acc, self.grad = arr, block, imap, diff, acc, grad
        self.weight, self.rows = weight, rows

    def spec(self):
        return pl.BlockSpec(memory_space=pltpu.VMEM) if self.weight else pl.BlockSpec(self.block, self.imap)


class Out:
    def __init__(self, shape, dtype, block, imap):
        self.shape, self.dtype, self.block, self.imap = shape, dtype, block, imap

    def spec(self):
        return pl.BlockSpec(self.block, self.imap)


class Wt:
    def __init__(self, ref, zeros=None):
        self.ref, self.zeros = ref, zeros


@jax.custom_vjp
def _wdot(a, w, z):
    return jnp.dot(a.astype(BF16), w, preferred_element_type=F32)


def _wdot_fwd(a, w, z):
    return _wdot(a, w, z), (a, w)


def _wdot_bwd(res, g):
    a, w = res
    gb = g.astype(BF16)
    da = lax.dot_general(gb, w, (((1,), (1,)), ((), ())), preferred_element_type=F32)
    dz = lax.dot_general(a.astype(BF16), gb, (((0,), (0,)), ((), ())), preferred_element_type=F32)
    return da, None, dz


_wdot.defvjp(_wdot_fwd, _wdot_bwd)


def wdot(a, wt, head=None):
    w = wt.ref[...] if head is None else wt.ref[head]
    if wt.zeros is None:
        return jnp.dot(a.astype(BF16), w, preferred_element_type=F32)
    return _wdot(a, w, wt.zeros[0 if head is None else head])


def seg_fwd(name, f, grid, ins, outs):
    n_in = len(ins)

    def body(*refs):
        res = f(*[Wt(r) if i.weight else r[...] for i, r in zip(ins, refs[:n_in])])
        for o_ref, r in zip(refs[n_in:], res):
            o_ref[...] = r.astype(o_ref.dtype)

    res = pl.pallas_call(
        body, name=name, grid=grid, in_specs=[i.spec() for i in ins], out_specs=[o.spec() for o in outs],
        out_shape=[jax.ShapeDtypeStruct(o.shape, o.dtype) for o in outs], compiler_params=_params(len(grid)),
    )(*[i.arr for i in ins])
    return list(res)


def seg_bwd(name, f, grid, ins, outs, cts, adds=None, row_block=None):
    n_in, n_ct = len(ins), len(cts)
    grad_idx = [k for k, i in enumerate(ins) if i.diff or i.weight]
    adds = adds or {}
    add_keys = sorted(adds)
    add_list = [adds[k] for k in add_keys]
    heads = {k: (ins[k].arr.shape[0] if ins[k].arr.ndim == 3 else 1) for k in grad_idx if ins[k].weight}
    tile_rows = outs[0].block[0]
    blocks = [None] if row_block is None else [pl.ds(r0, row_block) for r0 in range(0, tile_rows, row_block)]

    def body(*refs):
        in_refs, ct_refs = refs[:n_in], refs[n_in:n_in + n_ct]
        add_refs = dict(zip(add_keys, refs[n_in + n_ct:n_in + n_ct + len(add_list)]))
        g_refs = dict(zip(grad_idx, refs[n_in + n_ct + len(add_list):]))
        for k in grad_idx:
            if ins[k].weight or ins[k].acc is not None:
                @pl.when(_first(range(len(grid)) if ins[k].weight else ins[k].acc))
                def _(k=k):
                    g_refs[k][...] = jnp.zeros_like(g_refs[k])

        for rows in blocks:
            at = lambda ref, sliced: ref[...] if rows is None or not sliced else ref[rows, :]
            vals = [None if i.weight else at(r, i.rows) for i, r in zip(ins, in_refs)]
            primals, owner = [], []
            for k in grad_idx:
                if ins[k].weight:
                    for h in range(heads[k]):
                        primals.append(jnp.zeros(ins[k].arr.shape[-2:], F32))
                        owner.append((k, h))
                else:
                    primals.append(vals[k].astype(F32))
                    owner.append((k, None))

            def g(*dv, vals=vals, owner=owner):
                full = list(vals)
                zeros = {}
                for (k, h), v in zip(owner, dv):
                    if h is None:
                        full[k] = v
                    else:
                        zeros.setdefault(k, []).append(v)
                for k, z in zeros.items():
                    full[k] = Wt(in_refs[k], z)
                return tuple(f(*full))

            _, pull = jax.vjp(g, *primals)
            grads = pull(tuple(at(c, True).astype(F32) for c in ct_refs))
            for (k, h), gr in zip(owner, grads):
                if ins[k].weight:
                    if ins[k].arr.ndim == 3:
                        g_refs[k][h] += gr
                    else:
                        g_refs[k][...] += gr
                    continue
                if k in add_refs:
                    gr = gr + at(add_refs[k], True).astype(F32)
                if ins[k].acc is not None:
                    g_refs[k][...] += gr
                elif rows is None or not ins[k].rows:
                    g_refs[k][...] = gr.astype(g_refs[k].dtype)
                else:
                    g_refs[k][rows, :] = gr.astype(g_refs[k].dtype)

    g_specs, g_shapes = [], []
    for k in grad_idx:
        i = ins[k]
        if i.weight:
            g_specs.append(pl.BlockSpec(memory_space=pltpu.VMEM))
            g_shapes.append(jax.ShapeDtypeStruct(i.arr.shape, F32))
            continue
        shape, block, imap = i.grad if i.grad is not None else (i.arr.shape, i.block, i.imap)
        g_specs.append(pl.BlockSpec(block, imap))
        g_shapes.append(jax.ShapeDtypeStruct(shape, F32))
    in_specs = ([i.spec() for i in ins] + [o.spec() for o in outs]
                + [pl.BlockSpec(ins[k].block, ins[k].imap) for k in add_keys])
    res = pl.pallas_call(
        body, name=name, grid=grid, in_specs=in_specs, out_specs=g_specs, out_shape=g_shapes,
        compiler_params=_params(len(grid)),
    )(*[i.arr for i in ins], *cts, *add_list)
    return list(res)


def _rms(x, g):
    return x * lax.rsqrt(jnp.mean(x * x, axis=-1, keepdims=True) + EPS) * g


@jax.custom_vjp
def _bdot_nt(a, b):
    return lax.dot_general(a.astype(BF16), b.astype(BF16), (((1,), (1,)), ((), ())), preferred_element_type=F32)


def _bdot_nt_fwd(a, b):
    return _bdot_nt(a, b), (a, b)


def _bdot_nt_bwd(res, g):
    a, b = res
    gb = g.astype(BF16)
    da = lax.dot_general(gb, b.astype(BF16), (((1,), (0,)), ((), ())), preferred_element_type=F32)
    db = lax.dot_general(gb, a.astype(BF16), (((0,), (0,)), ((), ())), preferred_element_type=F32)
    return da, db


_bdot_nt.defvjp(_bdot_nt_fwd, _bdot_nt_bwd)


@jax.custom_vjp
def _bdot_nn(a, b):
    return lax.dot_general(a.astype(BF16), b.astype(BF16), (((1,), (0,)), ((), ())), preferred_element_type=F32)


def _bdot_nn_fwd(a, b):
    return _bdot_nn(a, b), (a, b)


def _bdot_nn_bwd(res, g):
    a, b = res
    gb = g.astype(BF16)
    da = lax.dot_general(gb, b.astype(BF16), (((1,), (1,)), ((), ())), preferred_element_type=F32)
    db = lax.dot_general(a.astype(BF16), gb, (((0,), (0,)), ((), ())), preferred_element_type=F32)
    return da, db


_bdot_nn.defvjp(_bdot_nn_fwd, _bdot_nn_bwd)


@functools.partial(jax.custom_vjp, nondiff_argnums=(1,))
def _lane_roll(x, shift):
    return pltpu.roll(x, shift, 1)


def _lane_roll_fwd(x, shift):
    return pltpu.roll(x, shift, 1), None


def _lane_roll_bwd(shift, _, g):
    return (pltpu.roll(g, (g.shape[1] - shift) % g.shape[1], 1),)


_lane_roll.defvjp(_lane_roll_fwd, _lane_roll_bwd)


def _hp_dot(a, b):
    return jnp.dot(a, b, precision=lax.Precision.HIGHEST, preferred_element_type=F32)


NT_DIMS = (((1,), (1,)), ((), ()))
TN_DIMS = (((0,), (0,)), ((), ()))


def ffn_fwd(name, x, gain, wg, wu, wd, target=None, tm=FFN_WIDE_TILE):
    S, D = x.shape
    tm = min(tm, S)
    nsh, Fs, _ = wg.shape
    with_loss = target is not None

    def body(*refs):
        if with_loss:
            x_ref, gain_ref, wg_ref, wu_ref, wd_ref, t_ref, xo_ref, h_ref, g_ref, u_ref, part_ref, acc = refs
        else:
            x_ref, gain_ref, wg_ref, wu_ref, wd_ref, xo_ref, h_ref, g_ref, u_ref, acc = refs
        j = pl.program_id(1)

        @pl.when(j == 0)
        def _():
            h_ref[...] = _rms(x_ref[...], gain_ref[...]).astype(BF16)
            acc[...] = jnp.zeros_like(acc)

        h = h_ref[...]
        g = lax.dot_general(h, wg_ref[...], NT_DIMS, preferred_element_type=F32)
        u = lax.dot_general(h, wu_ref[...], NT_DIMS, preferred_element_type=F32)
        g_ref[...] = g.astype(BF16)
        u_ref[...] = u.astype(BF16)
        a = g * jax.nn.sigmoid(g) * u
        acc[...] += jnp.dot(a.astype(BF16), wd_ref[...], preferred_element_type=F32)

        @pl.when(j == nsh - 1)
        def _():
            y = x_ref[...] + 0.5 * acc[...]
            if with_loss:
                err = y - t_ref[...]
                xo_ref[...] = err * (1.0 / D)
                part_ref[...] = jnp.full(part_ref.shape, 0.5 * jnp.sum(jnp.mean(err * err, axis=-1)), F32)
            else:
                xo_ref[...] = y

    rows = pl.BlockSpec((tm, D), lambda i, j: (i, 0))
    wspec = pl.BlockSpec((None, Fs, D), lambda i, j: (j, 0, 0))
    act = pl.BlockSpec((None, tm, Fs), lambda i, j: (j, i, 0))
    in_specs, args = [rows, pl.BlockSpec((1, D), lambda i, j: (0, 0)), wspec, wspec, wspec], [x, gain, wg, wu, wd]
    out_specs = [rows, rows, act, act]
    out_shape = [jax.ShapeDtypeStruct((S, D), F32), jax.ShapeDtypeStruct((S, D), BF16),
                 jax.ShapeDtypeStruct((nsh, S, Fs), BF16), jax.ShapeDtypeStruct((nsh, S, Fs), BF16)]
    if with_loss:
        in_specs.append(rows)
        args.append(target)
        out_specs.append(pl.BlockSpec((SUBLANES, 128), lambda i, j: (i, 0)))
        out_shape.append(jax.ShapeDtypeStruct((S // tm * SUBLANES, 128), F32))
    return pl.pallas_call(
        body, name=name, grid=(S // tm, nsh), in_specs=in_specs, out_specs=out_specs, out_shape=out_shape,
        scratch_shapes=[pltpu.VMEM((tm, D), F32)], compiler_params=_params(2),
    )(*args)


def ffn_bwd_act(name, dxo, x, gain, g, u, wg, wu, wd, tm=TOKEN_TILE):
    S, D = x.shape
    tm = min(tm, S)
    nsh, Fs, _ = wg.shape

    def body(dxo_ref, x_ref, gain_ref, g_ref, u_ref, wg_ref, wu_ref, wd_ref, dx_ref, dgain_ref, dg_ref, du_ref, dd, dh):
        i, j = pl.program_id(0), pl.program_id(1)

        @pl.when(j == 0)
        def _():
            dd[...] = (0.5 * dxo_ref[...]).astype(BF16)
            dh[...] = jnp.zeros_like(dh)

        for s in range(FFN_SHARDS_PER_STEP):
            for r0 in range(0, tm, FFN_ROW_BLOCK):
                rows = pl.ds(r0, FFN_ROW_BLOCK)
                da = lax.dot_general(dd[rows, :], wd_ref[s], NT_DIMS, preferred_element_type=F32)
                gf, uf = g_ref[s, rows, :].astype(F32), u_ref[s, rows, :].astype(F32)
                sig = jax.nn.sigmoid(gf)
                dgv = (da * uf * (sig * (1.0 + gf * (1.0 - sig)))).astype(BF16)
                duv = (da * (gf * sig)).astype(BF16)
                dg_ref[s, rows, :] = dgv
                du_ref[s, rows, :] = duv
                dh[rows, :] += (jnp.dot(dgv, wg_ref[s], preferred_element_type=F32)
                                + jnp.dot(duv, wu_ref[s], preferred_element_type=F32))

        @pl.when(j == nsh // FFN_SHARDS_PER_STEP - 1)
        def _():
            xv = x_ref[...]
            r = lax.rsqrt(jnp.mean(xv * xv, axis=-1, keepdims=True) + EPS)
            xhat = xv * r
            dhv = dh[...]
            dxn = dhv * gain_ref[...]
            dx_ref[...] = dxo_ref[...] + r * (dxn - xhat * jnp.mean(dxn * xhat, axis=-1, keepdims=True))
            part = jnp.sum(dhv * xhat, axis=0, keepdims=True)

            @pl.when(i == 0)
            def _():
                dgain_ref[...] = part

            @pl.when(i != 0)
            def _():
                dgain_ref[...] += part

    per = FFN_SHARDS_PER_STEP
    act = pl.BlockSpec((per, tm, Fs), lambda i, j: (j, i, 0))
    wspec = pl.BlockSpec((per, Fs, D), lambda i, j: (j, 0, 0))
    return pl.pallas_call(
        body, name=name, grid=(S // tm, nsh // per),
        in_specs=[pl.BlockSpec((tm, D), lambda i, j: (i, 0)), pl.BlockSpec((tm, D), lambda i, j: (i, 0)),
                  pl.BlockSpec((1, D), lambda i, j: (0, 0)), act, act, wspec, wspec, wspec],
        out_specs=[pl.BlockSpec((tm, D), lambda i, j: (i, 0)), pl.BlockSpec((1, D), lambda i, j: (0, 0)), act, act],
        out_shape=[jax.ShapeDtypeStruct((S, D), F32), jax.ShapeDtypeStruct((1, D), F32),
                   jax.ShapeDtypeStruct((nsh, S, Fs), BF16), jax.ShapeDtypeStruct((nsh, S, Fs), BF16)],
        scratch_shapes=[pltpu.VMEM((tm, D), BF16), pltpu.VMEM((tm, D), F32)], compiler_params=_params(2),
    )(dxo, x, gain, g, u, wg, wu, wd)


def ffn_bwd_w(name, h, dxo, g, u, dg, du, tm=FFN_WIDE_TILE):
    S, D = h.shape
    tm = min(tm, S)
    nsh, _, Fs = g.shape

    def body(h_ref, dxo_ref, g_ref, u_ref, dg_ref, du_ref, dwg_ref, dwu_ref, dwd_ref):
        i = pl.program_id(1)
        gf, uf = g_ref[...].astype(F32), u_ref[...].astype(F32)
        a = (gf * jax.nn.sigmoid(gf) * uf).astype(BF16)
        dd = (0.5 * dxo_ref[...]).astype(BF16)
        hv = h_ref[...]

        @pl.when(i == 0)
        def _():
            dwg_ref[...] = jnp.zeros_like(dwg_ref)
            dwu_ref[...] = jnp.zeros_like(dwu_ref)
            dwd_ref[...] = jnp.zeros_like(dwd_ref)

        dwg_ref[...] += lax.dot_general(dg_ref[...], hv, TN_DIMS, preferred_element_type=F32)
        dwu_ref[...] += lax.dot_general(du_ref[...], hv, TN_DIMS, preferred_element_type=F32)
        dwd_ref[...] += lax.dot_general(a, dd, TN_DIMS, preferred_element_type=F32)

    act = pl.BlockSpec((None, tm, Fs), lambda j, i: (j, i, 0))
    wspec = pl.BlockSpec((None, Fs, D), lambda j, i: (j, 0, 0))
    return pl.pallas_call(
        body, name=name, grid=(nsh, S // tm),
        in_specs=[pl.BlockSpec((tm, D), lambda j, i: (i, 0)), pl.BlockSpec((tm, D), lambda j, i: (i, 0)), act, act, act, act],
        out_specs=[wspec, wspec, wspec], out_shape=[jax.ShapeDtypeStruct((nsh, Fs, D), F32)] * 3,
        compiler_params=_params(2),
    )(h, dxo, g, u, dg, du)


NEG_BIG = -1e30


def _causal_pairs(n, by_key):
    pairs = [(qi, ki) for qi in range(n) for ki in range(qi + 1)]
    if by_key:
        pairs.sort(key=lambda p: (p[1], p[0]))
    return jnp.asarray([p[0] for p in pairs], jnp.int32), jnp.asarray([p[1] for p in pairs], jnp.int32)


def _scores(q, k, masked):
    s = lax.dot_general(q, k, (((1,), (1,)), ((), ())), preferred_element_type=F32)
    if masked:
        row = lax.broadcasted_iota(jnp.int32, s.shape, 0)
        col = lax.broadcasted_iota(jnp.int32, s.shape, 1)
        s = jnp.where(row >= col, s, NEG_BIG)
    return s


def attn_fwd(name, q, k, v, t=ATTN_TILE):
    S, Dk = q.shape[0], HEAD_PAD
    H = q.shape[1] // Dk
    Dv = v.shape[1] // H
    qt, kt = _causal_pairs(S // t, by_key=False)

    def body(qt_ref, kt_ref, q_ref, k_ref, v_ref, o_ref, lse_ref, m_sc, l_sc, acc):
        qi, ki = qt_ref[pl.program_id(1)], kt_ref[pl.program_id(1)]

        @pl.when(ki == 0)
        def _():
            m_sc[...] = jnp.full_like(m_sc, NEG_BIG)
            l_sc[...] = jnp.zeros_like(l_sc)
            acc[...] = jnp.zeros_like(acc)

        def step(masked):
            s = _scores(q_ref[...], k_ref[...], masked)
            m_prev = m_sc[...]
            m_next = jnp.maximum(m_prev, jnp.max(s, axis=-1, keepdims=True))
            alpha = jnp.exp(m_prev - m_next)
            p = jnp.exp(s - jnp.tile(m_next, (1, t // STAT_LANES)))
            l_sc[...] = alpha * l_sc[...] + jnp.sum(p, axis=-1, keepdims=True)
            acc[...] = alpha * acc[...] + jnp.dot(p.astype(BF16), v_ref[...].astype(BF16), preferred_element_type=F32)
            m_sc[...] = m_next

        @pl.when(ki < qi)
        def _():
            step(False)

        @pl.when(ki == qi)
        def _():
            step(True)
            o_ref[...] = acc[...] / l_sc[...]
            lse_ref[...] = m_sc[...] + jnp.log(l_sc[...])

    stat = pltpu.VMEM((t, STAT_LANES), F32)
    return pl.pallas_call(
        body, name=name,
        grid_spec=pltpu.PrefetchScalarGridSpec(
            num_scalar_prefetch=2, grid=(H, qt.shape[0]),
            in_specs=[pl.BlockSpec((t, Dk), lambda h, s, qt, kt: (qt[s], h)),
                      pl.BlockSpec((t, Dk), lambda h, s, qt, kt: (kt[s], h)),
                      pl.BlockSpec((t, Dv), lambda h, s, qt, kt: (kt[s], h))],
            out_specs=[pl.BlockSpec((t, Dv), lambda h, s, qt, kt: (qt[s], h)),
                       pl.BlockSpec((None, t, STAT_LANES), lambda h, s, qt, kt: (h, qt[s], 0))],
            scratch_shapes=[stat, stat, pltpu.VMEM((t, Dv), F32)]),
        out_shape=[jax.ShapeDtypeStruct((S, H * Dv), F32), jax.ShapeDtypeStruct((H, S, STAT_LANES), F32)],
        compiler_params=_params(2),
    )(qt, kt, q, k, v)


def attn_bwd(name, q, k, v, do, o, lse, t=ATTN_TILE):
    S, Dk = q.shape[0], HEAD_PAD
    H = q.shape[1] // Dk
    Dv = v.shape[1] // H
    qt, kt = _causal_pairs(S // t, by_key=True)
    tn_dims = (((0,), (0,)), ((), ()))

    def body(qt_ref, kt_ref, q_ref, k_ref, v_ref, do_ref, o_ref, lse_ref, dq_ref, dk_ref, dv_ref):
        step_id = pl.program_id(1)
        qi, ki = qt_ref[step_id], kt_ref[step_id]

        @pl.when(step_id == 0)
        def _():
            dq_ref[...] = jnp.zeros_like(dq_ref)

        def step(masked):
            s = _scores(q_ref[...], k_ref[...], masked)
            reps = (1, t // STAT_LANES)
            p = jnp.exp(s - jnp.tile(lse_ref[...], reps))
            dov = do_ref[...]
            delta = jnp.broadcast_to(jnp.sum(dov * o_ref[...], axis=-1, keepdims=True), (t, STAT_LANES))
            dob = dov.astype(BF16)
            dp = lax.dot_general(dob, v_ref[...].astype(BF16), (((1,), (1,)), ((), ())), preferred_element_type=F32)
            ds = (p * (dp - jnp.tile(delta, reps))).astype(BF16)
            pdv = lax.dot_general(p.astype(BF16), dob, tn_dims, preferred_element_type=F32)
            pdk = lax.dot_general(ds, q_ref[...], tn_dims, preferred_element_type=F32)
            rows = pl.ds(pl.multiple_of(qi * t, t), t)
            dq_ref[rows, :] += jnp.dot(ds, k_ref[...], preferred_element_type=F32)
            return pdk, pdv

        @pl.when(ki == qi)
        def _():
            dk_ref[...] = jnp.zeros_like(dk_ref)
            dv_ref[...] = jnp.zeros_like(dv_ref)

        def accumulate(masked):
            pdk, pdv = step(masked)
            dk_ref[...] += pdk
            dv_ref[...] += pdv

        @pl.when(ki == qi)
        def _():
            accumulate(True)

        @pl.when(ki < qi)
        def _():
            accumulate(False)

    qrow = lambda h, s, qt, kt: (qt[s], h)
    krow = lambda h, s, qt, kt: (kt[s], h)
    return pl.pallas_call(
        body, name=name,
        grid_spec=pltpu.PrefetchScalarGridSpec(
            num_scalar_prefetch=2, grid=(H, qt.shape[0]),
            in_specs=[pl.BlockSpec((t, Dk), qrow), pl.BlockSpec((t, Dk), krow), pl.BlockSpec((t, Dv), krow),
                      pl.BlockSpec((t, Dv), qrow), pl.BlockSpec((t, Dv), qrow),
                      pl.BlockSpec((None, t, STAT_LANES), lambda h, s, qt, kt: (h, qt[s], 0))],
            out_specs=[pl.BlockSpec((S, Dk), lambda h, s, qt, kt: (0, h)), pl.BlockSpec((t, Dk), krow),
                       pl.BlockSpec((t, Dv), krow)]),
        out_shape=[jax.ShapeDtypeStruct((S, H * Dk), F32), jax.ShapeDtypeStruct((S, H * Dk), F32),
                   jax.ShapeDtypeStruct((S, H * Dv), F32)],
        compiler_params=_params(2),
    )(qt, kt, q, k, v, do, o, lse)


def _cmul(ar, ai, br, bi):
    return ar * br - ai * bi, ar * bi + ai * br


def _scan_tile(x_r, x_i, ar_ref, ai_ref, cr_sc, ci_sc, *, reverse, first, states=None):
    tc, lanes = x_r.shape
    nblk, lb = tc // SUBLANES, SCAN_LANE_TILE
    with_da = states is not None
    if with_da:
        xr_all, xi_all, pr_all, pi_all, dar_all, dai_all, chunk = states

    @pl.when(first)
    def _():
        cr_sc[...] = jnp.zeros_like(cr_sc)
        ci_sc[...] = jnp.zeros_like(ci_sc)
        if with_da:
            dar_all[...] = jnp.zeros_like(dar_all)
            dai_all[...] = jnp.zeros_like(dai_all)

    row = lax.broadcasted_iota(jnp.int32, (SUBLANES, lb), 0)
    for l0 in range(0, lanes, lb):
        _scan_lanes(x_r.at[:, pl.ds(l0, lb)], x_i.at[:, pl.ds(l0, lb)], ar_ref[0:1, pl.ds(l0, lb)],
                    ai_ref[0:1, pl.ds(l0, lb)], cr_sc.at[:, pl.ds(l0, lb)], ci_sc.at[:, pl.ds(l0, lb)], row, reverse,
                    nblk, None if not with_da else tuple(r.at[:, pl.ds(l0, lb)] for r in states[:6]) + (chunk,))


def _scan_lanes(x_r, x_i, a1r, a1i, cr_sc, ci_sc, row, reverse, nblk, states):
    lb = x_r.shape[1]
    with_da = states is not None
    if with_da:
        xr_ref, xi_ref, pr_ref, pi_ref, dar_ref, dai_ref, chunk = states
    if reverse:
        a1i = -a1i
    a2r, a2i = _cmul(a1r, a1i, a1r, a1i)
    a4r, a4i = _cmul(a2r, a2i, a2r, a2i)
    pw_r, pw_i = jnp.zeros((SUBLANES, lb), F32), jnp.zeros((SUBLANES, lb), F32)
    cur_r, cur_i = a1r, a1i
    for e in range(SUBLANES):
        r_at = (SUBLANES - 1 - e) if reverse else e
        pw_r = jnp.where(row == r_at, cur_r, pw_r)
        pw_i = jnp.where(row == r_at, cur_i, pw_i)
        cur_r, cur_i = _cmul(cur_r, cur_i, a1r, a1i)
    steps = []
    for d, pr, pi in ((1, a1r, a1i), (2, a2r, a2i), (4, a4r, a4i)):
        keep = (row < SUBLANES - d) if reverse else (row >= d)
        steps.append((d, jnp.where(keep, pr, 0.0), jnp.where(keep, pi, 0.0)))

    def block(jb, carry):
        if with_da:
            cr, ci, acc_r, acc_i = carry
        else:
            cr, ci = carry
        idx = (nblk - 1 - jb) if reverse else jb
        r0 = pl.multiple_of(idx * SUBLANES, SUBLANES)
        xr = x_r[pl.ds(r0, SUBLANES), :]
        xi = x_i[pl.ds(r0, SUBLANES), :]
        for d, pr, pi in steps:
            shift = SUBLANES - d if reverse else d
            sr, si = pltpu.roll(xr, shift, 0), pltpu.roll(xi, shift, 0)
            xr, xi = xr + pr * sr - pi * si, xi + pr * si + pi * sr
        xr, xi = xr + pw_r * cr - pw_i * ci, xi + pw_r * ci + pw_i * cr
        x_r[pl.ds(r0, SUBLANES), :] = xr
        x_i[pl.ds(r0, SUBLANES), :] = xi
        edge = 0 if reverse else SUBLANES - 1
        cr, ci = xr[edge:edge + 1, :], xi[edge:edge + 1, :]
        if not with_da:
            return cr, ci
        fr = xr_ref[pl.ds(r0, SUBLANES), :]
        fi = xi_ref[pl.ds(r0, SUBLANES), :]
        rp = pl.multiple_of(jnp.maximum(idx - 1, 0) * SUBLANES, SUBLANES)
        inside = idx > 0
        before_r = jnp.where(inside, xr_ref[pl.ds(rp, SUBLANES), :], pr_ref[...])
        before_i = jnp.where(inside, xi_ref[pl.ds(rp, SUBLANES), :], pi_ref[...])
        live = jnp.where(jnp.logical_or(inside, chunk > 0), 1.0, 0.0)
        last_r = before_r[SUBLANES - 1:SUBLANES, :] * live
        last_i = before_i[SUBLANES - 1:SUBLANES, :] * live
        pvr = jnp.where(row == 0, last_r, pltpu.roll(fr, 1, 0))
        pvi = jnp.where(row == 0, last_i, pltpu.roll(fi, 1, 0))
        acc_r = acc_r + xr * pvr + xi * pvi
        acc_i = acc_i + xi * pvr - xr * pvi
        return cr, ci, acc_r, acc_i

    init = (cr_sc[...], ci_sc[...])
    if with_da:
        init = init + (jnp.zeros((SUBLANES, lb), F32), jnp.zeros((SUBLANES, lb), F32))
    fin = lax.fori_loop(0, nblk, block, init)
    cr_sc[...] = fin[0]
    ci_sc[...] = fin[1]
    if with_da:
        dar_ref[...] += fin[2]
        dai_ref[...] += fin[3]


SSM_BLOCKS = 4
BLOCK_CH = SSM_WIDTH // SSM_BLOCKS
PREP_LANES = SSM_LANES // SSM_BLOCKS


def ssm_fwd(name, u, bb_r, bb_i, cb_r, ncb_i, a_r8, a_i8):
    S = u.shape[0]
    tc = min(SCAN_TIME_TILE, S)

    def body(u_ref, bbr_ref, bbi_ref, cbr_ref, ncbi_ref, ar_ref, ai_ref, xr_ref, xi_ref, y_ref, cr_sc, ci_sc):
        ub = u_ref[...].astype(BF16)
        xr_ref[...] = jnp.dot(ub, bbr_ref[...], preferred_element_type=F32)
        xi_ref[...] = jnp.dot(ub, bbi_ref[...], preferred_element_type=F32)
        _scan_tile(xr_ref, xi_ref, ar_ref, ai_ref, cr_sc, ci_sc, reverse=False, first=pl.program_id(1) == 0)
        y_ref[...] = (lax.dot_general(xr_ref[...].astype(BF16), cbr_ref[...], NT_DIMS, preferred_element_type=F32)
                      + lax.dot_general(xi_ref[...].astype(BF16), ncbi_ref[...], NT_DIMS, preferred_element_type=F32))

    ch = pl.BlockSpec((tc, BLOCK_CH), lambda b, t: (t, b))
    st = pl.BlockSpec((tc, PREP_LANES), lambda b, t: (t, b))
    wt = pl.BlockSpec((None, BLOCK_CH, PREP_LANES), lambda b, t: (b, 0, 0))
    par = pl.BlockSpec((SUBLANES, PREP_LANES), lambda b, t: (0, b))
    return pl.pallas_call(
        body, name=name, grid=(SSM_BLOCKS, S // tc), in_specs=[ch, wt, wt, wt, wt, par, par], out_specs=[st, st, ch],
        out_shape=[jax.ShapeDtypeStruct((S, SSM_LANES), F32), jax.ShapeDtypeStruct((S, SSM_LANES), F32),
                   jax.ShapeDtypeStruct((S, SSM_WIDTH), F32)],
        scratch_shapes=[pltpu.VMEM((1, PREP_LANES), F32), pltpu.VMEM((1, PREP_LANES), F32)], compiler_params=_params(2),
    )(u, bb_r, bb_i, cb_r, ncb_i, a_r8, a_i8)


def ssm_bwd(name, dy, du_add, u, xs_r, xs_i, bb_r, bb_i, cb_r, ncb_i, a_r8, a_i8):
    S = u.shape[0]
    tc = min(SCAN_TIME_TILE, S)
    nt = S // tc

    def body(dy_ref, dua_ref, u_ref, xr_ref, xi_ref, pr_ref, pi_ref, bbr_ref, bbi_ref, cbr_ref, ncbi_ref, ar_ref, ai_ref,
             du_ref, dbbr_ref, dbbi_ref, dcbr_ref, dncbi_ref, dar_ref, dai_ref, lr_sc, li_sc, cr_sc, ci_sc):
        t = pl.program_id(1)
        first = t == 0

        @pl.when(first)
        def _():
            for r in (dbbr_ref, dbbi_ref, dcbr_ref, dncbi_ref):
                r[...] = jnp.zeros_like(r)

        dyb = dy_ref[...].astype(BF16)
        lr_sc[...] = jnp.dot(dyb, cbr_ref[...], preferred_element_type=F32)
        li_sc[...] = jnp.dot(dyb, ncbi_ref[...], preferred_element_type=F32)
        _scan_tile(lr_sc, li_sc, ar_ref, ai_ref, cr_sc, ci_sc, reverse=True, first=first,
                   states=(xr_ref, xi_ref, pr_ref, pi_ref, dar_ref, dai_ref, nt - 1 - t))
        lrb, lib = lr_sc[...].astype(BF16), li_sc[...].astype(BF16)
        du_ref[...] = (lax.dot_general(lrb, bbr_ref[...], NT_DIMS, preferred_element_type=F32)
                       + lax.dot_general(lib, bbi_ref[...], NT_DIMS, preferred_element_type=F32) + dua_ref[...])
        ub = u_ref[...].astype(BF16)
        dbbr_ref[...] += lax.dot_general(ub, lrb, TN_DIMS, preferred_element_type=F32)
        dbbi_ref[...] += lax.dot_general(ub, lib, TN_DIMS, preferred_element_type=F32)
        dcbr_ref[...] += lax.dot_general(dyb, xr_ref[...].astype(BF16), TN_DIMS, preferred_element_type=F32)
        dncbi_ref[...] += lax.dot_general(dyb, xi_ref[...].astype(BF16), TN_DIMS, preferred_element_type=F32)

    ch = pl.BlockSpec((tc, BLOCK_CH), lambda b, t: (nt - 1 - t, b))
    st = pl.BlockSpec((tc, PREP_LANES), lambda b, t: (nt - 1 - t, b))
    prev = pl.BlockSpec((SUBLANES, PREP_LANES), lambda b, t: (jnp.maximum((nt - 1 - t) * (tc // SUBLANES) - 1, 0), b))
    wt = pl.BlockSpec((None, BLOCK_CH, PREP_LANES), lambda b, t: (b, 0, 0))
    par = pl.BlockSpec((SUBLANES, PREP_LANES), lambda b, t: (0, b))
    blk = jax.ShapeDtypeStruct((SSM_BLOCKS, BLOCK_CH, PREP_LANES), F32)
    rows8 = jax.ShapeDtypeStruct((SUBLANES, SSM_LANES), F32)
    return pl.pallas_call(
        body, name=name, grid=(SSM_BLOCKS, nt), in_specs=[ch, ch, ch, st, st, prev, prev, wt, wt, wt, wt, par, par],
        out_specs=[ch, wt, wt, wt, wt, par, par],
        out_shape=[jax.ShapeDtypeStruct((S, SSM_WIDTH), F32), blk, blk, blk, blk, rows8, rows8],
        scratch_shapes=[pltpu.VMEM((tc, PREP_LANES), F32), pltpu.VMEM((tc, PREP_LANES), F32),
                        pltpu.VMEM((1, PREP_LANES), F32), pltpu.VMEM((1, PREP_LANES), F32)],
        compiler_params=_params(2),
    )(dy, du_add, u, xs_r, xs_i, xs_r, xs_i, bb_r, bb_i, cb_r, ncb_i, a_r8, a_i8)


def _ssm_prep_f(a_re, a_im, log_dt, bt_re, bt_im, c_re, c_im):
    first_group = pl.program_id(0) * (SSM_GROUPS // SSM_BLOCKS)
    iota = lambda shape, d: lax.broadcasted_iota(jnp.int32, shape, d)
    grp_of_row = lambda shape: iota(shape, 0) >> int(math.log2(SSM_GROUP))
    grp_of_lane = lambda shape: iota(shape, 1) >> int(math.log2(SSM_STATE))
    rep = (grp_of_row((BLOCK_CH, SSM_GROUPS)) + first_group == iota((BLOCK_CH, SSM_GROUPS), 1)).astype(F32)
    til = ((iota((SSM_STATE, PREP_LANES), 1) & (SSM_STATE - 1)) == iota((SSM_STATE, PREP_LANES), 0)).astype(F32)
    m_rows = (grp_of_row((BLOCK_CH, PREP_LANES)) == grp_of_lane((BLOCK_CH, PREP_LANES))).astype(F32)
    m_grp = (iota((SSM_GROUPS, PREP_LANES), 0) == grp_of_lane((SSM_GROUPS, PREP_LANES)) + first_group).astype(F32)
    dt = jnp.exp(log_dt)
    decay = jnp.exp(a_re * dt)
    ar = decay * jnp.cos(a_im * dt)
    ai = decay * jnp.sin(a_im * dt)
    den = a_re * a_re + a_im * a_im
    nr = ar - 1.0
    coef_r = (nr * a_re + ai * a_im) / den
    coef_i = (ai * a_re - nr * a_im) / den
    cr, ci = _hp_dot(rep, coef_r), _hp_dot(rep, coef_i)
    bb_r = cr * bt_re - ci * bt_im
    bb_i = cr * bt_im + ci * bt_re
    big = lambda m: _hp_dot(m, til) * m_rows
    lanes = lambda m: jnp.broadcast_to(jnp.sum(_hp_dot(m, til) * m_grp, axis=0, keepdims=True), (SUBLANES, PREP_LANES))
    return lanes(ar), lanes(ai), big(bb_r), big(bb_i), big(c_re), -big(c_im)


def _whole(arr, **kw):
    nd = arr.ndim
    return In(arr, arr.shape, lambda *_: (0,) * nd, **kw)


def _adamw_math(w, g, m, v):
    m = ADAM_B1 * m + (1.0 - ADAM_B1) * g
    v = ADAM_B2 * v + (1.0 - ADAM_B2) * (g * g)
    m_hat = m / (1.0 - ADAM_B1 ** ADAM_STEP)
    v_hat = v / (1.0 - ADAM_B2 ** ADAM_STEP)
    delta = -ADAM_LR * (m_hat / (jnp.sqrt(v_hat) + ADAM_EPS) + ADAM_WD * w)
    return delta, m, v


def adamw_big(name, w, g, m, v, after):
    R, C = w.shape
    tr = R
    for cand in (512, 344, 256, 128):
        if R % cand == 0:
            tr = cand
            break

    def body(w_ref, g_ref, m_ref, v_ref, after_ref, d_ref, nm_ref, nv_ref):
        d, nm, nv = _adamw_math(w_ref[...], g_ref[...], m_ref[...], v_ref[...])
        d_ref[...] = d
        nm_ref[...] = nm
        nv_ref[...] = nv

    spec = pl.BlockSpec((tr, C), lambda i: (i, 0))
    return pl.pallas_call(
        body, name=name, grid=(R // tr,), in_specs=[spec] * 4 + [pl.BlockSpec(memory_space=pl.ANY)], out_specs=[spec] * 3,
        out_shape=[jax.ShapeDtypeStruct((R, C), F32)] * 3, compiler_params=_params(1),
    )(w, g, m, v, after)


def adamw_small(name, ws, gs, ms, vs, after):
    n = len(ws)

    def body(*refs):
        outs = refs[4 * n + 1:]
        for k in range(n):
            d, nm, nv = _adamw_math(refs[k][...], refs[n + k][...], refs[2 * n + k][...], refs[3 * n + k][...])
            outs[k][...] = d
            outs[n + k][...] = nm
            outs[2 * n + k][...] = nv

    vm = pl.BlockSpec(memory_space=pltpu.VMEM)
    shapes = [jax.ShapeDtypeStruct(w.shape, F32) for w in ws]
    res = pl.pallas_call(
        body, name=name, in_specs=[vm] * (4 * n) + [pl.BlockSpec(memory_space=pl.ANY)], out_specs=[vm] * (3 * n),
        out_shape=shapes * 3, compiler_params=pltpu.CompilerParams(vmem_limit_bytes=VMEM_LIMIT_BYTES),
    )(*ws, *gs, *ms, *vs, after)
    return res[:n], res[n:2 * n], res[2 * n:]


def _place():
    return lax.axis_index("x"), lax.axis_index("y"), lax.axis_index("c")


def _other_chips(x, y):
    return [(1 - x, y), (x, 1 - y), (1 - x, 1 - y)]


HBM = pl.BlockSpec(memory_space=pl.ANY)


def all_gather_halves(name, blocks):
    n = len(blocks)

    def body(*refs):
        in_refs, out_refs = refs[:n], refs[n:2 * n]
        send_sems, recv_sems = refs[2 * n:]
        x, y, c = _place()
        me, sibling = (x, y, c), (x, y, 1 - c)
        chips = _other_chips(x, y)

        def slot(a, px, py, pc):
            return out_refs[a].at[4 * px + 2 * py + pc]

        def copy(a, k, block, to, src=None):
            return pltpu.make_async_remote_copy(
                src_ref=slot(a, *block) if src is None else src, dst_ref=slot(a, *block),
                send_sem=send_sems.at[a, k], recv_sem=recv_sems.at[a, k], device_id=to, device_id_type=MESH)

        first = []
        for a in range(n):
            first.append(copy(a, 0, me, sibling, src=in_refs[a]))
            first += [copy(a, 1 + j, me, (*chip, c), src=in_refs[a]) for j, chip in enumerate(chips)]
        for cp in first:
            cp.start()
        passed = []
        for j, chip in enumerate(chips):
            for a in range(n):
                copy(a, 1 + j, (*chip, c), me).wait_recv()
                fw = copy(a, 4 + j, (*chip, c), sibling)
                fw.start()
                passed.append(fw)
        for a in range(n):
            copy(a, 0, sibling, me).wait_recv()
            for j, chip in enumerate(chips):
                copy(a, 4 + j, (*chip, 1 - c), me).wait_recv()
        for cp in first + passed:
            cp.wait_send()

    res = pl.pallas_call(
        body, name=name, in_specs=[HBM] * n, out_specs=[HBM] * n,
        out_shape=[jax.ShapeDtypeStruct((N_DEV,) + b.shape, b.dtype) for b in blocks],
        scratch_shapes=[pltpu.SemaphoreType.DMA((n, 7)), pltpu.SemaphoreType.DMA((n, 7))],
    )(*blocks)
    return list(res)


def pair_send_halves(name, grads):
    n = len(grads)

    def body(*refs):
        in_refs, out_refs = refs[:n], refs[n:2 * n]
        send_sems, recv_sems = refs[2 * n:]
        x, y, c = _place()
        cps = []
        for a in range(n):
            cp = pltpu.make_async_remote_copy(
                src_ref=in_refs[a].at[:, 1 - c], dst_ref=out_refs[a], send_sem=send_sems.at[a], recv_sem=recv_sems.at[a],
                device_id=(x, y, 1 - c), device_id_type=MESH)
            cp.start()
            cps.append(cp)
        for cp in cps:
            cp.wait()

    res = pl.pallas_call(
        body, name=name, in_specs=[HBM] * n, out_specs=[HBM] * n,
        out_shape=[jax.ShapeDtypeStruct((g.shape[0],) + g.shape[2:], g.dtype) for g in grads],
        scratch_shapes=[pltpu.SemaphoreType.DMA((n,)), pltpu.SemaphoreType.DMA((n,))],
    )(*grads)
    return list(res)


def pair_add(name, grad, got, c_arr):
    nsh, _, M, N = grad.shape
    tr = M
    for cand in (512, 256, 192, 128, 64, 16):
        if M % cand == 0:
            tr = cand
            break

    def body(c_ref, g_ref, p_ref, o_ref):
        o_ref[...] = (g_ref[...] + p_ref[...]).astype(BF16)

    return pl.pallas_call(
        body, name=name,
        grid_spec=pltpu.PrefetchScalarGridSpec(
            num_scalar_prefetch=1, grid=(nsh, M // tr),
            in_specs=[pl.BlockSpec((None, None, tr, N), lambda j, i, c_ref: (j, c_ref[0], i, 0)),
                      pl.BlockSpec((None, tr, N), lambda j, i, c_ref: (j, i, 0))],
            out_specs=pl.BlockSpec((None, tr, N), lambda j, i, c_ref: (j, i, 0))),
        out_shape=jax.ShapeDtypeStruct((nsh, M, N), BF16), compiler_params=_params(2),
    )(c_arr, grad, got)


def sum_chips(name, q):
    nsh, M, N = q.shape
    tr = M
    for cand in (512, 256, 192, 128, 64, 16):
        if M % cand == 0:
            tr = cand
            break

    def body(q_ref, o_ref):
        acc = q_ref[0].astype(F32)
        for j in range(1, nsh):
            acc = acc + q_ref[j].astype(F32)
        o_ref[...] = acc

    return pl.pallas_call(
        body, name=name, grid=(M // tr,), in_specs=[pl.BlockSpec((nsh, tr, N), lambda i: (0, i, 0))],
        out_specs=pl.BlockSpec((tr, N), lambda i: (i, 0)), out_shape=jax.ShapeDtypeStruct((M, N), F32),
        compiler_params=_params(1),
    )(q)


def pair_exchange(name, halves):
    n = len(halves)

    def body(*refs):
        in_refs, out_refs = refs[:n], refs[n:2 * n]
        send_sems, recv_sems = refs[2 * n:]
        x, y, c = _place()
        cps = []
        for a in range(n):
            cp = pltpu.make_async_remote_copy(
                src_ref=in_refs[a], dst_ref=out_refs[a], send_sem=send_sems.at[a], recv_sem=recv_sems.at[a],
                device_id=(x, y, 1 - c), device_id_type=MESH)
            cp.start()
            cps.append(cp)
        for cp in cps:
            cp.wait()

    res = pl.pallas_call(
        body, name=name, in_specs=[HBM] * n, out_specs=[HBM] * n,
        out_shape=[jax.ShapeDtypeStruct(h.shape, h.dtype) for h in halves],
        scratch_shapes=[pltpu.SemaphoreType.DMA((n,)), pltpu.SemaphoreType.DMA((n,))],
    )(*halves)
    return list(res)


SEM = pl.BlockSpec(memory_space=pltpu.SEMAPHORE)
IN_HBM = pl.BlockSpec(memory_space=pltpu.HBM)
SPLIT_COPY = pltpu.CompilerParams(has_side_effects=pltpu.SideEffectType.DATAFLOW_SIDE_EFFECTING)


def _scatter_copies(src_refs, dst_refs, send_sems, recv_sems):
    x, y, c = _place()
    mine = 2 * x + y
    return [pltpu.make_async_remote_copy(
        src_ref=src_refs[a].at[2 * px + py], dst_ref=dst_refs[a].at[mine], send_sem=send_sems.at[a * (N_CHIPS - 1) + k],
        recv_sem=recv_sems.at[a * (N_CHIPS - 1) + k], device_id=(px, py, c), device_id_type=MESH)
        for a in range(len(src_refs)) for k, (px, py) in enumerate(_other_chips(x, y))]


def _pair_copies(src_refs, dst_refs, send_sems, recv_sems):
    x, y, c = _place()
    return [pltpu.make_async_remote_copy(
        src_ref=src_refs[a].at[:, 1 - c], dst_ref=dst_refs[a], send_sem=send_sems.at[a], recv_sem=recv_sems.at[a],
        device_id=(x, y, 1 - c), device_id_type=MESH) for a in range(len(src_refs))]


def _gather_copies(src_refs, dst_refs, send_sems, recv_sems):
    x, y, c = _place()
    me = 4 * x + 2 * y + c
    cps = []
    for a in range(len(src_refs)):
        for k in range(1, N_DEV):
            to = (1 - x if k & 4 else x, 1 - y if k & 2 else y, 1 - c if k & 1 else c)
            s = a * (N_DEV - 1) + k - 1
            cps.append(pltpu.make_async_remote_copy(
                src_ref=src_refs[a], dst_ref=dst_refs[a].at[me], send_sem=send_sems.at[s], recv_sem=recv_sems.at[s],
                device_id=to, device_id_type=MESH))
    return cps


def split_copy_start(name, copies, n_sem, srcs, land_shapes):
    n = len(srcs)
    lands = [lax.empty(s.shape, s.dtype) for s in land_shapes]

    def body(*refs):
        for cp in copies(refs[:n], refs[n:2 * n], refs[2 * n], refs[2 * n + 1]):
            cp.start()
        refs[-1][...] = jnp.zeros_like(refs[-1])

    thru = [pltpu.HBM(a.shape, a.dtype) for a in (*srcs, *lands)]
    res = pl.pallas_call(
        body, name=name, in_specs=[IN_HBM] * (2 * n),
        out_specs=(SEM, SEM, *[IN_HBM] * (2 * n), pl.BlockSpec(memory_space=pltpu.VMEM)),
        out_shape=(pltpu.SemaphoreType.DMA((n * n_sem,)), pltpu.SemaphoreType.DMA((n * n_sem,)), *thru,
                   jax.ShapeDtypeStruct((SUBLANES, LANES), F32)),
        input_output_aliases={i: 2 + i for i in range(2 * n)}, compiler_params=SPLIT_COPY,
    )(*[pltpu.with_memory_space_constraint(a, pltpu.HBM) for a in (*srcs, *lands)])
    return (copies, n, res[0], res[1], res[2:2 + 2 * n]), res[-1][0, 0]


def split_copy_wait(name, handle, after):
    copies, n, send_sems, recv_sems, thru = handle

    def body(*refs):
        for cp in copies(refs[:n], refs[n:2 * n], refs[2 * n], refs[2 * n + 1]):
            cp.wait_send()
            cp.wait_recv()

    res = pl.pallas_call(
        body, name=name, in_specs=[IN_HBM] * (2 * n) + [SEM, SEM, pl.BlockSpec(memory_space=pl.ANY)],
        out_specs=[IN_HBM] * (2 * n), out_shape=[pltpu.HBM(a.shape, a.dtype) for a in thru],
        input_output_aliases={i: i for i in range(2 * n)}, compiler_params=SPLIT_COPY,
    )(*thru, send_sems, recv_sems, after)
    return list(res[:n]), list(res[n:])


def all_reduce_small(name, v):
    R, C = v.shape

    def body(v_ref, o_ref, gath, send_sems, recv_sems):
        x, y, c = _place()
        me, sibling = (x, y, c), (x, y, 1 - c)
        chips = _other_chips(x, y)

        def slot(px, py, pc):
            return gath.at[4 * px + 2 * py + pc]

        def copy(k, block, to, src=None):
            return pltpu.make_async_remote_copy(
                src_ref=slot(*block) if src is None else src, dst_ref=slot(*block),
                send_sem=send_sems.at[k], recv_sem=recv_sems.at[k], device_id=to, device_id_type=MESH)

        first = [copy(0, me, sibling, src=v_ref)]
        first += [copy(1 + j, me, (*chip, c), src=v_ref) for j, chip in enumerate(chips)]
        for cp in first:
            cp.start()
        slot(*me)[...] = v_ref[...]
        passed = [copy(4 + j, (*chip, c), sibling) for j, chip in enumerate(chips)]
        for j, chip in enumerate(chips):
            copy(1 + j, (*chip, c), me).wait_recv()
            passed[j].start()
        copy(0, sibling, me).wait_recv()
        for j, chip in enumerate(chips):
            copy(4 + j, (*chip, 1 - c), me).wait_recv()
        for cp in first + passed:
            cp.wait_send()
        acc = gath[0]
        for d in range(1, N_DEV):
            acc = acc + gath[d]
        o_ref[...] = acc

    vm = pl.BlockSpec(memory_space=pltpu.VMEM)
    return pl.pallas_call(
        body, name=name, in_specs=[vm], out_specs=vm, out_shape=jax.ShapeDtypeStruct((R, C), F32),
        scratch_shapes=[pltpu.VMEM((N_DEV, R, C), F32), pltpu.SemaphoreType.DMA((7,)), pltpu.SemaphoreType.DMA((7,))],
        compiler_params=pltpu.CompilerParams(vmem_limit_bytes=VMEM_LIMIT_BYTES),
    )(v)


LANES = 128
PACK_ROW_MULTIPLE = 1024
SMALL_SHARDED = {
    "w_in": ((D_MODEL, 1216), 1), "mla_w_uq": ((MLA_Q_RANK, 768), 1), "mla_w_ukv": ((MLA_KV_RANK, 1024), 1),
    "ssm_w_glu": ((SSM_WIDTH, SSM_WIDTH), 0), "w_o": ((D_MODEL, D_MODEL), 0), "xattn_w_q": ((D_MODEL, 512), 0),
    "xattn_w_kv": ((D_MODEL, 1024), 0), "xattn_w_o": ((512, D_MODEL), 1),
}
FFN_NAMES = ["ffn1_w_gate", "ffn1_w_up", "ffn1_w_down", "ffn2_w_gate", "ffn2_w_up", "ffn2_w_down"]
TRANSPOSED_VIEW = ("ffn1_w_gate", "ffn1_w_up", "ffn2_w_gate", "ffn2_w_up", "w_in", "mla_w_uq")


def _shard_shape(name):
    (r, cdim), ax = SMALL_SHARDED[name]
    return (r // N_CHIPS, cdim) if ax == 0 else (r, cdim // N_CHIPS)


def _pack_shards(shards):
    parts = []
    for name in SMALL_SHARDED:
        a = shards[name]
        lead = a.shape[:-2]
        parts.append(a.reshape(lead + (a.shape[-2] * a.shape[-1] // LANES, LANES)))
    rows = sum(q.shape[-2] for q in parts)
    parts.append(jnp.zeros(lead + (-rows % PACK_ROW_MULTIPLE, LANES), parts[0].dtype))
    return jnp.concatenate(parts, axis=-2)


def _unpack_shards(packed):
    out, r0 = {}, 0
    lead = packed.shape[:-2]
    for name in SMALL_SHARDED:
        r, cdim = _shard_shape(name)
        rows = r * cdim // LANES
        out[name] = packed[..., r0:r0 + rows, :].reshape(lead + (r, cdim))
        r0 += rows
    return out


def _full_from_shards(name, sh):
    (r, cdim), ax = SMALL_SHARDED[name]
    if ax == 0:
        return sh.reshape(r, cdim)
    return jnp.transpose(sh, (1, 0, 2)).reshape(r, cdim)


def _shards_from_full(name, full):
    (r, cdim), ax = SMALL_SHARDED[name]
    if ax == 0:
        return full.reshape(N_CHIPS, r // N_CHIPS, cdim)
    return jnp.transpose(full.reshape(r, N_CHIPS, cdim // N_CHIPS), (1, 0, 2))


SMALL_REPL = {
    "ffn1_norm": (1, 1024), "mix_norm": (1, 1024), "mla_q_norm": (1, 384), "mla_kv_norm": (1, 256),
    "mla_qk_norm_q": (1, 192), "mla_qk_norm_k": (1, 192), "ssm_a_re": (32, 64), "ssm_a_im": (32, 64),
    "ssm_log_dt": (32, 1), "ssm_b_re": (32, 64, 16), "ssm_b_im": (32, 64, 16), "ssm_c_re": (32, 16, 64),
    "ssm_c_im": (32, 16, 64), "ssm_d": (1, 512), "ssm_b_glu": (1, 512), "out_norm_mla": (1, 512),
    "out_norm_ssm": (1, 512), "xattn_norm": (1, 1024), "mem_norm": (1, 1024), "xattn_q_norm": (1, 128),
    "xattn_k_norm": (1, 128), "ffn2_norm": (1, 1024),
}


def _pack_repl(grads, loss):
    flat = jnp.concatenate([grads[n].reshape(-1) for n in SMALL_REPL] + [loss.reshape(1)])
    rows = -(-flat.shape[0] // (LANES * SUBLANES)) * SUBLANES
    return jnp.pad(flat, (0, rows * LANES - flat.shape[0])).reshape(rows, LANES)


def _unpack_repl(packed):
    flat, out, o = packed.reshape(-1), {}, 0
    for n, shp in SMALL_REPL.items():
        size = int(np.prod(shp))
        out[n] = flat[o:o + size].reshape(shp)
        o += size
    out["loss"] = flat[o]
    return out


def _rope_tables(positions):
    half = MLA_ROPE // 2
    inv = ROPE_THETA ** (-jnp.arange(half, dtype=F32) / half)
    ang = positions.astype(F32)[:, None] * inv[None, :]
    cos, sin = jnp.cos(ang), jnp.sin(ang)
    S = positions.shape[0]
    z = lambda w: jnp.zeros((S, w), F32)
    keep = jnp.concatenate([jnp.ones((S, MLA_NOPE), F32), cos, cos, z(HEAD_PAD - MLA_QK)], axis=1)
    from_hi = jnp.concatenate([z(MLA_NOPE), -sin, z(HEAD_PAD - MLA_NOPE - half)], axis=1)
    from_lo = jnp.concatenate([z(MLA_NOPE + half), sin, z(HEAD_PAD - MLA_QK)], axis=1)
    return keep, from_hi, from_lo


def _norm_rope(x, g, keep, from_hi, from_lo):
    y = x * lax.rsqrt(jnp.sum(x * x, axis=-1, keepdims=True) * (1.0 / MLA_QK) + EPS) * g
    half = MLA_ROPE // 2
    return y * keep + _lane_roll(y, HEAD_PAD - half) * from_hi + _lane_roll(y, half) * from_lo


def local_step(x, mem, positions, target, w, wb, small_weights=None, ffn2_weights=None, on_grads=None):
    if small_weights is None:
        small_weights = lambda after: {}
    if ffn2_weights is None:
        ffn2_weights = lambda after: [wb[k] for k in FFN_NAMES[3:]]
    if on_grads is None:
        on_grads = lambda tag, g: 0.0
    S = x.shape[0]
    tm = min(FUSED_TILE, S)
    g1 = (S // tm,)
    tile = lambda arr, **kw: In(arr, (tm, arr.shape[1]), lambda i: (i, 0), rows=True, **kw)
    par = lambda arr, **kw: In(arr, arr.shape, lambda *_: (0, 0), diff=True, acc=(0,), **kw)
    wt = lambda arr: In(arr, weight=True)
    otile = lambda cols, dt: Out((S, cols), dt, (tm, cols), lambda i: (i, 0))
    grads = {}

    x1, h1, g_1, u_1 = ffn_fwd("ffn1_fwd", x, w["ffn1_norm"], wb["ffn1_w_gate"], wb["ffn1_w_up"], wb["ffn1_w_down"])
    wb = {**wb, **small_weights(x1)}

    keep, from_hi, from_lo = _rope_tables(positions)
    scale = MLA_QK ** -0.5

    def f_pre(xv, kp, fh, fl, g_mix, w_q, w_kv, w_kr, w_u, g_q, g_kv, w_uq, w_ukv, gq, gk):
        h2 = _rms(xv, g_mix)
        cq, ckv = _rms(wdot(h2, w_q), g_q), _rms(wdot(h2, w_kv), g_kv)
        kr = wdot(h2, w_kr)
        qs, ks, vs = [], [], []
        for h in range(MLA_HEADS):
            qs.append(_norm_rope(wdot(cq, w_uq, h), gq, kp, fh, fl) * scale)
            kv = wdot(ckv, w_ukv, h)
            ks.append(_norm_rope(jnp.concatenate([kv[:, :MLA_NOPE], kr], axis=-1), gk, kp, fh, fl))
            vs.append(kv[:, MLA_NOPE:])
        return jnp.concatenate(qs, axis=-1), jnp.concatenate(ks, axis=-1), jnp.concatenate(vs, axis=-1), wdot(h2, w_u)

    def pre_ins(gain_mix):
        return [tile(x1, diff=True), tile(keep), tile(from_hi), tile(from_lo), par(gain_mix),
                wt(wb["w_in_q"]), wt(wb["w_in_kv"]), wt(wb["w_in_kr"]), wt(wb["w_in_u"]),
                par(w["mla_q_norm"]), par(w["mla_kv_norm"]), wt(wb["w_uq"]), wt(wb["w_ukv"]),
                par(w["qk_gain_q"]), par(w["qk_gain_k"])]

    pre_outs = [otile(MLA_HEADS * HEAD_PAD, BF16), otile(MLA_HEADS * HEAD_PAD, BF16), otile(MLA_HEADS * MLA_V, BF16),
                otile(SSM_WIDTH, F32)]
    qh, kh, v0, pu = seg_fwd("pre_mixer_fwd", f_pre, g1, pre_ins(w["mix_norm"]), pre_outs)

    o_mla, lse = attn_fwd("mla_attn_fwd", qh, kh, v0, t=min(ATTN_TILE, S))

    prep_grid = (SSM_BLOCKS,)
    prep_ins = ([_whole(w[k], diff=True, acc=(0,)) for k in ("ssm_a_re", "ssm_a_im", "ssm_log_dt")]
                + [In(w[k], (BLOCK_CH, SSM_STATE), lambda i: (i, 0), diff=True)
                   for k in ("ssm_bt_re", "ssm_bt_im", "ssm_c2_re", "ssm_c2_im")])
    blk_out = Out((SSM_BLOCKS, BLOCK_CH, PREP_LANES), BF16, (None, BLOCK_CH, PREP_LANES), lambda i: (i, 0, 0))
    prep_outs = [Out((SUBLANES, SSM_LANES), F32, (SUBLANES, PREP_LANES), lambda i: (0, i))] * 2 + [blk_out] * 4
    a_r8, a_i8, bb_r, bb_i, cb_r, ncb_i = seg_fwd("ssm_prep_fwd", _ssm_prep_f, prep_grid, prep_ins, prep_outs)

    xs_r, xs_i, y_lin = ssm_fwd("ssm_fwd", pu, bb_r, bb_i, cb_r, ncb_i, a_r8, a_i8)

    M = mem.shape[0]
    f_norm = lambda xv, g: (_rms(xv, g),)
    mem_ins = [In(mem, (M, D_MODEL), lambda i: (0, 0)), par(w["mem_norm"])]
    mem_outs = [Out((M, D_MODEL), BF16, (M, D_MODEL), lambda i: (0, 0))]
    (mn,) = seg_fwd("mem_norm_fwd", f_norm, (1,), mem_ins, mem_outs)
    (kvm,) = mm("xattn_kv_fwd", [mn], [[wb["xattn_w_kv"]]], [F32])

    def f_knorm(kk, gk):
        return (jnp.concatenate([_rms(kk[:, h * XHD:(h + 1) * XHD], gk) for h in range(XH)], axis=-1),)

    keys = ((M, XH * XHD), (M, XH * XHD), lambda i: (0, 0))
    knorm_ins = [In(kvm, keys[1], keys[2], diff=True, grad=keys), par(w["xattn_k_norm"])]
    knorm_outs = [Out(keys[0], F32, keys[1], keys[2])]
    (kn_mem,) = seg_fwd("mem_key_norm_fwd", f_knorm, (1,), knorm_ins, knorm_outs)

    xscale = XHD ** -0.5

    def f_post(o, yl, u, xv, kn, vm, d, w_glu, b, gm, gs, w_o1, w_o2, gx, w_xq, gq, w_xo):
        gl = jax.nn.gelu(yl + d * u)
        so = gl * jax.nn.sigmoid(wdot(gl, w_glu) + b)
        x2 = xv + wdot(_rms(o, gm), w_o1) + wdot(_rms(so, gs), w_o2)
        h3 = _rms(x2, gx)
        heads = []
        for h in range(XH):
            qn = _rms(wdot(h3, w_xq, h), gq)
            p = jax.nn.softmax(_bdot_nt(qn, kn[:, h * XHD:(h + 1) * XHD]) * xscale, axis=-1)
            heads.append(_bdot_nn(p, vm[:, h * XHD:(h + 1) * XHD]))
        return (x2 + wdot(jnp.concatenate(heads, axis=-1), w_xo),)

    def post_ins(gain_d):
        half = (M, XH * XHD)
        return [tile(o_mla, diff=True), tile(y_lin, diff=True), tile(pu, diff=True), tile(x1, diff=True),
                In(kn_mem, half, lambda i: (0, 0), diff=True, acc=(0,)),
                In(kvm, half, lambda i: (0, 1), diff=True, acc=(0,), grad=(half, half, lambda i: (0, 0))),
                par(gain_d), wt(wb["ssm_w_glu"]), par(w["ssm_b_glu"]), par(w["out_norm_mla"]), par(w["out_norm_ssm"]),
                wt(wb["w_o_mla"]), wt(wb["w_o_ssm"]), par(w["xattn_norm"]), wt(wb["xattn_w_q"]), par(w["xattn_q_norm"]),
                wt(wb["xattn_w_o"])]

    post_outs = [otile(D_MODEL, F32)]
    (x3,) = seg_fwd("post_mixer_fwd", f_post, g1, post_ins(w["ssm_d"]), post_outs)

    wg2, wu2, wd2 = ffn2_weights(x3)
    dx4, h4, g_2, u_2, parts = ffn_fwd("ffn2_fwd_loss", x3, w["ffn2_norm"], wg2, wu2, wd2, target=target)
    loss = jnp.sum(parts[::SUBLANES, 0])

    dx3, grads["ffn2_norm"], dg_2, du_2 = ffn_bwd_act("ffn2_bwd_act", dx4, x3, w["ffn2_norm"], g_2, u_2, wg2, wu2, wd2)
    grads["ffn2_w_gate"], grads["ffn2_w_up"], grads["ffn2_w_down"] = ffn_bwd_w("ffn2_bwd_w", h4, dx4, g_2, u_2, dg_2, du_2)
    sent = on_grads("ffn2", grads)

    (do_mla, dy_lin, du_a, dx1_a, dkn, dvm, grads["ssm_d"], grads["ssm_w_glu"], grads["ssm_b_glu"], grads["out_norm_mla"],
     grads["out_norm_ssm"], grads["w_o_mla"], grads["w_o_ssm"], grads["xattn_norm"], grads["xattn_w_q"],
     grads["xattn_q_norm"], grads["xattn_w_o"]) = seg_bwd(
        "post_mixer_bwd", f_post, g1, post_ins(w["ssm_d"] + sent), post_outs, [dx3])

    dkk, grads["xattn_k_norm"] = seg_bwd("mem_key_norm_bwd", f_knorm, (1,), knorm_ins, knorm_outs, [dkn])
    w_kv = wb["xattn_w_kv"]
    (dmn,) = mm("xattn_kv_bwd", [dkk, dvm], [[w_kv[:, :XH * XHD]], [w_kv[:, XH * XHD:]]], [F32], trans=True)
    gk_w, gv_w = mm_tn("xattn_kv_bwd_w", [mn], [dkk, dvm], [(0, [0]), (0, [1])])
    grads["xattn_w_kv"] = jnp.concatenate([gk_w, gv_w], axis=1)
    (grads["mem_norm"],) = seg_bwd("mem_norm_bwd", f_norm, (1,), mem_ins, mem_outs, [dmn])

    du, dbb_r, dbb_i, dcb_r, dncb_i, da_r8, da_i8 = ssm_bwd("ssm_bwd", dy_lin, du_a, pu, xs_r, xs_i, bb_r, bb_i, cb_r, ncb_i,
                                                            a_r8, a_i8)
    prep_g = seg_bwd("ssm_prep_bwd", _ssm_prep_f, prep_grid, prep_ins, prep_outs, [da_r8, da_i8, dbb_r, dbb_i, dcb_r, dncb_i])
    for k, gname in enumerate(("ssm_a_re", "ssm_a_im", "ssm_log_dt", "ssm_bt_re", "ssm_bt_im", "ssm_c2_re", "ssm_c2_im")):
        grads[gname] = prep_g[k]

    dqh, dkh, dv0 = attn_bwd("mla_attn_bwd", qh, kh, v0, do_mla, o_mla, lse, t=min(ATTN_TILE, S))

    (dx1, grads["mix_norm"], grads["w_in_q"], grads["w_in_kv"], grads["w_in_kr"], grads["w_in_u"], grads["mla_q_norm"],
     grads["mla_kv_norm"], grads["w_uq"], grads["w_ukv"], grads["qk_gain_q"], grads["qk_gain_k"]) = seg_bwd(
        "pre_mixer_bwd", f_pre, g1, pre_ins(w["mix_norm"]), pre_outs, [dqh, dkh, dv0, du], adds={0: dx1_a},
        row_block=FUSED_ROW_BLOCK)

    sent = on_grads("small", grads)
    dx, grads["ffn1_norm"], dg_1, du_1 = ffn_bwd_act("ffn1_bwd_act", dx1, x, w["ffn1_norm"] + sent, g_1, u_1,
                                                     wb["ffn1_w_gate"], wb["ffn1_w_up"], wb["ffn1_w_down"])
    grads["ffn1_w_gate"], grads["ffn1_w_up"], grads["ffn1_w_down"] = ffn_bwd_w("ffn1_bwd_w", h1, dx1, g_1, u_1, dg_1, du_1)
    return loss, dx, grads


def _pad_cols(a, n):
    return jnp.pad(a, ((0, 0), (0, n - a.shape[1])))


def _step_weights(shards):
    wb = {}
    w_in = _full_from_shards("w_in", shards["w_in"])
    wb["w_in_q"] = w_in[:, :MLA_Q_RANK]
    wb["w_in_kv"] = w_in[:, MLA_Q_RANK:MLA_Q_RANK + MLA_KV_RANK]
    wb["w_in_kr"] = _pad_cols(w_in[:, MLA_Q_RANK + MLA_KV_RANK:MLA_Q_RANK + MLA_KV_RANK + MLA_ROPE], LANES)
    wb["w_in_u"] = w_in[:, MLA_Q_RANK + MLA_KV_RANK + MLA_ROPE:]
    wb["w_uq"] = jnp.pad(shards["mla_w_uq"], ((0, 0), (0, 0), (0, HEAD_PAD - MLA_QK)))
    wb["w_ukv"] = shards["mla_w_ukv"]
    wb["ssm_w_glu"] = _full_from_shards("ssm_w_glu", shards["ssm_w_glu"])
    w_o = _full_from_shards("w_o", shards["w_o"])
    wb["w_o_mla"], wb["w_o_ssm"] = w_o[:SSM_WIDTH], w_o[SSM_WIDTH:]
    w_xq = _full_from_shards("xattn_w_q", shards["xattn_w_q"])
    wb["xattn_w_q"] = jnp.transpose(w_xq.reshape(D_MODEL, XH, XHD), (1, 0, 2))
    wb["xattn_w_kv"] = _full_from_shards("xattn_w_kv", shards["xattn_w_kv"])
    wb["xattn_w_o"] = _full_from_shards("xattn_w_o", shards["xattn_w_o"])
    return wb


def _sharded_grads(g):
    out = {}
    kr = g["w_in_kr"][:, :MLA_ROPE]
    out["w_in"] = _shards_from_full("w_in", jnp.concatenate([g["w_in_q"], g["w_in_kv"], kr, g["w_in_u"]], axis=1))
    out["mla_w_uq"] = g["w_uq"][:, :, :MLA_QK]
    out["mla_w_ukv"] = g["w_ukv"]
    out["ssm_w_glu"] = _shards_from_full("ssm_w_glu", g["ssm_w_glu"])
    out["w_o"] = _shards_from_full("w_o", jnp.concatenate([g["w_o_mla"], g["w_o_ssm"]], axis=0))
    w_xq = jnp.transpose(g["xattn_w_q"], (1, 0, 2)).reshape(D_MODEL, XH * XHD)
    out["xattn_w_q"] = _shards_from_full("xattn_w_q", w_xq)
    out["xattn_w_kv"] = _shards_from_full("xattn_w_kv", g["xattn_w_kv"])
    out["xattn_w_o"] = _shards_from_full("xattn_w_o", g["xattn_w_o"])
    return out


def _problem_repl_grads(g):
    out = {}
    out["mla_qk_norm_q"] = g["qk_gain_q"][:, :MLA_QK]
    out["mla_qk_norm_k"] = g["qk_gain_k"][:, :MLA_QK]
    out["ssm_b_re"] = jnp.transpose(g["ssm_bt_re"].reshape(SSM_GROUPS, SSM_GROUP, SSM_STATE), (0, 2, 1))
    out["ssm_b_im"] = jnp.transpose(g["ssm_bt_im"].reshape(SSM_GROUPS, SSM_GROUP, SSM_STATE), (0, 2, 1))
    out["ssm_c_re"] = g["ssm_c2_re"].reshape(SSM_GROUPS, SSM_GROUP, SSM_STATE)
    out["ssm_c_im"] = g["ssm_c2_im"].reshape(SSM_GROUPS, SSM_GROUP, SSM_STATE)
    for k in SMALL_REPL:
        if k not in out:
            out[k] = g[k]
    return out


def _step_params(p):
    row = lambda a: a.reshape(1, -1)
    w = {k: row(p[k]) for k in ("ffn1_norm", "mix_norm", "mla_q_norm", "mla_kv_norm", "ssm_b_glu", "out_norm_mla",
                                "out_norm_ssm", "xattn_norm", "mem_norm", "xattn_q_norm", "xattn_k_norm", "ffn2_norm")}
    w["qk_gain_q"] = _pad_cols(row(p["mla_qk_norm_q"]), HEAD_PAD)
    w["qk_gain_k"] = _pad_cols(row(p["mla_qk_norm_k"]), HEAD_PAD)
    w["ssm_a_re"], w["ssm_a_im"] = p["ssm_a_re"], p["ssm_a_im"]
    w["ssm_log_dt"] = p["ssm_log_dt"].reshape(SSM_GROUPS, 1)
    w["ssm_bt_re"] = jnp.transpose(p["ssm_b_re"], (0, 2, 1)).reshape(SSM_WIDTH, SSM_STATE)
    w["ssm_bt_im"] = jnp.transpose(p["ssm_b_im"], (0, 2, 1)).reshape(SSM_WIDTH, SSM_STATE)
    w["ssm_c2_re"] = p["ssm_c_re"].reshape(SSM_WIDTH, SSM_STATE)
    w["ssm_c2_im"] = p["ssm_c_im"].reshape(SSM_WIDTH, SSM_STATE)
    w["ssm_d"] = p["ssm_d"].reshape(1, SSM_WIDTH)
    return w


ARG_NAMES = ['x', 'mem', 'positions', 'ffn1_norm', 'ffn1_w_gate', 'ffn1_w_up', 'ffn1_w_down', 'mix_norm', 'w_in', 'mla_q_norm', 'mla_w_uq', 'mla_kv_norm', 'mla_w_ukv', 'mla_qk_norm_q', 'mla_qk_norm_k', 'ssm_a_re', 'ssm_a_im', 'ssm_log_dt', 'ssm_b_re', 'ssm_b_im', 'ssm_c_re', 'ssm_c_im', 'ssm_d', 'ssm_w_glu', 'ssm_b_glu', 'out_norm_mla', 'out_norm_ssm', 'w_o', 'xattn_norm', 'mem_norm', 'xattn_w_q', 'xattn_w_kv', 'xattn_q_norm', 'xattn_k_norm', 'xattn_w_o', 'ffn2_norm', 'ffn2_w_gate', 'ffn2_w_up', 'ffn2_w_down']
WEIGHT_NAMES = ARG_NAMES[3:]


def _gather_weights(p, c):
    half = lambda a: lax.dynamic_slice_in_dim(a, c * (a.shape[0] // 2), a.shape[0] // 2, axis=0)
    ffn1 = [half(p[k].astype(BF16)) for k in FFN_NAMES[:3]]
    small = [half(_pack_shards({k: p[k].astype(BF16) for k in SMALL_SHARDED}))]
    ffn2 = [half(p[k].astype(BF16)) for k in FFN_NAMES[3:]]
    me = 4 * lax.axis_index("x") + 2 * lax.axis_index("y") + c
    own = lambda got, blocks: [lax.dynamic_update_index_in_dim(g, b, me, 0) for g, b in zip(got, blocks)]
    as_shards = lambda a: a.reshape(N_CHIPS, 2 * a.shape[1], a.shape[2])
    landing = lambda blocks: [jax.ShapeDtypeStruct((N_DEV,) + b.shape, b.dtype) for b in blocks]
    got1 = own(all_gather_halves("all_gather_weights_a", ffn1), ffn1)
    got1, small = lax.optimization_barrier((got1, small))
    flight_s, sent_s = split_copy_start("gather_small_start", _gather_copies, N_DEV - 1, small, landing(small))
    sent_s, ffn2 = lax.optimization_barrier((sent_s, ffn2))
    flight_2, sent_2 = split_copy_start("gather_ffn2_start", _gather_copies, N_DEV - 1, ffn2, landing(ffn2))
    wb = {k: as_shards(a) for k, a in zip(FFN_NAMES[:3], got1)}

    def small_weights(after):
        mine, got = split_copy_wait("gather_small_wait", flight_s, after)
        return _step_weights(_unpack_shards(own(got, mine)[0].reshape(N_CHIPS, -1, LANES)))

    def ffn2_weights(after):
        mine, got = split_copy_wait("gather_ffn2_wait", flight_2, after)
        return [as_shards(a) for a in own(got, mine)]

    return wb, small_weights, ffn2_weights, sent_s + sent_2


class _GradReduce:
    def __init__(self, c):
        self.c, self.c_arr = c, jnp.reshape(c, (1,)).astype(jnp.int32)
        self.chip = 2 * lax.axis_index("x") + lax.axis_index("y")
        self.flights = []

    def start(self, tag, arrs):
        split = [a.reshape(N_CHIPS, 2, a.shape[1] // 2, a.shape[2]) for a in arrs]
        return self._scatter(tag, split, pair_send_halves(f"grad_pair_send_{tag}", split))

    def send(self, tag, arrs):
        split = [a.reshape(N_CHIPS, 2, a.shape[1] // 2, a.shape[2]) for a in arrs]
        lands = [jax.ShapeDtypeStruct((s.shape[0],) + s.shape[2:], s.dtype) for s in split]
        flight, sent = split_copy_start(f"grad_pair_send_start_{tag}", _pair_copies, 1, split, lands)
        self.sending = (tag, flight)
        return sent

    def scatter(self, tag, after):
        sent_tag, flight = self.sending
        assert sent_tag == tag
        return self._scatter(tag, *split_copy_wait(f"grad_pair_send_wait_{tag}", flight, after))

    def _scatter(self, tag, split, got):
        parts = [pair_add(f"grad_pair_add_{tag}_{k}", s, g, self.c_arr) for k, (s, g) in enumerate(zip(split, got))]
        flight, sent = split_copy_start(f"grad_scatter_start_{tag}", _scatter_copies, N_CHIPS - 1, parts, parts)
        self.flights.append((tag, flight))
        return sent

    def finish(self, after):
        halves = []
        for tag, flight in self.flights:
            parts, landed = split_copy_wait(f"grad_scatter_wait_{tag}", flight, after)
            for k, (q, p) in enumerate(zip(landed, parts)):
                mine = lax.dynamic_index_in_dim(p, self.chip, 0, keepdims=False)
                halves.append(sum_chips(f"grad_sum_{tag}_{k}", lax.dynamic_update_index_in_dim(q, mine, self.chip, 0)))
        tags = "_".join(t for t, _ in self.flights)
        self.flights = []
        theirs = pair_exchange(f"grad_pair_exchange_{tags}", halves)
        return [jnp.where(self.c == 0, jnp.concatenate([h, t], axis=0), jnp.concatenate([t, h], axis=0))
                for h, t in zip(halves, theirs)]


def kernel(x, mem, positions, ffn1_norm, ffn1_w_gate, ffn1_w_up, ffn1_w_down, mix_norm, w_in, mla_q_norm, mla_w_uq, mla_kv_norm, mla_w_ukv, mla_qk_norm_q, mla_qk_norm_k, ssm_a_re, ssm_a_im, ssm_log_dt, ssm_b_re, ssm_b_im, ssm_c_re, ssm_c_im, ssm_d, ssm_w_glu, ssm_b_glu, out_norm_mla, out_norm_ssm, w_o, xattn_norm, mem_norm, xattn_w_q, xattn_w_kv, xattn_q_norm, xattn_k_norm, xattn_w_o, ffn2_norm, ffn2_w_gate, ffn2_w_up, ffn2_w_down, loss_target, m_ffn1_norm, m_ffn1_w_gate, m_ffn1_w_up, m_ffn1_w_down, m_mix_norm, m_w_in, m_mla_q_norm, m_mla_w_uq, m_mla_kv_norm, m_mla_w_ukv, m_mla_qk_norm_q, m_mla_qk_norm_k, m_ssm_a_re, m_ssm_a_im, m_ssm_log_dt, m_ssm_b_re, m_ssm_b_im, m_ssm_c_re, m_ssm_c_im, m_ssm_d, m_ssm_w_glu, m_ssm_b_glu, m_out_norm_mla, m_out_norm_ssm, m_w_o, m_xattn_norm, m_mem_norm, m_xattn_w_q, m_xattn_w_kv, m_xattn_q_norm, m_xattn_k_norm, m_xattn_w_o, m_ffn2_norm, m_ffn2_w_gate, m_ffn2_w_up, m_ffn2_w_down, v_ffn1_norm, v_ffn1_w_gate, v_ffn1_w_up, v_ffn1_w_down, v_mix_norm, v_w_in, v_mla_q_norm, v_mla_w_uq, v_mla_kv_norm, v_mla_w_ukv, v_mla_qk_norm_q, v_mla_qk_norm_k, v_ssm_a_re, v_ssm_a_im, v_ssm_log_dt, v_ssm_b_re, v_ssm_b_im, v_ssm_c_re, v_ssm_c_im, v_ssm_d, v_ssm_w_glu, v_ssm_b_glu, v_out_norm_mla, v_out_norm_ssm, v_w_o, v_xattn_norm, v_mem_norm, v_xattn_w_q, v_xattn_w_kv, v_xattn_q_norm, v_xattn_k_norm, v_xattn_w_o, v_ffn2_norm, v_ffn2_w_gate, v_ffn2_w_up, v_ffn2_w_down):
    args = dict(locals())
    c = lax.axis_index("c")
    view = lambda k, a: jnp.swapaxes(a, 0, 1) if k in TRANSPOSED_VIEW else a
    p = {k: view(k, args[k][0]) for k in WEIGHT_NAMES}
    mom = {k: view(k, args["m_" + k][0]) for k in WEIGHT_NAMES}
    var = {k: view(k, args["v_" + k][0]) for k in WEIGHT_NAMES}
    natural = {k: view(k, p[k]) for k in WEIGHT_NAMES}

    wb, small_weights, ffn2_weights, sent = _gather_weights(
        {k: (p[k] if k in FFN_NAMES else natural[k]) for k in WEIGHT_NAMES}, c)
    w = _step_params(natural)
    w["ffn1_norm"] = w["ffn1_norm"] + sent
    early, late = _GradReduce(c), _GradReduce(c)

    def on_grads(tag, g):
        if tag == "ffn2":
            return early.send(tag, [g[k] for k in FFN_NAMES[3:]])
        packed = _pack_shards(_sharded_grads(g))
        return early.scatter("ffn2", packed[0, :SUBLANES]) + early.start(tag, [packed])

    loss, dx, g = local_step(x[0], mem[0], positions[0], loss_target[0], w, wb, small_weights, ffn2_weights, on_grads)

    sent = late.send("ffn1", [g[k] for k in FFN_NAMES[:3]])
    shards = early.finish(dx[:SUBLANES, :LANES] + sent)
    grad = dict(zip(FFN_NAMES[3:], shards[:3]))
    small_sharded = _unpack_shards(shards[3])
    grad.update({k: view(k, small_sharded[k]) for k in SMALL_SHARDED})
    reduced = all_reduce_small("grad_all_reduce_small", _pack_repl(_problem_repl_grads(g), loss + sent))
    grad.update(_unpack_repl(reduced))
    loss = grad.pop("loss")
    started = jnp.reshape(late.scatter("ffn1", reduced[:SUBLANES]), (1, 1))

    delta, new_m, new_v = {}, {}, {}
    small = [k for k in WEIGHT_NAMES if k not in FFN_NAMES and k not in SMALL_SHARDED]
    as2d = lambda a: a.reshape(-1, a.shape[-1])

    def update(k, after):
        delta[k], new_m[k], new_v[k] = adamw_big("adamw_" + k, as2d(p[k]), as2d(grad[k]), as2d(mom[k]), as2d(var[k]), after)

    last = started
    for k in WEIGHT_NAMES:
        if k not in small and k not in FFN_NAMES[:3]:
            update(k, last)
            last = delta[k]
    ds, nms, nvs = adamw_small("adamw_small", [as2d(p[k]) for k in small], [as2d(grad[k].reshape(p[k].shape)) for k in small],
                               [as2d(mom[k]) for k in small], [as2d(var[k]) for k in small], last)
    for k, d, nm, nv in zip(small, ds, nms, nvs):
        delta[k], new_m[k], new_v[k] = d, nm, nv
    last = ds[0]

    grad.update(zip(FFN_NAMES[:3], late.finish(last)))
    for k in FFN_NAMES[:3]:
        update(k, last)

    shaped = lambda d, k: view(k, d.reshape(p[k].shape)).reshape(args[k].shape)
    return (loss, dx[None], *[shaped(grad[k], k) for k in WEIGHT_NAMES], *[shaped(delta[k], k) for k in WEIGHT_NAMES],
            *[shaped(new_m[k], k) for k in WEIGHT_NAMES], *[shaped(new_v[k], k) for k in WEIGHT_NAMES])
```

```python
import functools
import math

import jax
import jax.numpy as jnp
import numpy as np
from jax import lax
from jax.experimental import pallas as pl
from jax.experimental.pallas import tpu as pltpu

F32, BF16 = jnp.float32, jnp.bfloat16
EPS = 1e-6
MESH = pl.DeviceIdType.MESH

D_MODEL, D_FF = 1024, 2752
MLA_HEADS, MLA_Q_RANK, MLA_KV_RANK, MLA_NOPE, MLA_ROPE, MLA_V = 4, 384, 256, 128, 64, 128
MLA_QK = MLA_NOPE + MLA_ROPE
HEAD_PAD = 256
SSM_WIDTH, SSM_GROUP, SSM_GROUPS, SSM_STATE = 512, 16, 32, 64
SSM_LANES = SSM_GROUPS * SSM_STATE
XH, XHD = 4, 128
ROPE_THETA = 10000.0
ADAM_LR, ADAM_B1, ADAM_B2, ADAM_EPS, ADAM_WD, ADAM_STEP = 0.001, 0.9, 0.999, 1e-08, 0.01, 10
N_CHIPS, N_DEV = 4, 8

VMEM_LIMIT_BYTES = 56 * 2**20
TOKEN_TILE = 512
FUSED_TILE = 256
FUSED_ROW_BLOCK = 128
FFN_SHARDS_PER_STEP = 2
FFN_ROW_BLOCK = 256
FFN_WIDE_TILE = 1024
ATTN_TILE = 1024
STAT_LANES = 128
SCAN_TIME_TILE = 2048
SCAN_LANE_TILE = 512
SUBLANES = 8


def _params(n_axes):
    return pltpu.CompilerParams(dimension_semantics=("arbitrary",) * n_axes, vmem_limit_bytes=VMEM_LIMIT_BYTES)


def _first(axes):
    cond = None
    for a in axes:
        c = pl.program_id(a) == 0
        cond = c if cond is None else jnp.logical_and(cond, c)
    return cond


def mm(name, xs, ws, out_dtypes, *, trans=False, adds=None, tm=TOKEN_TILE):
    rows = xs[0].shape[0]
    tm = min(tm, rows)
    n_in, n_out = len(xs), len(out_dtypes)
    pairs = [(i, j) for i in range(n_in) for j in range(n_out) if ws[i][j] is not None]
    w_list = [ws[i][j] for (i, j) in pairs]
    adds = list(adds) if adds is not None else [None] * n_out
    add_list = [a for a in adds if a is not None]
    out_cols = [None] * n_out
    for (i, j), w in zip(pairs, w_list):
        out_cols[j] = w.shape[0] if trans else w.shape[1]
    contract = (((1,), (1 if trans else 0,)), ((), ()))

    def body(*refs):
        x_refs = refs[:n_in]
        w_refs = refs[n_in:n_in + len(pairs)]
        a_refs = list(refs[n_in + len(pairs):n_in + len(pairs) + len(add_list)])
        o_refs = refs[n_in + len(pairs) + len(add_list):]
        xb = [None] * n_in
        for j in range(n_out):
            acc = None
            for p, (i, jj) in enumerate(pairs):
                if jj != j:
                    continue
                if xb[i] is None:
                    xb[i] = x_refs[i][...].astype(BF16)
                d = lax.dot_general(xb[i], w_refs[p][...].astype(BF16), contract, preferred_element_type=F32)
                acc = d if acc is None else acc + d
            if adds[j] is not None:
                acc = acc + a_refs.pop(0)[...].astype(F32)
            o_refs[j][...] = acc.astype(o_refs[j].dtype)

    in_specs = ([pl.BlockSpec((tm, x.shape[1]), lambda i: (i, 0)) for x in xs]
                + [pl.BlockSpec(w.shape, lambda i: (0, 0)) for w in w_list]
                + [pl.BlockSpec((tm, a.shape[1]), lambda i: (i, 0)) for a in add_list])
    outs = pl.pallas_call(
        body, name=name, grid=(rows // tm,), in_specs=in_specs,
        out_specs=[pl.BlockSpec((tm, n), lambda i: (i, 0)) for n in out_cols],
        out_shape=[jax.ShapeDtypeStruct((rows, n), dt) for n, dt in zip(out_cols, out_dtypes)],
        compiler_params=_params(1),
    )(*xs, *w_list, *add_list)
    return list(outs)


def mm_tn(name, xs, dys, pairs, *, tm=TOKEN_TILE):
    rows = xs[0].shape[0]
    tm = min(tm, rows)
    n_x, n_dy = len(xs), len(dys)
    contract = (((0,), (0,)), ((), ()))

    def body(*refs):
        x_refs, dy_refs, o_refs = refs[:n_x], refs[n_x:n_x + n_dy], refs[n_x + n_dy:]
        @pl.when(pl.program_id(0) == 0)
        def _():
            for o in o_refs:
                o[...] = jnp.zeros_like(o)

        for k, (i, js) in enumerate(pairs):
            dy = None
            for j in js:
                t = dy_refs[j][...].astype(F32)
                dy = t if dy is None else dy + t
            o_refs[k][...] += lax.dot_general(x_refs[i][...].astype(BF16), dy.astype(BF16), contract,
                                              preferred_element_type=F32)

    shapes = [(xs[i].shape[1], dys[js[0]].shape[1]) for (i, js) in pairs]
    outs = pl.pallas_call(
        body, name=name, grid=(rows // tm,),
        in_specs=[pl.BlockSpec((tm, a.shape[1]), lambda i: (i, 0)) for a in (*xs, *dys)],
        out_specs=[pl.BlockSpec(s, lambda i: (0, 0)) for s in shapes],
        out_shape=[jax.ShapeDtypeStruct(s, F32) for s in shapes],
        compiler_params=_params(1),
    )(*xs, *dys)
    return list(outs)


class In:
    def __init__(self, arr, block=None, imap=None, *, diff=False, acc=None, grad=None, weight=False, rows=False):
        self.arr, self.block, self.imap, self.diff, self.acc, self.grad = arr, block, imap, diff, acc, grad
        self.weight, self.rows = weight, rows

    def spec(self):
        return pl.BlockSpec(memory_space=pltpu.VMEM) if self.weight else pl.BlockSpec(self.block, self.imap)


class Out:
    def __init__(self, shape, dtype, block, imap):
        self.shape, self.dtype, self.block, self.imap = shape, dtype, block, imap

    def spec(self):
        return pl.BlockSpec(self.block, self.imap)


class Wt:
    def __init__(self, ref, zeros=None):
        self.ref, self.zeros = ref, zeros


@jax.custom_vjp
def _wdot(a, w, z):
    return jnp.dot(a.astype(BF16), w, preferred_element_type=F32)


def _wdot_fwd(a, w, z):
    return _wdot(a, w, z), (a, w)


def _wdot_bwd(res, g):
    a, w = res
    gb = g.astype(BF16)
    da = lax.dot_general(gb, w, (((1,), (1,)), ((), ())), preferred_element_type=F32)
    dz = lax.dot_general(a.astype(BF16), gb, (((0,), (0,)), ((), ())), preferred_element_type=F32)
    return da, None, dz


_wdot.defvjp(_wdot_fwd, _wdot_bwd)


def wdot(a, wt, head=None):
    w = wt.ref[...] if head is None else wt.ref[head]
    if wt.zeros is None:
        return jnp.dot(a.astype(BF16), w, preferred_element_type=F32)
    return _wdot(a, w, wt.zeros[0 if head is None else head])


def seg_fwd(name, f, grid, ins, outs):
    n_in = len(ins)

    def body(*refs):
        res = f(*[Wt(r) if i.weight else r[...] for i, r in zip(ins, refs[:n_in])])
        for o_ref, r in zip(refs[n_in:], res):
            o_ref[...] = r.astype(o_ref.dtype)

    res = pl.pallas_call(
        body, name=name, grid=grid, in_specs=[i.spec() for i in ins], out_specs=[o.spec() for o in outs],
        out_shape=[jax.ShapeDtypeStruct(o.shape, o.dtype) for o in outs], compiler_params=_params(len(grid)),
    )(*[i.arr for i in ins])
    return list(res)


def seg_bwd(name, f, grid, ins, outs, cts, adds=None, row_block=None):
    n_in, n_ct = len(ins), len(cts)
    grad_idx = [k for k, i in enumerate(ins) if i.diff or i.weight]
    adds = adds or {}
    add_keys = sorted(adds)
    add_list = [adds[k] for k in add_keys]
    heads = {k: (ins[k].arr.shape[0] if ins[k].arr.ndim == 3 else 1) for k in grad_idx if ins[k].weight}
    tile_rows = outs[0].block[0]
    blocks = [None] if row_block is None else [pl.ds(r0, row_block) for r0 in range(0, tile_rows, row_block)]

    def body(*refs):
        in_refs, ct_refs = refs[:n_in], refs[n_in:n_in + n_ct]
        add_refs = dict(zip(add_keys, refs[n_in + n_ct:n_in + n_ct + len(add_list)]))
        g_refs = dict(zip(grad_idx, refs[n_in + n_ct + len(add_list):]))
        for k in grad_idx:
            if ins[k].weight or ins[k].acc is not None:
                @pl.when(_first(range(len(grid)) if ins[k].weight else ins[k].acc))
                def _(k=k):
                    g_refs[k][...] = jnp.zeros_like(g_refs[k])

        for rows in blocks:
            at = lambda ref, sliced: ref[...] if rows is None or not sliced else ref[rows, :]
            vals = [None if i.weight else at(r, i.rows) for i, r in zip(ins, in_refs)]
            primals, owner = [], []
            for k in grad_idx:
                if ins[k].weight:
                    for h in range(heads[k]):
                        primals.append(jnp.zeros(ins[k].arr.shape[-2:], F32))
                        owner.append((k, h))
                else:
                    primals.append(vals[k].astype(F32))
                    owner.append((k, None))

            def g(*dv, vals=vals, owner=owner):
                full = list(vals)
                zeros = {}
                for (k, h), v in zip(owner, dv):
                    if h is None:
                        full[k] = v
                    else:
                        zeros.setdefault(k, []).append(v)
                for k, z in zeros.items():
                    full[k] = Wt(in_refs[k], z)
                return tuple(f(*full))

            _, pull = jax.vjp(g, *primals)
            grads = pull(tuple(at(c, True).astype(F32) for c in ct_refs))
            for (k, h), gr in zip(owner, grads):
                if ins[k].weight:
                    if ins[k].arr.ndim == 3:
                        g_refs[k][h] += gr
                    else:
                        g_refs[k][...] += gr
                    continue
                if k in add_refs:
                    gr = gr + at(add_refs[k], True).astype(F32)
                if ins[k].acc is not None:
                    g_refs[k][...] += gr
                elif rows is None or not ins[k].rows:
                    g_refs[k][...] = gr.astype(g_refs[k].dtype)
                else:
                    g_refs[k][rows, :] = gr.astype(g_refs[k].dtype)

    g_specs, g_shapes = [], []
    for k in grad_idx:
        i = ins[k]
        if i.weight:
            g_specs.append(pl.BlockSpec(memory_space=pltpu.VMEM))
            g_shapes.append(jax.ShapeDtypeStruct(i.arr.shape, F32))
            continue
        shape, block, imap = i.grad if i.grad is not None else (i.arr.shape, i.block, i.imap)
        g_specs.append(pl.BlockSpec(block, imap))
        g_shapes.append(jax.ShapeDtypeStruct(shape, F32))
    in_specs = ([i.spec() for i in ins] + [o.spec() for o in outs]
                + [pl.BlockSpec(ins[k].block, ins[k].imap) for k in add_keys])
    res = pl.pallas_call(
        body, name=name, grid=grid, in_specs=in_specs, out_specs=g_specs, out_shape=g_shapes,
        compiler_params=_params(len(grid)),
    )(*[i.arr for i in ins], *cts, *add_list)
    return list(res)


def _rms(x, g):
    return x * lax.rsqrt(jnp.mean(x * x, axis=-1, keepdims=True) + EPS) * g


@jax.custom_vjp
def _bdot_nt(a, b):
    return lax.dot_general(a.astype(BF16), b.astype(BF16), (((1,), (1,)), ((), ())), preferred_element_type=F32)


def _bdot_nt_fwd(a, b):
    return _bdot_nt(a, b), (a, b)


def _bdot_nt_bwd(res, g):
    a, b = res
    gb = g.astype(BF16)
    da = lax.dot_general(gb, b.astype(BF16), (((1,), (0,)), ((), ())), preferred_element_type=F32)
    db = lax.dot_general(gb, a.astype(BF16), (((0,), (0,)), ((), ())), preferred_element_type=F32)
    return da, db


_bdot_nt.defvjp(_bdot_nt_fwd, _bdot_nt_bwd)


@jax.custom_vjp
def _bdot_nn(a, b):
    return lax.dot_general(a.astype(BF16), b.astype(BF16), (((1,), (0,)), ((), ())), preferred_element_type=F32)


def _bdot_nn_fwd(a, b):
    return _bdot_nn(a, b), (a, b)


def _bdot_nn_bwd(res, g):
    a, b = res
    gb = g.astype(BF16)
    da = lax.dot_general(gb, b.astype(BF16), (((1,), (1,)), ((), ())), preferred_element_type=F32)
    db = lax.dot_general(a.astype(BF16), gb, (((0,), (0,)), ((), ())), preferred_element_type=F32)
    return da, db


_bdot_nn.defvjp(_bdot_nn_fwd, _bdot_nn_bwd)


@functools.partial(jax.custom_vjp, nondiff_argnums=(1,))
def _lane_roll(x, shift):
    return pltpu.roll(x, shift, 1)


def _lane_roll_fwd(x, shift):
    return pltpu.roll(x, shift, 1), None


def _lane_roll_bwd(shift, _, g):
    return (pltpu.roll(g, (g.shape[1] - shift) % g.shape[1], 1),)


_lane_roll.defvjp(_lane_roll_fwd, _lane_roll_bwd)


def _hp_dot(a, b):
    return jnp.dot(a, b, precision=lax.Precision.HIGHEST, preferred_element_type=F32)


NT_DIMS = (((1,), (1,)), ((), ()))
TN_DIMS = (((0,), (0,)), ((), ()))


def ffn_fwd(name, x, gain, wg, wu, wd, target=None, tm=FFN_WIDE_TILE):
    S, D = x.shape
    tm = min(tm, S)
    nsh, Fs, _ = wg.shape
    with_loss = target is not None

    def body(*refs):
        if with_loss:
            x_ref, gain_ref, wg_ref, wu_ref, wd_ref, t_ref, xo_ref, h_ref, g_ref, u_ref, part_ref, acc = refs
        else:
            x_ref, gain_ref, wg_ref, wu_ref, wd_ref, xo_ref, h_ref, g_ref, u_ref, acc = refs
        j = pl.program_id(1)

        @pl.when(j == 0)
        def _():
            h_ref[...] = _rms(x_ref[...], gain_ref[...]).astype(BF16)
            acc[...] = jnp.zeros_like(acc)

        h = h_ref[...]
        g = lax.dot_general(h, wg_ref[...], NT_DIMS, preferred_element_type=F32)
        u = lax.dot_general(h, wu_ref[...], NT_DIMS, preferred_element_type=F32)
        g_ref[...] = g.astype(BF16)
        u_ref[...] = u.astype(BF16)
        a = g * jax.nn.sigmoid(g) * u
        acc[...] += jnp.dot(a.astype(BF16), wd_ref[...], preferred_element_type=F32)

        @pl.when(j == nsh - 1)
        def _():
            y = x_ref[...] + 0.5 * acc[...]
            if with_loss:
                err = y - t_ref[...]
                xo_ref[...] = err * (1.0 / D)
                part_ref[...] = jnp.full(part_ref.shape, 0.5 * jnp.sum(jnp.mean(err * err, axis=-1)), F32)
            else:
                xo_ref[...] = y

    rows = pl.BlockSpec((tm, D), lambda i, j: (i, 0))
    wspec = pl.BlockSpec((None, Fs, D), lambda i, j: (j, 0, 0))
    act = pl.BlockSpec((None, tm, Fs), lambda i, j: (j, i, 0))
    in_specs, args = [rows, pl.BlockSpec((1, D), lambda i, j: (0, 0)), wspec, wspec, wspec], [x, gain, wg, wu, wd]
    out_specs = [rows, rows, act, act]
    out_shape = [jax.ShapeDtypeStruct((S, D), F32), jax.ShapeDtypeStruct((S, D), BF16),
                 jax.ShapeDtypeStruct((nsh, S, Fs), BF16), jax.ShapeDtypeStruct((nsh, S, Fs), BF16)]
    if with_loss:
        in_specs.append(rows)
        args.append(target)
        out_specs.append(pl.BlockSpec((SUBLANES, 128), lambda i, j: (i, 0)))
        out_shape.append(jax.ShapeDtypeStruct((S // tm * SUBLANES, 128), F32))
    return pl.pallas_call(
        body, name=name, grid=(S // tm, nsh), in_specs=in_specs, out_specs=out_specs, out_shape=out_shape,
        scratch_shapes=[pltpu.VMEM((tm, D), F32)], compiler_params=_params(2),
    )(*args)


def ffn_bwd_act(name, dxo, x, gain, g, u, wg, wu, wd, tm=TOKEN_TILE):
    S, D = x.shape
    tm = min(tm, S)
    nsh, Fs, _ = wg.shape

    def body(dxo_ref, x_ref, gain_ref, g_ref, u_ref, wg_ref, wu_ref, wd_ref, dx_ref, dgain_ref, dg_ref, du_ref, dd, dh):
        i, j = pl.program_id(0), pl.program_id(1)

        @pl.when(j == 0)
        def _():
            dd[...] = (0.5 * dxo_ref[...]).astype(BF16)
            dh[...] = jnp.zeros_like(dh)

        for s in range(FFN_SHARDS_PER_STEP):
            for r0 in range(0, tm, FFN_ROW_BLOCK):
                rows = pl.ds(r0, FFN_ROW_BLOCK)
                da = lax.dot_general(dd[rows, :], wd_ref[s], NT_DIMS, preferred_element_type=F32)
                gf, uf = g_ref[s, rows, :].astype(F32), u_ref[s, rows, :].astype(F32)
                sig = jax.nn.sigmoid(gf)
                dgv = (da * uf * (sig * (1.0 + gf * (1.0 - sig)))).astype(BF16)
                duv = (da * (gf * sig)).astype(BF16)
                dg_ref[s, rows, :] = dgv
                du_ref[s, rows, :] = duv
                dh[rows, :] += (jnp.dot(dgv, wg_ref[s], preferred_element_type=F32)
                                + jnp.dot(duv, wu_ref[s], preferred_element_type=F32))

        @pl.when(j == nsh // FFN_SHARDS_PER_STEP - 1)
        def _():
            xv = x_ref[...]
            r = lax.rsqrt(jnp.mean(xv * xv, axis=-1, keepdims=True) + EPS)
            xhat = xv * r
            dhv = dh[...]
            dxn = dhv * gain_ref[...]
            dx_ref[...] = dxo_ref[...] + r * (dxn - xhat * jnp.mean(dxn * xhat, axis=-1, keepdims=True))
            part = jnp.sum(dhv * xhat, axis=0, keepdims=True)

            @pl.when(i == 0)
            def _():
                dgain_ref[...] = part

            @pl.when(i != 0)
            def _():
                dgain_ref[...] += part

    per = FFN_SHARDS_PER_STEP
    act = pl.BlockSpec((per, tm, Fs), lambda i, j: (j, i, 0))
    wspec = pl.BlockSpec((per, Fs, D), lambda i, j: (j, 0, 0))
    return pl.pallas_call(
        body, name=name, grid=(S // tm, nsh // per),
        in_specs=[pl.BlockSpec((tm, D), lambda i, j: (i, 0)), pl.BlockSpec((tm, D), lambda i, j: (i, 0)),
                  pl.BlockSpec((1, D), lambda i, j: (0, 0)), act, act, wspec, wspec, wspec],
        out_specs=[pl.BlockSpec((tm, D), lambda i, j: (i, 0)), pl.BlockSpec((1, D), lambda i, j: (0, 0)), act, act],
        out_shape=[jax.ShapeDtypeStruct((S, D), F32), jax.ShapeDtypeStruct((1, D), F32),
                   jax.ShapeDtypeStruct((nsh, S, Fs), BF16), jax.ShapeDtypeStruct((nsh, S, Fs), BF16)],
        scratch_shapes=[pltpu.VMEM((tm, D), BF16), pltpu.VMEM((tm, D), F32)], compiler_params=_params(2),
    )(dxo, x, gain, g, u, wg, wu, wd)


def ffn_bwd_w(name, h, dxo, g, u, dg, du, tm=FFN_WIDE_TILE):
    S, D = h.shape
    tm = min(tm, S)
    nsh, _, Fs = g.shape

    def body(h_ref, dxo_ref, g_ref, u_ref, dg_ref, du_ref, dwg_ref, dwu_ref, dwd_ref):
        i = pl.program_id(1)
        gf, uf = g_ref[...].astype(F32), u_ref[...].astype(F32)
        a = (gf * jax.nn.sigmoid(gf) * uf).astype(BF16)
        dd = (0.5 * dxo_ref[...]).astype(BF16)
        hv = h_ref[...]

        @pl.when(i == 0)
        def _():
            dwg_ref[...] = jnp.zeros_like(dwg_ref)
            dwu_ref[...] = jnp.zeros_like(dwu_ref)
            dwd_ref[...] = jnp.zeros_like(dwd_ref)

        dwg_ref[...] += lax.dot_general(dg_ref[...], hv, TN_DIMS, preferred_element_type=F32)
        dwu_ref[...] += lax.dot_general(du_ref[...], hv, TN_DIMS, preferred_element_type=F32)
        dwd_ref[...] += lax.dot_general(a, dd, TN_DIMS, preferred_element_type=F32)

    act = pl.BlockSpec((None, tm, Fs), lambda j, i: (j, i, 0))
    wspec = pl.BlockSpec((None, Fs, D), lambda j, i: (j, 0, 0))
    return pl.pallas_call(
        body, name=name, grid=(nsh, S // tm),
        in_specs=[pl.BlockSpec((tm, D), lambda j, i: (i, 0)), pl.BlockSpec((tm, D), lambda j, i: (i, 0)), act, act, act, act],
        out_specs=[wspec, wspec, wspec], out_shape=[jax.ShapeDtypeStruct((nsh, Fs, D), F32)] * 3,
        compiler_params=_params(2),
    )(h, dxo, g, u, dg, du)


NEG_BIG = -1e30


def _causal_pairs(n, by_key):
    pairs = [(qi, ki) for qi in range(n) for ki in range(qi + 1)]
    if by_key:
        pairs.sort(key=lambda p: (p[1], p[0]))
    return jnp.asarray([p[0] for p in pairs], jnp.int32), jnp.asarray([p[1] for p in pairs], jnp.int32)


def _scores(q, k, masked):
    s = lax.dot_general(q, k, (((1,), (1,)), ((), ())), preferred_element_type=F32)
    if masked:
        row = lax.broadcasted_iota(jnp.int32, s.shape, 0)
        col = lax.broadcasted_iota(jnp.int32, s.shape, 1)
        s = jnp.where(row >= col, s, NEG_BIG)
    return s


def attn_fwd(name, q, k, v, t=ATTN_TILE):
    S, Dk = q.shape[0], HEAD_PAD
    H = q.shape[1] // Dk
    Dv = v.shape[1] // H
    qt, kt = _causal_pairs(S // t, by_key=False)

    def body(qt_ref, kt_ref, q_ref, k_ref, v_ref, o_ref, lse_ref, m_sc, l_sc, acc):
        qi, ki = qt_ref[pl.program_id(1)], kt_ref[pl.program_id(1)]

        @pl.when(ki == 0)
        def _():
            m_sc[...] = jnp.full_like(m_sc, NEG_BIG)
            l_sc[...] = jnp.zeros_like(l_sc)
            acc[...] = jnp.zeros_like(acc)

        def step(masked):
            s = _scores(q_ref[...], k_ref[...], masked)
            m_prev = m_sc[...]
            m_next = jnp.maximum(m_prev, jnp.max(s, axis=-1, keepdims=True))
            alpha = jnp.exp(m_prev - m_next)
            p = jnp.exp(s - jnp.tile(m_next, (1, t // STAT_LANES)))
            l_sc[...] = alpha * l_sc[...] + jnp.sum(p, axis=-1, keepdims=True)
            acc[...] = alpha * acc[...] + jnp.dot(p.astype(BF16), v_ref[...].astype(BF16), preferred_element_type=F32)
            m_sc[...] = m_next

        @pl.when(ki < qi)
        def _():
            step(False)

        @pl.when(ki == qi)
        def _():
            step(True)
            o_ref[...] = acc[...] / l_sc[...]
            lse_ref[...] = m_sc[...] + jnp.log(l_sc[...])

    stat = pltpu.VMEM((t, STAT_LANES), F32)
    return pl.pallas_call(
        body, name=name,
        grid_spec=pltpu.PrefetchScalarGridSpec(
            num_scalar_prefetch=2, grid=(H, qt.shape[0]),
            in_specs=[pl.BlockSpec((t, Dk), lambda h, s, qt, kt: (qt[s], h)),
                      pl.BlockSpec((t, Dk), lambda h, s, qt, kt: (kt[s], h)),
                      pl.BlockSpec((t, Dv), lambda h, s, qt, kt: (kt[s], h))],
            out_specs=[pl.BlockSpec((t, Dv), lambda h, s, qt, kt: (qt[s], h)),
                       pl.BlockSpec((None, t, STAT_LANES), lambda h, s, qt, kt: (h, qt[s], 0))],
            scratch_shapes=[stat, stat, pltpu.VMEM((t, Dv), F32)]),
        out_shape=[jax.ShapeDtypeStruct((S, H * Dv), F32), jax.ShapeDtypeStruct((H, S, STAT_LANES), F32)],
        compiler_params=_params(2),
    )(qt, kt, q, k, v)


def attn_bwd(name, q, k, v, do, o, lse, t=ATTN_TILE):
    S, Dk = q.shape[0], HEAD_PAD
    H = q.shape[1] // Dk
    Dv = v.shape[1] // H
    qt, kt = _causal_pairs(S // t, by_key=True)
    tn_dims = (((0,), (0,)), ((), ()))

    def body(qt_ref, kt_ref, q_ref, k_ref, v_ref, do_ref, o_ref, lse_ref, dq_ref, dk_ref, dv_ref):
        step_id = pl.program_id(1)
        qi, ki = qt_ref[step_id], kt_ref[step_id]

        @pl.when(step_id == 0)
        def _():
            dq_ref[...] = jnp.zeros_like(dq_ref)

        def step(masked):
            s = _scores(q_ref[...], k_ref[...], masked)
            reps = (1, t // STAT_LANES)
            p = jnp.exp(s - jnp.tile(lse_ref[...], reps))
            dov = do_ref[...]
            delta = jnp.broadcast_to(jnp.sum(dov * o_ref[...], axis=-1, keepdims=True), (t, STAT_LANES))
            dob = dov.astype(BF16)
            dp = lax.dot_general(dob, v_ref[...].astype(BF16), (((1,), (1,)), ((), ())), preferred_element_type=F32)
            ds = (p * (dp - jnp.tile(delta, reps))).astype(BF16)
            pdv = lax.dot_general(p.astype(BF16), dob, tn_dims, preferred_element_type=F32)
            pdk = lax.dot_general(ds, q_ref[...], tn_dims, preferred_element_type=F32)
            rows = pl.ds(pl.multiple_of(qi * t, t), t)
            dq_ref[rows, :] += jnp.dot(ds, k_ref[...], preferred_element_type=F32)
            return pdk, pdv

        @pl.when(ki == qi)
        def _():
            dk_ref[...] = jnp.zeros_like(dk_ref)
            dv_ref[...] = jnp.zeros_like(dv_ref)

        def accumulate(masked):
            pdk, pdv = step(masked)
            dk_ref[...] += pdk
            dv_ref[...] += pdv

        @pl.when(ki == qi)
        def _():
            accumulate(True)

        @pl.when(ki < qi)
        def _():
            accumulate(False)

    qrow = lambda h, s, qt, kt: (qt[s], h)
    krow = lambda h, s, qt, kt: (kt[s], h)
    return pl.pallas_call(
        body, name=name,
        grid_spec=pltpu.PrefetchScalarGridSpec(
            num_scalar_prefetch=2, grid=(H, qt.shape[0]),
            in_specs=[pl.BlockSpec((t, Dk), qrow), pl.BlockSpec((t, Dk), krow), pl.BlockSpec((t, Dv), krow),
                      pl.BlockSpec((t, Dv), qrow), pl.BlockSpec((t, Dv), qrow),
                      pl.BlockSpec((None, t, STAT_LANES), lambda h, s, qt, kt: (h, qt[s], 0))],
            out_specs=[pl.BlockSpec((S, Dk), lambda h, s, qt, kt: (0, h)), pl.BlockSpec((t, Dk), krow),
                       pl.BlockSpec((t, Dv), krow)]),
        out_shape=[jax.ShapeDtypeStruct((S, H * Dk), F32), jax.ShapeDtypeStruct((S, H * Dk), F32),
                   jax.ShapeDtypeStruct((S, H * Dv), F32)],
        compiler_params=_params(2),
    )(qt, kt, q, k, v, do, o, lse)


def _cmul(ar, ai, br, bi):
    return ar * br - ai * bi, ar * bi + ai * br


def _scan_tile(x_r, x_i, ar_ref, ai_ref, cr_sc, ci_sc, *, reverse, first, states=None):
    tc, lanes = x_r.shape
    nblk, lb = tc // SUBLANES, SCAN_LANE_TILE
    with_da = states is not None
    if with_da:
        xr_all, xi_all, pr_all, pi_all, dar_all, dai_all, chunk = states

    @pl.when(first)
    def _():
        cr_sc[...] = jnp.zeros_like(cr_sc)
        ci_sc[...] = jnp.zeros_like(ci_sc)
        if with_da:
            dar_all[...] = jnp.zeros_like(dar_all)
            dai_all[...] = jnp.zeros_like(dai_all)

    row = lax.broadcasted_iota(jnp.int32, (SUBLANES, lb), 0)
    for l0 in range(0, lanes, lb):
        _scan_lanes(x_r.at[:, pl.ds(l0, lb)], x_i.at[:, pl.ds(l0, lb)], ar_ref[0:1, pl.ds(l0, lb)],
                    ai_ref[0:1, pl.ds(l0, lb)], cr_sc.at[:, pl.ds(l0, lb)], ci_sc.at[:, pl.ds(l0, lb)], row, reverse,
                    nblk, None if not with_da else tuple(r.at[:, pl.ds(l0, lb)] for r in states[:6]) + (chunk,))


def _scan_lanes(x_r, x_i, a1r, a1i, cr_sc, ci_sc, row, reverse, nblk, states):
    lb = x_r.shape[1]
    with_da = states is not None
    if with_da:
        xr_ref, xi_ref, pr_ref, pi_ref, dar_ref, dai_ref, chunk = states
    if reverse:
        a1i = -a1i
    a2r, a2i = _cmul(a1r, a1i, a1r, a1i)
    a4r, a4i = _cmul(a2r, a2i, a2r, a2i)
    pw_r, pw_i = jnp.zeros((SUBLANES, lb), F32), jnp.zeros((SUBLANES, lb), F32)
    cur_r, cur_i = a1r, a1i
    for e in range(SUBLANES):
        r_at = (SUBLANES - 1 - e) if reverse else e
        pw_r = jnp.where(row == r_at, cur_r, pw_r)
        pw_i = jnp.where(row == r_at, cur_i, pw_i)
        cur_r, cur_i = _cmul(cur_r, cur_i, a1r, a1i)
    steps = []
    for d, pr, pi in ((1, a1r, a1i), (2, a2r, a2i), (4, a4r, a4i)):
        keep = (row < SUBLANES - d) if reverse else (row >= d)
        steps.append((d, jnp.where(keep, pr, 0.0), jnp.where(keep, pi, 0.0)))

    def block(jb, carry):
        if with_da:
            cr, ci, acc_r, acc_i = carry
        else:
            cr, ci = carry
        idx = (nblk - 1 - jb) if reverse else jb
        r0 = pl.multiple_of(idx * SUBLANES, SUBLANES)
        xr = x_r[pl.ds(r0, SUBLANES), :]
        xi = x_i[pl.ds(r0, SUBLANES), :]
        for d, pr, pi in steps:
            shift = SUBLANES - d if reverse else d
            sr, si = pltpu.roll(xr, shift, 0), pltpu.roll(xi, shift, 0)
            xr, xi = xr + pr * sr - pi * si, xi + pr * si + pi * sr
        xr, xi = xr + pw_r * cr - pw_i * ci, xi + pw_r * ci + pw_i * cr
        x_r[pl.ds(r0, SUBLANES), :] = xr
        x_i[pl.ds(r0, SUBLANES), :] = xi
        edge = 0 if reverse else SUBLANES - 1
        cr, ci = xr[edge:edge + 1, :], xi[edge:edge + 1, :]
        if not with_da:
            return cr, ci
        fr = xr_ref[pl.ds(r0, SUBLANES), :]
        fi = xi_ref[pl.ds(r0, SUBLANES), :]
        rp = pl.multiple_of(jnp.maximum(idx - 1, 0) * SUBLANES, SUBLANES)
        inside = idx > 0
        before_r = jnp.where(inside, xr_ref[pl.ds(rp, SUBLANES), :], pr_ref[...])
        before_i = jnp.where(inside, xi_ref[pl.ds(rp, SUBLANES), :], pi_ref[...])
        live = jnp.where(jnp.logical_or(inside, chunk > 0), 1.0, 0.0)
        last_r = before_r[SUBLANES - 1:SUBLANES, :] * live
        last_i = before_i[SUBLANES - 1:SUBLANES, :] * live
        pvr = jnp.where(row == 0, last_r, pltpu.roll(fr, 1, 0))
        pvi = jnp.where(row == 0, last_i, pltpu.roll(fi, 1, 0))
        acc_r = acc_r + xr * pvr + xi * pvi
        acc_i = acc_i + xi * pvr - xr * pvi
        return cr, ci, acc_r, acc_i

    init = (cr_sc[...], ci_sc[...])
    if with_da:
        init = init + (jnp.zeros((SUBLANES, lb), F32), jnp.zeros((SUBLANES, lb), F32))
    fin = lax.fori_loop(0, nblk, block, init)
    cr_sc[...] = fin[0]
    ci_sc[...] = fin[1]
    if with_da:
        dar_ref[...] += fin[2]
        dai_ref[...] += fin[3]


SSM_BLOCKS = 4
BLOCK_CH = SSM_WIDTH // SSM_BLOCKS
PREP_LANES = SSM_LANES // SSM_BLOCKS


def ssm_fwd(name, u, bb_r, bb_i, cb_r, ncb_i, a_r8, a_i8):
    S = u.shape[0]
    tc = min(SCAN_TIME_TILE, S)

    def body(u_ref, bbr_ref, bbi_ref, cbr_ref, ncbi_ref, ar_ref, ai_ref, xr_ref, xi_ref, y_ref, cr_sc, ci_sc):
        ub = u_ref[...].astype(BF16)
        xr_ref[...] = jnp.dot(ub, bbr_ref[...], preferred_element_type=F32)
        xi_ref[...] = jnp.dot(ub, bbi_ref[...], preferred_element_type=F32)
        _scan_tile(xr_ref, xi_ref, ar_ref, ai_ref, cr_sc, ci_sc, reverse=False, first=pl.program_id(1) == 0)
        y_ref[...] = (lax.dot_general(xr_ref[...].astype(BF16), cbr_ref[...], NT_DIMS, preferred_element_type=F32)
                      + lax.dot_general(xi_ref[...].astype(BF16), ncbi_ref[...], NT_DIMS, preferred_element_type=F32))

    ch = pl.BlockSpec((tc, BLOCK_CH), lambda b, t: (t, b))
    st = pl.BlockSpec((tc, PREP_LANES), lambda b, t: (t, b))
    wt = pl.BlockSpec((None, BLOCK_CH, PREP_LANES), lambda b, t: (b, 0, 0))
    par = pl.BlockSpec((SUBLANES, PREP_LANES), lambda b, t: (0, b))
    return pl.pallas_call(
        body, name=name, grid=(SSM_BLOCKS, S // tc), in_specs=[ch, wt, wt, wt, wt, par, par], out_specs=[st, st, ch],
        out_shape=[jax.ShapeDtypeStruct((S, SSM_LANES), F32), jax.ShapeDtypeStruct((S, SSM_LANES), F32),
                   jax.ShapeDtypeStruct((S, SSM_WIDTH), F32)],
        scratch_shapes=[pltpu.VMEM((1, PREP_LANES), F32), pltpu.VMEM((1, PREP_LANES), F32)], compiler_params=_params(2),
    )(u, bb_r, bb_i, cb_r, ncb_i, a_r8, a_i8)


def ssm_bwd(name, dy, du_add, u, xs_r, xs_i, bb_r, bb_i, cb_r, ncb_i, a_r8, a_i8):
    S = u.shape[0]
    tc = min(SCAN_TIME_TILE, S)
    nt = S // tc

    def body(dy_ref, dua_ref, u_ref, xr_ref, xi_ref, pr_ref, pi_ref, bbr_ref, bbi_ref, cbr_ref, ncbi_ref, ar_ref, ai_ref,
             du_ref, dbbr_ref, dbbi_ref, dcbr_ref, dncbi_ref, dar_ref, dai_ref, lr_sc, li_sc, cr_sc, ci_sc):
        t = pl.program_id(1)
        first = t == 0

        @pl.when(first)
        def _():
            for r in (dbbr_ref, dbbi_ref, dcbr_ref, dncbi_ref):
                r[...] = jnp.zeros_like(r)

        dyb = dy_ref[...].astype(BF16)
        lr_sc[...] = jnp.dot(dyb, cbr_ref[...], preferred_element_type=F32)
        li_sc[...] = jnp.dot(dyb, ncbi_ref[...], preferred_element_type=F32)
        _scan_tile(lr_sc, li_sc, ar_ref, ai_ref, cr_sc, ci_sc, reverse=True, first=first,
                   states=(xr_ref, xi_ref, pr_ref, pi_ref, dar_ref, dai_ref, nt - 1 - t))
        lrb, lib = lr_sc[...].astype(BF16), li_sc[...].astype(BF16)
        du_ref[...] = (lax.dot_general(lrb, bbr_ref[...], NT_DIMS, preferred_element_type=F32)
                       + lax.dot_general(lib, bbi_ref[...], NT_DIMS, preferred_element_type=F32) + dua_ref[...])
        ub = u_ref[...].astype(BF16)
        dbbr_ref[...] += lax.dot_general(ub, lrb, TN_DIMS, preferred_element_type=F32)
        dbbi_ref[...] += lax.dot_general(ub, lib, TN_DIMS, preferred_element_type=F32)
        dcbr_ref[...] += lax.dot_general(dyb, xr_ref[...].astype(BF16), TN_DIMS, preferred_element_type=F32)
        dncbi_ref[...] += lax.dot_general(dyb, xi_ref[...].astype(BF16), TN_DIMS, preferred_element_type=F32)

    ch = pl.BlockSpec((tc, BLOCK_CH), lambda b, t: (nt - 1 - t, b))
    st = pl.BlockSpec((tc, PREP_LANES), lambda b, t: (nt - 1 - t, b))
    prev = pl.BlockSpec((SUBLANES, PREP_LANES), lambda b, t: (jnp.maximum((nt - 1 - t) * (tc // SUBLANES) - 1, 0), b))
    wt = pl.BlockSpec((None, BLOCK_CH, PREP_LANES), lambda b, t: (b, 0, 0))
    par = pl.BlockSpec((SUBLANES, PREP_LANES), lambda b, t: (0, b))
    blk = jax.ShapeDtypeStruct((SSM_BLOCKS, BLOCK_CH, PREP_LANES), F32)
    rows8 = jax.ShapeDtypeStruct((SUBLANES, SSM_LANES), F32)
    return pl.pallas_call(
        body, name=name, grid=(SSM_BLOCKS, nt), in_specs=[ch, ch, ch, st, st, prev, prev, wt, wt, wt, wt, par, par],
        out_specs=[ch, wt, wt, wt, wt, par, par],
        out_shape=[jax.ShapeDtypeStruct((S, SSM_WIDTH), F32), blk, blk, blk, blk, rows8, rows8],
        scratch_shapes=[pltpu.VMEM((tc, PREP_LANES), F32), pltpu.VMEM((tc, PREP_LANES), F32),
                        pltpu.VMEM((1, PREP_LANES), F32), pltpu.VMEM((1, PREP_LANES), F32)],
        compiler_params=_params(2),
    )(dy, du_add, u, xs_r, xs_i, xs_r, xs_i, bb_r, bb_i, cb_r, ncb_i, a_r8, a_i8)


def _ssm_prep_f(a_re, a_im, log_dt, bt_re, bt_im, c_re, c_im):
    first_group = pl.program_id(0) * (SSM_GROUPS // SSM_BLOCKS)
    iota = lambda shape, d: lax.broadcasted_iota(jnp.int32, shape, d)
    grp_of_row = lambda shape: iota(shape, 0) >> int(math.log2(SSM_GROUP))
    grp_of_lane = lambda shape: iota(shape, 1) >> int(math.log2(SSM_STATE))
    rep = (grp_of_row((BLOCK_CH, SSM_GROUPS)) + first_group == iota((BLOCK_CH, SSM_GROUPS), 1)).astype(F32)
    til = ((iota((SSM_STATE, PREP_LANES), 1) & (SSM_STATE - 1)) == iota((SSM_STATE, PREP_LANES), 0)).astype(F32)
    m_rows = (grp_of_row((BLOCK_CH, PREP_LANES)) == grp_of_lane((BLOCK_CH, PREP_LANES))).astype(F32)
    m_grp = (iota((SSM_GROUPS, PREP_LANES), 0) == grp_of_lane((SSM_GROUPS, PREP_LANES)) + first_group).astype(F32)
    dt = jnp.exp(log_dt)
    decay = jnp.exp(a_re * dt)
    ar = decay * jnp.cos(a_im * dt)
    ai = decay * jnp.sin(a_im * dt)
    den = a_re * a_re + a_im * a_im
    nr = ar - 1.0
    coef_r = (nr * a_re + ai * a_im) / den
    coef_i = (ai * a_re - nr * a_im) / den
    cr, ci = _hp_dot(rep, coef_r), _hp_dot(rep, coef_i)
    bb_r = cr * bt_re - ci * bt_im
    bb_i = cr * bt_im + ci * bt_re
    big = lambda m: _hp_dot(m, til) * m_rows
    lanes = lambda m: jnp.broadcast_to(jnp.sum(_hp_dot(m, til) * m_grp, axis=0, keepdims=True), (SUBLANES, PREP_LANES))
    return lanes(ar), lanes(ai), big(bb_r), big(bb_i), big(c_re), -big(c_im)


def _whole(arr, **kw):
    nd = arr.ndim
    return In(arr, arr.shape, lambda *_: (0,) * nd, **kw)


def _adamw_math(w, g, m, v):
    m = ADAM_B1 * m + (1.0 - ADAM_B1) * g
    v = ADAM_B2 * v + (1.0 - ADAM_B2) * (g * g)
    m_hat = m / (1.0 - ADAM_B1 ** ADAM_STEP)
    v_hat = v / (1.0 - ADAM_B2 ** ADAM_STEP)
    delta = -ADAM_LR * (m_hat / (jnp.sqrt(v_hat) + ADAM_EPS) + ADAM_WD * w)
    return delta, m, v


def adamw_big(name, w, g, m, v, after):
    R, C = w.shape
    tr = R
    for cand in (512, 344, 256, 128):
        if R % cand == 0:
            tr = cand
            break

    def body(w_ref, g_ref, m_ref, v_ref, after_ref, d_ref, nm_ref, nv_ref):
        d, nm, nv = _adamw_math(w_ref[...], g_ref[...], m_ref[...], v_ref[...])
        d_ref[...] = d
        nm_ref[...] = nm
        nv_ref[...] = nv

    spec = pl.BlockSpec((tr, C), lambda i: (i, 0))
    return pl.pallas_call(
        body, name=name, grid=(R // tr,), in_specs=[spec] * 4 + [pl.BlockSpec(memory_space=pl.ANY)], out_specs=[spec] * 3,
        out_shape=[jax.ShapeDtypeStruct((R, C), F32)] * 3, compiler_params=_params(1),
    )(w, g, m, v, after)


def adamw_small(name, ws, gs, ms, vs, after):
    n = len(ws)

    def body(*refs):
        outs = refs[4 * n + 1:]
        for k in range(n):
            d, nm, nv = _adamw_math(refs[k][...], refs[n + k][...], refs[2 * n + k][...], refs[3 * n + k][...])
            outs[k][...] = d
            outs[n + k][...] = nm
            outs[2 * n + k][...] = nv

    vm = pl.BlockSpec(memory_space=pltpu.VMEM)
    shapes = [jax.ShapeDtypeStruct(w.shape, F32) for w in ws]
    res = pl.pallas_call(
        body, name=name, in_specs=[vm] * (4 * n) + [pl.BlockSpec(memory_space=pl.ANY)], out_specs=[vm] * (3 * n),
        out_shape=shapes * 3, compiler_params=pltpu.CompilerParams(vmem_limit_bytes=VMEM_LIMIT_BYTES),
    )(*ws, *gs, *ms, *vs, after)
    return res[:n], res[n:2 * n], res[2 * n:]


def _place():
    return lax.axis_index("x"), lax.axis_index("y"), lax.axis_index("c")


def _other_chips(x, y):
    return [(1 - x, y), (x, 1 - y), (1 - x, 1 - y)]


HBM = pl.BlockSpec(memory_space=pl.ANY)


def all_gather_halves(name, blocks):
    n = len(blocks)

    def body(*refs):
        in_refs, out_refs = refs[:n], refs[n:2 * n]
        send_sems, recv_sems = refs[2 * n:]
        x, y, c = _place()
        me, sibling = (x, y, c), (x, y, 1 - c)
        chips = _other_chips(x, y)

        def slot(a, px, py, pc):
            return out_refs[a].at[4 * px + 2 * py + pc]

        def copy(a, k, block, to, src=None):
            return pltpu.make_async_remote_copy(
                src_ref=slot(a, *block) if src is None else src, dst_ref=slot(a, *block),
                send_sem=send_sems.at[a, k], recv_sem=recv_sems.at[a, k], device_id=to, device_id_type=MESH)

        first = []
        for a in range(n):
            first.append(copy(a, 0, me, sibling, src=in_refs[a]))
            first += [copy(a, 1 + j, me, (*chip, c), src=in_refs[a]) for j, chip in enumerate(chips)]
        for cp in first:
            cp.start()
        passed = []
        for j, chip in enumerate(chips):
            for a in range(n):
                copy(a, 1 + j, (*chip, c), me).wait_recv()
                fw = copy(a, 4 + j, (*chip, c), sibling)
                fw.start()
                passed.append(fw)
        for a in range(n):
            copy(a, 0, sibling, me).wait_recv()
            for j, chip in enumerate(chips):
                copy(a, 4 + j, (*chip, 1 - c), me).wait_recv()
        for cp in first + passed:
            cp.wait_send()

    res = pl.pallas_call(
        body, name=name, in_specs=[HBM] * n, out_specs=[HBM] * n,
        out_shape=[jax.ShapeDtypeStruct((N_DEV,) + b.shape, b.dtype) for b in blocks],
        scratch_shapes=[pltpu.SemaphoreType.DMA((n, 7)), pltpu.SemaphoreType.DMA((n, 7))],
    )(*blocks)
    return list(res)


def pair_send_halves(name, grads):
    n = len(grads)

    def body(*refs):
        in_refs, out_refs = refs[:n], refs[n:2 * n]
        send_sems, recv_sems = refs[2 * n:]
        x, y, c = _place()
        cps = []
        for a in range(n):
            cp = pltpu.make_async_remote_copy(
                src_ref=in_refs[a].at[:, 1 - c], dst_ref=out_refs[a], send_sem=send_sems.at[a], recv_sem=recv_sems.at[a],
                device_id=(x, y, 1 - c), device_id_type=MESH)
            cp.start()
            cps.append(cp)
        for cp in cps:
            cp.wait()

    res = pl.pallas_call(
        body, name=name, in_specs=[HBM] * n, out_specs=[HBM] * n,
        out_shape=[jax.ShapeDtypeStruct((g.shape[0],) + g.shape[2:], g.dtype) for g in grads],
        scratch_shapes=[pltpu.SemaphoreType.DMA((n,)), pltpu.SemaphoreType.DMA((n,))],
    )(*grads)
    return list(res)


def pair_add(name, grad, got, c_arr):
    nsh, _, M, N = grad.shape
    tr = M
    for cand in (2560, 512, 256, 192, 128, 64, 16):
        if M % cand == 0:
            tr = cand
            break

    def body(c_ref, g_ref, p_ref, o_ref):
        o_ref[...] = (g_ref[...] + p_ref[...]).astype(BF16)

    return pl.pallas_call(
        body, name=name,
        grid_spec=pltpu.PrefetchScalarGridSpec(
            num_scalar_prefetch=1, grid=(nsh, M // tr),
            in_specs=[pl.BlockSpec((None, None, tr, N), lambda j, i, c_ref: (j, c_ref[0], i, 0)),
                      pl.BlockSpec((None, tr, N), lambda j, i, c_ref: (j, i, 0))],
            out_specs=pl.BlockSpec((None, tr, N), lambda j, i, c_ref: (j, i, 0))),
        out_shape=jax.ShapeDtypeStruct((nsh, M, N), BF16), compiler_params=_params(2),
    )(c_arr, grad, got)


def sum_chips(name, q):
    nsh, M, N = q.shape
    tr = M
    for cand in (2560, 512, 256, 192, 128, 64, 16):
        if M % cand == 0:
            tr = cand
            break

    def body(q_ref, o_ref):
        acc = q_ref[0].astype(F32)
        for j in range(1, nsh):
            acc = acc + q_ref[j].astype(F32)
        o_ref[...] = acc

    return pl.pallas_call(
        body, name=name, grid=(M // tr,), in_specs=[pl.BlockSpec((nsh, tr, N), lambda i: (0, i, 0))],
        out_specs=pl.BlockSpec((tr, N), lambda i: (i, 0)), out_shape=jax.ShapeDtypeStruct((M, N), F32),
        compiler_params=_params(1),
    )(q)


def pair_exchange(name, halves):
    n = len(halves)

    def body(*refs):
        in_refs, out_refs = refs[:n], refs[n:2 * n]
        send_sems, recv_sems = refs[2 * n:]
        x, y, c = _place()
        cps = []
        for a in range(n):
            cp = pltpu.make_async_remote_copy(
                src_ref=in_refs[a], dst_ref=out_refs[a], send_sem=send_sems.at[a], recv_sem=recv_sems.at[a],
                device_id=(x, y, 1 - c), device_id_type=MESH)
            cp.start()
            cps.append(cp)
        for cp in cps:
            cp.wait()

    res = pl.pallas_call(
        body, name=name, in_specs=[HBM] * n, out_specs=[HBM] * n,
        out_shape=[jax.ShapeDtypeStruct(h.shape, h.dtype) for h in halves],
        scratch_shapes=[pltpu.SemaphoreType.DMA((n,)), pltpu.SemaphoreType.DMA((n,))],
    )(*halves)
    return list(res)


SEM = pl.BlockSpec(memory_space=pltpu.SEMAPHORE)
IN_HBM = pl.BlockSpec(memory_space=pltpu.HBM)
SPLIT_COPY = pltpu.CompilerParams(has_side_effects=pltpu.SideEffectType.DATAFLOW_SIDE_EFFECTING)


def _scatter_copies(src_refs, dst_refs, send_sems, recv_sems):
    x, y, c = _place()
    mine = 2 * x + y
    return [pltpu.make_async_remote_copy(
        src_ref=src_refs[a].at[2 * px + py], dst_ref=dst_refs[a].at[mine], send_sem=send_sems.at[a * (N_CHIPS - 1) + k],
        recv_sem=recv_sems.at[a * (N_CHIPS - 1) + k], device_id=(px, py, c), device_id_type=MESH)
        for a in range(len(src_refs)) for k, (px, py) in enumerate(_other_chips(x, y))]


def _pair_copies(src_refs, dst_refs, send_sems, recv_sems):
    x, y, c = _place()
    return [pltpu.make_async_remote_copy(
        src_ref=src_refs[a].at[:, 1 - c], dst_ref=dst_refs[a], send_sem=send_sems.at[a], recv_sem=recv_sems.at[a],
        device_id=(x, y, 1 - c), device_id_type=MESH) for a in range(len(src_refs))]


def _gather_copies(src_refs, dst_refs, send_sems, recv_sems):
    x, y, c = _place()
    me = 4 * x + 2 * y + c
    cps = []
    for a in range(len(src_refs)):
        for k in range(1, N_DEV):
            to = (1 - x if k & 4 else x, 1 - y if k & 2 else y, 1 - c if k & 1 else c)
            s = a * (N_DEV - 1) + k - 1
            cps.append(pltpu.make_async_remote_copy(
                src_ref=src_refs[a], dst_ref=dst_refs[a].at[me], send_sem=send_sems.at[s], recv_sem=recv_sems.at[s],
                device_id=to, device_id_type=MESH))
    return cps


def split_copy_start(name, copies, n_sem, srcs, land_shapes):
    n = len(srcs)
    lands = [lax.empty(s.shape, s.dtype) for s in land_shapes]

    def body(*refs):
        for cp in copies(refs[:n], refs[n:2 * n], refs[2 * n], refs[2 * n + 1]):
            cp.start()
        refs[-1][...] = jnp.zeros_like(refs[-1])

    thru = [pltpu.HBM(a.shape, a.dtype) for a in (*srcs, *lands)]
    res = pl.pallas_call(
        body, name=name, in_specs=[IN_HBM] * (2 * n),
        out_specs=(SEM, SEM, *[IN_HBM] * (2 * n), pl.BlockSpec(memory_space=pltpu.VMEM)),
        out_shape=(pltpu.SemaphoreType.DMA((n * n_sem,)), pltpu.SemaphoreType.DMA((n * n_sem,)), *thru,
                   jax.ShapeDtypeStruct((SUBLANES, LANES), F32)),
        input_output_aliases={i: 2 + i for i in range(2 * n)}, compiler_params=SPLIT_COPY,
    )(*[pltpu.with_memory_space_constraint(a, pltpu.HBM) for a in (*srcs, *lands)])
    return (copies, n, res[0], res[1], res[2:2 + 2 * n]), res[-1][0, 0]


def split_copy_wait(name, handle, after):
    copies, n, send_sems, recv_sems, thru = handle

    def body(*refs):
        for cp in copies(refs[:n], refs[n:2 * n], refs[2 * n], refs[2 * n + 1]):
            cp.wait_send()
            cp.wait_recv()

    res = pl.pallas_call(
        body, name=name, in_specs=[IN_HBM] * (2 * n) + [SEM, SEM, pl.BlockSpec(memory_space=pl.ANY)],
        out_specs=[IN_HBM] * (2 * n), out_shape=[pltpu.HBM(a.shape, a.dtype) for a in thru],
        input_output_aliases={i: i for i in range(2 * n)}, compiler_params=SPLIT_COPY,
    )(*thru, send_sems, recv_sems, after)
    return list(res[:n]), list(res[n:])


def all_reduce_small(name, v):
    R, C = v.shape

    def body(v_ref, o_ref, gath, send_sems, recv_sems):
        x, y, c = _place()
        me, sibling = (x, y, c), (x, y, 1 - c)
        chips = _other_chips(x, y)

        def slot(px, py, pc):
            return gath.at[4 * px + 2 * py + pc]

        def copy(k, block, to, src=None):
            return pltpu.make_async_remote_copy(
                src_ref=slot(*block) if src is None else src, dst_ref=slot(*block),
                send_sem=send_sems.at[k], recv_sem=recv_sems.at[k], device_id=to, device_id_type=MESH)

        first = [copy(0, me, sibling, src=v_ref)]
        first += [copy(1 + j, me, (*chip, c), src=v_ref) for j, chip in enumerate(chips)]
        for cp in first:
            cp.start()
        slot(*me)[...] = v_ref[...]
        passed = [copy(4 + j, (*chip, c), sibling) for j, chip in enumerate(chips)]
        for j, chip in enumerate(chips):
            copy(1 + j, (*chip, c), me).wait_recv()
            passed[j].start()
        copy(0, sibling, me).wait_recv()
        for j, chip in enumerate(chips):
            copy(4 + j, (*chip, 1 - c), me).wait_recv()
        for cp in first + passed:
            cp.wait_send()
        acc = gath[0]
        for d in range(1, N_DEV):
            acc = acc + gath[d]
        o_ref[...] = acc

    vm = pl.BlockSpec(memory_space=pltpu.VMEM)
    return pl.pallas_call(
        body, name=name, in_specs=[vm], out_specs=vm, out_shape=jax.ShapeDtypeStruct((R, C), F32),
        scratch_shapes=[pltpu.VMEM((N_DEV, R, C), F32), pltpu.SemaphoreType.DMA((7,)), pltpu.SemaphoreType.DMA((7,))],
        compiler_params=pltpu.CompilerParams(vmem_limit_bytes=VMEM_LIMIT_BYTES),
    )(v)


LANES = 128
PACK_ROW_MULTIPLE = 1024
SMALL_SHARDED = {
    "w_in": ((D_MODEL, 1216), 1), "mla_w_uq": ((MLA_Q_RANK, 768), 1), "mla_w_ukv": ((MLA_KV_RANK, 1024), 1),
    "ssm_w_glu": ((SSM_WIDTH, SSM_WIDTH), 0), "w_o": ((D_MODEL, D_MODEL), 0), "xattn_w_q": ((D_MODEL, 512), 0),
    "xattn_w_kv": ((D_MODEL, 1024), 0), "xattn_w_o": ((512, D_MODEL), 1),
}
FFN_NAMES = ["ffn1_w_gate", "ffn1_w_up", "ffn1_w_down", "ffn2_w_gate", "ffn2_w_up", "ffn2_w_down"]
TRANSPOSED_VIEW = ("ffn1_w_gate", "ffn1_w_up", "ffn2_w_gate", "ffn2_w_up", "w_in", "mla_w_uq")


def _shard_shape(name):
    (r, cdim), ax = SMALL_SHARDED[name]
    return (r // N_CHIPS, cdim) if ax == 0 else (r, cdim // N_CHIPS)


def _pack_shards(shards):
    parts = []
    for name in SMALL_SHARDED:
        a = shards[name]
        lead = a.shape[:-2]
        parts.append(a.reshape(lead + (a.shape[-2] * a.shape[-1] // LANES, LANES)))
    rows = sum(q.shape[-2] for q in parts)
    parts.append(jnp.zeros(lead + (-rows % PACK_ROW_MULTIPLE, LANES), parts[0].dtype))
    return jnp.concatenate(parts, axis=-2)


def _unpack_shards(packed):
    out, r0 = {}, 0
    lead = packed.shape[:-2]
    for name in SMALL_SHARDED:
        r, cdim = _shard_shape(name)
        rows = r * cdim // LANES
        out[name] = packed[..., r0:r0 + rows, :].reshape(lead + (r, cdim))
        r0 += rows
    return out


def _full_from_shards(name, sh):
    (r, cdim), ax = SMALL_SHARDED[name]
    if ax == 0:
        return sh.reshape(r, cdim)
    return jnp.transpose(sh, (1, 0, 2)).reshape(r, cdim)


def _shards_from_full(name, full):
    (r, cdim), ax = SMALL_SHARDED[name]
    if ax == 0:
        return full.reshape(N_CHIPS, r // N_CHIPS, cdim)
    return jnp.transpose(full.reshape(r, N_CHIPS, cdim // N_CHIPS), (1, 0, 2))


SMALL_REPL = {
    "ffn1_norm": (1, 1024), "mix_norm": (1, 1024), "mla_q_norm": (1, 384), "mla_kv_norm": (1, 256),
    "mla_qk_norm_q": (1, 192), "mla_qk_norm_k": (1, 192), "ssm_a_re": (32, 64), "ssm_a_im": (32, 64),
    "ssm_log_dt": (32, 1), "ssm_b_re": (32, 64, 16), "ssm_b_im": (32, 64, 16), "ssm_c_re": (32, 16, 64),
    "ssm_c_im": (32, 16, 64), "ssm_d": (1, 512), "ssm_b_glu": (1, 512), "out_norm_mla": (1, 512),
    "out_norm_ssm": (1, 512), "xattn_norm": (1, 1024), "mem_norm": (1, 1024), "xattn_q_norm": (1, 128),
    "xattn_k_norm": (1, 128), "ffn2_norm": (1, 1024),
}


def _pack_repl(grads, loss):
    flat = jnp.concatenate([grads[n].reshape(-1) for n in SMALL_REPL] + [loss.reshape(1)])
    rows = -(-flat.shape[0] // (LANES * SUBLANES)) * SUBLANES
    return jnp.pad(flat, (0, rows * LANES - flat.shape[0])).reshape(rows, LANES)


def _unpack_repl(packed):
    flat, out, o = packed.reshape(-1), {}, 0
    for n, shp in SMALL_REPL.items():
        size = int(np.prod(shp))
        out[n] = flat[o:o + size].reshape(shp)
        o += size
    out["loss"] = flat[o]
    return out


def _rope_tables(positions):
    half = MLA_ROPE // 2
    inv = ROPE_THETA ** (-jnp.arange(half, dtype=F32) / half)
    ang = positions.astype(F32)[:, None] * inv[None, :]
    cos, sin = jnp.cos(ang), jnp.sin(ang)
    S = positions.shape[0]
    z = lambda w: jnp.zeros((S, w), F32)
    keep = jnp.concatenate([jnp.ones((S, MLA_NOPE), F32), cos, cos, z(HEAD_PAD - MLA_QK)], axis=1)
    from_hi = jnp.concatenate([z(MLA_NOPE), -sin, z(HEAD_PAD - MLA_NOPE - half)], axis=1)
    from_lo = jnp.concatenate([z(MLA_NOPE + half), sin, z(HEAD_PAD - MLA_QK)], axis=1)
    return keep, from_hi, from_lo


def _norm_rope(x, g, keep, from_hi, from_lo):
    y = x * lax.rsqrt(jnp.sum(x * x, axis=-1, keepdims=True) * (1.0 / MLA_QK) + EPS) * g
    half = MLA_ROPE // 2
    return y * keep + _lane_roll(y, HEAD_PAD - half) * from_hi + _lane_roll(y, half) * from_lo


def local_step(x, mem, positions, target, w, wb, small_weights=None, ffn2_weights=None, on_grads=None):
    if small_weights is None:
        small_weights = lambda after: {}
    if ffn2_weights is None:
        ffn2_weights = lambda after: [wb[k] for k in FFN_NAMES[3:]]
    if on_grads is None:
        on_grads = lambda tag, g: 0.0
    S = x.shape[0]
    tm = min(FUSED_TILE, S)
    g1 = (S // tm,)
    tile = lambda arr, **kw: In(arr, (tm, arr.shape[1]), lambda i: (i, 0), rows=True, **kw)
    par = lambda arr, **kw: In(arr, arr.shape, lambda *_: (0, 0), diff=True, acc=(0,), **kw)
    wt = lambda arr: In(arr, weight=True)
    otile = lambda cols, dt: Out((S, cols), dt, (tm, cols), lambda i: (i, 0))
    grads = {}

    x1, h1, g_1, u_1 = ffn_fwd("ffn1_fwd", x, w["ffn1_norm"], wb["ffn1_w_gate"], wb["ffn1_w_up"], wb["ffn1_w_down"])
    wb = {**wb, **small_weights(x1)}

    keep, from_hi, from_lo = _rope_tables(positions)
    scale = MLA_QK ** -0.5

    def f_pre(xv, kp, fh, fl, g_mix, w_q, w_kv, w_kr, w_u, g_q, g_kv, w_uq, w_ukv, gq, gk):
        h2 = _rms(xv, g_mix)
        cq, ckv = _rms(wdot(h2, w_q), g_q), _rms(wdot(h2, w_kv), g_kv)
        kr = wdot(h2, w_kr)
        qs, ks, vs = [], [], []
        for h in range(MLA_HEADS):
            qs.append(_norm_rope(wdot(cq, w_uq, h), gq, kp, fh, fl) * scale)
            kv = wdot(ckv, w_ukv, h)
            ks.append(_norm_rope(jnp.concatenate([kv[:, :MLA_NOPE], kr], axis=-1), gk, kp, fh, fl))
            vs.append(kv[:, MLA_NOPE:])
        return jnp.concatenate(qs, axis=-1), jnp.concatenate(ks, axis=-1), jnp.concatenate(vs, axis=-1), wdot(h2, w_u)

    def pre_ins(gain_mix):
        return [tile(x1, diff=True), tile(keep), tile(from_hi), tile(from_lo), par(gain_mix),
                wt(wb["w_in_q"]), wt(wb["w_in_kv"]), wt(wb["w_in_kr"]), wt(wb["w_in_u"]),
                par(w["mla_q_norm"]), par(w["mla_kv_norm"]), wt(wb["w_uq"]), wt(wb["w_ukv"]),
                par(w["qk_gain_q"]), par(w["qk_gain_k"])]

    pre_outs = [otile(MLA_HEADS * HEAD_PAD, BF16), otile(MLA_HEADS * HEAD_PAD, BF16), otile(MLA_HEADS * MLA_V, BF16),
                otile(SSM_WIDTH, F32)]
    qh, kh, v0, pu = seg_fwd("pre_mixer_fwd", f_pre, g1, pre_ins(w["mix_norm"]), pre_outs)

    o_mla, lse = attn_fwd("mla_attn_fwd", qh, kh, v0, t=min(ATTN_TILE, S))

    prep_grid = (SSM_BLOCKS,)
    prep_ins = ([_whole(w[k], diff=True, acc=(0,)) for k in ("ssm_a_re", "ssm_a_im", "ssm_log_dt")]
                + [In(w[k], (BLOCK_CH, SSM_STATE), lambda i: (i, 0), diff=True)
                   for k in ("ssm_bt_re", "ssm_bt_im", "ssm_c2_re", "ssm_c2_im")])
    blk_out = Out((SSM_BLOCKS, BLOCK_CH, PREP_LANES), BF16, (None, BLOCK_CH, PREP_LANES), lambda i: (i, 0, 0))
    prep_outs = [Out((SUBLANES, SSM_LANES), F32, (SUBLANES, PREP_LANES), lambda i: (0, i))] * 2 + [blk_out] * 4
    a_r8, a_i8, bb_r, bb_i, cb_r, ncb_i = seg_fwd("ssm_prep_fwd", _ssm_prep_f, prep_grid, prep_ins, prep_outs)

    xs_r, xs_i, y_lin = ssm_fwd("ssm_fwd", pu, bb_r, bb_i, cb_r, ncb_i, a_r8, a_i8)

    M = mem.shape[0]
    f_norm = lambda xv, g: (_rms(xv, g),)
    mem_ins = [In(mem, (M, D_MODEL), lambda i: (0, 0)), par(w["mem_norm"])]
    mem_outs = [Out((M, D_MODEL), BF16, (M, D_MODEL), lambda i: (0, 0))]
    (mn,) = seg_fwd("mem_norm_fwd", f_norm, (1,), mem_ins, mem_outs)
    (kvm,) = mm("xattn_kv_fwd", [mn], [[wb["xattn_w_kv"]]], [F32])

    def f_knorm(kk, gk):
        return (jnp.concatenate([_rms(kk[:, h * XHD:(h + 1) * XHD], gk) for h in range(XH)], axis=-1),)

    keys = ((M, XH * XHD), (M, XH * XHD), lambda i: (0, 0))
    knorm_ins = [In(kvm, keys[1], keys[2], diff=True, grad=keys), par(w["xattn_k_norm"])]
    knorm_outs = [Out(keys[0], F32, keys[1], keys[2])]
    (kn_mem,) = seg_fwd("mem_key_norm_fwd", f_knorm, (1,), knorm_ins, knorm_outs)

    xscale = XHD ** -0.5

    def f_post(o, yl, u, xv, kn, vm, d, w_glu, b, gm, gs, w_o1, w_o2, gx, w_xq, gq, w_xo):
        gl = jax.nn.gelu(yl + d * u)
        so = gl * jax.nn.sigmoid(wdot(gl, w_glu) + b)
        x2 = xv + wdot(_rms(o, gm), w_o1) + wdot(_rms(so, gs), w_o2)
        h3 = _rms(x2, gx)
        heads = []
        for h in range(XH):
            qn = _rms(wdot(h3, w_xq, h), gq)
            p = jax.nn.softmax(_bdot_nt(qn, kn[:, h * XHD:(h + 1) * XHD]) * xscale, axis=-1)
            heads.append(_bdot_nn(p, vm[:, h * XHD:(h + 1) * XHD]))
        return (x2 + wdot(jnp.concatenate(heads, axis=-1), w_xo),)

    def post_ins(gain_d):
        half = (M, XH * XHD)
        return [tile(o_mla, diff=True), tile(y_lin, diff=True), tile(pu, diff=True), tile(x1, diff=True),
                In(kn_mem, half, lambda i: (0, 0), diff=True, acc=(0,)),
                In(kvm, half, lambda i: (0, 1), diff=True, acc=(0,), grad=(half, half, lambda i: (0, 0))),
                par(gain_d), wt(wb["ssm_w_glu"]), par(w["ssm_b_glu"]), par(w["out_norm_mla"]), par(w["out_norm_ssm"]),
                wt(wb["w_o_mla"]), wt(wb["w_o_ssm"]), par(w["xattn_norm"]), wt(wb["xattn_w_q"]), par(w["xattn_q_norm"]),
                wt(wb["xattn_w_o"])]

    post_outs = [otile(D_MODEL, F32)]
    (x3,) = seg_fwd("post_mixer_fwd", f_post, g1, post_ins(w["ssm_d"]), post_outs)

    wg2, wu2, wd2 = ffn2_weights(x3)
    dx4, h4, g_2, u_2, parts = ffn_fwd("ffn2_fwd_loss", x3, w["ffn2_norm"], wg2, wu2, wd2, target=target)
    loss = jnp.sum(parts[::SUBLANES, 0])

    dx3, grads["ffn2_norm"], dg_2, du_2 = ffn_bwd_act("ffn2_bwd_act", dx4, x3, w["ffn2_norm"], g_2, u_2, wg2, wu2, wd2)
    grads["ffn2_w_gate"], grads["ffn2_w_up"], grads["ffn2_w_down"] = ffn_bwd_w("ffn2_bwd_w", h4, dx4, g_2, u_2, dg_2, du_2)
    sent = on_grads("ffn2", grads)

    (do_mla, dy_lin, du_a, dx1_a, dkn, dvm, grads["ssm_d"], grads["ssm_w_glu"], grads["ssm_b_glu"], grads["out_norm_mla"],
     grads["out_norm_ssm"], grads["w_o_mla"], grads["w_o_ssm"], grads["xattn_norm"], grads["xattn_w_q"],
     grads["xattn_q_norm"], grads["xattn_w_o"]) = seg_bwd(
        "post_mixer_bwd", f_post, g1, post_ins(w["ssm_d"] + sent), post_outs, [dx3])

    dkk, grads["xattn_k_norm"] = seg_bwd("mem_key_norm_bwd", f_knorm, (1,), knorm_ins, knorm_outs, [dkn])
    w_kv = wb["xattn_w_kv"]
    (dmn,) = mm("xattn_kv_bwd", [dkk, dvm], [[w_kv[:, :XH * XHD]], [w_kv[:, XH * XHD:]]], [F32], trans=True)
    gk_w, gv_w = mm_tn("xattn_kv_bwd_w", [mn], [dkk, dvm], [(0, [0]), (0, [1])])
    grads["xattn_w_kv"] = jnp.concatenate([gk_w, gv_w], axis=1)
    (grads["mem_norm"],) = seg_bwd("mem_norm_bwd", f_norm, (1,), mem_ins, mem_outs, [dmn])

    du, dbb_r, dbb_i, dcb_r, dncb_i, da_r8, da_i8 = ssm_bwd("ssm_bwd", dy_lin, du_a, pu, xs_r, xs_i, bb_r, bb_i, cb_r, ncb_i,
                                                            a_r8, a_i8)
    prep_g = seg_bwd("ssm_prep_bwd", _ssm_prep_f, prep_grid, prep_ins, prep_outs, [da_r8, da_i8, dbb_r, dbb_i, dcb_r, dncb_i])
    for k, gname in enumerate(("ssm_a_re", "ssm_a_im", "ssm_log_dt", "ssm_bt_re", "ssm_bt_im", "ssm_c2_re", "ssm_c2_im")):
        grads[gname] = prep_g[k]

    dqh, dkh, dv0 = attn_bwd("mla_attn_bwd", qh, kh, v0, do_mla, o_mla, lse, t=min(ATTN_TILE, S))

    (dx1, grads["mix_norm"], grads["w_in_q"], grads["w_in_kv"], grads["w_in_kr"], grads["w_in_u"], grads["mla_q_norm"],
     grads["mla_kv_norm"], grads["w_uq"], grads["w_ukv"], grads["qk_gain_q"], grads["qk_gain_k"]) = seg_bwd(
        "pre_mixer_bwd", f_pre, g1, pre_ins(w["mix_norm"]), pre_outs, [dqh, dkh, dv0, du], adds={0: dx1_a},
        row_block=FUSED_ROW_BLOCK)

    sent = on_grads("small", grads)
    dx, grads["ffn1_norm"], dg_1, du_1 = ffn_bwd_act("ffn1_bwd_act", dx1, x, w["ffn1_norm"] + sent, g_1, u_1,
                                                     wb["ffn1_w_gate"], wb["ffn1_w_up"], wb["ffn1_w_down"])
    grads["ffn1_w_gate"], grads["ffn1_w_up"], grads["ffn1_w_down"] = ffn_bwd_w("ffn1_bwd_w", h1, dx1, g_1, u_1, dg_1, du_1)
    return loss, dx, grads


def _pad_cols(a, n):
    return jnp.pad(a, ((0, 0), (0, n - a.shape[1])))


def _step_weights(shards):
    wb = {}
    w_in = _full_from_shards("w_in", shards["w_in"])
    wb["w_in_q"] = w_in[:, :MLA_Q_RANK]
    wb["w_in_kv"] = w_in[:, MLA_Q_RANK:MLA_Q_RANK + MLA_KV_RANK]
    wb["w_in_kr"] = _pad_cols(w_in[:, MLA_Q_RANK + MLA_KV_RANK:MLA_Q_RANK + MLA_KV_RANK + MLA_ROPE], LANES)
    wb["w_in_u"] = w_in[:, MLA_Q_RANK + MLA_KV_RANK + MLA_ROPE:]
    wb["w_uq"] = jnp.pad(shards["mla_w_uq"], ((0, 0), (0, 0), (0, HEAD_PAD - MLA_QK)))
    wb["w_ukv"] = shards["mla_w_ukv"]
    wb["ssm_w_glu"] = _full_from_shards("ssm_w_glu", shards["ssm_w_glu"])
    w_o = _full_from_shards("w_o", shards["w_o"])
    wb["w_o_mla"], wb["w_o_ssm"] = w_o[:SSM_WIDTH], w_o[SSM_WIDTH:]
    w_xq = _full_from_shards("xattn_w_q", shards["xattn_w_q"])
    wb["xattn_w_q"] = jnp.transpose(w_xq.reshape(D_MODEL, XH, XHD), (1, 0, 2))
    wb["xattn_w_kv"] = _full_from_shards("xattn_w_kv", shards["xattn_w_kv"])
    wb["xattn_w_o"] = _full_from_shards("xattn_w_o", shards["xattn_w_o"])
    return wb


def _sharded_grads(g):
    out = {}
    kr = g["w_in_kr"][:, :MLA_ROPE]
    out["w_in"] = _shards_from_full("w_in", jnp.concatenate([g["w_in_q"], g["w_in_kv"], kr, g["w_in_u"]], axis=1))
    out["mla_w_uq"] = g["w_uq"][:, :, :MLA_QK]
    out["mla_w_ukv"] = g["w_ukv"]
    out["ssm_w_glu"] = _shards_from_full("ssm_w_glu", g["ssm_w_glu"])
    out["w_o"] = _shards_from_full("w_o", jnp.concatenate([g["w_o_mla"], g["w_o_ssm"]], axis=0))
    w_xq = jnp.transpose(g["xattn_w_q"], (1, 0, 2)).reshape(D_MODEL, XH * XHD)
    out["xattn_w_q"] = _shards_from_full("xattn_w_q", w_xq)
    out["xattn_w_kv"] = _shards_from_full("xattn_w_kv", g["xattn_w_kv"])
    out["xattn_w_o"] = _shards_from_full("xattn_w_o", g["xattn_w_o"])
    return out


def _problem_repl_grads(g):
    out = {}
    out["mla_qk_norm_q"] = g["qk_gain_q"][:, :MLA_QK]
    out["mla_qk_norm_k"] = g["qk_gain_k"][:, :MLA_QK]
    out["ssm_b_re"] = jnp.transpose(g["ssm_bt_re"].reshape(SSM_GROUPS, SSM_GROUP, SSM_STATE), (0, 2, 1))
    out["ssm_b_im"] = jnp.transpose(g["ssm_bt_im"].reshape(SSM_GROUPS, SSM_GROUP, SSM_STATE), (0, 2, 1))
    out["ssm_c_re"] = g["ssm_c2_re"].reshape(SSM_GROUPS, SSM_GROUP, SSM_STATE)
    out["ssm_c_im"] = g["ssm_c2_im"].reshape(SSM_GROUPS, SSM_GROUP, SSM_STATE)
    for k in SMALL_REPL:
        if k not in out:
            out[k] = g[k]
    return out


def _step_params(p):
    row = lambda a: a.reshape(1, -1)
    w = {k: row(p[k]) for k in ("ffn1_norm", "mix_norm", "mla_q_norm", "mla_kv_norm", "ssm_b_glu", "out_norm_mla",
                                "out_norm_ssm", "xattn_norm", "mem_norm", "xattn_q_norm", "xattn_k_norm", "ffn2_norm")}
    w["qk_gain_q"] = _pad_cols(row(p["mla_qk_norm_q"]), HEAD_PAD)
    w["qk_gain_k"] = _pad_cols(row(p["mla_qk_norm_k"]), HEAD_PAD)
    w["ssm_a_re"], w["ssm_a_im"] = p["ssm_a_re"], p["ssm_a_im"]
    w["ssm_log_dt"] = p["ssm_log_dt"].reshape(SSM_GROUPS, 1)
    w["ssm_bt_re"] = jnp.transpose(p["ssm_b_re"], (0, 2, 1)).reshape(SSM_WIDTH, SSM_STATE)
    w["ssm_bt_im"] = jnp.transpose(p["ssm_b_im"], (0, 2, 1)).reshape(SSM_WIDTH, SSM_STATE)
    w["ssm_c2_re"] = p["ssm_c_re"].reshape(SSM_WIDTH, SSM_STATE)
    w["ssm_c2_im"] = p["ssm_c_im"].reshape(SSM_WIDTH, SSM_STATE)
    w["ssm_d"] = p["ssm_d"].reshape(1, SSM_WIDTH)
    return w


ARG_NAMES = ['x', 'mem', 'positions', 'ffn1_norm', 'ffn1_w_gate', 'ffn1_w_up', 'ffn1_w_down', 'mix_norm', 'w_in', 'mla_q_norm', 'mla_w_uq', 'mla_kv_norm', 'mla_w_ukv', 'mla_qk_norm_q', 'mla_qk_norm_k', 'ssm_a_re', 'ssm_a_im', 'ssm_log_dt', 'ssm_b_re', 'ssm_b_im', 'ssm_c_re', 'ssm_c_im', 'ssm_d', 'ssm_w_glu', 'ssm_b_glu', 'out_norm_mla', 'out_norm_ssm', 'w_o', 'xattn_norm', 'mem_norm', 'xattn_w_q', 'xattn_w_kv', 'xattn_q_norm', 'xattn_k_norm', 'xattn_w_o', 'ffn2_norm', 'ffn2_w_gate', 'ffn2_w_up', 'ffn2_w_down']
WEIGHT_NAMES = ARG_NAMES[3:]


def _gather_weights(p, c):
    half = lambda a: lax.dynamic_slice_in_dim(a, c * (a.shape[0] // 2), a.shape[0] // 2, axis=0)
    ffn1 = [half(p[k].astype(BF16)) for k in FFN_NAMES[:3]]
    small = [half(_pack_shards({k: p[k].astype(BF16) for k in SMALL_SHARDED}))]
    ffn2 = [half(p[k].astype(BF16)) for k in FFN_NAMES[3:]]
    me = 4 * lax.axis_index("x") + 2 * lax.axis_index("y") + c
    own = lambda got, blocks: [lax.dynamic_update_index_in_dim(g, b, me, 0) for g, b in zip(got, blocks)]
    as_shards = lambda a: a.reshape(N_CHIPS, 2 * a.shape[1], a.shape[2])
    landing = lambda blocks: [jax.ShapeDtypeStruct((N_DEV,) + b.shape, b.dtype) for b in blocks]
    got1 = own(all_gather_halves("all_gather_weights_a", ffn1), ffn1)
    got1, small = lax.optimization_barrier((got1, small))
    flight_s, sent_s = split_copy_start("gather_small_start", _gather_copies, N_DEV - 1, small, landing(small))
    sent_s, ffn2 = lax.optimization_barrier((sent_s, ffn2))
    flight_2, sent_2 = split_copy_start("gather_ffn2_start", _gather_copies, N_DEV - 1, ffn2, landing(ffn2))
    wb = {k: as_shards(a) for k, a in zip(FFN_NAMES[:3], got1)}

    def small_weights(after):
        mine, got = split_copy_wait("gather_small_wait", flight_s, after)
        return _step_weights(_unpack_shards(own(got, mine)[0].reshape(N_CHIPS, -1, LANES)))

    def ffn2_weights(after):
        mine, got = split_copy_wait("gather_ffn2_wait", flight_2, after)
        return [as_shards(a) for a in own(got, mine)]

    return wb, small_weights, ffn2_weights, sent_s + sent_2


class _GradReduce:
    def __init__(self, c):
        self.c, self.c_arr = c, jnp.reshape(c, (1,)).astype(jnp.int32)
        self.chip = 2 * lax.axis_index("x") + lax.axis_index("y")
        self.flights = []

    def start(self, tag, arrs):
        split = [a.reshape(N_CHIPS, 2, a.shape[1] // 2, a.shape[2]) for a in arrs]
        return self._scatter(tag, split, pair_send_halves(f"grad_pair_send_{tag}", split))

    def send(self, tag, arrs):
        split = [a.reshape(N_CHIPS, 2, a.shape[1] // 2, a.shape[2]) for a in arrs]
        lands = [jax.ShapeDtypeStruct((s.shape[0],) + s.shape[2:], s.dtype) for s in split]
        flight, sent = split_copy_start(f"grad_pair_send_start_{tag}", _pair_copies, 1, split, lands)
        self.sending = (tag, flight)
        return sent

    def scatter(self, tag, after):
        sent_tag, flight = self.sending
        assert sent_tag == tag
        return self._scatter(tag, *split_copy_wait(f"grad_pair_send_wait_{tag}", flight, after))

    def _scatter(self, tag, split, got):
        parts = [pair_add(f"grad_pair_add_{tag}_{k}", s, g, self.c_arr) for k, (s, g) in enumerate(zip(split, got))]
        flight, sent = split_copy_start(f"grad_scatter_start_{tag}", _scatter_copies, N_CHIPS - 1, parts, parts)
        self.flights.append((tag, flight))
        return sent

    def finish(self, after):
        halves = []
        for tag, flight in self.flights:
            parts, landed = split_copy_wait(f"grad_scatter_wait_{tag}", flight, after)
            for k, (q, p) in enumerate(zip(landed, parts)):
                mine = lax.dynamic_index_in_dim(p, self.chip, 0, keepdims=False)
                halves.append(sum_chips(f"grad_sum_{tag}_{k}", lax.dynamic_update_index_in_dim(q, mine, self.chip, 0)))
        tags = "_".join(t for t, _ in self.flights)
        self.flights = []
        theirs = pair_exchange(f"grad_pair_exchange_{tags}", halves)
        return [jnp.where(self.c == 0, jnp.concatenate([h, t], axis=0), jnp.concatenate([t, h], axis=0))
                for h, t in zip(halves, theirs)]


def kernel(x, mem, positions, ffn1_norm, ffn1_w_gate, ffn1_w_up, ffn1_w_down, mix_norm, w_in, mla_q_norm, mla_w_uq, mla_kv_norm, mla_w_ukv, mla_qk_norm_q, mla_qk_norm_k, ssm_a_re, ssm_a_im, ssm_log_dt, ssm_b_re, ssm_b_im, ssm_c_re, ssm_c_im, ssm_d, ssm_w_glu, ssm_b_glu, out_norm_mla, out_norm_ssm, w_o, xattn_norm, mem_norm, xattn_w_q, xattn_w_kv, xattn_q_norm, xattn_k_norm, xattn_w_o, ffn2_norm, ffn2_w_gate, ffn2_w_up, ffn2_w_down, loss_target, m_ffn1_norm, m_ffn1_w_gate, m_ffn1_w_up, m_ffn1_w_down, m_mix_norm, m_w_in, m_mla_q_norm, m_mla_w_uq, m_mla_kv_norm, m_mla_w_ukv, m_mla_qk_norm_q, m_mla_qk_norm_k, m_ssm_a_re, m_ssm_a_im, m_ssm_log_dt, m_ssm_b_re, m_ssm_b_im, m_ssm_c_re, m_ssm_c_im, m_ssm_d, m_ssm_w_glu, m_ssm_b_glu, m_out_norm_mla, m_out_norm_ssm, m_w_o, m_xattn_norm, m_mem_norm, m_xattn_w_q, m_xattn_w_kv, m_xattn_q_norm, m_xattn_k_norm, m_xattn_w_o, m_ffn2_norm, m_ffn2_w_gate, m_ffn2_w_up, m_ffn2_w_down, v_ffn1_norm, v_ffn1_w_gate, v_ffn1_w_up, v_ffn1_w_down, v_mix_norm, v_w_in, v_mla_q_norm, v_mla_w_uq, v_mla_kv_norm, v_mla_w_ukv, v_mla_qk_norm_q, v_mla_qk_norm_k, v_ssm_a_re, v_ssm_a_im, v_ssm_log_dt, v_ssm_b_re, v_ssm_b_im, v_ssm_c_re, v_ssm_c_im, v_ssm_d, v_ssm_w_glu, v_ssm_b_glu, v_out_norm_mla, v_out_norm_ssm, v_w_o, v_xattn_norm, v_mem_norm, v_xattn_w_q, v_xattn_w_kv, v_xattn_q_norm, v_xattn_k_norm, v_xattn_w_o, v_ffn2_norm, v_ffn2_w_gate, v_ffn2_w_up, v_ffn2_w_down):
    args = dict(locals())
    c = lax.axis_index("c")
    view = lambda k, a: jnp.swapaxes(a, 0, 1) if k in TRANSPOSED_VIEW else a
    p = {k: view(k, args[k][0]) for k in WEIGHT_NAMES}
    mom = {k: view(k, args["m_" + k][0]) for k in WEIGHT_NAMES}
    var = {k: view(k, args["v_" + k][0]) for k in WEIGHT_NAMES}
    natural = {k: view(k, p[k]) for k in WEIGHT_NAMES}

    wb, small_weights, ffn2_weights, sent = _gather_weights(
        {k: (p[k] if k in FFN_NAMES else natural[k]) for k in WEIGHT_NAMES}, c)
    w = _step_params(natural)
    w["ffn1_norm"] = w["ffn1_norm"] + sent
    early, late = _GradReduce(c), _GradReduce(c)

    def on_grads(tag, g):
        if tag == "ffn2":
            return early.send(tag, [g[k] for k in FFN_NAMES[3:]])
        packed = _pack_shards(_sharded_grads(g))
        return early.scatter("ffn2", packed[0, :SUBLANES]) + early.start(tag, [packed])

    loss, dx, g = local_step(x[0], mem[0], positions[0], loss_target[0], w, wb, small_weights, ffn2_weights, on_grads)

    sent = late.send("ffn1", [g[k] for k in FFN_NAMES[:3]])
    shards = early.finish(dx[:SUBLANES, :LANES] + sent)
    grad = dict(zip(FFN_NAMES[3:], shards[:3]))
    small_sharded = _unpack_shards(shards[3])
    grad.update({k: view(k, small_sharded[k]) for k in SMALL_SHARDED})
    reduced = all_reduce_small("grad_all_reduce_small", _pack_repl(_problem_repl_grads(g), loss + sent))
    grad.update(_unpack_repl(reduced))
    loss = grad.pop("loss")
    started = jnp.reshape(late.scatter("ffn1", reduced[:SUBLANES]), (1, 1))

    delta, new_m, new_v = {}, {}, {}
    small = [k for k in WEIGHT_NAMES if k not in FFN_NAMES and k not in SMALL_SHARDED]
    as2d = lambda a: a.reshape(-1, a.shape[-1])

    def update(k, after):
        delta[k], new_m[k], new_v[k] = adamw_big("adamw_" + k, as2d(p[k]), as2d(grad[k]), as2d(mom[k]), as2d(var[k]), after)

    last = started
    for k in WEIGHT_NAMES:
        if k not in small and k not in FFN_NAMES[:3]:
            update(k, last)
            last = delta[k]
    ds, nms, nvs = adamw_small("adamw_small", [as2d(p[k]) for k in small], [as2d(grad[k].reshape(p[k].shape)) for k in small],
                               [as2d(mom[k]) for k in small], [as2d(var[k]) for k in small], last)
    for k, d, nm, nv in zip(small, ds, nms, nvs):
        delta[k], new_m[k], new_v[k] = d, nm, nv
    last = ds[0]

    grad.update(zip(FFN_NAMES[:3], late.finish(last)))
    for k in FFN_NAMES[:3]:
        update(k, last)

    shaped = lambda d, k: view(k, d.reshape(p[k].shape)).reshape(args[k].shape)
    return (loss, dx[None], *[shaped(grad[k], k) for k in WEIGHT_NAMES], *[shaped(delta[k], k) for k in WEIGHT_NAMES],
            *[shaped(new_m[k], k) for k in WEIGHT_NAMES], *[shaped(new_v[k], k) for k in WEIGHT_NAMES])
```

```python
import functools
import math

import jax
import jax.numpy as jnp
import numpy as np
from jax import lax
from jax.experimental import pallas as pl
from jax.experimental.pallas import tpu as pltpu

F32, BF16 = jnp.float32, jnp.bfloat16
EPS = 1e-6
MESH = pl.DeviceIdType.MESH

D_MODEL, D_FF = 1024, 2752
MLA_HEADS, MLA_Q_RANK, MLA_KV_RANK, MLA_NOPE, MLA_ROPE, MLA_V = 4, 384, 256, 128, 64, 128
MLA_QK = MLA_NOPE + MLA_ROPE
HEAD_PAD = 256
SSM_WIDTH, SSM_GROUP, SSM_GROUPS, SSM_STATE = 512, 16, 32, 64
SSM_LANES = SSM_GROUPS * SSM_STATE
XH, XHD = 4, 128
ROPE_THETA = 10000.0
ADAM_LR, ADAM_B1, ADAM_B2, ADAM_EPS, ADAM_WD, ADAM_STEP = 0.001, 0.9, 0.999, 1e-08, 0.01, 10
N_CHIPS, N_DEV = 4, 8

VMEM_LIMIT_BYTES = 56 * 2**20
TOKEN_TILE = 512
FUSED_TILE = 256
FUSED_ROW_BLOCK = 128
FFN_SHARDS_PER_STEP = 2
FFN_ROW_BLOCK = 256
FFN_WIDE_TILE = 1024
ATTN_TILE = 1024
STAT_LANES = 128
SCAN_TIME_TILE = 2048
SCAN_LANE_TILE = 512
SUBLANES = 8


def _params(n_axes):
    return pltpu.CompilerParams(dimension_semantics=("arbitrary",) * n_axes, vmem_limit_bytes=VMEM_LIMIT_BYTES)


def _first(axes):
    cond = None
    for a in axes:
        c = pl.program_id(a) == 0
        cond = c if cond is None else jnp.logical_and(cond, c)
    return cond


def mm(name, xs, ws, out_dtypes, *, trans=False, adds=None, tm=TOKEN_TILE):
    rows = xs[0].shape[0]
    tm = min(tm, rows)
    n_in, n_out = len(xs), len(out_dtypes)
    pairs = [(i, j) for i in range(n_in) for j in range(n_out) if ws[i][j] is not None]
    w_list = [ws[i][j] for (i, j) in pairs]
    adds = list(adds) if adds is not None else [None] * n_out
    add_list = [a for a in adds if a is not None]
    out_cols = [None] * n_out
    for (i, j), w in zip(pairs, w_list):
        out_cols[j] = w.shape[0] if trans else w.shape[1]
    contract = (((1,), (1 if trans else 0,)), ((), ()))

    def body(*refs):
        x_refs = refs[:n_in]
        w_refs = refs[n_in:n_in + len(pairs)]
        a_refs = list(refs[n_in + len(pairs):n_in + len(pairs) + len(add_list)])
        o_refs = refs[n_in + len(pairs) + len(add_list):]
        xb = [None] * n_in
        for j in range(n_out):
            acc = None
            for p, (i, jj) in enumerate(pairs):
                if jj != j:
                    continue
                if xb[i] is None:
                    xb[i] = x_refs[i][...].astype(BF16)
                d = lax.dot_general(xb[i], w_refs[p][...].astype(BF16), contract, preferred_element_type=F32)
                acc = d if acc is None else acc + d
            if adds[j] is not None:
                acc = acc + a_refs.pop(0)[...].astype(F32)
            o_refs[j][...] = acc.astype(o_refs[j].dtype)

    in_specs = ([pl.BlockSpec((tm, x.shape[1]), lambda i: (i, 0)) for x in xs]
                + [pl.BlockSpec(w.shape, lambda i: (0, 0)) for w in w_list]
                + [pl.BlockSpec((tm, a.shape[1]), lambda i: (i, 0)) for a in add_list])
    outs = pl.pallas_call(
        body, name=name, grid=(rows // tm,), in_specs=in_specs,
        out_specs=[pl.BlockSpec((tm, n), lambda i: (i, 0)) for n in out_cols],
        out_shape=[jax.ShapeDtypeStruct((rows, n), dt) for n, dt in zip(out_cols, out_dtypes)],
        compiler_params=_params(1),
    )(*xs, *w_list, *add_list)
    return list(outs)


def mm_tn(name, xs, dys, pairs, *, tm=TOKEN_TILE):
    rows = xs[0].shape[0]
    tm = min(tm, rows)
    n_x, n_dy = len(xs), len(dys)
    contract = (((0,), (0,)), ((), ()))

    def body(*refs):
        x_refs, dy_refs, o_refs = refs[:n_x], refs[n_x:n_x + n_dy], refs[n_x + n_dy:]
        @pl.when(pl.program_id(0) == 0)
        def _():
            for o in o_refs:
                o[...] = jnp.zeros_like(o)

        for k, (i, js) in enumerate(pairs):
            dy = None
            for j in js:
                t = dy_refs[j][...].astype(F32)
                dy = t if dy is None else dy + t
            o_refs[k][...] += lax.dot_general(x_refs[i][...].astype(BF16), dy.astype(BF16), contract,
                                              preferred_element_type=F32)

    shapes = [(xs[i].shape[1], dys[js[0]].shape[1]) for (i, js) in pairs]
    outs = pl.pallas_call(
        body, name=name, grid=(rows // tm,),
        in_specs=[pl.BlockSpec((tm, a.shape[1]), lambda i: (i, 0)) for a in (*xs, *dys)],
        out_specs=[pl.BlockSpec(s, lambda i: (0, 0)) for s in shapes],
        out_shape=[jax.ShapeDtypeStruct(s, F32) for s in shapes],
        compiler_params=_params(1),
    )(*xs, *dys)
    return list(outs)


class In:
    def __init__(self, arr, block=None, imap=None, *, diff=False, acc=None, grad=None, weight=False, rows=False):
        self.arr, self.block, self.imap, self.diff, self.acc, self.grad = arr, block, imap, diff, acc, grad
        self.weight, self.rows = weight, rows

    def spec(self):
        return pl.BlockSpec(memory_space=pltpu.VMEM) if self.weight else pl.BlockSpec(self.block, self.imap)


class Out:
    def __init__(self, shape, dtype, block, imap):
        self.shape, self.dtype, self.block, self.imap = shape, dtype, block, imap

    def spec(self):
        return pl.BlockSpec(self.block, self.imap)


class Wt:
    def __init__(self, ref, zeros=None):
        self.ref, self.zeros = ref, zeros


@jax.custom_vjp
def _wdot(a, w, z):
    return jnp.dot(a.astype(BF16), w, preferred_element_type=F32)


def _wdot_fwd(a, w, z):
    return _wdot(a, w, z), (a, w)


def _wdot_bwd(res, g):
    a, w = res
    gb = g.astype(BF16)
    da = lax.dot_general(gb, w, (((1,), (1,)), ((), ())), preferred_element_type=F32)
    dz = lax.dot_general(a.astype(BF16), gb, (((0,), (0,)), ((), ())), preferred_element_type=F32)
    return da, None, dz


_wdot.defvjp(_wdot_fwd, _wdot_bwd)


def wdot(a, wt, head=None):
    w = wt.ref[...] if head is None else wt.ref[head]
    if wt.zeros is None:
        return jnp.dot(a.astype(BF16), w, preferred_element_type=F32)
    return _wdot(a, w, wt.zeros[0 if head is None else head])


def seg_fwd(name, f, grid, ins, outs):
    n_in = len(ins)

    def body(*refs):
        res = f(*[Wt(r) if i.weight else r[...] for i, r in zip(ins, refs[:n_in])])
        for o_ref, r in zip(refs[n_in:], res):
            o_ref[...] = r.astype(o_ref.dtype)

    res = pl.pallas_call(
        body, name=name, grid=grid, in_specs=[i.spec() for i in ins], out_specs=[o.spec() for o in outs],
        out_shape=[jax.ShapeDtypeStruct(o.shape, o.dtype) for o in outs], compiler_params=_params(len(grid)),
    )(*[i.arr for i in ins])
    return list(res)


def seg_bwd(name, f, grid, ins, outs, cts, adds=None, row_block=None):
    n_in, n_ct = len(ins), len(cts)
    grad_idx = [k for k, i in enumerate(ins) if i.diff or i.weight]
    adds = adds or {}
    add_keys = sorted(adds)
    add_list = [adds[k] for k in add_keys]
    heads = {k: (ins[k].arr.shape[0] if ins[k].arr.ndim == 3 else 1) for k in grad_idx if ins[k].weight}
    tile_rows = outs[0].block[0]
    blocks = [None] if row_block is None else [pl.ds(r0, row_block) for r0 in range(0, tile_rows, row_block)]

    def body(*refs):
        in_refs, ct_refs = refs[:n_in], refs[n_in:n_in + n_ct]
        add_refs = dict(zip(add_keys, refs[n_in + n_ct:n_in + n_ct + len(add_list)]))
        g_refs = dict(zip(grad_idx, refs[n_in + n_ct + len(add_list):]))
        for k in grad_idx:
            if ins[k].weight or ins[k].acc is not None:
                @pl.when(_first(range(len(grid)) if ins[k].weight else ins[k].acc))
                def _(k=k):
                    g_refs[k][...] = jnp.zeros_like(g_refs[k])

        for rows in blocks:
            at = lambda ref, sliced: ref[...] if rows is None or not sliced else ref[rows, :]
            vals = [None if i.weight else at(r, i.rows) for i, r in zip(ins, in_refs)]
            primals, owner = [], []
            for k in grad_idx:
                if ins[k].weight:
                    for h in range(heads[k]):
                        primals.append(jnp.zeros(ins[k].arr.shape[-2:], F32))
                        owner.append((k, h))
                else:
                    primals.append(vals[k].astype(F32))
                    owner.append((k, None))

            def g(*dv, vals=vals, owner=owner):
                full = list(vals)
                zeros = {}
                for (k, h), v in zip(owner, dv):
                    if h is None:
                        full[k] = v
                    else:
                        zeros.setdefault(k, []).append(v)
                for k, z in zeros.items():
                    full[k] = Wt(in_refs[k], z)
                return tuple(f(*full))

            _, pull = jax.vjp(g, *primals)
            grads = pull(tuple(at(c, True).astype(F32) for c in ct_refs))
            for (k, h), gr in zip(owner, grads):
                if ins[k].weight:
                    if ins[k].arr.ndim == 3:
                        g_refs[k][h] += gr
                    else:
                        g_refs[k][...] += gr
                    continue
                if k in add_refs:
                    gr = gr + at(add_refs[k], True).astype(F32)
                if ins[k].acc is not None:
                    g_refs[k][...] += gr
                elif rows is None or not ins[k].rows:
                    g_refs[k][...] = gr.astype(g_refs[k].dtype)
                else:
                    g_refs[k][rows, :] = gr.astype(g_refs[k].dtype)

    g_specs, g_shapes = [], []
    for k in grad_idx:
        i = ins[k]
        if i.weight:
            g_specs.append(pl.BlockSpec(memory_space=pltpu.VMEM))
            g_shapes.append(jax.ShapeDtypeStruct(i.arr.shape, F32))
            continue
        shape, block, imap = i.grad if i.grad is not None else (i.arr.shape, i.block, i.imap)
        g_specs.append(pl.BlockSpec(block, imap))
        g_shapes.append(jax.ShapeDtypeStruct(shape, F32))
    in_specs = ([i.spec() for i in ins] + [o.spec() for o in outs]
                + [pl.BlockSpec(ins[k].block, ins[k].imap) for k in add_keys])
    res = pl.pallas_call(
        body, name=name, grid=grid, in_specs=in_specs, out_specs=g_specs, out_shape=g_shapes,
        compiler_params=_params(len(grid)),
    )(*[i.arr for i in ins], *cts, *add_list)
    return list(res)


def _rms(x, g):
    return x * lax.rsqrt(jnp.mean(x * x, axis=-1, keepdims=True) + EPS) * g


@jax.custom_vjp
def _bdot_nt(a, b):
    return lax.dot_general(a.astype(BF16), b.astype(BF16), (((1,), (1,)), ((), ())), preferred_element_type=F32)


def _bdot_nt_fwd(a, b):
    return _bdot_nt(a, b), (a, b)


def _bdot_nt_bwd(res, g):
    a, b = res
    gb = g.astype(BF16)
    da = lax.dot_general(gb, b.astype(BF16), (((1,), (0,)), ((), ())), preferred_element_type=F32)
    db = lax.dot_general(gb, a.astype(BF16), (((0,), (0,)), ((), ())), preferred_element_type=F32)
    return da, db


_bdot_nt.defvjp(_bdot_nt_fwd, _bdot_nt_bwd)


@jax.custom_vjp
def _bdot_nn(a, b):
    return lax.dot_general(a.astype(BF16), b.astype(BF16), (((1,), (0,)), ((), ())), preferred_element_type=F32)


def _bdot_nn_fwd(a, b):
    return _bdot_nn(a, b), (a, b)


def _bdot_nn_bwd(res, g):
    a, b = res
    gb = g.astype(BF16)
    da = lax.dot_general(gb, b.astype(BF16), (((1,), (1,)), ((), ())), preferred_element_type=F32)
    db = lax.dot_general(a.astype(BF16), gb, (((0,), (0,)), ((), ())), preferred_element_type=F32)
    return da, db


_bdot_nn.defvjp(_bdot_nn_fwd, _bdot_nn_bwd)


@functools.partial(jax.custom_vjp, nondiff_argnums=(1,))
def _lane_roll(x, shift):
    return pltpu.roll(x, shift, 1)


def _lane_roll_fwd(x, shift):
    return pltpu.roll(x, shift, 1), None


def _lane_roll_bwd(shift, _, g):
    return (pltpu.roll(g, (g.shape[1] - shift) % g.shape[1], 1),)


_lane_roll.defvjp(_lane_roll_fwd, _lane_roll_bwd)


def _hp_dot(a, b):
    return jnp.dot(a, b, precision=lax.Precision.HIGHEST, preferred_element_type=F32)


NT_DIMS = (((1,), (1,)), ((), ()))
TN_DIMS = (((0,), (0,)), ((), ()))


def ffn_fwd(name, x, gain, wg, wu, wd, target=None, tm=FFN_WIDE_TILE):
    S, D = x.shape
    tm = min(tm, S)
    nsh, Fs, _ = wg.shape
    with_loss = target is not None

    def body(*refs):
        if with_loss:
            x_ref, gain_ref, wg_ref, wu_ref, wd_ref, t_ref, xo_ref, h_ref, g_ref, u_ref, part_ref, acc = refs
        else:
            x_ref, gain_ref, wg_ref, wu_ref, wd_ref, xo_ref, h_ref, g_ref, u_ref, acc = refs
        j = pl.program_id(1)

        @pl.when(j == 0)
        def _():
            h_ref[...] = _rms(x_ref[...], gain_ref[...]).astype(BF16)
            acc[...] = jnp.zeros_like(acc)

        h = h_ref[...]
        g = lax.dot_general(h, wg_ref[...], NT_DIMS, preferred_element_type=F32)
        u = lax.dot_general(h, wu_ref[...], NT_DIMS, preferred_element_type=F32)
        g_ref[...] = g.astype(BF16)
        u_ref[...] = u.astype(BF16)
        a = g * jax.nn.sigmoid(g) * u
        acc[...] += jnp.dot(a.astype(BF16), wd_ref[...], preferred_element_type=F32)

        @pl.when(j == nsh - 1)
        def _():
            y = x_ref[...] + 0.5 * acc[...]
            if with_loss:
                err = y - t_ref[...]
                xo_ref[...] = err * (1.0 / D)
                part_ref[...] = jnp.full(part_ref.shape, 0.5 * jnp.sum(jnp.mean(err * err, axis=-1)), F32)
            else:
                xo_ref[...] = y

    rows = pl.BlockSpec((tm, D), lambda i, j: (i, 0))
    wspec = pl.BlockSpec((None, Fs, D), lambda i, j: (j, 0, 0))
    act = pl.BlockSpec((None, tm, Fs), lambda i, j: (j, i, 0))
    in_specs, args = [rows, pl.BlockSpec((1, D), lambda i, j: (0, 0)), wspec, wspec, wspec], [x, gain, wg, wu, wd]
    out_specs = [rows, rows, act, act]
    out_shape = [jax.ShapeDtypeStruct((S, D), F32), jax.ShapeDtypeStruct((S, D), BF16),
                 jax.ShapeDtypeStruct((nsh, S, Fs), BF16), jax.ShapeDtypeStruct((nsh, S, Fs), BF16)]
    if with_loss:
        in_specs.append(rows)
        args.append(target)
        out_specs.append(pl.BlockSpec((SUBLANES, 128), lambda i, j: (i, 0)))
        out_shape.append(jax.ShapeDtypeStruct((S // tm * SUBLANES, 128), F32))
    return pl.pallas_call(
        body, name=name, grid=(S // tm, nsh), in_specs=in_specs, out_specs=out_specs, out_shape=out_shape,
        scratch_shapes=[pltpu.VMEM((tm, D), F32)], compiler_params=_params(2),
    )(*args)


def ffn_bwd_act(name, dxo, x, gain, g, u, wg, wu, wd, tm=TOKEN_TILE):
    S, D = x.shape
    tm = min(tm, S)
    nsh, Fs, _ = wg.shape

    def body(dxo_ref, x_ref, gain_ref, g_ref, u_ref, wg_ref, wu_ref, wd_ref, dx_ref, dgain_ref, dg_ref, du_ref, dd, dh):
        i, j = pl.program_id(0), pl.program_id(1)

        @pl.when(j == 0)
        def _():
            dd[...] = (0.5 * dxo_ref[...]).astype(BF16)
            dh[...] = jnp.zeros_like(dh)

        for s in range(FFN_SHARDS_PER_STEP):
            for r0 in range(0, tm, FFN_ROW_BLOCK):
                rows = pl.ds(r0, FFN_ROW_BLOCK)
                da = lax.dot_general(dd[rows, :], wd_ref[s], NT_DIMS, preferred_element_type=F32)
                gf, uf = g_ref[s, rows, :].astype(F32), u_ref[s, rows, :].astype(F32)
                sig = jax.nn.sigmoid(gf)
                dgv = (da * uf * (sig * (1.0 + gf * (1.0 - sig)))).astype(BF16)
                duv = (da * (gf * sig)).astype(BF16)
                dg_ref[s, rows, :] = dgv
                du_ref[s, rows, :] = duv
                dh[rows, :] += (jnp.dot(dgv, wg_ref[s], preferred_element_type=F32)
                                + jnp.dot(duv, wu_ref[s], preferred_element_type=F32))

        @pl.when(j == nsh // FFN_SHARDS_PER_STEP - 1)
        def _():
            xv = x_ref[...]
            r = lax.rsqrt(jnp.mean(xv * xv, axis=-1, keepdims=True) + EPS)
            xhat = xv * r
            dhv = dh[...]
            dxn = dhv * gain_ref[...]
            dx_ref[...] = dxo_ref[...] + r * (dxn - xhat * jnp.mean(dxn * xhat, axis=-1, keepdims=True))
            part = jnp.sum(dhv * xhat, axis=0, keepdims=True)

            @pl.when(i == 0)
            def _():
                dgain_ref[...] = part

            @pl.when(i != 0)
            def _():
                dgain_ref[...] += part

    per = FFN_SHARDS_PER_STEP
    act = pl.BlockSpec((per, tm, Fs), lambda i, j: (j, i, 0))
    wspec = pl.BlockSpec((per, Fs, D), lambda i, j: (j, 0, 0))
    return pl.pallas_call(
        body, name=name, grid=(S // tm, nsh // per),
        in_specs=[pl.BlockSpec((tm, D), lambda i, j: (i, 0)), pl.BlockSpec((tm, D), lambda i, j: (i, 0)),
                  pl.BlockSpec((1, D), lambda i, j: (0, 0)), act, act, wspec, wspec, wspec],
        out_specs=[pl.BlockSpec((tm, D), lambda i, j: (i, 0)), pl.BlockSpec((1, D), lambda i, j: (0, 0)), act, act],
        out_shape=[jax.ShapeDtypeStruct((S, D), F32), jax.ShapeDtypeStruct((1, D), F32),
                   jax.ShapeDtypeStruct((nsh, S, Fs), BF16), jax.ShapeDtypeStruct((nsh, S, Fs), BF16)],
        scratch_shapes=[pltpu.VMEM((tm, D), BF16), pltpu.VMEM((tm, D), F32)], compiler_params=_params(2),
    )(dxo, x, gain, g, u, wg, wu, wd)


def ffn_bwd_w(name, h, dxo, g, u, dg, du, tm=FFN_WIDE_TILE):
    S, D = h.shape
    tm = min(tm, S)
    nsh, _, Fs = g.shape

    def body(h_ref, dxo_ref, g_ref, u_ref, dg_ref, du_ref, dwg_ref, dwu_ref, dwd_ref):
        i = pl.program_id(1)
        gf, uf = g_ref[...].astype(F32), u_ref[...].astype(F32)
        a = (gf * jax.nn.sigmoid(gf) * uf).astype(BF16)
        dd = (0.5 * dxo_ref[...]).astype(BF16)
        hv = h_ref[...]

        @pl.when(i == 0)
        def _():
            dwg_ref[...] = jnp.zeros_like(dwg_ref)
            dwu_ref[...] = jnp.zeros_like(dwu_ref)
            dwd_ref[...] = jnp.zeros_like(dwd_ref)

        dwg_ref[...] += lax.dot_general(dg_ref[...], hv, TN_DIMS, preferred_element_type=F32)
        dwu_ref[...] += lax.dot_general(du_ref[...], hv, TN_DIMS, preferred_element_type=F32)
        dwd_ref[...] += lax.dot_general(a, dd, TN_DIMS, preferred_element_type=F32)

    act = pl.BlockSpec((None, tm, Fs), lambda j, i: (j, i, 0))
    wspec = pl.BlockSpec((None, Fs, D), lambda j, i: (j, 0, 0))
    return pl.pallas_call(
        body, name=name, grid=(nsh, S // tm),
        in_specs=[pl.BlockSpec((tm, D), lambda j, i: (i, 0)), pl.BlockSpec((tm, D), lambda j, i: (i, 0)), act, act, act, act],
        out_specs=[wspec, wspec, wspec], out_shape=[jax.ShapeDtypeStruct((nsh, Fs, D), F32)] * 3,
        compiler_params=_params(2),
    )(h, dxo, g, u, dg, du)


NEG_BIG = -1e30


def _causal_pairs(n, by_key):
    pairs = [(qi, ki) for qi in range(n) for ki in range(qi + 1)]
    if by_key:
        pairs.sort(key=lambda p: (p[1], p[0]))
    return jnp.asarray([p[0] for p in pairs], jnp.int32), jnp.asarray([p[1] for p in pairs], jnp.int32)


def _scores(q, k, masked):
    s = lax.dot_general(q, k, (((1,), (1,)), ((), ())), preferred_element_type=F32)
    if masked:
        row = lax.broadcasted_iota(jnp.int32, s.shape, 0)
        col = lax.broadcasted_iota(jnp.int32, s.shape, 1)
        s = jnp.where(row >= col, s, NEG_BIG)
    return s


def attn_fwd(name, q, k, v, t=ATTN_TILE):
    S, Dk = q.shape[0], HEAD_PAD
    H = q.shape[1] // Dk
    Dv = v.shape[1] // H
    qt, kt = _causal_pairs(S // t, by_key=False)

    def body(qt_ref, kt_ref, q_ref, k_ref, v_ref, o_ref, lse_ref, m_sc, l_sc, acc):
        qi, ki = qt_ref[pl.program_id(1)], kt_ref[pl.program_id(1)]

        @pl.when(ki == 0)
        def _():
            m_sc[...] = jnp.full_like(m_sc, NEG_BIG)
            l_sc[...] = jnp.zeros_like(l_sc)
            acc[...] = jnp.zeros_like(acc)

        def step(masked):
            s = _scores(q_ref[...], k_ref[...], masked)
            m_prev = m_sc[...]
            m_next = jnp.maximum(m_prev, jnp.max(s, axis=-1, keepdims=True))
            alpha = jnp.exp(m_prev - m_next)
            p = jnp.exp(s - jnp.tile(m_next, (1, t // STAT_LANES)))
            l_sc[...] = alpha * l_sc[...] + jnp.sum(p, axis=-1, keepdims=True)
            acc[...] = alpha * acc[...] + jnp.dot(p.astype(BF16), v_ref[...].astype(BF16), preferred_element_type=F32)
            m_sc[...] = m_next

        @pl.when(ki < qi)
        def _():
            step(False)

        @pl.when(ki == qi)
        def _():
            step(True)
            o_ref[...] = acc[...] / l_sc[...]
            lse_ref[...] = m_sc[...] + jnp.log(l_sc[...])

    stat = pltpu.VMEM((t, STAT_LANES), F32)
    return pl.pallas_call(
        body, name=name,
        grid_spec=pltpu.PrefetchScalarGridSpec(
            num_scalar_prefetch=2, grid=(H, qt.shape[0]),
            in_specs=[pl.BlockSpec((t, Dk), lambda h, s, qt, kt: (qt[s], h)),
                      pl.BlockSpec((t, Dk), lambda h, s, qt, kt: (kt[s], h)),
                      pl.BlockSpec((t, Dv), lambda h, s, qt, kt: (kt[s], h))],
            out_specs=[pl.BlockSpec((t, Dv), lambda h, s, qt, kt: (qt[s], h)),
                       pl.BlockSpec((None, t, STAT_LANES), lambda h, s, qt, kt: (h, qt[s], 0))],
            scratch_shapes=[stat, stat, pltpu.VMEM((t, Dv), F32)]),
        out_shape=[jax.ShapeDtypeStruct((S, H * Dv), F32), jax.ShapeDtypeStruct((H, S, STAT_LANES), F32)],
        compiler_params=_params(2),
    )(qt, kt, q, k, v)


def attn_bwd(name, q, k, v, do, o, lse, t=ATTN_TILE):
    S, Dk = q.shape[0], HEAD_PAD
    H = q.shape[1] // Dk
    Dv = v.shape[1] // H
    qt, kt = _causal_pairs(S // t, by_key=True)
    tn_dims = (((0,), (0,)), ((), ()))

    def body(qt_ref, kt_ref, q_ref, k_ref, v_ref, do_ref, o_ref, lse_ref, dq_ref, dk_ref, dv_ref):
        step_id = pl.program_id(1)
        qi, ki = qt_ref[step_id], kt_ref[step_id]

        @pl.when(step_id == 0)
        def _():
            dq_ref[...] = jnp.zeros_like(dq_ref)

        def step(masked):
            s = _scores(q_ref[...], k_ref[...], masked)
            reps = (1, t // STAT_LANES)
            p = jnp.exp(s - jnp.tile(lse_ref[...], reps))
            dov = do_ref[...]
            delta = jnp.broadcast_to(jnp.sum(dov * o_ref[...], axis=-1, keepdims=True), (t, STAT_LANES))
            dob = dov.astype(BF16)
            dp = lax.dot_general(dob, v_ref[...].astype(BF16), (((1,), (1,)), ((), ())), preferred_element_type=F32)
            ds = (p * (dp - jnp.tile(delta, reps))).astype(BF16)
            pdv = lax.dot_general(p.astype(BF16), dob, tn_dims, preferred_element_type=F32)
            pdk = lax.dot_general(ds, q_ref[...], tn_dims, preferred_element_type=F32)
            rows = pl.ds(pl.multiple_of(qi * t, t), t)
            dq_ref[rows, :] += jnp.dot(ds, k_ref[...], preferred_element_type=F32)
            return pdk, pdv

        @pl.when(ki == qi)
        def _():
            dk_ref[...] = jnp.zeros_like(dk_ref)
            dv_ref[...] = jnp.zeros_like(dv_ref)

        def accumulate(masked):
            pdk, pdv = step(masked)
            dk_ref[...] += pdk
            dv_ref[...] += pdv

        @pl.when(ki == qi)
        def _():
            accumulate(True)

        @pl.when(ki < qi)
        def _():
            accumulate(False)

    qrow = lambda h, s, qt, kt: (qt[s], h)
    krow = lambda h, s, qt, kt: (kt[s], h)
    return pl.pallas_call(
        body, name=name,
        grid_spec=pltpu.PrefetchScalarGridSpec(
            num_scalar_prefetch=2, grid=(H, qt.shape[0]),
            in_specs=[pl.BlockSpec((t, Dk), qrow), pl.BlockSpec((t, Dk), krow), pl.BlockSpec((t, Dv), krow),
                      pl.BlockSpec((t, Dv), qrow), pl.BlockSpec((t, Dv), qrow),
                      pl.BlockSpec((None, t, STAT_LANES), lambda h, s, qt, kt: (h, qt[s], 0))],
            out_specs=[pl.BlockSpec((S, Dk), lambda h, s, qt, kt: (0, h)), pl.BlockSpec((t, Dk), krow),
                       pl.BlockSpec((t, Dv), krow)]),
        out_shape=[jax.ShapeDtypeStruct((S, H * Dk), F32), jax.ShapeDtypeStruct((S, H * Dk), F32),
                   jax.ShapeDtypeStruct((S, H * Dv), F32)],
        compiler_params=_params(2),
    )(qt, kt, q, k, v, do, o, lse)


def _cmul(ar, ai, br, bi):
    return ar * br - ai * bi, ar * bi + ai * br


def _scan_tile(x_r, x_i, ar_ref, ai_ref, cr_sc, ci_sc, *, reverse, first, states=None):
    tc, lanes = x_r.shape
    nblk, lb = tc // SUBLANES, SCAN_LANE_TILE
    with_da = states is not None
    if with_da:
        xr_all, xi_all, pr_all, pi_all, dar_all, dai_all, chunk = states

    @pl.when(first)
    def _():
        cr_sc[...] = jnp.zeros_like(cr_sc)
        ci_sc[...] = jnp.zeros_like(ci_sc)
        if with_da:
            dar_all[...] = jnp.zeros_like(dar_all)
            dai_all[...] = jnp.zeros_like(dai_all)

    row = lax.broadcasted_iota(jnp.int32, (SUBLANES, lb), 0)
    for l0 in range(0, lanes, lb):
        _scan_lanes(x_r.at[:, pl.ds(l0, lb)], x_i.at[:, pl.ds(l0, lb)], ar_ref[0:1, pl.ds(l0, lb)],
                    ai_ref[0:1, pl.ds(l0, lb)], cr_sc.at[:, pl.ds(l0, lb)], ci_sc.at[:, pl.ds(l0, lb)], row, reverse,
                    nblk, None if not with_da else tuple(r.at[:, pl.ds(l0, lb)] for r in states[:6]) + (chunk,))


def _scan_lanes(x_r, x_i, a1r, a1i, cr_sc, ci_sc, row, reverse, nblk, states):
    lb = x_r.shape[1]
    with_da = states is not None
    if with_da:
        xr_ref, xi_ref, pr_ref, pi_ref, dar_ref, dai_ref, chunk = states
    if reverse:
        a1i = -a1i
    a2r, a2i = _cmul(a1r, a1i, a1r, a1i)
    a4r, a4i = _cmul(a2r, a2i, a2r, a2i)
    pw_r, pw_i = jnp.zeros((SUBLANES, lb), F32), jnp.zeros((SUBLANES, lb), F32)
    cur_r, cur_i = a1r, a1i
    for e in range(SUBLANES):
        r_at = (SUBLANES - 1 - e) if reverse else e
        pw_r = jnp.where(row == r_at, cur_r, pw_r)
        pw_i = jnp.where(row == r_at, cur_i, pw_i)
        cur_r, cur_i = _cmul(cur_r, cur_i, a1r, a1i)
    steps = []
    for d, pr, pi in ((1, a1r, a1i), (2, a2r, a2i), (4, a4r, a4i)):
        keep = (row < SUBLANES - d) if reverse else (row >= d)
        steps.append((d, jnp.where(keep, pr, 0.0), jnp.where(keep, pi, 0.0)))

    def block(jb, carry):
        if with_da:
            cr, ci, acc_r, acc_i = carry
        else:
            cr, ci = carry
        idx = (nblk - 1 - jb) if reverse else jb
        r0 = pl.multiple_of(idx * SUBLANES, SUBLANES)
        xr = x_r[pl.ds(r0, SUBLANES), :]
        xi = x_i[pl.ds(r0, SUBLANES), :]
        for d, pr, pi in steps:
            shift = SUBLANES - d if reverse else d
            sr, si = pltpu.roll(xr, shift, 0), pltpu.roll(xi, shift, 0)
            xr, xi = xr + pr * sr - pi * si, xi + pr * si + pi * sr
        xr, xi = xr + pw_r * cr - pw_i * ci, xi + pw_r * ci + pw_i * cr
        x_r[pl.ds(r0, SUBLANES), :] = xr
        x_i[pl.ds(r0, SUBLANES), :] = xi
        edge = 0 if reverse else SUBLANES - 1
        cr, ci = xr[edge:edge + 1, :], xi[edge:edge + 1, :]
        if not with_da:
            return cr, ci
        fr = xr_ref[pl.ds(r0, SUBLANES), :]
        fi = xi_ref[pl.ds(r0, SUBLANES), :]
        rp = pl.multiple_of(jnp.maximum(idx - 1, 0) * SUBLANES, SUBLANES)
        inside = idx > 0
        before_r = jnp.where(inside, xr_ref[pl.ds(rp, SUBLANES), :], pr_ref[...])
        before_i = jnp.where(inside, xi_ref[pl.ds(rp, SUBLANES), :], pi_ref[...])
        live = jnp.where(jnp.logical_or(inside, chunk > 0), 1.0, 0.0)
        last_r = before_r[SUBLANES - 1:SUBLANES, :] * live
        last_i = before_i[SUBLANES - 1:SUBLANES, :] * live
        pvr = jnp.where(row == 0, last_r, pltpu.roll(fr, 1, 0))
        pvi = jnp.where(row == 0, last_i, pltpu.roll(fi, 1, 0))
        acc_r = acc_r + xr * pvr + xi * pvi
        acc_i = acc_i + xi * pvr - xr * pvi
        return cr, ci, acc_r, acc_i

    init = (cr_sc[...], ci_sc[...])
    if with_da:
        init = init + (jnp.zeros((SUBLANES, lb), F32), jnp.zeros((SUBLANES, lb), F32))
    fin = lax.fori_loop(0, nblk, block, init)
    cr_sc[...] = fin[0]
    ci_sc[...] = fin[1]
    if with_da:
        dar_ref[...] += fin[2]
        dai_ref[...] += fin[3]


SSM_BLOCKS = 4
BLOCK_CH = SSM_WIDTH // SSM_BLOCKS
PREP_LANES = SSM_LANES // SSM_BLOCKS


def ssm_fwd(name, u, bb_r, bb_i, cb_r, ncb_i, a_r8, a_i8):
    S = u.shape[0]
    tc = min(SCAN_TIME_TILE, S)

    def body(u_ref, bbr_ref, bbi_ref, cbr_ref, ncbi_ref, ar_ref, ai_ref, xr_ref, xi_ref, y_ref, cr_sc, ci_sc):
        ub = u_ref[...].astype(BF16)
        xr_ref[...] = jnp.dot(ub, bbr_ref[...], preferred_element_type=F32)
        xi_ref[...] = jnp.dot(ub, bbi_ref[...], preferred_element_type=F32)
        _scan_tile(xr_ref, xi_ref, ar_ref, ai_ref, cr_sc, ci_sc, reverse=False, first=pl.program_id(1) == 0)
        y_ref[...] = (lax.dot_general(xr_ref[...].astype(BF16), cbr_ref[...], NT_DIMS, preferred_element_type=F32)
                      + lax.dot_general(xi_ref[...].astype(BF16), ncbi_ref[...], NT_DIMS, preferred_element_type=F32))

    ch = pl.BlockSpec((tc, BLOCK_CH), lambda b, t: (t, b))
    st = pl.BlockSpec((tc, PREP_LANES), lambda b, t: (t, b))
    wt = pl.BlockSpec((None, BLOCK_CH, PREP_LANES), lambda b, t: (b, 0, 0))
    par = pl.BlockSpec((SUBLANES, PREP_LANES), lambda b, t: (0, b))
    return pl.pallas_call(
        body, name=name, grid=(SSM_BLOCKS, S // tc), in_specs=[ch, wt, wt, wt, wt, par, par], out_specs=[st, st, ch],
        out_shape=[jax.ShapeDtypeStruct((S, SSM_LANES), F32), jax.ShapeDtypeStruct((S, SSM_LANES), F32),
                   jax.ShapeDtypeStruct((S, SSM_WIDTH), F32)],
        scratch_shapes=[pltpu.VMEM((1, PREP_LANES), F32), pltpu.VMEM((1, PREP_LANES), F32)], compiler_params=_params(2),
    )(u, bb_r, bb_i, cb_r, ncb_i, a_r8, a_i8)


def ssm_bwd(name, dy, du_add, u, xs_r, xs_i, bb_r, bb_i, cb_r, ncb_i, a_r8, a_i8):
    S = u.shape[0]
    tc = min(SCAN_TIME_TILE, S)
    nt = S // tc

    def body(dy_ref, dua_ref, u_ref, xr_ref, xi_ref, pr_ref, pi_ref, bbr_ref, bbi_ref, cbr_ref, ncbi_ref, ar_ref, ai_ref,
             du_ref, dbbr_ref, dbbi_ref, dcbr_ref, dncbi_ref, dar_ref, dai_ref, lr_sc, li_sc, cr_sc, ci_sc):
        t = pl.program_id(1)
        first = t == 0

        @pl.when(first)
        def _():
            for r in (dbbr_ref, dbbi_ref, dcbr_ref, dncbi_ref):
                r[...] = jnp.zeros_like(r)

        dyb = dy_ref[...].astype(BF16)
        lr_sc[...] = jnp.dot(dyb, cbr_ref[...], preferred_element_type=F32)
        li_sc[...] = jnp.dot(dyb, ncbi_ref[...], preferred_element_type=F32)
        _scan_tile(lr_sc, li_sc, ar_ref, ai_ref, cr_sc, ci_sc, reverse=True, first=first,
                   states=(xr_ref, xi_ref, pr_ref, pi_ref, dar_ref, dai_ref, nt - 1 - t))
        lrb, lib = lr_sc[...].astype(BF16), li_sc[...].astype(BF16)
        du_ref[...] = (lax.dot_general(lrb, bbr_ref[...], NT_DIMS, preferred_element_type=F32)
                       + lax.dot_general(lib, bbi_ref[...], NT_DIMS, preferred_element_type=F32) + dua_ref[...])
        ub = u_ref[...].astype(BF16)
        dbbr_ref[...] += lax.dot_general(ub, lrb, TN_DIMS, preferred_element_type=F32)
        dbbi_ref[...] += lax.dot_general(ub, lib, TN_DIMS, preferred_element_type=F32)
        dcbr_ref[...] += lax.dot_general(dyb, xr_ref[...].astype(BF16), TN_DIMS, preferred_element_type=F32)
        dncbi_ref[...] += lax.dot_general(dyb, xi_ref[...].astype(BF16), TN_DIMS, preferred_element_type=F32)

    ch = pl.BlockSpec((tc, BLOCK_CH), lambda b, t: (nt - 1 - t, b))
    st = pl.BlockSpec((tc, PREP_LANES), lambda b, t: (nt - 1 - t, b))
    prev = pl.BlockSpec((SUBLANES, PREP_LANES), lambda b, t: (jnp.maximum((nt - 1 - t) * (tc // SUBLANES) - 1, 0), b))
    wt = pl.BlockSpec((None, BLOCK_CH, PREP_LANES), lambda b, t: (b, 0, 0))
    par = pl.BlockSpec((SUBLANES, PREP_LANES), lambda b, t: (0, b))
    blk = jax.ShapeDtypeStruct((SSM_BLOCKS, BLOCK_CH, PREP_LANES), F32)
    rows8 = jax.ShapeDtypeStruct((SUBLANES, SSM_LANES), F32)
    return pl.pallas_call(
        body, name=name, grid=(SSM_BLOCKS, nt), in_specs=[ch, ch, ch, st, st, prev, prev, wt, wt, wt, wt, par, par],
        out_specs=[ch, wt, wt, wt, wt, par, par],
        out_shape=[jax.ShapeDtypeStruct((S, SSM_WIDTH), F32), blk, blk, blk, blk, rows8, rows8],
        scratch_shapes=[pltpu.VMEM((tc, PREP_LANES), F32), pltpu.VMEM((tc, PREP_LANES), F32),
                        pltpu.VMEM((1, PREP_LANES), F32), pltpu.VMEM((1, PREP_LANES), F32)],
        compiler_params=_params(2),
    )(dy, du_add, u, xs_r, xs_i, xs_r, xs_i, bb_r, bb_i, cb_r, ncb_i, a_r8, a_i8)


def _ssm_prep_f(a_re, a_im, log_dt, bt_re, bt_im, c_re, c_im):
    first_group = pl.program_id(0) * (SSM_GROUPS // SSM_BLOCKS)
    iota = lambda shape, d: lax.broadcasted_iota(jnp.int32, shape, d)
    grp_of_row = lambda shape: iota(shape, 0) >> int(math.log2(SSM_GROUP))
    grp_of_lane = lambda shape: iota(shape, 1) >> int(math.log2(SSM_STATE))
    rep = (grp_of_row((BLOCK_CH, SSM_GROUPS)) + first_group == iota((BLOCK_CH, SSM_GROUPS), 1)).astype(F32)
    til = ((iota((SSM_STATE, PREP_LANES), 1) & (SSM_STATE - 1)) == iota((SSM_STATE, PREP_LANES), 0)).astype(F32)
    m_rows = (grp_of_row((BLOCK_CH, PREP_LANES)) == grp_of_lane((BLOCK_CH, PREP_LANES))).astype(F32)
    m_grp = (iota((SSM_GROUPS, PREP_LANES), 0) == grp_of_lane((SSM_GROUPS, PREP_LANES)) + first_group).astype(F32)
    dt = jnp.exp(log_dt)
    decay = jnp.exp(a_re * dt)
    ar = decay * jnp.cos(a_im * dt)
    ai = decay * jnp.sin(a_im * dt)
    den = a_re * a_re + a_im * a_im
    nr = ar - 1.0
    coef_r = (nr * a_re + ai * a_im) / den
    coef_i = (ai * a_re - nr * a_im) / den
    cr, ci = _hp_dot(rep, coef_r), _hp_dot(rep, coef_i)
    bb_r = cr * bt_re - ci * bt_im
    bb_i = cr * bt_im + ci * bt_re
    big = lambda m: _hp_dot(m, til) * m_rows
    lanes = lambda m: jnp.broadcast_to(jnp.sum(_hp_dot(m, til) * m_grp, axis=0, keepdims=True), (SUBLANES, PREP_LANES))
    return lanes(ar), lanes(ai), big(bb_r), big(bb_i), big(c_re), -big(c_im)


def _whole(arr, **kw):
    nd = arr.ndim
    return In(arr, arr.shape, lambda *_: (0,) * nd, **kw)


def _adamw_math(w, g, m, v):
    m = ADAM_B1 * m + (1.0 - ADAM_B1) * g
    v = ADAM_B2 * v + (1.0 - ADAM_B2) * (g * g)
    m_hat = m / (1.0 - ADAM_B1 ** ADAM_STEP)
    v_hat = v / (1.0 - ADAM_B2 ** ADAM_STEP)
    delta = -ADAM_LR * (m_hat / (jnp.sqrt(v_hat) + ADAM_EPS) + ADAM_WD * w)
    return delta, m, v


def adamw_big(name, w, g, m, v, after):
    R, C = w.shape
    tr = R
    for cand in (512, 344, 256, 128):
        if R % cand == 0:
            tr = cand
            break

    def body(w_ref, g_ref, m_ref, v_ref, after_ref, d_ref, nm_ref, nv_ref):
        d, nm, nv = _adamw_math(w_ref[...], g_ref[...], m_ref[...], v_ref[...])
        d_ref[...] = d
        nm_ref[...] = nm
        nv_ref[...] = nv

    spec = pl.BlockSpec((tr, C), lambda i: (i, 0))
    return pl.pallas_call(
        body, name=name, grid=(R // tr,), in_specs=[spec] * 4 + [pl.BlockSpec(memory_space=pl.ANY)], out_specs=[spec] * 3,
        out_shape=[jax.ShapeDtypeStruct((R, C), F32)] * 3, compiler_params=_params(1),
    )(w, g, m, v, after)


def adamw_small(name, ws, gs, ms, vs, after):
    n = len(ws)

    def body(*refs):
        outs = refs[4 * n + 1:]
        for k in range(n):
            d, nm, nv = _adamw_math(refs[k][...], refs[n + k][...], refs[2 * n + k][...], refs[3 * n + k][...])
            outs[k][...] = d
            outs[n + k][...] = nm
            outs[2 * n + k][...] = nv

    vm = pl.BlockSpec(memory_space=pltpu.VMEM)
    shapes = [jax.ShapeDtypeStruct(w.shape, F32) for w in ws]
    res = pl.pallas_call(
        body, name=name, in_specs=[vm] * (4 * n) + [pl.BlockSpec(memory_space=pl.ANY)], out_specs=[vm] * (3 * n),
        out_shape=shapes * 3, compiler_params=pltpu.CompilerParams(vmem_limit_bytes=VMEM_LIMIT_BYTES),
    )(*ws, *gs, *ms, *vs, after)
    return res[:n], res[n:2 * n], res[2 * n:]


def _place():
    return lax.axis_index("x"), lax.axis_index("y"), lax.axis_index("c")


def _other_chips(x, y):
    return [(1 - x, y), (x, 1 - y), (1 - x, 1 - y)]


HBM = pl.BlockSpec(memory_space=pl.ANY)


def all_gather_halves(name, blocks):
    n = len(blocks)

    def body(*refs):
        in_refs, out_refs = refs[:n], refs[n:2 * n]
        send_sems, recv_sems = refs[2 * n:]
        x, y, c = _place()
        me, sibling = (x, y, c), (x, y, 1 - c)
        chips = _other_chips(x, y)

        def slot(a, px, py, pc):
            return out_refs[a].at[4 * px + 2 * py + pc]

        def copy(a, k, block, to, src=None):
            return pltpu.make_async_remote_copy(
                src_ref=slot(a, *block) if src is None else src, dst_ref=slot(a, *block),
                send_sem=send_sems.at[a, k], recv_sem=recv_sems.at[a, k], device_id=to, device_id_type=MESH)

        first = []
        for a in range(n):
            first.append(copy(a, 0, me, sibling, src=in_refs[a]))
            first += [copy(a, 1 + j, me, (*chip, c), src=in_refs[a]) for j, chip in enumerate(chips)]
        for cp in first:
            cp.start()
        passed = []
        for j, chip in enumerate(chips):
            for a in range(n):
                copy(a, 1 + j, (*chip, c), me).wait_recv()
                fw = copy(a, 4 + j, (*chip, c), sibling)
                fw.start()
                passed.append(fw)
        for a in range(n):
            copy(a, 0, sibling, me).wait_recv()
            for j, chip in enumerate(chips):
                copy(a, 4 + j, (*chip, 1 - c), me).wait_recv()
        for cp in first + passed:
            cp.wait_send()

    res = pl.pallas_call(
        body, name=name, in_specs=[HBM] * n, out_specs=[HBM] * n,
        out_shape=[jax.ShapeDtypeStruct((N_DEV,) + b.shape, b.dtype) for b in blocks],
        scratch_shapes=[pltpu.SemaphoreType.DMA((n, 7)), pltpu.SemaphoreType.DMA((n, 7))],
    )(*blocks)
    return list(res)


def pair_send_halves(name, grads):
    n = len(grads)

    def body(*refs):
        in_refs, out_refs = refs[:n], refs[n:2 * n]
        send_sems, recv_sems = refs[2 * n:]
        x, y, c = _place()
        cps = []
        for a in range(n):
            cp = pltpu.make_async_remote_copy(
                src_ref=in_refs[a].at[:, 1 - c], dst_ref=out_refs[a], send_sem=send_sems.at[a], recv_sem=recv_sems.at[a],
                device_id=(x, y, 1 - c), device_id_type=MESH)
            cp.start()
            cps.append(cp)
        for cp in cps:
            cp.wait()

    res = pl.pallas_call(
        body, name=name, in_specs=[HBM] * n, out_specs=[HBM] * n,
        out_shape=[jax.ShapeDtypeStruct((g.shape[0],) + g.shape[2:], g.dtype) for g in grads],
        scratch_shapes=[pltpu.SemaphoreType.DMA((n,)), pltpu.SemaphoreType.DMA((n,))],
    )(*grads)
    return list(res)


def pair_add(name, grad, got, c_arr):
    nsh, _, M, N = grad.shape
    tr = M
    for cand in (2560, 512, 256, 192, 128, 64, 16):
        if M % cand == 0:
            tr = cand
            break

    def body(c_ref, g_ref, p_ref, o_ref):
        o_ref[...] = (g_ref[...] + p_ref[...]).astype(BF16)

    return pl.pallas_call(
        body, name=name,
        grid_spec=pltpu.PrefetchScalarGridSpec(
            num_scalar_prefetch=1, grid=(nsh, M // tr),
            in_specs=[pl.BlockSpec((None, None, tr, N), lambda j, i, c_ref: (j, c_ref[0], i, 0)),
                      pl.BlockSpec((None, tr, N), lambda j, i, c_ref: (j, i, 0))],
            out_specs=pl.BlockSpec((None, tr, N), lambda j, i, c_ref: (j, i, 0))),
        out_shape=jax.ShapeDtypeStruct((nsh, M, N), BF16), compiler_params=_params(2),
    )(c_arr, grad, got)


def sum_chips(name, q):
    nsh, M, N = q.shape
    tr = M
    for cand in (2560, 512, 256, 192, 128, 64, 16):
        if M % cand == 0:
            tr = cand
            break

    def body(q_ref, o_ref):
        acc = q_ref[0].astype(F32)
        for j in range(1, nsh):
            acc = acc + q_ref[j].astype(F32)
        o_ref[...] = acc

    return pl.pallas_call(
        body, name=name, grid=(M // tr,), in_specs=[pl.BlockSpec((nsh, tr, N), lambda i: (0, i, 0))],
        out_specs=pl.BlockSpec((tr, N), lambda i: (i, 0)), out_shape=jax.ShapeDtypeStruct((M, N), F32),
        compiler_params=_params(1),
    )(q)


def pair_exchange(name, halves):
    n = len(halves)

    def body(*refs):
        in_refs, out_refs = refs[:n], refs[n:2 * n]
        send_sems, recv_sems = refs[2 * n:]
        x, y, c = _place()
        cps = []
        for a in range(n):
            cp = pltpu.make_async_remote_copy(
                src_ref=in_refs[a], dst_ref=out_refs[a], send_sem=send_sems.at[a], recv_sem=recv_sems.at[a],
                device_id=(x, y, 1 - c), device_id_type=MESH)
            cp.start()
            cps.append(cp)
        for cp in cps:
            cp.wait()

    res = pl.pallas_call(
        body, name=name, in_specs=[HBM] * n, out_specs=[HBM] * n,
        out_shape=[jax.ShapeDtypeStruct(h.shape, h.dtype) for h in halves],
        scratch_shapes=[pltpu.SemaphoreType.DMA((n,)), pltpu.SemaphoreType.DMA((n,))],
    )(*halves)
    return list(res)


SEM = pl.BlockSpec(memory_space=pltpu.SEMAPHORE)
IN_HBM = pl.BlockSpec(memory_space=pltpu.HBM)
SPLIT_COPY = pltpu.CompilerParams(has_side_effects=pltpu.SideEffectType.DATAFLOW_SIDE_EFFECTING)


def _scatter_copies(src_refs, dst_refs, send_sems, recv_sems):
    x, y, c = _place()
    mine = 2 * x + y
    return [pltpu.make_async_remote_copy(
        src_ref=src_refs[a].at[2 * px + py], dst_ref=dst_refs[a].at[mine], send_sem=send_sems.at[a * (N_CHIPS - 1) + k],
        recv_sem=recv_sems.at[a * (N_CHIPS - 1) + k], device_id=(px, py, c), device_id_type=MESH)
        for a in range(len(src_refs)) for k, (px, py) in enumerate(_other_chips(x, y))]


def _pair_copies(src_refs, dst_refs, send_sems, recv_sems):
    x, y, c = _place()
    return [pltpu.make_async_remote_copy(
        src_ref=src_refs[a].at[:, 1 - c], dst_ref=dst_refs[a], send_sem=send_sems.at[a], recv_sem=recv_sems.at[a],
        device_id=(x, y, 1 - c), device_id_type=MESH) for a in range(len(src_refs))]


def _gather_copies(src_refs, dst_refs, send_sems, recv_sems):
    x, y, c = _place()
    me = 4 * x + 2 * y + c
    cps = []
    for a in range(len(src_refs)):
        for k in range(1, N_DEV):
            to = (1 - x if k & 4 else x, 1 - y if k & 2 else y, 1 - c if k & 1 else c)
            s = a * (N_DEV - 1) + k - 1
            cps.append(pltpu.make_async_remote_copy(
                src_ref=src_refs[a], dst_ref=dst_refs[a].at[me], send_sem=send_sems.at[s], recv_sem=recv_sems.at[s],
                device_id=to, device_id_type=MESH))
    return cps


def split_copy_start(name, copies, n_sem, srcs, land_shapes):
    n = len(srcs)
    lands = [lax.empty(s.shape, s.dtype) for s in land_shapes]

    def body(*refs):
        for cp in copies(refs[:n], refs[n:2 * n], refs[2 * n], refs[2 * n + 1]):
            cp.start()
        refs[-1][...] = jnp.zeros_like(refs[-1])

    thru = [pltpu.HBM(a.shape, a.dtype) for a in (*srcs, *lands)]
    res = pl.pallas_call(
        body, name=name, in_specs=[IN_HBM] * (2 * n),
        out_specs=(SEM, SEM, *[IN_HBM] * (2 * n), pl.BlockSpec(memory_space=pltpu.VMEM)),
        out_shape=(pltpu.SemaphoreType.DMA((n * n_sem,)), pltpu.SemaphoreType.DMA((n * n_sem,)), *thru,
                   jax.ShapeDtypeStruct((SUBLANES, LANES), F32)),
        input_output_aliases={i: 2 + i for i in range(2 * n)}, compiler_params=SPLIT_COPY,
    )(*[pltpu.with_memory_space_constraint(a, pltpu.HBM) for a in (*srcs, *lands)])
    return (copies, n, res[0], res[1], res[2:2 + 2 * n]), res[-1][0, 0]


def split_copy_wait(name, handle, after):
    copies, n, send_sems, recv_sems, thru = handle

    def body(*refs):
        for cp in copies(refs[:n], refs[n:2 * n], refs[2 * n], refs[2 * n + 1]):
            cp.wait_send()
            cp.wait_recv()

    res = pl.pallas_call(
        body, name=name, in_specs=[IN_HBM] * (2 * n) + [SEM, SEM, pl.BlockSpec(memory_space=pl.ANY)],
        out_specs=[IN_HBM] * (2 * n), out_shape=[pltpu.HBM(a.shape, a.dtype) for a in thru],
        input_output_aliases={i: i for i in range(2 * n)}, compiler_params=SPLIT_COPY,
    )(*thru, send_sems, recv_sems, after)
    return list(res[:n]), list(res[n:])


def all_reduce_small(name, v):
    R, C = v.shape

    def body(v_ref, o_ref, gath, send_sems, recv_sems):
        x, y, c = _place()
        me, sibling = (x, y, c), (x, y, 1 - c)
        chips = _other_chips(x, y)

        def slot(px, py, pc):
            return gath.at[4 * px + 2 * py + pc]

        def copy(k, block, to, src=None):
            return pltpu.make_async_remote_copy(
                src_ref=slot(*block) if src is None else src, dst_ref=slot(*block),
                send_sem=send_sems.at[k], recv_sem=recv_sems.at[k], device_id=to, device_id_type=MESH)

        first = [copy(0, me, sibling, src=v_ref)]
        first += [copy(1 + j, me, (*chip, c), src=v_ref) for j, chip in enumerate(chips)]
        for cp in first:
            cp.start()
        slot(*me)[...] = v_ref[...]
        passed = [copy(4 + j, (*chip, c), sibling) for j, chip in enumerate(chips)]
        for j, chip in enumerate(chips):
            copy(1 + j, (*chip, c), me).wait_recv()
            passed[j].start()
        copy(0, sibling, me).wait_recv()
        for j, chip in enumerate(chips):
            copy(4 + j, (*chip, 1 - c), me).wait_recv()
        for cp in first + passed:
            cp.wait_send()
        acc = gath[0]
        for d in range(1, N_DEV):
            acc = acc + gath[d]
        o_ref[...] = acc

    vm = pl.BlockSpec(memory_space=pltpu.VMEM)
    return pl.pallas_call(
        body, name=name, in_specs=[vm], out_specs=vm, out_shape=jax.ShapeDtypeStruct((R, C), F32),
        scratch_shapes=[pltpu.VMEM((N_DEV, R, C), F32), pltpu.SemaphoreType.DMA((7,)), pltpu.SemaphoreType.DMA((7,))],
        compiler_params=pltpu.CompilerParams(vmem_limit_bytes=VMEM_LIMIT_BYTES),
    )(v)


LANES = 128
PACK_ROW_MULTIPLE = 1024
SMALL_SHARDED = {
    "w_in": ((D_MODEL, 1216), 1), "mla_w_uq": ((MLA_Q_RANK, 768), 1), "mla_w_ukv": ((MLA_KV_RANK, 1024), 1),
    "ssm_w_glu": ((SSM_WIDTH, SSM_WIDTH), 0), "w_o": ((D_MODEL, D_MODEL), 0), "xattn_w_q": ((D_MODEL, 512), 0),
    "xattn_w_kv": ((D_MODEL, 1024), 0), "xattn_w_o": ((512, D_MODEL), 1),
}
FFN_NAMES = ["ffn1_w_gate", "ffn1_w_up", "ffn1_w_down", "ffn2_w_gate", "ffn2_w_up", "ffn2_w_down"]
TRANSPOSED_VIEW = ("ffn1_w_gate", "ffn1_w_up", "ffn2_w_gate", "ffn2_w_up", "w_in", "mla_w_uq")


def _shard_shape(name):
    (r, cdim), ax = SMALL_SHARDED[name]
    return (r // N_CHIPS, cdim) if ax == 0 else (r, cdim // N_CHIPS)


def _pack_shards(shards):
    parts = []
    for name in SMALL_SHARDED:
        a = shards[name]
        lead = a.shape[:-2]
        parts.append(a.reshape(lead + (a.shape[-2] * a.shape[-1] // LANES, LANES)))
    rows = sum(q.shape[-2] for q in parts)
    parts.append(jnp.zeros(lead + (-rows % PACK_ROW_MULTIPLE, LANES), parts[0].dtype))
    return jnp.concatenate(parts, axis=-2)


def _unpack_shards(packed):
    out, r0 = {}, 0
    lead = packed.shape[:-2]
    for name in SMALL_SHARDED:
        r, cdim = _shard_shape(name)
        rows = r * cdim // LANES
        out[name] = packed[..., r0:r0 + rows, :].reshape(lead + (r, cdim))
        r0 += rows
    return out


def _full_from_shards(name, sh):
    (r, cdim), ax = SMALL_SHARDED[name]
    if ax == 0:
        return sh.reshape(r, cdim)
    return jnp.transpose(sh, (1, 0, 2)).reshape(r, cdim)


def _shards_from_full(name, full):
    (r, cdim), ax = SMALL_SHARDED[name]
    if ax == 0:
        return full.reshape(N_CHIPS, r // N_CHIPS, cdim)
    return jnp.transpose(full.reshape(r, N_CHIPS, cdim // N_CHIPS), (1, 0, 2))


SMALL_REPL = {
    "ffn1_norm": (1, 1024), "mix_norm": (1, 1024), "mla_q_norm": (1, 384), "mla_kv_norm": (1, 256),
    "mla_qk_norm_q": (1, 192), "mla_qk_norm_k": (1, 192), "ssm_a_re": (32, 64), "ssm_a_im": (32, 64),
    "ssm_log_dt": (32, 1), "ssm_b_re": (32, 64, 16), "ssm_b_im": (32, 64, 16), "ssm_c_re": (32, 16, 64),
    "ssm_c_im": (32, 16, 64), "ssm_d": (1, 512), "ssm_b_glu": (1, 512), "out_norm_mla": (1, 512),
    "out_norm_ssm": (1, 512), "xattn_norm": (1, 1024), "mem_norm": (1, 1024), "xattn_q_norm": (1, 128),
    "xattn_k_norm": (1, 128), "ffn2_norm": (1, 1024),
}


def _pack_repl(grads, loss):
    flat = jnp.concatenate([grads[n].reshape(-1) for n in SMALL_REPL] + [loss.reshape(1)])
    rows = -(-flat.shape[0] // (LANES * SUBLANES)) * SUBLANES
    return jnp.pad(flat, (0, rows * LANES - flat.shape[0])).reshape(rows, LANES)


def _unpack_repl(packed):
    flat, out, o = packed.reshape(-1), {}, 0
    for n, shp in SMALL_REPL.items():
        size = int(np.prod(shp))
        out[n] = flat[o:o + size].reshape(shp)
        o += size
    out["loss"] = flat[o]
    return out


def _rope_tables(positions):
    half = MLA_ROPE // 2
    inv = ROPE_THETA ** (-jnp.arange(half, dtype=F32) / half)
    ang = positions.astype(F32)[:, None] * inv[None, :]
    cos, sin = jnp.cos(ang), jnp.sin(ang)
    S = positions.shape[0]
    z = lambda w: jnp.zeros((S, w), F32)
    keep = jnp.concatenate([jnp.ones((S, MLA_NOPE), F32), cos, cos, z(HEAD_PAD - MLA_QK)], axis=1)
    from_hi = jnp.concatenate([z(MLA_NOPE), -sin, z(HEAD_PAD - MLA_NOPE - half)], axis=1)
    from_lo = jnp.concatenate([z(MLA_NOPE + half), sin, z(HEAD_PAD - MLA_QK)], axis=1)
    return keep, from_hi, from_lo


def _norm_rope(x, g, keep, from_hi, from_lo):
    y = x * lax.rsqrt(jnp.sum(x * x, axis=-1, keepdims=True) * (1.0 / MLA_QK) + EPS) * g
    half = MLA_ROPE // 2
    return y * keep + _lane_roll(y, HEAD_PAD - half) * from_hi + _lane_roll(y, half) * from_lo


def local_step(x, mem, positions, target, w, wb, small_weights=None, ffn2_weights=None, on_grads=None):
    if small_weights is None:
        small_weights = lambda after: {}
    if ffn2_weights is None:
        ffn2_weights = lambda after: [wb[k] for k in FFN_NAMES[3:]]
    if on_grads is None:
        on_grads = lambda tag, g: 0.0
    S = x.shape[0]
    tm = min(FUSED_TILE, S)
    g1 = (S // tm,)
    tile = lambda arr, **kw: In(arr, (tm, arr.shape[1]), lambda i: (i, 0), rows=True, **kw)
    par = lambda arr, **kw: In(arr, arr.shape, lambda *_: (0, 0), diff=True, acc=(0,), **kw)
    wt = lambda arr: In(arr, weight=True)
    otile = lambda cols, dt: Out((S, cols), dt, (tm, cols), lambda i: (i, 0))
    grads = {}

    x1, h1, g_1, u_1 = ffn_fwd("ffn1_fwd", x, w["ffn1_norm"], wb["ffn1_w_gate"], wb["ffn1_w_up"], wb["ffn1_w_down"])
    wb = {**wb, **small_weights(x1)}

    keep, from_hi, from_lo = _rope_tables(positions)
    scale = MLA_QK ** -0.5

    def f_pre(xv, kp, fh, fl, g_mix, w_q, w_kv, w_kr, w_u, g_q, g_kv, w_uq, w_ukv, gq, gk):
        h2 = _rms(xv, g_mix)
        cq, ckv = _rms(wdot(h2, w_q), g_q), _rms(wdot(h2, w_kv), g_kv)
        kr = wdot(h2, w_kr)
        qs, ks, vs = [], [], []
        for h in range(MLA_HEADS):
            qs.append(_norm_rope(wdot(cq, w_uq, h), gq, kp, fh, fl) * scale)
            kv = wdot(ckv, w_ukv, h)
            ks.append(_norm_rope(jnp.concatenate([kv[:, :MLA_NOPE], kr], axis=-1), gk, kp, fh, fl))
            vs.append(kv[:, MLA_NOPE:])
        return jnp.concatenate(qs, axis=-1), jnp.concatenate(ks, axis=-1), jnp.concatenate(vs, axis=-1), wdot(h2, w_u)

    def pre_ins(gain_mix):
        return [tile(x1, diff=True), tile(keep), tile(from_hi), tile(from_lo), par(gain_mix),
                wt(wb["w_in_q"]), wt(wb["w_in_kv"]), wt(wb["w_in_kr"]), wt(wb["w_in_u"]),
                par(w["mla_q_norm"]), par(w["mla_kv_norm"]), wt(wb["w_uq"]), wt(wb["w_ukv"]),
                par(w["qk_gain_q"]), par(w["qk_gain_k"])]

    pre_outs = [otile(MLA_HEADS * HEAD_PAD, BF16), otile(MLA_HEADS * HEAD_PAD, BF16), otile(MLA_HEADS * MLA_V, BF16),
                otile(SSM_WIDTH, F32)]
    qh, kh, v0, pu = seg_fwd("pre_mixer_fwd", f_pre, g1, pre_ins(w["mix_norm"]), pre_outs)

    o_mla, lse = attn_fwd("mla_attn_fwd", qh, kh, v0, t=min(ATTN_TILE, S))

    prep_grid = (SSM_BLOCKS,)
    prep_ins = ([_whole(w[k], diff=True, acc=(0,)) for k in ("ssm_a_re", "ssm_a_im", "ssm_log_dt")]
                + [In(w[k], (BLOCK_CH, SSM_STATE), lambda i: (i, 0), diff=True)
                   for k in ("ssm_bt_re", "ssm_bt_im", "ssm_c2_re", "ssm_c2_im")])
    blk_out = Out((SSM_BLOCKS, BLOCK_CH, PREP_LANES), BF16, (None, BLOCK_CH, PREP_LANES), lambda i: (i, 0, 0))
    prep_outs = [Out((SUBLANES, SSM_LANES), F32, (SUBLANES, PREP_LANES), lambda i: (0, i))] * 2 + [blk_out] * 4
    a_r8, a_i8, bb_r, bb_i, cb_r, ncb_i = seg_fwd("ssm_prep_fwd", _ssm_prep_f, prep_grid, prep_ins, prep_outs)

    xs_r, xs_i, y_lin = ssm_fwd("ssm_fwd", pu, bb_r, bb_i, cb_r, ncb_i, a_r8, a_i8)

    M = mem.shape[0]
    f_norm = lambda xv, g: (_rms(xv, g),)
    mem_ins = [In(mem, (M, D_MODEL), lambda i: (0, 0)), par(w["mem_norm"])]
    mem_outs = [Out((M, D_MODEL), BF16, (M, D_MODEL), lambda i: (0, 0))]
    (mn,) = seg_fwd("mem_norm_fwd", f_norm, (1,), mem_ins, mem_outs)
    (kvm,) = mm("xattn_kv_fwd", [mn], [[wb["xattn_w_kv"]]], [F32])

    def f_knorm(kk, gk):
        return (jnp.concatenate([_rms(kk[:, h * XHD:(h + 1) * XHD], gk) for h in range(XH)], axis=-1),)

    keys = ((M, XH * XHD), (M, XH * XHD), lambda i: (0, 0))
    knorm_ins = [In(kvm, keys[1], keys[2], diff=True, grad=keys), par(w["xattn_k_norm"])]
    knorm_outs = [Out(keys[0], F32, keys[1], keys[2])]
    (kn_mem,) = seg_fwd("mem_key_norm_fwd", f_knorm, (1,), knorm_ins, knorm_outs)

    xscale = XHD ** -0.5

    def f_post(o, yl, u, xv, kn, vm, d, w_glu, b, gm, gs, w_o1, w_o2, gx, w_xq, gq, w_xo):
        gl = jax.nn.gelu(yl + d * u)
        so = gl * jax.nn.sigmoid(wdot(gl, w_glu) + b)
        x2 = xv + wdot(_rms(o, gm), w_o1) + wdot(_rms(so, gs), w_o2)
        h3 = _rms(x2, gx)
        heads = []
        for h in range(XH):
            qn = _rms(wdot(h3, w_xq, h), gq)
            s = _bdot_nt(qn, kn[:, h * XHD:(h + 1) * XHD]) * xscale
            e = jnp.exp(s - lax.stop_gradient(jnp.max(s, axis=-1, keepdims=True)))
            p = e * (1.0 / jnp.sum(e, axis=-1, keepdims=True))
            heads.append(_bdot_nn(p, vm[:, h * XHD:(h + 1) * XHD]))
        return (x2 + wdot(jnp.concatenate(heads, axis=-1), w_xo),)

    def post_ins(gain_d):
        half = (M, XH * XHD)
        return [tile(o_mla, diff=True), tile(y_lin, diff=True), tile(pu, diff=True), tile(x1, diff=True),
                In(kn_mem, half, lambda i: (0, 0), diff=True, acc=(0,)),
                In(kvm, half, lambda i: (0, 1), diff=True, acc=(0,), grad=(half, half, lambda i: (0, 0))),
                par(gain_d), wt(wb["ssm_w_glu"]), par(w["ssm_b_glu"]), par(w["out_norm_mla"]), par(w["out_norm_ssm"]),
                wt(wb["w_o_mla"]), wt(wb["w_o_ssm"]), par(w["xattn_norm"]), wt(wb["xattn_w_q"]), par(w["xattn_q_norm"]),
                wt(wb["xattn_w_o"])]

    post_outs = [otile(D_MODEL, F32)]
    (x3,) = seg_fwd("post_mixer_fwd", f_post, g1, post_ins(w["ssm_d"]), post_outs)

    wg2, wu2, wd2 = ffn2_weights(x3)
    dx4, h4, g_2, u_2, parts = ffn_fwd("ffn2_fwd_loss", x3, w["ffn2_norm"], wg2, wu2, wd2, target=target)
    loss = jnp.sum(parts[::SUBLANES, 0])

    dx3, grads["ffn2_norm"], dg_2, du_2 = ffn_bwd_act("ffn2_bwd_act", dx4, x3, w["ffn2_norm"], g_2, u_2, wg2, wu2, wd2)
    grads["ffn2_w_gate"], grads["ffn2_w_up"], grads["ffn2_w_down"] = ffn_bwd_w("ffn2_bwd_w", h4, dx4, g_2, u_2, dg_2, du_2)
    sent = on_grads("ffn2", grads)

    (do_mla, dy_lin, du_a, dx1_a, dkn, dvm, grads["ssm_d"], grads["ssm_w_glu"], grads["ssm_b_glu"], grads["out_norm_mla"],
     grads["out_norm_ssm"], grads["w_o_mla"], grads["w_o_ssm"], grads["xattn_norm"], grads["xattn_w_q"],
     grads["xattn_q_norm"], grads["xattn_w_o"]) = seg_bwd(
        "post_mixer_bwd", f_post, g1, post_ins(w["ssm_d"] + sent), post_outs, [dx3])

    dkk, grads["xattn_k_norm"] = seg_bwd("mem_key_norm_bwd", f_knorm, (1,), knorm_ins, knorm_outs, [dkn])
    w_kv = wb["xattn_w_kv"]
    (dmn,) = mm("xattn_kv_bwd", [dkk, dvm], [[w_kv[:, :XH * XHD]], [w_kv[:, XH * XHD:]]], [F32], trans=True)
    gk_w, gv_w = mm_tn("xattn_kv_bwd_w", [mn], [dkk, dvm], [(0, [0]), (0, [1])])
    grads["xattn_w_kv"] = jnp.concatenate([gk_w, gv_w], axis=1)
    (grads["mem_norm"],) = seg_bwd("mem_norm_bwd", f_norm, (1,), mem_ins, mem_outs, [dmn])

    du, dbb_r, dbb_i, dcb_r, dncb_i, da_r8, da_i8 = ssm_bwd("ssm_bwd", dy_lin, du_a, pu, xs_r, xs_i, bb_r, bb_i, cb_r, ncb_i,
                                                            a_r8, a_i8)
    prep_g = seg_bwd("ssm_prep_bwd", _ssm_prep_f, prep_grid, prep_ins, prep_outs, [da_r8, da_i8, dbb_r, dbb_i, dcb_r, dncb_i])
    for k, gname in enumerate(("ssm_a_re", "ssm_a_im", "ssm_log_dt", "ssm_bt_re", "ssm_bt_im", "ssm_c2_re", "ssm_c2_im")):
        grads[gname] = prep_g[k]

    dqh, dkh, dv0 = attn_bwd("mla_attn_bwd", qh, kh, v0, do_mla, o_mla, lse, t=min(ATTN_TILE, S))

    (dx1, grads["mix_norm"], grads["w_in_q"], grads["w_in_kv"], grads["w_in_kr"], grads["w_in_u"], grads["mla_q_norm"],
     grads["mla_kv_norm"], grads["w_uq"], grads["w_ukv"], grads["qk_gain_q"], grads["qk_gain_k"]) = seg_bwd(
        "pre_mixer_bwd", f_pre, g1, pre_ins(w["mix_norm"]), pre_outs, [dqh, dkh, dv0, du], adds={0: dx1_a},
        row_block=FUSED_ROW_BLOCK)

    sent = on_grads("small", grads)
    dx, grads["ffn1_norm"], dg_1, du_1 = ffn_bwd_act("ffn1_bwd_act", dx1, x, w["ffn1_norm"] + sent, g_1, u_1,
                                                     wb["ffn1_w_gate"], wb["ffn1_w_up"], wb["ffn1_w_down"])
    grads["ffn1_w_gate"], grads["ffn1_w_up"], grads["ffn1_w_down"] = ffn_bwd_w("ffn1_bwd_w", h1, dx1, g_1, u_1, dg_1, du_1)
    return loss, dx, grads


def _pad_cols(a, n):
    return jnp.pad(a, ((0, 0), (0, n - a.shape[1])))


def _step_weights(shards):
    wb = {}
    w_in = _full_from_shards("w_in", shards["w_in"])
    wb["w_in_q"] = w_in[:, :MLA_Q_RANK]
    wb["w_in_kv"] = w_in[:, MLA_Q_RANK:MLA_Q_RANK + MLA_KV_RANK]
    wb["w_in_kr"] = _pad_cols(w_in[:, MLA_Q_RANK + MLA_KV_RANK:MLA_Q_RANK + MLA_KV_RANK + MLA_ROPE], LANES)
    wb["w_in_u"] = w_in[:, MLA_Q_RANK + MLA_KV_RANK + MLA_ROPE:]
    wb["w_uq"] = jnp.pad(shards["mla_w_uq"], ((0, 0), (0, 0), (0, HEAD_PAD - MLA_QK)))
    wb["w_ukv"] = shards["mla_w_ukv"]
    wb["ssm_w_glu"] = _full_from_shards("ssm_w_glu", shards["ssm_w_glu"])
    w_o = _full_from_shards("w_o", shards["w_o"])
    wb["w_o_mla"], wb["w_o_ssm"] = w_o[:SSM_WIDTH], w_o[SSM_WIDTH:]
    w_xq = _full_from_shards("xattn_w_q", shards["xattn_w_q"])
    wb["xattn_w_q"] = jnp.transpose(w_xq.reshape(D_MODEL, XH, XHD), (1, 0, 2))
    wb["xattn_w_kv"] = _full_from_shards("xattn_w_kv", shards["xattn_w_kv"])
    wb["xattn_w_o"] = _full_from_shards("xattn_w_o", shards["xattn_w_o"])
    return wb


def _sharded_grads(g):
    out = {}
    kr = g["w_in_kr"][:, :MLA_ROPE]
    out["w_in"] = _shards_from_full("w_in", jnp.concatenate([g["w_in_q"], g["w_in_kv"], kr, g["w_in_u"]], axis=1))
    out["mla_w_uq"] = g["w_uq"][:, :, :MLA_QK]
    out["mla_w_ukv"] = g["w_ukv"]
    out["ssm_w_glu"] = _shards_from_full("ssm_w_glu", g["ssm_w_glu"])
    out["w_o"] = _shards_from_full("w_o", jnp.concatenate([g["w_o_mla"], g["w_o_ssm"]], axis=0))
    w_xq = jnp.transpose(g["xattn_w_q"], (1, 0, 2)).reshape(D_MODEL, XH * XHD)
    out["xattn_w_q"] = _shards_from_full("xattn_w_q", w_xq)
    out["xattn_w_kv"] = _shards_from_full("xattn_w_kv", g["xattn_w_kv"])
    out["xattn_w_o"] = _shards_from_full("xattn_w_o", g["xattn_w_o"])
    return out


def _problem_repl_grads(g):
    out = {}
    out["mla_qk_norm_q"] = g["qk_gain_q"][:, :MLA_QK]
    out["mla_qk_norm_k"] = g["qk_gain_k"][:, :MLA_QK]
    out["ssm_b_re"] = jnp.transpose(g["ssm_bt_re"].reshape(SSM_GROUPS, SSM_GROUP, SSM_STATE), (0, 2, 1))
    out["ssm_b_im"] = jnp.transpose(g["ssm_bt_im"].reshape(SSM_GROUPS, SSM_GROUP, SSM_STATE), (0, 2, 1))
    out["ssm_c_re"] = g["ssm_c2_re"].reshape(SSM_GROUPS, SSM_GROUP, SSM_STATE)
    out["ssm_c_im"] = g["ssm_c2_im"].reshape(SSM_GROUPS, SSM_GROUP, SSM_STATE)
    for k in SMALL_REPL:
        if k not in out:
            out[k] = g[k]
    return out


def _step_params(p):
    row = lambda a: a.reshape(1, -1)
    w = {k: row(p[k]) for k in ("ffn1_norm", "mix_norm", "mla_q_norm", "mla_kv_norm", "ssm_b_glu", "out_norm_mla",
                                "out_norm_ssm", "xattn_norm", "mem_norm", "xattn_q_norm", "xattn_k_norm", "ffn2_norm")}
    w["qk_gain_q"] = _pad_cols(row(p["mla_qk_norm_q"]), HEAD_PAD)
    w["qk_gain_k"] = _pad_cols(row(p["mla_qk_norm_k"]), HEAD_PAD)
    w["ssm_a_re"], w["ssm_a_im"] = p["ssm_a_re"], p["ssm_a_im"]
    w["ssm_log_dt"] = p["ssm_log_dt"].reshape(SSM_GROUPS, 1)
    w["ssm_bt_re"] = jnp.transpose(p["ssm_b_re"], (0, 2, 1)).reshape(SSM_WIDTH, SSM_STATE)
    w["ssm_bt_im"] = jnp.transpose(p["ssm_b_im"], (0, 2, 1)).reshape(SSM_WIDTH, SSM_STATE)
    w["ssm_c2_re"] = p["ssm_c_re"].reshape(SSM_WIDTH, SSM_STATE)
    w["ssm_c2_im"] = p["ssm_c_im"].reshape(SSM_WIDTH, SSM_STATE)
    w["ssm_d"] = p["ssm_d"].reshape(1, SSM_WIDTH)
    return w


ARG_NAMES = ['x', 'mem', 'positions', 'ffn1_norm', 'ffn1_w_gate', 'ffn1_w_up', 'ffn1_w_down', 'mix_norm', 'w_in', 'mla_q_norm', 'mla_w_uq', 'mla_kv_norm', 'mla_w_ukv', 'mla_qk_norm_q', 'mla_qk_norm_k', 'ssm_a_re', 'ssm_a_im', 'ssm_log_dt', 'ssm_b_re', 'ssm_b_im', 'ssm_c_re', 'ssm_c_im', 'ssm_d', 'ssm_w_glu', 'ssm_b_glu', 'out_norm_mla', 'out_norm_ssm', 'w_o', 'xattn_norm', 'mem_norm', 'xattn_w_q', 'xattn_w_kv', 'xattn_q_norm', 'xattn_k_norm', 'xattn_w_o', 'ffn2_norm', 'ffn2_w_gate', 'ffn2_w_up', 'ffn2_w_down']
WEIGHT_NAMES = ARG_NAMES[3:]


def _gather_weights(p, c):
    half = lambda a: lax.dynamic_slice_in_dim(a, c * (a.shape[0] // 2), a.shape[0] // 2, axis=0)
    ffn1 = [half(p[k].astype(BF16)) for k in FFN_NAMES[:3]]
    small = [half(_pack_shards({k: p[k].astype(BF16) for k in SMALL_SHARDED}))]
    ffn2 = [half(p[k].astype(BF16)) for k in FFN_NAMES[3:]]
    me = 4 * lax.axis_index("x") + 2 * lax.axis_index("y") + c
    own = lambda got, blocks: [lax.dynamic_update_index_in_dim(g, b, me, 0) for g, b in zip(got, blocks)]
    as_shards = lambda a: a.reshape(N_CHIPS, 2 * a.shape[1], a.shape[2])
    landing = lambda blocks: [jax.ShapeDtypeStruct((N_DEV,) + b.shape, b.dtype) for b in blocks]
    got1 = own(all_gather_halves("all_gather_weights_a", ffn1), ffn1)
    got1, small = lax.optimization_barrier((got1, small))
    flight_s, sent_s = split_copy_start("gather_small_start", _gather_copies, N_DEV - 1, small, landing(small))
    sent_s, ffn2 = lax.optimization_barrier((sent_s, ffn2))
    flight_2, sent_2 = split_copy_start("gather_ffn2_start", _gather_copies, N_DEV - 1, ffn2, landing(ffn2))
    wb = {k: as_shards(a) for k, a in zip(FFN_NAMES[:3], got1)}

    def small_weights(after):
        mine, got = split_copy_wait("gather_small_wait", flight_s, after)
        return _step_weights(_unpack_shards(own(got, mine)[0].reshape(N_CHIPS, -1, LANES)))

    def ffn2_weights(after):
        mine, got = split_copy_wait("gather_ffn2_wait", flight_2, after)
        return [as_shards(a) for a in own(got, mine)]

    return wb, small_weights, ffn2_weights, sent_s + sent_2


class _GradReduce:
    def __init__(self, c):
        self.c, self.c_arr = c, jnp.reshape(c, (1,)).astype(jnp.int32)
        self.chip = 2 * lax.axis_index("x") + lax.axis_index("y")
        self.flights = []

    def start(self, tag, arrs):
        split = [a.reshape(N_CHIPS, 2, a.shape[1] // 2, a.shape[2]) for a in arrs]
        return self._scatter(tag, split, pair_send_halves(f"grad_pair_send_{tag}", split))

    def send(self, tag, arrs):
        split = [a.reshape(N_CHIPS, 2, a.shape[1] // 2, a.shape[2]) for a in arrs]
        lands = [jax.ShapeDtypeStruct((s.shape[0],) + s.shape[2:], s.dtype) for s in split]
        flight, sent = split_copy_start(f"grad_pair_send_start_{tag}", _pair_copies, 1, split, lands)
        self.sending = (tag, flight)
        return sent

    def scatter(self, tag, after):
        sent_tag, flight = self.sending
        assert sent_tag == tag
        return self._scatter(tag, *split_copy_wait(f"grad_pair_send_wait_{tag}", flight, after))

    def _scatter(self, tag, split, got):
        parts = [pair_add(f"grad_pair_add_{tag}_{k}", s, g, self.c_arr) for k, (s, g) in enumerate(zip(split, got))]
        flight, sent = split_copy_start(f"grad_scatter_start_{tag}", _scatter_copies, N_CHIPS - 1, parts, parts)
        self.flights.append((tag, flight))
        return sent

    def finish(self, after):
        halves = []
        for tag, flight in self.flights:
            parts, landed = split_copy_wait(f"grad_scatter_wait_{tag}", flight, after)
            for k, (q, p) in enumerate(zip(landed, parts)):
                mine = lax.dynamic_index_in_dim(p, self.chip, 0, keepdims=False)
                halves.append(sum_chips(f"grad_sum_{tag}_{k}", lax.dynamic_update_index_in_dim(q, mine, self.chip, 0)))
        tags = "_".join(t for t, _ in self.flights)
        self.flights = []
        theirs = pair_exchange(f"grad_pair_exchange_{tags}", halves)
        return [jnp.where(self.c == 0, jnp.concatenate([h, t], axis=0), jnp.concatenate([t, h], axis=0))
                for h, t in zip(halves, theirs)]


def kernel(x, mem, positions, ffn1_norm, ffn1_w_gate, ffn1_w_up, ffn1_w_down, mix_norm, w_in, mla_q_norm, mla_w_uq, mla_kv_norm, mla_w_ukv, mla_qk_norm_q, mla_qk_norm_k, ssm_a_re, ssm_a_im, ssm_log_dt, ssm_b_re, ssm_b_im, ssm_c_re, ssm_c_im, ssm_d, ssm_w_glu, ssm_b_glu, out_norm_mla, out_norm_ssm, w_o, xattn_norm, mem_norm, xattn_w_q, xattn_w_kv, xattn_q_norm, xattn_k_norm, xattn_w_o, ffn2_norm, ffn2_w_gate, ffn2_w_up, ffn2_w_down, loss_target, m_ffn1_norm, m_ffn1_w_gate, m_ffn1_w_up, m_ffn1_w_down, m_mix_norm, m_w_in, m_mla_q_norm, m_mla_w_uq, m_mla_kv_norm, m_mla_w_ukv, m_mla_qk_norm_q, m_mla_qk_norm_k, m_ssm_a_re, m_ssm_a_im, m_ssm_log_dt, m_ssm_b_re, m_ssm_b_im, m_ssm_c_re, m_ssm_c_im, m_ssm_d, m_ssm_w_glu, m_ssm_b_glu, m_out_norm_mla, m_out_norm_ssm, m_w_o, m_xattn_norm, m_mem_norm, m_xattn_w_q, m_xattn_w_kv, m_xattn_q_norm, m_xattn_k_norm, m_xattn_w_o, m_ffn2_norm, m_ffn2_w_gate, m_ffn2_w_up, m_ffn2_w_down, v_ffn1_norm, v_ffn1_w_gate, v_ffn1_w_up, v_ffn1_w_down, v_mix_norm, v_w_in, v_mla_q_norm, v_mla_w_uq, v_mla_kv_norm, v_mla_w_ukv, v_mla_qk_norm_q, v_mla_qk_norm_k, v_ssm_a_re, v_ssm_a_im, v_ssm_log_dt, v_ssm_b_re, v_ssm_b_im, v_ssm_c_re, v_ssm_c_im, v_ssm_d, v_ssm_w_glu, v_ssm_b_glu, v_out_norm_mla, v_out_norm_ssm, v_w_o, v_xattn_norm, v_mem_norm, v_xattn_w_q, v_xattn_w_kv, v_xattn_q_norm, v_xattn_k_norm, v_xattn_w_o, v_ffn2_norm, v_ffn2_w_gate, v_ffn2_w_up, v_ffn2_w_down):
    args = dict(locals())
    c = lax.axis_index("c")
    view = lambda k, a: jnp.swapaxes(a, 0, 1) if k in TRANSPOSED_VIEW else a
    p = {k: view(k, args[k][0]) for k in WEIGHT_NAMES}
    mom = {k: view(k, args["m_" + k][0]) for k in WEIGHT_NAMES}
    var = {k: view(k, args["v_" + k][0]) for k in WEIGHT_NAMES}
    natural = {k: view(k, p[k]) for k in WEIGHT_NAMES}

    wb, small_weights, ffn2_weights, sent = _gather_weights(
        {k: (p[k] if k in FFN_NAMES else natural[k]) for k in WEIGHT_NAMES}, c)
    w = _step_params(natural)
    w["ffn1_norm"] = w["ffn1_norm"] + sent
    early, late = _GradReduce(c), _GradReduce(c)

    def on_grads(tag, g):
        if tag == "ffn2":
            return early.send(tag, [g[k] for k in FFN_NAMES[3:]])
        packed = _pack_shards(_sharded_grads(g))
        return early.scatter("ffn2", packed[0, :SUBLANES]) + early.start(tag, [packed])

    loss, dx, g = local_step(x[0], mem[0], positions[0], loss_target[0], w, wb, small_weights, ffn2_weights, on_grads)

    sent = late.send("ffn1", [g[k] for k in FFN_NAMES[:3]])
    shards = early.finish(dx[:SUBLANES, :LANES] + sent)
    grad = dict(zip(FFN_NAMES[3:], shards[:3]))
    small_sharded = _unpack_shards(shards[3])
    grad.update({k: view(k, small_sharded[k]) for k in SMALL_SHARDED})
    reduced = all_reduce_small("grad_all_reduce_small", _pack_repl(_problem_repl_grads(g), loss + sent))
    grad.update(_unpack_repl(reduced))
    loss = grad.pop("loss")
    started = jnp.reshape(late.scatter("ffn1", reduced[:SUBLANES]), (1, 1))

    delta, new_m, new_v = {}, {}, {}
    small = [k for k in WEIGHT_NAMES if k not in FFN_NAMES and k not in SMALL_SHARDED]
    as2d = lambda a: a.reshape(-1, a.shape[-1])

    def update(k, after):
        delta[k], new_m[k], new_v[k] = adamw_big("adamw_" + k, as2d(p[k]), as2d(grad[k]), as2d(mom[k]), as2d(var[k]), after)

    last = started
    for k in WEIGHT_NAMES:
        if k not in small and k not in FFN_NAMES[:3]:
            update(k, last)
            last = delta[k]
    ds, nms, nvs = adamw_small("adamw_small", [as2d(p[k]) for k in small], [as2d(grad[k].reshape(p[k].shape)) for k in small],
                               [as2d(mom[k]) for k in small], [as2d(var[k]) for k in small], last)
    for k, d, nm, nv in zip(small, ds, nms, nvs):
        delta[k], new_m[k], new_v[k] = d, nm, nv
    last = ds[0]

    grad.update(zip(FFN_NAMES[:3], late.finish(last)))
    for k in FFN_NAMES[:3]:
        update(k, last)

    shaped = lambda d, k: view(k, d.reshape(p[k].shape)).reshape(args[k].shape)
    return (loss, dx[None], *[shaped(grad[k], k) for k in WEIGHT_NAMES], *[shaped(delta[k], k) for k in WEIGHT_NAMES],
            *[shaped(new_m[k], k) for k in WEIGHT_NAMES], *[shaped(new_v[k], k) for k in WEIGHT_NAMES])
```
